```python
import math
import jax, jax.numpy as jnp
from jax import lax
import numpy as np

D_MODEL = 1024
BATCH = 8
SEQ = 8192
DEPTH = 2

N_A_LAYERS = DEPTH // 2
N_B_LAYERS = DEPTH - N_A_LAYERS

CONV_WIDTH = D_MODEL
CONV_K = 3

HEAD_DIM = 64
N_Q_HEADS = D_MODEL // HEAD_DIM
N_KV_HEADS = max(1, N_Q_HEADS // 8)
GROUP = N_Q_HEADS // N_KV_HEADS
ATTN_WIDTH = N_Q_HEADS * HEAD_DIM
KV_WIDTH = N_KV_HEADS * HEAD_DIM
WINDOW = 128
BLOCK = 128

N_BUCKETS = 32
MAX_DISTANCE = 128

EPS = 1e-6
NEG_INF = -1e30

kernel_name = "yoco_shortconv_swa_sink_hybrid"


def rmsnorm(x, g):
    xf = x.astype(jnp.float32)
    y = xf * lax.rsqrt(jnp.mean(xf * xf, axis=-1, keepdims=True) + EPS) * g.astype(jnp.float32)
    return y.astype(x.dtype)


def t5_causal_bucket(dist):
    max_exact = N_BUCKETS // 2
    is_small = dist < max_exact
    d = jnp.maximum(dist, 1).astype(jnp.float32)
    large = max_exact + (jnp.log(d / max_exact) / math.log(MAX_DISTANCE / max_exact)
                         * (N_BUCKETS - max_exact)).astype(jnp.int32)
    large = jnp.minimum(large, N_BUCKETS - 1)
    return jnp.where(is_small, dist, large)


def short_conv_mixer(h, w_in, conv_w, w_out):
    proj = h @ w_in
    b_gate, c_gate, u, z = jnp.split(proj, 4, axis=-1)
    v = c_gate * u
    conv = lax.conv_general_dilated(
        v, conv_w[:, None, :].astype(v.dtype),
        window_strides=(1,), padding=[(CONV_K - 1, 0)],
        dimension_numbers=("NWC", "WIO", "NWC"),
        feature_group_count=CONV_WIDTH)
    y = b_gate * conv * jax.nn.silu(z)
    return y @ w_out


def shared_kv(h, kv_norm, w_kv):
    bsz, seq, _ = h.shape
    nb = seq // BLOCK
    kv = rmsnorm(h, kv_norm) @ w_kv
    k, v = jnp.split(kv, 2, axis=-1)
    k = k.reshape(bsz, nb, BLOCK, N_KV_HEADS, HEAD_DIM)
    v = v.reshape(bsz, nb, BLOCK, N_KV_HEADS, HEAD_DIM)

    def band(t):
        prev = jnp.concatenate([jnp.zeros_like(t[:, :1]), t[:, :-1]], axis=1)
        return jnp.concatenate([prev, t], axis=2)

    return band(k), band(v)


def banded_bias_and_mask(nb, rel_bias):
    q_loc = jnp.arange(BLOCK, dtype=jnp.int32)[:, None]
    s_loc = jnp.arange(2 * BLOCK, dtype=jnp.int32)[None, :]
    dist = q_loc + BLOCK - s_loc
    in_window = (dist >= 0) & (dist < WINDOW)
    bucket = t5_causal_bucket(jnp.maximum(dist, 0))
    bias = rel_bias.astype(jnp.float32)[bucket]
    bias = jnp.transpose(bias, (2, 0, 1)).reshape(N_KV_HEADS, GROUP, BLOCK, 2 * BLOCK)
    blk = jnp.arange(nb, dtype=jnp.int32)[:, None, None]
    exists = (blk > 0) | (s_loc >= BLOCK)[None]
    mask = in_window[None] & exists
    return bias, mask


def swa_sink_attention(q, keys, vals, sinks, bias, mask):
    bsz, seq, _ = q.shape
    nb = seq // BLOCK
    qb = q.reshape(bsz, nb, BLOCK, N_KV_HEADS, GROUP, HEAD_DIM)
    scores = jnp.einsum("bnqkgd,bnskd->bnkgqs", qb, keys).astype(jnp.float32)
    logits = scores * (HEAD_DIM ** -0.5) + bias[None, None]
    logits = jnp.where(mask[None, :, None, None], logits, NEG_INF)
    sink = sinks.astype(jnp.float32).reshape(1, 1, N_KV_HEADS, GROUP, 1, 1)
    m = jnp.maximum(jnp.max(logits, axis=-1, keepdims=True), sink)
    p = jnp.exp(logits - m)
    p = p / (jnp.sum(p, axis=-1, keepdims=True) + jnp.exp(sink - m))
    out = jnp.einsum("bnkgqs,bnskd->bnqkgd", p.astype(vals.dtype), vals)
    return out.reshape(bsz, seq, ATTN_WIDTH)


def _fwd_setup_inputs(seed: int = 0) -> dict:
    key = jax.random.key(seed)
    ks = jax.random.split(key, 16)
    f32 = jnp.float32
    nrm = lambda k, shape, s: jax.random.normal(k, shape, f32) * s
    return {
        "x": nrm(ks[0], (BATCH, SEQ, D_MODEL), 1.0),
        "a_pre_norm": 1.0 + nrm(ks[1], (N_A_LAYERS, D_MODEL), 0.05),
        "a_w_in": nrm(ks[2], (N_A_LAYERS, D_MODEL, 4 * CONV_WIDTH), D_MODEL ** -0.5),
        "a_conv_w": nrm(ks[3], (N_A_LAYERS, CONV_K, CONV_WIDTH), CONV_K ** -0.5),
        "a_w_out": nrm(ks[4], (N_A_LAYERS, CONV_WIDTH, D_MODEL), CONV_WIDTH ** -0.5),
        "a_post_norm": 1.0 + nrm(ks[5], (N_A_LAYERS, D_MODEL), 0.05),
        "kv_norm": 1.0 + nrm(ks[6], (D_MODEL,), 0.05),
        "w_kv": nrm(ks[7], (D_MODEL, 2 * KV_WIDTH), D_MODEL ** -0.5),
        "rel_bias": nrm(ks[8], (N_BUCKETS, N_Q_HEADS), 0.1),
        "b_pre_norm": 1.0 + nrm(ks[9], (N_B_LAYERS, D_MODEL), 0.05),
        "b_w_in": nrm(ks[10], (N_B_LAYERS, D_MODEL, 2 * ATTN_WIDTH), D_MODEL ** -0.5),
        "b_sinks": nrm(ks[11], (N_B_LAYERS, N_Q_HEADS), 0.5),
        "b_w_out": nrm(ks[12], (N_B_LAYERS, ATTN_WIDTH, D_MODEL), ATTN_WIDTH ** -0.5),
        "b_post_norm": 1.0 + nrm(ks[13], (N_B_LAYERS, D_MODEL), 0.05),
    }


def _fwd_reference(x, a_pre_norm, a_w_in, a_conv_w, a_w_out, a_post_norm,
              kv_norm, w_kv, rel_bias,
              b_pre_norm, b_w_in, b_sinks, b_w_out, b_post_norm):
    h = x
    nb = x.shape[1] // BLOCK
    bias, mask = banded_bias_and_mask(nb, rel_bias)
    keys = vals = None
    for layer in range(DEPTH):
        if layer < N_A_LAYERS:
            i = layer
            y = short_conv_mixer(rmsnorm(h, a_pre_norm[i]), a_w_in[i], a_conv_w[i], a_w_out[i])
            h = h + rmsnorm(y, a_post_norm[i])
            if layer == N_A_LAYERS - 1:
                keys, vals = shared_kv(h, kv_norm, w_kv)
        else:
            j = layer - N_A_LAYERS
            qz = rmsnorm(h, b_pre_norm[j]) @ b_w_in[j]
            q, z = jnp.split(qz, 2, axis=-1)
            o = swa_sink_attention(q, keys, vals, b_sinks[j], bias, mask) * jax.nn.silu(z)
            y = o @ b_w_out[j]
            h = h + rmsnorm(y, b_post_norm[j])
    return h


import jax as _jax
import jax.numpy as _jnp

TWIN_FORMAT = 'train_step'
FWD_PARAMS = ['x', 'a_pre_norm', 'a_w_in', 'a_conv_w', 'a_w_out', 'a_post_norm', 'kv_norm', 'w_kv', 'rel_bias', 'b_pre_norm', 'b_w_in', 'b_sinks', 'b_w_out', 'b_post_norm']
TWIN_WEIGHTS = ['a_pre_norm', 'a_w_in', 'a_conv_w', 'a_w_out', 'a_post_norm', 'kv_norm', 'w_kv', 'rel_bias', 'b_pre_norm', 'b_w_in', 'b_sinks', 'b_w_out', 'b_post_norm']
TWIN_DIFF_INPUT = 'x'
TWIN_INPUTS = ['x', 'a_pre_norm', 'a_w_in', 'a_conv_w', 'a_w_out', 'a_post_norm', 'kv_norm', 'w_kv', 'rel_bias', 'b_pre_norm', 'b_w_in', 'b_sinks', 'b_w_out', 'b_post_norm', 'loss_target', 'm_a_pre_norm', 'm_a_w_in', 'm_a_conv_w', 'm_a_w_out', 'm_a_post_norm', 'm_kv_norm', 'm_w_kv', 'm_rel_bias', 'm_b_pre_norm', 'm_b_w_in', 'm_b_sinks', 'm_b_w_out', 'm_b_post_norm', 'v_a_pre_norm', 'v_a_w_in', 'v_a_conv_w', 'v_a_w_out', 'v_a_post_norm', 'v_kv_norm', 'v_w_kv', 'v_rel_bias', 'v_b_pre_norm', 'v_b_w_in', 'v_b_sinks', 'v_b_w_out', 'v_b_post_norm']
TWIN_OUTPUTS = ['loss', 'grad_x', 'grad_a_pre_norm', 'grad_a_w_in', 'grad_a_conv_w', 'grad_a_w_out', 'grad_a_post_norm', 'grad_kv_norm', 'grad_w_kv', 'grad_rel_bias', 'grad_b_pre_norm', 'grad_b_w_in', 'grad_b_sinks', 'grad_b_w_out', 'grad_b_post_norm', 'delta_a_pre_norm', 'delta_a_w_in', 'delta_a_conv_w', 'delta_a_w_out', 'delta_a_post_norm', 'delta_kv_norm', 'delta_w_kv', 'delta_rel_bias', 'delta_b_pre_norm', 'delta_b_w_in', 'delta_b_sinks', 'delta_b_w_out', 'delta_b_post_norm', 'new_m_a_pre_norm', 'new_m_a_w_in', 'new_m_a_conv_w', 'new_m_a_w_out', 'new_m_a_post_norm', 'new_m_kv_norm', 'new_m_w_kv', 'new_m_rel_bias', 'new_m_b_pre_norm', 'new_m_b_w_in', 'new_m_b_sinks', 'new_m_b_w_out', 'new_m_b_post_norm', 'new_v_a_pre_norm', 'new_v_a_w_in', 'new_v_a_conv_w', 'new_v_a_w_out', 'new_v_a_post_norm', 'new_v_kv_norm', 'new_v_w_kv', 'new_v_rel_bias', 'new_v_b_pre_norm', 'new_v_b_w_in', 'new_v_b_sinks', 'new_v_b_w_out', 'new_v_b_post_norm']
TWIN_LEAF_KINDS = {'loss': 'loss', 'grad_x': 'grad_x', 'grad_a_pre_norm': 'grad_w', 'grad_a_w_in': 'grad_w', 'grad_a_conv_w': 'grad_w', 'grad_a_w_out': 'grad_w', 'grad_a_post_norm': 'grad_w', 'grad_kv_norm': 'grad_w', 'grad_w_kv': 'grad_w', 'grad_rel_bias': 'grad_w', 'grad_b_pre_norm': 'grad_w', 'grad_b_w_in': 'grad_w', 'grad_b_sinks': 'grad_w', 'grad_b_w_out': 'grad_w', 'grad_b_post_norm': 'grad_w', 'delta_a_pre_norm': 'delta_w', 'delta_a_w_in': 'delta_w', 'delta_a_conv_w': 'delta_w', 'delta_a_w_out': 'delta_w', 'delta_a_post_norm': 'delta_w', 'delta_kv_norm': 'delta_w', 'delta_w_kv': 'delta_w', 'delta_rel_bias': 'delta_w', 'delta_b_pre_norm': 'delta_w', 'delta_b_w_in': 'delta_w', 'delta_b_sinks': 'delta_w', 'delta_b_w_out': 'delta_w', 'delta_b_post_norm': 'delta_w', 'new_m_a_pre_norm': 'new_m', 'new_m_a_w_in': 'new_m', 'new_m_a_conv_w': 'new_m', 'new_m_a_w_out': 'new_m', 'new_m_a_post_norm': 'new_m', 'new_m_kv_norm': 'new_m', 'new_m_w_kv': 'new_m', 'new_m_rel_bias': 'new_m', 'new_m_b_pre_norm': 'new_m', 'new_m_b_w_in': 'new_m', 'new_m_b_sinks': 'new_m', 'new_m_b_w_out': 'new_m', 'new_m_b_post_norm': 'new_m', 'new_v_a_pre_norm': 'new_v', 'new_v_a_w_in': 'new_v', 'new_v_a_conv_w': 'new_v', 'new_v_a_w_out': 'new_v', 'new_v_a_post_norm': 'new_v', 'new_v_kv_norm': 'new_v', 'new_v_w_kv': 'new_v', 'new_v_rel_bias': 'new_v', 'new_v_b_pre_norm': 'new_v', 'new_v_b_w_in': 'new_v', 'new_v_b_sinks': 'new_v', 'new_v_b_w_out': 'new_v', 'new_v_b_post_norm': 'new_v'}


def _forward(args):
    return _fwd_reference(*[args[k] for k in FWD_PARAMS])


def _output_shape():
    def fwd():
        inp = _fwd_setup_inputs(0)
        return _fwd_reference(*[inp[k] for k in FWD_PARAMS])
    out = _jax.eval_shape(fwd)
    return out.shape, out.dtype

N_MICROBATCH = 1
ADAM_LR = 0.001
ADAM_B1 = 0.9
ADAM_B2 = 0.999
ADAM_EPS = 1e-08
ADAM_WD = 0.01
ADAM_STEP = 10
PER_EXAMPLE_BATCH_AXIS = {'x': 0, 'loss_target': 0}
SHARED_INPUTS = []
_WEIGHT_DTYPES = {'a_pre_norm': _jnp.float32, 'a_w_in': _jnp.float32, 'a_conv_w': _jnp.float32, 'a_w_out': _jnp.float32, 'a_post_norm': _jnp.float32, 'kv_norm': _jnp.float32, 'w_kv': _jnp.float32, 'rel_bias': _jnp.float32, 'b_pre_norm': _jnp.float32, 'b_w_in': _jnp.float32, 'b_sinks': _jnp.float32, 'b_w_out': _jnp.float32, 'b_post_norm': _jnp.float32}
MOMENT_SCALE = {'a_pre_norm': 1.513578e+00, 'a_w_in': 7.742440e-01, 'a_conv_w': 8.354736e-01, 'a_w_out': 7.900105e-01, 'a_post_norm': 6.407892e+01, 'kv_norm': 6.797359e-01, 'w_kv': 1.311703e+00, 'rel_bias': 5.672955e-01, 'b_pre_norm': 6.509593e-01, 'b_w_in': 4.578384e-01, 'b_sinks': 1.987962e-01, 'b_w_out': 4.827393e-01, 'b_post_norm': 6.406388e+01}


def _to_microbatches(a, axis):
    t = _jnp.moveaxis(a, axis, 0)
    t = t.reshape((N_MICROBATCH, t.shape[0] // N_MICROBATCH) + t.shape[1:])
    return _jnp.moveaxis(t, 1, axis + 1)


def setup_inputs(seed: int = 0) -> dict:
    inp = _fwd_setup_inputs(seed)
    key = _jax.random.fold_in(_jax.random.key(seed), 7919)
    shape, _ = _output_shape()
    out = dict(inp)
    out["loss_target"] = _jax.random.normal(_jax.random.fold_in(key, 0), shape, _jnp.float32)
    for i, name in enumerate(TWIN_WEIGHTS):
        w = inp[name].astype(_jnp.float32)
        if MOMENT_SCALE is None:
            s = _jnp.sqrt(_jnp.mean(_jnp.square(w)) + 1e-30)
        else:
            s = MOMENT_SCALE[name]
        km, kv = _jax.random.split(_jax.random.fold_in(key, i + 1))
        out[name] = w
        out["m_" + name] = s * _jax.random.normal(km, w.shape, _jnp.float32)
        out["v_" + name] = (s * s) * _jax.random.uniform(kv, w.shape, _jnp.float32, 0.5, 1.5)
    if N_MICROBATCH > 1:
        for name, axis in PER_EXAMPLE_BATCH_AXIS.items():
            out[name] = _to_microbatches(out[name], axis)
    return {'x': out['x'], 'a_pre_norm': out['a_pre_norm'], 'a_w_in': out['a_w_in'], 'a_conv_w': out['a_conv_w'], 'a_w_out': out['a_w_out'], 'a_post_norm': out['a_post_norm'], 'kv_norm': out['kv_norm'], 'w_kv': out['w_kv'], 'rel_bias': out['rel_bias'], 'b_pre_norm': out['b_pre_norm'], 'b_w_in': out['b_w_in'], 'b_sinks': out['b_sinks'], 'b_w_out': out['b_w_out'], 'b_post_norm': out['b_post_norm'], 'loss_target': out['loss_target'], 'm_a_pre_norm': out['m_a_pre_norm'], 'm_a_w_in': out['m_a_w_in'], 'm_a_conv_w': out['m_a_conv_w'], 'm_a_w_out': out['m_a_w_out'], 'm_a_post_norm': out['m_a_post_norm'], 'm_kv_norm': out['m_kv_norm'], 'm_w_kv': out['m_w_kv'], 'm_rel_bias': out['m_rel_bias'], 'm_b_pre_norm': out['m_b_pre_norm'], 'm_b_w_in': out['m_b_w_in'], 'm_b_sinks': out['m_b_sinks'], 'm_b_w_out': out['m_b_w_out'], 'm_b_post_norm': out['m_b_post_norm'], 'v_a_pre_norm': out['v_a_pre_norm'], 'v_a_w_in': out['v_a_w_in'], 'v_a_conv_w': out['v_a_conv_w'], 'v_a_w_out': out['v_a_w_out'], 'v_a_post_norm': out['v_a_post_norm'], 'v_kv_norm': out['v_kv_norm'], 'v_w_kv': out['v_w_kv'], 'v_rel_bias': out['v_rel_bias'], 'v_b_pre_norm': out['v_b_pre_norm'], 'v_b_w_in': out['v_b_w_in'], 'v_b_sinks': out['v_b_sinks'], 'v_b_w_out': out['v_b_w_out'], 'v_b_post_norm': out['v_b_post_norm']}


def _loss(weights, diff, rest, loss_target):
    with _jax.named_scope("forward"):
        args = {**rest, TWIN_DIFF_INPUT: diff, **{k: w.astype(_WEIGHT_DTYPES[k]) for k, w in weights.items()}}
        y = _forward(args)
    with _jax.named_scope("loss_head"):
        err = _jnp.square(y.astype(_jnp.float32) - loss_target)
        return 0.5 * _jnp.sum(_jnp.mean(err, axis=-1)) if err.ndim else 0.5 * err


def _adamw(w, g, m, v):
    m = ADAM_B1 * m + (1.0 - ADAM_B1) * g
    v = ADAM_B2 * v + (1.0 - ADAM_B2) * _jnp.square(g)
    m_hat = m / (1.0 - ADAM_B1 ** ADAM_STEP)
    v_hat = v / (1.0 - ADAM_B2 ** ADAM_STEP)
    delta = -ADAM_LR * (m_hat / (_jnp.sqrt(v_hat) + ADAM_EPS) + ADAM_WD * w)
    return delta, m, v


def reference(x, a_pre_norm, a_w_in, a_conv_w, a_w_out, a_post_norm, kv_norm, w_kv, rel_bias, b_pre_norm, b_w_in, b_sinks, b_w_out, b_post_norm, loss_target, m_a_pre_norm, m_a_w_in, m_a_conv_w, m_a_w_out, m_a_post_norm, m_kv_norm, m_w_kv, m_rel_bias, m_b_pre_norm, m_b_w_in, m_b_sinks, m_b_w_out, m_b_post_norm, v_a_pre_norm, v_a_w_in, v_a_conv_w, v_a_w_out, v_a_post_norm, v_kv_norm, v_w_kv, v_rel_bias, v_b_pre_norm, v_b_w_in, v_b_sinks, v_b_w_out, v_b_post_norm):
    given = dict(x=x, a_pre_norm=a_pre_norm, a_w_in=a_w_in, a_conv_w=a_conv_w, a_w_out=a_w_out, a_post_norm=a_post_norm, kv_norm=kv_norm, w_kv=w_kv, rel_bias=rel_bias, b_pre_norm=b_pre_norm, b_w_in=b_w_in, b_sinks=b_sinks, b_w_out=b_w_out, b_post_norm=b_post_norm, loss_target=loss_target, m_a_pre_norm=m_a_pre_norm, m_a_w_in=m_a_w_in, m_a_conv_w=m_a_conv_w, m_a_w_out=m_a_w_out, m_a_post_norm=m_a_post_norm, m_kv_norm=m_kv_norm, m_w_kv=m_w_kv, m_rel_bias=m_rel_bias, m_b_pre_norm=m_b_pre_norm, m_b_w_in=m_b_w_in, m_b_sinks=m_b_sinks, m_b_w_out=m_b_w_out, m_b_post_norm=m_b_post_norm, v_a_pre_norm=v_a_pre_norm, v_a_w_in=v_a_w_in, v_a_conv_w=v_a_conv_w, v_a_w_out=v_a_w_out, v_a_post_norm=v_a_post_norm, v_kv_norm=v_kv_norm, v_w_kv=v_w_kv, v_rel_bias=v_rel_bias, v_b_pre_norm=v_b_pre_norm, v_b_w_in=v_b_w_in, v_b_sinks=v_b_sinks, v_b_w_out=v_b_w_out, v_b_post_norm=v_b_post_norm)
    weights = {n: given[n] for n in TWIN_WEIGHTS}
    shared = {n: given[n] for n in SHARED_INPUTS}
    per_example = {n: given[n] for n in ['x']}
    grad_fn = _jax.value_and_grad(_loss, argnums=(0, 1))

    def one_microbatch(ex, loss_target):
        ex = dict(ex)
        diff = ex.pop(TWIN_DIFF_INPUT)
        return grad_fn(weights, diff, {**shared, **ex}, loss_target)

    if N_MICROBATCH == 1:
        loss, (grad_w, grad_x) = one_microbatch(per_example, given["loss_target"])
    else:
        def body(carry, xs):
            loss_sum, grad_sum = carry
            l_k, (gw_k, gx_k) = one_microbatch(xs[0], xs[1])
            with _jax.named_scope("update"):
                return (loss_sum + l_k, _jax.tree.map(_jnp.add, grad_sum, gw_k)), gx_k

        init = (_jnp.zeros((), _jnp.float32), _jax.tree.map(_jnp.zeros_like, weights))
        (loss, grad_w), grad_x = _jax.lax.scan(body, init, (per_example, given["loss_target"]))
    with _jax.named_scope("update"):
        delta_w, new_m, new_v = {}, {}, {}
        for n in TWIN_WEIGHTS:
            delta_w[n], new_m[n], new_v[n] = _adamw(weights[n], grad_w[n], given["m_" + n], given["v_" + n])
    return (loss, grad_x, *[grad_w[n] for n in TWIN_WEIGHTS], *[delta_w[n] for n in TWIN_WEIGHTS],
            *[new_m[n] for n in TWIN_WEIGHTS], *[new_v[n] for n in TWIN_WEIGHTS])
```

```python
import functools
import math

import jax
import jax.numpy as jnp
from jax import lax
from jax.experimental import pallas as pl
from jax.experimental.pallas import tpu as pltpu

F32 = jnp.float32
BF16 = jnp.bfloat16
MESH = pl.DeviceIdType.MESH
SDS = jax.ShapeDtypeStruct

D = 1024
HEAD_DIM = 64
N_HEADS = 16
GROUP = 8
KV_W = 128
BLK = 128
N_BUCKETS = 32
MAX_EXACT = 16
MAX_DISTANCE = 128
EPS = 1e-6
NEG_INF = -1e30
Q_SCALE = HEAD_DIM ** -0.5

ADAM_LR = 0.001
ADAM_B1 = 0.9
ADAM_B2 = 0.999
ADAM_EPS = 1e-08
ADAM_WD = 0.01
ADAM_STEP = 10

N_CHIPS = 4
N_DEV = 8
VMEM_LIMIT = 56 * 1024 * 1024
SMALL_ROWS = 16


def _bucket_thresholds():
    def bucket(d):
        big = MAX_EXACT + int(math.log(d / MAX_EXACT) / math.log(MAX_DISTANCE / MAX_EXACT)
                              * (N_BUCKETS - MAX_EXACT))
        return d if d < MAX_EXACT else min(big, N_BUCKETS - 1)
    out = []
    for b in range(MAX_EXACT + 1, N_BUCKETS):
        out.append(min(d for d in range(MAX_EXACT, MAX_DISTANCE) if bucket(d) >= b))
    return tuple(out)


BUCKET_THRESHOLDS = _bucket_thresholds()


def _params(semantics=None, vmem=VMEM_LIMIT):
    return pltpu.CompilerParams(dimension_semantics=semantics, vmem_limit_bytes=vmem)


def _tile(n, pref):
    return pref if n >= 2 * pref else max(n // 2, 8)


def _rms_scale(v):
    return lax.rsqrt(jnp.mean(v * v, axis=-1, keepdims=True) + EPS)


def _nt(a, b):
    return lax.dot_general(a, b, (((1,), (1,)), ((), ())), preferred_element_type=F32)


def _tn(a, b):
    return lax.dot_general(a, b, (((0,), (0,)), ((), ())), preferred_element_type=F32)


def _nn(a, b):
    return jnp.dot(a, b, preferred_element_type=F32)


def _silu_parts(z):
    sg = jax.nn.sigmoid(z)
    return sg, z * sg


def _dsilu(z, sg):
    return sg * (1.0 + z * (1.0 - sg))


def _acc_row(ref, row, val):
    ref[row:row + 1, :] += val


def _gather_weights(shards, small):
    n = len(shards)

    def body(*refs):
        ins, small_in = refs[:n], refs[n]
        outs, small_out = refs[n + 1:2 * n + 1], refs[2 * n + 1]
        ici_send, ici_recv, d2d_send, d2d_recv = refs[2 * n + 2:]
        x, y, c = lax.axis_index("x"), lax.axis_index("y"), lax.axis_index("c")
        k = 2 * x + y
        chips = [(x, 1 - y), (1 - x, y), (1 - x, 1 - y)]
        for i_ref, o_ref in zip(ins, outs):
            o_ref[k] = i_ref[...].astype(BF16)
        small_out[k] = small_in[...]

        def half(o_ref, chip, core):
            h = o_ref.shape[1] // 2
            return o_ref.at[chip, pl.ds(pl.multiple_of(core * h, 16), h)]

        sends = []
        for a, o_ref in enumerate(list(outs) + [small_out]):
            split = a < n
            for j, (px, py) in enumerate(chips):
                src = half(o_ref, k, c) if split else o_ref.at[k]
                cp = pltpu.make_async_remote_copy(
                    src_ref=src, dst_ref=src, send_sem=ici_send.at[3 * a + j],
                    recv_sem=ici_recv.at[3 * a + j], device_id=(px, py, c), device_id_type=MESH)
                cp.start()
                sends.append(cp)
        for a, o_ref in enumerate(list(outs) + [small_out]):
            split = a < n
            for j, (px, py) in enumerate(chips):
                kj = 2 * px + py
                got = half(o_ref, kj, c) if split else o_ref.at[kj]
                pltpu.make_async_remote_copy(
                    src_ref=got, dst_ref=got, send_sem=ici_send.at[3 * a + j],
                    recv_sem=ici_recv.at[3 * a + j], device_id=(px, py, c),
                    device_id_type=MESH).wait_recv()
                if split:
                    fw = pltpu.make_async_remote_copy(
                        src_ref=got, dst_ref=got, send_sem=d2d_send.at[3 * a + j],
                        recv_sem=d2d_recv.at[3 * a + j], device_id=(x, y, 1 - c),
                        device_id_type=MESH)
                    fw.start()
                    sends.append(fw)
        for a, o_ref in enumerate(outs):
            for j, (px, py) in enumerate(chips):
                other = half(o_ref, 2 * px + py, 1 - c)
                pltpu.make_async_remote_copy(
                    src_ref=other, dst_ref=other, send_sem=d2d_send.at[3 * a + j],
                    recv_sem=d2d_recv.at[3 * a + j], device_id=(x, y, 1 - c),
                    device_id_type=MESH).wait_recv()
        for cp in sends:
            cp.wait_send()

    vm = pl.BlockSpec(memory_space=pltpu.VMEM)
    out_shape = [SDS((N_CHIPS,) + s.shape, BF16) for s in shards] + [SDS((N_CHIPS,) + small.shape, F32)]
    return pl.pallas_call(
        body, name="gather_weights", out_shape=out_shape,
        in_specs=[vm] * (n + 1), out_specs=[vm] * (n + 1),
        scratch_shapes=[pltpu.SemaphoreType.DMA((3 * (n + 1),)), pltpu.SemaphoreType.DMA((3 * (n + 1),)),
                        pltpu.SemaphoreType.DMA((3 * n,)), pltpu.SemaphoreType.DMA((3 * n,))],
        compiler_params=pltpu.CompilerParams(vmem_limit_bytes=VMEM_LIMIT),
    )(*shards, small)


def _a_in(x, g_pre, win_g, tm):
    s = x.shape[0]

    def body(x_ref, g_ref, w_ref, proj_ref, n1_ref):
        @pl.when(pl.program_id(1) == 0)
        def _():
            xv = x_ref[...]
            n1_ref[...] = (xv * _rms_scale(xv) * g_ref[...]).astype(BF16)
        proj_ref[...] = _nn(n1_ref[...], w_ref[0])

    return pl.pallas_call(
        body, name="a_in", grid=(s // tm, 4),
        in_specs=[pl.BlockSpec((tm, D), lambda i, j: (i, 0)), pl.BlockSpec((1, D), lambda i, j: (0, 0)),
                  pl.BlockSpec((1, D, D), lambda i, j: (j, 0, 0))],
        out_specs=[pl.BlockSpec((tm, D), lambda i, j: (i, j)), pl.BlockSpec((tm, D), lambda i, j: (i, 0))],
        out_shape=[SDS((s, 4 * D), F32), SDS((s, D), BF16)],
        compiler_params=_params(("parallel", "arbitrary")),
    )(x, g_pre, win_g)


def _shift_rows(v, before, rows):
    v1 = jnp.where(rows >= 1, pltpu.roll(v, 1, 0), before[7:8, :])
    v2 = jnp.where(rows >= 2, pltpu.roll(v, 2, 0), jnp.where(rows == 1, before[7:8, :], before[6:7, :]))
    return v1, v2


def _a_mix(proj, x, conv_w, w_out, g_post, tm):
    s = x.shape[0]

    def body(proj_ref, x_ref, cw_ref, w_ref, g_ref, ya_ref, oa_ref, h1_ref, carry):
        @pl.when(pl.program_id(0) == 0)
        def _():
            carry[...] = jnp.zeros_like(carry)
        v = proj_ref[:, D:2 * D] * proj_ref[:, 2 * D:3 * D]
        rows = lax.broadcasted_iota(jnp.int32, (tm, D), 0)
        v1, v2 = _shift_rows(v, carry[...], rows)
        carry[...] = v[tm - 8:tm, :]
        conv = cw_ref[0:1, :] * v2 + cw_ref[1:2, :] * v1 + cw_ref[2:3, :] * v
        _, sz = _silu_parts(proj_ref[:, 3 * D:4 * D])
        ya = (proj_ref[:, 0:D] * conv * sz).astype(BF16)
        ya_ref[...] = ya
        oa = _nn(ya, w_ref[...])
        oa_ref[...] = oa
        h1_ref[...] = x_ref[...] + oa * _rms_scale(oa) * g_ref[...]

    row = lambda i: (i, 0)
    fix = lambda i: (0, 0)
    return pl.pallas_call(
        body, name="a_mix", grid=(s // tm,),
        in_specs=[pl.BlockSpec((tm, 4 * D), row), pl.BlockSpec((tm, D), row), pl.BlockSpec((8, D), fix),
                  pl.BlockSpec((D, D), fix), pl.BlockSpec((1, D), fix)],
        out_specs=[pl.BlockSpec((tm, D), row)] * 3,
        out_shape=[SDS((s, D), BF16), SDS((s, D), F32), SDS((s, D), F32)],
        scratch_shapes=[pltpu.VMEM((8, D), F32)],
        compiler_params=_params(("arbitrary",)),
    )(proj, x, conv_w, w_out, g_post)


def _b_in(h1, g_kv, g_pre, w_kv, wbin_g, tm):
    s = h1.shape[0]

    def body(h_ref, gk_ref, gb_ref, wkv_ref, wb_ref, nk_ref, nb_ref, kv_ref, q_ref, z_ref):
        h = h_ref[...]
        hh = h * _rms_scale(h)
        nk = (hh * gk_ref[...]).astype(BF16)
        nb = (hh * gb_ref[...]).astype(BF16)
        nk_ref[...] = nk
        nb_ref[...] = nb
        kv_ref[...] = _nn(nk, wkv_ref[...]).astype(BF16)
        for j in range(2):
            q_ref[:, 512 * j:512 * (j + 1)] = (_nn(nb, wb_ref[j]) * Q_SCALE).astype(BF16)
            z_ref[:, 512 * j:512 * (j + 1)] = _nn(nb, wb_ref[2 + j])

    row = lambda i: (i, 0)
    fix = lambda i: (0, 0)
    return pl.pallas_call(
        body, name="b_in", grid=(s // tm,),
        in_specs=[pl.BlockSpec((tm, D), row), pl.BlockSpec((1, D), fix), pl.BlockSpec((1, D), fix),
                  pl.BlockSpec((D, 2 * KV_W), fix), pl.BlockSpec((4, D, 512), lambda i: (0, 0, 0))],
        out_specs=[pl.BlockSpec((tm, D), row), pl.BlockSpec((tm, D), row), pl.BlockSpec((tm, 2 * KV_W), row),
                   pl.BlockSpec((tm, D), row), pl.BlockSpec((tm, D), row)],
        out_shape=[SDS((s, D), BF16), SDS((s, D), BF16), SDS((s, 2 * KV_W), BF16), SDS((s, D), BF16),
                   SDS((s, D), F32)],
        compiler_params=_params(("parallel",)),
    )(h1, g_kv, g_pre, w_kv, wbin_g)


def _band_buckets():
    q = lax.broadcasted_iota(jnp.int32, (BLK, 2 * BLK), 0)
    k = lax.broadcasted_iota(jnp.int32, (BLK, 2 * BLK), 1)
    dist = q + BLK - k
    bucket = jnp.where(dist < MAX_EXACT, dist, MAX_EXACT)
    for t in BUCKET_THRESHOLDS:
        bucket = bucket + jnp.where(dist >= t, 1, 0)
    in_window = (dist >= 0) & (dist < BLK)
    return jnp.where(in_window, bucket, -1)


def _bias_table(rel_bias):
    def body(rb_ref, tab_ref):
        bucket = _band_buckets()
        for h in range(N_HEADS):
            acc = jnp.where(bucket < 0, NEG_INF, 0.0).astype(F32)
            for b in range(N_BUCKETS):
                acc = jnp.where(bucket == b, rb_ref[b, h], acc)
            tab_ref[h] = acc

    return pl.pallas_call(
        body, name="bias_table", out_shape=SDS((N_HEADS, BLK, 2 * BLK), F32),
        in_specs=[pl.BlockSpec(memory_space=pltpu.SMEM)],
        out_specs=pl.BlockSpec(memory_space=pltpu.VMEM),
    )(rel_bias)


def _bias_fold(dtab):
    def body(dtab_ref, out_ref):
        bucket = _band_buckets()
        row = lax.broadcasted_iota(jnp.int32, (N_BUCKETS, 128), 0)
        lane = lax.broadcasted_iota(jnp.int32, (N_BUCKETS, 128), 1)
        acc = jnp.zeros((N_BUCKETS, 128), F32)
        for h in range(N_HEADS):
            dt = dtab_ref[h]
            for b in range(N_BUCKETS):
                val = jnp.sum(jnp.where(bucket == b, dt, 0.0))
                acc = acc + jnp.where((row == b) & (lane == h), val, 0.0)
        out_ref[...] = acc

    return pl.pallas_call(
        body, name="bias_fold", out_shape=SDS((N_BUCKETS, 128), F32),
        in_specs=[pl.BlockSpec(memory_space=pltpu.VMEM)],
        out_specs=pl.BlockSpec(memory_space=pltpu.VMEM),
    )(dtab)


def _pair_operands(prev, cur):
    t = jnp.concatenate([prev, cur], axis=0).astype(F32)
    tr = pltpu.roll(t, HEAD_DIM, 1)
    lo = lax.broadcasted_iota(jnp.int32, t.shape, 1) < HEAD_DIM
    zero = jnp.zeros_like(t)
    head0 = jnp.concatenate([jnp.where(lo, t, zero), jnp.where(lo, zero, tr)], axis=0).astype(BF16)
    head1 = jnp.concatenate([jnp.where(lo, tr, zero), jnp.where(lo, zero, t)], axis=0).astype(BF16)
    return head0, head1


def _pair_fold(d0, d1):
    lo = lax.broadcasted_iota(jnp.int32, (2 * BLK, KV_W), 1) < HEAD_DIM
    zero = jnp.zeros((2 * BLK, KV_W), F32)
    g0 = jnp.where(lo, d0[0:256], zero) + pltpu.roll(jnp.where(lo, zero, d0[256:512]), HEAD_DIM, 1)
    g1 = pltpu.roll(jnp.where(lo, d1[0:256], zero), HEAD_DIM, 1) + jnp.where(lo, zero, d1[256:512])
    return g0 + g1


def _first_block_mask(n):
    col = lax.broadcasted_iota(jnp.int32, (BLK, 2 * BLK), 1)
    return jnp.where((n == 0) & (col < BLK), NEG_INF, 0.0).astype(F32)


def _attn_fwd(q, kv, tab, sinks):
    s = q.shape[0]

    def body(sink_ref, q_ref, kp_ref, kc_ref, vp_ref, vc_ref, tab_ref, att_ref, stats_ref):
        n = pl.program_id(0)
        k2 = _pair_operands(kp_ref[...], kc_ref[...])
        v2 = _pair_operands(vp_ref[...], vc_ref[...])
        first = _first_block_mask(n)
        lane = lax.broadcasted_iota(jnp.int32, (BLK, 128), 1)
        stats = jnp.zeros((BLK, 128), F32)
        for j in range(N_HEADS // 2):
            kh = (2 * j) // GROUP
            sc = _nt(q_ref[:, 128 * j:128 * (j + 1)], k2[kh])
            ps = []
            for e in range(2):
                h = 2 * j + e
                sink = sink_ref[h]
                lg = sc[:, 256 * e:256 * (e + 1)] + tab_ref[h] + first
                m = jnp.maximum(jnp.max(lg, axis=-1, keepdims=True), sink)
                ex = jnp.exp(lg - m)
                den = jnp.sum(ex, axis=-1, keepdims=True) + jnp.exp(sink - m)
                ps.append(ex * (1.0 / den))
                stats = jnp.where(lane == h, m + jnp.log(den), stats)
            p2 = jnp.concatenate(ps, axis=1).astype(BF16)
            att_ref[:, 128 * j:128 * (j + 1)] = _nn(p2, v2[kh])
        stats_ref[...] = stats

    cur = lambda n: (n, 0)
    prev = lambda n: (jnp.maximum(n - 1, 0), 0)
    return pl.pallas_call(
        body, name="attn_fwd", grid=(s // BLK,),
        in_specs=[pl.BlockSpec(memory_space=pltpu.SMEM), pl.BlockSpec((BLK, D), cur),
                  pl.BlockSpec((BLK, KV_W), prev), pl.BlockSpec((BLK, KV_W), cur),
                  pl.BlockSpec((BLK, KV_W), lambda n: (jnp.maximum(n - 1, 0), 1)),
                  pl.BlockSpec((BLK, KV_W), lambda n: (n, 1)),
                  pl.BlockSpec((N_HEADS, BLK, 2 * BLK), lambda n: (0, 0, 0))],
        out_specs=[pl.BlockSpec((BLK, D), cur), pl.BlockSpec((BLK, 128), cur)],
        out_shape=[SDS((s, D), F32), SDS((s, 128), F32)],
        compiler_params=_params(("parallel",)),
    )(sinks, q, kv, kv, kv, kv, tab)


def _mid(att, zb, h1, tgt, w_out, g_post, tm):
    s = att.shape[0]

    def body(att_ref, z_ref, h1_ref, t_ref, w_ref, g_ref,
             ob_ref, dy_ref, dh_ref, dqz_ref, datt_ref, loss_ref, dg_ref):
        @pl.when(pl.program_id(0) == 0)
        def _():
            loss_ref[...] = jnp.zeros_like(loss_ref)
            dg_ref[...] = jnp.zeros_like(dg_ref)
        att = att_ref[...]
        z = z_ref[...]
        sg, sz = _silu_parts(z)
        ob = (att * sz).astype(BF16)
        ob_ref[...] = ob
        y2 = _nn(ob, w_ref[...])
        r2 = _rms_scale(y2)
        yh = y2 * r2
        g = g_ref[...]
        err = (h1_ref[...] + yh * g) - t_ref[...]
        loss_ref[...] += jnp.sum(jnp.sum(err * err, axis=-1, keepdims=True) / D)
        dh = err / D
        dh_ref[...] = dh
        _acc_row(dg_ref, 0, jnp.sum(dh * yh, axis=0, keepdims=True))
        dyh = dh * g
        dy = (r2 * (dyh - yh * jnp.mean(dyh * yh, axis=-1, keepdims=True))).astype(BF16)
        dy_ref[...] = dy
        dob = _nt(dy, w_ref[...])
        datt_ref[...] = (dob * sz).astype(BF16)
        dqz_ref[...] = (dob * att * _dsilu(z, sg)).astype(BF16)

    row = lambda i: (i, 0)
    fix = lambda i: (0, 0)
    return pl.pallas_call(
        body, name="mid", grid=(s // tm,),
        in_specs=[pl.BlockSpec((tm, D), row)] * 4 + [pl.BlockSpec((D, D), fix), pl.BlockSpec((1, D), fix)],
        out_specs=[pl.BlockSpec((tm, D), row), pl.BlockSpec((tm, D), row), pl.BlockSpec((tm, D), row),
                   pl.BlockSpec((tm, D), lambda i: (i, 1)), pl.BlockSpec((tm, D), row),
                   pl.BlockSpec((8, 128), fix), pl.BlockSpec((8, D), fix)],
        out_shape=[SDS((s, D), BF16), SDS((s, D), BF16), SDS((s, D), F32), SDS((s, 2 * D), BF16),
                   SDS((s, D), BF16), SDS((8, 128), F32), SDS((8, D), F32)],
        compiler_params=_params(("arbitrary",)),
    )(att, zb, h1, tgt, w_out, g_post)


def _attn_bwd(q, kv, datt, stats, tab, sinks, dqz):
    s = q.shape[0]
    nb = s // BLK

    def body(sink_ref, q_ref, kp_ref, kc_ref, vp_ref, vc_ref, da_ref, st_ref, tab_ref, dqz_in,
             dq_ref, dkv_ref, dtab_ref, dsink_ref, dk_carry, dv_carry):
        del dqz_in
        n = pl.program_id(0)

        @pl.when(n == 0)
        def _():
            dtab_ref[...] = jnp.zeros_like(dtab_ref)
            dsink_ref[...] = jnp.zeros_like(dsink_ref)
            dk_carry[...] = jnp.zeros_like(dk_carry)
            dv_carry[...] = jnp.zeros_like(dv_carry)

        @pl.when(n < nb)
        def _():
            k2 = _pair_operands(kp_ref[...], kc_ref[...])
            v2 = _pair_operands(vp_ref[...], vc_ref[...])
            first = _first_block_mask(n)
            lane = lax.broadcasted_iota(jnp.int32, (BLK, 128), 1)
            lane8 = lax.broadcasted_iota(jnp.int32, (8, 128), 1)
            stats = st_ref[...]
            dk2 = [jnp.zeros((4 * BLK, KV_W), F32), jnp.zeros((4 * BLK, KV_W), F32)]
            dv2 = [jnp.zeros((4 * BLK, KV_W), F32), jnp.zeros((4 * BLK, KV_W), F32)]
            dsink = jnp.zeros((8, 128), F32)
            for j in range(N_HEADS // 2):
                kh = (2 * j) // GROUP
                q2 = q_ref[:, 128 * j:128 * (j + 1)]
                da2 = da_ref[:, 128 * j:128 * (j + 1)]
                sc = _nt(q2, k2[kh])
                dp = _nt(da2, v2[kh])
                ps, dss = [], []
                for e in range(2):
                    h = 2 * j + e
                    lse = jnp.sum(jnp.where(lane == h, stats, 0.0), axis=-1, keepdims=True)
                    p = jnp.exp(sc[:, 256 * e:256 * (e + 1)] + tab_ref[h] + first - lse)
                    dpe = dp[:, 256 * e:256 * (e + 1)]
                    delta = jnp.sum(p * dpe, axis=-1, keepdims=True)
                    ds = p * (dpe - delta)
                    dtab_ref[h] += ds
                    dsink = dsink - jnp.where(lane8 == h, jnp.sum(jnp.exp(sink_ref[h] - lse) * delta), 0.0)
                    ps.append(p)
                    dss.append(ds)
                p2 = jnp.concatenate(ps, axis=1).astype(BF16)
                ds2 = jnp.concatenate(dss, axis=1).astype(BF16)
                dq_ref[:, 128 * j:128 * (j + 1)] = (_nn(ds2, k2[kh]) * Q_SCALE).astype(BF16)
                dk2[kh] = dk2[kh] + _tn(ds2, q2)
                dv2[kh] = dv2[kh] + _tn(p2, da2)
            dsink_ref[...] += dsink
            dkk = _pair_fold(dk2[0], dk2[1])
            dvv = _pair_fold(dv2[0], dv2[1])
            dkv_ref[:, 0:KV_W] = (dk_carry[...] + dkk[0:BLK]).astype(BF16)
            dkv_ref[:, KV_W:2 * KV_W] = (dv_carry[...] + dvv[0:BLK]).astype(BF16)
            dk_carry[...] = dkk[BLK:2 * BLK]
            dv_carry[...] = dvv[BLK:2 * BLK]

        @pl.when(n == nb)
        def _():
            dkv_ref[:, 0:KV_W] = dk_carry[...].astype(BF16)
            dkv_ref[:, KV_W:2 * KV_W] = dv_carry[...].astype(BF16)

    cur = lambda n: (jnp.minimum(n, nb - 1), 0)
    prev = lambda n: (jnp.clip(n - 1, 0, nb - 1), 0)
    return pl.pallas_call(
        body, name="attn_bwd", grid=(nb + 1,),
        in_specs=[pl.BlockSpec(memory_space=pltpu.SMEM), pl.BlockSpec((BLK, D), cur),
                  pl.BlockSpec((BLK, KV_W), prev), pl.BlockSpec((BLK, KV_W), cur),
                  pl.BlockSpec((BLK, KV_W), lambda n: (jnp.clip(n - 1, 0, nb - 1), 1)),
                  pl.BlockSpec((BLK, KV_W), lambda n: (jnp.minimum(n, nb - 1), 1)),
                  pl.BlockSpec((BLK, D), cur), pl.BlockSpec((BLK, 128), cur),
                  pl.BlockSpec((N_HEADS, BLK, 2 * BLK), lambda n: (0, 0, 0)),
                  pl.BlockSpec(memory_space=pl.ANY)],
        out_specs=[pl.BlockSpec((BLK, D), cur), pl.BlockSpec((BLK, 2 * KV_W), prev),
                   pl.BlockSpec((N_HEADS, BLK, 2 * BLK), lambda n: (0, 0, 0)),
                   pl.BlockSpec((8, 128), lambda n: (0, 0))],
        out_shape=[SDS((s, 2 * D), BF16), SDS((s, 2 * KV_W), BF16), SDS((N_HEADS, BLK, 2 * BLK), F32),
                   SDS((8, 128), F32)],
        scratch_shapes=[pltpu.VMEM((BLK, KV_W), F32), pltpu.VMEM((BLK, KV_W), F32)],
        input_output_aliases={9: 0},
        compiler_params=_params(("arbitrary",)),
    )(sinks, q, kv, kv, kv, kv, datt, stats, tab, dqz)


def _b_bwd(dqz, dkv, h1, dh2, oa, wbin_g, w_kv, g_kv, g_pre, g_apost, tm):
    s = h1.shape[0]

    def body(dqz_ref, dkv_ref, h_ref, dh2_ref, oa_ref, wb_ref, wkv_ref, gk_ref, gb_ref, ga_ref,
             dh1_ref, doa_ref, dg_ref):
        @pl.when(pl.program_id(0) == 0)
        def _():
            dg_ref[...] = jnp.zeros_like(dg_ref)
        dnb = _nt(dqz_ref[:, 0:512], wb_ref[0])
        for j in range(1, 4):
            dnb = dnb + _nt(dqz_ref[:, 512 * j:512 * (j + 1)], wb_ref[j])
        dnk = _nt(dkv_ref[...], wkv_ref[...])
        h = h_ref[...]
        r = _rms_scale(h)
        hh = h * r
        _acc_row(dg_ref, 0, jnp.sum(dnk * hh, axis=0, keepdims=True))
        _acc_row(dg_ref, 1, jnp.sum(dnb * hh, axis=0, keepdims=True))
        dhh = dnb * gb_ref[...] + dnk * gk_ref[...]
        dh1 = dh2_ref[...] + r * (dhh - hh * jnp.mean(dhh * hh, axis=-1, keepdims=True))
        dh1_ref[...] = dh1
        oa = oa_ref[...]
        ra = _rms_scale(oa)
        oh = oa * ra
        _acc_row(dg_ref, 2, jnp.sum(dh1 * oh, axis=0, keepdims=True))
        doh = dh1 * ga_ref[...]
        doa_ref[...] = (ra * (doh - oh * jnp.mean(doh * oh, axis=-1, keepdims=True))).astype(BF16)

    row = lambda i: (i, 0)
    fix = lambda i: (0, 0)
    return pl.pallas_call(
        body, name="b_bwd", grid=(s // tm,),
        in_specs=[pl.BlockSpec((tm, 2 * D), row), pl.BlockSpec((tm, 2 * KV_W), row), pl.BlockSpec((tm, D), row),
                  pl.BlockSpec((tm, D), row), pl.BlockSpec((tm, D), row),
                  pl.BlockSpec((4, D, 512), lambda i: (0, 0, 0)), pl.BlockSpec((D, 2 * KV_W), fix),
                  pl.BlockSpec((1, D), fix), pl.BlockSpec((1, D), fix), pl.BlockSpec((1, D), fix)],
        out_specs=[pl.BlockSpec((tm, D), row), pl.BlockSpec((tm, D), row), pl.BlockSpec((8, D), fix)],
        out_shape=[SDS((s, D), F32), SDS((s, D), BF16), SDS((8, D), F32)],
        compiler_params=_params(("arbitrary",)),
    )(dqz, dkv, h1, dh2, oa, wbin_g, w_kv, g_kv, g_pre, g_apost)


def _a_bwd(doa, proj, conv_w, w_out, tm):
    s = doa.shape[0]
    nt = s // tm

    def body(doa_ref, proj_ref, halo_ref, cw_ref, w_ref, dproj_ref, dcw_ref, carry):
        i = pl.program_id(0)
        r = nt - 1 - i

        @pl.when(i == 0)
        def _():
            dcw_ref[...] = jnp.zeros_like(dcw_ref)
            carry[...] = jnp.zeros_like(carry)
        dya = _nt(doa_ref[...], w_ref[...])
        bg = proj_ref[:, 0:D]
        cg = proj_ref[:, D:2 * D]
        u = proj_ref[:, 2 * D:3 * D]
        z = proj_ref[:, 3 * D:4 * D]
        v = cg * u
        before = jnp.where(r > 0, halo_ref[:, D:2 * D] * halo_ref[:, 2 * D:3 * D], 0.0)
        rows = lax.broadcasted_iota(jnp.int32, (tm, D), 0)
        v1, v2 = _shift_rows(v, before, rows)
        conv = cw_ref[0:1, :] * v2 + cw_ref[1:2, :] * v1 + cw_ref[2:3, :] * v
        sg, sz = _silu_parts(z)
        dproj_ref[:, 0:D] = (dya * conv * sz).astype(BF16)
        dproj_ref[:, 3 * D:4 * D] = (dya * bg * conv * _dsilu(z, sg)).astype(BF16)
        dconv = dya * bg * sz
        _acc_row(dcw_ref, 0, jnp.sum(dconv * v2, axis=0, keepdims=True))
        _acc_row(dcw_ref, 1, jnp.sum(dconv * v1, axis=0, keepdims=True))
        _acc_row(dcw_ref, 2, jnp.sum(dconv * v, axis=0, keepdims=True))
        after = carry[...]
        up1 = jnp.where(rows < tm - 1, pltpu.roll(dconv, tm - 1, 0), after[0:1, :])
        up2 = jnp.where(rows < tm - 2, pltpu.roll(dconv, tm - 2, 0),
                        jnp.where(rows == tm - 2, after[0:1, :], after[1:2, :]))
        carry[...] = dconv[0:8, :]
        dv = cw_ref[2:3, :] * dconv + cw_ref[1:2, :] * up1 + cw_ref[0:1, :] * up2
        dproj_ref[:, D:2 * D] = (dv * u).astype(BF16)
        dproj_ref[:, 2 * D:3 * D] = (dv * cg).astype(BF16)

    rev = lambda i: (nt - 1 - i, 0)
    fix = lambda i: (0, 0)
    halo = lambda i: (jnp.maximum((nt - 1 - i) * (tm // 8) - 1, 0), 0)
    return pl.pallas_call(
        body, name="a_bwd", grid=(nt,),
        in_specs=[pl.BlockSpec((tm, D), rev), pl.BlockSpec((tm, 4 * D), rev), pl.BlockSpec((8, 4 * D), halo),
                  pl.BlockSpec((8, D), fix), pl.BlockSpec((D, D), fix)],
        out_specs=[pl.BlockSpec((tm, 4 * D), rev), pl.BlockSpec((8, D), fix)],
        out_shape=[SDS((s, 4 * D), BF16), SDS((8, D), F32)],
        scratch_shapes=[pltpu.VMEM((8, D), F32)],
        compiler_params=_params(("arbitrary",)),
    )(doa, proj, proj, conv_w, w_out)


def _a_in_bwd(dproj, x, dh1, win_g, g_pre, tm):
    s = x.shape[0]

    def body(dp_ref, x_ref, dh_ref, w_ref, g_ref, gx_ref, dg_ref):
        @pl.when(pl.program_id(0) == 0)
        def _():
            dg_ref[...] = jnp.zeros_like(dg_ref)
        dn = _nt(dp_ref[:, 0:D], w_ref[0])
        for j in range(1, 4):
            dn = dn + _nt(dp_ref[:, D * j:D * (j + 1)], w_ref[j])
        xv = x_ref[...]
        r = _rms_scale(xv)
        xh = xv * r
        _acc_row(dg_ref, 0, jnp.sum(dn * xh, axis=0, keepdims=True))
        dxh = dn * g_ref[...]
        gx_ref[...] = dh_ref[...] + r * (dxh - xh * jnp.mean(dxh * xh, axis=-1, keepdims=True))

    row = lambda i: (i, 0)
    fix = lambda i: (0, 0)
    return pl.pallas_call(
        body, name="a_in_bwd", grid=(s // tm,),
        in_specs=[pl.BlockSpec((tm, 4 * D), row), pl.BlockSpec((tm, D), row), pl.BlockSpec((tm, D), row),
                  pl.BlockSpec((4, D, D), lambda i: (0, 0, 0)), pl.BlockSpec((1, D), fix)],
        out_specs=[pl.BlockSpec((tm, D), row), pl.BlockSpec((8, D), fix)],
        out_shape=[SDS((s, D), F32), SDS((8, D), F32)],
        compiler_params=_params(("arbitrary",)),
    )(dproj, x, dh1, win_g, g_pre)


def _dw(a, b, tn, tmw, name):
    s, k = a.shape
    n = b.shape[1]

    def body(a_ref, b_ref, o_ref):
        @pl.when(pl.program_id(1) == 0)
        def _():
            o_ref[...] = jnp.zeros_like(o_ref)
        o_ref[0] += _tn(a_ref[...], b_ref[...])

    return pl.pallas_call(
        body, name=name, grid=(n // tn, s // tmw),
        in_specs=[pl.BlockSpec((tmw, k), lambda j, t: (t, 0)), pl.BlockSpec((tmw, tn), lambda j, t: (t, j))],
        out_specs=pl.BlockSpec((1, k, tn), lambda j, t: (j, 0, 0)),
        out_shape=SDS((n // tn, k, tn), F32),
        compiler_params=_params(("parallel", "arbitrary")),
    )(a, b)


def _grads_to_sibling(grads, smalls):
    n = len(grads)

    def body(*refs):
        gs, small_in = refs[:n], refs[n]
        rs, small_all = refs[n + 1:2 * n + 1], refs[2 * n + 1]
        dsend, drecv, ssend, srecv = refs[2 * n + 2:]
        x, y, c = lax.axis_index("x"), lax.axis_index("y"), lax.axis_index("c")
        me = 4 * x + 2 * y + c
        copies = []
        for a, (g, r) in enumerate(zip(gs, rs)):
            h = g.shape[1] // 2
            src = g.at[:, pl.ds(pl.multiple_of((1 - c) * h, 8), h), :]
            cp = pltpu.make_async_remote_copy(src_ref=src, dst_ref=r, send_sem=dsend.at[a], recv_sem=drecv.at[a],
                                              device_id=(x, y, 1 - c), device_id_type=MESH)
            cp.start()
            copies.append(cp)
        small_all[me] = small_in[...]
        for rel in range(1, N_DEV):
            fx, fy, fc = rel >> 2, (rel >> 1) & 1, rel & 1
            peer = (x + fx - 2 * x * fx, y + fy - 2 * y * fy, c + fc - 2 * c * fc)
            cp = pltpu.make_async_remote_copy(src_ref=small_in, dst_ref=small_all.at[me], send_sem=ssend.at[rel - 1],
                                              recv_sem=srecv.at[rel - 1], device_id=peer, device_id_type=MESH)
            cp.start()
            copies.append(cp)
        for a, (g, r) in enumerate(zip(gs, rs)):
            pltpu.make_async_remote_copy(src_ref=r, dst_ref=r, send_sem=dsend.at[a], recv_sem=drecv.at[a],
                                         device_id=(x, y, 1 - c), device_id_type=MESH).wait_recv()
        for rel in range(1, N_DEV):
            fx, fy, fc = rel >> 2, (rel >> 1) & 1, rel & 1
            peer = (x + fx - 2 * x * fx, y + fy - 2 * y * fy, c + fc - 2 * c * fc)
            src = 4 * peer[0] + 2 * peer[1] + peer[2]
            pltpu.make_async_remote_copy(src_ref=small_in, dst_ref=small_all.at[src], send_sem=ssend.at[rel - 1],
                                         recv_sem=srecv.at[rel - 1], device_id=peer, device_id_type=MESH).wait_recv()
        for cp in copies:
            cp.wait_send()

    anyspace = pl.BlockSpec(memory_space=pl.ANY)
    vm = pl.BlockSpec(memory_space=pltpu.VMEM)
    out_shape = [SDS((N_CHIPS, g.shape[1] // 2, g.shape[2]), F32) for g in grads]
    out_shape.append(SDS((N_DEV,) + smalls.shape, F32))
    return pl.pallas_call(
        body, name="grads_to_sibling", out_shape=out_shape,
        in_specs=[anyspace] * n + [vm], out_specs=[anyspace] * n + [vm],
        scratch_shapes=[pltpu.SemaphoreType.DMA((n,)), pltpu.SemaphoreType.DMA((n,)),
                        pltpu.SemaphoreType.DMA((N_DEV - 1,)), pltpu.SemaphoreType.DMA((N_DEV - 1,))],
    )(*grads, smalls)


def _add_sibling(core, g, r, name):
    _, rows, cols = g.shape
    h = rows // 2
    tr = min(h, 256)
    nh = h // tr

    def body(core_ref, g_ref, r_ref, t_ref):
        del core_ref
        t_ref[...] = g_ref[...] + r_ref[...]

    return pl.pallas_call(
        body, name=name,
        grid_spec=pltpu.PrefetchScalarGridSpec(
            num_scalar_prefetch=1, grid=(N_CHIPS, nh),
            in_specs=[pl.BlockSpec((1, tr, cols), lambda k, i, core: (k, core[0] * nh + i, 0)),
                      pl.BlockSpec((1, tr, cols), lambda k, i, core: (k, i, 0))],
            out_specs=pl.BlockSpec((1, tr, cols), lambda k, i, core: (k, i, 0))),
        out_shape=SDS((N_CHIPS, h, cols), F32),
        compiler_params=_params(("parallel", "parallel")),
    )(core, g, r)


def _grads_to_owner(parts):
    n = len(parts)

    def body(*refs):
        ts, rs = refs[:n], refs[n:2 * n]
        send, recv = refs[2 * n:]
        x, y, c = lax.axis_index("x"), lax.axis_index("y"), lax.axis_index("c")
        chips = [(x, 1 - y), (1 - x, y), (1 - x, 1 - y)]
        copies = []
        for a, (t, r) in enumerate(zip(ts, rs)):
            for j, (px, py) in enumerate(chips):
                cp = pltpu.make_async_remote_copy(src_ref=t.at[2 * px + py], dst_ref=r.at[j], send_sem=send.at[3 * a + j],
                                                  recv_sem=recv.at[3 * a + j], device_id=(px, py, c), device_id_type=MESH)
                cp.start()
                copies.append(cp)
        for cp in copies:
            cp.wait()

    anyspace = pl.BlockSpec(memory_space=pl.ANY)
    return pl.pallas_call(
        body, name="grads_to_owner", out_shape=[SDS((3,) + t.shape[1:], F32) for t in parts],
        in_specs=[anyspace] * n, out_specs=[anyspace] * n,
        scratch_shapes=[pltpu.SemaphoreType.DMA((3 * n,)), pltpu.SemaphoreType.DMA((3 * n,))],
    )(*parts)


def _add_chips(chip, t, r, name):
    _, h, cols = t.shape
    tr = min(h, 256)

    def body(chip_ref, t_ref, r_ref, o_ref):
        del chip_ref
        o_ref[...] = ((t_ref[0] + r_ref[0]) + r_ref[1]) + r_ref[2]

    return pl.pallas_call(
        body, name=name,
        grid_spec=pltpu.PrefetchScalarGridSpec(
            num_scalar_prefetch=1, grid=(h // tr,),
            in_specs=[pl.BlockSpec((1, tr, cols), lambda i, chip: (chip[0], i, 0)),
                      pl.BlockSpec((3, tr, cols), lambda i, chip: (0, i, 0))],
            out_specs=pl.BlockSpec((tr, cols), lambda i, chip: (i, 0))),
        out_shape=SDS((h, cols), F32),
        compiler_params=_params(("parallel",)),
    )(chip, t, r)


def _share_with_sibling(halves):
    n = len(halves)

    def body(*refs):
        hs, fs = refs[:n], refs[n:2 * n]
        send, recv, local = refs[2 * n:]
        x, y, c = lax.axis_index("x"), lax.axis_index("y"), lax.axis_index("c")
        copies = []
        for a, (hf, full) in enumerate(zip(hs, fs)):
            h = hf.shape[0]
            mine = full.at[pl.ds(pl.multiple_of(c * h, 8), h)]
            lc = pltpu.make_async_copy(hf, mine, local.at[a])
            lc.start()
            cp = pltpu.make_async_remote_copy(src_ref=hf, dst_ref=mine, send_sem=send.at[a], recv_sem=recv.at[a],
                                              device_id=(x, y, 1 - c), device_id_type=MESH)
            cp.start()
            copies.append((lc, cp))
        for a, (hf, full) in enumerate(zip(hs, fs)):
            h = hf.shape[0]
            theirs = full.at[pl.ds(pl.multiple_of((1 - c) * h, 8), h)]
            pltpu.make_async_remote_copy(src_ref=hf, dst_ref=theirs, send_sem=send.at[a], recv_sem=recv.at[a],
                                         device_id=(x, y, 1 - c), device_id_type=MESH).wait_recv()
        for lc, cp in copies:
            lc.wait()
            cp.wait_send()

    anyspace = pl.BlockSpec(memory_space=pl.ANY)
    return pl.pallas_call(
        body, name="share_with_sibling", out_shape=[SDS((2 * hf.shape[0], hf.shape[1]), F32) for hf in halves],
        in_specs=[anyspace] * n, out_specs=[anyspace] * n,
        scratch_shapes=[pltpu.SemaphoreType.DMA((n,)), pltpu.SemaphoreType.DMA((n,)), pltpu.SemaphoreType.DMA((n,))],
    )(*halves)


def _sum_smalls(small_all):
    def body(all_ref, o_ref):
        acc = all_ref[0]
        for dev in range(1, N_DEV):
            acc = acc + all_ref[dev]
        o_ref[...] = acc

    return pl.pallas_call(
        body, name="sum_smalls", out_shape=SDS(small_all.shape[1:], F32),
        in_specs=[pl.BlockSpec(memory_space=pltpu.VMEM)], out_specs=pl.BlockSpec(memory_space=pltpu.VMEM),
    )(small_all)


def _adamw(g, w, m, v, name):
    rows, cols = g.shape
    tr = min(rows, 256)

    def body(g_ref, w_ref, m_ref, v_ref, d_ref, nm_ref, nv_ref):
        gv = g_ref[...]
        nm = ADAM_B1 * m_ref[...] + (1.0 - ADAM_B1) * gv
        nv = ADAM_B2 * v_ref[...] + (1.0 - ADAM_B2) * (gv * gv)
        nm_ref[...] = nm
        nv_ref[...] = nv
        m_hat = nm / (1.0 - ADAM_B1 ** ADAM_STEP)
        v_hat = nv / (1.0 - ADAM_B2 ** ADAM_STEP)
        d_ref[...] = -ADAM_LR * (m_hat / (jnp.sqrt(v_hat) + ADAM_EPS) + ADAM_WD * w_ref[...])

    spec = pl.BlockSpec((tr, cols), lambda i: (i, 0))
    return pl.pallas_call(
        body, name=name, grid=(rows // tr,), in_specs=[spec] * 4, out_specs=[spec] * 3,
        out_shape=[SDS(g.shape, F32)] * 3, compiler_params=_params(("parallel",)),
    )(g, w, m, v)


def _pad_rows(a, rows):
    return jnp.concatenate([a, jnp.zeros((rows - a.shape[0], a.shape[1]), a.dtype)], axis=0)


def _pad_cols(a, cols):
    return jnp.concatenate([a, jnp.zeros((a.shape[0], cols - a.shape[1]), a.dtype)], axis=1)


def kernel(x, a_pre_norm, a_w_in, a_conv_w, a_w_out, a_post_norm, kv_norm, w_kv, rel_bias, b_pre_norm, b_w_in, b_sinks, b_w_out, b_post_norm, loss_target, m_a_pre_norm, m_a_w_in, m_a_conv_w, m_a_w_out, m_a_post_norm, m_kv_norm, m_w_kv, m_rel_bias, m_b_pre_norm, m_b_w_in, m_b_sinks, m_b_w_out, m_b_post_norm, v_a_pre_norm, v_a_w_in, v_a_conv_w, v_a_w_out, v_a_post_norm, v_kv_norm, v_w_kv, v_rel_bias, v_b_pre_norm, v_b_w_in, v_b_sinks, v_b_w_out, v_b_post_norm):
    seq = x.shape[1]
    xs = x.reshape(seq, D)
    tgt = loss_target.reshape(seq, D)
    chip = 2 * lax.axis_index("x") + lax.axis_index("y")
    core = lax.axis_index("c")
    tm = _tile(seq, 512)
    tm_mix = _tile(seq, 256)
    tmw = _tile(seq, 1024)

    shards = [a_w_in[0], a_w_out[0], w_kv, b_w_in[0], b_w_out[0]]
    small_w = _pad_rows(jnp.concatenate([a_pre_norm, a_conv_w[0], a_post_norm], axis=0), 8)
    win_g, wouta_g, wkv_g, wbin_g, woutb_g, small_g = _gather_weights(shards, small_w)
    small_full = small_g.transpose(1, 0, 2).reshape(8, D)
    g_apre, conv_w, g_apost = small_full[0:1], _pad_rows(small_full[1:4], 8), small_full[4:5]
    wouta = wouta_g.reshape(D, D)
    wkv = wkv_g.reshape(D, 2 * KV_W)
    woutb = woutb_g.reshape(D, D)
    g_kv = kv_norm.reshape(1, D)

    proj, n1 = _a_in(xs, g_apre, win_g, tm)
    ya, oa, h1 = _a_mix(proj, xs, conv_w, wouta, g_apost, tm_mix)
    nk, nb, kv, q, zb = _b_in(h1, g_kv, b_pre_norm, wkv, wbin_g, tm)
    tab = _bias_table(rel_bias)
    sinks = b_sinks.reshape(N_HEADS)
    att, stats = _attn_fwd(q, kv, tab, sinks)
    ob, dy2, dh2, dqz, datt, loss_acc, dg_bpost = _mid(att, zb, h1, tgt, woutb, b_post_norm, tm)

    dqz, dkv, dtab, dsink = _attn_bwd(q, kv, datt, stats, tab, sinks, dqz)
    dh1, doa, dg_b = _b_bwd(dqz, dkv, h1, dh2, oa, wbin_g, wkv, g_kv, b_pre_norm, g_apost, tm)
    dproj, dconv_w = _a_bwd(doa, proj, conv_w, wouta, tm_mix)
    grad_x, dg_apre = _a_in_bwd(dproj, xs, dh1, win_g, g_apre, tm)
    dw_in = _dw(n1, dproj, D, tmw, "dw_a_in")
    dw_outa = _dw(ya, doa, D, tmw, "dw_a_out").reshape(N_CHIPS, D // 4, D)
    dw_kv = _dw(nk, dkv, 2 * KV_W, tmw, "dw_kv").reshape(N_CHIPS, D // 4, 2 * KV_W)
    dw_bin = _dw(nb, dqz, 512, tmw, "dw_b_in")
    dw_outb = _dw(ob, dy2, D, tmw, "dw_b_out").reshape(N_CHIPS, D // 4, D)
    drel = _bias_fold(dtab)

    smalls = jnp.concatenate([
        dg_apre[0:1], dconv_w[0:3], dg_b[2:3], dg_b[0:1], dg_b[1:2], dg_bpost[0:1],
        _pad_cols(drel[:, 0:N_HEADS].reshape(1, N_BUCKETS * N_HEADS), D), _pad_cols(dsink[0:1], D),
        _pad_cols(loss_acc[0:1], D), jnp.zeros((SMALL_ROWS - 11, D), F32)], axis=0)
    grads = [dw_in, dw_outa, dw_kv, dw_bin, dw_outb]
    *from_sibling, small_all = _grads_to_sibling(grads, smalls)
    core_arr = jnp.reshape(core, (1,)).astype(jnp.int32)
    chip_arr = jnp.reshape(chip, (1,)).astype(jnp.int32)
    names = ["a_w_in", "a_w_out", "w_kv", "b_w_in", "b_w_out"]
    parts = [_add_sibling(core_arr, g, r, "add_sibling_" + nm) for g, r, nm in zip(grads, from_sibling, names)]
    from_chips = _grads_to_owner(parts)
    halves = [_add_chips(chip_arr, t, r, "add_chips_" + nm) for t, r, nm in zip(parts, from_chips, names)]
    g_win, g_wouta, g_wkv, g_wbin, g_woutb = _share_with_sibling(halves)
    tot = _sum_smalls(small_all)

    big = {}
    for nm, g, w, m, v in [("a_w_in", g_win, a_w_in, m_a_w_in, v_a_w_in), ("a_w_out", g_wouta, a_w_out, m_a_w_out, v_a_w_out),
                           ("w_kv", g_wkv, w_kv, m_w_kv, v_w_kv), ("b_w_in", g_wbin, b_w_in, m_b_w_in, v_b_w_in),
                           ("b_w_out", g_woutb, b_w_out, m_b_w_out, v_b_w_out)]:
        shp = w.shape
        two = (shp[-2], shp[-1])
        d, nm_, nv_ = _adamw(g, w.reshape(two), m.reshape(two), v.reshape(two), "adamw_" + nm)
        big[nm] = (g.reshape(shp), d.reshape(shp), nm_.reshape(shp), nv_.reshape(shp))

    col0 = chip * (D // 4)
    sharded = lax.dynamic_slice(tot, (0, col0), (8, D // 4))
    g_shard = sharded
    w_shard = small_w
    m_shard = _pad_rows(jnp.concatenate([m_a_pre_norm, m_a_conv_w[0], m_a_post_norm], axis=0), 8)
    v_shard = _pad_rows(jnp.concatenate([v_a_pre_norm, v_a_conv_w[0], v_a_post_norm], axis=0), 8)
    ds_, ms_, vs_ = _adamw(g_shard, w_shard, m_shard, v_shard, "adamw_small_sharded")

    def rep_pack(kvn, bpre, bpost, rel, snk):
        rows = [kvn.reshape(1, D), bpre.reshape(1, D), bpost.reshape(1, D),
                _pad_cols(rel.reshape(1, N_BUCKETS * N_HEADS), D), _pad_cols(snk.reshape(1, N_HEADS), D)]
        return jnp.concatenate(rows + [jnp.zeros((3, D), F32)], axis=0)

    g_rep = tot[5:13]
    w_rep = rep_pack(kv_norm, b_pre_norm, b_post_norm, rel_bias, b_sinks)
    m_rep = rep_pack(m_kv_norm, m_b_pre_norm, m_b_post_norm, m_rel_bias, m_b_sinks)
    v_rep = rep_pack(v_kv_norm, v_b_pre_norm, v_b_post_norm, v_rel_bias, v_b_sinks)
    dr_, mr_, vr_ = _adamw(g_rep, w_rep, m_rep, v_rep, "adamw_small_replicated")

    def unshard(p):
        return {"a_pre_norm": p[0:1], "a_conv_w": p[1:4].reshape(1, 3, D // 4), "a_post_norm": p[4:5]}

    def unrep(p):
        return {"kv_norm": p[0], "b_pre_norm": p[1:2], "b_post_norm": p[2:3],
                "rel_bias": p[3, 0:N_BUCKETS * N_HEADS].reshape(N_BUCKETS, N_HEADS), "b_sinks": p[4:5, 0:N_HEADS]}

    order = ["a_pre_norm", "a_w_in", "a_conv_w", "a_w_out", "a_post_norm", "kv_norm", "w_kv", "rel_bias",
             "b_pre_norm", "b_w_in", "b_sinks", "b_w_out", "b_post_norm"]
    outs = []
    for which, sh, rp in [(0, g_shard, g_rep), (1, ds_, dr_), (2, ms_, mr_), (3, vs_, vr_)]:
        small = {**unshard(sh), **unrep(rp)}
        for nm in order:
            outs.append(big[nm][which] if nm in big else small[nm])
    loss = 0.5 * tot[10, 0]
    return (loss, grad_x.reshape(x.shape), *outs)
```

```python
import functools
import math

import jax
import jax.numpy as jnp
from jax import lax
from jax.experimental import pallas as pl
from jax.experimental.pallas import tpu as pltpu

F32 = jnp.float32
BF16 = jnp.bfloat16
MESH = pl.DeviceIdType.MESH
SDS = jax.ShapeDtypeStruct

D = 1024
HEAD_DIM = 64
N_HEADS = 16
GROUP = 8
KV_W = 128
BLK = 128
N_BUCKETS = 32
MAX_EXACT = 16
MAX_DISTANCE = 128
EPS = 1e-6
NEG_INF = -1e30
Q_SCALE = HEAD_DIM ** -0.5

ADAM_LR = 0.001
ADAM_B1 = 0.9
ADAM_B2 = 0.999
ADAM_EPS = 1e-08
ADAM_WD = 0.01
ADAM_STEP = 10

N_CHIPS = 4
N_DEV = 8
VMEM_LIMIT = 56 * 1024 * 1024
SMALL_ROWS = 16


def _bucket_thresholds():
    def bucket(d):
        big = MAX_EXACT + int(math.log(d / MAX_EXACT) / math.log(MAX_DISTANCE / MAX_EXACT)
                              * (N_BUCKETS - MAX_EXACT))
        return d if d < MAX_EXACT else min(big, N_BUCKETS - 1)
    out = []
    for b in range(MAX_EXACT + 1, N_BUCKETS):
        out.append(min(d for d in range(MAX_EXACT, MAX_DISTANCE) if bucket(d) >= b))
    return tuple(out)


BUCKET_THRESHOLDS = _bucket_thresholds()


def _params(semantics=None, vmem=VMEM_LIMIT):
    return pltpu.CompilerParams(dimension_semantics=semantics, vmem_limit_bytes=vmem)


def _tile(n, pref):
    return pref if n >= 2 * pref else max(n // 2, 8)


def _rms_scale(v):
    return lax.rsqrt(jnp.mean(v * v, axis=-1, keepdims=True) + EPS)


def _nt(a, b):
    return lax.dot_general(a, b, (((1,), (1,)), ((), ())), preferred_element_type=F32)


def _tn(a, b):
    return lax.dot_general(a, b, (((0,), (0,)), ((), ())), preferred_element_type=F32)


def _nn(a, b):
    return jnp.dot(a, b, preferred_element_type=F32)


def _silu_parts(z):
    sg = jax.nn.sigmoid(z)
    return sg, z * sg


def _dsilu(z, sg):
    return sg * (1.0 + z * (1.0 - sg))


def _acc_row(ref, row, val):
    ref[row:row + 1, :] += val


def _gather_weights(shards, small):
    n = len(shards)

    def body(*refs):
        ins, small_in = refs[:n], refs[n]
        outs, small_out = refs[n + 1:2 * n + 1], refs[2 * n + 1]
        ici_send, ici_recv, d2d_send, d2d_recv = refs[2 * n + 2:]
        x, y, c = lax.axis_index("x"), lax.axis_index("y"), lax.axis_index("c")
        k = 2 * x + y
        chips = [(x, 1 - y), (1 - x, y), (1 - x, 1 - y)]
        for i_ref, o_ref in zip(ins, outs):
            o_ref[k] = i_ref[...].astype(BF16)
        small_out[k] = small_in[...]

        def half(o_ref, chip, core):
            h = o_ref.shape[1] // 2
            return o_ref.at[chip, pl.ds(pl.multiple_of(core * h, 16), h)]

        sends = []
        for a, o_ref in enumerate(list(outs) + [small_out]):
            split = a < n
            for j, (px, py) in enumerate(chips):
                src = half(o_ref, k, c) if split else o_ref.at[k]
                cp = pltpu.make_async_remote_copy(
                    src_ref=src, dst_ref=src, send_sem=ici_send.at[3 * a + j],
                    recv_sem=ici_recv.at[3 * a + j], device_id=(px, py, c), device_id_type=MESH)
                cp.start()
                sends.append(cp)
        for a, o_ref in enumerate(list(outs) + [small_out]):
            split = a < n
            for j, (px, py) in enumerate(chips):
                kj = 2 * px + py
                got = half(o_ref, kj, c) if split else o_ref.at[kj]
                pltpu.make_async_remote_copy(
                    src_ref=got, dst_ref=got, send_sem=ici_send.at[3 * a + j],
                    recv_sem=ici_recv.at[3 * a + j], device_id=(px, py, c),
                    device_id_type=MESH).wait_recv()
                if split:
                    fw = pltpu.make_async_remote_copy(
                        src_ref=got, dst_ref=got, send_sem=d2d_send.at[3 * a + j],
                        recv_sem=d2d_recv.at[3 * a + j], device_id=(x, y, 1 - c),
                        device_id_type=MESH)
                    fw.start()
                    sends.append(fw)
        for a, o_ref in enumerate(outs):
            for j, (px, py) in enumerate(chips):
                other = half(o_ref, 2 * px + py, 1 - c)
                pltpu.make_async_remote_copy(
                    src_ref=other, dst_ref=other, send_sem=d2d_send.at[3 * a + j],
                    recv_sem=d2d_recv.at[3 * a + j], device_id=(x, y, 1 - c),
                    device_id_type=MESH).wait_recv()
        for cp in sends:
            cp.wait_send()

    vm = pl.BlockSpec(memory_space=pltpu.VMEM)
    out_shape = [SDS((N_CHIPS,) + s.shape, BF16) for s in shards] + [SDS((N_CHIPS,) + small.shape, F32)]
    return pl.pallas_call(
        body, name="gather_weights", out_shape=out_shape,
        in_specs=[vm] * (n + 1), out_specs=[vm] * (n + 1),
        scratch_shapes=[pltpu.SemaphoreType.DMA((3 * (n + 1),)), pltpu.SemaphoreType.DMA((3 * (n + 1),)),
                        pltpu.SemaphoreType.DMA((3 * n,)), pltpu.SemaphoreType.DMA((3 * n,))],
        compiler_params=pltpu.CompilerParams(vmem_limit_bytes=VMEM_LIMIT),
    )(*shards, small)


def _a_in(x, g_pre, win_g, tm):
    s = x.shape[0]

    def body(x_ref, g_ref, w_ref, proj_ref, n1_ref):
        @pl.when(pl.program_id(1) == 0)
        def _():
            xv = x_ref[...]
            n1_ref[...] = (xv * _rms_scale(xv) * g_ref[...]).astype(BF16)
        proj_ref[...] = _nn(n1_ref[...], w_ref[0])

    return pl.pallas_call(
        body, name="a_in", grid=(s // tm, 4),
        in_specs=[pl.BlockSpec((tm, D), lambda i, j: (i, 0)), pl.BlockSpec((1, D), lambda i, j: (0, 0)),
                  pl.BlockSpec((1, D, D), lambda i, j: (j, 0, 0))],
        out_specs=[pl.BlockSpec((tm, D), lambda i, j: (i, j)), pl.BlockSpec((tm, D), lambda i, j: (i, 0))],
        out_shape=[SDS((s, 4 * D), F32), SDS((s, D), BF16)],
        compiler_params=_params(("parallel", "arbitrary")),
    )(x, g_pre, win_g)


def _shift_rows(v, before, rows):
    v1 = jnp.where(rows >= 1, pltpu.roll(v, 1, 0), before[7:8, :])
    v2 = jnp.where(rows >= 2, pltpu.roll(v, 2, 0), jnp.where(rows == 1, before[7:8, :], before[6:7, :]))
    return v1, v2


def _a_mix(proj, x, conv_w, w_out, g_post, tm):
    s = x.shape[0]

    def body(proj_ref, x_ref, cw_ref, w_ref, g_ref, ya_ref, oa_ref, h1_ref, carry):
        @pl.when(pl.program_id(0) == 0)
        def _():
            carry[...] = jnp.zeros_like(carry)
        v = proj_ref[:, D:2 * D] * proj_ref[:, 2 * D:3 * D]
        rows = lax.broadcasted_iota(jnp.int32, (tm, D), 0)
        v1, v2 = _shift_rows(v, carry[...], rows)
        carry[...] = v[tm - 8:tm, :]
        conv = cw_ref[0:1, :] * v2 + cw_ref[1:2, :] * v1 + cw_ref[2:3, :] * v
        _, sz = _silu_parts(proj_ref[:, 3 * D:4 * D])
        ya = (proj_ref[:, 0:D] * conv * sz).astype(BF16)
        ya_ref[...] = ya
        oa = _nn(ya, w_ref[...])
        oa_ref[...] = oa
        h1_ref[...] = x_ref[...] + oa * _rms_scale(oa) * g_ref[...]

    row = lambda i: (i, 0)
    fix = lambda i: (0, 0)
    return pl.pallas_call(
        body, name="a_mix", grid=(s // tm,),
        in_specs=[pl.BlockSpec((tm, 4 * D), row), pl.BlockSpec((tm, D), row), pl.BlockSpec((8, D), fix),
                  pl.BlockSpec((D, D), fix), pl.BlockSpec((1, D), fix)],
        out_specs=[pl.BlockSpec((tm, D), row)] * 3,
        out_shape=[SDS((s, D), BF16), SDS((s, D), F32), SDS((s, D), F32)],
        scratch_shapes=[pltpu.VMEM((8, D), F32)],
        compiler_params=_params(("arbitrary",)),
    )(proj, x, conv_w, w_out, g_post)


def _b_in(h1, g_kv, g_pre, w_kv, wbin_g, tm):
    s = h1.shape[0]

    def body(h_ref, gk_ref, gb_ref, wkv_ref, wb_ref, nk_ref, nb_ref, kv_ref, q_ref, z_ref):
        h = h_ref[...]
        hh = h * _rms_scale(h)
        nk = (hh * gk_ref[...]).astype(BF16)
        nb = (hh * gb_ref[...]).astype(BF16)
        nk_ref[...] = nk
        nb_ref[...] = nb
        kv_ref[...] = _nn(nk, wkv_ref[...]).astype(BF16)
        for j in range(2):
            q_ref[:, 512 * j:512 * (j + 1)] = (_nn(nb, wb_ref[j]) * Q_SCALE).astype(BF16)
            z_ref[:, 512 * j:512 * (j + 1)] = _nn(nb, wb_ref[2 + j])

    row = lambda i: (i, 0)
    fix = lambda i: (0, 0)
    return pl.pallas_call(
        body, name="b_in", grid=(s // tm,),
        in_specs=[pl.BlockSpec((tm, D), row), pl.BlockSpec((1, D), fix), pl.BlockSpec((1, D), fix),
                  pl.BlockSpec((D, 2 * KV_W), fix), pl.BlockSpec((4, D, 512), lambda i: (0, 0, 0))],
        out_specs=[pl.BlockSpec((tm, D), row), pl.BlockSpec((tm, D), row), pl.BlockSpec((tm, 2 * KV_W), row),
                   pl.BlockSpec((tm, D), row), pl.BlockSpec((tm, D), row)],
        out_shape=[SDS((s, D), BF16), SDS((s, D), BF16), SDS((s, 2 * KV_W), BF16), SDS((s, D), BF16),
                   SDS((s, D), F32)],
        compiler_params=_params(("parallel",)),
    )(h1, g_kv, g_pre, w_kv, wbin_g)


def _band_buckets():
    q = lax.broadcasted_iota(jnp.int32, (BLK, 2 * BLK), 0)
    k = lax.broadcasted_iota(jnp.int32, (BLK, 2 * BLK), 1)
    dist = q + BLK - k
    bucket = jnp.where(dist < MAX_EXACT, dist, MAX_EXACT)
    for t in BUCKET_THRESHOLDS:
        bucket = bucket + jnp.where(dist >= t, 1, 0)
    in_window = (dist >= 0) & (dist < BLK)
    return jnp.where(in_window, bucket, -1)


def _bias_table(rel_bias):
    def body(rb_ref, tab_ref):
        bucket = _band_buckets()
        for h in range(N_HEADS):
            acc = jnp.where(bucket < 0, NEG_INF, 0.0).astype(F32)
            for b in range(N_BUCKETS):
                acc = jnp.where(bucket == b, rb_ref[b, h], acc)
            tab_ref[h] = acc

    return pl.pallas_call(
        body, name="bias_table", out_shape=SDS((N_HEADS, BLK, 2 * BLK), F32),
        in_specs=[pl.BlockSpec(memory_space=pltpu.SMEM)],
        out_specs=pl.BlockSpec(memory_space=pltpu.VMEM),
    )(rel_bias)


def _bias_fold(dtab):
    def body(dtab_ref, out_ref):
        bucket = _band_buckets()
        row = lax.broadcasted_iota(jnp.int32, (N_BUCKETS, 128), 0)
        lane = lax.broadcasted_iota(jnp.int32, (N_BUCKETS, 128), 1)
        acc = jnp.zeros((N_BUCKETS, 128), F32)
        for h in range(N_HEADS):
            dt = dtab_ref[h]
            for b in range(N_BUCKETS):
                val = jnp.sum(jnp.where(bucket == b, dt, 0.0))
                acc = acc + jnp.where((row == b) & (lane == h), val, 0.0)
        out_ref[...] = acc

    return pl.pallas_call(
        body, name="bias_fold", out_shape=SDS((N_BUCKETS, 128), F32),
        in_specs=[pl.BlockSpec(memory_space=pltpu.VMEM)],
        out_specs=pl.BlockSpec(memory_space=pltpu.VMEM),
    )(dtab)


def _pair_operands(prev, cur):
    t = jnp.concatenate([prev, cur], axis=0).astype(F32)
    tr = pltpu.roll(t, HEAD_DIM, 1)
    lo = lax.broadcasted_iota(jnp.int32, t.shape, 1) < HEAD_DIM
    zero = jnp.zeros_like(t)
    head0 = jnp.concatenate([jnp.where(lo, t, zero), jnp.where(lo, zero, tr)], axis=0).astype(BF16)
    head1 = jnp.concatenate([jnp.where(lo, tr, zero), jnp.where(lo, zero, t)], axis=0).astype(BF16)
    return head0, head1


def _pair_fold(d0, d1):
    lo = lax.broadcasted_iota(jnp.int32, (2 * BLK, KV_W), 1) < HEAD_DIM
    zero = jnp.zeros((2 * BLK, KV_W), F32)
    g0 = jnp.where(lo, d0[0:256], zero) + pltpu.roll(jnp.where(lo, zero, d0[256:512]), HEAD_DIM, 1)
    g1 = pltpu.roll(jnp.where(lo, d1[0:256], zero), HEAD_DIM, 1) + jnp.where(lo, zero, d1[256:512])
    return g0 + g1


def _first_block_mask(n):
    col = lax.broadcasted_iota(jnp.int32, (BLK, 2 * BLK), 1)
    return jnp.where((n == 0) & (col < BLK), NEG_INF, 0.0).astype(F32)


def _attn_fwd(q, kv, tab, sinks):
    s = q.shape[0]

    def body(sink_ref, q_ref, kp_ref, kc_ref, vp_ref, vc_ref, tab_ref, att_ref, stats_ref):
        n = pl.program_id(0)
        k2 = _pair_operands(kp_ref[...], kc_ref[...])
        v2 = _pair_operands(vp_ref[...], vc_ref[...])
        first = _first_block_mask(n)
        lane = lax.broadcasted_iota(jnp.int32, (BLK, 128), 1)
        stats = jnp.zeros((BLK, 128), F32)
        for j in range(N_HEADS // 2):
            kh = (2 * j) // GROUP
            sc = _nt(q_ref[:, 128 * j:128 * (j + 1)], k2[kh])
            ps = []
            for e in range(2):
                h = 2 * j + e
                sink = sink_ref[h]
                lg = sc[:, 256 * e:256 * (e + 1)] + tab_ref[h] + first
                m = jnp.maximum(jnp.max(lg, axis=-1, keepdims=True), sink)
                ex = jnp.exp(lg - m)
                den = jnp.sum(ex, axis=-1, keepdims=True) + jnp.exp(sink - m)
                ps.append(ex * (1.0 / den))
                stats = jnp.where(lane == h, m + jnp.log(den), stats)
            p2 = jnp.concatenate(ps, axis=1).astype(BF16)
            att_ref[:, 128 * j:128 * (j + 1)] = _nn(p2, v2[kh])
        stats_ref[...] = stats

    cur = lambda n: (n, 0)
    prev = lambda n: (jnp.maximum(n - 1, 0), 0)
    return pl.pallas_call(
        body, name="attn_fwd", grid=(s // BLK,),
        in_specs=[pl.BlockSpec(memory_space=pltpu.SMEM), pl.BlockSpec((BLK, D), cur),
                  pl.BlockSpec((BLK, KV_W), prev), pl.BlockSpec((BLK, KV_W), cur),
                  pl.BlockSpec((BLK, KV_W), lambda n: (jnp.maximum(n - 1, 0), 1)),
                  pl.BlockSpec((BLK, KV_W), lambda n: (n, 1)),
                  pl.BlockSpec((N_HEADS, BLK, 2 * BLK), lambda n: (0, 0, 0))],
        out_specs=[pl.BlockSpec((BLK, D), cur), pl.BlockSpec((BLK, 128), cur)],
        out_shape=[SDS((s, D), F32), SDS((s, 128), F32)],
        compiler_params=_params(("parallel",)),
    )(sinks, q, kv, kv, kv, kv, tab)


def _mid(att, zb, h1, tgt, w_out, g_post, tm):
    s = att.shape[0]

    def body(att_ref, z_ref, h1_ref, t_ref, w_ref, g_ref,
             ob_ref, dy_ref, dh_ref, dqz_ref, datt_ref, loss_ref, dg_ref):
        @pl.when(pl.program_id(0) == 0)
        def _():
            loss_ref[...] = jnp.zeros_like(loss_ref)
            dg_ref[...] = jnp.zeros_like(dg_ref)
        att = att_ref[...]
        z = z_ref[...]
        sg, sz = _silu_parts(z)
        ob = (att * sz).astype(BF16)
        ob_ref[...] = ob
        y2 = _nn(ob, w_ref[...])
        r2 = _rms_scale(y2)
        yh = y2 * r2
        g = g_ref[...]
        err = (h1_ref[...] + yh * g) - t_ref[...]
        loss_ref[...] += jnp.sum(jnp.sum(err * err, axis=-1, keepdims=True) / D)
        dh = err / D
        dh_ref[...] = dh
        _acc_row(dg_ref, 0, jnp.sum(dh * yh, axis=0, keepdims=True))
        dyh = dh * g
        dy = (r2 * (dyh - yh * jnp.mean(dyh * yh, axis=-1, keepdims=True))).astype(BF16)
        dy_ref[...] = dy
        dob = _nt(dy, w_ref[...])
        datt_ref[...] = (dob * sz).astype(BF16)
        dqz_ref[...] = (dob * att * _dsilu(z, sg)).astype(BF16)

    row = lambda i: (i, 0)
    fix = lambda i: (0, 0)
    return pl.pallas_call(
        body, name="mid", grid=(s // tm,),
        in_specs=[pl.BlockSpec((tm, D), row)] * 4 + [pl.BlockSpec((D, D), fix), pl.BlockSpec((1, D), fix)],
        out_specs=[pl.BlockSpec((tm, D), row), pl.BlockSpec((tm, D), row), pl.BlockSpec((tm, D), row),
                   pl.BlockSpec((tm, D), lambda i: (i, 1)), pl.BlockSpec((tm, D), row),
                   pl.BlockSpec((8, 128), fix), pl.BlockSpec((8, D), fix)],
        out_shape=[SDS((s, D), BF16), SDS((s, D), BF16), SDS((s, D), F32), SDS((s, 2 * D), BF16),
                   SDS((s, D), BF16), SDS((8, 128), F32), SDS((8, D), F32)],
        compiler_params=_params(("arbitrary",)),
    )(att, zb, h1, tgt, w_out, g_post)


def _attn_bwd(q, kv, datt, stats, tab, sinks, dqz):
    s = q.shape[0]
    nb = s // BLK

    def body(sink_ref, q_ref, kp_ref, kc_ref, vp_ref, vc_ref, da_ref, st_ref, tab_ref, dqz_in,
             dq_ref, dkv_ref, dtab_ref, dsink_ref, dk_carry, dv_carry):
        del dqz_in
        n = pl.program_id(0)

        @pl.when(n == 0)
        def _():
            dtab_ref[...] = jnp.zeros_like(dtab_ref)
            dsink_ref[...] = jnp.zeros_like(dsink_ref)
            dk_carry[...] = jnp.zeros_like(dk_carry)
            dv_carry[...] = jnp.zeros_like(dv_carry)

        @pl.when(n < nb)
        def _():
            k2 = _pair_operands(kp_ref[...], kc_ref[...])
            v2 = _pair_operands(vp_ref[...], vc_ref[...])
            first = _first_block_mask(n)
            lane = lax.broadcasted_iota(jnp.int32, (BLK, 128), 1)
            lane8 = lax.broadcasted_iota(jnp.int32, (8, 128), 1)
            stats = st_ref[...]
            dk2 = [jnp.zeros((4 * BLK, KV_W), F32), jnp.zeros((4 * BLK, KV_W), F32)]
            dv2 = [jnp.zeros((4 * BLK, KV_W), F32), jnp.zeros((4 * BLK, KV_W), F32)]
            dsink = jnp.zeros((8, 128), F32)
            for j in range(N_HEADS // 2):
                kh = (2 * j) // GROUP
                q2 = q_ref[:, 128 * j:128 * (j + 1)]
                da2 = da_ref[:, 128 * j:128 * (j + 1)]
                sc = _nt(q2, k2[kh])
                dp = _nt(da2, v2[kh])
                ps, dss = [], []
                for e in range(2):
                    h = 2 * j + e
                    lse = jnp.sum(jnp.where(lane == h, stats, 0.0), axis=-1, keepdims=True)
                    p = jnp.exp(sc[:, 256 * e:256 * (e + 1)] + tab_ref[h] + first - lse)
                    dpe = dp[:, 256 * e:256 * (e + 1)]
                    delta = jnp.sum(p * dpe, axis=-1, keepdims=True)
                    ds = p * (dpe - delta)
                    dtab_ref[h] += ds
                    dsink = dsink - jnp.where(lane8 == h, jnp.sum(jnp.exp(sink_ref[h] - lse) * delta), 0.0)
                    ps.append(p)
                    dss.append(ds)
                p2 = jnp.concatenate(ps, axis=1).astype(BF16)
                ds2 = jnp.concatenate(dss, axis=1).astype(BF16)
                dq_ref[:, 128 * j:128 * (j + 1)] = (_nn(ds2, k2[kh]) * Q_SCALE).astype(BF16)
                dk2[kh] = dk2[kh] + _tn(ds2, q2)
                dv2[kh] = dv2[kh] + _tn(p2, da2)
            dsink_ref[...] += dsink
            dkk = _pair_fold(dk2[0], dk2[1])
            dvv = _pair_fold(dv2[0], dv2[1])
            dkv_ref[:, 0:KV_W] = (dk_carry[...] + dkk[0:BLK]).astype(BF16)
            dkv_ref[:, KV_W:2 * KV_W] = (dv_carry[...] + dvv[0:BLK]).astype(BF16)
            dk_carry[...] = dkk[BLK:2 * BLK]
            dv_carry[...] = dvv[BLK:2 * BLK]

        @pl.when(n == nb)
        def _():
            dkv_ref[:, 0:KV_W] = dk_carry[...].astype(BF16)
            dkv_ref[:, KV_W:2 * KV_W] = dv_carry[...].astype(BF16)

    cur = lambda n: (jnp.minimum(n, nb - 1), 0)
    prev = lambda n: (jnp.clip(n - 1, 0, nb - 1), 0)
    return pl.pallas_call(
        body, name="attn_bwd", grid=(nb + 1,),
        in_specs=[pl.BlockSpec(memory_space=pltpu.SMEM), pl.BlockSpec((BLK, D), cur),
                  pl.BlockSpec((BLK, KV_W), prev), pl.BlockSpec((BLK, KV_W), cur),
                  pl.BlockSpec((BLK, KV_W), lambda n: (jnp.clip(n - 1, 0, nb - 1), 1)),
                  pl.BlockSpec((BLK, KV_W), lambda n: (jnp.minimum(n, nb - 1), 1)),
                  pl.BlockSpec((BLK, D), cur), pl.BlockSpec((BLK, 128), cur),
                  pl.BlockSpec((N_HEADS, BLK, 2 * BLK), lambda n: (0, 0, 0)),
                  pl.BlockSpec(memory_space=pl.ANY)],
        out_specs=[pl.BlockSpec((BLK, D), cur), pl.BlockSpec((BLK, 2 * KV_W), prev),
                   pl.BlockSpec((N_HEADS, BLK, 2 * BLK), lambda n: (0, 0, 0)),
                   pl.BlockSpec((8, 128), lambda n: (0, 0))],
        out_shape=[SDS((s, 2 * D), BF16), SDS((s, 2 * KV_W), BF16), SDS((N_HEADS, BLK, 2 * BLK), F32),
                   SDS((8, 128), F32)],
        scratch_shapes=[pltpu.VMEM((BLK, KV_W), F32), pltpu.VMEM((BLK, KV_W), F32)],
        input_output_aliases={9: 0},
        compiler_params=_params(("arbitrary",)),
    )(sinks, q, kv, kv, kv, kv, datt, stats, tab, dqz)


def _b_bwd(dqz, dkv, h1, dh2, oa, wbin_g, w_kv, g_kv, g_pre, g_apost, tm):
    s = h1.shape[0]

    def body(dqz_ref, dkv_ref, h_ref, dh2_ref, oa_ref, wb_ref, wkv_ref, gk_ref, gb_ref, ga_ref,
             dh1_ref, doa_ref, dg_ref):
        @pl.when(pl.program_id(0) == 0)
        def _():
            dg_ref[...] = jnp.zeros_like(dg_ref)
        dnb = _nt(dqz_ref[:, 0:512], wb_ref[0])
        for j in range(1, 4):
            dnb = dnb + _nt(dqz_ref[:, 512 * j:512 * (j + 1)], wb_ref[j])
        dnk = _nt(dkv_ref[...], wkv_ref[...])
        h = h_ref[...]
        r = _rms_scale(h)
        hh = h * r
        _acc_row(dg_ref, 0, jnp.sum(dnk * hh, axis=0, keepdims=True))
        _acc_row(dg_ref, 1, jnp.sum(dnb * hh, axis=0, keepdims=True))
        dhh = dnb * gb_ref[...] + dnk * gk_ref[...]
        dh1 = dh2_ref[...] + r * (dhh - hh * jnp.mean(dhh * hh, axis=-1, keepdims=True))
        dh1_ref[...] = dh1
        oa = oa_ref[...]
        ra = _rms_scale(oa)
        oh = oa * ra
        _acc_row(dg_ref, 2, jnp.sum(dh1 * oh, axis=0, keepdims=True))
        doh = dh1 * ga_ref[...]
        doa_ref[...] = (ra * (doh - oh * jnp.mean(doh * oh, axis=-1, keepdims=True))).astype(BF16)

    row = lambda i: (i, 0)
    fix = lambda i: (0, 0)
    return pl.pallas_call(
        body, name="b_bwd", grid=(s // tm,),
        in_specs=[pl.BlockSpec((tm, 2 * D), row), pl.BlockSpec((tm, 2 * KV_W), row), pl.BlockSpec((tm, D), row),
                  pl.BlockSpec((tm, D), row), pl.BlockSpec((tm, D), row),
                  pl.BlockSpec((4, D, 512), lambda i: (0, 0, 0)), pl.BlockSpec((D, 2 * KV_W), fix),
                  pl.BlockSpec((1, D), fix), pl.BlockSpec((1, D), fix), pl.BlockSpec((1, D), fix)],
        out_specs=[pl.BlockSpec((tm, D), row), pl.BlockSpec((tm, D), row), pl.BlockSpec((8, D), fix)],
        out_shape=[SDS((s, D), F32), SDS((s, D), BF16), SDS((8, D), F32)],
        compiler_params=_params(("arbitrary",)),
    )(dqz, dkv, h1, dh2, oa, wbin_g, w_kv, g_kv, g_pre, g_apost)


def _chip_exchange(parts, recvs, send, recv):
    x, y, c = lax.axis_index("x"), lax.axis_index("y"), lax.axis_index("c")
    chips = [(x, 1 - y), (1 - x, y), (1 - x, 1 - y)]
    copies = []
    for a, (t, r) in enumerate(zip(parts, recvs)):
        for j, (px, py) in enumerate(chips):
            copies.append(pltpu.make_async_remote_copy(
                src_ref=t.at[2 * px + py], dst_ref=r.at[j], send_sem=send.at[3 * a + j],
                recv_sem=recv.at[3 * a + j], device_id=(px, py, c), device_id_type=MESH))
    return copies


def _exchange_specs(parts):
    anyspace = pl.BlockSpec(memory_space=pl.ANY)
    n = len(parts)
    return ([anyspace] * n, [anyspace] * n, [SDS((3,) + t.shape[1:], t.dtype) for t in parts],
            [pltpu.SemaphoreType.DMA((3 * n,)), pltpu.SemaphoreType.DMA((3 * n,))])


def _a_bwd(doa, proj, conv_w, w_out, tm, parts):
    s = doa.shape[0]
    nt = s // tm
    n = len(parts)
    ex_in, ex_out, ex_shape, ex_sems = _exchange_specs(parts)

    def body(*refs):
        doa_ref, proj_ref, halo_ref, cw_ref, w_ref = refs[:5]
        part_refs = refs[5:5 + n]
        dproj_ref, dcw_ref = refs[5 + n:7 + n]
        recv_refs = refs[7 + n:7 + 2 * n]
        carry, send, recv = refs[7 + 2 * n:]
        i = pl.program_id(0)
        r = nt - 1 - i

        @pl.when(i == 0)
        def _():
            dcw_ref[...] = jnp.zeros_like(dcw_ref)
            carry[...] = jnp.zeros_like(carry)
            for cp in _chip_exchange(part_refs, recv_refs, send, recv):
                cp.start()
        dya = _nt(doa_ref[...], w_ref[...])
        bg = proj_ref[:, 0:D]
        cg = proj_ref[:, D:2 * D]
        u = proj_ref[:, 2 * D:3 * D]
        z = proj_ref[:, 3 * D:4 * D]
        v = cg * u
        before = jnp.where(r > 0, halo_ref[:, D:2 * D] * halo_ref[:, 2 * D:3 * D], 0.0)
        rows = lax.broadcasted_iota(jnp.int32, (tm, D), 0)
        v1, v2 = _shift_rows(v, before, rows)
        conv = cw_ref[0:1, :] * v2 + cw_ref[1:2, :] * v1 + cw_ref[2:3, :] * v
        sg, sz = _silu_parts(z)
        dproj_ref[:, 0:D] = (dya * conv * sz).astype(BF16)
        dproj_ref[:, 3 * D:4 * D] = (dya * bg * conv * _dsilu(z, sg)).astype(BF16)
        dconv = dya * bg * sz
        _acc_row(dcw_ref, 0, jnp.sum(dconv * v2, axis=0, keepdims=True))
        _acc_row(dcw_ref, 1, jnp.sum(dconv * v1, axis=0, keepdims=True))
        _acc_row(dcw_ref, 2, jnp.sum(dconv * v, axis=0, keepdims=True))
        after = carry[...]
        up1 = jnp.where(rows < tm - 1, pltpu.roll(dconv, tm - 1, 0), after[0:1, :])
        up2 = jnp.where(rows < tm - 2, pltpu.roll(dconv, tm - 2, 0),
                        jnp.where(rows == tm - 2, after[0:1, :], after[1:2, :]))
        carry[...] = dconv[0:8, :]
        dv = cw_ref[2:3, :] * dconv + cw_ref[1:2, :] * up1 + cw_ref[0:1, :] * up2
        dproj_ref[:, D:2 * D] = (dv * u).astype(BF16)
        dproj_ref[:, 2 * D:3 * D] = (dv * cg).astype(BF16)

        @pl.when(i == nt - 1)
        def _():
            for cp in _chip_exchange(part_refs, recv_refs, send, recv):
                cp.wait()

    rev = lambda i: (nt - 1 - i, 0)
    fix = lambda i: (0, 0)
    halo = lambda i: (jnp.maximum((nt - 1 - i) * (tm // 8) - 1, 0), 0)
    dproj, dcw, *got = pl.pallas_call(
        body, name="a_bwd", grid=(nt,),
        in_specs=[pl.BlockSpec((tm, D), rev), pl.BlockSpec((tm, 4 * D), rev), pl.BlockSpec((8, 4 * D), halo),
                  pl.BlockSpec((8, D), fix), pl.BlockSpec((D, D), fix)] + ex_in,
        out_specs=[pl.BlockSpec((tm, 4 * D), rev), pl.BlockSpec((8, D), fix)] + ex_out,
        out_shape=[SDS((s, 4 * D), BF16), SDS((8, D), F32)] + ex_shape,
        scratch_shapes=[pltpu.VMEM((8, D), F32)] + ex_sems,
        compiler_params=_params(("arbitrary",)),
    )(doa, proj, proj, conv_w, w_out, *parts)
    return dproj, dcw, got


def _a_in_bwd(dproj, x, dh1, win_g, g_pre, tm, parts):
    s = x.shape[0]
    nt = s // tm
    n = len(parts)
    ex_in, ex_out, ex_shape, ex_sems = _exchange_specs(parts)

    def body(*refs):
        dp_ref, x_ref, dh_ref, w_ref, g_ref = refs[:5]
        part_refs = refs[5:5 + n]
        gx_ref, dg_ref = refs[5 + n:7 + n]
        recv_refs = refs[7 + n:7 + 2 * n]
        send, recv = refs[7 + 2 * n:]

        @pl.when(pl.program_id(0) == 0)
        def _():
            dg_ref[...] = jnp.zeros_like(dg_ref)
            for cp in _chip_exchange(part_refs, recv_refs, send, recv):
                cp.start()
        dn = _nt(dp_ref[:, 0:D], w_ref[0])
        for j in range(1, 4):
            dn = dn + _nt(dp_ref[:, D * j:D * (j + 1)], w_ref[j])
        xv = x_ref[...]
        r = _rms_scale(xv)
        xh = xv * r
        _acc_row(dg_ref, 0, jnp.sum(dn * xh, axis=0, keepdims=True))
        dxh = dn * g_ref[...]
        gx_ref[...] = dh_ref[...] + r * (dxh - xh * jnp.mean(dxh * xh, axis=-1, keepdims=True))

        @pl.when(pl.program_id(0) == nt - 1)
        def _():
            for cp in _chip_exchange(part_refs, recv_refs, send, recv):
                cp.wait()

    row = lambda i: (i, 0)
    fix = lambda i: (0, 0)
    gx, dg, *got = pl.pallas_call(
        body, name="a_in_bwd", grid=(nt,),
        in_specs=[pl.BlockSpec((tm, 4 * D), row), pl.BlockSpec((tm, D), row), pl.BlockSpec((tm, D), row),
                  pl.BlockSpec((4, D, D), lambda i: (0, 0, 0)), pl.BlockSpec((1, D), fix)] + ex_in,
        out_specs=[pl.BlockSpec((tm, D), row), pl.BlockSpec((8, D), fix)] + ex_out,
        out_shape=[SDS((s, D), F32), SDS((8, D), F32)] + ex_shape,
        scratch_shapes=ex_sems,
        compiler_params=_params(("arbitrary",)),
    )(dproj, x, dh1, win_g, g_pre, *parts)
    return gx, dg, got


def _dw(a, b, tn, tmw, name):
    s, k = a.shape
    n = b.shape[1]

    def body(a_ref, b_ref, o_ref):
        @pl.when(pl.program_id(1) == 0)
        def _():
            o_ref[...] = jnp.zeros_like(o_ref)
        o_ref[0] += _tn(a_ref[...], b_ref[...])

    return pl.pallas_call(
        body, name=name, grid=(n // tn, s // tmw),
        in_specs=[pl.BlockSpec((tmw, k), lambda j, t: (t, 0)), pl.BlockSpec((tmw, tn), lambda j, t: (t, j))],
        out_specs=pl.BlockSpec((1, k, tn), lambda j, t: (j, 0, 0)),
        out_shape=SDS((n // tn, k, tn), F32),
        compiler_params=_params(("parallel", "arbitrary")),
    )(a, b)


def _sibling_exchange(name, to_sibling=(), halves=(), smalls=None):
    n_g, n_h = len(to_sibling), len(halves)
    has_small = smalls is not None

    def body(*refs):
        gs, hs = refs[:n_g], refs[n_g:n_g + n_h]
        pos = n_g + n_h
        small_in = refs[pos] if has_small else None
        pos += has_small
        rs, fs = refs[pos:pos + n_g], refs[pos + n_g:pos + n_g + n_h]
        pos += n_g + n_h
        small_all = refs[pos] if has_small else None
        pos += has_small
        dsend, drecv, local, ssend, srecv = refs[pos:]
        x, y, c = lax.axis_index("x"), lax.axis_index("y"), lax.axis_index("c")
        sibling = (x, y, 1 - c)
        sends, arrivals, placed = [], [], []
        for a, (g, r) in enumerate(zip(gs, rs)):
            h = g.shape[1] // 2
            src = g.at[:, pl.ds(pl.multiple_of((1 - c) * h, 8), h), :]
            sends.append(pltpu.make_async_remote_copy(src_ref=src, dst_ref=r, send_sem=dsend.at[a], recv_sem=drecv.at[a],
                                                      device_id=sibling, device_id_type=MESH))
            arrivals.append(pltpu.make_async_remote_copy(src_ref=r, dst_ref=r, send_sem=dsend.at[a], recv_sem=drecv.at[a],
                                                         device_id=sibling, device_id_type=MESH))
        for b, (hf, full) in enumerate(zip(hs, fs)):
            h = hf.shape[0]
            mine = full.at[pl.ds(pl.multiple_of(c * h, 8), h)]
            theirs = full.at[pl.ds(pl.multiple_of((1 - c) * h, 8), h)]
            placed.append(pltpu.make_async_copy(hf, mine, local.at[b]))
            sends.append(pltpu.make_async_remote_copy(src_ref=hf, dst_ref=mine, send_sem=dsend.at[n_g + b],
                                                      recv_sem=drecv.at[n_g + b], device_id=sibling, device_id_type=MESH))
            arrivals.append(pltpu.make_async_remote_copy(src_ref=hf, dst_ref=theirs, send_sem=dsend.at[n_g + b],
                                                         recv_sem=drecv.at[n_g + b], device_id=sibling, device_id_type=MESH))
        if has_small:
            me = 4 * x + 2 * y + c
            small_all[me] = small_in[...]
            for rel in range(1, N_DEV):
                fx, fy, fc = rel >> 2, (rel >> 1) & 1, rel & 1
                peer = (x + fx - 2 * x * fx, y + fy - 2 * y * fy, c + fc - 2 * c * fc)
                sender = 4 * peer[0] + 2 * peer[1] + peer[2]
                sends.append(pltpu.make_async_remote_copy(
                    src_ref=small_in, dst_ref=small_all.at[me], send_sem=ssend.at[rel - 1], recv_sem=srecv.at[rel - 1],
                    device_id=peer, device_id_type=MESH))
                arrivals.append(pltpu.make_async_remote_copy(
                    src_ref=small_in, dst_ref=small_all.at[sender], send_sem=ssend.at[rel - 1], recv_sem=srecv.at[rel - 1],
                    device_id=peer, device_id_type=MESH))
        for cp in placed + sends:
            cp.start()
        for cp in arrivals:
            cp.wait_recv()
        for cp in placed:
            cp.wait()
        for cp in sends:
            cp.wait_send()

    anyspace = pl.BlockSpec(memory_space=pl.ANY)
    vm = pl.BlockSpec(memory_space=pltpu.VMEM)
    out_shape = [SDS((N_CHIPS, g.shape[1] // 2, g.shape[2]), F32) for g in to_sibling]
    out_shape += [SDS((2 * hf.shape[0], hf.shape[1]), F32) for hf in halves]
    if has_small:
        out_shape.append(SDS((N_DEV,) + smalls.shape, F32))
    n_d2d = max(n_g + n_h, 1)
    outs = pl.pallas_call(
        body, name=name, out_shape=out_shape,
        in_specs=[anyspace] * (n_g + n_h) + [vm] * has_small, out_specs=[anyspace] * (n_g + n_h) + [vm] * has_small,
        scratch_shapes=[pltpu.SemaphoreType.DMA((n_d2d,)), pltpu.SemaphoreType.DMA((n_d2d,)),
                        pltpu.SemaphoreType.DMA((max(n_h, 1),)),
                        pltpu.SemaphoreType.DMA((N_DEV - 1,)), pltpu.SemaphoreType.DMA((N_DEV - 1,))],
    )(*to_sibling, *halves, *([smalls] if has_small else []))
    return outs[:n_g], outs[n_g:n_g + n_h], (outs[n_g + n_h] if has_small else None)


def _add_sibling(where, g, r, name):
    _, rows, cols = g.shape
    h = rows // 2
    tr = min(h, 256)
    nh = h // tr

    def body(where_ref, g_ref, r_ref, t_ref, own_ref):
        t = g_ref[0] + r_ref[0]
        t_ref[0] = t.astype(BF16)

        @pl.when(pl.program_id(1) == where_ref[1])
        def _():
            own_ref[...] = t

    return pl.pallas_call(
        body, name=name,
        grid_spec=pltpu.PrefetchScalarGridSpec(
            num_scalar_prefetch=1, grid=(nh, N_CHIPS),
            in_specs=[pl.BlockSpec((1, tr, cols), lambda i, k, w: (k, w[0] * nh + i, 0)),
                      pl.BlockSpec((1, tr, cols), lambda i, k, w: (k, i, 0))],
            out_specs=[pl.BlockSpec((1, tr, cols), lambda i, k, w: (k, i, 0)),
                       pl.BlockSpec((tr, cols), lambda i, k, w: (i, 0))]),
        out_shape=[SDS((N_CHIPS, h, cols), BF16), SDS((h, cols), F32)],
        compiler_params=_params(("parallel", "arbitrary")),
    )(where, g, r)


def _add_chips(own, r, name):
    h, cols = own.shape
    tr = min(h, 256)

    def body(t_ref, r_ref, o_ref):
        o_ref[...] = ((t_ref[...] + r_ref[0].astype(F32)) + r_ref[1].astype(F32)) + r_ref[2].astype(F32)

    return pl.pallas_call(
        body, name=name, grid=(h // tr,),
        in_specs=[pl.BlockSpec((tr, cols), lambda i: (i, 0)), pl.BlockSpec((3, tr, cols), lambda i: (0, i, 0))],
        out_specs=pl.BlockSpec((tr, cols), lambda i: (i, 0)),
        out_shape=SDS((h, cols), F32),
        compiler_params=_params(("parallel",)),
    )(own, r)


def _sum_smalls(small_all):
    def body(all_ref, o_ref):
        acc = all_ref[0]
        for dev in range(1, N_DEV):
            acc = acc + all_ref[dev]
        o_ref[...] = acc

    return pl.pallas_call(
        body, name="sum_smalls", out_shape=SDS(small_all.shape[1:], F32),
        in_specs=[pl.BlockSpec(memory_space=pltpu.VMEM)], out_specs=pl.BlockSpec(memory_space=pltpu.VMEM),
    )(small_all)


def _adamw(g, w, m, v, name):
    rows, cols = g.shape
    tr = min(rows, 256)

    def body(g_ref, w_ref, m_ref, v_ref, d_ref, nm_ref, nv_ref):
        gv = g_ref[...]
        nm = ADAM_B1 * m_ref[...] + (1.0 - ADAM_B1) * gv
        nv = ADAM_B2 * v_ref[...] + (1.0 - ADAM_B2) * (gv * gv)
        nm_ref[...] = nm
        nv_ref[...] = nv
        m_hat = nm / (1.0 - ADAM_B1 ** ADAM_STEP)
        v_hat = nv / (1.0 - ADAM_B2 ** ADAM_STEP)
        d_ref[...] = -ADAM_LR * (m_hat / (jnp.sqrt(v_hat) + ADAM_EPS) + ADAM_WD * w_ref[...])

    spec = pl.BlockSpec((tr, cols), lambda i: (i, 0))
    return pl.pallas_call(
        body, name=name, grid=(rows // tr,), in_specs=[spec] * 4, out_specs=[spec] * 3,
        out_shape=[SDS(g.shape, F32)] * 3, compiler_params=_params(("parallel",)),
    )(g, w, m, v)


def _pad_rows(a, rows):
    return jnp.concatenate([a, jnp.zeros((rows - a.shape[0], a.shape[1]), a.dtype)], axis=0)


def _pad_cols(a, cols):
    return jnp.concatenate([a, jnp.zeros((a.shape[0], cols - a.shape[1]), a.dtype)], axis=1)


def kernel(x, a_pre_norm, a_w_in, a_conv_w, a_w_out, a_post_norm, kv_norm, w_kv, rel_bias, b_pre_norm, b_w_in, b_sinks, b_w_out, b_post_norm, loss_target, m_a_pre_norm, m_a_w_in, m_a_conv_w, m_a_w_out, m_a_post_norm, m_kv_norm, m_w_kv, m_rel_bias, m_b_pre_norm, m_b_w_in, m_b_sinks, m_b_w_out, m_b_post_norm, v_a_pre_norm, v_a_w_in, v_a_conv_w, v_a_w_out, v_a_post_norm, v_kv_norm, v_w_kv, v_rel_bias, v_b_pre_norm, v_b_w_in, v_b_sinks, v_b_w_out, v_b_post_norm):
    seq = x.shape[1]
    xs = x.reshape(seq, D)
    tgt = loss_target.reshape(seq, D)
    chip = 2 * lax.axis_index("x") + lax.axis_index("y")
    core = lax.axis_index("c")
    tm = _tile(seq, 512)
    tm_mix = _tile(seq, 256)
    tmw = _tile(seq, 1024)

    shards = [a_w_in[0], a_w_out[0], w_kv, b_w_in[0], b_w_out[0]]
    small_w = _pad_rows(jnp.concatenate([a_pre_norm, a_conv_w[0], a_post_norm], axis=0), 8)
    win_g, wouta_g, wkv_g, wbin_g, woutb_g, small_g = _gather_weights(shards, small_w)
    small_full = small_g.transpose(1, 0, 2).reshape(8, D)
    g_apre, conv_w, g_apost = small_full[0:1], _pad_rows(small_full[1:4], 8), small_full[4:5]
    wouta = wouta_g.reshape(D, D)
    wkv = wkv_g.reshape(D, 2 * KV_W)
    woutb = woutb_g.reshape(D, D)
    g_kv = kv_norm.reshape(1, D)

    proj, n1 = _a_in(xs, g_apre, win_g, tm)
    ya, oa, h1 = _a_mix(proj, xs, conv_w, wouta, g_apost, tm_mix)
    nk, nb, kv, q, zb = _b_in(h1, g_kv, b_pre_norm, wkv, wbin_g, tm)
    tab = _bias_table(rel_bias)
    sinks = b_sinks.reshape(N_HEADS)
    att, stats = _attn_fwd(q, kv, tab, sinks)
    ob, dy2, dh2, dqz, datt, loss_acc, dg_bpost = _mid(att, zb, h1, tgt, woutb, b_post_norm, tm)

    dqz, dkv, dtab, dsink = _attn_bwd(q, kv, datt, stats, tab, sinks, dqz)
    dh1, doa, dg_b = _b_bwd(dqz, dkv, h1, dh2, oa, wbin_g, wkv, g_kv, b_pre_norm, g_apost, tm)
    where = jnp.stack([core, chip]).astype(jnp.int32)
    dw_outa = _dw(ya, doa, D, tmw, "dw_a_out").reshape(N_CHIPS, D // 4, D)
    dw_kv = _dw(nk, dkv, 2 * KV_W, tmw, "dw_kv").reshape(N_CHIPS, D // 4, 2 * KV_W)
    dw_bin = _dw(nb, dqz, 512, tmw, "dw_b_in")
    dw_outb = _dw(ob, dy2, D, tmw, "dw_b_out").reshape(N_CHIPS, D // 4, D)
    grads1 = [dw_outa, dw_kv, dw_bin, dw_outb]
    names1 = ["a_w_out", "w_kv", "b_w_in", "b_w_out"]
    from_sibling1, _, _ = _sibling_exchange("to_sibling_1", to_sibling=grads1)
    sums1 = [_add_sibling(where, g, r, "add_sibling_" + nm) for g, r, nm in zip(grads1, from_sibling1, names1)]
    dproj, dconv_w, from_chips1 = _a_bwd(doa, proj, conv_w, wouta, tm_mix, [t for t, _ in sums1])
    halves1 = [_add_chips(own, r, "add_chips_" + nm) for (_, own), r, nm in zip(sums1, from_chips1, names1)]
    dw_in = _dw(n1, dproj, D, tmw, "dw_a_in")
    from_sibling2, (g_wouta, g_wkv, g_wbin, g_woutb), _ = _sibling_exchange(
        "to_sibling_2", to_sibling=[dw_in], halves=halves1)
    part2, own2 = _add_sibling(where, dw_in, from_sibling2[0], "add_sibling_a_w_in")
    grad_x, dg_apre, from_chips2 = _a_in_bwd(dproj, xs, dh1, win_g, g_apre, tm, [part2])
    half2 = _add_chips(own2, from_chips2[0], "add_chips_a_w_in")
    drel = _bias_fold(dtab)

    smalls = jnp.concatenate([
        dg_apre[0:1], dconv_w[0:3], dg_b[2:3], dg_b[0:1], dg_b[1:2], dg_bpost[0:1],
        _pad_cols(drel[:, 0:N_HEADS].reshape(1, N_BUCKETS * N_HEADS), D), _pad_cols(dsink[0:1], D),
        _pad_cols(loss_acc[0:1], D), jnp.zeros((SMALL_ROWS - 11, D), F32)], axis=0)
    _, (g_win,), small_all = _sibling_exchange("share_last", halves=[half2], smalls=smalls)
    tot = _sum_smalls(small_all)

    big = {}
    for nm, g, w, m, v in [("a_w_in", g_win, a_w_in, m_a_w_in, v_a_w_in), ("a_w_out", g_wouta, a_w_out, m_a_w_out, v_a_w_out),
                           ("w_kv", g_wkv, w_kv, m_w_kv, v_w_kv), ("b_w_in", g_wbin, b_w_in, m_b_w_in, v_b_w_in),
                           ("b_w_out", g_woutb, b_w_out, m_b_w_out, v_b_w_out)]:
        shp = w.shape
        two = (shp[-2], shp[-1])
        d, nm_, nv_ = _adamw(g, w.reshape(two), m.reshape(two), v.reshape(two), "adamw_" + nm)
        big[nm] = (g.reshape(shp), d.reshape(shp), nm_.reshape(shp), nv_.reshape(shp))

    col0 = chip * (D // 4)
    sharded = lax.dynamic_slice(tot, (0, col0), (8, D // 4))
    g_shard = sharded
    w_shard = small_w
    m_shard = _pad_rows(jnp.concatenate([m_a_pre_norm, m_a_conv_w[0], m_a_post_norm], axis=0), 8)
    v_shard = _pad_rows(jnp.concatenate([v_a_pre_norm, v_a_conv_w[0], v_a_post_norm], axis=0), 8)
    ds_, ms_, vs_ = _adamw(g_shard, w_shard, m_shard, v_shard, "adamw_small_sharded")

    def rep_pack(kvn, bpre, bpost, rel, snk):
        rows = [kvn.reshape(1, D), bpre.reshape(1, D), bpost.reshape(1, D),
                _pad_cols(rel.reshape(1, N_BUCKETS * N_HEADS), D), _pad_cols(snk.reshape(1, N_HEADS), D)]
        return jnp.concatenate(rows + [jnp.zeros((3, D), F32)], axis=0)

    g_rep = tot[5:13]
    w_rep = rep_pack(kv_norm, b_pre_norm, b_post_norm, rel_bias, b_sinks)
    m_rep = rep_pack(m_kv_norm, m_b_pre_norm, m_b_post_norm, m_rel_bias, m_b_sinks)
    v_rep = rep_pack(v_kv_norm, v_b_pre_norm, v_b_post_norm, v_rel_bias, v_b_sinks)
    dr_, mr_, vr_ = _adamw(g_rep, w_rep, m_rep, v_rep, "adamw_small_replicated")

    def unshard(p):
        return {"a_pre_norm": p[0:1], "a_conv_w": p[1:4].reshape(1, 3, D // 4), "a_post_norm": p[4:5]}

    def unrep(p):
        return {"kv_norm": p[0], "b_pre_norm": p[1:2], "b_post_norm": p[2:3],
                "rel_bias": p[3, 0:N_BUCKETS * N_HEADS].reshape(N_BUCKETS, N_HEADS), "b_sinks": p[4:5, 0:N_HEADS]}

    order = ["a_pre_norm", "a_w_in", "a_conv_w", "a_w_out", "a_post_norm", "kv_norm", "w_kv", "rel_bias",
             "b_pre_norm", "b_w_in", "b_sinks", "b_w_out", "b_post_norm"]
    outs = []
    for which, sh, rp in [(0, g_shard, g_rep), (1, ds_, dr_), (2, ms_, mr_), (3, vs_, vr_)]:
        small = {**unshard(sh), **unrep(rp)}
        for nm in order:
            outs.append(big[nm][which] if nm in big else small[nm])
    loss = 0.5 * tot[10, 0]
    return (loss, grad_x.reshape(x.shape), *outs)
```

```python
import functools
import math

import jax
import jax.numpy as jnp
from jax import lax
from jax.experimental import pallas as pl
from jax.experimental.pallas import tpu as pltpu

F32 = jnp.float32
BF16 = jnp.bfloat16
MESH = pl.DeviceIdType.MESH
SDS = jax.ShapeDtypeStruct

D = 1024
HEAD_DIM = 64
N_HEADS = 16
GROUP = 8
KV_W = 128
BLK = 128
N_BUCKETS = 32
MAX_EXACT = 16
MAX_DISTANCE = 128
EPS = 1e-6
NEG_INF = -1e30
Q_SCALE = HEAD_DIM ** -0.5

ADAM_LR = 0.001
ADAM_B1 = 0.9
ADAM_B2 = 0.999
ADAM_EPS = 1e-08
ADAM_WD = 0.01
ADAM_STEP = 10

N_CHIPS = 4
N_DEV = 8
VMEM_LIMIT = 56 * 1024 * 1024
SMALL_ROWS = 16


def _bucket_thresholds():
    def bucket(d):
        big = MAX_EXACT + int(math.log(d / MAX_EXACT) / math.log(MAX_DISTANCE / MAX_EXACT)
                              * (N_BUCKETS - MAX_EXACT))
        return d if d < MAX_EXACT else min(big, N_BUCKETS - 1)
    out = []
    for b in range(MAX_EXACT + 1, N_BUCKETS):
        out.append(min(d for d in range(MAX_EXACT, MAX_DISTANCE) if bucket(d) >= b))
    return tuple(out)


BUCKET_THRESHOLDS = _bucket_thresholds()


def _params(semantics=None, vmem=VMEM_LIMIT):
    return pltpu.CompilerParams(dimension_semantics=semantics, vmem_limit_bytes=vmem)


def _tile(n, pref):
    return pref if n >= 2 * pref else max(n // 2, 8)


def _rms_scale(v):
    return lax.rsqrt(jnp.mean(v * v, axis=-1, keepdims=True) + EPS)


def _nt(a, b):
    return lax.dot_general(a, b, (((1,), (1,)), ((), ())), preferred_element_type=F32)


def _tn(a, b):
    return lax.dot_general(a, b, (((0,), (0,)), ((), ())), preferred_element_type=F32)


def _nn(a, b):
    return jnp.dot(a, b, preferred_element_type=F32)


def _silu_parts(z):
    sg = jax.nn.sigmoid(z)
    return sg, z * sg


def _dsilu(z, sg):
    return sg * (1.0 + z * (1.0 - sg))


def _acc_row(ref, row, val):
    ref[row:row + 1, :] += val


def _gather_weights(shards, small):
    n = len(shards)

    def body(*refs):
        ins, small_in = refs[:n], refs[n]
        outs, small_out = refs[n + 1:2 * n + 1], refs[2 * n + 1]
        ici_send, ici_recv, d2d_send, d2d_recv = refs[2 * n + 2:]
        x, y, c = lax.axis_index("x"), lax.axis_index("y"), lax.axis_index("c")
        k = 2 * x + y
        chips = [(x, 1 - y), (1 - x, y), (1 - x, 1 - y)]
        for i_ref, o_ref in zip(ins, outs):
            o_ref[k] = i_ref[...].astype(BF16)
        small_out[k] = small_in[...]

        def half(o_ref, chip, core):
            h = o_ref.shape[1] // 2
            return o_ref.at[chip, pl.ds(pl.multiple_of(core * h, 16), h)]

        sends = []
        for a, o_ref in enumerate(list(outs) + [small_out]):
            split = a < n
            for j, (px, py) in enumerate(chips):
                src = half(o_ref, k, c) if split else o_ref.at[k]
                cp = pltpu.make_async_remote_copy(
                    src_ref=src, dst_ref=src, send_sem=ici_send.at[3 * a + j],
                    recv_sem=ici_recv.at[3 * a + j], device_id=(px, py, c), device_id_type=MESH)
                cp.start()
                sends.append(cp)
        for a, o_ref in enumerate(list(outs) + [small_out]):
            split = a < n
            for j, (px, py) in enumerate(chips):
                kj = 2 * px + py
                got = half(o_ref, kj, c) if split else o_ref.at[kj]
                pltpu.make_async_remote_copy(
                    src_ref=got, dst_ref=got, send_sem=ici_send.at[3 * a + j],
                    recv_sem=ici_recv.at[3 * a + j], device_id=(px, py, c),
                    device_id_type=MESH).wait_recv()
                if split:
                    fw = pltpu.make_async_remote_copy(
                        src_ref=got, dst_ref=got, send_sem=d2d_send.at[3 * a + j],
                        recv_sem=d2d_recv.at[3 * a + j], device_id=(x, y, 1 - c),
                        device_id_type=MESH)
                    fw.start()
                    sends.append(fw)
        for a, o_ref in enumerate(outs):
            for j, (px, py) in enumerate(chips):
                other = half(o_ref, 2 * px + py, 1 - c)
                pltpu.make_async_remote_copy(
                    src_ref=other, dst_ref=other, send_sem=d2d_send.at[3 * a + j],
                    recv_sem=d2d_recv.at[3 * a + j], device_id=(x, y, 1 - c),
                    device_id_type=MESH).wait_recv()
        for cp in sends:
            cp.wait_send()

    vm = pl.BlockSpec(memory_space=pltpu.VMEM)
    out_shape = [SDS((N_CHIPS,) + s.shape, BF16) for s in shards] + [SDS((N_CHIPS,) + small.shape, F32)]
    return pl.pallas_call(
        body, name="gather_weights", out_shape=out_shape,
        in_specs=[vm] * (n + 1), out_specs=[vm] * (n + 1),
        scratch_shapes=[pltpu.SemaphoreType.DMA((3 * (n + 1),)), pltpu.SemaphoreType.DMA((3 * (n + 1),)),
                        pltpu.SemaphoreType.DMA((3 * n,)), pltpu.SemaphoreType.DMA((3 * n,))],
        compiler_params=pltpu.CompilerParams(vmem_limit_bytes=VMEM_LIMIT),
    )(*shards, small)


def _a_in(x, g_pre, win_g, tm):
    s = x.shape[0]

    def body(x_ref, g_ref, w_ref, proj_ref, n1_ref):
        @pl.when(pl.program_id(1) == 0)
        def _():
            xv = x_ref[...]
            n1_ref[...] = (xv * _rms_scale(xv) * g_ref[...]).astype(BF16)
        proj_ref[...] = _nn(n1_ref[...], w_ref[0])

    return pl.pallas_call(
        body, name="a_in", grid=(s // tm, 4),
        in_specs=[pl.BlockSpec((tm, D), lambda i, j: (i, 0)), pl.BlockSpec((1, D), lambda i, j: (0, 0)),
                  pl.BlockSpec((1, D, D), lambda i, j: (j, 0, 0))],
        out_specs=[pl.BlockSpec((tm, D), lambda i, j: (i, j)), pl.BlockSpec((tm, D), lambda i, j: (i, 0))],
        out_shape=[SDS((s, 4 * D), F32), SDS((s, D), BF16)],
        compiler_params=_params(("parallel", "arbitrary")),
    )(x, g_pre, win_g)


def _shift_rows(v, before, rows):
    v1 = jnp.where(rows >= 1, pltpu.roll(v, 1, 0), before[7:8, :])
    v2 = jnp.where(rows >= 2, pltpu.roll(v, 2, 0), jnp.where(rows == 1, before[7:8, :], before[6:7, :]))
    return v1, v2


def _a_mix(proj, x, conv_w, w_out, g_post, tm):
    s = x.shape[0]

    def body(proj_ref, x_ref, cw_ref, w_ref, g_ref, ya_ref, oa_ref, h1_ref, carry):
        @pl.when(pl.program_id(0) == 0)
        def _():
            carry[...] = jnp.zeros_like(carry)
        v = proj_ref[:, D:2 * D] * proj_ref[:, 2 * D:3 * D]
        rows = lax.broadcasted_iota(jnp.int32, (tm, D), 0)
        v1, v2 = _shift_rows(v, carry[...], rows)
        carry[...] = v[tm - 8:tm, :]
        conv = cw_ref[0:1, :] * v2 + cw_ref[1:2, :] * v1 + cw_ref[2:3, :] * v
        _, sz = _silu_parts(proj_ref[:, 3 * D:4 * D])
        ya = (proj_ref[:, 0:D] * conv * sz).astype(BF16)
        ya_ref[...] = ya
        oa = _nn(ya, w_ref[...])
        oa_ref[...] = oa
        h1_ref[...] = x_ref[...] + oa * _rms_scale(oa) * g_ref[...]

    row = lambda i: (i, 0)
    fix = lambda i: (0, 0)
    return pl.pallas_call(
        body, name="a_mix", grid=(s // tm,),
        in_specs=[pl.BlockSpec((tm, 4 * D), row), pl.BlockSpec((tm, D), row), pl.BlockSpec((8, D), fix),
                  pl.BlockSpec((D, D), fix), pl.BlockSpec((1, D), fix)],
        out_specs=[pl.BlockSpec((tm, D), row)] * 3,
        out_shape=[SDS((s, D), BF16), SDS((s, D), F32), SDS((s, D), F32)],
        scratch_shapes=[pltpu.VMEM((8, D), F32)],
        compiler_params=_params(("arbitrary",)),
    )(proj, x, conv_w, w_out, g_post)


def _b_in(h1, g_kv, g_pre, w_kv, wbin_g, tm):
    s = h1.shape[0]

    def body(h_ref, gk_ref, gb_ref, wkv_ref, wb_ref, nk_ref, nb_ref, kv_ref, q_ref, z_ref):
        h = h_ref[...]
        hh = h * _rms_scale(h)
        nk = (hh * gk_ref[...]).astype(BF16)
        nb = (hh * gb_ref[...]).astype(BF16)
        nk_ref[...] = nk
        nb_ref[...] = nb
        kv_ref[...] = _nn(nk, wkv_ref[...]).astype(BF16)
        for j in range(2):
            q_ref[:, 512 * j:512 * (j + 1)] = (_nn(nb, wb_ref[j]) * Q_SCALE).astype(BF16)
            z_ref[:, 512 * j:512 * (j + 1)] = _nn(nb, wb_ref[2 + j])

    row = lambda i: (i, 0)
    fix = lambda i: (0, 0)
    return pl.pallas_call(
        body, name="b_in", grid=(s // tm,),
        in_specs=[pl.BlockSpec((tm, D), row), pl.BlockSpec((1, D), fix), pl.BlockSpec((1, D), fix),
                  pl.BlockSpec((D, 2 * KV_W), fix), pl.BlockSpec((4, D, 512), lambda i: (0, 0, 0))],
        out_specs=[pl.BlockSpec((tm, D), row), pl.BlockSpec((tm, D), row), pl.BlockSpec((tm, 2 * KV_W), row),
                   pl.BlockSpec((tm, D), row), pl.BlockSpec((tm, D), row)],
        out_shape=[SDS((s, D), BF16), SDS((s, D), BF16), SDS((s, 2 * KV_W), BF16), SDS((s, D), BF16),
                   SDS((s, D), F32)],
        compiler_params=_params(("parallel",)),
    )(h1, g_kv, g_pre, w_kv, wbin_g)


def _band_buckets():
    q = lax.broadcasted_iota(jnp.int32, (BLK, 2 * BLK), 0)
    k = lax.broadcasted_iota(jnp.int32, (BLK, 2 * BLK), 1)
    dist = q + BLK - k
    bucket = jnp.where(dist < MAX_EXACT, dist, MAX_EXACT)
    for t in BUCKET_THRESHOLDS:
        bucket = bucket + jnp.where(dist >= t, 1, 0)
    in_window = (dist >= 0) & (dist < BLK)
    return jnp.where(in_window, bucket, -1)


def _bias_table(rel_bias):
    def body(rb_ref, tab_ref):
        bucket = _band_buckets()
        for h in range(N_HEADS):
            acc = jnp.where(bucket < 0, NEG_INF, 0.0).astype(F32)
            for b in range(N_BUCKETS):
                acc = jnp.where(bucket == b, rb_ref[b, h], acc)
            tab_ref[h] = acc

    return pl.pallas_call(
        body, name="bias_table", out_shape=SDS((N_HEADS, BLK, 2 * BLK), F32),
        in_specs=[pl.BlockSpec(memory_space=pltpu.SMEM)],
        out_specs=pl.BlockSpec(memory_space=pltpu.VMEM),
    )(rel_bias)


def _bias_fold(dtab):
    def body(dtab_ref, out_ref):
        bucket = _band_buckets()
        row = lax.broadcasted_iota(jnp.int32, (N_BUCKETS, 128), 0)
        lane = lax.broadcasted_iota(jnp.int32, (N_BUCKETS, 128), 1)
        acc = jnp.zeros((N_BUCKETS, 128), F32)
        for h in range(N_HEADS):
            dt = dtab_ref[h]
            for b in range(N_BUCKETS):
                val = jnp.sum(jnp.where(bucket == b, dt, 0.0))
                acc = acc + jnp.where((row == b) & (lane == h), val, 0.0)
        out_ref[...] = acc

    return pl.pallas_call(
        body, name="bias_fold", out_shape=SDS((N_BUCKETS, 128), F32),
        in_specs=[pl.BlockSpec(memory_space=pltpu.VMEM)],
        out_specs=pl.BlockSpec(memory_space=pltpu.VMEM),
    )(dtab)


def _pair_operands(prev, cur):
    t = jnp.concatenate([prev, cur], axis=0).astype(F32)
    tr = pltpu.roll(t, HEAD_DIM, 1)
    lo = lax.broadcasted_iota(jnp.int32, t.shape, 1) < HEAD_DIM
    zero = jnp.zeros_like(t)
    head0 = jnp.concatenate([jnp.where(lo, t, zero), jnp.where(lo, zero, tr)], axis=0).astype(BF16)
    head1 = jnp.concatenate([jnp.where(lo, tr, zero), jnp.where(lo, zero, t)], axis=0).astype(BF16)
    return head0, head1


def _pair_fold(d0, d1):
    lo = lax.broadcasted_iota(jnp.int32, (2 * BLK, KV_W), 1) < HEAD_DIM
    zero = jnp.zeros((2 * BLK, KV_W), F32)
    g0 = jnp.where(lo, d0[0:256], zero) + pltpu.roll(jnp.where(lo, zero, d0[256:512]), HEAD_DIM, 1)
    g1 = pltpu.roll(jnp.where(lo, d1[0:256], zero), HEAD_DIM, 1) + jnp.where(lo, zero, d1[256:512])
    return g0 + g1


def _first_block_mask(n):
    col = lax.broadcasted_iota(jnp.int32, (BLK, 2 * BLK), 1)
    return jnp.where((n == 0) & (col < BLK), NEG_INF, 0.0).astype(F32)


def _attn_fwd(q, kv, tab, sinks):
    s = q.shape[0]

    def body(sink_ref, q_ref, kp_ref, kc_ref, vp_ref, vc_ref, tab_ref, att_ref, stats_ref):
        n = pl.program_id(0)
        k2 = _pair_operands(kp_ref[...], kc_ref[...])
        v2 = _pair_operands(vp_ref[...], vc_ref[...])
        first = _first_block_mask(n)
        lane = lax.broadcasted_iota(jnp.int32, (BLK, 128), 1)
        stats = jnp.zeros((BLK, 128), F32)
        for j in range(N_HEADS // 2):
            kh = (2 * j) // GROUP
            sc = _nt(q_ref[:, 128 * j:128 * (j + 1)], k2[kh])
            ps = []
            for e in range(2):
                h = 2 * j + e
                sink = sink_ref[h]
                lg = sc[:, 256 * e:256 * (e + 1)] + tab_ref[h] + first
                m = jnp.maximum(jnp.max(lg, axis=-1, keepdims=True), sink)
                ex = jnp.exp(lg - m)
                den = jnp.sum(ex, axis=-1, keepdims=True) + jnp.exp(sink - m)
                ps.append(ex * (1.0 / den))
                stats = jnp.where(lane == h, m + jnp.log(den), stats)
            p2 = jnp.concatenate(ps, axis=1).astype(BF16)
            att_ref[:, 128 * j:128 * (j + 1)] = _nn(p2, v2[kh])
        stats_ref[...] = stats

    cur = lambda n: (n, 0)
    prev = lambda n: (jnp.maximum(n - 1, 0), 0)
    return pl.pallas_call(
        body, name="attn_fwd", grid=(s // BLK,),
        in_specs=[pl.BlockSpec(memory_space=pltpu.SMEM), pl.BlockSpec((BLK, D), cur),
                  pl.BlockSpec((BLK, KV_W), prev), pl.BlockSpec((BLK, KV_W), cur),
                  pl.BlockSpec((BLK, KV_W), lambda n: (jnp.maximum(n - 1, 0), 1)),
                  pl.BlockSpec((BLK, KV_W), lambda n: (n, 1)),
                  pl.BlockSpec((N_HEADS, BLK, 2 * BLK), lambda n: (0, 0, 0))],
        out_specs=[pl.BlockSpec((BLK, D), cur), pl.BlockSpec((BLK, 128), cur)],
        out_shape=[SDS((s, D), F32), SDS((s, 128), F32)],
        compiler_params=_params(("parallel",)),
    )(sinks, q, kv, kv, kv, kv, tab)


def _mid(att, zb, h1, tgt, w_out, g_post, tm):
    s = att.shape[0]

    def body(att_ref, z_ref, h1_ref, t_ref, w_ref, g_ref,
             ob_ref, dy_ref, dh_ref, dqz_ref, datt_ref, loss_ref, dg_ref):
        @pl.when(pl.program_id(0) == 0)
        def _():
            loss_ref[...] = jnp.zeros_like(loss_ref)
            dg_ref[...] = jnp.zeros_like(dg_ref)
        att = att_ref[...]
        z = z_ref[...]
        sg, sz = _silu_parts(z)
        ob = (att * sz).astype(BF16)
        ob_ref[...] = ob
        y2 = _nn(ob, w_ref[...])
        r2 = _rms_scale(y2)
        yh = y2 * r2
        g = g_ref[...]
        err = (h1_ref[...] + yh * g) - t_ref[...]
        loss_ref[...] += jnp.sum(jnp.sum(err * err, axis=-1, keepdims=True) / D)
        dh = err / D
        dh_ref[...] = dh
        _acc_row(dg_ref, 0, jnp.sum(dh * yh, axis=0, keepdims=True))
        dyh = dh * g
        dy = (r2 * (dyh - yh * jnp.mean(dyh * yh, axis=-1, keepdims=True))).astype(BF16)
        dy_ref[...] = dy
        dob = _nt(dy, w_ref[...])
        datt_ref[...] = (dob * sz).astype(BF16)
        dqz_ref[...] = (dob * att * _dsilu(z, sg)).astype(BF16)

    row = lambda i: (i, 0)
    fix = lambda i: (0, 0)
    return pl.pallas_call(
        body, name="mid", grid=(s // tm,),
        in_specs=[pl.BlockSpec((tm, D), row)] * 4 + [pl.BlockSpec((D, D), fix), pl.BlockSpec((1, D), fix)],
        out_specs=[pl.BlockSpec((tm, D), row), pl.BlockSpec((tm, D), row), pl.BlockSpec((tm, D), row),
                   pl.BlockSpec((tm, D), lambda i: (i, 1)), pl.BlockSpec((tm, D), row),
                   pl.BlockSpec((8, 128), fix), pl.BlockSpec((8, D), fix)],
        out_shape=[SDS((s, D), BF16), SDS((s, D), BF16), SDS((s, D), F32), SDS((s, 2 * D), BF16),
                   SDS((s, D), BF16), SDS((8, 128), F32), SDS((8, D), F32)],
        compiler_params=_params(("arbitrary",)),
    )(att, zb, h1, tgt, w_out, g_post)


def _attn_bwd(q, kv, datt, stats, tab, sinks, dqz):
    s = q.shape[0]
    nb = s // BLK

    def body(sink_ref, q_ref, kp_ref, kc_ref, vp_ref, vc_ref, da_ref, st_ref, tab_ref, dqz_in,
             dq_ref, dkv_ref, dtab_ref, dsink_ref, dk_carry, dv_carry):
        del dqz_in
        n = pl.program_id(0)

        @pl.when(n == 0)
        def _():
            dtab_ref[...] = jnp.zeros_like(dtab_ref)
            dsink_ref[...] = jnp.zeros_like(dsink_ref)
            dk_carry[...] = jnp.zeros_like(dk_carry)
            dv_carry[...] = jnp.zeros_like(dv_carry)

        @pl.when(n < nb)
        def _():
            k2 = _pair_operands(kp_ref[...], kc_ref[...])
            v2 = _pair_operands(vp_ref[...], vc_ref[...])
            first = _first_block_mask(n)
            lane = lax.broadcasted_iota(jnp.int32, (BLK, 128), 1)
            lane8 = lax.broadcasted_iota(jnp.int32, (8, 128), 1)
            stats = st_ref[...]
            dk2 = [jnp.zeros((4 * BLK, KV_W), F32), jnp.zeros((4 * BLK, KV_W), F32)]
            dv2 = [jnp.zeros((4 * BLK, KV_W), F32), jnp.zeros((4 * BLK, KV_W), F32)]
            dsink = jnp.zeros((8, 128), F32)
            for j in range(N_HEADS // 2):
                kh = (2 * j) // GROUP
                q2 = q_ref[:, 128 * j:128 * (j + 1)]
                da2 = da_ref[:, 128 * j:128 * (j + 1)]
                sc = _nt(q2, k2[kh])
                dp = _nt(da2, v2[kh])
                ps, dss = [], []
                for e in range(2):
                    h = 2 * j + e
                    lse = jnp.sum(jnp.where(lane == h, stats, 0.0), axis=-1, keepdims=True)
                    p = jnp.exp(sc[:, 256 * e:256 * (e + 1)] + tab_ref[h] + first - lse)
                    dpe = dp[:, 256 * e:256 * (e + 1)]
                    delta = jnp.sum(p * dpe, axis=-1, keepdims=True)
                    ds = p * (dpe - delta)
                    dtab_ref[h] += ds
                    dsink = dsink - jnp.where(lane8 == h, jnp.sum(jnp.exp(sink_ref[h] - lse) * delta), 0.0)
                    ps.append(p)
                    dss.append(ds)
                p2 = jnp.concatenate(ps, axis=1).astype(BF16)
                ds2 = jnp.concatenate(dss, axis=1).astype(BF16)
                dq_ref[:, 128 * j:128 * (j + 1)] = (_nn(ds2, k2[kh]) * Q_SCALE).astype(BF16)
                dk2[kh] = dk2[kh] + _tn(ds2, q2)
                dv2[kh] = dv2[kh] + _tn(p2, da2)
            dsink_ref[...] += dsink
            dkk = _pair_fold(dk2[0], dk2[1])
            dvv = _pair_fold(dv2[0], dv2[1])
            dkv_ref[:, 0:KV_W] = (dk_carry[...] + dkk[0:BLK]).astype(BF16)
            dkv_ref[:, KV_W:2 * KV_W] = (dv_carry[...] + dvv[0:BLK]).astype(BF16)
            dk_carry[...] = dkk[BLK:2 * BLK]
            dv_carry[...] = dvv[BLK:2 * BLK]

        @pl.when(n == nb)
        def _():
            dkv_ref[:, 0:KV_W] = dk_carry[...].astype(BF16)
            dkv_ref[:, KV_W:2 * KV_W] = dv_carry[...].astype(BF16)

    cur = lambda n: (jnp.minimum(n, nb - 1), 0)
    prev = lambda n: (jnp.clip(n - 1, 0, nb - 1), 0)
    return pl.pallas_call(
        body, name="attn_bwd", grid=(nb + 1,),
        in_specs=[pl.BlockSpec(memory_space=pltpu.SMEM), pl.BlockSpec((BLK, D), cur),
                  pl.BlockSpec((BLK, KV_W), prev), pl.BlockSpec((BLK, KV_W), cur),
                  pl.BlockSpec((BLK, KV_W), lambda n: (jnp.clip(n - 1, 0, nb - 1), 1)),
                  pl.BlockSpec((BLK, KV_W), lambda n: (jnp.minimum(n, nb - 1), 1)),
                  pl.BlockSpec((BLK, D), cur), pl.BlockSpec((BLK, 128), cur),
                  pl.BlockSpec((N_HEADS, BLK, 2 * BLK), lambda n: (0, 0, 0)),
                  pl.BlockSpec(memory_space=pl.ANY)],
        out_specs=[pl.BlockSpec((BLK, D), cur), pl.BlockSpec((BLK, 2 * KV_W), prev),
                   pl.BlockSpec((N_HEADS, BLK, 2 * BLK), lambda n: (0, 0, 0)),
                   pl.BlockSpec((8, 128), lambda n: (0, 0))],
        out_shape=[SDS((s, 2 * D), BF16), SDS((s, 2 * KV_W), BF16), SDS((N_HEADS, BLK, 2 * BLK), F32),
                   SDS((8, 128), F32)],
        scratch_shapes=[pltpu.VMEM((BLK, KV_W), F32), pltpu.VMEM((BLK, KV_W), F32)],
        input_output_aliases={9: 0},
        compiler_params=_params(("arbitrary",)),
    )(sinks, q, kv, kv, kv, kv, datt, stats, tab, dqz)


def _b_bwd(dqz, dkv, h1, dh2, oa, wbin_g, w_kv, g_kv, g_pre, g_apost, tm):
    s = h1.shape[0]

    def body(dqz_ref, dkv_ref, h_ref, dh2_ref, oa_ref, wb_ref, wkv_ref, gk_ref, gb_ref, ga_ref,
             dh1_ref, doa_ref, dg_ref):
        @pl.when(pl.program_id(0) == 0)
        def _():
            dg_ref[...] = jnp.zeros_like(dg_ref)
        dnb = _nt(dqz_ref[:, 0:512], wb_ref[0])
        for j in range(1, 4):
            dnb = dnb + _nt(dqz_ref[:, 512 * j:512 * (j + 1)], wb_ref[j])
        dnk = _nt(dkv_ref[...], wkv_ref[...])
        h = h_ref[...]
        r = _rms_scale(h)
        hh = h * r
        _acc_row(dg_ref, 0, jnp.sum(dnk * hh, axis=0, keepdims=True))
        _acc_row(dg_ref, 1, jnp.sum(dnb * hh, axis=0, keepdims=True))
        dhh = dnb * gb_ref[...] + dnk * gk_ref[...]
        dh1 = dh2_ref[...] + r * (dhh - hh * jnp.mean(dhh * hh, axis=-1, keepdims=True))
        dh1_ref[...] = dh1
        oa = oa_ref[...]
        ra = _rms_scale(oa)
        oh = oa * ra
        _acc_row(dg_ref, 2, jnp.sum(dh1 * oh, axis=0, keepdims=True))
        doh = dh1 * ga_ref[...]
        doa_ref[...] = (ra * (doh - oh * jnp.mean(doh * oh, axis=-1, keepdims=True))).astype(BF16)

    row = lambda i: (i, 0)
    fix = lambda i: (0, 0)
    return pl.pallas_call(
        body, name="b_bwd", grid=(s // tm,),
        in_specs=[pl.BlockSpec((tm, 2 * D), row), pl.BlockSpec((tm, 2 * KV_W), row), pl.BlockSpec((tm, D), row),
                  pl.BlockSpec((tm, D), row), pl.BlockSpec((tm, D), row),
                  pl.BlockSpec((4, D, 512), lambda i: (0, 0, 0)), pl.BlockSpec((D, 2 * KV_W), fix),
                  pl.BlockSpec((1, D), fix), pl.BlockSpec((1, D), fix), pl.BlockSpec((1, D), fix)],
        out_specs=[pl.BlockSpec((tm, D), row), pl.BlockSpec((tm, D), row), pl.BlockSpec((8, D), fix)],
        out_shape=[SDS((s, D), F32), SDS((s, D), BF16), SDS((8, D), F32)],
        compiler_params=_params(("arbitrary",)),
    )(dqz, dkv, h1, dh2, oa, wbin_g, w_kv, g_kv, g_pre, g_apost)


def _chip_exchange(parts, recvs, send, recv):
    x, y, c = lax.axis_index("x"), lax.axis_index("y"), lax.axis_index("c")
    chips = [(x, 1 - y), (1 - x, y), (1 - x, 1 - y)]
    copies = []
    for a, (t, r) in enumerate(zip(parts, recvs)):
        for j, (px, py) in enumerate(chips):
            copies.append(pltpu.make_async_remote_copy(
                src_ref=t.at[2 * px + py], dst_ref=r.at[j], send_sem=send.at[3 * a + j],
                recv_sem=recv.at[3 * a + j], device_id=(px, py, c), device_id_type=MESH))
    return copies


def _exchange_specs(parts):
    anyspace = pl.BlockSpec(memory_space=pl.ANY)
    n = len(parts)
    return ([anyspace] * n, [anyspace] * n, [SDS((3,) + t.shape[1:], t.dtype) for t in parts],
            [pltpu.SemaphoreType.DMA((3 * n,)), pltpu.SemaphoreType.DMA((3 * n,))])


def _a_bwd(doa, proj, conv_w, w_out, tm, parts):
    s = doa.shape[0]
    nt = s // tm
    n = len(parts)
    ex_in, ex_out, ex_shape, ex_sems = _exchange_specs(parts)

    def body(*refs):
        doa_ref, proj_ref, halo_ref, cw_ref, w_ref = refs[:5]
        part_refs = refs[5:5 + n]
        dproj_ref, dcw_ref = refs[5 + n:7 + n]
        recv_refs = refs[7 + n:7 + 2 * n]
        carry, send, recv = refs[7 + 2 * n:]
        i = pl.program_id(0)
        r = nt - 1 - i

        @pl.when(i == 0)
        def _():
            dcw_ref[...] = jnp.zeros_like(dcw_ref)
            carry[...] = jnp.zeros_like(carry)
            for cp in _chip_exchange(part_refs, recv_refs, send, recv):
                cp.start()
        dya = _nt(doa_ref[...], w_ref[...])
        bg = proj_ref[:, 0:D]
        cg = proj_ref[:, D:2 * D]
        u = proj_ref[:, 2 * D:3 * D]
        z = proj_ref[:, 3 * D:4 * D]
        v = cg * u
        before = jnp.where(r > 0, halo_ref[:, D:2 * D] * halo_ref[:, 2 * D:3 * D], 0.0)
        rows = lax.broadcasted_iota(jnp.int32, (tm, D), 0)
        v1, v2 = _shift_rows(v, before, rows)
        conv = cw_ref[0:1, :] * v2 + cw_ref[1:2, :] * v1 + cw_ref[2:3, :] * v
        sg, sz = _silu_parts(z)
        dproj_ref[:, 0:D] = (dya * conv * sz).astype(BF16)
        dproj_ref[:, 3 * D:4 * D] = (dya * bg * conv * _dsilu(z, sg)).astype(BF16)
        dconv = dya * bg * sz
        _acc_row(dcw_ref, 0, jnp.sum(dconv * v2, axis=0, keepdims=True))
        _acc_row(dcw_ref, 1, jnp.sum(dconv * v1, axis=0, keepdims=True))
        _acc_row(dcw_ref, 2, jnp.sum(dconv * v, axis=0, keepdims=True))
        after = carry[...]
        up1 = jnp.where(rows < tm - 1, pltpu.roll(dconv, tm - 1, 0), after[0:1, :])
        up2 = jnp.where(rows < tm - 2, pltpu.roll(dconv, tm - 2, 0),
                        jnp.where(rows == tm - 2, after[0:1, :], after[1:2, :]))
        carry[...] = dconv[0:8, :]
        dv = cw_ref[2:3, :] * dconv + cw_ref[1:2, :] * up1 + cw_ref[0:1, :] * up2
        dproj_ref[:, D:2 * D] = (dv * u).astype(BF16)
        dproj_ref[:, 2 * D:3 * D] = (dv * cg).astype(BF16)

        @pl.when(i == nt - 1)
        def _():
            for cp in _chip_exchange(part_refs, recv_refs, send, recv):
                cp.wait()

    rev = lambda i: (nt - 1 - i, 0)
    fix = lambda i: (0, 0)
    halo = lambda i: (jnp.maximum((nt - 1 - i) * (tm // 8) - 1, 0), 0)
    dproj, dcw, *got = pl.pallas_call(
        body, name="a_bwd", grid=(nt,),
        in_specs=[pl.BlockSpec((tm, D), rev), pl.BlockSpec((tm, 4 * D), rev), pl.BlockSpec((8, 4 * D), halo),
                  pl.BlockSpec((8, D), fix), pl.BlockSpec((D, D), fix)] + ex_in,
        out_specs=[pl.BlockSpec((tm, 4 * D), rev), pl.BlockSpec((8, D), fix)] + ex_out,
        out_shape=[SDS((s, 4 * D), BF16), SDS((8, D), F32)] + ex_shape,
        scratch_shapes=[pltpu.VMEM((8, D), F32)] + ex_sems,
        compiler_params=_params(("arbitrary",)),
    )(doa, proj, proj, conv_w, w_out, *parts)
    return dproj, dcw, got


def _a_in_bwd(dproj, x, dh1, win_g, g_pre, tm, first, count, name, parts=(), before=None):
    s = x.shape[0]
    nt = count
    n = len(parts)
    ex_in, ex_out, ex_shape, ex_sems = _exchange_specs(parts) if n else ([], [], [], [])
    goes_on = before is not None

    def body(*refs):
        dp_ref, x_ref, dh_ref, w_ref, g_ref = refs[:5]
        part_refs = refs[5:5 + n]
        pos = 5 + n
        dg_before = refs[pos + 1] if goes_on else None
        pos += 2 * goes_on
        gx_ref, dg_ref = refs[pos:pos + 2]
        recv_refs = refs[pos + 2:pos + 2 + n]
        sems = refs[pos + 2 + n:]

        @pl.when(pl.program_id(0) == 0)
        def _():
            dg_ref[...] = dg_before[...] if goes_on else jnp.zeros_like(dg_ref)
            if n:
                for cp in _chip_exchange(part_refs, recv_refs, *sems):
                    cp.start()
        dn = _nt(dp_ref[:, 0:D], w_ref[0])
        for j in range(1, 4):
            dn = dn + _nt(dp_ref[:, D * j:D * (j + 1)], w_ref[j])
        xv = x_ref[...]
        r = _rms_scale(xv)
        xh = xv * r
        _acc_row(dg_ref, 0, jnp.sum(dn * xh, axis=0, keepdims=True))
        dxh = dn * g_ref[...]
        gx_ref[...] = dh_ref[...] + r * (dxh - xh * jnp.mean(dxh * xh, axis=-1, keepdims=True))

        if n:
            @pl.when(pl.program_id(0) == nt - 1)
            def _():
                for cp in _chip_exchange(part_refs, recv_refs, *sems):
                    cp.wait()

    row = lambda i: (first + i, 0)
    fix = lambda i: (0, 0)
    goes_on_in = [pl.BlockSpec(memory_space=pl.ANY), pl.BlockSpec((8, D), fix)] if goes_on else []
    gx, dg, *got = pl.pallas_call(
        body, name=name, grid=(nt,),
        in_specs=[pl.BlockSpec((tm, 4 * D), row), pl.BlockSpec((tm, D), row), pl.BlockSpec((tm, D), row),
                  pl.BlockSpec((4, D, D), lambda i: (0, 0, 0)), pl.BlockSpec((1, D), fix)] + ex_in + goes_on_in,
        out_specs=[pl.BlockSpec((tm, D), row), pl.BlockSpec((8, D), fix)] + ex_out,
        out_shape=[SDS((s, D), F32), SDS((8, D), F32)] + ex_shape,
        scratch_shapes=ex_sems,
        input_output_aliases={5 + n: 0} if goes_on else {},
        compiler_params=_params(("arbitrary",)),
    )(dproj, x, dh1, win_g, g_pre, *parts, *(before if goes_on else ()))
    return gx, dg, got


def _dw(a, b, tn, tmw, name):
    s, k = a.shape
    n = b.shape[1]

    def body(a_ref, b_ref, o_ref):
        @pl.when(pl.program_id(1) == 0)
        def _():
            o_ref[...] = jnp.zeros_like(o_ref)
        o_ref[0] += _tn(a_ref[...], b_ref[...])

    return pl.pallas_call(
        body, name=name, grid=(n // tn, s // tmw),
        in_specs=[pl.BlockSpec((tmw, k), lambda j, t: (t, 0)), pl.BlockSpec((tmw, tn), lambda j, t: (t, j))],
        out_specs=pl.BlockSpec((1, k, tn), lambda j, t: (j, 0, 0)),
        out_shape=SDS((n // tn, k, tn), F32),
        compiler_params=_params(("parallel", "arbitrary")),
    )(a, b)


def _sibling_exchange(name, to_sibling=(), shards=(), smalls=None):
    n_g, n_h = len(to_sibling), len(shards)
    has_small = smalls is not None

    def body(*refs):
        gs = refs[:n_g]
        pos = n_g + n_h
        small_in = refs[pos] if has_small else None
        pos += has_small
        rs, fs = refs[pos:pos + n_g], refs[pos + n_g:pos + n_g + n_h]
        pos += n_g + n_h
        small_all = refs[pos] if has_small else None
        pos += has_small
        dsend, drecv, ssend, srecv = refs[pos:]
        x, y, c = lax.axis_index("x"), lax.axis_index("y"), lax.axis_index("c")
        sibling = (x, y, 1 - c)
        sends, arrivals = [], []
        for a, (g, r) in enumerate(zip(gs, rs)):
            h = g.shape[1] // 2
            src = g.at[:, pl.ds(pl.multiple_of((1 - c) * h, 8), h), :]
            sends.append(pltpu.make_async_remote_copy(src_ref=src, dst_ref=r, send_sem=dsend.at[a], recv_sem=drecv.at[a],
                                                      device_id=sibling, device_id_type=MESH))
            arrivals.append(pltpu.make_async_remote_copy(src_ref=r, dst_ref=r, send_sem=dsend.at[a], recv_sem=drecv.at[a],
                                                         device_id=sibling, device_id_type=MESH))
        for b, full in enumerate(fs):
            h = full.shape[0] // 2
            mine = full.at[pl.ds(pl.multiple_of(c * h, 8), h)]
            theirs = full.at[pl.ds(pl.multiple_of((1 - c) * h, 8), h)]
            sends.append(pltpu.make_async_remote_copy(src_ref=mine, dst_ref=mine, send_sem=dsend.at[n_g + b],
                                                      recv_sem=drecv.at[n_g + b], device_id=sibling, device_id_type=MESH))
            arrivals.append(pltpu.make_async_remote_copy(src_ref=mine, dst_ref=theirs, send_sem=dsend.at[n_g + b],
                                                         recv_sem=drecv.at[n_g + b], device_id=sibling, device_id_type=MESH))
        if has_small:
            me = 4 * x + 2 * y + c
            small_all[me] = small_in[...]
            for rel in range(1, N_DEV):
                fx, fy, fc = rel >> 2, (rel >> 1) & 1, rel & 1
                peer = (x + fx - 2 * x * fx, y + fy - 2 * y * fy, c + fc - 2 * c * fc)
                sender = 4 * peer[0] + 2 * peer[1] + peer[2]
                sends.append(pltpu.make_async_remote_copy(
                    src_ref=small_in, dst_ref=small_all.at[me], send_sem=ssend.at[rel - 1], recv_sem=srecv.at[rel - 1],
                    device_id=peer, device_id_type=MESH))
                arrivals.append(pltpu.make_async_remote_copy(
                    src_ref=small_in, dst_ref=small_all.at[sender], send_sem=ssend.at[rel - 1], recv_sem=srecv.at[rel - 1],
                    device_id=peer, device_id_type=MESH))
        for cp in sends:
            cp.start()
        for cp in arrivals:
            cp.wait_recv()
        for cp in sends:
            cp.wait_send()

    anyspace = pl.BlockSpec(memory_space=pl.ANY)
    vm = pl.BlockSpec(memory_space=pltpu.VMEM)
    out_shape = [SDS((N_CHIPS, g.shape[1] // 2, g.shape[2]), F32) for g in to_sibling]
    out_shape += [SDS(full.shape, F32) for full in shards]
    if has_small:
        out_shape.append(SDS((N_DEV,) + smalls.shape, F32))
    n_d2d = max(n_g + n_h, 1)
    outs = pl.pallas_call(
        body, name=name, out_shape=out_shape,
        in_specs=[anyspace] * (n_g + n_h) + [vm] * has_small, out_specs=[anyspace] * (n_g + n_h) + [vm] * has_small,
        scratch_shapes=[pltpu.SemaphoreType.DMA((n_d2d,)), pltpu.SemaphoreType.DMA((n_d2d,)),
                        pltpu.SemaphoreType.DMA((N_DEV - 1,)), pltpu.SemaphoreType.DMA((N_DEV - 1,))],
        input_output_aliases={n_g + b: n_g + b for b in range(n_h)},
    )(*to_sibling, *shards, *([smalls] if has_small else []))
    return outs[:n_g], outs[n_g:n_g + n_h], (outs[n_g + n_h] if has_small else None)


def _add_sibling(where, g, r, name):
    _, rows, cols = g.shape
    h = rows // 2
    tr = min(h, 256)
    nh = h // tr

    def body(where_ref, g_ref, r_ref, t_ref, own_ref):
        t = g_ref[0] + r_ref[0]
        t_ref[0] = t.astype(BF16)

        @pl.when(pl.program_id(1) == where_ref[1])
        def _():
            own_ref[...] = t

    return pl.pallas_call(
        body, name=name,
        grid_spec=pltpu.PrefetchScalarGridSpec(
            num_scalar_prefetch=1, grid=(nh, N_CHIPS),
            in_specs=[pl.BlockSpec((1, tr, cols), lambda i, k, w: (k, w[0] * nh + i, 0)),
                      pl.BlockSpec((1, tr, cols), lambda i, k, w: (k, i, 0))],
            out_specs=[pl.BlockSpec((1, tr, cols), lambda i, k, w: (k, i, 0)),
                       pl.BlockSpec((tr, cols), lambda i, k, w: (i, 0))]),
        out_shape=[SDS((N_CHIPS, h, cols), BF16), SDS((h, cols), F32)],
        compiler_params=_params(("parallel", "arbitrary")),
    )(where, g, r)


def _add_chips(where, own, r, name):
    h, cols = own.shape
    tr = min(h, 256)
    nh = h // tr

    def body(where_ref, t_ref, r_ref, o_ref):
        del where_ref
        o_ref[...] = ((t_ref[...] + r_ref[0].astype(F32)) + r_ref[1].astype(F32)) + r_ref[2].astype(F32)

    return pl.pallas_call(
        body, name=name,
        grid_spec=pltpu.PrefetchScalarGridSpec(
            num_scalar_prefetch=1, grid=(nh,),
            in_specs=[pl.BlockSpec((tr, cols), lambda i, w: (i, 0)), pl.BlockSpec((3, tr, cols), lambda i, w: (0, i, 0))],
            out_specs=pl.BlockSpec((tr, cols), lambda i, w: (w[0] * nh + i, 0))),
        out_shape=SDS((2 * h, cols), F32),
        compiler_params=_params(("parallel",)),
    )(where, own, r)


def _sum_smalls(small_all):
    def body(all_ref, o_ref):
        acc = all_ref[0]
        for dev in range(1, N_DEV):
            acc = acc + all_ref[dev]
        o_ref[...] = acc

    return pl.pallas_call(
        body, name="sum_smalls", out_shape=SDS(small_all.shape[1:], F32),
        in_specs=[pl.BlockSpec(memory_space=pltpu.VMEM)], out_specs=pl.BlockSpec(memory_space=pltpu.VMEM),
    )(small_all)


def _adamw(g, w, m, v, name):
    rows, cols = g.shape
    tr = min(rows, 256)

    def body(g_ref, w_ref, m_ref, v_ref, d_ref, nm_ref, nv_ref):
        gv = g_ref[...]
        nm = ADAM_B1 * m_ref[...] + (1.0 - ADAM_B1) * gv
        nv = ADAM_B2 * v_ref[...] + (1.0 - ADAM_B2) * (gv * gv)
        nm_ref[...] = nm
        nv_ref[...] = nv
        m_hat = nm / (1.0 - ADAM_B1 ** ADAM_STEP)
        v_hat = nv / (1.0 - ADAM_B2 ** ADAM_STEP)
        d_ref[...] = -ADAM_LR * (m_hat / (jnp.sqrt(v_hat) + ADAM_EPS) + ADAM_WD * w_ref[...])

    spec = pl.BlockSpec((tr, cols), lambda i: (i, 0))
    return pl.pallas_call(
        body, name=name, grid=(rows // tr,), in_specs=[spec] * 4, out_specs=[spec] * 3,
        out_shape=[SDS(g.shape, F32)] * 3, compiler_params=_params(("parallel",)),
    )(g, w, m, v)


def _pad_rows(a, rows):
    return jnp.concatenate([a, jnp.zeros((rows - a.shape[0], a.shape[1]), a.dtype)], axis=0)


def _pad_cols(a, cols):
    return jnp.concatenate([a, jnp.zeros((a.shape[0], cols - a.shape[1]), a.dtype)], axis=1)


def kernel(x, a_pre_norm, a_w_in, a_conv_w, a_w_out, a_post_norm, kv_norm, w_kv, rel_bias, b_pre_norm, b_w_in, b_sinks, b_w_out, b_post_norm, loss_target, m_a_pre_norm, m_a_w_in, m_a_conv_w, m_a_w_out, m_a_post_norm, m_kv_norm, m_w_kv, m_rel_bias, m_b_pre_norm, m_b_w_in, m_b_sinks, m_b_w_out, m_b_post_norm, v_a_pre_norm, v_a_w_in, v_a_conv_w, v_a_w_out, v_a_post_norm, v_kv_norm, v_w_kv, v_rel_bias, v_b_pre_norm, v_b_w_in, v_b_sinks, v_b_w_out, v_b_post_norm):
    seq = x.shape[1]
    xs = x.reshape(seq, D)
    tgt = loss_target.reshape(seq, D)
    chip = 2 * lax.axis_index("x") + lax.axis_index("y")
    core = lax.axis_index("c")
    tm = _tile(seq, 512)
    tm_mix = _tile(seq, 256)
    tmw = _tile(seq, 1024)

    shards = [a_w_in[0], a_w_out[0], w_kv, b_w_in[0], b_w_out[0]]
    small_w = _pad_rows(jnp.concatenate([a_pre_norm, a_conv_w[0], a_post_norm], axis=0), 8)
    win_g, wouta_g, wkv_g, wbin_g, woutb_g, small_g = _gather_weights(shards, small_w)
    small_full = small_g.transpose(1, 0, 2).reshape(8, D)
    g_apre, conv_w, g_apost = small_full[0:1], _pad_rows(small_full[1:4], 8), small_full[4:5]
    wouta = wouta_g.reshape(D, D)
    wkv = wkv_g.reshape(D, 2 * KV_W)
    woutb = woutb_g.reshape(D, D)
    g_kv = kv_norm.reshape(1, D)

    proj, n1 = _a_in(xs, g_apre, win_g, tm)
    ya, oa, h1 = _a_mix(proj, xs, conv_w, wouta, g_apost, tm_mix)
    nk, nb, kv, q, zb = _b_in(h1, g_kv, b_pre_norm, wkv, wbin_g, tm)
    tab = _bias_table(rel_bias)
    sinks = b_sinks.reshape(N_HEADS)
    att, stats = _attn_fwd(q, kv, tab, sinks)
    ob, dy2, dh2, dqz, datt, loss_acc, dg_bpost = _mid(att, zb, h1, tgt, woutb, b_post_norm, tm)

    dqz, dkv, dtab, dsink = _attn_bwd(q, kv, datt, stats, tab, sinks, dqz)
    dh1, doa, dg_b = _b_bwd(dqz, dkv, h1, dh2, oa, wbin_g, wkv, g_kv, b_pre_norm, g_apost, tm)
    where = jnp.stack([core, chip]).astype(jnp.int32)
    dw_outa = _dw(ya, doa, D, tmw, "dw_a_out").reshape(N_CHIPS, D // 4, D)
    dw_kv = _dw(nk, dkv, 2 * KV_W, tmw, "dw_kv").reshape(N_CHIPS, D // 4, 2 * KV_W)
    dw_bin = _dw(nb, dqz, 512, tmw, "dw_b_in")
    dw_outb = _dw(ob, dy2, D, tmw, "dw_b_out").reshape(N_CHIPS, D // 4, D)
    grads1 = [dw_outa, dw_kv, dw_bin, dw_outb]
    names1 = ["a_w_out", "w_kv", "b_w_in", "b_w_out"]
    from_sibling1, _, _ = _sibling_exchange("to_sibling_1", to_sibling=grads1)
    sums1 = [_add_sibling(where, g, r, "add_sibling_" + nm) for g, r, nm in zip(grads1, from_sibling1, names1)]
    dproj, dconv_w, from_chips1 = _a_bwd(doa, proj, conv_w, wouta, tm_mix, [t for t, _ in sums1])
    shards1 = [_add_chips(where, own, r, "add_chips_" + nm) for (_, own), r, nm in zip(sums1, from_chips1, names1)]
    dw_in = _dw(n1, dproj, D, tmw, "dw_a_in")
    from_sibling2, (g_wouta, g_wkv, g_wbin, g_woutb), _ = _sibling_exchange(
        "to_sibling_2", to_sibling=[dw_in], shards=shards1)
    part2, own2 = _add_sibling(where, dw_in, from_sibling2[0], "add_sibling_a_w_in")
    nt = seq // tm
    nt_first = max(nt - max(nt // 4, 1), 1)
    gx_first, dg_first, from_chips2 = _a_in_bwd(dproj, xs, dh1, win_g, g_apre, tm, 0, nt_first, "a_in_bwd_first",
                                                parts=[part2])
    grad_x, dg_apre, _ = _a_in_bwd(dproj, xs, dh1, win_g, g_apre, tm, nt_first, nt - nt_first, "a_in_bwd_rest",
                                   before=(gx_first, dg_first))
    shard2 = _add_chips(where, own2, from_chips2[0], "add_chips_a_w_in")
    drel = _bias_fold(dtab)

    smalls = jnp.concatenate([
        dg_apre[0:1], dconv_w[0:3], dg_b[2:3], dg_b[0:1], dg_b[1:2], dg_bpost[0:1],
        _pad_cols(drel[:, 0:N_HEADS].reshape(1, N_BUCKETS * N_HEADS), D), _pad_cols(dsink[0:1], D),
        _pad_cols(loss_acc[0:1], D), jnp.zeros((SMALL_ROWS - 11, D), F32)], axis=0)
    _, (g_win,), small_all = _sibling_exchange("share_last", shards=[shard2], smalls=smalls)
    tot = _sum_smalls(small_all)

    big = {}
    for nm, g, w, m, v in [("a_w_in", g_win, a_w_in, m_a_w_in, v_a_w_in), ("a_w_out", g_wouta, a_w_out, m_a_w_out, v_a_w_out),
                           ("w_kv", g_wkv, w_kv, m_w_kv, v_w_kv), ("b_w_in", g_wbin, b_w_in, m_b_w_in, v_b_w_in),
                           ("b_w_out", g_woutb, b_w_out, m_b_w_out, v_b_w_out)]:
        shp = w.shape
        two = (shp[-2], shp[-1])
        d, nm_, nv_ = _adamw(g, w.reshape(two), m.reshape(two), v.reshape(two), "adamw_" + nm)
        big[nm] = (g.reshape(shp), d.reshape(shp), nm_.reshape(shp), nv_.reshape(shp))

    col0 = chip * (D // 4)
    sharded = lax.dynamic_slice(tot, (0, col0), (8, D // 4))
    g_shard = sharded
    w_shard = small_w
    m_shard = _pad_rows(jnp.concatenate([m_a_pre_norm, m_a_conv_w[0], m_a_post_norm], axis=0), 8)
    v_shard = _pad_rows(jnp.concatenate([v_a_pre_norm, v_a_conv_w[0], v_a_post_norm], axis=0), 8)
    ds_, ms_, vs_ = _adamw(g_shard, w_shard, m_shard, v_shard, "adamw_small_sharded")

    def rep_pack(kvn, bpre, bpost, rel, snk):
        rows = [kvn.reshape(1, D), bpre.reshape(1, D), bpost.reshape(1, D),
                _pad_cols(rel.reshape(1, N_BUCKETS * N_HEADS), D), _pad_cols(snk.reshape(1, N_HEADS), D)]
        return jnp.concatenate(rows + [jnp.zeros((3, D), F32)], axis=0)

    g_rep = tot[5:13]
    w_rep = rep_pack(kv_norm, b_pre_norm, b_post_norm, rel_bias, b_sinks)
    m_rep = rep_pack(m_kv_norm, m_b_pre_norm, m_b_post_norm, m_rel_bias, m_b_sinks)
    v_rep = rep_pack(v_kv_norm, v_b_pre_norm, v_b_post_norm, v_rel_bias, v_b_sinks)
    dr_, mr_, vr_ = _adamw(g_rep, w_rep, m_rep, v_rep, "adamw_small_replicated")

    def unshard(p):
        return {"a_pre_norm": p[0:1], "a_conv_w": p[1:4].reshape(1, 3, D // 4), "a_post_norm": p[4:5]}

    def unrep(p):
        return {"kv_norm": p[0], "b_pre_norm": p[1:2], "b_post_norm": p[2:3],
                "rel_bias": p[3, 0:N_BUCKETS * N_HEADS].reshape(N_BUCKETS, N_HEADS), "b_sinks": p[4:5, 0:N_HEADS]}

    order = ["a_pre_norm", "a_w_in", "a_conv_w", "a_w_out", "a_post_norm", "kv_norm", "w_kv", "rel_bias",
             "b_pre_norm", "b_w_in", "b_sinks", "b_w_out", "b_post_norm"]
    outs = []
    for which, sh, rp in [(0, g_shard, g_rep), (1, ds_, dr_), (2, ms_, mr_), (3, vs_, vr_)]:
        small = {**unshard(sh), **unrep(rp)}
        for nm in order:
            outs.append(big[nm][which] if nm in big else small[nm])
    loss = 0.5 * tot[10, 0]
    return (loss, grad_x.reshape(x.shape), *outs)
```

```python
import functools
import math

import jax
import jax.numpy as jnp
from jax import lax
from jax.experimental import pallas as pl
from jax.experimental.pallas import tpu as pltpu

F32 = jnp.float32
BF16 = jnp.bfloat16
MESH = pl.DeviceIdType.MESH
SDS = jax.ShapeDtypeStruct

D = 1024
HEAD_DIM = 64
N_HEADS = 16
N_KV = 2
GROUP = 8
KV_W = 128
BLK = 128
N_BUCKETS = 32
MAX_EXACT = 16
MAX_DISTANCE = 128
EPS = 1e-6
NEG_INF = -1e30
Q_SCALE = HEAD_DIM ** -0.5

ADAM_LR = 0.001
ADAM_B1 = 0.9
ADAM_B2 = 0.999
ADAM_EPS = 1e-08
ADAM_WD = 0.01
ADAM_STEP = 10

N_CHIPS = 4
N_DEV = 8
VMEM_LIMIT = 56 * 1024 * 1024
SMALL_ROWS = 16
HALO = 16


def _bucket_thresholds():
    def bucket(d):
        big = MAX_EXACT + int(math.log(d / MAX_EXACT) / math.log(MAX_DISTANCE / MAX_EXACT)
                              * (N_BUCKETS - MAX_EXACT))
        return d if d < MAX_EXACT else min(big, N_BUCKETS - 1)
    out = []
    for b in range(MAX_EXACT + 1, N_BUCKETS):
        out.append(min(d for d in range(MAX_EXACT, MAX_DISTANCE) if bucket(d) >= b))
    return tuple(out)


BUCKET_THRESHOLDS = _bucket_thresholds()


def _params(semantics=None, vmem=VMEM_LIMIT):
    return pltpu.CompilerParams(dimension_semantics=semantics, vmem_limit_bytes=vmem)


def _tile(n, pref):
    return pref if n >= 2 * pref else max(n // 2, 8)


def _rms_scale(v):
    return lax.rsqrt(jnp.mean(v * v, axis=-1, keepdims=True) + EPS)


def _nt(a, b):
    return lax.dot_general(a, b, (((1,), (1,)), ((), ())), preferred_element_type=F32)


def _tn(a, b):
    return lax.dot_general(a, b, (((0,), (0,)), ((), ())), preferred_element_type=F32)


def _nn(a, b):
    return jnp.dot(a, b, preferred_element_type=F32)


def _silu_parts(z):
    sg = jax.nn.sigmoid(z)
    return sg, z * sg


def _dsilu(z, sg):
    return sg * (1.0 + z * (1.0 - sg))


def _acc_row(ref, row, val):
    ref[row:row + 1, :] += val


def _gather_weights(shards, small):
    n = len(shards)

    def body(*refs):
        ins, small_in = refs[:n], refs[n]
        outs, small_out = refs[n + 1:2 * n + 1], refs[2 * n + 1]
        ici_send, ici_recv, d2d_send, d2d_recv = refs[2 * n + 2:]
        x, y, c = lax.axis_index("x"), lax.axis_index("y"), lax.axis_index("c")
        k = 2 * x + y
        chips = [(x, 1 - y), (1 - x, y), (1 - x, 1 - y)]
        for i_ref, o_ref in zip(ins, outs):
            o_ref[k] = i_ref[...].astype(BF16)
        small_out[k] = small_in[...]

        def half(o_ref, chip, core):
            h = o_ref.shape[1] // 2
            return o_ref.at[chip, pl.ds(pl.multiple_of(core * h, 16), h)]

        sends = []
        for a, o_ref in enumerate(list(outs) + [small_out]):
            split = a < n
            for j, (px, py) in enumerate(chips):
                src = half(o_ref, k, c) if split else o_ref.at[k]
                cp = pltpu.make_async_remote_copy(
                    src_ref=src, dst_ref=src, send_sem=ici_send.at[3 * a + j],
                    recv_sem=ici_recv.at[3 * a + j], device_id=(px, py, c), device_id_type=MESH)
                cp.start()
                sends.append(cp)
        for a, o_ref in enumerate(list(outs) + [small_out]):
            split = a < n
            for j, (px, py) in enumerate(chips):
                kj = 2 * px + py
                got = half(o_ref, kj, c) if split else o_ref.at[kj]
                pltpu.make_async_remote_copy(
                    src_ref=got, dst_ref=got, send_sem=ici_send.at[3 * a + j],
                    recv_sem=ici_recv.at[3 * a + j], device_id=(px, py, c),
                    device_id_type=MESH).wait_recv()
                if split:
                    fw = pltpu.make_async_remote_copy(
                        src_ref=got, dst_ref=got, send_sem=d2d_send.at[3 * a + j],
                        recv_sem=d2d_recv.at[3 * a + j], device_id=(x, y, 1 - c),
                        device_id_type=MESH)
                    fw.start()
                    sends.append(fw)
        for a, o_ref in enumerate(outs):
            for j, (px, py) in enumerate(chips):
                other = half(o_ref, 2 * px + py, 1 - c)
                pltpu.make_async_remote_copy(
                    src_ref=other, dst_ref=other, send_sem=d2d_send.at[3 * a + j],
                    recv_sem=d2d_recv.at[3 * a + j], device_id=(x, y, 1 - c),
                    device_id_type=MESH).wait_recv()
        for cp in sends:
            cp.wait_send()

    vm = pl.BlockSpec(memory_space=pltpu.VMEM)
    out_shape = [SDS((N_CHIPS,) + s.shape, BF16) for s in shards] + [SDS((N_CHIPS,) + small.shape, F32)]
    return pl.pallas_call(
        body, name="gather_weights", out_shape=out_shape,
        in_specs=[vm] * (n + 1), out_specs=[vm] * (n + 1),
        scratch_shapes=[pltpu.SemaphoreType.DMA((3 * (n + 1),)), pltpu.SemaphoreType.DMA((3 * (n + 1),)),
                        pltpu.SemaphoreType.DMA((3 * n,)), pltpu.SemaphoreType.DMA((3 * n,))],
        compiler_params=pltpu.CompilerParams(vmem_limit_bytes=VMEM_LIMIT),
    )(*shards, small)


def _a_in(x, g_pre, win_g, tm):
    s = x.shape[0]

    def body(x_ref, g_ref, w_ref, proj_ref, n1_ref):
        xv = x_ref[...]
        n1 = (xv * _rms_scale(xv) * g_ref[...]).astype(BF16)
        n1_ref[...] = n1
        for j in range(4):
            proj_ref[:, D * j:D * (j + 1)] = _nn(n1, w_ref[j]).astype(BF16)

    row = lambda i: (i, 0)
    return pl.pallas_call(
        body, name="a_in", grid=(s // tm,),
        in_specs=[pl.BlockSpec((tm, D), row), pl.BlockSpec((1, D), lambda i: (0, 0)),
                  pl.BlockSpec((4, D, D), lambda i: (0, 0, 0))],
        out_specs=[pl.BlockSpec((tm, 4 * D), row), pl.BlockSpec((tm, D), row)],
        out_shape=[SDS((s, 4 * D), BF16), SDS((s, D), BF16)],
        compiler_params=_params(("parallel",)),
    )(x, g_pre, win_g)


def _shift_rows(v, last, second_last, rows):
    v1 = jnp.where(rows >= 1, pltpu.roll(v, 1, 0), last)
    v2 = jnp.where(rows >= 2, pltpu.roll(v, 2, 0), jnp.where(rows == 1, last, second_last))
    return v1, v2


def _a_mix(proj, x, conv_w, w_out, g_post, tm):
    s = x.shape[0]

    def body(proj_ref, x_ref, cw_ref, w_ref, g_ref, ya_ref, oa_ref, h1_ref, carry):
        @pl.when(pl.program_id(0) == 0)
        def _():
            carry[...] = jnp.zeros_like(carry)
        v = proj_ref[:, D:2 * D].astype(F32) * proj_ref[:, 2 * D:3 * D].astype(F32)
        rows = lax.broadcasted_iota(jnp.int32, (tm, D), 0)
        before = carry[...]
        v1, v2 = _shift_rows(v, before[7:8, :], before[6:7, :], rows)
        carry[...] = v[tm - 8:tm, :]
        conv = cw_ref[0:1, :] * v2 + cw_ref[1:2, :] * v1 + cw_ref[2:3, :] * v
        _, sz = _silu_parts(proj_ref[:, 3 * D:4 * D].astype(F32))
        ya = (proj_ref[:, 0:D].astype(F32) * conv * sz).astype(BF16)
        ya_ref[...] = ya
        oa = _nn(ya, w_ref[...])
        oa_ref[...] = oa
        h1_ref[...] = x_ref[...] + oa * _rms_scale(oa) * g_ref[...]

    row = lambda i: (i, 0)
    fix = lambda i: (0, 0)
    return pl.pallas_call(
        body, name="a_mix", grid=(s // tm,),
        in_specs=[pl.BlockSpec((tm, 4 * D), row), pl.BlockSpec((tm, D), row), pl.BlockSpec((8, D), fix),
                  pl.BlockSpec((D, D), fix), pl.BlockSpec((1, D), fix)],
        out_specs=[pl.BlockSpec((tm, D), row)] * 3,
        out_shape=[SDS((s, D), BF16), SDS((s, D), F32), SDS((s, D), F32)],
        scratch_shapes=[pltpu.VMEM((8, D), F32)],
        compiler_params=_params(("arbitrary",)),
    )(proj, x, conv_w, w_out, g_post)


def _b_in(h1, g_kv, g_pre, w_kv, wbin_g, tm):
    s = h1.shape[0]

    def body(h_ref, gk_ref, gb_ref, wkv_ref, wb_ref, nk_ref, nb_ref, kv_ref, q_ref, z_ref):
        h = h_ref[...]
        hh = h * _rms_scale(h)
        nk = (hh * gk_ref[...]).astype(BF16)
        nb = (hh * gb_ref[...]).astype(BF16)
        nk_ref[...] = nk
        nb_ref[...] = nb
        kv_ref[...] = _nn(nk, wkv_ref[...]).astype(BF16)
        for j in range(2):
            q_ref[:, 512 * j:512 * (j + 1)] = (_nn(nb, wb_ref[j]) * Q_SCALE).astype(BF16)
            z_ref[:, 512 * j:512 * (j + 1)] = _nn(nb, wb_ref[2 + j])

    row = lambda i: (i, 0)
    fix = lambda i: (0, 0)
    return pl.pallas_call(
        body, name="b_in", grid=(s // tm,),
        in_specs=[pl.BlockSpec((tm, D), row), pl.BlockSpec((1, D), fix), pl.BlockSpec((1, D), fix),
                  pl.BlockSpec((D, 2 * KV_W), fix), pl.BlockSpec((4, D, 512), lambda i: (0, 0, 0))],
        out_specs=[pl.BlockSpec((tm, D), row), pl.BlockSpec((tm, D), row), pl.BlockSpec((tm, 2 * KV_W), row),
                   pl.BlockSpec((tm, D), row), pl.BlockSpec((tm, D), row)],
        out_shape=[SDS((s, D), BF16), SDS((s, D), BF16), SDS((s, 2 * KV_W), BF16), SDS((s, D), BF16),
                   SDS((s, D), F32)],
        compiler_params=_params(("parallel",)),
    )(h1, g_kv, g_pre, w_kv, wbin_g)


def _band_buckets():
    q = lax.broadcasted_iota(jnp.int32, (BLK, 2 * BLK), 0)
    k = lax.broadcasted_iota(jnp.int32, (BLK, 2 * BLK), 1)
    dist = q + BLK - k
    bucket = jnp.where(dist < MAX_EXACT, dist, MAX_EXACT)
    for t in BUCKET_THRESHOLDS:
        bucket = bucket + jnp.where(dist >= t, 1, 0)
    in_window = (dist >= 0) & (dist < BLK)
    return jnp.where(in_window, bucket, -1)


def _head_place(h):
    kh, j, e = h // GROUP, (h % GROUP) // 2, h % 2
    return kh, slice(BLK * j, BLK * (j + 1)), slice(2 * BLK * e, 2 * BLK * (e + 1))


def _bias_table(rel_bias):
    def body(rb_ref, tab_ref):
        bucket = _band_buckets()
        for h in range(N_HEADS):
            acc = jnp.where(bucket < 0, NEG_INF, 0.0).astype(F32)
            for b in range(N_BUCKETS):
                acc = jnp.where(bucket == b, rb_ref[b, h], acc)
            kh, rows, cols = _head_place(h)
            tab_ref[kh, rows, cols] = acc

    return pl.pallas_call(
        body, name="bias_table", out_shape=SDS((N_KV, 4 * BLK, 4 * BLK), F32),
        in_specs=[pl.BlockSpec(memory_space=pltpu.SMEM)],
        out_specs=pl.BlockSpec(memory_space=pltpu.VMEM),
    )(rel_bias)


def _bias_fold(dtab):
    def body(dtab_ref, out_ref):
        bucket = _band_buckets()
        row = lax.broadcasted_iota(jnp.int32, (N_BUCKETS, 128), 0)
        lane = lax.broadcasted_iota(jnp.int32, (N_BUCKETS, 128), 1)
        acc = jnp.zeros((N_BUCKETS, 128), F32)
        for h in range(N_HEADS):
            kh, rows, cols = _head_place(h)
            dt = dtab_ref[kh, rows, cols]
            for b in range(N_BUCKETS):
                val = jnp.sum(jnp.where(bucket == b, dt, 0.0))
                acc = acc + jnp.where((row == b) & (lane == h), val, 0.0)
        out_ref[...] = acc

    return pl.pallas_call(
        body, name="bias_fold", out_shape=SDS((N_BUCKETS, 128), F32),
        in_specs=[pl.BlockSpec(memory_space=pltpu.VMEM)],
        out_specs=pl.BlockSpec(memory_space=pltpu.VMEM),
    )(dtab)


def _pair_operands(prev, cur):
    t = jnp.concatenate([prev, cur], axis=0).astype(F32)
    tr = pltpu.roll(t, HEAD_DIM, 1)
    lo = lax.broadcasted_iota(jnp.int32, t.shape, 1) < HEAD_DIM
    zero = jnp.zeros_like(t)
    head0 = jnp.concatenate([jnp.where(lo, t, zero), jnp.where(lo, zero, tr)], axis=0).astype(BF16)
    head1 = jnp.concatenate([jnp.where(lo, tr, zero), jnp.where(lo, zero, t)], axis=0).astype(BF16)
    return head0, head1


def _pair_fold(d0, d1):
    lo = lax.broadcasted_iota(jnp.int32, (2 * BLK, KV_W), 1) < HEAD_DIM
    zero = jnp.zeros((2 * BLK, KV_W), F32)
    g0 = jnp.where(lo, d0[0:256], zero) + pltpu.roll(jnp.where(lo, zero, d0[256:512]), HEAD_DIM, 1)
    g1 = pltpu.roll(jnp.where(lo, d1[0:256], zero), HEAD_DIM, 1) + jnp.where(lo, zero, d1[256:512])
    return g0 + g1


def _first_block_mask(n):
    col = lax.broadcasted_iota(jnp.int32, (4 * BLK, 2 * BLK), 1)
    return jnp.where((n == 0) & (col < BLK), NEG_INF, 0.0).astype(F32)


def _stack_pairs(ref, kh):
    return jnp.concatenate([ref[:, 128 * (4 * kh + j):128 * (4 * kh + j + 1)] for j in range(4)], axis=0)


def _per_pair(vals):
    grp = lax.broadcasted_iota(jnp.int32, (4 * BLK, 1), 0) // BLK
    col = jnp.full((4 * BLK, 1), vals[0], F32)
    for j in range(1, 4):
        col = jnp.where(grp == j, vals[j], col)
    return col


def _attn_fwd(q, kv, tab, sinks):
    s = q.shape[0]

    def body(sink_ref, q_ref, kp_ref, kc_ref, vp_ref, vc_ref, tab_ref, att_ref, stats_ref):
        n = pl.program_id(0)
        k2 = _pair_operands(kp_ref[...], kc_ref[...])
        v2 = _pair_operands(vp_ref[...], vc_ref[...])
        first = _first_block_mask(n)
        lane = lax.broadcasted_iota(jnp.int32, (BLK, 128), 1)
        stats = jnp.zeros((BLK, 128), F32)
        for kh in range(N_KV):
            sc = _nt(_stack_pairs(q_ref, kh), k2[kh])
            ps = []
            for e in range(2):
                heads = [GROUP * kh + 2 * j + e for j in range(4)]
                sink = _per_pair([sink_ref[h] for h in heads])
                lg = sc[:, 256 * e:256 * (e + 1)] + tab_ref[kh, :, 256 * e:256 * (e + 1)] + first
                m = jnp.maximum(jnp.max(lg, axis=-1, keepdims=True), sink)
                ex = jnp.exp(lg - m)
                den = jnp.sum(ex, axis=-1, keepdims=True) + jnp.exp(sink - m)
                ps.append(ex * (1.0 / den))
                lse = m + jnp.log(den)
                for j, h in enumerate(heads):
                    stats = jnp.where(lane == h, lse[BLK * j:BLK * (j + 1)], stats)
            out = _nn(jnp.concatenate(ps, axis=1).astype(BF16), v2[kh])
            for j in range(4):
                att_ref[:, 128 * (4 * kh + j):128 * (4 * kh + j + 1)] = out[BLK * j:BLK * (j + 1)]
        stats_ref[...] = stats

    cur = lambda n: (n, 0)
    prev = lambda n: (jnp.maximum(n - 1, 0), 0)
    return pl.pallas_call(
        body, name="attn_fwd", grid=(s // BLK,),
        in_specs=[pl.BlockSpec(memory_space=pltpu.SMEM), pl.BlockSpec((BLK, D), cur),
                  pl.BlockSpec((BLK, KV_W), prev), pl.BlockSpec((BLK, KV_W), cur),
                  pl.BlockSpec((BLK, KV_W), lambda n: (jnp.maximum(n - 1, 0), 1)),
                  pl.BlockSpec((BLK, KV_W), lambda n: (n, 1)),
                  pl.BlockSpec((N_KV, 4 * BLK, 4 * BLK), lambda n: (0, 0, 0))],
        out_specs=[pl.BlockSpec((BLK, D), cur), pl.BlockSpec((BLK, 128), cur)],
        out_shape=[SDS((s, D), F32), SDS((s, 128), F32)],
        compiler_params=_params(("parallel",)),
    )(sinks, q, kv, kv, kv, kv, tab)


def _mid(att, zb, h1, tgt, w_out, g_post, tm):
    s = att.shape[0]

    def body(att_ref, z_ref, h1_ref, t_ref, w_ref, g_ref,
             ob_ref, dy_ref, dh_ref, dqz_ref, datt_ref, loss_ref, dg_ref):
        @pl.when(pl.program_id(0) == 0)
        def _():
            loss_ref[...] = jnp.zeros_like(loss_ref)
            dg_ref[...] = jnp.zeros_like(dg_ref)
        att = att_ref[...]
        z = z_ref[...]
        sg, sz = _silu_parts(z)
        ob = (att * sz).astype(BF16)
        ob_ref[...] = ob
        y2 = _nn(ob, w_ref[...])
        r2 = _rms_scale(y2)
        yh = y2 * r2
        g = g_ref[...]
        err = (h1_ref[...] + yh * g) - t_ref[...]
        loss_ref[...] += jnp.sum(jnp.sum(err * err, axis=-1, keepdims=True) / D)
        dh = err / D
        dh_ref[...] = dh
        _acc_row(dg_ref, 0, jnp.sum(dh * yh, axis=0, keepdims=True))
        dyh = dh * g
        dy = (r2 * (dyh - yh * jnp.mean(dyh * yh, axis=-1, keepdims=True))).astype(BF16)
        dy_ref[...] = dy
        dob = _nt(dy, w_ref[...])
        datt_ref[...] = (dob * sz).astype(BF16)
        dqz_ref[...] = (dob * att * _dsilu(z, sg)).astype(BF16)

    row = lambda i: (i, 0)
    fix = lambda i: (0, 0)
    return pl.pallas_call(
        body, name="mid", grid=(s // tm,),
        in_specs=[pl.BlockSpec((tm, D), row)] * 4 + [pl.BlockSpec((D, D), fix), pl.BlockSpec((1, D), fix)],
        out_specs=[pl.BlockSpec((tm, D), row), pl.BlockSpec((tm, D), row), pl.BlockSpec((tm, D), row),
                   pl.BlockSpec((tm, D), lambda i: (i, 1)), pl.BlockSpec((tm, D), row),
                   pl.BlockSpec((8, 128), fix), pl.BlockSpec((8, D), fix)],
        out_shape=[SDS((s, D), BF16), SDS((s, D), BF16), SDS((s, D), F32), SDS((s, 2 * D), BF16),
                   SDS((s, D), BF16), SDS((8, 128), F32), SDS((8, D), F32)],
        compiler_params=_params(("arbitrary",)),
    )(att, zb, h1, tgt, w_out, g_post)


def _attn_bwd(q, kv, datt, stats, tab, sinks, dqz):
    s = q.shape[0]
    nb = s // BLK

    def body(sink_ref, q_ref, kp_ref, kc_ref, vp_ref, vc_ref, da_ref, st_ref, tab_ref, dqz_in,
             dq_ref, dkv_ref, dtab_ref, dsink_ref, dk_carry, dv_carry):
        del dqz_in
        n = pl.program_id(0)

        @pl.when(n == 0)
        def _():
            dtab_ref[...] = jnp.zeros_like(dtab_ref)
            dsink_ref[...] = jnp.zeros_like(dsink_ref)
            dk_carry[...] = jnp.zeros_like(dk_carry)
            dv_carry[...] = jnp.zeros_like(dv_carry)

        @pl.when(n < nb)
        def _():
            k2 = _pair_operands(kp_ref[...], kc_ref[...])
            v2 = _pair_operands(vp_ref[...], vc_ref[...])
            first = _first_block_mask(n)
            lane = lax.broadcasted_iota(jnp.int32, (BLK, 128), 1)
            lane8 = lax.broadcasted_iota(jnp.int32, (8, 128), 1)
            stats = st_ref[...]
            dk2, dv2 = [], []
            dsink = jnp.zeros((8, 128), F32)
            for kh in range(N_KV):
                qs = _stack_pairs(q_ref, kh)
                das = _stack_pairs(da_ref, kh)
                sc = _nt(qs, k2[kh])
                dp = _nt(das, v2[kh])
                ps, dss = [], []
                for e in range(2):
                    heads = [GROUP * kh + 2 * j + e for j in range(4)]
                    lse = jnp.concatenate([jnp.sum(jnp.where(lane == h, stats, 0.0), axis=-1, keepdims=True)
                                           for h in heads], axis=0)
                    cols = slice(256 * e, 256 * (e + 1))
                    p = jnp.exp(sc[:, cols] + tab_ref[kh, :, cols] + first - lse)
                    delta = jnp.sum(p * dp[:, cols], axis=-1, keepdims=True)
                    ds = p * (dp[:, cols] - delta)
                    dtab_ref[kh, :, cols] += ds
                    to_sink = jnp.exp(_per_pair([sink_ref[h] for h in heads]) - lse) * delta
                    for j, h in enumerate(heads):
                        dsink = dsink - jnp.where(lane8 == h, jnp.sum(to_sink[BLK * j:BLK * (j + 1)]), 0.0)
                    ps.append(p)
                    dss.append(ds)
                p2 = jnp.concatenate(ps, axis=1).astype(BF16)
                ds2 = jnp.concatenate(dss, axis=1).astype(BF16)
                dq = _nn(ds2, k2[kh]) * Q_SCALE
                for j in range(4):
                    dq_ref[:, 128 * (4 * kh + j):128 * (4 * kh + j + 1)] = dq[BLK * j:BLK * (j + 1)].astype(BF16)
                dk2.append(_tn(ds2, qs))
                dv2.append(_tn(p2, das))
            dsink_ref[...] += dsink
            dkk = _pair_fold(dk2[0], dk2[1])
            dvv = _pair_fold(dv2[0], dv2[1])
            dkv_ref[:, 0:KV_W] = (dk_carry[...] + dkk[0:BLK]).astype(BF16)
            dkv_ref[:, KV_W:2 * KV_W] = (dv_carry[...] + dvv[0:BLK]).astype(BF16)
            dk_carry[...] = dkk[BLK:2 * BLK]
            dv_carry[...] = dvv[BLK:2 * BLK]

        @pl.when(n == nb)
        def _():
            dkv_ref[:, 0:KV_W] = dk_carry[...].astype(BF16)
            dkv_ref[:, KV_W:2 * KV_W] = dv_carry[...].astype(BF16)

    cur = lambda n: (jnp.minimum(n, nb - 1), 0)
    prev = lambda n: (jnp.clip(n - 1, 0, nb - 1), 0)
    return pl.pallas_call(
        body, name="attn_bwd", grid=(nb + 1,),
        in_specs=[pl.BlockSpec(memory_space=pltpu.SMEM), pl.BlockSpec((BLK, D), cur),
                  pl.BlockSpec((BLK, KV_W), prev), pl.BlockSpec((BLK, KV_W), cur),
                  pl.BlockSpec((BLK, KV_W), lambda n: (jnp.clip(n - 1, 0, nb - 1), 1)),
                  pl.BlockSpec((BLK, KV_W), lambda n: (jnp.minimum(n, nb - 1), 1)),
                  pl.BlockSpec((BLK, D), cur), pl.BlockSpec((BLK, 128), cur),
                  pl.BlockSpec((N_KV, 4 * BLK, 4 * BLK), lambda n: (0, 0, 0)),
                  pl.BlockSpec(memory_space=pl.ANY)],
        out_specs=[pl.BlockSpec((BLK, D), cur), pl.BlockSpec((BLK, 2 * KV_W), prev),
                   pl.BlockSpec((N_KV, 4 * BLK, 4 * BLK), lambda n: (0, 0, 0)),
                   pl.BlockSpec((8, 128), lambda n: (0, 0))],
        out_shape=[SDS((s, 2 * D), BF16), SDS((s, 2 * KV_W), BF16), SDS((N_KV, 4 * BLK, 4 * BLK), F32),
                   SDS((8, 128), F32)],
        scratch_shapes=[pltpu.VMEM((BLK, KV_W), F32), pltpu.VMEM((BLK, KV_W), F32)],
        input_output_aliases={9: 0},
        compiler_params=_params(("arbitrary",)),
    )(sinks, q, kv, kv, kv, kv, datt, stats, tab, dqz)


def _b_bwd(dqz, dkv, h1, dh2, oa, wbin_g, w_kv, g_kv, g_pre, g_apost, tm):
    s = h1.shape[0]

    def body(dqz_ref, dkv_ref, h_ref, dh2_ref, oa_ref, wb_ref, wkv_ref, gk_ref, gb_ref, ga_ref,
             dh1_ref, doa_ref, dg_ref):
        @pl.when(pl.program_id(0) == 0)
        def _():
            dg_ref[...] = jnp.zeros_like(dg_ref)
        dnb = _nt(dqz_ref[:, 0:512], wb_ref[0])
        for j in range(1, 4):
            dnb = dnb + _nt(dqz_ref[:, 512 * j:512 * (j + 1)], wb_ref[j])
        dnk = _nt(dkv_ref[...], wkv_ref[...])
        h = h_ref[...]
        r = _rms_scale(h)
        hh = h * r
        _acc_row(dg_ref, 0, jnp.sum(dnk * hh, axis=0, keepdims=True))
        _acc_row(dg_ref, 1, jnp.sum(dnb * hh, axis=0, keepdims=True))
        dhh = dnb * gb_ref[...] + dnk * gk_ref[...]
        dh1 = dh2_ref[...] + r * (dhh - hh * jnp.mean(dhh * hh, axis=-1, keepdims=True))
        dh1_ref[...] = dh1
        oa = oa_ref[...]
        ra = _rms_scale(oa)
        oh = oa * ra
        _acc_row(dg_ref, 2, jnp.sum(dh1 * oh, axis=0, keepdims=True))
        doh = dh1 * ga_ref[...]
        doa_ref[...] = (ra * (doh - oh * jnp.mean(doh * oh, axis=-1, keepdims=True))).astype(BF16)

    row = lambda i: (i, 0)
    fix = lambda i: (0, 0)
    return pl.pallas_call(
        body, name="b_bwd", grid=(s // tm,),
        in_specs=[pl.BlockSpec((tm, 2 * D), row), pl.BlockSpec((tm, 2 * KV_W), row), pl.BlockSpec((tm, D), row),
                  pl.BlockSpec((tm, D), row), pl.BlockSpec((tm, D), row),
                  pl.BlockSpec((4, D, 512), lambda i: (0, 0, 0)), pl.BlockSpec((D, 2 * KV_W), fix),
                  pl.BlockSpec((1, D), fix), pl.BlockSpec((1, D), fix), pl.BlockSpec((1, D), fix)],
        out_specs=[pl.BlockSpec((tm, D), row), pl.BlockSpec((tm, D), row), pl.BlockSpec((8, D), fix)],
        out_shape=[SDS((s, D), F32), SDS((s, D), BF16), SDS((8, D), F32)],
        compiler_params=_params(("arbitrary",)),
    )(dqz, dkv, h1, dh2, oa, wbin_g, w_kv, g_kv, g_pre, g_apost)


def _chip_exchange(parts, recvs, send, recv):
    x, y, c = lax.axis_index("x"), lax.axis_index("y"), lax.axis_index("c")
    chips = [(x, 1 - y), (1 - x, y), (1 - x, 1 - y)]
    copies = []
    for a, (t, r) in enumerate(zip(parts, recvs)):
        for j, (px, py) in enumerate(chips):
            copies.append(pltpu.make_async_remote_copy(
                src_ref=t.at[2 * px + py], dst_ref=r.at[j], send_sem=send.at[3 * a + j],
                recv_sem=recv.at[3 * a + j], device_id=(px, py, c), device_id_type=MESH))
    return copies


def _exchange_specs(parts):
    anyspace = pl.BlockSpec(memory_space=pl.ANY)
    n = len(parts)
    return ([anyspace] * n, [anyspace] * n, [SDS((3,) + t.shape[1:], t.dtype) for t in parts],
            [pltpu.SemaphoreType.DMA((3 * n,)), pltpu.SemaphoreType.DMA((3 * n,))])


def _a_bwd(doa, proj, conv_w, w_out, tm, parts):
    s = doa.shape[0]
    nt = s // tm
    n = len(parts)
    ex_in, ex_out, ex_shape, ex_sems = _exchange_specs(parts)

    def body(*refs):
        doa_ref, proj_ref, halo_ref, cw_ref, w_ref = refs[:5]
        part_refs = refs[5:5 + n]
        dproj_ref, dcw_ref = refs[5 + n:7 + n]
        recv_refs = refs[7 + n:7 + 2 * n]
        carry, send, recv = refs[7 + 2 * n:]
        i = pl.program_id(0)
        r = nt - 1 - i

        @pl.when(i == 0)
        def _():
            dcw_ref[...] = jnp.zeros_like(dcw_ref)
            carry[...] = jnp.zeros_like(carry)
            for cp in _chip_exchange(part_refs, recv_refs, send, recv):
                cp.start()
        dya = _nt(doa_ref[...], w_ref[...])
        bg = proj_ref[:, 0:D].astype(F32)
        cg = proj_ref[:, D:2 * D].astype(F32)
        u = proj_ref[:, 2 * D:3 * D].astype(F32)
        z = proj_ref[:, 3 * D:4 * D].astype(F32)
        v = cg * u
        before = jnp.where(r > 0, halo_ref[:, D:2 * D].astype(F32) * halo_ref[:, 2 * D:3 * D].astype(F32), 0.0)
        rows = lax.broadcasted_iota(jnp.int32, (tm, D), 0)
        v1, v2 = _shift_rows(v, before[HALO - 1:HALO, :], before[HALO - 2:HALO - 1, :], rows)
        conv = cw_ref[0:1, :] * v2 + cw_ref[1:2, :] * v1 + cw_ref[2:3, :] * v
        sg, sz = _silu_parts(z)
        dproj_ref[:, 0:D] = (dya * conv * sz).astype(BF16)
        dproj_ref[:, 3 * D:4 * D] = (dya * bg * conv * _dsilu(z, sg)).astype(BF16)
        dconv = dya * bg * sz
        _acc_row(dcw_ref, 0, jnp.sum(dconv * v2, axis=0, keepdims=True))
        _acc_row(dcw_ref, 1, jnp.sum(dconv * v1, axis=0, keepdims=True))
        _acc_row(dcw_ref, 2, jnp.sum(dconv * v, axis=0, keepdims=True))
        after = carry[...]
        up1 = jnp.where(rows < tm - 1, pltpu.roll(dconv, tm - 1, 0), after[0:1, :])
        up2 = jnp.where(rows < tm - 2, pltpu.roll(dconv, tm - 2, 0),
                        jnp.where(rows == tm - 2, after[0:1, :], after[1:2, :]))
        carry[...] = dconv[0:8, :]
        dv = cw_ref[2:3, :] * dconv + cw_ref[1:2, :] * up1 + cw_ref[0:1, :] * up2
        dproj_ref[:, D:2 * D] = (dv * u).astype(BF16)
        dproj_ref[:, 2 * D:3 * D] = (dv * cg).astype(BF16)

        @pl.when(i == nt - 1)
        def _():
            for cp in _chip_exchange(part_refs, recv_refs, send, recv):
                cp.wait()

    rev = lambda i: (nt - 1 - i, 0)
    fix = lambda i: (0, 0)
    halo = lambda i: (jnp.maximum((nt - 1 - i) * (tm // HALO) - 1, 0), 0)
    dproj, dcw, *got = pl.pallas_call(
        body, name="a_bwd", grid=(nt,),
        in_specs=[pl.BlockSpec((tm, D), rev), pl.BlockSpec((tm, 4 * D), rev), pl.BlockSpec((HALO, 4 * D), halo),
                  pl.BlockSpec((8, D), fix), pl.BlockSpec((D, D), fix)] + ex_in,
        out_specs=[pl.BlockSpec((tm, 4 * D), rev), pl.BlockSpec((8, D), fix)] + ex_out,
        out_shape=[SDS((s, 4 * D), BF16), SDS((8, D), F32)] + ex_shape,
        scratch_shapes=[pltpu.VMEM((8, D), F32)] + ex_sems,
        compiler_params=_params(("arbitrary",)),
    )(doa, proj, proj, conv_w, w_out, *parts)
    return dproj, dcw, got


def _a_in_bwd(dproj, x, dh1, win_g, g_pre, tm, first, count, name, parts=(), before=None):
    s = x.shape[0]
    nt = count
    n = len(parts)
    ex_in, ex_out, ex_shape, ex_sems = _exchange_specs(parts) if n else ([], [], [], [])
    goes_on = before is not None

    def body(*refs):
        dp_ref, x_ref, dh_ref, w_ref, g_ref = refs[:5]
        part_refs = refs[5:5 + n]
        pos = 5 + n
        dg_before = refs[pos + 1] if goes_on else None
        pos += 2 * goes_on
        gx_ref, dg_ref = refs[pos:pos + 2]
        recv_refs = refs[pos + 2:pos + 2 + n]
        sems = refs[pos + 2 + n:]

        @pl.when(pl.program_id(0) == 0)
        def _():
            dg_ref[...] = dg_before[...] if goes_on else jnp.zeros_like(dg_ref)
            if n:
                for cp in _chip_exchange(part_refs, recv_refs, *sems):
                    cp.start()
        dn = _nt(dp_ref[:, 0:D], w_ref[0])
        for j in range(1, 4):
            dn = dn + _nt(dp_ref[:, D * j:D * (j + 1)], w_ref[j])
        xv = x_ref[...]
        r = _rms_scale(xv)
        xh = xv * r
        _acc_row(dg_ref, 0, jnp.sum(dn * xh, axis=0, keepdims=True))
        dxh = dn * g_ref[...]
        gx_ref[...] = dh_ref[...] + r * (dxh - xh * jnp.mean(dxh * xh, axis=-1, keepdims=True))

        if n:
            @pl.when(pl.program_id(0) == nt - 1)
            def _():
                for cp in _chip_exchange(part_refs, recv_refs, *sems):
                    cp.wait()

    row = lambda i: (first + i, 0)
    fix = lambda i: (0, 0)
    goes_on_in = [pl.BlockSpec(memory_space=pl.ANY), pl.BlockSpec((8, D), fix)] if goes_on else []
    gx, dg, *got = pl.pallas_call(
        body, name=name, grid=(nt,),
        in_specs=[pl.BlockSpec((tm, 4 * D), row), pl.BlockSpec((tm, D), row), pl.BlockSpec((tm, D), row),
                  pl.BlockSpec((4, D, D), lambda i: (0, 0, 0)), pl.BlockSpec((1, D), fix)] + ex_in + goes_on_in,
        out_specs=[pl.BlockSpec((tm, D), row), pl.BlockSpec((8, D), fix)] + ex_out,
        out_shape=[SDS((s, D), F32), SDS((8, D), F32)] + ex_shape,
        scratch_shapes=ex_sems,
        input_output_aliases={5 + n: 0} if goes_on else {},
        compiler_params=_params(("arbitrary",)),
    )(dproj, x, dh1, win_g, g_pre, *parts, *(before if goes_on else ()))
    return gx, dg, got


def _dw(a, b, tn, tmw, name):
    s, k = a.shape
    n = b.shape[1]

    def body(a_ref, b_ref, o_ref):
        @pl.when(pl.program_id(1) == 0)
        def _():
            o_ref[...] = jnp.zeros_like(o_ref)
        o_ref[0] += _tn(a_ref[...], b_ref[...])

    return pl.pallas_call(
        body, name=name, grid=(n // tn, s // tmw),
        in_specs=[pl.BlockSpec((tmw, k), lambda j, t: (t, 0)), pl.BlockSpec((tmw, tn), lambda j, t: (t, j))],
        out_specs=pl.BlockSpec((1, k, tn), lambda j, t: (j, 0, 0)),
        out_shape=SDS((n // tn, k, tn), F32),
        compiler_params=_params(("parallel", "arbitrary")),
    )(a, b)


def _sibling_exchange(name, to_sibling=(), shards=(), smalls=None):
    n_g, n_h = len(to_sibling), len(shards)
    has_small = smalls is not None

    def body(*refs):
        gs = refs[:n_g]
        pos = n_g + n_h
        small_in = refs[pos] if has_small else None
        pos += has_small
        rs, fs = refs[pos:pos + n_g], refs[pos + n_g:pos + n_g + n_h]
        pos += n_g + n_h
        small_all = refs[pos] if has_small else None
        pos += has_small
        dsend, drecv, ssend, srecv = refs[pos:]
        x, y, c = lax.axis_index("x"), lax.axis_index("y"), lax.axis_index("c")
        sibling = (x, y, 1 - c)
        sends, arrivals = [], []
        for a, (g, r) in enumerate(zip(gs, rs)):
            h = g.shape[1] // 2
            src = g.at[:, pl.ds(pl.multiple_of((1 - c) * h, 8), h), :]
            sends.append(pltpu.make_async_remote_copy(src_ref=src, dst_ref=r, send_sem=dsend.at[a], recv_sem=drecv.at[a],
                                                      device_id=sibling, device_id_type=MESH))
            arrivals.append(pltpu.make_async_remote_copy(src_ref=r, dst_ref=r, send_sem=dsend.at[a], recv_sem=drecv.at[a],
                                                         device_id=sibling, device_id_type=MESH))
        for b, full in enumerate(fs):
            h = full.shape[0] // 2
            mine = full.at[pl.ds(pl.multiple_of(c * h, 8), h)]
            theirs = full.at[pl.ds(pl.multiple_of((1 - c) * h, 8), h)]
            sends.append(pltpu.make_async_remote_copy(src_ref=mine, dst_ref=mine, send_sem=dsend.at[n_g + b],
                                                      recv_sem=drecv.at[n_g + b], device_id=sibling, device_id_type=MESH))
            arrivals.append(pltpu.make_async_remote_copy(src_ref=mine, dst_ref=theirs, send_sem=dsend.at[n_g + b],
                                                         recv_sem=drecv.at[n_g + b], device_id=sibling, device_id_type=MESH))
        if has_small:
            me = 4 * x + 2 * y + c
            small_all[me] = small_in[...]
            for rel in range(1, N_DEV):
                fx, fy, fc = rel >> 2, (rel >> 1) & 1, rel & 1
                peer = (x + fx - 2 * x * fx, y + fy - 2 * y * fy, c + fc - 2 * c * fc)
                sender = 4 * peer[0] + 2 * peer[1] + peer[2]
                sends.append(pltpu.make_async_remote_copy(
                    src_ref=small_in, dst_ref=small_all.at[me], send_sem=ssend.at[rel - 1], recv_sem=srecv.at[rel - 1],
                    device_id=peer, device_id_type=MESH))
                arrivals.append(pltpu.make_async_remote_copy(
                    src_ref=small_in, dst_ref=small_all.at[sender], send_sem=ssend.at[rel - 1], recv_sem=srecv.at[rel - 1],
                    device_id=peer, device_id_type=MESH))
        for cp in sends:
            cp.start()
        for cp in arrivals:
            cp.wait_recv()
        for cp in sends:
            cp.wait_send()

    anyspace = pl.BlockSpec(memory_space=pl.ANY)
    vm = pl.BlockSpec(memory_space=pltpu.VMEM)
    out_shape = [SDS((N_CHIPS, g.shape[1] // 2, g.shape[2]), F32) for g in to_sibling]
    out_shape += [SDS(full.shape, F32) for full in shards]
    if has_small:
        out_shape.append(SDS((N_DEV,) + smalls.shape, F32))
    n_d2d = max(n_g + n_h, 1)
    outs = pl.pallas_call(
        body, name=name, out_shape=out_shape,
        in_specs=[anyspace] * (n_g + n_h) + [vm] * has_small, out_specs=[anyspace] * (n_g + n_h) + [vm] * has_small,
        scratch_shapes=[pltpu.SemaphoreType.DMA((n_d2d,)), pltpu.SemaphoreType.DMA((n_d2d,)),
                        pltpu.SemaphoreType.DMA((N_DEV - 1,)), pltpu.SemaphoreType.DMA((N_DEV - 1,))],
        input_output_aliases={n_g + b: n_g + b for b in range(n_h)},
    )(*to_sibling, *shards, *([smalls] if has_small else []))
    return outs[:n_g], outs[n_g:n_g + n_h], (outs[n_g + n_h] if has_small else None)


def _add_sibling(where, g, r, name):
    _, rows, cols = g.shape
    h = rows // 2
    tr = min(h, 256)
    nh = h // tr

    def body(where_ref, g_ref, r_ref, t_ref, own_ref):
        t = g_ref[0] + r_ref[0]
        t_ref[0] = t.astype(BF16)

        @pl.when(pl.program_id(1) == where_ref[1])
        def _():
            own_ref[...] = t

    return pl.pallas_call(
        body, name=name,
        grid_spec=pltpu.PrefetchScalarGridSpec(
            num_scalar_prefetch=1, grid=(nh, N_CHIPS),
            in_specs=[pl.BlockSpec((1, tr, cols), lambda i, k, w: (k, w[0] * nh + i, 0)),
                      pl.BlockSpec((1, tr, cols), lambda i, k, w: (k, i, 0))],
            out_specs=[pl.BlockSpec((1, tr, cols), lambda i, k, w: (k, i, 0)),
                       pl.BlockSpec((tr, cols), lambda i, k, w: (i, 0))]),
        out_shape=[SDS((N_CHIPS, h, cols), BF16), SDS((h, cols), F32)],
        compiler_params=_params(("parallel", "arbitrary")),
    )(where, g, r)


def _add_chips(where, own, r, name):
    h, cols = own.shape
    tr = min(h, 256)
    nh = h // tr

    def body(where_ref, t_ref, r_ref, o_ref):
        del where_ref
        o_ref[...] = ((t_ref[...] + r_ref[0].astype(F32)) + r_ref[1].astype(F32)) + r_ref[2].astype(F32)

    return pl.pallas_call(
        body, name=name,
        grid_spec=pltpu.PrefetchScalarGridSpec(
            num_scalar_prefetch=1, grid=(nh,),
            in_specs=[pl.BlockSpec((tr, cols), lambda i, w: (i, 0)), pl.BlockSpec((3, tr, cols), lambda i, w: (0, i, 0))],
            out_specs=pl.BlockSpec((tr, cols), lambda i, w: (w[0] * nh + i, 0))),
        out_shape=SDS((2 * h, cols), F32),
        compiler_params=_params(("parallel",)),
    )(where, own, r)


def _sum_smalls(small_all):
    def body(all_ref, o_ref):
        acc = all_ref[0]
        for dev in range(1, N_DEV):
            acc = acc + all_ref[dev]
        o_ref[...] = acc

    return pl.pallas_call(
        body, name="sum_smalls", out_shape=SDS(small_all.shape[1:], F32),
        in_specs=[pl.BlockSpec(memory_space=pltpu.VMEM)], out_specs=pl.BlockSpec(memory_space=pltpu.VMEM),
    )(small_all)


def _adamw(g, w, m, v, name):
    rows, cols = g.shape
    tr = min(rows, 256)

    def body(g_ref, w_ref, m_ref, v_ref, d_ref, nm_ref, nv_ref):
        gv = g_ref[...]
        nm = ADAM_B1 * m_ref[...] + (1.0 - ADAM_B1) * gv
        nv = ADAM_B2 * v_ref[...] + (1.0 - ADAM_B2) * (gv * gv)
        nm_ref[...] = nm
        nv_ref[...] = nv
        m_hat = nm / (1.0 - ADAM_B1 ** ADAM_STEP)
        v_hat = nv / (1.0 - ADAM_B2 ** ADAM_STEP)
        d_ref[...] = -ADAM_LR * (m_hat / (jnp.sqrt(v_hat) + ADAM_EPS) + ADAM_WD * w_ref[...])

    spec = pl.BlockSpec((tr, cols), lambda i: (i, 0))
    return pl.pallas_call(
        body, name=name, grid=(rows // tr,), in_specs=[spec] * 4, out_specs=[spec] * 3,
        out_shape=[SDS(g.shape, F32)] * 3, compiler_params=_params(("parallel",)),
    )(g, w, m, v)


def _pad_rows(a, rows):
    return jnp.concatenate([a, jnp.zeros((rows - a.shape[0], a.shape[1]), a.dtype)], axis=0)


def _pad_cols(a, cols):
    return jnp.concatenate([a, jnp.zeros((a.shape[0], cols - a.shape[1]), a.dtype)], axis=1)


def kernel(x, a_pre_norm, a_w_in, a_conv_w, a_w_out, a_post_norm, kv_norm, w_kv, rel_bias, b_pre_norm, b_w_in, b_sinks, b_w_out, b_post_norm, loss_target, m_a_pre_norm, m_a_w_in, m_a_conv_w, m_a_w_out, m_a_post_norm, m_kv_norm, m_w_kv, m_rel_bias, m_b_pre_norm, m_b_w_in, m_b_sinks, m_b_w_out, m_b_post_norm, v_a_pre_norm, v_a_w_in, v_a_conv_w, v_a_w_out, v_a_post_norm, v_kv_norm, v_w_kv, v_rel_bias, v_b_pre_norm, v_b_w_in, v_b_sinks, v_b_w_out, v_b_post_norm):
    seq = x.shape[1]
    xs = x.reshape(seq, D)
    tgt = loss_target.reshape(seq, D)
    chip = 2 * lax.axis_index("x") + lax.axis_index("y")
    core = lax.axis_index("c")
    tm = _tile(seq, 512)
    tm_mix = _tile(seq, 256)
    tmw = _tile(seq, 1024)

    shards = [a_w_in[0], a_w_out[0], w_kv, b_w_in[0], b_w_out[0]]
    small_w = _pad_rows(jnp.concatenate([a_pre_norm, a_conv_w[0], a_post_norm], axis=0), 8)
    win_g, wouta_g, wkv_g, wbin_g, woutb_g, small_g = _gather_weights(shards, small_w)
    small_full = small_g.transpose(1, 0, 2).reshape(8, D)
    g_apre, conv_w, g_apost = small_full[0:1], _pad_rows(small_full[1:4], 8), small_full[4:5]
    wouta = wouta_g.reshape(D, D)
    wkv = wkv_g.reshape(D, 2 * KV_W)
    woutb = woutb_g.reshape(D, D)
    g_kv = kv_norm.reshape(1, D)

    proj, n1 = _a_in(xs, g_apre, win_g, tm)
    ya, oa, h1 = _a_mix(proj, xs, conv_w, wouta, g_apost, tm_mix)
    nk, nb, kv, q, zb = _b_in(h1, g_kv, b_pre_norm, wkv, wbin_g, tm)
    tab = _bias_table(rel_bias)
    sinks = b_sinks.reshape(N_HEADS)
    att, stats = _attn_fwd(q, kv, tab, sinks)
    ob, dy2, dh2, dqz, datt, loss_acc, dg_bpost = _mid(att, zb, h1, tgt, woutb, b_post_norm, tm)

    dqz, dkv, dtab, dsink = _attn_bwd(q, kv, datt, stats, tab, sinks, dqz)
    dh1, doa, dg_b = _b_bwd(dqz, dkv, h1, dh2, oa, wbin_g, wkv, g_kv, b_pre_norm, g_apost, tm)
    where = jnp.stack([core, chip]).astype(jnp.int32)
    dw_outa = _dw(ya, doa, D, tmw, "dw_a_out").reshape(N_CHIPS, D // 4, D)
    dw_kv = _dw(nk, dkv, 2 * KV_W, tmw, "dw_kv").reshape(N_CHIPS, D // 4, 2 * KV_W)
    dw_bin = _dw(nb, dqz, 512, tmw, "dw_b_in")
    dw_outb = _dw(ob, dy2, D, tmw, "dw_b_out").reshape(N_CHIPS, D // 4, D)
    grads1 = [dw_outa, dw_kv, dw_bin, dw_outb]
    names1 = ["a_w_out", "w_kv", "b_w_in", "b_w_out"]
    from_sibling1, _, _ = _sibling_exchange("to_sibling_1", to_sibling=grads1)
    sums1 = [_add_sibling(where, g, r, "add_sibling_" + nm) for g, r, nm in zip(grads1, from_sibling1, names1)]
    dproj, dconv_w, from_chips1 = _a_bwd(doa, proj, conv_w, wouta, tm_mix, [t for t, _ in sums1])
    shards1 = [_add_chips(where, own, r, "add_chips_" + nm) for (_, own), r, nm in zip(sums1, from_chips1, names1)]
    dw_in = _dw(n1, dproj, D, tmw, "dw_a_in")
    from_sibling2, (g_wouta, g_wkv, g_wbin, g_woutb), _ = _sibling_exchange(
        "to_sibling_2", to_sibling=[dw_in], shards=shards1)
    part2, own2 = _add_sibling(where, dw_in, from_sibling2[0], "add_sibling_a_w_in")
    nt = seq // tm
    nt_first = max(nt - max(nt // 4, 1), 1)
    gx_first, dg_first, from_chips2 = _a_in_bwd(dproj, xs, dh1, win_g, g_apre, tm, 0, nt_first, "a_in_bwd_first",
                                                parts=[part2])
    grad_x, dg_apre, _ = _a_in_bwd(dproj, xs, dh1, win_g, g_apre, tm, nt_first, nt - nt_first, "a_in_bwd_rest",
                                   before=(gx_first, dg_first))
    shard2 = _add_chips(where, own2, from_chips2[0], "add_chips_a_w_in")
    drel = _bias_fold(dtab)

    smalls = jnp.concatenate([
        dg_apre[0:1], dconv_w[0:3], dg_b[2:3], dg_b[0:1], dg_b[1:2], dg_bpost[0:1],
        _pad_cols(drel[:, 0:N_HEADS].reshape(1, N_BUCKETS * N_HEADS), D), _pad_cols(dsink[0:1], D),
        _pad_cols(loss_acc[0:1], D), jnp.zeros((SMALL_ROWS - 11, D), F32)], axis=0)
    _, (g_win,), small_all = _sibling_exchange("share_last", shards=[shard2], smalls=smalls)
    tot = _sum_smalls(small_all)

    big = {}
    for nm, g, w, m, v in [("a_w_in", g_win, a_w_in, m_a_w_in, v_a_w_in), ("a_w_out", g_wouta, a_w_out, m_a_w_out, v_a_w_out),
                           ("w_kv", g_wkv, w_kv, m_w_kv, v_w_kv), ("b_w_in", g_wbin, b_w_in, m_b_w_in, v_b_w_in),
                           ("b_w_out", g_woutb, b_w_out, m_b_w_out, v_b_w_out)]:
        shp = w.shape
        two = (shp[-2], shp[-1])
        d, nm_, nv_ = _adamw(g, w.reshape(two), m.reshape(two), v.reshape(two), "adamw_" + nm)
        big[nm] = (g.reshape(shp), d.reshape(shp), nm_.reshape(shp), nv_.reshape(shp))

    col0 = chip * (D // 4)
    sharded = lax.dynamic_slice(tot, (0, col0), (8, D // 4))
    g_shard = sharded
    w_shard = small_w
    m_shard = _pad_rows(jnp.concatenate([m_a_pre_norm, m_a_conv_w[0], m_a_post_norm], axis=0), 8)
    v_shard = _pad_rows(jnp.concatenate([v_a_pre_norm, v_a_conv_w[0], v_a_post_norm], axis=0), 8)
    ds_, ms_, vs_ = _adamw(g_shard, w_shard, m_shard, v_shard, "adamw_small_sharded")

    def rep_pack(kvn, bpre, bpost, rel, snk):
        rows = [kvn.reshape(1, D), bpre.reshape(1, D), bpost.reshape(1, D),
                _pad_cols(rel.reshape(1, N_BUCKETS * N_HEADS), D), _pad_cols(snk.reshape(1, N_HEADS), D)]
        return jnp.concatenate(rows + [jnp.zeros((3, D), F32)], axis=0)

    g_rep = tot[5:13]
    w_rep = rep_pack(kv_norm, b_pre_norm, b_post_norm, rel_bias, b_sinks)
    m_rep = rep_pack(m_kv_norm, m_b_pre_norm, m_b_post_norm, m_rel_bias, m_b_sinks)
    v_rep = rep_pack(v_kv_norm, v_b_pre_norm, v_b_post_norm, v_rel_bias, v_b_sinks)
    dr_, mr_, vr_ = _adamw(g_rep, w_rep, m_rep, v_rep, "adamw_small_replicated")

    def unshard(p):
        return {"a_pre_norm": p[0:1], "a_conv_w": p[1:4].reshape(1, 3, D // 4), "a_post_norm": p[4:5]}

    def unrep(p):
        return {"kv_norm": p[0], "b_pre_norm": p[1:2], "b_post_norm": p[2:3],
                "rel_bias": p[3, 0:N_BUCKETS * N_HEADS].reshape(N_BUCKETS, N_HEADS), "b_sinks": p[4:5, 0:N_HEADS]}

    order = ["a_pre_norm", "a_w_in", "a_conv_w", "a_w_out", "a_post_norm", "kv_norm", "w_kv", "rel_bias",
             "b_pre_norm", "b_w_in", "b_sinks", "b_w_out", "b_post_norm"]
    outs = []
    for which, sh, rp in [(0, g_shard, g_rep), (1, ds_, dr_), (2, ms_, mr_), (3, vs_, vr_)]:
        small = {**unshard(sh), **unrep(rp)}
        for nm in order:
            outs.append(big[nm][which] if nm in big else small[nm])
    loss = 0.5 * tot[10, 0]
    return (loss, grad_x.reshape(x.shape), *outs)
```

```python
import functools
import math

import jax
import jax.numpy as jnp
from jax import lax
from jax.experimental import pallas as pl
from jax.experimental.pallas import tpu as pltpu

F32 = jnp.float32
BF16 = jnp.bfloat16
MESH = pl.DeviceIdType.MESH
SDS = jax.ShapeDtypeStruct

D = 1024
HEAD_DIM = 64
N_HEADS = 16
N_KV = 2
GROUP = 8
KV_W = 128
BLK = 128
N_BUCKETS = 32
MAX_EXACT = 16
MAX_DISTANCE = 128
EPS = 1e-6
NEG_INF = -1e30
Q_SCALE = HEAD_DIM ** -0.5

ADAM_LR = 0.001
ADAM_B1 = 0.9
ADAM_B2 = 0.999
ADAM_EPS = 1e-08
ADAM_WD = 0.01
ADAM_STEP = 10

N_CHIPS = 4
N_DEV = 8
VMEM_LIMIT = 56 * 1024 * 1024
SMALL_ROWS = 16
HALO = 16


def _bucket_thresholds():
    def bucket(d):
        big = MAX_EXACT + int(math.log(d / MAX_EXACT) / math.log(MAX_DISTANCE / MAX_EXACT)
                              * (N_BUCKETS - MAX_EXACT))
        return d if d < MAX_EXACT else min(big, N_BUCKETS - 1)
    out = []
    for b in range(MAX_EXACT + 1, N_BUCKETS):
        out.append(min(d for d in range(MAX_EXACT, MAX_DISTANCE) if bucket(d) >= b))
    return tuple(out)


BUCKET_THRESHOLDS = _bucket_thresholds()


def _params(semantics=None, vmem=VMEM_LIMIT):
    return pltpu.CompilerParams(dimension_semantics=semantics, vmem_limit_bytes=vmem)


def _tile(n, pref):
    return pref if n >= 2 * pref else max(n // 2, 8)


def _rms_scale(v):
    return lax.rsqrt(jnp.mean(v * v, axis=-1, keepdims=True) + EPS)


def _nt(a, b):
    return lax.dot_general(a, b, (((1,), (1,)), ((), ())), preferred_element_type=F32)


def _tn(a, b):
    return lax.dot_general(a, b, (((0,), (0,)), ((), ())), preferred_element_type=F32)


def _nn(a, b):
    return jnp.dot(a, b, preferred_element_type=F32)


def _silu_parts(z):
    sg = jax.nn.sigmoid(z)
    return sg, z * sg


def _dsilu(z, sg):
    return sg * (1.0 + z * (1.0 - sg))


def _acc_row(ref, row, val):
    ref[row:row + 1, :] += val


def _gather_copies(outs, splits, ici_send, ici_recv, d2d_send, d2d_recv):
    x, y, c = lax.axis_index("x"), lax.axis_index("y"), lax.axis_index("c")
    k = 2 * x + y
    sibling = (x, y, 1 - c)

    def part(o_ref, chip, core, split):
        if not split:
            return o_ref.at[chip]
        h = o_ref.shape[1] // 2
        return o_ref.at[chip, pl.ds(pl.multiple_of(core * h, 16), h)]

    def remote(ref, a, j, sems, to):
        return pltpu.make_async_remote_copy(src_ref=ref, dst_ref=ref, send_sem=sems[0].at[3 * a + j],
                                            recv_sem=sems[1].at[3 * a + j], device_id=to, device_id_type=MESH)

    copies = []
    for a, (o_ref, split) in enumerate(zip(outs, splits)):
        for j, (px, py) in enumerate([(x, 1 - y), (1 - x, y), (1 - x, 1 - y)]):
            kj = 2 * px + py
            ici, d2d = (ici_send, ici_recv), (d2d_send, d2d_recv)
            copies.append((remote(part(o_ref, k, c, split), a, j, ici, (px, py, c)),
                           remote(part(o_ref, kj, c, split), a, j, ici, (px, py, c)),
                           remote(part(o_ref, kj, c, split), a, j, d2d, sibling) if split else None,
                           remote(part(o_ref, kj, 1 - c, split), a, j, d2d, sibling) if split else None))
    return copies


def _gather_sems(n):
    return [pltpu.SemaphoreType.DMA((3 * n,)) for _ in range(4)]


def _gather_weights(shards, small, n_now):
    n = len(shards)

    def body(*refs):
        ins, small_in = refs[:n], refs[n]
        outs, small_out = refs[n + 1:2 * n + 1], refs[2 * n + 1]
        sems = refs[2 * n + 2:]
        k = 2 * lax.axis_index("x") + lax.axis_index("y")
        for i_ref, o_ref in zip(ins, outs):
            o_ref[k] = i_ref[...].astype(BF16)
        small_out[k] = small_in[...]
        copies = _gather_copies(list(outs[:n_now]) + [small_out], [True] * n_now + [False], *sems)
        for send, _, _, _ in copies:
            send.start()
        for _, arrival, forward, _ in copies:
            arrival.wait_recv()
            if forward is not None:
                forward.start()
        for send, _, forward, forwarded in copies:
            if forward is not None:
                forwarded.wait_recv()
                forward.wait_send()
            send.wait_send()

    vm = pl.BlockSpec(memory_space=pltpu.VMEM)
    out_shape = [SDS((N_CHIPS,) + s.shape, BF16) for s in shards] + [SDS((N_CHIPS,) + small.shape, F32)]
    return pl.pallas_call(
        body, name="gather_weights", out_shape=out_shape,
        in_specs=[vm] * (n + 1), out_specs=[vm] * (n + 1),
        scratch_shapes=_gather_sems(n_now + 1),
        compiler_params=pltpu.CompilerParams(vmem_limit_bytes=VMEM_LIMIT),
    )(*shards, small)


def _a_in(x, g_pre, win_g, tm, later):
    s = x.shape[0]
    nt = s // tm
    n = len(later)

    def body(*refs):
        x_ref, g_ref, w_ref = refs[:3]
        proj_ref, n1_ref = refs[3 + n:5 + n]
        gathered = refs[5 + n:5 + 2 * n]
        sems = refs[5 + 2 * n:]
        i = pl.program_id(0)

        @pl.when(i == 0)
        def _():
            for send, _, _, _ in _gather_copies(gathered, [True] * n, *sems):
                send.start()
        xv = x_ref[...]
        n1 = (xv * _rms_scale(xv) * g_ref[...]).astype(BF16)
        n1_ref[...] = n1
        for j in range(4):
            proj_ref[:, D * j:D * (j + 1)] = _nn(n1, w_ref[j]).astype(BF16)

        @pl.when(i == nt // 2)
        def _():
            for _, arrival, forward, _ in _gather_copies(gathered, [True] * n, *sems):
                arrival.wait_recv()
                forward.start()

        @pl.when(i == nt - 1)
        def _():
            for send, _, forward, forwarded in _gather_copies(gathered, [True] * n, *sems):
                forwarded.wait_recv()
                forward.wait_send()
                send.wait_send()

    row = lambda i: (i, 0)
    anyspace = pl.BlockSpec(memory_space=pl.ANY)
    proj, n1, *gathered = pl.pallas_call(
        body, name="a_in", grid=(nt,),
        in_specs=[pl.BlockSpec((tm, D), row), pl.BlockSpec((1, D), lambda i: (0, 0)),
                  pl.BlockSpec((4, D, D), lambda i: (0, 0, 0))] + [anyspace] * n,
        out_specs=[pl.BlockSpec((tm, 4 * D), row), pl.BlockSpec((tm, D), row)] + [anyspace] * n,
        out_shape=[SDS((s, 4 * D), BF16), SDS((s, D), BF16)] + [SDS(w.shape, w.dtype) for w in later],
        scratch_shapes=_gather_sems(n),
        input_output_aliases={3 + a: 2 + a for a in range(n)},
        compiler_params=_params(("arbitrary",)),
    )(x, g_pre, win_g, *later)
    return proj, n1, gathered


def _shift_rows(v, last, second_last, rows):
    v1 = jnp.where(rows >= 1, pltpu.roll(v, 1, 0), last)
    v2 = jnp.where(rows >= 2, pltpu.roll(v, 2, 0), jnp.where(rows == 1, last, second_last))
    return v1, v2


def _a_mix(proj, x, conv_w, w_out, g_post, tm):
    s = x.shape[0]

    def body(proj_ref, x_ref, cw_ref, w_ref, g_ref, ya_ref, oa_ref, h1_ref, carry):
        @pl.when(pl.program_id(0) == 0)
        def _():
            carry[...] = jnp.zeros_like(carry)
        v = proj_ref[:, D:2 * D].astype(F32) * proj_ref[:, 2 * D:3 * D].astype(F32)
        rows = lax.broadcasted_iota(jnp.int32, (tm, D), 0)
        before = carry[...]
        v1, v2 = _shift_rows(v, before[7:8, :], before[6:7, :], rows)
        carry[...] = v[tm - 8:tm, :]
        conv = cw_ref[0:1, :] * v2 + cw_ref[1:2, :] * v1 + cw_ref[2:3, :] * v
        _, sz = _silu_parts(proj_ref[:, 3 * D:4 * D].astype(F32))
        ya = (proj_ref[:, 0:D].astype(F32) * conv * sz).astype(BF16)
        ya_ref[...] = ya
        oa = _nn(ya, w_ref[...])
        oa_ref[...] = oa
        h1_ref[...] = x_ref[...] + oa * _rms_scale(oa) * g_ref[...]

    row = lambda i: (i, 0)
    fix = lambda i: (0, 0)
    return pl.pallas_call(
        body, name="a_mix", grid=(s // tm,),
        in_specs=[pl.BlockSpec((tm, 4 * D), row), pl.BlockSpec((tm, D), row), pl.BlockSpec((8, D), fix),
                  pl.BlockSpec((D, D), fix), pl.BlockSpec((1, D), fix)],
        out_specs=[pl.BlockSpec((tm, D), row)] * 3,
        out_shape=[SDS((s, D), BF16), SDS((s, D), F32), SDS((s, D), F32)],
        scratch_shapes=[pltpu.VMEM((8, D), F32)],
        compiler_params=_params(("arbitrary",)),
    )(proj, x, conv_w, w_out, g_post)


def _b_in(h1, g_kv, g_pre, w_kv, wbin_g, tm):
    s = h1.shape[0]

    def body(h_ref, gk_ref, gb_ref, wkv_ref, wb_ref, nk_ref, nb_ref, kv_ref, q_ref, z_ref):
        h = h_ref[...]
        hh = h * _rms_scale(h)
        nk = (hh * gk_ref[...]).astype(BF16)
        nb = (hh * gb_ref[...]).astype(BF16)
        nk_ref[...] = nk
        nb_ref[...] = nb
        kv_ref[...] = _nn(nk, wkv_ref[...]).astype(BF16)
        for j in range(2):
            q_ref[:, 512 * j:512 * (j + 1)] = (_nn(nb, wb_ref[j]) * Q_SCALE).astype(BF16)
            z_ref[:, 512 * j:512 * (j + 1)] = _nn(nb, wb_ref[2 + j]).astype(BF16)

    row = lambda i: (i, 0)
    fix = lambda i: (0, 0)
    return pl.pallas_call(
        body, name="b_in", grid=(s // tm,),
        in_specs=[pl.BlockSpec((tm, D), row), pl.BlockSpec((1, D), fix), pl.BlockSpec((1, D), fix),
                  pl.BlockSpec((D, 2 * KV_W), fix), pl.BlockSpec((4, D, 512), lambda i: (0, 0, 0))],
        out_specs=[pl.BlockSpec((tm, D), row), pl.BlockSpec((tm, D), row), pl.BlockSpec((tm, 2 * KV_W), row),
                   pl.BlockSpec((tm, D), row), pl.BlockSpec((tm, D), row)],
        out_shape=[SDS((s, D), BF16), SDS((s, D), BF16), SDS((s, 2 * KV_W), BF16), SDS((s, D), BF16),
                   SDS((s, D), BF16)],
        compiler_params=_params(("parallel",)),
    )(h1, g_kv, g_pre, w_kv, wbin_g)


def _band_buckets():
    q = lax.broadcasted_iota(jnp.int32, (BLK, 2 * BLK), 0)
    k = lax.broadcasted_iota(jnp.int32, (BLK, 2 * BLK), 1)
    dist = q + BLK - k
    bucket = jnp.where(dist < MAX_EXACT, dist, MAX_EXACT)
    for t in BUCKET_THRESHOLDS:
        bucket = bucket + jnp.where(dist >= t, 1, 0)
    in_window = (dist >= 0) & (dist < BLK)
    return jnp.where(in_window, bucket, -1)


def _head_place(h):
    kh, j, e = h // GROUP, (h % GROUP) // 2, h % 2
    return kh, slice(BLK * j, BLK * (j + 1)), slice(2 * BLK * e, 2 * BLK * (e + 1))


def _bias_table(rel_bias, sinks):
    def body(rb_ref, sink_ref, tab_ref):
        bucket = _band_buckets()
        col = lax.broadcasted_iota(jnp.int32, (BLK, 2 * BLK), 1)
        for h in range(N_HEADS):
            acc = jnp.where(bucket < 0, NEG_INF, 0.0).astype(F32)
            for b in range(N_BUCKETS):
                acc = jnp.where(bucket == b, rb_ref[b, h], acc)
            acc = jnp.where(col == 0, sink_ref[h], acc)
            kh, rows, cols = _head_place(h)
            tab_ref[1, kh, rows, cols] = acc
            tab_ref[0, kh, rows, cols] = jnp.where((col > 0) & (col < BLK), NEG_INF, acc)

    return pl.pallas_call(
        body, name="bias_table", out_shape=SDS((2, N_KV, 4 * BLK, 4 * BLK), F32),
        in_specs=[pl.BlockSpec(memory_space=pltpu.SMEM), pl.BlockSpec(memory_space=pltpu.SMEM)],
        out_specs=pl.BlockSpec(memory_space=pltpu.VMEM),
    )(rel_bias, sinks)


def _bias_fold(dtab):
    def body(dtab_ref, out_ref, dsink_ref):
        bucket = _band_buckets()
        row = lax.broadcasted_iota(jnp.int32, (N_BUCKETS, 128), 0)
        lane = lax.broadcasted_iota(jnp.int32, (N_BUCKETS, 128), 1)
        row8 = lax.broadcasted_iota(jnp.int32, (8, 128), 0)
        lane8 = lax.broadcasted_iota(jnp.int32, (8, 128), 1)
        acc = jnp.zeros((N_BUCKETS, 128), F32)
        dsink = jnp.zeros((8, 128), F32)
        for h in range(N_HEADS):
            kh, rows, cols = _head_place(h)
            dt = dtab_ref[kh, rows, cols]
            for b in range(N_BUCKETS):
                val = jnp.sum(jnp.where(bucket == b, dt, 0.0))
                acc = acc + jnp.where((row == b) & (lane == h), val, 0.0)
            dsink = dsink + jnp.where((row8 == 0) & (lane8 == h), jnp.sum(dt[:, 0:1]), 0.0)
        out_ref[...] = acc
        dsink_ref[...] = dsink

    vm = pl.BlockSpec(memory_space=pltpu.VMEM)
    return pl.pallas_call(
        body, name="bias_fold", out_shape=[SDS((N_BUCKETS, 128), F32), SDS((8, 128), F32)],
        in_specs=[vm], out_specs=[vm, vm],
    )(dtab)


def _pair_operands(prev, cur):
    t = jnp.concatenate([prev, cur], axis=0).astype(F32)
    t = jnp.where(lax.broadcasted_iota(jnp.int32, t.shape, 0) == 0, 0.0, t)
    tr = pltpu.roll(t, HEAD_DIM, 1)
    lo = lax.broadcasted_iota(jnp.int32, t.shape, 1) < HEAD_DIM
    zero = jnp.zeros_like(t)
    head0 = jnp.concatenate([jnp.where(lo, t, zero), jnp.where(lo, zero, tr)], axis=0).astype(BF16)
    head1 = jnp.concatenate([jnp.where(lo, tr, zero), jnp.where(lo, zero, t)], axis=0).astype(BF16)
    return head0, head1


def _pair_fold(d0, d1):
    lo = lax.broadcasted_iota(jnp.int32, (2 * BLK, KV_W), 1) < HEAD_DIM
    zero = jnp.zeros((2 * BLK, KV_W), F32)
    g0 = jnp.where(lo, d0[0:256], zero) + pltpu.roll(jnp.where(lo, zero, d0[256:512]), HEAD_DIM, 1)
    g1 = pltpu.roll(jnp.where(lo, d1[0:256], zero), HEAD_DIM, 1) + jnp.where(lo, zero, d1[256:512])
    return jnp.where(lax.broadcasted_iota(jnp.int32, (2 * BLK, KV_W), 0) == 0, 0.0, g0 + g1)


def _stack_pairs(ref, kh):
    return jnp.concatenate([ref[:, 128 * (4 * kh + j):128 * (4 * kh + j + 1)] for j in range(4)], axis=0)


def _table_spec():
    return pl.BlockSpec((1, N_KV, 4 * BLK, 4 * BLK), lambda n: (jnp.minimum(n, 1), 0, 0, 0))


def _attn_fwd(q, kv, tab):
    s = q.shape[0]

    def body(q_ref, kp_ref, kc_ref, vp_ref, vc_ref, tab_ref, att_ref, stats_ref):
        k2 = _pair_operands(kp_ref[...], kc_ref[...])
        v2 = _pair_operands(vp_ref[...], vc_ref[...])
        lane = lax.broadcasted_iota(jnp.int32, (BLK, 128), 1)
        stats = jnp.zeros((BLK, 128), F32)
        for kh in range(N_KV):
            sc = _nt(_stack_pairs(q_ref, kh), k2[kh])
            ps = []
            for e in range(2):
                lg = sc[:, 256 * e:256 * (e + 1)] + tab_ref[0, kh, :, 256 * e:256 * (e + 1)]
                m = jnp.max(lg, axis=-1, keepdims=True)
                ex = jnp.exp(lg - m)
                den = jnp.sum(ex, axis=-1, keepdims=True)
                ps.append(ex * (1.0 / den))
                lse = m + jnp.log(den)
                for j in range(4):
                    stats = jnp.where(lane == GROUP * kh + 2 * j + e, lse[BLK * j:BLK * (j + 1)], stats)
            out = _nn(jnp.concatenate(ps, axis=1).astype(BF16), v2[kh])
            for j in range(4):
                att_ref[:, 128 * (4 * kh + j):128 * (4 * kh + j + 1)] = out[BLK * j:BLK * (j + 1)].astype(BF16)
        stats_ref[...] = stats

    cur = lambda n: (n, 0)
    prev = lambda n: (jnp.maximum(n - 1, 0), 0)
    return pl.pallas_call(
        body, name="attn_fwd", grid=(s // BLK,),
        in_specs=[pl.BlockSpec((BLK, D), cur),
                  pl.BlockSpec((BLK, KV_W), prev), pl.BlockSpec((BLK, KV_W), cur),
                  pl.BlockSpec((BLK, KV_W), lambda n: (jnp.maximum(n - 1, 0), 1)),
                  pl.BlockSpec((BLK, KV_W), lambda n: (n, 1)), _table_spec()],
        out_specs=[pl.BlockSpec((BLK, D), cur), pl.BlockSpec((BLK, 128), cur)],
        out_shape=[SDS((s, D), BF16), SDS((s, 128), F32)],
        compiler_params=_params(("parallel",)),
    )(q, kv, kv, kv, kv, tab)


def _mid(att, zb, h1, tgt, w_out, g_post, tm):
    s = att.shape[0]

    def body(att_ref, z_ref, h1_ref, t_ref, w_ref, g_ref,
             ob_ref, dy_ref, dh_ref, dqz_ref, datt_ref, loss_ref, dg_ref):
        @pl.when(pl.program_id(0) == 0)
        def _():
            loss_ref[...] = jnp.zeros_like(loss_ref)
            dg_ref[...] = jnp.zeros_like(dg_ref)
        att = att_ref[...].astype(F32)
        z = z_ref[...].astype(F32)
        sg, sz = _silu_parts(z)
        ob = (att * sz).astype(BF16)
        ob_ref[...] = ob
        y2 = _nn(ob, w_ref[...])
        r2 = _rms_scale(y2)
        yh = y2 * r2
        g = g_ref[...]
        err = (h1_ref[...] + yh * g) - t_ref[...]
        loss_ref[...] += jnp.sum(jnp.sum(err * err, axis=-1, keepdims=True) / D)
        dh = err / D
        dh_ref[...] = dh
        _acc_row(dg_ref, 0, jnp.sum(dh * yh, axis=0, keepdims=True))
        dyh = dh * g
        dy = (r2 * (dyh - yh * jnp.mean(dyh * yh, axis=-1, keepdims=True))).astype(BF16)
        dy_ref[...] = dy
        dob = _nt(dy, w_ref[...])
        datt_ref[...] = (dob * sz).astype(BF16)
        dqz_ref[...] = (dob * att * _dsilu(z, sg)).astype(BF16)

    row = lambda i: (i, 0)
    fix = lambda i: (0, 0)
    return pl.pallas_call(
        body, name="mid", grid=(s // tm,),
        in_specs=[pl.BlockSpec((tm, D), row)] * 4 + [pl.BlockSpec((D, D), fix), pl.BlockSpec((1, D), fix)],
        out_specs=[pl.BlockSpec((tm, D), row), pl.BlockSpec((tm, D), row), pl.BlockSpec((tm, D), row),
                   pl.BlockSpec((tm, D), lambda i: (i, 1)), pl.BlockSpec((tm, D), row),
                   pl.BlockSpec((8, 128), fix), pl.BlockSpec((8, D), fix)],
        out_shape=[SDS((s, D), BF16), SDS((s, D), BF16), SDS((s, D), F32), SDS((s, 2 * D), BF16),
                   SDS((s, D), BF16), SDS((8, 128), F32), SDS((8, D), F32)],
        compiler_params=_params(("arbitrary",)),
    )(att, zb, h1, tgt, w_out, g_post)


def _attn_bwd(q, kv, datt, stats, tab, dqz):
    s = q.shape[0]
    nb = s // BLK

    def body(q_ref, kp_ref, kc_ref, vp_ref, vc_ref, da_ref, st_ref, tab_ref, dqz_in,
             dq_ref, dkv_ref, dtab_ref, dk_carry, dv_carry):
        del dqz_in
        n = pl.program_id(0)

        @pl.when(n == 0)
        def _():
            dtab_ref[...] = jnp.zeros_like(dtab_ref)
            dk_carry[...] = jnp.zeros_like(dk_carry)
            dv_carry[...] = jnp.zeros_like(dv_carry)

        @pl.when(n < nb)
        def _():
            k2 = _pair_operands(kp_ref[...], kc_ref[...])
            v2 = _pair_operands(vp_ref[...], vc_ref[...])
            lane = lax.broadcasted_iota(jnp.int32, (BLK, 128), 1)
            stats = st_ref[...]
            dk2, dv2 = [], []
            for kh in range(N_KV):
                qs = _stack_pairs(q_ref, kh)
                das = _stack_pairs(da_ref, kh)
                sc = _nt(qs, k2[kh])
                dp = _nt(das, v2[kh])
                ps, dss = [], []
                for e in range(2):
                    heads = [GROUP * kh + 2 * j + e for j in range(4)]
                    lse = jnp.concatenate([jnp.sum(jnp.where(lane == h, stats, 0.0), axis=-1, keepdims=True)
                                           for h in heads], axis=0)
                    cols = slice(256 * e, 256 * (e + 1))
                    p = jnp.exp(sc[:, cols] + tab_ref[0, kh, :, cols] - lse)
                    delta = jnp.sum(p * dp[:, cols], axis=-1, keepdims=True)
                    ds = p * (dp[:, cols] - delta)
                    dtab_ref[kh, :, cols] += ds
                    ps.append(p)
                    dss.append(ds)
                p2 = jnp.concatenate(ps, axis=1).astype(BF16)
                ds2 = jnp.concatenate(dss, axis=1).astype(BF16)
                dq = _nn(ds2, k2[kh]) * Q_SCALE
                for j in range(4):
                    dq_ref[:, 128 * (4 * kh + j):128 * (4 * kh + j + 1)] = dq[BLK * j:BLK * (j + 1)].astype(BF16)
                dk2.append(_tn(ds2, qs))
                dv2.append(_tn(p2, das))
            dkk = _pair_fold(dk2[0], dk2[1])
            dvv = _pair_fold(dv2[0], dv2[1])
            dkv_ref[:, 0:KV_W] = (dk_carry[...] + dkk[0:BLK]).astype(BF16)
            dkv_ref[:, KV_W:2 * KV_W] = (dv_carry[...] + dvv[0:BLK]).astype(BF16)
            dk_carry[...] = dkk[BLK:2 * BLK]
            dv_carry[...] = dvv[BLK:2 * BLK]

        @pl.when(n == nb)
        def _():
            dkv_ref[:, 0:KV_W] = dk_carry[...].astype(BF16)
            dkv_ref[:, KV_W:2 * KV_W] = dv_carry[...].astype(BF16)

    cur = lambda n: (jnp.minimum(n, nb - 1), 0)
    prev = lambda n: (jnp.clip(n - 1, 0, nb - 1), 0)
    return pl.pallas_call(
        body, name="attn_bwd", grid=(nb + 1,),
        in_specs=[pl.BlockSpec((BLK, D), cur),
                  pl.BlockSpec((BLK, KV_W), prev), pl.BlockSpec((BLK, KV_W), cur),
                  pl.BlockSpec((BLK, KV_W), lambda n: (jnp.clip(n - 1, 0, nb - 1), 1)),
                  pl.BlockSpec((BLK, KV_W), lambda n: (jnp.minimum(n, nb - 1), 1)),
                  pl.BlockSpec((BLK, D), cur), pl.BlockSpec((BLK, 128), cur), _table_spec(),
                  pl.BlockSpec(memory_space=pl.ANY)],
        out_specs=[pl.BlockSpec((BLK, D), cur), pl.BlockSpec((BLK, 2 * KV_W), prev),
                   pl.BlockSpec((N_KV, 4 * BLK, 4 * BLK), lambda n: (0, 0, 0))],
        out_shape=[SDS((s, 2 * D), BF16), SDS((s, 2 * KV_W), BF16), SDS((N_KV, 4 * BLK, 4 * BLK), F32)],
        scratch_shapes=[pltpu.VMEM((BLK, KV_W), F32), pltpu.VMEM((BLK, KV_W), F32)],
        input_output_aliases={8: 0},
        compiler_params=_params(("arbitrary",)),
    )(q, kv, kv, kv, kv, datt, stats, tab, dqz)


def _b_bwd(dqz, dkv, h1, dh2, oa, wbin_g, w_kv, g_kv, g_pre, g_apost, tm):
    s = h1.shape[0]

    def body(dqz_ref, dkv_ref, h_ref, dh2_ref, oa_ref, wb_ref, wkv_ref, gk_ref, gb_ref, ga_ref,
             dh1_ref, doa_ref, dg_ref):
        @pl.when(pl.program_id(0) == 0)
        def _():
            dg_ref[...] = jnp.zeros_like(dg_ref)
        dnb = _nt(dqz_ref[:, 0:512], wb_ref[0])
        for j in range(1, 4):
            dnb = dnb + _nt(dqz_ref[:, 512 * j:512 * (j + 1)], wb_ref[j])
        dnk = _nt(dkv_ref[...], wkv_ref[...])
        h = h_ref[...]
        r = _rms_scale(h)
        hh = h * r
        _acc_row(dg_ref, 0, jnp.sum(dnk * hh, axis=0, keepdims=True))
        _acc_row(dg_ref, 1, jnp.sum(dnb * hh, axis=0, keepdims=True))
        dhh = dnb * gb_ref[...] + dnk * gk_ref[...]
        dh1 = dh2_ref[...] + r * (dhh - hh * jnp.mean(dhh * hh, axis=-1, keepdims=True))
        dh1_ref[...] = dh1
        oa = oa_ref[...]
        ra = _rms_scale(oa)
        oh = oa * ra
        _acc_row(dg_ref, 2, jnp.sum(dh1 * oh, axis=0, keepdims=True))
        doh = dh1 * ga_ref[...]
        doa_ref[...] = (ra * (doh - oh * jnp.mean(doh * oh, axis=-1, keepdims=True))).astype(BF16)

    row = lambda i: (i, 0)
    fix = lambda i: (0, 0)
    return pl.pallas_call(
        body, name="b_bwd", grid=(s // tm,),
        in_specs=[pl.BlockSpec((tm, 2 * D), row), pl.BlockSpec((tm, 2 * KV_W), row), pl.BlockSpec((tm, D), row),
                  pl.BlockSpec((tm, D), row), pl.BlockSpec((tm, D), row),
                  pl.BlockSpec((4, D, 512), lambda i: (0, 0, 0)), pl.BlockSpec((D, 2 * KV_W), fix),
                  pl.BlockSpec((1, D), fix), pl.BlockSpec((1, D), fix), pl.BlockSpec((1, D), fix)],
        out_specs=[pl.BlockSpec((tm, D), row), pl.BlockSpec((tm, D), row), pl.BlockSpec((8, D), fix)],
        out_shape=[SDS((s, D), F32), SDS((s, D), BF16), SDS((8, D), F32)],
        compiler_params=_params(("arbitrary",)),
    )(dqz, dkv, h1, dh2, oa, wbin_g, w_kv, g_kv, g_pre, g_apost)


def _chip_exchange(parts, recvs, send, recv):
    x, y, c = lax.axis_index("x"), lax.axis_index("y"), lax.axis_index("c")
    chips = [(x, 1 - y), (1 - x, y), (1 - x, 1 - y)]
    copies = []
    for a, (t, r) in enumerate(zip(parts, recvs)):
        for j, (px, py) in enumerate(chips):
            copies.append(pltpu.make_async_remote_copy(
                src_ref=t.at[2 * px + py], dst_ref=r.at[j], send_sem=send.at[3 * a + j],
                recv_sem=recv.at[3 * a + j], device_id=(px, py, c), device_id_type=MESH))
    return copies


def _exchange_specs(parts):
    anyspace = pl.BlockSpec(memory_space=pl.ANY)
    n = len(parts)
    return ([anyspace] * n, [anyspace] * n, [SDS((3,) + t.shape[1:], t.dtype) for t in parts],
            [pltpu.SemaphoreType.DMA((3 * n,)), pltpu.SemaphoreType.DMA((3 * n,))])


def _a_bwd(doa, proj, conv_w, w_out, tm, parts):
    s = doa.shape[0]
    nt = s // tm
    n = len(parts)
    ex_in, ex_out, ex_shape, ex_sems = _exchange_specs(parts)

    def body(*refs):
        doa_ref, proj_ref, halo_ref, cw_ref, w_ref = refs[:5]
        part_refs = refs[5:5 + n]
        dproj_ref, dcw_ref = refs[5 + n:7 + n]
        recv_refs = refs[7 + n:7 + 2 * n]
        carry, send, recv = refs[7 + 2 * n:]
        i = pl.program_id(0)
        r = nt - 1 - i

        @pl.when(i == 0)
        def _():
            dcw_ref[...] = jnp.zeros_like(dcw_ref)
            carry[...] = jnp.zeros_like(carry)
            for cp in _chip_exchange(part_refs, recv_refs, send, recv):
                cp.start()
        dya = _nt(doa_ref[...], w_ref[...])
        bg = proj_ref[:, 0:D].astype(F32)
        cg = proj_ref[:, D:2 * D].astype(F32)
        u = proj_ref[:, 2 * D:3 * D].astype(F32)
        z = proj_ref[:, 3 * D:4 * D].astype(F32)
        v = cg * u
        before = jnp.where(r > 0, halo_ref[:, D:2 * D].astype(F32) * halo_ref[:, 2 * D:3 * D].astype(F32), 0.0)
        rows = lax.broadcasted_iota(jnp.int32, (tm, D), 0)
        v1, v2 = _shift_rows(v, before[HALO - 1:HALO, :], before[HALO - 2:HALO - 1, :], rows)
        conv = cw_ref[0:1, :] * v2 + cw_ref[1:2, :] * v1 + cw_ref[2:3, :] * v
        sg, sz = _silu_parts(z)
        dproj_ref[:, 0:D] = (dya * conv * sz).astype(BF16)
        dproj_ref[:, 3 * D:4 * D] = (dya * bg * conv * _dsilu(z, sg)).astype(BF16)
        dconv = dya * bg * sz
        _acc_row(dcw_ref, 0, jnp.sum(dconv * v2, axis=0, keepdims=True))
        _acc_row(dcw_ref, 1, jnp.sum(dconv * v1, axis=0, keepdims=True))
        _acc_row(dcw_ref, 2, jnp.sum(dconv * v, axis=0, keepdims=True))
        after = carry[...]
        up1 = jnp.where(rows < tm - 1, pltpu.roll(dconv, tm - 1, 0), after[0:1, :])
        up2 = jnp.where(rows < tm - 2, pltpu.roll(dconv, tm - 2, 0),
                        jnp.where(rows == tm - 2, after[0:1, :], after[1:2, :]))
        carry[...] = dconv[0:8, :]
        dv = cw_ref[2:3, :] * dconv + cw_ref[1:2, :] * up1 + cw_ref[0:1, :] * up2
        dproj_ref[:, D:2 * D] = (dv * u).astype(BF16)
        dproj_ref[:, 2 * D:3 * D] = (dv * cg).astype(BF16)

        @pl.when(i == nt - 1)
        def _():
            for cp in _chip_exchange(part_refs, recv_refs, send, recv):
                cp.wait()

    rev = lambda i: (nt - 1 - i, 0)
    fix = lambda i: (0, 0)
    halo = lambda i: (jnp.maximum((nt - 1 - i) * (tm // HALO) - 1, 0), 0)
    dproj, dcw, *got = pl.pallas_call(
        body, name="a_bwd", grid=(nt,),
        in_specs=[pl.BlockSpec((tm, D), rev), pl.BlockSpec((tm, 4 * D), rev), pl.BlockSpec((HALO, 4 * D), halo),
                  pl.BlockSpec((8, D), fix), pl.BlockSpec((D, D), fix)] + ex_in,
        out_specs=[pl.BlockSpec((tm, 4 * D), rev), pl.BlockSpec((8, D), fix)] + ex_out,
        out_shape=[SDS((s, 4 * D), BF16), SDS((8, D), F32)] + ex_shape,
        scratch_shapes=[pltpu.VMEM((8, D), F32)] + ex_sems,
        compiler_params=_params(("arbitrary",)),
    )(doa, proj, proj, conv_w, w_out, *parts)
    return dproj, dcw, got


def _a_in_bwd(dproj, x, dh1, win_g, g_pre, tm, first, count, name, parts=(), before=None):
    s = x.shape[0]
    nt = count
    n = len(parts)
    ex_in, ex_out, ex_shape, ex_sems = _exchange_specs(parts) if n else ([], [], [], [])
    goes_on = before is not None

    def body(*refs):
        dp_ref, x_ref, dh_ref, w_ref, g_ref = refs[:5]
        part_refs = refs[5:5 + n]
        pos = 5 + n
        dg_before = refs[pos + 1] if goes_on else None
        pos += 2 * goes_on
        gx_ref, dg_ref = refs[pos:pos + 2]
        recv_refs = refs[pos + 2:pos + 2 + n]
        sems = refs[pos + 2 + n:]

        @pl.when(pl.program_id(0) == 0)
        def _():
            dg_ref[...] = dg_before[...] if goes_on else jnp.zeros_like(dg_ref)
            if n:
                for cp in _chip_exchange(part_refs, recv_refs, *sems):
                    cp.start()
        dn = _nt(dp_ref[:, 0:D], w_ref[0])
        for j in range(1, 4):
            dn = dn + _nt(dp_ref[:, D * j:D * (j + 1)], w_ref[j])
        xv = x_ref[...]
        r = _rms_scale(xv)
        xh = xv * r
        _acc_row(dg_ref, 0, jnp.sum(dn * xh, axis=0, keepdims=True))
        dxh = dn * g_ref[...]
        gx_ref[...] = dh_ref[...] + r * (dxh - xh * jnp.mean(dxh * xh, axis=-1, keepdims=True))

        if n:
            @pl.when(pl.program_id(0) == nt - 1)
            def _():
                for cp in _chip_exchange(part_refs, recv_refs, *sems):
                    cp.wait()

    row = lambda i: (first + i, 0)
    fix = lambda i: (0, 0)
    goes_on_in = [pl.BlockSpec(memory_space=pl.ANY), pl.BlockSpec((8, D), fix)] if goes_on else []
    gx, dg, *got = pl.pallas_call(
        body, name=name, grid=(nt,),
        in_specs=[pl.BlockSpec((tm, 4 * D), row), pl.BlockSpec((tm, D), row), pl.BlockSpec((tm, D), row),
                  pl.BlockSpec((4, D, D), lambda i: (0, 0, 0)), pl.BlockSpec((1, D), fix)] + ex_in + goes_on_in,
        out_specs=[pl.BlockSpec((tm, D), row), pl.BlockSpec((8, D), fix)] + ex_out,
        out_shape=[SDS((s, D), F32), SDS((8, D), F32)] + ex_shape,
        scratch_shapes=ex_sems,
        input_output_aliases={5 + n: 0} if goes_on else {},
        compiler_params=_params(("arbitrary",)),
    )(dproj, x, dh1, win_g, g_pre, *parts, *(before if goes_on else ()))
    return gx, dg, got


def _dw(a, b, tn, tmw, name):
    s, k = a.shape
    n = b.shape[1]

    def body(a_ref, b_ref, o_ref):
        @pl.when(pl.program_id(1) == 0)
        def _():
            o_ref[...] = jnp.zeros_like(o_ref)
        o_ref[0] += _tn(a_ref[...], b_ref[...])

    return pl.pallas_call(
        body, name=name, grid=(n // tn, s // tmw),
        in_specs=[pl.BlockSpec((tmw, k), lambda j, t: (t, 0)), pl.BlockSpec((tmw, tn), lambda j, t: (t, j))],
        out_specs=pl.BlockSpec((1, k, tn), lambda j, t: (j, 0, 0)),
        out_shape=SDS((n // tn, k, tn), F32),
        compiler_params=_params(("parallel", "arbitrary")),
    )(a, b)


def _sibling_exchange(name, to_sibling=(), shards=(), smalls=None):
    n_g, n_h = len(to_sibling), len(shards)
    has_small = smalls is not None

    def body(*refs):
        gs = refs[:n_g]
        pos = n_g + n_h
        small_in = refs[pos] if has_small else None
        pos += has_small
        rs, fs = refs[pos:pos + n_g], refs[pos + n_g:pos + n_g + n_h]
        pos += n_g + n_h
        small_all = refs[pos] if has_small else None
        pos += has_small
        dsend, drecv, ssend, srecv = refs[pos:]
        x, y, c = lax.axis_index("x"), lax.axis_index("y"), lax.axis_index("c")
        sibling = (x, y, 1 - c)
        sends, arrivals = [], []
        for a, (g, r) in enumerate(zip(gs, rs)):
            h = g.shape[1] // 2
            src = g.at[:, pl.ds(pl.multiple_of((1 - c) * h, 8), h), :]
            sends.append(pltpu.make_async_remote_copy(src_ref=src, dst_ref=r, send_sem=dsend.at[a], recv_sem=drecv.at[a],
                                                      device_id=sibling, device_id_type=MESH))
            arrivals.append(pltpu.make_async_remote_copy(src_ref=r, dst_ref=r, send_sem=dsend.at[a], recv_sem=drecv.at[a],
                                                         device_id=sibling, device_id_type=MESH))
        for b, full in enumerate(fs):
            h = full.shape[0] // 2
            mine = full.at[pl.ds(pl.multiple_of(c * h, 8), h)]
            theirs = full.at[pl.ds(pl.multiple_of((1 - c) * h, 8), h)]
            sends.append(pltpu.make_async_remote_copy(src_ref=mine, dst_ref=mine, send_sem=dsend.at[n_g + b],
                                                      recv_sem=drecv.at[n_g + b], device_id=sibling, device_id_type=MESH))
            arrivals.append(pltpu.make_async_remote_copy(src_ref=mine, dst_ref=theirs, send_sem=dsend.at[n_g + b],
                                                         recv_sem=drecv.at[n_g + b], device_id=sibling, device_id_type=MESH))
        if has_small:
            me = 4 * x + 2 * y + c
            small_all[me] = small_in[...]
            for rel in range(1, N_DEV):
                fx, fy, fc = rel >> 2, (rel >> 1) & 1, rel & 1
                peer = (x + fx - 2 * x * fx, y + fy - 2 * y * fy, c + fc - 2 * c * fc)
                sender = 4 * peer[0] + 2 * peer[1] + peer[2]
                sends.append(pltpu.make_async_remote_copy(
                    src_ref=small_in, dst_ref=small_all.at[me], send_sem=ssend.at[rel - 1], recv_sem=srecv.at[rel - 1],
                    device_id=peer, device_id_type=MESH))
                arrivals.append(pltpu.make_async_remote_copy(
                    src_ref=small_in, dst_ref=small_all.at[sender], send_sem=ssend.at[rel - 1], recv_sem=srecv.at[rel - 1],
                    device_id=peer, device_id_type=MESH))
        for cp in sends:
            cp.start()
        for cp in arrivals:
            cp.wait_recv()
        for cp in sends:
            cp.wait_send()

    anyspace = pl.BlockSpec(memory_space=pl.ANY)
    vm = pl.BlockSpec(memory_space=pltpu.VMEM)
    out_shape = [SDS((N_CHIPS, g.shape[1] // 2, g.shape[2]), F32) for g in to_sibling]
    out_shape += [SDS(full.shape, F32) for full in shards]
    if has_small:
        out_shape.append(SDS((N_DEV,) + smalls.shape, F32))
    n_d2d = max(n_g + n_h, 1)
    outs = pl.pallas_call(
        body, name=name, out_shape=out_shape,
        in_specs=[anyspace] * (n_g + n_h) + [vm] * has_small, out_specs=[anyspace] * (n_g + n_h) + [vm] * has_small,
        scratch_shapes=[pltpu.SemaphoreType.DMA((n_d2d,)), pltpu.SemaphoreType.DMA((n_d2d,)),
                        pltpu.SemaphoreType.DMA((N_DEV - 1,)), pltpu.SemaphoreType.DMA((N_DEV - 1,))],
        input_output_aliases={n_g + b: n_g + b for b in range(n_h)},
    )(*to_sibling, *shards, *([smalls] if has_small else []))
    return outs[:n_g], outs[n_g:n_g + n_h], (outs[n_g + n_h] if has_small else None)


def _add_sibling(where, g, r, name):
    _, rows, cols = g.shape
    h = rows // 2
    tr = min(h, 256)
    nh = h // tr

    def body(where_ref, g_ref, r_ref, t_ref, own_ref):
        t = g_ref[0] + r_ref[0]
        t_ref[0] = t.astype(BF16)

        @pl.when(pl.program_id(1) == where_ref[1])
        def _():
            own_ref[...] = t

    return pl.pallas_call(
        body, name=name,
        grid_spec=pltpu.PrefetchScalarGridSpec(
            num_scalar_prefetch=1, grid=(nh, N_CHIPS),
            in_specs=[pl.BlockSpec((1, tr, cols), lambda i, k, w: (k, w[0] * nh + i, 0)),
                      pl.BlockSpec((1, tr, cols), lambda i, k, w: (k, i, 0))],
            out_specs=[pl.BlockSpec((1, tr, cols), lambda i, k, w: (k, i, 0)),
                       pl.BlockSpec((tr, cols), lambda i, k, w: (i, 0))]),
        out_shape=[SDS((N_CHIPS, h, cols), BF16), SDS((h, cols), F32)],
        compiler_params=_params(("parallel", "arbitrary")),
    )(where, g, r)


def _add_chips(where, own, r, name):
    h, cols = own.shape
    tr = min(h, 256)
    nh = h // tr

    def body(where_ref, t_ref, r_ref, o_ref):
        del where_ref
        o_ref[...] = ((t_ref[...] + r_ref[0].astype(F32)) + r_ref[1].astype(F32)) + r_ref[2].astype(F32)

    return pl.pallas_call(
        body, name=name,
        grid_spec=pltpu.PrefetchScalarGridSpec(
            num_scalar_prefetch=1, grid=(nh,),
            in_specs=[pl.BlockSpec((tr, cols), lambda i, w: (i, 0)), pl.BlockSpec((3, tr, cols), lambda i, w: (0, i, 0))],
            out_specs=pl.BlockSpec((tr, cols), lambda i, w: (w[0] * nh + i, 0))),
        out_shape=SDS((2 * h, cols), F32),
        compiler_params=_params(("parallel",)),
    )(where, own, r)


def _sum_smalls(small_all):
    def body(all_ref, o_ref):
        acc = all_ref[0]
        for dev in range(1, N_DEV):
            acc = acc + all_ref[dev]
        o_ref[...] = acc

    return pl.pallas_call(
        body, name="sum_smalls", out_shape=SDS(small_all.shape[1:], F32),
        in_specs=[pl.BlockSpec(memory_space=pltpu.VMEM)], out_specs=pl.BlockSpec(memory_space=pltpu.VMEM),
    )(small_all)


def _adamw(g, w, m, v, name):
    rows, cols = g.shape
    tr = min(rows, 256)

    def body(g_ref, w_ref, m_ref, v_ref, d_ref, nm_ref, nv_ref):
        gv = g_ref[...]
        nm = ADAM_B1 * m_ref[...] + (1.0 - ADAM_B1) * gv
        nv = ADAM_B2 * v_ref[...] + (1.0 - ADAM_B2) * (gv * gv)
        nm_ref[...] = nm
        nv_ref[...] = nv
        m_hat = nm / (1.0 - ADAM_B1 ** ADAM_STEP)
        v_hat = nv / (1.0 - ADAM_B2 ** ADAM_STEP)
        d_ref[...] = -ADAM_LR * (m_hat / (jnp.sqrt(v_hat) + ADAM_EPS) + ADAM_WD * w_ref[...])

    spec = pl.BlockSpec((tr, cols), lambda i: (i, 0))
    return pl.pallas_call(
        body, name=name, grid=(rows // tr,), in_specs=[spec] * 4, out_specs=[spec] * 3,
        out_shape=[SDS(g.shape, F32)] * 3, compiler_params=_params(("parallel",)),
    )(g, w, m, v)


def _pad_rows(a, rows):
    return jnp.concatenate([a, jnp.zeros((rows - a.shape[0], a.shape[1]), a.dtype)], axis=0)


def _pad_cols(a, cols):
    return jnp.concatenate([a, jnp.zeros((a.shape[0], cols - a.shape[1]), a.dtype)], axis=1)


def kernel(x, a_pre_norm, a_w_in, a_conv_w, a_w_out, a_post_norm, kv_norm, w_kv, rel_bias, b_pre_norm, b_w_in, b_sinks, b_w_out, b_post_norm, loss_target, m_a_pre_norm, m_a_w_in, m_a_conv_w, m_a_w_out, m_a_post_norm, m_kv_norm, m_w_kv, m_rel_bias, m_b_pre_norm, m_b_w_in, m_b_sinks, m_b_w_out, m_b_post_norm, v_a_pre_norm, v_a_w_in, v_a_conv_w, v_a_w_out, v_a_post_norm, v_kv_norm, v_w_kv, v_rel_bias, v_b_pre_norm, v_b_w_in, v_b_sinks, v_b_w_out, v_b_post_norm):
    seq = x.shape[1]
    xs = x.reshape(seq, D)
    tgt = loss_target.reshape(seq, D)
    chip = 2 * lax.axis_index("x") + lax.axis_index("y")
    core = lax.axis_index("c")
    tm = _tile(seq, 512)
    tm_mix = _tile(seq, 256)
    tmw = _tile(seq, 1024)

    shards = [a_w_in[0], a_w_out[0], w_kv, b_w_in[0], b_w_out[0]]
    small_w = _pad_rows(jnp.concatenate([a_pre_norm, a_conv_w[0], a_post_norm], axis=0), 8)
    win_g, *own_only, small_g = _gather_weights(shards, small_w, 1)
    small_full = small_g.transpose(1, 0, 2).reshape(8, D)
    g_apre, conv_w, g_apost = small_full[0:1], _pad_rows(small_full[1:4], 8), small_full[4:5]
    g_kv = kv_norm.reshape(1, D)

    proj, n1, (wouta_g, wkv_g, wbin_g, woutb_g) = _a_in(xs, g_apre, win_g, tm, own_only)
    wouta = wouta_g.reshape(D, D)
    wkv = wkv_g.reshape(D, 2 * KV_W)
    woutb = woutb_g.reshape(D, D)
    ya, oa, h1 = _a_mix(proj, xs, conv_w, wouta, g_apost, tm_mix)
    nk, nb, kv, q, zb = _b_in(h1, g_kv, b_pre_norm, wkv, wbin_g, tm)
    tab = _bias_table(rel_bias, b_sinks.reshape(N_HEADS))
    att, stats = _attn_fwd(q, kv, tab)
    ob, dy2, dh2, dqz, datt, loss_acc, dg_bpost = _mid(att, zb, h1, tgt, woutb, b_post_norm, tm)

    dqz, dkv, dtab = _attn_bwd(q, kv, datt, stats, tab, dqz)
    dh1, doa, dg_b = _b_bwd(dqz, dkv, h1, dh2, oa, wbin_g, wkv, g_kv, b_pre_norm, g_apost, tm)
    where = jnp.stack([core, chip]).astype(jnp.int32)
    dw_outa = _dw(ya, doa, D, tmw, "dw_a_out").reshape(N_CHIPS, D // 4, D)
    dw_kv = _dw(nk, dkv, 2 * KV_W, tmw, "dw_kv").reshape(N_CHIPS, D // 4, 2 * KV_W)
    dw_bin = _dw(nb, dqz, 512, tmw, "dw_b_in")
    dw_outb = _dw(ob, dy2, D, tmw, "dw_b_out").reshape(N_CHIPS, D // 4, D)
    grads1 = [dw_outa, dw_kv, dw_bin, dw_outb]
    names1 = ["a_w_out", "w_kv", "b_w_in", "b_w_out"]
    from_sibling1, _, _ = _sibling_exchange("to_sibling_1", to_sibling=grads1)
    sums1 = [_add_sibling(where, g, r, "add_sibling_" + nm) for g, r, nm in zip(grads1, from_sibling1, names1)]
    dproj, dconv_w, from_chips1 = _a_bwd(doa, proj, conv_w, wouta, tm_mix, [t for t, _ in sums1])
    shards1 = [_add_chips(where, own, r, "add_chips_" + nm) for (_, own), r, nm in zip(sums1, from_chips1, names1)]
    dw_in = _dw(n1, dproj, D, tmw, "dw_a_in")
    from_sibling2, (g_wouta, g_wkv, g_wbin, g_woutb), _ = _sibling_exchange(
        "to_sibling_2", to_sibling=[dw_in], shards=shards1)
    part2, own2 = _add_sibling(where, dw_in, from_sibling2[0], "add_sibling_a_w_in")
    nt = seq // tm
    nt_first = max(nt - max(nt // 4, 1), 1)
    gx_first, dg_first, from_chips2 = _a_in_bwd(dproj, xs, dh1, win_g, g_apre, tm, 0, nt_first, "a_in_bwd_first",
                                                parts=[part2])
    grad_x, dg_apre, _ = _a_in_bwd(dproj, xs, dh1, win_g, g_apre, tm, nt_first, nt - nt_first, "a_in_bwd_rest",
                                   before=(gx_first, dg_first))
    shard2 = _add_chips(where, own2, from_chips2[0], "add_chips_a_w_in")
    drel, dsink = _bias_fold(dtab)

    smalls = jnp.concatenate([
        dg_apre[0:1], dconv_w[0:3], dg_b[2:3], dg_b[0:1], dg_b[1:2], dg_bpost[0:1],
        _pad_cols(drel[:, 0:N_HEADS].reshape(1, N_BUCKETS * N_HEADS), D), _pad_cols(dsink[0:1], D),
        _pad_cols(loss_acc[0:1], D), jnp.zeros((SMALL_ROWS - 11, D), F32)], axis=0)
    _, (g_win,), small_all = _sibling_exchange("share_last", shards=[shard2], smalls=smalls)
    tot = _sum_smalls(small_all)

    big = {}
    for nm, g, w, m, v in [("a_w_in", g_win, a_w_in, m_a_w_in, v_a_w_in), ("a_w_out", g_wouta, a_w_out, m_a_w_out, v_a_w_out),
                           ("w_kv", g_wkv, w_kv, m_w_kv, v_w_kv), ("b_w_in", g_wbin, b_w_in, m_b_w_in, v_b_w_in),
                           ("b_w_out", g_woutb, b_w_out, m_b_w_out, v_b_w_out)]:
        shp = w.shape
        two = (shp[-2], shp[-1])
        d, nm_, nv_ = _adamw(g, w.reshape(two), m.reshape(two), v.reshape(two), "adamw_" + nm)
        big[nm] = (g.reshape(shp), d.reshape(shp), nm_.reshape(shp), nv_.reshape(shp))

    col0 = chip * (D // 4)
    sharded = lax.dynamic_slice(tot, (0, col0), (8, D // 4))
    g_shard = sharded
    w_shard = small_w
    m_shard = _pad_rows(jnp.concatenate([m_a_pre_norm, m_a_conv_w[0], m_a_post_norm], axis=0), 8)
    v_shard = _pad_rows(jnp.concatenate([v_a_pre_norm, v_a_conv_w[0], v_a_post_norm], axis=0), 8)
    ds_, ms_, vs_ = _adamw(g_shard, w_shard, m_shard, v_shard, "adamw_small_sharded")

    def rep_pack(kvn, bpre, bpost, rel, snk):
        rows = [kvn.reshape(1, D), bpre.reshape(1, D), bpost.reshape(1, D),
                _pad_cols(rel.reshape(1, N_BUCKETS * N_HEADS), D), _pad_cols(snk.reshape(1, N_HEADS), D)]
        return jnp.concatenate(rows + [jnp.zeros((3, D), F32)], axis=0)

    g_rep = tot[5:13]
    w_rep = rep_pack(kv_norm, b_pre_norm, b_post_norm, rel_bias, b_sinks)
    m_rep = rep_pack(m_kv_norm, m_b_pre_norm, m_b_post_norm, m_rel_bias, m_b_sinks)
    v_rep = rep_pack(v_kv_norm, v_b_pre_norm, v_b_post_norm, v_rel_bias, v_b_sinks)
    dr_, mr_, vr_ = _adamw(g_rep, w_rep, m_rep, v_rep, "adamw_small_replicated")

    def unshard(p):
        return {"a_pre_norm": p[0:1], "a_conv_w": p[1:4].reshape(1, 3, D // 4), "a_post_norm": p[4:5]}

    def unrep(p):
        return {"kv_norm": p[0], "b_pre_norm": p[1:2], "b_post_norm": p[2:3],
                "rel_bias": p[3, 0:N_BUCKETS * N_HEADS].reshape(N_BUCKETS, N_HEADS), "b_sinks": p[4:5, 0:N_HEADS]}

    order = ["a_pre_norm", "a_w_in", "a_conv_w", "a_w_out", "a_post_norm", "kv_norm", "w_kv", "rel_bias",
             "b_pre_norm", "b_w_in", "b_sinks", "b_w_out", "b_post_norm"]
    outs = []
    for which, sh, rp in [(0, g_shard, g_rep), (1, ds_, dr_), (2, ms_, mr_), (3, vs_, vr_)]:
        small = {**unshard(sh), **unrep(rp)}
        for nm in order:
            outs.append(big[nm][which] if nm in big else small[nm])
    loss = 0.5 * tot[10, 0]
    return (loss, grad_x.reshape(x.shape), *outs)
```

```python
import functools
import math

import jax
import jax.numpy as jnp
from jax import lax
from jax.experimental import pallas as pl
from jax.experimental.pallas import tpu as pltpu

F32 = jnp.float32
BF16 = jnp.bfloat16
MESH = pl.DeviceIdType.MESH
SDS = jax.ShapeDtypeStruct

D = 1024
HEAD_DIM = 64
N_HEADS = 16
N_KV = 2
GROUP = 8
KV_W = 128
BLK = 128
N_BUCKETS = 32
MAX_EXACT = 16
MAX_DISTANCE = 128
EPS = 1e-6
NEG_INF = -1e30
Q_SCALE = HEAD_DIM ** -0.5

ADAM_LR = 0.001
ADAM_B1 = 0.9
ADAM_B2 = 0.999
ADAM_EPS = 1e-08
ADAM_WD = 0.01
ADAM_STEP = 10

N_CHIPS = 4
N_DEV = 8
VMEM_LIMIT = 56 * 1024 * 1024
SMALL_ROWS = 48
LOSS_ROW = 6
SMALL_PLACES = {
    "a_pre_norm": (True, 0, (1, D // 4)), "a_conv_w": (True, 8, (3, D // 4)), "a_post_norm": (True, 1, (1, D // 4)),
    "kv_norm": (False, 2, (1, D)), "rel_bias": (False, 16, (N_BUCKETS, N_HEADS)), "b_pre_norm": (False, 3, (1, D)),
    "b_sinks": (False, 5, (1, N_HEADS)), "b_post_norm": (False, 4, (1, D)),
}
HALO = 16


def _bucket_thresholds():
    def bucket(d):
        big = MAX_EXACT + int(math.log(d / MAX_EXACT) / math.log(MAX_DISTANCE / MAX_EXACT)
                              * (N_BUCKETS - MAX_EXACT))
        return d if d < MAX_EXACT else min(big, N_BUCKETS - 1)
    out = []
    for b in range(MAX_EXACT + 1, N_BUCKETS):
        out.append(min(d for d in range(MAX_EXACT, MAX_DISTANCE) if bucket(d) >= b))
    return tuple(out)


BUCKET_THRESHOLDS = _bucket_thresholds()


def _params(semantics=None, vmem=VMEM_LIMIT):
    return pltpu.CompilerParams(dimension_semantics=semantics, vmem_limit_bytes=vmem)


def _tile(n, pref):
    return pref if n >= 2 * pref else max(n // 2, 8)


def _rms_scale(v):
    return lax.rsqrt(jnp.mean(v * v, axis=-1, keepdims=True) + EPS)


def _nt(a, b):
    return lax.dot_general(a, b, (((1,), (1,)), ((), ())), preferred_element_type=F32)


def _tn(a, b):
    return lax.dot_general(a, b, (((0,), (0,)), ((), ())), preferred_element_type=F32)


def _nn(a, b):
    return jnp.dot(a, b, preferred_element_type=F32)


def _silu_parts(z):
    sg = jax.nn.sigmoid(z)
    return sg, z * sg


def _dsilu(z, sg):
    return sg * (1.0 + z * (1.0 - sg))


def _acc_row(ref, row, val):
    ref[row:row + 1, :] += val


def _gather_copies(outs, splits, ici_send, ici_recv, d2d_send, d2d_recv):
    x, y, c = lax.axis_index("x"), lax.axis_index("y"), lax.axis_index("c")
    k = 2 * x + y
    sibling = (x, y, 1 - c)

    def part(o_ref, chip, core, split):
        if not split:
            return o_ref.at[chip]
        h = o_ref.shape[1] // 2
        return o_ref.at[chip, pl.ds(pl.multiple_of(core * h, 16), h)]

    def remote(ref, a, j, sems, to):
        return pltpu.make_async_remote_copy(src_ref=ref, dst_ref=ref, send_sem=sems[0].at[3 * a + j],
                                            recv_sem=sems[1].at[3 * a + j], device_id=to, device_id_type=MESH)

    copies = []
    for a, (o_ref, split) in enumerate(zip(outs, splits)):
        for j, (px, py) in enumerate([(x, 1 - y), (1 - x, y), (1 - x, 1 - y)]):
            kj = 2 * px + py
            ici, d2d = (ici_send, ici_recv), (d2d_send, d2d_recv)
            copies.append((remote(part(o_ref, k, c, split), a, j, ici, (px, py, c)),
                           remote(part(o_ref, kj, c, split), a, j, ici, (px, py, c)),
                           remote(part(o_ref, kj, c, split), a, j, d2d, sibling) if split else None,
                           remote(part(o_ref, kj, 1 - c, split), a, j, d2d, sibling) if split else None))
    return copies


def _gather_sems(n):
    return [pltpu.SemaphoreType.DMA((3 * n,)) for _ in range(4)]


def _gather_weights(shards, small, n_now):
    n = len(shards)

    def body(*refs):
        ins, small_in = refs[:n], refs[n]
        outs, small_out = refs[n + 1:2 * n + 1], refs[2 * n + 1]
        sems = refs[2 * n + 2:]
        k = 2 * lax.axis_index("x") + lax.axis_index("y")
        for i_ref, o_ref in zip(ins, outs):
            o_ref[k] = i_ref[...].astype(BF16)
        small_out[k] = small_in[...]
        copies = _gather_copies(list(outs[:n_now]) + [small_out], [True] * n_now + [False], *sems)
        for send, _, _, _ in copies:
            send.start()
        for _, arrival, forward, _ in copies:
            arrival.wait_recv()
            if forward is not None:
                forward.start()
        for send, _, forward, forwarded in copies:
            if forward is not None:
                forwarded.wait_recv()
                forward.wait_send()
            send.wait_send()

    vm = pl.BlockSpec(memory_space=pltpu.VMEM)
    out_shape = [SDS((N_CHIPS,) + s.shape, BF16) for s in shards] + [SDS((N_CHIPS,) + small.shape, F32)]
    return pl.pallas_call(
        body, name="gather_weights", out_shape=out_shape,
        in_specs=[vm] * (n + 1), out_specs=[vm] * (n + 1),
        scratch_shapes=_gather_sems(n_now + 1),
        compiler_params=pltpu.CompilerParams(vmem_limit_bytes=VMEM_LIMIT),
    )(*shards, small)


def _a_in(x, g_pre, win_g, tm, later):
    s = x.shape[0]
    nt = s // tm
    n = len(later)

    def body(*refs):
        x_ref, g_ref, w_ref = refs[:3]
        proj_ref, n1_ref = refs[3 + n:5 + n]
        gathered = refs[5 + n:5 + 2 * n]
        sems = refs[5 + 2 * n:]
        i = pl.program_id(0)

        @pl.when(i == 0)
        def _():
            for send, _, _, _ in _gather_copies(gathered, [True] * n, *sems):
                send.start()
        xv = x_ref[...]
        n1 = (xv * _rms_scale(xv) * g_ref[...]).astype(BF16)
        n1_ref[...] = n1
        for j in range(4):
            proj_ref[:, D * j:D * (j + 1)] = _nn(n1, w_ref[j]).astype(BF16)

        @pl.when(i == (3 * nt) // 4)
        def _():
            for _, arrival, forward, _ in _gather_copies(gathered, [True] * n, *sems):
                arrival.wait_recv()
                forward.start()

        @pl.when(i == nt - 1)
        def _():
            for send, _, forward, forwarded in _gather_copies(gathered, [True] * n, *sems):
                forwarded.wait_recv()
                forward.wait_send()
                send.wait_send()

    row = lambda i: (i, 0)
    anyspace = pl.BlockSpec(memory_space=pl.ANY)
    proj, n1, *gathered = pl.pallas_call(
        body, name="a_in", grid=(nt,),
        in_specs=[pl.BlockSpec((tm, D), row), pl.BlockSpec((1, D), lambda i: (0, 0)),
                  pl.BlockSpec((4, D, D), lambda i: (0, 0, 0))] + [anyspace] * n,
        out_specs=[pl.BlockSpec((tm, 4 * D), row), pl.BlockSpec((tm, D), row)] + [anyspace] * n,
        out_shape=[SDS((s, 4 * D), BF16), SDS((s, D), BF16)] + [SDS(w.shape, w.dtype) for w in later],
        scratch_shapes=_gather_sems(n),
        input_output_aliases={3 + a: 2 + a for a in range(n)},
        compiler_params=_params(("arbitrary",)),
    )(x, g_pre, win_g, *later)
    return proj, n1, gathered


def _shift_rows(v, last, second_last, rows):
    v1 = jnp.where(rows >= 1, pltpu.roll(v, 1, 0), last)
    v2 = jnp.where(rows >= 2, pltpu.roll(v, 2, 0), jnp.where(rows == 1, last, second_last))
    return v1, v2


def _a_mix(proj, x, conv_w, w_out, g_post, tm):
    s = x.shape[0]

    def body(proj_ref, x_ref, cw_ref, w_ref, g_ref, ya_ref, oa_ref, h1_ref, carry):
        @pl.when(pl.program_id(0) == 0)
        def _():
            carry[...] = jnp.zeros_like(carry)
        v = proj_ref[:, D:2 * D].astype(F32) * proj_ref[:, 2 * D:3 * D].astype(F32)
        rows = lax.broadcasted_iota(jnp.int32, (tm, D), 0)
        before = carry[...]
        v1, v2 = _shift_rows(v, before[7:8, :], before[6:7, :], rows)
        carry[...] = v[tm - 8:tm, :]
        conv = cw_ref[0:1, :] * v2 + cw_ref[1:2, :] * v1 + cw_ref[2:3, :] * v
        _, sz = _silu_parts(proj_ref[:, 3 * D:4 * D].astype(F32))
        ya = (proj_ref[:, 0:D].astype(F32) * conv * sz).astype(BF16)
        ya_ref[...] = ya
        oa = _nn(ya, w_ref[...])
        oa_ref[...] = oa
        h1_ref[...] = x_ref[...] + oa * _rms_scale(oa) * g_ref[...]

    row = lambda i: (i, 0)
    fix = lambda i: (0, 0)
    return pl.pallas_call(
        body, name="a_mix", grid=(s // tm,),
        in_specs=[pl.BlockSpec((tm, 4 * D), row), pl.BlockSpec((tm, D), row), pl.BlockSpec((8, D), fix),
                  pl.BlockSpec((D, D), fix), pl.BlockSpec((1, D), fix)],
        out_specs=[pl.BlockSpec((tm, D), row)] * 3,
        out_shape=[SDS((s, D), BF16), SDS((s, D), F32), SDS((s, D), F32)],
        scratch_shapes=[pltpu.VMEM((8, D), F32)],
        compiler_params=_params(("arbitrary",)),
    )(proj, x, conv_w, w_out, g_post)


def _b_in(h1, g_kv, g_pre, w_kv, wbin_g, tm):
    s = h1.shape[0]

    def body(h_ref, gk_ref, gb_ref, wkv_ref, wb_ref, nk_ref, nb_ref, kv_ref, q_ref, z_ref):
        h = h_ref[...]
        hh = h * _rms_scale(h)
        nk = (hh * gk_ref[...]).astype(BF16)
        nb = (hh * gb_ref[...]).astype(BF16)
        nk_ref[...] = nk
        nb_ref[...] = nb
        kv_ref[...] = _nn(nk, wkv_ref[...]).astype(BF16)
        for j in range(2):
            q_ref[:, 512 * j:512 * (j + 1)] = (_nn(nb, wb_ref[j]) * Q_SCALE).astype(BF16)
            z_ref[:, 512 * j:512 * (j + 1)] = _nn(nb, wb_ref[2 + j]).astype(BF16)

    row = lambda i: (i, 0)
    fix = lambda i: (0, 0)
    return pl.pallas_call(
        body, name="b_in", grid=(s // tm,),
        in_specs=[pl.BlockSpec((tm, D), row), pl.BlockSpec((1, D), fix), pl.BlockSpec((1, D), fix),
                  pl.BlockSpec((D, 2 * KV_W), fix), pl.BlockSpec((4, D, 512), lambda i: (0, 0, 0))],
        out_specs=[pl.BlockSpec((tm, D), row), pl.BlockSpec((tm, D), row), pl.BlockSpec((tm, 2 * KV_W), row),
                   pl.BlockSpec((tm, D), row), pl.BlockSpec((tm, D), row)],
        out_shape=[SDS((s, D), BF16), SDS((s, D), BF16), SDS((s, 2 * KV_W), BF16), SDS((s, D), BF16),
                   SDS((s, D), BF16)],
        compiler_params=_params(("parallel",)),
    )(h1, g_kv, g_pre, w_kv, wbin_g)


def _band_buckets():
    q = lax.broadcasted_iota(jnp.int32, (BLK, 2 * BLK), 0)
    k = lax.broadcasted_iota(jnp.int32, (BLK, 2 * BLK), 1)
    dist = q + BLK - k
    bucket = jnp.where(dist < MAX_EXACT, dist, MAX_EXACT)
    for t in BUCKET_THRESHOLDS:
        bucket = bucket + jnp.where(dist >= t, 1, 0)
    in_window = (dist >= 0) & (dist < BLK)
    return jnp.where(in_window, bucket, -1)


def _head_place(h):
    kh, j, e = h // GROUP, (h % GROUP) // 2, h % 2
    return kh, slice(BLK * j, BLK * (j + 1)), slice(2 * BLK * e, 2 * BLK * (e + 1))


def _bias_table(rel_bias, sinks):
    def body(rb_ref, sink_ref, tab_ref):
        bucket = _band_buckets()
        col = lax.broadcasted_iota(jnp.int32, (BLK, 2 * BLK), 1)
        for h in range(N_HEADS):
            acc = jnp.where(bucket < 0, NEG_INF, 0.0).astype(F32)
            for b in range(N_BUCKETS):
                acc = jnp.where(bucket == b, rb_ref[b, h], acc)
            acc = jnp.where(col == 0, sink_ref[h], acc)
            kh, rows, cols = _head_place(h)
            tab_ref[1, kh, rows, cols] = acc
            tab_ref[0, kh, rows, cols] = jnp.where((col > 0) & (col < BLK), NEG_INF, acc)

    return pl.pallas_call(
        body, name="bias_table", out_shape=SDS((2, N_KV, 4 * BLK, 4 * BLK), F32),
        in_specs=[pl.BlockSpec(memory_space=pltpu.SMEM), pl.BlockSpec(memory_space=pltpu.SMEM)],
        out_specs=pl.BlockSpec(memory_space=pltpu.VMEM),
    )(rel_bias, sinks)


def _bias_fold(dtab):
    def body(dtab_ref, out_ref, dsink_ref):
        bucket = _band_buckets()
        row = lax.broadcasted_iota(jnp.int32, (N_BUCKETS, 128), 0)
        lane = lax.broadcasted_iota(jnp.int32, (N_BUCKETS, 128), 1)
        row8 = lax.broadcasted_iota(jnp.int32, (8, 128), 0)
        lane8 = lax.broadcasted_iota(jnp.int32, (8, 128), 1)
        acc = jnp.zeros((N_BUCKETS, 128), F32)
        dsink = jnp.zeros((8, 128), F32)
        for h in range(N_HEADS):
            kh, rows, cols = _head_place(h)
            dt = dtab_ref[kh, rows, cols]
            for b in range(N_BUCKETS):
                val = jnp.sum(jnp.where(bucket == b, dt, 0.0))
                acc = acc + jnp.where((row == b) & (lane == h), val, 0.0)
            dsink = dsink + jnp.where((row8 == 0) & (lane8 == h), jnp.sum(dt[:, 0:1]), 0.0)
        out_ref[...] = acc
        dsink_ref[...] = dsink

    vm = pl.BlockSpec(memory_space=pltpu.VMEM)
    return pl.pallas_call(
        body, name="bias_fold", out_shape=[SDS((N_BUCKETS, 128), F32), SDS((8, 128), F32)],
        in_specs=[vm], out_specs=[vm, vm],
    )(dtab)


def _pair_operands(prev, cur):
    t = jnp.concatenate([prev, cur], axis=0).astype(F32)
    t = jnp.where(lax.broadcasted_iota(jnp.int32, t.shape, 0) == 0, 0.0, t)
    tr = pltpu.roll(t, HEAD_DIM, 1)
    lo = lax.broadcasted_iota(jnp.int32, t.shape, 1) < HEAD_DIM
    zero = jnp.zeros_like(t)
    head0 = jnp.concatenate([jnp.where(lo, t, zero), jnp.where(lo, zero, tr)], axis=0).astype(BF16)
    head1 = jnp.concatenate([jnp.where(lo, tr, zero), jnp.where(lo, zero, t)], axis=0).astype(BF16)
    return head0, head1


def _pair_fold(d0, d1):
    lo = lax.broadcasted_iota(jnp.int32, (2 * BLK, KV_W), 1) < HEAD_DIM
    zero = jnp.zeros((2 * BLK, KV_W), F32)
    g0 = jnp.where(lo, d0[0:256], zero) + pltpu.roll(jnp.where(lo, zero, d0[256:512]), HEAD_DIM, 1)
    g1 = pltpu.roll(jnp.where(lo, d1[0:256], zero), HEAD_DIM, 1) + jnp.where(lo, zero, d1[256:512])
    return jnp.where(lax.broadcasted_iota(jnp.int32, (2 * BLK, KV_W), 0) == 0, 0.0, g0 + g1)


def _stack_pairs(ref, kh):
    return jnp.concatenate([ref[:, 128 * (4 * kh + j):128 * (4 * kh + j + 1)] for j in range(4)], axis=0)


def _table_spec():
    return pl.BlockSpec((1, N_KV, 4 * BLK, 4 * BLK), lambda n: (jnp.minimum(n, 1), 0, 0, 0))


def _attn_fwd(q, kv, tab):
    s = q.shape[0]

    def body(q_ref, kp_ref, kc_ref, vp_ref, vc_ref, tab_ref, att_ref, stats_ref):
        k2 = _pair_operands(kp_ref[...], kc_ref[...])
        v2 = _pair_operands(vp_ref[...], vc_ref[...])
        lane = lax.broadcasted_iota(jnp.int32, (BLK, 128), 1)
        stats = jnp.zeros((BLK, 128), F32)
        for kh in range(N_KV):
            sc = _nt(_stack_pairs(q_ref, kh), k2[kh])
            ps = []
            for e in range(2):
                lg = sc[:, 256 * e:256 * (e + 1)] + tab_ref[0, kh, :, 256 * e:256 * (e + 1)]
                m = jnp.max(lg, axis=-1, keepdims=True)
                ex = jnp.exp(lg - m)
                den = jnp.sum(ex, axis=-1, keepdims=True)
                ps.append(ex * (1.0 / den))
                lse = m + jnp.log(den)
                for j in range(4):
                    stats = jnp.where(lane == GROUP * kh + 2 * j + e, lse[BLK * j:BLK * (j + 1)], stats)
            out = _nn(jnp.concatenate(ps, axis=1).astype(BF16), v2[kh])
            for j in range(4):
                att_ref[:, 128 * (4 * kh + j):128 * (4 * kh + j + 1)] = out[BLK * j:BLK * (j + 1)].astype(BF16)
        stats_ref[...] = stats

    cur = lambda n: (n, 0)
    prev = lambda n: (jnp.maximum(n - 1, 0), 0)
    return pl.pallas_call(
        body, name="attn_fwd", grid=(s // BLK,),
        in_specs=[pl.BlockSpec((BLK, D), cur),
                  pl.BlockSpec((BLK, KV_W), prev), pl.BlockSpec((BLK, KV_W), cur),
                  pl.BlockSpec((BLK, KV_W), lambda n: (jnp.maximum(n - 1, 0), 1)),
                  pl.BlockSpec((BLK, KV_W), lambda n: (n, 1)), _table_spec()],
        out_specs=[pl.BlockSpec((BLK, D), cur), pl.BlockSpec((BLK, 128), cur)],
        out_shape=[SDS((s, D), BF16), SDS((s, 128), F32)],
        compiler_params=_params(("parallel",)),
    )(q, kv, kv, kv, kv, tab)


def _mid(att, zb, h1, tgt, w_out, g_post, tm):
    s = att.shape[0]

    def body(att_ref, z_ref, h1_ref, t_ref, w_ref, g_ref,
             ob_ref, dy_ref, dh_ref, dqz_ref, datt_ref, loss_ref, dg_ref):
        @pl.when(pl.program_id(0) == 0)
        def _():
            loss_ref[...] = jnp.zeros_like(loss_ref)
            dg_ref[...] = jnp.zeros_like(dg_ref)
        att = att_ref[...].astype(F32)
        z = z_ref[...].astype(F32)
        sg, sz = _silu_parts(z)
        ob = (att * sz).astype(BF16)
        ob_ref[...] = ob
        y2 = _nn(ob, w_ref[...])
        r2 = _rms_scale(y2)
        yh = y2 * r2
        g = g_ref[...]
        err = (h1_ref[...] + yh * g) - t_ref[...]
        loss_ref[...] += jnp.sum(jnp.sum(err * err, axis=-1, keepdims=True) / D)
        dh = err / D
        dh_ref[...] = dh
        _acc_row(dg_ref, 0, jnp.sum(dh * yh, axis=0, keepdims=True))
        dyh = dh * g
        dy = (r2 * (dyh - yh * jnp.mean(dyh * yh, axis=-1, keepdims=True))).astype(BF16)
        dy_ref[...] = dy
        dob = _nt(dy, w_ref[...])
        datt_ref[...] = (dob * sz).astype(BF16)
        dqz_ref[...] = (dob * att * _dsilu(z, sg)).astype(BF16)

    row = lambda i: (i, 0)
    fix = lambda i: (0, 0)
    return pl.pallas_call(
        body, name="mid", grid=(s // tm,),
        in_specs=[pl.BlockSpec((tm, D), row)] * 4 + [pl.BlockSpec((D, D), fix), pl.BlockSpec((1, D), fix)],
        out_specs=[pl.BlockSpec((tm, D), row), pl.BlockSpec((tm, D), row), pl.BlockSpec((tm, D), row),
                   pl.BlockSpec((tm, D), lambda i: (i, 1)), pl.BlockSpec((tm, D), row),
                   pl.BlockSpec((8, 128), fix), pl.BlockSpec((8, D), fix)],
        out_shape=[SDS((s, D), BF16), SDS((s, D), BF16), SDS((s, D), F32), SDS((s, 2 * D), BF16),
                   SDS((s, D), BF16), SDS((8, 128), F32), SDS((8, D), F32)],
        compiler_params=_params(("arbitrary",)),
    )(att, zb, h1, tgt, w_out, g_post)


def _attn_bwd(q, kv, datt, stats, tab, dqz):
    s = q.shape[0]
    nb = s // BLK

    def body(q_ref, kp_ref, kc_ref, vp_ref, vc_ref, da_ref, st_ref, tab_ref, dqz_in,
             dq_ref, dkv_ref, dtab_ref, dk_carry, dv_carry):
        del dqz_in
        n = pl.program_id(0)

        @pl.when(n == 0)
        def _():
            dtab_ref[...] = jnp.zeros_like(dtab_ref)
            dk_carry[...] = jnp.zeros_like(dk_carry)
            dv_carry[...] = jnp.zeros_like(dv_carry)

        @pl.when(n < nb)
        def _():
            k2 = _pair_operands(kp_ref[...], kc_ref[...])
            v2 = _pair_operands(vp_ref[...], vc_ref[...])
            lane = lax.broadcasted_iota(jnp.int32, (BLK, 128), 1)
            stats = st_ref[...]
            dk2, dv2 = [], []
            for kh in range(N_KV):
                qs = _stack_pairs(q_ref, kh)
                das = _stack_pairs(da_ref, kh)
                sc = _nt(qs, k2[kh])
                dp = _nt(das, v2[kh])
                ps, dss = [], []
                for e in range(2):
                    heads = [GROUP * kh + 2 * j + e for j in range(4)]
                    lse = jnp.concatenate([jnp.sum(jnp.where(lane == h, stats, 0.0), axis=-1, keepdims=True)
                                           for h in heads], axis=0)
                    cols = slice(256 * e, 256 * (e + 1))
                    p = jnp.exp(sc[:, cols] + tab_ref[0, kh, :, cols] - lse)
                    delta = jnp.sum(p * dp[:, cols], axis=-1, keepdims=True)
                    ds = p * (dp[:, cols] - delta)
                    dtab_ref[kh, :, cols] += ds
                    ps.append(p)
                    dss.append(ds)
                p2 = jnp.concatenate(ps, axis=1).astype(BF16)
                ds2 = jnp.concatenate(dss, axis=1).astype(BF16)
                dq = _nn(ds2, k2[kh]) * Q_SCALE
                for j in range(4):
                    dq_ref[:, 128 * (4 * kh + j):128 * (4 * kh + j + 1)] = dq[BLK * j:BLK * (j + 1)].astype(BF16)
                dk2.append(_tn(ds2, qs))
                dv2.append(_tn(p2, das))
            dkk = _pair_fold(dk2[0], dk2[1])
            dvv = _pair_fold(dv2[0], dv2[1])
            dkv_ref[:, 0:KV_W] = (dk_carry[...] + dkk[0:BLK]).astype(BF16)
            dkv_ref[:, KV_W:2 * KV_W] = (dv_carry[...] + dvv[0:BLK]).astype(BF16)
            dk_carry[...] = dkk[BLK:2 * BLK]
            dv_carry[...] = dvv[BLK:2 * BLK]

        @pl.when(n == nb)
        def _():
            dkv_ref[:, 0:KV_W] = dk_carry[...].astype(BF16)
            dkv_ref[:, KV_W:2 * KV_W] = dv_carry[...].astype(BF16)

    cur = lambda n: (jnp.minimum(n, nb - 1), 0)
    prev = lambda n: (jnp.clip(n - 1, 0, nb - 1), 0)
    return pl.pallas_call(
        body, name="attn_bwd", grid=(nb + 1,),
        in_specs=[pl.BlockSpec((BLK, D), cur),
                  pl.BlockSpec((BLK, KV_W), prev), pl.BlockSpec((BLK, KV_W), cur),
                  pl.BlockSpec((BLK, KV_W), lambda n: (jnp.clip(n - 1, 0, nb - 1), 1)),
                  pl.BlockSpec((BLK, KV_W), lambda n: (jnp.minimum(n, nb - 1), 1)),
                  pl.BlockSpec((BLK, D), cur), pl.BlockSpec((BLK, 128), cur), _table_spec(),
                  pl.BlockSpec(memory_space=pl.ANY)],
        out_specs=[pl.BlockSpec((BLK, D), cur), pl.BlockSpec((BLK, 2 * KV_W), prev),
                   pl.BlockSpec((N_KV, 4 * BLK, 4 * BLK), lambda n: (0, 0, 0))],
        out_shape=[SDS((s, 2 * D), BF16), SDS((s, 2 * KV_W), BF16), SDS((N_KV, 4 * BLK, 4 * BLK), F32)],
        scratch_shapes=[pltpu.VMEM((BLK, KV_W), F32), pltpu.VMEM((BLK, KV_W), F32)],
        input_output_aliases={8: 0},
        compiler_params=_params(("arbitrary",)),
    )(q, kv, kv, kv, kv, datt, stats, tab, dqz)


def _b_bwd(dqz, dkv, h1, dh2, oa, wbin_g, w_kv, g_kv, g_pre, g_apost, tm):
    s = h1.shape[0]

    def body(dqz_ref, dkv_ref, h_ref, dh2_ref, oa_ref, wb_ref, wkv_ref, gk_ref, gb_ref, ga_ref,
             dh1_ref, doa_ref, dg_ref):
        @pl.when(pl.program_id(0) == 0)
        def _():
            dg_ref[...] = jnp.zeros_like(dg_ref)
        dnb = _nt(dqz_ref[:, 0:512], wb_ref[0])
        for j in range(1, 4):
            dnb = dnb + _nt(dqz_ref[:, 512 * j:512 * (j + 1)], wb_ref[j])
        dnk = _nt(dkv_ref[...], wkv_ref[...])
        h = h_ref[...]
        r = _rms_scale(h)
        hh = h * r
        _acc_row(dg_ref, 0, jnp.sum(dnk * hh, axis=0, keepdims=True))
        _acc_row(dg_ref, 1, jnp.sum(dnb * hh, axis=0, keepdims=True))
        dhh = dnb * gb_ref[...] + dnk * gk_ref[...]
        dh1 = dh2_ref[...] + r * (dhh - hh * jnp.mean(dhh * hh, axis=-1, keepdims=True))
        dh1_ref[...] = dh1
        oa = oa_ref[...]
        ra = _rms_scale(oa)
        oh = oa * ra
        _acc_row(dg_ref, 2, jnp.sum(dh1 * oh, axis=0, keepdims=True))
        doh = dh1 * ga_ref[...]
        doa_ref[...] = (ra * (doh - oh * jnp.mean(doh * oh, axis=-1, keepdims=True))).astype(BF16)

    row = lambda i: (i, 0)
    fix = lambda i: (0, 0)
    return pl.pallas_call(
        body, name="b_bwd", grid=(s // tm,),
        in_specs=[pl.BlockSpec((tm, 2 * D), row), pl.BlockSpec((tm, 2 * KV_W), row), pl.BlockSpec((tm, D), row),
                  pl.BlockSpec((tm, D), row), pl.BlockSpec((tm, D), row),
                  pl.BlockSpec((4, D, 512), lambda i: (0, 0, 0)), pl.BlockSpec((D, 2 * KV_W), fix),
                  pl.BlockSpec((1, D), fix), pl.BlockSpec((1, D), fix), pl.BlockSpec((1, D), fix)],
        out_specs=[pl.BlockSpec((tm, D), row), pl.BlockSpec((tm, D), row), pl.BlockSpec((8, D), fix)],
        out_shape=[SDS((s, D), F32), SDS((s, D), BF16), SDS((8, D), F32)],
        compiler_params=_params(("arbitrary",)),
    )(dqz, dkv, h1, dh2, oa, wbin_g, w_kv, g_kv, g_pre, g_apost)


def _chip_exchange(parts, recvs, send, recv):
    x, y, c = lax.axis_index("x"), lax.axis_index("y"), lax.axis_index("c")
    chips = [(x, 1 - y), (1 - x, y), (1 - x, 1 - y)]
    copies = []
    for a, (t, r) in enumerate(zip(parts, recvs)):
        for j, (px, py) in enumerate(chips):
            copies.append(pltpu.make_async_remote_copy(
                src_ref=t.at[2 * px + py], dst_ref=r.at[j], send_sem=send.at[3 * a + j],
                recv_sem=recv.at[3 * a + j], device_id=(px, py, c), device_id_type=MESH))
    return copies


def _exchange_specs(parts):
    anyspace = pl.BlockSpec(memory_space=pl.ANY)
    n = len(parts)
    return ([anyspace] * n, [anyspace] * n, [SDS((3,) + t.shape[1:], t.dtype) for t in parts],
            [pltpu.SemaphoreType.DMA((3 * n,)), pltpu.SemaphoreType.DMA((3 * n,))])


def _a_bwd(doa, proj, conv_w, w_out, tm, parts):
    s = doa.shape[0]
    nt = s // tm
    n = len(parts)
    ex_in, ex_out, ex_shape, ex_sems = _exchange_specs(parts)

    def body(*refs):
        doa_ref, proj_ref, halo_ref, cw_ref, w_ref = refs[:5]
        part_refs = refs[5:5 + n]
        dproj_ref, dcw_ref = refs[5 + n:7 + n]
        recv_refs = refs[7 + n:7 + 2 * n]
        carry, send, recv = refs[7 + 2 * n:]
        i = pl.program_id(0)
        r = nt - 1 - i

        @pl.when(i == 0)
        def _():
            dcw_ref[...] = jnp.zeros_like(dcw_ref)
            carry[...] = jnp.zeros_like(carry)
            for cp in _chip_exchange(part_refs, recv_refs, send, recv):
                cp.start()
        dya = _nt(doa_ref[...], w_ref[...])
        bg = proj_ref[:, 0:D].astype(F32)
        cg = proj_ref[:, D:2 * D].astype(F32)
        u = proj_ref[:, 2 * D:3 * D].astype(F32)
        z = proj_ref[:, 3 * D:4 * D].astype(F32)
        v = cg * u
        before = jnp.where(r > 0, halo_ref[:, D:2 * D].astype(F32) * halo_ref[:, 2 * D:3 * D].astype(F32), 0.0)
        rows = lax.broadcasted_iota(jnp.int32, (tm, D), 0)
        v1, v2 = _shift_rows(v, before[HALO - 1:HALO, :], before[HALO - 2:HALO - 1, :], rows)
        conv = cw_ref[0:1, :] * v2 + cw_ref[1:2, :] * v1 + cw_ref[2:3, :] * v
        sg, sz = _silu_parts(z)
        dproj_ref[:, 0:D] = (dya * conv * sz).astype(BF16)
        dproj_ref[:, 3 * D:4 * D] = (dya * bg * conv * _dsilu(z, sg)).astype(BF16)
        dconv = dya * bg * sz
        _acc_row(dcw_ref, 0, jnp.sum(dconv * v2, axis=0, keepdims=True))
        _acc_row(dcw_ref, 1, jnp.sum(dconv * v1, axis=0, keepdims=True))
        _acc_row(dcw_ref, 2, jnp.sum(dconv * v, axis=0, keepdims=True))
        after = carry[...]
        up1 = jnp.where(rows < tm - 1, pltpu.roll(dconv, tm - 1, 0), after[0:1, :])
        up2 = jnp.where(rows < tm - 2, pltpu.roll(dconv, tm - 2, 0),
                        jnp.where(rows == tm - 2, after[0:1, :], after[1:2, :]))
        carry[...] = dconv[0:8, :]
        dv = cw_ref[2:3, :] * dconv + cw_ref[1:2, :] * up1 + cw_ref[0:1, :] * up2
        dproj_ref[:, D:2 * D] = (dv * u).astype(BF16)
        dproj_ref[:, 2 * D:3 * D] = (dv * cg).astype(BF16)

        @pl.when(i == nt - 1)
        def _():
            for cp in _chip_exchange(part_refs, recv_refs, send, recv):
                cp.wait()

    rev = lambda i: (nt - 1 - i, 0)
    fix = lambda i: (0, 0)
    halo = lambda i: (jnp.maximum((nt - 1 - i) * (tm // HALO) - 1, 0), 0)
    dproj, dcw, *got = pl.pallas_call(
        body, name="a_bwd", grid=(nt,),
        in_specs=[pl.BlockSpec((tm, D), rev), pl.BlockSpec((tm, 4 * D), rev), pl.BlockSpec((HALO, 4 * D), halo),
                  pl.BlockSpec((8, D), fix), pl.BlockSpec((D, D), fix)] + ex_in,
        out_specs=[pl.BlockSpec((tm, 4 * D), rev), pl.BlockSpec((8, D), fix)] + ex_out,
        out_shape=[SDS((s, 4 * D), BF16), SDS((8, D), F32)] + ex_shape,
        scratch_shapes=[pltpu.VMEM((8, D), F32)] + ex_sems,
        compiler_params=_params(("arbitrary",)),
    )(doa, proj, proj, conv_w, w_out, *parts)
    return dproj, dcw, got


def _dn1(dp_ref, w_ref):
    dn = _nt(dp_ref[:, 0:D], w_ref[0])
    for j in range(1, 4):
        dn = dn + _nt(dp_ref[:, D * j:D * (j + 1)], w_ref[j])
    return dn


def _a_in_bwd_matmul(dproj, win_g, tm, count, parts):
    n = len(parts)
    ex_in, ex_out, ex_shape, ex_sems = _exchange_specs(parts)

    def body(*refs):
        dp_ref, w_ref = refs[:2]
        part_refs = refs[2:2 + n]
        dn_ref = refs[2 + n]
        recv_refs = refs[3 + n:3 + 2 * n]
        sems = refs[3 + 2 * n:]

        @pl.when(pl.program_id(0) == 0)
        def _():
            for cp in _chip_exchange(part_refs, recv_refs, *sems):
                cp.start()
        dn_ref[...] = _dn1(dp_ref, w_ref)

        @pl.when(pl.program_id(0) == count - 1)
        def _():
            for cp in _chip_exchange(part_refs, recv_refs, *sems):
                cp.wait()

    row = lambda i: (i, 0)
    dn, *got = pl.pallas_call(
        body, name="a_in_bwd_matmul", grid=(count,),
        in_specs=[pl.BlockSpec((tm, 4 * D), row), pl.BlockSpec((4, D, D), lambda i: (0, 0, 0))] + ex_in,
        out_specs=[pl.BlockSpec((tm, D), row)] + ex_out,
        out_shape=[SDS((count * tm, D), F32)] + ex_shape,
        scratch_shapes=ex_sems,
        compiler_params=_params(("arbitrary",)),
    )(dproj, win_g, *parts)
    return dn, got


def _a_in_bwd(dn_first, dproj, x, dh1, win_g, g_pre, tm):
    s = x.shape[0]
    nt = s // tm
    count = dn_first.shape[0] // tm

    def body(dn_ref, dp_ref, x_ref, dh_ref, w_ref, g_ref, gx_ref, dg_ref, dn_s):
        i = pl.program_id(0)

        @pl.when(i == 0)
        def _():
            dg_ref[...] = jnp.zeros_like(dg_ref)

        @pl.when(i < count)
        def _():
            dn_s[...] = dn_ref[...]

        @pl.when(i >= count)
        def _():
            dn_s[...] = _dn1(dp_ref, w_ref)
        dn = dn_s[...]
        xv = x_ref[...]
        r = _rms_scale(xv)
        xh = xv * r
        _acc_row(dg_ref, 0, jnp.sum(dn * xh, axis=0, keepdims=True))
        dxh = dn * g_ref[...]
        gx_ref[...] = dh_ref[...] + r * (dxh - xh * jnp.mean(dxh * xh, axis=-1, keepdims=True))

    row = lambda i: (i, 0)
    fix = lambda i: (0, 0)
    return pl.pallas_call(
        body, name="a_in_bwd", grid=(nt,),
        in_specs=[pl.BlockSpec((tm, D), lambda i: (jnp.minimum(i, count - 1), 0)),
                  pl.BlockSpec((tm, 4 * D), lambda i: (jnp.maximum(i, count), 0)),
                  pl.BlockSpec((tm, D), row), pl.BlockSpec((tm, D), row),
                  pl.BlockSpec((4, D, D), lambda i: (0, 0, 0)), pl.BlockSpec((1, D), fix)],
        out_specs=[pl.BlockSpec((tm, D), row), pl.BlockSpec((8, D), fix)],
        out_shape=[SDS((s, D), F32), SDS((8, D), F32)],
        scratch_shapes=[pltpu.VMEM((tm, D), F32)],
        compiler_params=_params(("arbitrary",)),
    )(dn_first, dproj, x, dh1, win_g, g_pre)


def _dw(a, b, tn, tmw, name):
    s, k = a.shape
    n = b.shape[1]

    def body(a_ref, b_ref, o_ref):
        @pl.when(pl.program_id(1) == 0)
        def _():
            o_ref[...] = jnp.zeros_like(o_ref)
        o_ref[0] += _tn(a_ref[...], b_ref[...])

    return pl.pallas_call(
        body, name=name, grid=(n // tn, s // tmw),
        in_specs=[pl.BlockSpec((tmw, k), lambda j, t: (t, 0)), pl.BlockSpec((tmw, tn), lambda j, t: (t, j))],
        out_specs=pl.BlockSpec((1, k, tn), lambda j, t: (j, 0, 0)),
        out_shape=SDS((n // tn, k, tn), F32),
        compiler_params=_params(("parallel", "arbitrary")),
    )(a, b)


def _sibling_exchange(name, to_sibling=(), shards=(), smalls=None):
    n_g, n_h = len(to_sibling), len(shards)
    has_small = smalls is not None

    def body(*refs):
        gs = refs[:n_g]
        pos = n_g + n_h
        small_in = refs[pos] if has_small else None
        pos += has_small
        rs, fs = refs[pos:pos + n_g], refs[pos + n_g:pos + n_g + n_h]
        pos += n_g + n_h
        small_all = refs[pos] if has_small else None
        pos += has_small
        dsend, drecv, ssend, srecv = refs[pos:]
        x, y, c = lax.axis_index("x"), lax.axis_index("y"), lax.axis_index("c")
        sibling = (x, y, 1 - c)
        sends, arrivals = [], []
        for a, (g, r) in enumerate(zip(gs, rs)):
            h = g.shape[1] // 2
            src = g.at[:, pl.ds(pl.multiple_of((1 - c) * h, 8), h), :]
            sends.append(pltpu.make_async_remote_copy(src_ref=src, dst_ref=r, send_sem=dsend.at[a], recv_sem=drecv.at[a],
                                                      device_id=sibling, device_id_type=MESH))
            arrivals.append(pltpu.make_async_remote_copy(src_ref=r, dst_ref=r, send_sem=dsend.at[a], recv_sem=drecv.at[a],
                                                         device_id=sibling, device_id_type=MESH))
        for b, full in enumerate(fs):
            h = full.shape[0] // 2
            mine = full.at[pl.ds(pl.multiple_of(c * h, 8), h)]
            theirs = full.at[pl.ds(pl.multiple_of((1 - c) * h, 8), h)]
            sends.append(pltpu.make_async_remote_copy(src_ref=mine, dst_ref=mine, send_sem=dsend.at[n_g + b],
                                                      recv_sem=drecv.at[n_g + b], device_id=sibling, device_id_type=MESH))
            arrivals.append(pltpu.make_async_remote_copy(src_ref=mine, dst_ref=theirs, send_sem=dsend.at[n_g + b],
                                                         recv_sem=drecv.at[n_g + b], device_id=sibling, device_id_type=MESH))
        if has_small:
            me = 4 * x + 2 * y + c
            small_all[me] = small_in[...]
            for rel in range(1, N_DEV):
                fx, fy, fc = rel >> 2, (rel >> 1) & 1, rel & 1
                peer = (x + fx - 2 * x * fx, y + fy - 2 * y * fy, c + fc - 2 * c * fc)
                sender = 4 * peer[0] + 2 * peer[1] + peer[2]
                sends.append(pltpu.make_async_remote_copy(
                    src_ref=small_in, dst_ref=small_all.at[me], send_sem=ssend.at[rel - 1], recv_sem=srecv.at[rel - 1],
                    device_id=peer, device_id_type=MESH))
                arrivals.append(pltpu.make_async_remote_copy(
                    src_ref=small_in, dst_ref=small_all.at[sender], send_sem=ssend.at[rel - 1], recv_sem=srecv.at[rel - 1],
                    device_id=peer, device_id_type=MESH))
        for cp in sends:
            cp.start()
        for cp in arrivals:
            cp.wait_recv()
        for cp in sends:
            cp.wait_send()

    anyspace = pl.BlockSpec(memory_space=pl.ANY)
    vm = pl.BlockSpec(memory_space=pltpu.VMEM)
    out_shape = [SDS((N_CHIPS, g.shape[1] // 2, g.shape[2]), F32) for g in to_sibling]
    out_shape += [SDS(full.shape, F32) for full in shards]
    if has_small:
        out_shape.append(SDS((N_DEV,) + smalls.shape, F32))
    n_d2d = max(n_g + n_h, 1)
    outs = pl.pallas_call(
        body, name=name, out_shape=out_shape,
        in_specs=[anyspace] * (n_g + n_h) + [vm] * has_small, out_specs=[anyspace] * (n_g + n_h) + [vm] * has_small,
        scratch_shapes=[pltpu.SemaphoreType.DMA((n_d2d,)), pltpu.SemaphoreType.DMA((n_d2d,)),
                        pltpu.SemaphoreType.DMA((N_DEV - 1,)), pltpu.SemaphoreType.DMA((N_DEV - 1,))],
        input_output_aliases={n_g + b: n_g + b for b in range(n_h)},
    )(*to_sibling, *shards, *([smalls] if has_small else []))
    return outs[:n_g], outs[n_g:n_g + n_h], (outs[n_g + n_h] if has_small else None)


def _add_sibling(where, g, r, name):
    _, rows, cols = g.shape
    h = rows // 2
    tr = min(h, 256)
    nh = h // tr

    def body(where_ref, g_ref, r_ref, t_ref, own_ref):
        t = g_ref[0] + r_ref[0]
        t_ref[0] = t.astype(BF16)

        @pl.when(pl.program_id(1) == where_ref[1])
        def _():
            own_ref[...] = t

    return pl.pallas_call(
        body, name=name,
        grid_spec=pltpu.PrefetchScalarGridSpec(
            num_scalar_prefetch=1, grid=(nh, N_CHIPS),
            in_specs=[pl.BlockSpec((1, tr, cols), lambda i, k, w: (k, w[0] * nh + i, 0)),
                      pl.BlockSpec((1, tr, cols), lambda i, k, w: (k, i, 0))],
            out_specs=[pl.BlockSpec((1, tr, cols), lambda i, k, w: (k, i, 0)),
                       pl.BlockSpec((tr, cols), lambda i, k, w: (i, 0))]),
        out_shape=[SDS((N_CHIPS, h, cols), BF16), SDS((h, cols), F32)],
        compiler_params=_params(("parallel", "arbitrary")),
    )(where, g, r)


def _add_chips(where, own, r, name):
    h, cols = own.shape
    tr = min(h, 256)
    nh = h // tr

    def body(where_ref, t_ref, r_ref, o_ref):
        del where_ref
        o_ref[...] = ((t_ref[...] + r_ref[0].astype(F32)) + r_ref[1].astype(F32)) + r_ref[2].astype(F32)

    return pl.pallas_call(
        body, name=name,
        grid_spec=pltpu.PrefetchScalarGridSpec(
            num_scalar_prefetch=1, grid=(nh,),
            in_specs=[pl.BlockSpec((tr, cols), lambda i, w: (i, 0)), pl.BlockSpec((3, tr, cols), lambda i, w: (0, i, 0))],
            out_specs=pl.BlockSpec((tr, cols), lambda i, w: (w[0] * nh + i, 0))),
        out_shape=SDS((2 * h, cols), F32),
        compiler_params=_params(("parallel",)),
    )(where, own, r)


def _sum_smalls(small_all):
    def body(all_ref, o_ref):
        acc = all_ref[0]
        for dev in range(1, N_DEV):
            acc = acc + all_ref[dev]
        o_ref[...] = acc

    return pl.pallas_call(
        body, name="sum_smalls", out_shape=SDS(small_all.shape[1:], F32),
        in_specs=[pl.BlockSpec(memory_space=pltpu.VMEM)], out_specs=pl.BlockSpec(memory_space=pltpu.VMEM),
    )(small_all)


def _adam_step(g, w, m, v):
    nm = ADAM_B1 * m + (1.0 - ADAM_B1) * g
    nv = ADAM_B2 * v + (1.0 - ADAM_B2) * (g * g)
    m_hat = nm / (1.0 - ADAM_B1 ** ADAM_STEP)
    v_hat = nv / (1.0 - ADAM_B2 ** ADAM_STEP)
    return -ADAM_LR * (m_hat / (jnp.sqrt(v_hat) + ADAM_EPS) + ADAM_WD * w), nm, nv


def _adamw(g, w, m, v, name):
    rows, cols = g.shape
    tr = min(rows, 256)

    def body(g_ref, w_ref, m_ref, v_ref, d_ref, nm_ref, nv_ref):
        d_ref[...], nm_ref[...], nv_ref[...] = _adam_step(g_ref[...], w_ref[...], m_ref[...], v_ref[...])

    spec = pl.BlockSpec((tr, cols), lambda i: (i, 0))
    return pl.pallas_call(
        body, name=name, grid=(rows // tr,), in_specs=[spec] * 4, out_specs=[spec] * 3,
        out_shape=[SDS(g.shape, F32)] * 3, compiler_params=_params(("parallel",)),
    )(g, w, m, v)


def _small_update(chip, tot, wmv):
    names = list(SMALL_PLACES)
    n = len(names)

    def body(chip_ref, tot_ref, quarter_ref, *refs):
        del chip_ref
        ins, outs = refs[:3 * n], refs[3 * n:]
        for i, nm in enumerate(names):
            sharded, row, (rows, cols) = SMALL_PLACES[nm]
            g = (quarter_ref if sharded else tot_ref)[row:row + rows, 0:cols]
            outs[4 * i][...] = g
            outs[4 * i + 1][...], outs[4 * i + 2][...], outs[4 * i + 3][...] = _adam_step(
                g, ins[3 * i][...], ins[3 * i + 1][...], ins[3 * i + 2][...])

    whole = lambda shape: pl.BlockSpec(shape, lambda i, c: (0,) * len(shape))
    shapes = [SMALL_PLACES[nm][2] for nm in names]
    outs = pl.pallas_call(
        body, name="small_update",
        grid_spec=pltpu.PrefetchScalarGridSpec(
            num_scalar_prefetch=1, grid=(1,),
            in_specs=[whole(tot.shape), pl.BlockSpec((tot.shape[0], D // 4), lambda i, c: (0, c[0]))]
            + [whole(shp) for shp in shapes for _ in range(3)],
            out_specs=[whole(shp) for shp in shapes for _ in range(4)]),
        out_shape=[SDS(shp, F32) for shp in shapes for _ in range(4)],
    )(chip, tot, tot, *[a for nm in names for a in wmv[nm]])
    return {nm: tuple(outs[4 * i:4 * i + 4]) for i, nm in enumerate(names)}


def _pad_rows(a, rows):
    return jnp.concatenate([a, jnp.zeros((rows - a.shape[0], a.shape[1]), a.dtype)], axis=0)


def _pad_cols(a, cols):
    return jnp.concatenate([a, jnp.zeros((a.shape[0], cols - a.shape[1]), a.dtype)], axis=1)


def kernel(x, a_pre_norm, a_w_in, a_conv_w, a_w_out, a_post_norm, kv_norm, w_kv, rel_bias, b_pre_norm, b_w_in, b_sinks, b_w_out, b_post_norm, loss_target, m_a_pre_norm, m_a_w_in, m_a_conv_w, m_a_w_out, m_a_post_norm, m_kv_norm, m_w_kv, m_rel_bias, m_b_pre_norm, m_b_w_in, m_b_sinks, m_b_w_out, m_b_post_norm, v_a_pre_norm, v_a_w_in, v_a_conv_w, v_a_w_out, v_a_post_norm, v_kv_norm, v_w_kv, v_rel_bias, v_b_pre_norm, v_b_w_in, v_b_sinks, v_b_w_out, v_b_post_norm):
    seq = x.shape[1]
    xs = x.reshape(seq, D)
    tgt = loss_target.reshape(seq, D)
    chip = 2 * lax.axis_index("x") + lax.axis_index("y")
    core = lax.axis_index("c")
    tm = _tile(seq, 512)
    tm_mix = _tile(seq, 256)
    tmw = _tile(seq, 1024)

    shards = [a_w_in[0], a_w_out[0], w_kv, b_w_in[0], b_w_out[0]]
    small_w = _pad_rows(jnp.concatenate([a_pre_norm, a_conv_w[0], a_post_norm], axis=0), 8)
    win_g, *own_only, small_g = _gather_weights(shards, small_w, 1)
    small_full = small_g.transpose(1, 0, 2).reshape(8, D)
    g_apre, conv_w, g_apost = small_full[0:1], _pad_rows(small_full[1:4], 8), small_full[4:5]
    g_kv = kv_norm.reshape(1, D)

    proj, n1, (wouta_g, wkv_g, wbin_g, woutb_g) = _a_in(xs, g_apre, win_g, tm, own_only)
    wouta = wouta_g.reshape(D, D)
    wkv = wkv_g.reshape(D, 2 * KV_W)
    woutb = woutb_g.reshape(D, D)
    ya, oa, h1 = _a_mix(proj, xs, conv_w, wouta, g_apost, tm_mix)
    nk, nb, kv, q, zb = _b_in(h1, g_kv, b_pre_norm, wkv, wbin_g, tm)
    tab = _bias_table(rel_bias, b_sinks.reshape(N_HEADS))
    att, stats = _attn_fwd(q, kv, tab)
    ob, dy2, dh2, dqz, datt, loss_acc, dg_bpost = _mid(att, zb, h1, tgt, woutb, b_post_norm, tm)

    dqz, dkv, dtab = _attn_bwd(q, kv, datt, stats, tab, dqz)
    dh1, doa, dg_b = _b_bwd(dqz, dkv, h1, dh2, oa, wbin_g, wkv, g_kv, b_pre_norm, g_apost, tm)
    where = jnp.stack([core, chip]).astype(jnp.int32)
    dw_outa = _dw(ya, doa, D, tmw, "dw_a_out").reshape(N_CHIPS, D // 4, D)
    dw_kv = _dw(nk, dkv, 2 * KV_W, tmw, "dw_kv").reshape(N_CHIPS, D // 4, 2 * KV_W)
    dw_bin = _dw(nb, dqz, 512, tmw, "dw_b_in")
    dw_outb = _dw(ob, dy2, D, tmw, "dw_b_out").reshape(N_CHIPS, D // 4, D)
    grads1 = [dw_outa, dw_kv, dw_bin, dw_outb]
    names1 = ["a_w_out", "w_kv", "b_w_in", "b_w_out"]
    from_sibling1, _, _ = _sibling_exchange("to_sibling_1", to_sibling=grads1)
    sums1 = [_add_sibling(where, g, r, "add_sibling_" + nm) for g, r, nm in zip(grads1, from_sibling1, names1)]
    dproj, dconv_w, from_chips1 = _a_bwd(doa, proj, conv_w, wouta, tm_mix, [t for t, _ in sums1])
    shards1 = [_add_chips(where, own, r, "add_chips_" + nm) for (_, own), r, nm in zip(sums1, from_chips1, names1)]
    dw_in = _dw(n1, dproj, D, tmw, "dw_a_in")
    from_sibling2, (g_wouta, g_wkv, g_wbin, g_woutb), _ = _sibling_exchange(
        "to_sibling_2", to_sibling=[dw_in], shards=shards1)
    part2, own2 = _add_sibling(where, dw_in, from_sibling2[0], "add_sibling_a_w_in")
    nt = seq // tm
    dn_first, from_chips2 = _a_in_bwd_matmul(dproj, win_g, tm, max(nt - max(nt // 4, 1), 1), [part2])
    grad_x, dg_apre = _a_in_bwd(dn_first, dproj, xs, dh1, win_g, g_apre, tm)
    shard2 = _add_chips(where, own2, from_chips2[0], "add_chips_a_w_in")
    drel, dsink = _bias_fold(dtab)

    smalls = jnp.concatenate([
        dg_apre[0:1], dg_b[2:3], dg_b[0:1], dg_b[1:2], dg_bpost[0:1], _pad_cols(dsink[0:1], D),
        _pad_cols(loss_acc[0:1], D), jnp.zeros((1, D), F32), dconv_w, _pad_cols(drel, D)], axis=0)
    _, (g_win,), small_all = _sibling_exchange("share_last", shards=[shard2], smalls=smalls)
    tot = _sum_smalls(small_all)

    big = {}
    for nm, g, w, m, v in [("a_w_in", g_win, a_w_in, m_a_w_in, v_a_w_in), ("a_w_out", g_wouta, a_w_out, m_a_w_out, v_a_w_out),
                           ("w_kv", g_wkv, w_kv, m_w_kv, v_w_kv), ("b_w_in", g_wbin, b_w_in, m_b_w_in, v_b_w_in),
                           ("b_w_out", g_woutb, b_w_out, m_b_w_out, v_b_w_out)]:
        shp = w.shape
        two = (shp[-2], shp[-1])
        d, nm_, nv_ = _adamw(g, w.reshape(two), m.reshape(two), v.reshape(two), "adamw_" + nm)
        big[nm] = (g.reshape(shp), d.reshape(shp), nm_.reshape(shp), nv_.reshape(shp))

    given = {"a_pre_norm": (a_pre_norm, m_a_pre_norm, v_a_pre_norm), "a_conv_w": (a_conv_w, m_a_conv_w, v_a_conv_w),
             "a_post_norm": (a_post_norm, m_a_post_norm, v_a_post_norm), "kv_norm": (kv_norm, m_kv_norm, v_kv_norm),
             "rel_bias": (rel_bias, m_rel_bias, v_rel_bias), "b_pre_norm": (b_pre_norm, m_b_pre_norm, v_b_pre_norm),
             "b_sinks": (b_sinks, m_b_sinks, v_b_sinks), "b_post_norm": (b_post_norm, m_b_post_norm, v_b_post_norm)}
    small = _small_update(where[1:2], tot, {nm: tuple(a.reshape(SMALL_PLACES[nm][2]) for a in wmv)
                                            for nm, wmv in given.items()})
    order = ["a_pre_norm", "a_w_in", "a_conv_w", "a_w_out", "a_post_norm", "kv_norm", "w_kv", "rel_bias",
             "b_pre_norm", "b_w_in", "b_sinks", "b_w_out", "b_post_norm"]
    outs = []
    for which in range(4):
        for nm in order:
            outs.append(big[nm][which] if nm in big else small[nm][which].reshape(given[nm][0].shape))
    loss = 0.5 * tot[LOSS_ROW, 0]
    return (loss, grad_x.reshape(x.shape), *outs)
```

```python
import functools
import math

import jax
import jax.numpy as jnp
from jax import lax
from jax.experimental import pallas as pl
from jax.experimental.pallas import tpu as pltpu

F32 = jnp.float32
BF16 = jnp.bfloat16
MESH = pl.DeviceIdType.MESH
SDS = jax.ShapeDtypeStruct

D = 1024
HEAD_DIM = 64
N_HEADS = 16
N_KV = 2
GROUP = 8
KV_W = 128
BLK = 128
N_BUCKETS = 32
MAX_EXACT = 16
MAX_DISTANCE = 128
EPS = 1e-6
NEG_INF = -1e30
Q_SCALE = HEAD_DIM ** -0.5

ADAM_LR = 0.001
ADAM_B1 = 0.9
ADAM_B2 = 0.999
ADAM_EPS = 1e-08
ADAM_WD = 0.01
ADAM_STEP = 10

N_CHIPS = 4
N_DEV = 8
VMEM_LIMIT = 56 * 1024 * 1024
SMALL_ROWS = 48
LOSS_ROW = 6
SMALL_PLACES = {
    "a_pre_norm": (True, 0, (1, D // 4)), "a_conv_w": (True, 8, (3, D // 4)), "a_post_norm": (True, 1, (1, D // 4)),
    "kv_norm": (False, 2, (1, D)), "rel_bias": (False, 16, (N_BUCKETS, N_HEADS)), "b_pre_norm": (False, 3, (1, D)),
    "b_sinks": (False, 5, (1, N_HEADS)), "b_post_norm": (False, 4, (1, D)),
}
HALO = 16


def _bucket_thresholds():
    def bucket(d):
        big = MAX_EXACT + int(math.log(d / MAX_EXACT) / math.log(MAX_DISTANCE / MAX_EXACT)
                              * (N_BUCKETS - MAX_EXACT))
        return d if d < MAX_EXACT else min(big, N_BUCKETS - 1)
    out = []
    for b in range(MAX_EXACT + 1, N_BUCKETS):
        out.append(min(d for d in range(MAX_EXACT, MAX_DISTANCE) if bucket(d) >= b))
    return tuple(out)


BUCKET_THRESHOLDS = _bucket_thresholds()


def _params(semantics=None, vmem=VMEM_LIMIT):
    return pltpu.CompilerParams(dimension_semantics=semantics, vmem_limit_bytes=vmem)


def _tile(n, pref):
    return pref if n >= 2 * pref else max(n // 2, 8)


def _rms_scale(v):
    return lax.rsqrt(jnp.mean(v * v, axis=-1, keepdims=True) + EPS)


def _nt(a, b):
    return lax.dot_general(a, b, (((1,), (1,)), ((), ())), preferred_element_type=F32)


def _tn(a, b):
    return lax.dot_general(a, b, (((0,), (0,)), ((), ())), preferred_element_type=F32)


def _nn(a, b):
    return jnp.dot(a, b, preferred_element_type=F32)


def _silu_parts(z):
    sg = jax.nn.sigmoid(z)
    return sg, z * sg


def _dsilu(z, sg):
    return sg * (1.0 + z * (1.0 - sg))


def _acc_row(ref, row, val):
    ref[row:row + 1, :] += val


def _gather_copies(outs, splits, ici_send, ici_recv, d2d_send, d2d_recv):
    x, y, c = lax.axis_index("x"), lax.axis_index("y"), lax.axis_index("c")
    k = 2 * x + y
    sibling = (x, y, 1 - c)

    def part(o_ref, chip, core, split):
        if not split:
            return o_ref.at[chip]
        h = o_ref.shape[1] // 2
        return o_ref.at[chip, pl.ds(pl.multiple_of(core * h, 16), h)]

    def remote(ref, a, j, sems, to):
        return pltpu.make_async_remote_copy(src_ref=ref, dst_ref=ref, send_sem=sems[0].at[3 * a + j],
                                            recv_sem=sems[1].at[3 * a + j], device_id=to, device_id_type=MESH)

    copies = []
    for a, (o_ref, split) in enumerate(zip(outs, splits)):
        for j, (px, py) in enumerate([(x, 1 - y), (1 - x, y), (1 - x, 1 - y)]):
            kj = 2 * px + py
            ici, d2d = (ici_send, ici_recv), (d2d_send, d2d_recv)
            copies.append((remote(part(o_ref, k, c, split), a, j, ici, (px, py, c)),
                           remote(part(o_ref, kj, c, split), a, j, ici, (px, py, c)),
                           remote(part(o_ref, kj, c, split), a, j, d2d, sibling) if split else None,
                           remote(part(o_ref, kj, 1 - c, split), a, j, d2d, sibling) if split else None))
    return copies


def _gather_sems(n):
    return [pltpu.SemaphoreType.DMA((3 * n,)) for _ in range(4)]


def _gather_weights(shards, small, n_now):
    n = len(shards)

    def body(*refs):
        ins, small_in = refs[:n], refs[n]
        outs, small_out = refs[n + 1:2 * n + 1], refs[2 * n + 1]
        sems = refs[2 * n + 2:]
        k = 2 * lax.axis_index("x") + lax.axis_index("y")
        for i_ref, o_ref in zip(ins, outs):
            o_ref[k] = i_ref[...].astype(BF16)
        small_out[k] = small_in[...]
        copies = _gather_copies(list(outs[:n_now]) + [small_out], [True] * n_now + [False], *sems)
        for send, _, _, _ in copies:
            send.start()
        for _, arrival, forward, _ in copies:
            arrival.wait_recv()
            if forward is not None:
                forward.start()
        for send, _, forward, forwarded in copies:
            if forward is not None:
                forwarded.wait_recv()
                forward.wait_send()
            send.wait_send()

    vm = pl.BlockSpec(memory_space=pltpu.VMEM)
    out_shape = [SDS((N_CHIPS,) + s.shape, BF16) for s in shards] + [SDS((N_CHIPS,) + small.shape, F32)]
    return pl.pallas_call(
        body, name="gather_weights", out_shape=out_shape,
        in_specs=[vm] * (n + 1), out_specs=[vm] * (n + 1),
        scratch_shapes=_gather_sems(n_now + 1),
        compiler_params=pltpu.CompilerParams(vmem_limit_bytes=VMEM_LIMIT),
    )(*shards, small)


def _a_in(x, g_pre, win_g, tm, later):
    s = x.shape[0]
    nt = s // tm
    n = len(later)

    def body(*refs):
        x_ref, g_ref, w_ref = refs[:3]
        proj_ref, n1_ref = refs[3 + n:5 + n]
        gathered = refs[5 + n:5 + 2 * n]
        sems = refs[5 + 2 * n:]
        i = pl.program_id(0)

        @pl.when(i == 0)
        def _():
            for send, _, _, _ in _gather_copies(gathered, [True] * n, *sems):
                send.start()
        xv = x_ref[...]
        n1 = (xv * _rms_scale(xv) * g_ref[...]).astype(BF16)
        n1_ref[...] = n1
        for j in range(4):
            proj_ref[:, D * j:D * (j + 1)] = _nn(n1, w_ref[j]).astype(BF16)

        @pl.when(i == (3 * nt) // 4)
        def _():
            for _, arrival, forward, _ in _gather_copies(gathered, [True] * n, *sems):
                arrival.wait_recv()
                forward.start()

        @pl.when(i == nt - 1)
        def _():
            for send, _, forward, forwarded in _gather_copies(gathered, [True] * n, *sems):
                forwarded.wait_recv()
                forward.wait_send()
                send.wait_send()

    row = lambda i: (i, 0)
    anyspace = pl.BlockSpec(memory_space=pl.ANY)
    proj, n1, *gathered = pl.pallas_call(
        body, name="a_in", grid=(nt,),
        in_specs=[pl.BlockSpec((tm, D), row), pl.BlockSpec((1, D), lambda i: (0, 0)),
                  pl.BlockSpec((4, D, D), lambda i: (0, 0, 0))] + [anyspace] * n,
        out_specs=[pl.BlockSpec((tm, 4 * D), row), pl.BlockSpec((tm, D), row)] + [anyspace] * n,
        out_shape=[SDS((s, 4 * D), BF16), SDS((s, D), BF16)] + [SDS(w.shape, w.dtype) for w in later],
        scratch_shapes=_gather_sems(n),
        input_output_aliases={3 + a: 2 + a for a in range(n)},
        compiler_params=_params(("arbitrary",)),
    )(x, g_pre, win_g, *later)
    return proj, n1, gathered


def _shift_rows(v, last, second_last, rows):
    v1 = jnp.where(rows >= 1, pltpu.roll(v, 1, 0), last)
    v2 = jnp.where(rows >= 2, pltpu.roll(v, 2, 0), jnp.where(rows == 1, last, second_last))
    return v1, v2


def _a_mix(proj, x, conv_w, w_out, g_post, tm):
    s = x.shape[0]

    def body(proj_ref, x_ref, cw_ref, w_ref, g_ref, ya_ref, oa_ref, h1_ref, carry):
        @pl.when(pl.program_id(0) == 0)
        def _():
            carry[...] = jnp.zeros_like(carry)
        v = proj_ref[:, D:2 * D].astype(F32) * proj_ref[:, 2 * D:3 * D].astype(F32)
        rows = lax.broadcasted_iota(jnp.int32, (tm, D), 0)
        before = carry[...]
        v1, v2 = _shift_rows(v, before[7:8, :], before[6:7, :], rows)
        carry[...] = v[tm - 8:tm, :]
        conv = cw_ref[0:1, :] * v2 + cw_ref[1:2, :] * v1 + cw_ref[2:3, :] * v
        _, sz = _silu_parts(proj_ref[:, 3 * D:4 * D].astype(F32))
        ya = (proj_ref[:, 0:D].astype(F32) * conv * sz).astype(BF16)
        ya_ref[...] = ya
        oa = _nn(ya, w_ref[...])
        oa_ref[...] = oa
        h1_ref[...] = x_ref[...] + oa * _rms_scale(oa) * g_ref[...]

    row = lambda i: (i, 0)
    fix = lambda i: (0, 0)
    return pl.pallas_call(
        body, name="a_mix", grid=(s // tm,),
        in_specs=[pl.BlockSpec((tm, 4 * D), row), pl.BlockSpec((tm, D), row), pl.BlockSpec((8, D), fix),
                  pl.BlockSpec((D, D), fix), pl.BlockSpec((1, D), fix)],
        out_specs=[pl.BlockSpec((tm, D), row)] * 3,
        out_shape=[SDS((s, D), BF16), SDS((s, D), F32), SDS((s, D), F32)],
        scratch_shapes=[pltpu.VMEM((8, D), F32)],
        compiler_params=_params(("arbitrary",)),
    )(proj, x, conv_w, w_out, g_post)


def _b_in(h1, g_kv, g_pre, w_kv, wbin_g, tm):
    s = h1.shape[0]

    def body(h_ref, gk_ref, gb_ref, wkv_ref, wb_ref, kv_ref, q_ref, z_ref):
        h = h_ref[...]
        hh = h * _rms_scale(h)
        nk = (hh * gk_ref[...]).astype(BF16)
        nb = (hh * gb_ref[...]).astype(BF16)
        kv_ref[...] = _nn(nk, wkv_ref[...]).astype(BF16)
        for j in range(2):
            q_ref[:, 512 * j:512 * (j + 1)] = (_nn(nb, wb_ref[j]) * Q_SCALE).astype(BF16)
            z_ref[:, 512 * j:512 * (j + 1)] = _nn(nb, wb_ref[2 + j]).astype(BF16)

    row = lambda i: (i, 0)
    fix = lambda i: (0, 0)
    return pl.pallas_call(
        body, name="b_in", grid=(s // tm,),
        in_specs=[pl.BlockSpec((tm, D), row), pl.BlockSpec((1, D), fix), pl.BlockSpec((1, D), fix),
                  pl.BlockSpec((D, 2 * KV_W), fix), pl.BlockSpec((4, D, 512), lambda i: (0, 0, 0))],
        out_specs=[pl.BlockSpec((tm, 2 * KV_W), row), pl.BlockSpec((tm, D), row), pl.BlockSpec((tm, D), row)],
        out_shape=[SDS((s, 2 * KV_W), BF16), SDS((s, D), BF16), SDS((s, D), BF16)],
        compiler_params=_params(("parallel",)),
    )(h1, g_kv, g_pre, w_kv, wbin_g)


def _band_buckets():
    q = lax.broadcasted_iota(jnp.int32, (BLK, 2 * BLK), 0)
    k = lax.broadcasted_iota(jnp.int32, (BLK, 2 * BLK), 1)
    dist = q + BLK - k
    bucket = jnp.where(dist < MAX_EXACT, dist, MAX_EXACT)
    for t in BUCKET_THRESHOLDS:
        bucket = bucket + jnp.where(dist >= t, 1, 0)
    in_window = (dist >= 0) & (dist < BLK)
    return jnp.where(in_window, bucket, -1)


def _head_place(h):
    kh, j, e = h // GROUP, (h % GROUP) // 2, h % 2
    return kh, slice(BLK * j, BLK * (j + 1)), slice(2 * BLK * e, 2 * BLK * (e + 1))


def _bias_table(rel_bias, sinks):
    def body(rb_ref, sink_ref, tab_ref):
        bucket = _band_buckets()
        col = lax.broadcasted_iota(jnp.int32, (BLK, 2 * BLK), 1)
        for h in range(N_HEADS):
            acc = jnp.where(bucket < 0, NEG_INF, 0.0).astype(F32)
            for b in range(N_BUCKETS):
                acc = jnp.where(bucket == b, rb_ref[b, h], acc)
            acc = jnp.where(col == 0, sink_ref[h], acc)
            kh, rows, cols = _head_place(h)
            tab_ref[1, kh, rows, cols] = acc
            tab_ref[0, kh, rows, cols] = jnp.where((col > 0) & (col < BLK), NEG_INF, acc)

    return pl.pallas_call(
        body, name="bias_table", out_shape=SDS((2, N_KV, 4 * BLK, 4 * BLK), F32),
        in_specs=[pl.BlockSpec(memory_space=pltpu.SMEM), pl.BlockSpec(memory_space=pltpu.SMEM)],
        out_specs=pl.BlockSpec(memory_space=pltpu.VMEM),
    )(rel_bias, sinks)


def _bias_fold(dtab):
    def body(dtab_ref, out_ref, dsink_ref):
        bucket = _band_buckets()
        row = lax.broadcasted_iota(jnp.int32, (N_BUCKETS, 128), 0)
        lane = lax.broadcasted_iota(jnp.int32, (N_BUCKETS, 128), 1)
        row8 = lax.broadcasted_iota(jnp.int32, (8, 128), 0)
        lane8 = lax.broadcasted_iota(jnp.int32, (8, 128), 1)
        acc = jnp.zeros((N_BUCKETS, 128), F32)
        dsink = jnp.zeros((8, 128), F32)
        for h in range(N_HEADS):
            kh, rows, cols = _head_place(h)
            dt = dtab_ref[kh, rows, cols]
            for b in range(N_BUCKETS):
                val = jnp.sum(jnp.where(bucket == b, dt, 0.0))
                acc = acc + jnp.where((row == b) & (lane == h), val, 0.0)
            dsink = dsink + jnp.where((row8 == 0) & (lane8 == h), jnp.sum(dt[:, 0:1]), 0.0)
        out_ref[...] = acc
        dsink_ref[...] = dsink

    vm = pl.BlockSpec(memory_space=pltpu.VMEM)
    return pl.pallas_call(
        body, name="bias_fold", out_shape=[SDS((N_BUCKETS, 128), F32), SDS((8, 128), F32)],
        in_specs=[vm], out_specs=[vm, vm],
    )(dtab)


def _pair_operands(prev, cur):
    t = jnp.concatenate([prev, cur], axis=0).astype(F32)
    t = jnp.where(lax.broadcasted_iota(jnp.int32, t.shape, 0) == 0, 0.0, t)
    tr = pltpu.roll(t, HEAD_DIM, 1)
    lo = lax.broadcasted_iota(jnp.int32, t.shape, 1) < HEAD_DIM
    zero = jnp.zeros_like(t)
    head0 = jnp.concatenate([jnp.where(lo, t, zero), jnp.where(lo, zero, tr)], axis=0).astype(BF16)
    head1 = jnp.concatenate([jnp.where(lo, tr, zero), jnp.where(lo, zero, t)], axis=0).astype(BF16)
    return head0, head1


def _pair_fold(d0, d1):
    lo = lax.broadcasted_iota(jnp.int32, (2 * BLK, KV_W), 1) < HEAD_DIM
    zero = jnp.zeros((2 * BLK, KV_W), F32)
    g0 = jnp.where(lo, d0[0:256], zero) + pltpu.roll(jnp.where(lo, zero, d0[256:512]), HEAD_DIM, 1)
    g1 = pltpu.roll(jnp.where(lo, d1[0:256], zero), HEAD_DIM, 1) + jnp.where(lo, zero, d1[256:512])
    return jnp.where(lax.broadcasted_iota(jnp.int32, (2 * BLK, KV_W), 0) == 0, 0.0, g0 + g1)


def _stack_pairs(ref, kh):
    return jnp.concatenate([ref[:, 128 * (4 * kh + j):128 * (4 * kh + j + 1)] for j in range(4)], axis=0)


def _table_spec():
    return pl.BlockSpec((1, N_KV, 4 * BLK, 4 * BLK), lambda n: (jnp.minimum(n, 1), 0, 0, 0))


def _attn_fwd(q, kv, tab):
    s = q.shape[0]

    def body(q_ref, kp_ref, kc_ref, vp_ref, vc_ref, tab_ref, att_ref, stats_ref):
        k2 = _pair_operands(kp_ref[...], kc_ref[...])
        v2 = _pair_operands(vp_ref[...], vc_ref[...])
        lane = lax.broadcasted_iota(jnp.int32, (BLK, 128), 1)
        stats = jnp.zeros((BLK, 128), F32)
        for kh in range(N_KV):
            sc = _nt(_stack_pairs(q_ref, kh), k2[kh])
            ps = []
            for e in range(2):
                lg = sc[:, 256 * e:256 * (e + 1)] + tab_ref[0, kh, :, 256 * e:256 * (e + 1)]
                m = jnp.max(lg, axis=-1, keepdims=True)
                ex = jnp.exp(lg - m)
                den = jnp.sum(ex, axis=-1, keepdims=True)
                ps.append(ex * (1.0 / den))
                lse = m + jnp.log(den)
                for j in range(4):
                    stats = jnp.where(lane == GROUP * kh + 2 * j + e, lse[BLK * j:BLK * (j + 1)], stats)
            out = _nn(jnp.concatenate(ps, axis=1).astype(BF16), v2[kh])
            for j in range(4):
                att_ref[:, 128 * (4 * kh + j):128 * (4 * kh + j + 1)] = out[BLK * j:BLK * (j + 1)].astype(BF16)
        stats_ref[...] = stats

    cur = lambda n: (n, 0)
    prev = lambda n: (jnp.maximum(n - 1, 0), 0)
    return pl.pallas_call(
        body, name="attn_fwd", grid=(s // BLK,),
        in_specs=[pl.BlockSpec((BLK, D), cur),
                  pl.BlockSpec((BLK, KV_W), prev), pl.BlockSpec((BLK, KV_W), cur),
                  pl.BlockSpec((BLK, KV_W), lambda n: (jnp.maximum(n - 1, 0), 1)),
                  pl.BlockSpec((BLK, KV_W), lambda n: (n, 1)), _table_spec()],
        out_specs=[pl.BlockSpec((BLK, D), cur), pl.BlockSpec((BLK, 128), cur)],
        out_shape=[SDS((s, D), BF16), SDS((s, 128), F32)],
        compiler_params=_params(("parallel",)),
    )(q, kv, kv, kv, kv, tab)


def _mid(att, zb, h1, tgt, w_out, g_post, tm):
    s = att.shape[0]
    nt = s // tm

    def body(att_ref, z_ref, h1_ref, t_ref, w_ref, g_ref,
             dh_ref, dqz_ref, datt_ref, loss_ref, dg_ref, dw_ref, dw_acc):
        @pl.when(pl.program_id(0) == 0)
        def _():
            loss_ref[...] = jnp.zeros_like(loss_ref)
            dg_ref[...] = jnp.zeros_like(dg_ref)
            dw_acc[...] = jnp.zeros_like(dw_acc)
        att = att_ref[...].astype(F32)
        z = z_ref[...].astype(F32)
        sg, sz = _silu_parts(z)
        ob = (att * sz).astype(BF16)
        y2 = _nn(ob, w_ref[...])
        r2 = _rms_scale(y2)
        yh = y2 * r2
        g = g_ref[...]
        err = (h1_ref[...] + yh * g) - t_ref[...]
        loss_ref[...] += jnp.sum(jnp.sum(err * err, axis=-1, keepdims=True) / D)
        dh = err / D
        dh_ref[...] = dh
        _acc_row(dg_ref, 0, jnp.sum(dh * yh, axis=0, keepdims=True))
        dyh = dh * g
        dy = (r2 * (dyh - yh * jnp.mean(dyh * yh, axis=-1, keepdims=True))).astype(BF16)
        dw_acc[...] += _tn(ob, dy)
        dob = _nt(dy, w_ref[...])
        datt_ref[...] = (dob * sz).astype(BF16)
        dqz_ref[...] = (dob * att * _dsilu(z, sg)).astype(BF16)

        @pl.when(pl.program_id(0) == nt - 1)
        def _():
            pltpu.sync_copy(dw_acc, dw_ref)

    row = lambda i: (i, 0)
    fix = lambda i: (0, 0)
    return pl.pallas_call(
        body, name="mid", grid=(nt,),
        in_specs=[pl.BlockSpec((tm, D), row)] * 4 + [pl.BlockSpec((D, D), fix), pl.BlockSpec((1, D), fix)],
        out_specs=[pl.BlockSpec((tm, D), row), pl.BlockSpec((tm, D), lambda i: (i, 1)), pl.BlockSpec((tm, D), row),
                   pl.BlockSpec((8, 128), fix), pl.BlockSpec((8, D), fix), pl.BlockSpec(memory_space=pl.ANY)],
        out_shape=[SDS((s, D), F32), SDS((s, 2 * D), BF16), SDS((s, D), BF16), SDS((8, 128), F32),
                   SDS((8, D), F32), SDS((D, D), F32)],
        scratch_shapes=[pltpu.VMEM((D, D), F32)],
        compiler_params=_params(("arbitrary",)),
    )(att, zb, h1, tgt, w_out, g_post)


def _attn_bwd(q, kv, datt, stats, tab, dqz):
    s = q.shape[0]
    nb = s // BLK

    def body(q_ref, kp_ref, kc_ref, vp_ref, vc_ref, da_ref, st_ref, tab_ref, dqz_in,
             dq_ref, dkv_ref, dtab_ref, dk_carry, dv_carry):
        del dqz_in
        n = pl.program_id(0)

        @pl.when(n == 0)
        def _():
            dtab_ref[...] = jnp.zeros_like(dtab_ref)
            dk_carry[...] = jnp.zeros_like(dk_carry)
            dv_carry[...] = jnp.zeros_like(dv_carry)

        @pl.when(n < nb)
        def _():
            k2 = _pair_operands(kp_ref[...], kc_ref[...])
            v2 = _pair_operands(vp_ref[...], vc_ref[...])
            lane = lax.broadcasted_iota(jnp.int32, (BLK, 128), 1)
            stats = st_ref[...]
            dk2, dv2 = [], []
            for kh in range(N_KV):
                qs = _stack_pairs(q_ref, kh)
                das = _stack_pairs(da_ref, kh)
                sc = _nt(qs, k2[kh])
                dp = _nt(das, v2[kh])
                ps, dss = [], []
                for e in range(2):
                    heads = [GROUP * kh + 2 * j + e for j in range(4)]
                    lse = jnp.concatenate([jnp.sum(jnp.where(lane == h, stats, 0.0), axis=-1, keepdims=True)
                                           for h in heads], axis=0)
                    cols = slice(256 * e, 256 * (e + 1))
                    p = jnp.exp(sc[:, cols] + tab_ref[0, kh, :, cols] - lse)
                    delta = jnp.sum(p * dp[:, cols], axis=-1, keepdims=True)
                    ds = p * (dp[:, cols] - delta)
                    dtab_ref[kh, :, cols] += ds
                    ps.append(p)
                    dss.append(ds)
                p2 = jnp.concatenate(ps, axis=1).astype(BF16)
                ds2 = jnp.concatenate(dss, axis=1).astype(BF16)
                dq = _nn(ds2, k2[kh]) * Q_SCALE
                for j in range(4):
                    dq_ref[:, 128 * (4 * kh + j):128 * (4 * kh + j + 1)] = dq[BLK * j:BLK * (j + 1)].astype(BF16)
                dk2.append(_tn(ds2, qs))
                dv2.append(_tn(p2, das))
            dkk = _pair_fold(dk2[0], dk2[1])
            dvv = _pair_fold(dv2[0], dv2[1])
            dkv_ref[:, 0:KV_W] = (dk_carry[...] + dkk[0:BLK]).astype(BF16)
            dkv_ref[:, KV_W:2 * KV_W] = (dv_carry[...] + dvv[0:BLK]).astype(BF16)
            dk_carry[...] = dkk[BLK:2 * BLK]
            dv_carry[...] = dvv[BLK:2 * BLK]

        @pl.when(n == nb)
        def _():
            dkv_ref[:, 0:KV_W] = dk_carry[...].astype(BF16)
            dkv_ref[:, KV_W:2 * KV_W] = dv_carry[...].astype(BF16)

    cur = lambda n: (jnp.minimum(n, nb - 1), 0)
    prev = lambda n: (jnp.clip(n - 1, 0, nb - 1), 0)
    return pl.pallas_call(
        body, name="attn_bwd", grid=(nb + 1,),
        in_specs=[pl.BlockSpec((BLK, D), cur),
                  pl.BlockSpec((BLK, KV_W), prev), pl.BlockSpec((BLK, KV_W), cur),
                  pl.BlockSpec((BLK, KV_W), lambda n: (jnp.clip(n - 1, 0, nb - 1), 1)),
                  pl.BlockSpec((BLK, KV_W), lambda n: (jnp.minimum(n, nb - 1), 1)),
                  pl.BlockSpec((BLK, D), cur), pl.BlockSpec((BLK, 128), cur), _table_spec(),
                  pl.BlockSpec(memory_space=pl.ANY)],
        out_specs=[pl.BlockSpec((BLK, D), cur), pl.BlockSpec((BLK, 2 * KV_W), prev),
                   pl.BlockSpec((N_KV, 4 * BLK, 4 * BLK), lambda n: (0, 0, 0))],
        out_shape=[SDS((s, 2 * D), BF16), SDS((s, 2 * KV_W), BF16), SDS((N_KV, 4 * BLK, 4 * BLK), F32)],
        scratch_shapes=[pltpu.VMEM((BLK, KV_W), F32), pltpu.VMEM((BLK, KV_W), F32)],
        input_output_aliases={8: 0},
        compiler_params=_params(("arbitrary",)),
    )(q, kv, kv, kv, kv, datt, stats, tab, dqz)


def _b_bwd(dqz, dkv, h1, dh2, oa, wbin_g, w_kv, g_kv, g_pre, g_apost, tm):
    s = h1.shape[0]
    nt = s // tm

    def body(dqz_ref, dkv_ref, h_ref, dh2_ref, oa_ref, wb_ref, wkv_ref, gk_ref, gb_ref, ga_ref,
             dh1_ref, doa_ref, dg_ref, dwb_ref, dwkv_ref, dwb_acc, dwkv_acc):
        @pl.when(pl.program_id(0) == 0)
        def _():
            dg_ref[...] = jnp.zeros_like(dg_ref)
            dwb_acc[...] = jnp.zeros_like(dwb_acc)
            dwkv_acc[...] = jnp.zeros_like(dwkv_acc)
        dnb = _nt(dqz_ref[:, 0:512], wb_ref[0])
        for j in range(1, 4):
            dnb = dnb + _nt(dqz_ref[:, 512 * j:512 * (j + 1)], wb_ref[j])
        dnk = _nt(dkv_ref[...], wkv_ref[...])
        h = h_ref[...]
        r = _rms_scale(h)
        hh = h * r
        nb = (hh * gb_ref[...]).astype(BF16)
        for j in range(4):
            dwb_acc[j] += _tn(nb, dqz_ref[:, 512 * j:512 * (j + 1)])
        dwkv_acc[...] += _tn((hh * gk_ref[...]).astype(BF16), dkv_ref[...])
        _acc_row(dg_ref, 0, jnp.sum(dnk * hh, axis=0, keepdims=True))
        _acc_row(dg_ref, 1, jnp.sum(dnb * hh, axis=0, keepdims=True))
        dhh = dnb * gb_ref[...] + dnk * gk_ref[...]
        dh1 = dh2_ref[...] + r * (dhh - hh * jnp.mean(dhh * hh, axis=-1, keepdims=True))
        dh1_ref[...] = dh1
        oa = oa_ref[...]
        ra = _rms_scale(oa)
        oh = oa * ra
        _acc_row(dg_ref, 2, jnp.sum(dh1 * oh, axis=0, keepdims=True))
        doh = dh1 * ga_ref[...]
        doa_ref[...] = (ra * (doh - oh * jnp.mean(doh * oh, axis=-1, keepdims=True))).astype(BF16)

        @pl.when(pl.program_id(0) == nt - 1)
        def _():
            pltpu.sync_copy(dwb_acc, dwb_ref)
            pltpu.sync_copy(dwkv_acc, dwkv_ref)

    row = lambda i: (i, 0)
    fix = lambda i: (0, 0)
    anyspace = pl.BlockSpec(memory_space=pl.ANY)
    return pl.pallas_call(
        body, name="b_bwd", grid=(nt,),
        in_specs=[pl.BlockSpec((tm, 2 * D), row), pl.BlockSpec((tm, 2 * KV_W), row), pl.BlockSpec((tm, D), row),
                  pl.BlockSpec((tm, D), row), pl.BlockSpec((tm, D), row),
                  pl.BlockSpec((4, D, 512), lambda i: (0, 0, 0)), pl.BlockSpec((D, 2 * KV_W), fix),
                  pl.BlockSpec((1, D), fix), pl.BlockSpec((1, D), fix), pl.BlockSpec((1, D), fix)],
        out_specs=[pl.BlockSpec((tm, D), row), pl.BlockSpec((tm, D), row), pl.BlockSpec((8, D), fix), anyspace, anyspace],
        out_shape=[SDS((s, D), F32), SDS((s, D), BF16), SDS((8, D), F32), SDS((4, D, 512), F32),
                   SDS((D, 2 * KV_W), F32)],
        scratch_shapes=[pltpu.VMEM((4, D, 512), F32), pltpu.VMEM((D, 2 * KV_W), F32)],
        compiler_params=_params(("arbitrary",)),
    )(dqz, dkv, h1, dh2, oa, wbin_g, w_kv, g_kv, g_pre, g_apost)


def _chip_exchange(parts, recvs, send, recv):
    x, y, c = lax.axis_index("x"), lax.axis_index("y"), lax.axis_index("c")
    chips = [(x, 1 - y), (1 - x, y), (1 - x, 1 - y)]
    copies = []
    for a, (t, r) in enumerate(zip(parts, recvs)):
        for j, (px, py) in enumerate(chips):
            copies.append(pltpu.make_async_remote_copy(
                src_ref=t.at[2 * px + py], dst_ref=r.at[j], send_sem=send.at[3 * a + j],
                recv_sem=recv.at[3 * a + j], device_id=(px, py, c), device_id_type=MESH))
    return copies


def _exchange_specs(parts):
    anyspace = pl.BlockSpec(memory_space=pl.ANY)
    n = len(parts)
    return ([anyspace] * n, [anyspace] * n, [SDS((3,) + t.shape[1:], t.dtype) for t in parts],
            [pltpu.SemaphoreType.DMA((3 * n,)), pltpu.SemaphoreType.DMA((3 * n,))])


def _a_bwd(doa, proj, conv_w, w_out, tm, parts):
    s = doa.shape[0]
    nt = s // tm
    n = len(parts)
    ex_in, ex_out, ex_shape, ex_sems = _exchange_specs(parts)

    def body(*refs):
        doa_ref, proj_ref, halo_ref, cw_ref, w_ref = refs[:5]
        part_refs = refs[5:5 + n]
        dproj_ref, dcw_ref = refs[5 + n:7 + n]
        recv_refs = refs[7 + n:7 + 2 * n]
        carry, send, recv = refs[7 + 2 * n:]
        i = pl.program_id(0)
        r = nt - 1 - i

        @pl.when(i == 0)
        def _():
            dcw_ref[...] = jnp.zeros_like(dcw_ref)
            carry[...] = jnp.zeros_like(carry)
            for cp in _chip_exchange(part_refs, recv_refs, send, recv):
                cp.start()
        dya = _nt(doa_ref[...], w_ref[...])
        bg = proj_ref[:, 0:D].astype(F32)
        cg = proj_ref[:, D:2 * D].astype(F32)
        u = proj_ref[:, 2 * D:3 * D].astype(F32)
        z = proj_ref[:, 3 * D:4 * D].astype(F32)
        v = cg * u
        before = jnp.where(r > 0, halo_ref[:, D:2 * D].astype(F32) * halo_ref[:, 2 * D:3 * D].astype(F32), 0.0)
        rows = lax.broadcasted_iota(jnp.int32, (tm, D), 0)
        v1, v2 = _shift_rows(v, before[HALO - 1:HALO, :], before[HALO - 2:HALO - 1, :], rows)
        conv = cw_ref[0:1, :] * v2 + cw_ref[1:2, :] * v1 + cw_ref[2:3, :] * v
        sg, sz = _silu_parts(z)
        dproj_ref[:, 0:D] = (dya * conv * sz).astype(BF16)
        dproj_ref[:, 3 * D:4 * D] = (dya * bg * conv * _dsilu(z, sg)).astype(BF16)
        dconv = dya * bg * sz
        _acc_row(dcw_ref, 0, jnp.sum(dconv * v2, axis=0, keepdims=True))
        _acc_row(dcw_ref, 1, jnp.sum(dconv * v1, axis=0, keepdims=True))
        _acc_row(dcw_ref, 2, jnp.sum(dconv * v, axis=0, keepdims=True))
        after = carry[...]
        up1 = jnp.where(rows < tm - 1, pltpu.roll(dconv, tm - 1, 0), after[0:1, :])
        up2 = jnp.where(rows < tm - 2, pltpu.roll(dconv, tm - 2, 0),
                        jnp.where(rows == tm - 2, after[0:1, :], after[1:2, :]))
        carry[...] = dconv[0:8, :]
        dv = cw_ref[2:3, :] * dconv + cw_ref[1:2, :] * up1 + cw_ref[0:1, :] * up2
        dproj_ref[:, D:2 * D] = (dv * u).astype(BF16)
        dproj_ref[:, 2 * D:3 * D] = (dv * cg).astype(BF16)

        @pl.when(i == nt - 1)
        def _():
            for cp in _chip_exchange(part_refs, recv_refs, send, recv):
                cp.wait()

    rev = lambda i: (nt - 1 - i, 0)
    fix = lambda i: (0, 0)
    halo = lambda i: (jnp.maximum((nt - 1 - i) * (tm // HALO) - 1, 0), 0)
    dproj, dcw, *got = pl.pallas_call(
        body, name="a_bwd", grid=(nt,),
        in_specs=[pl.BlockSpec((tm, D), rev), pl.BlockSpec((tm, 4 * D), rev), pl.BlockSpec((HALO, 4 * D), halo),
                  pl.BlockSpec((8, D), fix), pl.BlockSpec((D, D), fix)] + ex_in,
        out_specs=[pl.BlockSpec((tm, 4 * D), rev), pl.BlockSpec((8, D), fix)] + ex_out,
        out_shape=[SDS((s, 4 * D), BF16), SDS((8, D), F32)] + ex_shape,
        scratch_shapes=[pltpu.VMEM((8, D), F32)] + ex_sems,
        compiler_params=_params(("arbitrary",)),
    )(doa, proj, proj, conv_w, w_out, *parts)
    return dproj, dcw, got


def _dn1(dp_ref, w_ref):
    dn = _nt(dp_ref[:, 0:D], w_ref[0])
    for j in range(1, 4):
        dn = dn + _nt(dp_ref[:, D * j:D * (j + 1)], w_ref[j])
    return dn


def _a_in_bwd_matmul(dproj, win_g, tm, count, parts):
    n = len(parts)
    ex_in, ex_out, ex_shape, ex_sems = _exchange_specs(parts)

    def body(*refs):
        dp_ref, w_ref = refs[:2]
        part_refs = refs[2:2 + n]
        dn_ref = refs[2 + n]
        recv_refs = refs[3 + n:3 + 2 * n]
        sems = refs[3 + 2 * n:]

        @pl.when(pl.program_id(0) == 0)
        def _():
            for cp in _chip_exchange(part_refs, recv_refs, *sems):
                cp.start()
        dn_ref[...] = _dn1(dp_ref, w_ref)

        @pl.when(pl.program_id(0) == count - 1)
        def _():
            for cp in _chip_exchange(part_refs, recv_refs, *sems):
                cp.wait()

    row = lambda i: (i, 0)
    dn, *got = pl.pallas_call(
        body, name="a_in_bwd_matmul", grid=(count,),
        in_specs=[pl.BlockSpec((tm, 4 * D), row), pl.BlockSpec((4, D, D), lambda i: (0, 0, 0))] + ex_in,
        out_specs=[pl.BlockSpec((tm, D), row)] + ex_out,
        out_shape=[SDS((count * tm, D), F32)] + ex_shape,
        scratch_shapes=ex_sems,
        compiler_params=_params(("arbitrary",)),
    )(dproj, win_g, *parts)
    return dn, got


def _a_in_bwd(dn_first, dproj, x, dh1, win_g, g_pre, tm):
    s = x.shape[0]
    nt = s // tm
    count = dn_first.shape[0] // tm

    def body(dn_ref, dp_ref, x_ref, dh_ref, w_ref, g_ref, gx_ref, dg_ref, dn_s):
        i = pl.program_id(0)

        @pl.when(i == 0)
        def _():
            dg_ref[...] = jnp.zeros_like(dg_ref)

        @pl.when(i < count)
        def _():
            dn_s[...] = dn_ref[...]

        @pl.when(i >= count)
        def _():
            dn_s[...] = _dn1(dp_ref, w_ref)
        dn = dn_s[...]
        xv = x_ref[...]
        r = _rms_scale(xv)
        xh = xv * r
        _acc_row(dg_ref, 0, jnp.sum(dn * xh, axis=0, keepdims=True))
        dxh = dn * g_ref[...]
        gx_ref[...] = dh_ref[...] + r * (dxh - xh * jnp.mean(dxh * xh, axis=-1, keepdims=True))

    row = lambda i: (i, 0)
    fix = lambda i: (0, 0)
    return pl.pallas_call(
        body, name="a_in_bwd", grid=(nt,),
        in_specs=[pl.BlockSpec((tm, D), lambda i: (jnp.minimum(i, count - 1), 0)),
                  pl.BlockSpec((tm, 4 * D), lambda i: (jnp.maximum(i, count), 0)),
                  pl.BlockSpec((tm, D), row), pl.BlockSpec((tm, D), row),
                  pl.BlockSpec((4, D, D), lambda i: (0, 0, 0)), pl.BlockSpec((1, D), fix)],
        out_specs=[pl.BlockSpec((tm, D), row), pl.BlockSpec((8, D), fix)],
        out_shape=[SDS((s, D), F32), SDS((8, D), F32)],
        scratch_shapes=[pltpu.VMEM((tm, D), F32)],
        compiler_params=_params(("arbitrary",)),
    )(dn_first, dproj, x, dh1, win_g, g_pre)


def _dw(a, b, tn, tmw, name):
    s, k = a.shape
    n = b.shape[1]

    def body(a_ref, b_ref, o_ref):
        @pl.when(pl.program_id(1) == 0)
        def _():
            o_ref[...] = jnp.zeros_like(o_ref)
        o_ref[0] += _tn(a_ref[...], b_ref[...])

    return pl.pallas_call(
        body, name=name, grid=(n // tn, s // tmw),
        in_specs=[pl.BlockSpec((tmw, k), lambda j, t: (t, 0)), pl.BlockSpec((tmw, tn), lambda j, t: (t, j))],
        out_specs=pl.BlockSpec((1, k, tn), lambda j, t: (j, 0, 0)),
        out_shape=SDS((n // tn, k, tn), F32),
        compiler_params=_params(("parallel", "arbitrary")),
    )(a, b)


def _sibling_exchange(name, to_sibling=(), shards=(), smalls=None):
    n_g, n_h = len(to_sibling), len(shards)
    has_small = smalls is not None

    def body(*refs):
        gs = refs[:n_g]
        pos = n_g + n_h
        small_in = refs[pos] if has_small else None
        pos += has_small
        rs, fs = refs[pos:pos + n_g], refs[pos + n_g:pos + n_g + n_h]
        pos += n_g + n_h
        small_all = refs[pos] if has_small else None
        pos += has_small
        dsend, drecv, ssend, srecv = refs[pos:]
        x, y, c = lax.axis_index("x"), lax.axis_index("y"), lax.axis_index("c")
        sibling = (x, y, 1 - c)
        sends, arrivals = [], []
        for a, (g, r) in enumerate(zip(gs, rs)):
            h = g.shape[1] // 2
            src = g.at[:, pl.ds(pl.multiple_of((1 - c) * h, 8), h), :]
            sends.append(pltpu.make_async_remote_copy(src_ref=src, dst_ref=r, send_sem=dsend.at[a], recv_sem=drecv.at[a],
                                                      device_id=sibling, device_id_type=MESH))
            arrivals.append(pltpu.make_async_remote_copy(src_ref=r, dst_ref=r, send_sem=dsend.at[a], recv_sem=drecv.at[a],
                                                         device_id=sibling, device_id_type=MESH))
        for b, full in enumerate(fs):
            h = full.shape[0] // 2
            mine = full.at[pl.ds(pl.multiple_of(c * h, 8), h)]
            theirs = full.at[pl.ds(pl.multiple_of((1 - c) * h, 8), h)]
            sends.append(pltpu.make_async_remote_copy(src_ref=mine, dst_ref=mine, send_sem=dsend.at[n_g + b],
                                                      recv_sem=drecv.at[n_g + b], device_id=sibling, device_id_type=MESH))
            arrivals.append(pltpu.make_async_remote_copy(src_ref=mine, dst_ref=theirs, send_sem=dsend.at[n_g + b],
                                                         recv_sem=drecv.at[n_g + b], device_id=sibling, device_id_type=MESH))
        if has_small:
            me = 4 * x + 2 * y + c
            small_all[me] = small_in[...]
            for rel in range(1, N_DEV):
                fx, fy, fc = rel >> 2, (rel >> 1) & 1, rel & 1
                peer = (x + fx - 2 * x * fx, y + fy - 2 * y * fy, c + fc - 2 * c * fc)
                sender = 4 * peer[0] + 2 * peer[1] + peer[2]
                sends.append(pltpu.make_async_remote_copy(
                    src_ref=small_in, dst_ref=small_all.at[me], send_sem=ssend.at[rel - 1], recv_sem=srecv.at[rel - 1],
                    device_id=peer, device_id_type=MESH))
                arrivals.append(pltpu.make_async_remote_copy(
                    src_ref=small_in, dst_ref=small_all.at[sender], send_sem=ssend.at[rel - 1], recv_sem=srecv.at[rel - 1],
                    device_id=peer, device_id_type=MESH))
        for cp in sends:
            cp.start()
        for cp in arrivals:
            cp.wait_recv()
        for cp in sends:
            cp.wait_send()

    anyspace = pl.BlockSpec(memory_space=pl.ANY)
    vm = pl.BlockSpec(memory_space=pltpu.VMEM)
    out_shape = [SDS((N_CHIPS, g.shape[1] // 2, g.shape[2]), F32) for g in to_sibling]
    out_shape += [SDS(full.shape, F32) for full in shards]
    if has_small:
        out_shape.append(SDS((N_DEV,) + smalls.shape, F32))
    n_d2d = max(n_g + n_h, 1)
    outs = pl.pallas_call(
        body, name=name, out_shape=out_shape,
        in_specs=[anyspace] * (n_g + n_h) + [vm] * has_small, out_specs=[anyspace] * (n_g + n_h) + [vm] * has_small,
        scratch_shapes=[pltpu.SemaphoreType.DMA((n_d2d,)), pltpu.SemaphoreType.DMA((n_d2d,)),
                        pltpu.SemaphoreType.DMA((N_DEV - 1,)), pltpu.SemaphoreType.DMA((N_DEV - 1,))],
        input_output_aliases={n_g + b: n_g + b for b in range(n_h)},
    )(*to_sibling, *shards, *([smalls] if has_small else []))
    return outs[:n_g], outs[n_g:n_g + n_h], (outs[n_g + n_h] if has_small else None)


def _add_sibling(where, g, r, name):
    _, rows, cols = g.shape
    h = rows // 2
    tr = min(h, 256)
    nh = h // tr

    def body(where_ref, g_ref, r_ref, t_ref, own_ref):
        t = g_ref[0] + r_ref[0]
        t_ref[0] = t.astype(BF16)

        @pl.when(pl.program_id(1) == where_ref[1])
        def _():
            own_ref[...] = t

    return pl.pallas_call(
        body, name=name,
        grid_spec=pltpu.PrefetchScalarGridSpec(
            num_scalar_prefetch=1, grid=(nh, N_CHIPS),
            in_specs=[pl.BlockSpec((1, tr, cols), lambda i, k, w: (k, w[0] * nh + i, 0)),
                      pl.BlockSpec((1, tr, cols), lambda i, k, w: (k, i, 0))],
            out_specs=[pl.BlockSpec((1, tr, cols), lambda i, k, w: (k, i, 0)),
                       pl.BlockSpec((tr, cols), lambda i, k, w: (i, 0))]),
        out_shape=[SDS((N_CHIPS, h, cols), BF16), SDS((h, cols), F32)],
        compiler_params=_params(("parallel", "arbitrary")),
    )(where, g, r)


def _add_chips(where, own, r, name):
    h, cols = own.shape
    tr = min(h, 256)
    nh = h // tr

    def body(where_ref, t_ref, r_ref, o_ref):
        del where_ref
        o_ref[...] = ((t_ref[...] + r_ref[0].astype(F32)) + r_ref[1].astype(F32)) + r_ref[2].astype(F32)

    return pl.pallas_call(
        body, name=name,
        grid_spec=pltpu.PrefetchScalarGridSpec(
            num_scalar_prefetch=1, grid=(nh,),
            in_specs=[pl.BlockSpec((tr, cols), lambda i, w: (i, 0)), pl.BlockSpec((3, tr, cols), lambda i, w: (0, i, 0))],
            out_specs=pl.BlockSpec((tr, cols), lambda i, w: (w[0] * nh + i, 0))),
        out_shape=SDS((2 * h, cols), F32),
        compiler_params=_params(("parallel",)),
    )(where, own, r)


def _sum_smalls(small_all):
    def body(all_ref, o_ref):
        acc = all_ref[0]
        for dev in range(1, N_DEV):
            acc = acc + all_ref[dev]
        o_ref[...] = acc

    return pl.pallas_call(
        body, name="sum_smalls", out_shape=SDS(small_all.shape[1:], F32),
        in_specs=[pl.BlockSpec(memory_space=pltpu.VMEM)], out_specs=pl.BlockSpec(memory_space=pltpu.VMEM),
    )(small_all)


def _adam_step(g, w, m, v):
    nm = ADAM_B1 * m + (1.0 - ADAM_B1) * g
    nv = ADAM_B2 * v + (1.0 - ADAM_B2) * (g * g)
    m_hat = nm / (1.0 - ADAM_B1 ** ADAM_STEP)
    v_hat = nv / (1.0 - ADAM_B2 ** ADAM_STEP)
    return -ADAM_LR * (m_hat / (jnp.sqrt(v_hat) + ADAM_EPS) + ADAM_WD * w), nm, nv


def _adamw(g, w, m, v, name):
    rows, cols = g.shape
    tr = min(rows, 256)

    def body(g_ref, w_ref, m_ref, v_ref, d_ref, nm_ref, nv_ref):
        d_ref[...], nm_ref[...], nv_ref[...] = _adam_step(g_ref[...], w_ref[...], m_ref[...], v_ref[...])

    spec = pl.BlockSpec((tr, cols), lambda i: (i, 0))
    return pl.pallas_call(
        body, name=name, grid=(rows // tr,), in_specs=[spec] * 4, out_specs=[spec] * 3,
        out_shape=[SDS(g.shape, F32)] * 3, compiler_params=_params(("parallel",)),
    )(g, w, m, v)


def _small_update(chip, tot, wmv):
    names = list(SMALL_PLACES)
    n = len(names)

    def body(chip_ref, tot_ref, quarter_ref, *refs):
        del chip_ref
        ins, outs = refs[:3 * n], refs[3 * n:]
        for i, nm in enumerate(names):
            sharded, row, (rows, cols) = SMALL_PLACES[nm]
            g = (quarter_ref if sharded else tot_ref)[row:row + rows, 0:cols]
            outs[4 * i][...] = g
            outs[4 * i + 1][...], outs[4 * i + 2][...], outs[4 * i + 3][...] = _adam_step(
                g, ins[3 * i][...], ins[3 * i + 1][...], ins[3 * i + 2][...])

    whole = lambda shape: pl.BlockSpec(shape, lambda i, c: (0,) * len(shape))
    shapes = [SMALL_PLACES[nm][2] for nm in names]
    outs = pl.pallas_call(
        body, name="small_update",
        grid_spec=pltpu.PrefetchScalarGridSpec(
            num_scalar_prefetch=1, grid=(1,),
            in_specs=[whole(tot.shape), pl.BlockSpec((tot.shape[0], D // 4), lambda i, c: (0, c[0]))]
            + [whole(shp) for shp in shapes for _ in range(3)],
            out_specs=[whole(shp) for shp in shapes for _ in range(4)]),
        out_shape=[SDS(shp, F32) for shp in shapes for _ in range(4)],
    )(chip, tot, tot, *[a for nm in names for a in wmv[nm]])
    return {nm: tuple(outs[4 * i:4 * i + 4]) for i, nm in enumerate(names)}


def _pad_rows(a, rows):
    return jnp.concatenate([a, jnp.zeros((rows - a.shape[0], a.shape[1]), a.dtype)], axis=0)


def _pad_cols(a, cols):
    return jnp.concatenate([a, jnp.zeros((a.shape[0], cols - a.shape[1]), a.dtype)], axis=1)


def kernel(x, a_pre_norm, a_w_in, a_conv_w, a_w_out, a_post_norm, kv_norm, w_kv, rel_bias, b_pre_norm, b_w_in, b_sinks, b_w_out, b_post_norm, loss_target, m_a_pre_norm, m_a_w_in, m_a_conv_w, m_a_w_out, m_a_post_norm, m_kv_norm, m_w_kv, m_rel_bias, m_b_pre_norm, m_b_w_in, m_b_sinks, m_b_w_out, m_b_post_norm, v_a_pre_norm, v_a_w_in, v_a_conv_w, v_a_w_out, v_a_post_norm, v_kv_norm, v_w_kv, v_rel_bias, v_b_pre_norm, v_b_w_in, v_b_sinks, v_b_w_out, v_b_post_norm):
    seq = x.shape[1]
    xs = x.reshape(seq, D)
    tgt = loss_target.reshape(seq, D)
    chip = 2 * lax.axis_index("x") + lax.axis_index("y")
    core = lax.axis_index("c")
    tm = _tile(seq, 512)
    tm_mix = _tile(seq, 256)
    tmw = _tile(seq, 1024)

    shards = [a_w_in[0], a_w_out[0], w_kv, b_w_in[0], b_w_out[0]]
    small_w = _pad_rows(jnp.concatenate([a_pre_norm, a_conv_w[0], a_post_norm], axis=0), 8)
    win_g, *own_only, small_g = _gather_weights(shards, small_w, 1)
    small_full = small_g.transpose(1, 0, 2).reshape(8, D)
    g_apre, conv_w, g_apost = small_full[0:1], _pad_rows(small_full[1:4], 8), small_full[4:5]
    g_kv = kv_norm.reshape(1, D)

    proj, n1, (wouta_g, wkv_g, wbin_g, woutb_g) = _a_in(xs, g_apre, win_g, tm, own_only)
    wouta = wouta_g.reshape(D, D)
    wkv = wkv_g.reshape(D, 2 * KV_W)
    woutb = woutb_g.reshape(D, D)
    ya, oa, h1 = _a_mix(proj, xs, conv_w, wouta, g_apost, tm_mix)
    kv, q, zb = _b_in(h1, g_kv, b_pre_norm, wkv, wbin_g, tm)
    tab = _bias_table(rel_bias, b_sinks.reshape(N_HEADS))
    att, stats = _attn_fwd(q, kv, tab)
    dh2, dqz, datt, loss_acc, dg_bpost, dw_outb = _mid(att, zb, h1, tgt, woutb, b_post_norm, tm)

    dqz, dkv, dtab = _attn_bwd(q, kv, datt, stats, tab, dqz)
    dh1, doa, dg_b, dw_bin, dw_kv = _b_bwd(dqz, dkv, h1, dh2, oa, wbin_g, wkv, g_kv, b_pre_norm, g_apost, tm)
    where = jnp.stack([core, chip]).astype(jnp.int32)
    dw_outa = _dw(ya, doa, D, tmw, "dw_a_out").reshape(N_CHIPS, D // 4, D)
    dw_kv = dw_kv.reshape(N_CHIPS, D // 4, 2 * KV_W)
    dw_outb = dw_outb.reshape(N_CHIPS, D // 4, D)
    grads1 = [dw_outa, dw_kv, dw_bin, dw_outb]
    names1 = ["a_w_out", "w_kv", "b_w_in", "b_w_out"]
    from_sibling1, _, _ = _sibling_exchange("to_sibling_1", to_sibling=grads1)
    sums1 = [_add_sibling(where, g, r, "add_sibling_" + nm) for g, r, nm in zip(grads1, from_sibling1, names1)]
    dproj, dconv_w, from_chips1 = _a_bwd(doa, proj, conv_w, wouta, tm_mix, [t for t, _ in sums1])
    shards1 = [_add_chips(where, own, r, "add_chips_" + nm) for (_, own), r, nm in zip(sums1, from_chips1, names1)]
    dw_in = _dw(n1, dproj, D, tmw, "dw_a_in")
    from_sibling2, (g_wouta, g_wkv, g_wbin, g_woutb), _ = _sibling_exchange(
        "to_sibling_2", to_sibling=[dw_in], shards=shards1)
    part2, own2 = _add_sibling(where, dw_in, from_sibling2[0], "add_sibling_a_w_in")
    nt = seq // tm
    dn_first, from_chips2 = _a_in_bwd_matmul(dproj, win_g, tm, max(nt - max(nt // 4, 1), 1), [part2])
    grad_x, dg_apre = _a_in_bwd(dn_first, dproj, xs, dh1, win_g, g_apre, tm)
    shard2 = _add_chips(where, own2, from_chips2[0], "add_chips_a_w_in")
    drel, dsink = _bias_fold(dtab)

    smalls = jnp.concatenate([
        dg_apre[0:1], dg_b[2:3], dg_b[0:1], dg_b[1:2], dg_bpost[0:1], _pad_cols(dsink[0:1], D),
        _pad_cols(loss_acc[0:1], D), jnp.zeros((1, D), F32), dconv_w, _pad_cols(drel, D)], axis=0)
    _, (g_win,), small_all = _sibling_exchange("share_last", shards=[shard2], smalls=smalls)
    tot = _sum_smalls(small_all)

    big = {}
    for nm, g, w, m, v in [("a_w_in", g_win, a_w_in, m_a_w_in, v_a_w_in), ("a_w_out", g_wouta, a_w_out, m_a_w_out, v_a_w_out),
                           ("w_kv", g_wkv, w_kv, m_w_kv, v_w_kv), ("b_w_in", g_wbin, b_w_in, m_b_w_in, v_b_w_in),
                           ("b_w_out", g_woutb, b_w_out, m_b_w_out, v_b_w_out)]:
        shp = w.shape
        two = (shp[-2], shp[-1])
        d, nm_, nv_ = _adamw(g, w.reshape(two), m.reshape(two), v.reshape(two), "adamw_" + nm)
        big[nm] = (g.reshape(shp), d.reshape(shp), nm_.reshape(shp), nv_.reshape(shp))

    given = {"a_pre_norm": (a_pre_norm, m_a_pre_norm, v_a_pre_norm), "a_conv_w": (a_conv_w, m_a_conv_w, v_a_conv_w),
             "a_post_norm": (a_post_norm, m_a_post_norm, v_a_post_norm), "kv_norm": (kv_norm, m_kv_norm, v_kv_norm),
             "rel_bias": (rel_bias, m_rel_bias, v_rel_bias), "b_pre_norm": (b_pre_norm, m_b_pre_norm, v_b_pre_norm),
             "b_sinks": (b_sinks, m_b_sinks, v_b_sinks), "b_post_norm": (b_post_norm, m_b_post_norm, v_b_post_norm)}
    small = _small_update(where[1:2], tot, {nm: tuple(a.reshape(SMALL_PLACES[nm][2]) for a in wmv)
                                            for nm, wmv in given.items()})
    order = ["a_pre_norm", "a_w_in", "a_conv_w", "a_w_out", "a_post_norm", "kv_norm", "w_kv", "rel_bias",
             "b_pre_norm", "b_w_in", "b_sinks", "b_w_out", "b_post_norm"]
    outs = []
    for which in range(4):
        for nm in order:
            outs.append(big[nm][which] if nm in big else small[nm][which].reshape(given[nm][0].shape))
    loss = 0.5 * tot[LOSS_ROW, 0]
    return (loss, grad_x.reshape(x.shape), *outs)
```

```python
import functools
import math

import jax
import jax.numpy as jnp
from jax import lax
from jax.experimental import pallas as pl
from jax.experimental.pallas import tpu as pltpu

F32 = jnp.float32
BF16 = jnp.bfloat16
MESH = pl.DeviceIdType.MESH
SDS = jax.ShapeDtypeStruct

D = 1024
HEAD_DIM = 64
N_HEADS = 16
N_KV = 2
GROUP = 8
KV_W = 128
BLK = 128
N_BUCKETS = 32
MAX_EXACT = 16
MAX_DISTANCE = 128
EPS = 1e-6
NEG_INF = -1e30
Q_SCALE = HEAD_DIM ** -0.5

ADAM_LR = 0.001
ADAM_B1 = 0.9
ADAM_B2 = 0.999
ADAM_EPS = 1e-08
ADAM_WD = 0.01
ADAM_STEP = 10

N_CHIPS = 4
N_DEV = 8
VMEM_LIMIT = 56 * 1024 * 1024
SMALL_ROWS = 48
LOSS_ROW = 6
SMALL_PLACES = {
    "a_pre_norm": (True, 0, (1, D // 4)), "a_conv_w": (True, 8, (3, D // 4)), "a_post_norm": (True, 1, (1, D // 4)),
    "kv_norm": (False, 2, (1, D)), "rel_bias": (False, 16, (N_BUCKETS, N_HEADS)), "b_pre_norm": (False, 3, (1, D)),
    "b_sinks": (False, 5, (1, N_HEADS)), "b_post_norm": (False, 4, (1, D)),
}
HALO = 16


def _bucket_thresholds():
    def bucket(d):
        big = MAX_EXACT + int(math.log(d / MAX_EXACT) / math.log(MAX_DISTANCE / MAX_EXACT)
                              * (N_BUCKETS - MAX_EXACT))
        return d if d < MAX_EXACT else min(big, N_BUCKETS - 1)
    out = []
    for b in range(MAX_EXACT + 1, N_BUCKETS):
        out.append(min(d for d in range(MAX_EXACT, MAX_DISTANCE) if bucket(d) >= b))
    return tuple(out)


BUCKET_THRESHOLDS = _bucket_thresholds()


def _params(semantics=None, vmem=VMEM_LIMIT):
    return pltpu.CompilerParams(dimension_semantics=semantics, vmem_limit_bytes=vmem)


def _tile(n, pref):
    return pref if n >= 2 * pref else max(n // 2, 8)


def _rms_scale(v):
    return lax.rsqrt(jnp.mean(v * v, axis=-1, keepdims=True) + EPS)


def _nt(a, b):
    return lax.dot_general(a, b, (((1,), (1,)), ((), ())), preferred_element_type=F32)


def _tn(a, b):
    return lax.dot_general(a, b, (((0,), (0,)), ((), ())), preferred_element_type=F32)


def _nn(a, b):
    return jnp.dot(a, b, preferred_element_type=F32)


def _silu_parts(z):
    sg = jax.nn.sigmoid(z)
    return sg, z * sg


def _dsilu(z, sg):
    return sg * (1.0 + z * (1.0 - sg))


def _acc_row(ref, row, val):
    ref[row:row + 1, :] += val


def _gather_copies(outs, splits, ici_send, ici_recv, d2d_send, d2d_recv):
    x, y, c = lax.axis_index("x"), lax.axis_index("y"), lax.axis_index("c")
    k = 2 * x + y
    sibling = (x, y, 1 - c)

    def part(o_ref, chip, core, split):
        if not split:
            return o_ref.at[chip]
        h = o_ref.shape[1] // 2
        return o_ref.at[chip, pl.ds(pl.multiple_of(core * h, 16), h)]

    def remote(ref, a, j, sems, to):
        return pltpu.make_async_remote_copy(src_ref=ref, dst_ref=ref, send_sem=sems[0].at[3 * a + j],
                                            recv_sem=sems[1].at[3 * a + j], device_id=to, device_id_type=MESH)

    copies = []
    for a, (o_ref, split) in enumerate(zip(outs, splits)):
        for j, (px, py) in enumerate([(x, 1 - y), (1 - x, y), (1 - x, 1 - y)]):
            kj = 2 * px + py
            ici, d2d = (ici_send, ici_recv), (d2d_send, d2d_recv)
            copies.append((remote(part(o_ref, k, c, split), a, j, ici, (px, py, c)),
                           remote(part(o_ref, kj, c, split), a, j, ici, (px, py, c)),
                           remote(part(o_ref, kj, c, split), a, j, d2d, sibling) if split else None,
                           remote(part(o_ref, kj, 1 - c, split), a, j, d2d, sibling) if split else None))
    return copies


def _gather_sems(n):
    return [pltpu.SemaphoreType.DMA((3 * n,)) for _ in range(4)]


def _gather_weights(shards, small, n_now):
    n = len(shards)

    def body(*refs):
        ins, small_in = refs[:n], refs[n]
        outs, small_out = refs[n + 1:2 * n + 1], refs[2 * n + 1]
        sems = refs[2 * n + 2:]
        k = 2 * lax.axis_index("x") + lax.axis_index("y")
        for i_ref, o_ref in zip(ins, outs):
            o_ref[k] = i_ref[...].astype(BF16)
        small_out[k] = small_in[...]
        copies = _gather_copies(list(outs[:n_now]) + [small_out], [True] * n_now + [False], *sems)
        for send, _, _, _ in copies:
            send.start()
        for _, arrival, forward, _ in copies:
            arrival.wait_recv()
            if forward is not None:
                forward.start()
        for send, _, forward, forwarded in copies:
            if forward is not None:
                forwarded.wait_recv()
                forward.wait_send()
            send.wait_send()

    vm = pl.BlockSpec(memory_space=pltpu.VMEM)
    out_shape = [SDS((N_CHIPS,) + s.shape, BF16) for s in shards] + [SDS((N_CHIPS,) + small.shape, F32)]
    return pl.pallas_call(
        body, name="gather_weights", out_shape=out_shape,
        in_specs=[vm] * (n + 1), out_specs=[vm] * (n + 1),
        scratch_shapes=_gather_sems(n_now + 1),
        compiler_params=pltpu.CompilerParams(vmem_limit_bytes=VMEM_LIMIT),
    )(*shards, small)


def _a_in(chip, x, g_pre, weights, tm):
    s = x.shape[0]
    nt = s // tm
    n = len(weights)

    def body(chip_ref, x_ref, g_ref, *refs):
        proj_ref, n1_ref = refs[n:n + 2]
        gathered = refs[n + 2:2 * n + 2]
        wbuf, fetch_sem = refs[2 * n + 2:2 * n + 4]
        sems = refs[2 * n + 4:]
        jj, i = pl.program_id(0), pl.program_id(1)
        copies = _gather_copies(gathered, [True] * n, *sems)

        def fetch(rel):
            slot = jnp.bitwise_xor(chip_ref[0], rel)
            return pltpu.make_async_copy(gathered[0].at[slot], wbuf.at[rel % 2], fetch_sem.at[rel % 2])

        @pl.when((jj == 0) & (i == 0))
        def _():
            fetch(0).start()
            for send, _, _, _ in copies:
                send.start()
            fetch(0).wait()

        for rel in (1, 2, 3):
            @pl.when((jj == rel) & (i == 0))
            def _():
                fetch(rel).wait()

        xv = x_ref[...]
        n1 = (xv * _rms_scale(xv) * g_ref[...]).astype(BF16)

        @pl.when(jj == 0)
        def _():
            n1_ref[...] = n1
        proj_ref[...] = _nn(n1, wbuf[jj % 2]).astype(BF16)

        for rel in (1, 2, 3):
            @pl.when((jj == rel - 1) & (i == nt // 2))
            def _():
                _, arrival, forward, forwarded = copies[rel - 1]
                arrival.wait_recv()
                forward.start()
                forwarded.wait_recv()
                fetch(rel).start()

        @pl.when((jj == 3) & (i == nt // 2))
        def _():
            for _, arrival, forward, _ in copies[3:]:
                arrival.wait_recv()
                forward.start()

        @pl.when((jj == 3) & (i == nt - 1))
        def _():
            for _, _, _, forwarded in copies[3:]:
                forwarded.wait_recv()
            for send, _, forward, _ in copies:
                forward.wait_send()
                send.wait_send()

    anyspace = pl.BlockSpec(memory_space=pl.ANY)
    proj, n1, *gathered = pl.pallas_call(
        body, name="a_in",
        grid_spec=pltpu.PrefetchScalarGridSpec(
            num_scalar_prefetch=1, grid=(4, nt),
            in_specs=[pl.BlockSpec((tm, D), lambda jj, i, c: (i, 0)), pl.BlockSpec((1, D), lambda jj, i, c: (0, 0))]
            + [anyspace] * n,
            out_specs=[pl.BlockSpec((tm, D), lambda jj, i, c: (i, jnp.bitwise_xor(c[0], jj))),
                       pl.BlockSpec((tm, D), lambda jj, i, c: (jnp.where(jj == 0, i, nt - 1), 0))] + [anyspace] * n,
            scratch_shapes=[pltpu.VMEM((2, D, D), BF16), pltpu.SemaphoreType.DMA((2,))] + _gather_sems(n)),
        out_shape=[SDS((s, 4 * D), BF16), SDS((s, D), BF16)] + [SDS(w.shape, w.dtype) for w in weights],
        input_output_aliases={3 + a: 2 + a for a in range(n)},
        compiler_params=_params(("arbitrary", "arbitrary")),
    )(chip, x, g_pre, *weights)
    return proj, n1, gathered


def _shift_rows(v, last, second_last, rows):
    v1 = jnp.where(rows >= 1, pltpu.roll(v, 1, 0), last)
    v2 = jnp.where(rows >= 2, pltpu.roll(v, 2, 0), jnp.where(rows == 1, last, second_last))
    return v1, v2


def _a_mix(proj, x, conv_w, w_out, g_post, tm):
    s = x.shape[0]

    def body(proj_ref, x_ref, cw_ref, w_ref, g_ref, ya_ref, oa_ref, h1_ref, carry):
        @pl.when(pl.program_id(0) == 0)
        def _():
            carry[...] = jnp.zeros_like(carry)
        v = proj_ref[:, D:2 * D].astype(F32) * proj_ref[:, 2 * D:3 * D].astype(F32)
        rows = lax.broadcasted_iota(jnp.int32, (tm, D), 0)
        before = carry[...]
        v1, v2 = _shift_rows(v, before[7:8, :], before[6:7, :], rows)
        carry[...] = v[tm - 8:tm, :]
        conv = cw_ref[0:1, :] * v2 + cw_ref[1:2, :] * v1 + cw_ref[2:3, :] * v
        _, sz = _silu_parts(proj_ref[:, 3 * D:4 * D].astype(F32))
        ya = (proj_ref[:, 0:D].astype(F32) * conv * sz).astype(BF16)
        ya_ref[...] = ya
        oa = _nn(ya, w_ref[...])
        oa_ref[...] = oa
        h1_ref[...] = x_ref[...] + oa * _rms_scale(oa) * g_ref[...]

    row = lambda i: (i, 0)
    fix = lambda i: (0, 0)
    return pl.pallas_call(
        body, name="a_mix", grid=(s // tm,),
        in_specs=[pl.BlockSpec((tm, 4 * D), row), pl.BlockSpec((tm, D), row), pl.BlockSpec((8, D), fix),
                  pl.BlockSpec((D, D), fix), pl.BlockSpec((1, D), fix)],
        out_specs=[pl.BlockSpec((tm, D), row)] * 3,
        out_shape=[SDS((s, D), BF16), SDS((s, D), F32), SDS((s, D), F32)],
        scratch_shapes=[pltpu.VMEM((8, D), F32)],
        compiler_params=_params(("arbitrary",)),
    )(proj, x, conv_w, w_out, g_post)


def _b_in(h1, g_kv, g_pre, w_kv, wbin_g, tm):
    s = h1.shape[0]

    def body(h_ref, gk_ref, gb_ref, wkv_ref, wb_ref, kv_ref, q_ref, z_ref):
        h = h_ref[...]
        hh = h * _rms_scale(h)
        nk = (hh * gk_ref[...]).astype(BF16)
        nb = (hh * gb_ref[...]).astype(BF16)
        kv_ref[...] = _nn(nk, wkv_ref[...]).astype(BF16)
        for j in range(2):
            q_ref[:, 512 * j:512 * (j + 1)] = (_nn(nb, wb_ref[j]) * Q_SCALE).astype(BF16)
            z_ref[:, 512 * j:512 * (j + 1)] = _nn(nb, wb_ref[2 + j]).astype(BF16)

    row = lambda i: (i, 0)
    fix = lambda i: (0, 0)
    return pl.pallas_call(
        body, name="b_in", grid=(s // tm,),
        in_specs=[pl.BlockSpec((tm, D), row), pl.BlockSpec((1, D), fix), pl.BlockSpec((1, D), fix),
                  pl.BlockSpec((D, 2 * KV_W), fix), pl.BlockSpec((4, D, 512), lambda i: (0, 0, 0))],
        out_specs=[pl.BlockSpec((tm, 2 * KV_W), row), pl.BlockSpec((tm, D), row), pl.BlockSpec((tm, D), row)],
        out_shape=[SDS((s, 2 * KV_W), BF16), SDS((s, D), BF16), SDS((s, D), BF16)],
        compiler_params=_params(("parallel",)),
    )(h1, g_kv, g_pre, w_kv, wbin_g)


def _band_buckets():
    q = lax.broadcasted_iota(jnp.int32, (BLK, 2 * BLK), 0)
    k = lax.broadcasted_iota(jnp.int32, (BLK, 2 * BLK), 1)
    dist = q + BLK - k
    bucket = jnp.where(dist < MAX_EXACT, dist, MAX_EXACT)
    for t in BUCKET_THRESHOLDS:
        bucket = bucket + jnp.where(dist >= t, 1, 0)
    in_window = (dist >= 0) & (dist < BLK)
    return jnp.where(in_window, bucket, -1)


def _head_place(h):
    kh, j, e = h // GROUP, (h % GROUP) // 2, h % 2
    return kh, slice(BLK * j, BLK * (j + 1)), slice(2 * BLK * e, 2 * BLK * (e + 1))


def _bias_table(rel_bias, sinks):
    def body(rb_ref, sink_ref, tab_ref):
        bucket = _band_buckets()
        col = lax.broadcasted_iota(jnp.int32, (BLK, 2 * BLK), 1)
        for h in range(N_HEADS):
            acc = jnp.where(bucket < 0, NEG_INF, 0.0).astype(F32)
            for b in range(N_BUCKETS):
                acc = jnp.where(bucket == b, rb_ref[b, h], acc)
            acc = jnp.where(col == 0, sink_ref[h], acc)
            kh, rows, cols = _head_place(h)
            tab_ref[1, kh, rows, cols] = acc
            tab_ref[0, kh, rows, cols] = jnp.where((col > 0) & (col < BLK), NEG_INF, acc)

    return pl.pallas_call(
        body, name="bias_table", out_shape=SDS((2, N_KV, 4 * BLK, 4 * BLK), F32),
        in_specs=[pl.BlockSpec(memory_space=pltpu.SMEM), pl.BlockSpec(memory_space=pltpu.SMEM)],
        out_specs=pl.BlockSpec(memory_space=pltpu.VMEM),
    )(rel_bias, sinks)


def _bias_fold(dtab):
    def body(dtab_ref, out_ref, dsink_ref):
        bucket = _band_buckets()
        row = lax.broadcasted_iota(jnp.int32, (N_BUCKETS, 128), 0)
        lane = lax.broadcasted_iota(jnp.int32, (N_BUCKETS, 128), 1)
        row8 = lax.broadcasted_iota(jnp.int32, (8, 128), 0)
        lane8 = lax.broadcasted_iota(jnp.int32, (8, 128), 1)
        acc = jnp.zeros((N_BUCKETS, 128), F32)
        dsink = jnp.zeros((8, 128), F32)
        for h in range(N_HEADS):
            kh, rows, cols = _head_place(h)
            dt = dtab_ref[kh, rows, cols]
            for b in range(N_BUCKETS):
                val = jnp.sum(jnp.where(bucket == b, dt, 0.0))
                acc = acc + jnp.where((row == b) & (lane == h), val, 0.0)
            dsink = dsink + jnp.where((row8 == 0) & (lane8 == h), jnp.sum(dt[:, 0:1]), 0.0)
        out_ref[...] = acc
        dsink_ref[...] = dsink

    vm = pl.BlockSpec(memory_space=pltpu.VMEM)
    return pl.pallas_call(
        body, name="bias_fold", out_shape=[SDS((N_BUCKETS, 128), F32), SDS((8, 128), F32)],
        in_specs=[vm], out_specs=[vm, vm],
    )(dtab)


def _pair_operands(prev, cur):
    t = jnp.concatenate([prev, cur], axis=0).astype(F32)
    t = jnp.where(lax.broadcasted_iota(jnp.int32, t.shape, 0) == 0, 0.0, t)
    tr = pltpu.roll(t, HEAD_DIM, 1)
    lo = lax.broadcasted_iota(jnp.int32, t.shape, 1) < HEAD_DIM
    zero = jnp.zeros_like(t)
    head0 = jnp.concatenate([jnp.where(lo, t, zero), jnp.where(lo, zero, tr)], axis=0).astype(BF16)
    head1 = jnp.concatenate([jnp.where(lo, tr, zero), jnp.where(lo, zero, t)], axis=0).astype(BF16)
    return head0, head1


def _pair_fold(d0, d1):
    lo = lax.broadcasted_iota(jnp.int32, (2 * BLK, KV_W), 1) < HEAD_DIM
    zero = jnp.zeros((2 * BLK, KV_W), F32)
    g0 = jnp.where(lo, d0[0:256], zero) + pltpu.roll(jnp.where(lo, zero, d0[256:512]), HEAD_DIM, 1)
    g1 = pltpu.roll(jnp.where(lo, d1[0:256], zero), HEAD_DIM, 1) + jnp.where(lo, zero, d1[256:512])
    return jnp.where(lax.broadcasted_iota(jnp.int32, (2 * BLK, KV_W), 0) == 0, 0.0, g0 + g1)


def _stack_pairs(ref, kh):
    return jnp.concatenate([ref[:, 128 * (4 * kh + j):128 * (4 * kh + j + 1)] for j in range(4)], axis=0)


def _table_spec():
    return pl.BlockSpec((1, N_KV, 4 * BLK, 4 * BLK), lambda n: (jnp.minimum(n, 1), 0, 0, 0))


def _attn_fwd(q, kv, tab):
    s = q.shape[0]

    def body(q_ref, kp_ref, kc_ref, vp_ref, vc_ref, tab_ref, att_ref, stats_ref):
        k2 = _pair_operands(kp_ref[...], kc_ref[...])
        v2 = _pair_operands(vp_ref[...], vc_ref[...])
        lane = lax.broadcasted_iota(jnp.int32, (BLK, 128), 1)
        stats = jnp.zeros((BLK, 128), F32)
        for kh in range(N_KV):
            sc = _nt(_stack_pairs(q_ref, kh), k2[kh])
            ps = []
            for e in range(2):
                lg = sc[:, 256 * e:256 * (e + 1)] + tab_ref[0, kh, :, 256 * e:256 * (e + 1)]
                m = jnp.max(lg, axis=-1, keepdims=True)
                ex = jnp.exp(lg - m)
                den = jnp.sum(ex, axis=-1, keepdims=True)
                ps.append(ex * (1.0 / den))
                lse = m + jnp.log(den)
                for j in range(4):
                    stats = jnp.where(lane == GROUP * kh + 2 * j + e, lse[BLK * j:BLK * (j + 1)], stats)
            out = _nn(jnp.concatenate(ps, axis=1).astype(BF16), v2[kh])
            for j in range(4):
                att_ref[:, 128 * (4 * kh + j):128 * (4 * kh + j + 1)] = out[BLK * j:BLK * (j + 1)].astype(BF16)
        stats_ref[...] = stats

    cur = lambda n: (n, 0)
    prev = lambda n: (jnp.maximum(n - 1, 0), 0)
    return pl.pallas_call(
        body, name="attn_fwd", grid=(s // BLK,),
        in_specs=[pl.BlockSpec((BLK, D), cur),
                  pl.BlockSpec((BLK, KV_W), prev), pl.BlockSpec((BLK, KV_W), cur),
                  pl.BlockSpec((BLK, KV_W), lambda n: (jnp.maximum(n - 1, 0), 1)),
                  pl.BlockSpec((BLK, KV_W), lambda n: (n, 1)), _table_spec()],
        out_specs=[pl.BlockSpec((BLK, D), cur), pl.BlockSpec((BLK, 128), cur)],
        out_shape=[SDS((s, D), BF16), SDS((s, 128), F32)],
        compiler_params=_params(("parallel",)),
    )(q, kv, kv, kv, kv, tab)


def _mid(att, zb, h1, tgt, w_out, g_post, tm):
    s = att.shape[0]
    nt = s // tm

    def body(att_ref, z_ref, h1_ref, t_ref, w_ref, g_ref,
             dh_ref, dqz_ref, datt_ref, loss_ref, dg_ref, dw_ref, dw_acc):
        @pl.when(pl.program_id(0) == 0)
        def _():
            loss_ref[...] = jnp.zeros_like(loss_ref)
            dg_ref[...] = jnp.zeros_like(dg_ref)
            dw_acc[...] = jnp.zeros_like(dw_acc)
        att = att_ref[...].astype(F32)
        z = z_ref[...].astype(F32)
        sg, sz = _silu_parts(z)
        ob = (att * sz).astype(BF16)
        y2 = _nn(ob, w_ref[...])
        r2 = _rms_scale(y2)
        yh = y2 * r2
        g = g_ref[...]
        err = (h1_ref[...] + yh * g) - t_ref[...]
        loss_ref[...] += jnp.sum(jnp.sum(err * err, axis=-1, keepdims=True) / D)
        dh = err / D
        dh_ref[...] = dh
        _acc_row(dg_ref, 0, jnp.sum(dh * yh, axis=0, keepdims=True))
        dyh = dh * g
        dy = (r2 * (dyh - yh * jnp.mean(dyh * yh, axis=-1, keepdims=True))).astype(BF16)
        dw_acc[...] += _tn(ob, dy)
        dob = _nt(dy, w_ref[...])
        datt_ref[...] = (dob * sz).astype(BF16)
        dqz_ref[...] = (dob * att * _dsilu(z, sg)).astype(BF16)

        @pl.when(pl.program_id(0) == nt - 1)
        def _():
            pltpu.sync_copy(dw_acc, dw_ref)

    row = lambda i: (i, 0)
    fix = lambda i: (0, 0)
    return pl.pallas_call(
        body, name="mid", grid=(nt,),
        in_specs=[pl.BlockSpec((tm, D), row)] * 4 + [pl.BlockSpec((D, D), fix), pl.BlockSpec((1, D), fix)],
        out_specs=[pl.BlockSpec((tm, D), row), pl.BlockSpec((tm, D), lambda i: (i, 1)), pl.BlockSpec((tm, D), row),
                   pl.BlockSpec((8, 128), fix), pl.BlockSpec((8, D), fix), pl.BlockSpec(memory_space=pl.ANY)],
        out_shape=[SDS((s, D), F32), SDS((s, 2 * D), BF16), SDS((s, D), BF16), SDS((8, 128), F32),
                   SDS((8, D), F32), SDS((D, D), F32)],
        scratch_shapes=[pltpu.VMEM((D, D), F32)],
        compiler_params=_params(("arbitrary",)),
    )(att, zb, h1, tgt, w_out, g_post)


def _attn_bwd(q, kv, datt, stats, tab, dqz):
    s = q.shape[0]
    nb = s // BLK

    def body(q_ref, kp_ref, kc_ref, vp_ref, vc_ref, da_ref, st_ref, tab_ref, dqz_in,
             dq_ref, dkv_ref, dtab_ref, dk_carry, dv_carry):
        del dqz_in
        n = pl.program_id(0)

        @pl.when(n == 0)
        def _():
            dtab_ref[...] = jnp.zeros_like(dtab_ref)
            dk_carry[...] = jnp.zeros_like(dk_carry)
            dv_carry[...] = jnp.zeros_like(dv_carry)

        @pl.when(n < nb)
        def _():
            k2 = _pair_operands(kp_ref[...], kc_ref[...])
            v2 = _pair_operands(vp_ref[...], vc_ref[...])
            lane = lax.broadcasted_iota(jnp.int32, (BLK, 128), 1)
            stats = st_ref[...]
            dk2, dv2 = [], []
            for kh in range(N_KV):
                qs = _stack_pairs(q_ref, kh)
                das = _stack_pairs(da_ref, kh)
                sc = _nt(qs, k2[kh])
                dp = _nt(das, v2[kh])
                ps, dss = [], []
                for e in range(2):
                    heads = [GROUP * kh + 2 * j + e for j in range(4)]
                    lse = jnp.concatenate([jnp.sum(jnp.where(lane == h, stats, 0.0), axis=-1, keepdims=True)
                                           for h in heads], axis=0)
                    cols = slice(256 * e, 256 * (e + 1))
                    p = jnp.exp(sc[:, cols] + tab_ref[0, kh, :, cols] - lse)
                    delta = jnp.sum(p * dp[:, cols], axis=-1, keepdims=True)
                    ds = p * (dp[:, cols] - delta)
                    dtab_ref[kh, :, cols] += ds
                    ps.append(p)
                    dss.append(ds)
                p2 = jnp.concatenate(ps, axis=1).astype(BF16)
                ds2 = jnp.concatenate(dss, axis=1).astype(BF16)
                dq = _nn(ds2, k2[kh]) * Q_SCALE
                for j in range(4):
                    dq_ref[:, 128 * (4 * kh + j):128 * (4 * kh + j + 1)] = dq[BLK * j:BLK * (j + 1)].astype(BF16)
                dk2.append(_tn(ds2, qs))
                dv2.append(_tn(p2, das))
            dkk = _pair_fold(dk2[0], dk2[1])
            dvv = _pair_fold(dv2[0], dv2[1])
            dkv_ref[:, 0:KV_W] = (dk_carry[...] + dkk[0:BLK]).astype(BF16)
            dkv_ref[:, KV_W:2 * KV_W] = (dv_carry[...] + dvv[0:BLK]).astype(BF16)
            dk_carry[...] = dkk[BLK:2 * BLK]
            dv_carry[...] = dvv[BLK:2 * BLK]

        @pl.when(n == nb)
        def _():
            dkv_ref[:, 0:KV_W] = dk_carry[...].astype(BF16)
            dkv_ref[:, KV_W:2 * KV_W] = dv_carry[...].astype(BF16)

    cur = lambda n: (jnp.minimum(n, nb - 1), 0)
    prev = lambda n: (jnp.clip(n - 1, 0, nb - 1), 0)
    return pl.pallas_call(
        body, name="attn_bwd", grid=(nb + 1,),
        in_specs=[pl.BlockSpec((BLK, D), cur),
                  pl.BlockSpec((BLK, KV_W), prev), pl.BlockSpec((BLK, KV_W), cur),
                  pl.BlockSpec((BLK, KV_W), lambda n: (jnp.clip(n - 1, 0, nb - 1), 1)),
                  pl.BlockSpec((BLK, KV_W), lambda n: (jnp.minimum(n, nb - 1), 1)),
                  pl.BlockSpec((BLK, D), cur), pl.BlockSpec((BLK, 128), cur), _table_spec(),
                  pl.BlockSpec(memory_space=pl.ANY)],
        out_specs=[pl.BlockSpec((BLK, D), cur), pl.BlockSpec((BLK, 2 * KV_W), prev),
                   pl.BlockSpec((N_KV, 4 * BLK, 4 * BLK), lambda n: (0, 0, 0))],
        out_shape=[SDS((s, 2 * D), BF16), SDS((s, 2 * KV_W), BF16), SDS((N_KV, 4 * BLK, 4 * BLK), F32)],
        scratch_shapes=[pltpu.VMEM((BLK, KV_W), F32), pltpu.VMEM((BLK, KV_W), F32)],
        input_output_aliases={8: 0},
        compiler_params=_params(("arbitrary",)),
    )(q, kv, kv, kv, kv, datt, stats, tab, dqz)


def _b_bwd(dqz, dkv, h1, dh2, oa, wbin_g, w_kv, g_kv, g_pre, g_apost, tm):
    s = h1.shape[0]
    nt = s // tm

    def body(dqz_ref, dkv_ref, h_ref, dh2_ref, oa_ref, wb_ref, wkv_ref, gk_ref, gb_ref, ga_ref,
             dh1_ref, doa_ref, dg_ref, dwb_ref, dwkv_ref, dwb_acc, dwkv_acc):
        @pl.when(pl.program_id(0) == 0)
        def _():
            dg_ref[...] = jnp.zeros_like(dg_ref)
            dwb_acc[...] = jnp.zeros_like(dwb_acc)
            dwkv_acc[...] = jnp.zeros_like(dwkv_acc)
        dnb = _nt(dqz_ref[:, 0:512], wb_ref[0])
        for j in range(1, 4):
            dnb = dnb + _nt(dqz_ref[:, 512 * j:512 * (j + 1)], wb_ref[j])
        dnk = _nt(dkv_ref[...], wkv_ref[...])
        h = h_ref[...]
        r = _rms_scale(h)
        hh = h * r
        nb = (hh * gb_ref[...]).astype(BF16)
        for j in range(4):
            dwb_acc[j] += _tn(nb, dqz_ref[:, 512 * j:512 * (j + 1)])
        dwkv_acc[...] += _tn((hh * gk_ref[...]).astype(BF16), dkv_ref[...])
        _acc_row(dg_ref, 0, jnp.sum(dnk * hh, axis=0, keepdims=True))
        _acc_row(dg_ref, 1, jnp.sum(dnb * hh, axis=0, keepdims=True))
        dhh = dnb * gb_ref[...] + dnk * gk_ref[...]
        dh1 = dh2_ref[...] + r * (dhh - hh * jnp.mean(dhh * hh, axis=-1, keepdims=True))
        dh1_ref[...] = dh1
        oa = oa_ref[...]
        ra = _rms_scale(oa)
        oh = oa * ra
        _acc_row(dg_ref, 2, jnp.sum(dh1 * oh, axis=0, keepdims=True))
        doh = dh1 * ga_ref[...]
        doa_ref[...] = (ra * (doh - oh * jnp.mean(doh * oh, axis=-1, keepdims=True))).astype(BF16)

        @pl.when(pl.program_id(0) == nt - 1)
        def _():
            pltpu.sync_copy(dwb_acc, dwb_ref)
            pltpu.sync_copy(dwkv_acc, dwkv_ref)

    row = lambda i: (i, 0)
    fix = lambda i: (0, 0)
    anyspace = pl.BlockSpec(memory_space=pl.ANY)
    return pl.pallas_call(
        body, name="b_bwd", grid=(nt,),
        in_specs=[pl.BlockSpec((tm, 2 * D), row), pl.BlockSpec((tm, 2 * KV_W), row), pl.BlockSpec((tm, D), row),
                  pl.BlockSpec((tm, D), row), pl.BlockSpec((tm, D), row),
                  pl.BlockSpec((4, D, 512), lambda i: (0, 0, 0)), pl.BlockSpec((D, 2 * KV_W), fix),
                  pl.BlockSpec((1, D), fix), pl.BlockSpec((1, D), fix), pl.BlockSpec((1, D), fix)],
        out_specs=[pl.BlockSpec((tm, D), row), pl.BlockSpec((tm, D), row), pl.BlockSpec((8, D), fix), anyspace, anyspace],
        out_shape=[SDS((s, D), F32), SDS((s, D), BF16), SDS((8, D), F32), SDS((4, D, 512), F32),
                   SDS((D, 2 * KV_W), F32)],
        scratch_shapes=[pltpu.VMEM((4, D, 512), F32), pltpu.VMEM((D, 2 * KV_W), F32)],
        compiler_params=_params(("arbitrary",)),
    )(dqz, dkv, h1, dh2, oa, wbin_g, w_kv, g_kv, g_pre, g_apost)


def _chip_exchange(parts, recvs, send, recv):
    x, y, c = lax.axis_index("x"), lax.axis_index("y"), lax.axis_index("c")
    chips = [(x, 1 - y), (1 - x, y), (1 - x, 1 - y)]
    copies = []
    for a, (t, r) in enumerate(zip(parts, recvs)):
        for j, (px, py) in enumerate(chips):
            copies.append(pltpu.make_async_remote_copy(
                src_ref=t.at[2 * px + py], dst_ref=r.at[j], send_sem=send.at[3 * a + j],
                recv_sem=recv.at[3 * a + j], device_id=(px, py, c), device_id_type=MESH))
    return copies


def _exchange_specs(parts):
    anyspace = pl.BlockSpec(memory_space=pl.ANY)
    n = len(parts)
    return ([anyspace] * n, [anyspace] * n, [SDS((3,) + t.shape[1:], t.dtype) for t in parts],
            [pltpu.SemaphoreType.DMA((3 * n,)), pltpu.SemaphoreType.DMA((3 * n,))])


def _a_bwd(doa, proj, conv_w, w_out, tm, parts):
    s = doa.shape[0]
    nt = s // tm
    n = len(parts)
    ex_in, ex_out, ex_shape, ex_sems = _exchange_specs(parts)

    def body(*refs):
        doa_ref, proj_ref, halo_ref, cw_ref, w_ref = refs[:5]
        part_refs = refs[5:5 + n]
        dproj_ref, dcw_ref = refs[5 + n:7 + n]
        recv_refs = refs[7 + n:7 + 2 * n]
        carry, send, recv = refs[7 + 2 * n:]
        i = pl.program_id(0)
        r = nt - 1 - i

        @pl.when(i == 0)
        def _():
            dcw_ref[...] = jnp.zeros_like(dcw_ref)
            carry[...] = jnp.zeros_like(carry)
            for cp in _chip_exchange(part_refs, recv_refs, send, recv):
                cp.start()
        dya = _nt(doa_ref[...], w_ref[...])
        bg = proj_ref[:, 0:D].astype(F32)
        cg = proj_ref[:, D:2 * D].astype(F32)
        u = proj_ref[:, 2 * D:3 * D].astype(F32)
        z = proj_ref[:, 3 * D:4 * D].astype(F32)
        v = cg * u
        before = jnp.where(r > 0, halo_ref[:, D:2 * D].astype(F32) * halo_ref[:, 2 * D:3 * D].astype(F32), 0.0)
        rows = lax.broadcasted_iota(jnp.int32, (tm, D), 0)
        v1, v2 = _shift_rows(v, before[HALO - 1:HALO, :], before[HALO - 2:HALO - 1, :], rows)
        conv = cw_ref[0:1, :] * v2 + cw_ref[1:2, :] * v1 + cw_ref[2:3, :] * v
        sg, sz = _silu_parts(z)
        dproj_ref[:, 0:D] = (dya * conv * sz).astype(BF16)
        dproj_ref[:, 3 * D:4 * D] = (dya * bg * conv * _dsilu(z, sg)).astype(BF16)
        dconv = dya * bg * sz
        _acc_row(dcw_ref, 0, jnp.sum(dconv * v2, axis=0, keepdims=True))
        _acc_row(dcw_ref, 1, jnp.sum(dconv * v1, axis=0, keepdims=True))
        _acc_row(dcw_ref, 2, jnp.sum(dconv * v, axis=0, keepdims=True))
        after = carry[...]
        up1 = jnp.where(rows < tm - 1, pltpu.roll(dconv, tm - 1, 0), after[0:1, :])
        up2 = jnp.where(rows < tm - 2, pltpu.roll(dconv, tm - 2, 0),
                        jnp.where(rows == tm - 2, after[0:1, :], after[1:2, :]))
        carry[...] = dconv[0:8, :]
        dv = cw_ref[2:3, :] * dconv + cw_ref[1:2, :] * up1 + cw_ref[0:1, :] * up2
        dproj_ref[:, D:2 * D] = (dv * u).astype(BF16)
        dproj_ref[:, 2 * D:3 * D] = (dv * cg).astype(BF16)

        @pl.when(i == nt - 1)
        def _():
            for cp in _chip_exchange(part_refs, recv_refs, send, recv):
                cp.wait()

    rev = lambda i: (nt - 1 - i, 0)
    fix = lambda i: (0, 0)
    halo = lambda i: (jnp.maximum((nt - 1 - i) * (tm // HALO) - 1, 0), 0)
    dproj, dcw, *got = pl.pallas_call(
        body, name="a_bwd", grid=(nt,),
        in_specs=[pl.BlockSpec((tm, D), rev), pl.BlockSpec((tm, 4 * D), rev), pl.BlockSpec((HALO, 4 * D), halo),
                  pl.BlockSpec((8, D), fix), pl.BlockSpec((D, D), fix)] + ex_in,
        out_specs=[pl.BlockSpec((tm, 4 * D), rev), pl.BlockSpec((8, D), fix)] + ex_out,
        out_shape=[SDS((s, 4 * D), BF16), SDS((8, D), F32)] + ex_shape,
        scratch_shapes=[pltpu.VMEM((8, D), F32)] + ex_sems,
        compiler_params=_params(("arbitrary",)),
    )(doa, proj, proj, conv_w, w_out, *parts)
    return dproj, dcw, got


def _dn1(dp_ref, w_ref):
    dn = _nt(dp_ref[:, 0:D], w_ref[0])
    for j in range(1, 4):
        dn = dn + _nt(dp_ref[:, D * j:D * (j + 1)], w_ref[j])
    return dn


def _a_in_bwd_matmul(dproj, win_g, tm, count, parts):
    n = len(parts)
    ex_in, ex_out, ex_shape, ex_sems = _exchange_specs(parts)

    def body(*refs):
        dp_ref, w_ref = refs[:2]
        part_refs = refs[2:2 + n]
        dn_ref = refs[2 + n]
        recv_refs = refs[3 + n:3 + 2 * n]
        sems = refs[3 + 2 * n:]

        @pl.when(pl.program_id(0) == 0)
        def _():
            for cp in _chip_exchange(part_refs, recv_refs, *sems):
                cp.start()
        dn_ref[...] = _dn1(dp_ref, w_ref)

        @pl.when(pl.program_id(0) == count - 1)
        def _():
            for cp in _chip_exchange(part_refs, recv_refs, *sems):
                cp.wait()

    row = lambda i: (i, 0)
    dn, *got = pl.pallas_call(
        body, name="a_in_bwd_matmul", grid=(count,),
        in_specs=[pl.BlockSpec((tm, 4 * D), row), pl.BlockSpec((4, D, D), lambda i: (0, 0, 0))] + ex_in,
        out_specs=[pl.BlockSpec((tm, D), row)] + ex_out,
        out_shape=[SDS((count * tm, D), F32)] + ex_shape,
        scratch_shapes=ex_sems,
        compiler_params=_params(("arbitrary",)),
    )(dproj, win_g, *parts)
    return dn, got


def _a_in_bwd(dn_first, dproj, x, dh1, win_g, g_pre, tm):
    s = x.shape[0]
    nt = s // tm
    count = dn_first.shape[0] // tm

    def body(dn_ref, dp_ref, x_ref, dh_ref, w_ref, g_ref, gx_ref, dg_ref, dn_s):
        i = pl.program_id(0)

        @pl.when(i == 0)
        def _():
            dg_ref[...] = jnp.zeros_like(dg_ref)

        @pl.when(i < count)
        def _():
            dn_s[...] = dn_ref[...]

        @pl.when(i >= count)
        def _():
            dn_s[...] = _dn1(dp_ref, w_ref)
        dn = dn_s[...]
        xv = x_ref[...]
        r = _rms_scale(xv)
        xh = xv * r
        _acc_row(dg_ref, 0, jnp.sum(dn * xh, axis=0, keepdims=True))
        dxh = dn * g_ref[...]
        gx_ref[...] = dh_ref[...] + r * (dxh - xh * jnp.mean(dxh * xh, axis=-1, keepdims=True))

    row = lambda i: (i, 0)
    fix = lambda i: (0, 0)
    return pl.pallas_call(
        body, name="a_in_bwd", grid=(nt,),
        in_specs=[pl.BlockSpec((tm, D), lambda i: (jnp.minimum(i, count - 1), 0)),
                  pl.BlockSpec((tm, 4 * D), lambda i: (jnp.maximum(i, count), 0)),
                  pl.BlockSpec((tm, D), row), pl.BlockSpec((tm, D), row),
                  pl.BlockSpec((4, D, D), lambda i: (0, 0, 0)), pl.BlockSpec((1, D), fix)],
        out_specs=[pl.BlockSpec((tm, D), row), pl.BlockSpec((8, D), fix)],
        out_shape=[SDS((s, D), F32), SDS((8, D), F32)],
        scratch_shapes=[pltpu.VMEM((tm, D), F32)],
        compiler_params=_params(("arbitrary",)),
    )(dn_first, dproj, x, dh1, win_g, g_pre)


def _dw(a, b, tn, tmw, name):
    s, k = a.shape
    n = b.shape[1]

    def body(a_ref, b_ref, o_ref):
        @pl.when(pl.program_id(1) == 0)
        def _():
            o_ref[...] = jnp.zeros_like(o_ref)
        o_ref[0] += _tn(a_ref[...], b_ref[...])

    return pl.pallas_call(
        body, name=name, grid=(n // tn, s // tmw),
        in_specs=[pl.BlockSpec((tmw, k), lambda j, t: (t, 0)), pl.BlockSpec((tmw, tn), lambda j, t: (t, j))],
        out_specs=pl.BlockSpec((1, k, tn), lambda j, t: (j, 0, 0)),
        out_shape=SDS((n // tn, k, tn), F32),
        compiler_params=_params(("parallel", "arbitrary")),
    )(a, b)


def _sibling_exchange(name, to_sibling=(), shards=(), smalls=None):
    n_g, n_h = len(to_sibling), len(shards)
    has_small = smalls is not None

    def body(*refs):
        gs = refs[:n_g]
        pos = n_g + n_h
        small_in = refs[pos] if has_small else None
        pos += has_small
        rs, fs = refs[pos:pos + n_g], refs[pos + n_g:pos + n_g + n_h]
        pos += n_g + n_h
        small_all = refs[pos] if has_small else None
        pos += has_small
        dsend, drecv, ssend, srecv = refs[pos:]
        x, y, c = lax.axis_index("x"), lax.axis_index("y"), lax.axis_index("c")
        sibling = (x, y, 1 - c)
        sends, arrivals = [], []
        for a, (g, r) in enumerate(zip(gs, rs)):
            h = g.shape[1] // 2
            src = g.at[:, pl.ds(pl.multiple_of((1 - c) * h, 8), h), :]
            sends.append(pltpu.make_async_remote_copy(src_ref=src, dst_ref=r, send_sem=dsend.at[a], recv_sem=drecv.at[a],
                                                      device_id=sibling, device_id_type=MESH))
            arrivals.append(pltpu.make_async_remote_copy(src_ref=r, dst_ref=r, send_sem=dsend.at[a], recv_sem=drecv.at[a],
                                                         device_id=sibling, device_id_type=MESH))
        for b, full in enumerate(fs):
            h = full.shape[0] // 2
            mine = full.at[pl.ds(pl.multiple_of(c * h, 8), h)]
            theirs = full.at[pl.ds(pl.multiple_of((1 - c) * h, 8), h)]
            sends.append(pltpu.make_async_remote_copy(src_ref=mine, dst_ref=mine, send_sem=dsend.at[n_g + b],
                                                      recv_sem=drecv.at[n_g + b], device_id=sibling, device_id_type=MESH))
            arrivals.append(pltpu.make_async_remote_copy(src_ref=mine, dst_ref=theirs, send_sem=dsend.at[n_g + b],
                                                         recv_sem=drecv.at[n_g + b], device_id=sibling, device_id_type=MESH))
        if has_small:
            me = 4 * x + 2 * y + c
            small_all[me] = small_in[...]
            for rel in range(1, N_DEV):
                fx, fy, fc = rel >> 2, (rel >> 1) & 1, rel & 1
                peer = (x + fx - 2 * x * fx, y + fy - 2 * y * fy, c + fc - 2 * c * fc)
                sender = 4 * peer[0] + 2 * peer[1] + peer[2]
                sends.append(pltpu.make_async_remote_copy(
                    src_ref=small_in, dst_ref=small_all.at[me], send_sem=ssend.at[rel - 1], recv_sem=srecv.at[rel - 1],
                    device_id=peer, device_id_type=MESH))
                arrivals.append(pltpu.make_async_remote_copy(
                    src_ref=small_in, dst_ref=small_all.at[sender], send_sem=ssend.at[rel - 1], recv_sem=srecv.at[rel - 1],
                    device_id=peer, device_id_type=MESH))
        for cp in sends:
            cp.start()
        for cp in arrivals:
            cp.wait_recv()
        for cp in sends:
            cp.wait_send()

    anyspace = pl.BlockSpec(memory_space=pl.ANY)
    vm = pl.BlockSpec(memory_space=pltpu.VMEM)
    out_shape = [SDS((N_CHIPS, g.shape[1] // 2, g.shape[2]), F32) for g in to_sibling]
    out_shape += [SDS(full.shape, F32) for full in shards]
    if has_small:
        out_shape.append(SDS((N_DEV,) + smalls.shape, F32))
    n_d2d = max(n_g + n_h, 1)
    outs = pl.pallas_call(
        body, name=name, out_shape=out_shape,
        in_specs=[anyspace] * (n_g + n_h) + [vm] * has_small, out_specs=[anyspace] * (n_g + n_h) + [vm] * has_small,
        scratch_shapes=[pltpu.SemaphoreType.DMA((n_d2d,)), pltpu.SemaphoreType.DMA((n_d2d,)),
                        pltpu.SemaphoreType.DMA((N_DEV - 1,)), pltpu.SemaphoreType.DMA((N_DEV - 1,))],
        input_output_aliases={n_g + b: n_g + b for b in range(n_h)},
    )(*to_sibling, *shards, *([smalls] if has_small else []))
    return outs[:n_g], outs[n_g:n_g + n_h], (outs[n_g + n_h] if has_small else None)


def _add_sibling(where, g, r, name):
    _, rows, cols = g.shape
    h = rows // 2
    tr = min(h, 256)
    nh = h // tr

    def body(where_ref, g_ref, r_ref, t_ref, own_ref):
        t = g_ref[0] + r_ref[0]
        t_ref[0] = t.astype(BF16)

        @pl.when(pl.program_id(1) == where_ref[1])
        def _():
            own_ref[...] = t

    return pl.pallas_call(
        body, name=name,
        grid_spec=pltpu.PrefetchScalarGridSpec(
            num_scalar_prefetch=1, grid=(nh, N_CHIPS),
            in_specs=[pl.BlockSpec((1, tr, cols), lambda i, k, w: (k, w[0] * nh + i, 0)),
                      pl.BlockSpec((1, tr, cols), lambda i, k, w: (k, i, 0))],
            out_specs=[pl.BlockSpec((1, tr, cols), lambda i, k, w: (k, i, 0)),
                       pl.BlockSpec((tr, cols), lambda i, k, w: (i, 0))]),
        out_shape=[SDS((N_CHIPS, h, cols), BF16), SDS((h, cols), F32)],
        compiler_params=_params(("parallel", "arbitrary")),
    )(where, g, r)


def _add_chips(where, own, r, name):
    h, cols = own.shape
    tr = min(h, 256)
    nh = h // tr

    def body(where_ref, t_ref, r_ref, o_ref):
        del where_ref
        o_ref[...] = ((t_ref[...] + r_ref[0].astype(F32)) + r_ref[1].astype(F32)) + r_ref[2].astype(F32)

    return pl.pallas_call(
        body, name=name,
        grid_spec=pltpu.PrefetchScalarGridSpec(
            num_scalar_prefetch=1, grid=(nh,),
            in_specs=[pl.BlockSpec((tr, cols), lambda i, w: (i, 0)), pl.BlockSpec((3, tr, cols), lambda i, w: (0, i, 0))],
            out_specs=pl.BlockSpec((tr, cols), lambda i, w: (w[0] * nh + i, 0))),
        out_shape=SDS((2 * h, cols), F32),
        compiler_params=_params(("parallel",)),
    )(where, own, r)


def _sum_smalls(small_all):
    def body(all_ref, o_ref):
        acc = all_ref[0]
        for dev in range(1, N_DEV):
            acc = acc + all_ref[dev]
        o_ref[...] = acc

    return pl.pallas_call(
        body, name="sum_smalls", out_shape=SDS(small_all.shape[1:], F32),
        in_specs=[pl.BlockSpec(memory_space=pltpu.VMEM)], out_specs=pl.BlockSpec(memory_space=pltpu.VMEM),
    )(small_all)


def _adam_step(g, w, m, v):
    nm = ADAM_B1 * m + (1.0 - ADAM_B1) * g
    nv = ADAM_B2 * v + (1.0 - ADAM_B2) * (g * g)
    m_hat = nm / (1.0 - ADAM_B1 ** ADAM_STEP)
    v_hat = nv / (1.0 - ADAM_B2 ** ADAM_STEP)
    return -ADAM_LR * (m_hat / (jnp.sqrt(v_hat) + ADAM_EPS) + ADAM_WD * w), nm, nv


def _adamw(g, w, m, v, name):
    rows, cols = g.shape
    tr = min(rows, 256)

    def body(g_ref, w_ref, m_ref, v_ref, d_ref, nm_ref, nv_ref):
        d_ref[...], nm_ref[...], nv_ref[...] = _adam_step(g_ref[...], w_ref[...], m_ref[...], v_ref[...])

    spec = pl.BlockSpec((tr, cols), lambda i: (i, 0))
    return pl.pallas_call(
        body, name=name, grid=(rows // tr,), in_specs=[spec] * 4, out_specs=[spec] * 3,
        out_shape=[SDS(g.shape, F32)] * 3, compiler_params=_params(("parallel",)),
    )(g, w, m, v)


def _small_update(chip, tot, wmv):
    names = list(SMALL_PLACES)
    n = len(names)

    def body(chip_ref, tot_ref, quarter_ref, *refs):
        del chip_ref
        ins, outs = refs[:3 * n], refs[3 * n:]
        for i, nm in enumerate(names):
            sharded, row, (rows, cols) = SMALL_PLACES[nm]
            g = (quarter_ref if sharded else tot_ref)[row:row + rows, 0:cols]
            outs[4 * i][...] = g
            outs[4 * i + 1][...], outs[4 * i + 2][...], outs[4 * i + 3][...] = _adam_step(
                g, ins[3 * i][...], ins[3 * i + 1][...], ins[3 * i + 2][...])

    whole = lambda shape: pl.BlockSpec(shape, lambda i, c: (0,) * len(shape))
    shapes = [SMALL_PLACES[nm][2] for nm in names]
    outs = pl.pallas_call(
        body, name="small_update",
        grid_spec=pltpu.PrefetchScalarGridSpec(
            num_scalar_prefetch=1, grid=(1,),
            in_specs=[whole(tot.shape), pl.BlockSpec((tot.shape[0], D // 4), lambda i, c: (0, c[0]))]
            + [whole(shp) for shp in shapes for _ in range(3)],
            out_specs=[whole(shp) for shp in shapes for _ in range(4)]),
        out_shape=[SDS(shp, F32) for shp in shapes for _ in range(4)],
    )(chip, tot, tot, *[a for nm in names for a in wmv[nm]])
    return {nm: tuple(outs[4 * i:4 * i + 4]) for i, nm in enumerate(names)}


def _pad_rows(a, rows):
    return jnp.concatenate([a, jnp.zeros((rows - a.shape[0], a.shape[1]), a.dtype)], axis=0)


def _pad_cols(a, cols):
    return jnp.concatenate([a, jnp.zeros((a.shape[0], cols - a.shape[1]), a.dtype)], axis=1)


def kernel(x, a_pre_norm, a_w_in, a_conv_w, a_w_out, a_post_norm, kv_norm, w_kv, rel_bias, b_pre_norm, b_w_in, b_sinks, b_w_out, b_post_norm, loss_target, m_a_pre_norm, m_a_w_in, m_a_conv_w, m_a_w_out, m_a_post_norm, m_kv_norm, m_w_kv, m_rel_bias, m_b_pre_norm, m_b_w_in, m_b_sinks, m_b_w_out, m_b_post_norm, v_a_pre_norm, v_a_w_in, v_a_conv_w, v_a_w_out, v_a_post_norm, v_kv_norm, v_w_kv, v_rel_bias, v_b_pre_norm, v_b_w_in, v_b_sinks, v_b_w_out, v_b_post_norm):
    seq = x.shape[1]
    xs = x.reshape(seq, D)
    tgt = loss_target.reshape(seq, D)
    chip = 2 * lax.axis_index("x") + lax.axis_index("y")
    core = lax.axis_index("c")
    tm = _tile(seq, 512)
    tm_mix = _tile(seq, 256)
    tmw = _tile(seq, 1024)

    shards = [a_w_in[0], a_w_out[0], w_kv, b_w_in[0], b_w_out[0]]
    small_w = _pad_rows(jnp.concatenate([a_pre_norm, a_conv_w[0], a_post_norm], axis=0), 8)
    *own_only, small_g = _gather_weights(shards, small_w, 0)
    where = jnp.stack([core, chip]).astype(jnp.int32)
    small_full = small_g.transpose(1, 0, 2).reshape(8, D)
    g_apre, conv_w, g_apost = small_full[0:1], _pad_rows(small_full[1:4], 8), small_full[4:5]
    g_kv = kv_norm.reshape(1, D)

    proj, n1, (win_g, wouta_g, wkv_g, wbin_g, woutb_g) = _a_in(where[1:2], xs, g_apre, own_only, tm)
    wouta = wouta_g.reshape(D, D)
    wkv = wkv_g.reshape(D, 2 * KV_W)
    woutb = woutb_g.reshape(D, D)
    ya, oa, h1 = _a_mix(proj, xs, conv_w, wouta, g_apost, tm_mix)
    kv, q, zb = _b_in(h1, g_kv, b_pre_norm, wkv, wbin_g, tm)
    tab = _bias_table(rel_bias, b_sinks.reshape(N_HEADS))
    att, stats = _attn_fwd(q, kv, tab)
    dh2, dqz, datt, loss_acc, dg_bpost, dw_outb = _mid(att, zb, h1, tgt, woutb, b_post_norm, tm)

    dqz, dkv, dtab = _attn_bwd(q, kv, datt, stats, tab, dqz)
    dh1, doa, dg_b, dw_bin, dw_kv = _b_bwd(dqz, dkv, h1, dh2, oa, wbin_g, wkv, g_kv, b_pre_norm, g_apost, tm)
    dw_outa = _dw(ya, doa, D, tmw, "dw_a_out").reshape(N_CHIPS, D // 4, D)
    dw_kv = dw_kv.reshape(N_CHIPS, D // 4, 2 * KV_W)
    dw_outb = dw_outb.reshape(N_CHIPS, D // 4, D)
    grads1 = [dw_outa, dw_kv, dw_bin, dw_outb]
    names1 = ["a_w_out", "w_kv", "b_w_in", "b_w_out"]
    from_sibling1, _, _ = _sibling_exchange("to_sibling_1", to_sibling=grads1)
    sums1 = [_add_sibling(where, g, r, "add_sibling_" + nm) for g, r, nm in zip(grads1, from_sibling1, names1)]
    dproj, dconv_w, from_chips1 = _a_bwd(doa, proj, conv_w, wouta, tm_mix, [t for t, _ in sums1])
    shards1 = [_add_chips(where, own, r, "add_chips_" + nm) for (_, own), r, nm in zip(sums1, from_chips1, names1)]
    dw_in = _dw(n1, dproj, D, tmw, "dw_a_in")
    from_sibling2, (g_wouta, g_wkv, g_wbin, g_woutb), _ = _sibling_exchange(
        "to_sibling_2", to_sibling=[dw_in], shards=shards1)
    part2, own2 = _add_sibling(where, dw_in, from_sibling2[0], "add_sibling_a_w_in")
    nt = seq // tm
    dn_first, from_chips2 = _a_in_bwd_matmul(dproj, win_g, tm, max(nt - max(nt // 4, 1), 1), [part2])
    grad_x, dg_apre = _a_in_bwd(dn_first, dproj, xs, dh1, win_g, g_apre, tm)
    shard2 = _add_chips(where, own2, from_chips2[0], "add_chips_a_w_in")
    drel, dsink = _bias_fold(dtab)

    smalls = jnp.concatenate([
        dg_apre[0:1], dg_b[2:3], dg_b[0:1], dg_b[1:2], dg_bpost[0:1], _pad_cols(dsink[0:1], D),
        _pad_cols(loss_acc[0:1], D), jnp.zeros((1, D), F32), dconv_w, _pad_cols(drel, D)], axis=0)
    _, (g_win,), small_all = _sibling_exchange("share_last", shards=[shard2], smalls=smalls)
    tot = _sum_smalls(small_all)

    big = {}
    for nm, g, w, m, v in [("a_w_in", g_win, a_w_in, m_a_w_in, v_a_w_in), ("a_w_out", g_wouta, a_w_out, m_a_w_out, v_a_w_out),
                           ("w_kv", g_wkv, w_kv, m_w_kv, v_w_kv), ("b_w_in", g_wbin, b_w_in, m_b_w_in, v_b_w_in),
                           ("b_w_out", g_woutb, b_w_out, m_b_w_out, v_b_w_out)]:
        shp = w.shape
        two = (shp[-2], shp[-1])
        d, nm_, nv_ = _adamw(g, w.reshape(two), m.reshape(two), v.reshape(two), "adamw_" + nm)
        big[nm] = (g.reshape(shp), d.reshape(shp), nm_.reshape(shp), nv_.reshape(shp))

    given = {"a_pre_norm": (a_pre_norm, m_a_pre_norm, v_a_pre_norm), "a_conv_w": (a_conv_w, m_a_conv_w, v_a_conv_w),
             "a_post_norm": (a_post_norm, m_a_post_norm, v_a_post_norm), "kv_norm": (kv_norm, m_kv_norm, v_kv_norm),
             "rel_bias": (rel_bias, m_rel_bias, v_rel_bias), "b_pre_norm": (b_pre_norm, m_b_pre_norm, v_b_pre_norm),
             "b_sinks": (b_sinks, m_b_sinks, v_b_sinks), "b_post_norm": (b_post_norm, m_b_post_norm, v_b_post_norm)}
    small = _small_update(where[1:2], tot, {nm: tuple(a.reshape(SMALL_PLACES[nm][2]) for a in wmv)
                                            for nm, wmv in given.items()})
    order = ["a_pre_norm", "a_w_in", "a_conv_w", "a_w_out", "a_post_norm", "kv_norm", "w_kv", "rel_bias",
             "b_pre_norm", "b_w_in", "b_sinks", "b_w_out", "b_post_norm"]
    outs = []
    for which in range(4):
        for nm in order:
            outs.append(big[nm][which] if nm in big else small[nm][which].reshape(given[nm][0].shape))
    loss = 0.5 * tot[LOSS_ROW, 0]
    return (loss, grad_x.reshape(x.shape), *outs)
```

```python
import functools
import math

import jax
import jax.numpy as jnp
from jax import lax
from jax.experimental import pallas as pl
from jax.experimental.pallas import tpu as pltpu

F32 = jnp.float32
BF16 = jnp.bfloat16
MESH = pl.DeviceIdType.MESH
SDS = jax.ShapeDtypeStruct

D = 1024
HEAD_DIM = 64
N_HEADS = 16
N_KV = 2
GROUP = 8
KV_W = 128
BLK = 128
N_BUCKETS = 32
MAX_EXACT = 16
MAX_DISTANCE = 128
EPS = 1e-6
NEG_INF = -1e30
Q_SCALE = HEAD_DIM ** -0.5

ADAM_LR = 0.001
ADAM_B1 = 0.9
ADAM_B2 = 0.999
ADAM_EPS = 1e-08
ADAM_WD = 0.01
ADAM_STEP = 10

N_CHIPS = 4
N_DEV = 8
VMEM_LIMIT = 56 * 1024 * 1024
SMALL_ROWS = 48
LOSS_ROW = 6
SMALL_PLACES = {
    "a_pre_norm": (True, 0, (1, D // 4)), "a_conv_w": (True, 8, (3, D // 4)), "a_post_norm": (True, 1, (1, D // 4)),
    "kv_norm": (False, 2, (1, D)), "rel_bias": (False, 16, (N_BUCKETS, N_HEADS)), "b_pre_norm": (False, 3, (1, D)),
    "b_sinks": (False, 5, (1, N_HEADS)), "b_post_norm": (False, 4, (1, D)),
}
HALO = 16


def _bucket_thresholds():
    def bucket(d):
        big = MAX_EXACT + int(math.log(d / MAX_EXACT) / math.log(MAX_DISTANCE / MAX_EXACT)
                              * (N_BUCKETS - MAX_EXACT))
        return d if d < MAX_EXACT else min(big, N_BUCKETS - 1)
    out = []
    for b in range(MAX_EXACT + 1, N_BUCKETS):
        out.append(min(d for d in range(MAX_EXACT, MAX_DISTANCE) if bucket(d) >= b))
    return tuple(out)


BUCKET_THRESHOLDS = _bucket_thresholds()


def _params(semantics=None, vmem=VMEM_LIMIT):
    return pltpu.CompilerParams(dimension_semantics=semantics, vmem_limit_bytes=vmem)


def _tile(n, pref):
    return pref if n >= 2 * pref else max(n // 2, 8)


def _rms_scale(v):
    return lax.rsqrt(jnp.mean(v * v, axis=-1, keepdims=True) + EPS)


def _nt(a, b):
    return lax.dot_general(a, b, (((1,), (1,)), ((), ())), preferred_element_type=F32)


def _tn(a, b):
    return lax.dot_general(a, b, (((0,), (0,)), ((), ())), preferred_element_type=F32)


def _nn(a, b):
    return jnp.dot(a, b, preferred_element_type=F32)


def _silu_parts(z):
    sg = jax.nn.sigmoid(z)
    return sg, z * sg


def _dsilu(z, sg):
    return sg * (1.0 + z * (1.0 - sg))


def _acc_row(ref, row, val):
    ref[row:row + 1, :] += val


def _gather_copies(outs, splits, ici_send, ici_recv, d2d_send, d2d_recv):
    x, y, c = lax.axis_index("x"), lax.axis_index("y"), lax.axis_index("c")
    k = 2 * x + y
    sibling = (x, y, 1 - c)

    def part(o_ref, chip, core, split):
        if not split:
            return o_ref.at[chip]
        h = o_ref.shape[1] // 2
        return o_ref.at[chip, pl.ds(pl.multiple_of(core * h, 16), h)]

    def remote(ref, a, j, sems, to):
        return pltpu.make_async_remote_copy(src_ref=ref, dst_ref=ref, send_sem=sems[0].at[3 * a + j],
                                            recv_sem=sems[1].at[3 * a + j], device_id=to, device_id_type=MESH)

    copies = []
    for a, (o_ref, split) in enumerate(zip(outs, splits)):
        for j, (px, py) in enumerate([(x, 1 - y), (1 - x, y), (1 - x, 1 - y)]):
            kj = 2 * px + py
            ici, d2d = (ici_send, ici_recv), (d2d_send, d2d_recv)
            copies.append((remote(part(o_ref, k, c, split), a, j, ici, (px, py, c)),
                           remote(part(o_ref, kj, c, split), a, j, ici, (px, py, c)),
                           remote(part(o_ref, kj, c, split), a, j, d2d, sibling) if split else None,
                           remote(part(o_ref, kj, 1 - c, split), a, j, d2d, sibling) if split else None))
    return copies


def _gather_sems(n):
    return [pltpu.SemaphoreType.DMA((3 * n,)) for _ in range(4)]


def _gather_weights(shards, small, n_now):
    n = len(shards)

    def body(*refs):
        ins, small_in = refs[:n], refs[n]
        outs, small_out = refs[n + 1:2 * n + 1], refs[2 * n + 1]
        sems = refs[2 * n + 2:]
        k = 2 * lax.axis_index("x") + lax.axis_index("y")
        for i_ref, o_ref in zip(ins, outs):
            o_ref[k] = i_ref[...].astype(BF16)
        small_out[k] = small_in[...]
        copies = _gather_copies(list(outs[:n_now]) + [small_out], [True] * n_now + [False], *sems)
        for send, _, _, _ in copies:
            send.start()
        for _, arrival, forward, _ in copies:
            arrival.wait_recv()
            if forward is not None:
                forward.start()
        for send, _, forward, forwarded in copies:
            if forward is not None:
                forwarded.wait_recv()
                forward.wait_send()
            send.wait_send()

    vm = pl.BlockSpec(memory_space=pltpu.VMEM)
    out_shape = [SDS((N_CHIPS,) + s.shape, BF16) for s in shards] + [SDS((N_CHIPS,) + small.shape, F32)]
    return pl.pallas_call(
        body, name="gather_weights", out_shape=out_shape,
        in_specs=[vm] * (n + 1), out_specs=[vm] * (n + 1),
        scratch_shapes=_gather_sems(n_now + 1),
        compiler_params=pltpu.CompilerParams(vmem_limit_bytes=VMEM_LIMIT),
    )(*shards, small)


def _a_in(chip, x, g_pre, weights, tm):
    s = x.shape[0]
    nt = s // tm
    n = len(weights)

    def body(chip_ref, x_ref, g_ref, *refs):
        proj_ref, n1_ref = refs[n:n + 2]
        gathered = refs[n + 2:2 * n + 2]
        wbuf, fetch_sem = refs[2 * n + 2:2 * n + 4]
        sems = refs[2 * n + 4:]
        jj, i = pl.program_id(0), pl.program_id(1)
        copies = _gather_copies(gathered, [True] * n, *sems)

        def fetch(rel):
            slot = jnp.bitwise_xor(chip_ref[0], rel)
            return pltpu.make_async_copy(gathered[0].at[slot], wbuf.at[rel % 2], fetch_sem.at[rel % 2])

        @pl.when((jj == 0) & (i == 0))
        def _():
            fetch(0).start()
            copies[0][0].start()
            copies[1][0].start()
            fetch(0).wait()

        for rel in (1, 2, 3):
            @pl.when((jj == rel) & (i == 0))
            def _():
                fetch(rel).wait()

        xv = x_ref[...]
        n1 = (xv * _rms_scale(xv) * g_ref[...]).astype(BF16)

        @pl.when(jj == 0)
        def _():
            n1_ref[...] = n1
        proj_ref[...] = _nn(n1, wbuf[jj % 2]).astype(BF16)

        for rel in (1, 2, 3):
            @pl.when((jj == rel - 1) & (i == nt // 2))
            def _():
                _, arrival, forward, forwarded = copies[rel - 1]
                arrival.wait_recv()
                forward.start()
                forwarded.wait_recv()
                fetch(rel).start()
                if rel == 1:
                    copies[2][0].start()
                if rel == 2:
                    for send, _, _, _ in copies[3:]:
                        send.start()

        @pl.when((jj == 3) & (i == max(nt - 2, 0)))
        def _():
            for _, arrival, forward, _ in copies[3:]:
                arrival.wait_recv()
                forward.start()

        @pl.when((jj == 3) & (i == nt - 1))
        def _():
            for _, _, _, forwarded in copies[3:]:
                forwarded.wait_recv()
            for send, _, forward, _ in copies:
                forward.wait_send()
                send.wait_send()

    anyspace = pl.BlockSpec(memory_space=pl.ANY)
    proj, n1, *gathered = pl.pallas_call(
        body, name="a_in",
        grid_spec=pltpu.PrefetchScalarGridSpec(
            num_scalar_prefetch=1, grid=(4, nt),
            in_specs=[pl.BlockSpec((tm, D), lambda jj, i, c: (i, 0)), pl.BlockSpec((1, D), lambda jj, i, c: (0, 0))]
            + [anyspace] * n,
            out_specs=[pl.BlockSpec((tm, D), lambda jj, i, c: (i, jnp.bitwise_xor(c[0], jj))),
                       pl.BlockSpec((tm, D), lambda jj, i, c: (jnp.where(jj == 0, i, nt - 1), 0))] + [anyspace] * n,
            scratch_shapes=[pltpu.VMEM((2, D, D), BF16), pltpu.SemaphoreType.DMA((2,))] + _gather_sems(n)),
        out_shape=[SDS((s, 4 * D), BF16), SDS((s, D), BF16)] + [SDS(w.shape, w.dtype) for w in weights],
        input_output_aliases={3 + a: 2 + a for a in range(n)},
        compiler_params=_params(("arbitrary", "arbitrary")),
    )(chip, x, g_pre, *weights)
    return proj, n1, gathered


def _shift_rows(v, last, second_last, rows):
    v1 = jnp.where(rows >= 1, pltpu.roll(v, 1, 0), last)
    v2 = jnp.where(rows >= 2, pltpu.roll(v, 2, 0), jnp.where(rows == 1, last, second_last))
    return v1, v2


def _a_mix(proj, x, conv_w, w_out, g_post, tm):
    s = x.shape[0]

    def body(proj_ref, x_ref, cw_ref, w_ref, g_ref, ya_ref, oa_ref, h1_ref, carry):
        @pl.when(pl.program_id(0) == 0)
        def _():
            carry[...] = jnp.zeros_like(carry)
        v = proj_ref[:, D:2 * D].astype(F32) * proj_ref[:, 2 * D:3 * D].astype(F32)
        rows = lax.broadcasted_iota(jnp.int32, (tm, D), 0)
        before = carry[...]
        v1, v2 = _shift_rows(v, before[7:8, :], before[6:7, :], rows)
        carry[...] = v[tm - 8:tm, :]
        conv = cw_ref[0:1, :] * v2 + cw_ref[1:2, :] * v1 + cw_ref[2:3, :] * v
        _, sz = _silu_parts(proj_ref[:, 3 * D:4 * D].astype(F32))
        ya = (proj_ref[:, 0:D].astype(F32) * conv * sz).astype(BF16)
        ya_ref[...] = ya
        oa = _nn(ya, w_ref[...])
        oa_ref[...] = oa.astype(BF16)
        h1_ref[...] = x_ref[...] + oa * _rms_scale(oa) * g_ref[...]

    row = lambda i: (i, 0)
    fix = lambda i: (0, 0)
    return pl.pallas_call(
        body, name="a_mix", grid=(s // tm,),
        in_specs=[pl.BlockSpec((tm, 4 * D), row), pl.BlockSpec((tm, D), row), pl.BlockSpec((8, D), fix),
                  pl.BlockSpec((D, D), fix), pl.BlockSpec((1, D), fix)],
        out_specs=[pl.BlockSpec((tm, D), row)] * 3,
        out_shape=[SDS((s, D), BF16), SDS((s, D), BF16), SDS((s, D), F32)],
        scratch_shapes=[pltpu.VMEM((8, D), F32)],
        compiler_params=_params(("arbitrary",)),
    )(proj, x, conv_w, w_out, g_post)


def _b_in(h1, g_kv, g_pre, w_kv, wbin_g, tm):
    s = h1.shape[0]

    def body(h_ref, gk_ref, gb_ref, wkv_ref, wb_ref, kv_ref, q_ref, z_ref):
        h = h_ref[...]
        hh = h * _rms_scale(h)
        nk = (hh * gk_ref[...]).astype(BF16)
        nb = (hh * gb_ref[...]).astype(BF16)
        kv_ref[...] = _nn(nk, wkv_ref[...]).astype(BF16)
        for j in range(2):
            q_ref[:, 512 * j:512 * (j + 1)] = (_nn(nb, wb_ref[j]) * Q_SCALE).astype(BF16)
            z_ref[:, 512 * j:512 * (j + 1)] = _nn(nb, wb_ref[2 + j]).astype(BF16)

    row = lambda i: (i, 0)
    fix = lambda i: (0, 0)
    return pl.pallas_call(
        body, name="b_in", grid=(s // tm,),
        in_specs=[pl.BlockSpec((tm, D), row), pl.BlockSpec((1, D), fix), pl.BlockSpec((1, D), fix),
                  pl.BlockSpec((D, 2 * KV_W), fix), pl.BlockSpec((4, D, 512), lambda i: (0, 0, 0))],
        out_specs=[pl.BlockSpec((tm, 2 * KV_W), row), pl.BlockSpec((tm, D), row), pl.BlockSpec((tm, D), row)],
        out_shape=[SDS((s, 2 * KV_W), BF16), SDS((s, D), BF16), SDS((s, D), BF16)],
        compiler_params=_params(("parallel",)),
    )(h1, g_kv, g_pre, w_kv, wbin_g)


def _band_buckets():
    q = lax.broadcasted_iota(jnp.int32, (BLK, 2 * BLK), 0)
    k = lax.broadcasted_iota(jnp.int32, (BLK, 2 * BLK), 1)
    dist = q + BLK - k
    bucket = jnp.where(dist < MAX_EXACT, dist, MAX_EXACT)
    for t in BUCKET_THRESHOLDS:
        bucket = bucket + jnp.where(dist >= t, 1, 0)
    in_window = (dist >= 0) & (dist < BLK)
    return jnp.where(in_window, bucket, -1)


def _head_place(h):
    kh, j, e = h // GROUP, (h % GROUP) // 2, h % 2
    return kh, slice(BLK * j, BLK * (j + 1)), slice(2 * BLK * e, 2 * BLK * (e + 1))


def _bias_table(rel_bias, sinks):
    def body(rb_ref, sink_ref, tab_ref):
        bucket = _band_buckets()
        col = lax.broadcasted_iota(jnp.int32, (BLK, 2 * BLK), 1)
        for h in range(N_HEADS):
            acc = jnp.where(bucket < 0, NEG_INF, 0.0).astype(F32)
            for b in range(N_BUCKETS):
                acc = jnp.where(bucket == b, rb_ref[b, h], acc)
            acc = jnp.where(col == 0, sink_ref[h], acc)
            kh, rows, cols = _head_place(h)
            tab_ref[1, kh, rows, cols] = acc
            tab_ref[0, kh, rows, cols] = jnp.where((col > 0) & (col < BLK), NEG_INF, acc)

    return pl.pallas_call(
        body, name="bias_table", out_shape=SDS((2, N_KV, 4 * BLK, 4 * BLK), F32),
        in_specs=[pl.BlockSpec(memory_space=pltpu.SMEM), pl.BlockSpec(memory_space=pltpu.SMEM)],
        out_specs=pl.BlockSpec(memory_space=pltpu.VMEM),
    )(rel_bias, sinks)


def _bias_fold(dtab):
    def body(dtab_ref, out_ref, dsink_ref):
        bucket = _band_buckets()
        row = lax.broadcasted_iota(jnp.int32, (N_BUCKETS, 128), 0)
        lane = lax.broadcasted_iota(jnp.int32, (N_BUCKETS, 128), 1)
        row8 = lax.broadcasted_iota(jnp.int32, (8, 128), 0)
        lane8 = lax.broadcasted_iota(jnp.int32, (8, 128), 1)
        acc = jnp.zeros((N_BUCKETS, 128), F32)
        dsink = jnp.zeros((8, 128), F32)
        for h in range(N_HEADS):
            kh, rows, cols = _head_place(h)
            dt = dtab_ref[kh, rows, cols]
            for b in range(N_BUCKETS):
                val = jnp.sum(jnp.where(bucket == b, dt, 0.0))
                acc = acc + jnp.where((row == b) & (lane == h), val, 0.0)
            dsink = dsink + jnp.where((row8 == 0) & (lane8 == h), jnp.sum(dt[:, 0:1]), 0.0)
        out_ref[...] = acc
        dsink_ref[...] = dsink

    vm = pl.BlockSpec(memory_space=pltpu.VMEM)
    return pl.pallas_call(
        body, name="bias_fold", out_shape=[SDS((N_BUCKETS, 128), F32), SDS((8, 128), F32)],
        in_specs=[vm], out_specs=[vm, vm],
    )(dtab)


def _pair_operands(prev, cur):
    t = jnp.concatenate([prev, cur], axis=0).astype(F32)
    t = jnp.where(lax.broadcasted_iota(jnp.int32, t.shape, 0) == 0, 0.0, t)
    tr = pltpu.roll(t, HEAD_DIM, 1)
    lo = lax.broadcasted_iota(jnp.int32, t.shape, 1) < HEAD_DIM
    zero = jnp.zeros_like(t)
    head0 = jnp.concatenate([jnp.where(lo, t, zero), jnp.where(lo, zero, tr)], axis=0).astype(BF16)
    head1 = jnp.concatenate([jnp.where(lo, tr, zero), jnp.where(lo, zero, t)], axis=0).astype(BF16)
    return head0, head1


def _pair_fold(d0, d1):
    lo = lax.broadcasted_iota(jnp.int32, (2 * BLK, KV_W), 1) < HEAD_DIM
    zero = jnp.zeros((2 * BLK, KV_W), F32)
    g0 = jnp.where(lo, d0[0:256], zero) + pltpu.roll(jnp.where(lo, zero, d0[256:512]), HEAD_DIM, 1)
    g1 = pltpu.roll(jnp.where(lo, d1[0:256], zero), HEAD_DIM, 1) + jnp.where(lo, zero, d1[256:512])
    return jnp.where(lax.broadcasted_iota(jnp.int32, (2 * BLK, KV_W), 0) == 0, 0.0, g0 + g1)


def _stack_pairs(ref, kh):
    return jnp.concatenate([ref[:, 128 * (4 * kh + j):128 * (4 * kh + j + 1)] for j in range(4)], axis=0)


def _table_spec():
    return pl.BlockSpec((1, N_KV, 4 * BLK, 4 * BLK), lambda n: (jnp.minimum(n, 1), 0, 0, 0))


def _attn_fwd(q, kv, tab):
    s = q.shape[0]

    def body(q_ref, kp_ref, kc_ref, vp_ref, vc_ref, tab_ref, att_ref, stats_ref):
        k2 = _pair_operands(kp_ref[...], kc_ref[...])
        v2 = _pair_operands(vp_ref[...], vc_ref[...])
        lane = lax.broadcasted_iota(jnp.int32, (BLK, 128), 1)
        stats = jnp.zeros((BLK, 128), F32)
        for kh in range(N_KV):
            sc = _nt(_stack_pairs(q_ref, kh), k2[kh])
            ps = []
            for e in range(2):
                lg = sc[:, 256 * e:256 * (e + 1)] + tab_ref[0, kh, :, 256 * e:256 * (e + 1)]
                m = jnp.max(lg, axis=-1, keepdims=True)
                ex = jnp.exp(lg - m)
                den = jnp.sum(ex, axis=-1, keepdims=True)
                ps.append(ex * (1.0 / den))
                lse = m + jnp.log(den)
                for j in range(4):
                    stats = jnp.where(lane == GROUP * kh + 2 * j + e, lse[BLK * j:BLK * (j + 1)], stats)
            out = _nn(jnp.concatenate(ps, axis=1).astype(BF16), v2[kh])
            for j in range(4):
                att_ref[:, 128 * (4 * kh + j):128 * (4 * kh + j + 1)] = out[BLK * j:BLK * (j + 1)].astype(BF16)
        stats_ref[...] = stats

    cur = lambda n: (n, 0)
    prev = lambda n: (jnp.maximum(n - 1, 0), 0)
    return pl.pallas_call(
        body, name="attn_fwd", grid=(s // BLK,),
        in_specs=[pl.BlockSpec((BLK, D), cur),
                  pl.BlockSpec((BLK, KV_W), prev), pl.BlockSpec((BLK, KV_W), cur),
                  pl.BlockSpec((BLK, KV_W), lambda n: (jnp.maximum(n - 1, 0), 1)),
                  pl.BlockSpec((BLK, KV_W), lambda n: (n, 1)), _table_spec()],
        out_specs=[pl.BlockSpec((BLK, D), cur), pl.BlockSpec((BLK, 128), cur)],
        out_shape=[SDS((s, D), BF16), SDS((s, 128), F32)],
        compiler_params=_params(("parallel",)),
    )(q, kv, kv, kv, kv, tab)


def _mid(att, zb, h1, tgt, w_out, g_post, tm):
    s = att.shape[0]
    nt = s // tm

    def body(att_ref, z_ref, h1_ref, t_ref, w_ref, g_ref,
             dh_ref, dqz_ref, datt_ref, loss_ref, dg_ref, dw_ref, dw_acc):
        @pl.when(pl.program_id(0) == 0)
        def _():
            loss_ref[...] = jnp.zeros_like(loss_ref)
            dg_ref[...] = jnp.zeros_like(dg_ref)
            dw_acc[...] = jnp.zeros_like(dw_acc)
        att = att_ref[...].astype(F32)
        z = z_ref[...].astype(F32)
        sg, sz = _silu_parts(z)
        ob = (att * sz).astype(BF16)
        y2 = _nn(ob, w_ref[...])
        r2 = _rms_scale(y2)
        yh = y2 * r2
        g = g_ref[...]
        err = (h1_ref[...] + yh * g) - t_ref[...]
        loss_ref[...] += jnp.sum(jnp.sum(err * err, axis=-1, keepdims=True) / D)
        dh = err / D
        dh_ref[...] = dh
        _acc_row(dg_ref, 0, jnp.sum(dh * yh, axis=0, keepdims=True))
        dyh = dh * g
        dy = (r2 * (dyh - yh * jnp.mean(dyh * yh, axis=-1, keepdims=True))).astype(BF16)
        dw_acc[...] += _tn(ob, dy)
        dob = _nt(dy, w_ref[...])
        datt_ref[...] = (dob * sz).astype(BF16)
        dqz_ref[...] = (dob * att * _dsilu(z, sg)).astype(BF16)

        @pl.when(pl.program_id(0) == nt - 1)
        def _():
            pltpu.sync_copy(dw_acc, dw_ref)

    row = lambda i: (i, 0)
    fix = lambda i: (0, 0)
    return pl.pallas_call(
        body, name="mid", grid=(nt,),
        in_specs=[pl.BlockSpec((tm, D), row)] * 4 + [pl.BlockSpec((D, D), fix), pl.BlockSpec((1, D), fix)],
        out_specs=[pl.BlockSpec((tm, D), row), pl.BlockSpec((tm, D), lambda i: (i, 1)), pl.BlockSpec((tm, D), row),
                   pl.BlockSpec((8, 128), fix), pl.BlockSpec((8, D), fix), pl.BlockSpec(memory_space=pl.ANY)],
        out_shape=[SDS((s, D), F32), SDS((s, 2 * D), BF16), SDS((s, D), BF16), SDS((8, 128), F32),
                   SDS((8, D), F32), SDS((D, D), F32)],
        scratch_shapes=[pltpu.VMEM((D, D), F32)],
        compiler_params=_params(("arbitrary",)),
    )(att, zb, h1, tgt, w_out, g_post)


def _attn_bwd(q, kv, datt, stats, tab, dqz):
    s = q.shape[0]
    nb = s // BLK

    def body(q_ref, kp_ref, kc_ref, vp_ref, vc_ref, da_ref, st_ref, tab_ref, dqz_in,
             dq_ref, dkv_ref, dtab_ref, dk_carry, dv_carry):
        del dqz_in
        n = pl.program_id(0)

        @pl.when(n == 0)
        def _():
            dtab_ref[...] = jnp.zeros_like(dtab_ref)
            dk_carry[...] = jnp.zeros_like(dk_carry)
            dv_carry[...] = jnp.zeros_like(dv_carry)

        @pl.when(n < nb)
        def _():
            k2 = _pair_operands(kp_ref[...], kc_ref[...])
            v2 = _pair_operands(vp_ref[...], vc_ref[...])
            lane = lax.broadcasted_iota(jnp.int32, (BLK, 128), 1)
            stats = st_ref[...]
            dk2, dv2 = [], []
            for kh in range(N_KV):
                qs = _stack_pairs(q_ref, kh)
                das = _stack_pairs(da_ref, kh)
                sc = _nt(qs, k2[kh])
                dp = _nt(das, v2[kh])
                ps, dss = [], []
                for e in range(2):
                    heads = [GROUP * kh + 2 * j + e for j in range(4)]
                    lse = jnp.concatenate([jnp.sum(jnp.where(lane == h, stats, 0.0), axis=-1, keepdims=True)
                                           for h in heads], axis=0)
                    cols = slice(256 * e, 256 * (e + 1))
                    p = jnp.exp(sc[:, cols] + tab_ref[0, kh, :, cols] - lse)
                    delta = jnp.sum(p * dp[:, cols], axis=-1, keepdims=True)
                    ds = p * (dp[:, cols] - delta)
                    dtab_ref[kh, :, cols] += ds
                    ps.append(p)
                    dss.append(ds)
                p2 = jnp.concatenate(ps, axis=1).astype(BF16)
                ds2 = jnp.concatenate(dss, axis=1).astype(BF16)
                dq = _nn(ds2, k2[kh]) * Q_SCALE
                for j in range(4):
                    dq_ref[:, 128 * (4 * kh + j):128 * (4 * kh + j + 1)] = dq[BLK * j:BLK * (j + 1)].astype(BF16)
                dk2.append(_tn(ds2, qs))
                dv2.append(_tn(p2, das))
            dkk = _pair_fold(dk2[0], dk2[1])
            dvv = _pair_fold(dv2[0], dv2[1])
            dkv_ref[:, 0:KV_W] = (dk_carry[...] + dkk[0:BLK]).astype(BF16)
            dkv_ref[:, KV_W:2 * KV_W] = (dv_carry[...] + dvv[0:BLK]).astype(BF16)
            dk_carry[...] = dkk[BLK:2 * BLK]
            dv_carry[...] = dvv[BLK:2 * BLK]

        @pl.when(n == nb)
        def _():
            dkv_ref[:, 0:KV_W] = dk_carry[...].astype(BF16)
            dkv_ref[:, KV_W:2 * KV_W] = dv_carry[...].astype(BF16)

    cur = lambda n: (jnp.minimum(n, nb - 1), 0)
    prev = lambda n: (jnp.clip(n - 1, 0, nb - 1), 0)
    return pl.pallas_call(
        body, name="attn_bwd", grid=(nb + 1,),
        in_specs=[pl.BlockSpec((BLK, D), cur),
                  pl.BlockSpec((BLK, KV_W), prev), pl.BlockSpec((BLK, KV_W), cur),
                  pl.BlockSpec((BLK, KV_W), lambda n: (jnp.clip(n - 1, 0, nb - 1), 1)),
                  pl.BlockSpec((BLK, KV_W), lambda n: (jnp.minimum(n, nb - 1), 1)),
                  pl.BlockSpec((BLK, D), cur), pl.BlockSpec((BLK, 128), cur), _table_spec(),
                  pl.BlockSpec(memory_space=pl.ANY)],
        out_specs=[pl.BlockSpec((BLK, D), cur), pl.BlockSpec((BLK, 2 * KV_W), prev),
                   pl.BlockSpec((N_KV, 4 * BLK, 4 * BLK), lambda n: (0, 0, 0))],
        out_shape=[SDS((s, 2 * D), BF16), SDS((s, 2 * KV_W), BF16), SDS((N_KV, 4 * BLK, 4 * BLK), F32)],
        scratch_shapes=[pltpu.VMEM((BLK, KV_W), F32), pltpu.VMEM((BLK, KV_W), F32)],
        input_output_aliases={8: 0},
        compiler_params=_params(("arbitrary",)),
    )(q, kv, kv, kv, kv, datt, stats, tab, dqz)


def _b_bwd(dqz, dkv, h1, dh2, oa, wbin_g, w_kv, g_kv, g_pre, g_apost, tm):
    s = h1.shape[0]
    nt = s // tm

    def body(dqz_ref, dkv_ref, h_ref, dh2_ref, oa_ref, wb_ref, wkv_ref, gk_ref, gb_ref, ga_ref,
             dh1_ref, doa_ref, dg_ref, dwb_ref, dwkv_ref, dwb_acc, dwkv_acc):
        @pl.when(pl.program_id(0) == 0)
        def _():
            dg_ref[...] = jnp.zeros_like(dg_ref)
            dwb_acc[...] = jnp.zeros_like(dwb_acc)
            dwkv_acc[...] = jnp.zeros_like(dwkv_acc)
        dnb = _nt(dqz_ref[:, 0:512], wb_ref[0])
        for j in range(1, 4):
            dnb = dnb + _nt(dqz_ref[:, 512 * j:512 * (j + 1)], wb_ref[j])
        dnk = _nt(dkv_ref[...], wkv_ref[...])
        h = h_ref[...]
        r = _rms_scale(h)
        hh = h * r
        nb = (hh * gb_ref[...]).astype(BF16)
        for j in range(4):
            dwb_acc[j] += _tn(nb, dqz_ref[:, 512 * j:512 * (j + 1)])
        dwkv_acc[...] += _tn((hh * gk_ref[...]).astype(BF16), dkv_ref[...])
        _acc_row(dg_ref, 0, jnp.sum(dnk * hh, axis=0, keepdims=True))
        _acc_row(dg_ref, 1, jnp.sum(dnb * hh, axis=0, keepdims=True))
        dhh = dnb * gb_ref[...] + dnk * gk_ref[...]
        dh1 = dh2_ref[...] + r * (dhh - hh * jnp.mean(dhh * hh, axis=-1, keepdims=True))
        dh1_ref[...] = dh1
        oa = oa_ref[...].astype(F32)
        ra = _rms_scale(oa)
        oh = oa * ra
        _acc_row(dg_ref, 2, jnp.sum(dh1 * oh, axis=0, keepdims=True))
        doh = dh1 * ga_ref[...]
        doa_ref[...] = (ra * (doh - oh * jnp.mean(doh * oh, axis=-1, keepdims=True))).astype(BF16)

        @pl.when(pl.program_id(0) == nt - 1)
        def _():
            pltpu.sync_copy(dwb_acc, dwb_ref)
            pltpu.sync_copy(dwkv_acc, dwkv_ref)

    row = lambda i: (i, 0)
    fix = lambda i: (0, 0)
    anyspace = pl.BlockSpec(memory_space=pl.ANY)
    return pl.pallas_call(
        body, name="b_bwd", grid=(nt,),
        in_specs=[pl.BlockSpec((tm, 2 * D), row), pl.BlockSpec((tm, 2 * KV_W), row), pl.BlockSpec((tm, D), row),
                  pl.BlockSpec((tm, D), row), pl.BlockSpec((tm, D), row),
                  pl.BlockSpec((4, D, 512), lambda i: (0, 0, 0)), pl.BlockSpec((D, 2 * KV_W), fix),
                  pl.BlockSpec((1, D), fix), pl.BlockSpec((1, D), fix), pl.BlockSpec((1, D), fix)],
        out_specs=[pl.BlockSpec((tm, D), row), pl.BlockSpec((tm, D), row), pl.BlockSpec((8, D), fix), anyspace, anyspace],
        out_shape=[SDS((s, D), F32), SDS((s, D), BF16), SDS((8, D), F32), SDS((4, D, 512), F32),
                   SDS((D, 2 * KV_W), F32)],
        scratch_shapes=[pltpu.VMEM((4, D, 512), F32), pltpu.VMEM((D, 2 * KV_W), F32)],
        compiler_params=_params(("arbitrary",)),
    )(dqz, dkv, h1, dh2, oa, wbin_g, w_kv, g_kv, g_pre, g_apost)


def _chip_exchange(parts, recvs, send, recv):
    x, y, c = lax.axis_index("x"), lax.axis_index("y"), lax.axis_index("c")
    chips = [(x, 1 - y), (1 - x, y), (1 - x, 1 - y)]
    copies = []
    for a, (t, r) in enumerate(zip(parts, recvs)):
        for j, (px, py) in enumerate(chips):
            copies.append(pltpu.make_async_remote_copy(
                src_ref=t.at[2 * px + py], dst_ref=r.at[j], send_sem=send.at[3 * a + j],
                recv_sem=recv.at[3 * a + j], device_id=(px, py, c), device_id_type=MESH))
    return copies


def _exchange_specs(parts):
    anyspace = pl.BlockSpec(memory_space=pl.ANY)
    n = len(parts)
    return ([anyspace] * n, [anyspace] * n, [SDS((3,) + t.shape[1:], t.dtype) for t in parts],
            [pltpu.SemaphoreType.DMA((3 * n,)), pltpu.SemaphoreType.DMA((3 * n,))])


def _a_bwd(doa, proj, conv_w, w_out, tm, parts):
    s = doa.shape[0]
    nt = s // tm
    n = len(parts)
    ex_in, ex_out, ex_shape, ex_sems = _exchange_specs(parts)

    def body(*refs):
        doa_ref, proj_ref, halo_ref, cw_ref, w_ref = refs[:5]
        part_refs = refs[5:5 + n]
        dproj_ref, dcw_ref = refs[5 + n:7 + n]
        recv_refs = refs[7 + n:7 + 2 * n]
        carry, send, recv = refs[7 + 2 * n:]
        i = pl.program_id(0)
        r = nt - 1 - i

        @pl.when(i == 0)
        def _():
            dcw_ref[...] = jnp.zeros_like(dcw_ref)
            carry[...] = jnp.zeros_like(carry)
            for cp in _chip_exchange(part_refs, recv_refs, send, recv):
                cp.start()
        dya = _nt(doa_ref[...], w_ref[...])
        bg = proj_ref[:, 0:D].astype(F32)
        cg = proj_ref[:, D:2 * D].astype(F32)
        u = proj_ref[:, 2 * D:3 * D].astype(F32)
        z = proj_ref[:, 3 * D:4 * D].astype(F32)
        v = cg * u
        before = jnp.where(r > 0, halo_ref[:, D:2 * D].astype(F32) * halo_ref[:, 2 * D:3 * D].astype(F32), 0.0)
        rows = lax.broadcasted_iota(jnp.int32, (tm, D), 0)
        v1, v2 = _shift_rows(v, before[HALO - 1:HALO, :], before[HALO - 2:HALO - 1, :], rows)
        conv = cw_ref[0:1, :] * v2 + cw_ref[1:2, :] * v1 + cw_ref[2:3, :] * v
        sg, sz = _silu_parts(z)
        dproj_ref[:, 0:D] = (dya * conv * sz).astype(BF16)
        dproj_ref[:, 3 * D:4 * D] = (dya * bg * conv * _dsilu(z, sg)).astype(BF16)
        dconv = dya * bg * sz
        _acc_row(dcw_ref, 0, jnp.sum(dconv * v2, axis=0, keepdims=True))
        _acc_row(dcw_ref, 1, jnp.sum(dconv * v1, axis=0, keepdims=True))
        _acc_row(dcw_ref, 2, jnp.sum(dconv * v, axis=0, keepdims=True))
        after = carry[...]
        up1 = jnp.where(rows < tm - 1, pltpu.roll(dconv, tm - 1, 0), after[0:1, :])
        up2 = jnp.where(rows < tm - 2, pltpu.roll(dconv, tm - 2, 0),
                        jnp.where(rows == tm - 2, after[0:1, :], after[1:2, :]))
        carry[...] = dconv[0:8, :]
        dv = cw_ref[2:3, :] * dconv + cw_ref[1:2, :] * up1 + cw_ref[0:1, :] * up2
        dproj_ref[:, D:2 * D] = (dv * u).astype(BF16)
        dproj_ref[:, 2 * D:3 * D] = (dv * cg).astype(BF16)

        @pl.when(i == nt - 1)
        def _():
            for cp in _chip_exchange(part_refs, recv_refs, send, recv):
                cp.wait()

    rev = lambda i: (nt - 1 - i, 0)
    fix = lambda i: (0, 0)
    halo = lambda i: (jnp.maximum((nt - 1 - i) * (tm // HALO) - 1, 0), 0)
    dproj, dcw, *got = pl.pallas_call(
        body, name="a_bwd", grid=(nt,),
        in_specs=[pl.BlockSpec((tm, D), rev), pl.BlockSpec((tm, 4 * D), rev), pl.BlockSpec((HALO, 4 * D), halo),
                  pl.BlockSpec((8, D), fix), pl.BlockSpec((D, D), fix)] + ex_in,
        out_specs=[pl.BlockSpec((tm, 4 * D), rev), pl.BlockSpec((8, D), fix)] + ex_out,
        out_shape=[SDS((s, 4 * D), BF16), SDS((8, D), F32)] + ex_shape,
        scratch_shapes=[pltpu.VMEM((8, D), F32)] + ex_sems,
        compiler_params=_params(("arbitrary",)),
    )(doa, proj, proj, conv_w, w_out, *parts)
    return dproj, dcw, got


def _dn1(dp_ref, w_ref):
    dn = _nt(dp_ref[:, 0:D], w_ref[0])
    for j in range(1, 4):
        dn = dn + _nt(dp_ref[:, D * j:D * (j + 1)], w_ref[j])
    return dn


def _a_in_bwd_matmul(dproj, win_g, tm, count, parts):
    n = len(parts)
    ex_in, ex_out, ex_shape, ex_sems = _exchange_specs(parts)

    def body(*refs):
        dp_ref, w_ref = refs[:2]
        part_refs = refs[2:2 + n]
        dn_ref = refs[2 + n]
        recv_refs = refs[3 + n:3 + 2 * n]
        sems = refs[3 + 2 * n:]

        @pl.when(pl.program_id(0) == 0)
        def _():
            for cp in _chip_exchange(part_refs, recv_refs, *sems):
                cp.start()
        dn_ref[...] = _dn1(dp_ref, w_ref)

        @pl.when(pl.program_id(0) == count - 1)
        def _():
            for cp in _chip_exchange(part_refs, recv_refs, *sems):
                cp.wait()

    row = lambda i: (i, 0)
    dn, *got = pl.pallas_call(
        body, name="a_in_bwd_matmul", grid=(count,),
        in_specs=[pl.BlockSpec((tm, 4 * D), row), pl.BlockSpec((4, D, D), lambda i: (0, 0, 0))] + ex_in,
        out_specs=[pl.BlockSpec((tm, D), row)] + ex_out,
        out_shape=[SDS((count * tm, D), F32)] + ex_shape,
        scratch_shapes=ex_sems,
        compiler_params=_params(("arbitrary",)),
    )(dproj, win_g, *parts)
    return dn, got


def _a_in_bwd(dn_first, dproj, x, dh1, win_g, g_pre, tm):
    s = x.shape[0]
    nt = s // tm
    count = dn_first.shape[0] // tm

    def body(dn_ref, dp_ref, x_ref, dh_ref, w_ref, g_ref, gx_ref, dg_ref, dn_s):
        i = pl.program_id(0)

        @pl.when(i == 0)
        def _():
            dg_ref[...] = jnp.zeros_like(dg_ref)

        @pl.when(i < count)
        def _():
            dn_s[...] = dn_ref[...]

        @pl.when(i >= count)
        def _():
            dn_s[...] = _dn1(dp_ref, w_ref)
        dn = dn_s[...]
        xv = x_ref[...]
        r = _rms_scale(xv)
        xh = xv * r
        _acc_row(dg_ref, 0, jnp.sum(dn * xh, axis=0, keepdims=True))
        dxh = dn * g_ref[...]
        gx_ref[...] = dh_ref[...] + r * (dxh - xh * jnp.mean(dxh * xh, axis=-1, keepdims=True))

    row = lambda i: (i, 0)
    fix = lambda i: (0, 0)
    return pl.pallas_call(
        body, name="a_in_bwd", grid=(nt,),
        in_specs=[pl.BlockSpec((tm, D), lambda i: (jnp.minimum(i, count - 1), 0)),
                  pl.BlockSpec((tm, 4 * D), lambda i: (jnp.maximum(i, count), 0)),
                  pl.BlockSpec((tm, D), row), pl.BlockSpec((tm, D), row),
                  pl.BlockSpec((4, D, D), lambda i: (0, 0, 0)), pl.BlockSpec((1, D), fix)],
        out_specs=[pl.BlockSpec((tm, D), row), pl.BlockSpec((8, D), fix)],
        out_shape=[SDS((s, D), F32), SDS((8, D), F32)],
        scratch_shapes=[pltpu.VMEM((tm, D), F32)],
        compiler_params=_params(("arbitrary",)),
    )(dn_first, dproj, x, dh1, win_g, g_pre)


def _dw(a, b, tn, tmw, name):
    s, k = a.shape
    n = b.shape[1]

    def body(a_ref, b_ref, o_ref):
        @pl.when(pl.program_id(1) == 0)
        def _():
            o_ref[...] = jnp.zeros_like(o_ref)
        o_ref[0] += _tn(a_ref[...], b_ref[...])

    return pl.pallas_call(
        body, name=name, grid=(n // tn, s // tmw),
        in_specs=[pl.BlockSpec((tmw, k), lambda j, t: (t, 0)), pl.BlockSpec((tmw, tn), lambda j, t: (t, j))],
        out_specs=pl.BlockSpec((1, k, tn), lambda j, t: (j, 0, 0)),
        out_shape=SDS((n // tn, k, tn), F32),
        compiler_params=_params(("parallel", "arbitrary")),
    )(a, b)


def _sibling_exchange(name, to_sibling=(), shards=(), smalls=None):
    n_g, n_h = len(to_sibling), len(shards)
    has_small = smalls is not None

    def body(*refs):
        gs = refs[:n_g]
        pos = n_g + n_h
        small_in = refs[pos] if has_small else None
        pos += has_small
        rs, fs = refs[pos:pos + n_g], refs[pos + n_g:pos + n_g + n_h]
        pos += n_g + n_h
        small_all = refs[pos] if has_small else None
        pos += has_small
        dsend, drecv, ssend, srecv = refs[pos:]
        x, y, c = lax.axis_index("x"), lax.axis_index("y"), lax.axis_index("c")
        sibling = (x, y, 1 - c)
        sends, arrivals = [], []
        for a, (g, r) in enumerate(zip(gs, rs)):
            h = g.shape[1] // 2
            src = g.at[:, pl.ds(pl.multiple_of((1 - c) * h, 8), h), :]
            sends.append(pltpu.make_async_remote_copy(src_ref=src, dst_ref=r, send_sem=dsend.at[a], recv_sem=drecv.at[a],
                                                      device_id=sibling, device_id_type=MESH))
            arrivals.append(pltpu.make_async_remote_copy(src_ref=r, dst_ref=r, send_sem=dsend.at[a], recv_sem=drecv.at[a],
                                                         device_id=sibling, device_id_type=MESH))
        for b, full in enumerate(fs):
            h = full.shape[0] // 2
            mine = full.at[pl.ds(pl.multiple_of(c * h, 8), h)]
            theirs = full.at[pl.ds(pl.multiple_of((1 - c) * h, 8), h)]
            sends.append(pltpu.make_async_remote_copy(src_ref=mine, dst_ref=mine, send_sem=dsend.at[n_g + b],
                                                      recv_sem=drecv.at[n_g + b], device_id=sibling, device_id_type=MESH))
            arrivals.append(pltpu.make_async_remote_copy(src_ref=mine, dst_ref=theirs, send_sem=dsend.at[n_g + b],
                                                         recv_sem=drecv.at[n_g + b], device_id=sibling, device_id_type=MESH))
        if has_small:
            me = 4 * x + 2 * y + c
            small_all[me] = small_in[...]
            for rel in range(1, N_DEV):
                fx, fy, fc = rel >> 2, (rel >> 1) & 1, rel & 1
                peer = (x + fx - 2 * x * fx, y + fy - 2 * y * fy, c + fc - 2 * c * fc)
                sender = 4 * peer[0] + 2 * peer[1] + peer[2]
                sends.append(pltpu.make_async_remote_copy(
                    src_ref=small_in, dst_ref=small_all.at[me], send_sem=ssend.at[rel - 1], recv_sem=srecv.at[rel - 1],
                    device_id=peer, device_id_type=MESH))
                arrivals.append(pltpu.make_async_remote_copy(
                    src_ref=small_in, dst_ref=small_all.at[sender], send_sem=ssend.at[rel - 1], recv_sem=srecv.at[rel - 1],
                    device_id=peer, device_id_type=MESH))
        for cp in sends:
            cp.start()
        for cp in arrivals:
            cp.wait_recv()
        for cp in sends:
            cp.wait_send()

    anyspace = pl.BlockSpec(memory_space=pl.ANY)
    vm = pl.BlockSpec(memory_space=pltpu.VMEM)
    out_shape = [SDS((N_CHIPS, g.shape[1] // 2, g.shape[2]), F32) for g in to_sibling]
    out_shape += [SDS(full.shape, F32) for full in shards]
    if has_small:
        out_shape.append(SDS((N_DEV,) + smalls.shape, F32))
    n_d2d = max(n_g + n_h, 1)
    outs = pl.pallas_call(
        body, name=name, out_shape=out_shape,
        in_specs=[anyspace] * (n_g + n_h) + [vm] * has_small, out_specs=[anyspace] * (n_g + n_h) + [vm] * has_small,
        scratch_shapes=[pltpu.SemaphoreType.DMA((n_d2d,)), pltpu.SemaphoreType.DMA((n_d2d,)),
                        pltpu.SemaphoreType.DMA((N_DEV - 1,)), pltpu.SemaphoreType.DMA((N_DEV - 1,))],
        input_output_aliases={n_g + b: n_g + b for b in range(n_h)},
    )(*to_sibling, *shards, *([smalls] if has_small else []))
    return outs[:n_g], outs[n_g:n_g + n_h], (outs[n_g + n_h] if has_small else None)


def _add_sibling(where, g, r, name):
    _, rows, cols = g.shape
    h = rows // 2
    tr = min(h, 256)
    nh = h // tr

    def body(where_ref, g_ref, r_ref, t_ref, own_ref):
        t = g_ref[0] + r_ref[0]
        t_ref[0] = t.astype(BF16)

        @pl.when(pl.program_id(1) == where_ref[1])
        def _():
            own_ref[...] = t

    return pl.pallas_call(
        body, name=name,
        grid_spec=pltpu.PrefetchScalarGridSpec(
            num_scalar_prefetch=1, grid=(nh, N_CHIPS),
            in_specs=[pl.BlockSpec((1, tr, cols), lambda i, k, w: (k, w[0] * nh + i, 0)),
                      pl.BlockSpec((1, tr, cols), lambda i, k, w: (k, i, 0))],
            out_specs=[pl.BlockSpec((1, tr, cols), lambda i, k, w: (k, i, 0)),
                       pl.BlockSpec((tr, cols), lambda i, k, w: (i, 0))]),
        out_shape=[SDS((N_CHIPS, h, cols), BF16), SDS((h, cols), F32)],
        compiler_params=_params(("parallel", "arbitrary")),
    )(where, g, r)


def _add_chips(where, own, r, name):
    h, cols = own.shape
    tr = min(h, 256)
    nh = h // tr

    def body(where_ref, t_ref, r_ref, o_ref):
        del where_ref
        o_ref[...] = ((t_ref[...] + r_ref[0].astype(F32)) + r_ref[1].astype(F32)) + r_ref[2].astype(F32)

    return pl.pallas_call(
        body, name=name,
        grid_spec=pltpu.PrefetchScalarGridSpec(
            num_scalar_prefetch=1, grid=(nh,),
            in_specs=[pl.BlockSpec((tr, cols), lambda i, w: (i, 0)), pl.BlockSpec((3, tr, cols), lambda i, w: (0, i, 0))],
            out_specs=pl.BlockSpec((tr, cols), lambda i, w: (w[0] * nh + i, 0))),
        out_shape=SDS((2 * h, cols), F32),
        compiler_params=_params(("parallel",)),
    )(where, own, r)


def _sum_smalls(small_all):
    def body(all_ref, o_ref):
        acc = all_ref[0]
        for dev in range(1, N_DEV):
            acc = acc + all_ref[dev]
        o_ref[...] = acc

    return pl.pallas_call(
        body, name="sum_smalls", out_shape=SDS(small_all.shape[1:], F32),
        in_specs=[pl.BlockSpec(memory_space=pltpu.VMEM)], out_specs=pl.BlockSpec(memory_space=pltpu.VMEM),
    )(small_all)


def _adam_step(g, w, m, v):
    nm = ADAM_B1 * m + (1.0 - ADAM_B1) * g
    nv = ADAM_B2 * v + (1.0 - ADAM_B2) * (g * g)
    m_hat = nm / (1.0 - ADAM_B1 ** ADAM_STEP)
    v_hat = nv / (1.0 - ADAM_B2 ** ADAM_STEP)
    return -ADAM_LR * (m_hat / (jnp.sqrt(v_hat) + ADAM_EPS) + ADAM_WD * w), nm, nv


def _adamw(g, w, m, v, name):
    rows, cols = g.shape
    tr = min(rows, 256)

    def body(g_ref, w_ref, m_ref, v_ref, d_ref, nm_ref, nv_ref):
        d_ref[...], nm_ref[...], nv_ref[...] = _adam_step(g_ref[...], w_ref[...], m_ref[...], v_ref[...])

    spec = pl.BlockSpec((tr, cols), lambda i: (i, 0))
    return pl.pallas_call(
        body, name=name, grid=(rows // tr,), in_specs=[spec] * 4, out_specs=[spec] * 3,
        out_shape=[SDS(g.shape, F32)] * 3, compiler_params=_params(("parallel",)),
    )(g, w, m, v)


def _small_update(chip, tot, wmv):
    names = list(SMALL_PLACES)
    n = len(names)

    def body(chip_ref, tot_ref, quarter_ref, *refs):
        del chip_ref
        ins, outs = refs[:3 * n], refs[3 * n:]
        for i, nm in enumerate(names):
            sharded, row, (rows, cols) = SMALL_PLACES[nm]
            g = (quarter_ref if sharded else tot_ref)[row:row + rows, 0:cols]
            outs[4 * i][...] = g
            outs[4 * i + 1][...], outs[4 * i + 2][...], outs[4 * i + 3][...] = _adam_step(
                g, ins[3 * i][...], ins[3 * i + 1][...], ins[3 * i + 2][...])

    whole = lambda shape: pl.BlockSpec(shape, lambda i, c: (0,) * len(shape))
    shapes = [SMALL_PLACES[nm][2] for nm in names]
    outs = pl.pallas_call(
        body, name="small_update",
        grid_spec=pltpu.PrefetchScalarGridSpec(
            num_scalar_prefetch=1, grid=(1,),
            in_specs=[whole(tot.shape), pl.BlockSpec((tot.shape[0], D // 4), lambda i, c: (0, c[0]))]
            + [whole(shp) for shp in shapes for _ in range(3)],
            out_specs=[whole(shp) for shp in shapes for _ in range(4)]),
        out_shape=[SDS(shp, F32) for shp in shapes for _ in range(4)],
    )(chip, tot, tot, *[a for nm in names for a in wmv[nm]])
    return {nm: tuple(outs[4 * i:4 * i + 4]) for i, nm in enumerate(names)}


def _pad_rows(a, rows):
    return jnp.concatenate([a, jnp.zeros((rows - a.shape[0], a.shape[1]), a.dtype)], axis=0)


def _pad_cols(a, cols):
    return jnp.concatenate([a, jnp.zeros((a.shape[0], cols - a.shape[1]), a.dtype)], axis=1)


def kernel(x, a_pre_norm, a_w_in, a_conv_w, a_w_out, a_post_norm, kv_norm, w_kv, rel_bias, b_pre_norm, b_w_in, b_sinks, b_w_out, b_post_norm, loss_target, m_a_pre_norm, m_a_w_in, m_a_conv_w, m_a_w_out, m_a_post_norm, m_kv_norm, m_w_kv, m_rel_bias, m_b_pre_norm, m_b_w_in, m_b_sinks, m_b_w_out, m_b_post_norm, v_a_pre_norm, v_a_w_in, v_a_conv_w, v_a_w_out, v_a_post_norm, v_kv_norm, v_w_kv, v_rel_bias, v_b_pre_norm, v_b_w_in, v_b_sinks, v_b_w_out, v_b_post_norm):
    seq = x.shape[1]
    xs = x.reshape(seq, D)
    tgt = loss_target.reshape(seq, D)
    chip = 2 * lax.axis_index("x") + lax.axis_index("y")
    core = lax.axis_index("c")
    tm = _tile(seq, 512)
    tm_mix = _tile(seq, 256)
    tmw = _tile(seq, 1024)

    shards = [a_w_in[0], a_w_out[0], w_kv, b_w_in[0], b_w_out[0]]
    small_w = _pad_rows(jnp.concatenate([a_pre_norm, a_conv_w[0], a_post_norm], axis=0), 8)
    *own_only, small_g = _gather_weights(shards, small_w, 0)
    where = jnp.stack([core, chip]).astype(jnp.int32)
    small_full = small_g.transpose(1, 0, 2).reshape(8, D)
    g_apre, conv_w, g_apost = small_full[0:1], _pad_rows(small_full[1:4], 8), small_full[4:5]
    g_kv = kv_norm.reshape(1, D)

    proj, n1, (win_g, wouta_g, wkv_g, wbin_g, woutb_g) = _a_in(where[1:2], xs, g_apre, own_only, tm)
    wouta = wouta_g.reshape(D, D)
    wkv = wkv_g.reshape(D, 2 * KV_W)
    woutb = woutb_g.reshape(D, D)
    ya, oa, h1 = _a_mix(proj, xs, conv_w, wouta, g_apost, tm_mix)
    kv, q, zb = _b_in(h1, g_kv, b_pre_norm, wkv, wbin_g, tm)
    tab = _bias_table(rel_bias, b_sinks.reshape(N_HEADS))
    att, stats = _attn_fwd(q, kv, tab)
    dh2, dqz, datt, loss_acc, dg_bpost, dw_outb = _mid(att, zb, h1, tgt, woutb, b_post_norm, tm)

    dqz, dkv, dtab = _attn_bwd(q, kv, datt, stats, tab, dqz)
    dh1, doa, dg_b, dw_bin, dw_kv = _b_bwd(dqz, dkv, h1, dh2, oa, wbin_g, wkv, g_kv, b_pre_norm, g_apost, tm)
    dw_outa = _dw(ya, doa, D, tmw, "dw_a_out").reshape(N_CHIPS, D // 4, D)
    dw_kv = dw_kv.reshape(N_CHIPS, D // 4, 2 * KV_W)
    dw_outb = dw_outb.reshape(N_CHIPS, D // 4, D)
    grads1 = [dw_outa, dw_kv, dw_bin, dw_outb]
    names1 = ["a_w_out", "w_kv", "b_w_in", "b_w_out"]
    from_sibling1, _, _ = _sibling_exchange("to_sibling_1", to_sibling=grads1)
    sums1 = [_add_sibling(where, g, r, "add_sibling_" + nm) for g, r, nm in zip(grads1, from_sibling1, names1)]
    dproj, dconv_w, from_chips1 = _a_bwd(doa, proj, conv_w, wouta, tm_mix, [t for t, _ in sums1])
    shards1 = [_add_chips(where, own, r, "add_chips_" + nm) for (_, own), r, nm in zip(sums1, from_chips1, names1)]
    dw_in = _dw(n1, dproj, D, tmw, "dw_a_in")
    from_sibling2, (g_wouta, g_wkv, g_wbin, g_woutb), _ = _sibling_exchange(
        "to_sibling_2", to_sibling=[dw_in], shards=shards1)
    part2, own2 = _add_sibling(where, dw_in, from_sibling2[0], "add_sibling_a_w_in")
    nt = seq // tm
    dn_first, from_chips2 = _a_in_bwd_matmul(dproj, win_g, tm, max(nt - max(nt // 4, 1), 1), [part2])
    grad_x, dg_apre = _a_in_bwd(dn_first, dproj, xs, dh1, win_g, g_apre, tm)
    shard2 = _add_chips(where, own2, from_chips2[0], "add_chips_a_w_in")
    drel, dsink = _bias_fold(dtab)

    smalls = jnp.concatenate([
        dg_apre[0:1], dg_b[2:3], dg_b[0:1], dg_b[1:2], dg_bpost[0:1], _pad_cols(dsink[0:1], D),
        _pad_cols(loss_acc[0:1], D), jnp.zeros((1, D), F32), dconv_w, _pad_cols(drel, D)], axis=0)
    _, (g_win,), small_all = _sibling_exchange("share_last", shards=[shard2], smalls=smalls)
    tot = _sum_smalls(small_all)

    big = {}
    for nm, g, w, m, v in [("a_w_in", g_win, a_w_in, m_a_w_in, v_a_w_in), ("a_w_out", g_wouta, a_w_out, m_a_w_out, v_a_w_out),
                           ("w_kv", g_wkv, w_kv, m_w_kv, v_w_kv), ("b_w_in", g_wbin, b_w_in, m_b_w_in, v_b_w_in),
                           ("b_w_out", g_woutb, b_w_out, m_b_w_out, v_b_w_out)]:
        shp = w.shape
        two = (shp[-2], shp[-1])
        d, nm_, nv_ = _adamw(g, w.reshape(two), m.reshape(two), v.reshape(two), "adamw_" + nm)
        big[nm] = (g.reshape(shp), d.reshape(shp), nm_.reshape(shp), nv_.reshape(shp))

    given = {"a_pre_norm": (a_pre_norm, m_a_pre_norm, v_a_pre_norm), "a_conv_w": (a_conv_w, m_a_conv_w, v_a_conv_w),
             "a_post_norm": (a_post_norm, m_a_post_norm, v_a_post_norm), "kv_norm": (kv_norm, m_kv_norm, v_kv_norm),
             "rel_bias": (rel_bias, m_rel_bias, v_rel_bias), "b_pre_norm": (b_pre_norm, m_b_pre_norm, v_b_pre_norm),
             "b_sinks": (b_sinks, m_b_sinks, v_b_sinks), "b_post_norm": (b_post_norm, m_b_post_norm, v_b_post_norm)}
    small = _small_update(where[1:2], tot, {nm: tuple(a.reshape(SMALL_PLACES[nm][2]) for a in wmv)
                                            for nm, wmv in given.items()})
    order = ["a_pre_norm", "a_w_in", "a_conv_w", "a_w_out", "a_post_norm", "kv_norm", "w_kv", "rel_bias",
             "b_pre_norm", "b_w_in", "b_sinks", "b_w_out", "b_post_norm"]
    outs = []
    for which in range(4):
        for nm in order:
            outs.append(big[nm][which] if nm in big else small[nm][which].reshape(given[nm][0].shape))
    loss = 0.5 * tot[LOSS_ROW, 0]
    return (loss, grad_x.reshape(x.shape), *outs)
```

```python
import functools
import math

import jax
import jax.numpy as jnp
from jax import lax
from jax.experimental import pallas as pl
from jax.experimental.pallas import tpu as pltpu

F32 = jnp.float32
BF16 = jnp.bfloat16
MESH = pl.DeviceIdType.MESH
SDS = jax.ShapeDtypeStruct

D = 1024
HEAD_DIM = 64
N_HEADS = 16
N_KV = 2
GROUP = 8
KV_W = 128
BLK = 128
N_BUCKETS = 32
MAX_EXACT = 16
MAX_DISTANCE = 128
EPS = 1e-6
NEG_INF = -1e30
Q_SCALE = HEAD_DIM ** -0.5

ADAM_LR = 0.001
ADAM_B1 = 0.9
ADAM_B2 = 0.999
ADAM_EPS = 1e-08
ADAM_WD = 0.01
ADAM_STEP = 10

N_CHIPS = 4
N_DEV = 8
VMEM_LIMIT = 56 * 1024 * 1024
SMALL_ROWS = 48
LOSS_ROW = 6
SMALL_PLACES = {
    "a_pre_norm": (True, 0, (1, D // 4)), "a_conv_w": (True, 8, (3, D // 4)), "a_post_norm": (True, 1, (1, D // 4)),
    "kv_norm": (False, 2, (1, D)), "rel_bias": (False, 16, (N_BUCKETS, N_HEADS)), "b_pre_norm": (False, 3, (1, D)),
    "b_sinks": (False, 5, (1, N_HEADS)), "b_post_norm": (False, 4, (1, D)),
}
HALO = 16


def _bucket_thresholds():
    def bucket(d):
        big = MAX_EXACT + int(math.log(d / MAX_EXACT) / math.log(MAX_DISTANCE / MAX_EXACT)
                              * (N_BUCKETS - MAX_EXACT))
        return d if d < MAX_EXACT else min(big, N_BUCKETS - 1)
    out = []
    for b in range(MAX_EXACT + 1, N_BUCKETS):
        out.append(min(d for d in range(MAX_EXACT, MAX_DISTANCE) if bucket(d) >= b))
    return tuple(out)


BUCKET_THRESHOLDS = _bucket_thresholds()


def _params(semantics=None, vmem=VMEM_LIMIT):
    return pltpu.CompilerParams(dimension_semantics=semantics, vmem_limit_bytes=vmem)


def _tile(n, pref):
    return pref if n >= 2 * pref else max(n // 2, 8)


def _rms_scale(v):
    return lax.rsqrt(jnp.mean(v * v, axis=-1, keepdims=True) + EPS)


def _nt(a, b):
    return lax.dot_general(a, b, (((1,), (1,)), ((), ())), preferred_element_type=F32)


def _tn(a, b):
    return lax.dot_general(a, b, (((0,), (0,)), ((), ())), preferred_element_type=F32)


def _nn(a, b):
    return jnp.dot(a, b, preferred_element_type=F32)


def _silu_parts(z):
    sg = jax.nn.sigmoid(z)
    return sg, z * sg


def _dsilu(z, sg):
    return sg * (1.0 + z * (1.0 - sg))


def _acc_row(ref, row, val):
    ref[row:row + 1, :] += val


def _gather_copies(outs, splits, ici_send, ici_recv, d2d_send, d2d_recv):
    x, y, c = lax.axis_index("x"), lax.axis_index("y"), lax.axis_index("c")
    k = 2 * x + y
    sibling = (x, y, 1 - c)

    def part(o_ref, chip, core, split):
        if not split:
            return o_ref.at[chip]
        h = o_ref.shape[1] // 2
        return o_ref.at[chip, pl.ds(pl.multiple_of(core * h, 16), h)]

    def remote(ref, a, j, sems, to):
        return pltpu.make_async_remote_copy(src_ref=ref, dst_ref=ref, send_sem=sems[0].at[3 * a + j],
                                            recv_sem=sems[1].at[3 * a + j], device_id=to, device_id_type=MESH)

    copies = []
    for a, (o_ref, split) in enumerate(zip(outs, splits)):
        for j, (px, py) in enumerate([(x, 1 - y), (1 - x, y), (1 - x, 1 - y)]):
            kj = 2 * px + py
            ici, d2d = (ici_send, ici_recv), (d2d_send, d2d_recv)
            copies.append((remote(part(o_ref, k, c, split), a, j, ici, (px, py, c)),
                           remote(part(o_ref, kj, c, split), a, j, ici, (px, py, c)),
                           remote(part(o_ref, kj, c, split), a, j, d2d, sibling) if split else None,
                           remote(part(o_ref, kj, 1 - c, split), a, j, d2d, sibling) if split else None))
    return copies


def _gather_sems(n):
    return [pltpu.SemaphoreType.DMA((3 * n,)) for _ in range(4)]


def _gather_weights(shards, small, n_now):
    n = len(shards)

    def body(*refs):
        ins, small_in = refs[:n], refs[n]
        outs, small_out = refs[n + 1:2 * n + 1], refs[2 * n + 1]
        sems = refs[2 * n + 2:]
        k = 2 * lax.axis_index("x") + lax.axis_index("y")
        for i_ref, o_ref in zip(ins, outs):
            o_ref[k] = i_ref[...].astype(BF16)
        small_out[k] = small_in[...]
        copies = _gather_copies(list(outs[:n_now]) + [small_out], [True] * n_now + [False], *sems)
        for send, _, _, _ in copies:
            send.start()
        for _, arrival, forward, _ in copies:
            arrival.wait_recv()
            if forward is not None:
                forward.start()
        for send, _, forward, forwarded in copies:
            if forward is not None:
                forwarded.wait_recv()
                forward.wait_send()
            send.wait_send()

    vm = pl.BlockSpec(memory_space=pltpu.VMEM)
    out_shape = [SDS((N_CHIPS,) + s.shape, BF16) for s in shards] + [SDS((N_CHIPS,) + small.shape, F32)]
    return pl.pallas_call(
        body, name="gather_weights", out_shape=out_shape,
        in_specs=[vm] * (n + 1), out_specs=[vm] * (n + 1),
        scratch_shapes=_gather_sems(n_now + 1),
        compiler_params=pltpu.CompilerParams(vmem_limit_bytes=VMEM_LIMIT),
    )(*shards, small)


def _a_in(chip, x, g_pre, weights, tm):
    s = x.shape[0]
    nt = s // tm
    n = len(weights)

    def body(chip_ref, x_ref, g_ref, *refs):
        proj_ref, n1_ref = refs[n:n + 2]
        gathered = refs[n + 2:2 * n + 2]
        wbuf, n1_all, fetch_sem = refs[2 * n + 2:2 * n + 5]
        sems = refs[2 * n + 5:]
        jj, i = pl.program_id(0), pl.program_id(1)
        copies = _gather_copies(gathered, [True] * n, *sems)

        def fetch(rel):
            slot = jnp.bitwise_xor(chip_ref[0], rel)
            return pltpu.make_async_copy(gathered[0].at[slot], wbuf.at[rel % 2], fetch_sem.at[rel % 2])

        @pl.when((jj == 0) & (i == 0))
        def _():
            fetch(0).start()
            copies[0][0].start()
            copies[1][0].start()
            fetch(0).wait()

        for rel in (1, 2, 3):
            @pl.when((jj == rel) & (i == 0))
            def _():
                fetch(rel).wait()

        @pl.when(jj == 0)
        def _():
            xv = x_ref[...]
            n1 = (xv * _rms_scale(xv) * g_ref[...]).astype(BF16)
            n1_ref[...] = n1
            n1_all[i] = n1
        proj_ref[...] = _nn(n1_all[i], wbuf[jj % 2]).astype(BF16)

        for rel in (1, 2, 3):
            @pl.when((jj == rel - 1) & (i == nt // 2))
            def _():
                _, arrival, forward, forwarded = copies[rel - 1]
                arrival.wait_recv()
                forward.start()
                forwarded.wait_recv()
                fetch(rel).start()
                if rel == 1:
                    copies[2][0].start()
                if rel == 2:
                    for send, _, _, _ in copies[3:]:
                        send.start()

        @pl.when((jj == 3) & (i == max(nt - 2, 0)))
        def _():
            for _, arrival, forward, _ in copies[3:]:
                arrival.wait_recv()
                forward.start()

        @pl.when((jj == 3) & (i == nt - 1))
        def _():
            for _, _, _, forwarded in copies[3:]:
                forwarded.wait_recv()
            for send, _, forward, _ in copies:
                forward.wait_send()
                send.wait_send()

    anyspace = pl.BlockSpec(memory_space=pl.ANY)
    proj, n1, *gathered = pl.pallas_call(
        body, name="a_in",
        grid_spec=pltpu.PrefetchScalarGridSpec(
            num_scalar_prefetch=1, grid=(4, nt),
            in_specs=[pl.BlockSpec((tm, D), lambda jj, i, c: (jnp.where(jj == 0, i, nt - 1), 0)),
                      pl.BlockSpec((1, D), lambda jj, i, c: (0, 0))] + [anyspace] * n,
            out_specs=[pl.BlockSpec((tm, D), lambda jj, i, c: (i, jnp.bitwise_xor(c[0], jj))),
                       pl.BlockSpec((tm, D), lambda jj, i, c: (jnp.where(jj == 0, i, nt - 1), 0))] + [anyspace] * n,
            scratch_shapes=[pltpu.VMEM((2, D, D), BF16), pltpu.VMEM((nt, tm, D), BF16),
                            pltpu.SemaphoreType.DMA((2,))] + _gather_sems(n)),
        out_shape=[SDS((s, 4 * D), BF16), SDS((s, D), BF16)] + [SDS(w.shape, w.dtype) for w in weights],
        input_output_aliases={3 + a: 2 + a for a in range(n)},
        compiler_params=_params(("arbitrary", "arbitrary")),
    )(chip, x, g_pre, *weights)
    return proj, n1, gathered


def _shift_rows(v, last, second_last, rows):
    v1 = jnp.where(rows >= 1, pltpu.roll(v, 1, 0), last)
    v2 = jnp.where(rows >= 2, pltpu.roll(v, 2, 0), jnp.where(rows == 1, last, second_last))
    return v1, v2


def _a_mix(proj, x, conv_w, w_out, g_post, tm):
    s = x.shape[0]

    def body(proj_ref, x_ref, cw_ref, w_ref, g_ref, ya_ref, oa_ref, h1_ref, carry):
        @pl.when(pl.program_id(0) == 0)
        def _():
            carry[...] = jnp.zeros_like(carry)
        v = proj_ref[:, D:2 * D].astype(F32) * proj_ref[:, 2 * D:3 * D].astype(F32)
        rows = lax.broadcasted_iota(jnp.int32, (tm, D), 0)
        before = carry[...]
        v1, v2 = _shift_rows(v, before[7:8, :], before[6:7, :], rows)
        carry[...] = v[tm - 8:tm, :]
        conv = cw_ref[0:1, :] * v2 + cw_ref[1:2, :] * v1 + cw_ref[2:3, :] * v
        _, sz = _silu_parts(proj_ref[:, 3 * D:4 * D].astype(F32))
        ya = (proj_ref[:, 0:D].astype(F32) * conv * sz).astype(BF16)
        ya_ref[...] = ya
        oa = _nn(ya, w_ref[...])
        oa_ref[...] = oa.astype(BF16)
        h1_ref[...] = x_ref[...] + oa * _rms_scale(oa) * g_ref[...]

    row = lambda i: (i, 0)
    fix = lambda i: (0, 0)
    return pl.pallas_call(
        body, name="a_mix", grid=(s // tm,),
        in_specs=[pl.BlockSpec((tm, 4 * D), row), pl.BlockSpec((tm, D), row), pl.BlockSpec((8, D), fix),
                  pl.BlockSpec((D, D), fix), pl.BlockSpec((1, D), fix)],
        out_specs=[pl.BlockSpec((tm, D), row)] * 3,
        out_shape=[SDS((s, D), BF16), SDS((s, D), BF16), SDS((s, D), F32)],
        scratch_shapes=[pltpu.VMEM((8, D), F32)],
        compiler_params=_params(("arbitrary",)),
    )(proj, x, conv_w, w_out, g_post)


def _b_in(h1, g_kv, g_pre, w_kv, wbin_g, tm):
    s = h1.shape[0]

    def body(h_ref, gk_ref, gb_ref, wkv_ref, wb_ref, kv_ref, q_ref, z_ref):
        h = h_ref[...]
        hh = h * _rms_scale(h)
        nk = (hh * gk_ref[...]).astype(BF16)
        nb = (hh * gb_ref[...]).astype(BF16)
        kv_ref[...] = _nn(nk, wkv_ref[...]).astype(BF16)
        for j in range(2):
            q_ref[:, 512 * j:512 * (j + 1)] = (_nn(nb, wb_ref[j]) * Q_SCALE).astype(BF16)
            z_ref[:, 512 * j:512 * (j + 1)] = _nn(nb, wb_ref[2 + j]).astype(BF16)

    row = lambda i: (i, 0)
    fix = lambda i: (0, 0)
    return pl.pallas_call(
        body, name="b_in", grid=(s // tm,),
        in_specs=[pl.BlockSpec((tm, D), row), pl.BlockSpec((1, D), fix), pl.BlockSpec((1, D), fix),
                  pl.BlockSpec((D, 2 * KV_W), fix), pl.BlockSpec((4, D, 512), lambda i: (0, 0, 0))],
        out_specs=[pl.BlockSpec((tm, 2 * KV_W), row), pl.BlockSpec((tm, D), row), pl.BlockSpec((tm, D), row)],
        out_shape=[SDS((s, 2 * KV_W), BF16), SDS((s, D), BF16), SDS((s, D), BF16)],
        compiler_params=_params(("parallel",)),
    )(h1, g_kv, g_pre, w_kv, wbin_g)


def _band_buckets():
    q = lax.broadcasted_iota(jnp.int32, (BLK, 2 * BLK), 0)
    k = lax.broadcasted_iota(jnp.int32, (BLK, 2 * BLK), 1)
    dist = q + BLK - k
    bucket = jnp.where(dist < MAX_EXACT, dist, MAX_EXACT)
    for t in BUCKET_THRESHOLDS:
        bucket = bucket + jnp.where(dist >= t, 1, 0)
    in_window = (dist >= 0) & (dist < BLK)
    return jnp.where(in_window, bucket, -1)


def _head_place(h):
    kh, j, e = h // GROUP, (h % GROUP) // 2, h % 2
    return kh, slice(BLK * j, BLK * (j + 1)), slice(2 * BLK * e, 2 * BLK * (e + 1))


def _bias_table(rel_bias, sinks):
    def body(rb_ref, sink_ref, tab_ref):
        bucket = _band_buckets()
        col = lax.broadcasted_iota(jnp.int32, (BLK, 2 * BLK), 1)
        for h in range(N_HEADS):
            acc = jnp.where(bucket < 0, NEG_INF, 0.0).astype(F32)
            for b in range(N_BUCKETS):
                acc = jnp.where(bucket == b, rb_ref[b, h], acc)
            acc = jnp.where(col == 0, sink_ref[h], acc)
            kh, rows, cols = _head_place(h)
            tab_ref[1, kh, rows, cols] = acc
            tab_ref[0, kh, rows, cols] = jnp.where((col > 0) & (col < BLK), NEG_INF, acc)

    return pl.pallas_call(
        body, name="bias_table", out_shape=SDS((2, N_KV, 4 * BLK, 4 * BLK), F32),
        in_specs=[pl.BlockSpec(memory_space=pltpu.SMEM), pl.BlockSpec(memory_space=pltpu.SMEM)],
        out_specs=pl.BlockSpec(memory_space=pltpu.VMEM),
    )(rel_bias, sinks)


def _bias_fold(dtab):
    def body(dtab_ref, out_ref, dsink_ref):
        bucket = _band_buckets()
        row = lax.broadcasted_iota(jnp.int32, (N_BUCKETS, 128), 0)
        lane = lax.broadcasted_iota(jnp.int32, (N_BUCKETS, 128), 1)
        row8 = lax.broadcasted_iota(jnp.int32, (8, 128), 0)
        lane8 = lax.broadcasted_iota(jnp.int32, (8, 128), 1)
        acc = jnp.zeros((N_BUCKETS, 128), F32)
        dsink = jnp.zeros((8, 128), F32)
        for h in range(N_HEADS):
            kh, rows, cols = _head_place(h)
            dt = dtab_ref[kh, rows, cols]
            for b in range(N_BUCKETS):
                val = jnp.sum(jnp.where(bucket == b, dt, 0.0))
                acc = acc + jnp.where((row == b) & (lane == h), val, 0.0)
            dsink = dsink + jnp.where((row8 == 0) & (lane8 == h), jnp.sum(dt[:, 0:1]), 0.0)
        out_ref[...] = acc
        dsink_ref[...] = dsink

    vm = pl.BlockSpec(memory_space=pltpu.VMEM)
    return pl.pallas_call(
        body, name="bias_fold", out_shape=[SDS((N_BUCKETS, 128), F32), SDS((8, 128), F32)],
        in_specs=[vm], out_specs=[vm, vm],
    )(dtab)


def _pair_operands(prev, cur):
    t = jnp.concatenate([prev, cur], axis=0).astype(F32)
    t = jnp.where(lax.broadcasted_iota(jnp.int32, t.shape, 0) == 0, 0.0, t)
    tr = pltpu.roll(t, HEAD_DIM, 1)
    lo = lax.broadcasted_iota(jnp.int32, t.shape, 1) < HEAD_DIM
    zero = jnp.zeros_like(t)
    head0 = jnp.concatenate([jnp.where(lo, t, zero), jnp.where(lo, zero, tr)], axis=0).astype(BF16)
    head1 = jnp.concatenate([jnp.where(lo, tr, zero), jnp.where(lo, zero, t)], axis=0).astype(BF16)
    return head0, head1


def _pair_fold(d0, d1):
    lo = lax.broadcasted_iota(jnp.int32, (2 * BLK, KV_W), 1) < HEAD_DIM
    zero = jnp.zeros((2 * BLK, KV_W), F32)
    g0 = jnp.where(lo, d0[0:256], zero) + pltpu.roll(jnp.where(lo, zero, d0[256:512]), HEAD_DIM, 1)
    g1 = pltpu.roll(jnp.where(lo, d1[0:256], zero), HEAD_DIM, 1) + jnp.where(lo, zero, d1[256:512])
    return jnp.where(lax.broadcasted_iota(jnp.int32, (2 * BLK, KV_W), 0) == 0, 0.0, g0 + g1)


def _stack_pairs(ref, kh):
    return jnp.concatenate([ref[:, 128 * (4 * kh + j):128 * (4 * kh + j + 1)] for j in range(4)], axis=0)


def _table_spec():
    return pl.BlockSpec((1, N_KV, 4 * BLK, 4 * BLK), lambda n: (jnp.minimum(n, 1), 0, 0, 0))


def _attn_fwd(q, kv, tab):
    s = q.shape[0]

    def body(q_ref, kp_ref, kc_ref, vp_ref, vc_ref, tab_ref, att_ref, stats_ref):
        k2 = _pair_operands(kp_ref[...], kc_ref[...])
        v2 = _pair_operands(vp_ref[...], vc_ref[...])
        lane = lax.broadcasted_iota(jnp.int32, (BLK, 128), 1)
        stats = jnp.zeros((BLK, 128), F32)
        for kh in range(N_KV):
            sc = _nt(_stack_pairs(q_ref, kh), k2[kh])
            ps = []
            for e in range(2):
                lg = sc[:, 256 * e:256 * (e + 1)] + tab_ref[0, kh, :, 256 * e:256 * (e + 1)]
                m = jnp.max(lg, axis=-1, keepdims=True)
                ex = jnp.exp(lg - m)
                den = jnp.sum(ex, axis=-1, keepdims=True)
                ps.append(ex * (1.0 / den))
                lse = m + jnp.log(den)
                for j in range(4):
                    stats = jnp.where(lane == GROUP * kh + 2 * j + e, lse[BLK * j:BLK * (j + 1)], stats)
            out = _nn(jnp.concatenate(ps, axis=1).astype(BF16), v2[kh])
            for j in range(4):
                att_ref[:, 128 * (4 * kh + j):128 * (4 * kh + j + 1)] = out[BLK * j:BLK * (j + 1)].astype(BF16)
        stats_ref[...] = stats

    cur = lambda n: (n, 0)
    prev = lambda n: (jnp.maximum(n - 1, 0), 0)
    return pl.pallas_call(
        body, name="attn_fwd", grid=(s // BLK,),
        in_specs=[pl.BlockSpec((BLK, D), cur),
                  pl.BlockSpec((BLK, KV_W), prev), pl.BlockSpec((BLK, KV_W), cur),
                  pl.BlockSpec((BLK, KV_W), lambda n: (jnp.maximum(n - 1, 0), 1)),
                  pl.BlockSpec((BLK, KV_W), lambda n: (n, 1)), _table_spec()],
        out_specs=[pl.BlockSpec((BLK, D), cur), pl.BlockSpec((BLK, 128), cur)],
        out_shape=[SDS((s, D), BF16), SDS((s, 128), F32)],
        compiler_params=_params(("parallel",)),
    )(q, kv, kv, kv, kv, tab)


def _mid(att, zb, h1, tgt, w_out, g_post, tm):
    s = att.shape[0]
    nt = s // tm

    def body(att_ref, z_ref, h1_ref, t_ref, w_ref, g_ref,
             dh_ref, dqz_ref, datt_ref, loss_ref, dg_ref, dw_ref, dw_acc):
        @pl.when(pl.program_id(0) == 0)
        def _():
            loss_ref[...] = jnp.zeros_like(loss_ref)
            dg_ref[...] = jnp.zeros_like(dg_ref)
            dw_acc[...] = jnp.zeros_like(dw_acc)
        att = att_ref[...].astype(F32)
        z = z_ref[...].astype(F32)
        sg, sz = _silu_parts(z)
        ob = (att * sz).astype(BF16)
        y2 = _nn(ob, w_ref[...])
        r2 = _rms_scale(y2)
        yh = y2 * r2
        g = g_ref[...]
        err = (h1_ref[...] + yh * g) - t_ref[...]
        loss_ref[...] += jnp.sum(jnp.sum(err * err, axis=-1, keepdims=True) / D)
        dh = err / D
        dh_ref[...] = dh
        _acc_row(dg_ref, 0, jnp.sum(dh * yh, axis=0, keepdims=True))
        dyh = dh * g
        dy = (r2 * (dyh - yh * jnp.mean(dyh * yh, axis=-1, keepdims=True))).astype(BF16)
        dw_acc[...] += _tn(ob, dy)
        dob = _nt(dy, w_ref[...])
        datt_ref[...] = (dob * sz).astype(BF16)
        dqz_ref[...] = (dob * att * _dsilu(z, sg)).astype(BF16)

        @pl.when(pl.program_id(0) == nt - 1)
        def _():
            pltpu.sync_copy(dw_acc, dw_ref)

    row = lambda i: (i, 0)
    fix = lambda i: (0, 0)
    return pl.pallas_call(
        body, name="mid", grid=(nt,),
        in_specs=[pl.BlockSpec((tm, D), row)] * 4 + [pl.BlockSpec((D, D), fix), pl.BlockSpec((1, D), fix)],
        out_specs=[pl.BlockSpec((tm, D), row), pl.BlockSpec((tm, D), lambda i: (i, 1)), pl.BlockSpec((tm, D), row),
                   pl.BlockSpec((8, 128), fix), pl.BlockSpec((8, D), fix), pl.BlockSpec(memory_space=pl.ANY)],
        out_shape=[SDS((s, D), F32), SDS((s, 2 * D), BF16), SDS((s, D), BF16), SDS((8, 128), F32),
                   SDS((8, D), F32), SDS((D, D), F32)],
        scratch_shapes=[pltpu.VMEM((D, D), F32)],
        compiler_params=_params(("arbitrary",)),
    )(att, zb, h1, tgt, w_out, g_post)


def _attn_bwd(q, kv, datt, stats, tab, dqz):
    s = q.shape[0]
    nb = s // BLK

    def body(q_ref, kp_ref, kc_ref, vp_ref, vc_ref, da_ref, st_ref, tab_ref, dqz_in,
             dq_ref, dkv_ref, dtab_ref, dk_carry, dv_carry):
        del dqz_in
        n = pl.program_id(0)

        @pl.when(n == 0)
        def _():
            dtab_ref[...] = jnp.zeros_like(dtab_ref)
            dk_carry[...] = jnp.zeros_like(dk_carry)
            dv_carry[...] = jnp.zeros_like(dv_carry)

        @pl.when(n < nb)
        def _():
            k2 = _pair_operands(kp_ref[...], kc_ref[...])
            v2 = _pair_operands(vp_ref[...], vc_ref[...])
            lane = lax.broadcasted_iota(jnp.int32, (BLK, 128), 1)
            stats = st_ref[...]
            dk2, dv2 = [], []
            for kh in range(N_KV):
                qs = _stack_pairs(q_ref, kh)
                das = _stack_pairs(da_ref, kh)
                sc = _nt(qs, k2[kh])
                dp = _nt(das, v2[kh])
                ps, dss = [], []
                for e in range(2):
                    heads = [GROUP * kh + 2 * j + e for j in range(4)]
                    lse = jnp.concatenate([jnp.sum(jnp.where(lane == h, stats, 0.0), axis=-1, keepdims=True)
                                           for h in heads], axis=0)
                    cols = slice(256 * e, 256 * (e + 1))
                    p = jnp.exp(sc[:, cols] + tab_ref[0, kh, :, cols] - lse)
                    delta = jnp.sum(p * dp[:, cols], axis=-1, keepdims=True)
                    ds = p * (dp[:, cols] - delta)
                    dtab_ref[kh, :, cols] += ds
                    ps.append(p)
                    dss.append(ds)
                p2 = jnp.concatenate(ps, axis=1).astype(BF16)
                ds2 = jnp.concatenate(dss, axis=1).astype(BF16)
                dq = _nn(ds2, k2[kh]) * Q_SCALE
                for j in range(4):
                    dq_ref[:, 128 * (4 * kh + j):128 * (4 * kh + j + 1)] = dq[BLK * j:BLK * (j + 1)].astype(BF16)
                dk2.append(_tn(ds2, qs))
                dv2.append(_tn(p2, das))
            dkk = _pair_fold(dk2[0], dk2[1])
            dvv = _pair_fold(dv2[0], dv2[1])
            dkv_ref[:, 0:KV_W] = (dk_carry[...] + dkk[0:BLK]).astype(BF16)
            dkv_ref[:, KV_W:2 * KV_W] = (dv_carry[...] + dvv[0:BLK]).astype(BF16)
            dk_carry[...] = dkk[BLK:2 * BLK]
            dv_carry[...] = dvv[BLK:2 * BLK]

        @pl.when(n == nb)
        def _():
            dkv_ref[:, 0:KV_W] = dk_carry[...].astype(BF16)
            dkv_ref[:, KV_W:2 * KV_W] = dv_carry[...].astype(BF16)

    cur = lambda n: (jnp.minimum(n, nb - 1), 0)
    prev = lambda n: (jnp.clip(n - 1, 0, nb - 1), 0)
    return pl.pallas_call(
        body, name="attn_bwd", grid=(nb + 1,),
        in_specs=[pl.BlockSpec((BLK, D), cur),
                  pl.BlockSpec((BLK, KV_W), prev), pl.BlockSpec((BLK, KV_W), cur),
                  pl.BlockSpec((BLK, KV_W), lambda n: (jnp.clip(n - 1, 0, nb - 1), 1)),
                  pl.BlockSpec((BLK, KV_W), lambda n: (jnp.minimum(n, nb - 1), 1)),
                  pl.BlockSpec((BLK, D), cur), pl.BlockSpec((BLK, 128), cur), _table_spec(),
                  pl.BlockSpec(memory_space=pl.ANY)],
        out_specs=[pl.BlockSpec((BLK, D), cur), pl.BlockSpec((BLK, 2 * KV_W), prev),
                   pl.BlockSpec((N_KV, 4 * BLK, 4 * BLK), lambda n: (0, 0, 0))],
        out_shape=[SDS((s, 2 * D), BF16), SDS((s, 2 * KV_W), BF16), SDS((N_KV, 4 * BLK, 4 * BLK), F32)],
        scratch_shapes=[pltpu.VMEM((BLK, KV_W), F32), pltpu.VMEM((BLK, KV_W), F32)],
        input_output_aliases={8: 0},
        compiler_params=_params(("arbitrary",)),
    )(q, kv, kv, kv, kv, datt, stats, tab, dqz)


def _b_bwd(dqz, dkv, h1, dh2, oa, wbin_g, w_kv, g_kv, g_pre, g_apost, tm):
    s = h1.shape[0]
    nt = s // tm

    def body(dqz_ref, dkv_ref, h_ref, dh2_ref, oa_ref, wb_ref, wkv_ref, gk_ref, gb_ref, ga_ref,
             dh1_ref, doa_ref, dg_ref, dwb_ref, dwkv_ref, dwb_acc, dwkv_acc):
        @pl.when(pl.program_id(0) == 0)
        def _():
            dg_ref[...] = jnp.zeros_like(dg_ref)
            dwb_acc[...] = jnp.zeros_like(dwb_acc)
            dwkv_acc[...] = jnp.zeros_like(dwkv_acc)
        dnb = _nt(dqz_ref[:, 0:512], wb_ref[0])
        for j in range(1, 4):
            dnb = dnb + _nt(dqz_ref[:, 512 * j:512 * (j + 1)], wb_ref[j])
        dnk = _nt(dkv_ref[...], wkv_ref[...])
        h = h_ref[...]
        r = _rms_scale(h)
        hh = h * r
        nb = (hh * gb_ref[...]).astype(BF16)
        for j in range(4):
            dwb_acc[j] += _tn(nb, dqz_ref[:, 512 * j:512 * (j + 1)])
        dwkv_acc[...] += _tn((hh * gk_ref[...]).astype(BF16), dkv_ref[...])
        _acc_row(dg_ref, 0, jnp.sum(dnk * hh, axis=0, keepdims=True))
        _acc_row(dg_ref, 1, jnp.sum(dnb * hh, axis=0, keepdims=True))
        dhh = dnb * gb_ref[...] + dnk * gk_ref[...]
        dh1 = dh2_ref[...] + r * (dhh - hh * jnp.mean(dhh * hh, axis=-1, keepdims=True))
        dh1_ref[...] = dh1
        oa = oa_ref[...].astype(F32)
        ra = _rms_scale(oa)
        oh = oa * ra
        _acc_row(dg_ref, 2, jnp.sum(dh1 * oh, axis=0, keepdims=True))
        doh = dh1 * ga_ref[...]
        doa_ref[...] = (ra * (doh - oh * jnp.mean(doh * oh, axis=-1, keepdims=True))).astype(BF16)

        @pl.when(pl.program_id(0) == nt - 1)
        def _():
            pltpu.sync_copy(dwb_acc, dwb_ref)
            pltpu.sync_copy(dwkv_acc, dwkv_ref)

    row = lambda i: (i, 0)
    fix = lambda i: (0, 0)
    anyspace = pl.BlockSpec(memory_space=pl.ANY)
    return pl.pallas_call(
        body, name="b_bwd", grid=(nt,),
        in_specs=[pl.BlockSpec((tm, 2 * D), row), pl.BlockSpec((tm, 2 * KV_W), row), pl.BlockSpec((tm, D), row),
                  pl.BlockSpec((tm, D), row), pl.BlockSpec((tm, D), row),
                  pl.BlockSpec((4, D, 512), lambda i: (0, 0, 0)), pl.BlockSpec((D, 2 * KV_W), fix),
                  pl.BlockSpec((1, D), fix), pl.BlockSpec((1, D), fix), pl.BlockSpec((1, D), fix)],
        out_specs=[pl.BlockSpec((tm, D), row), pl.BlockSpec((tm, D), row), pl.BlockSpec((8, D), fix), anyspace, anyspace],
        out_shape=[SDS((s, D), F32), SDS((s, D), BF16), SDS((8, D), F32), SDS((4, D, 512), F32),
                   SDS((D, 2 * KV_W), F32)],
        scratch_shapes=[pltpu.VMEM((4, D, 512), F32), pltpu.VMEM((D, 2 * KV_W), F32)],
        compiler_params=_params(("arbitrary",)),
    )(dqz, dkv, h1, dh2, oa, wbin_g, w_kv, g_kv, g_pre, g_apost)


def _chip_exchange(parts, recvs, send, recv):
    x, y, c = lax.axis_index("x"), lax.axis_index("y"), lax.axis_index("c")
    chips = [(x, 1 - y), (1 - x, y), (1 - x, 1 - y)]
    copies = []
    for a, (t, r) in enumerate(zip(parts, recvs)):
        for j, (px, py) in enumerate(chips):
            copies.append(pltpu.make_async_remote_copy(
                src_ref=t.at[2 * px + py], dst_ref=r.at[j], send_sem=send.at[3 * a + j],
                recv_sem=recv.at[3 * a + j], device_id=(px, py, c), device_id_type=MESH))
    return copies


def _exchange_specs(parts):
    anyspace = pl.BlockSpec(memory_space=pl.ANY)
    n = len(parts)
    return ([anyspace] * n, [anyspace] * n, [SDS((3,) + t.shape[1:], t.dtype) for t in parts],
            [pltpu.SemaphoreType.DMA((3 * n,)), pltpu.SemaphoreType.DMA((3 * n,))])


def _a_bwd(doa, proj, conv_w, w_out, tm, parts):
    s = doa.shape[0]
    nt = s // tm
    n = len(parts)
    ex_in, ex_out, ex_shape, ex_sems = _exchange_specs(parts)

    def body(*refs):
        doa_ref, proj_ref, halo_ref, cw_ref, w_ref = refs[:5]
        part_refs = refs[5:5 + n]
        dproj_ref, dcw_ref = refs[5 + n:7 + n]
        recv_refs = refs[7 + n:7 + 2 * n]
        carry, send, recv = refs[7 + 2 * n:]
        i = pl.program_id(0)
        r = nt - 1 - i

        @pl.when(i == 0)
        def _():
            dcw_ref[...] = jnp.zeros_like(dcw_ref)
            carry[...] = jnp.zeros_like(carry)
            for cp in _chip_exchange(part_refs, recv_refs, send, recv):
                cp.start()
        dya = _nt(doa_ref[...], w_ref[...])
        bg = proj_ref[:, 0:D].astype(F32)
        cg = proj_ref[:, D:2 * D].astype(F32)
        u = proj_ref[:, 2 * D:3 * D].astype(F32)
        z = proj_ref[:, 3 * D:4 * D].astype(F32)
        v = cg * u
        before = jnp.where(r > 0, halo_ref[:, D:2 * D].astype(F32) * halo_ref[:, 2 * D:3 * D].astype(F32), 0.0)
        rows = lax.broadcasted_iota(jnp.int32, (tm, D), 0)
        v1, v2 = _shift_rows(v, before[HALO - 1:HALO, :], before[HALO - 2:HALO - 1, :], rows)
        conv = cw_ref[0:1, :] * v2 + cw_ref[1:2, :] * v1 + cw_ref[2:3, :] * v
        sg, sz = _silu_parts(z)
        dproj_ref[:, 0:D] = (dya * conv * sz).astype(BF16)
        dproj_ref[:, 3 * D:4 * D] = (dya * bg * conv * _dsilu(z, sg)).astype(BF16)
        dconv = dya * bg * sz
        _acc_row(dcw_ref, 0, jnp.sum(dconv * v2, axis=0, keepdims=True))
        _acc_row(dcw_ref, 1, jnp.sum(dconv * v1, axis=0, keepdims=True))
        _acc_row(dcw_ref, 2, jnp.sum(dconv * v, axis=0, keepdims=True))
        after = carry[...]
        up1 = jnp.where(rows < tm - 1, pltpu.roll(dconv, tm - 1, 0), after[0:1, :])
        up2 = jnp.where(rows < tm - 2, pltpu.roll(dconv, tm - 2, 0),
                        jnp.where(rows == tm - 2, after[0:1, :], after[1:2, :]))
        carry[...] = dconv[0:8, :]
        dv = cw_ref[2:3, :] * dconv + cw_ref[1:2, :] * up1 + cw_ref[0:1, :] * up2
        dproj_ref[:, D:2 * D] = (dv * u).astype(BF16)
        dproj_ref[:, 2 * D:3 * D] = (dv * cg).astype(BF16)

        @pl.when(i == nt - 1)
        def _():
            for cp in _chip_exchange(part_refs, recv_refs, send, recv):
                cp.wait()

    rev = lambda i: (nt - 1 - i, 0)
    fix = lambda i: (0, 0)
    halo = lambda i: (jnp.maximum((nt - 1 - i) * (tm // HALO) - 1, 0), 0)
    dproj, dcw, *got = pl.pallas_call(
        body, name="a_bwd", grid=(nt,),
        in_specs=[pl.BlockSpec((tm, D), rev), pl.BlockSpec((tm, 4 * D), rev), pl.BlockSpec((HALO, 4 * D), halo),
                  pl.BlockSpec((8, D), fix), pl.BlockSpec((D, D), fix)] + ex_in,
        out_specs=[pl.BlockSpec((tm, 4 * D), rev), pl.BlockSpec((8, D), fix)] + ex_out,
        out_shape=[SDS((s, 4 * D), BF16), SDS((8, D), F32)] + ex_shape,
        scratch_shapes=[pltpu.VMEM((8, D), F32)] + ex_sems,
        compiler_params=_params(("arbitrary",)),
    )(doa, proj, proj, conv_w, w_out, *parts)
    return dproj, dcw, got


def _dn1(dp_ref, w_ref):
    dn = _nt(dp_ref[:, 0:D], w_ref[0])
    for j in range(1, 4):
        dn = dn + _nt(dp_ref[:, D * j:D * (j + 1)], w_ref[j])
    return dn


def _a_in_bwd_matmul(dproj, win_g, tm, count, parts):
    n = len(parts)
    ex_in, ex_out, ex_shape, ex_sems = _exchange_specs(parts)

    def body(*refs):
        dp_ref, w_ref = refs[:2]
        part_refs = refs[2:2 + n]
        dn_ref = refs[2 + n]
        recv_refs = refs[3 + n:3 + 2 * n]
        sems = refs[3 + 2 * n:]

        @pl.when(pl.program_id(0) == 0)
        def _():
            for cp in _chip_exchange(part_refs, recv_refs, *sems):
                cp.start()
        dn_ref[...] = _dn1(dp_ref, w_ref)

        @pl.when(pl.program_id(0) == count - 1)
        def _():
            for cp in _chip_exchange(part_refs, recv_refs, *sems):
                cp.wait()

    row = lambda i: (i, 0)
    dn, *got = pl.pallas_call(
        body, name="a_in_bwd_matmul", grid=(count,),
        in_specs=[pl.BlockSpec((tm, 4 * D), row), pl.BlockSpec((4, D, D), lambda i: (0, 0, 0))] + ex_in,
        out_specs=[pl.BlockSpec((tm, D), row)] + ex_out,
        out_shape=[SDS((count * tm, D), F32)] + ex_shape,
        scratch_shapes=ex_sems,
        compiler_params=_params(("arbitrary",)),
    )(dproj, win_g, *parts)
    return dn, got


def _a_in_bwd(dn_first, dproj, x, dh1, win_g, g_pre, tm):
    s = x.shape[0]
    nt = s // tm
    count = dn_first.shape[0] // tm

    def body(dn_ref, dp_ref, x_ref, dh_ref, w_ref, g_ref, gx_ref, dg_ref, dn_s):
        i = pl.program_id(0)

        @pl.when(i == 0)
        def _():
            dg_ref[...] = jnp.zeros_like(dg_ref)

        @pl.when(i < count)
        def _():
            dn_s[...] = dn_ref[...]

        @pl.when(i >= count)
        def _():
            dn_s[...] = _dn1(dp_ref, w_ref)
        dn = dn_s[...]
        xv = x_ref[...]
        r = _rms_scale(xv)
        xh = xv * r
        _acc_row(dg_ref, 0, jnp.sum(dn * xh, axis=0, keepdims=True))
        dxh = dn * g_ref[...]
        gx_ref[...] = dh_ref[...] + r * (dxh - xh * jnp.mean(dxh * xh, axis=-1, keepdims=True))

    row = lambda i: (i, 0)
    fix = lambda i: (0, 0)
    return pl.pallas_call(
        body, name="a_in_bwd", grid=(nt,),
        in_specs=[pl.BlockSpec((tm, D), lambda i: (jnp.minimum(i, count - 1), 0)),
                  pl.BlockSpec((tm, 4 * D), lambda i: (jnp.maximum(i, count), 0)),
                  pl.BlockSpec((tm, D), row), pl.BlockSpec((tm, D), row),
                  pl.BlockSpec((4, D, D), lambda i: (0, 0, 0)), pl.BlockSpec((1, D), fix)],
        out_specs=[pl.BlockSpec((tm, D), row), pl.BlockSpec((8, D), fix)],
        out_shape=[SDS((s, D), F32), SDS((8, D), F32)],
        scratch_shapes=[pltpu.VMEM((tm, D), F32)],
        compiler_params=_params(("arbitrary",)),
    )(dn_first, dproj, x, dh1, win_g, g_pre)


def _dw(a, b, tn, tmw, name):
    s, k = a.shape
    n = b.shape[1]

    def body(a_ref, b_ref, o_ref):
        @pl.when(pl.program_id(1) == 0)
        def _():
            o_ref[...] = jnp.zeros_like(o_ref)
        o_ref[0] += _tn(a_ref[...], b_ref[...])

    return pl.pallas_call(
        body, name=name, grid=(n // tn, s // tmw),
        in_specs=[pl.BlockSpec((tmw, k), lambda j, t: (t, 0)), pl.BlockSpec((tmw, tn), lambda j, t: (t, j))],
        out_specs=pl.BlockSpec((1, k, tn), lambda j, t: (j, 0, 0)),
        out_shape=SDS((n // tn, k, tn), F32),
        compiler_params=_params(("parallel", "arbitrary")),
    )(a, b)


def _sibling_exchange(name, to_sibling=(), shards=(), smalls=None):
    n_g, n_h = len(to_sibling), len(shards)
    has_small = smalls is not None

    def body(*refs):
        gs = refs[:n_g]
        pos = n_g + n_h
        small_in = refs[pos] if has_small else None
        pos += has_small
        rs, fs = refs[pos:pos + n_g], refs[pos + n_g:pos + n_g + n_h]
        pos += n_g + n_h
        small_all = refs[pos] if has_small else None
        pos += has_small
        dsend, drecv, ssend, srecv = refs[pos:]
        x, y, c = lax.axis_index("x"), lax.axis_index("y"), lax.axis_index("c")
        sibling = (x, y, 1 - c)
        sends, arrivals = [], []
        for a, (g, r) in enumerate(zip(gs, rs)):
            h = g.shape[1] // 2
            src = g.at[:, pl.ds(pl.multiple_of((1 - c) * h, 8), h), :]
            sends.append(pltpu.make_async_remote_copy(src_ref=src, dst_ref=r, send_sem=dsend.at[a], recv_sem=drecv.at[a],
                                                      device_id=sibling, device_id_type=MESH))
            arrivals.append(pltpu.make_async_remote_copy(src_ref=r, dst_ref=r, send_sem=dsend.at[a], recv_sem=drecv.at[a],
                                                         device_id=sibling, device_id_type=MESH))
        for b, full in enumerate(fs):
            h = full.shape[0] // 2
            mine = full.at[pl.ds(pl.multiple_of(c * h, 8), h)]
            theirs = full.at[pl.ds(pl.multiple_of((1 - c) * h, 8), h)]
            sends.append(pltpu.make_async_remote_copy(src_ref=mine, dst_ref=mine, send_sem=dsend.at[n_g + b],
                                                      recv_sem=drecv.at[n_g + b], device_id=sibling, device_id_type=MESH))
            arrivals.append(pltpu.make_async_remote_copy(src_ref=mine, dst_ref=theirs, send_sem=dsend.at[n_g + b],
                                                         recv_sem=drecv.at[n_g + b], device_id=sibling, device_id_type=MESH))
        if has_small:
            me = 4 * x + 2 * y + c
            small_all[me] = small_in[...]
            for rel in range(1, N_DEV):
                fx, fy, fc = rel >> 2, (rel >> 1) & 1, rel & 1
                peer = (x + fx - 2 * x * fx, y + fy - 2 * y * fy, c + fc - 2 * c * fc)
                sender = 4 * peer[0] + 2 * peer[1] + peer[2]
                sends.append(pltpu.make_async_remote_copy(
                    src_ref=small_in, dst_ref=small_all.at[me], send_sem=ssend.at[rel - 1], recv_sem=srecv.at[rel - 1],
                    device_id=peer, device_id_type=MESH))
                arrivals.append(pltpu.make_async_remote_copy(
                    src_ref=small_in, dst_ref=small_all.at[sender], send_sem=ssend.at[rel - 1], recv_sem=srecv.at[rel - 1],
                    device_id=peer, device_id_type=MESH))
        for cp in sends:
            cp.start()
        for cp in arrivals:
            cp.wait_recv()
        for cp in sends:
            cp.wait_send()

    anyspace = pl.BlockSpec(memory_space=pl.ANY)
    vm = pl.BlockSpec(memory_space=pltpu.VMEM)
    out_shape = [SDS((N_CHIPS, g.shape[1] // 2, g.shape[2]), F32) for g in to_sibling]
    out_shape += [SDS(full.shape, F32) for full in shards]
    if has_small:
        out_shape.append(SDS((N_DEV,) + smalls.shape, F32))
    n_d2d = max(n_g + n_h, 1)
    outs = pl.pallas_call(
        body, name=name, out_shape=out_shape,
        in_specs=[anyspace] * (n_g + n_h) + [vm] * has_small, out_specs=[anyspace] * (n_g + n_h) + [vm] * has_small,
        scratch_shapes=[pltpu.SemaphoreType.DMA((n_d2d,)), pltpu.SemaphoreType.DMA((n_d2d,)),
                        pltpu.SemaphoreType.DMA((N_DEV - 1,)), pltpu.SemaphoreType.DMA((N_DEV - 1,))],
        input_output_aliases={n_g + b: n_g + b for b in range(n_h)},
    )(*to_sibling, *shards, *([smalls] if has_small else []))
    return outs[:n_g], outs[n_g:n_g + n_h], (outs[n_g + n_h] if has_small else None)


def _add_sibling(where, g, r, name):
    _, rows, cols = g.shape
    h = rows // 2
    tr = min(h, 256)
    nh = h // tr

    def body(where_ref, g_ref, r_ref, t_ref, own_ref):
        t = g_ref[0] + r_ref[0]
        t_ref[0] = t.astype(BF16)

        @pl.when(pl.program_id(1) == where_ref[1])
        def _():
            own_ref[...] = t

    return pl.pallas_call(
        body, name=name,
        grid_spec=pltpu.PrefetchScalarGridSpec(
            num_scalar_prefetch=1, grid=(nh, N_CHIPS),
            in_specs=[pl.BlockSpec((1, tr, cols), lambda i, k, w: (k, w[0] * nh + i, 0)),
                      pl.BlockSpec((1, tr, cols), lambda i, k, w: (k, i, 0))],
            out_specs=[pl.BlockSpec((1, tr, cols), lambda i, k, w: (k, i, 0)),
                       pl.BlockSpec((tr, cols), lambda i, k, w: (i, 0))]),
        out_shape=[SDS((N_CHIPS, h, cols), BF16), SDS((h, cols), F32)],
        compiler_params=_params(("parallel", "arbitrary")),
    )(where, g, r)


def _add_chips(where, own, r, name):
    h, cols = own.shape
    tr = min(h, 256)
    nh = h // tr

    def body(where_ref, t_ref, r_ref, o_ref):
        del where_ref
        o_ref[...] = ((t_ref[...] + r_ref[0].astype(F32)) + r_ref[1].astype(F32)) + r_ref[2].astype(F32)

    return pl.pallas_call(
        body, name=name,
        grid_spec=pltpu.PrefetchScalarGridSpec(
            num_scalar_prefetch=1, grid=(nh,),
            in_specs=[pl.BlockSpec((tr, cols), lambda i, w: (i, 0)), pl.BlockSpec((3, tr, cols), lambda i, w: (0, i, 0))],
            out_specs=pl.BlockSpec((tr, cols), lambda i, w: (w[0] * nh + i, 0))),
        out_shape=SDS((2 * h, cols), F32),
        compiler_params=_params(("parallel",)),
    )(where, own, r)


def _sum_smalls(small_all):
    def body(all_ref, o_ref):
        acc = all_ref[0]
        for dev in range(1, N_DEV):
            acc = acc + all_ref[dev]
        o_ref[...] = acc

    return pl.pallas_call(
        body, name="sum_smalls", out_shape=SDS(small_all.shape[1:], F32),
        in_specs=[pl.BlockSpec(memory_space=pltpu.VMEM)], out_specs=pl.BlockSpec(memory_space=pltpu.VMEM),
    )(small_all)


def _adam_step(g, w, m, v):
    nm = ADAM_B1 * m + (1.0 - ADAM_B1) * g
    nv = ADAM_B2 * v + (1.0 - ADAM_B2) * (g * g)
    m_hat = nm / (1.0 - ADAM_B1 ** ADAM_STEP)
    v_hat = nv / (1.0 - ADAM_B2 ** ADAM_STEP)
    return -ADAM_LR * (m_hat / (jnp.sqrt(v_hat) + ADAM_EPS) + ADAM_WD * w), nm, nv


def _adamw(g, w, m, v, name):
    rows, cols = g.shape
    tr = min(rows, 256)

    def body(g_ref, w_ref, m_ref, v_ref, d_ref, nm_ref, nv_ref):
        d_ref[...], nm_ref[...], nv_ref[...] = _adam_step(g_ref[...], w_ref[...], m_ref[...], v_ref[...])

    spec = pl.BlockSpec((tr, cols), lambda i: (i, 0))
    return pl.pallas_call(
        body, name=name, grid=(rows // tr,), in_specs=[spec] * 4, out_specs=[spec] * 3,
        out_shape=[SDS(g.shape, F32)] * 3, compiler_params=_params(("parallel",)),
    )(g, w, m, v)


def _small_update(chip, tot, wmv):
    names = list(SMALL_PLACES)
    n = len(names)

    def body(chip_ref, tot_ref, quarter_ref, *refs):
        del chip_ref
        ins, outs = refs[:3 * n], refs[3 * n:]
        for i, nm in enumerate(names):
            sharded, row, (rows, cols) = SMALL_PLACES[nm]
            g = (quarter_ref if sharded else tot_ref)[row:row + rows, 0:cols]
            outs[4 * i][...] = g
            outs[4 * i + 1][...], outs[4 * i + 2][...], outs[4 * i + 3][...] = _adam_step(
                g, ins[3 * i][...], ins[3 * i + 1][...], ins[3 * i + 2][...])

    whole = lambda shape: pl.BlockSpec(shape, lambda i, c: (0,) * len(shape))
    shapes = [SMALL_PLACES[nm][2] for nm in names]
    outs = pl.pallas_call(
        body, name="small_update",
        grid_spec=pltpu.PrefetchScalarGridSpec(
            num_scalar_prefetch=1, grid=(1,),
            in_specs=[whole(tot.shape), pl.BlockSpec((tot.shape[0], D // 4), lambda i, c: (0, c[0]))]
            + [whole(shp) for shp in shapes for _ in range(3)],
            out_specs=[whole(shp) for shp in shapes for _ in range(4)]),
        out_shape=[SDS(shp, F32) for shp in shapes for _ in range(4)],
    )(chip, tot, tot, *[a for nm in names for a in wmv[nm]])
    return {nm: tuple(outs[4 * i:4 * i + 4]) for i, nm in enumerate(names)}


def _pad_rows(a, rows):
    return jnp.concatenate([a, jnp.zeros((rows - a.shape[0], a.shape[1]), a.dtype)], axis=0)


def _pad_cols(a, cols):
    return jnp.concatenate([a, jnp.zeros((a.shape[0], cols - a.shape[1]), a.dtype)], axis=1)


def kernel(x, a_pre_norm, a_w_in, a_conv_w, a_w_out, a_post_norm, kv_norm, w_kv, rel_bias, b_pre_norm, b_w_in, b_sinks, b_w_out, b_post_norm, loss_target, m_a_pre_norm, m_a_w_in, m_a_conv_w, m_a_w_out, m_a_post_norm, m_kv_norm, m_w_kv, m_rel_bias, m_b_pre_norm, m_b_w_in, m_b_sinks, m_b_w_out, m_b_post_norm, v_a_pre_norm, v_a_w_in, v_a_conv_w, v_a_w_out, v_a_post_norm, v_kv_norm, v_w_kv, v_rel_bias, v_b_pre_norm, v_b_w_in, v_b_sinks, v_b_w_out, v_b_post_norm):
    seq = x.shape[1]
    xs = x.reshape(seq, D)
    tgt = loss_target.reshape(seq, D)
    chip = 2 * lax.axis_index("x") + lax.axis_index("y")
    core = lax.axis_index("c")
    tm = _tile(seq, 512)
    tm_mix = _tile(seq, 256)
    tmw = _tile(seq, 1024)

    shards = [a_w_in[0], a_w_out[0], w_kv, b_w_in[0], b_w_out[0]]
    small_w = _pad_rows(jnp.concatenate([a_pre_norm, a_conv_w[0], a_post_norm], axis=0), 8)
    *own_only, small_g = _gather_weights(shards, small_w, 0)
    where = jnp.stack([core, chip]).astype(jnp.int32)
    small_full = small_g.transpose(1, 0, 2).reshape(8, D)
    g_apre, conv_w, g_apost = small_full[0:1], _pad_rows(small_full[1:4], 8), small_full[4:5]
    g_kv = kv_norm.reshape(1, D)

    proj, n1, (win_g, wouta_g, wkv_g, wbin_g, woutb_g) = _a_in(where[1:2], xs, g_apre, own_only, tm)
    wouta = wouta_g.reshape(D, D)
    wkv = wkv_g.reshape(D, 2 * KV_W)
    woutb = woutb_g.reshape(D, D)
    ya, oa, h1 = _a_mix(proj, xs, conv_w, wouta, g_apost, tm_mix)
    kv, q, zb = _b_in(h1, g_kv, b_pre_norm, wkv, wbin_g, tm)
    tab = _bias_table(rel_bias, b_sinks.reshape(N_HEADS))
    att, stats = _attn_fwd(q, kv, tab)
    dh2, dqz, datt, loss_acc, dg_bpost, dw_outb = _mid(att, zb, h1, tgt, woutb, b_post_norm, tm)

    dqz, dkv, dtab = _attn_bwd(q, kv, datt, stats, tab, dqz)
    dh1, doa, dg_b, dw_bin, dw_kv = _b_bwd(dqz, dkv, h1, dh2, oa, wbin_g, wkv, g_kv, b_pre_norm, g_apost, tm)
    dw_outa = _dw(ya, doa, D, tmw, "dw_a_out").reshape(N_CHIPS, D // 4, D)
    dw_kv = dw_kv.reshape(N_CHIPS, D // 4, 2 * KV_W)
    dw_outb = dw_outb.reshape(N_CHIPS, D // 4, D)
    grads1 = [dw_outa, dw_kv, dw_bin, dw_outb]
    names1 = ["a_w_out", "w_kv", "b_w_in", "b_w_out"]
    from_sibling1, _, _ = _sibling_exchange("to_sibling_1", to_sibling=grads1)
    sums1 = [_add_sibling(where, g, r, "add_sibling_" + nm) for g, r, nm in zip(grads1, from_sibling1, names1)]
    dproj, dconv_w, from_chips1 = _a_bwd(doa, proj, conv_w, wouta, tm_mix, [t for t, _ in sums1])
    shards1 = [_add_chips(where, own, r, "add_chips_" + nm) for (_, own), r, nm in zip(sums1, from_chips1, names1)]
    dw_in = _dw(n1, dproj, D, tmw, "dw_a_in")
    from_sibling2, (g_wouta, g_wkv, g_wbin, g_woutb), _ = _sibling_exchange(
        "to_sibling_2", to_sibling=[dw_in], shards=shards1)
    part2, own2 = _add_sibling(where, dw_in, from_sibling2[0], "add_sibling_a_w_in")
    nt = seq // tm
    dn_first, from_chips2 = _a_in_bwd_matmul(dproj, win_g, tm, max(nt - max(nt // 4, 1), 1), [part2])
    grad_x, dg_apre = _a_in_bwd(dn_first, dproj, xs, dh1, win_g, g_apre, tm)
    shard2 = _add_chips(where, own2, from_chips2[0], "add_chips_a_w_in")
    drel, dsink = _bias_fold(dtab)

    smalls = jnp.concatenate([
        dg_apre[0:1], dg_b[2:3], dg_b[0:1], dg_b[1:2], dg_bpost[0:1], _pad_cols(dsink[0:1], D),
        _pad_cols(loss_acc[0:1], D), jnp.zeros((1, D), F32), dconv_w, _pad_cols(drel, D)], axis=0)
    _, (g_win,), small_all = _sibling_exchange("share_last", shards=[shard2], smalls=smalls)
    tot = _sum_smalls(small_all)

    big = {}
    for nm, g, w, m, v in [("a_w_in", g_win, a_w_in, m_a_w_in, v_a_w_in), ("a_w_out", g_wouta, a_w_out, m_a_w_out, v_a_w_out),
                           ("w_kv", g_wkv, w_kv, m_w_kv, v_w_kv), ("b_w_in", g_wbin, b_w_in, m_b_w_in, v_b_w_in),
                           ("b_w_out", g_woutb, b_w_out, m_b_w_out, v_b_w_out)]:
        shp = w.shape
        two = (shp[-2], shp[-1])
        d, nm_, nv_ = _adamw(g, w.reshape(two), m.reshape(two), v.reshape(two), "adamw_" + nm)
        big[nm] = (g.reshape(shp), d.reshape(shp), nm_.reshape(shp), nv_.reshape(shp))

    given = {"a_pre_norm": (a_pre_norm, m_a_pre_norm, v_a_pre_norm), "a_conv_w": (a_conv_w, m_a_conv_w, v_a_conv_w),
             "a_post_norm": (a_post_norm, m_a_post_norm, v_a_post_norm), "kv_norm": (kv_norm, m_kv_norm, v_kv_norm),
             "rel_bias": (rel_bias, m_rel_bias, v_rel_bias), "b_pre_norm": (b_pre_norm, m_b_pre_norm, v_b_pre_norm),
             "b_sinks": (b_sinks, m_b_sinks, v_b_sinks), "b_post_norm": (b_post_norm, m_b_post_norm, v_b_post_norm)}
    small = _small_update(where[1:2], tot, {nm: tuple(a.reshape(SMALL_PLACES[nm][2]) for a in wmv)
                                            for nm, wmv in given.items()})
    order = ["a_pre_norm", "a_w_in", "a_conv_w", "a_w_out", "a_post_norm", "kv_norm", "w_kv", "rel_bias",
             "b_pre_norm", "b_w_in", "b_sinks", "b_w_out", "b_post_norm"]
    outs = []
    for which in range(4):
        for nm in order:
            outs.append(big[nm][which] if nm in big else small[nm][which].reshape(given[nm][0].shape))
    loss = 0.5 * tot[LOSS_ROW, 0]
    return (loss, grad_x.reshape(x.shape), *outs)
```

```python
import functools
import math

import jax
import jax.numpy as jnp
from jax import lax
from jax.experimental import pallas as pl
from jax.experimental.pallas import tpu as pltpu

F32 = jnp.float32
BF16 = jnp.bfloat16
MESH = pl.DeviceIdType.MESH
SDS = jax.ShapeDtypeStruct

D = 1024
HEAD_DIM = 64
N_HEADS = 16
N_KV = 2
GROUP = 8
KV_W = 128
BLK = 128
N_BUCKETS = 32
MAX_EXACT = 16
MAX_DISTANCE = 128
EPS = 1e-6
NEG_INF = -1e30
Q_SCALE = HEAD_DIM ** -0.5

ADAM_LR = 0.001
ADAM_B1 = 0.9
ADAM_B2 = 0.999
ADAM_EPS = 1e-08
ADAM_WD = 0.01
ADAM_STEP = 10

N_CHIPS = 4
N_DEV = 8
VMEM_LIMIT = 56 * 1024 * 1024
SMALL_ROWS = 48
LOSS_ROW = 6
SMALL_PLACES = {
    "a_pre_norm": (True, 0, (1, D // 4)), "a_conv_w": (True, 8, (3, D // 4)), "a_post_norm": (True, 1, (1, D // 4)),
    "kv_norm": (False, 2, (1, D)), "rel_bias": (False, 16, (N_BUCKETS, N_HEADS)), "b_pre_norm": (False, 3, (1, D)),
    "b_sinks": (False, 5, (1, N_HEADS)), "b_post_norm": (False, 4, (1, D)),
}
HALO = 16


def _bucket_thresholds():
    def bucket(d):
        big = MAX_EXACT + int(math.log(d / MAX_EXACT) / math.log(MAX_DISTANCE / MAX_EXACT)
                              * (N_BUCKETS - MAX_EXACT))
        return d if d < MAX_EXACT else min(big, N_BUCKETS - 1)
    out = []
    for b in range(MAX_EXACT + 1, N_BUCKETS):
        out.append(min(d for d in range(MAX_EXACT, MAX_DISTANCE) if bucket(d) >= b))
    return tuple(out)


BUCKET_THRESHOLDS = _bucket_thresholds()


def _params(semantics=None, vmem=VMEM_LIMIT):
    return pltpu.CompilerParams(dimension_semantics=semantics, vmem_limit_bytes=vmem)


def _tile(n, pref):
    return pref if n >= 2 * pref else max(n // 2, 8)


def _rms_scale(v):
    return lax.rsqrt(jnp.mean(v * v, axis=-1, keepdims=True) + EPS)


def _nt(a, b):
    return lax.dot_general(a, b, (((1,), (1,)), ((), ())), preferred_element_type=F32)


def _tn(a, b):
    return lax.dot_general(a, b, (((0,), (0,)), ((), ())), preferred_element_type=F32)


def _nn(a, b):
    return jnp.dot(a, b, preferred_element_type=F32)


def _silu_parts(z):
    sg = jax.nn.sigmoid(z)
    return sg, z * sg


def _dsilu(z, sg):
    return sg * (1.0 + z * (1.0 - sg))


def _acc_row(ref, row, val):
    ref[row:row + 1, :] += val


def _gather_copies(outs, splits, ici_send, ici_recv, d2d_send, d2d_recv):
    x, y, c = lax.axis_index("x"), lax.axis_index("y"), lax.axis_index("c")
    k = 2 * x + y
    sibling = (x, y, 1 - c)

    def part(o_ref, chip, core, split):
        if not split:
            return o_ref.at[chip]
        h = o_ref.shape[1] // 2
        return o_ref.at[chip, pl.ds(pl.multiple_of(core * h, 16), h)]

    def remote(ref, a, j, sems, to):
        return pltpu.make_async_remote_copy(src_ref=ref, dst_ref=ref, send_sem=sems[0].at[3 * a + j],
                                            recv_sem=sems[1].at[3 * a + j], device_id=to, device_id_type=MESH)

    copies = []
    for a, (o_ref, split) in enumerate(zip(outs, splits)):
        for j, (px, py) in enumerate([(x, 1 - y), (1 - x, y), (1 - x, 1 - y)]):
            kj = 2 * px + py
            ici, d2d = (ici_send, ici_recv), (d2d_send, d2d_recv)
            copies.append((remote(part(o_ref, k, c, split), a, j, ici, (px, py, c)),
                           remote(part(o_ref, kj, c, split), a, j, ici, (px, py, c)),
                           remote(part(o_ref, kj, c, split), a, j, d2d, sibling) if split else None,
                           remote(part(o_ref, kj, 1 - c, split), a, j, d2d, sibling) if split else None))
    return copies


def _gather_sems(n):
    return [pltpu.SemaphoreType.DMA((3 * n,)) for _ in range(4)]


def _gather_weights(shards, small, n_now):
    n = len(shards)

    def body(*refs):
        ins, small_in = refs[:n], refs[n]
        outs, small_out = refs[n + 1:2 * n + 1], refs[2 * n + 1]
        sems = refs[2 * n + 2:]
        k = 2 * lax.axis_index("x") + lax.axis_index("y")
        for i_ref, o_ref in zip(ins, outs):
            o_ref[k] = i_ref[...].astype(BF16)
        small_out[k] = small_in[...]
        copies = _gather_copies(list(outs[:n_now]) + [small_out], [True] * n_now + [False], *sems)
        for send, _, _, _ in copies:
            send.start()
        for _, arrival, forward, _ in copies:
            arrival.wait_recv()
            if forward is not None:
                forward.start()
        for send, _, forward, forwarded in copies:
            if forward is not None:
                forwarded.wait_recv()
                forward.wait_send()
            send.wait_send()

    vm = pl.BlockSpec(memory_space=pltpu.VMEM)
    out_shape = [SDS((N_CHIPS,) + s.shape, BF16) for s in shards] + [SDS((N_CHIPS,) + small.shape, F32)]
    return pl.pallas_call(
        body, name="gather_weights", out_shape=out_shape,
        in_specs=[vm] * (n + 1), out_specs=[vm] * (n + 1),
        scratch_shapes=_gather_sems(n_now + 1),
        compiler_params=pltpu.CompilerParams(vmem_limit_bytes=VMEM_LIMIT),
    )(*shards, small)


def _a_in(chip, x, g_pre, weights, tm):
    s = x.shape[0]
    nt = s // tm
    n = len(weights)

    def body(chip_ref, x_ref, g_ref, *refs):
        proj_ref, n1_ref = refs[n:n + 2]
        gathered = refs[n + 2:2 * n + 2]
        wbuf, n1_all, fetch_sem = refs[2 * n + 2:2 * n + 5]
        sems = refs[2 * n + 5:]
        jj, i = pl.program_id(0), pl.program_id(1)
        copies = _gather_copies(gathered, [True] * n, *sems)

        def fetch(rel):
            slot = jnp.bitwise_xor(chip_ref[0], rel)
            return pltpu.make_async_copy(gathered[0].at[slot], wbuf.at[rel % 2], fetch_sem.at[rel % 2])

        @pl.when((jj == 0) & (i == 0))
        def _():
            fetch(0).start()
            copies[0][0].start()
            copies[1][0].start()
            fetch(0).wait()

        for rel in (1, 2, 3):
            @pl.when((jj == rel) & (i == 0))
            def _():
                fetch(rel).wait()

        @pl.when(jj == 0)
        def _():
            xv = x_ref[...]
            n1 = (xv * _rms_scale(xv) * g_ref[...]).astype(BF16)
            n1_ref[...] = n1
            n1_all[i] = n1
        proj_ref[...] = _nn(n1_all[i], wbuf[jj % 2]).astype(BF16)

        for rel in (1, 2, 3):
            @pl.when((jj == rel - 1) & (i == max(nt - 3, nt // 2)))
            def _():
                _, arrival, forward, forwarded = copies[rel - 1]
                arrival.wait_recv()
                forward.start()
                forwarded.wait_recv()
                fetch(rel).start()
                if rel == 1:
                    copies[2][0].start()
                if rel == 2:
                    for send, _, _, _ in copies[3:]:
                        send.start()

        @pl.when((jj == 3) & (i == max(nt - 2, 0)))
        def _():
            for _, arrival, forward, _ in copies[3:]:
                arrival.wait_recv()
                forward.start()

        @pl.when((jj == 3) & (i == nt - 1))
        def _():
            for _, _, _, forwarded in copies[3:]:
                forwarded.wait_recv()
            for send, _, forward, _ in copies:
                forward.wait_send()
                send.wait_send()

    anyspace = pl.BlockSpec(memory_space=pl.ANY)
    proj, n1, *gathered = pl.pallas_call(
        body, name="a_in",
        grid_spec=pltpu.PrefetchScalarGridSpec(
            num_scalar_prefetch=1, grid=(4, nt),
            in_specs=[pl.BlockSpec((tm, D), lambda jj, i, c: (jnp.where(jj == 0, i, nt - 1), 0)),
                      pl.BlockSpec((1, D), lambda jj, i, c: (0, 0))] + [anyspace] * n,
            out_specs=[pl.BlockSpec((tm, D), lambda jj, i, c: (i, jnp.bitwise_xor(c[0], jj))),
                       pl.BlockSpec((tm, D), lambda jj, i, c: (jnp.where(jj == 0, i, nt - 1), 0))] + [anyspace] * n,
            scratch_shapes=[pltpu.VMEM((2, D, D), BF16), pltpu.VMEM((nt, tm, D), BF16),
                            pltpu.SemaphoreType.DMA((2,))] + _gather_sems(n)),
        out_shape=[SDS((s, 4 * D), BF16), SDS((s, D), BF16)] + [SDS(w.shape, w.dtype) for w in weights],
        input_output_aliases={3 + a: 2 + a for a in range(n)},
        compiler_params=_params(("arbitrary", "arbitrary")),
    )(chip, x, g_pre, *weights)
    return proj, n1, gathered


def _shift_rows(v, last, second_last, rows):
    v1 = jnp.where(rows >= 1, pltpu.roll(v, 1, 0), last)
    v2 = jnp.where(rows >= 2, pltpu.roll(v, 2, 0), jnp.where(rows == 1, last, second_last))
    return v1, v2


def _a_mix(proj, x, conv_w, w_out, g_post, tm):
    s = x.shape[0]

    def body(proj_ref, x_ref, cw_ref, w_ref, g_ref, ya_ref, oa_ref, h1_ref, carry):
        @pl.when(pl.program_id(0) == 0)
        def _():
            carry[...] = jnp.zeros_like(carry)
        v = proj_ref[:, D:2 * D].astype(F32) * proj_ref[:, 2 * D:3 * D].astype(F32)
        rows = lax.broadcasted_iota(jnp.int32, (tm, D), 0)
        before = carry[...]
        v1, v2 = _shift_rows(v, before[7:8, :], before[6:7, :], rows)
        carry[...] = v[tm - 8:tm, :]
        conv = cw_ref[0:1, :] * v2 + cw_ref[1:2, :] * v1 + cw_ref[2:3, :] * v
        _, sz = _silu_parts(proj_ref[:, 3 * D:4 * D].astype(F32))
        ya = (proj_ref[:, 0:D].astype(F32) * conv * sz).astype(BF16)
        ya_ref[...] = ya
        oa = _nn(ya, w_ref[...])
        oa_ref[...] = oa.astype(BF16)
        h1_ref[...] = x_ref[...] + oa * _rms_scale(oa) * g_ref[...]

    row = lambda i: (i, 0)
    fix = lambda i: (0, 0)
    return pl.pallas_call(
        body, name="a_mix", grid=(s // tm,),
        in_specs=[pl.BlockSpec((tm, 4 * D), row), pl.BlockSpec((tm, D), row), pl.BlockSpec((8, D), fix),
                  pl.BlockSpec((D, D), fix), pl.BlockSpec((1, D), fix)],
        out_specs=[pl.BlockSpec((tm, D), row)] * 3,
        out_shape=[SDS((s, D), BF16), SDS((s, D), BF16), SDS((s, D), F32)],
        scratch_shapes=[pltpu.VMEM((8, D), F32)],
        compiler_params=_params(("arbitrary",)),
    )(proj, x, conv_w, w_out, g_post)


def _b_in(h1, g_kv, g_pre, w_kv, wbin_g, tm):
    s = h1.shape[0]

    def body(h_ref, gk_ref, gb_ref, wkv_ref, wb_ref, kv_ref, q_ref, z_ref):
        h = h_ref[...]
        hh = h * _rms_scale(h)
        nk = (hh * gk_ref[...]).astype(BF16)
        nb = (hh * gb_ref[...]).astype(BF16)
        kv_ref[...] = _nn(nk, wkv_ref[...]).astype(BF16)
        for j in range(2):
            q_ref[:, 512 * j:512 * (j + 1)] = (_nn(nb, wb_ref[j]) * Q_SCALE).astype(BF16)
            z_ref[:, 512 * j:512 * (j + 1)] = _nn(nb, wb_ref[2 + j]).astype(BF16)

    row = lambda i: (i, 0)
    fix = lambda i: (0, 0)
    return pl.pallas_call(
        body, name="b_in", grid=(s // tm,),
        in_specs=[pl.BlockSpec((tm, D), row), pl.BlockSpec((1, D), fix), pl.BlockSpec((1, D), fix),
                  pl.BlockSpec((D, 2 * KV_W), fix), pl.BlockSpec((4, D, 512), lambda i: (0, 0, 0))],
        out_specs=[pl.BlockSpec((tm, 2 * KV_W), row), pl.BlockSpec((tm, D), row), pl.BlockSpec((tm, D), row)],
        out_shape=[SDS((s, 2 * KV_W), BF16), SDS((s, D), BF16), SDS((s, D), BF16)],
        compiler_params=_params(("parallel",)),
    )(h1, g_kv, g_pre, w_kv, wbin_g)


def _band_buckets():
    q = lax.broadcasted_iota(jnp.int32, (BLK, 2 * BLK), 0)
    k = lax.broadcasted_iota(jnp.int32, (BLK, 2 * BLK), 1)
    dist = q + BLK - k
    bucket = jnp.where(dist < MAX_EXACT, dist, MAX_EXACT)
    for t in BUCKET_THRESHOLDS:
        bucket = bucket + jnp.where(dist >= t, 1, 0)
    in_window = (dist >= 0) & (dist < BLK)
    return jnp.where(in_window, bucket, -1)


def _head_place(h):
    kh, j, e = h // GROUP, (h % GROUP) // 2, h % 2
    return kh, slice(BLK * j, BLK * (j + 1)), slice(2 * BLK * e, 2 * BLK * (e + 1))


def _bias_table(rel_bias, sinks):
    def body(rb_ref, sink_ref, tab_ref):
        bucket = _band_buckets()
        col = lax.broadcasted_iota(jnp.int32, (BLK, 2 * BLK), 1)
        for h in range(N_HEADS):
            acc = jnp.where(bucket < 0, NEG_INF, 0.0).astype(F32)
            for b in range(N_BUCKETS):
                acc = jnp.where(bucket == b, rb_ref[b, h], acc)
            acc = jnp.where(col == 0, sink_ref[h], acc)
            kh, rows, cols = _head_place(h)
            tab_ref[1, kh, rows, cols] = acc
            tab_ref[0, kh, rows, cols] = jnp.where((col > 0) & (col < BLK), NEG_INF, acc)

    return pl.pallas_call(
        body, name="bias_table", out_shape=SDS((2, N_KV, 4 * BLK, 4 * BLK), F32),
        in_specs=[pl.BlockSpec(memory_space=pltpu.SMEM), pl.BlockSpec(memory_space=pltpu.SMEM)],
        out_specs=pl.BlockSpec(memory_space=pltpu.VMEM),
    )(rel_bias, sinks)


def _bias_fold(dtab):
    def body(dtab_ref, out_ref, dsink_ref):
        bucket = _band_buckets()
        row = lax.broadcasted_iota(jnp.int32, (N_BUCKETS, 128), 0)
        lane = lax.broadcasted_iota(jnp.int32, (N_BUCKETS, 128), 1)
        row8 = lax.broadcasted_iota(jnp.int32, (8, 128), 0)
        lane8 = lax.broadcasted_iota(jnp.int32, (8, 128), 1)
        acc = jnp.zeros((N_BUCKETS, 128), F32)
        dsink = jnp.zeros((8, 128), F32)
        for h in range(N_HEADS):
            kh, rows, cols = _head_place(h)
            dt = dtab_ref[kh, rows, cols]
            for b in range(N_BUCKETS):
                val = jnp.sum(jnp.where(bucket == b, dt, 0.0))
                acc = acc + jnp.where((row == b) & (lane == h), val, 0.0)
            dsink = dsink + jnp.where((row8 == 0) & (lane8 == h), jnp.sum(dt[:, 0:1]), 0.0)
        out_ref[...] = acc
        dsink_ref[...] = dsink

    vm = pl.BlockSpec(memory_space=pltpu.VMEM)
    return pl.pallas_call(
        body, name="bias_fold", out_shape=[SDS((N_BUCKETS, 128), F32), SDS((8, 128), F32)],
        in_specs=[vm], out_specs=[vm, vm],
    )(dtab)


def _pair_operands(prev, cur):
    t = jnp.concatenate([prev, cur], axis=0).astype(F32)
    t = jnp.where(lax.broadcasted_iota(jnp.int32, t.shape, 0) == 0, 0.0, t)
    tr = pltpu.roll(t, HEAD_DIM, 1)
    lo = lax.broadcasted_iota(jnp.int32, t.shape, 1) < HEAD_DIM
    zero = jnp.zeros_like(t)
    head0 = jnp.concatenate([jnp.where(lo, t, zero), jnp.where(lo, zero, tr)], axis=0).astype(BF16)
    head1 = jnp.concatenate([jnp.where(lo, tr, zero), jnp.where(lo, zero, t)], axis=0).astype(BF16)
    return head0, head1


def _pair_fold(d0, d1):
    lo = lax.broadcasted_iota(jnp.int32, (2 * BLK, KV_W), 1) < HEAD_DIM
    zero = jnp.zeros((2 * BLK, KV_W), F32)
    g0 = jnp.where(lo, d0[0:256], zero) + pltpu.roll(jnp.where(lo, zero, d0[256:512]), HEAD_DIM, 1)
    g1 = pltpu.roll(jnp.where(lo, d1[0:256], zero), HEAD_DIM, 1) + jnp.where(lo, zero, d1[256:512])
    return jnp.where(lax.broadcasted_iota(jnp.int32, (2 * BLK, KV_W), 0) == 0, 0.0, g0 + g1)


def _stack_pairs(ref, kh):
    return jnp.concatenate([ref[:, 128 * (4 * kh + j):128 * (4 * kh + j + 1)] for j in range(4)], axis=0)


def _table_spec():
    return pl.BlockSpec((1, N_KV, 4 * BLK, 4 * BLK), lambda n: (jnp.minimum(n, 1), 0, 0, 0))


def _attn_fwd(q, kv, tab):
    s = q.shape[0]

    def body(q_ref, kp_ref, kc_ref, vp_ref, vc_ref, tab_ref, att_ref, stats_ref):
        k2 = _pair_operands(kp_ref[...], kc_ref[...])
        v2 = _pair_operands(vp_ref[...], vc_ref[...])
        lane = lax.broadcasted_iota(jnp.int32, (BLK, 128), 1)
        stats = jnp.zeros((BLK, 128), F32)
        for kh in range(N_KV):
            sc = _nt(_stack_pairs(q_ref, kh), k2[kh])
            ps = []
            for e in range(2):
                lg = sc[:, 256 * e:256 * (e + 1)] + tab_ref[0, kh, :, 256 * e:256 * (e + 1)]
                m = jnp.max(lg, axis=-1, keepdims=True)
                ex = jnp.exp(lg - m)
                den = jnp.sum(ex, axis=-1, keepdims=True)
                ps.append(ex * (1.0 / den))
                lse = m + jnp.log(den)
                for j in range(4):
                    stats = jnp.where(lane == GROUP * kh + 2 * j + e, lse[BLK * j:BLK * (j + 1)], stats)
            out = _nn(jnp.concatenate(ps, axis=1).astype(BF16), v2[kh])
            for j in range(4):
                att_ref[:, 128 * (4 * kh + j):128 * (4 * kh + j + 1)] = out[BLK * j:BLK * (j + 1)].astype(BF16)
        stats_ref[...] = stats

    cur = lambda n: (n, 0)
    prev = lambda n: (jnp.maximum(n - 1, 0), 0)
    return pl.pallas_call(
        body, name="attn_fwd", grid=(s // BLK,),
        in_specs=[pl.BlockSpec((BLK, D), cur),
                  pl.BlockSpec((BLK, KV_W), prev), pl.BlockSpec((BLK, KV_W), cur),
                  pl.BlockSpec((BLK, KV_W), lambda n: (jnp.maximum(n - 1, 0), 1)),
                  pl.BlockSpec((BLK, KV_W), lambda n: (n, 1)), _table_spec()],
        out_specs=[pl.BlockSpec((BLK, D), cur), pl.BlockSpec((BLK, 128), cur)],
        out_shape=[SDS((s, D), BF16), SDS((s, 128), F32)],
        compiler_params=_params(("parallel",)),
    )(q, kv, kv, kv, kv, tab)


def _mid(att, zb, h1, tgt, w_out, g_post, tm):
    s = att.shape[0]
    nt = s // tm

    def body(att_ref, z_ref, h1_ref, t_ref, w_ref, g_ref,
             dh_ref, dqz_ref, datt_ref, loss_ref, dg_ref, dw_ref, dw_acc):
        @pl.when(pl.program_id(0) == 0)
        def _():
            loss_ref[...] = jnp.zeros_like(loss_ref)
            dg_ref[...] = jnp.zeros_like(dg_ref)
            dw_acc[...] = jnp.zeros_like(dw_acc)
        att = att_ref[...].astype(F32)
        z = z_ref[...].astype(F32)
        sg, sz = _silu_parts(z)
        ob = (att * sz).astype(BF16)
        y2 = _nn(ob, w_ref[...])
        r2 = _rms_scale(y2)
        yh = y2 * r2
        g = g_ref[...]
        err = (h1_ref[...] + yh * g) - t_ref[...]
        loss_ref[...] += jnp.sum(jnp.sum(err * err, axis=-1, keepdims=True) / D)
        dh = err / D
        dh_ref[...] = dh
        _acc_row(dg_ref, 0, jnp.sum(dh * yh, axis=0, keepdims=True))
        dyh = dh * g
        dy = (r2 * (dyh - yh * jnp.mean(dyh * yh, axis=-1, keepdims=True))).astype(BF16)
        dw_acc[...] += _tn(ob, dy)
        dob = _nt(dy, w_ref[...])
        datt_ref[...] = (dob * sz).astype(BF16)
        dqz_ref[...] = (dob * att * _dsilu(z, sg)).astype(BF16)

        @pl.when(pl.program_id(0) == nt - 1)
        def _():
            pltpu.sync_copy(dw_acc, dw_ref)

    row = lambda i: (i, 0)
    fix = lambda i: (0, 0)
    return pl.pallas_call(
        body, name="mid", grid=(nt,),
        in_specs=[pl.BlockSpec((tm, D), row)] * 4 + [pl.BlockSpec((D, D), fix), pl.BlockSpec((1, D), fix)],
        out_specs=[pl.BlockSpec((tm, D), row), pl.BlockSpec((tm, D), lambda i: (i, 1)), pl.BlockSpec((tm, D), row),
                   pl.BlockSpec((8, 128), fix), pl.BlockSpec((8, D), fix), pl.BlockSpec(memory_space=pl.ANY)],
        out_shape=[SDS((s, D), F32), SDS((s, 2 * D), BF16), SDS((s, D), BF16), SDS((8, 128), F32),
                   SDS((8, D), F32), SDS((D, D), F32)],
        scratch_shapes=[pltpu.VMEM((D, D), F32)],
        compiler_params=_params(("arbitrary",)),
    )(att, zb, h1, tgt, w_out, g_post)


def _attn_bwd(q, kv, datt, stats, tab, dqz):
    s = q.shape[0]
    nb = s // BLK

    def body(q_ref, kp_ref, kc_ref, vp_ref, vc_ref, da_ref, st_ref, tab_ref, dqz_in,
             dq_ref, dkv_ref, dtab_ref, dk_carry, dv_carry):
        del dqz_in
        n = pl.program_id(0)

        @pl.when(n == 0)
        def _():
            dtab_ref[...] = jnp.zeros_like(dtab_ref)
            dk_carry[...] = jnp.zeros_like(dk_carry)
            dv_carry[...] = jnp.zeros_like(dv_carry)

        @pl.when(n < nb)
        def _():
            k2 = _pair_operands(kp_ref[...], kc_ref[...])
            v2 = _pair_operands(vp_ref[...], vc_ref[...])
            lane = lax.broadcasted_iota(jnp.int32, (BLK, 128), 1)
            stats = st_ref[...]
            dk2, dv2 = [], []
            for kh in range(N_KV):
                qs = _stack_pairs(q_ref, kh)
                das = _stack_pairs(da_ref, kh)
                sc = _nt(qs, k2[kh])
                dp = _nt(das, v2[kh])
                ps, dss = [], []
                for e in range(2):
                    heads = [GROUP * kh + 2 * j + e for j in range(4)]
                    lse = jnp.concatenate([jnp.sum(jnp.where(lane == h, stats, 0.0), axis=-1, keepdims=True)
                                           for h in heads], axis=0)
                    cols = slice(256 * e, 256 * (e + 1))
                    p = jnp.exp(sc[:, cols] + tab_ref[0, kh, :, cols] - lse)
                    delta = jnp.sum(p * dp[:, cols], axis=-1, keepdims=True)
                    ds = p * (dp[:, cols] - delta)
                    dtab_ref[kh, :, cols] += ds
                    ps.append(p)
                    dss.append(ds)
                p2 = jnp.concatenate(ps, axis=1).astype(BF16)
                ds2 = jnp.concatenate(dss, axis=1).astype(BF16)
                dq = _nn(ds2, k2[kh]) * Q_SCALE
                for j in range(4):
                    dq_ref[:, 128 * (4 * kh + j):128 * (4 * kh + j + 1)] = dq[BLK * j:BLK * (j + 1)].astype(BF16)
                dk2.append(_tn(ds2, qs))
                dv2.append(_tn(p2, das))
            dkk = _pair_fold(dk2[0], dk2[1])
            dvv = _pair_fold(dv2[0], dv2[1])
            dkv_ref[:, 0:KV_W] = (dk_carry[...] + dkk[0:BLK]).astype(BF16)
            dkv_ref[:, KV_W:2 * KV_W] = (dv_carry[...] + dvv[0:BLK]).astype(BF16)
            dk_carry[...] = dkk[BLK:2 * BLK]
            dv_carry[...] = dvv[BLK:2 * BLK]

        @pl.when(n == nb)
        def _():
            dkv_ref[:, 0:KV_W] = dk_carry[...].astype(BF16)
            dkv_ref[:, KV_W:2 * KV_W] = dv_carry[...].astype(BF16)

    cur = lambda n: (jnp.minimum(n, nb - 1), 0)
    prev = lambda n: (jnp.clip(n - 1, 0, nb - 1), 0)
    return pl.pallas_call(
        body, name="attn_bwd", grid=(nb + 1,),
        in_specs=[pl.BlockSpec((BLK, D), cur),
                  pl.BlockSpec((BLK, KV_W), prev), pl.BlockSpec((BLK, KV_W), cur),
                  pl.BlockSpec((BLK, KV_W), lambda n: (jnp.clip(n - 1, 0, nb - 1), 1)),
                  pl.BlockSpec((BLK, KV_W), lambda n: (jnp.minimum(n, nb - 1), 1)),
                  pl.BlockSpec((BLK, D), cur), pl.BlockSpec((BLK, 128), cur), _table_spec(),
                  pl.BlockSpec(memory_space=pl.ANY)],
        out_specs=[pl.BlockSpec((BLK, D), cur), pl.BlockSpec((BLK, 2 * KV_W), prev),
                   pl.BlockSpec((N_KV, 4 * BLK, 4 * BLK), lambda n: (0, 0, 0))],
        out_shape=[SDS((s, 2 * D), BF16), SDS((s, 2 * KV_W), BF16), SDS((N_KV, 4 * BLK, 4 * BLK), F32)],
        scratch_shapes=[pltpu.VMEM((BLK, KV_W), F32), pltpu.VMEM((BLK, KV_W), F32)],
        input_output_aliases={8: 0},
        compiler_params=_params(("arbitrary",)),
    )(q, kv, kv, kv, kv, datt, stats, tab, dqz)


def _b_bwd(dqz, dkv, h1, dh2, oa, wbin_g, w_kv, g_kv, g_pre, g_apost, tm):
    s = h1.shape[0]
    nt = s // tm

    def body(dqz_ref, dkv_ref, h_ref, dh2_ref, oa_ref, wb_ref, wkv_ref, gk_ref, gb_ref, ga_ref,
             dh1_ref, doa_ref, dg_ref, dwb_ref, dwkv_ref, dwb_acc, dwkv_acc):
        @pl.when(pl.program_id(0) == 0)
        def _():
            dg_ref[...] = jnp.zeros_like(dg_ref)
            dwb_acc[...] = jnp.zeros_like(dwb_acc)
            dwkv_acc[...] = jnp.zeros_like(dwkv_acc)
        dnb = _nt(dqz_ref[:, 0:512], wb_ref[0])
        for j in range(1, 4):
            dnb = dnb + _nt(dqz_ref[:, 512 * j:512 * (j + 1)], wb_ref[j])
        dnk = _nt(dkv_ref[...], wkv_ref[...])
        h = h_ref[...]
        r = _rms_scale(h)
        hh = h * r
        nb = (hh * gb_ref[...]).astype(BF16)
        for j in range(4):
            dwb_acc[j] += _tn(nb, dqz_ref[:, 512 * j:512 * (j + 1)])
        dwkv_acc[...] += _tn((hh * gk_ref[...]).astype(BF16), dkv_ref[...])
        _acc_row(dg_ref, 0, jnp.sum(dnk * hh, axis=0, keepdims=True))
        _acc_row(dg_ref, 1, jnp.sum(dnb * hh, axis=0, keepdims=True))
        dhh = dnb * gb_ref[...] + dnk * gk_ref[...]
        dh1 = dh2_ref[...] + r * (dhh - hh * jnp.mean(dhh * hh, axis=-1, keepdims=True))
        dh1_ref[...] = dh1
        oa = oa_ref[...].astype(F32)
        ra = _rms_scale(oa)
        oh = oa * ra
        _acc_row(dg_ref, 2, jnp.sum(dh1 * oh, axis=0, keepdims=True))
        doh = dh1 * ga_ref[...]
        doa_ref[...] = (ra * (doh - oh * jnp.mean(doh * oh, axis=-1, keepdims=True))).astype(BF16)

        @pl.when(pl.program_id(0) == nt - 1)
        def _():
            pltpu.sync_copy(dwb_acc, dwb_ref)
            pltpu.sync_copy(dwkv_acc, dwkv_ref)

    row = lambda i: (i, 0)
    fix = lambda i: (0, 0)
    anyspace = pl.BlockSpec(memory_space=pl.ANY)
    return pl.pallas_call(
        body, name="b_bwd", grid=(nt,),
        in_specs=[pl.BlockSpec((tm, 2 * D), row), pl.BlockSpec((tm, 2 * KV_W), row), pl.BlockSpec((tm, D), row),
                  pl.BlockSpec((tm, D), row), pl.BlockSpec((tm, D), row),
                  pl.BlockSpec((4, D, 512), lambda i: (0, 0, 0)), pl.BlockSpec((D, 2 * KV_W), fix),
                  pl.BlockSpec((1, D), fix), pl.BlockSpec((1, D), fix), pl.BlockSpec((1, D), fix)],
        out_specs=[pl.BlockSpec((tm, D), row), pl.BlockSpec((tm, D), row), pl.BlockSpec((8, D), fix), anyspace, anyspace],
        out_shape=[SDS((s, D), F32), SDS((s, D), BF16), SDS((8, D), F32), SDS((4, D, 512), F32),
                   SDS((D, 2 * KV_W), F32)],
        scratch_shapes=[pltpu.VMEM((4, D, 512), F32), pltpu.VMEM((D, 2 * KV_W), F32)],
        compiler_params=_params(("arbitrary",)),
    )(dqz, dkv, h1, dh2, oa, wbin_g, w_kv, g_kv, g_pre, g_apost)


def _chip_exchange(parts, recvs, send, recv):
    x, y, c = lax.axis_index("x"), lax.axis_index("y"), lax.axis_index("c")
    chips = [(x, 1 - y), (1 - x, y), (1 - x, 1 - y)]
    copies = []
    for a, (t, r) in enumerate(zip(parts, recvs)):
        for j, (px, py) in enumerate(chips):
            copies.append(pltpu.make_async_remote_copy(
                src_ref=t.at[2 * px + py], dst_ref=r.at[j], send_sem=send.at[3 * a + j],
                recv_sem=recv.at[3 * a + j], device_id=(px, py, c), device_id_type=MESH))
    return copies


def _exchange_specs(parts):
    anyspace = pl.BlockSpec(memory_space=pl.ANY)
    n = len(parts)
    return ([anyspace] * n, [anyspace] * n, [SDS((3,) + t.shape[1:], t.dtype) for t in parts],
            [pltpu.SemaphoreType.DMA((3 * n,)), pltpu.SemaphoreType.DMA((3 * n,))])


def _a_bwd(doa, proj, conv_w, w_out, tm, parts):
    s = doa.shape[0]
    nt = s // tm
    n = len(parts)
    ex_in, ex_out, ex_shape, ex_sems = _exchange_specs(parts)

    def body(*refs):
        doa_ref, proj_ref, halo_ref, cw_ref, w_ref = refs[:5]
        part_refs = refs[5:5 + n]
        dproj_ref, dcw_ref = refs[5 + n:7 + n]
        recv_refs = refs[7 + n:7 + 2 * n]
        carry, send, recv = refs[7 + 2 * n:]
        i = pl.program_id(0)
        r = nt - 1 - i

        @pl.when(i == 0)
        def _():
            dcw_ref[...] = jnp.zeros_like(dcw_ref)
            carry[...] = jnp.zeros_like(carry)
            for cp in _chip_exchange(part_refs, recv_refs, send, recv):
                cp.start()
        dya = _nt(doa_ref[...], w_ref[...])
        bg = proj_ref[:, 0:D].astype(F32)
        cg = proj_ref[:, D:2 * D].astype(F32)
        u = proj_ref[:, 2 * D:3 * D].astype(F32)
        z = proj_ref[:, 3 * D:4 * D].astype(F32)
        v = cg * u
        before = jnp.where(r > 0, halo_ref[:, D:2 * D].astype(F32) * halo_ref[:, 2 * D:3 * D].astype(F32), 0.0)
        rows = lax.broadcasted_iota(jnp.int32, (tm, D), 0)
        v1, v2 = _shift_rows(v, before[HALO - 1:HALO, :], before[HALO - 2:HALO - 1, :], rows)
        conv = cw_ref[0:1, :] * v2 + cw_ref[1:2, :] * v1 + cw_ref[2:3, :] * v
        sg, sz = _silu_parts(z)
        dproj_ref[:, 0:D] = (dya * conv * sz).astype(BF16)
        dproj_ref[:, 3 * D:4 * D] = (dya * bg * conv * _dsilu(z, sg)).astype(BF16)
        dconv = dya * bg * sz
        _acc_row(dcw_ref, 0, jnp.sum(dconv * v2, axis=0, keepdims=True))
        _acc_row(dcw_ref, 1, jnp.sum(dconv * v1, axis=0, keepdims=True))
        _acc_row(dcw_ref, 2, jnp.sum(dconv * v, axis=0, keepdims=True))
        after = carry[...]
        up1 = jnp.where(rows < tm - 1, pltpu.roll(dconv, tm - 1, 0), after[0:1, :])
        up2 = jnp.where(rows < tm - 2, pltpu.roll(dconv, tm - 2, 0),
                        jnp.where(rows == tm - 2, after[0:1, :], after[1:2, :]))
        carry[...] = dconv[0:8, :]
        dv = cw_ref[2:3, :] * dconv + cw_ref[1:2, :] * up1 + cw_ref[0:1, :] * up2
        dproj_ref[:, D:2 * D] = (dv * u).astype(BF16)
        dproj_ref[:, 2 * D:3 * D] = (dv * cg).astype(BF16)

        @pl.when(i == nt - 1)
        def _():
            for cp in _chip_exchange(part_refs, recv_refs, send, recv):
                cp.wait()

    rev = lambda i: (nt - 1 - i, 0)
    fix = lambda i: (0, 0)
    halo = lambda i: (jnp.maximum((nt - 1 - i) * (tm // HALO) - 1, 0), 0)
    dproj, dcw, *got = pl.pallas_call(
        body, name="a_bwd", grid=(nt,),
        in_specs=[pl.BlockSpec((tm, D), rev), pl.BlockSpec((tm, 4 * D), rev), pl.BlockSpec((HALO, 4 * D), halo),
                  pl.BlockSpec((8, D), fix), pl.BlockSpec((D, D), fix)] + ex_in,
        out_specs=[pl.BlockSpec((tm, 4 * D), rev), pl.BlockSpec((8, D), fix)] + ex_out,
        out_shape=[SDS((s, 4 * D), BF16), SDS((8, D), F32)] + ex_shape,
        scratch_shapes=[pltpu.VMEM((8, D), F32)] + ex_sems,
        compiler_params=_params(("arbitrary",)),
    )(doa, proj, proj, conv_w, w_out, *parts)
    return dproj, dcw, got


def _dn1(dp_ref, w_ref):
    dn = _nt(dp_ref[:, 0:D], w_ref[0])
    for j in range(1, 4):
        dn = dn + _nt(dp_ref[:, D * j:D * (j + 1)], w_ref[j])
    return dn


def _a_in_bwd_matmul(dproj, win_g, tm, count, parts):
    n = len(parts)
    ex_in, ex_out, ex_shape, ex_sems = _exchange_specs(parts)

    def body(*refs):
        dp_ref, w_ref = refs[:2]
        part_refs = refs[2:2 + n]
        dn_ref = refs[2 + n]
        recv_refs = refs[3 + n:3 + 2 * n]
        sems = refs[3 + 2 * n:]

        @pl.when(pl.program_id(0) == 0)
        def _():
            for cp in _chip_exchange(part_refs, recv_refs, *sems):
                cp.start()
        dn_ref[...] = _dn1(dp_ref, w_ref)

        @pl.when(pl.program_id(0) == count - 1)
        def _():
            for cp in _chip_exchange(part_refs, recv_refs, *sems):
                cp.wait()

    row = lambda i: (i, 0)
    dn, *got = pl.pallas_call(
        body, name="a_in_bwd_matmul", grid=(count,),
        in_specs=[pl.BlockSpec((tm, 4 * D), row), pl.BlockSpec((4, D, D), lambda i: (0, 0, 0))] + ex_in,
        out_specs=[pl.BlockSpec((tm, D), row)] + ex_out,
        out_shape=[SDS((count * tm, D), F32)] + ex_shape,
        scratch_shapes=ex_sems,
        compiler_params=_params(("arbitrary",)),
    )(dproj, win_g, *parts)
    return dn, got


def _a_in_bwd(dn_first, dproj, x, dh1, win_g, g_pre, tm):
    s = x.shape[0]
    nt = s // tm
    count = dn_first.shape[0] // tm

    def body(dn_ref, dp_ref, x_ref, dh_ref, w_ref, g_ref, gx_ref, dg_ref, dn_s):
        i = pl.program_id(0)

        @pl.when(i == 0)
        def _():
            dg_ref[...] = jnp.zeros_like(dg_ref)

        @pl.when(i < count)
        def _():
            dn_s[...] = dn_ref[...]

        @pl.when(i >= count)
        def _():
            dn_s[...] = _dn1(dp_ref, w_ref)
        dn = dn_s[...]
        xv = x_ref[...]
        r = _rms_scale(xv)
        xh = xv * r
        _acc_row(dg_ref, 0, jnp.sum(dn * xh, axis=0, keepdims=True))
        dxh = dn * g_ref[...]
        gx_ref[...] = dh_ref[...] + r * (dxh - xh * jnp.mean(dxh * xh, axis=-1, keepdims=True))

    row = lambda i: (i, 0)
    fix = lambda i: (0, 0)
    return pl.pallas_call(
        body, name="a_in_bwd", grid=(nt,),
        in_specs=[pl.BlockSpec((tm, D), lambda i: (jnp.minimum(i, count - 1), 0)),
                  pl.BlockSpec((tm, 4 * D), lambda i: (jnp.maximum(i, count), 0)),
                  pl.BlockSpec((tm, D), row), pl.BlockSpec((tm, D), row),
                  pl.BlockSpec((4, D, D), lambda i: (0, 0, 0)), pl.BlockSpec((1, D), fix)],
        out_specs=[pl.BlockSpec((tm, D), row), pl.BlockSpec((8, D), fix)],
        out_shape=[SDS((s, D), F32), SDS((8, D), F32)],
        scratch_shapes=[pltpu.VMEM((tm, D), F32)],
        compiler_params=_params(("arbitrary",)),
    )(dn_first, dproj, x, dh1, win_g, g_pre)


def _dw(a, b, tn, tmw, name):
    s, k = a.shape
    n = b.shape[1]

    def body(a_ref, b_ref, o_ref):
        @pl.when(pl.program_id(1) == 0)
        def _():
            o_ref[...] = jnp.zeros_like(o_ref)
        o_ref[0] += _tn(a_ref[...], b_ref[...])

    return pl.pallas_call(
        body, name=name, grid=(n // tn, s // tmw),
        in_specs=[pl.BlockSpec((tmw, k), lambda j, t: (t, 0)), pl.BlockSpec((tmw, tn), lambda j, t: (t, j))],
        out_specs=pl.BlockSpec((1, k, tn), lambda j, t: (j, 0, 0)),
        out_shape=SDS((n // tn, k, tn), F32),
        compiler_params=_params(("parallel", "arbitrary")),
    )(a, b)


def _sibling_exchange(name, to_sibling=(), shards=(), smalls=None):
    n_g, n_h = len(to_sibling), len(shards)
    has_small = smalls is not None

    def body(*refs):
        gs = refs[:n_g]
        pos = n_g + n_h
        small_in = refs[pos] if has_small else None
        pos += has_small
        rs, fs = refs[pos:pos + n_g], refs[pos + n_g:pos + n_g + n_h]
        pos += n_g + n_h
        small_all = refs[pos] if has_small else None
        pos += has_small
        dsend, drecv, ssend, srecv = refs[pos:]
        x, y, c = lax.axis_index("x"), lax.axis_index("y"), lax.axis_index("c")
        sibling = (x, y, 1 - c)
        sends, arrivals = [], []
        for a, (g, r) in enumerate(zip(gs, rs)):
            h = g.shape[1] // 2
            src = g.at[:, pl.ds(pl.multiple_of((1 - c) * h, 8), h), :]
            sends.append(pltpu.make_async_remote_copy(src_ref=src, dst_ref=r, send_sem=dsend.at[a], recv_sem=drecv.at[a],
                                                      device_id=sibling, device_id_type=MESH))
            arrivals.append(pltpu.make_async_remote_copy(src_ref=r, dst_ref=r, send_sem=dsend.at[a], recv_sem=drecv.at[a],
                                                         device_id=sibling, device_id_type=MESH))
        for b, full in enumerate(fs):
            h = full.shape[0] // 2
            mine = full.at[pl.ds(pl.multiple_of(c * h, 8), h)]
            theirs = full.at[pl.ds(pl.multiple_of((1 - c) * h, 8), h)]
            sends.append(pltpu.make_async_remote_copy(src_ref=mine, dst_ref=mine, send_sem=dsend.at[n_g + b],
                                                      recv_sem=drecv.at[n_g + b], device_id=sibling, device_id_type=MESH))
            arrivals.append(pltpu.make_async_remote_copy(src_ref=mine, dst_ref=theirs, send_sem=dsend.at[n_g + b],
                                                         recv_sem=drecv.at[n_g + b], device_id=sibling, device_id_type=MESH))
        if has_small:
            me = 4 * x + 2 * y + c
            small_all[me] = small_in[...]
            for rel in range(1, N_DEV):
                fx, fy, fc = rel >> 2, (rel >> 1) & 1, rel & 1
                peer = (x + fx - 2 * x * fx, y + fy - 2 * y * fy, c + fc - 2 * c * fc)
                sender = 4 * peer[0] + 2 * peer[1] + peer[2]
                sends.append(pltpu.make_async_remote_copy(
                    src_ref=small_in, dst_ref=small_all.at[me], send_sem=ssend.at[rel - 1], recv_sem=srecv.at[rel - 1],
                    device_id=peer, device_id_type=MESH))
                arrivals.append(pltpu.make_async_remote_copy(
                    src_ref=small_in, dst_ref=small_all.at[sender], send_sem=ssend.at[rel - 1], recv_sem=srecv.at[rel - 1],
                    device_id=peer, device_id_type=MESH))
        for cp in sends:
            cp.start()
        for cp in arrivals:
            cp.wait_recv()
        for cp in sends:
            cp.wait_send()

    anyspace = pl.BlockSpec(memory_space=pl.ANY)
    vm = pl.BlockSpec(memory_space=pltpu.VMEM)
    out_shape = [SDS((N_CHIPS, g.shape[1] // 2, g.shape[2]), F32) for g in to_sibling]
    out_shape += [SDS(full.shape, F32) for full in shards]
    if has_small:
        out_shape.append(SDS((N_DEV,) + smalls.shape, F32))
    n_d2d = max(n_g + n_h, 1)
    outs = pl.pallas_call(
        body, name=name, out_shape=out_shape,
        in_specs=[anyspace] * (n_g + n_h) + [vm] * has_small, out_specs=[anyspace] * (n_g + n_h) + [vm] * has_small,
        scratch_shapes=[pltpu.SemaphoreType.DMA((n_d2d,)), pltpu.SemaphoreType.DMA((n_d2d,)),
                        pltpu.SemaphoreType.DMA((N_DEV - 1,)), pltpu.SemaphoreType.DMA((N_DEV - 1,))],
        input_output_aliases={n_g + b: n_g + b for b in range(n_h)},
    )(*to_sibling, *shards, *([smalls] if has_small else []))
    return outs[:n_g], outs[n_g:n_g + n_h], (outs[n_g + n_h] if has_small else None)


def _add_sibling(where, g, r, name):
    _, rows, cols = g.shape
    h = rows // 2
    tr = min(h, 256)
    nh = h // tr

    def body(where_ref, g_ref, r_ref, t_ref, own_ref):
        t = g_ref[0] + r_ref[0]
        t_ref[0] = t.astype(BF16)

        @pl.when(pl.program_id(1) == where_ref[1])
        def _():
            own_ref[...] = t

    return pl.pallas_call(
        body, name=name,
        grid_spec=pltpu.PrefetchScalarGridSpec(
            num_scalar_prefetch=1, grid=(nh, N_CHIPS),
            in_specs=[pl.BlockSpec((1, tr, cols), lambda i, k, w: (k, w[0] * nh + i, 0)),
                      pl.BlockSpec((1, tr, cols), lambda i, k, w: (k, i, 0))],
            out_specs=[pl.BlockSpec((1, tr, cols), lambda i, k, w: (k, i, 0)),
                       pl.BlockSpec((tr, cols), lambda i, k, w: (i, 0))]),
        out_shape=[SDS((N_CHIPS, h, cols), BF16), SDS((h, cols), F32)],
        compiler_params=_params(("parallel", "arbitrary")),
    )(where, g, r)


def _add_chips(where, own, r, name):
    h, cols = own.shape
    tr = min(h, 256)
    nh = h // tr

    def body(where_ref, t_ref, r_ref, o_ref):
        del where_ref
        o_ref[...] = ((t_ref[...] + r_ref[0].astype(F32)) + r_ref[1].astype(F32)) + r_ref[2].astype(F32)

    return pl.pallas_call(
        body, name=name,
        grid_spec=pltpu.PrefetchScalarGridSpec(
            num_scalar_prefetch=1, grid=(nh,),
            in_specs=[pl.BlockSpec((tr, cols), lambda i, w: (i, 0)), pl.BlockSpec((3, tr, cols), lambda i, w: (0, i, 0))],
            out_specs=pl.BlockSpec((tr, cols), lambda i, w: (w[0] * nh + i, 0))),
        out_shape=SDS((2 * h, cols), F32),
        compiler_params=_params(("parallel",)),
    )(where, own, r)


def _sum_smalls(small_all):
    def body(all_ref, o_ref):
        acc = all_ref[0]
        for dev in range(1, N_DEV):
            acc = acc + all_ref[dev]
        o_ref[...] = acc

    return pl.pallas_call(
        body, name="sum_smalls", out_shape=SDS(small_all.shape[1:], F32),
        in_specs=[pl.BlockSpec(memory_space=pltpu.VMEM)], out_specs=pl.BlockSpec(memory_space=pltpu.VMEM),
    )(small_all)


def _adam_step(g, w, m, v):
    nm = ADAM_B1 * m + (1.0 - ADAM_B1) * g
    nv = ADAM_B2 * v + (1.0 - ADAM_B2) * (g * g)
    m_hat = nm / (1.0 - ADAM_B1 ** ADAM_STEP)
    v_hat = nv / (1.0 - ADAM_B2 ** ADAM_STEP)
    return -ADAM_LR * (m_hat / (jnp.sqrt(v_hat) + ADAM_EPS) + ADAM_WD * w), nm, nv


def _adamw(g, w, m, v, name):
    rows, cols = g.shape
    tr = min(rows, 256)

    def body(g_ref, w_ref, m_ref, v_ref, d_ref, nm_ref, nv_ref):
        d_ref[...], nm_ref[...], nv_ref[...] = _adam_step(g_ref[...], w_ref[...], m_ref[...], v_ref[...])

    spec = pl.BlockSpec((tr, cols), lambda i: (i, 0))
    return pl.pallas_call(
        body, name=name, grid=(rows // tr,), in_specs=[spec] * 4, out_specs=[spec] * 3,
        out_shape=[SDS(g.shape, F32)] * 3, compiler_params=_params(("parallel",)),
    )(g, w, m, v)


def _small_update(chip, tot, wmv):
    names = list(SMALL_PLACES)
    n = len(names)

    def body(chip_ref, tot_ref, quarter_ref, *refs):
        del chip_ref
        ins, outs = refs[:3 * n], refs[3 * n:]
        for i, nm in enumerate(names):
            sharded, row, (rows, cols) = SMALL_PLACES[nm]
            g = (quarter_ref if sharded else tot_ref)[row:row + rows, 0:cols]
            outs[4 * i][...] = g
            outs[4 * i + 1][...], outs[4 * i + 2][...], outs[4 * i + 3][...] = _adam_step(
                g, ins[3 * i][...], ins[3 * i + 1][...], ins[3 * i + 2][...])

    whole = lambda shape: pl.BlockSpec(shape, lambda i, c: (0,) * len(shape))
    shapes = [SMALL_PLACES[nm][2] for nm in names]
    outs = pl.pallas_call(
        body, name="small_update",
        grid_spec=pltpu.PrefetchScalarGridSpec(
            num_scalar_prefetch=1, grid=(1,),
            in_specs=[whole(tot.shape), pl.BlockSpec((tot.shape[0], D // 4), lambda i, c: (0, c[0]))]
            + [whole(shp) for shp in shapes for _ in range(3)],
            out_specs=[whole(shp) for shp in shapes for _ in range(4)]),
        out_shape=[SDS(shp, F32) for shp in shapes for _ in range(4)],
    )(chip, tot, tot, *[a for nm in names for a in wmv[nm]])
    return {nm: tuple(outs[4 * i:4 * i + 4]) for i, nm in enumerate(names)}


def _pad_rows(a, rows):
    return jnp.concatenate([a, jnp.zeros((rows - a.shape[0], a.shape[1]), a.dtype)], axis=0)


def _pad_cols(a, cols):
    return jnp.concatenate([a, jnp.zeros((a.shape[0], cols - a.shape[1]), a.dtype)], axis=1)


def kernel(x, a_pre_norm, a_w_in, a_conv_w, a_w_out, a_post_norm, kv_norm, w_kv, rel_bias, b_pre_norm, b_w_in, b_sinks, b_w_out, b_post_norm, loss_target, m_a_pre_norm, m_a_w_in, m_a_conv_w, m_a_w_out, m_a_post_norm, m_kv_norm, m_w_kv, m_rel_bias, m_b_pre_norm, m_b_w_in, m_b_sinks, m_b_w_out, m_b_post_norm, v_a_pre_norm, v_a_w_in, v_a_conv_w, v_a_w_out, v_a_post_norm, v_kv_norm, v_w_kv, v_rel_bias, v_b_pre_norm, v_b_w_in, v_b_sinks, v_b_w_out, v_b_post_norm):
    seq = x.shape[1]
    xs = x.reshape(seq, D)
    tgt = loss_target.reshape(seq, D)
    chip = 2 * lax.axis_index("x") + lax.axis_index("y")
    core = lax.axis_index("c")
    tm = _tile(seq, 512)
    tm_mix = _tile(seq, 256)
    tmw = _tile(seq, 1024)

    shards = [a_w_in[0], a_w_out[0], w_kv, b_w_in[0], b_w_out[0]]
    small_w = _pad_rows(jnp.concatenate([a_pre_norm, a_conv_w[0], a_post_norm], axis=0), 8)
    *own_only, small_g = _gather_weights(shards, small_w, 0)
    where = jnp.stack([core, chip]).astype(jnp.int32)
    small_full = small_g.transpose(1, 0, 2).reshape(8, D)
    g_apre, conv_w, g_apost = small_full[0:1], _pad_rows(small_full[1:4], 8), small_full[4:5]
    g_kv = kv_norm.reshape(1, D)

    proj, n1, (win_g, wouta_g, wkv_g, wbin_g, woutb_g) = _a_in(where[1:2], xs, g_apre, own_only, tmw)
    wouta = wouta_g.reshape(D, D)
    wkv = wkv_g.reshape(D, 2 * KV_W)
    woutb = woutb_g.reshape(D, D)
    ya, oa, h1 = _a_mix(proj, xs, conv_w, wouta, g_apost, tm_mix)
    kv, q, zb = _b_in(h1, g_kv, b_pre_norm, wkv, wbin_g, tm)
    tab = _bias_table(rel_bias, b_sinks.reshape(N_HEADS))
    att, stats = _attn_fwd(q, kv, tab)
    dh2, dqz, datt, loss_acc, dg_bpost, dw_outb = _mid(att, zb, h1, tgt, woutb, b_post_norm, tm)

    dqz, dkv, dtab = _attn_bwd(q, kv, datt, stats, tab, dqz)
    dh1, doa, dg_b, dw_bin, dw_kv = _b_bwd(dqz, dkv, h1, dh2, oa, wbin_g, wkv, g_kv, b_pre_norm, g_apost, tm)
    dw_outa = _dw(ya, doa, D, tmw, "dw_a_out").reshape(N_CHIPS, D // 4, D)
    dw_kv = dw_kv.reshape(N_CHIPS, D // 4, 2 * KV_W)
    dw_outb = dw_outb.reshape(N_CHIPS, D // 4, D)
    grads1 = [dw_outa, dw_kv, dw_bin, dw_outb]
    names1 = ["a_w_out", "w_kv", "b_w_in", "b_w_out"]
    from_sibling1, _, _ = _sibling_exchange("to_sibling_1", to_sibling=grads1)
    sums1 = [_add_sibling(where, g, r, "add_sibling_" + nm) for g, r, nm in zip(grads1, from_sibling1, names1)]
    dproj, dconv_w, from_chips1 = _a_bwd(doa, proj, conv_w, wouta, tm_mix, [t for t, _ in sums1])
    shards1 = [_add_chips(where, own, r, "add_chips_" + nm) for (_, own), r, nm in zip(sums1, from_chips1, names1)]
    dw_in = _dw(n1, dproj, D, tmw, "dw_a_in")
    from_sibling2, (g_wouta, g_wkv, g_wbin, g_woutb), _ = _sibling_exchange(
        "to_sibling_2", to_sibling=[dw_in], shards=shards1)
    part2, own2 = _add_sibling(where, dw_in, from_sibling2[0], "add_sibling_a_w_in")
    nt = seq // tm
    dn_first, from_chips2 = _a_in_bwd_matmul(dproj, win_g, tm, max(nt - max(nt // 4, 1), 1), [part2])
    grad_x, dg_apre = _a_in_bwd(dn_first, dproj, xs, dh1, win_g, g_apre, tm)
    shard2 = _add_chips(where, own2, from_chips2[0], "add_chips_a_w_in")
    drel, dsink = _bias_fold(dtab)

    smalls = jnp.concatenate([
        dg_apre[0:1], dg_b[2:3], dg_b[0:1], dg_b[1:2], dg_bpost[0:1], _pad_cols(dsink[0:1], D),
        _pad_cols(loss_acc[0:1], D), jnp.zeros((1, D), F32), dconv_w, _pad_cols(drel, D)], axis=0)
    _, (g_win,), small_all = _sibling_exchange("share_last", shards=[shard2], smalls=smalls)
    tot = _sum_smalls(small_all)

    big = {}
    for nm, g, w, m, v in [("a_w_in", g_win, a_w_in, m_a_w_in, v_a_w_in), ("a_w_out", g_wouta, a_w_out, m_a_w_out, v_a_w_out),
                           ("w_kv", g_wkv, w_kv, m_w_kv, v_w_kv), ("b_w_in", g_wbin, b_w_in, m_b_w_in, v_b_w_in),
                           ("b_w_out", g_woutb, b_w_out, m_b_w_out, v_b_w_out)]:
        shp = w.shape
        two = (shp[-2], shp[-1])
        d, nm_, nv_ = _adamw(g, w.reshape(two), m.reshape(two), v.reshape(two), "adamw_" + nm)
        big[nm] = (g.reshape(shp), d.reshape(shp), nm_.reshape(shp), nv_.reshape(shp))

    given = {"a_pre_norm": (a_pre_norm, m_a_pre_norm, v_a_pre_norm), "a_conv_w": (a_conv_w, m_a_conv_w, v_a_conv_w),
             "a_post_norm": (a_post_norm, m_a_post_norm, v_a_post_norm), "kv_norm": (kv_norm, m_kv_norm, v_kv_norm),
             "rel_bias": (rel_bias, m_rel_bias, v_rel_bias), "b_pre_norm": (b_pre_norm, m_b_pre_norm, v_b_pre_norm),
             "b_sinks": (b_sinks, m_b_sinks, v_b_sinks), "b_post_norm": (b_post_norm, m_b_post_norm, v_b_post_norm)}
    small = _small_update(where[1:2], tot, {nm: tuple(a.reshape(SMALL_PLACES[nm][2]) for a in wmv)
                                            for nm, wmv in given.items()})
    order = ["a_pre_norm", "a_w_in", "a_conv_w", "a_w_out", "a_post_norm", "kv_norm", "w_kv", "rel_bias",
             "b_pre_norm", "b_w_in", "b_sinks", "b_w_out", "b_post_norm"]
    outs = []
    for which in range(4):
        for nm in order:
            outs.append(big[nm][which] if nm in big else small[nm][which].reshape(given[nm][0].shape))
    loss = 0.5 * tot[LOSS_ROW, 0]
    return (loss, grad_x.reshape(x.shape), *outs)
```

```python
import functools
import math

import jax
import jax.numpy as jnp
from jax import lax
from jax.experimental import pallas as pl
from jax.experimental.pallas import tpu as pltpu

F32 = jnp.float32
BF16 = jnp.bfloat16
MESH = pl.DeviceIdType.MESH
SDS = jax.ShapeDtypeStruct

D = 1024
HEAD_DIM = 64
N_HEADS = 16
N_KV = 2
GROUP = 8
KV_W = 128
BLK = 128
N_BUCKETS = 32
MAX_EXACT = 16
MAX_DISTANCE = 128
EPS = 1e-6
NEG_INF = -1e30
Q_SCALE = HEAD_DIM ** -0.5

ADAM_LR = 0.001
ADAM_B1 = 0.9
ADAM_B2 = 0.999
ADAM_EPS = 1e-08
ADAM_WD = 0.01
ADAM_STEP = 10

N_CHIPS = 4
N_DEV = 8
VMEM_LIMIT = 56 * 1024 * 1024
SMALL_ROWS = 48
LOSS_ROW = 6
SMALL_PLACES = {
    "a_pre_norm": (True, 0, (1, D // 4)), "a_conv_w": (True, 8, (3, D // 4)), "a_post_norm": (True, 1, (1, D // 4)),
    "kv_norm": (False, 2, (1, D)), "rel_bias": (False, 16, (N_BUCKETS, N_HEADS)), "b_pre_norm": (False, 3, (1, D)),
    "b_sinks": (False, 5, (1, N_HEADS)), "b_post_norm": (False, 4, (1, D)),
}
HALO = 16


def _bucket_thresholds():
    def bucket(d):
        big = MAX_EXACT + int(math.log(d / MAX_EXACT) / math.log(MAX_DISTANCE / MAX_EXACT)
                              * (N_BUCKETS - MAX_EXACT))
        return d if d < MAX_EXACT else min(big, N_BUCKETS - 1)
    out = []
    for b in range(MAX_EXACT + 1, N_BUCKETS):
        out.append(min(d for d in range(MAX_EXACT, MAX_DISTANCE) if bucket(d) >= b))
    return tuple(out)


BUCKET_THRESHOLDS = _bucket_thresholds()


def _params(semantics=None, vmem=VMEM_LIMIT):
    return pltpu.CompilerParams(dimension_semantics=semantics, vmem_limit_bytes=vmem)


def _tile(n, pref):
    return pref if n >= 2 * pref else max(n // 2, 8)


def _rms_scale(v):
    return lax.rsqrt(jnp.mean(v * v, axis=-1, keepdims=True) + EPS)


def _nt(a, b):
    return lax.dot_general(a, b, (((1,), (1,)), ((), ())), preferred_element_type=F32)


def _tn(a, b):
    return lax.dot_general(a, b, (((0,), (0,)), ((), ())), preferred_element_type=F32)


def _nn(a, b):
    return jnp.dot(a, b, preferred_element_type=F32)


def _silu_parts(z):
    sg = jax.nn.sigmoid(z)
    return sg, z * sg


def _dsilu(z, sg):
    return sg * (1.0 + z * (1.0 - sg))


def _acc_row(ref, row, val):
    ref[row:row + 1, :] += val


def _gather_copies(outs, splits, ici_send, ici_recv, d2d_send, d2d_recv):
    x, y, c = lax.axis_index("x"), lax.axis_index("y"), lax.axis_index("c")
    k = 2 * x + y
    sibling = (x, y, 1 - c)

    def part(o_ref, chip, core, split):
        if not split:
            return o_ref.at[chip]
        h = o_ref.shape[1] // 2
        return o_ref.at[chip, pl.ds(pl.multiple_of(core * h, 16), h)]

    def remote(ref, a, j, sems, to):
        return pltpu.make_async_remote_copy(src_ref=ref, dst_ref=ref, send_sem=sems[0].at[3 * a + j],
                                            recv_sem=sems[1].at[3 * a + j], device_id=to, device_id_type=MESH)

    copies = []
    for a, (o_ref, split) in enumerate(zip(outs, splits)):
        for j, (px, py) in enumerate([(x, 1 - y), (1 - x, y), (1 - x, 1 - y)]):
            kj = 2 * px + py
            ici, d2d = (ici_send, ici_recv), (d2d_send, d2d_recv)
            copies.append((remote(part(o_ref, k, c, split), a, j, ici, (px, py, c)),
                           remote(part(o_ref, kj, c, split), a, j, ici, (px, py, c)),
                           remote(part(o_ref, kj, c, split), a, j, d2d, sibling) if split else None,
                           remote(part(o_ref, kj, 1 - c, split), a, j, d2d, sibling) if split else None))
    return copies


def _gather_sems(n):
    return [pltpu.SemaphoreType.DMA((3 * n,)) for _ in range(4)]


def _gather_weights(shards, small, n_now):
    n = len(shards)

    def body(*refs):
        ins, small_in = refs[:n], refs[n]
        outs, small_out = refs[n + 1:2 * n + 1], refs[2 * n + 1]
        sems = refs[2 * n + 2:]
        k = 2 * lax.axis_index("x") + lax.axis_index("y")
        for i_ref, o_ref in zip(ins, outs):
            o_ref[k] = i_ref[...].astype(BF16)
        small_out[k] = small_in[...]
        copies = _gather_copies(list(outs[:n_now]) + [small_out], [True] * n_now + [False], *sems)
        for send, _, _, _ in copies:
            send.start()
        for _, arrival, forward, _ in copies:
            arrival.wait_recv()
            if forward is not None:
                forward.start()
        for send, _, forward, forwarded in copies:
            if forward is not None:
                forwarded.wait_recv()
                forward.wait_send()
            send.wait_send()

    vm = pl.BlockSpec(memory_space=pltpu.VMEM)
    out_shape = [SDS((N_CHIPS,) + s.shape, BF16) for s in shards] + [SDS((N_CHIPS,) + small.shape, F32)]
    return pl.pallas_call(
        body, name="gather_weights", out_shape=out_shape,
        in_specs=[vm] * (n + 1), out_specs=[vm] * (n + 1),
        scratch_shapes=_gather_sems(n_now + 1),
        compiler_params=pltpu.CompilerParams(vmem_limit_bytes=VMEM_LIMIT),
    )(*shards, small)


def _a_in(chip, x, g_pre, weights, tm):
    s = x.shape[0]
    nt = s // tm
    n = len(weights)

    def body(chip_ref, x_ref, g_ref, *refs):
        proj_ref, n1_ref = refs[n:n + 2]
        gathered = refs[n + 2:2 * n + 2]
        wbuf, n1_all, fetch_sem = refs[2 * n + 2:2 * n + 5]
        sems = refs[2 * n + 5:]
        jj, i = pl.program_id(0), pl.program_id(1)
        copies = _gather_copies(gathered, [True] * n, *sems)

        def fetch(rel):
            slot = jnp.bitwise_xor(chip_ref[0], rel)
            return pltpu.make_async_copy(gathered[0].at[slot], wbuf.at[rel % 2], fetch_sem.at[rel % 2])

        @pl.when((jj == 0) & (i == 0))
        def _():
            fetch(0).start()
            copies[0][0].start()
            copies[1][0].start()
            fetch(0).wait()

        for rel in (1, 2, 3):
            @pl.when((jj == rel) & (i == 0))
            def _():
                fetch(rel).wait()

        @pl.when(jj == 0)
        def _():
            xv = x_ref[...]
            n1 = (xv * _rms_scale(xv) * g_ref[...]).astype(BF16)
            n1_ref[...] = n1
            n1_all[i] = n1
        proj_ref[...] = _nn(n1_all[i], wbuf[jj % 2]).astype(BF16)

        for rel in (1, 2, 3):
            @pl.when((jj == rel - 1) & (i == max(nt - 3, nt // 2)))
            def _():
                _, arrival, forward, forwarded = copies[rel - 1]
                arrival.wait_recv()
                forward.start()
                forwarded.wait_recv()
                fetch(rel).start()
                if rel == 1:
                    copies[2][0].start()
                if rel == 2:
                    for send, _, _, _ in copies[3:]:
                        send.start()

        @pl.when((jj == 3) & (i == max(nt - 2, 0)))
        def _():
            for _, arrival, forward, _ in copies[3:]:
                arrival.wait_recv()
                forward.start()

        @pl.when((jj == 3) & (i == nt - 1))
        def _():
            for _, _, _, forwarded in copies[3:]:
                forwarded.wait_recv()
            for send, _, forward, _ in copies:
                forward.wait_send()
                send.wait_send()

    anyspace = pl.BlockSpec(memory_space=pl.ANY)
    proj, n1, *gathered = pl.pallas_call(
        body, name="a_in",
        grid_spec=pltpu.PrefetchScalarGridSpec(
            num_scalar_prefetch=1, grid=(4, nt),
            in_specs=[pl.BlockSpec((tm, D), lambda jj, i, c: (jnp.where(jj == 0, i, nt - 1), 0)),
                      pl.BlockSpec((1, D), lambda jj, i, c: (0, 0))] + [anyspace] * n,
            out_specs=[pl.BlockSpec((tm, D), lambda jj, i, c: (i, jnp.bitwise_xor(c[0], jj))),
                       pl.BlockSpec((tm, D), lambda jj, i, c: (jnp.where(jj == 0, i, nt - 1), 0))] + [anyspace] * n,
            scratch_shapes=[pltpu.VMEM((2, D, D), BF16), pltpu.VMEM((nt, tm, D), BF16),
                            pltpu.SemaphoreType.DMA((2,))] + _gather_sems(n)),
        out_shape=[SDS((s, 4 * D), BF16), SDS((s, D), BF16)] + [SDS(w.shape, w.dtype) for w in weights],
        input_output_aliases={3 + a: 2 + a for a in range(n)},
        compiler_params=_params(("arbitrary", "arbitrary")),
    )(chip, x, g_pre, *weights)
    return proj, n1, gathered


def _shift_rows(v, last, second_last, rows):
    v1 = jnp.where(rows >= 1, pltpu.roll(v, 1, 0), last)
    v2 = jnp.where(rows >= 2, pltpu.roll(v, 2, 0), jnp.where(rows == 1, last, second_last))
    return v1, v2


def _a_mix(proj, x, conv_w, w_out, g_post, tm):
    s = x.shape[0]

    def body(proj_ref, x_ref, cw_ref, w_ref, g_ref, ya_ref, oa_ref, h1_ref, carry):
        @pl.when(pl.program_id(0) == 0)
        def _():
            carry[...] = jnp.zeros_like(carry)
        v = proj_ref[:, D:2 * D].astype(F32) * proj_ref[:, 2 * D:3 * D].astype(F32)
        rows = lax.broadcasted_iota(jnp.int32, (tm, D), 0)
        before = carry[...]
        v1, v2 = _shift_rows(v, before[7:8, :], before[6:7, :], rows)
        carry[...] = v[tm - 8:tm, :]
        conv = cw_ref[0:1, :] * v2 + cw_ref[1:2, :] * v1 + cw_ref[2:3, :] * v
        _, sz = _silu_parts(proj_ref[:, 3 * D:4 * D].astype(F32))
        ya = (proj_ref[:, 0:D].astype(F32) * conv * sz).astype(BF16)
        ya_ref[...] = ya
        oa = _nn(ya, w_ref[...])
        oa_ref[...] = oa.astype(BF16)
        h1_ref[...] = x_ref[...] + oa * _rms_scale(oa) * g_ref[...]

    row = lambda i: (i, 0)
    fix = lambda i: (0, 0)
    return pl.pallas_call(
        body, name="a_mix", grid=(s // tm,),
        in_specs=[pl.BlockSpec((tm, 4 * D), row), pl.BlockSpec((tm, D), row), pl.BlockSpec((8, D), fix),
                  pl.BlockSpec((D, D), fix), pl.BlockSpec((1, D), fix)],
        out_specs=[pl.BlockSpec((tm, D), row)] * 3,
        out_shape=[SDS((s, D), BF16), SDS((s, D), BF16), SDS((s, D), F32)],
        scratch_shapes=[pltpu.VMEM((8, D), F32)],
        compiler_params=_params(("arbitrary",)),
    )(proj, x, conv_w, w_out, g_post)


def _b_in(h1, g_kv, g_pre, w_kv, wbin_g, tm):
    s = h1.shape[0]

    def body(h_ref, gk_ref, gb_ref, wkv_ref, wb_ref, kv_ref, q_ref, z_ref):
        h = h_ref[...]
        hh = h * _rms_scale(h)
        nk = (hh * gk_ref[...]).astype(BF16)
        nb = (hh * gb_ref[...]).astype(BF16)
        kv_ref[...] = _nn(nk, wkv_ref[...]).astype(BF16)
        for j in range(2):
            q_ref[:, 512 * j:512 * (j + 1)] = (_nn(nb, wb_ref[j]) * Q_SCALE).astype(BF16)
            z_ref[:, 512 * j:512 * (j + 1)] = _nn(nb, wb_ref[2 + j]).astype(BF16)

    row = lambda i: (i, 0)
    fix = lambda i: (0, 0)
    return pl.pallas_call(
        body, name="b_in", grid=(s // tm,),
        in_specs=[pl.BlockSpec((tm, D), row), pl.BlockSpec((1, D), fix), pl.BlockSpec((1, D), fix),
                  pl.BlockSpec((D, 2 * KV_W), fix), pl.BlockSpec((4, D, 512), lambda i: (0, 0, 0))],
        out_specs=[pl.BlockSpec((tm, 2 * KV_W), row), pl.BlockSpec((tm, D), row), pl.BlockSpec((tm, D), row)],
        out_shape=[SDS((s, 2 * KV_W), BF16), SDS((s, D), BF16), SDS((s, D), BF16)],
        compiler_params=_params(("parallel",)),
    )(h1, g_kv, g_pre, w_kv, wbin_g)


def _band_buckets():
    q = lax.broadcasted_iota(jnp.int32, (BLK, 2 * BLK), 0)
    k = lax.broadcasted_iota(jnp.int32, (BLK, 2 * BLK), 1)
    dist = q + BLK - k
    bucket = jnp.where(dist < MAX_EXACT, dist, MAX_EXACT)
    for t in BUCKET_THRESHOLDS:
        bucket = bucket + jnp.where(dist >= t, 1, 0)
    in_window = (dist >= 0) & (dist < BLK)
    return jnp.where(in_window, bucket, -1)


def _head_place(h):
    kh, j, e = h // GROUP, (h % GROUP) // 2, h % 2
    return kh, slice(BLK * j, BLK * (j + 1)), slice(2 * BLK * e, 2 * BLK * (e + 1))


def _bias_table(rel_bias, sinks):
    def body(rb_ref, sink_ref, tab_ref):
        bucket = _band_buckets()
        col = lax.broadcasted_iota(jnp.int32, (BLK, 2 * BLK), 1)
        for h in range(N_HEADS):
            acc = jnp.where(bucket < 0, NEG_INF, 0.0).astype(F32)
            for b in range(N_BUCKETS):
                acc = jnp.where(bucket == b, rb_ref[b, h], acc)
            acc = jnp.where(col == 0, sink_ref[h], acc)
            kh, rows, cols = _head_place(h)
            tab_ref[1, kh, rows, cols] = acc
            tab_ref[0, kh, rows, cols] = jnp.where((col > 0) & (col < BLK), NEG_INF, acc)

    return pl.pallas_call(
        body, name="bias_table", out_shape=SDS((2, N_KV, 4 * BLK, 4 * BLK), F32),
        in_specs=[pl.BlockSpec(memory_space=pltpu.SMEM), pl.BlockSpec(memory_space=pltpu.SMEM)],
        out_specs=pl.BlockSpec(memory_space=pltpu.VMEM),
    )(rel_bias, sinks)


def _bias_fold(dtab):
    def body(dtab_ref, out_ref, dsink_ref):
        bucket = _band_buckets()
        row = lax.broadcasted_iota(jnp.int32, (N_BUCKETS, 128), 0)
        lane = lax.broadcasted_iota(jnp.int32, (N_BUCKETS, 128), 1)
        row8 = lax.broadcasted_iota(jnp.int32, (8, 128), 0)
        lane8 = lax.broadcasted_iota(jnp.int32, (8, 128), 1)
        acc = jnp.zeros((N_BUCKETS, 128), F32)
        dsink = jnp.zeros((8, 128), F32)
        for h in range(N_HEADS):
            kh, rows, cols = _head_place(h)
            dt = dtab_ref[kh, rows, cols]
            for b in range(N_BUCKETS):
                val = jnp.sum(jnp.where(bucket == b, dt, 0.0))
                acc = acc + jnp.where((row == b) & (lane == h), val, 0.0)
            dsink = dsink + jnp.where((row8 == 0) & (lane8 == h), jnp.sum(dt[:, 0:1]), 0.0)
        out_ref[...] = acc
        dsink_ref[...] = dsink

    vm = pl.BlockSpec(memory_space=pltpu.VMEM)
    return pl.pallas_call(
        body, name="bias_fold", out_shape=[SDS((N_BUCKETS, 128), F32), SDS((8, 128), F32)],
        in_specs=[vm], out_specs=[vm, vm],
    )(dtab)


def _pair_operands(prev, cur):
    t = jnp.concatenate([prev, cur], axis=0).astype(F32)
    t = jnp.where(lax.broadcasted_iota(jnp.int32, t.shape, 0) == 0, 0.0, t)
    tr = pltpu.roll(t, HEAD_DIM, 1)
    lo = lax.broadcasted_iota(jnp.int32, t.shape, 1) < HEAD_DIM
    zero = jnp.zeros_like(t)
    head0 = jnp.concatenate([jnp.where(lo, t, zero), jnp.where(lo, zero, tr)], axis=0).astype(BF16)
    head1 = jnp.concatenate([jnp.where(lo, tr, zero), jnp.where(lo, zero, t)], axis=0).astype(BF16)
    return head0, head1


def _pair_fold(d0, d1):
    lo = lax.broadcasted_iota(jnp.int32, (2 * BLK, KV_W), 1) < HEAD_DIM
    zero = jnp.zeros((2 * BLK, KV_W), F32)
    g0 = jnp.where(lo, d0[0:256], zero) + pltpu.roll(jnp.where(lo, zero, d0[256:512]), HEAD_DIM, 1)
    g1 = pltpu.roll(jnp.where(lo, d1[0:256], zero), HEAD_DIM, 1) + jnp.where(lo, zero, d1[256:512])
    return jnp.where(lax.broadcasted_iota(jnp.int32, (2 * BLK, KV_W), 0) == 0, 0.0, g0 + g1)


def _stack_pairs(ref, kh, rows=slice(None)):
    return jnp.concatenate([ref[rows, 128 * (4 * kh + j):128 * (4 * kh + j + 1)] for j in range(4)], axis=0)


def _table_spec():
    return pl.BlockSpec((1, N_KV, 4 * BLK, 4 * BLK), lambda n: (jnp.minimum(n, 1), 0, 0, 0))


def _attn_fwd(q, kv, tab):
    s = q.shape[0]

    def body(q_ref, kp_ref, k0_ref, k1_ref, vp_ref, v0_ref, v1_ref, tab0_ref, tab1_ref, att_ref, stats_ref):
        lane = lax.broadcasted_iota(jnp.int32, (BLK, 128), 1)
        for sub, (kp, kc, vp, vc, tab_ref) in enumerate([(kp_ref, k0_ref, vp_ref, v0_ref, tab0_ref),
                                                         (k0_ref, k1_ref, v0_ref, v1_ref, tab1_ref)]):
            rows = slice(BLK * sub, BLK * (sub + 1))
            k2 = _pair_operands(kp[...], kc[...])
            v2 = _pair_operands(vp[...], vc[...])
            stats = jnp.zeros((BLK, 128), F32)
            for kh in range(N_KV):
                sc = _nt(_stack_pairs(q_ref, kh, rows), k2[kh])
                ps = []
                for e in range(2):
                    lg = sc[:, 256 * e:256 * (e + 1)] + tab_ref[0, kh, :, 256 * e:256 * (e + 1)]
                    m = jnp.max(lg, axis=-1, keepdims=True)
                    ex = jnp.exp(lg - m)
                    den = jnp.sum(ex, axis=-1, keepdims=True)
                    ps.append(ex * (1.0 / den))
                    lse = m + jnp.log(den)
                    for j in range(4):
                        stats = jnp.where(lane == GROUP * kh + 2 * j + e, lse[BLK * j:BLK * (j + 1)], stats)
                out = _nn(jnp.concatenate(ps, axis=1).astype(BF16), v2[kh])
                for j in range(4):
                    att_ref[rows, 128 * (4 * kh + j):128 * (4 * kh + j + 1)] = out[BLK * j:BLK * (j + 1)].astype(BF16)
            stats_ref[rows, :] = stats

    two = lambda m: (m, 0)
    table = lambda pick: pl.BlockSpec((1, N_KV, 4 * BLK, 4 * BLK), lambda m: (pick(m), 0, 0, 0))
    return pl.pallas_call(
        body, name="attn_fwd", grid=(s // (2 * BLK),),
        in_specs=[pl.BlockSpec((2 * BLK, D), two)]
        + [pl.BlockSpec((BLK, KV_W), lambda m, col=col, off=off: (jnp.maximum(2 * m + off, 0), col))
           for col in (0, 1) for off in (-1, 0, 1)]
        + [table(lambda m: jnp.minimum(m, 1)), table(lambda m: 1)],
        out_specs=[pl.BlockSpec((2 * BLK, D), two), pl.BlockSpec((2 * BLK, 128), two)],
        out_shape=[SDS((s, D), BF16), SDS((s, 128), F32)],
        compiler_params=_params(("parallel",)),
    )(q, kv, kv, kv, kv, kv, kv, tab, tab)


def _mid(att, zb, h1, tgt, w_out, g_post, tm):
    s = att.shape[0]
    nt = s // tm

    def body(att_ref, z_ref, h1_ref, t_ref, w_ref, g_ref,
             dh_ref, dqz_ref, datt_ref, loss_ref, dg_ref, dw_ref, dw_acc):
        @pl.when(pl.program_id(0) == 0)
        def _():
            loss_ref[...] = jnp.zeros_like(loss_ref)
            dg_ref[...] = jnp.zeros_like(dg_ref)
            dw_acc[...] = jnp.zeros_like(dw_acc)
        att = att_ref[...].astype(F32)
        z = z_ref[...].astype(F32)
        sg, sz = _silu_parts(z)
        ob = (att * sz).astype(BF16)
        y2 = _nn(ob, w_ref[...])
        r2 = _rms_scale(y2)
        yh = y2 * r2
        g = g_ref[...]
        err = (h1_ref[...] + yh * g) - t_ref[...]
        loss_ref[...] += jnp.sum(jnp.sum(err * err, axis=-1, keepdims=True) / D)
        dh = err / D
        dh_ref[...] = dh
        _acc_row(dg_ref, 0, jnp.sum(dh * yh, axis=0, keepdims=True))
        dyh = dh * g
        dy = (r2 * (dyh - yh * jnp.mean(dyh * yh, axis=-1, keepdims=True))).astype(BF16)
        dw_acc[...] += _tn(ob, dy)
        dob = _nt(dy, w_ref[...])
        datt_ref[...] = (dob * sz).astype(BF16)
        dqz_ref[...] = (dob * att * _dsilu(z, sg)).astype(BF16)

        @pl.when(pl.program_id(0) == nt - 1)
        def _():
            pltpu.sync_copy(dw_acc, dw_ref)

    row = lambda i: (i, 0)
    fix = lambda i: (0, 0)
    return pl.pallas_call(
        body, name="mid", grid=(nt,),
        in_specs=[pl.BlockSpec((tm, D), row)] * 4 + [pl.BlockSpec((D, D), fix), pl.BlockSpec((1, D), fix)],
        out_specs=[pl.BlockSpec((tm, D), row), pl.BlockSpec((tm, D), lambda i: (i, 1)), pl.BlockSpec((tm, D), row),
                   pl.BlockSpec((8, 128), fix), pl.BlockSpec((8, D), fix), pl.BlockSpec(memory_space=pl.ANY)],
        out_shape=[SDS((s, D), F32), SDS((s, 2 * D), BF16), SDS((s, D), BF16), SDS((8, 128), F32),
                   SDS((8, D), F32), SDS((D, D), F32)],
        scratch_shapes=[pltpu.VMEM((D, D), F32)],
        compiler_params=_params(("arbitrary",)),
    )(att, zb, h1, tgt, w_out, g_post)


def _attn_bwd(q, kv, datt, stats, tab, dqz):
    s = q.shape[0]
    nb = s // BLK

    def body(q_ref, kp_ref, kc_ref, vp_ref, vc_ref, da_ref, st_ref, tab_ref, dqz_in,
             dq_ref, dkv_ref, dtab_ref, dk_carry, dv_carry):
        del dqz_in
        n = pl.program_id(0)

        @pl.when(n == 0)
        def _():
            dtab_ref[...] = jnp.zeros_like(dtab_ref)
            dk_carry[...] = jnp.zeros_like(dk_carry)
            dv_carry[...] = jnp.zeros_like(dv_carry)

        @pl.when(n < nb)
        def _():
            k2 = _pair_operands(kp_ref[...], kc_ref[...])
            v2 = _pair_operands(vp_ref[...], vc_ref[...])
            lane = lax.broadcasted_iota(jnp.int32, (BLK, 128), 1)
            stats = st_ref[...]
            dk2, dv2 = [], []
            for kh in range(N_KV):
                qs = _stack_pairs(q_ref, kh)
                das = _stack_pairs(da_ref, kh)
                sc = _nt(qs, k2[kh])
                dp = _nt(das, v2[kh])
                ps, dss = [], []
                for e in range(2):
                    heads = [GROUP * kh + 2 * j + e for j in range(4)]
                    lse = jnp.concatenate([jnp.sum(jnp.where(lane == h, stats, 0.0), axis=-1, keepdims=True)
                                           for h in heads], axis=0)
                    cols = slice(256 * e, 256 * (e + 1))
                    p = jnp.exp(sc[:, cols] + tab_ref[0, kh, :, cols] - lse)
                    delta = jnp.sum(p * dp[:, cols], axis=-1, keepdims=True)
                    ds = p * (dp[:, cols] - delta)
                    dtab_ref[kh, :, cols] += ds
                    ps.append(p)
                    dss.append(ds)
                p2 = jnp.concatenate(ps, axis=1).astype(BF16)
                ds2 = jnp.concatenate(dss, axis=1).astype(BF16)
                dq = _nn(ds2, k2[kh]) * Q_SCALE
                for j in range(4):
                    dq_ref[:, 128 * (4 * kh + j):128 * (4 * kh + j + 1)] = dq[BLK * j:BLK * (j + 1)].astype(BF16)
                dk2.append(_tn(ds2, qs))
                dv2.append(_tn(p2, das))
            dkk = _pair_fold(dk2[0], dk2[1])
            dvv = _pair_fold(dv2[0], dv2[1])
            dkv_ref[:, 0:KV_W] = (dk_carry[...] + dkk[0:BLK]).astype(BF16)
            dkv_ref[:, KV_W:2 * KV_W] = (dv_carry[...] + dvv[0:BLK]).astype(BF16)
            dk_carry[...] = dkk[BLK:2 * BLK]
            dv_carry[...] = dvv[BLK:2 * BLK]

        @pl.when(n == nb)
        def _():
            dkv_ref[:, 0:KV_W] = dk_carry[...].astype(BF16)
            dkv_ref[:, KV_W:2 * KV_W] = dv_carry[...].astype(BF16)

    cur = lambda n: (jnp.minimum(n, nb - 1), 0)
    prev = lambda n: (jnp.clip(n - 1, 0, nb - 1), 0)
    return pl.pallas_call(
        body, name="attn_bwd", grid=(nb + 1,),
        in_specs=[pl.BlockSpec((BLK, D), cur),
                  pl.BlockSpec((BLK, KV_W), prev), pl.BlockSpec((BLK, KV_W), cur),
                  pl.BlockSpec((BLK, KV_W), lambda n: (jnp.clip(n - 1, 0, nb - 1), 1)),
                  pl.BlockSpec((BLK, KV_W), lambda n: (jnp.minimum(n, nb - 1), 1)),
                  pl.BlockSpec((BLK, D), cur), pl.BlockSpec((BLK, 128), cur), _table_spec(),
                  pl.BlockSpec(memory_space=pl.ANY)],
        out_specs=[pl.BlockSpec((BLK, D), cur), pl.BlockSpec((BLK, 2 * KV_W), prev),
                   pl.BlockSpec((N_KV, 4 * BLK, 4 * BLK), lambda n: (0, 0, 0))],
        out_shape=[SDS((s, 2 * D), BF16), SDS((s, 2 * KV_W), BF16), SDS((N_KV, 4 * BLK, 4 * BLK), F32)],
        scratch_shapes=[pltpu.VMEM((BLK, KV_W), F32), pltpu.VMEM((BLK, KV_W), F32)],
        input_output_aliases={8: 0},
        compiler_params=_params(("arbitrary",)),
    )(q, kv, kv, kv, kv, datt, stats, tab, dqz)


def _b_bwd(dqz, dkv, h1, dh2, oa, wbin_g, w_kv, g_kv, g_pre, g_apost, tm):
    s = h1.shape[0]
    nt = s // tm

    def body(dqz_ref, dkv_ref, h_ref, dh2_ref, oa_ref, wb_ref, wkv_ref, gk_ref, gb_ref, ga_ref,
             dh1_ref, doa_ref, dg_ref, dwb_ref, dwkv_ref, dwb_acc, dwkv_acc):
        @pl.when(pl.program_id(0) == 0)
        def _():
            dg_ref[...] = jnp.zeros_like(dg_ref)
            dwb_acc[...] = jnp.zeros_like(dwb_acc)
            dwkv_acc[...] = jnp.zeros_like(dwkv_acc)
        dnb = _nt(dqz_ref[:, 0:512], wb_ref[0])
        for j in range(1, 4):
            dnb = dnb + _nt(dqz_ref[:, 512 * j:512 * (j + 1)], wb_ref[j])
        dnk = _nt(dkv_ref[...], wkv_ref[...])
        h = h_ref[...]
        r = _rms_scale(h)
        hh = h * r
        nb = (hh * gb_ref[...]).astype(BF16)
        for j in range(4):
            dwb_acc[j] += _tn(nb, dqz_ref[:, 512 * j:512 * (j + 1)])
        dwkv_acc[...] += _tn((hh * gk_ref[...]).astype(BF16), dkv_ref[...])
        _acc_row(dg_ref, 0, jnp.sum(dnk * hh, axis=0, keepdims=True))
        _acc_row(dg_ref, 1, jnp.sum(dnb * hh, axis=0, keepdims=True))
        dhh = dnb * gb_ref[...] + dnk * gk_ref[...]
        dh1 = dh2_ref[...] + r * (dhh - hh * jnp.mean(dhh * hh, axis=-1, keepdims=True))
        dh1_ref[...] = dh1
        oa = oa_ref[...].astype(F32)
        ra = _rms_scale(oa)
        oh = oa * ra
        _acc_row(dg_ref, 2, jnp.sum(dh1 * oh, axis=0, keepdims=True))
        doh = dh1 * ga_ref[...]
        doa_ref[...] = (ra * (doh - oh * jnp.mean(doh * oh, axis=-1, keepdims=True))).astype(BF16)

        @pl.when(pl.program_id(0) == nt - 1)
        def _():
            pltpu.sync_copy(dwb_acc, dwb_ref)
            pltpu.sync_copy(dwkv_acc, dwkv_ref)

    row = lambda i: (i, 0)
    fix = lambda i: (0, 0)
    anyspace = pl.BlockSpec(memory_space=pl.ANY)
    return pl.pallas_call(
        body, name="b_bwd", grid=(nt,),
        in_specs=[pl.BlockSpec((tm, 2 * D), row), pl.BlockSpec((tm, 2 * KV_W), row), pl.BlockSpec((tm, D), row),
                  pl.BlockSpec((tm, D), row), pl.BlockSpec((tm, D), row),
                  pl.BlockSpec((4, D, 512), lambda i: (0, 0, 0)), pl.BlockSpec((D, 2 * KV_W), fix),
                  pl.BlockSpec((1, D), fix), pl.BlockSpec((1, D), fix), pl.BlockSpec((1, D), fix)],
        out_specs=[pl.BlockSpec((tm, D), row), pl.BlockSpec((tm, D), row), pl.BlockSpec((8, D), fix), anyspace, anyspace],
        out_shape=[SDS((s, D), F32), SDS((s, D), BF16), SDS((8, D), F32), SDS((4, D, 512), F32),
                   SDS((D, 2 * KV_W), F32)],
        scratch_shapes=[pltpu.VMEM((4, D, 512), F32), pltpu.VMEM((D, 2 * KV_W), F32)],
        compiler_params=_params(("arbitrary",)),
    )(dqz, dkv, h1, dh2, oa, wbin_g, w_kv, g_kv, g_pre, g_apost)


def _chip_exchange(parts, recvs, send, recv):
    x, y, c = lax.axis_index("x"), lax.axis_index("y"), lax.axis_index("c")
    chips = [(x, 1 - y), (1 - x, y), (1 - x, 1 - y)]
    copies = []
    for a, (t, r) in enumerate(zip(parts, recvs)):
        for j, (px, py) in enumerate(chips):
            copies.append(pltpu.make_async_remote_copy(
                src_ref=t.at[2 * px + py], dst_ref=r.at[j], send_sem=send.at[3 * a + j],
                recv_sem=recv.at[3 * a + j], device_id=(px, py, c), device_id_type=MESH))
    return copies


def _exchange_specs(parts):
    anyspace = pl.BlockSpec(memory_space=pl.ANY)
    n = len(parts)
    return ([anyspace] * n, [anyspace] * n, [SDS((3,) + t.shape[1:], t.dtype) for t in parts],
            [pltpu.SemaphoreType.DMA((3 * n,)), pltpu.SemaphoreType.DMA((3 * n,))])


def _a_bwd(doa, ya, proj, conv_w, w_out, tm, parts):
    s = doa.shape[0]
    nt = s // tm
    n = len(parts)
    ex_in, ex_out, ex_shape, ex_sems = _exchange_specs(parts)

    def body(*refs):
        doa_ref, ya_ref, proj_ref, halo_ref, cw_ref, w_ref = refs[:6]
        part_refs = refs[6:6 + n]
        dproj_ref, dcw_ref, dw_ref = refs[6 + n:9 + n]
        recv_refs = refs[9 + n:9 + 2 * n]
        carry, dw_acc, send, recv = refs[9 + 2 * n:]
        i = pl.program_id(0)
        r = nt - 1 - i

        @pl.when(i == 0)
        def _():
            dcw_ref[...] = jnp.zeros_like(dcw_ref)
            carry[...] = jnp.zeros_like(carry)
            dw_acc[...] = jnp.zeros_like(dw_acc)
            for cp in _chip_exchange(part_refs, recv_refs, send, recv):
                cp.start()
        dya = _nt(doa_ref[...], w_ref[...])
        dw_acc[...] += _tn(ya_ref[...], doa_ref[...])
        bg = proj_ref[:, 0:D].astype(F32)
        cg = proj_ref[:, D:2 * D].astype(F32)
        u = proj_ref[:, 2 * D:3 * D].astype(F32)
        z = proj_ref[:, 3 * D:4 * D].astype(F32)
        v = cg * u
        before = jnp.where(r > 0, halo_ref[:, D:2 * D].astype(F32) * halo_ref[:, 2 * D:3 * D].astype(F32), 0.0)
        rows = lax.broadcasted_iota(jnp.int32, (tm, D), 0)
        v1, v2 = _shift_rows(v, before[HALO - 1:HALO, :], before[HALO - 2:HALO - 1, :], rows)
        conv = cw_ref[0:1, :] * v2 + cw_ref[1:2, :] * v1 + cw_ref[2:3, :] * v
        sg, sz = _silu_parts(z)
        dproj_ref[:, 0:D] = (dya * conv * sz).astype(BF16)
        dproj_ref[:, 3 * D:4 * D] = (dya * bg * conv * _dsilu(z, sg)).astype(BF16)
        dconv = dya * bg * sz
        _acc_row(dcw_ref, 0, jnp.sum(dconv * v2, axis=0, keepdims=True))
        _acc_row(dcw_ref, 1, jnp.sum(dconv * v1, axis=0, keepdims=True))
        _acc_row(dcw_ref, 2, jnp.sum(dconv * v, axis=0, keepdims=True))
        after = carry[...]
        up1 = jnp.where(rows < tm - 1, pltpu.roll(dconv, tm - 1, 0), after[0:1, :])
        up2 = jnp.where(rows < tm - 2, pltpu.roll(dconv, tm - 2, 0),
                        jnp.where(rows == tm - 2, after[0:1, :], after[1:2, :]))
        carry[...] = dconv[0:8, :]
        dv = cw_ref[2:3, :] * dconv + cw_ref[1:2, :] * up1 + cw_ref[0:1, :] * up2
        dproj_ref[:, D:2 * D] = (dv * u).astype(BF16)
        dproj_ref[:, 2 * D:3 * D] = (dv * cg).astype(BF16)

        @pl.when(i == nt - 1)
        def _():
            pltpu.sync_copy(dw_acc, dw_ref)
            for cp in _chip_exchange(part_refs, recv_refs, send, recv):
                cp.wait()

    rev = lambda i: (nt - 1 - i, 0)
    fix = lambda i: (0, 0)
    halo = lambda i: (jnp.maximum((nt - 1 - i) * (tm // HALO) - 1, 0), 0)
    dproj, dcw, dw, *got = pl.pallas_call(
        body, name="a_bwd", grid=(nt,),
        in_specs=[pl.BlockSpec((tm, D), rev), pl.BlockSpec((tm, D), rev), pl.BlockSpec((tm, 4 * D), rev),
                  pl.BlockSpec((HALO, 4 * D), halo), pl.BlockSpec((8, D), fix), pl.BlockSpec((D, D), fix)] + ex_in,
        out_specs=[pl.BlockSpec((tm, 4 * D), rev), pl.BlockSpec((8, D), fix), pl.BlockSpec(memory_space=pl.ANY)] + ex_out,
        out_shape=[SDS((s, 4 * D), BF16), SDS((8, D), F32), SDS((D, D), F32)] + ex_shape,
        scratch_shapes=[pltpu.VMEM((8, D), F32), pltpu.VMEM((D, D), F32)] + ex_sems,
        compiler_params=_params(("arbitrary",)),
    )(doa, ya, proj, proj, conv_w, w_out, *parts)
    return dproj, dcw, dw, got


def _dn1(dp_ref, w_ref):
    dn = _nt(dp_ref[:, 0:D], w_ref[0])
    for j in range(1, 4):
        dn = dn + _nt(dp_ref[:, D * j:D * (j + 1)], w_ref[j])
    return dn


def _a_in_bwd_matmul(dproj, win_g, tm, count, parts):
    n = len(parts)
    ex_in, ex_out, ex_shape, ex_sems = _exchange_specs(parts)

    def body(*refs):
        dp_ref, w_ref = refs[:2]
        part_refs = refs[2:2 + n]
        dn_ref = refs[2 + n]
        recv_refs = refs[3 + n:3 + 2 * n]
        sems = refs[3 + 2 * n:]

        @pl.when(pl.program_id(0) == 0)
        def _():
            for cp in _chip_exchange(part_refs, recv_refs, *sems):
                cp.start()
        dn_ref[...] = _dn1(dp_ref, w_ref)

        @pl.when(pl.program_id(0) == count - 1)
        def _():
            for cp in _chip_exchange(part_refs, recv_refs, *sems):
                cp.wait()

    row = lambda i: (i, 0)
    dn, *got = pl.pallas_call(
        body, name="a_in_bwd_matmul", grid=(count,),
        in_specs=[pl.BlockSpec((tm, 4 * D), row), pl.BlockSpec((4, D, D), lambda i: (0, 0, 0))] + ex_in,
        out_specs=[pl.BlockSpec((tm, D), row)] + ex_out,
        out_shape=[SDS((count * tm, D), F32)] + ex_shape,
        scratch_shapes=ex_sems,
        compiler_params=_params(("arbitrary",)),
    )(dproj, win_g, *parts)
    return dn, got


def _a_in_bwd(dn_first, dproj, x, dh1, win_g, g_pre, tm):
    s = x.shape[0]
    nt = s // tm
    count = dn_first.shape[0] // tm

    def body(dn_ref, dp_ref, x_ref, dh_ref, w_ref, g_ref, gx_ref, dg_ref, dn_s):
        i = pl.program_id(0)

        @pl.when(i == 0)
        def _():
            dg_ref[...] = jnp.zeros_like(dg_ref)

        @pl.when(i < count)
        def _():
            dn_s[...] = dn_ref[...]

        @pl.when(i >= count)
        def _():
            dn_s[...] = _dn1(dp_ref, w_ref)
        dn = dn_s[...]
        xv = x_ref[...]
        r = _rms_scale(xv)
        xh = xv * r
        _acc_row(dg_ref, 0, jnp.sum(dn * xh, axis=0, keepdims=True))
        dxh = dn * g_ref[...]
        gx_ref[...] = dh_ref[...] + r * (dxh - xh * jnp.mean(dxh * xh, axis=-1, keepdims=True))

    row = lambda i: (i, 0)
    fix = lambda i: (0, 0)
    return pl.pallas_call(
        body, name="a_in_bwd", grid=(nt,),
        in_specs=[pl.BlockSpec((tm, D), lambda i: (jnp.minimum(i, count - 1), 0)),
                  pl.BlockSpec((tm, 4 * D), lambda i: (jnp.maximum(i, count), 0)),
                  pl.BlockSpec((tm, D), row), pl.BlockSpec((tm, D), row),
                  pl.BlockSpec((4, D, D), lambda i: (0, 0, 0)), pl.BlockSpec((1, D), fix)],
        out_specs=[pl.BlockSpec((tm, D), row), pl.BlockSpec((8, D), fix)],
        out_shape=[SDS((s, D), F32), SDS((8, D), F32)],
        scratch_shapes=[pltpu.VMEM((tm, D), F32)],
        compiler_params=_params(("arbitrary",)),
    )(dn_first, dproj, x, dh1, win_g, g_pre)


def _dw(a, b, tn, tmw, name):
    s, k = a.shape
    n = b.shape[1]

    def body(a_ref, b_ref, o_ref):
        @pl.when(pl.program_id(1) == 0)
        def _():
            o_ref[...] = jnp.zeros_like(o_ref)
        o_ref[0] += _tn(a_ref[...], b_ref[...])

    return pl.pallas_call(
        body, name=name, grid=(n // tn, s // tmw),
        in_specs=[pl.BlockSpec((tmw, k), lambda j, t: (t, 0)), pl.BlockSpec((tmw, tn), lambda j, t: (t, j))],
        out_specs=pl.BlockSpec((1, k, tn), lambda j, t: (j, 0, 0)),
        out_shape=SDS((n // tn, k, tn), F32),
        compiler_params=_params(("parallel", "arbitrary")),
    )(a, b)


def _sibling_exchange(name, to_sibling=(), shards=(), smalls=None):
    n_g, n_h = len(to_sibling), len(shards)
    has_small = smalls is not None

    def body(*refs):
        gs = refs[:n_g]
        pos = n_g + n_h
        small_in = refs[pos] if has_small else None
        pos += has_small
        rs, fs = refs[pos:pos + n_g], refs[pos + n_g:pos + n_g + n_h]
        pos += n_g + n_h
        small_all = refs[pos] if has_small else None
        pos += has_small
        dsend, drecv, ssend, srecv = refs[pos:]
        x, y, c = lax.axis_index("x"), lax.axis_index("y"), lax.axis_index("c")
        sibling = (x, y, 1 - c)
        sends, arrivals = [], []
        for a, (g, r) in enumerate(zip(gs, rs)):
            h = g.shape[1] // 2
            src = g.at[:, pl.ds(pl.multiple_of((1 - c) * h, 8), h), :]
            sends.append(pltpu.make_async_remote_copy(src_ref=src, dst_ref=r, send_sem=dsend.at[a], recv_sem=drecv.at[a],
                                                      device_id=sibling, device_id_type=MESH))
            arrivals.append(pltpu.make_async_remote_copy(src_ref=r, dst_ref=r, send_sem=dsend.at[a], recv_sem=drecv.at[a],
                                                         device_id=sibling, device_id_type=MESH))
        for b, full in enumerate(fs):
            h = full.shape[0] // 2
            mine = full.at[pl.ds(pl.multiple_of(c * h, 8), h)]
            theirs = full.at[pl.ds(pl.multiple_of((1 - c) * h, 8), h)]
            sends.append(pltpu.make_async_remote_copy(src_ref=mine, dst_ref=mine, send_sem=dsend.at[n_g + b],
                                                      recv_sem=drecv.at[n_g + b], device_id=sibling, device_id_type=MESH))
            arrivals.append(pltpu.make_async_remote_copy(src_ref=mine, dst_ref=theirs, send_sem=dsend.at[n_g + b],
                                                         recv_sem=drecv.at[n_g + b], device_id=sibling, device_id_type=MESH))
        if has_small:
            me = 4 * x + 2 * y + c
            small_all[me] = small_in[...]
            for rel in range(1, N_DEV):
                fx, fy, fc = rel >> 2, (rel >> 1) & 1, rel & 1
                peer = (x + fx - 2 * x * fx, y + fy - 2 * y * fy, c + fc - 2 * c * fc)
                sender = 4 * peer[0] + 2 * peer[1] + peer[2]
                sends.append(pltpu.make_async_remote_copy(
                    src_ref=small_in, dst_ref=small_all.at[me], send_sem=ssend.at[rel - 1], recv_sem=srecv.at[rel - 1],
                    device_id=peer, device_id_type=MESH))
                arrivals.append(pltpu.make_async_remote_copy(
                    src_ref=small_in, dst_ref=small_all.at[sender], send_sem=ssend.at[rel - 1], recv_sem=srecv.at[rel - 1],
                    device_id=peer, device_id_type=MESH))
        for cp in sends:
            cp.start()
        for cp in arrivals:
            cp.wait_recv()
        for cp in sends:
            cp.wait_send()

    anyspace = pl.BlockSpec(memory_space=pl.ANY)
    vm = pl.BlockSpec(memory_space=pltpu.VMEM)
    out_shape = [SDS((N_CHIPS, g.shape[1] // 2, g.shape[2]), F32) for g in to_sibling]
    out_shape += [SDS(full.shape, F32) for full in shards]
    if has_small:
        out_shape.append(SDS((N_DEV,) + smalls.shape, F32))
    n_d2d = max(n_g + n_h, 1)
    outs = pl.pallas_call(
        body, name=name, out_shape=out_shape,
        in_specs=[anyspace] * (n_g + n_h) + [vm] * has_small, out_specs=[anyspace] * (n_g + n_h) + [vm] * has_small,
        scratch_shapes=[pltpu.SemaphoreType.DMA((n_d2d,)), pltpu.SemaphoreType.DMA((n_d2d,)),
                        pltpu.SemaphoreType.DMA((N_DEV - 1,)), pltpu.SemaphoreType.DMA((N_DEV - 1,))],
        input_output_aliases={n_g + b: n_g + b for b in range(n_h)},
    )(*to_sibling, *shards, *([smalls] if has_small else []))
    return outs[:n_g], outs[n_g:n_g + n_h], (outs[n_g + n_h] if has_small else None)


def _add_sibling(where, g, r, name):
    _, rows, cols = g.shape
    h = rows // 2
    tr = min(h, 256)
    nh = h // tr

    def body(where_ref, g_ref, r_ref, t_ref, own_ref):
        t = g_ref[0] + r_ref[0]
        t_ref[0] = t.astype(BF16)

        @pl.when(pl.program_id(1) == where_ref[1])
        def _():
            own_ref[...] = t

    return pl.pallas_call(
        body, name=name,
        grid_spec=pltpu.PrefetchScalarGridSpec(
            num_scalar_prefetch=1, grid=(nh, N_CHIPS),
            in_specs=[pl.BlockSpec((1, tr, cols), lambda i, k, w: (k, w[0] * nh + i, 0)),
                      pl.BlockSpec((1, tr, cols), lambda i, k, w: (k, i, 0))],
            out_specs=[pl.BlockSpec((1, tr, cols), lambda i, k, w: (k, i, 0)),
                       pl.BlockSpec((tr, cols), lambda i, k, w: (i, 0))]),
        out_shape=[SDS((N_CHIPS, h, cols), BF16), SDS((h, cols), F32)],
        compiler_params=_params(("parallel", "arbitrary")),
    )(where, g, r)


def _add_chips(where, own, r, name):
    h, cols = own.shape
    tr = min(h, 256)
    nh = h // tr

    def body(where_ref, t_ref, r_ref, o_ref):
        del where_ref
        o_ref[...] = ((t_ref[...] + r_ref[0].astype(F32)) + r_ref[1].astype(F32)) + r_ref[2].astype(F32)

    return pl.pallas_call(
        body, name=name,
        grid_spec=pltpu.PrefetchScalarGridSpec(
            num_scalar_prefetch=1, grid=(nh,),
            in_specs=[pl.BlockSpec((tr, cols), lambda i, w: (i, 0)), pl.BlockSpec((3, tr, cols), lambda i, w: (0, i, 0))],
            out_specs=pl.BlockSpec((tr, cols), lambda i, w: (w[0] * nh + i, 0))),
        out_shape=SDS((2 * h, cols), F32),
        compiler_params=_params(("parallel",)),
    )(where, own, r)


def _sum_smalls(small_all):
    def body(all_ref, o_ref):
        acc = all_ref[0]
        for dev in range(1, N_DEV):
            acc = acc + all_ref[dev]
        o_ref[...] = acc

    return pl.pallas_call(
        body, name="sum_smalls", out_shape=SDS(small_all.shape[1:], F32),
        in_specs=[pl.BlockSpec(memory_space=pltpu.VMEM)], out_specs=pl.BlockSpec(memory_space=pltpu.VMEM),
    )(small_all)


def _adam_step(g, w, m, v):
    nm = ADAM_B1 * m + (1.0 - ADAM_B1) * g
    nv = ADAM_B2 * v + (1.0 - ADAM_B2) * (g * g)
    m_hat = nm / (1.0 - ADAM_B1 ** ADAM_STEP)
    v_hat = nv / (1.0 - ADAM_B2 ** ADAM_STEP)
    return -ADAM_LR * (m_hat / (jnp.sqrt(v_hat) + ADAM_EPS) + ADAM_WD * w), nm, nv


def _adamw(g, w, m, v, name):
    rows, cols = g.shape
    tr = min(rows, 256)

    def body(g_ref, w_ref, m_ref, v_ref, d_ref, nm_ref, nv_ref):
        d_ref[...], nm_ref[...], nv_ref[...] = _adam_step(g_ref[...], w_ref[...], m_ref[...], v_ref[...])

    spec = pl.BlockSpec((tr, cols), lambda i: (i, 0))
    return pl.pallas_call(
        body, name=name, grid=(rows // tr,), in_specs=[spec] * 4, out_specs=[spec] * 3,
        out_shape=[SDS(g.shape, F32)] * 3, compiler_params=_params(("parallel",)),
    )(g, w, m, v)


def _small_update(chip, tot, wmv):
    names = list(SMALL_PLACES)
    n = len(names)

    def body(chip_ref, tot_ref, quarter_ref, *refs):
        del chip_ref
        ins, outs = refs[:3 * n], refs[3 * n:]
        for i, nm in enumerate(names):
            sharded, row, (rows, cols) = SMALL_PLACES[nm]
            g = (quarter_ref if sharded else tot_ref)[row:row + rows, 0:cols]
            outs[4 * i][...] = g
            outs[4 * i + 1][...], outs[4 * i + 2][...], outs[4 * i + 3][...] = _adam_step(
                g, ins[3 * i][...], ins[3 * i + 1][...], ins[3 * i + 2][...])

    whole = lambda shape: pl.BlockSpec(shape, lambda i, c: (0,) * len(shape))
    shapes = [SMALL_PLACES[nm][2] for nm in names]
    outs = pl.pallas_call(
        body, name="small_update",
        grid_spec=pltpu.PrefetchScalarGridSpec(
            num_scalar_prefetch=1, grid=(1,),
            in_specs=[whole(tot.shape), pl.BlockSpec((tot.shape[0], D // 4), lambda i, c: (0, c[0]))]
            + [whole(shp) for shp in shapes for _ in range(3)],
            out_specs=[whole(shp) for shp in shapes for _ in range(4)]),
        out_shape=[SDS(shp, F32) for shp in shapes for _ in range(4)],
    )(chip, tot, tot, *[a for nm in names for a in wmv[nm]])
    return {nm: tuple(outs[4 * i:4 * i + 4]) for i, nm in enumerate(names)}


def _pad_rows(a, rows):
    return jnp.concatenate([a, jnp.zeros((rows - a.shape[0], a.shape[1]), a.dtype)], axis=0)


def _pad_cols(a, cols):
    return jnp.concatenate([a, jnp.zeros((a.shape[0], cols - a.shape[1]), a.dtype)], axis=1)


def kernel(x, a_pre_norm, a_w_in, a_conv_w, a_w_out, a_post_norm, kv_norm, w_kv, rel_bias, b_pre_norm, b_w_in, b_sinks, b_w_out, b_post_norm, loss_target, m_a_pre_norm, m_a_w_in, m_a_conv_w, m_a_w_out, m_a_post_norm, m_kv_norm, m_w_kv, m_rel_bias, m_b_pre_norm, m_b_w_in, m_b_sinks, m_b_w_out, m_b_post_norm, v_a_pre_norm, v_a_w_in, v_a_conv_w, v_a_w_out, v_a_post_norm, v_kv_norm, v_w_kv, v_rel_bias, v_b_pre_norm, v_b_w_in, v_b_sinks, v_b_w_out, v_b_post_norm):
    seq = x.shape[1]
    xs = x.reshape(seq, D)
    tgt = loss_target.reshape(seq, D)
    chip = 2 * lax.axis_index("x") + lax.axis_index("y")
    core = lax.axis_index("c")
    tm = _tile(seq, 512)
    tm_mix = _tile(seq, 256)
    tmw = _tile(seq, 1024)

    shards = [a_w_in[0], a_w_out[0], w_kv, b_w_in[0], b_w_out[0]]
    small_w = _pad_rows(jnp.concatenate([a_pre_norm, a_conv_w[0], a_post_norm], axis=0), 8)
    *own_only, small_g = _gather_weights(shards, small_w, 0)
    where = jnp.stack([core, chip]).astype(jnp.int32)
    small_full = small_g.transpose(1, 0, 2).reshape(8, D)
    g_apre, conv_w, g_apost = small_full[0:1], _pad_rows(small_full[1:4], 8), small_full[4:5]
    g_kv = kv_norm.reshape(1, D)

    proj, n1, (win_g, wouta_g, wkv_g, wbin_g, woutb_g) = _a_in(where[1:2], xs, g_apre, own_only, tmw)
    wouta = wouta_g.reshape(D, D)
    wkv = wkv_g.reshape(D, 2 * KV_W)
    woutb = woutb_g.reshape(D, D)
    ya, oa, h1 = _a_mix(proj, xs, conv_w, wouta, g_apost, tm_mix)
    kv, q, zb = _b_in(h1, g_kv, b_pre_norm, wkv, wbin_g, tm)
    tab = _bias_table(rel_bias, b_sinks.reshape(N_HEADS))
    att, stats = _attn_fwd(q, kv, tab)
    dh2, dqz, datt, loss_acc, dg_bpost, dw_outb = _mid(att, zb, h1, tgt, woutb, b_post_norm, tm)

    dqz, dkv, dtab = _attn_bwd(q, kv, datt, stats, tab, dqz)
    dh1, doa, dg_b, dw_bin, dw_kv = _b_bwd(dqz, dkv, h1, dh2, oa, wbin_g, wkv, g_kv, b_pre_norm, g_apost, tm)
    dw_kv = dw_kv.reshape(N_CHIPS, D // 4, 2 * KV_W)
    dw_outb = dw_outb.reshape(N_CHIPS, D // 4, D)
    grads1 = [dw_kv, dw_bin, dw_outb]
    names1 = ["w_kv", "b_w_in", "b_w_out"]
    from_sibling1, _, _ = _sibling_exchange("to_sibling_1", to_sibling=grads1)
    sums1 = [_add_sibling(where, g, r, "add_sibling_" + nm) for g, r, nm in zip(grads1, from_sibling1, names1)]
    dproj, dconv_w, dw_outa, from_chips1 = _a_bwd(doa, ya, proj, conv_w, wouta, tm_mix, [t for t, _ in sums1])
    shards1 = [_add_chips(where, own, r, "add_chips_" + nm) for (_, own), r, nm in zip(sums1, from_chips1, names1)]
    dw_in = _dw(n1, dproj, D, tmw, "dw_a_in")
    grads2 = [dw_in, dw_outa.reshape(N_CHIPS, D // 4, D)]
    names2 = ["a_w_in", "a_w_out"]
    from_sibling2, (g_wkv, g_wbin, g_woutb), _ = _sibling_exchange("to_sibling_2", to_sibling=grads2, shards=shards1)
    sums2 = [_add_sibling(where, g, r, "add_sibling_" + nm) for g, r, nm in zip(grads2, from_sibling2, names2)]
    nt = seq // tm
    dn_first, from_chips2 = _a_in_bwd_matmul(dproj, win_g, tm, max(nt - max(nt // 4, 1), 1), [t for t, _ in sums2])
    grad_x, dg_apre = _a_in_bwd(dn_first, dproj, xs, dh1, win_g, g_apre, tm)
    shards2 = [_add_chips(where, own, r, "add_chips_" + nm) for (_, own), r, nm in zip(sums2, from_chips2, names2)]
    drel, dsink = _bias_fold(dtab)

    smalls = jnp.concatenate([
        dg_apre[0:1], dg_b[2:3], dg_b[0:1], dg_b[1:2], dg_bpost[0:1], _pad_cols(dsink[0:1], D),
        _pad_cols(loss_acc[0:1], D), jnp.zeros((1, D), F32), dconv_w, _pad_cols(drel, D)], axis=0)
    _, (g_win, g_wouta), small_all = _sibling_exchange("share_last", shards=shards2, smalls=smalls)
    tot = _sum_smalls(small_all)

    big = {}
    for nm, g, w, m, v in [("a_w_in", g_win, a_w_in, m_a_w_in, v_a_w_in), ("a_w_out", g_wouta, a_w_out, m_a_w_out, v_a_w_out),
                           ("w_kv", g_wkv, w_kv, m_w_kv, v_w_kv), ("b_w_in", g_wbin, b_w_in, m_b_w_in, v_b_w_in),
                           ("b_w_out", g_woutb, b_w_out, m_b_w_out, v_b_w_out)]:
        shp = w.shape
        two = (shp[-2], shp[-1])
        d, nm_, nv_ = _adamw(g, w.reshape(two), m.reshape(two), v.reshape(two), "adamw_" + nm)
        big[nm] = (g.reshape(shp), d.reshape(shp), nm_.reshape(shp), nv_.reshape(shp))

    given = {"a_pre_norm": (a_pre_norm, m_a_pre_norm, v_a_pre_norm), "a_conv_w": (a_conv_w, m_a_conv_w, v_a_conv_w),
             "a_post_norm": (a_post_norm, m_a_post_norm, v_a_post_norm), "kv_norm": (kv_norm, m_kv_norm, v_kv_norm),
             "rel_bias": (rel_bias, m_rel_bias, v_rel_bias), "b_pre_norm": (b_pre_norm, m_b_pre_norm, v_b_pre_norm),
             "b_sinks": (b_sinks, m_b_sinks, v_b_sinks), "b_post_norm": (b_post_norm, m_b_post_norm, v_b_post_norm)}
    small = _small_update(where[1:2], tot, {nm: tuple(a.reshape(SMALL_PLACES[nm][2]) for a in wmv)
                                            for nm, wmv in given.items()})
    order = ["a_pre_norm", "a_w_in", "a_conv_w", "a_w_out", "a_post_norm", "kv_norm", "w_kv", "rel_bias",
             "b_pre_norm", "b_w_in", "b_sinks", "b_w_out", "b_post_norm"]
    outs = []
    for which in range(4):
        for nm in order:
            outs.append(big[nm][which] if nm in big else small[nm][which].reshape(given[nm][0].shape))
    loss = 0.5 * tot[LOSS_ROW, 0]
    return (loss, grad_x.reshape(x.shape), *outs)
```

```python
import functools
import math

import jax
import jax.numpy as jnp
from jax import lax
from jax.experimental import pallas as pl
from jax.experimental.pallas import tpu as pltpu

F32 = jnp.float32
BF16 = jnp.bfloat16
MESH = pl.DeviceIdType.MESH
SDS = jax.ShapeDtypeStruct

D = 1024
HEAD_DIM = 64
N_HEADS = 16
N_KV = 2
GROUP = 8
KV_W = 128
BLK = 128
N_BUCKETS = 32
MAX_EXACT = 16
MAX_DISTANCE = 128
EPS = 1e-6
NEG_INF = -1e30
Q_SCALE = HEAD_DIM ** -0.5

ADAM_LR = 0.001
ADAM_B1 = 0.9
ADAM_B2 = 0.999
ADAM_EPS = 1e-08
ADAM_WD = 0.01
ADAM_STEP = 10

N_CHIPS = 4
N_DEV = 8
VMEM_LIMIT = 56 * 1024 * 1024
SMALL_ROWS = 48
LOSS_ROW = 6
SMALL_PLACES = {
    "a_pre_norm": (True, 0, (1, D // 4)), "a_conv_w": (True, 8, (3, D // 4)), "a_post_norm": (True, 1, (1, D // 4)),
    "kv_norm": (False, 2, (1, D)), "rel_bias": (False, 16, (N_BUCKETS, N_HEADS)), "b_pre_norm": (False, 3, (1, D)),
    "b_sinks": (False, 5, (1, N_HEADS)), "b_post_norm": (False, 4, (1, D)),
}
HALO = 16


def _bucket_thresholds():
    def bucket(d):
        big = MAX_EXACT + int(math.log(d / MAX_EXACT) / math.log(MAX_DISTANCE / MAX_EXACT)
                              * (N_BUCKETS - MAX_EXACT))
        return d if d < MAX_EXACT else min(big, N_BUCKETS - 1)
    out = []
    for b in range(MAX_EXACT + 1, N_BUCKETS):
        out.append(min(d for d in range(MAX_EXACT, MAX_DISTANCE) if bucket(d) >= b))
    return tuple(out)


BUCKET_THRESHOLDS = _bucket_thresholds()


def _params(semantics=None, vmem=VMEM_LIMIT):
    return pltpu.CompilerParams(dimension_semantics=semantics, vmem_limit_bytes=vmem)


def _tile(n, pref):
    return pref if n >= 2 * pref else max(n // 2, 8)


def _rms_scale(v):
    return lax.rsqrt(jnp.mean(v * v, axis=-1, keepdims=True) + EPS)


def _nt(a, b):
    return lax.dot_general(a, b, (((1,), (1,)), ((), ())), preferred_element_type=F32)


def _tn(a, b):
    return lax.dot_general(a, b, (((0,), (0,)), ((), ())), preferred_element_type=F32)


def _nn(a, b):
    return jnp.dot(a, b, preferred_element_type=F32)


def _silu_parts(z):
    sg = jax.nn.sigmoid(z)
    return sg, z * sg


def _dsilu(z, sg):
    return sg * (1.0 + z * (1.0 - sg))


def _acc_row(ref, row, val):
    ref[row:row + 1, :] += val


def _gather_copies(outs, splits, ici_send, ici_recv, d2d_send, d2d_recv):
    x, y, c = lax.axis_index("x"), lax.axis_index("y"), lax.axis_index("c")
    k = 2 * x + y
    sibling = (x, y, 1 - c)

    def part(o_ref, chip, core, split):
        if not split:
            return o_ref.at[chip]
        h = o_ref.shape[1] // 2
        return o_ref.at[chip, pl.ds(pl.multiple_of(core * h, 16), h)]

    def remote(ref, a, j, sems, to):
        return pltpu.make_async_remote_copy(src_ref=ref, dst_ref=ref, send_sem=sems[0].at[3 * a + j],
                                            recv_sem=sems[1].at[3 * a + j], device_id=to, device_id_type=MESH)

    copies = []
    for a, (o_ref, split) in enumerate(zip(outs, splits)):
        for j, (px, py) in enumerate([(x, 1 - y), (1 - x, y), (1 - x, 1 - y)]):
            kj = 2 * px + py
            ici, d2d = (ici_send, ici_recv), (d2d_send, d2d_recv)
            copies.append((remote(part(o_ref, k, c, split), a, j, ici, (px, py, c)),
                           remote(part(o_ref, kj, c, split), a, j, ici, (px, py, c)),
                           remote(part(o_ref, kj, c, split), a, j, d2d, sibling) if split else None,
                           remote(part(o_ref, kj, 1 - c, split), a, j, d2d, sibling) if split else None))
    return copies


def _gather_sems(n):
    return [pltpu.SemaphoreType.DMA((3 * n,)) for _ in range(4)]


def _gather_weights(shards, small, n_now):
    n = len(shards)

    def body(*refs):
        ins, small_in = refs[:n], refs[n]
        outs, small_out = refs[n + 1:2 * n + 1], refs[2 * n + 1]
        sems = refs[2 * n + 2:]
        k = 2 * lax.axis_index("x") + lax.axis_index("y")
        for i_ref, o_ref in zip(ins, outs):
            o_ref[k] = i_ref[...].astype(BF16)
        small_out[k] = small_in[...]
        copies = _gather_copies(list(outs[:n_now]) + [small_out], [True] * n_now + [False], *sems)
        for send, _, _, _ in copies:
            send.start()
        for _, arrival, forward, _ in copies:
            arrival.wait_recv()
            if forward is not None:
                forward.start()
        for send, _, forward, forwarded in copies:
            if forward is not None:
                forwarded.wait_recv()
                forward.wait_send()
            send.wait_send()

    vm = pl.BlockSpec(memory_space=pltpu.VMEM)
    out_shape = [SDS((N_CHIPS,) + s.shape, BF16) for s in shards] + [SDS((N_CHIPS,) + small.shape, F32)]
    return pl.pallas_call(
        body, name="gather_weights", out_shape=out_shape,
        in_specs=[vm] * (n + 1), out_specs=[vm] * (n + 1),
        scratch_shapes=_gather_sems(n_now + 1),
        compiler_params=pltpu.CompilerParams(vmem_limit_bytes=VMEM_LIMIT),
    )(*shards, small)


def _a_in(chip, x, g_pre, weights, tm):
    s = x.shape[0]
    nt = s // tm
    n = len(weights)

    def body(chip_ref, x_ref, g_ref, *refs):
        proj_ref, n1_ref = refs[n:n + 2]
        gathered = refs[n + 2:2 * n + 2]
        wbuf, n1_all, fetch_sem = refs[2 * n + 2:2 * n + 5]
        sems = refs[2 * n + 5:]
        jj, i = pl.program_id(0), pl.program_id(1)
        copies = _gather_copies(gathered, [True] * n, *sems)

        def fetch(rel):
            slot = jnp.bitwise_xor(chip_ref[0], rel)
            return pltpu.make_async_copy(gathered[0].at[slot], wbuf.at[rel % 2], fetch_sem.at[rel % 2])

        @pl.when((jj == 0) & (i == 0))
        def _():
            fetch(0).start()
            copies[0][0].start()
            copies[1][0].start()
            fetch(0).wait()

        for rel in (1, 2, 3):
            @pl.when((jj == rel) & (i == 0))
            def _():
                fetch(rel).wait()

        @pl.when(jj == 0)
        def _():
            xv = x_ref[...]
            n1 = (xv * _rms_scale(xv) * g_ref[...]).astype(BF16)
            n1_ref[...] = n1
            n1_all[i] = n1
        proj_ref[...] = _nn(n1_all[i], wbuf[jj % 2]).astype(BF16)

        for rel in (1, 2, 3):
            @pl.when((jj == rel - 1) & (i == max(nt - 3, nt // 2)))
            def _():
                _, arrival, forward, forwarded = copies[rel - 1]
                arrival.wait_recv()
                forward.start()
                forwarded.wait_recv()
                fetch(rel).start()
                if rel == 1:
                    copies[2][0].start()
                if rel == 2:
                    for send, _, _, _ in copies[3:]:
                        send.start()

        @pl.when((jj == 3) & (i == max(nt - 2, 0)))
        def _():
            for _, arrival, forward, _ in copies[3:]:
                arrival.wait_recv()
                forward.start()

        @pl.when((jj == 3) & (i == nt - 1))
        def _():
            for _, _, _, forwarded in copies[3:]:
                forwarded.wait_recv()
            for send, _, forward, _ in copies:
                forward.wait_send()
                send.wait_send()

    anyspace = pl.BlockSpec(memory_space=pl.ANY)
    proj, n1, *gathered = pl.pallas_call(
        body, name="a_in",
        grid_spec=pltpu.PrefetchScalarGridSpec(
            num_scalar_prefetch=1, grid=(4, nt),
            in_specs=[pl.BlockSpec((tm, D), lambda jj, i, c: (jnp.where(jj == 0, i, nt - 1), 0)),
                      pl.BlockSpec((1, D), lambda jj, i, c: (0, 0))] + [anyspace] * n,
            out_specs=[pl.BlockSpec((tm, D), lambda jj, i, c: (i, jnp.bitwise_xor(c[0], jj))),
                       pl.BlockSpec((tm, D), lambda jj, i, c: (jnp.where(jj == 0, i, nt - 1), 0))] + [anyspace] * n,
            scratch_shapes=[pltpu.VMEM((2, D, D), BF16), pltpu.VMEM((nt, tm, D), BF16),
                            pltpu.SemaphoreType.DMA((2,))] + _gather_sems(n)),
        out_shape=[SDS((s, 4 * D), BF16), SDS((s, D), BF16)] + [SDS(w.shape, w.dtype) for w in weights],
        input_output_aliases={3 + a: 2 + a for a in range(n)},
        compiler_params=_params(("arbitrary", "arbitrary")),
    )(chip, x, g_pre, *weights)
    return proj, n1, gathered


def _shift_rows(v, last, second_last, rows):
    v1 = jnp.where(rows >= 1, pltpu.roll(v, 1, 0), last)
    v2 = jnp.where(rows >= 2, pltpu.roll(v, 2, 0), jnp.where(rows == 1, last, second_last))
    return v1, v2


def _a_mix(proj, x, conv_w, w_out, g_post, tm):
    s = x.shape[0]

    def body(proj_ref, x_ref, cw_ref, w_ref, g_ref, ya_ref, oa_ref, h1_ref, carry):
        @pl.when(pl.program_id(0) == 0)
        def _():
            carry[...] = jnp.zeros_like(carry)
        v = proj_ref[:, D:2 * D].astype(F32) * proj_ref[:, 2 * D:3 * D].astype(F32)
        rows = lax.broadcasted_iota(jnp.int32, (tm, D), 0)
        before = carry[...]
        v1, v2 = _shift_rows(v, before[7:8, :], before[6:7, :], rows)
        carry[...] = v[tm - 8:tm, :]
        conv = cw_ref[0:1, :] * v2 + cw_ref[1:2, :] * v1 + cw_ref[2:3, :] * v
        _, sz = _silu_parts(proj_ref[:, 3 * D:4 * D].astype(F32))
        ya = (proj_ref[:, 0:D].astype(F32) * conv * sz).astype(BF16)
        ya_ref[...] = ya
        oa = _nn(ya, w_ref[...])
        oa_ref[...] = oa.astype(BF16)
        h1_ref[...] = x_ref[...] + oa * _rms_scale(oa) * g_ref[...]

    row = lambda i: (i, 0)
    fix = lambda i: (0, 0)
    return pl.pallas_call(
        body, name="a_mix", grid=(s // tm,),
        in_specs=[pl.BlockSpec((tm, 4 * D), row), pl.BlockSpec((tm, D), row), pl.BlockSpec((8, D), fix),
                  pl.BlockSpec((D, D), fix), pl.BlockSpec((1, D), fix)],
        out_specs=[pl.BlockSpec((tm, D), row)] * 3,
        out_shape=[SDS((s, D), BF16), SDS((s, D), BF16), SDS((s, D), F32)],
        scratch_shapes=[pltpu.VMEM((8, D), F32)],
        compiler_params=_params(("arbitrary",)),
    )(proj, x, conv_w, w_out, g_post)


def _b_in(h1, g_kv, g_pre, w_kv, wbin_g, tm):
    s = h1.shape[0]

    def body(h_ref, gk_ref, gb_ref, wkv_ref, wb_ref, kv_ref, q_ref, z_ref):
        h = h_ref[...]
        hh = h * _rms_scale(h)
        nk = (hh * gk_ref[...]).astype(BF16)
        nb = (hh * gb_ref[...]).astype(BF16)
        kv_ref[...] = _nn(nk, wkv_ref[...]).astype(BF16)
        for j in range(2):
            q_ref[:, 512 * j:512 * (j + 1)] = (_nn(nb, wb_ref[j]) * Q_SCALE).astype(BF16)
            z_ref[:, 512 * j:512 * (j + 1)] = _nn(nb, wb_ref[2 + j]).astype(BF16)

    row = lambda i: (i, 0)
    fix = lambda i: (0, 0)
    return pl.pallas_call(
        body, name="b_in", grid=(s // tm,),
        in_specs=[pl.BlockSpec((tm, D), row), pl.BlockSpec((1, D), fix), pl.BlockSpec((1, D), fix),
                  pl.BlockSpec((D, 2 * KV_W), fix), pl.BlockSpec((4, D, 512), lambda i: (0, 0, 0))],
        out_specs=[pl.BlockSpec((tm, 2 * KV_W), row), pl.BlockSpec((tm, D), row), pl.BlockSpec((tm, D), row)],
        out_shape=[SDS((s, 2 * KV_W), BF16), SDS((s, D), BF16), SDS((s, D), BF16)],
        compiler_params=_params(("parallel",)),
    )(h1, g_kv, g_pre, w_kv, wbin_g)


def _band_buckets():
    q = lax.broadcasted_iota(jnp.int32, (BLK, 2 * BLK), 0)
    k = lax.broadcasted_iota(jnp.int32, (BLK, 2 * BLK), 1)
    dist = q + BLK - k
    bucket = jnp.where(dist < MAX_EXACT, dist, MAX_EXACT)
    for t in BUCKET_THRESHOLDS:
        bucket = bucket + jnp.where(dist >= t, 1, 0)
    in_window = (dist >= 0) & (dist < BLK)
    return jnp.where(in_window, bucket, -1)


def _head_place(h):
    kh, j, e = h // GROUP, (h % GROUP) // 2, h % 2
    return kh, slice(BLK * j, BLK * (j + 1)), slice(2 * BLK * e, 2 * BLK * (e + 1))


def _bias_table(rel_bias, sinks):
    def body(rb_ref, sink_ref, tab_ref):
        bucket = _band_buckets()
        col = lax.broadcasted_iota(jnp.int32, (BLK, 2 * BLK), 1)
        for h in range(N_HEADS):
            acc = jnp.where(bucket < 0, NEG_INF, 0.0).astype(F32)
            for b in range(N_BUCKETS):
                acc = jnp.where(bucket == b, rb_ref[b, h], acc)
            acc = jnp.where(col == 0, sink_ref[h], acc)
            kh, rows, cols = _head_place(h)
            tab_ref[1, kh, rows, cols] = acc
            tab_ref[0, kh, rows, cols] = jnp.where((col > 0) & (col < BLK), NEG_INF, acc)

    return pl.pallas_call(
        body, name="bias_table", out_shape=SDS((2, N_KV, 4 * BLK, 4 * BLK), F32),
        in_specs=[pl.BlockSpec(memory_space=pltpu.SMEM), pl.BlockSpec(memory_space=pltpu.SMEM)],
        out_specs=pl.BlockSpec(memory_space=pltpu.VMEM),
    )(rel_bias, sinks)


def _bias_fold(dtab):
    def body(dtab_ref, out_ref, dsink_ref):
        bucket = _band_buckets()
        row = lax.broadcasted_iota(jnp.int32, (N_BUCKETS, 128), 0)
        lane = lax.broadcasted_iota(jnp.int32, (N_BUCKETS, 128), 1)
        row8 = lax.broadcasted_iota(jnp.int32, (8, 128), 0)
        lane8 = lax.broadcasted_iota(jnp.int32, (8, 128), 1)
        acc = jnp.zeros((N_BUCKETS, 128), F32)
        dsink = jnp.zeros((8, 128), F32)
        for h in range(N_HEADS):
            kh, rows, cols = _head_place(h)
            dt = dtab_ref[kh, rows, cols]
            for b in range(N_BUCKETS):
                val = jnp.sum(jnp.where(bucket == b, dt, 0.0))
                acc = acc + jnp.where((row == b) & (lane == h), val, 0.0)
            dsink = dsink + jnp.where((row8 == 0) & (lane8 == h), jnp.sum(dt[:, 0:1]), 0.0)
        out_ref[...] = acc
        dsink_ref[...] = dsink

    vm = pl.BlockSpec(memory_space=pltpu.VMEM)
    return pl.pallas_call(
        body, name="bias_fold", out_shape=[SDS((N_BUCKETS, 128), F32), SDS((8, 128), F32)],
        in_specs=[vm], out_specs=[vm, vm],
    )(dtab)


def _pair_operands(prev, cur):
    t = jnp.concatenate([prev, cur], axis=0).astype(F32)
    t = jnp.where(lax.broadcasted_iota(jnp.int32, t.shape, 0) == 0, 0.0, t)
    tr = pltpu.roll(t, HEAD_DIM, 1)
    lo = lax.broadcasted_iota(jnp.int32, t.shape, 1) < HEAD_DIM
    zero = jnp.zeros_like(t)
    head0 = jnp.concatenate([jnp.where(lo, t, zero), jnp.where(lo, zero, tr)], axis=0).astype(BF16)
    head1 = jnp.concatenate([jnp.where(lo, tr, zero), jnp.where(lo, zero, t)], axis=0).astype(BF16)
    return head0, head1


def _pair_fold(d0, d1):
    lo = lax.broadcasted_iota(jnp.int32, (2 * BLK, KV_W), 1) < HEAD_DIM
    zero = jnp.zeros((2 * BLK, KV_W), F32)
    g0 = jnp.where(lo, d0[0:256], zero) + pltpu.roll(jnp.where(lo, zero, d0[256:512]), HEAD_DIM, 1)
    g1 = pltpu.roll(jnp.where(lo, d1[0:256], zero), HEAD_DIM, 1) + jnp.where(lo, zero, d1[256:512])
    return jnp.where(lax.broadcasted_iota(jnp.int32, (2 * BLK, KV_W), 0) == 0, 0.0, g0 + g1)


def _stack_pairs(ref, kh, rows=slice(None)):
    return jnp.concatenate([ref[rows, 128 * (4 * kh + j):128 * (4 * kh + j + 1)] for j in range(4)], axis=0)


def _table_spec():
    return pl.BlockSpec((1, N_KV, 4 * BLK, 4 * BLK), lambda n: (jnp.minimum(n, 1), 0, 0, 0))


def _attn_fwd(q, kv, tab):
    s = q.shape[0]

    def body(q_ref, kp_ref, k0_ref, k1_ref, vp_ref, v0_ref, v1_ref, tab0_ref, tab1_ref, att_ref, stats_ref):
        lane = lax.broadcasted_iota(jnp.int32, (BLK, 128), 1)
        for sub, (kp, kc, vp, vc, tab_ref) in enumerate([(kp_ref, k0_ref, vp_ref, v0_ref, tab0_ref),
                                                         (k0_ref, k1_ref, v0_ref, v1_ref, tab1_ref)]):
            rows = slice(BLK * sub, BLK * (sub + 1))
            k2 = _pair_operands(kp[...], kc[...])
            v2 = _pair_operands(vp[...], vc[...])
            stats = jnp.zeros((BLK, 128), F32)
            for kh in range(N_KV):
                sc = _nt(_stack_pairs(q_ref, kh, rows), k2[kh])
                ps = []
                for e in range(2):
                    lg = sc[:, 256 * e:256 * (e + 1)] + tab_ref[0, kh, :, 256 * e:256 * (e + 1)]
                    m = jnp.max(lg, axis=-1, keepdims=True)
                    ex = jnp.exp(lg - m)
                    den = jnp.sum(ex, axis=-1, keepdims=True)
                    ps.append(ex * (1.0 / den))
                    lse = m + jnp.log(den)
                    for j in range(4):
                        stats = jnp.where(lane == GROUP * kh + 2 * j + e, lse[BLK * j:BLK * (j + 1)], stats)
                out = _nn(jnp.concatenate(ps, axis=1).astype(BF16), v2[kh])
                for j in range(4):
                    att_ref[rows, 128 * (4 * kh + j):128 * (4 * kh + j + 1)] = out[BLK * j:BLK * (j + 1)].astype(BF16)
            stats_ref[rows, :] = stats

    two = lambda m: (m, 0)
    table = lambda pick: pl.BlockSpec((1, N_KV, 4 * BLK, 4 * BLK), lambda m: (pick(m), 0, 0, 0))
    return pl.pallas_call(
        body, name="attn_fwd", grid=(s // (2 * BLK),),
        in_specs=[pl.BlockSpec((2 * BLK, D), two)]
        + [pl.BlockSpec((BLK, KV_W), lambda m, col=col, off=off: (jnp.maximum(2 * m + off, 0), col))
           for col in (0, 1) for off in (-1, 0, 1)]
        + [table(lambda m: jnp.minimum(m, 1)), table(lambda m: 1)],
        out_specs=[pl.BlockSpec((2 * BLK, D), two), pl.BlockSpec((2 * BLK, 128), two)],
        out_shape=[SDS((s, D), BF16), SDS((s, 128), F32)],
        compiler_params=_params(("parallel",)),
    )(q, kv, kv, kv, kv, kv, kv, tab, tab)


def _mid(att, zb, h1, tgt, w_out, g_post, tm):
    s = att.shape[0]
    nt = s // tm

    def body(att_ref, z_ref, h1_ref, t_ref, w_ref, g_ref,
             dh_ref, dqz_ref, datt_ref, loss_ref, dg_ref, dw_ref, dw_acc):
        @pl.when(pl.program_id(0) == 0)
        def _():
            loss_ref[...] = jnp.zeros_like(loss_ref)
            dg_ref[...] = jnp.zeros_like(dg_ref)
            dw_acc[...] = jnp.zeros_like(dw_acc)
        att = att_ref[...].astype(F32)
        z = z_ref[...].astype(F32)
        sg, sz = _silu_parts(z)
        ob = (att * sz).astype(BF16)
        y2 = _nn(ob, w_ref[...])
        r2 = _rms_scale(y2)
        yh = y2 * r2
        g = g_ref[...]
        err = (h1_ref[...] + yh * g) - t_ref[...]
        loss_ref[...] += jnp.sum(jnp.sum(err * err, axis=-1, keepdims=True) / D)
        dh = err / D
        dh_ref[...] = dh
        _acc_row(dg_ref, 0, jnp.sum(dh * yh, axis=0, keepdims=True))
        dyh = dh * g
        dy = (r2 * (dyh - yh * jnp.mean(dyh * yh, axis=-1, keepdims=True))).astype(BF16)
        dw_acc[...] += _tn(ob, dy)
        dob = _nt(dy, w_ref[...])
        datt_ref[...] = (dob * sz).astype(BF16)
        dqz_ref[...] = (dob * att * _dsilu(z, sg)).astype(BF16)

        @pl.when(pl.program_id(0) == nt - 1)
        def _():
            pltpu.sync_copy(dw_acc, dw_ref)

    row = lambda i: (i, 0)
    fix = lambda i: (0, 0)
    return pl.pallas_call(
        body, name="mid", grid=(nt,),
        in_specs=[pl.BlockSpec((tm, D), row)] * 4 + [pl.BlockSpec((D, D), fix), pl.BlockSpec((1, D), fix)],
        out_specs=[pl.BlockSpec((tm, D), row), pl.BlockSpec((tm, D), lambda i: (i, 1)), pl.BlockSpec((tm, D), row),
                   pl.BlockSpec((8, 128), fix), pl.BlockSpec((8, D), fix), pl.BlockSpec(memory_space=pl.ANY)],
        out_shape=[SDS((s, D), F32), SDS((s, 2 * D), BF16), SDS((s, D), BF16), SDS((8, 128), F32),
                   SDS((8, D), F32), SDS((D, D), F32)],
        scratch_shapes=[pltpu.VMEM((D, D), F32)],
        compiler_params=_params(("arbitrary",)),
    )(att, zb, h1, tgt, w_out, g_post)


def _attn_bwd(q, kv, datt, stats, tab, dqz):
    s = q.shape[0]
    nb = s // BLK

    def body(q_ref, kp_ref, kc_ref, vp_ref, vc_ref, da_ref, st_ref, tab_ref, dqz_in,
             dq_ref, dkv_ref, dtab_ref, dk_carry, dv_carry):
        del dqz_in
        n = pl.program_id(0)

        @pl.when(n == 0)
        def _():
            dtab_ref[...] = jnp.zeros_like(dtab_ref)
            dk_carry[...] = jnp.zeros_like(dk_carry)
            dv_carry[...] = jnp.zeros_like(dv_carry)

        @pl.when(n < nb)
        def _():
            k2 = _pair_operands(kp_ref[...], kc_ref[...])
            v2 = _pair_operands(vp_ref[...], vc_ref[...])
            lane = lax.broadcasted_iota(jnp.int32, (BLK, 128), 1)
            stats = st_ref[...]
            dk2, dv2 = [], []
            for kh in range(N_KV):
                qs = _stack_pairs(q_ref, kh)
                das = _stack_pairs(da_ref, kh)
                sc = _nt(qs, k2[kh])
                dp = _nt(das, v2[kh])
                ps, dss = [], []
                for e in range(2):
                    heads = [GROUP * kh + 2 * j + e for j in range(4)]
                    lse = jnp.concatenate([jnp.sum(jnp.where(lane == h, stats, 0.0), axis=-1, keepdims=True)
                                           for h in heads], axis=0)
                    cols = slice(256 * e, 256 * (e + 1))
                    p = jnp.exp(sc[:, cols] + tab_ref[0, kh, :, cols] - lse)
                    delta = jnp.sum(p * dp[:, cols], axis=-1, keepdims=True)
                    ds = p * (dp[:, cols] - delta)
                    dtab_ref[kh, :, cols] += ds
                    ps.append(p)
                    dss.append(ds)
                p2 = jnp.concatenate(ps, axis=1).astype(BF16)
                ds2 = jnp.concatenate(dss, axis=1).astype(BF16)
                dq = _nn(ds2, k2[kh]) * Q_SCALE
                for j in range(4):
                    dq_ref[:, 128 * (4 * kh + j):128 * (4 * kh + j + 1)] = dq[BLK * j:BLK * (j + 1)].astype(BF16)
                dk2.append(_tn(ds2, qs))
                dv2.append(_tn(p2, das))
            dkk = _pair_fold(dk2[0], dk2[1])
            dvv = _pair_fold(dv2[0], dv2[1])
            dkv_ref[:, 0:KV_W] = (dk_carry[...] + dkk[0:BLK]).astype(BF16)
            dkv_ref[:, KV_W:2 * KV_W] = (dv_carry[...] + dvv[0:BLK]).astype(BF16)
            dk_carry[...] = dkk[BLK:2 * BLK]
            dv_carry[...] = dvv[BLK:2 * BLK]

        @pl.when(n == nb)
        def _():
            dkv_ref[:, 0:KV_W] = dk_carry[...].astype(BF16)
            dkv_ref[:, KV_W:2 * KV_W] = dv_carry[...].astype(BF16)

    cur = lambda n: (jnp.minimum(n, nb - 1), 0)
    prev = lambda n: (jnp.clip(n - 1, 0, nb - 1), 0)
    return pl.pallas_call(
        body, name="attn_bwd", grid=(nb + 1,),
        in_specs=[pl.BlockSpec((BLK, D), cur),
                  pl.BlockSpec((BLK, KV_W), prev), pl.BlockSpec((BLK, KV_W), cur),
                  pl.BlockSpec((BLK, KV_W), lambda n: (jnp.clip(n - 1, 0, nb - 1), 1)),
                  pl.BlockSpec((BLK, KV_W), lambda n: (jnp.minimum(n, nb - 1), 1)),
                  pl.BlockSpec((BLK, D), cur), pl.BlockSpec((BLK, 128), cur), _table_spec(),
                  pl.BlockSpec(memory_space=pl.ANY)],
        out_specs=[pl.BlockSpec((BLK, D), cur), pl.BlockSpec((BLK, 2 * KV_W), prev),
                   pl.BlockSpec((N_KV, 4 * BLK, 4 * BLK), lambda n: (0, 0, 0))],
        out_shape=[SDS((s, 2 * D), BF16), SDS((s, 2 * KV_W), BF16), SDS((N_KV, 4 * BLK, 4 * BLK), F32)],
        scratch_shapes=[pltpu.VMEM((BLK, KV_W), F32), pltpu.VMEM((BLK, KV_W), F32)],
        input_output_aliases={8: 0},
        compiler_params=_params(("arbitrary",)),
    )(q, kv, kv, kv, kv, datt, stats, tab, dqz)


def _b_bwd(dqz, dkv, h1, dh2, oa, wbin_g, w_kv, g_kv, g_pre, g_apost, tm):
    s = h1.shape[0]
    nt = s // tm

    def body(dqz_ref, dkv_ref, h_ref, dh2_ref, oa_ref, wb_ref, wkv_ref, gk_ref, gb_ref, ga_ref,
             dh1_ref, doa_ref, dg_ref, dwb_ref, dwkv_ref, dwb_acc, dwkv_acc):
        @pl.when(pl.program_id(0) == 0)
        def _():
            dg_ref[...] = jnp.zeros_like(dg_ref)
            dwb_acc[...] = jnp.zeros_like(dwb_acc)
            dwkv_acc[...] = jnp.zeros_like(dwkv_acc)
        dnb = _nt(dqz_ref[:, 0:512], wb_ref[0])
        for j in range(1, 4):
            dnb = dnb + _nt(dqz_ref[:, 512 * j:512 * (j + 1)], wb_ref[j])
        dnk = _nt(dkv_ref[...], wkv_ref[...])
        h = h_ref[...]
        r = _rms_scale(h)
        hh = h * r
        nb = (hh * gb_ref[...]).astype(BF16)
        for j in range(4):
            dwb_acc[j] += _tn(nb, dqz_ref[:, 512 * j:512 * (j + 1)])
        dwkv_acc[...] += _tn((hh * gk_ref[...]).astype(BF16), dkv_ref[...])
        _acc_row(dg_ref, 0, jnp.sum(dnk * hh, axis=0, keepdims=True))
        _acc_row(dg_ref, 1, jnp.sum(dnb * hh, axis=0, keepdims=True))
        dhh = dnb * gb_ref[...] + dnk * gk_ref[...]
        dh1 = dh2_ref[...] + r * (dhh - hh * jnp.mean(dhh * hh, axis=-1, keepdims=True))
        dh1_ref[...] = dh1
        oa = oa_ref[...].astype(F32)
        ra = _rms_scale(oa)
        oh = oa * ra
        _acc_row(dg_ref, 2, jnp.sum(dh1 * oh, axis=0, keepdims=True))
        doh = dh1 * ga_ref[...]
        doa_ref[...] = (ra * (doh - oh * jnp.mean(doh * oh, axis=-1, keepdims=True))).astype(BF16)

        @pl.when(pl.program_id(0) == nt - 1)
        def _():
            pltpu.sync_copy(dwb_acc, dwb_ref)
            pltpu.sync_copy(dwkv_acc, dwkv_ref)

    row = lambda i: (i, 0)
    fix = lambda i: (0, 0)
    anyspace = pl.BlockSpec(memory_space=pl.ANY)
    return pl.pallas_call(
        body, name="b_bwd", grid=(nt,),
        in_specs=[pl.BlockSpec((tm, 2 * D), row), pl.BlockSpec((tm, 2 * KV_W), row), pl.BlockSpec((tm, D), row),
                  pl.BlockSpec((tm, D), row), pl.BlockSpec((tm, D), row),
                  pl.BlockSpec((4, D, 512), lambda i: (0, 0, 0)), pl.BlockSpec((D, 2 * KV_W), fix),
                  pl.BlockSpec((1, D), fix), pl.BlockSpec((1, D), fix), pl.BlockSpec((1, D), fix)],
        out_specs=[pl.BlockSpec((tm, D), row), pl.BlockSpec((tm, D), row), pl.BlockSpec((8, D), fix), anyspace, anyspace],
        out_shape=[SDS((s, D), F32), SDS((s, D), BF16), SDS((8, D), F32), SDS((4, D, 512), F32),
                   SDS((D, 2 * KV_W), F32)],
        scratch_shapes=[pltpu.VMEM((4, D, 512), F32), pltpu.VMEM((D, 2 * KV_W), F32)],
        compiler_params=_params(("arbitrary",)),
    )(dqz, dkv, h1, dh2, oa, wbin_g, w_kv, g_kv, g_pre, g_apost)


def _chip_exchange(parts, recvs, send, recv):
    x, y, c = lax.axis_index("x"), lax.axis_index("y"), lax.axis_index("c")
    chips = [(x, 1 - y), (1 - x, y), (1 - x, 1 - y)]
    copies = []
    for a, (t, r) in enumerate(zip(parts, recvs)):
        for j, (px, py) in enumerate(chips):
            copies.append(pltpu.make_async_remote_copy(
                src_ref=t.at[2 * px + py], dst_ref=r.at[j], send_sem=send.at[3 * a + j],
                recv_sem=recv.at[3 * a + j], device_id=(px, py, c), device_id_type=MESH))
    return copies


def _exchange_specs(parts):
    anyspace = pl.BlockSpec(memory_space=pl.ANY)
    n = len(parts)
    return ([anyspace] * n, [anyspace] * n, [SDS((3,) + t.shape[1:], t.dtype) for t in parts],
            [pltpu.SemaphoreType.DMA((3 * n,)), pltpu.SemaphoreType.DMA((3 * n,))])


def _a_bwd(doa, ya, proj, conv_w, w_out, tm, parts):
    s = doa.shape[0]
    nt = s // tm
    n = len(parts)
    ex_in, ex_out, ex_shape, ex_sems = _exchange_specs(parts)

    def body(*refs):
        doa_ref, ya_ref, proj_ref, halo_ref, cw_ref, w_ref = refs[:6]
        part_refs = refs[6:6 + n]
        dproj_ref, dcw_ref, dw_ref = refs[6 + n:9 + n]
        recv_refs = refs[9 + n:9 + 2 * n]
        carry, dw_acc, send, recv = refs[9 + 2 * n:]
        i = pl.program_id(0)
        r = nt - 1 - i

        @pl.when(i == 0)
        def _():
            dcw_ref[...] = jnp.zeros_like(dcw_ref)
            carry[...] = jnp.zeros_like(carry)
            dw_acc[...] = jnp.zeros_like(dw_acc)
            for cp in _chip_exchange(part_refs, recv_refs, send, recv):
                cp.start()
        dya = _nt(doa_ref[...], w_ref[...])
        dw_acc[...] += _tn(ya_ref[...], doa_ref[...])
        bg = proj_ref[:, 0:D].astype(F32)
        cg = proj_ref[:, D:2 * D].astype(F32)
        u = proj_ref[:, 2 * D:3 * D].astype(F32)
        z = proj_ref[:, 3 * D:4 * D].astype(F32)
        v = cg * u
        before = jnp.where(r > 0, halo_ref[:, D:2 * D].astype(F32) * halo_ref[:, 2 * D:3 * D].astype(F32), 0.0)
        rows = lax.broadcasted_iota(jnp.int32, (tm, D), 0)
        v1, v2 = _shift_rows(v, before[HALO - 1:HALO, :], before[HALO - 2:HALO - 1, :], rows)
        conv = cw_ref[0:1, :] * v2 + cw_ref[1:2, :] * v1 + cw_ref[2:3, :] * v
        sg, sz = _silu_parts(z)
        dproj_ref[:, 0:D] = (dya * conv * sz).astype(BF16)
        dproj_ref[:, 3 * D:4 * D] = (dya * bg * conv * _dsilu(z, sg)).astype(BF16)
        dconv = dya * bg * sz
        _acc_row(dcw_ref, 0, jnp.sum(dconv * v2, axis=0, keepdims=True))
        _acc_row(dcw_ref, 1, jnp.sum(dconv * v1, axis=0, keepdims=True))
        _acc_row(dcw_ref, 2, jnp.sum(dconv * v, axis=0, keepdims=True))
        after = carry[...]
        up1 = jnp.where(rows < tm - 1, pltpu.roll(dconv, tm - 1, 0), after[0:1, :])
        up2 = jnp.where(rows < tm - 2, pltpu.roll(dconv, tm - 2, 0),
                        jnp.where(rows == tm - 2, after[0:1, :], after[1:2, :]))
        carry[...] = dconv[0:8, :]
        dv = cw_ref[2:3, :] * dconv + cw_ref[1:2, :] * up1 + cw_ref[0:1, :] * up2
        dproj_ref[:, D:2 * D] = (dv * u).astype(BF16)
        dproj_ref[:, 2 * D:3 * D] = (dv * cg).astype(BF16)

        @pl.when(i == nt - 1)
        def _():
            pltpu.sync_copy(dw_acc, dw_ref)
            for cp in _chip_exchange(part_refs, recv_refs, send, recv):
                cp.wait()

    rev = lambda i: (nt - 1 - i, 0)
    fix = lambda i: (0, 0)
    halo = lambda i: (jnp.maximum((nt - 1 - i) * (tm // HALO) - 1, 0), 0)
    dproj, dcw, dw, *got = pl.pallas_call(
        body, name="a_bwd", grid=(nt,),
        in_specs=[pl.BlockSpec((tm, D), rev), pl.BlockSpec((tm, D), rev), pl.BlockSpec((tm, 4 * D), rev),
                  pl.BlockSpec((HALO, 4 * D), halo), pl.BlockSpec((8, D), fix), pl.BlockSpec((D, D), fix)] + ex_in,
        out_specs=[pl.BlockSpec((tm, 4 * D), rev), pl.BlockSpec((8, D), fix), pl.BlockSpec(memory_space=pl.ANY)] + ex_out,
        out_shape=[SDS((s, 4 * D), BF16), SDS((8, D), F32), SDS((D, D), F32)] + ex_shape,
        scratch_shapes=[pltpu.VMEM((8, D), F32), pltpu.VMEM((D, D), F32)] + ex_sems,
        compiler_params=_params(("arbitrary",)),
    )(doa, ya, proj, proj, conv_w, w_out, *parts)
    return dproj, dcw, dw, got


def _dn1(dp_ref, w_ref):
    dn = _nt(dp_ref[:, 0:D], w_ref[0])
    for j in range(1, 4):
        dn = dn + _nt(dp_ref[:, D * j:D * (j + 1)], w_ref[j])
    return dn


def _a_in_bwd_matmul(dproj, win_g, tm, count, parts):
    n = len(parts)
    ex_in, ex_out, ex_shape, ex_sems = _exchange_specs(parts)

    def body(*refs):
        dp_ref, w_ref = refs[:2]
        part_refs = refs[2:2 + n]
        dn_ref = refs[2 + n]
        recv_refs = refs[3 + n:3 + 2 * n]
        sems = refs[3 + 2 * n:]

        @pl.when(pl.program_id(0) == 0)
        def _():
            for cp in _chip_exchange(part_refs, recv_refs, *sems):
                cp.start()
        dn_ref[...] = _dn1(dp_ref, w_ref).astype(BF16)

        @pl.when(pl.program_id(0) == count - 1)
        def _():
            for cp in _chip_exchange(part_refs, recv_refs, *sems):
                cp.wait()

    row = lambda i: (i, 0)
    dn, *got = pl.pallas_call(
        body, name="a_in_bwd_matmul", grid=(count,),
        in_specs=[pl.BlockSpec((tm, 4 * D), row), pl.BlockSpec((4, D, D), lambda i: (0, 0, 0))] + ex_in,
        out_specs=[pl.BlockSpec((tm, D), row)] + ex_out,
        out_shape=[SDS((count * tm, D), BF16)] + ex_shape,
        scratch_shapes=ex_sems,
        compiler_params=_params(("arbitrary",)),
    )(dproj, win_g, *parts)
    return dn, got


def _a_in_bwd(dn_first, dproj, x, dh1, win_g, g_pre, tm):
    s = x.shape[0]
    nt = s // tm
    count = dn_first.shape[0] // tm

    def body(dn_ref, dp_ref, x_ref, dh_ref, w_ref, g_ref, gx_ref, dg_ref, dn_s):
        i = pl.program_id(0)

        @pl.when(i == 0)
        def _():
            dg_ref[...] = jnp.zeros_like(dg_ref)

        @pl.when(i < count)
        def _():
            dn_s[...] = dn_ref[...].astype(F32)

        @pl.when(i >= count)
        def _():
            dn_s[...] = _dn1(dp_ref, w_ref)
        dn = dn_s[...]
        xv = x_ref[...]
        r = _rms_scale(xv)
        xh = xv * r
        _acc_row(dg_ref, 0, jnp.sum(dn * xh, axis=0, keepdims=True))
        dxh = dn * g_ref[...]
        gx_ref[...] = dh_ref[...] + r * (dxh - xh * jnp.mean(dxh * xh, axis=-1, keepdims=True))

    row = lambda i: (i, 0)
    fix = lambda i: (0, 0)
    return pl.pallas_call(
        body, name="a_in_bwd", grid=(nt,),
        in_specs=[pl.BlockSpec((tm, D), lambda i: (jnp.minimum(i, count - 1), 0)),
                  pl.BlockSpec((tm, 4 * D), lambda i: (jnp.maximum(i, count), 0)),
                  pl.BlockSpec((tm, D), row), pl.BlockSpec((tm, D), row),
                  pl.BlockSpec((4, D, D), lambda i: (0, 0, 0)), pl.BlockSpec((1, D), fix)],
        out_specs=[pl.BlockSpec((tm, D), row), pl.BlockSpec((8, D), fix)],
        out_shape=[SDS((s, D), F32), SDS((8, D), F32)],
        scratch_shapes=[pltpu.VMEM((tm, D), F32)],
        compiler_params=_params(("arbitrary",)),
    )(dn_first, dproj, x, dh1, win_g, g_pre)


def _dw(a, b, tn, tmw, name):
    s, k = a.shape
    n = b.shape[1]

    def body(a_ref, b_ref, o_ref):
        @pl.when(pl.program_id(1) == 0)
        def _():
            o_ref[...] = jnp.zeros_like(o_ref)
        o_ref[0] += _tn(a_ref[...], b_ref[...])

    return pl.pallas_call(
        body, name=name, grid=(n // tn, s // tmw),
        in_specs=[pl.BlockSpec((tmw, k), lambda j, t: (t, 0)), pl.BlockSpec((tmw, tn), lambda j, t: (t, j))],
        out_specs=pl.BlockSpec((1, k, tn), lambda j, t: (j, 0, 0)),
        out_shape=SDS((n // tn, k, tn), F32),
        compiler_params=_params(("parallel", "arbitrary")),
    )(a, b)


def _sibling_exchange(name, to_sibling=(), shards=(), smalls=None):
    n_g, n_h = len(to_sibling), len(shards)
    has_small = smalls is not None

    def body(*refs):
        gs = refs[:n_g]
        pos = n_g + n_h
        small_in = refs[pos] if has_small else None
        pos += has_small
        rs, fs = refs[pos:pos + n_g], refs[pos + n_g:pos + n_g + n_h]
        pos += n_g + n_h
        small_all = refs[pos] if has_small else None
        pos += has_small
        dsend, drecv, ssend, srecv = refs[pos:]
        x, y, c = lax.axis_index("x"), lax.axis_index("y"), lax.axis_index("c")
        sibling = (x, y, 1 - c)
        sends, arrivals = [], []
        for a, (g, r) in enumerate(zip(gs, rs)):
            h = g.shape[1] // 2
            src = g.at[:, pl.ds(pl.multiple_of((1 - c) * h, 8), h), :]
            sends.append(pltpu.make_async_remote_copy(src_ref=src, dst_ref=r, send_sem=dsend.at[a], recv_sem=drecv.at[a],
                                                      device_id=sibling, device_id_type=MESH))
            arrivals.append(pltpu.make_async_remote_copy(src_ref=r, dst_ref=r, send_sem=dsend.at[a], recv_sem=drecv.at[a],
                                                         device_id=sibling, device_id_type=MESH))
        for b, full in enumerate(fs):
            h = full.shape[0] // 2
            mine = full.at[pl.ds(pl.multiple_of(c * h, 8), h)]
            theirs = full.at[pl.ds(pl.multiple_of((1 - c) * h, 8), h)]
            sends.append(pltpu.make_async_remote_copy(src_ref=mine, dst_ref=mine, send_sem=dsend.at[n_g + b],
                                                      recv_sem=drecv.at[n_g + b], device_id=sibling, device_id_type=MESH))
            arrivals.append(pltpu.make_async_remote_copy(src_ref=mine, dst_ref=theirs, send_sem=dsend.at[n_g + b],
                                                         recv_sem=drecv.at[n_g + b], device_id=sibling, device_id_type=MESH))
        if has_small:
            me = 4 * x + 2 * y + c
            small_all[me] = small_in[...]
            for rel in range(1, N_DEV):
                fx, fy, fc = rel >> 2, (rel >> 1) & 1, rel & 1
                peer = (x + fx - 2 * x * fx, y + fy - 2 * y * fy, c + fc - 2 * c * fc)
                sender = 4 * peer[0] + 2 * peer[1] + peer[2]
                sends.append(pltpu.make_async_remote_copy(
                    src_ref=small_in, dst_ref=small_all.at[me], send_sem=ssend.at[rel - 1], recv_sem=srecv.at[rel - 1],
                    device_id=peer, device_id_type=MESH))
                arrivals.append(pltpu.make_async_remote_copy(
                    src_ref=small_in, dst_ref=small_all.at[sender], send_sem=ssend.at[rel - 1], recv_sem=srecv.at[rel - 1],
                    device_id=peer, device_id_type=MESH))
        for cp in sends:
            cp.start()
        for cp in arrivals:
            cp.wait_recv()
        for cp in sends:
            cp.wait_send()

    anyspace = pl.BlockSpec(memory_space=pl.ANY)
    vm = pl.BlockSpec(memory_space=pltpu.VMEM)
    out_shape = [SDS((N_CHIPS, g.shape[1] // 2, g.shape[2]), F32) for g in to_sibling]
    out_shape += [SDS(full.shape, F32) for full in shards]
    if has_small:
        out_shape.append(SDS((N_DEV,) + smalls.shape, F32))
    n_d2d = max(n_g + n_h, 1)
    outs = pl.pallas_call(
        body, name=name, out_shape=out_shape,
        in_specs=[anyspace] * (n_g + n_h) + [vm] * has_small, out_specs=[anyspace] * (n_g + n_h) + [vm] * has_small,
        scratch_shapes=[pltpu.SemaphoreType.DMA((n_d2d,)), pltpu.SemaphoreType.DMA((n_d2d,)),
                        pltpu.SemaphoreType.DMA((N_DEV - 1,)), pltpu.SemaphoreType.DMA((N_DEV - 1,))],
        input_output_aliases={n_g + b: n_g + b for b in range(n_h)},
    )(*to_sibling, *shards, *([smalls] if has_small else []))
    return outs[:n_g], outs[n_g:n_g + n_h], (outs[n_g + n_h] if has_small else None)


def _add_sibling(where, g, r, name):
    _, rows, cols = g.shape
    h = rows // 2
    tr = min(h, 256)
    nh = h // tr

    def body(where_ref, g_ref, r_ref, t_ref, own_ref):
        t = g_ref[0] + r_ref[0]
        t_ref[0] = t.astype(BF16)

        @pl.when(pl.program_id(1) == where_ref[1])
        def _():
            own_ref[...] = t

    return pl.pallas_call(
        body, name=name,
        grid_spec=pltpu.PrefetchScalarGridSpec(
            num_scalar_prefetch=1, grid=(nh, N_CHIPS),
            in_specs=[pl.BlockSpec((1, tr, cols), lambda i, k, w: (k, w[0] * nh + i, 0)),
                      pl.BlockSpec((1, tr, cols), lambda i, k, w: (k, i, 0))],
            out_specs=[pl.BlockSpec((1, tr, cols), lambda i, k, w: (k, i, 0)),
                       pl.BlockSpec((tr, cols), lambda i, k, w: (i, 0))]),
        out_shape=[SDS((N_CHIPS, h, cols), BF16), SDS((h, cols), F32)],
        compiler_params=_params(("parallel", "arbitrary")),
    )(where, g, r)


def _add_chips(where, own, r, name):
    h, cols = own.shape
    tr = min(h, 256)
    nh = h // tr

    def body(where_ref, t_ref, r_ref, o_ref):
        del where_ref
        o_ref[...] = ((t_ref[...] + r_ref[0].astype(F32)) + r_ref[1].astype(F32)) + r_ref[2].astype(F32)

    return pl.pallas_call(
        body, name=name,
        grid_spec=pltpu.PrefetchScalarGridSpec(
            num_scalar_prefetch=1, grid=(nh,),
            in_specs=[pl.BlockSpec((tr, cols), lambda i, w: (i, 0)), pl.BlockSpec((3, tr, cols), lambda i, w: (0, i, 0))],
            out_specs=pl.BlockSpec((tr, cols), lambda i, w: (w[0] * nh + i, 0))),
        out_shape=SDS((2 * h, cols), F32),
        compiler_params=_params(("parallel",)),
    )(where, own, r)


def _sum_smalls(small_all):
    def body(all_ref, o_ref):
        acc = all_ref[0]
        for dev in range(1, N_DEV):
            acc = acc + all_ref[dev]
        o_ref[...] = acc

    return pl.pallas_call(
        body, name="sum_smalls", out_shape=SDS(small_all.shape[1:], F32),
        in_specs=[pl.BlockSpec(memory_space=pltpu.VMEM)], out_specs=pl.BlockSpec(memory_space=pltpu.VMEM),
    )(small_all)


def _adam_step(g, w, m, v):
    nm = ADAM_B1 * m + (1.0 - ADAM_B1) * g
    nv = ADAM_B2 * v + (1.0 - ADAM_B2) * (g * g)
    m_hat = nm / (1.0 - ADAM_B1 ** ADAM_STEP)
    v_hat = nv / (1.0 - ADAM_B2 ** ADAM_STEP)
    return -ADAM_LR * (m_hat / (jnp.sqrt(v_hat) + ADAM_EPS) + ADAM_WD * w), nm, nv


def _adamw(g, w, m, v, name):
    rows, cols = g.shape
    tr = min(rows, 256)

    def body(g_ref, w_ref, m_ref, v_ref, d_ref, nm_ref, nv_ref):
        d_ref[...], nm_ref[...], nv_ref[...] = _adam_step(g_ref[...], w_ref[...], m_ref[...], v_ref[...])

    spec = pl.BlockSpec((tr, cols), lambda i: (i, 0))
    return pl.pallas_call(
        body, name=name, grid=(rows // tr,), in_specs=[spec] * 4, out_specs=[spec] * 3,
        out_shape=[SDS(g.shape, F32)] * 3, compiler_params=_params(("parallel",)),
    )(g, w, m, v)


def _small_update(chip, tot, wmv):
    names = list(SMALL_PLACES)
    n = len(names)

    def body(chip_ref, tot_ref, quarter_ref, *refs):
        del chip_ref
        ins, outs = refs[:3 * n], refs[3 * n:]
        for i, nm in enumerate(names):
            sharded, row, (rows, cols) = SMALL_PLACES[nm]
            g = (quarter_ref if sharded else tot_ref)[row:row + rows, 0:cols]
            outs[4 * i][...] = g
            outs[4 * i + 1][...], outs[4 * i + 2][...], outs[4 * i + 3][...] = _adam_step(
                g, ins[3 * i][...], ins[3 * i + 1][...], ins[3 * i + 2][...])

    whole = lambda shape: pl.BlockSpec(shape, lambda i, c: (0,) * len(shape))
    shapes = [SMALL_PLACES[nm][2] for nm in names]
    outs = pl.pallas_call(
        body, name="small_update",
        grid_spec=pltpu.PrefetchScalarGridSpec(
            num_scalar_prefetch=1, grid=(1,),
            in_specs=[whole(tot.shape), pl.BlockSpec((tot.shape[0], D // 4), lambda i, c: (0, c[0]))]
            + [whole(shp) for shp in shapes for _ in range(3)],
            out_specs=[whole(shp) for shp in shapes for _ in range(4)]),
        out_shape=[SDS(shp, F32) for shp in shapes for _ in range(4)],
    )(chip, tot, tot, *[a for nm in names for a in wmv[nm]])
    return {nm: tuple(outs[4 * i:4 * i + 4]) for i, nm in enumerate(names)}


def _pad_rows(a, rows):
    return jnp.concatenate([a, jnp.zeros((rows - a.shape[0], a.shape[1]), a.dtype)], axis=0)


def _pad_cols(a, cols):
    return jnp.concatenate([a, jnp.zeros((a.shape[0], cols - a.shape[1]), a.dtype)], axis=1)


def kernel(x, a_pre_norm, a_w_in, a_conv_w, a_w_out, a_post_norm, kv_norm, w_kv, rel_bias, b_pre_norm, b_w_in, b_sinks, b_w_out, b_post_norm, loss_target, m_a_pre_norm, m_a_w_in, m_a_conv_w, m_a_w_out, m_a_post_norm, m_kv_norm, m_w_kv, m_rel_bias, m_b_pre_norm, m_b_w_in, m_b_sinks, m_b_w_out, m_b_post_norm, v_a_pre_norm, v_a_w_in, v_a_conv_w, v_a_w_out, v_a_post_norm, v_kv_norm, v_w_kv, v_rel_bias, v_b_pre_norm, v_b_w_in, v_b_sinks, v_b_w_out, v_b_post_norm):
    seq = x.shape[1]
    xs = x.reshape(seq, D)
    tgt = loss_target.reshape(seq, D)
    chip = 2 * lax.axis_index("x") + lax.axis_index("y")
    core = lax.axis_index("c")
    tm = _tile(seq, 512)
    tm_mix = _tile(seq, 512)
    tmw = _tile(seq, 1024)

    shards = [a_w_in[0], a_w_out[0], w_kv, b_w_in[0], b_w_out[0]]
    small_w = _pad_rows(jnp.concatenate([a_pre_norm, a_conv_w[0], a_post_norm], axis=0), 8)
    *own_only, small_g = _gather_weights(shards, small_w, 0)
    where = jnp.stack([core, chip]).astype(jnp.int32)
    small_full = small_g.transpose(1, 0, 2).reshape(8, D)
    g_apre, conv_w, g_apost = small_full[0:1], _pad_rows(small_full[1:4], 8), small_full[4:5]
    g_kv = kv_norm.reshape(1, D)

    proj, n1, (win_g, wouta_g, wkv_g, wbin_g, woutb_g) = _a_in(where[1:2], xs, g_apre, own_only, tmw)
    wouta = wouta_g.reshape(D, D)
    wkv = wkv_g.reshape(D, 2 * KV_W)
    woutb = woutb_g.reshape(D, D)
    ya, oa, h1 = _a_mix(proj, xs, conv_w, wouta, g_apost, tm_mix)
    kv, q, zb = _b_in(h1, g_kv, b_pre_norm, wkv, wbin_g, tm)
    tab = _bias_table(rel_bias, b_sinks.reshape(N_HEADS))
    att, stats = _attn_fwd(q, kv, tab)
    dh2, dqz, datt, loss_acc, dg_bpost, dw_outb = _mid(att, zb, h1, tgt, woutb, b_post_norm, tm)

    dqz, dkv, dtab = _attn_bwd(q, kv, datt, stats, tab, dqz)
    dh1, doa, dg_b, dw_bin, dw_kv = _b_bwd(dqz, dkv, h1, dh2, oa, wbin_g, wkv, g_kv, b_pre_norm, g_apost, tm)
    dw_kv = dw_kv.reshape(N_CHIPS, D // 4, 2 * KV_W)
    dw_outb = dw_outb.reshape(N_CHIPS, D // 4, D)
    grads1 = [dw_kv, dw_bin, dw_outb]
    names1 = ["w_kv", "b_w_in", "b_w_out"]
    from_sibling1, _, _ = _sibling_exchange("to_sibling_1", to_sibling=grads1)
    sums1 = [_add_sibling(where, g, r, "add_sibling_" + nm) for g, r, nm in zip(grads1, from_sibling1, names1)]
    dproj, dconv_w, dw_outa, from_chips1 = _a_bwd(doa, ya, proj, conv_w, wouta, tm_mix, [t for t, _ in sums1])
    shards1 = [_add_chips(where, own, r, "add_chips_" + nm) for (_, own), r, nm in zip(sums1, from_chips1, names1)]
    dw_in = _dw(n1, dproj, D, tmw, "dw_a_in")
    grads2 = [dw_in, dw_outa.reshape(N_CHIPS, D // 4, D)]
    names2 = ["a_w_in", "a_w_out"]
    from_sibling2, (g_wkv, g_wbin, g_woutb), _ = _sibling_exchange("to_sibling_2", to_sibling=grads2, shards=shards1)
    sums2 = [_add_sibling(where, g, r, "add_sibling_" + nm) for g, r, nm in zip(grads2, from_sibling2, names2)]
    nt = seq // tm
    dn_first, from_chips2 = _a_in_bwd_matmul(dproj, win_g, tm, max(nt - max(nt // 4, 1), 1), [t for t, _ in sums2])
    grad_x, dg_apre = _a_in_bwd(dn_first, dproj, xs, dh1, win_g, g_apre, tm)
    shards2 = [_add_chips(where, own, r, "add_chips_" + nm) for (_, own), r, nm in zip(sums2, from_chips2, names2)]
    drel, dsink = _bias_fold(dtab)

    smalls = jnp.concatenate([
        dg_apre[0:1], dg_b[2:3], dg_b[0:1], dg_b[1:2], dg_bpost[0:1], _pad_cols(dsink[0:1], D),
        _pad_cols(loss_acc[0:1], D), jnp.zeros((1, D), F32), dconv_w, _pad_cols(drel, D)], axis=0)
    _, (g_win, g_wouta), small_all = _sibling_exchange("share_last", shards=shards2, smalls=smalls)
    tot = _sum_smalls(small_all)

    big = {}
    for nm, g, w, m, v in [("a_w_in", g_win, a_w_in, m_a_w_in, v_a_w_in), ("a_w_out", g_wouta, a_w_out, m_a_w_out, v_a_w_out),
                           ("w_kv", g_wkv, w_kv, m_w_kv, v_w_kv), ("b_w_in", g_wbin, b_w_in, m_b_w_in, v_b_w_in),
                           ("b_w_out", g_woutb, b_w_out, m_b_w_out, v_b_w_out)]:
        shp = w.shape
        two = (shp[-2], shp[-1])
        d, nm_, nv_ = _adamw(g, w.reshape(two), m.reshape(two), v.reshape(two), "adamw_" + nm)
        big[nm] = (g.reshape(shp), d.reshape(shp), nm_.reshape(shp), nv_.reshape(shp))

    given = {"a_pre_norm": (a_pre_norm, m_a_pre_norm, v_a_pre_norm), "a_conv_w": (a_conv_w, m_a_conv_w, v_a_conv_w),
             "a_post_norm": (a_post_norm, m_a_post_norm, v_a_post_norm), "kv_norm": (kv_norm, m_kv_norm, v_kv_norm),
             "rel_bias": (rel_bias, m_rel_bias, v_rel_bias), "b_pre_norm": (b_pre_norm, m_b_pre_norm, v_b_pre_norm),
             "b_sinks": (b_sinks, m_b_sinks, v_b_sinks), "b_post_norm": (b_post_norm, m_b_post_norm, v_b_post_norm)}
    small = _small_update(where[1:2], tot, {nm: tuple(a.reshape(SMALL_PLACES[nm][2]) for a in wmv)
                                            for nm, wmv in given.items()})
    order = ["a_pre_norm", "a_w_in", "a_conv_w", "a_w_out", "a_post_norm", "kv_norm", "w_kv", "rel_bias",
             "b_pre_norm", "b_w_in", "b_sinks", "b_w_out", "b_post_norm"]
    outs = []
    for which in range(4):
        for nm in order:
            outs.append(big[nm][which] if nm in big else small[nm][which].reshape(given[nm][0].shape))
    loss = 0.5 * tot[LOSS_ROW, 0]
    return (loss, grad_x.reshape(x.shape), *outs)
```

```python
import functools
import math

import jax
import jax.numpy as jnp
from jax import lax
from jax.experimental import pallas as pl
from jax.experimental.pallas import tpu as pltpu

F32 = jnp.float32
BF16 = jnp.bfloat16
MESH = pl.DeviceIdType.MESH
SDS = jax.ShapeDtypeStruct

D = 1024
HEAD_DIM = 64
N_HEADS = 16
N_KV = 2
GROUP = 8
KV_W = 128
BLK = 128
N_BUCKETS = 32
MAX_EXACT = 16
MAX_DISTANCE = 128
EPS = 1e-6
NEG_INF = -1e30
Q_SCALE = HEAD_DIM ** -0.5

ADAM_LR = 0.001
ADAM_B1 = 0.9
ADAM_B2 = 0.999
ADAM_EPS = 1e-08
ADAM_WD = 0.01
ADAM_STEP = 10

N_CHIPS = 4
N_DEV = 8
VMEM_LIMIT = 56 * 1024 * 1024
SMALL_ROWS = 48
LOSS_ROW = 6
SMALL_PLACES = {
    "a_pre_norm": (True, 0, (1, D // 4)), "a_conv_w": (True, 8, (3, D // 4)), "a_post_norm": (True, 1, (1, D // 4)),
    "kv_norm": (False, 2, (1, D)), "rel_bias": (False, 16, (N_BUCKETS, N_HEADS)), "b_pre_norm": (False, 3, (1, D)),
    "b_sinks": (False, 5, (1, N_HEADS)), "b_post_norm": (False, 4, (1, D)),
}
HALO = 16


def _bucket_thresholds():
    def bucket(d):
        big = MAX_EXACT + int(math.log(d / MAX_EXACT) / math.log(MAX_DISTANCE / MAX_EXACT)
                              * (N_BUCKETS - MAX_EXACT))
        return d if d < MAX_EXACT else min(big, N_BUCKETS - 1)
    out = []
    for b in range(MAX_EXACT + 1, N_BUCKETS):
        out.append(min(d for d in range(MAX_EXACT, MAX_DISTANCE) if bucket(d) >= b))
    return tuple(out)


BUCKET_THRESHOLDS = _bucket_thresholds()


def _params(semantics=None, vmem=VMEM_LIMIT):
    return pltpu.CompilerParams(dimension_semantics=semantics, vmem_limit_bytes=vmem)


def _tile(n, pref):
    return pref if n >= 2 * pref else max(n // 2, 8)


def _rms_scale(v):
    return lax.rsqrt(jnp.mean(v * v, axis=-1, keepdims=True) + EPS)


def _nt(a, b):
    return lax.dot_general(a, b, (((1,), (1,)), ((), ())), preferred_element_type=F32)


def _tn(a, b):
    return lax.dot_general(a, b, (((0,), (0,)), ((), ())), preferred_element_type=F32)


def _nn(a, b):
    return jnp.dot(a, b, preferred_element_type=F32)


def _silu_parts(z):
    sg = jax.nn.sigmoid(z)
    return sg, z * sg


def _dsilu(z, sg):
    return sg * (1.0 + z * (1.0 - sg))


def _acc_row(ref, row, val):
    ref[row:row + 1, :] += val


def _gather_copies(outs, splits, ici_send, ici_recv, d2d_send, d2d_recv):
    x, y, c = lax.axis_index("x"), lax.axis_index("y"), lax.axis_index("c")
    k = 2 * x + y
    sibling = (x, y, 1 - c)

    def part(o_ref, chip, core, split):
        if not split:
            return o_ref.at[chip]
        h = o_ref.shape[1] // 2
        return o_ref.at[chip, pl.ds(pl.multiple_of(core * h, 16), h)]

    def remote(ref, a, j, sems, to):
        return pltpu.make_async_remote_copy(src_ref=ref, dst_ref=ref, send_sem=sems[0].at[3 * a + j],
                                            recv_sem=sems[1].at[3 * a + j], device_id=to, device_id_type=MESH)

    copies = []
    for a, (o_ref, split) in enumerate(zip(outs, splits)):
        for j, (px, py) in enumerate([(x, 1 - y), (1 - x, y), (1 - x, 1 - y)]):
            kj = 2 * px + py
            ici, d2d = (ici_send, ici_recv), (d2d_send, d2d_recv)
            copies.append((remote(part(o_ref, k, c, split), a, j, ici, (px, py, c)),
                           remote(part(o_ref, kj, c, split), a, j, ici, (px, py, c)),
                           remote(part(o_ref, kj, c, split), a, j, d2d, sibling) if split else None,
                           remote(part(o_ref, kj, 1 - c, split), a, j, d2d, sibling) if split else None))
    return copies


def _gather_sems(n):
    return [pltpu.SemaphoreType.DMA((3 * n,)) for _ in range(4)]


def _gather_weights(shards, small, n_now):
    n = len(shards)

    def body(*refs):
        ins, small_in = refs[:n], refs[n]
        outs, small_out = refs[n + 1:2 * n + 1], refs[2 * n + 1]
        sems = refs[2 * n + 2:]
        k = 2 * lax.axis_index("x") + lax.axis_index("y")
        for i_ref, o_ref in zip(ins, outs):
            o_ref[k] = i_ref[...].astype(BF16)
        small_out[k] = small_in[...]
        copies = _gather_copies(list(outs[:n_now]) + [small_out], [True] * n_now + [False], *sems)
        for send, _, _, _ in copies:
            send.start()
        for _, arrival, forward, _ in copies:
            arrival.wait_recv()
            if forward is not None:
                forward.start()
        for send, _, forward, forwarded in copies:
            if forward is not None:
                forwarded.wait_recv()
                forward.wait_send()
            send.wait_send()

    vm = pl.BlockSpec(memory_space=pltpu.VMEM)
    out_shape = [SDS((N_CHIPS,) + s.shape, BF16) for s in shards] + [SDS((N_CHIPS,) + small.shape, F32)]
    return pl.pallas_call(
        body, name="gather_weights", out_shape=out_shape,
        in_specs=[vm] * (n + 1), out_specs=[vm] * (n + 1),
        scratch_shapes=_gather_sems(n_now + 1),
        compiler_params=pltpu.CompilerParams(vmem_limit_bytes=VMEM_LIMIT),
    )(*shards, small)


def _a_in(chip, x, g_pre, weights, tm):
    s = x.shape[0]
    nt = s // tm
    n = len(weights)

    def body(chip_ref, x_ref, g_ref, *refs):
        proj_ref, n1_ref = refs[n:n + 2]
        gathered = refs[n + 2:2 * n + 2]
        wbuf, n1_all, fetch_sem = refs[2 * n + 2:2 * n + 5]
        sems = refs[2 * n + 5:]
        jj, i = pl.program_id(0), pl.program_id(1)
        copies = _gather_copies(gathered, [True] * n, *sems)

        def fetch(rel):
            slot = jnp.bitwise_xor(chip_ref[0], rel)
            return pltpu.make_async_copy(gathered[0].at[slot], wbuf.at[rel % 2], fetch_sem.at[rel % 2])

        @pl.when((jj == 0) & (i == 0))
        def _():
            fetch(0).start()
            copies[0][0].start()
            copies[1][0].start()
            fetch(0).wait()

        for rel in (1, 2, 3):
            @pl.when((jj == rel) & (i == 0))
            def _():
                fetch(rel).wait()

        @pl.when(jj == 0)
        def _():
            xv = x_ref[...]
            n1 = (xv * _rms_scale(xv) * g_ref[...]).astype(BF16)
            n1_ref[...] = n1
            n1_all[i] = n1
        proj_ref[...] = _nn(n1_all[i], wbuf[jj % 2]).astype(BF16)

        for rel in (1, 2, 3):
            @pl.when((jj == rel - 1) & (i == max(nt - 3, nt // 2)))
            def _():
                _, arrival, forward, forwarded = copies[rel - 1]
                arrival.wait_recv()
                forward.start()
                forwarded.wait_recv()
                fetch(rel).start()
                if rel == 1:
                    copies[2][0].start()
                if rel == 2:
                    for send, _, _, _ in copies[3:]:
                        send.start()

        @pl.when((jj == 3) & (i == max(nt - 2, 0)))
        def _():
            for _, arrival, forward, _ in copies[3:]:
                arrival.wait_recv()
                forward.start()

        @pl.when((jj == 3) & (i == nt - 1))
        def _():
            for _, _, _, forwarded in copies[3:]:
                forwarded.wait_recv()
            for send, _, forward, _ in copies:
                forward.wait_send()
                send.wait_send()

    anyspace = pl.BlockSpec(memory_space=pl.ANY)
    proj, n1, *gathered = pl.pallas_call(
        body, name="a_in",
        grid_spec=pltpu.PrefetchScalarGridSpec(
            num_scalar_prefetch=1, grid=(4, nt),
            in_specs=[pl.BlockSpec((tm, D), lambda jj, i, c: (jnp.where(jj == 0, i, nt - 1), 0)),
                      pl.BlockSpec((1, D), lambda jj, i, c: (0, 0))] + [anyspace] * n,
            out_specs=[pl.BlockSpec((tm, D), lambda jj, i, c: (i, jnp.bitwise_xor(c[0], jj))),
                       pl.BlockSpec((tm, D), lambda jj, i, c: (jnp.where(jj == 0, i, nt - 1), 0))] + [anyspace] * n,
            scratch_shapes=[pltpu.VMEM((2, D, D), BF16), pltpu.VMEM((nt, tm, D), BF16),
                            pltpu.SemaphoreType.DMA((2,))] + _gather_sems(n)),
        out_shape=[SDS((s, 4 * D), BF16), SDS((s, D), BF16)] + [SDS(w.shape, w.dtype) for w in weights],
        input_output_aliases={3 + a: 2 + a for a in range(n)},
        compiler_params=_params(("arbitrary", "arbitrary")),
    )(chip, x, g_pre, *weights)
    return proj, n1, gathered


def _shift_rows(v, last, second_last, rows):
    v1 = jnp.where(rows >= 1, pltpu.roll(v, 1, 0), last)
    v2 = jnp.where(rows >= 2, pltpu.roll(v, 2, 0), jnp.where(rows == 1, last, second_last))
    return v1, v2


def _a_mix(proj, x, conv_w, w_out, g_post, tm):
    s = x.shape[0]

    def body(proj_ref, x_ref, cw_ref, w_ref, g_ref, ya_ref, oa_ref, h1_ref, carry):
        @pl.when(pl.program_id(0) == 0)
        def _():
            carry[...] = jnp.zeros_like(carry)
        v = proj_ref[:, D:2 * D].astype(F32) * proj_ref[:, 2 * D:3 * D].astype(F32)
        rows = lax.broadcasted_iota(jnp.int32, (tm, D), 0)
        before = carry[...]
        v1, v2 = _shift_rows(v, before[7:8, :], before[6:7, :], rows)
        carry[...] = v[tm - 8:tm, :]
        conv = cw_ref[0:1, :] * v2 + cw_ref[1:2, :] * v1 + cw_ref[2:3, :] * v
        _, sz = _silu_parts(proj_ref[:, 3 * D:4 * D].astype(F32))
        ya = (proj_ref[:, 0:D].astype(F32) * conv * sz).astype(BF16)
        ya_ref[...] = ya
        oa = _nn(ya, w_ref[...])
        oa_ref[...] = oa.astype(BF16)
        h1_ref[...] = x_ref[...] + oa * _rms_scale(oa) * g_ref[...]

    row = lambda i: (i, 0)
    fix = lambda i: (0, 0)
    return pl.pallas_call(
        body, name="a_mix", grid=(s // tm,),
        in_specs=[pl.BlockSpec((tm, 4 * D), row), pl.BlockSpec((tm, D), row), pl.BlockSpec((8, D), fix),
                  pl.BlockSpec((D, D), fix), pl.BlockSpec((1, D), fix)],
        out_specs=[pl.BlockSpec((tm, D), row)] * 3,
        out_shape=[SDS((s, D), BF16), SDS((s, D), BF16), SDS((s, D), F32)],
        scratch_shapes=[pltpu.VMEM((8, D), F32)],
        compiler_params=_params(("arbitrary",)),
    )(proj, x, conv_w, w_out, g_post)


def _b_in(h1, g_kv, g_pre, w_kv, wbin_g, tm):
    s = h1.shape[0]

    def body(h_ref, gk_ref, gb_ref, wkv_ref, wb_ref, kv_ref, q_ref, z_ref):
        h = h_ref[...]
        hh = h * _rms_scale(h)
        nk = (hh * gk_ref[...]).astype(BF16)
        nb = (hh * gb_ref[...]).astype(BF16)
        kv_ref[...] = _nn(nk, wkv_ref[...]).astype(BF16)
        for j in range(2):
            q_ref[:, 512 * j:512 * (j + 1)] = (_nn(nb, wb_ref[j]) * Q_SCALE).astype(BF16)
            z_ref[:, 512 * j:512 * (j + 1)] = _nn(nb, wb_ref[2 + j]).astype(BF16)

    row = lambda i: (i, 0)
    fix = lambda i: (0, 0)
    return pl.pallas_call(
        body, name="b_in", grid=(s // tm,),
        in_specs=[pl.BlockSpec((tm, D), row), pl.BlockSpec((1, D), fix), pl.BlockSpec((1, D), fix),
                  pl.BlockSpec((D, 2 * KV_W), fix), pl.BlockSpec((4, D, 512), lambda i: (0, 0, 0))],
        out_specs=[pl.BlockSpec((tm, 2 * KV_W), row), pl.BlockSpec((tm, D), row), pl.BlockSpec((tm, D), row)],
        out_shape=[SDS((s, 2 * KV_W), BF16), SDS((s, D), BF16), SDS((s, D), BF16)],
        compiler_params=_params(("parallel",)),
    )(h1, g_kv, g_pre, w_kv, wbin_g)


def _band_buckets():
    q = lax.broadcasted_iota(jnp.int32, (BLK, 2 * BLK), 0)
    k = lax.broadcasted_iota(jnp.int32, (BLK, 2 * BLK), 1)
    dist = q + BLK - k
    bucket = jnp.where(dist < MAX_EXACT, dist, MAX_EXACT)
    for t in BUCKET_THRESHOLDS:
        bucket = bucket + jnp.where(dist >= t, 1, 0)
    in_window = (dist >= 0) & (dist < BLK)
    return jnp.where(in_window, bucket, -1)


def _head_place(h):
    kh, j, e = h // GROUP, (h % GROUP) // 2, h % 2
    return kh, slice(BLK * j, BLK * (j + 1)), slice(2 * BLK * e, 2 * BLK * (e + 1))


def _bias_table(rel_bias, sinks):
    def body(rb_ref, sink_ref, tab_ref):
        bucket = _band_buckets()
        col = lax.broadcasted_iota(jnp.int32, (BLK, 2 * BLK), 1)
        for h in range(N_HEADS):
            acc = jnp.where(bucket < 0, NEG_INF, 0.0).astype(F32)
            for b in range(N_BUCKETS):
                acc = jnp.where(bucket == b, rb_ref[b, h], acc)
            acc = jnp.where(col == 0, sink_ref[h], acc)
            kh, rows, cols = _head_place(h)
            tab_ref[1, kh, rows, cols] = acc
            tab_ref[0, kh, rows, cols] = jnp.where((col > 0) & (col < BLK), NEG_INF, acc)

    return pl.pallas_call(
        body, name="bias_table", out_shape=SDS((2, N_KV, 4 * BLK, 4 * BLK), F32),
        in_specs=[pl.BlockSpec(memory_space=pltpu.SMEM), pl.BlockSpec(memory_space=pltpu.SMEM)],
        out_specs=pl.BlockSpec(memory_space=pltpu.VMEM),
    )(rel_bias, sinks)


def _bias_fold(dtab):
    def body(dtab_ref, out_ref, dsink_ref):
        bucket = _band_buckets()
        row = lax.broadcasted_iota(jnp.int32, (N_BUCKETS, 128), 0)
        lane = lax.broadcasted_iota(jnp.int32, (N_BUCKETS, 128), 1)
        row8 = lax.broadcasted_iota(jnp.int32, (8, 128), 0)
        lane8 = lax.broadcasted_iota(jnp.int32, (8, 128), 1)
        acc = jnp.zeros((N_BUCKETS, 128), F32)
        dsink = jnp.zeros((8, 128), F32)
        for h in range(N_HEADS):
            kh, rows, cols = _head_place(h)
            dt = dtab_ref[kh, rows, cols]
            for b in range(N_BUCKETS):
                val = jnp.sum(jnp.where(bucket == b, dt, 0.0))
                acc = acc + jnp.where((row == b) & (lane == h), val, 0.0)
            dsink = dsink + jnp.where((row8 == 0) & (lane8 == h), jnp.sum(dt[:, 0:1]), 0.0)
        out_ref[...] = acc
        dsink_ref[...] = dsink

    vm = pl.BlockSpec(memory_space=pltpu.VMEM)
    return pl.pallas_call(
        body, name="bias_fold", out_shape=[SDS((N_BUCKETS, 128), F32), SDS((8, 128), F32)],
        in_specs=[vm], out_specs=[vm, vm],
    )(dtab)


def _pair_operands(prev, cur):
    t = jnp.concatenate([prev, cur], axis=0).astype(F32)
    t = jnp.where(lax.broadcasted_iota(jnp.int32, t.shape, 0) == 0, 0.0, t)
    tr = pltpu.roll(t, HEAD_DIM, 1)
    lo = lax.broadcasted_iota(jnp.int32, t.shape, 1) < HEAD_DIM
    zero = jnp.zeros_like(t)
    head0 = jnp.concatenate([jnp.where(lo, t, zero), jnp.where(lo, zero, tr)], axis=0).astype(BF16)
    head1 = jnp.concatenate([jnp.where(lo, tr, zero), jnp.where(lo, zero, t)], axis=0).astype(BF16)
    return head0, head1


def _pair_fold(d0, d1):
    lo = lax.broadcasted_iota(jnp.int32, (2 * BLK, KV_W), 1) < HEAD_DIM
    zero = jnp.zeros((2 * BLK, KV_W), F32)
    g0 = jnp.where(lo, d0[0:256], zero) + pltpu.roll(jnp.where(lo, zero, d0[256:512]), HEAD_DIM, 1)
    g1 = pltpu.roll(jnp.where(lo, d1[0:256], zero), HEAD_DIM, 1) + jnp.where(lo, zero, d1[256:512])
    return jnp.where(lax.broadcasted_iota(jnp.int32, (2 * BLK, KV_W), 0) == 0, 0.0, g0 + g1)


def _stack_pairs(ref, kh, rows=slice(None)):
    return jnp.concatenate([ref[rows, 128 * (4 * kh + j):128 * (4 * kh + j + 1)] for j in range(4)], axis=0)


def _table_spec():
    return pl.BlockSpec((1, N_KV, 4 * BLK, 4 * BLK), lambda n: (jnp.minimum(n, 1), 0, 0, 0))


def _attn_fwd(q, kv, tab):
    s = q.shape[0]

    def body(q_ref, kp_ref, k0_ref, k1_ref, vp_ref, v0_ref, v1_ref, tab0_ref, tab1_ref, att_ref, stats_ref):
        lane = lax.broadcasted_iota(jnp.int32, (BLK, 128), 1)
        for sub, (kp, kc, vp, vc, tab_ref) in enumerate([(kp_ref, k0_ref, vp_ref, v0_ref, tab0_ref),
                                                         (k0_ref, k1_ref, v0_ref, v1_ref, tab1_ref)]):
            rows = slice(BLK * sub, BLK * (sub + 1))
            k2 = _pair_operands(kp[...], kc[...])
            v2 = _pair_operands(vp[...], vc[...])
            stats = jnp.zeros((BLK, 128), F32)
            for kh in range(N_KV):
                sc = _nt(_stack_pairs(q_ref, kh, rows), k2[kh])
                ps = []
                for e in range(2):
                    lg = sc[:, 256 * e:256 * (e + 1)] + tab_ref[0, kh, :, 256 * e:256 * (e + 1)]
                    m = jnp.max(lg, axis=-1, keepdims=True)
                    ex = jnp.exp(lg - m)
                    den = jnp.sum(ex, axis=-1, keepdims=True)
                    ps.append(ex * (1.0 / den))
                    lse = m + jnp.log(den)
                    for j in range(4):
                        stats = jnp.where(lane == GROUP * kh + 2 * j + e, lse[BLK * j:BLK * (j + 1)], stats)
                out = _nn(jnp.concatenate(ps, axis=1).astype(BF16), v2[kh])
                for j in range(4):
                    att_ref[rows, 128 * (4 * kh + j):128 * (4 * kh + j + 1)] = out[BLK * j:BLK * (j + 1)].astype(BF16)
            stats_ref[rows, :] = stats

    two = lambda m: (m, 0)
    table = lambda pick: pl.BlockSpec((1, N_KV, 4 * BLK, 4 * BLK), lambda m: (pick(m), 0, 0, 0))
    return pl.pallas_call(
        body, name="attn_fwd", grid=(s // (2 * BLK),),
        in_specs=[pl.BlockSpec((2 * BLK, D), two)]
        + [pl.BlockSpec((BLK, KV_W), lambda m, col=col, off=off: (jnp.maximum(2 * m + off, 0), col))
           for col in (0, 1) for off in (-1, 0, 1)]
        + [table(lambda m: jnp.minimum(m, 1)), table(lambda m: 1)],
        out_specs=[pl.BlockSpec((2 * BLK, D), two), pl.BlockSpec((2 * BLK, 128), two)],
        out_shape=[SDS((s, D), BF16), SDS((s, 128), F32)],
        compiler_params=_params(("parallel",)),
    )(q, kv, kv, kv, kv, kv, kv, tab, tab)


def _mid(att, zb, h1, tgt, w_out, g_post, tm):
    s = att.shape[0]
    nt = s // tm

    def body(att_ref, z_ref, h1_ref, t_ref, w_ref, g_ref,
             dh_ref, dqz_ref, datt_ref, loss_ref, dg_ref, dw_ref, dw_acc):
        @pl.when(pl.program_id(0) == 0)
        def _():
            loss_ref[...] = jnp.zeros_like(loss_ref)
            dg_ref[...] = jnp.zeros_like(dg_ref)
            dw_acc[...] = jnp.zeros_like(dw_acc)
        att = att_ref[...].astype(F32)
        z = z_ref[...].astype(F32)
        sg, sz = _silu_parts(z)
        ob = (att * sz).astype(BF16)
        y2 = _nn(ob, w_ref[...])
        r2 = _rms_scale(y2)
        yh = y2 * r2
        g = g_ref[...]
        err = (h1_ref[...] + yh * g) - t_ref[...]
        loss_ref[...] += jnp.sum(jnp.sum(err * err, axis=-1, keepdims=True) / D)
        dh = err / D
        dh_ref[...] = dh
        _acc_row(dg_ref, 0, jnp.sum(dh * yh, axis=0, keepdims=True))
        dyh = dh * g
        dy = (r2 * (dyh - yh * jnp.mean(dyh * yh, axis=-1, keepdims=True))).astype(BF16)
        dw_acc[...] += _tn(ob, dy)
        dob = _nt(dy, w_ref[...])
        datt_ref[...] = (dob * sz).astype(BF16)
        dqz_ref[...] = (dob * att * _dsilu(z, sg)).astype(BF16)

        @pl.when(pl.program_id(0) == nt - 1)
        def _():
            pltpu.sync_copy(dw_acc, dw_ref)

    row = lambda i: (i, 0)
    fix = lambda i: (0, 0)
    return pl.pallas_call(
        body, name="mid", grid=(nt,),
        in_specs=[pl.BlockSpec((tm, D), row)] * 4 + [pl.BlockSpec((D, D), fix), pl.BlockSpec((1, D), fix)],
        out_specs=[pl.BlockSpec((tm, D), row), pl.BlockSpec((tm, D), lambda i: (i, 1)), pl.BlockSpec((tm, D), row),
                   pl.BlockSpec((8, 128), fix), pl.BlockSpec((8, D), fix), pl.BlockSpec(memory_space=pl.ANY)],
        out_shape=[SDS((s, D), F32), SDS((s, 2 * D), BF16), SDS((s, D), BF16), SDS((8, 128), F32),
                   SDS((8, D), F32), SDS((D, D), F32)],
        scratch_shapes=[pltpu.VMEM((D, D), F32)],
        compiler_params=_params(("arbitrary",)),
    )(att, zb, h1, tgt, w_out, g_post)


def _attn_bwd(q, kv, datt, stats, tab, dqz):
    s = q.shape[0]
    nb = s // BLK

    def body(q_ref, kp_ref, kc_ref, vp_ref, vc_ref, da_ref, st_ref, tab_ref, dqz_in,
             dq_ref, dkv_ref, dtab_ref, dk_carry, dv_carry):
        del dqz_in
        n = pl.program_id(0)

        @pl.when(n == 0)
        def _():
            dtab_ref[...] = jnp.zeros_like(dtab_ref)
            dk_carry[...] = jnp.zeros_like(dk_carry)
            dv_carry[...] = jnp.zeros_like(dv_carry)

        @pl.when(n < nb)
        def _():
            k2 = _pair_operands(kp_ref[...], kc_ref[...])
            v2 = _pair_operands(vp_ref[...], vc_ref[...])
            lane = lax.broadcasted_iota(jnp.int32, (BLK, 128), 1)
            stats = st_ref[...]
            dk2, dv2 = [], []
            for kh in range(N_KV):
                qs = _stack_pairs(q_ref, kh)
                das = _stack_pairs(da_ref, kh)
                sc = _nt(qs, k2[kh])
                dp = _nt(das, v2[kh])
                ps, dss = [], []
                for e in range(2):
                    heads = [GROUP * kh + 2 * j + e for j in range(4)]
                    lse = jnp.concatenate([jnp.sum(jnp.where(lane == h, stats, 0.0), axis=-1, keepdims=True)
                                           for h in heads], axis=0)
                    cols = slice(256 * e, 256 * (e + 1))
                    p = jnp.exp(sc[:, cols] + tab_ref[0, kh, :, cols] - lse)
                    delta = jnp.sum(p * dp[:, cols], axis=-1, keepdims=True)
                    ds = p * (dp[:, cols] - delta)
                    dtab_ref[kh, :, cols] += ds
                    ps.append(p)
                    dss.append(ds)
                p2 = jnp.concatenate(ps, axis=1).astype(BF16)
                ds2 = jnp.concatenate(dss, axis=1).astype(BF16)
                dq = _nn(ds2, k2[kh]) * Q_SCALE
                for j in range(4):
                    dq_ref[:, 128 * (4 * kh + j):128 * (4 * kh + j + 1)] = dq[BLK * j:BLK * (j + 1)].astype(BF16)
                dk2.append(_tn(ds2, qs))
                dv2.append(_tn(p2, das))
            dkk = _pair_fold(dk2[0], dk2[1])
            dvv = _pair_fold(dv2[0], dv2[1])
            dkv_ref[:, 0:KV_W] = (dk_carry[...] + dkk[0:BLK]).astype(BF16)
            dkv_ref[:, KV_W:2 * KV_W] = (dv_carry[...] + dvv[0:BLK]).astype(BF16)
            dk_carry[...] = dkk[BLK:2 * BLK]
            dv_carry[...] = dvv[BLK:2 * BLK]

        @pl.when(n == nb)
        def _():
            dkv_ref[:, 0:KV_W] = dk_carry[...].astype(BF16)
            dkv_ref[:, KV_W:2 * KV_W] = dv_carry[...].astype(BF16)

    cur = lambda n: (jnp.minimum(n, nb - 1), 0)
    prev = lambda n: (jnp.clip(n - 1, 0, nb - 1), 0)
    return pl.pallas_call(
        body, name="attn_bwd", grid=(nb + 1,),
        in_specs=[pl.BlockSpec((BLK, D), cur),
                  pl.BlockSpec((BLK, KV_W), prev), pl.BlockSpec((BLK, KV_W), cur),
                  pl.BlockSpec((BLK, KV_W), lambda n: (jnp.clip(n - 1, 0, nb - 1), 1)),
                  pl.BlockSpec((BLK, KV_W), lambda n: (jnp.minimum(n, nb - 1), 1)),
                  pl.BlockSpec((BLK, D), cur), pl.BlockSpec((BLK, 128), cur), _table_spec(),
                  pl.BlockSpec(memory_space=pl.ANY)],
        out_specs=[pl.BlockSpec((BLK, D), cur), pl.BlockSpec((BLK, 2 * KV_W), prev),
                   pl.BlockSpec((N_KV, 4 * BLK, 4 * BLK), lambda n: (0, 0, 0))],
        out_shape=[SDS((s, 2 * D), BF16), SDS((s, 2 * KV_W), BF16), SDS((N_KV, 4 * BLK, 4 * BLK), F32)],
        scratch_shapes=[pltpu.VMEM((BLK, KV_W), F32), pltpu.VMEM((BLK, KV_W), F32)],
        input_output_aliases={8: 0},
        compiler_params=_params(("arbitrary",)),
    )(q, kv, kv, kv, kv, datt, stats, tab, dqz)


def _b_bwd(dqz, dkv, h1, dh2, oa, wbin_g, w_kv, g_kv, g_pre, g_apost, tm):
    s = h1.shape[0]
    nt = s // tm

    def body(dqz_ref, dkv_ref, h_ref, dh2_ref, oa_ref, wb_ref, wkv_ref, gk_ref, gb_ref, ga_ref,
             dh1_ref, doa_ref, dg_ref, dwb_ref, dwkv_ref, dwb_acc, dwkv_acc):
        @pl.when(pl.program_id(0) == 0)
        def _():
            dg_ref[...] = jnp.zeros_like(dg_ref)
            dwb_acc[...] = jnp.zeros_like(dwb_acc)
            dwkv_acc[...] = jnp.zeros_like(dwkv_acc)
        dnb = _nt(dqz_ref[:, 0:512], wb_ref[0])
        for j in range(1, 4):
            dnb = dnb + _nt(dqz_ref[:, 512 * j:512 * (j + 1)], wb_ref[j])
        dnk = _nt(dkv_ref[...], wkv_ref[...])
        h = h_ref[...]
        r = _rms_scale(h)
        hh = h * r
        nb = (hh * gb_ref[...]).astype(BF16)
        for j in range(4):
            dwb_acc[j] += _tn(nb, dqz_ref[:, 512 * j:512 * (j + 1)])
        dwkv_acc[...] += _tn((hh * gk_ref[...]).astype(BF16), dkv_ref[...])
        _acc_row(dg_ref, 0, jnp.sum(dnk * hh, axis=0, keepdims=True))
        _acc_row(dg_ref, 1, jnp.sum(dnb * hh, axis=0, keepdims=True))
        dhh = dnb * gb_ref[...] + dnk * gk_ref[...]
        dh1 = dh2_ref[...] + r * (dhh - hh * jnp.mean(dhh * hh, axis=-1, keepdims=True))
        dh1_ref[...] = dh1
        oa = oa_ref[...].astype(F32)
        ra = _rms_scale(oa)
        oh = oa * ra
        _acc_row(dg_ref, 2, jnp.sum(dh1 * oh, axis=0, keepdims=True))
        doh = dh1 * ga_ref[...]
        doa_ref[...] = (ra * (doh - oh * jnp.mean(doh * oh, axis=-1, keepdims=True))).astype(BF16)

        @pl.when(pl.program_id(0) == nt - 1)
        def _():
            pltpu.sync_copy(dwb_acc, dwb_ref)
            pltpu.sync_copy(dwkv_acc, dwkv_ref)

    row = lambda i: (i, 0)
    fix = lambda i: (0, 0)
    anyspace = pl.BlockSpec(memory_space=pl.ANY)
    return pl.pallas_call(
        body, name="b_bwd", grid=(nt,),
        in_specs=[pl.BlockSpec((tm, 2 * D), row), pl.BlockSpec((tm, 2 * KV_W), row), pl.BlockSpec((tm, D), row),
                  pl.BlockSpec((tm, D), row), pl.BlockSpec((tm, D), row),
                  pl.BlockSpec((4, D, 512), lambda i: (0, 0, 0)), pl.BlockSpec((D, 2 * KV_W), fix),
                  pl.BlockSpec((1, D), fix), pl.BlockSpec((1, D), fix), pl.BlockSpec((1, D), fix)],
        out_specs=[pl.BlockSpec((tm, D), row), pl.BlockSpec((tm, D), row), pl.BlockSpec((8, D), fix), anyspace, anyspace],
        out_shape=[SDS((s, D), F32), SDS((s, D), BF16), SDS((8, D), F32), SDS((4, D, 512), F32),
                   SDS((D, 2 * KV_W), F32)],
        scratch_shapes=[pltpu.VMEM((4, D, 512), F32), pltpu.VMEM((D, 2 * KV_W), F32)],
        compiler_params=_params(("arbitrary",)),
    )(dqz, dkv, h1, dh2, oa, wbin_g, w_kv, g_kv, g_pre, g_apost)


def _chip_exchange(parts, recvs, send, recv):
    x, y, c = lax.axis_index("x"), lax.axis_index("y"), lax.axis_index("c")
    chips = [(x, 1 - y), (1 - x, y), (1 - x, 1 - y)]
    copies = []
    for a, (t, r) in enumerate(zip(parts, recvs)):
        for j, (px, py) in enumerate(chips):
            copies.append(pltpu.make_async_remote_copy(
                src_ref=t.at[2 * px + py], dst_ref=r.at[j], send_sem=send.at[3 * a + j],
                recv_sem=recv.at[3 * a + j], device_id=(px, py, c), device_id_type=MESH))
    return copies


def _exchange_specs(parts):
    anyspace = pl.BlockSpec(memory_space=pl.ANY)
    n = len(parts)
    return ([anyspace] * n, [anyspace] * n, [SDS((3,) + t.shape[1:], t.dtype) for t in parts],
            [pltpu.SemaphoreType.DMA((3 * n,)), pltpu.SemaphoreType.DMA((3 * n,))])


def _a_bwd(doa, ya, proj, conv_w, w_out, tm, parts):
    s = doa.shape[0]
    nt = s // tm
    n = len(parts)
    ex_in, ex_out, ex_shape, ex_sems = _exchange_specs(parts)

    def body(*refs):
        doa_ref, ya_ref, proj_ref, halo_ref, cw_ref, w_ref = refs[:6]
        part_refs = refs[6:6 + n]
        dproj_ref, dcw_ref, dw_ref = refs[6 + n:9 + n]
        recv_refs = refs[9 + n:9 + 2 * n]
        carry, dw_acc, send, recv = refs[9 + 2 * n:]
        i = pl.program_id(0)
        r = nt - 1 - i

        @pl.when(i == 0)
        def _():
            dcw_ref[...] = jnp.zeros_like(dcw_ref)
            carry[...] = jnp.zeros_like(carry)
            dw_acc[...] = jnp.zeros_like(dw_acc)
            for cp in _chip_exchange(part_refs, recv_refs, send, recv):
                cp.start()
        dya = _nt(doa_ref[...], w_ref[...])
        dw_acc[...] += _tn(ya_ref[...], doa_ref[...])
        bg = proj_ref[:, 0:D].astype(F32)
        cg = proj_ref[:, D:2 * D].astype(F32)
        u = proj_ref[:, 2 * D:3 * D].astype(F32)
        z = proj_ref[:, 3 * D:4 * D].astype(F32)
        v = cg * u
        before = jnp.where(r > 0, halo_ref[:, D:2 * D].astype(F32) * halo_ref[:, 2 * D:3 * D].astype(F32), 0.0)
        rows = lax.broadcasted_iota(jnp.int32, (tm, D), 0)
        v1, v2 = _shift_rows(v, before[HALO - 1:HALO, :], before[HALO - 2:HALO - 1, :], rows)
        conv = cw_ref[0:1, :] * v2 + cw_ref[1:2, :] * v1 + cw_ref[2:3, :] * v
        sg, sz = _silu_parts(z)
        dproj_ref[:, 0:D] = (dya * conv * sz).astype(BF16)
        dproj_ref[:, 3 * D:4 * D] = (dya * bg * conv * _dsilu(z, sg)).astype(BF16)
        dconv = dya * bg * sz
        _acc_row(dcw_ref, 0, jnp.sum(dconv * v2, axis=0, keepdims=True))
        _acc_row(dcw_ref, 1, jnp.sum(dconv * v1, axis=0, keepdims=True))
        _acc_row(dcw_ref, 2, jnp.sum(dconv * v, axis=0, keepdims=True))
        after = carry[...]
        up1 = jnp.where(rows < tm - 1, pltpu.roll(dconv, tm - 1, 0), after[0:1, :])
        up2 = jnp.where(rows < tm - 2, pltpu.roll(dconv, tm - 2, 0),
                        jnp.where(rows == tm - 2, after[0:1, :], after[1:2, :]))
        carry[...] = dconv[0:8, :]
        dv = cw_ref[2:3, :] * dconv + cw_ref[1:2, :] * up1 + cw_ref[0:1, :] * up2
        dproj_ref[:, D:2 * D] = (dv * u).astype(BF16)
        dproj_ref[:, 2 * D:3 * D] = (dv * cg).astype(BF16)

        @pl.when(i == nt - 1)
        def _():
            pltpu.sync_copy(dw_acc, dw_ref)
            for cp in _chip_exchange(part_refs, recv_refs, send, recv):
                cp.wait()

    rev = lambda i: (nt - 1 - i, 0)
    fix = lambda i: (0, 0)
    halo = lambda i: (jnp.maximum((nt - 1 - i) * (tm // HALO) - 1, 0), 0)
    dproj, dcw, dw, *got = pl.pallas_call(
        body, name="a_bwd", grid=(nt,),
        in_specs=[pl.BlockSpec((tm, D), rev), pl.BlockSpec((tm, D), rev), pl.BlockSpec((tm, 4 * D), rev),
                  pl.BlockSpec((HALO, 4 * D), halo), pl.BlockSpec((8, D), fix), pl.BlockSpec((D, D), fix)] + ex_in,
        out_specs=[pl.BlockSpec((tm, 4 * D), rev), pl.BlockSpec((8, D), fix), pl.BlockSpec(memory_space=pl.ANY)] + ex_out,
        out_shape=[SDS((s, 4 * D), BF16), SDS((8, D), F32), SDS((D, D), F32)] + ex_shape,
        scratch_shapes=[pltpu.VMEM((8, D), F32), pltpu.VMEM((D, D), F32)] + ex_sems,
        compiler_params=_params(("arbitrary",)),
    )(doa, ya, proj, proj, conv_w, w_out, *parts)
    return dproj, dcw, dw, got


def _dn1(dp_ref, w_ref):
    dn = _nt(dp_ref[:, 0:D], w_ref[0])
    for j in range(1, 4):
        dn = dn + _nt(dp_ref[:, D * j:D * (j + 1)], w_ref[j])
    return dn


def _a_in_bwd_matmul(dproj, win_g, tm, count, parts):
    n = len(parts)
    ex_in, ex_out, ex_shape, ex_sems = _exchange_specs(parts)

    def body(*refs):
        dp_ref, w_ref = refs[:2]
        part_refs = refs[2:2 + n]
        dn_ref = refs[2 + n]
        recv_refs = refs[3 + n:3 + 2 * n]
        sems = refs[3 + 2 * n:]

        @pl.when(pl.program_id(0) == 0)
        def _():
            for cp in _chip_exchange(part_refs, recv_refs, *sems):
                cp.start()
        dn_ref[...] = _dn1(dp_ref, w_ref).astype(BF16)

        @pl.when(pl.program_id(0) == count - 1)
        def _():
            for cp in _chip_exchange(part_refs, recv_refs, *sems):
                cp.wait()

    row = lambda i: (i, 0)
    dn, *got = pl.pallas_call(
        body, name="a_in_bwd_matmul", grid=(count,),
        in_specs=[pl.BlockSpec((tm, 4 * D), row), pl.BlockSpec((4, D, D), lambda i: (0, 0, 0))] + ex_in,
        out_specs=[pl.BlockSpec((tm, D), row)] + ex_out,
        out_shape=[SDS((count * tm, D), BF16)] + ex_shape,
        scratch_shapes=ex_sems,
        compiler_params=_params(("arbitrary",)),
    )(dproj, win_g, *parts)
    return dn, got


def _a_in_bwd(dn_first, dproj, x, dh1, win_g, g_pre, tm):
    s = x.shape[0]
    nt = s // tm
    count = dn_first.shape[0] // tm

    def body(dn_ref, dp_ref, x_ref, dh_ref, w_ref, g_ref, gx_ref, dg_ref, dn_s):
        i = pl.program_id(0)

        @pl.when(i == 0)
        def _():
            dg_ref[...] = jnp.zeros_like(dg_ref)

        @pl.when(i < count)
        def _():
            dn_s[...] = dn_ref[...].astype(F32)

        @pl.when(i >= count)
        def _():
            dn_s[...] = _dn1(dp_ref, w_ref)
        dn = dn_s[...]
        xv = x_ref[...]
        r = _rms_scale(xv)
        xh = xv * r
        _acc_row(dg_ref, 0, jnp.sum(dn * xh, axis=0, keepdims=True))
        dxh = dn * g_ref[...]
        gx_ref[...] = dh_ref[...] + r * (dxh - xh * jnp.mean(dxh * xh, axis=-1, keepdims=True))

    row = lambda i: (i, 0)
    fix = lambda i: (0, 0)
    return pl.pallas_call(
        body, name="a_in_bwd", grid=(nt,),
        in_specs=[pl.BlockSpec((tm, D), lambda i: (jnp.minimum(i, count - 1), 0)),
                  pl.BlockSpec((tm, 4 * D), lambda i: (jnp.maximum(i, count), 0)),
                  pl.BlockSpec((tm, D), row), pl.BlockSpec((tm, D), row),
                  pl.BlockSpec((4, D, D), lambda i: (0, 0, 0)), pl.BlockSpec((1, D), fix)],
        out_specs=[pl.BlockSpec((tm, D), row), pl.BlockSpec((8, D), fix)],
        out_shape=[SDS((s, D), F32), SDS((8, D), F32)],
        scratch_shapes=[pltpu.VMEM((tm, D), F32)],
        compiler_params=_params(("arbitrary",)),
    )(dn_first, dproj, x, dh1, win_g, g_pre)


def _dw(a, b, tn, tmw, name):
    s, k = a.shape
    n = b.shape[1]

    def body(a_ref, b_ref, o_ref):
        @pl.when(pl.program_id(1) == 0)
        def _():
            o_ref[...] = jnp.zeros_like(o_ref)
        o_ref[0] += _tn(a_ref[...], b_ref[...])

    return pl.pallas_call(
        body, name=name, grid=(n // tn, s // tmw),
        in_specs=[pl.BlockSpec((tmw, k), lambda j, t: (t, 0)), pl.BlockSpec((tmw, tn), lambda j, t: (t, j))],
        out_specs=pl.BlockSpec((1, k, tn), lambda j, t: (j, 0, 0)),
        out_shape=SDS((n // tn, k, tn), F32),
        compiler_params=_params(("parallel", "arbitrary")),
    )(a, b)


def _sibling_exchange(name, to_sibling=(), shards=(), smalls=None):
    n_g, n_h = len(to_sibling), len(shards)
    has_small = smalls is not None

    def body(*refs):
        gs = refs[:n_g]
        pos = n_g + n_h
        small_in = refs[pos] if has_small else None
        pos += has_small
        rs, fs = refs[pos:pos + n_g], refs[pos + n_g:pos + n_g + n_h]
        pos += n_g + n_h
        small_all = refs[pos] if has_small else None
        pos += has_small
        dsend, drecv, ssend, srecv = refs[pos:]
        x, y, c = lax.axis_index("x"), lax.axis_index("y"), lax.axis_index("c")
        sibling = (x, y, 1 - c)
        sends, arrivals = [], []
        for a, (g, r) in enumerate(zip(gs, rs)):
            h = g.shape[1] // 2
            src = g.at[:, pl.ds(pl.multiple_of((1 - c) * h, 8), h), :]
            sends.append(pltpu.make_async_remote_copy(src_ref=src, dst_ref=r, send_sem=dsend.at[a], recv_sem=drecv.at[a],
                                                      device_id=sibling, device_id_type=MESH))
            arrivals.append(pltpu.make_async_remote_copy(src_ref=r, dst_ref=r, send_sem=dsend.at[a], recv_sem=drecv.at[a],
                                                         device_id=sibling, device_id_type=MESH))
        for b, full in enumerate(fs):
            h = full.shape[0] // 2
            mine = full.at[pl.ds(pl.multiple_of(c * h, 8), h)]
            theirs = full.at[pl.ds(pl.multiple_of((1 - c) * h, 8), h)]
            sends.append(pltpu.make_async_remote_copy(src_ref=mine, dst_ref=mine, send_sem=dsend.at[n_g + b],
                                                      recv_sem=drecv.at[n_g + b], device_id=sibling, device_id_type=MESH))
            arrivals.append(pltpu.make_async_remote_copy(src_ref=mine, dst_ref=theirs, send_sem=dsend.at[n_g + b],
                                                         recv_sem=drecv.at[n_g + b], device_id=sibling, device_id_type=MESH))
        if has_small:
            me = 4 * x + 2 * y + c
            small_all[me] = small_in[...]
            for rel in range(1, N_DEV):
                fx, fy, fc = rel >> 2, (rel >> 1) & 1, rel & 1
                peer = (x + fx - 2 * x * fx, y + fy - 2 * y * fy, c + fc - 2 * c * fc)
                sender = 4 * peer[0] + 2 * peer[1] + peer[2]
                sends.append(pltpu.make_async_remote_copy(
                    src_ref=small_in, dst_ref=small_all.at[me], send_sem=ssend.at[rel - 1], recv_sem=srecv.at[rel - 1],
                    device_id=peer, device_id_type=MESH))
                arrivals.append(pltpu.make_async_remote_copy(
                    src_ref=small_in, dst_ref=small_all.at[sender], send_sem=ssend.at[rel - 1], recv_sem=srecv.at[rel - 1],
                    device_id=peer, device_id_type=MESH))
        for cp in sends:
            cp.start()
        for cp in arrivals:
            cp.wait_recv()
        for cp in sends:
            cp.wait_send()

    anyspace = pl.BlockSpec(memory_space=pl.ANY)
    vm = pl.BlockSpec(memory_space=pltpu.VMEM)
    out_shape = [SDS((N_CHIPS, g.shape[1] // 2, g.shape[2]), F32) for g in to_sibling]
    out_shape += [SDS(full.shape, F32) for full in shards]
    if has_small:
        out_shape.append(SDS((N_DEV,) + smalls.shape, F32))
    n_d2d = max(n_g + n_h, 1)
    outs = pl.pallas_call(
        body, name=name, out_shape=out_shape,
        in_specs=[anyspace] * (n_g + n_h) + [vm] * has_small, out_specs=[anyspace] * (n_g + n_h) + [vm] * has_small,
        scratch_shapes=[pltpu.SemaphoreType.DMA((n_d2d,)), pltpu.SemaphoreType.DMA((n_d2d,)),
                        pltpu.SemaphoreType.DMA((N_DEV - 1,)), pltpu.SemaphoreType.DMA((N_DEV - 1,))],
        input_output_aliases={n_g + b: n_g + b for b in range(n_h)},
    )(*to_sibling, *shards, *([smalls] if has_small else []))
    return outs[:n_g], outs[n_g:n_g + n_h], (outs[n_g + n_h] if has_small else None)


def _add_sibling(where, g, r, name):
    _, rows, cols = g.shape
    h = rows // 2
    tr = min(h, 256)
    nh = h // tr

    def body(where_ref, g_ref, r_ref, t_ref, own_ref):
        t = g_ref[0] + r_ref[0]
        t_ref[0] = t.astype(BF16)

        @pl.when(pl.program_id(1) == where_ref[1])
        def _():
            own_ref[...] = t

    return pl.pallas_call(
        body, name=name,
        grid_spec=pltpu.PrefetchScalarGridSpec(
            num_scalar_prefetch=1, grid=(nh, N_CHIPS),
            in_specs=[pl.BlockSpec((1, tr, cols), lambda i, k, w: (k, w[0] * nh + i, 0)),
                      pl.BlockSpec((1, tr, cols), lambda i, k, w: (k, i, 0))],
            out_specs=[pl.BlockSpec((1, tr, cols), lambda i, k, w: (k, i, 0)),
                       pl.BlockSpec((tr, cols), lambda i, k, w: (i, 0))]),
        out_shape=[SDS((N_CHIPS, h, cols), BF16), SDS((h, cols), F32)],
        compiler_params=_params(("parallel", "arbitrary")),
    )(where, g, r)


def _add_chips(where, own, r, name):
    h, cols = own.shape
    tr = min(h, 256)
    nh = h // tr

    def body(where_ref, t_ref, r_ref, o_ref):
        del where_ref
        o_ref[...] = ((t_ref[...] + r_ref[0].astype(F32)) + r_ref[1].astype(F32)) + r_ref[2].astype(F32)

    return pl.pallas_call(
        body, name=name,
        grid_spec=pltpu.PrefetchScalarGridSpec(
            num_scalar_prefetch=1, grid=(nh,),
            in_specs=[pl.BlockSpec((tr, cols), lambda i, w: (i, 0)), pl.BlockSpec((3, tr, cols), lambda i, w: (0, i, 0))],
            out_specs=pl.BlockSpec((tr, cols), lambda i, w: (w[0] * nh + i, 0))),
        out_shape=SDS((2 * h, cols), F32),
        compiler_params=_params(("parallel",)),
    )(where, own, r)


def _sum_smalls(small_all):
    def body(all_ref, o_ref):
        acc = all_ref[0]
        for dev in range(1, N_DEV):
            acc = acc + all_ref[dev]
        o_ref[...] = acc

    return pl.pallas_call(
        body, name="sum_smalls", out_shape=SDS(small_all.shape[1:], F32),
        in_specs=[pl.BlockSpec(memory_space=pltpu.VMEM)], out_specs=pl.BlockSpec(memory_space=pltpu.VMEM),
    )(small_all)


def _adam_step(g, w, m, v):
    nm = ADAM_B1 * m + (1.0 - ADAM_B1) * g
    nv = ADAM_B2 * v + (1.0 - ADAM_B2) * (g * g)
    m_hat = nm / (1.0 - ADAM_B1 ** ADAM_STEP)
    v_hat = nv / (1.0 - ADAM_B2 ** ADAM_STEP)
    return -ADAM_LR * (m_hat / (jnp.sqrt(v_hat) + ADAM_EPS) + ADAM_WD * w), nm, nv


def _adamw(g, w, m, v, name):
    rows, cols = g.shape
    tr = min(rows, 256)

    def body(g_ref, w_ref, m_ref, v_ref, d_ref, nm_ref, nv_ref):
        d_ref[...], nm_ref[...], nv_ref[...] = _adam_step(g_ref[...], w_ref[...], m_ref[...], v_ref[...])

    spec = pl.BlockSpec((tr, cols), lambda i: (i, 0))
    return pl.pallas_call(
        body, name=name, grid=(rows // tr,), in_specs=[spec] * 4, out_specs=[spec] * 3,
        out_shape=[SDS(g.shape, F32)] * 3, compiler_params=_params(("parallel",)),
    )(g, w, m, v)


def _small_update(chip, tot, wmv):
    names = list(SMALL_PLACES)
    n = len(names)

    def body(chip_ref, tot_ref, quarter_ref, *refs):
        del chip_ref
        ins, outs = refs[:3 * n], refs[3 * n:]
        for i, nm in enumerate(names):
            sharded, row, (rows, cols) = SMALL_PLACES[nm]
            g = (quarter_ref if sharded else tot_ref)[row:row + rows, 0:cols]
            outs[4 * i][...] = g
            outs[4 * i + 1][...], outs[4 * i + 2][...], outs[4 * i + 3][...] = _adam_step(
                g, ins[3 * i][...], ins[3 * i + 1][...], ins[3 * i + 2][...])

    whole = lambda shape: pl.BlockSpec(shape, lambda i, c: (0,) * len(shape))
    shapes = [SMALL_PLACES[nm][2] for nm in names]
    outs = pl.pallas_call(
        body, name="small_update",
        grid_spec=pltpu.PrefetchScalarGridSpec(
            num_scalar_prefetch=1, grid=(1,),
            in_specs=[whole(tot.shape), pl.BlockSpec((tot.shape[0], D // 4), lambda i, c: (0, c[0]))]
            + [whole(shp) for shp in shapes for _ in range(3)],
            out_specs=[whole(shp) for shp in shapes for _ in range(4)]),
        out_shape=[SDS(shp, F32) for shp in shapes for _ in range(4)],
    )(chip, tot, tot, *[a for nm in names for a in wmv[nm]])
    return {nm: tuple(outs[4 * i:4 * i + 4]) for i, nm in enumerate(names)}


def _pad_rows(a, rows):
    return jnp.concatenate([a, jnp.zeros((rows - a.shape[0], a.shape[1]), a.dtype)], axis=0)


def _pad_cols(a, cols):
    return jnp.concatenate([a, jnp.zeros((a.shape[0], cols - a.shape[1]), a.dtype)], axis=1)


def kernel(x, a_pre_norm, a_w_in, a_conv_w, a_w_out, a_post_norm, kv_norm, w_kv, rel_bias, b_pre_norm, b_w_in, b_sinks, b_w_out, b_post_norm, loss_target, m_a_pre_norm, m_a_w_in, m_a_conv_w, m_a_w_out, m_a_post_norm, m_kv_norm, m_w_kv, m_rel_bias, m_b_pre_norm, m_b_w_in, m_b_sinks, m_b_w_out, m_b_post_norm, v_a_pre_norm, v_a_w_in, v_a_conv_w, v_a_w_out, v_a_post_norm, v_kv_norm, v_w_kv, v_rel_bias, v_b_pre_norm, v_b_w_in, v_b_sinks, v_b_w_out, v_b_post_norm):
    seq = x.shape[1]
    xs = x.reshape(seq, D)
    tgt = loss_target.reshape(seq, D)
    chip = 2 * lax.axis_index("x") + lax.axis_index("y")
    core = lax.axis_index("c")
    tm = _tile(seq, 512)
    tm_mix = _tile(seq, 512)
    tmw = _tile(seq, 1024)

    shards = [a_w_in[0], a_w_out[0], w_kv, b_w_in[0], b_w_out[0]]
    small_w = _pad_rows(jnp.concatenate([a_pre_norm, a_conv_w[0], a_post_norm], axis=0), 8)
    *own_only, small_g = _gather_weights(shards, small_w, 0)
    where = jnp.stack([core, chip]).astype(jnp.int32)
    small_full = small_g.transpose(1, 0, 2).reshape(8, D)
    g_apre, conv_w, g_apost = small_full[0:1], _pad_rows(small_full[1:4], 8), small_full[4:5]
    g_kv = kv_norm.reshape(1, D)

    proj, n1, (win_g, wouta_g, wkv_g, wbin_g, woutb_g) = _a_in(where[1:2], xs, g_apre, own_only, tmw)
    wouta = wouta_g.reshape(D, D)
    wkv = wkv_g.reshape(D, 2 * KV_W)
    woutb = woutb_g.reshape(D, D)
    ya, oa, h1 = _a_mix(proj, xs, conv_w, wouta, g_apost, tm_mix)
    kv, q, zb = _b_in(h1, g_kv, b_pre_norm, wkv, wbin_g, tmw)
    tab = _bias_table(rel_bias, b_sinks.reshape(N_HEADS))
    att, stats = _attn_fwd(q, kv, tab)
    dh2, dqz, datt, loss_acc, dg_bpost, dw_outb = _mid(att, zb, h1, tgt, woutb, b_post_norm, tm)

    dqz, dkv, dtab = _attn_bwd(q, kv, datt, stats, tab, dqz)
    dh1, doa, dg_b, dw_bin, dw_kv = _b_bwd(dqz, dkv, h1, dh2, oa, wbin_g, wkv, g_kv, b_pre_norm, g_apost, tm)
    dw_kv = dw_kv.reshape(N_CHIPS, D // 4, 2 * KV_W)
    dw_outb = dw_outb.reshape(N_CHIPS, D // 4, D)
    grads1 = [dw_kv, dw_bin, dw_outb]
    names1 = ["w_kv", "b_w_in", "b_w_out"]
    from_sibling1, _, _ = _sibling_exchange("to_sibling_1", to_sibling=grads1)
    sums1 = [_add_sibling(where, g, r, "add_sibling_" + nm) for g, r, nm in zip(grads1, from_sibling1, names1)]
    dproj, dconv_w, dw_outa, from_chips1 = _a_bwd(doa, ya, proj, conv_w, wouta, tm_mix, [t for t, _ in sums1])
    shards1 = [_add_chips(where, own, r, "add_chips_" + nm) for (_, own), r, nm in zip(sums1, from_chips1, names1)]
    dw_in = _dw(n1, dproj, D, _tile(seq, 2048), "dw_a_in")
    grads2 = [dw_in, dw_outa.reshape(N_CHIPS, D // 4, D)]
    names2 = ["a_w_in", "a_w_out"]
    from_sibling2, (g_wkv, g_wbin, g_woutb), _ = _sibling_exchange("to_sibling_2", to_sibling=grads2, shards=shards1)
    sums2 = [_add_sibling(where, g, r, "add_sibling_" + nm) for g, r, nm in zip(grads2, from_sibling2, names2)]
    nt = seq // tmw
    dn_first, from_chips2 = _a_in_bwd_matmul(dproj, win_g, tmw, max(nt - max(nt // 4, 1), 1), [t for t, _ in sums2])
    grad_x, dg_apre = _a_in_bwd(dn_first, dproj, xs, dh1, win_g, g_apre, tm)
    shards2 = [_add_chips(where, own, r, "add_chips_" + nm) for (_, own), r, nm in zip(sums2, from_chips2, names2)]
    drel, dsink = _bias_fold(dtab)

    smalls = jnp.concatenate([
        dg_apre[0:1], dg_b[2:3], dg_b[0:1], dg_b[1:2], dg_bpost[0:1], _pad_cols(dsink[0:1], D),
        _pad_cols(loss_acc[0:1], D), jnp.zeros((1, D), F32), dconv_w, _pad_cols(drel, D)], axis=0)
    _, (g_win, g_wouta), small_all = _sibling_exchange("share_last", shards=shards2, smalls=smalls)
    tot = _sum_smalls(small_all)

    big = {}
    for nm, g, w, m, v in [("a_w_in", g_win, a_w_in, m_a_w_in, v_a_w_in), ("a_w_out", g_wouta, a_w_out, m_a_w_out, v_a_w_out),
                           ("w_kv", g_wkv, w_kv, m_w_kv, v_w_kv), ("b_w_in", g_wbin, b_w_in, m_b_w_in, v_b_w_in),
                           ("b_w_out", g_woutb, b_w_out, m_b_w_out, v_b_w_out)]:
        shp = w.shape
        two = (shp[-2], shp[-1])
        d, nm_, nv_ = _adamw(g, w.reshape(two), m.reshape(two), v.reshape(two), "adamw_" + nm)
        big[nm] = (g.reshape(shp), d.reshape(shp), nm_.reshape(shp), nv_.reshape(shp))

    given = {"a_pre_norm": (a_pre_norm, m_a_pre_norm, v_a_pre_norm), "a_conv_w": (a_conv_w, m_a_conv_w, v_a_conv_w),
             "a_post_norm": (a_post_norm, m_a_post_norm, v_a_post_norm), "kv_norm": (kv_norm, m_kv_norm, v_kv_norm),
             "rel_bias": (rel_bias, m_rel_bias, v_rel_bias), "b_pre_norm": (b_pre_norm, m_b_pre_norm, v_b_pre_norm),
             "b_sinks": (b_sinks, m_b_sinks, v_b_sinks), "b_post_norm": (b_post_norm, m_b_post_norm, v_b_post_norm)}
    small = _small_update(where[1:2], tot, {nm: tuple(a.reshape(SMALL_PLACES[nm][2]) for a in wmv)
                                            for nm, wmv in given.items()})
    order = ["a_pre_norm", "a_w_in", "a_conv_w", "a_w_out", "a_post_norm", "kv_norm", "w_kv", "rel_bias",
             "b_pre_norm", "b_w_in", "b_sinks", "b_w_out", "b_post_norm"]
    outs = []
    for which in range(4):
        for nm in order:
            outs.append(big[nm][which] if nm in big else small[nm][which].reshape(given[nm][0].shape))
    loss = 0.5 * tot[LOSS_ROW, 0]
    return (loss, grad_x.reshape(x.shape), *outs)
```

```python
import math

import jax
import jax.numpy as jnp
from jax import lax
from jax.experimental import pallas as pl
from jax.experimental.pallas import tpu as pltpu

F32 = jnp.float32
BF16 = jnp.bfloat16
MESH = pl.DeviceIdType.MESH
SDS = jax.ShapeDtypeStruct

D = 1024
HEAD_DIM = 64
N_HEADS = 16
N_KV = 2
GROUP = 8
KV_W = 128
BLK = 128
N_BUCKETS = 32
MAX_EXACT = 16
MAX_DISTANCE = 128
EPS = 1e-6
NEG_INF = -1e30
Q_SCALE = HEAD_DIM ** -0.5

ADAM_LR = 0.001
ADAM_B1 = 0.9
ADAM_B2 = 0.999
ADAM_EPS = 1e-08
ADAM_WD = 0.01
ADAM_STEP = 10

N_CHIPS = 4
N_DEV = 8
BIN_COLS = 2 * D // N_CHIPS
VMEM_LIMIT = 56 * 1024 * 1024
SMALL_ROWS = 16
LOSS_ROW = 6
SMALL_PLACES = {
    "a_pre_norm": ("quarter", 0, (1, D // 4)), "a_conv_w": ("quarter", 8, (3, D // 4)),
    "a_post_norm": ("quarter", 1, (1, D // 4)), "kv_norm": ("rows", 2, (1, D)),
    "rel_bias": ("rel", 0, (N_BUCKETS, N_HEADS)), "b_pre_norm": ("rows", 3, (1, D)),
    "b_sinks": ("rows", 5, (1, N_HEADS)), "b_post_norm": ("rows", 4, (1, D)),
}
HALO = 16


def _bucket_thresholds():
    def bucket(d):
        big = MAX_EXACT + int(math.log(d / MAX_EXACT) / math.log(MAX_DISTANCE / MAX_EXACT)
                              * (N_BUCKETS - MAX_EXACT))
        return d if d < MAX_EXACT else min(big, N_BUCKETS - 1)
    out = []
    for b in range(MAX_EXACT + 1, N_BUCKETS):
        out.append(min(d for d in range(MAX_EXACT, MAX_DISTANCE) if bucket(d) >= b))
    return tuple(out)


BUCKET_THRESHOLDS = _bucket_thresholds()


def _params(semantics=None, vmem=VMEM_LIMIT):
    return pltpu.CompilerParams(dimension_semantics=semantics, vmem_limit_bytes=vmem)


def _tile(n, pref):
    return pref if n >= 2 * pref else max(n // 2, 8)


def _rms_scale(v):
    return lax.rsqrt(jnp.mean(v * v, axis=-1, keepdims=True) + EPS)


def _nt(a, b):
    return lax.dot_general(a, b, (((1,), (1,)), ((), ())), preferred_element_type=F32)


def _tn(a, b):
    return lax.dot_general(a, b, (((0,), (0,)), ((), ())), preferred_element_type=F32)


def _nn(a, b):
    return jnp.dot(a, b, preferred_element_type=F32)


def _silu_parts(z):
    sg = jax.nn.sigmoid(z)
    return sg, z * sg


def _dsilu(z, sg):
    return sg * (1.0 + z * (1.0 - sg))


def _acc_row(ref, row, val):
    ref[row:row + 1, :] += val


def _gather_copies(outs, splits, ici_send, ici_recv, d2d_send, d2d_recv):
    x, y, c = lax.axis_index("x"), lax.axis_index("y"), lax.axis_index("c")
    k = 2 * x + y
    sibling = (x, y, 1 - c)

    def part(o_ref, chip, core, split):
        if not split:
            return o_ref.at[chip]
        h = o_ref.shape[1] // 2
        return o_ref.at[chip, pl.ds(pl.multiple_of(core * h, 16), h)]

    def remote(ref, a, j, sems, to):
        return pltpu.make_async_remote_copy(src_ref=ref, dst_ref=ref, send_sem=sems[0].at[3 * a + j],
                                            recv_sem=sems[1].at[3 * a + j], device_id=to, device_id_type=MESH)

    copies = []
    for a, (o_ref, split) in enumerate(zip(outs, splits)):
        for j, (px, py) in enumerate([(x, 1 - y), (1 - x, y), (1 - x, 1 - y)]):
            kj = 2 * px + py
            ici, d2d = (ici_send, ici_recv), (d2d_send, d2d_recv)
            copies.append((remote(part(o_ref, k, c, split), a, j, ici, (px, py, c)),
                           remote(part(o_ref, kj, c, split), a, j, ici, (px, py, c)),
                           remote(part(o_ref, kj, c, split), a, j, d2d, sibling) if split else None,
                           remote(part(o_ref, kj, 1 - c, split), a, j, d2d, sibling) if split else None))
    return copies


def _gather_sems(n):
    return [pltpu.SemaphoreType.DMA((3 * n,)) for _ in range(4)]


def _prepare_weights(shards, small):
    n = len(shards)

    def body(*refs):
        ins, small_in = refs[:n], refs[n]
        outs, small_out = refs[n + 1:2 * n + 1], refs[2 * n + 1]
        stages, put_sem = refs[2 * n + 2:3 * n + 2], refs[3 * n + 2]
        sems = refs[3 * n + 3:]
        k = 2 * lax.axis_index("x") + lax.axis_index("y")
        puts = []
        for a, (i_ref, stage, o_ref) in enumerate(zip(ins, stages, outs)):
            stage[...] = i_ref[...].astype(BF16)
            puts.append(pltpu.make_async_copy(stage, o_ref.at[k], put_sem.at[a]))
            puts[-1].start()
        small_out[k] = small_in[...]
        copies = _gather_copies([small_out], [False], *sems)
        for send, _, _, _ in copies:
            send.start()
        for _, arrival, _, _ in copies:
            arrival.wait_recv()
        for send, _, _, _ in copies:
            send.wait_send()
        for put in puts:
            put.wait()

    vm = pl.BlockSpec(memory_space=pltpu.VMEM)
    anyspace = pl.BlockSpec(memory_space=pl.ANY)
    out_shape = [SDS((N_CHIPS,) + s.shape, BF16) for s in shards] + [SDS((N_CHIPS,) + small.shape, F32)]
    return pl.pallas_call(
        body, name="prepare_weights", out_shape=out_shape,
        in_specs=[vm] * (n + 1), out_specs=[anyspace] * n + [vm],
        scratch_shapes=[pltpu.VMEM(s.shape, BF16) for s in shards] + [pltpu.SemaphoreType.DMA((n,))] + _gather_sems(1),
        compiler_params=pltpu.CompilerParams(vmem_limit_bytes=VMEM_LIMIT),
    )(*shards, small)


def _a_in(chip, x, g_pre, weights, tm):
    s = x.shape[0]
    nt = s // tm
    n = len(weights)

    def body(chip_ref, x_ref, g_ref, *refs):
        proj_ref, n1_ref = refs[n:n + 2]
        gathered = refs[n + 2:2 * n + 2]
        wbuf, n1_all, fetch_sem = refs[2 * n + 2:2 * n + 5]
        sems = refs[2 * n + 5:]
        jj, i = pl.program_id(0), pl.program_id(1)
        copies = _gather_copies(gathered, [True] * n, *sems)

        def fetch(rel):
            slot = jnp.bitwise_xor(chip_ref[0], rel)
            return pltpu.make_async_copy(gathered[0].at[slot], wbuf.at[rel % 2], fetch_sem.at[rel % 2])

        @pl.when((jj == 0) & (i == 0))
        def _():
            fetch(0).start()
            copies[0][0].start()
            copies[1][0].start()
            fetch(0).wait()

        for rel in (1, 2, 3):
            @pl.when((jj == rel) & (i == 0))
            def _():
                fetch(rel).wait()

        @pl.when(jj == 0)
        def _():
            xv = x_ref[...]
            n1 = (xv * _rms_scale(xv) * g_ref[...]).astype(BF16)
            n1_ref[...] = n1
            n1_all[i] = n1
        proj_ref[...] = _nn(n1_all[i], wbuf[jj % 2]).astype(BF16)

        for rel in (1, 2, 3):
            @pl.when((jj == rel - 1) & (i == max(nt - 3, nt // 2)))
            def _():
                _, arrival, forward, forwarded = copies[rel - 1]
                arrival.wait_recv()
                forward.start()
                forwarded.wait_recv()
                fetch(rel).start()
                if rel == 1:
                    copies[2][0].start()
                if rel == 2:
                    for send, _, _, _ in copies[3:]:
                        send.start()

        @pl.when((jj == 3) & (i == max(nt - 2, 0)))
        def _():
            for _, arrival, forward, _ in copies[3:]:
                arrival.wait_recv()
                forward.start()

        @pl.when((jj == 3) & (i == nt - 1))
        def _():
            for _, _, _, forwarded in copies[3:]:
                forwarded.wait_recv()
            for send, _, forward, _ in copies:
                forward.wait_send()
                send.wait_send()

    anyspace = pl.BlockSpec(memory_space=pl.ANY)
    proj, n1, *gathered = pl.pallas_call(
        body, name="a_in",
        grid_spec=pltpu.PrefetchScalarGridSpec(
            num_scalar_prefetch=1, grid=(4, nt),
            in_specs=[pl.BlockSpec((tm, D), lambda jj, i, c: (jnp.where(jj == 0, i, nt - 1), 0)),
                      pl.BlockSpec((1, D), lambda jj, i, c: (0, 0))] + [anyspace] * n,
            out_specs=[pl.BlockSpec((tm, D), lambda jj, i, c: (i, jnp.bitwise_xor(c[0], jj))),
                       pl.BlockSpec((tm, D), lambda jj, i, c: (jnp.where(jj == 0, i, nt - 1), 0))] + [anyspace] * n,
            scratch_shapes=[pltpu.VMEM((2, D, D), BF16), pltpu.VMEM((nt, tm, D), BF16),
                            pltpu.SemaphoreType.DMA((2,))] + _gather_sems(n)),
        out_shape=[SDS((s, 4 * D), BF16), SDS((s, D), BF16)] + [SDS(w.shape, w.dtype) for w in weights],
        input_output_aliases={3 + a: 2 + a for a in range(n)},
        compiler_params=_params(("arbitrary", "arbitrary")),
    )(chip, x, g_pre, *weights)
    return proj, n1, gathered


def _shift_rows(v, last, second_last, rows):
    v1 = jnp.where(rows >= 1, pltpu.roll(v, 1, 0), last)
    v2 = jnp.where(rows >= 2, pltpu.roll(v, 2, 0), jnp.where(rows == 1, last, second_last))
    return v1, v2


def _a_mix(proj, x, conv_w, w_out, g_post, tm):
    s = x.shape[0]

    def body(proj_ref, x_ref, cw_ref, w_ref, g_ref, ya_ref, oa_ref, h1_ref, carry):
        @pl.when(pl.program_id(0) == 0)
        def _():
            carry[...] = jnp.zeros_like(carry)
        v = proj_ref[:, D:2 * D].astype(F32) * proj_ref[:, 2 * D:3 * D].astype(F32)
        rows = lax.broadcasted_iota(jnp.int32, (tm, D), 0)
        before = carry[...]
        v1, v2 = _shift_rows(v, before[7:8, :], before[6:7, :], rows)
        carry[...] = v[tm - 8:tm, :]
        conv = cw_ref[0:1, :] * v2 + cw_ref[1:2, :] * v1 + cw_ref[2:3, :] * v
        _, sz = _silu_parts(proj_ref[:, 3 * D:4 * D].astype(F32))
        ya = (proj_ref[:, 0:D].astype(F32) * conv * sz).astype(BF16)
        ya_ref[...] = ya
        oa = _nn(ya, w_ref[...])
        oa_ref[...] = oa.astype(BF16)
        h1_ref[...] = x_ref[...] + oa * _rms_scale(oa) * g_ref[...]

    row = lambda i: (i, 0)
    fix = lambda i: (0, 0)
    return pl.pallas_call(
        body, name="a_mix", grid=(s // tm,),
        in_specs=[pl.BlockSpec((tm, 4 * D), row), pl.BlockSpec((tm, D), row), pl.BlockSpec((8, D), fix),
                  pl.BlockSpec((D, D), fix), pl.BlockSpec((1, D), fix)],
        out_specs=[pl.BlockSpec((tm, D), row)] * 3,
        out_shape=[SDS((s, D), BF16), SDS((s, D), BF16), SDS((s, D), F32)],
        scratch_shapes=[pltpu.VMEM((8, D), F32)],
        compiler_params=_params(("arbitrary",)),
    )(proj, x, conv_w, w_out, g_post)


def _b_in(h1, g_kv, g_pre, w_kv, wbin_g, tm):
    s = h1.shape[0]

    def body(h_ref, gk_ref, gb_ref, wkv_ref, wb_ref, kv_ref, q_ref, z_ref):
        h = h_ref[...]
        hh = h * _rms_scale(h)
        nk = (hh * gk_ref[...]).astype(BF16)
        nb = (hh * gb_ref[...]).astype(BF16)
        kv_ref[...] = _nn(nk, wkv_ref[...]).astype(BF16)
        for j in range(2):
            q_ref[:, BIN_COLS * j:BIN_COLS * (j + 1)] = (_nn(nb, wb_ref[j]) * Q_SCALE).astype(BF16)
            z_ref[:, BIN_COLS * j:BIN_COLS * (j + 1)] = _nn(nb, wb_ref[2 + j]).astype(BF16)

    row = lambda i: (i, 0)
    fix = lambda i: (0, 0)
    return pl.pallas_call(
        body, name="b_in", grid=(s // tm,),
        in_specs=[pl.BlockSpec((tm, D), row), pl.BlockSpec((1, D), fix), pl.BlockSpec((1, D), fix),
                  pl.BlockSpec((D, 2 * KV_W), fix), pl.BlockSpec((N_CHIPS, D, BIN_COLS), lambda i: (0, 0, 0))],
        out_specs=[pl.BlockSpec((tm, 2 * KV_W), row), pl.BlockSpec((tm, D), row), pl.BlockSpec((tm, D), row)],
        out_shape=[SDS((s, 2 * KV_W), BF16), SDS((s, D), BF16), SDS((s, D), BF16)],
        compiler_params=_params(("parallel",)),
    )(h1, g_kv, g_pre, w_kv, wbin_g)


def _band_buckets():
    q = lax.broadcasted_iota(jnp.int32, (BLK, 2 * BLK), 0)
    k = lax.broadcasted_iota(jnp.int32, (BLK, 2 * BLK), 1)
    dist = q + BLK - k
    bucket = jnp.where(dist < MAX_EXACT, dist, MAX_EXACT)
    for t in BUCKET_THRESHOLDS:
        bucket = bucket + jnp.where(dist >= t, 1, 0)
    in_window = (dist >= 0) & (dist < BLK)
    return jnp.where(in_window, bucket, -1)


def _head_place(h):
    kh, j, e = h // GROUP, (h % GROUP) // 2, h % 2
    return kh, slice(BLK * j, BLK * (j + 1)), slice(2 * BLK * e, 2 * BLK * (e + 1))


def _bias_table(rel_bias, sinks):
    def body(rb_ref, sink_ref, tab_ref):
        bucket = _band_buckets()
        col = lax.broadcasted_iota(jnp.int32, (BLK, 2 * BLK), 1)
        for h in range(N_HEADS):
            acc = jnp.where(bucket < 0, NEG_INF, 0.0).astype(F32)
            for b in range(N_BUCKETS):
                acc = jnp.where(bucket == b, rb_ref[b, h], acc)
            acc = jnp.where(col == 0, sink_ref[h], acc)
            kh, rows, cols = _head_place(h)
            tab_ref[1, kh, rows, cols] = acc
            tab_ref[0, kh, rows, cols] = jnp.where((col > 0) & (col < BLK), NEG_INF, acc)

    return pl.pallas_call(
        body, name="bias_table", out_shape=SDS((2, N_KV, 4 * BLK, 4 * BLK), F32),
        in_specs=[pl.BlockSpec(memory_space=pltpu.SMEM), pl.BlockSpec(memory_space=pltpu.SMEM)],
        out_specs=pl.BlockSpec(memory_space=pltpu.VMEM),
    )(rel_bias, sinks)


def _bias_fold(dtab):
    def body(dtab_ref, out_ref, dsink_ref):
        bucket = _band_buckets()
        row = lax.broadcasted_iota(jnp.int32, (N_BUCKETS, 128), 0)
        lane = lax.broadcasted_iota(jnp.int32, (N_BUCKETS, 128), 1)
        row8 = lax.broadcasted_iota(jnp.int32, (8, 128), 0)
        lane8 = lax.broadcasted_iota(jnp.int32, (8, 128), 1)
        acc = jnp.zeros((N_BUCKETS, 128), F32)
        dsink = jnp.zeros((8, 128), F32)
        for h in range(N_HEADS):
            kh, rows, cols = _head_place(h)
            dt = dtab_ref[kh, rows, cols]
            for b in range(N_BUCKETS):
                val = jnp.sum(jnp.where(bucket == b, dt, 0.0))
                acc = acc + jnp.where((row == b) & (lane == h), val, 0.0)
            dsink = dsink + jnp.where((row8 == 0) & (lane8 == h), jnp.sum(dt[:, 0:1]), 0.0)
        out_ref[...] = acc
        dsink_ref[...] = dsink

    vm = pl.BlockSpec(memory_space=pltpu.VMEM)
    return pl.pallas_call(
        body, name="bias_fold", out_shape=[SDS((N_BUCKETS, 128), F32), SDS((8, 128), F32)],
        in_specs=[vm], out_specs=[vm, vm],
    )(dtab)


def _pair_operands(prev, cur):
    t = jnp.concatenate([prev, cur], axis=0).astype(F32)
    t = jnp.where(lax.broadcasted_iota(jnp.int32, t.shape, 0) == 0, 0.0, t)
    tr = pltpu.roll(t, HEAD_DIM, 1)
    lo = lax.broadcasted_iota(jnp.int32, t.shape, 1) < HEAD_DIM
    zero = jnp.zeros_like(t)
    head0 = jnp.concatenate([jnp.where(lo, t, zero), jnp.where(lo, zero, tr)], axis=0).astype(BF16)
    head1 = jnp.concatenate([jnp.where(lo, tr, zero), jnp.where(lo, zero, t)], axis=0).astype(BF16)
    return head0, head1


def _pair_fold(d0, d1):
    lo = lax.broadcasted_iota(jnp.int32, (2 * BLK, KV_W), 1) < HEAD_DIM
    zero = jnp.zeros((2 * BLK, KV_W), F32)
    g0 = jnp.where(lo, d0[0:256], zero) + pltpu.roll(jnp.where(lo, zero, d0[256:512]), HEAD_DIM, 1)
    g1 = pltpu.roll(jnp.where(lo, d1[0:256], zero), HEAD_DIM, 1) + jnp.where(lo, zero, d1[256:512])
    return jnp.where(lax.broadcasted_iota(jnp.int32, (2 * BLK, KV_W), 0) == 0, 0.0, g0 + g1)


def _stack_pairs(ref, kh, rows=slice(None)):
    return jnp.concatenate([ref[rows, 128 * (4 * kh + j):128 * (4 * kh + j + 1)] for j in range(4)], axis=0)


def _table_spec():
    return pl.BlockSpec((1, N_KV, 4 * BLK, 4 * BLK), lambda n: (jnp.minimum(n, 1), 0, 0, 0))


def _attn_fwd(q, kv, tab):
    s = q.shape[0]

    def body(q_ref, kp_ref, k0_ref, k1_ref, vp_ref, v0_ref, v1_ref, tab0_ref, tab1_ref, att_ref, stats_ref):
        lane = lax.broadcasted_iota(jnp.int32, (BLK, 128), 1)
        for sub, (kp, kc, vp, vc, tab_ref) in enumerate([(kp_ref, k0_ref, vp_ref, v0_ref, tab0_ref),
                                                         (k0_ref, k1_ref, v0_ref, v1_ref, tab1_ref)]):
            rows = slice(BLK * sub, BLK * (sub + 1))
            k2 = _pair_operands(kp[...], kc[...])
            v2 = _pair_operands(vp[...], vc[...])
            stats = jnp.zeros((BLK, 128), F32)
            for kh in range(N_KV):
                sc = _nt(_stack_pairs(q_ref, kh, rows), k2[kh])
                ps = []
                for e in range(2):
                    lg = sc[:, 256 * e:256 * (e + 1)] + tab_ref[0, kh, :, 256 * e:256 * (e + 1)]
                    m = jnp.max(lg, axis=-1, keepdims=True)
                    ex = jnp.exp(lg - m)
                    den = jnp.sum(ex, axis=-1, keepdims=True)
                    ps.append(ex * (1.0 / den))
                    lse = m + jnp.log(den)
                    for j in range(4):
                        stats = jnp.where(lane == GROUP * kh + 2 * j + e, lse[BLK * j:BLK * (j + 1)], stats)
                out = _nn(jnp.concatenate(ps, axis=1).astype(BF16), v2[kh])
                for j in range(4):
                    att_ref[rows, 128 * (4 * kh + j):128 * (4 * kh + j + 1)] = out[BLK * j:BLK * (j + 1)].astype(BF16)
            stats_ref[rows, :] = stats

    two = lambda m: (m, 0)
    table = lambda pick: pl.BlockSpec((1, N_KV, 4 * BLK, 4 * BLK), lambda m: (pick(m), 0, 0, 0))
    return pl.pallas_call(
        body, name="attn_fwd", grid=(s // (2 * BLK),),
        in_specs=[pl.BlockSpec((2 * BLK, D), two)]
        + [pl.BlockSpec((BLK, KV_W), lambda m, col=col, off=off: (jnp.maximum(2 * m + off, 0), col))
           for col in (0, 1) for off in (-1, 0, 1)]
        + [table(lambda m: jnp.minimum(m, 1)), table(lambda m: 1)],
        out_specs=[pl.BlockSpec((2 * BLK, D), two), pl.BlockSpec((2 * BLK, 128), two)],
        out_shape=[SDS((s, D), BF16), SDS((s, 128), F32)],
        compiler_params=_params(("parallel",)),
    )(q, kv, kv, kv, kv, kv, kv, tab, tab)


def _mid(att, zb, h1, tgt, w_out, g_post, tm):
    s = att.shape[0]
    nt = s // tm

    def body(att_ref, z_ref, h1_ref, t_ref, w_ref, g_ref,
             dh_ref, dqz_ref, datt_ref, loss_ref, dg_ref, dw_ref, dw_acc):
        @pl.when(pl.program_id(0) == 0)
        def _():
            loss_ref[...] = jnp.zeros_like(loss_ref)
            dg_ref[...] = jnp.zeros_like(dg_ref)
            dw_acc[...] = jnp.zeros_like(dw_acc)
        att = att_ref[...].astype(F32)
        z = z_ref[...].astype(F32)
        sg, sz = _silu_parts(z)
        ob = (att * sz).astype(BF16)
        y2 = _nn(ob, w_ref[...])
        r2 = _rms_scale(y2)
        yh = y2 * r2
        g = g_ref[...]
        err = (h1_ref[...] + yh * g) - t_ref[...]
        loss_ref[...] += jnp.sum(jnp.sum(err * err, axis=-1, keepdims=True) / D)
        dh = err / D
        dh_ref[...] = dh
        _acc_row(dg_ref, 0, jnp.sum(dh * yh, axis=0, keepdims=True))
        dyh = dh * g
        dy = (r2 * (dyh - yh * jnp.mean(dyh * yh, axis=-1, keepdims=True))).astype(BF16)
        dw_acc[...] += _tn(ob, dy)
        dob = _nt(dy, w_ref[...])
        datt_ref[...] = (dob * sz).astype(BF16)
        dqz_ref[...] = (dob * att * _dsilu(z, sg)).astype(BF16)

        @pl.when(pl.program_id(0) == nt - 1)
        def _():
            pltpu.sync_copy(dw_acc, dw_ref)

    row = lambda i: (i, 0)
    fix = lambda i: (0, 0)
    return pl.pallas_call(
        body, name="mid", grid=(nt,),
        in_specs=[pl.BlockSpec((tm, D), row)] * 4 + [pl.BlockSpec((D, D), fix), pl.BlockSpec((1, D), fix)],
        out_specs=[pl.BlockSpec((tm, D), row), pl.BlockSpec((tm, D), lambda i: (i, 1)), pl.BlockSpec((tm, D), row),
                   pl.BlockSpec((8, 128), fix), pl.BlockSpec((8, D), fix), pl.BlockSpec(memory_space=pl.ANY)],
        out_shape=[SDS((s, D), F32), SDS((s, 2 * D), BF16), SDS((s, D), BF16), SDS((8, 128), F32),
                   SDS((8, D), F32), SDS((D, D), F32)],
        scratch_shapes=[pltpu.VMEM((D, D), F32)],
        compiler_params=_params(("arbitrary",)),
    )(att, zb, h1, tgt, w_out, g_post)


def _attn_bwd(q, kv, datt, stats, tab, dqz):
    s = q.shape[0]
    nb = s // BLK

    def body(q_ref, kp_ref, kc_ref, vp_ref, vc_ref, da_ref, st_ref, tab_ref, dqz_in,
             dq_ref, dkv_ref, dtab_ref, dk_carry, dv_carry):
        del dqz_in
        n = pl.program_id(0)

        @pl.when(n == 0)
        def _():
            dtab_ref[...] = jnp.zeros_like(dtab_ref)
            dk_carry[...] = jnp.zeros_like(dk_carry)
            dv_carry[...] = jnp.zeros_like(dv_carry)

        @pl.when(n < nb)
        def _():
            k2 = _pair_operands(kp_ref[...], kc_ref[...])
            v2 = _pair_operands(vp_ref[...], vc_ref[...])
            lane = lax.broadcasted_iota(jnp.int32, (BLK, 128), 1)
            stats = st_ref[...]
            dk2, dv2 = [], []
            for kh in range(N_KV):
                qs = _stack_pairs(q_ref, kh)
                das = _stack_pairs(da_ref, kh)
                sc = _nt(qs, k2[kh])
                dp = _nt(das, v2[kh])
                ps, dss = [], []
                for e in range(2):
                    heads = [GROUP * kh + 2 * j + e for j in range(4)]
                    lse = jnp.concatenate([jnp.sum(jnp.where(lane == h, stats, 0.0), axis=-1, keepdims=True)
                                           for h in heads], axis=0)
                    cols = slice(256 * e, 256 * (e + 1))
                    p = jnp.exp(sc[:, cols] + tab_ref[0, kh, :, cols] - lse)
                    delta = jnp.sum(p * dp[:, cols], axis=-1, keepdims=True)
                    ds = p * (dp[:, cols] - delta)
                    dtab_ref[kh, :, cols] += ds
                    ps.append(p)
                    dss.append(ds)
                p2 = jnp.concatenate(ps, axis=1).astype(BF16)
                ds2 = jnp.concatenate(dss, axis=1).astype(BF16)
                dq = _nn(ds2, k2[kh]) * Q_SCALE
                for j in range(4):
                    dq_ref[:, 128 * (4 * kh + j):128 * (4 * kh + j + 1)] = dq[BLK * j:BLK * (j + 1)].astype(BF16)
                dk2.append(_tn(ds2, qs))
                dv2.append(_tn(p2, das))
            dkk = _pair_fold(dk2[0], dk2[1])
            dvv = _pair_fold(dv2[0], dv2[1])
            dkv_ref[:, 0:KV_W] = (dk_carry[...] + dkk[0:BLK]).astype(BF16)
            dkv_ref[:, KV_W:2 * KV_W] = (dv_carry[...] + dvv[0:BLK]).astype(BF16)
            dk_carry[...] = dkk[BLK:2 * BLK]
            dv_carry[...] = dvv[BLK:2 * BLK]

        @pl.when(n == nb)
        def _():
            dkv_ref[:, 0:KV_W] = dk_carry[...].astype(BF16)
            dkv_ref[:, KV_W:2 * KV_W] = dv_carry[...].astype(BF16)

    cur = lambda n: (jnp.minimum(n, nb - 1), 0)
    prev = lambda n: (jnp.clip(n - 1, 0, nb - 1), 0)
    return pl.pallas_call(
        body, name="attn_bwd", grid=(nb + 1,),
        in_specs=[pl.BlockSpec((BLK, D), cur),
                  pl.BlockSpec((BLK, KV_W), prev), pl.BlockSpec((BLK, KV_W), cur),
                  pl.BlockSpec((BLK, KV_W), lambda n: (jnp.clip(n - 1, 0, nb - 1), 1)),
                  pl.BlockSpec((BLK, KV_W), lambda n: (jnp.minimum(n, nb - 1), 1)),
                  pl.BlockSpec((BLK, D), cur), pl.BlockSpec((BLK, 128), cur), _table_spec(),
                  pl.BlockSpec(memory_space=pl.ANY)],
        out_specs=[pl.BlockSpec((BLK, D), cur), pl.BlockSpec((BLK, 2 * KV_W), prev),
                   pl.BlockSpec((N_KV, 4 * BLK, 4 * BLK), lambda n: (0, 0, 0))],
        out_shape=[SDS((s, 2 * D), BF16), SDS((s, 2 * KV_W), BF16), SDS((N_KV, 4 * BLK, 4 * BLK), F32)],
        scratch_shapes=[pltpu.VMEM((BLK, KV_W), F32), pltpu.VMEM((BLK, KV_W), F32)],
        input_output_aliases={8: 0},
        compiler_params=_params(("arbitrary",)),
    )(q, kv, kv, kv, kv, datt, stats, tab, dqz)


def _b_bwd(dqz, dkv, h1, dh2, oa, wbin_g, w_kv, g_kv, g_pre, g_apost, tm):
    s = h1.shape[0]
    nt = s // tm

    def body(dqz_ref, dkv_ref, h_ref, dh2_ref, oa_ref, wb_ref, wkv_ref, gk_ref, gb_ref, ga_ref,
             dh1_ref, doa_ref, dg_ref, dwb_ref, dwkv_ref, dwb_acc, dwkv_acc):
        @pl.when(pl.program_id(0) == 0)
        def _():
            dg_ref[...] = jnp.zeros_like(dg_ref)
            dwb_acc[...] = jnp.zeros_like(dwb_acc)
            dwkv_acc[...] = jnp.zeros_like(dwkv_acc)
        dnb = _nt(dqz_ref[:, 0:BIN_COLS], wb_ref[0])
        for j in range(1, 4):
            dnb = dnb + _nt(dqz_ref[:, BIN_COLS * j:BIN_COLS * (j + 1)], wb_ref[j])
        dnk = _nt(dkv_ref[...], wkv_ref[...])
        h = h_ref[...]
        r = _rms_scale(h)
        hh = h * r
        nb = (hh * gb_ref[...]).astype(BF16)
        for j in range(4):
            dwb_acc[j] += _tn(nb, dqz_ref[:, BIN_COLS * j:BIN_COLS * (j + 1)])
        dwkv_acc[...] += _tn((hh * gk_ref[...]).astype(BF16), dkv_ref[...])
        _acc_row(dg_ref, 0, jnp.sum(dnk * hh, axis=0, keepdims=True))
        _acc_row(dg_ref, 1, jnp.sum(dnb * hh, axis=0, keepdims=True))
        dhh = dnb * gb_ref[...] + dnk * gk_ref[...]
        dh1 = dh2_ref[...] + r * (dhh - hh * jnp.mean(dhh * hh, axis=-1, keepdims=True))
        dh1_ref[...] = dh1
        oa = oa_ref[...].astype(F32)
        ra = _rms_scale(oa)
        oh = oa * ra
        _acc_row(dg_ref, 2, jnp.sum(dh1 * oh, axis=0, keepdims=True))
        doh = dh1 * ga_ref[...]
        doa_ref[...] = (ra * (doh - oh * jnp.mean(doh * oh, axis=-1, keepdims=True))).astype(BF16)

        @pl.when(pl.program_id(0) == nt - 1)
        def _():
            pltpu.sync_copy(dwb_acc, dwb_ref)
            pltpu.sync_copy(dwkv_acc, dwkv_ref)

    row = lambda i: (i, 0)
    fix = lambda i: (0, 0)
    anyspace = pl.BlockSpec(memory_space=pl.ANY)
    return pl.pallas_call(
        body, name="b_bwd", grid=(nt,),
        in_specs=[pl.BlockSpec((tm, 2 * D), row), pl.BlockSpec((tm, 2 * KV_W), row), pl.BlockSpec((tm, D), row),
                  pl.BlockSpec((tm, D), row), pl.BlockSpec((tm, D), row),
                  pl.BlockSpec((N_CHIPS, D, BIN_COLS), lambda i: (0, 0, 0)), pl.BlockSpec((D, 2 * KV_W), fix),
                  pl.BlockSpec((1, D), fix), pl.BlockSpec((1, D), fix), pl.BlockSpec((1, D), fix)],
        out_specs=[pl.BlockSpec((tm, D), row), pl.BlockSpec((tm, D), row), pl.BlockSpec((8, D), fix), anyspace, anyspace],
        out_shape=[SDS((s, D), F32), SDS((s, D), BF16), SDS((8, D), F32), SDS((N_CHIPS, D, BIN_COLS), F32),
                   SDS((D, 2 * KV_W), F32)],
        scratch_shapes=[pltpu.VMEM((N_CHIPS, D, BIN_COLS), F32), pltpu.VMEM((D, 2 * KV_W), F32)],
        compiler_params=_params(("arbitrary",)),
    )(dqz, dkv, h1, dh2, oa, wbin_g, w_kv, g_kv, g_pre, g_apost)


def _chip_exchange(parts, recvs, send, recv):
    x, y, c = lax.axis_index("x"), lax.axis_index("y"), lax.axis_index("c")
    chips = [(x, 1 - y), (1 - x, y), (1 - x, 1 - y)]
    copies = []
    for a, (t, r) in enumerate(zip(parts, recvs)):
        for j, (px, py) in enumerate(chips):
            copies.append(pltpu.make_async_remote_copy(
                src_ref=t.at[2 * px + py], dst_ref=r.at[j], send_sem=send.at[3 * a + j],
                recv_sem=recv.at[3 * a + j], device_id=(px, py, c), device_id_type=MESH))
    return copies


def _exchange_specs(parts):
    anyspace = pl.BlockSpec(memory_space=pl.ANY)
    n = len(parts)
    return ([anyspace] * n, [anyspace] * n, [SDS((3,) + t.shape[1:], t.dtype) for t in parts],
            [pltpu.SemaphoreType.DMA((3 * n,)), pltpu.SemaphoreType.DMA((3 * n,))])


def _a_bwd(doa, ya, proj, conv_w, w_out, tm, parts):
    s = doa.shape[0]
    nt = s // tm
    n = len(parts)
    ex_in, ex_out, ex_shape, ex_sems = _exchange_specs(parts)

    def body(*refs):
        doa_ref, ya_ref, proj_ref, halo_ref, cw_ref, w_ref = refs[:6]
        part_refs = refs[6:6 + n]
        dproj_ref, dcw_ref, dw_ref = refs[6 + n:9 + n]
        recv_refs = refs[9 + n:9 + 2 * n]
        carry, dw_acc, send, recv = refs[9 + 2 * n:]
        i = pl.program_id(0)
        r = nt - 1 - i

        @pl.when(i == 0)
        def _():
            dcw_ref[...] = jnp.zeros_like(dcw_ref)
            carry[...] = jnp.zeros_like(carry)
            dw_acc[...] = jnp.zeros_like(dw_acc)
            for cp in _chip_exchange(part_refs, recv_refs, send, recv):
                cp.start()
        dya = _nt(doa_ref[...], w_ref[...])
        dw_acc[...] += _tn(ya_ref[...], doa_ref[...])
        bg = proj_ref[:, 0:D].astype(F32)
        cg = proj_ref[:, D:2 * D].astype(F32)
        u = proj_ref[:, 2 * D:3 * D].astype(F32)
        z = proj_ref[:, 3 * D:4 * D].astype(F32)
        v = cg * u
        before = jnp.where(r > 0, halo_ref[:, D:2 * D].astype(F32) * halo_ref[:, 2 * D:3 * D].astype(F32), 0.0)
        rows = lax.broadcasted_iota(jnp.int32, (tm, D), 0)
        v1, v2 = _shift_rows(v, before[HALO - 1:HALO, :], before[HALO - 2:HALO - 1, :], rows)
        conv = cw_ref[0:1, :] * v2 + cw_ref[1:2, :] * v1 + cw_ref[2:3, :] * v
        sg, sz = _silu_parts(z)
        dproj_ref[:, 0:D] = (dya * conv * sz).astype(BF16)
        dproj_ref[:, 3 * D:4 * D] = (dya * bg * conv * _dsilu(z, sg)).astype(BF16)
        dconv = dya * bg * sz
        _acc_row(dcw_ref, 0, jnp.sum(dconv * v2, axis=0, keepdims=True))
        _acc_row(dcw_ref, 1, jnp.sum(dconv * v1, axis=0, keepdims=True))
        _acc_row(dcw_ref, 2, jnp.sum(dconv * v, axis=0, keepdims=True))
        after = carry[...]
        up1 = jnp.where(rows < tm - 1, pltpu.roll(dconv, tm - 1, 0), after[0:1, :])
        up2 = jnp.where(rows < tm - 2, pltpu.roll(dconv, tm - 2, 0),
                        jnp.where(rows == tm - 2, after[0:1, :], after[1:2, :]))
        carry[...] = dconv[0:8, :]
        dv = cw_ref[2:3, :] * dconv + cw_ref[1:2, :] * up1 + cw_ref[0:1, :] * up2
        dproj_ref[:, D:2 * D] = (dv * u).astype(BF16)
        dproj_ref[:, 2 * D:3 * D] = (dv * cg).astype(BF16)

        @pl.when(i == nt - 1)
        def _():
            pltpu.sync_copy(dw_acc, dw_ref)
            for cp in _chip_exchange(part_refs, recv_refs, send, recv):
                cp.wait()

    rev = lambda i: (nt - 1 - i, 0)
    fix = lambda i: (0, 0)
    halo = lambda i: (jnp.maximum((nt - 1 - i) * (tm // HALO) - 1, 0), 0)
    dproj, dcw, dw, *got = pl.pallas_call(
        body, name="a_bwd", grid=(nt,),
        in_specs=[pl.BlockSpec((tm, D), rev), pl.BlockSpec((tm, D), rev), pl.BlockSpec((tm, 4 * D), rev),
                  pl.BlockSpec((HALO, 4 * D), halo), pl.BlockSpec((8, D), fix), pl.BlockSpec((D, D), fix)] + ex_in,
        out_specs=[pl.BlockSpec((tm, 4 * D), rev), pl.BlockSpec((8, D), fix), pl.BlockSpec(memory_space=pl.ANY)] + ex_out,
        out_shape=[SDS((s, 4 * D), BF16), SDS((8, D), F32), SDS((D, D), F32)] + ex_shape,
        scratch_shapes=[pltpu.VMEM((8, D), F32), pltpu.VMEM((D, D), F32)] + ex_sems,
        compiler_params=_params(("arbitrary",)),
    )(doa, ya, proj, proj, conv_w, w_out, *parts)
    return dproj, dcw, dw, got


def _dn1(dp_ref, w_ref):
    dn = _nt(dp_ref[:, 0:D], w_ref[0])
    for j in range(1, 4):
        dn = dn + _nt(dp_ref[:, D * j:D * (j + 1)], w_ref[j])
    return dn


def _a_in_bwd_matmul(dproj, win_g, tm, count, parts):
    n = len(parts)
    ex_in, ex_out, ex_shape, ex_sems = _exchange_specs(parts)

    def body(*refs):
        dp_ref, w_ref = refs[:2]
        part_refs = refs[2:2 + n]
        dn_ref = refs[2 + n]
        recv_refs = refs[3 + n:3 + 2 * n]
        sems = refs[3 + 2 * n:]

        @pl.when(pl.program_id(0) == 0)
        def _():
            for cp in _chip_exchange(part_refs, recv_refs, *sems):
                cp.start()
        dn_ref[...] = _dn1(dp_ref, w_ref).astype(BF16)

        @pl.when(pl.program_id(0) == count - 1)
        def _():
            for cp in _chip_exchange(part_refs, recv_refs, *sems):
                cp.wait()

    row = lambda i: (i, 0)
    dn, *got = pl.pallas_call(
        body, name="a_in_bwd_matmul", grid=(count,),
        in_specs=[pl.BlockSpec((tm, 4 * D), row), pl.BlockSpec((4, D, D), lambda i: (0, 0, 0))] + ex_in,
        out_specs=[pl.BlockSpec((tm, D), row)] + ex_out,
        out_shape=[SDS((count * tm, D), BF16)] + ex_shape,
        scratch_shapes=ex_sems,
        compiler_params=_params(("arbitrary",)),
    )(dproj, win_g, *parts)
    return dn, got


def _a_in_bwd(dn_first, dproj, x, dh1, win_g, g_pre, tm):
    s = x.shape[0]
    nt = s // tm
    count = dn_first.shape[0] // tm

    def body(dn_ref, dp_ref, x_ref, dh_ref, w_ref, g_ref, gx_ref, dg_ref, dn_s):
        i = pl.program_id(0)

        @pl.when(i == 0)
        def _():
            dg_ref[...] = jnp.zeros_like(dg_ref)

        @pl.when(i < count)
        def _():
            dn_s[...] = dn_ref[...].astype(F32)

        @pl.when(i >= count)
        def _():
            dn_s[...] = _dn1(dp_ref, w_ref)
        dn = dn_s[...]
        xv = x_ref[...]
        r = _rms_scale(xv)
        xh = xv * r
        _acc_row(dg_ref, 0, jnp.sum(dn * xh, axis=0, keepdims=True))
        dxh = dn * g_ref[...]
        gx_ref[...] = dh_ref[...] + r * (dxh - xh * jnp.mean(dxh * xh, axis=-1, keepdims=True))

    row = lambda i: (i, 0)
    fix = lambda i: (0, 0)
    return pl.pallas_call(
        body, name="a_in_bwd", grid=(nt,),
        in_specs=[pl.BlockSpec((tm, D), lambda i: (jnp.minimum(i, count - 1), 0)),
                  pl.BlockSpec((tm, 4 * D), lambda i: (jnp.maximum(i, count), 0)),
                  pl.BlockSpec((tm, D), row), pl.BlockSpec((tm, D), row),
                  pl.BlockSpec((4, D, D), lambda i: (0, 0, 0)), pl.BlockSpec((1, D), fix)],
        out_specs=[pl.BlockSpec((tm, D), row), pl.BlockSpec((8, D), fix)],
        out_shape=[SDS((s, D), F32), SDS((8, D), F32)],
        scratch_shapes=[pltpu.VMEM((tm, D), F32)],
        compiler_params=_params(("arbitrary",)),
    )(dn_first, dproj, x, dh1, win_g, g_pre)


def _dw(a, b, tn, tmw, name):
    s, k = a.shape
    n = b.shape[1]

    def body(a_ref, b_ref, o_ref):
        @pl.when(pl.program_id(1) == 0)
        def _():
            o_ref[...] = jnp.zeros_like(o_ref)
        o_ref[0] += _tn(a_ref[...], b_ref[...])

    return pl.pallas_call(
        body, name=name, grid=(n // tn, s // tmw),
        in_specs=[pl.BlockSpec((tmw, k), lambda j, t: (t, 0)), pl.BlockSpec((tmw, tn), lambda j, t: (t, j))],
        out_specs=pl.BlockSpec((1, k, tn), lambda j, t: (j, 0, 0)),
        out_shape=SDS((n // tn, k, tn), F32),
        compiler_params=_params(("parallel", "arbitrary")),
    )(a, b)


def _sibling_exchange(name, to_sibling=(), shards=(), smalls=()):
    n_g, n_h, n_s = len(to_sibling), len(shards), len(smalls)

    def body(*refs):
        gs = refs[:n_g]
        pos = n_g + n_h
        small_ins = refs[pos:pos + n_s]
        pos += n_s
        rs, fs = refs[pos:pos + n_g], refs[pos + n_g:pos + n_g + n_h]
        pos += n_g + n_h
        small_alls = refs[pos:pos + n_s]
        pos += n_s
        dsend, drecv, ssend, srecv = refs[pos:]
        x, y, c = lax.axis_index("x"), lax.axis_index("y"), lax.axis_index("c")
        sibling = (x, y, 1 - c)
        sends, arrivals = [], []
        for a, (g, r) in enumerate(zip(gs, rs)):
            h = g.shape[1] // 2
            src = g.at[:, pl.ds(pl.multiple_of((1 - c) * h, 8), h), :]
            sends.append(pltpu.make_async_remote_copy(src_ref=src, dst_ref=r, send_sem=dsend.at[a], recv_sem=drecv.at[a],
                                                      device_id=sibling, device_id_type=MESH))
            arrivals.append(pltpu.make_async_remote_copy(src_ref=r, dst_ref=r, send_sem=dsend.at[a], recv_sem=drecv.at[a],
                                                         device_id=sibling, device_id_type=MESH))
        for b, full in enumerate(fs):
            h = full.shape[0] // 2
            mine = full.at[pl.ds(pl.multiple_of(c * h, 8), h)]
            theirs = full.at[pl.ds(pl.multiple_of((1 - c) * h, 8), h)]
            sends.append(pltpu.make_async_remote_copy(src_ref=mine, dst_ref=mine, send_sem=dsend.at[n_g + b],
                                                      recv_sem=drecv.at[n_g + b], device_id=sibling, device_id_type=MESH))
            arrivals.append(pltpu.make_async_remote_copy(src_ref=mine, dst_ref=theirs, send_sem=dsend.at[n_g + b],
                                                         recv_sem=drecv.at[n_g + b], device_id=sibling, device_id_type=MESH))
        me = 4 * x + 2 * y + c
        for k, (small_in, small_all) in enumerate(zip(small_ins, small_alls)):
            small_all[me] = small_in[...]
            for rel in range(1, N_DEV):
                fx, fy, fc = rel >> 2, (rel >> 1) & 1, rel & 1
                peer = (x + fx - 2 * x * fx, y + fy - 2 * y * fy, c + fc - 2 * c * fc)
                sender = 4 * peer[0] + 2 * peer[1] + peer[2]
                sem = (N_DEV - 1) * k + rel - 1
                sends.append(pltpu.make_async_remote_copy(
                    src_ref=small_in, dst_ref=small_all.at[me], send_sem=ssend.at[sem], recv_sem=srecv.at[sem],
                    device_id=peer, device_id_type=MESH))
                arrivals.append(pltpu.make_async_remote_copy(
                    src_ref=small_in, dst_ref=small_all.at[sender], send_sem=ssend.at[sem], recv_sem=srecv.at[sem],
                    device_id=peer, device_id_type=MESH))
        for cp in sends:
            cp.start()
        for cp in arrivals:
            cp.wait_recv()
        for cp in sends:
            cp.wait_send()

    anyspace = pl.BlockSpec(memory_space=pl.ANY)
    vm = pl.BlockSpec(memory_space=pltpu.VMEM)
    out_shape = [SDS((N_CHIPS, g.shape[1] // 2, g.shape[2]), F32) for g in to_sibling]
    out_shape += [SDS(full.shape, F32) for full in shards]
    out_shape += [SDS((N_DEV,) + sm.shape, F32) for sm in smalls]
    n_d2d = max(n_g + n_h, 1)
    n_all = (N_DEV - 1) * max(n_s, 1)
    outs = pl.pallas_call(
        body, name=name, out_shape=out_shape,
        in_specs=[anyspace] * (n_g + n_h) + [vm] * n_s, out_specs=[anyspace] * (n_g + n_h) + [vm] * n_s,
        scratch_shapes=[pltpu.SemaphoreType.DMA((n_d2d,)), pltpu.SemaphoreType.DMA((n_d2d,)),
                        pltpu.SemaphoreType.DMA((n_all,)), pltpu.SemaphoreType.DMA((n_all,))],
        input_output_aliases={n_g + b: n_g + b for b in range(n_h)},
    )(*to_sibling, *shards, *smalls)
    return outs[:n_g], outs[n_g:n_g + n_h], outs[n_g + n_h:]


def _add_sibling(where, g, r, name):
    _, rows, cols = g.shape
    h = rows // 2
    tr = min(h, 256)
    nh = h // tr

    def body(where_ref, g_ref, r_ref, t_ref, own_ref):
        t = g_ref[0] + r_ref[0]
        t_ref[0] = t.astype(BF16)

        @pl.when(pl.program_id(1) == where_ref[1])
        def _():
            own_ref[...] = t

    return pl.pallas_call(
        body, name=name,
        grid_spec=pltpu.PrefetchScalarGridSpec(
            num_scalar_prefetch=1, grid=(nh, N_CHIPS),
            in_specs=[pl.BlockSpec((1, tr, cols), lambda i, k, w: (k, w[0] * nh + i, 0)),
                      pl.BlockSpec((1, tr, cols), lambda i, k, w: (k, i, 0))],
            out_specs=[pl.BlockSpec((1, tr, cols), lambda i, k, w: (k, i, 0)),
                       pl.BlockSpec((tr, cols), lambda i, k, w: (i, 0))]),
        out_shape=[SDS((N_CHIPS, h, cols), BF16), SDS((h, cols), F32)],
        compiler_params=_params(("parallel", "arbitrary")),
    )(where, g, r)


def _add_chips(where, own, r, name):
    h, cols = own.shape
    tr = min(h, 256)
    nh = h // tr

    def body(where_ref, t_ref, r_ref, o_ref):
        del where_ref
        o_ref[...] = ((t_ref[...] + r_ref[0].astype(F32)) + r_ref[1].astype(F32)) + r_ref[2].astype(F32)

    return pl.pallas_call(
        body, name=name,
        grid_spec=pltpu.PrefetchScalarGridSpec(
            num_scalar_prefetch=1, grid=(nh,),
            in_specs=[pl.BlockSpec((tr, cols), lambda i, w: (i, 0)), pl.BlockSpec((3, tr, cols), lambda i, w: (0, i, 0))],
            out_specs=pl.BlockSpec((tr, cols), lambda i, w: (w[0] * nh + i, 0))),
        out_shape=SDS((2 * h, cols), F32),
        compiler_params=_params(("parallel",)),
    )(where, own, r)


def _sum_smalls(gathered):
    n = len(gathered)

    def body(*refs):
        for all_ref, o_ref in zip(refs[:n], refs[n:]):
            acc = all_ref[0]
            for dev in range(1, N_DEV):
                acc = acc + all_ref[dev]
            o_ref[...] = acc

    vm = pl.BlockSpec(memory_space=pltpu.VMEM)
    return pl.pallas_call(
        body, name="sum_smalls", out_shape=[SDS(a.shape[1:], F32) for a in gathered],
        in_specs=[vm] * n, out_specs=[vm] * n,
    )(*gathered)


def _adam_step(g, w, m, v):
    nm = ADAM_B1 * m + (1.0 - ADAM_B1) * g
    nv = ADAM_B2 * v + (1.0 - ADAM_B2) * (g * g)
    m_hat = nm / (1.0 - ADAM_B1 ** ADAM_STEP)
    v_hat = nv / (1.0 - ADAM_B2 ** ADAM_STEP)
    return -ADAM_LR * (m_hat / (jnp.sqrt(v_hat) + ADAM_EPS) + ADAM_WD * w), nm, nv


def _adamw(g, w, m, v, name):
    rows, cols = g.shape
    tr = min(rows, 256)

    def body(g_ref, w_ref, m_ref, v_ref, d_ref, nm_ref, nv_ref):
        d_ref[...], nm_ref[...], nv_ref[...] = _adam_step(g_ref[...], w_ref[...], m_ref[...], v_ref[...])

    spec = pl.BlockSpec((tr, cols), lambda i: (i, 0))
    return pl.pallas_call(
        body, name=name, grid=(rows // tr,), in_specs=[spec] * 4, out_specs=[spec] * 3,
        out_shape=[SDS(g.shape, F32)] * 3, compiler_params=_params(("parallel",)),
    )(g, w, m, v)


def _small_update(chip, tot, tot_rel, wmv):
    names = list(SMALL_PLACES)
    n = len(names)

    def body(chip_ref, tot_ref, quarter_ref, rel_ref, *refs):
        del chip_ref
        ins, outs = refs[:3 * n], refs[3 * n:]
        for i, nm in enumerate(names):
            source, row, (rows, cols) = SMALL_PLACES[nm]
            g = {"rows": tot_ref, "quarter": quarter_ref, "rel": rel_ref}[source][row:row + rows, 0:cols]
            outs[4 * i][...] = g
            outs[4 * i + 1][...], outs[4 * i + 2][...], outs[4 * i + 3][...] = _adam_step(
                g, ins[3 * i][...], ins[3 * i + 1][...], ins[3 * i + 2][...])

    whole = lambda shape: pl.BlockSpec(shape, lambda i, c: (0,) * len(shape))
    shapes = [SMALL_PLACES[nm][2] for nm in names]
    outs = pl.pallas_call(
        body, name="small_update",
        grid_spec=pltpu.PrefetchScalarGridSpec(
            num_scalar_prefetch=1, grid=(1,),
            in_specs=[whole(tot.shape), pl.BlockSpec((tot.shape[0], D // 4), lambda i, c: (0, c[0])),
                      whole(tot_rel.shape)] + [whole(shp) for shp in shapes for _ in range(3)],
            out_specs=[whole(shp) for shp in shapes for _ in range(4)]),
        out_shape=[SDS(shp, F32) for shp in shapes for _ in range(4)],
    )(chip, tot, tot, tot_rel, *[a for nm in names for a in wmv[nm]])
    return {nm: tuple(outs[4 * i:4 * i + 4]) for i, nm in enumerate(names)}


def _pad_rows(a, rows):
    return jnp.concatenate([a, jnp.zeros((rows - a.shape[0], a.shape[1]), a.dtype)], axis=0)


def _pad_cols(a, cols):
    return jnp.concatenate([a, jnp.zeros((a.shape[0], cols - a.shape[1]), a.dtype)], axis=1)


def kernel(x, a_pre_norm, a_w_in, a_conv_w, a_w_out, a_post_norm, kv_norm, w_kv, rel_bias, b_pre_norm, b_w_in, b_sinks, b_w_out, b_post_norm, loss_target, m_a_pre_norm, m_a_w_in, m_a_conv_w, m_a_w_out, m_a_post_norm, m_kv_norm, m_w_kv, m_rel_bias, m_b_pre_norm, m_b_w_in, m_b_sinks, m_b_w_out, m_b_post_norm, v_a_pre_norm, v_a_w_in, v_a_conv_w, v_a_w_out, v_a_post_norm, v_kv_norm, v_w_kv, v_rel_bias, v_b_pre_norm, v_b_w_in, v_b_sinks, v_b_w_out, v_b_post_norm):
    seq = x.shape[1]
    xs = x.reshape(seq, D)
    tgt = loss_target.reshape(seq, D)
    chip = 2 * lax.axis_index("x") + lax.axis_index("y")
    core = lax.axis_index("c")
    tm = _tile(seq, 512)
    tmw = _tile(seq, 1024)

    shards = [a_w_in[0], a_w_out[0], w_kv, b_w_in[0], b_w_out[0]]
    small_w = _pad_rows(jnp.concatenate([a_pre_norm, a_conv_w[0], a_post_norm], axis=0), 8)
    *own_only, small_g = _prepare_weights(shards, small_w)
    where = jnp.stack([core, chip]).astype(jnp.int32)
    small_full = small_g.transpose(1, 0, 2).reshape(8, D)
    g_apre, conv_w, g_apost = small_full[0:1], _pad_rows(small_full[1:4], 8), small_full[4:5]
    g_kv = kv_norm.reshape(1, D)

    proj, n1, (win_g, wouta_g, wkv_g, wbin_g, woutb_g) = _a_in(where[1:2], xs, g_apre, own_only, tmw)
    wouta = wouta_g.reshape(D, D)
    wkv = wkv_g.reshape(D, 2 * KV_W)
    woutb = woutb_g.reshape(D, D)
    ya, oa, h1 = _a_mix(proj, xs, conv_w, wouta, g_apost, tm)
    kv, q, zb = _b_in(h1, g_kv, b_pre_norm, wkv, wbin_g, tmw)
    tab = _bias_table(rel_bias, b_sinks.reshape(N_HEADS))
    att, stats = _attn_fwd(q, kv, tab)
    dh2, dqz, datt, loss_acc, dg_bpost, dw_outb = _mid(att, zb, h1, tgt, woutb, b_post_norm, tm)

    dqz, dkv, dtab = _attn_bwd(q, kv, datt, stats, tab, dqz)
    dh1, doa, dg_b, dw_bin, dw_kv = _b_bwd(dqz, dkv, h1, dh2, oa, wbin_g, wkv, g_kv, b_pre_norm, g_apost, tm)
    dw_kv = dw_kv.reshape(N_CHIPS, D // 4, 2 * KV_W)
    dw_outb = dw_outb.reshape(N_CHIPS, D // 4, D)
    grads1 = [dw_kv, dw_bin, dw_outb]
    names1 = ["w_kv", "b_w_in", "b_w_out"]
    from_sibling1, _, _ = _sibling_exchange("to_sibling_1", to_sibling=grads1)
    sums1 = [_add_sibling(where, g, r, "add_sibling_" + nm) for g, r, nm in zip(grads1, from_sibling1, names1)]
    dproj, dconv_w, dw_outa, from_chips1 = _a_bwd(doa, ya, proj, conv_w, wouta, tm, [t for t, _ in sums1])
    shards1 = [_add_chips(where, own, r, "add_chips_" + nm) for (_, own), r, nm in zip(sums1, from_chips1, names1)]
    dw_in = _dw(n1, dproj, D, _tile(seq, 2048), "dw_a_in")
    grads2 = [dw_in, dw_outa.reshape(N_CHIPS, D // 4, D)]
    names2 = ["a_w_in", "a_w_out"]
    from_sibling2, (g_wkv, g_wbin, g_woutb), _ = _sibling_exchange("to_sibling_2", to_sibling=grads2, shards=shards1)
    sums2 = [_add_sibling(where, g, r, "add_sibling_" + nm) for g, r, nm in zip(grads2, from_sibling2, names2)]
    nt = seq // tmw
    dn_first, from_chips2 = _a_in_bwd_matmul(dproj, win_g, tmw, max(nt - max(nt // 4, 1), 1), [t for t, _ in sums2])
    grad_x, dg_apre = _a_in_bwd(dn_first, dproj, xs, dh1, win_g, g_apre, tm)
    shards2 = [_add_chips(where, own, r, "add_chips_" + nm) for (_, own), r, nm in zip(sums2, from_chips2, names2)]
    drel, dsink = _bias_fold(dtab)

    smalls = jnp.concatenate([
        dg_apre[0:1], dg_b[2:3], dg_b[0:1], dg_b[1:2], dg_bpost[0:1], _pad_cols(dsink[0:1], D),
        _pad_cols(loss_acc[0:1], D), jnp.zeros((1, D), F32), dconv_w], axis=0)
    assert smalls.shape == (SMALL_ROWS, D)
    _, (g_win, g_wouta), gathered = _sibling_exchange("share_last", shards=shards2, smalls=(smalls, drel))
    tot, tot_rel = _sum_smalls(gathered)

    big = {}
    for nm, g, w, m, v in [("a_w_in", g_win, a_w_in, m_a_w_in, v_a_w_in), ("a_w_out", g_wouta, a_w_out, m_a_w_out, v_a_w_out),
                           ("w_kv", g_wkv, w_kv, m_w_kv, v_w_kv), ("b_w_in", g_wbin, b_w_in, m_b_w_in, v_b_w_in),
                           ("b_w_out", g_woutb, b_w_out, m_b_w_out, v_b_w_out)]:
        shp = w.shape
        two = (shp[-2], shp[-1])
        d, nm_, nv_ = _adamw(g, w.reshape(two), m.reshape(two), v.reshape(two), "adamw_" + nm)
        big[nm] = (g.reshape(shp), d.reshape(shp), nm_.reshape(shp), nv_.reshape(shp))

    given = {"a_pre_norm": (a_pre_norm, m_a_pre_norm, v_a_pre_norm), "a_conv_w": (a_conv_w, m_a_conv_w, v_a_conv_w),
             "a_post_norm": (a_post_norm, m_a_post_norm, v_a_post_norm), "kv_norm": (kv_norm, m_kv_norm, v_kv_norm),
             "rel_bias": (rel_bias, m_rel_bias, v_rel_bias), "b_pre_norm": (b_pre_norm, m_b_pre_norm, v_b_pre_norm),
             "b_sinks": (b_sinks, m_b_sinks, v_b_sinks), "b_post_norm": (b_post_norm, m_b_post_norm, v_b_post_norm)}
    small = _small_update(where[1:2], tot, tot_rel, {nm: tuple(a.reshape(SMALL_PLACES[nm][2]) for a in wmv)
                                            for nm, wmv in given.items()})
    order = ["a_pre_norm", "a_w_in", "a_conv_w", "a_w_out", "a_post_norm", "kv_norm", "w_kv", "rel_bias",
             "b_pre_norm", "b_w_in", "b_sinks", "b_w_out", "b_post_norm"]
    outs = []
    for which in range(4):
        for nm in order:
            outs.append(big[nm][which] if nm in big else small[nm][which].reshape(given[nm][0].shape))
    loss = 0.5 * tot[LOSS_ROW, 0]
    return (loss, grad_x.reshape(x.shape), *outs)
```

```python
import math

import jax
import jax.numpy as jnp
from jax import lax
from jax.experimental import pallas as pl
from jax.experimental.pallas import tpu as pltpu

F32 = jnp.float32
BF16 = jnp.bfloat16
MESH = pl.DeviceIdType.MESH
SDS = jax.ShapeDtypeStruct

D = 1024
HEAD_DIM = 64
N_HEADS = 16
N_KV = 2
GROUP = 8
KV_W = 128
BLK = 128
N_BUCKETS = 32
MAX_EXACT = 16
MAX_DISTANCE = 128
EPS = 1e-6
NEG_INF = -1e30
Q_SCALE = HEAD_DIM ** -0.5

ADAM_LR = 0.001
ADAM_B1 = 0.9
ADAM_B2 = 0.999
ADAM_EPS = 1e-08
ADAM_WD = 0.01
ADAM_STEP = 10

N_CHIPS = 4
N_DEV = 8
BIN_COLS = 2 * D // N_CHIPS
VMEM_LIMIT = 56 * 1024 * 1024
SMALL_ROWS = 16
LOSS_ROW = 6
SMALL_PLACES = {
    "a_pre_norm": ("quarter", 0, (1, D // 4)), "a_conv_w": ("quarter", 8, (3, D // 4)),
    "a_post_norm": ("quarter", 1, (1, D // 4)), "kv_norm": ("rows", 2, (1, D)),
    "rel_bias": ("rel", 0, (N_BUCKETS, N_HEADS)), "b_pre_norm": ("rows", 3, (1, D)),
    "b_sinks": ("rows", 5, (1, N_HEADS)), "b_post_norm": ("rows", 4, (1, D)),
}
HALO = 16


def _bucket_thresholds():
    def bucket(d):
        big = MAX_EXACT + int(math.log(d / MAX_EXACT) / math.log(MAX_DISTANCE / MAX_EXACT)
                              * (N_BUCKETS - MAX_EXACT))
        return d if d < MAX_EXACT else min(big, N_BUCKETS - 1)
    out = []
    for b in range(MAX_EXACT + 1, N_BUCKETS):
        out.append(min(d for d in range(MAX_EXACT, MAX_DISTANCE) if bucket(d) >= b))
    return tuple(out)


BUCKET_THRESHOLDS = _bucket_thresholds()


def _params(semantics=None, vmem=VMEM_LIMIT):
    return pltpu.CompilerParams(dimension_semantics=semantics, vmem_limit_bytes=vmem)


def _tile(n, pref):
    return pref if n >= 2 * pref else max(n // 2, 8)


def _rms_scale(v):
    return lax.rsqrt(jnp.mean(v * v, axis=-1, keepdims=True) + EPS)


def _nt(a, b):
    return lax.dot_general(a, b, (((1,), (1,)), ((), ())), preferred_element_type=F32)


def _tn(a, b):
    return lax.dot_general(a, b, (((0,), (0,)), ((), ())), preferred_element_type=F32)


def _nn(a, b):
    return jnp.dot(a, b, preferred_element_type=F32)


def _silu_parts(z):
    sg = jax.nn.sigmoid(z)
    return sg, z * sg


def _dsilu(z, sg):
    return sg * (1.0 + z * (1.0 - sg))


def _acc_row(ref, row, val):
    ref[row:row + 1, :] += val


def _gather_copies(outs, splits, ici_send, ici_recv, d2d_send, d2d_recv):
    x, y, c = lax.axis_index("x"), lax.axis_index("y"), lax.axis_index("c")
    k = 2 * x + y
    sibling = (x, y, 1 - c)

    def part(o_ref, chip, core, split):
        if not split:
            return o_ref.at[chip]
        h = o_ref.shape[1] // 2
        return o_ref.at[chip, pl.ds(pl.multiple_of(core * h, 16), h)]

    def remote(ref, a, j, sems, to):
        return pltpu.make_async_remote_copy(src_ref=ref, dst_ref=ref, send_sem=sems[0].at[3 * a + j],
                                            recv_sem=sems[1].at[3 * a + j], device_id=to, device_id_type=MESH)

    copies = []
    for a, (o_ref, split) in enumerate(zip(outs, splits)):
        for j, (px, py) in enumerate([(x, 1 - y), (1 - x, y), (1 - x, 1 - y)]):
            kj = 2 * px + py
            ici, d2d = (ici_send, ici_recv), (d2d_send, d2d_recv)
            copies.append((remote(part(o_ref, k, c, split), a, j, ici, (px, py, c)),
                           remote(part(o_ref, kj, c, split), a, j, ici, (px, py, c)),
                           remote(part(o_ref, kj, c, split), a, j, d2d, sibling) if split else None,
                           remote(part(o_ref, kj, 1 - c, split), a, j, d2d, sibling) if split else None))
    return copies


def _gather_sems(n):
    return [pltpu.SemaphoreType.DMA((3 * n,)) for _ in range(4)]


def _prepare_weights(shards, small):
    n = len(shards)

    def body(*refs):
        ins, small_in = refs[:n], refs[n]
        outs, small_out = refs[n + 1:2 * n + 1], refs[2 * n + 1]
        stages, put_sem = refs[2 * n + 2:3 * n + 2], refs[3 * n + 2]
        sems = refs[3 * n + 3:]
        k = 2 * lax.axis_index("x") + lax.axis_index("y")
        puts = []
        for a, (i_ref, stage, o_ref) in enumerate(zip(ins, stages, outs)):
            stage[...] = i_ref[...].astype(BF16)
            puts.append(pltpu.make_async_copy(stage, o_ref.at[k], put_sem.at[a]))
            puts[-1].start()
        small_out[k] = small_in[...]
        copies = _gather_copies([small_out], [False], *sems)
        for send, _, _, _ in copies:
            send.start()
        for _, arrival, _, _ in copies:
            arrival.wait_recv()
        for send, _, _, _ in copies:
            send.wait_send()
        for put in puts:
            put.wait()

    vm = pl.BlockSpec(memory_space=pltpu.VMEM)
    anyspace = pl.BlockSpec(memory_space=pl.ANY)
    out_shape = [SDS((N_CHIPS,) + s.shape, BF16) for s in shards] + [SDS((N_CHIPS,) + small.shape, F32)]
    return pl.pallas_call(
        body, name="prepare_weights", out_shape=out_shape,
        in_specs=[vm] * (n + 1), out_specs=[anyspace] * n + [vm],
        scratch_shapes=[pltpu.VMEM(s.shape, BF16) for s in shards] + [pltpu.SemaphoreType.DMA((n,))] + _gather_sems(1),
        compiler_params=pltpu.CompilerParams(vmem_limit_bytes=VMEM_LIMIT),
    )(*shards, small)


def _a_in(chip, x, g_pre, weights, tm):
    s = x.shape[0]
    nt = s // tm
    n = len(weights)

    def body(chip_ref, x_ref, g_ref, *refs):
        proj_ref, n1_ref = refs[n:n + 2]
        gathered = refs[n + 2:2 * n + 2]
        wbuf, n1_all, fetch_sem = refs[2 * n + 2:2 * n + 5]
        sems = refs[2 * n + 5:]
        jj, i = pl.program_id(0), pl.program_id(1)
        copies = _gather_copies(gathered, [True] * n, *sems)

        def fetch(rel):
            slot = jnp.bitwise_xor(chip_ref[0], rel)
            return pltpu.make_async_copy(gathered[0].at[slot], wbuf.at[rel % 2], fetch_sem.at[rel % 2])

        @pl.when((jj == 0) & (i == 0))
        def _():
            fetch(0).start()
            copies[0][0].start()
            copies[1][0].start()
            fetch(0).wait()

        for rel in (1, 2, 3):
            @pl.when((jj == rel) & (i == 0))
            def _():
                fetch(rel).wait()

        @pl.when(jj == 0)
        def _():
            xv = x_ref[...]
            n1 = (xv * _rms_scale(xv) * g_ref[...]).astype(BF16)
            n1_ref[...] = n1
            n1_all[i] = n1
        proj_ref[...] = _nn(n1_all[i], wbuf[jj % 2]).astype(BF16)

        for rel in (1, 2, 3):
            @pl.when((jj == rel - 1) & (i == max(nt - 2, nt // 2)))
            def _():
                _, arrival, forward, forwarded = copies[rel - 1]
                arrival.wait_recv()
                forward.start()
                forwarded.wait_recv()
                fetch(rel).start()
                if rel == 1:
                    for send, _, _, _ in copies[2:]:
                        send.start()

        @pl.when((jj == 3) & (i == max(nt - 2, 0)))
        def _():
            for _, arrival, forward, _ in copies[3:]:
                arrival.wait_recv()
                forward.start()

        @pl.when((jj == 3) & (i == nt - 1))
        def _():
            for _, _, _, forwarded in copies[3:]:
                forwarded.wait_recv()
            for send, _, forward, _ in copies:
                forward.wait_send()
                send.wait_send()

    anyspace = pl.BlockSpec(memory_space=pl.ANY)
    proj, n1, *gathered = pl.pallas_call(
        body, name="a_in",
        grid_spec=pltpu.PrefetchScalarGridSpec(
            num_scalar_prefetch=1, grid=(4, nt),
            in_specs=[pl.BlockSpec((tm, D), lambda jj, i, c: (jnp.where(jj == 0, i, nt - 1), 0)),
                      pl.BlockSpec((1, D), lambda jj, i, c: (0, 0))] + [anyspace] * n,
            out_specs=[pl.BlockSpec((tm, D), lambda jj, i, c: (i, jnp.bitwise_xor(c[0], jj))),
                       pl.BlockSpec((tm, D), lambda jj, i, c: (jnp.where(jj == 0, i, nt - 1), 0))] + [anyspace] * n,
            scratch_shapes=[pltpu.VMEM((2, D, D), BF16), pltpu.VMEM((nt, tm, D), BF16),
                            pltpu.SemaphoreType.DMA((2,))] + _gather_sems(n)),
        out_shape=[SDS((s, 4 * D), BF16), SDS((s, D), BF16)] + [SDS(w.shape, w.dtype) for w in weights],
        input_output_aliases={3 + a: 2 + a for a in range(n)},
        compiler_params=_params(("arbitrary", "arbitrary")),
    )(chip, x, g_pre, *weights)
    return proj, n1, gathered


def _shift_rows(v, last, second_last, rows):
    v1 = jnp.where(rows >= 1, pltpu.roll(v, 1, 0), last)
    v2 = jnp.where(rows >= 2, pltpu.roll(v, 2, 0), jnp.where(rows == 1, last, second_last))
    return v1, v2


def _a_mix(proj, x, conv_w, w_out, g_post, tm):
    s = x.shape[0]

    def body(proj_ref, x_ref, cw_ref, w_ref, g_ref, ya_ref, oa_ref, h1_ref, carry):
        @pl.when(pl.program_id(0) == 0)
        def _():
            carry[...] = jnp.zeros_like(carry)
        v = proj_ref[:, D:2 * D].astype(F32) * proj_ref[:, 2 * D:3 * D].astype(F32)
        rows = lax.broadcasted_iota(jnp.int32, (tm, D), 0)
        before = carry[...]
        v1, v2 = _shift_rows(v, before[7:8, :], before[6:7, :], rows)
        carry[...] = v[tm - 8:tm, :]
        conv = cw_ref[0:1, :] * v2 + cw_ref[1:2, :] * v1 + cw_ref[2:3, :] * v
        _, sz = _silu_parts(proj_ref[:, 3 * D:4 * D].astype(F32))
        ya = (proj_ref[:, 0:D].astype(F32) * conv * sz).astype(BF16)
        ya_ref[...] = ya
        oa = _nn(ya, w_ref[...])
        oa_ref[...] = oa.astype(BF16)
        h1_ref[...] = x_ref[...] + oa * _rms_scale(oa) * g_ref[...]

    row = lambda i: (i, 0)
    fix = lambda i: (0, 0)
    return pl.pallas_call(
        body, name="a_mix", grid=(s // tm,),
        in_specs=[pl.BlockSpec((tm, 4 * D), row), pl.BlockSpec((tm, D), row), pl.BlockSpec((8, D), fix),
                  pl.BlockSpec((D, D), fix), pl.BlockSpec((1, D), fix)],
        out_specs=[pl.BlockSpec((tm, D), row)] * 3,
        out_shape=[SDS((s, D), BF16), SDS((s, D), BF16), SDS((s, D), F32)],
        scratch_shapes=[pltpu.VMEM((8, D), F32)],
        compiler_params=_params(("arbitrary",)),
    )(proj, x, conv_w, w_out, g_post)


def _b_in(h1, g_kv, g_pre, w_kv, wbin_g, tm):
    s = h1.shape[0]

    def body(h_ref, gk_ref, gb_ref, wkv_ref, wb_ref, kv_ref, q_ref, z_ref):
        h = h_ref[...]
        hh = h * _rms_scale(h)
        nk = (hh * gk_ref[...]).astype(BF16)
        nb = (hh * gb_ref[...]).astype(BF16)
        kv_ref[...] = _nn(nk, wkv_ref[...]).astype(BF16)
        for j in range(2):
            q_ref[:, BIN_COLS * j:BIN_COLS * (j + 1)] = (_nn(nb, wb_ref[j]) * Q_SCALE).astype(BF16)
            z_ref[:, BIN_COLS * j:BIN_COLS * (j + 1)] = _nn(nb, wb_ref[2 + j]).astype(BF16)

    row = lambda i: (i, 0)
    fix = lambda i: (0, 0)
    return pl.pallas_call(
        body, name="b_in", grid=(s // tm,),
        in_specs=[pl.BlockSpec((tm, D), row), pl.BlockSpec((1, D), fix), pl.BlockSpec((1, D), fix),
                  pl.BlockSpec((D, 2 * KV_W), fix), pl.BlockSpec((N_CHIPS, D, BIN_COLS), lambda i: (0, 0, 0))],
        out_specs=[pl.BlockSpec((tm, 2 * KV_W), row), pl.BlockSpec((tm, D), row), pl.BlockSpec((tm, D), row)],
        out_shape=[SDS((s, 2 * KV_W), BF16), SDS((s, D), BF16), SDS((s, D), BF16)],
        compiler_params=_params(("parallel",)),
    )(h1, g_kv, g_pre, w_kv, wbin_g)


def _band_buckets():
    q = lax.broadcasted_iota(jnp.int32, (BLK, 2 * BLK), 0)
    k = lax.broadcasted_iota(jnp.int32, (BLK, 2 * BLK), 1)
    dist = q + BLK - k
    bucket = jnp.where(dist < MAX_EXACT, dist, MAX_EXACT)
    for t in BUCKET_THRESHOLDS:
        bucket = bucket + jnp.where(dist >= t, 1, 0)
    in_window = (dist >= 0) & (dist < BLK)
    return jnp.where(in_window, bucket, -1)


def _head_place(h):
    kh, j, e = h // GROUP, (h % GROUP) // 2, h % 2
    return kh, slice(BLK * j, BLK * (j + 1)), slice(2 * BLK * e, 2 * BLK * (e + 1))


def _bias_table(rel_bias, sinks):
    def body(rb_ref, sink_ref, tab_ref):
        bucket = _band_buckets()
        col = lax.broadcasted_iota(jnp.int32, (BLK, 2 * BLK), 1)
        for h in range(N_HEADS):
            acc = jnp.where(bucket < 0, NEG_INF, 0.0).astype(F32)
            for b in range(N_BUCKETS):
                acc = jnp.where(bucket == b, rb_ref[b, h], acc)
            acc = jnp.where(col == 0, sink_ref[h], acc)
            kh, rows, cols = _head_place(h)
            tab_ref[1, kh, rows, cols] = acc
            tab_ref[0, kh, rows, cols] = jnp.where((col > 0) & (col < BLK), NEG_INF, acc)

    return pl.pallas_call(
        body, name="bias_table", out_shape=SDS((2, N_KV, 4 * BLK, 4 * BLK), F32),
        in_specs=[pl.BlockSpec(memory_space=pltpu.SMEM), pl.BlockSpec(memory_space=pltpu.SMEM)],
        out_specs=pl.BlockSpec(memory_space=pltpu.VMEM),
    )(rel_bias, sinks)


def _bias_fold(dtab):
    def body(dtab_ref, out_ref, dsink_ref):
        bucket = _band_buckets()
        row = lax.broadcasted_iota(jnp.int32, (N_BUCKETS, 128), 0)
        lane = lax.broadcasted_iota(jnp.int32, (N_BUCKETS, 128), 1)
        row8 = lax.broadcasted_iota(jnp.int32, (8, 128), 0)
        lane8 = lax.broadcasted_iota(jnp.int32, (8, 128), 1)
        acc = jnp.zeros((N_BUCKETS, 128), F32)
        dsink = jnp.zeros((8, 128), F32)
        for h in range(N_HEADS):
            kh, rows, cols = _head_place(h)
            dt = dtab_ref[kh, rows, cols]
            for b in range(N_BUCKETS):
                val = jnp.sum(jnp.where(bucket == b, dt, 0.0))
                acc = acc + jnp.where((row == b) & (lane == h), val, 0.0)
            dsink = dsink + jnp.where((row8 == 0) & (lane8 == h), jnp.sum(dt[:, 0:1]), 0.0)
        out_ref[...] = acc
        dsink_ref[...] = dsink

    vm = pl.BlockSpec(memory_space=pltpu.VMEM)
    return pl.pallas_call(
        body, name="bias_fold", out_shape=[SDS((N_BUCKETS, 128), F32), SDS((8, 128), F32)],
        in_specs=[vm], out_specs=[vm, vm],
    )(dtab)


def _pair_operands(prev, cur):
    t = jnp.concatenate([prev, cur], axis=0).astype(F32)
    t = jnp.where(lax.broadcasted_iota(jnp.int32, t.shape, 0) == 0, 0.0, t)
    tr = pltpu.roll(t, HEAD_DIM, 1)
    lo = lax.broadcasted_iota(jnp.int32, t.shape, 1) < HEAD_DIM
    zero = jnp.zeros_like(t)
    head0 = jnp.concatenate([jnp.where(lo, t, zero), jnp.where(lo, zero, tr)], axis=0).astype(BF16)
    head1 = jnp.concatenate([jnp.where(lo, tr, zero), jnp.where(lo, zero, t)], axis=0).astype(BF16)
    return head0, head1


def _pair_fold(d0, d1):
    lo = lax.broadcasted_iota(jnp.int32, (2 * BLK, KV_W), 1) < HEAD_DIM
    zero = jnp.zeros((2 * BLK, KV_W), F32)
    g0 = jnp.where(lo, d0[0:256], zero) + pltpu.roll(jnp.where(lo, zero, d0[256:512]), HEAD_DIM, 1)
    g1 = pltpu.roll(jnp.where(lo, d1[0:256], zero), HEAD_DIM, 1) + jnp.where(lo, zero, d1[256:512])
    return jnp.where(lax.broadcasted_iota(jnp.int32, (2 * BLK, KV_W), 0) == 0, 0.0, g0 + g1)


def _stack_pairs(ref, kh, rows=slice(None)):
    return jnp.concatenate([ref[rows, 128 * (4 * kh + j):128 * (4 * kh + j + 1)] for j in range(4)], axis=0)


def _table_spec():
    return pl.BlockSpec((1, N_KV, 4 * BLK, 4 * BLK), lambda n: (jnp.minimum(n, 1), 0, 0, 0))


def _attn_fwd(q, kv, tab):
    s = q.shape[0]

    def body(q_ref, kp_ref, k0_ref, k1_ref, vp_ref, v0_ref, v1_ref, tab0_ref, tab1_ref, att_ref, stats_ref):
        lane = lax.broadcasted_iota(jnp.int32, (BLK, 128), 1)
        for sub, (kp, kc, vp, vc, tab_ref) in enumerate([(kp_ref, k0_ref, vp_ref, v0_ref, tab0_ref),
                                                         (k0_ref, k1_ref, v0_ref, v1_ref, tab1_ref)]):
            rows = slice(BLK * sub, BLK * (sub + 1))
            k2 = _pair_operands(kp[...], kc[...])
            v2 = _pair_operands(vp[...], vc[...])
            stats = jnp.zeros((BLK, 128), F32)
            for kh in range(N_KV):
                sc = _nt(_stack_pairs(q_ref, kh, rows), k2[kh])
                ps = []
                for e in range(2):
                    lg = sc[:, 256 * e:256 * (e + 1)] + tab_ref[0, kh, :, 256 * e:256 * (e + 1)]
                    m = jnp.max(lg, axis=-1, keepdims=True)
                    ex = jnp.exp(lg - m)
                    den = jnp.sum(ex, axis=-1, keepdims=True)
                    ps.append(ex * (1.0 / den))
                    lse = m + jnp.log(den)
                    for j in range(4):
                        stats = jnp.where(lane == GROUP * kh + 2 * j + e, lse[BLK * j:BLK * (j + 1)], stats)
                out = _nn(jnp.concatenate(ps, axis=1).astype(BF16), v2[kh])
                for j in range(4):
                    att_ref[rows, 128 * (4 * kh + j):128 * (4 * kh + j + 1)] = out[BLK * j:BLK * (j + 1)].astype(BF16)
            stats_ref[rows, :] = stats

    two = lambda m: (m, 0)
    table = lambda pick: pl.BlockSpec((1, N_KV, 4 * BLK, 4 * BLK), lambda m: (pick(m), 0, 0, 0))
    return pl.pallas_call(
        body, name="attn_fwd", grid=(s // (2 * BLK),),
        in_specs=[pl.BlockSpec((2 * BLK, D), two)]
        + [pl.BlockSpec((BLK, KV_W), lambda m, col=col, off=off: (jnp.maximum(2 * m + off, 0), col))
           for col in (0, 1) for off in (-1, 0, 1)]
        + [table(lambda m: jnp.minimum(m, 1)), table(lambda m: 1)],
        out_specs=[pl.BlockSpec((2 * BLK, D), two), pl.BlockSpec((2 * BLK, 128), two)],
        out_shape=[SDS((s, D), BF16), SDS((s, 128), F32)],
        compiler_params=_params(("parallel",)),
    )(q, kv, kv, kv, kv, kv, kv, tab, tab)


def _mid(att, zb, h1, tgt, w_out, g_post, tm):
    s = att.shape[0]
    nt = s // tm

    def body(att_ref, z_ref, h1_ref, t_ref, w_ref, g_ref,
             dh_ref, dqz_ref, datt_ref, loss_ref, dg_ref, dw_ref, dw_acc):
        @pl.when(pl.program_id(0) == 0)
        def _():
            loss_ref[...] = jnp.zeros_like(loss_ref)
            dg_ref[...] = jnp.zeros_like(dg_ref)
            dw_acc[...] = jnp.zeros_like(dw_acc)
        att = att_ref[...].astype(F32)
        z = z_ref[...].astype(F32)
        sg, sz = _silu_parts(z)
        ob = (att * sz).astype(BF16)
        y2 = _nn(ob, w_ref[...])
        r2 = _rms_scale(y2)
        yh = y2 * r2
        g = g_ref[...]
        err = (h1_ref[...] + yh * g) - t_ref[...]
        loss_ref[...] += jnp.sum(jnp.sum(err * err, axis=-1, keepdims=True) / D)
        dh = err / D
        dh_ref[...] = dh
        _acc_row(dg_ref, 0, jnp.sum(dh * yh, axis=0, keepdims=True))
        dyh = dh * g
        dy = (r2 * (dyh - yh * jnp.mean(dyh * yh, axis=-1, keepdims=True))).astype(BF16)
        dw_acc[...] += _tn(ob, dy)
        dob = _nt(dy, w_ref[...])
        datt_ref[...] = (dob * sz).astype(BF16)
        dqz_ref[...] = (dob * att * _dsilu(z, sg)).astype(BF16)

        @pl.when(pl.program_id(0) == nt - 1)
        def _():
            pltpu.sync_copy(dw_acc, dw_ref)

    row = lambda i: (i, 0)
    fix = lambda i: (0, 0)
    return pl.pallas_call(
        body, name="mid", grid=(nt,),
        in_specs=[pl.BlockSpec((tm, D), row)] * 4 + [pl.BlockSpec((D, D), fix), pl.BlockSpec((1, D), fix)],
        out_specs=[pl.BlockSpec((tm, D), row), pl.BlockSpec((tm, D), lambda i: (i, 1)), pl.BlockSpec((tm, D), row),
                   pl.BlockSpec((8, 128), fix), pl.BlockSpec((8, D), fix), pl.BlockSpec(memory_space=pl.ANY)],
        out_shape=[SDS((s, D), F32), SDS((s, 2 * D), BF16), SDS((s, D), BF16), SDS((8, 128), F32),
                   SDS((8, D), F32), SDS((D, D), F32)],
        scratch_shapes=[pltpu.VMEM((D, D), F32)],
        compiler_params=_params(("arbitrary",)),
    )(att, zb, h1, tgt, w_out, g_post)


def _attn_bwd(q, kv, datt, stats, tab, dqz):
    s = q.shape[0]
    nb = s // BLK

    def body(q_ref, kp_ref, kc_ref, vp_ref, vc_ref, da_ref, st_ref, tab_ref, dqz_in,
             dq_ref, dkv_ref, dtab_ref, dk_carry, dv_carry):
        del dqz_in
        n = pl.program_id(0)

        @pl.when(n == 0)
        def _():
            dtab_ref[...] = jnp.zeros_like(dtab_ref)
            dk_carry[...] = jnp.zeros_like(dk_carry)
            dv_carry[...] = jnp.zeros_like(dv_carry)

        @pl.when(n < nb)
        def _():
            k2 = _pair_operands(kp_ref[...], kc_ref[...])
            v2 = _pair_operands(vp_ref[...], vc_ref[...])
            lane = lax.broadcasted_iota(jnp.int32, (BLK, 128), 1)
            stats = st_ref[...]
            dk2, dv2 = [], []
            for kh in range(N_KV):
                qs = _stack_pairs(q_ref, kh)
                das = _stack_pairs(da_ref, kh)
                sc = _nt(qs, k2[kh])
                dp = _nt(das, v2[kh])
                ps, dss = [], []
                for e in range(2):
                    heads = [GROUP * kh + 2 * j + e for j in range(4)]
                    lse = jnp.concatenate([jnp.sum(jnp.where(lane == h, stats, 0.0), axis=-1, keepdims=True)
                                           for h in heads], axis=0)
                    cols = slice(256 * e, 256 * (e + 1))
                    p = jnp.exp(sc[:, cols] + tab_ref[0, kh, :, cols] - lse)
                    delta = jnp.sum(p * dp[:, cols], axis=-1, keepdims=True)
                    ds = p * (dp[:, cols] - delta)
                    dtab_ref[kh, :, cols] += ds
                    ps.append(p)
                    dss.append(ds)
                p2 = jnp.concatenate(ps, axis=1).astype(BF16)
                ds2 = jnp.concatenate(dss, axis=1).astype(BF16)
                dq = _nn(ds2, k2[kh]) * Q_SCALE
                for j in range(4):
                    dq_ref[:, 128 * (4 * kh + j):128 * (4 * kh + j + 1)] = dq[BLK * j:BLK * (j + 1)].astype(BF16)
                dk2.append(_tn(ds2, qs))
                dv2.append(_tn(p2, das))
            dkk = _pair_fold(dk2[0], dk2[1])
            dvv = _pair_fold(dv2[0], dv2[1])
            dkv_ref[:, 0:KV_W] = (dk_carry[...] + dkk[0:BLK]).astype(BF16)
            dkv_ref[:, KV_W:2 * KV_W] = (dv_carry[...] + dvv[0:BLK]).astype(BF16)
            dk_carry[...] = dkk[BLK:2 * BLK]
            dv_carry[...] = dvv[BLK:2 * BLK]

        @pl.when(n == nb)
        def _():
            dkv_ref[:, 0:KV_W] = dk_carry[...].astype(BF16)
            dkv_ref[:, KV_W:2 * KV_W] = dv_carry[...].astype(BF16)

    cur = lambda n: (jnp.minimum(n, nb - 1), 0)
    prev = lambda n: (jnp.clip(n - 1, 0, nb - 1), 0)
    return pl.pallas_call(
        body, name="attn_bwd", grid=(nb + 1,),
        in_specs=[pl.BlockSpec((BLK, D), cur),
                  pl.BlockSpec((BLK, KV_W), prev), pl.BlockSpec((BLK, KV_W), cur),
                  pl.BlockSpec((BLK, KV_W), lambda n: (jnp.clip(n - 1, 0, nb - 1), 1)),
                  pl.BlockSpec((BLK, KV_W), lambda n: (jnp.minimum(n, nb - 1), 1)),
                  pl.BlockSpec((BLK, D), cur), pl.BlockSpec((BLK, 128), cur), _table_spec(),
                  pl.BlockSpec(memory_space=pl.ANY)],
        out_specs=[pl.BlockSpec((BLK, D), cur), pl.BlockSpec((BLK, 2 * KV_W), prev),
                   pl.BlockSpec((N_KV, 4 * BLK, 4 * BLK), lambda n: (0, 0, 0))],
        out_shape=[SDS((s, 2 * D), BF16), SDS((s, 2 * KV_W), BF16), SDS((N_KV, 4 * BLK, 4 * BLK), F32)],
        scratch_shapes=[pltpu.VMEM((BLK, KV_W), F32), pltpu.VMEM((BLK, KV_W), F32)],
        input_output_aliases={8: 0},
        compiler_params=_params(("arbitrary",)),
    )(q, kv, kv, kv, kv, datt, stats, tab, dqz)


def _b_bwd(dqz, dkv, h1, dh2, oa, wbin_g, w_kv, g_kv, g_pre, g_apost, tm):
    s = h1.shape[0]
    nt = s // tm

    def body(dqz_ref, dkv_ref, h_ref, dh2_ref, oa_ref, wb_ref, wkv_ref, gk_ref, gb_ref, ga_ref,
             dh1_ref, doa_ref, dg_ref, dwb_ref, dwkv_ref, wcat, dwb_acc, dwkv_acc):
        @pl.when(pl.program_id(0) == 0)
        def _():
            dg_ref[...] = jnp.zeros_like(dg_ref)
            dwb_acc[...] = jnp.zeros_like(dwb_acc)
            dwkv_acc[...] = jnp.zeros_like(dwkv_acc)
            for j in range(N_CHIPS):
                pltpu.sync_copy(wb_ref.at[j], wcat.at[:, pl.ds(BIN_COLS * j, BIN_COLS)])
        dnb = _nt(dqz_ref[...], wcat[...])
        dnk = _nt(dkv_ref[...], wkv_ref[...])
        h = h_ref[...]
        r = _rms_scale(h)
        hh = h * r
        dwb_acc[...] += _tn((hh * gb_ref[...]).astype(BF16), dqz_ref[...])
        dwkv_acc[...] += _tn((hh * gk_ref[...]).astype(BF16), dkv_ref[...])
        _acc_row(dg_ref, 0, jnp.sum(dnk * hh, axis=0, keepdims=True))
        _acc_row(dg_ref, 1, jnp.sum(dnb * hh, axis=0, keepdims=True))
        dhh = dnb * gb_ref[...] + dnk * gk_ref[...]
        dh1 = dh2_ref[...] + r * (dhh - hh * jnp.mean(dhh * hh, axis=-1, keepdims=True))
        dh1_ref[...] = dh1
        oa = oa_ref[...].astype(F32)
        ra = _rms_scale(oa)
        oh = oa * ra
        _acc_row(dg_ref, 2, jnp.sum(dh1 * oh, axis=0, keepdims=True))
        doh = dh1 * ga_ref[...]
        doa_ref[...] = (ra * (doh - oh * jnp.mean(doh * oh, axis=-1, keepdims=True))).astype(BF16)

        @pl.when(pl.program_id(0) == nt - 1)
        def _():
            for j in range(N_CHIPS):
                pltpu.sync_copy(dwb_acc.at[:, pl.ds(BIN_COLS * j, BIN_COLS)], dwb_ref.at[j])
            pltpu.sync_copy(dwkv_acc, dwkv_ref)

    row = lambda i: (i, 0)
    fix = lambda i: (0, 0)
    anyspace = pl.BlockSpec(memory_space=pl.ANY)
    return pl.pallas_call(
        body, name="b_bwd", grid=(nt,),
        in_specs=[pl.BlockSpec((tm, 2 * D), row), pl.BlockSpec((tm, 2 * KV_W), row), pl.BlockSpec((tm, D), row),
                  pl.BlockSpec((tm, D), row), pl.BlockSpec((tm, D), row), anyspace, pl.BlockSpec((D, 2 * KV_W), fix),
                  pl.BlockSpec((1, D), fix), pl.BlockSpec((1, D), fix), pl.BlockSpec((1, D), fix)],
        out_specs=[pl.BlockSpec((tm, D), row), pl.BlockSpec((tm, D), row), pl.BlockSpec((8, D), fix), anyspace, anyspace],
        out_shape=[SDS((s, D), F32), SDS((s, D), BF16), SDS((8, D), F32), SDS((N_CHIPS, D, BIN_COLS), F32),
                   SDS((D, 2 * KV_W), F32)],
        scratch_shapes=[pltpu.VMEM((D, 2 * D), BF16), pltpu.VMEM((D, 2 * D), F32), pltpu.VMEM((D, 2 * KV_W), F32)],
        compiler_params=_params(("arbitrary",)),
    )(dqz, dkv, h1, dh2, oa, wbin_g, w_kv, g_kv, g_pre, g_apost)


def _chip_exchange(parts, recvs, send, recv):
    x, y, c = lax.axis_index("x"), lax.axis_index("y"), lax.axis_index("c")
    chips = [(x, 1 - y), (1 - x, y), (1 - x, 1 - y)]
    copies = []
    for a, (t, r) in enumerate(zip(parts, recvs)):
        for j, (px, py) in enumerate(chips):
            copies.append(pltpu.make_async_remote_copy(
                src_ref=t.at[2 * px + py], dst_ref=r.at[j], send_sem=send.at[3 * a + j],
                recv_sem=recv.at[3 * a + j], device_id=(px, py, c), device_id_type=MESH))
    return copies


def _exchange_specs(parts):
    anyspace = pl.BlockSpec(memory_space=pl.ANY)
    n = len(parts)
    return ([anyspace] * n, [anyspace] * n, [SDS((3,) + t.shape[1:], t.dtype) for t in parts],
            [pltpu.SemaphoreType.DMA((3 * n,)), pltpu.SemaphoreType.DMA((3 * n,))])


def _a_bwd(doa, ya, proj, conv_w, w_out, tm, parts):
    s = doa.shape[0]
    nt = s // tm
    n = len(parts)
    ex_in, ex_out, ex_shape, ex_sems = _exchange_specs(parts)

    def body(*refs):
        doa_ref, ya_ref, proj_ref, halo_ref, cw_ref, w_ref = refs[:6]
        part_refs = refs[6:6 + n]
        dproj_ref, dcw_ref, dw_ref = refs[6 + n:9 + n]
        recv_refs = refs[9 + n:9 + 2 * n]
        carry, dw_acc, send, recv = refs[9 + 2 * n:]
        i = pl.program_id(0)
        r = nt - 1 - i

        @pl.when(i == 0)
        def _():
            dcw_ref[...] = jnp.zeros_like(dcw_ref)
            carry[...] = jnp.zeros_like(carry)
            dw_acc[...] = jnp.zeros_like(dw_acc)
            for cp in _chip_exchange(part_refs, recv_refs, send, recv):
                cp.start()
        dya = _nt(doa_ref[...], w_ref[...])
        dw_acc[...] += _tn(ya_ref[...], doa_ref[...])
        bg = proj_ref[:, 0:D].astype(F32)
        cg = proj_ref[:, D:2 * D].astype(F32)
        u = proj_ref[:, 2 * D:3 * D].astype(F32)
        z = proj_ref[:, 3 * D:4 * D].astype(F32)
        v = cg * u
        before = jnp.where(r > 0, halo_ref[:, D:2 * D].astype(F32) * halo_ref[:, 2 * D:3 * D].astype(F32), 0.0)
        rows = lax.broadcasted_iota(jnp.int32, (tm, D), 0)
        v1, v2 = _shift_rows(v, before[HALO - 1:HALO, :], before[HALO - 2:HALO - 1, :], rows)
        conv = cw_ref[0:1, :] * v2 + cw_ref[1:2, :] * v1 + cw_ref[2:3, :] * v
        sg, sz = _silu_parts(z)
        dproj_ref[:, 0:D] = (dya * conv * sz).astype(BF16)
        dproj_ref[:, 3 * D:4 * D] = (dya * bg * conv * _dsilu(z, sg)).astype(BF16)
        dconv = dya * bg * sz
        _acc_row(dcw_ref, 0, jnp.sum(dconv * v2, axis=0, keepdims=True))
        _acc_row(dcw_ref, 1, jnp.sum(dconv * v1, axis=0, keepdims=True))
        _acc_row(dcw_ref, 2, jnp.sum(dconv * v, axis=0, keepdims=True))
        after = carry[...]
        up1 = jnp.where(rows < tm - 1, pltpu.roll(dconv, tm - 1, 0), after[0:1, :])
        up2 = jnp.where(rows < tm - 2, pltpu.roll(dconv, tm - 2, 0),
                        jnp.where(rows == tm - 2, after[0:1, :], after[1:2, :]))
        carry[...] = dconv[0:8, :]
        dv = cw_ref[2:3, :] * dconv + cw_ref[1:2, :] * up1 + cw_ref[0:1, :] * up2
        dproj_ref[:, D:2 * D] = (dv * u).astype(BF16)
        dproj_ref[:, 2 * D:3 * D] = (dv * cg).astype(BF16)

        @pl.when(i == nt - 1)
        def _():
            pltpu.sync_copy(dw_acc, dw_ref)
            for cp in _chip_exchange(part_refs, recv_refs, send, recv):
                cp.wait()

    rev = lambda i: (nt - 1 - i, 0)
    fix = lambda i: (0, 0)
    halo = lambda i: (jnp.maximum((nt - 1 - i) * (tm // HALO) - 1, 0), 0)
    dproj, dcw, dw, *got = pl.pallas_call(
        body, name="a_bwd", grid=(nt,),
        in_specs=[pl.BlockSpec((tm, D), rev), pl.BlockSpec((tm, D), rev), pl.BlockSpec((tm, 4 * D), rev),
                  pl.BlockSpec((HALO, 4 * D), halo), pl.BlockSpec((8, D), fix), pl.BlockSpec((D, D), fix)] + ex_in,
        out_specs=[pl.BlockSpec((tm, 4 * D), rev), pl.BlockSpec((8, D), fix), pl.BlockSpec(memory_space=pl.ANY)] + ex_out,
        out_shape=[SDS((s, 4 * D), BF16), SDS((8, D), F32), SDS((D, D), F32)] + ex_shape,
        scratch_shapes=[pltpu.VMEM((8, D), F32), pltpu.VMEM((D, D), F32)] + ex_sems,
        compiler_params=_params(("arbitrary",)),
    )(doa, ya, proj, proj, conv_w, w_out, *parts)
    return dproj, dcw, dw, got


def _dn1(dp_ref, w_ref):
    dn = _nt(dp_ref[:, 0:D], w_ref[0])
    for j in range(1, 4):
        dn = dn + _nt(dp_ref[:, D * j:D * (j + 1)], w_ref[j])
    return dn


def _a_in_bwd_matmul(dproj, win_g, tm, count, parts):
    n = len(parts)
    ex_in, ex_out, ex_shape, ex_sems = _exchange_specs(parts)

    def body(*refs):
        dp_ref, w_ref = refs[:2]
        part_refs = refs[2:2 + n]
        dn_ref = refs[2 + n]
        recv_refs = refs[3 + n:3 + 2 * n]
        wcat = refs[3 + 2 * n]
        sems = refs[4 + 2 * n:]

        @pl.when(pl.program_id(0) == 0)
        def _():
            for cp in _chip_exchange(part_refs, recv_refs, *sems):
                cp.start()
            for j in range(N_CHIPS):
                pltpu.sync_copy(w_ref.at[j], wcat.at[:, pl.ds(D * j, D)])
        dn_ref[...] = _nt(dp_ref[...], wcat[...]).astype(BF16)

        @pl.when(pl.program_id(0) == count - 1)
        def _():
            for cp in _chip_exchange(part_refs, recv_refs, *sems):
                cp.wait()

    row = lambda i: (i, 0)
    dn, *got = pl.pallas_call(
        body, name="a_in_bwd_matmul", grid=(count,),
        in_specs=[pl.BlockSpec((tm, 4 * D), row), pl.BlockSpec(memory_space=pl.ANY)] + ex_in,
        out_specs=[pl.BlockSpec((tm, D), row)] + ex_out,
        out_shape=[SDS((count * tm, D), BF16)] + ex_shape,
        scratch_shapes=[pltpu.VMEM((D, 4 * D), BF16)] + ex_sems,
        compiler_params=_params(("arbitrary",)),
    )(dproj, win_g, *parts)
    return dn, got


def _a_in_bwd(dn_first, dproj, x, dh1, win_g, g_pre, tm):
    s = x.shape[0]
    nt = s // tm
    count = dn_first.shape[0] // tm

    def body(dn_ref, dp_ref, x_ref, dh_ref, w_ref, g_ref, gx_ref, dg_ref, dn_s):
        i = pl.program_id(0)

        @pl.when(i == 0)
        def _():
            dg_ref[...] = jnp.zeros_like(dg_ref)

        @pl.when(i < count)
        def _():
            dn_s[...] = dn_ref[...].astype(F32)

        @pl.when(i >= count)
        def _():
            dn_s[...] = _dn1(dp_ref, w_ref)
        dn = dn_s[...]
        xv = x_ref[...]
        r = _rms_scale(xv)
        xh = xv * r
        _acc_row(dg_ref, 0, jnp.sum(dn * xh, axis=0, keepdims=True))
        dxh = dn * g_ref[...]
        gx_ref[...] = dh_ref[...] + r * (dxh - xh * jnp.mean(dxh * xh, axis=-1, keepdims=True))

    row = lambda i: (i, 0)
    fix = lambda i: (0, 0)
    return pl.pallas_call(
        body, name="a_in_bwd", grid=(nt,),
        in_specs=[pl.BlockSpec((tm, D), lambda i: (jnp.minimum(i, count - 1), 0)),
                  pl.BlockSpec((tm, 4 * D), lambda i: (jnp.maximum(i, count), 0)),
                  pl.BlockSpec((tm, D), row), pl.BlockSpec((tm, D), row),
                  pl.BlockSpec((4, D, D), lambda i: (0, 0, 0)), pl.BlockSpec((1, D), fix)],
        out_specs=[pl.BlockSpec((tm, D), row), pl.BlockSpec((8, D), fix)],
        out_shape=[SDS((s, D), F32), SDS((8, D), F32)],
        scratch_shapes=[pltpu.VMEM((tm, D), F32)],
        compiler_params=_params(("arbitrary",)),
    )(dn_first, dproj, x, dh1, win_g, g_pre)


def _dw(a, b, tn, tmw, name):
    s, k = a.shape
    n = b.shape[1]

    def body(a_ref, b_ref, o_ref):
        @pl.when(pl.program_id(1) == 0)
        def _():
            o_ref[...] = jnp.zeros_like(o_ref)
        o_ref[0] += _tn(a_ref[...], b_ref[...])

    return pl.pallas_call(
        body, name=name, grid=(n // tn, s // tmw),
        in_specs=[pl.BlockSpec((tmw, k), lambda j, t: (t, 0)), pl.BlockSpec((tmw, tn), lambda j, t: (t, j))],
        out_specs=pl.BlockSpec((1, k, tn), lambda j, t: (j, 0, 0)),
        out_shape=SDS((n // tn, k, tn), F32),
        compiler_params=_params(("parallel", "arbitrary")),
    )(a, b)


def _sibling_exchange(name, to_sibling=(), shards=(), smalls=()):
    n_g, n_h, n_s = len(to_sibling), len(shards), len(smalls)

    def body(*refs):
        gs = refs[:n_g]
        pos = n_g + n_h
        small_ins = refs[pos:pos + n_s]
        pos += n_s
        rs, fs = refs[pos:pos + n_g], refs[pos + n_g:pos + n_g + n_h]
        pos += n_g + n_h
        small_alls = refs[pos:pos + n_s]
        pos += n_s
        dsend, drecv, ssend, srecv = refs[pos:]
        x, y, c = lax.axis_index("x"), lax.axis_index("y"), lax.axis_index("c")
        sibling = (x, y, 1 - c)
        sends, arrivals = [], []
        for a, (g, r) in enumerate(zip(gs, rs)):
            h = g.shape[1] // 2
            src = g.at[:, pl.ds(pl.multiple_of((1 - c) * h, 8), h), :]
            sends.append(pltpu.make_async_remote_copy(src_ref=src, dst_ref=r, send_sem=dsend.at[a], recv_sem=drecv.at[a],
                                                      device_id=sibling, device_id_type=MESH))
            arrivals.append(pltpu.make_async_remote_copy(src_ref=r, dst_ref=r, send_sem=dsend.at[a], recv_sem=drecv.at[a],
                                                         device_id=sibling, device_id_type=MESH))
        for b, full in enumerate(fs):
            h = full.shape[0] // 2
            mine = full.at[pl.ds(pl.multiple_of(c * h, 8), h)]
            theirs = full.at[pl.ds(pl.multiple_of((1 - c) * h, 8), h)]
            sends.append(pltpu.make_async_remote_copy(src_ref=mine, dst_ref=mine, send_sem=dsend.at[n_g + b],
                                                      recv_sem=drecv.at[n_g + b], device_id=sibling, device_id_type=MESH))
            arrivals.append(pltpu.make_async_remote_copy(src_ref=mine, dst_ref=theirs, send_sem=dsend.at[n_g + b],
                                                         recv_sem=drecv.at[n_g + b], device_id=sibling, device_id_type=MESH))
        me = 4 * x + 2 * y + c
        for k, (small_in, small_all) in enumerate(zip(small_ins, small_alls)):
            small_all[me] = small_in[...]
            for rel in range(1, N_DEV):
                fx, fy, fc = rel >> 2, (rel >> 1) & 1, rel & 1
                peer = (x + fx - 2 * x * fx, y + fy - 2 * y * fy, c + fc - 2 * c * fc)
                sender = 4 * peer[0] + 2 * peer[1] + peer[2]
                sem = (N_DEV - 1) * k + rel - 1
                sends.append(pltpu.make_async_remote_copy(
                    src_ref=small_in, dst_ref=small_all.at[me], send_sem=ssend.at[sem], recv_sem=srecv.at[sem],
                    device_id=peer, device_id_type=MESH))
                arrivals.append(pltpu.make_async_remote_copy(
                    src_ref=small_in, dst_ref=small_all.at[sender], send_sem=ssend.at[sem], recv_sem=srecv.at[sem],
                    device_id=peer, device_id_type=MESH))
        for cp in sends:
            cp.start()
        for cp in arrivals:
            cp.wait_recv()
        for cp in sends:
            cp.wait_send()

    anyspace = pl.BlockSpec(memory_space=pl.ANY)
    vm = pl.BlockSpec(memory_space=pltpu.VMEM)
    out_shape = [SDS((N_CHIPS, g.shape[1] // 2, g.shape[2]), F32) for g in to_sibling]
    out_shape += [SDS(full.shape, F32) for full in shards]
    out_shape += [SDS((N_DEV,) + sm.shape, F32) for sm in smalls]
    n_d2d = max(n_g + n_h, 1)
    n_all = (N_DEV - 1) * max(n_s, 1)
    outs = pl.pallas_call(
        body, name=name, out_shape=out_shape,
        in_specs=[anyspace] * (n_g + n_h) + [vm] * n_s, out_specs=[anyspace] * (n_g + n_h) + [vm] * n_s,
        scratch_shapes=[pltpu.SemaphoreType.DMA((n_d2d,)), pltpu.SemaphoreType.DMA((n_d2d,)),
                        pltpu.SemaphoreType.DMA((n_all,)), pltpu.SemaphoreType.DMA((n_all,))],
        input_output_aliases={n_g + b: n_g + b for b in range(n_h)},
    )(*to_sibling, *shards, *smalls)
    return outs[:n_g], outs[n_g:n_g + n_h], outs[n_g + n_h:]


def _add_sibling(where, g, r, name):
    _, rows, cols = g.shape
    h = rows // 2
    tr = min(h, 256)
    nh = h // tr

    def body(where_ref, g_ref, r_ref, t_ref, own_ref):
        t = g_ref[0] + r_ref[0]
        t_ref[0] = t.astype(BF16)

        @pl.when(pl.program_id(1) == where_ref[1])
        def _():
            own_ref[...] = t

    return pl.pallas_call(
        body, name=name,
        grid_spec=pltpu.PrefetchScalarGridSpec(
            num_scalar_prefetch=1, grid=(nh, N_CHIPS),
            in_specs=[pl.BlockSpec((1, tr, cols), lambda i, k, w: (k, w[0] * nh + i, 0)),
                      pl.BlockSpec((1, tr, cols), lambda i, k, w: (k, i, 0))],
            out_specs=[pl.BlockSpec((1, tr, cols), lambda i, k, w: (k, i, 0)),
                       pl.BlockSpec((tr, cols), lambda i, k, w: (i, 0))]),
        out_shape=[SDS((N_CHIPS, h, cols), BF16), SDS((h, cols), F32)],
        compiler_params=_params(("parallel", "arbitrary")),
    )(where, g, r)


def _add_chips(where, own, r, name):
    h, cols = own.shape
    tr = min(h, 256)
    nh = h // tr

    def body(where_ref, t_ref, r_ref, o_ref):
        del where_ref
        o_ref[...] = ((t_ref[...] + r_ref[0].astype(F32)) + r_ref[1].astype(F32)) + r_ref[2].astype(F32)

    return pl.pallas_call(
        body, name=name,
        grid_spec=pltpu.PrefetchScalarGridSpec(
            num_scalar_prefetch=1, grid=(nh,),
            in_specs=[pl.BlockSpec((tr, cols), lambda i, w: (i, 0)), pl.BlockSpec((3, tr, cols), lambda i, w: (0, i, 0))],
            out_specs=pl.BlockSpec((tr, cols), lambda i, w: (w[0] * nh + i, 0))),
        out_shape=SDS((2 * h, cols), F32),
        compiler_params=_params(("parallel",)),
    )(where, own, r)


def _sum_smalls(gathered):
    n = len(gathered)

    def body(*refs):
        for all_ref, o_ref in zip(refs[:n], refs[n:]):
            acc = all_ref[0]
            for dev in range(1, N_DEV):
                acc = acc + all_ref[dev]
            o_ref[...] = acc

    vm = pl.BlockSpec(memory_space=pltpu.VMEM)
    return pl.pallas_call(
        body, name="sum_smalls", out_shape=[SDS(a.shape[1:], F32) for a in gathered],
        in_specs=[vm] * n, out_specs=[vm] * n,
    )(*gathered)


def _adam_step(g, w, m, v):
    nm = ADAM_B1 * m + (1.0 - ADAM_B1) * g
    nv = ADAM_B2 * v + (1.0 - ADAM_B2) * (g * g)
    m_hat = nm / (1.0 - ADAM_B1 ** ADAM_STEP)
    v_hat = nv / (1.0 - ADAM_B2 ** ADAM_STEP)
    return -ADAM_LR * (m_hat / (jnp.sqrt(v_hat) + ADAM_EPS) + ADAM_WD * w), nm, nv


def _adamw(g, w, m, v, name):
    rows, cols = g.shape
    tr = min(rows, 256)

    def body(g_ref, w_ref, m_ref, v_ref, d_ref, nm_ref, nv_ref):
        d_ref[...], nm_ref[...], nv_ref[...] = _adam_step(g_ref[...], w_ref[...], m_ref[...], v_ref[...])

    spec = pl.BlockSpec((tr, cols), lambda i: (i, 0))
    return pl.pallas_call(
        body, name=name, grid=(rows // tr,), in_specs=[spec] * 4, out_specs=[spec] * 3,
        out_shape=[SDS(g.shape, F32)] * 3, compiler_params=_params(("parallel",)),
    )(g, w, m, v)


def _small_update(chip, tot, tot_rel, wmv):
    names = list(SMALL_PLACES)
    n = len(names)

    def body(chip_ref, tot_ref, quarter_ref, rel_ref, *refs):
        del chip_ref
        ins, outs = refs[:3 * n], refs[3 * n:]
        for i, nm in enumerate(names):
            source, row, (rows, cols) = SMALL_PLACES[nm]
            g = {"rows": tot_ref, "quarter": quarter_ref, "rel": rel_ref}[source][row:row + rows, 0:cols]
            outs[4 * i][...] = g
            outs[4 * i + 1][...], outs[4 * i + 2][...], outs[4 * i + 3][...] = _adam_step(
                g, ins[3 * i][...], ins[3 * i + 1][...], ins[3 * i + 2][...])

    whole = lambda shape: pl.BlockSpec(shape, lambda i, c: (0,) * len(shape))
    shapes = [SMALL_PLACES[nm][2] for nm in names]
    outs = pl.pallas_call(
        body, name="small_update",
        grid_spec=pltpu.PrefetchScalarGridSpec(
            num_scalar_prefetch=1, grid=(1,),
            in_specs=[whole(tot.shape), pl.BlockSpec((tot.shape[0], D // 4), lambda i, c: (0, c[0])),
                      whole(tot_rel.shape)] + [whole(shp) for shp in shapes for _ in range(3)],
            out_specs=[whole(shp) for shp in shapes for _ in range(4)]),
        out_shape=[SDS(shp, F32) for shp in shapes for _ in range(4)],
    )(chip, tot, tot, tot_rel, *[a for nm in names for a in wmv[nm]])
    return {nm: tuple(outs[4 * i:4 * i + 4]) for i, nm in enumerate(names)}


def _pad_rows(a, rows):
    return jnp.concatenate([a, jnp.zeros((rows - a.shape[0], a.shape[1]), a.dtype)], axis=0)


def _pad_cols(a, cols):
    return jnp.concatenate([a, jnp.zeros((a.shape[0], cols - a.shape[1]), a.dtype)], axis=1)


def kernel(x, a_pre_norm, a_w_in, a_conv_w, a_w_out, a_post_norm, kv_norm, w_kv, rel_bias, b_pre_norm, b_w_in, b_sinks, b_w_out, b_post_norm, loss_target, m_a_pre_norm, m_a_w_in, m_a_conv_w, m_a_w_out, m_a_post_norm, m_kv_norm, m_w_kv, m_rel_bias, m_b_pre_norm, m_b_w_in, m_b_sinks, m_b_w_out, m_b_post_norm, v_a_pre_norm, v_a_w_in, v_a_conv_w, v_a_w_out, v_a_post_norm, v_kv_norm, v_w_kv, v_rel_bias, v_b_pre_norm, v_b_w_in, v_b_sinks, v_b_w_out, v_b_post_norm):
    seq = x.shape[1]
    xs = x.reshape(seq, D)
    tgt = loss_target.reshape(seq, D)
    chip = 2 * lax.axis_index("x") + lax.axis_index("y")
    core = lax.axis_index("c")
    tm = _tile(seq, 512)
    tmw = _tile(seq, 1024)

    shards = [a_w_in[0], a_w_out[0], w_kv, b_w_in[0], b_w_out[0]]
    small_w = _pad_rows(jnp.concatenate([a_pre_norm, a_conv_w[0], a_post_norm], axis=0), 8)
    *own_only, small_g = _prepare_weights(shards, small_w)
    where = jnp.stack([core, chip]).astype(jnp.int32)
    small_full = small_g.transpose(1, 0, 2).reshape(8, D)
    g_apre, conv_w, g_apost = small_full[0:1], _pad_rows(small_full[1:4], 8), small_full[4:5]
    g_kv = kv_norm.reshape(1, D)

    proj, n1, (win_g, wouta_g, wkv_g, wbin_g, woutb_g) = _a_in(where[1:2], xs, g_apre, own_only, tmw)
    wouta = wouta_g.reshape(D, D)
    wkv = wkv_g.reshape(D, 2 * KV_W)
    woutb = woutb_g.reshape(D, D)
    ya, oa, h1 = _a_mix(proj, xs, conv_w, wouta, g_apost, tm)
    kv, q, zb = _b_in(h1, g_kv, b_pre_norm, wkv, wbin_g, tmw)
    tab = _bias_table(rel_bias, b_sinks.reshape(N_HEADS))
    att, stats = _attn_fwd(q, kv, tab)
    dh2, dqz, datt, loss_acc, dg_bpost, dw_outb = _mid(att, zb, h1, tgt, woutb, b_post_norm, tm)

    dqz, dkv, dtab = _attn_bwd(q, kv, datt, stats, tab, dqz)
    dh1, doa, dg_b, dw_bin, dw_kv = _b_bwd(dqz, dkv, h1, dh2, oa, wbin_g, wkv, g_kv, b_pre_norm, g_apost, tm)
    dw_kv = dw_kv.reshape(N_CHIPS, D // 4, 2 * KV_W)
    dw_outb = dw_outb.reshape(N_CHIPS, D // 4, D)
    grads1 = [dw_kv, dw_bin, dw_outb]
    names1 = ["w_kv", "b_w_in", "b_w_out"]
    from_sibling1, _, _ = _sibling_exchange("to_sibling_1", to_sibling=grads1)
    sums1 = [_add_sibling(where, g, r, "add_sibling_" + nm) for g, r, nm in zip(grads1, from_sibling1, names1)]
    dproj, dconv_w, dw_outa, from_chips1 = _a_bwd(doa, ya, proj, conv_w, wouta, tm, [t for t, _ in sums1])
    shards1 = [_add_chips(where, own, r, "add_chips_" + nm) for (_, own), r, nm in zip(sums1, from_chips1, names1)]
    dw_in = _dw(n1, dproj, D, _tile(seq, 2048), "dw_a_in")
    grads2 = [dw_in, dw_outa.reshape(N_CHIPS, D // 4, D)]
    names2 = ["a_w_in", "a_w_out"]
    from_sibling2, (g_wkv, g_wbin, g_woutb), _ = _sibling_exchange("to_sibling_2", to_sibling=grads2, shards=shards1)
    sums2 = [_add_sibling(where, g, r, "add_sibling_" + nm) for g, r, nm in zip(grads2, from_sibling2, names2)]
    nt = seq // tmw
    dn_first, from_chips2 = _a_in_bwd_matmul(dproj, win_g, tmw, max(nt - max(nt // 4, 1), 1), [t for t, _ in sums2])
    grad_x, dg_apre = _a_in_bwd(dn_first, dproj, xs, dh1, win_g, g_apre, tm)
    shards2 = [_add_chips(where, own, r, "add_chips_" + nm) for (_, own), r, nm in zip(sums2, from_chips2, names2)]
    drel, dsink = _bias_fold(dtab)

    smalls = jnp.concatenate([
        dg_apre[0:1], dg_b[2:3], dg_b[0:1], dg_b[1:2], dg_bpost[0:1], _pad_cols(dsink[0:1], D),
        _pad_cols(loss_acc[0:1], D), jnp.zeros((1, D), F32), dconv_w], axis=0)
    assert smalls.shape == (SMALL_ROWS, D)
    _, (g_win, g_wouta), gathered = _sibling_exchange("share_last", shards=shards2, smalls=(smalls, drel))
    tot, tot_rel = _sum_smalls(gathered)

    big = {}
    for nm, g, w, m, v in [("a_w_in", g_win, a_w_in, m_a_w_in, v_a_w_in), ("a_w_out", g_wouta, a_w_out, m_a_w_out, v_a_w_out),
                           ("w_kv", g_wkv, w_kv, m_w_kv, v_w_kv), ("b_w_in", g_wbin, b_w_in, m_b_w_in, v_b_w_in),
                           ("b_w_out", g_woutb, b_w_out, m_b_w_out, v_b_w_out)]:
        shp = w.shape
        two = (shp[-2], shp[-1])
        d, nm_, nv_ = _adamw(g, w.reshape(two), m.reshape(two), v.reshape(two), "adamw_" + nm)
        big[nm] = (g.reshape(shp), d.reshape(shp), nm_.reshape(shp), nv_.reshape(shp))

    given = {"a_pre_norm": (a_pre_norm, m_a_pre_norm, v_a_pre_norm), "a_conv_w": (a_conv_w, m_a_conv_w, v_a_conv_w),
             "a_post_norm": (a_post_norm, m_a_post_norm, v_a_post_norm), "kv_norm": (kv_norm, m_kv_norm, v_kv_norm),
             "rel_bias": (rel_bias, m_rel_bias, v_rel_bias), "b_pre_norm": (b_pre_norm, m_b_pre_norm, v_b_pre_norm),
             "b_sinks": (b_sinks, m_b_sinks, v_b_sinks), "b_post_norm": (b_post_norm, m_b_post_norm, v_b_post_norm)}
    small = _small_update(where[1:2], tot, tot_rel, {nm: tuple(a.reshape(SMALL_PLACES[nm][2]) for a in wmv)
                                            for nm, wmv in given.items()})
    order = ["a_pre_norm", "a_w_in", "a_conv_w", "a_w_out", "a_post_norm", "kv_norm", "w_kv", "rel_bias",
             "b_pre_norm", "b_w_in", "b_sinks", "b_w_out", "b_post_norm"]
    outs = []
    for which in range(4):
        for nm in order:
            outs.append(big[nm][which] if nm in big else small[nm][which].reshape(given[nm][0].shape))
    loss = 0.5 * tot[LOSS_ROW, 0]
    return (loss, grad_x.reshape(x.shape), *outs)
```

```python
import math

import jax
import jax.numpy as jnp
from jax import lax
from jax.experimental import pallas as pl
from jax.experimental.pallas import tpu as pltpu

F32 = jnp.float32
BF16 = jnp.bfloat16
MESH = pl.DeviceIdType.MESH
SDS = jax.ShapeDtypeStruct

D = 1024
HEAD_DIM = 64
N_HEADS = 16
N_KV = 2
GROUP = 8
KV_W = 128
BLK = 128
N_BUCKETS = 32
MAX_EXACT = 16
MAX_DISTANCE = 128
EPS = 1e-6
NEG_INF = -1e30
Q_SCALE = HEAD_DIM ** -0.5

ADAM_LR = 0.001
ADAM_B1 = 0.9
ADAM_B2 = 0.999
ADAM_EPS = 1e-08
ADAM_WD = 0.01
ADAM_STEP = 10

N_CHIPS = 4
N_DEV = 8
BIN_COLS = 2 * D // N_CHIPS
VMEM_LIMIT = 56 * 1024 * 1024
SMALL_ROWS = 16
LOSS_ROW = 6
SMALL_PLACES = {
    "a_pre_norm": ("quarter", 0, (1, D // 4)), "a_conv_w": ("quarter", 8, (3, D // 4)),
    "a_post_norm": ("quarter", 1, (1, D // 4)), "kv_norm": ("rows", 2, (1, D)),
    "rel_bias": ("rel", 0, (N_BUCKETS, N_HEADS)), "b_pre_norm": ("rows", 3, (1, D)),
    "b_sinks": ("rows", 5, (1, N_HEADS)), "b_post_norm": ("rows", 4, (1, D)),
}
HALO = 16


def _bucket_thresholds():
    def bucket(d):
        big = MAX_EXACT + int(math.log(d / MAX_EXACT) / math.log(MAX_DISTANCE / MAX_EXACT)
                              * (N_BUCKETS - MAX_EXACT))
        return d if d < MAX_EXACT else min(big, N_BUCKETS - 1)
    out = []
    for b in range(MAX_EXACT + 1, N_BUCKETS):
        out.append(min(d for d in range(MAX_EXACT, MAX_DISTANCE) if bucket(d) >= b))
    return tuple(out)


BUCKET_THRESHOLDS = _bucket_thresholds()


def _params(semantics=None, vmem=VMEM_LIMIT):
    return pltpu.CompilerParams(dimension_semantics=semantics, vmem_limit_bytes=vmem)


def _tile(n, pref):
    return pref if n >= 2 * pref else max(n // 2, 8)


def _rms_scale(v):
    return lax.rsqrt(jnp.mean(v * v, axis=-1, keepdims=True) + EPS)


def _nt(a, b):
    return lax.dot_general(a, b, (((1,), (1,)), ((), ())), preferred_element_type=F32)


def _tn(a, b):
    return lax.dot_general(a, b, (((0,), (0,)), ((), ())), preferred_element_type=F32)


def _nn(a, b):
    return jnp.dot(a, b, preferred_element_type=F32)


def _silu_parts(z):
    sg = jax.nn.sigmoid(z)
    return sg, z * sg


def _dsilu(z, sg):
    return sg * (1.0 + z * (1.0 - sg))


def _write_gradient(acc, out32, out16, stage):
    pltpu.sync_copy(acc, out32)
    rows = stage.shape[0]
    for k in range(acc.shape[0] // rows):
        stage[...] = acc[rows * k:rows * (k + 1), :].astype(BF16)
        pltpu.sync_copy(stage, out16.at[pl.ds(rows * k, rows)])


def _acc_row(ref, row, val):
    ref[row:row + 1, :] += val


def _gather_copies(outs, splits, ici_send, ici_recv, d2d_send, d2d_recv):
    x, y, c = lax.axis_index("x"), lax.axis_index("y"), lax.axis_index("c")
    k = 2 * x + y
    sibling = (x, y, 1 - c)

    def part(o_ref, chip, core, split):
        if not split:
            return o_ref.at[chip]
        h = o_ref.shape[1] // 2
        return o_ref.at[chip, pl.ds(pl.multiple_of(core * h, 16), h)]

    def remote(ref, a, j, sems, to):
        return pltpu.make_async_remote_copy(src_ref=ref, dst_ref=ref, send_sem=sems[0].at[3 * a + j],
                                            recv_sem=sems[1].at[3 * a + j], device_id=to, device_id_type=MESH)

    copies = []
    for a, (o_ref, split) in enumerate(zip(outs, splits)):
        for j, (px, py) in enumerate([(x, 1 - y), (1 - x, y), (1 - x, 1 - y)]):
            kj = 2 * px + py
            ici, d2d = (ici_send, ici_recv), (d2d_send, d2d_recv)
            copies.append((remote(part(o_ref, k, c, split), a, j, ici, (px, py, c)),
                           remote(part(o_ref, kj, c, split), a, j, ici, (px, py, c)),
                           remote(part(o_ref, kj, c, split), a, j, d2d, sibling) if split else None,
                           remote(part(o_ref, kj, 1 - c, split), a, j, d2d, sibling) if split else None))
    return copies


def _gather_sems(n):
    return [pltpu.SemaphoreType.DMA((3 * n,)) for _ in range(4)]


def _prepare_weights(shards, small):
    n = len(shards)

    def body(*refs):
        ins, small_in = refs[:n], refs[n]
        outs, small_out = refs[n + 1:2 * n + 1], refs[2 * n + 1]
        stages, put_sem = refs[2 * n + 2:3 * n + 2], refs[3 * n + 2]
        sems = refs[3 * n + 3:]
        k = 2 * lax.axis_index("x") + lax.axis_index("y")
        puts = []
        for a, (i_ref, stage, o_ref) in enumerate(zip(ins, stages, outs)):
            stage[...] = i_ref[...].astype(BF16)
            puts.append(pltpu.make_async_copy(stage, o_ref.at[k], put_sem.at[a]))
            puts[-1].start()
        small_out[k] = small_in[...]
        copies = _gather_copies([small_out], [False], *sems)
        for send, _, _, _ in copies:
            send.start()
        for _, arrival, _, _ in copies:
            arrival.wait_recv()
        for send, _, _, _ in copies:
            send.wait_send()
        for put in puts:
            put.wait()

    vm = pl.BlockSpec(memory_space=pltpu.VMEM)
    anyspace = pl.BlockSpec(memory_space=pl.ANY)
    out_shape = [SDS((N_CHIPS,) + s.shape, BF16) for s in shards] + [SDS((N_CHIPS,) + small.shape, F32)]
    return pl.pallas_call(
        body, name="prepare_weights", out_shape=out_shape,
        in_specs=[vm] * (n + 1), out_specs=[anyspace] * n + [vm],
        scratch_shapes=[pltpu.VMEM(s.shape, BF16) for s in shards] + [pltpu.SemaphoreType.DMA((n,))] + _gather_sems(1),
        compiler_params=pltpu.CompilerParams(vmem_limit_bytes=VMEM_LIMIT),
    )(*shards, small)


def _a_in(chip, x, g_pre, weights, tm):
    s = x.shape[0]
    nt = s // tm
    n = len(weights)

    def body(chip_ref, x_ref, g_ref, *refs):
        proj_ref, n1_ref = refs[n:n + 2]
        gathered = refs[n + 2:2 * n + 2]
        wbuf, n1_all, fetch_sem = refs[2 * n + 2:2 * n + 5]
        sems = refs[2 * n + 5:]
        jj, i = pl.program_id(0), pl.program_id(1)
        copies = _gather_copies(gathered, [True] * n, *sems)

        def fetch(rel):
            slot = jnp.bitwise_xor(chip_ref[0], rel)
            return pltpu.make_async_copy(gathered[0].at[slot], wbuf.at[rel % 2], fetch_sem.at[rel % 2])

        @pl.when((jj == 0) & (i == 0))
        def _():
            fetch(0).start()
            copies[0][0].start()
            copies[1][0].start()
            fetch(0).wait()

        for rel in (1, 2, 3):
            @pl.when((jj == rel) & (i == 0))
            def _():
                fetch(rel).wait()

        @pl.when(jj == 0)
        def _():
            xv = x_ref[...]
            n1 = (xv * _rms_scale(xv) * g_ref[...]).astype(BF16)
            n1_ref[...] = n1
            n1_all[i] = n1
        proj_ref[...] = _nn(n1_all[i], wbuf[jj % 2]).astype(BF16)

        for rel in (1, 2, 3):
            @pl.when((jj == rel - 1) & (i == max(nt - 3, nt // 2)))
            def _():
                _, arrival, forward, forwarded = copies[rel - 1]
                arrival.wait_recv()
                forward.start()
                forwarded.wait_recv()
                fetch(rel).start()
                if rel == 1:
                    copies[2][0].start()
                if rel == 2:
                    for send, _, _, _ in copies[3:]:
                        send.start()

        @pl.when((jj == 3) & (i == max(nt - 2, 0)))
        def _():
            for _, arrival, forward, _ in copies[3:]:
                arrival.wait_recv()
                forward.start()

        @pl.when((jj == 3) & (i == nt - 1))
        def _():
            for _, _, _, forwarded in copies[3:]:
                forwarded.wait_recv()
            for send, _, forward, _ in copies:
                forward.wait_send()
                send.wait_send()

    anyspace = pl.BlockSpec(memory_space=pl.ANY)
    proj, n1, *gathered = pl.pallas_call(
        body, name="a_in",
        grid_spec=pltpu.PrefetchScalarGridSpec(
            num_scalar_prefetch=1, grid=(4, nt),
            in_specs=[pl.BlockSpec((tm, D), lambda jj, i, c: (jnp.where(jj == 0, i, nt - 1), 0)),
                      pl.BlockSpec((1, D), lambda jj, i, c: (0, 0))] + [anyspace] * n,
            out_specs=[pl.BlockSpec((tm, D), lambda jj, i, c: (i, jnp.bitwise_xor(c[0], jj))),
                       pl.BlockSpec((tm, D), lambda jj, i, c: (jnp.where(jj == 0, i, nt - 1), 0))] + [anyspace] * n,
            scratch_shapes=[pltpu.VMEM((2, D, D), BF16), pltpu.VMEM((nt, tm, D), BF16),
                            pltpu.SemaphoreType.DMA((2,))] + _gather_sems(n)),
        out_shape=[SDS((s, 4 * D), BF16), SDS((s, D), BF16)] + [SDS(w.shape, w.dtype) for w in weights],
        input_output_aliases={3 + a: 2 + a for a in range(n)},
        compiler_params=_params(("arbitrary", "arbitrary")),
    )(chip, x, g_pre, *weights)
    return proj, n1, gathered


def _shift_rows(v, last, second_last, rows):
    v1 = jnp.where(rows >= 1, pltpu.roll(v, 1, 0), last)
    v2 = jnp.where(rows >= 2, pltpu.roll(v, 2, 0), jnp.where(rows == 1, last, second_last))
    return v1, v2


def _a_mix(proj, x, conv_w, w_out, g_post, tm):
    s = x.shape[0]

    def body(proj_ref, x_ref, cw_ref, w_ref, g_ref, ya_ref, oa_ref, h1_ref, carry):
        @pl.when(pl.program_id(0) == 0)
        def _():
            carry[...] = jnp.zeros_like(carry)
        v = proj_ref[:, D:2 * D].astype(F32) * proj_ref[:, 2 * D:3 * D].astype(F32)
        rows = lax.broadcasted_iota(jnp.int32, (tm, D), 0)
        before = carry[...]
        v1, v2 = _shift_rows(v, before[7:8, :], before[6:7, :], rows)
        carry[...] = v[tm - 8:tm, :]
        conv = cw_ref[0:1, :] * v2 + cw_ref[1:2, :] * v1 + cw_ref[2:3, :] * v
        _, sz = _silu_parts(proj_ref[:, 3 * D:4 * D].astype(F32))
        ya = (proj_ref[:, 0:D].astype(F32) * conv * sz).astype(BF16)
        ya_ref[...] = ya
        oa = _nn(ya, w_ref[...])
        oa_ref[...] = oa.astype(BF16)
        h1_ref[...] = x_ref[...] + oa * _rms_scale(oa) * g_ref[...]

    row = lambda i: (i, 0)
    fix = lambda i: (0, 0)
    return pl.pallas_call(
        body, name="a_mix", grid=(s // tm,),
        in_specs=[pl.BlockSpec((tm, 4 * D), row), pl.BlockSpec((tm, D), row), pl.BlockSpec((8, D), fix),
                  pl.BlockSpec((D, D), fix), pl.BlockSpec((1, D), fix)],
        out_specs=[pl.BlockSpec((tm, D), row)] * 3,
        out_shape=[SDS((s, D), BF16), SDS((s, D), BF16), SDS((s, D), F32)],
        scratch_shapes=[pltpu.VMEM((8, D), F32)],
        compiler_params=_params(("arbitrary",)),
    )(proj, x, conv_w, w_out, g_post)


def _b_in(h1, g_kv, g_pre, w_kv, wbin_g, tm):
    s = h1.shape[0]

    def body(h_ref, gk_ref, gb_ref, wkv_ref, wb_ref, kv_ref, q_ref, z_ref):
        h = h_ref[...]
        hh = h * _rms_scale(h)
        nk = (hh * gk_ref[...]).astype(BF16)
        nb = (hh * gb_ref[...]).astype(BF16)
        kv_ref[...] = _nn(nk, wkv_ref[...]).astype(BF16)
        for j in range(2):
            q_ref[:, BIN_COLS * j:BIN_COLS * (j + 1)] = (_nn(nb, wb_ref[j]) * Q_SCALE).astype(BF16)
            z_ref[:, BIN_COLS * j:BIN_COLS * (j + 1)] = _nn(nb, wb_ref[2 + j]).astype(BF16)

    row = lambda i: (i, 0)
    fix = lambda i: (0, 0)
    return pl.pallas_call(
        body, name="b_in", grid=(s // tm,),
        in_specs=[pl.BlockSpec((tm, D), row), pl.BlockSpec((1, D), fix), pl.BlockSpec((1, D), fix),
                  pl.BlockSpec((D, 2 * KV_W), fix), pl.BlockSpec((N_CHIPS, D, BIN_COLS), lambda i: (0, 0, 0))],
        out_specs=[pl.BlockSpec((tm, 2 * KV_W), row), pl.BlockSpec((tm, D), row), pl.BlockSpec((tm, D), row)],
        out_shape=[SDS((s, 2 * KV_W), BF16), SDS((s, D), BF16), SDS((s, D), BF16)],
        compiler_params=_params(("parallel",)),
    )(h1, g_kv, g_pre, w_kv, wbin_g)


def _band_buckets():
    q = lax.broadcasted_iota(jnp.int32, (BLK, 2 * BLK), 0)
    k = lax.broadcasted_iota(jnp.int32, (BLK, 2 * BLK), 1)
    dist = q + BLK - k
    bucket = jnp.where(dist < MAX_EXACT, dist, MAX_EXACT)
    for t in BUCKET_THRESHOLDS:
        bucket = bucket + jnp.where(dist >= t, 1, 0)
    in_window = (dist >= 0) & (dist < BLK)
    return jnp.where(in_window, bucket, -1)


def _head_place(h):
    kh, j, e = h // GROUP, (h % GROUP) // 2, h % 2
    return kh, slice(BLK * j, BLK * (j + 1)), slice(2 * BLK * e, 2 * BLK * (e + 1))


def _bias_table(rel_bias, sinks):
    def body(rb_ref, sink_ref, tab_ref):
        bucket = _band_buckets()
        col = lax.broadcasted_iota(jnp.int32, (BLK, 2 * BLK), 1)
        for h in range(N_HEADS):
            acc = jnp.where(bucket < 0, NEG_INF, 0.0).astype(F32)
            for b in range(N_BUCKETS):
                acc = jnp.where(bucket == b, rb_ref[b, h], acc)
            acc = jnp.where(col == 0, sink_ref[h], acc)
            kh, rows, cols = _head_place(h)
            tab_ref[1, kh, rows, cols] = acc
            tab_ref[0, kh, rows, cols] = jnp.where((col > 0) & (col < BLK), NEG_INF, acc)

    return pl.pallas_call(
        body, name="bias_table", out_shape=SDS((2, N_KV, 4 * BLK, 4 * BLK), F32),
        in_specs=[pl.BlockSpec(memory_space=pltpu.SMEM), pl.BlockSpec(memory_space=pltpu.SMEM)],
        out_specs=pl.BlockSpec(memory_space=pltpu.VMEM),
    )(rel_bias, sinks)


def _bias_fold(dtab):
    def body(dtab_ref, out_ref, dsink_ref):
        bucket = _band_buckets()
        row = lax.broadcasted_iota(jnp.int32, (N_BUCKETS, 128), 0)
        lane = lax.broadcasted_iota(jnp.int32, (N_BUCKETS, 128), 1)
        row8 = lax.broadcasted_iota(jnp.int32, (8, 128), 0)
        lane8 = lax.broadcasted_iota(jnp.int32, (8, 128), 1)
        acc = jnp.zeros((N_BUCKETS, 128), F32)
        dsink = jnp.zeros((8, 128), F32)
        for h in range(N_HEADS):
            kh, rows, cols = _head_place(h)
            dt = dtab_ref[kh, rows, cols]
            for b in range(N_BUCKETS):
                val = jnp.sum(jnp.where(bucket == b, dt, 0.0))
                acc = acc + jnp.where((row == b) & (lane == h), val, 0.0)
            dsink = dsink + jnp.where((row8 == 0) & (lane8 == h), jnp.sum(dt[:, 0:1]), 0.0)
        out_ref[...] = acc
        dsink_ref[...] = dsink

    vm = pl.BlockSpec(memory_space=pltpu.VMEM)
    return pl.pallas_call(
        body, name="bias_fold", out_shape=[SDS((N_BUCKETS, 128), F32), SDS((8, 128), F32)],
        in_specs=[vm], out_specs=[vm, vm],
    )(dtab)


def _pair_operands(prev, cur):
    t = jnp.concatenate([prev, cur], axis=0).astype(F32)
    t = jnp.where(lax.broadcasted_iota(jnp.int32, t.shape, 0) == 0, 0.0, t)
    tr = pltpu.roll(t, HEAD_DIM, 1)
    lo = lax.broadcasted_iota(jnp.int32, t.shape, 1) < HEAD_DIM
    zero = jnp.zeros_like(t)
    head0 = jnp.concatenate([jnp.where(lo, t, zero), jnp.where(lo, zero, tr)], axis=0).astype(BF16)
    head1 = jnp.concatenate([jnp.where(lo, tr, zero), jnp.where(lo, zero, t)], axis=0).astype(BF16)
    return head0, head1


def _pair_fold(d0, d1):
    lo = lax.broadcasted_iota(jnp.int32, (2 * BLK, KV_W), 1) < HEAD_DIM
    zero = jnp.zeros((2 * BLK, KV_W), F32)
    g0 = jnp.where(lo, d0[0:256], zero) + pltpu.roll(jnp.where(lo, zero, d0[256:512]), HEAD_DIM, 1)
    g1 = pltpu.roll(jnp.where(lo, d1[0:256], zero), HEAD_DIM, 1) + jnp.where(lo, zero, d1[256:512])
    return jnp.where(lax.broadcasted_iota(jnp.int32, (2 * BLK, KV_W), 0) == 0, 0.0, g0 + g1)


def _stack_pairs(ref, kh, rows=slice(None)):
    return jnp.concatenate([ref[rows, 128 * (4 * kh + j):128 * (4 * kh + j + 1)] for j in range(4)], axis=0)


def _table_spec():
    return pl.BlockSpec((1, N_KV, 4 * BLK, 4 * BLK), lambda n: (jnp.minimum(n, 1), 0, 0, 0))


def _attn_fwd(q, kv, tab):
    s = q.shape[0]

    def body(q_ref, kp_ref, k0_ref, k1_ref, vp_ref, v0_ref, v1_ref, tab0_ref, tab1_ref, att_ref, stats_ref):
        lane = lax.broadcasted_iota(jnp.int32, (BLK, 128), 1)
        for sub, (kp, kc, vp, vc, tab_ref) in enumerate([(kp_ref, k0_ref, vp_ref, v0_ref, tab0_ref),
                                                         (k0_ref, k1_ref, v0_ref, v1_ref, tab1_ref)]):
            rows = slice(BLK * sub, BLK * (sub + 1))
            k2 = _pair_operands(kp[...], kc[...])
            v2 = _pair_operands(vp[...], vc[...])
            stats = jnp.zeros((BLK, 128), F32)
            for kh in range(N_KV):
                sc = _nt(_stack_pairs(q_ref, kh, rows), k2[kh])
                ps = []
                for e in range(2):
                    lg = sc[:, 256 * e:256 * (e + 1)] + tab_ref[0, kh, :, 256 * e:256 * (e + 1)]
                    m = jnp.max(lg, axis=-1, keepdims=True)
                    ex = jnp.exp(lg - m)
                    den = jnp.sum(ex, axis=-1, keepdims=True)
                    ps.append(ex * (1.0 / den))
                    lse = m + jnp.log(den)
                    for j in range(4):
                        stats = jnp.where(lane == GROUP * kh + 2 * j + e, lse[BLK * j:BLK * (j + 1)], stats)
                out = _nn(jnp.concatenate(ps, axis=1).astype(BF16), v2[kh])
                for j in range(4):
                    att_ref[rows, 128 * (4 * kh + j):128 * (4 * kh + j + 1)] = out[BLK * j:BLK * (j + 1)].astype(BF16)
            stats_ref[rows, :] = stats

    two = lambda m: (m, 0)
    table = lambda pick: pl.BlockSpec((1, N_KV, 4 * BLK, 4 * BLK), lambda m: (pick(m), 0, 0, 0))
    return pl.pallas_call(
        body, name="attn_fwd", grid=(s // (2 * BLK),),
        in_specs=[pl.BlockSpec((2 * BLK, D), two)]
        + [pl.BlockSpec((BLK, KV_W), lambda m, col=col, off=off: (jnp.maximum(2 * m + off, 0), col))
           for col in (0, 1) for off in (-1, 0, 1)]
        + [table(lambda m: jnp.minimum(m, 1)), table(lambda m: 1)],
        out_specs=[pl.BlockSpec((2 * BLK, D), two), pl.BlockSpec((2 * BLK, 128), two)],
        out_shape=[SDS((s, D), BF16), SDS((s, 128), F32)],
        compiler_params=_params(("parallel",)),
    )(q, kv, kv, kv, kv, kv, kv, tab, tab)


def _mid(att, zb, h1, tgt, w_out, g_post, tm):
    s = att.shape[0]
    nt = s // tm

    def body(att_ref, z_ref, h1_ref, t_ref, w_ref, g_ref,
             dh_ref, dqz_ref, datt_ref, loss_ref, dg_ref, dw_ref, dw16_ref, dw_acc, stage):
        @pl.when(pl.program_id(0) == 0)
        def _():
            loss_ref[...] = jnp.zeros_like(loss_ref)
            dg_ref[...] = jnp.zeros_like(dg_ref)
            dw_acc[...] = jnp.zeros_like(dw_acc)
        att = att_ref[...].astype(F32)
        z = z_ref[...].astype(F32)
        sg, sz = _silu_parts(z)
        ob = (att * sz).astype(BF16)
        y2 = _nn(ob, w_ref[...])
        r2 = _rms_scale(y2)
        yh = y2 * r2
        g = g_ref[...]
        err = (h1_ref[...] + yh * g) - t_ref[...]
        loss_ref[...] += jnp.sum(jnp.sum(err * err, axis=-1, keepdims=True) / D)
        dh = err / D
        dh_ref[...] = dh
        _acc_row(dg_ref, 0, jnp.sum(dh * yh, axis=0, keepdims=True))
        dyh = dh * g
        dy = (r2 * (dyh - yh * jnp.mean(dyh * yh, axis=-1, keepdims=True))).astype(BF16)
        dw_acc[...] += _tn(ob, dy)
        dob = _nt(dy, w_ref[...])
        datt_ref[...] = (dob * sz).astype(BF16)
        dqz_ref[...] = (dob * att * _dsilu(z, sg)).astype(BF16)

        @pl.when(pl.program_id(0) == nt - 1)
        def _():
            _write_gradient(dw_acc, dw_ref, dw16_ref, stage)

    row = lambda i: (i, 0)
    fix = lambda i: (0, 0)
    anyspace = pl.BlockSpec(memory_space=pl.ANY)
    return pl.pallas_call(
        body, name="mid", grid=(nt,),
        in_specs=[pl.BlockSpec((tm, D), row)] * 4 + [pl.BlockSpec((D, D), fix), pl.BlockSpec((1, D), fix)],
        out_specs=[pl.BlockSpec((tm, D), row), pl.BlockSpec((tm, D), lambda i: (i, 1)), pl.BlockSpec((tm, D), row),
                   pl.BlockSpec((8, 128), fix), pl.BlockSpec((8, D), fix), anyspace, anyspace],
        out_shape=[SDS((s, D), F32), SDS((s, 2 * D), BF16), SDS((s, D), BF16), SDS((8, 128), F32),
                   SDS((8, D), F32), SDS((D, D), F32), SDS((D, D), BF16)],
        scratch_shapes=[pltpu.VMEM((D, D), F32), pltpu.VMEM((D // 4, D), BF16)],
        compiler_params=_params(("arbitrary",)),
    )(att, zb, h1, tgt, w_out, g_post)


def _attn_bwd(q, kv, datt, stats, tab, dqz):
    s = q.shape[0]
    nb = s // BLK

    def body(q_ref, kp_ref, kc_ref, vp_ref, vc_ref, da_ref, st_ref, tab_ref, dqz_in,
             dq_ref, dkv_ref, dtab_ref, dk_carry, dv_carry):
        del dqz_in
        n = pl.program_id(0)

        @pl.when(n == 0)
        def _():
            dtab_ref[...] = jnp.zeros_like(dtab_ref)
            dk_carry[...] = jnp.zeros_like(dk_carry)
            dv_carry[...] = jnp.zeros_like(dv_carry)

        @pl.when(n < nb)
        def _():
            k2 = _pair_operands(kp_ref[...], kc_ref[...])
            v2 = _pair_operands(vp_ref[...], vc_ref[...])
            lane = lax.broadcasted_iota(jnp.int32, (BLK, 128), 1)
            stats = st_ref[...]
            dk2, dv2 = [], []
            for kh in range(N_KV):
                qs = _stack_pairs(q_ref, kh)
                das = _stack_pairs(da_ref, kh)
                sc = _nt(qs, k2[kh])
                dp = _nt(das, v2[kh])
                ps, dss = [], []
                for e in range(2):
                    heads = [GROUP * kh + 2 * j + e for j in range(4)]
                    lse = jnp.concatenate([jnp.sum(jnp.where(lane == h, stats, 0.0), axis=-1, keepdims=True)
                                           for h in heads], axis=0)
                    cols = slice(256 * e, 256 * (e + 1))
                    p = jnp.exp(sc[:, cols] + tab_ref[0, kh, :, cols] - lse)
                    delta = jnp.sum(p * dp[:, cols], axis=-1, keepdims=True)
                    ds = p * (dp[:, cols] - delta)
                    dtab_ref[kh, :, cols] += ds
                    ps.append(p)
                    dss.append(ds)
                p2 = jnp.concatenate(ps, axis=1).astype(BF16)
                ds2 = jnp.concatenate(dss, axis=1).astype(BF16)
                dq = _nn(ds2, k2[kh]) * Q_SCALE
                for j in range(4):
                    dq_ref[:, 128 * (4 * kh + j):128 * (4 * kh + j + 1)] = dq[BLK * j:BLK * (j + 1)].astype(BF16)
                dk2.append(_tn(ds2, qs))
                dv2.append(_tn(p2, das))
            dkk = _pair_fold(dk2[0], dk2[1])
            dvv = _pair_fold(dv2[0], dv2[1])
            dkv_ref[:, 0:KV_W] = (dk_carry[...] + dkk[0:BLK]).astype(BF16)
            dkv_ref[:, KV_W:2 * KV_W] = (dv_carry[...] + dvv[0:BLK]).astype(BF16)
            dk_carry[...] = dkk[BLK:2 * BLK]
            dv_carry[...] = dvv[BLK:2 * BLK]

        @pl.when(n == nb)
        def _():
            dkv_ref[:, 0:KV_W] = dk_carry[...].astype(BF16)
            dkv_ref[:, KV_W:2 * KV_W] = dv_carry[...].astype(BF16)

    cur = lambda n: (jnp.minimum(n, nb - 1), 0)
    prev = lambda n: (jnp.clip(n - 1, 0, nb - 1), 0)
    return pl.pallas_call(
        body, name="attn_bwd", grid=(nb + 1,),
        in_specs=[pl.BlockSpec((BLK, D), cur),
                  pl.BlockSpec((BLK, KV_W), prev), pl.BlockSpec((BLK, KV_W), cur),
                  pl.BlockSpec((BLK, KV_W), lambda n: (jnp.clip(n - 1, 0, nb - 1), 1)),
                  pl.BlockSpec((BLK, KV_W), lambda n: (jnp.minimum(n, nb - 1), 1)),
                  pl.BlockSpec((BLK, D), cur), pl.BlockSpec((BLK, 128), cur), _table_spec(),
                  pl.BlockSpec(memory_space=pl.ANY)],
        out_specs=[pl.BlockSpec((BLK, D), cur), pl.BlockSpec((BLK, 2 * KV_W), prev),
                   pl.BlockSpec((N_KV, 4 * BLK, 4 * BLK), lambda n: (0, 0, 0))],
        out_shape=[SDS((s, 2 * D), BF16), SDS((s, 2 * KV_W), BF16), SDS((N_KV, 4 * BLK, 4 * BLK), F32)],
        scratch_shapes=[pltpu.VMEM((BLK, KV_W), F32), pltpu.VMEM((BLK, KV_W), F32)],
        input_output_aliases={8: 0},
        compiler_params=_params(("arbitrary",)),
    )(q, kv, kv, kv, kv, datt, stats, tab, dqz)


def _b_bwd(dqz, dkv, h1, dh2, oa, wbin_g, w_kv, g_kv, g_pre, g_apost, tm):
    s = h1.shape[0]
    nt = s // tm

    def body(dqz_ref, dkv_ref, h_ref, dh2_ref, oa_ref, wb_ref, wkv_ref, gk_ref, gb_ref, ga_ref,
             dh1_ref, doa_ref, dg_ref, dwb_ref, dwkv_ref, dwb16_ref, dwkv16_ref, wcat, dwb_acc, dwkv_acc):
        @pl.when(pl.program_id(0) == 0)
        def _():
            dg_ref[...] = jnp.zeros_like(dg_ref)
            dwb_acc[...] = jnp.zeros_like(dwb_acc)
            dwkv_acc[...] = jnp.zeros_like(dwkv_acc)
            for j in range(N_CHIPS):
                pltpu.sync_copy(wb_ref.at[j], wcat.at[:, pl.ds(BIN_COLS * j, BIN_COLS)])
        dnb = _nt(dqz_ref[...], wcat[...])
        dnk = _nt(dkv_ref[...], wkv_ref[...])
        h = h_ref[...]
        r = _rms_scale(h)
        hh = h * r
        dwb_acc[...] += _tn((hh * gb_ref[...]).astype(BF16), dqz_ref[...])
        dwkv_acc[...] += _tn((hh * gk_ref[...]).astype(BF16), dkv_ref[...])
        _acc_row(dg_ref, 0, jnp.sum(dnk * hh, axis=0, keepdims=True))
        _acc_row(dg_ref, 1, jnp.sum(dnb * hh, axis=0, keepdims=True))
        dhh = dnb * gb_ref[...] + dnk * gk_ref[...]
        dh1 = dh2_ref[...] + r * (dhh - hh * jnp.mean(dhh * hh, axis=-1, keepdims=True))
        dh1_ref[...] = dh1
        oa = oa_ref[...].astype(F32)
        ra = _rms_scale(oa)
        oh = oa * ra
        _acc_row(dg_ref, 2, jnp.sum(dh1 * oh, axis=0, keepdims=True))
        doh = dh1 * ga_ref[...]
        doa_ref[...] = (ra * (doh - oh * jnp.mean(doh * oh, axis=-1, keepdims=True))).astype(BF16)

        @pl.when(pl.program_id(0) == nt - 1)
        def _():
            wcat[...] = dwb_acc[...].astype(BF16)
            for j in range(N_CHIPS):
                pltpu.sync_copy(dwb_acc.at[:, pl.ds(BIN_COLS * j, BIN_COLS)], dwb_ref.at[j])
                pltpu.sync_copy(wcat.at[:, pl.ds(BIN_COLS * j, BIN_COLS)], dwb16_ref.at[j])
            pltpu.sync_copy(dwkv_acc, dwkv_ref)
            wcat[:, 0:2 * KV_W] = dwkv_acc[...].astype(BF16)
            pltpu.sync_copy(wcat.at[:, pl.ds(0, 2 * KV_W)], dwkv16_ref)

    row = lambda i: (i, 0)
    fix = lambda i: (0, 0)
    anyspace = pl.BlockSpec(memory_space=pl.ANY)
    return pl.pallas_call(
        body, name="b_bwd", grid=(nt,),
        in_specs=[pl.BlockSpec((tm, 2 * D), row), pl.BlockSpec((tm, 2 * KV_W), row), pl.BlockSpec((tm, D), row),
                  pl.BlockSpec((tm, D), row), pl.BlockSpec((tm, D), row), anyspace, pl.BlockSpec((D, 2 * KV_W), fix),
                  pl.BlockSpec((1, D), fix), pl.BlockSpec((1, D), fix), pl.BlockSpec((1, D), fix)],
        out_specs=[pl.BlockSpec((tm, D), row), pl.BlockSpec((tm, D), row), pl.BlockSpec((8, D), fix)] + [anyspace] * 4,
        out_shape=[SDS((s, D), F32), SDS((s, D), BF16), SDS((8, D), F32), SDS((N_CHIPS, D, BIN_COLS), F32),
                   SDS((D, 2 * KV_W), F32), SDS((N_CHIPS, D, BIN_COLS), BF16), SDS((D, 2 * KV_W), BF16)],
        scratch_shapes=[pltpu.VMEM((D, 2 * D), BF16), pltpu.VMEM((D, 2 * D), F32), pltpu.VMEM((D, 2 * KV_W), F32)],
        compiler_params=_params(("arbitrary",)),
    )(dqz, dkv, h1, dh2, oa, wbin_g, w_kv, g_kv, g_pre, g_apost)


def _chip_exchange(parts, recvs, send, recv):
    x, y, c = lax.axis_index("x"), lax.axis_index("y"), lax.axis_index("c")
    chips = [(x, 1 - y), (1 - x, y), (1 - x, 1 - y)]
    copies = []
    for a, (t, r) in enumerate(zip(parts, recvs)):
        for j, (px, py) in enumerate(chips):
            copies.append(pltpu.make_async_remote_copy(
                src_ref=t.at[2 * px + py], dst_ref=r.at[j], send_sem=send.at[3 * a + j],
                recv_sem=recv.at[3 * a + j], device_id=(px, py, c), device_id_type=MESH))
    return copies


def _exchange_specs(parts):
    anyspace = pl.BlockSpec(memory_space=pl.ANY)
    n = len(parts)
    return ([anyspace] * n, [anyspace] * n, [SDS((3,) + t.shape[1:], t.dtype) for t in parts],
            [pltpu.SemaphoreType.DMA((3 * n,)), pltpu.SemaphoreType.DMA((3 * n,))])


def _device_exchange(grads, recvs, send, recv):
    x, y, c = lax.axis_index("x"), lax.axis_index("y"), lax.axis_index("c")
    copies = []
    for a, (g, r) in enumerate(zip(grads, recvs)):
        h = g.shape[1] // 2
        for rel in range(1, N_DEV):
            fx, fy, fc = rel >> 2, (rel >> 1) & 1, rel & 1
            px, py, pc = x + fx - 2 * x * fx, y + fy - 2 * y * fy, c + fc - 2 * c * fc
            sem = (N_DEV - 1) * a + rel - 1
            copies.append(pltpu.make_async_remote_copy(
                src_ref=g.at[2 * px + py, pl.ds(pl.multiple_of(pc * h, 16), h)], dst_ref=r.at[rel - 1],
                send_sem=send.at[sem], recv_sem=recv.at[sem], device_id=(px, py, pc), device_id_type=MESH))
    return copies


def _device_exchange_specs(grads):
    anyspace = pl.BlockSpec(memory_space=pl.ANY)
    n = len(grads)
    count = (N_DEV - 1) * n
    return ([anyspace] * n, [anyspace] * n,
            [SDS((N_DEV - 1, g.shape[1] // 2, g.shape[2]), g.dtype) for g in grads],
            [pltpu.SemaphoreType.DMA((count,)), pltpu.SemaphoreType.DMA((count,))])


def _a_bwd(doa, ya, proj, conv_w, w_out, tm, parts):
    s = doa.shape[0]
    nt = s // tm
    n = len(parts)
    ex_in, ex_out, ex_shape, ex_sems = _device_exchange_specs(parts)

    def body(*refs):
        doa_ref, ya_ref, proj_ref, halo_ref, cw_ref, w_ref = refs[:6]
        part_refs = refs[6:6 + n]
        dproj_ref, dcw_ref, dw_ref = refs[6 + n:9 + n]
        recv_refs = refs[9 + n:9 + 2 * n]
        carry, dw_acc, send, recv = refs[9 + 2 * n:]
        i = pl.program_id(0)
        r = nt - 1 - i

        @pl.when(i == 0)
        def _():
            dcw_ref[...] = jnp.zeros_like(dcw_ref)
            carry[...] = jnp.zeros_like(carry)
            dw_acc[...] = jnp.zeros_like(dw_acc)
            for cp in _device_exchange(part_refs, recv_refs, send, recv):
                cp.start()
        dya = _nt(doa_ref[...], w_ref[...])
        dw_acc[...] += _tn(ya_ref[...], doa_ref[...])
        bg = proj_ref[:, 0:D].astype(F32)
        cg = proj_ref[:, D:2 * D].astype(F32)
        u = proj_ref[:, 2 * D:3 * D].astype(F32)
        z = proj_ref[:, 3 * D:4 * D].astype(F32)
        v = cg * u
        before = jnp.where(r > 0, halo_ref[:, D:2 * D].astype(F32) * halo_ref[:, 2 * D:3 * D].astype(F32), 0.0)
        rows = lax.broadcasted_iota(jnp.int32, (tm, D), 0)
        v1, v2 = _shift_rows(v, before[HALO - 1:HALO, :], before[HALO - 2:HALO - 1, :], rows)
        conv = cw_ref[0:1, :] * v2 + cw_ref[1:2, :] * v1 + cw_ref[2:3, :] * v
        sg, sz = _silu_parts(z)
        dproj_ref[:, 0:D] = (dya * conv * sz).astype(BF16)
        dproj_ref[:, 3 * D:4 * D] = (dya * bg * conv * _dsilu(z, sg)).astype(BF16)
        dconv = dya * bg * sz
        _acc_row(dcw_ref, 0, jnp.sum(dconv * v2, axis=0, keepdims=True))
        _acc_row(dcw_ref, 1, jnp.sum(dconv * v1, axis=0, keepdims=True))
        _acc_row(dcw_ref, 2, jnp.sum(dconv * v, axis=0, keepdims=True))
        after = carry[...]
        up1 = jnp.where(rows < tm - 1, pltpu.roll(dconv, tm - 1, 0), after[0:1, :])
        up2 = jnp.where(rows < tm - 2, pltpu.roll(dconv, tm - 2, 0),
                        jnp.where(rows == tm - 2, after[0:1, :], after[1:2, :]))
        carry[...] = dconv[0:8, :]
        dv = cw_ref[2:3, :] * dconv + cw_ref[1:2, :] * up1 + cw_ref[0:1, :] * up2
        dproj_ref[:, D:2 * D] = (dv * u).astype(BF16)
        dproj_ref[:, 2 * D:3 * D] = (dv * cg).astype(BF16)

        @pl.when(i == nt - 1)
        def _():
            pltpu.sync_copy(dw_acc, dw_ref)
            for cp in _device_exchange(part_refs, recv_refs, send, recv):
                cp.wait()

    rev = lambda i: (nt - 1 - i, 0)
    fix = lambda i: (0, 0)
    halo = lambda i: (jnp.maximum((nt - 1 - i) * (tm // HALO) - 1, 0), 0)
    dproj, dcw, dw, *got = pl.pallas_call(
        body, name="a_bwd", grid=(nt,),
        in_specs=[pl.BlockSpec((tm, D), rev), pl.BlockSpec((tm, D), rev), pl.BlockSpec((tm, 4 * D), rev),
                  pl.BlockSpec((HALO, 4 * D), halo), pl.BlockSpec((8, D), fix), pl.BlockSpec((D, D), fix)] + ex_in,
        out_specs=[pl.BlockSpec((tm, 4 * D), rev), pl.BlockSpec((8, D), fix), pl.BlockSpec(memory_space=pl.ANY)] + ex_out,
        out_shape=[SDS((s, 4 * D), BF16), SDS((8, D), F32), SDS((D, D), F32)] + ex_shape,
        scratch_shapes=[pltpu.VMEM((8, D), F32), pltpu.VMEM((D, D), F32)] + ex_sems,
        compiler_params=_params(("arbitrary",)),
    )(doa, ya, proj, proj, conv_w, w_out, *parts)
    return dproj, dcw, dw, got


def _dn1(dp_ref, w_ref):
    dn = _nt(dp_ref[:, 0:D], w_ref[0])
    for j in range(1, 4):
        dn = dn + _nt(dp_ref[:, D * j:D * (j + 1)], w_ref[j])
    return dn


def _a_in_bwd_matmul(dproj, win_g, tm, count, parts):
    n = len(parts)
    ex_in, ex_out, ex_shape, ex_sems = _exchange_specs(parts)

    def body(*refs):
        dp_ref, w_ref = refs[:2]
        part_refs = refs[2:2 + n]
        dn_ref = refs[2 + n]
        recv_refs = refs[3 + n:3 + 2 * n]
        wcat = refs[3 + 2 * n]
        sems = refs[4 + 2 * n:]

        @pl.when(pl.program_id(0) == 0)
        def _():
            for cp in _chip_exchange(part_refs, recv_refs, *sems):
                cp.start()
            for j in range(N_CHIPS):
                pltpu.sync_copy(w_ref.at[j], wcat.at[:, pl.ds(D * j, D)])
        dn_ref[...] = _nt(dp_ref[...], wcat[...]).astype(BF16)

        @pl.when(pl.program_id(0) == count - 1)
        def _():
            for cp in _chip_exchange(part_refs, recv_refs, *sems):
                cp.wait()

    row = lambda i: (i, 0)
    dn, *got = pl.pallas_call(
        body, name="a_in_bwd_matmul", grid=(count,),
        in_specs=[pl.BlockSpec((tm, 4 * D), row), pl.BlockSpec(memory_space=pl.ANY)] + ex_in,
        out_specs=[pl.BlockSpec((tm, D), row)] + ex_out,
        out_shape=[SDS((count * tm, D), BF16)] + ex_shape,
        scratch_shapes=[pltpu.VMEM((D, 4 * D), BF16)] + ex_sems,
        compiler_params=_params(("arbitrary",)),
    )(dproj, win_g, *parts)
    return dn, got


def _a_in_bwd(dn_first, dproj, x, dh1, win_g, g_pre, tm):
    s = x.shape[0]
    nt = s // tm
    count = dn_first.shape[0] // tm

    def body(dn_ref, dp_ref, x_ref, dh_ref, w_ref, g_ref, gx_ref, dg_ref, dn_s):
        i = pl.program_id(0)

        @pl.when(i == 0)
        def _():
            dg_ref[...] = jnp.zeros_like(dg_ref)

        @pl.when(i < count)
        def _():
            dn_s[...] = dn_ref[...].astype(F32)

        @pl.when(i >= count)
        def _():
            dn_s[...] = _dn1(dp_ref, w_ref)
        dn = dn_s[...]
        xv = x_ref[...]
        r = _rms_scale(xv)
        xh = xv * r
        _acc_row(dg_ref, 0, jnp.sum(dn * xh, axis=0, keepdims=True))
        dxh = dn * g_ref[...]
        gx_ref[...] = dh_ref[...] + r * (dxh - xh * jnp.mean(dxh * xh, axis=-1, keepdims=True))

    row = lambda i: (i, 0)
    fix = lambda i: (0, 0)
    return pl.pallas_call(
        body, name="a_in_bwd", grid=(nt,),
        in_specs=[pl.BlockSpec((tm, D), lambda i: (jnp.minimum(i, count - 1), 0)),
                  pl.BlockSpec((tm, 4 * D), lambda i: (jnp.maximum(i, count), 0)),
                  pl.BlockSpec((tm, D), row), pl.BlockSpec((tm, D), row),
                  pl.BlockSpec((4, D, D), lambda i: (0, 0, 0)), pl.BlockSpec((1, D), fix)],
        out_specs=[pl.BlockSpec((tm, D), row), pl.BlockSpec((8, D), fix)],
        out_shape=[SDS((s, D), F32), SDS((8, D), F32)],
        scratch_shapes=[pltpu.VMEM((tm, D), F32)],
        compiler_params=_params(("arbitrary",)),
    )(dn_first, dproj, x, dh1, win_g, g_pre)


def _dw(a, b, tn, tmw, name):
    s, k = a.shape
    n = b.shape[1]

    def body(a_ref, b_ref, o_ref):
        @pl.when(pl.program_id(1) == 0)
        def _():
            o_ref[...] = jnp.zeros_like(o_ref)
        o_ref[0] += _tn(a_ref[...], b_ref[...])

    return pl.pallas_call(
        body, name=name, grid=(n // tn, s // tmw),
        in_specs=[pl.BlockSpec((tmw, k), lambda j, t: (t, 0)), pl.BlockSpec((tmw, tn), lambda j, t: (t, j))],
        out_specs=pl.BlockSpec((1, k, tn), lambda j, t: (j, 0, 0)),
        out_shape=SDS((n // tn, k, tn), F32),
        compiler_params=_params(("parallel", "arbitrary")),
    )(a, b)


def _sibling_exchange(name, to_sibling=(), shards=(), smalls=()):
    n_g, n_h, n_s = len(to_sibling), len(shards), len(smalls)

    def body(*refs):
        gs = refs[:n_g]
        pos = n_g + n_h
        small_ins = refs[pos:pos + n_s]
        pos += n_s
        rs, fs = refs[pos:pos + n_g], refs[pos + n_g:pos + n_g + n_h]
        pos += n_g + n_h
        small_alls = refs[pos:pos + n_s]
        pos += n_s
        dsend, drecv, ssend, srecv = refs[pos:]
        x, y, c = lax.axis_index("x"), lax.axis_index("y"), lax.axis_index("c")
        sibling = (x, y, 1 - c)
        sends, arrivals = [], []
        for a, (g, r) in enumerate(zip(gs, rs)):
            h = g.shape[1] // 2
            src = g.at[:, pl.ds(pl.multiple_of((1 - c) * h, 8), h), :]
            sends.append(pltpu.make_async_remote_copy(src_ref=src, dst_ref=r, send_sem=dsend.at[a], recv_sem=drecv.at[a],
                                                      device_id=sibling, device_id_type=MESH))
            arrivals.append(pltpu.make_async_remote_copy(src_ref=r, dst_ref=r, send_sem=dsend.at[a], recv_sem=drecv.at[a],
                                                         device_id=sibling, device_id_type=MESH))
        for b, full in enumerate(fs):
            h = full.shape[0] // 2
            mine = full.at[pl.ds(pl.multiple_of(c * h, 8), h)]
            theirs = full.at[pl.ds(pl.multiple_of((1 - c) * h, 8), h)]
            sends.append(pltpu.make_async_remote_copy(src_ref=mine, dst_ref=mine, send_sem=dsend.at[n_g + b],
                                                      recv_sem=drecv.at[n_g + b], device_id=sibling, device_id_type=MESH))
            arrivals.append(pltpu.make_async_remote_copy(src_ref=mine, dst_ref=theirs, send_sem=dsend.at[n_g + b],
                                                         recv_sem=drecv.at[n_g + b], device_id=sibling, device_id_type=MESH))
        me = 4 * x + 2 * y + c
        for k, (small_in, small_all) in enumerate(zip(small_ins, small_alls)):
            small_all[me] = small_in[...]
            for rel in range(1, N_DEV):
                fx, fy, fc = rel >> 2, (rel >> 1) & 1, rel & 1
                peer = (x + fx - 2 * x * fx, y + fy - 2 * y * fy, c + fc - 2 * c * fc)
                sender = 4 * peer[0] + 2 * peer[1] + peer[2]
                sem = (N_DEV - 1) * k + rel - 1
                sends.append(pltpu.make_async_remote_copy(
                    src_ref=small_in, dst_ref=small_all.at[me], send_sem=ssend.at[sem], recv_sem=srecv.at[sem],
                    device_id=peer, device_id_type=MESH))
                arrivals.append(pltpu.make_async_remote_copy(
                    src_ref=small_in, dst_ref=small_all.at[sender], send_sem=ssend.at[sem], recv_sem=srecv.at[sem],
                    device_id=peer, device_id_type=MESH))
        for cp in sends:
            cp.start()
        for cp in arrivals:
            cp.wait_recv()
        for cp in sends:
            cp.wait_send()

    anyspace = pl.BlockSpec(memory_space=pl.ANY)
    vm = pl.BlockSpec(memory_space=pltpu.VMEM)
    out_shape = [SDS((N_CHIPS, g.shape[1] // 2, g.shape[2]), F32) for g in to_sibling]
    out_shape += [SDS(full.shape, F32) for full in shards]
    out_shape += [SDS((N_DEV,) + sm.shape, F32) for sm in smalls]
    n_d2d = max(n_g + n_h, 1)
    n_all = (N_DEV - 1) * max(n_s, 1)
    outs = pl.pallas_call(
        body, name=name, out_shape=out_shape,
        in_specs=[anyspace] * (n_g + n_h) + [vm] * n_s, out_specs=[anyspace] * (n_g + n_h) + [vm] * n_s,
        scratch_shapes=[pltpu.SemaphoreType.DMA((n_d2d,)), pltpu.SemaphoreType.DMA((n_d2d,)),
                        pltpu.SemaphoreType.DMA((n_all,)), pltpu.SemaphoreType.DMA((n_all,))],
        input_output_aliases={n_g + b: n_g + b for b in range(n_h)},
    )(*to_sibling, *shards, *smalls)
    return outs[:n_g], outs[n_g:n_g + n_h], outs[n_g + n_h:]


def _add_sibling(where, g, r, name):
    _, rows, cols = g.shape
    h = rows // 2
    tr = min(h, 256)
    nh = h // tr

    def body(where_ref, g_ref, r_ref, t_ref, own_ref):
        t = g_ref[0] + r_ref[0]
        t_ref[0] = t.astype(BF16)

        @pl.when(pl.program_id(1) == where_ref[1])
        def _():
            own_ref[...] = t

    return pl.pallas_call(
        body, name=name,
        grid_spec=pltpu.PrefetchScalarGridSpec(
            num_scalar_prefetch=1, grid=(nh, N_CHIPS),
            in_specs=[pl.BlockSpec((1, tr, cols), lambda i, k, w: (k, w[0] * nh + i, 0)),
                      pl.BlockSpec((1, tr, cols), lambda i, k, w: (k, i, 0))],
            out_specs=[pl.BlockSpec((1, tr, cols), lambda i, k, w: (k, i, 0)),
                       pl.BlockSpec((tr, cols), lambda i, k, w: (i, 0))]),
        out_shape=[SDS((N_CHIPS, h, cols), BF16), SDS((h, cols), F32)],
        compiler_params=_params(("parallel", "arbitrary")),
    )(where, g, r)


def _add_devices(where, g, r, name):
    _, rows, cols = g.shape
    h = rows // 2
    tr = min(h, 256)
    nh = h // tr

    def body(where_ref, g_ref, r_ref, o_ref):
        del where_ref
        acc = g_ref[0]
        for k in range(N_DEV - 1):
            acc = acc + r_ref[k].astype(F32)
        o_ref[...] = acc

    return pl.pallas_call(
        body, name=name,
        grid_spec=pltpu.PrefetchScalarGridSpec(
            num_scalar_prefetch=1, grid=(nh,),
            in_specs=[pl.BlockSpec((1, tr, cols), lambda i, w: (w[1], w[0] * nh + i, 0)),
                      pl.BlockSpec((N_DEV - 1, tr, cols), lambda i, w: (0, i, 0))],
            out_specs=pl.BlockSpec((tr, cols), lambda i, w: (w[0] * nh + i, 0))),
        out_shape=SDS((rows, cols), F32),
        compiler_params=_params(("parallel",)),
    )(where, g, r)


def _add_chips(where, own, r, name):
    h, cols = own.shape
    tr = min(h, 256)
    nh = h // tr

    def body(where_ref, t_ref, r_ref, o_ref):
        del where_ref
        o_ref[...] = ((t_ref[...] + r_ref[0].astype(F32)) + r_ref[1].astype(F32)) + r_ref[2].astype(F32)

    return pl.pallas_call(
        body, name=name,
        grid_spec=pltpu.PrefetchScalarGridSpec(
            num_scalar_prefetch=1, grid=(nh,),
            in_specs=[pl.BlockSpec((tr, cols), lambda i, w: (i, 0)), pl.BlockSpec((3, tr, cols), lambda i, w: (0, i, 0))],
            out_specs=pl.BlockSpec((tr, cols), lambda i, w: (w[0] * nh + i, 0))),
        out_shape=SDS((2 * h, cols), F32),
        compiler_params=_params(("parallel",)),
    )(where, own, r)


def _sum_smalls(gathered):
    n = len(gathered)

    def body(*refs):
        for all_ref, o_ref in zip(refs[:n], refs[n:]):
            acc = all_ref[0]
            for dev in range(1, N_DEV):
                acc = acc + all_ref[dev]
            o_ref[...] = acc

    vm = pl.BlockSpec(memory_space=pltpu.VMEM)
    return pl.pallas_call(
        body, name="sum_smalls", out_shape=[SDS(a.shape[1:], F32) for a in gathered],
        in_specs=[vm] * n, out_specs=[vm] * n,
    )(*gathered)


def _adam_step(g, w, m, v):
    nm = ADAM_B1 * m + (1.0 - ADAM_B1) * g
    nv = ADAM_B2 * v + (1.0 - ADAM_B2) * (g * g)
    m_hat = nm / (1.0 - ADAM_B1 ** ADAM_STEP)
    v_hat = nv / (1.0 - ADAM_B2 ** ADAM_STEP)
    return -ADAM_LR * (m_hat / (jnp.sqrt(v_hat) + ADAM_EPS) + ADAM_WD * w), nm, nv


def _adamw(g, w, m, v, name):
    rows, cols = g.shape
    tr = min(rows, 256)

    def body(g_ref, w_ref, m_ref, v_ref, d_ref, nm_ref, nv_ref):
        d_ref[...], nm_ref[...], nv_ref[...] = _adam_step(g_ref[...], w_ref[...], m_ref[...], v_ref[...])

    spec = pl.BlockSpec((tr, cols), lambda i: (i, 0))
    return pl.pallas_call(
        body, name=name, grid=(rows // tr,), in_specs=[spec] * 4, out_specs=[spec] * 3,
        out_shape=[SDS(g.shape, F32)] * 3, compiler_params=_params(("parallel",)),
    )(g, w, m, v)


def _small_update(chip, tot, tot_rel, wmv):
    names = list(SMALL_PLACES)
    n = len(names)

    def body(chip_ref, tot_ref, quarter_ref, rel_ref, *refs):
        del chip_ref
        ins, outs = refs[:3 * n], refs[3 * n:]
        for i, nm in enumerate(names):
            source, row, (rows, cols) = SMALL_PLACES[nm]
            g = {"rows": tot_ref, "quarter": quarter_ref, "rel": rel_ref}[source][row:row + rows, 0:cols]
            outs[4 * i][...] = g
            outs[4 * i + 1][...], outs[4 * i + 2][...], outs[4 * i + 3][...] = _adam_step(
                g, ins[3 * i][...], ins[3 * i + 1][...], ins[3 * i + 2][...])

    whole = lambda shape: pl.BlockSpec(shape, lambda i, c: (0,) * len(shape))
    shapes = [SMALL_PLACES[nm][2] for nm in names]
    outs = pl.pallas_call(
        body, name="small_update",
        grid_spec=pltpu.PrefetchScalarGridSpec(
            num_scalar_prefetch=1, grid=(1,),
            in_specs=[whole(tot.shape), pl.BlockSpec((tot.shape[0], D // 4), lambda i, c: (0, c[0])),
                      whole(tot_rel.shape)] + [whole(shp) for shp in shapes for _ in range(3)],
            out_specs=[whole(shp) for shp in shapes for _ in range(4)]),
        out_shape=[SDS(shp, F32) for shp in shapes for _ in range(4)],
    )(chip, tot, tot, tot_rel, *[a for nm in names for a in wmv[nm]])
    return {nm: tuple(outs[4 * i:4 * i + 4]) for i, nm in enumerate(names)}


def _pad_rows(a, rows):
    return jnp.concatenate([a, jnp.zeros((rows - a.shape[0], a.shape[1]), a.dtype)], axis=0)


def _pad_cols(a, cols):
    return jnp.concatenate([a, jnp.zeros((a.shape[0], cols - a.shape[1]), a.dtype)], axis=1)


def kernel(x, a_pre_norm, a_w_in, a_conv_w, a_w_out, a_post_norm, kv_norm, w_kv, rel_bias, b_pre_norm, b_w_in, b_sinks, b_w_out, b_post_norm, loss_target, m_a_pre_norm, m_a_w_in, m_a_conv_w, m_a_w_out, m_a_post_norm, m_kv_norm, m_w_kv, m_rel_bias, m_b_pre_norm, m_b_w_in, m_b_sinks, m_b_w_out, m_b_post_norm, v_a_pre_norm, v_a_w_in, v_a_conv_w, v_a_w_out, v_a_post_norm, v_kv_norm, v_w_kv, v_rel_bias, v_b_pre_norm, v_b_w_in, v_b_sinks, v_b_w_out, v_b_post_norm):
    seq = x.shape[1]
    xs = x.reshape(seq, D)
    tgt = loss_target.reshape(seq, D)
    chip = 2 * lax.axis_index("x") + lax.axis_index("y")
    core = lax.axis_index("c")
    tm = _tile(seq, 512)
    tmw = _tile(seq, 1024)

    shards = [a_w_in[0], a_w_out[0], w_kv, b_w_in[0], b_w_out[0]]
    small_w = _pad_rows(jnp.concatenate([a_pre_norm, a_conv_w[0], a_post_norm], axis=0), 8)
    *own_only, small_g = _prepare_weights(shards, small_w)
    where = jnp.stack([core, chip]).astype(jnp.int32)
    small_full = small_g.transpose(1, 0, 2).reshape(8, D)
    g_apre, conv_w, g_apost = small_full[0:1], _pad_rows(small_full[1:4], 8), small_full[4:5]
    g_kv = kv_norm.reshape(1, D)

    proj, n1, (win_g, wouta_g, wkv_g, wbin_g, woutb_g) = _a_in(where[1:2], xs, g_apre, own_only, tmw)
    wouta = wouta_g.reshape(D, D)
    wkv = wkv_g.reshape(D, 2 * KV_W)
    woutb = woutb_g.reshape(D, D)
    ya, oa, h1 = _a_mix(proj, xs, conv_w, wouta, g_apost, tm)
    kv, q, zb = _b_in(h1, g_kv, b_pre_norm, wkv, wbin_g, tmw)
    tab = _bias_table(rel_bias, b_sinks.reshape(N_HEADS))
    att, stats = _attn_fwd(q, kv, tab)
    dh2, dqz, datt, loss_acc, dg_bpost, dw_outb, dw_outb16 = _mid(att, zb, h1, tgt, woutb, b_post_norm, tm)

    dqz, dkv, dtab = _attn_bwd(q, kv, datt, stats, tab, dqz)
    dh1, doa, dg_b, dw_bin, dw_kv, dw_bin16, dw_kv16 = _b_bwd(dqz, dkv, h1, dh2, oa, wbin_g, wkv, g_kv, b_pre_norm,
                                                              g_apost, tm)
    by_chip = lambda a, cols: a.reshape(N_CHIPS, D // 4, cols)
    grads1 = [by_chip(dw_kv, 2 * KV_W), dw_bin, by_chip(dw_outb, D)]
    sent1 = [by_chip(dw_kv16, 2 * KV_W), dw_bin16, by_chip(dw_outb16, D)]
    names1 = ["w_kv", "b_w_in", "b_w_out"]
    dproj, dconv_w, dw_outa, from_devices1 = _a_bwd(doa, ya, proj, conv_w, wouta, tm, sent1)
    shards1 = [_add_devices(where, g, r, "add_devices_" + nm) for g, r, nm in zip(grads1, from_devices1, names1)]
    dw_in = _dw(n1, dproj, D, _tile(seq, 2048), "dw_a_in")
    grads2 = [dw_in, dw_outa.reshape(N_CHIPS, D // 4, D)]
    names2 = ["a_w_in", "a_w_out"]
    from_sibling2, (g_wkv, g_wbin, g_woutb), _ = _sibling_exchange("to_sibling_2", to_sibling=grads2, shards=shards1)
    sums2 = [_add_sibling(where, g, r, "add_sibling_" + nm) for g, r, nm in zip(grads2, from_sibling2, names2)]
    nt = seq // tmw
    dn_first, from_chips2 = _a_in_bwd_matmul(dproj, win_g, tmw, max(nt - max(nt // 4, 1), 1), [t for t, _ in sums2])
    grad_x, dg_apre = _a_in_bwd(dn_first, dproj, xs, dh1, win_g, g_apre, tm)
    shards2 = [_add_chips(where, own, r, "add_chips_" + nm) for (_, own), r, nm in zip(sums2, from_chips2, names2)]
    drel, dsink = _bias_fold(dtab)

    smalls = jnp.concatenate([
        dg_apre[0:1], dg_b[2:3], dg_b[0:1], dg_b[1:2], dg_bpost[0:1], _pad_cols(dsink[0:1], D),
        _pad_cols(loss_acc[0:1], D), jnp.zeros((1, D), F32), dconv_w], axis=0)
    assert smalls.shape == (SMALL_ROWS, D)
    _, (g_win, g_wouta), gathered = _sibling_exchange("share_last", shards=shards2, smalls=(smalls, drel))
    tot, tot_rel = _sum_smalls(gathered)

    big = {}
    for nm, g, w, m, v in [("a_w_in", g_win, a_w_in, m_a_w_in, v_a_w_in), ("a_w_out", g_wouta, a_w_out, m_a_w_out, v_a_w_out),
                           ("w_kv", g_wkv, w_kv, m_w_kv, v_w_kv), ("b_w_in", g_wbin, b_w_in, m_b_w_in, v_b_w_in),
                           ("b_w_out", g_woutb, b_w_out, m_b_w_out, v_b_w_out)]:
        shp = w.shape
        two = (shp[-2], shp[-1])
        d, nm_, nv_ = _adamw(g, w.reshape(two), m.reshape(two), v.reshape(two), "adamw_" + nm)
        big[nm] = (g.reshape(shp), d.reshape(shp), nm_.reshape(shp), nv_.reshape(shp))

    given = {"a_pre_norm": (a_pre_norm, m_a_pre_norm, v_a_pre_norm), "a_conv_w": (a_conv_w, m_a_conv_w, v_a_conv_w),
             "a_post_norm": (a_post_norm, m_a_post_norm, v_a_post_norm), "kv_norm": (kv_norm, m_kv_norm, v_kv_norm),
             "rel_bias": (rel_bias, m_rel_bias, v_rel_bias), "b_pre_norm": (b_pre_norm, m_b_pre_norm, v_b_pre_norm),
             "b_sinks": (b_sinks, m_b_sinks, v_b_sinks), "b_post_norm": (b_post_norm, m_b_post_norm, v_b_post_norm)}
    small = _small_update(where[1:2], tot, tot_rel, {nm: tuple(a.reshape(SMALL_PLACES[nm][2]) for a in wmv)
                                            for nm, wmv in given.items()})
    order = ["a_pre_norm", "a_w_in", "a_conv_w", "a_w_out", "a_post_norm", "kv_norm", "w_kv", "rel_bias",
             "b_pre_norm", "b_w_in", "b_sinks", "b_w_out", "b_post_norm"]
    outs = []
    for which in range(4):
        for nm in order:
            outs.append(big[nm][which] if nm in big else small[nm][which].reshape(given[nm][0].shape))
    loss = 0.5 * tot[LOSS_ROW, 0]
    return (loss, grad_x.reshape(x.shape), *outs)
```

```python
import math

import jax
import jax.numpy as jnp
from jax import lax
from jax.experimental import pallas as pl
from jax.experimental.pallas import tpu as pltpu

F32 = jnp.float32
BF16 = jnp.bfloat16
MESH = pl.DeviceIdType.MESH
SDS = jax.ShapeDtypeStruct

D = 1024
HEAD_DIM = 64
N_HEADS = 16
N_KV = 2
GROUP = 8
KV_W = 128
BLK = 128
N_BUCKETS = 32
MAX_EXACT = 16
MAX_DISTANCE = 128
EPS = 1e-6
NEG_INF = -1e30
Q_SCALE = HEAD_DIM ** -0.5

ADAM_LR = 0.001
ADAM_B1 = 0.9
ADAM_B2 = 0.999
ADAM_EPS = 1e-08
ADAM_WD = 0.01
ADAM_STEP = 10

N_CHIPS = 4
N_DEV = 8
BIN_COLS = 2 * D // N_CHIPS
VMEM_LIMIT = 56 * 1024 * 1024
SMALL_ROWS = 16
LOSS_ROW = 6
SMALL_PLACES = {
    "a_pre_norm": ("quarter", 0, (1, D // 4)), "a_conv_w": ("quarter", 8, (3, D // 4)),
    "a_post_norm": ("quarter", 1, (1, D // 4)), "kv_norm": ("rows", 2, (1, D)),
    "rel_bias": ("rel", 0, (N_BUCKETS, N_HEADS)), "b_pre_norm": ("rows", 3, (1, D)),
    "b_sinks": ("rows", 5, (1, N_HEADS)), "b_post_norm": ("rows", 4, (1, D)),
}
HALO = 16


def _bucket_thresholds():
    def bucket(d):
        big = MAX_EXACT + int(math.log(d / MAX_EXACT) / math.log(MAX_DISTANCE / MAX_EXACT)
                              * (N_BUCKETS - MAX_EXACT))
        return d if d < MAX_EXACT else min(big, N_BUCKETS - 1)
    out = []
    for b in range(MAX_EXACT + 1, N_BUCKETS):
        out.append(min(d for d in range(MAX_EXACT, MAX_DISTANCE) if bucket(d) >= b))
    return tuple(out)


BUCKET_THRESHOLDS = _bucket_thresholds()


def _params(semantics=None, vmem=VMEM_LIMIT):
    return pltpu.CompilerParams(dimension_semantics=semantics, vmem_limit_bytes=vmem)


def _tile(n, pref):
    return pref if n >= 2 * pref else max(n // 2, 8)


def _rms_scale(v):
    return lax.rsqrt(jnp.mean(v * v, axis=-1, keepdims=True) + EPS)


def _nt(a, b):
    return lax.dot_general(a, b, (((1,), (1,)), ((), ())), preferred_element_type=F32)


def _tn(a, b):
    return lax.dot_general(a, b, (((0,), (0,)), ((), ())), preferred_element_type=F32)


def _nn(a, b):
    return jnp.dot(a, b, preferred_element_type=F32)


def _silu_parts(z):
    sg = jax.nn.sigmoid(z)
    return sg, z * sg


def _dsilu(z, sg):
    return sg * (1.0 + z * (1.0 - sg))


def _write_gradient(acc, out32, out16, stage):
    pltpu.sync_copy(acc, out32)
    rows = stage.shape[0]
    for k in range(acc.shape[0] // rows):
        stage[...] = acc[rows * k:rows * (k + 1), :].astype(BF16)
        pltpu.sync_copy(stage, out16.at[pl.ds(rows * k, rows)])


def _acc_row(ref, row, val):
    ref[row:row + 1, :] += val


def _gather_copies(outs, splits, ici_send, ici_recv, d2d_send, d2d_recv):
    x, y, c = lax.axis_index("x"), lax.axis_index("y"), lax.axis_index("c")
    k = 2 * x + y
    sibling = (x, y, 1 - c)

    def part(o_ref, chip, core, split):
        if not split:
            return o_ref.at[chip]
        h = o_ref.shape[1] // 2
        return o_ref.at[chip, pl.ds(pl.multiple_of(core * h, 16), h)]

    def remote(ref, a, j, sems, to):
        return pltpu.make_async_remote_copy(src_ref=ref, dst_ref=ref, send_sem=sems[0].at[3 * a + j],
                                            recv_sem=sems[1].at[3 * a + j], device_id=to, device_id_type=MESH)

    copies = []
    for a, (o_ref, split) in enumerate(zip(outs, splits)):
        for j, (px, py) in enumerate([(x, 1 - y), (1 - x, y), (1 - x, 1 - y)]):
            kj = 2 * px + py
            ici, d2d = (ici_send, ici_recv), (d2d_send, d2d_recv)
            copies.append((remote(part(o_ref, k, c, split), a, j, ici, (px, py, c)),
                           remote(part(o_ref, kj, c, split), a, j, ici, (px, py, c)),
                           remote(part(o_ref, kj, c, split), a, j, d2d, sibling) if split else None,
                           remote(part(o_ref, kj, 1 - c, split), a, j, d2d, sibling) if split else None))
    return copies


def _gather_sems(n):
    return [pltpu.SemaphoreType.DMA((3 * n,)) for _ in range(4)]


def _prepare_weights(shards, small):
    n = len(shards)

    def body(*refs):
        ins, small_in = refs[:n], refs[n]
        outs, small_out = refs[n + 1:2 * n + 1], refs[2 * n + 1]
        stages, put_sem = refs[2 * n + 2:3 * n + 2], refs[3 * n + 2]
        sems = refs[3 * n + 3:]
        k = 2 * lax.axis_index("x") + lax.axis_index("y")
        puts = []
        for a, (i_ref, stage, o_ref) in enumerate(zip(ins, stages, outs)):
            stage[...] = i_ref[...].astype(BF16)
            puts.append(pltpu.make_async_copy(stage, o_ref.at[k], put_sem.at[a]))
            puts[-1].start()
        small_out[k] = small_in[...]
        copies = _gather_copies([small_out], [False], *sems)
        for send, _, _, _ in copies:
            send.start()
        for _, arrival, _, _ in copies:
            arrival.wait_recv()
        for send, _, _, _ in copies:
            send.wait_send()
        for put in puts:
            put.wait()

    vm = pl.BlockSpec(memory_space=pltpu.VMEM)
    anyspace = pl.BlockSpec(memory_space=pl.ANY)
    out_shape = [SDS((N_CHIPS,) + s.shape, BF16) for s in shards] + [SDS((N_CHIPS,) + small.shape, F32)]
    return pl.pallas_call(
        body, name="prepare_weights", out_shape=out_shape,
        in_specs=[vm] * (n + 1), out_specs=[anyspace] * n + [vm],
        scratch_shapes=[pltpu.VMEM(s.shape, BF16) for s in shards] + [pltpu.SemaphoreType.DMA((n,))] + _gather_sems(1),
        compiler_params=pltpu.CompilerParams(vmem_limit_bytes=VMEM_LIMIT),
    )(*shards, small)


def _a_in(chip, x, g_pre, weights, tm):
    s = x.shape[0]
    nt = s // tm
    n = len(weights)

    def body(chip_ref, x_ref, g_ref, *refs):
        proj_ref, n1_ref = refs[n:n + 2]
        gathered = refs[n + 2:2 * n + 2]
        wbuf, n1_all, fetch_sem = refs[2 * n + 2:2 * n + 5]
        sems = refs[2 * n + 5:]
        jj, i = pl.program_id(0), pl.program_id(1)
        copies = _gather_copies(gathered, [True] * n, *sems)

        def fetch(rel):
            slot = jnp.bitwise_xor(chip_ref[0], rel)
            return pltpu.make_async_copy(gathered[0].at[slot], wbuf.at[rel % 2], fetch_sem.at[rel % 2])

        @pl.when((jj == 0) & (i == 0))
        def _():
            fetch(0).start()
            copies[0][0].start()
            copies[1][0].start()
            fetch(0).wait()

        for rel in (1, 2, 3):
            @pl.when((jj == rel) & (i == 0))
            def _():
                fetch(rel).wait()

        @pl.when(jj == 0)
        def _():
            xv = x_ref[...]
            n1 = (xv * _rms_scale(xv) * g_ref[...]).astype(BF16)
            n1_ref[...] = n1
            n1_all[i] = n1
        proj_ref[...] = _nn(n1_all[i], wbuf[jj % 2]).astype(BF16)

        for rel in (1, 2, 3):
            @pl.when((jj == rel - 1) & (i == max(nt - 2, nt // 2)))
            def _():
                _, arrival, forward, forwarded = copies[rel - 1]
                arrival.wait_recv()
                forward.start()
                forwarded.wait_recv()
                fetch(rel).start()
                if rel == 1:
                    for send, _, _, _ in copies[2:]:
                        send.start()

        @pl.when((jj == 3) & (i == max(nt - 2, 0)))
        def _():
            for _, arrival, forward, _ in copies[3:]:
                arrival.wait_recv()
                forward.start()

        @pl.when((jj == 3) & (i == nt - 1))
        def _():
            for _, _, _, forwarded in copies[3:]:
                forwarded.wait_recv()
            for send, _, forward, _ in copies:
                forward.wait_send()
                send.wait_send()

    anyspace = pl.BlockSpec(memory_space=pl.ANY)
    proj, n1, *gathered = pl.pallas_call(
        body, name="a_in",
        grid_spec=pltpu.PrefetchScalarGridSpec(
            num_scalar_prefetch=1, grid=(4, nt),
            in_specs=[pl.BlockSpec((tm, D), lambda jj, i, c: (jnp.where(jj == 0, i, nt - 1), 0)),
                      pl.BlockSpec((1, D), lambda jj, i, c: (0, 0))] + [anyspace] * n,
            out_specs=[pl.BlockSpec((tm, D), lambda jj, i, c: (i, jnp.bitwise_xor(c[0], jj))),
                       pl.BlockSpec((tm, D), lambda jj, i, c: (jnp.where(jj == 0, i, nt - 1), 0))] + [anyspace] * n,
            scratch_shapes=[pltpu.VMEM((2, D, D), BF16), pltpu.VMEM((nt, tm, D), BF16),
                            pltpu.SemaphoreType.DMA((2,))] + _gather_sems(n)),
        out_shape=[SDS((s, 4 * D), BF16), SDS((s, D), BF16)] + [SDS(w.shape, w.dtype) for w in weights],
        input_output_aliases={3 + a: 2 + a for a in range(n)},
        compiler_params=_params(("arbitrary", "arbitrary")),
    )(chip, x, g_pre, *weights)
    return proj, n1, gathered


def _shift_rows(v, last, second_last, rows):
    v1 = jnp.where(rows >= 1, pltpu.roll(v, 1, 0), last)
    v2 = jnp.where(rows >= 2, pltpu.roll(v, 2, 0), jnp.where(rows == 1, last, second_last))
    return v1, v2


def _a_mix(proj, x, conv_w, w_out, g_post, tm):
    s = x.shape[0]

    def body(proj_ref, x_ref, cw_ref, w_ref, g_ref, ya_ref, oa_ref, h1_ref, carry):
        @pl.when(pl.program_id(0) == 0)
        def _():
            carry[...] = jnp.zeros_like(carry)
        v = proj_ref[:, D:2 * D].astype(F32) * proj_ref[:, 2 * D:3 * D].astype(F32)
        rows = lax.broadcasted_iota(jnp.int32, (tm, D), 0)
        before = carry[...]
        v1, v2 = _shift_rows(v, before[7:8, :], before[6:7, :], rows)
        carry[...] = v[tm - 8:tm, :]
        conv = cw_ref[0:1, :] * v2 + cw_ref[1:2, :] * v1 + cw_ref[2:3, :] * v
        _, sz = _silu_parts(proj_ref[:, 3 * D:4 * D].astype(F32))
        ya = (proj_ref[:, 0:D].astype(F32) * conv * sz).astype(BF16)
        ya_ref[...] = ya
        oa = _nn(ya, w_ref[...])
        oa_ref[...] = oa.astype(BF16)
        h1_ref[...] = x_ref[...] + oa * _rms_scale(oa) * g_ref[...]

    row = lambda i: (i, 0)
    fix = lambda i: (0, 0)
    return pl.pallas_call(
        body, name="a_mix", grid=(s // tm,),
        in_specs=[pl.BlockSpec((tm, 4 * D), row), pl.BlockSpec((tm, D), row), pl.BlockSpec((8, D), fix),
                  pl.BlockSpec((D, D), fix), pl.BlockSpec((1, D), fix)],
        out_specs=[pl.BlockSpec((tm, D), row)] * 3,
        out_shape=[SDS((s, D), BF16), SDS((s, D), BF16), SDS((s, D), F32)],
        scratch_shapes=[pltpu.VMEM((8, D), F32)],
        compiler_params=_params(("arbitrary",)),
    )(proj, x, conv_w, w_out, g_post)


def _b_in(h1, g_kv, g_pre, w_kv, wbin_g, tm):
    s = h1.shape[0]

    def body(h_ref, gk_ref, gb_ref, wkv_ref, wb_ref, kv_ref, q_ref, z_ref):
        h = h_ref[...]
        hh = h * _rms_scale(h)
        nk = (hh * gk_ref[...]).astype(BF16)
        nb = (hh * gb_ref[...]).astype(BF16)
        kv_ref[...] = _nn(nk, wkv_ref[...]).astype(BF16)
        for j in range(2):
            q_ref[:, BIN_COLS * j:BIN_COLS * (j + 1)] = (_nn(nb, wb_ref[j]) * Q_SCALE).astype(BF16)
            z_ref[:, BIN_COLS * j:BIN_COLS * (j + 1)] = _nn(nb, wb_ref[2 + j]).astype(BF16)

    row = lambda i: (i, 0)
    fix = lambda i: (0, 0)
    return pl.pallas_call(
        body, name="b_in", grid=(s // tm,),
        in_specs=[pl.BlockSpec((tm, D), row), pl.BlockSpec((1, D), fix), pl.BlockSpec((1, D), fix),
                  pl.BlockSpec((D, 2 * KV_W), fix), pl.BlockSpec((N_CHIPS, D, BIN_COLS), lambda i: (0, 0, 0))],
        out_specs=[pl.BlockSpec((tm, 2 * KV_W), row), pl.BlockSpec((tm, D), row), pl.BlockSpec((tm, D), row)],
        out_shape=[SDS((s, 2 * KV_W), BF16), SDS((s, D), BF16), SDS((s, D), BF16)],
        compiler_params=_params(("parallel",)),
    )(h1, g_kv, g_pre, w_kv, wbin_g)


def _band_buckets():
    q = lax.broadcasted_iota(jnp.int32, (BLK, 2 * BLK), 0)
    k = lax.broadcasted_iota(jnp.int32, (BLK, 2 * BLK), 1)
    dist = q + BLK - k
    bucket = jnp.where(dist < MAX_EXACT, dist, MAX_EXACT)
    for t in BUCKET_THRESHOLDS:
        bucket = bucket + jnp.where(dist >= t, 1, 0)
    in_window = (dist >= 0) & (dist < BLK)
    return jnp.where(in_window, bucket, -1)


def _head_place(h):
    kh, j, e = h // GROUP, (h % GROUP) // 2, h % 2
    return kh, slice(BLK * j, BLK * (j + 1)), slice(2 * BLK * e, 2 * BLK * (e + 1))


def _bias_table(rel_bias, sinks):
    def body(rb_ref, sink_ref, tab_ref):
        bucket = _band_buckets()
        col = lax.broadcasted_iota(jnp.int32, (BLK, 2 * BLK), 1)
        for h in range(N_HEADS):
            acc = jnp.where(bucket < 0, NEG_INF, 0.0).astype(F32)
            for b in range(N_BUCKETS):
                acc = jnp.where(bucket == b, rb_ref[b, h], acc)
            acc = jnp.where(col == 0, sink_ref[h], acc)
            kh, rows, cols = _head_place(h)
            tab_ref[1, kh, rows, cols] = acc
            tab_ref[0, kh, rows, cols] = jnp.where((col > 0) & (col < BLK), NEG_INF, acc)

    return pl.pallas_call(
        body, name="bias_table", out_shape=SDS((2, N_KV, 4 * BLK, 4 * BLK), F32),
        in_specs=[pl.BlockSpec(memory_space=pltpu.SMEM), pl.BlockSpec(memory_space=pltpu.SMEM)],
        out_specs=pl.BlockSpec(memory_space=pltpu.VMEM),
    )(rel_bias, sinks)


def _bias_fold(dtab):
    def body(dtab_ref, out_ref, dsink_ref):
        bucket = _band_buckets()
        row = lax.broadcasted_iota(jnp.int32, (N_BUCKETS, 128), 0)
        lane = lax.broadcasted_iota(jnp.int32, (N_BUCKETS, 128), 1)
        row8 = lax.broadcasted_iota(jnp.int32, (8, 128), 0)
        lane8 = lax.broadcasted_iota(jnp.int32, (8, 128), 1)
        acc = jnp.zeros((N_BUCKETS, 128), F32)
        dsink = jnp.zeros((8, 128), F32)
        for h in range(N_HEADS):
            kh, rows, cols = _head_place(h)
            dt = dtab_ref[kh, rows, cols]
            for b in range(N_BUCKETS):
                val = jnp.sum(jnp.where(bucket == b, dt, 0.0))
                acc = acc + jnp.where((row == b) & (lane == h), val, 0.0)
            dsink = dsink + jnp.where((row8 == 0) & (lane8 == h), jnp.sum(dt[:, 0:1]), 0.0)
        out_ref[...] = acc
        dsink_ref[...] = dsink

    vm = pl.BlockSpec(memory_space=pltpu.VMEM)
    return pl.pallas_call(
        body, name="bias_fold", out_shape=[SDS((N_BUCKETS, 128), F32), SDS((8, 128), F32)],
        in_specs=[vm], out_specs=[vm, vm],
    )(dtab)


def _pair_operands(prev, cur):
    t = jnp.concatenate([prev, cur], axis=0).astype(F32)
    t = jnp.where(lax.broadcasted_iota(jnp.int32, t.shape, 0) == 0, 0.0, t)
    tr = pltpu.roll(t, HEAD_DIM, 1)
    lo = lax.broadcasted_iota(jnp.int32, t.shape, 1) < HEAD_DIM
    zero = jnp.zeros_like(t)
    head0 = jnp.concatenate([jnp.where(lo, t, zero), jnp.where(lo, zero, tr)], axis=0).astype(BF16)
    head1 = jnp.concatenate([jnp.where(lo, tr, zero), jnp.where(lo, zero, t)], axis=0).astype(BF16)
    return head0, head1


def _pair_fold(d0, d1):
    lo = lax.broadcasted_iota(jnp.int32, (2 * BLK, KV_W), 1) < HEAD_DIM
    zero = jnp.zeros((2 * BLK, KV_W), F32)
    g0 = jnp.where(lo, d0[0:256], zero) + pltpu.roll(jnp.where(lo, zero, d0[256:512]), HEAD_DIM, 1)
    g1 = pltpu.roll(jnp.where(lo, d1[0:256], zero), HEAD_DIM, 1) + jnp.where(lo, zero, d1[256:512])
    return jnp.where(lax.broadcasted_iota(jnp.int32, (2 * BLK, KV_W), 0) == 0, 0.0, g0 + g1)


def _stack_pairs(ref, kh, rows=slice(None)):
    return jnp.concatenate([ref[rows, 128 * (4 * kh + j):128 * (4 * kh + j + 1)] for j in range(4)], axis=0)


def _table_spec():
    return pl.BlockSpec((1, N_KV, 4 * BLK, 4 * BLK), lambda n: (jnp.minimum(n, 1), 0, 0, 0))


def _attn_fwd(q, kv, tab):
    s = q.shape[0]

    def body(q_ref, kp_ref, k0_ref, k1_ref, vp_ref, v0_ref, v1_ref, tab0_ref, tab1_ref, att_ref, stats_ref):
        lane = lax.broadcasted_iota(jnp.int32, (BLK, 128), 1)
        for sub, (kp, kc, vp, vc, tab_ref) in enumerate([(kp_ref, k0_ref, vp_ref, v0_ref, tab0_ref),
                                                         (k0_ref, k1_ref, v0_ref, v1_ref, tab1_ref)]):
            rows = slice(BLK * sub, BLK * (sub + 1))
            k2 = _pair_operands(kp[...], kc[...])
            v2 = _pair_operands(vp[...], vc[...])
            stats = jnp.zeros((BLK, 128), F32)
            for kh in range(N_KV):
                sc = _nt(_stack_pairs(q_ref, kh, rows), k2[kh])
                ps = []
                for e in range(2):
                    lg = sc[:, 256 * e:256 * (e + 1)] + tab_ref[0, kh, :, 256 * e:256 * (e + 1)]
                    m = jnp.max(lg, axis=-1, keepdims=True)
                    ex = jnp.exp(lg - m)
                    den = jnp.sum(ex, axis=-1, keepdims=True)
                    ps.append(ex * (1.0 / den))
                    lse = m + jnp.log(den)
                    for j in range(4):
                        stats = jnp.where(lane == GROUP * kh + 2 * j + e, lse[BLK * j:BLK * (j + 1)], stats)
                out = _nn(jnp.concatenate(ps, axis=1).astype(BF16), v2[kh])
                for j in range(4):
                    att_ref[rows, 128 * (4 * kh + j):128 * (4 * kh + j + 1)] = out[BLK * j:BLK * (j + 1)].astype(BF16)
            stats_ref[rows, :] = stats

    two = lambda m: (m, 0)
    table = lambda pick: pl.BlockSpec((1, N_KV, 4 * BLK, 4 * BLK), lambda m: (pick(m), 0, 0, 0))
    return pl.pallas_call(
        body, name="attn_fwd", grid=(s // (2 * BLK),),
        in_specs=[pl.BlockSpec((2 * BLK, D), two)]
        + [pl.BlockSpec((BLK, KV_W), lambda m, col=col, off=off: (jnp.maximum(2 * m + off, 0), col))
           for col in (0, 1) for off in (-1, 0, 1)]
        + [table(lambda m: jnp.minimum(m, 1)), table(lambda m: 1)],
        out_specs=[pl.BlockSpec((2 * BLK, D), two), pl.BlockSpec((2 * BLK, 128), two)],
        out_shape=[SDS((s, D), BF16), SDS((s, 128), F32)],
        compiler_params=_params(("parallel",)),
    )(q, kv, kv, kv, kv, kv, kv, tab, tab)


def _mid(att, zb, h1, tgt, w_out, g_post, tm):
    s = att.shape[0]
    nt = s // tm

    def body(att_ref, z_ref, h1_ref, t_ref, w_ref, g_ref,
             dh_ref, dqz_ref, datt_ref, loss_ref, dg_ref, dw_ref, dw16_ref, dw_acc, stage):
        @pl.when(pl.program_id(0) == 0)
        def _():
            loss_ref[...] = jnp.zeros_like(loss_ref)
            dg_ref[...] = jnp.zeros_like(dg_ref)
            dw_acc[...] = jnp.zeros_like(dw_acc)
        att = att_ref[...].astype(F32)
        z = z_ref[...].astype(F32)
        sg, sz = _silu_parts(z)
        ob = (att * sz).astype(BF16)
        y2 = _nn(ob, w_ref[...])
        r2 = _rms_scale(y2)
        yh = y2 * r2
        g = g_ref[...]
        err = (h1_ref[...] + yh * g) - t_ref[...]
        loss_ref[...] += jnp.sum(jnp.sum(err * err, axis=-1, keepdims=True) / D)
        dh = err / D
        dh_ref[...] = dh
        _acc_row(dg_ref, 0, jnp.sum(dh * yh, axis=0, keepdims=True))
        dyh = dh * g
        dy = (r2 * (dyh - yh * jnp.mean(dyh * yh, axis=-1, keepdims=True))).astype(BF16)
        dw_acc[...] += _tn(ob, dy)
        dob = _nt(dy, w_ref[...])
        datt_ref[...] = (dob * sz).astype(BF16)
        dqz_ref[...] = (dob * att * _dsilu(z, sg)).astype(BF16)

        @pl.when(pl.program_id(0) == nt - 1)
        def _():
            _write_gradient(dw_acc, dw_ref, dw16_ref, stage)

    row = lambda i: (i, 0)
    fix = lambda i: (0, 0)
    anyspace = pl.BlockSpec(memory_space=pl.ANY)
    return pl.pallas_call(
        body, name="mid", grid=(nt,),
        in_specs=[pl.BlockSpec((tm, D), row)] * 4 + [pl.BlockSpec((D, D), fix), pl.BlockSpec((1, D), fix)],
        out_specs=[pl.BlockSpec((tm, D), row), pl.BlockSpec((tm, D), lambda i: (i, 1)), pl.BlockSpec((tm, D), row),
                   pl.BlockSpec((8, 128), fix), pl.BlockSpec((8, D), fix), anyspace, anyspace],
        out_shape=[SDS((s, D), F32), SDS((s, 2 * D), BF16), SDS((s, D), BF16), SDS((8, 128), F32),
                   SDS((8, D), F32), SDS((D, D), F32), SDS((D, D), BF16)],
        scratch_shapes=[pltpu.VMEM((D, D), F32), pltpu.VMEM((D // 4, D), BF16)],
        compiler_params=_params(("arbitrary",)),
    )(att, zb, h1, tgt, w_out, g_post)


def _attn_bwd(q, kv, datt, stats, tab, dqz):
    s = q.shape[0]
    nb = s // BLK

    def body(q_ref, kp_ref, kc_ref, vp_ref, vc_ref, da_ref, st_ref, tab_ref, dqz_in,
             dq_ref, dkv_ref, dtab_ref, dk_carry, dv_carry):
        del dqz_in
        n = pl.program_id(0)

        @pl.when(n == 0)
        def _():
            dtab_ref[...] = jnp.zeros_like(dtab_ref)
            dk_carry[...] = jnp.zeros_like(dk_carry)
            dv_carry[...] = jnp.zeros_like(dv_carry)

        @pl.when(n < nb)
        def _():
            k2 = _pair_operands(kp_ref[...], kc_ref[...])
            v2 = _pair_operands(vp_ref[...], vc_ref[...])
            lane = lax.broadcasted_iota(jnp.int32, (BLK, 128), 1)
            stats = st_ref[...]
            dk2, dv2 = [], []
            for kh in range(N_KV):
                qs = _stack_pairs(q_ref, kh)
                das = _stack_pairs(da_ref, kh)
                sc = _nt(qs, k2[kh])
                dp = _nt(das, v2[kh])
                ps, dss = [], []
                for e in range(2):
                    heads = [GROUP * kh + 2 * j + e for j in range(4)]
                    lse = jnp.concatenate([jnp.sum(jnp.where(lane == h, stats, 0.0), axis=-1, keepdims=True)
                                           for h in heads], axis=0)
                    cols = slice(256 * e, 256 * (e + 1))
                    p = jnp.exp(sc[:, cols] + tab_ref[0, kh, :, cols] - lse)
                    delta = jnp.sum(p * dp[:, cols], axis=-1, keepdims=True)
                    ds = p * (dp[:, cols] - delta)
                    dtab_ref[kh, :, cols] += ds
                    ps.append(p)
                    dss.append(ds)
                p2 = jnp.concatenate(ps, axis=1).astype(BF16)
                ds2 = jnp.concatenate(dss, axis=1).astype(BF16)
                dq = _nn(ds2, k2[kh]) * Q_SCALE
                for j in range(4):
                    dq_ref[:, 128 * (4 * kh + j):128 * (4 * kh + j + 1)] = dq[BLK * j:BLK * (j + 1)].astype(BF16)
                dk2.append(_tn(ds2, qs))
                dv2.append(_tn(p2, das))
            dkk = _pair_fold(dk2[0], dk2[1])
            dvv = _pair_fold(dv2[0], dv2[1])
            dkv_ref[:, 0:KV_W] = (dk_carry[...] + dkk[0:BLK]).astype(BF16)
            dkv_ref[:, KV_W:2 * KV_W] = (dv_carry[...] + dvv[0:BLK]).astype(BF16)
            dk_carry[...] = dkk[BLK:2 * BLK]
            dv_carry[...] = dvv[BLK:2 * BLK]

        @pl.when(n == nb)
        def _():
            dkv_ref[:, 0:KV_W] = dk_carry[...].astype(BF16)
            dkv_ref[:, KV_W:2 * KV_W] = dv_carry[...].astype(BF16)

    cur = lambda n: (jnp.minimum(n, nb - 1), 0)
    prev = lambda n: (jnp.clip(n - 1, 0, nb - 1), 0)
    return pl.pallas_call(
        body, name="attn_bwd", grid=(nb + 1,),
        in_specs=[pl.BlockSpec((BLK, D), cur),
                  pl.BlockSpec((BLK, KV_W), prev), pl.BlockSpec((BLK, KV_W), cur),
                  pl.BlockSpec((BLK, KV_W), lambda n: (jnp.clip(n - 1, 0, nb - 1), 1)),
                  pl.BlockSpec((BLK, KV_W), lambda n: (jnp.minimum(n, nb - 1), 1)),
                  pl.BlockSpec((BLK, D), cur), pl.BlockSpec((BLK, 128), cur), _table_spec(),
                  pl.BlockSpec(memory_space=pl.ANY)],
        out_specs=[pl.BlockSpec((BLK, D), cur), pl.BlockSpec((BLK, 2 * KV_W), prev),
                   pl.BlockSpec((N_KV, 4 * BLK, 4 * BLK), lambda n: (0, 0, 0))],
        out_shape=[SDS((s, 2 * D), BF16), SDS((s, 2 * KV_W), BF16), SDS((N_KV, 4 * BLK, 4 * BLK), F32)],
        scratch_shapes=[pltpu.VMEM((BLK, KV_W), F32), pltpu.VMEM((BLK, KV_W), F32)],
        input_output_aliases={8: 0},
        compiler_params=_params(("arbitrary",)),
    )(q, kv, kv, kv, kv, datt, stats, tab, dqz)


def _b_bwd(dqz, dkv, h1, dh2, oa, wbin_g, w_kv, g_kv, g_pre, g_apost, tm):
    s = h1.shape[0]
    nt = s // tm

    def body(dqz_ref, dkv_ref, h_ref, dh2_ref, oa_ref, wb_ref, wkv_ref, gk_ref, gb_ref, ga_ref,
             dh1_ref, doa_ref, dg_ref, dwb_ref, dwkv_ref, dwb16_ref, dwkv16_ref, wcat, dwb_acc, dwkv_acc):
        @pl.when(pl.program_id(0) == 0)
        def _():
            dg_ref[...] = jnp.zeros_like(dg_ref)
            dwb_acc[...] = jnp.zeros_like(dwb_acc)
            dwkv_acc[...] = jnp.zeros_like(dwkv_acc)
            for j in range(N_CHIPS):
                pltpu.sync_copy(wb_ref.at[j], wcat.at[:, pl.ds(BIN_COLS * j, BIN_COLS)])
        dnb = _nt(dqz_ref[...], wcat[...])
        dnk = _nt(dkv_ref[...], wkv_ref[...])
        h = h_ref[...]
        r = _rms_scale(h)
        hh = h * r
        dwb_acc[...] += _tn((hh * gb_ref[...]).astype(BF16), dqz_ref[...])
        dwkv_acc[...] += _tn((hh * gk_ref[...]).astype(BF16), dkv_ref[...])
        _acc_row(dg_ref, 0, jnp.sum(dnk * hh, axis=0, keepdims=True))
        _acc_row(dg_ref, 1, jnp.sum(dnb * hh, axis=0, keepdims=True))
        dhh = dnb * gb_ref[...] + dnk * gk_ref[...]
        dh1 = dh2_ref[...] + r * (dhh - hh * jnp.mean(dhh * hh, axis=-1, keepdims=True))
        dh1_ref[...] = dh1
        oa = oa_ref[...].astype(F32)
        ra = _rms_scale(oa)
        oh = oa * ra
        _acc_row(dg_ref, 2, jnp.sum(dh1 * oh, axis=0, keepdims=True))
        doh = dh1 * ga_ref[...]
        doa_ref[...] = (ra * (doh - oh * jnp.mean(doh * oh, axis=-1, keepdims=True))).astype(BF16)

        @pl.when(pl.program_id(0) == nt - 1)
        def _():
            wcat[...] = dwb_acc[...].astype(BF16)
            for j in range(N_CHIPS):
                pltpu.sync_copy(dwb_acc.at[:, pl.ds(BIN_COLS * j, BIN_COLS)], dwb_ref.at[j])
                pltpu.sync_copy(wcat.at[:, pl.ds(BIN_COLS * j, BIN_COLS)], dwb16_ref.at[j])
            pltpu.sync_copy(dwkv_acc, dwkv_ref)
            wcat[:, 0:2 * KV_W] = dwkv_acc[...].astype(BF16)
            pltpu.sync_copy(wcat.at[:, pl.ds(0, 2 * KV_W)], dwkv16_ref)

    row = lambda i: (i, 0)
    fix = lambda i: (0, 0)
    anyspace = pl.BlockSpec(memory_space=pl.ANY)
    return pl.pallas_call(
        body, name="b_bwd", grid=(nt,),
        in_specs=[pl.BlockSpec((tm, 2 * D), row), pl.BlockSpec((tm, 2 * KV_W), row), pl.BlockSpec((tm, D), row),
                  pl.BlockSpec((tm, D), row), pl.BlockSpec((tm, D), row), anyspace, pl.BlockSpec((D, 2 * KV_W), fix),
                  pl.BlockSpec((1, D), fix), pl.BlockSpec((1, D), fix), pl.BlockSpec((1, D), fix)],
        out_specs=[pl.BlockSpec((tm, D), row), pl.BlockSpec((tm, D), row), pl.BlockSpec((8, D), fix)] + [anyspace] * 4,
        out_shape=[SDS((s, D), F32), SDS((s, D), BF16), SDS((8, D), F32), SDS((N_CHIPS, D, BIN_COLS), F32),
                   SDS((D, 2 * KV_W), F32), SDS((N_CHIPS, D, BIN_COLS), BF16), SDS((D, 2 * KV_W), BF16)],
        scratch_shapes=[pltpu.VMEM((D, 2 * D), BF16), pltpu.VMEM((D, 2 * D), F32), pltpu.VMEM((D, 2 * KV_W), F32)],
        compiler_params=_params(("arbitrary",)),
    )(dqz, dkv, h1, dh2, oa, wbin_g, w_kv, g_kv, g_pre, g_apost)


def _to_owner_core(pieces, r, send, recv, core, action):
    x, y, c = lax.axis_index("x"), lax.axis_index("y"), lax.axis_index("c")
    for kp in range(N_CHIPS):
        px, py = kp >> 1, kp & 1
        rel = 4 * (x + px - 2 * x * px) + 2 * (y + py - 2 * y * py) + (c + core - 2 * c * core)

        @pl.when(rel != 0)
        def _():
            cp = pltpu.make_async_remote_copy(src_ref=pieces.at[kp], dst_ref=r.at[rel - 1], send_sem=send.at[kp],
                                              recv_sem=recv.at[rel - 1], device_id=(px, py, core), device_id_type=MESH)
            if action == "start":
                cp.start()
            else:
                cp.wait_send()
    if action == "wait":
        @pl.when(c == core)
        def _():
            for rel in range(1, N_DEV):
                pltpu.make_async_remote_copy(src_ref=pieces.at[0], dst_ref=r.at[rel - 1], send_sem=send.at[0],
                                             recv_sem=recv.at[rel - 1], device_id=(x, y, c),
                                             device_id_type=MESH).wait_recv()


def _owner_core_sems():
    return [pltpu.SemaphoreType.DMA((N_CHIPS,)), pltpu.SemaphoreType.DMA((N_DEV - 1,))]


def _device_exchange(grads, recvs, send, recv):
    x, y, c = lax.axis_index("x"), lax.axis_index("y"), lax.axis_index("c")
    copies = []
    for a, (g, r) in enumerate(zip(grads, recvs)):
        h = g.shape[1] // 2
        for rel in range(1, N_DEV):
            fx, fy, fc = rel >> 2, (rel >> 1) & 1, rel & 1
            px, py, pc = x + fx - 2 * x * fx, y + fy - 2 * y * fy, c + fc - 2 * c * fc
            sem = (N_DEV - 1) * a + rel - 1
            copies.append(pltpu.make_async_remote_copy(
                src_ref=g.at[2 * px + py, pl.ds(pl.multiple_of(pc * h, 16), h)], dst_ref=r.at[rel - 1],
                send_sem=send.at[sem], recv_sem=recv.at[sem], device_id=(px, py, pc), device_id_type=MESH))
    return copies


def _device_exchange_specs(grads):
    anyspace = pl.BlockSpec(memory_space=pl.ANY)
    n = len(grads)
    count = (N_DEV - 1) * n
    return ([anyspace] * n, [anyspace] * n,
            [SDS((N_DEV - 1, g.shape[1] // 2, g.shape[2]), g.dtype) for g in grads],
            [pltpu.SemaphoreType.DMA((count,)), pltpu.SemaphoreType.DMA((count,))])


def _a_bwd(doa, ya, proj, conv_w, w_out, tm, parts):
    s = doa.shape[0]
    nt = s // tm
    n = len(parts)
    ex_in, ex_out, ex_shape, ex_sems = _device_exchange_specs(parts)

    def body(*refs):
        doa_ref, ya_ref, proj_ref, halo_ref, cw_ref, w_ref = refs[:6]
        part_refs = refs[6:6 + n]
        dproj_ref, dcw_ref, dw_ref, dw16_ref = refs[6 + n:10 + n]
        recv_refs = refs[10 + n:10 + 2 * n]
        carry, dw_acc, stage, send, recv = refs[10 + 2 * n:]
        i = pl.program_id(0)
        r = nt - 1 - i

        @pl.when(i == 0)
        def _():
            dcw_ref[...] = jnp.zeros_like(dcw_ref)
            carry[...] = jnp.zeros_like(carry)
            dw_acc[...] = jnp.zeros_like(dw_acc)
            for cp in _device_exchange(part_refs, recv_refs, send, recv):
                cp.start()
        dya = _nt(doa_ref[...], w_ref[...])
        dw_acc[...] += _tn(ya_ref[...], doa_ref[...])
        bg = proj_ref[:, 0:D].astype(F32)
        cg = proj_ref[:, D:2 * D].astype(F32)
        u = proj_ref[:, 2 * D:3 * D].astype(F32)
        z = proj_ref[:, 3 * D:4 * D].astype(F32)
        v = cg * u
        before = jnp.where(r > 0, halo_ref[:, D:2 * D].astype(F32) * halo_ref[:, 2 * D:3 * D].astype(F32), 0.0)
        rows = lax.broadcasted_iota(jnp.int32, (tm, D), 0)
        v1, v2 = _shift_rows(v, before[HALO - 1:HALO, :], before[HALO - 2:HALO - 1, :], rows)
        conv = cw_ref[0:1, :] * v2 + cw_ref[1:2, :] * v1 + cw_ref[2:3, :] * v
        sg, sz = _silu_parts(z)
        dproj_ref[:, 0:D] = (dya * conv * sz).astype(BF16)
        dproj_ref[:, 3 * D:4 * D] = (dya * bg * conv * _dsilu(z, sg)).astype(BF16)
        dconv = dya * bg * sz
        _acc_row(dcw_ref, 0, jnp.sum(dconv * v2, axis=0, keepdims=True))
        _acc_row(dcw_ref, 1, jnp.sum(dconv * v1, axis=0, keepdims=True))
        _acc_row(dcw_ref, 2, jnp.sum(dconv * v, axis=0, keepdims=True))
        after = carry[...]
        up1 = jnp.where(rows < tm - 1, pltpu.roll(dconv, tm - 1, 0), after[0:1, :])
        up2 = jnp.where(rows < tm - 2, pltpu.roll(dconv, tm - 2, 0),
                        jnp.where(rows == tm - 2, after[0:1, :], after[1:2, :]))
        carry[...] = dconv[0:8, :]
        dv = cw_ref[2:3, :] * dconv + cw_ref[1:2, :] * up1 + cw_ref[0:1, :] * up2
        dproj_ref[:, D:2 * D] = (dv * u).astype(BF16)
        dproj_ref[:, 2 * D:3 * D] = (dv * cg).astype(BF16)

        @pl.when(i == nt - 1)
        def _():
            _write_gradient(dw_acc, dw_ref, dw16_ref, stage)
            for cp in _device_exchange(part_refs, recv_refs, send, recv):
                cp.wait()

    rev = lambda i: (nt - 1 - i, 0)
    fix = lambda i: (0, 0)
    halo = lambda i: (jnp.maximum((nt - 1 - i) * (tm // HALO) - 1, 0), 0)
    anyspace = pl.BlockSpec(memory_space=pl.ANY)
    dproj, dcw, dw, dw16, *got = pl.pallas_call(
        body, name="a_bwd", grid=(nt,),
        in_specs=[pl.BlockSpec((tm, D), rev), pl.BlockSpec((tm, D), rev), pl.BlockSpec((tm, 4 * D), rev),
                  pl.BlockSpec((HALO, 4 * D), halo), pl.BlockSpec((8, D), fix), pl.BlockSpec((D, D), fix)] + ex_in,
        out_specs=[pl.BlockSpec((tm, 4 * D), rev), pl.BlockSpec((8, D), fix), anyspace, anyspace] + ex_out,
        out_shape=[SDS((s, 4 * D), BF16), SDS((8, D), F32), SDS((D, D), F32), SDS((D, D), BF16)] + ex_shape,
        scratch_shapes=[pltpu.VMEM((8, D), F32), pltpu.VMEM((D, D), F32), pltpu.VMEM((D // 4, D), BF16)] + ex_sems,
        compiler_params=_params(("arbitrary",)),
    )(doa, ya, proj, proj, conv_w, w_out, *parts)
    return dproj, dcw, dw, dw16, got


def _dn1(dp_ref, w_ref):
    dn = _nt(dp_ref[:, 0:D], w_ref[0])
    for j in range(1, 4):
        dn = dn + _nt(dp_ref[:, D * j:D * (j + 1)], w_ref[j])
    return dn


def _a_in_bwd_matmul(dproj, win_g, tm, count, win_half, win_got, outa):
    ex_in, ex_out, ex_shape, ex_sems = _device_exchange_specs([outa])

    def body(dp_ref, w_ref, half_ref, got_in, outa_ref, dn_ref, got_ref, outa_got, wcat, hsend, hrecv, osend, orecv):
        del got_in

        @pl.when(pl.program_id(0) == 0)
        def _():
            _to_owner_core(half_ref, got_ref, hsend, hrecv, 1, "start")
            for cp in _device_exchange([outa_ref], [outa_got], osend, orecv):
                cp.start()
            for j in range(N_CHIPS):
                pltpu.sync_copy(w_ref.at[j], wcat.at[:, pl.ds(D * j, D)])
        dn_ref[...] = _nt(dp_ref[...], wcat[...]).astype(BF16)

        @pl.when(pl.program_id(0) == count - 1)
        def _():
            _to_owner_core(half_ref, got_ref, hsend, hrecv, 1, "wait")
            for cp in _device_exchange([outa_ref], [outa_got], osend, orecv):
                cp.wait()

    row = lambda i: (i, 0)
    anyspace = pl.BlockSpec(memory_space=pl.ANY)
    dn, got, outa_got = pl.pallas_call(
        body, name="a_in_bwd_matmul", grid=(count,),
        in_specs=[pl.BlockSpec((tm, 4 * D), row), anyspace, anyspace, anyspace] + ex_in,
        out_specs=[pl.BlockSpec((tm, D), row), anyspace] + ex_out,
        out_shape=[SDS((count * tm, D), BF16), SDS(win_got.shape, win_got.dtype)] + ex_shape,
        scratch_shapes=[pltpu.VMEM((D, 4 * D), BF16)] + _owner_core_sems() + ex_sems,
        input_output_aliases={3: 1},
        compiler_params=_params(("arbitrary",)),
    )(dproj, win_g, win_half, win_got, outa)
    return dn, got, outa_got


def _a_in_bwd(dn_first, dproj, x, dh1, win_g, g_pre, tm):
    s = x.shape[0]
    nt = s // tm
    count = dn_first.shape[0] // tm

    def body(dn_ref, dp_ref, x_ref, dh_ref, w_ref, g_ref, gx_ref, dg_ref, dn_s):
        i = pl.program_id(0)

        @pl.when(i == 0)
        def _():
            dg_ref[...] = jnp.zeros_like(dg_ref)

        @pl.when(i < count)
        def _():
            dn_s[...] = dn_ref[...].astype(F32)

        @pl.when(i >= count)
        def _():
            dn_s[...] = _dn1(dp_ref, w_ref)
        dn = dn_s[...]
        xv = x_ref[...]
        r = _rms_scale(xv)
        xh = xv * r
        _acc_row(dg_ref, 0, jnp.sum(dn * xh, axis=0, keepdims=True))
        dxh = dn * g_ref[...]
        gx_ref[...] = dh_ref[...] + r * (dxh - xh * jnp.mean(dxh * xh, axis=-1, keepdims=True))

    row = lambda i: (i, 0)
    fix = lambda i: (0, 0)
    return pl.pallas_call(
        body, name="a_in_bwd", grid=(nt,),
        in_specs=[pl.BlockSpec((tm, D), lambda i: (jnp.minimum(i, count - 1), 0)),
                  pl.BlockSpec((tm, 4 * D), lambda i: (jnp.maximum(i, count), 0)),
                  pl.BlockSpec((tm, D), row), pl.BlockSpec((tm, D), row),
                  pl.BlockSpec((4, D, D), lambda i: (0, 0, 0)), pl.BlockSpec((1, D), fix)],
        out_specs=[pl.BlockSpec((tm, D), row), pl.BlockSpec((8, D), fix)],
        out_shape=[SDS((s, D), F32), SDS((8, D), F32)],
        scratch_shapes=[pltpu.VMEM((tm, D), F32)],
        compiler_params=_params(("arbitrary",)),
    )(dn_first, dproj, x, dh1, win_g, g_pre)


def _dw_in_half(n1, dproj, core, tmw, name, carried=None):
    s = n1.shape[0]
    h = D // 2
    nt = s // tmw
    rides = carried is not None

    def body(*refs):
        a_ref, b_ref = refs[:2]
        o_ref, o16_ref = refs[2 + rides:4 + rides]
        j, t = pl.program_id(0), pl.program_id(1)
        if rides:
            sent, got, send, recv = refs[2], refs[4 + rides], refs[5 + rides], refs[6 + rides]

            @pl.when((j == 0) & (t == 0))
            def _():
                _to_owner_core(sent, got, send, recv, 1 - core, "start")

        @pl.when(t == 0)
        def _():
            o_ref[...] = jnp.zeros_like(o_ref)
        o_ref[0] += _tn(a_ref[...], b_ref[...])

        @pl.when(t == nt - 1)
        def _():
            o16_ref[...] = o_ref[...].astype(BF16)
        if rides:
            @pl.when((j == N_CHIPS - 1) & (t == nt - 1))
            def _():
                _to_owner_core(sent, got, send, recv, 1 - core, "wait")

    anyspace = pl.BlockSpec(memory_space=pl.ANY)
    slot = pl.BlockSpec((1, h, D), lambda j, t: (j, 0, 0))
    outs = pl.pallas_call(
        body, name=name, grid=(N_CHIPS, nt),
        in_specs=[pl.BlockSpec((tmw, h), lambda j, t: (t, core)), pl.BlockSpec((tmw, D), lambda j, t: (t, j))]
        + [anyspace] * rides,
        out_specs=[slot, slot] + [anyspace] * rides,
        out_shape=[SDS((N_CHIPS, h, D), F32), SDS((N_CHIPS, h, D), BF16)]
        + [SDS((N_DEV - 1, h, D), BF16)] * rides,
        scratch_shapes=_owner_core_sems() if rides else [],
        compiler_params=_params(("arbitrary", "arbitrary")),
    )(n1, dproj, *([carried] if rides else []))
    return outs


def _sibling_exchange(name, to_sibling=(), shards=(), smalls=()):
    n_g, n_h, n_s = len(to_sibling), len(shards), len(smalls)

    def body(*refs):
        gs = refs[:n_g]
        pos = n_g + n_h
        small_ins = refs[pos:pos + n_s]
        pos += n_s
        rs, fs = refs[pos:pos + n_g], refs[pos + n_g:pos + n_g + n_h]
        pos += n_g + n_h
        small_alls = refs[pos:pos + n_s]
        pos += n_s
        dsend, drecv, ssend, srecv = refs[pos:]
        x, y, c = lax.axis_index("x"), lax.axis_index("y"), lax.axis_index("c")
        sibling = (x, y, 1 - c)
        sends, arrivals = [], []
        for a, (g, r) in enumerate(zip(gs, rs)):
            h = g.shape[1] // 2
            src = g.at[:, pl.ds(pl.multiple_of((1 - c) * h, 8), h), :]
            sends.append(pltpu.make_async_remote_copy(src_ref=src, dst_ref=r, send_sem=dsend.at[a], recv_sem=drecv.at[a],
                                                      device_id=sibling, device_id_type=MESH))
            arrivals.append(pltpu.make_async_remote_copy(src_ref=r, dst_ref=r, send_sem=dsend.at[a], recv_sem=drecv.at[a],
                                                         device_id=sibling, device_id_type=MESH))
        for b, full in enumerate(fs):
            h = full.shape[0] // 2
            mine = full.at[pl.ds(pl.multiple_of(c * h, 8), h)]
            theirs = full.at[pl.ds(pl.multiple_of((1 - c) * h, 8), h)]
            sends.append(pltpu.make_async_remote_copy(src_ref=mine, dst_ref=mine, send_sem=dsend.at[n_g + b],
                                                      recv_sem=drecv.at[n_g + b], device_id=sibling, device_id_type=MESH))
            arrivals.append(pltpu.make_async_remote_copy(src_ref=mine, dst_ref=theirs, send_sem=dsend.at[n_g + b],
                                                         recv_sem=drecv.at[n_g + b], device_id=sibling, device_id_type=MESH))
        me = 4 * x + 2 * y + c
        for k, (small_in, small_all) in enumerate(zip(small_ins, small_alls)):
            small_all[me] = small_in[...]
            for rel in range(1, N_DEV):
                fx, fy, fc = rel >> 2, (rel >> 1) & 1, rel & 1
                peer = (x + fx - 2 * x * fx, y + fy - 2 * y * fy, c + fc - 2 * c * fc)
                sender = 4 * peer[0] + 2 * peer[1] + peer[2]
                sem = (N_DEV - 1) * k + rel - 1
                sends.append(pltpu.make_async_remote_copy(
                    src_ref=small_in, dst_ref=small_all.at[me], send_sem=ssend.at[sem], recv_sem=srecv.at[sem],
                    device_id=peer, device_id_type=MESH))
                arrivals.append(pltpu.make_async_remote_copy(
                    src_ref=small_in, dst_ref=small_all.at[sender], send_sem=ssend.at[sem], recv_sem=srecv.at[sem],
                    device_id=peer, device_id_type=MESH))
        for cp in sends:
            cp.start()
        for cp in arrivals:
            cp.wait_recv()
        for cp in sends:
            cp.wait_send()

    anyspace = pl.BlockSpec(memory_space=pl.ANY)
    vm = pl.BlockSpec(memory_space=pltpu.VMEM)
    out_shape = [SDS((N_CHIPS, g.shape[1] // 2, g.shape[2]), F32) for g in to_sibling]
    out_shape += [SDS(full.shape, F32) for full in shards]
    out_shape += [SDS((N_DEV,) + sm.shape, F32) for sm in smalls]
    n_d2d = max(n_g + n_h, 1)
    n_all = (N_DEV - 1) * max(n_s, 1)
    outs = pl.pallas_call(
        body, name=name, out_shape=out_shape,
        in_specs=[anyspace] * (n_g + n_h) + [vm] * n_s, out_specs=[anyspace] * (n_g + n_h) + [vm] * n_s,
        scratch_shapes=[pltpu.SemaphoreType.DMA((n_d2d,)), pltpu.SemaphoreType.DMA((n_d2d,)),
                        pltpu.SemaphoreType.DMA((n_all,)), pltpu.SemaphoreType.DMA((n_all,))],
        input_output_aliases={n_g + b: n_g + b for b in range(n_h)},
    )(*to_sibling, *shards, *smalls)
    return outs[:n_g], outs[n_g:n_g + n_h], outs[n_g + n_h:]


def _add_win(where, lo, hi, r, name):
    _, h, cols = lo.shape
    tr = min(h, 256)
    nh = h // tr

    def body(where_ref, lo_ref, hi_ref, r_ref, o_ref):
        acc = jnp.where(where_ref[0] == 0, lo_ref[0], hi_ref[0])
        for k in range(N_DEV - 1):
            acc = acc + r_ref[k].astype(F32)
        o_ref[...] = acc

    own = pl.BlockSpec((1, tr, cols), lambda i, w: (w[1], i, 0))
    return pl.pallas_call(
        body, name=name,
        grid_spec=pltpu.PrefetchScalarGridSpec(
            num_scalar_prefetch=1, grid=(nh,),
            in_specs=[own, own, pl.BlockSpec((N_DEV - 1, tr, cols), lambda i, w: (0, i, 0))],
            out_specs=pl.BlockSpec((tr, cols), lambda i, w: (w[0] * nh + i, 0))),
        out_shape=SDS((2 * h, cols), F32),
        compiler_params=_params(("parallel",)),
    )(where, lo, hi, r)


def _add_devices(where, g, r, name):
    _, rows, cols = g.shape
    h = rows // 2
    tr = min(h, 256)
    nh = h // tr

    def body(where_ref, g_ref, r_ref, o_ref):
        del where_ref
        acc = g_ref[0]
        for k in range(N_DEV - 1):
            acc = acc + r_ref[k].astype(F32)
        o_ref[...] = acc

    return pl.pallas_call(
        body, name=name,
        grid_spec=pltpu.PrefetchScalarGridSpec(
            num_scalar_prefetch=1, grid=(nh,),
            in_specs=[pl.BlockSpec((1, tr, cols), lambda i, w: (w[1], w[0] * nh + i, 0)),
                      pl.BlockSpec((N_DEV - 1, tr, cols), lambda i, w: (0, i, 0))],
            out_specs=pl.BlockSpec((tr, cols), lambda i, w: (w[0] * nh + i, 0))),
        out_shape=SDS((rows, cols), F32),
        compiler_params=_params(("parallel",)),
    )(where, g, r)


def _sum_smalls(gathered):
    n = len(gathered)

    def body(*refs):
        for all_ref, o_ref in zip(refs[:n], refs[n:]):
            acc = all_ref[0]
            for dev in range(1, N_DEV):
                acc = acc + all_ref[dev]
            o_ref[...] = acc

    vm = pl.BlockSpec(memory_space=pltpu.VMEM)
    return pl.pallas_call(
        body, name="sum_smalls", out_shape=[SDS(a.shape[1:], F32) for a in gathered],
        in_specs=[vm] * n, out_specs=[vm] * n,
    )(*gathered)


def _adam_step(g, w, m, v):
    nm = ADAM_B1 * m + (1.0 - ADAM_B1) * g
    nv = ADAM_B2 * v + (1.0 - ADAM_B2) * (g * g)
    m_hat = nm / (1.0 - ADAM_B1 ** ADAM_STEP)
    v_hat = nv / (1.0 - ADAM_B2 ** ADAM_STEP)
    return -ADAM_LR * (m_hat / (jnp.sqrt(v_hat) + ADAM_EPS) + ADAM_WD * w), nm, nv


def _adamw(g, w, m, v, name):
    rows, cols = g.shape
    tr = min(rows, 256)

    def body(g_ref, w_ref, m_ref, v_ref, d_ref, nm_ref, nv_ref):
        d_ref[...], nm_ref[...], nv_ref[...] = _adam_step(g_ref[...], w_ref[...], m_ref[...], v_ref[...])

    spec = pl.BlockSpec((tr, cols), lambda i: (i, 0))
    return pl.pallas_call(
        body, name=name, grid=(rows // tr,), in_specs=[spec] * 4, out_specs=[spec] * 3,
        out_shape=[SDS(g.shape, F32)] * 3, compiler_params=_params(("parallel",)),
    )(g, w, m, v)


def _small_update(chip, tot, tot_rel, wmv):
    names = list(SMALL_PLACES)
    n = len(names)

    def body(chip_ref, tot_ref, quarter_ref, rel_ref, *refs):
        del chip_ref
        ins, outs = refs[:3 * n], refs[3 * n:]
        for i, nm in enumerate(names):
            source, row, (rows, cols) = SMALL_PLACES[nm]
            g = {"rows": tot_ref, "quarter": quarter_ref, "rel": rel_ref}[source][row:row + rows, 0:cols]
            outs[4 * i][...] = g
            outs[4 * i + 1][...], outs[4 * i + 2][...], outs[4 * i + 3][...] = _adam_step(
                g, ins[3 * i][...], ins[3 * i + 1][...], ins[3 * i + 2][...])

    whole = lambda shape: pl.BlockSpec(shape, lambda i, c: (0,) * len(shape))
    shapes = [SMALL_PLACES[nm][2] for nm in names]
    outs = pl.pallas_call(
        body, name="small_update",
        grid_spec=pltpu.PrefetchScalarGridSpec(
            num_scalar_prefetch=1, grid=(1,),
            in_specs=[whole(tot.shape), pl.BlockSpec((tot.shape[0], D // 4), lambda i, c: (0, c[0])),
                      whole(tot_rel.shape)] + [whole(shp) for shp in shapes for _ in range(3)],
            out_specs=[whole(shp) for shp in shapes for _ in range(4)]),
        out_shape=[SDS(shp, F32) for shp in shapes for _ in range(4)],
    )(chip, tot, tot, tot_rel, *[a for nm in names for a in wmv[nm]])
    return {nm: tuple(outs[4 * i:4 * i + 4]) for i, nm in enumerate(names)}


def _pad_rows(a, rows):
    return jnp.concatenate([a, jnp.zeros((rows - a.shape[0], a.shape[1]), a.dtype)], axis=0)


def _pad_cols(a, cols):
    return jnp.concatenate([a, jnp.zeros((a.shape[0], cols - a.shape[1]), a.dtype)], axis=1)


def kernel(x, a_pre_norm, a_w_in, a_conv_w, a_w_out, a_post_norm, kv_norm, w_kv, rel_bias, b_pre_norm, b_w_in, b_sinks, b_w_out, b_post_norm, loss_target, m_a_pre_norm, m_a_w_in, m_a_conv_w, m_a_w_out, m_a_post_norm, m_kv_norm, m_w_kv, m_rel_bias, m_b_pre_norm, m_b_w_in, m_b_sinks, m_b_w_out, m_b_post_norm, v_a_pre_norm, v_a_w_in, v_a_conv_w, v_a_w_out, v_a_post_norm, v_kv_norm, v_w_kv, v_rel_bias, v_b_pre_norm, v_b_w_in, v_b_sinks, v_b_w_out, v_b_post_norm):
    seq = x.shape[1]
    xs = x.reshape(seq, D)
    tgt = loss_target.reshape(seq, D)
    chip = 2 * lax.axis_index("x") + lax.axis_index("y")
    core = lax.axis_index("c")
    tm = _tile(seq, 512)
    tmw = _tile(seq, 1024)

    shards = [a_w_in[0], a_w_out[0], w_kv, b_w_in[0], b_w_out[0]]
    small_w = _pad_rows(jnp.concatenate([a_pre_norm, a_conv_w[0], a_post_norm], axis=0), 8)
    *own_only, small_g = _prepare_weights(shards, small_w)
    where = jnp.stack([core, chip]).astype(jnp.int32)
    small_full = small_g.transpose(1, 0, 2).reshape(8, D)
    g_apre, conv_w, g_apost = small_full[0:1], _pad_rows(small_full[1:4], 8), small_full[4:5]
    g_kv = kv_norm.reshape(1, D)

    proj, n1, (win_g, wouta_g, wkv_g, wbin_g, woutb_g) = _a_in(where[1:2], xs, g_apre, own_only, tmw)
    wouta = wouta_g.reshape(D, D)
    wkv = wkv_g.reshape(D, 2 * KV_W)
    woutb = woutb_g.reshape(D, D)
    ya, oa, h1 = _a_mix(proj, xs, conv_w, wouta, g_apost, tm)
    kv, q, zb = _b_in(h1, g_kv, b_pre_norm, wkv, wbin_g, tmw)
    tab = _bias_table(rel_bias, b_sinks.reshape(N_HEADS))
    att, stats = _attn_fwd(q, kv, tab)
    dh2, dqz, datt, loss_acc, dg_bpost, dw_outb, dw_outb16 = _mid(att, zb, h1, tgt, woutb, b_post_norm, tm)

    dqz, dkv, dtab = _attn_bwd(q, kv, datt, stats, tab, dqz)
    dh1, doa, dg_b, dw_bin, dw_kv, dw_bin16, dw_kv16 = _b_bwd(dqz, dkv, h1, dh2, oa, wbin_g, wkv, g_kv, b_pre_norm,
                                                              g_apost, tm)
    by_chip = lambda a, cols: a.reshape(N_CHIPS, D // 4, cols)
    grads1 = [by_chip(dw_kv, 2 * KV_W), dw_bin, by_chip(dw_outb, D)]
    sent1 = [by_chip(dw_kv16, 2 * KV_W), dw_bin16, by_chip(dw_outb16, D)]
    names1 = ["w_kv", "b_w_in", "b_w_out"]
    dproj, dconv_w, dw_outa, dw_outa16, from_devices1 = _a_bwd(doa, ya, proj, conv_w, wouta, tm, sent1)
    shards1 = [_add_devices(where, g, r, "add_devices_" + nm) for g, r, nm in zip(grads1, from_devices1, names1)]
    tmw2 = _tile(seq, 2048)
    win_lo, win_lo16 = _dw_in_half(n1, dproj, 0, tmw2, "dw_a_in_lo")
    win_hi, win_hi16, win_got = _dw_in_half(n1, dproj, 1, tmw2, "dw_a_in_hi", carried=win_lo16)
    nt = seq // tmw
    dn_first, win_got, outa_got = _a_in_bwd_matmul(dproj, win_g, tmw, max(nt - max(nt // 4, 1), 1), win_hi16, win_got,
                                                   by_chip(dw_outa16, D))
    grad_x, dg_apre = _a_in_bwd(dn_first, dproj, xs, dh1, win_g, g_apre, tm)
    shards2 = [_add_win(where, win_lo, win_hi, win_got, "add_devices_a_w_in"),
               _add_devices(where, by_chip(dw_outa, D), outa_got, "add_devices_a_w_out")]
    drel, dsink = _bias_fold(dtab)

    smalls = jnp.concatenate([
        dg_apre[0:1], dg_b[2:3], dg_b[0:1], dg_b[1:2], dg_bpost[0:1], _pad_cols(dsink[0:1], D),
        _pad_cols(loss_acc[0:1], D), jnp.zeros((1, D), F32), dconv_w], axis=0)
    assert smalls.shape == (SMALL_ROWS, D)
    _, (g_wkv, g_wbin, g_woutb, g_win, g_wouta), gathered = _sibling_exchange(
        "share_last", shards=shards1 + shards2, smalls=(smalls, drel))
    tot, tot_rel = _sum_smalls(gathered)

    big = {}
    for nm, g, w, m, v in [("a_w_in", g_win, a_w_in, m_a_w_in, v_a_w_in), ("a_w_out", g_wouta, a_w_out, m_a_w_out, v_a_w_out),
                           ("w_kv", g_wkv, w_kv, m_w_kv, v_w_kv), ("b_w_in", g_wbin, b_w_in, m_b_w_in, v_b_w_in),
                           ("b_w_out", g_woutb, b_w_out, m_b_w_out, v_b_w_out)]:
        shp = w.shape
        two = (shp[-2], shp[-1])
        d, nm_, nv_ = _adamw(g, w.reshape(two), m.reshape(two), v.reshape(two), "adamw_" + nm)
        big[nm] = (g.reshape(shp), d.reshape(shp), nm_.reshape(shp), nv_.reshape(shp))

    given = {"a_pre_norm": (a_pre_norm, m_a_pre_norm, v_a_pre_norm), "a_conv_w": (a_conv_w, m_a_conv_w, v_a_conv_w),
             "a_post_norm": (a_post_norm, m_a_post_norm, v_a_post_norm), "kv_norm": (kv_norm, m_kv_norm, v_kv_norm),
             "rel_bias": (rel_bias, m_rel_bias, v_rel_bias), "b_pre_norm": (b_pre_norm, m_b_pre_norm, v_b_pre_norm),
             "b_sinks": (b_sinks, m_b_sinks, v_b_sinks), "b_post_norm": (b_post_norm, m_b_post_norm, v_b_post_norm)}
    small = _small_update(where[1:2], tot, tot_rel, {nm: tuple(a.reshape(SMALL_PLACES[nm][2]) for a in wmv)
                                            for nm, wmv in given.items()})
    order = ["a_pre_norm", "a_w_in", "a_conv_w", "a_w_out", "a_post_norm", "kv_norm", "w_kv", "rel_bias",
             "b_pre_norm", "b_w_in", "b_sinks", "b_w_out", "b_post_norm"]
    outs = []
    for which in range(4):
        for nm in order:
            outs.append(big[nm][which] if nm in big else small[nm][which].reshape(given[nm][0].shape))
    loss = 0.5 * tot[LOSS_ROW, 0]
    return (loss, grad_x.reshape(x.shape), *outs)
```

```python
import math

import jax
import jax.numpy as jnp
from jax import lax
from jax.experimental import pallas as pl
from jax.experimental.pallas import tpu as pltpu

F32 = jnp.float32
BF16 = jnp.bfloat16
MESH = pl.DeviceIdType.MESH
SDS = jax.ShapeDtypeStruct

D = 1024
HEAD_DIM = 64
N_HEADS = 16
N_KV = 2
GROUP = 8
KV_W = 128
BLK = 128
N_BUCKETS = 32
MAX_EXACT = 16
MAX_DISTANCE = 128
EPS = 1e-6
NEG_INF = -1e30
Q_SCALE = HEAD_DIM ** -0.5

ADAM_LR = 0.001
ADAM_B1 = 0.9
ADAM_B2 = 0.999
ADAM_EPS = 1e-08
ADAM_WD = 0.01
ADAM_STEP = 10

N_CHIPS = 4
N_DEV = 8
BIN_COLS = 2 * D // N_CHIPS
VMEM_LIMIT = 56 * 1024 * 1024
SMALL_ROWS = 16
LOSS_ROW = 6
SMALL_PLACES = {
    "a_pre_norm": ("quarter", 0, (1, D // 4)), "a_conv_w": ("quarter", 8, (3, D // 4)),
    "a_post_norm": ("quarter", 1, (1, D // 4)), "kv_norm": ("rows", 2, (1, D)),
    "rel_bias": ("rel", 0, (N_BUCKETS, N_HEADS)), "b_pre_norm": ("rows", 3, (1, D)),
    "b_sinks": ("rows", 5, (1, N_HEADS)), "b_post_norm": ("rows", 4, (1, D)),
}
HALO = 16


def _bucket_thresholds():
    def bucket(d):
        big = MAX_EXACT + int(math.log(d / MAX_EXACT) / math.log(MAX_DISTANCE / MAX_EXACT)
                              * (N_BUCKETS - MAX_EXACT))
        return d if d < MAX_EXACT else min(big, N_BUCKETS - 1)
    out = []
    for b in range(MAX_EXACT + 1, N_BUCKETS):
        out.append(min(d for d in range(MAX_EXACT, MAX_DISTANCE) if bucket(d) >= b))
    return tuple(out)


BUCKET_THRESHOLDS = _bucket_thresholds()


def _params(semantics=None, vmem=VMEM_LIMIT):
    return pltpu.CompilerParams(dimension_semantics=semantics, vmem_limit_bytes=vmem)


def _tile(n, pref):
    return pref if n >= 2 * pref else max(n // 2, 8)


def _rms_scale(v):
    return lax.rsqrt(jnp.mean(v * v, axis=-1, keepdims=True) + EPS)


def _nt(a, b):
    return lax.dot_general(a, b, (((1,), (1,)), ((), ())), preferred_element_type=F32)


def _tn(a, b):
    return lax.dot_general(a, b, (((0,), (0,)), ((), ())), preferred_element_type=F32)


def _nn(a, b):
    return jnp.dot(a, b, preferred_element_type=F32)


def _silu_parts(z):
    sg = jax.nn.sigmoid(z)
    return sg, z * sg


def _dsilu(z, sg):
    return sg * (1.0 + z * (1.0 - sg))


def _write_gradient(acc, out32, out16, stage):
    pltpu.sync_copy(acc, out32)
    rows = stage.shape[0]
    for k in range(acc.shape[0] // rows):
        stage[...] = acc[rows * k:rows * (k + 1), :].astype(BF16)
        pltpu.sync_copy(stage, out16.at[pl.ds(rows * k, rows)])


def _acc_row(ref, row, val):
    ref[row:row + 1, :] += val


def _gather_copies(outs, splits, ici_send, ici_recv, d2d_send, d2d_recv):
    x, y, c = lax.axis_index("x"), lax.axis_index("y"), lax.axis_index("c")
    k = 2 * x + y
    sibling = (x, y, 1 - c)

    def part(o_ref, chip, core, split):
        if not split:
            return o_ref.at[chip]
        h = o_ref.shape[1] // 2
        return o_ref.at[chip, pl.ds(pl.multiple_of(core * h, 16), h)]

    def remote(ref, a, j, sems, to):
        return pltpu.make_async_remote_copy(src_ref=ref, dst_ref=ref, send_sem=sems[0].at[3 * a + j],
                                            recv_sem=sems[1].at[3 * a + j], device_id=to, device_id_type=MESH)

    copies = []
    for a, (o_ref, split) in enumerate(zip(outs, splits)):
        for j, (px, py) in enumerate([(x, 1 - y), (1 - x, y), (1 - x, 1 - y)]):
            kj = 2 * px + py
            ici, d2d = (ici_send, ici_recv), (d2d_send, d2d_recv)
            copies.append((remote(part(o_ref, k, c, split), a, j, ici, (px, py, c)),
                           remote(part(o_ref, kj, c, split), a, j, ici, (px, py, c)),
                           remote(part(o_ref, kj, c, split), a, j, d2d, sibling) if split else None,
                           remote(part(o_ref, kj, 1 - c, split), a, j, d2d, sibling) if split else None))
    return copies


def _gather_sems(n):
    return [pltpu.SemaphoreType.DMA((3 * n,)) for _ in range(4)]


def _prepare_weights(shards, small):
    n = len(shards)

    def body(*refs):
        ins, small_in = refs[:n], refs[n]
        outs, small_out = refs[n + 1:2 * n + 1], refs[2 * n + 1]
        stages, put_sem = refs[2 * n + 2:3 * n + 2], refs[3 * n + 2]
        sems = refs[3 * n + 3:]
        k = 2 * lax.axis_index("x") + lax.axis_index("y")
        puts = []
        for a, (i_ref, stage, o_ref) in enumerate(zip(ins, stages, outs)):
            stage[...] = i_ref[...].astype(BF16)
            puts.append(pltpu.make_async_copy(stage, o_ref.at[k], put_sem.at[a]))
            puts[-1].start()
        small_out[k] = small_in[...]
        copies = _gather_copies([small_out], [False], *sems)
        for send, _, _, _ in copies:
            send.start()
        for _, arrival, _, _ in copies:
            arrival.wait_recv()
        for send, _, _, _ in copies:
            send.wait_send()
        for put in puts:
            put.wait()

    vm = pl.BlockSpec(memory_space=pltpu.VMEM)
    anyspace = pl.BlockSpec(memory_space=pl.ANY)
    out_shape = [SDS((N_CHIPS,) + s.shape, BF16) for s in shards] + [SDS((N_CHIPS,) + small.shape, F32)]
    return pl.pallas_call(
        body, name="prepare_weights", out_shape=out_shape,
        in_specs=[vm] * (n + 1), out_specs=[anyspace] * n + [vm],
        scratch_shapes=[pltpu.VMEM(s.shape, BF16) for s in shards] + [pltpu.SemaphoreType.DMA((n,))] + _gather_sems(1),
        compiler_params=pltpu.CompilerParams(vmem_limit_bytes=VMEM_LIMIT),
    )(*shards, small)


def _a_in(chip, x, g_pre, weights, tm):
    s = x.shape[0]
    nt = s // tm
    n = len(weights)

    def body(chip_ref, x_ref, g_ref, *refs):
        proj_ref, n1_ref = refs[n:n + 2]
        gathered = refs[n + 2:2 * n + 2]
        wbuf, n1_all, fetch_sem = refs[2 * n + 2:2 * n + 5]
        sems = refs[2 * n + 5:]
        jj, i = pl.program_id(0), pl.program_id(1)
        copies = _gather_copies(gathered, [True] * n, *sems)

        def fetch(rel):
            slot = jnp.bitwise_xor(chip_ref[0], rel)
            return pltpu.make_async_copy(gathered[0].at[slot], wbuf.at[rel % 2], fetch_sem.at[rel % 2])

        @pl.when((jj == 0) & (i == 0))
        def _():
            fetch(0).start()
            copies[0][0].start()
            copies[1][0].start()
            fetch(0).wait()

        for rel in (1, 2, 3):
            @pl.when((jj == rel) & (i == 0))
            def _():
                fetch(rel).wait()

        @pl.when(jj == 0)
        def _():
            xv = x_ref[...]
            n1 = (xv * _rms_scale(xv) * g_ref[...]).astype(BF16)
            n1_ref[...] = n1
            n1_all[i] = n1
        proj_ref[...] = _nn(n1_all[i], wbuf[jj % 2]).astype(BF16)

        for rel in (1, 2, 3):
            @pl.when((jj == rel - 1) & (i == max(nt - 2, nt // 2)))
            def _():
                _, arrival, forward, forwarded = copies[rel - 1]
                arrival.wait_recv()
                forward.start()
                forwarded.wait_recv()
                fetch(rel).start()
                if rel == 1:
                    for send, _, _, _ in copies[2:]:
                        send.start()

        @pl.when((jj == 3) & (i == max(nt - 2, 0)))
        def _():
            for _, arrival, forward, _ in copies[3:]:
                arrival.wait_recv()
                forward.start()

        @pl.when((jj == 3) & (i == nt - 1))
        def _():
            for _, _, _, forwarded in copies[3:]:
                forwarded.wait_recv()
            for send, _, forward, _ in copies:
                forward.wait_send()
                send.wait_send()

    anyspace = pl.BlockSpec(memory_space=pl.ANY)
    proj, n1, *gathered = pl.pallas_call(
        body, name="a_in",
        grid_spec=pltpu.PrefetchScalarGridSpec(
            num_scalar_prefetch=1, grid=(4, nt),
            in_specs=[pl.BlockSpec((tm, D), lambda jj, i, c: (jnp.where(jj == 0, i, nt - 1), 0)),
                      pl.BlockSpec((1, D), lambda jj, i, c: (0, 0))] + [anyspace] * n,
            out_specs=[pl.BlockSpec((tm, D), lambda jj, i, c: (i, jnp.bitwise_xor(c[0], jj))),
                       pl.BlockSpec((tm, D), lambda jj, i, c: (jnp.where(jj == 0, i, nt - 1), 0))] + [anyspace] * n,
            scratch_shapes=[pltpu.VMEM((2, D, D), BF16), pltpu.VMEM((nt, tm, D), BF16),
                            pltpu.SemaphoreType.DMA((2,))] + _gather_sems(n)),
        out_shape=[SDS((s, 4 * D), BF16), SDS((s, D), BF16)] + [SDS(w.shape, w.dtype) for w in weights],
        input_output_aliases={3 + a: 2 + a for a in range(n)},
        compiler_params=_params(("arbitrary", "arbitrary")),
    )(chip, x, g_pre, *weights)
    return proj, n1, gathered


def _shift_rows(v, last, second_last, rows):
    v1 = jnp.where(rows >= 1, pltpu.roll(v, 1, 0), last)
    v2 = jnp.where(rows >= 2, pltpu.roll(v, 2, 0), jnp.where(rows == 1, last, second_last))
    return v1, v2


def _a_mix(proj, x, conv_w, w_out, g_post, tm):
    s = x.shape[0]

    def body(proj_ref, x_ref, cw_ref, w_ref, g_ref, ya_ref, oa_ref, h1_ref, carry):
        @pl.when(pl.program_id(0) == 0)
        def _():
            carry[...] = jnp.zeros_like(carry)
        v = proj_ref[:, D:2 * D].astype(F32) * proj_ref[:, 2 * D:3 * D].astype(F32)
        rows = lax.broadcasted_iota(jnp.int32, (tm, D), 0)
        before = carry[...]
        v1, v2 = _shift_rows(v, before[7:8, :], before[6:7, :], rows)
        carry[...] = v[tm - 8:tm, :]
        conv = cw_ref[0:1, :] * v2 + cw_ref[1:2, :] * v1 + cw_ref[2:3, :] * v
        _, sz = _silu_parts(proj_ref[:, 3 * D:4 * D].astype(F32))
        ya = (proj_ref[:, 0:D].astype(F32) * conv * sz).astype(BF16)
        ya_ref[...] = ya
        oa = _nn(ya, w_ref[...])
        oa_ref[...] = oa.astype(BF16)
        h1_ref[...] = x_ref[...] + oa * _rms_scale(oa) * g_ref[...]

    row = lambda i: (i, 0)
    fix = lambda i: (0, 0)
    return pl.pallas_call(
        body, name="a_mix", grid=(s // tm,),
        in_specs=[pl.BlockSpec((tm, 4 * D), row), pl.BlockSpec((tm, D), row), pl.BlockSpec((8, D), fix),
                  pl.BlockSpec((D, D), fix), pl.BlockSpec((1, D), fix)],
        out_specs=[pl.BlockSpec((tm, D), row)] * 3,
        out_shape=[SDS((s, D), BF16), SDS((s, D), BF16), SDS((s, D), F32)],
        scratch_shapes=[pltpu.VMEM((8, D), F32)],
        compiler_params=_params(("arbitrary",)),
    )(proj, x, conv_w, w_out, g_post)


def _b_in(h1, g_kv, g_pre, w_kv, wbin_g, tm):
    s = h1.shape[0]

    def body(h_ref, gk_ref, gb_ref, wkv_ref, wb_ref, kv_ref, q_ref, z_ref):
        h = h_ref[...]
        hh = h * _rms_scale(h)
        nk = (hh * gk_ref[...]).astype(BF16)
        nb = (hh * gb_ref[...]).astype(BF16)
        kv_ref[...] = _nn(nk, wkv_ref[...]).astype(BF16)
        for j in range(2):
            q_ref[:, BIN_COLS * j:BIN_COLS * (j + 1)] = (_nn(nb, wb_ref[j]) * Q_SCALE).astype(BF16)
            z_ref[:, BIN_COLS * j:BIN_COLS * (j + 1)] = _nn(nb, wb_ref[2 + j]).astype(BF16)

    row = lambda i: (i, 0)
    fix = lambda i: (0, 0)
    return pl.pallas_call(
        body, name="b_in", grid=(s // tm,),
        in_specs=[pl.BlockSpec((tm, D), row), pl.BlockSpec((1, D), fix), pl.BlockSpec((1, D), fix),
                  pl.BlockSpec((D, 2 * KV_W), fix), pl.BlockSpec((N_CHIPS, D, BIN_COLS), lambda i: (0, 0, 0))],
        out_specs=[pl.BlockSpec((tm, 2 * KV_W), row), pl.BlockSpec((tm, D), row), pl.BlockSpec((tm, D), row)],
        out_shape=[SDS((s, 2 * KV_W), BF16), SDS((s, D), BF16), SDS((s, D), BF16)],
        compiler_params=_params(("parallel",)),
    )(h1, g_kv, g_pre, w_kv, wbin_g)


def _band_buckets():
    q = lax.broadcasted_iota(jnp.int32, (BLK, 2 * BLK), 0)
    k = lax.broadcasted_iota(jnp.int32, (BLK, 2 * BLK), 1)
    dist = q + BLK - k
    bucket = jnp.where(dist < MAX_EXACT, dist, MAX_EXACT)
    for t in BUCKET_THRESHOLDS:
        bucket = bucket + jnp.where(dist >= t, 1, 0)
    in_window = (dist >= 0) & (dist < BLK)
    return jnp.where(in_window, bucket, -1)


def _head_place(h):
    kh, j, e = h // GROUP, (h % GROUP) // 2, h % 2
    return kh, slice(BLK * j, BLK * (j + 1)), slice(2 * BLK * e, 2 * BLK * (e + 1))


def _bias_table(rel_bias, sinks):
    def body(rb_ref, sink_ref, tab_ref):
        bucket = _band_buckets()
        col = lax.broadcasted_iota(jnp.int32, (BLK, 2 * BLK), 1)
        for h in range(N_HEADS):
            acc = jnp.where(bucket < 0, NEG_INF, 0.0).astype(F32)
            for b in range(N_BUCKETS):
                acc = jnp.where(bucket == b, rb_ref[b, h], acc)
            acc = jnp.where(col == 0, sink_ref[h], acc)
            kh, rows, cols = _head_place(h)
            tab_ref[1, kh, rows, cols] = acc
            tab_ref[0, kh, rows, cols] = jnp.where((col > 0) & (col < BLK), NEG_INF, acc)

    return pl.pallas_call(
        body, name="bias_table", out_shape=SDS((2, N_KV, 4 * BLK, 4 * BLK), F32),
        in_specs=[pl.BlockSpec(memory_space=pltpu.SMEM), pl.BlockSpec(memory_space=pltpu.SMEM)],
        out_specs=pl.BlockSpec(memory_space=pltpu.VMEM),
    )(rel_bias, sinks)


def _bias_fold(dtab):
    def body(dtab_ref, out_ref, dsink_ref):
        bucket = _band_buckets()
        row = lax.broadcasted_iota(jnp.int32, (N_BUCKETS, 128), 0)
        lane = lax.broadcasted_iota(jnp.int32, (N_BUCKETS, 128), 1)
        row8 = lax.broadcasted_iota(jnp.int32, (8, 128), 0)
        lane8 = lax.broadcasted_iota(jnp.int32, (8, 128), 1)
        acc = jnp.zeros((N_BUCKETS, 128), F32)
        dsink = jnp.zeros((8, 128), F32)
        for h in range(N_HEADS):
            kh, rows, cols = _head_place(h)
            dt = dtab_ref[kh, rows, cols]
            for b in range(N_BUCKETS):
                val = jnp.sum(jnp.where(bucket == b, dt, 0.0))
                acc = acc + jnp.where((row == b) & (lane == h), val, 0.0)
            dsink = dsink + jnp.where((row8 == 0) & (lane8 == h), jnp.sum(dt[:, 0:1]), 0.0)
        out_ref[...] = acc
        dsink_ref[...] = dsink

    vm = pl.BlockSpec(memory_space=pltpu.VMEM)
    return pl.pallas_call(
        body, name="bias_fold", out_shape=[SDS((N_BUCKETS, 128), F32), SDS((8, 128), F32)],
        in_specs=[vm], out_specs=[vm, vm],
    )(dtab)


def _pair_operands(prev, cur):
    t = jnp.concatenate([prev, cur], axis=0).astype(F32)
    t = jnp.where(lax.broadcasted_iota(jnp.int32, t.shape, 0) == 0, 0.0, t)
    tr = pltpu.roll(t, HEAD_DIM, 1)
    lo = lax.broadcasted_iota(jnp.int32, t.shape, 1) < HEAD_DIM
    zero = jnp.zeros_like(t)
    head0 = jnp.concatenate([jnp.where(lo, t, zero), jnp.where(lo, zero, tr)], axis=0).astype(BF16)
    head1 = jnp.concatenate([jnp.where(lo, tr, zero), jnp.where(lo, zero, t)], axis=0).astype(BF16)
    return head0, head1


def _pair_fold(d0, d1):
    lo = lax.broadcasted_iota(jnp.int32, (2 * BLK, KV_W), 1) < HEAD_DIM
    zero = jnp.zeros((2 * BLK, KV_W), F32)
    g0 = jnp.where(lo, d0[0:256], zero) + pltpu.roll(jnp.where(lo, zero, d0[256:512]), HEAD_DIM, 1)
    g1 = pltpu.roll(jnp.where(lo, d1[0:256], zero), HEAD_DIM, 1) + jnp.where(lo, zero, d1[256:512])
    return jnp.where(lax.broadcasted_iota(jnp.int32, (2 * BLK, KV_W), 0) == 0, 0.0, g0 + g1)


def _stack_pairs(ref, kh, rows=slice(None)):
    return jnp.concatenate([ref[rows, 128 * (4 * kh + j):128 * (4 * kh + j + 1)] for j in range(4)], axis=0)


def _table_spec():
    return pl.BlockSpec((1, N_KV, 4 * BLK, 4 * BLK), lambda n: (jnp.minimum(n, 1), 0, 0, 0))


def _attn_fwd(q, kv, tab):
    s = q.shape[0]

    def body(q_ref, kp_ref, k0_ref, k1_ref, vp_ref, v0_ref, v1_ref, tab0_ref, tab1_ref, att_ref, stats_ref):
        lane = lax.broadcasted_iota(jnp.int32, (BLK, 128), 1)
        for sub, (kp, kc, vp, vc, tab_ref) in enumerate([(kp_ref, k0_ref, vp_ref, v0_ref, tab0_ref),
                                                         (k0_ref, k1_ref, v0_ref, v1_ref, tab1_ref)]):
            rows = slice(BLK * sub, BLK * (sub + 1))
            k2 = _pair_operands(kp[...], kc[...])
            v2 = _pair_operands(vp[...], vc[...])
            stats = jnp.zeros((BLK, 128), F32)
            for kh in range(N_KV):
                sc = _nt(_stack_pairs(q_ref, kh, rows), k2[kh])
                ps = []
                for e in range(2):
                    lg = sc[:, 256 * e:256 * (e + 1)] + tab_ref[0, kh, :, 256 * e:256 * (e + 1)]
                    m = jnp.max(lg, axis=-1, keepdims=True)
                    ex = jnp.exp(lg - m)
                    den = jnp.sum(ex, axis=-1, keepdims=True)
                    ps.append(ex * (1.0 / den))
                    lse = m + jnp.log(den)
                    for j in range(4):
                        stats = jnp.where(lane == GROUP * kh + 2 * j + e, lse[BLK * j:BLK * (j + 1)], stats)
                out = _nn(jnp.concatenate(ps, axis=1).astype(BF16), v2[kh])
                for j in range(4):
                    att_ref[rows, 128 * (4 * kh + j):128 * (4 * kh + j + 1)] = out[BLK * j:BLK * (j + 1)].astype(BF16)
            stats_ref[rows, :] = stats

    two = lambda m: (m, 0)
    table = lambda pick: pl.BlockSpec((1, N_KV, 4 * BLK, 4 * BLK), lambda m: (pick(m), 0, 0, 0))
    return pl.pallas_call(
        body, name="attn_fwd", grid=(s // (2 * BLK),),
        in_specs=[pl.BlockSpec((2 * BLK, D), two)]
        + [pl.BlockSpec((BLK, KV_W), lambda m, col=col, off=off: (jnp.maximum(2 * m + off, 0), col))
           for col in (0, 1) for off in (-1, 0, 1)]
        + [table(lambda m: jnp.minimum(m, 1)), table(lambda m: 1)],
        out_specs=[pl.BlockSpec((2 * BLK, D), two), pl.BlockSpec((2 * BLK, 128), two)],
        out_shape=[SDS((s, D), BF16), SDS((s, 128), F32)],
        compiler_params=_params(("parallel",)),
    )(q, kv, kv, kv, kv, kv, kv, tab, tab)


def _mid(att, zb, h1, tgt, w_out, g_post, tm):
    s = att.shape[0]
    nt = s // tm

    def body(att_ref, z_ref, h1_ref, t_ref, w_ref, g_ref,
             dh_ref, dqz_ref, datt_ref, loss_ref, dg_ref, dw_ref, dw16_ref, dw_acc, stage):
        @pl.when(pl.program_id(0) == 0)
        def _():
            loss_ref[...] = jnp.zeros_like(loss_ref)
            dg_ref[...] = jnp.zeros_like(dg_ref)
            dw_acc[...] = jnp.zeros_like(dw_acc)
        att = att_ref[...].astype(F32)
        z = z_ref[...].astype(F32)
        sg, sz = _silu_parts(z)
        ob = (att * sz).astype(BF16)
        y2 = _nn(ob, w_ref[...])
        r2 = _rms_scale(y2)
        yh = y2 * r2
        g = g_ref[...]
        err = (h1_ref[...] + yh * g) - t_ref[...]
        loss_ref[...] += jnp.sum(jnp.sum(err * err, axis=-1, keepdims=True) / D)
        dh = err / D
        dh_ref[...] = dh
        _acc_row(dg_ref, 0, jnp.sum(dh * yh, axis=0, keepdims=True))
        dyh = dh * g
        dy = (r2 * (dyh - yh * jnp.mean(dyh * yh, axis=-1, keepdims=True))).astype(BF16)
        dw_acc[...] += _tn(ob, dy)
        dob = _nt(dy, w_ref[...])
        datt_ref[...] = (dob * sz).astype(BF16)
        dqz_ref[...] = (dob * att * _dsilu(z, sg)).astype(BF16)

        @pl.when(pl.program_id(0) == nt - 1)
        def _():
            _write_gradient(dw_acc, dw_ref, dw16_ref, stage)

    row = lambda i: (i, 0)
    fix = lambda i: (0, 0)
    anyspace = pl.BlockSpec(memory_space=pl.ANY)
    return pl.pallas_call(
        body, name="mid", grid=(nt,),
        in_specs=[pl.BlockSpec((tm, D), row)] * 4 + [pl.BlockSpec((D, D), fix), pl.BlockSpec((1, D), fix)],
        out_specs=[pl.BlockSpec((tm, D), row), pl.BlockSpec((tm, D), lambda i: (i, 1)), pl.BlockSpec((tm, D), row),
                   pl.BlockSpec((8, 128), fix), pl.BlockSpec((8, D), fix), anyspace, anyspace],
        out_shape=[SDS((s, D), F32), SDS((s, 2 * D), BF16), SDS((s, D), BF16), SDS((8, 128), F32),
                   SDS((8, D), F32), SDS((D, D), F32), SDS((D, D), BF16)],
        scratch_shapes=[pltpu.VMEM((D, D), F32), pltpu.VMEM((D // 4, D), BF16)],
        compiler_params=_params(("arbitrary",)),
    )(att, zb, h1, tgt, w_out, g_post)


def _attn_bwd(q, kv, datt, stats, tab, dqz):
    s = q.shape[0]
    nb = s // BLK

    def body(q_ref, kp_ref, kc_ref, vp_ref, vc_ref, da_ref, st_ref, tab_ref, dqz_in,
             dq_ref, dkv_ref, dtab_ref, dk_carry, dv_carry):
        del dqz_in
        n = pl.program_id(0)

        @pl.when(n == 0)
        def _():
            dtab_ref[...] = jnp.zeros_like(dtab_ref)
            dk_carry[...] = jnp.zeros_like(dk_carry)
            dv_carry[...] = jnp.zeros_like(dv_carry)

        @pl.when(n < nb)
        def _():
            k2 = _pair_operands(kp_ref[...], kc_ref[...])
            v2 = _pair_operands(vp_ref[...], vc_ref[...])
            lane = lax.broadcasted_iota(jnp.int32, (BLK, 128), 1)
            stats = st_ref[...]
            dk2, dv2 = [], []
            for kh in range(N_KV):
                qs = _stack_pairs(q_ref, kh)
                das = _stack_pairs(da_ref, kh)
                sc = _nt(qs, k2[kh])
                dp = _nt(das, v2[kh])
                ps, dss = [], []
                for e in range(2):
                    heads = [GROUP * kh + 2 * j + e for j in range(4)]
                    lse = jnp.concatenate([jnp.sum(jnp.where(lane == h, stats, 0.0), axis=-1, keepdims=True)
                                           for h in heads], axis=0)
                    cols = slice(256 * e, 256 * (e + 1))
                    p = jnp.exp(sc[:, cols] + tab_ref[0, kh, :, cols] - lse)
                    delta = jnp.sum(p * dp[:, cols], axis=-1, keepdims=True)
                    ds = p * (dp[:, cols] - delta)
                    dtab_ref[kh, :, cols] += ds
                    ps.append(p)
                    dss.append(ds)
                p2 = jnp.concatenate(ps, axis=1).astype(BF16)
                ds2 = jnp.concatenate(dss, axis=1).astype(BF16)
                dq = _nn(ds2, k2[kh]) * Q_SCALE
                for j in range(4):
                    dq_ref[:, 128 * (4 * kh + j):128 * (4 * kh + j + 1)] = dq[BLK * j:BLK * (j + 1)].astype(BF16)
                dk2.append(_tn(ds2, qs))
                dv2.append(_tn(p2, das))
            dkk = _pair_fold(dk2[0], dk2[1])
            dvv = _pair_fold(dv2[0], dv2[1])
            dkv_ref[:, 0:KV_W] = (dk_carry[...] + dkk[0:BLK]).astype(BF16)
            dkv_ref[:, KV_W:2 * KV_W] = (dv_carry[...] + dvv[0:BLK]).astype(BF16)
            dk_carry[...] = dkk[BLK:2 * BLK]
            dv_carry[...] = dvv[BLK:2 * BLK]

        @pl.when(n == nb)
        def _():
            dkv_ref[:, 0:KV_W] = dk_carry[...].astype(BF16)
            dkv_ref[:, KV_W:2 * KV_W] = dv_carry[...].astype(BF16)

    cur = lambda n: (jnp.minimum(n, nb - 1), 0)
    prev = lambda n: (jnp.clip(n - 1, 0, nb - 1), 0)
    return pl.pallas_call(
        body, name="attn_bwd", grid=(nb + 1,),
        in_specs=[pl.BlockSpec((BLK, D), cur),
                  pl.BlockSpec((BLK, KV_W), prev), pl.BlockSpec((BLK, KV_W), cur),
                  pl.BlockSpec((BLK, KV_W), lambda n: (jnp.clip(n - 1, 0, nb - 1), 1)),
                  pl.BlockSpec((BLK, KV_W), lambda n: (jnp.minimum(n, nb - 1), 1)),
                  pl.BlockSpec((BLK, D), cur), pl.BlockSpec((BLK, 128), cur), _table_spec(),
                  pl.BlockSpec(memory_space=pl.ANY)],
        out_specs=[pl.BlockSpec((BLK, D), cur), pl.BlockSpec((BLK, 2 * KV_W), prev),
                   pl.BlockSpec((N_KV, 4 * BLK, 4 * BLK), lambda n: (0, 0, 0))],
        out_shape=[SDS((s, 2 * D), BF16), SDS((s, 2 * KV_W), BF16), SDS((N_KV, 4 * BLK, 4 * BLK), F32)],
        scratch_shapes=[pltpu.VMEM((BLK, KV_W), F32), pltpu.VMEM((BLK, KV_W), F32)],
        input_output_aliases={8: 0},
        compiler_params=_params(("arbitrary",)),
    )(q, kv, kv, kv, kv, datt, stats, tab, dqz)


def _b_bwd(dqz, dkv, h1, dh2, oa, wbin_g, w_kv, g_kv, g_pre, g_apost, tm):
    s = h1.shape[0]
    nt = s // tm

    def body(dqz_ref, dkv_ref, h_ref, dh2_ref, oa_ref, wb_ref, wkv_ref, gk_ref, gb_ref, ga_ref,
             dh1_ref, doa_ref, dg_ref, dwb_ref, dwkv_ref, dwb16_ref, dwkv16_ref, wcat, dwb_acc, dwkv_acc):
        @pl.when(pl.program_id(0) == 0)
        def _():
            dg_ref[...] = jnp.zeros_like(dg_ref)
            dwb_acc[...] = jnp.zeros_like(dwb_acc)
            dwkv_acc[...] = jnp.zeros_like(dwkv_acc)
            for j in range(N_CHIPS):
                pltpu.sync_copy(wb_ref.at[j], wcat.at[:, pl.ds(BIN_COLS * j, BIN_COLS)])
        dnb = _nt(dqz_ref[...], wcat[...])
        dnk = _nt(dkv_ref[...], wkv_ref[...])
        h = h_ref[...]
        r = _rms_scale(h)
        hh = h * r
        dwb_acc[...] += _tn((hh * gb_ref[...]).astype(BF16), dqz_ref[...])
        dwkv_acc[...] += _tn((hh * gk_ref[...]).astype(BF16), dkv_ref[...])
        _acc_row(dg_ref, 0, jnp.sum(dnk * hh, axis=0, keepdims=True))
        _acc_row(dg_ref, 1, jnp.sum(dnb * hh, axis=0, keepdims=True))
        dhh = dnb * gb_ref[...] + dnk * gk_ref[...]
        dh1 = dh2_ref[...] + r * (dhh - hh * jnp.mean(dhh * hh, axis=-1, keepdims=True))
        dh1_ref[...] = dh1
        oa = oa_ref[...].astype(F32)
        ra = _rms_scale(oa)
        oh = oa * ra
        _acc_row(dg_ref, 2, jnp.sum(dh1 * oh, axis=0, keepdims=True))
        doh = dh1 * ga_ref[...]
        doa_ref[...] = (ra * (doh - oh * jnp.mean(doh * oh, axis=-1, keepdims=True))).astype(BF16)

        @pl.when(pl.program_id(0) == nt - 1)
        def _():
            wcat[...] = dwb_acc[...].astype(BF16)
            for j in range(N_CHIPS):
                pltpu.sync_copy(dwb_acc.at[:, pl.ds(BIN_COLS * j, BIN_COLS)], dwb_ref.at[j])
                pltpu.sync_copy(wcat.at[:, pl.ds(BIN_COLS * j, BIN_COLS)], dwb16_ref.at[j])
            pltpu.sync_copy(dwkv_acc, dwkv_ref)
            wcat[:, 0:2 * KV_W] = dwkv_acc[...].astype(BF16)
            pltpu.sync_copy(wcat.at[:, pl.ds(0, 2 * KV_W)], dwkv16_ref)

    row = lambda i: (i, 0)
    fix = lambda i: (0, 0)
    anyspace = pl.BlockSpec(memory_space=pl.ANY)
    return pl.pallas_call(
        body, name="b_bwd", grid=(nt,),
        in_specs=[pl.BlockSpec((tm, 2 * D), row), pl.BlockSpec((tm, 2 * KV_W), row), pl.BlockSpec((tm, D), row),
                  pl.BlockSpec((tm, D), row), pl.BlockSpec((tm, D), row), anyspace, pl.BlockSpec((D, 2 * KV_W), fix),
                  pl.BlockSpec((1, D), fix), pl.BlockSpec((1, D), fix), pl.BlockSpec((1, D), fix)],
        out_specs=[pl.BlockSpec((tm, D), row), pl.BlockSpec((tm, D), row), pl.BlockSpec((8, D), fix)] + [anyspace] * 4,
        out_shape=[SDS((s, D), F32), SDS((s, D), BF16), SDS((8, D), F32), SDS((N_CHIPS, D, BIN_COLS), F32),
                   SDS((D, 2 * KV_W), F32), SDS((N_CHIPS, D, BIN_COLS), BF16), SDS((D, 2 * KV_W), BF16)],
        scratch_shapes=[pltpu.VMEM((D, 2 * D), BF16), pltpu.VMEM((D, 2 * D), F32), pltpu.VMEM((D, 2 * KV_W), F32)],
        compiler_params=_params(("arbitrary",)),
    )(dqz, dkv, h1, dh2, oa, wbin_g, w_kv, g_kv, g_pre, g_apost)


def _to_owner_core(pieces, r, send, recv, core, action):
    x, y, c = lax.axis_index("x"), lax.axis_index("y"), lax.axis_index("c")
    for kp in range(N_CHIPS):
        px, py = kp >> 1, kp & 1
        rel = 4 * (x + px - 2 * x * px) + 2 * (y + py - 2 * y * py) + (c + core - 2 * c * core)

        @pl.when(rel != 0)
        def _():
            cp = pltpu.make_async_remote_copy(src_ref=pieces.at[kp], dst_ref=r.at[rel - 1], send_sem=send.at[kp],
                                              recv_sem=recv.at[rel - 1], device_id=(px, py, core), device_id_type=MESH)
            if action == "start":
                cp.start()
            else:
                cp.wait_send()
    if action == "wait":
        @pl.when(c == core)
        def _():
            for rel in range(1, N_DEV):
                pltpu.make_async_remote_copy(src_ref=pieces.at[0], dst_ref=r.at[rel - 1], send_sem=send.at[0],
                                             recv_sem=recv.at[rel - 1], device_id=(x, y, c),
                                             device_id_type=MESH).wait_recv()


def _owner_core_sems():
    return [pltpu.SemaphoreType.DMA((N_CHIPS,)), pltpu.SemaphoreType.DMA((N_DEV - 1,))]


def _device_exchange(grads, recvs, send, recv):
    x, y, c = lax.axis_index("x"), lax.axis_index("y"), lax.axis_index("c")
    copies = []
    for a, (g, r) in enumerate(zip(grads, recvs)):
        h = g.shape[1] // 2
        for rel in range(1, N_DEV):
            fx, fy, fc = rel >> 2, (rel >> 1) & 1, rel & 1
            px, py, pc = x + fx - 2 * x * fx, y + fy - 2 * y * fy, c + fc - 2 * c * fc
            sem = (N_DEV - 1) * a + rel - 1
            copies.append(pltpu.make_async_remote_copy(
                src_ref=g.at[2 * px + py, pl.ds(pl.multiple_of(pc * h, 16), h)], dst_ref=r.at[rel - 1],
                send_sem=send.at[sem], recv_sem=recv.at[sem], device_id=(px, py, pc), device_id_type=MESH))
    return copies


def _device_exchange_specs(grads):
    anyspace = pl.BlockSpec(memory_space=pl.ANY)
    n = len(grads)
    count = (N_DEV - 1) * n
    return ([anyspace] * n, [anyspace] * n,
            [SDS((N_DEV - 1, g.shape[1] // 2, g.shape[2]), g.dtype) for g in grads],
            [pltpu.SemaphoreType.DMA((count,)), pltpu.SemaphoreType.DMA((count,))])


def _a_bwd(doa, ya, proj, conv_w, w_out, tm, parts):
    s = doa.shape[0]
    nt = s // tm
    n = len(parts)
    ex_in, ex_out, ex_shape, ex_sems = _device_exchange_specs(parts)

    def body(*refs):
        doa_ref, ya_ref, proj_ref, halo_ref, cw_ref, w_ref = refs[:6]
        part_refs = refs[6:6 + n]
        dproj_ref, dcw_ref, dw_ref, dw16_ref = refs[6 + n:10 + n]
        recv_refs = refs[10 + n:10 + 2 * n]
        carry, dw_acc, stage, send, recv = refs[10 + 2 * n:]
        i = pl.program_id(0)
        r = nt - 1 - i

        @pl.when(i == 0)
        def _():
            dcw_ref[...] = jnp.zeros_like(dcw_ref)
            carry[...] = jnp.zeros_like(carry)
            dw_acc[...] = jnp.zeros_like(dw_acc)
            for cp in _device_exchange(part_refs, recv_refs, send, recv):
                cp.start()
        dya = _nt(doa_ref[...], w_ref[...])
        dw_acc[...] += _tn(ya_ref[...], doa_ref[...])
        bg = proj_ref[:, 0:D].astype(F32)
        cg = proj_ref[:, D:2 * D].astype(F32)
        u = proj_ref[:, 2 * D:3 * D].astype(F32)
        z = proj_ref[:, 3 * D:4 * D].astype(F32)
        v = cg * u
        before = jnp.where(r > 0, halo_ref[:, D:2 * D].astype(F32) * halo_ref[:, 2 * D:3 * D].astype(F32), 0.0)
        rows = lax.broadcasted_iota(jnp.int32, (tm, D), 0)
        v1, v2 = _shift_rows(v, before[HALO - 1:HALO, :], before[HALO - 2:HALO - 1, :], rows)
        conv = cw_ref[0:1, :] * v2 + cw_ref[1:2, :] * v1 + cw_ref[2:3, :] * v
        sg, sz = _silu_parts(z)
        dproj_ref[:, 0:D] = (dya * conv * sz).astype(BF16)
        dproj_ref[:, 3 * D:4 * D] = (dya * bg * conv * _dsilu(z, sg)).astype(BF16)
        dconv = dya * bg * sz
        _acc_row(dcw_ref, 0, jnp.sum(dconv * v2, axis=0, keepdims=True))
        _acc_row(dcw_ref, 1, jnp.sum(dconv * v1, axis=0, keepdims=True))
        _acc_row(dcw_ref, 2, jnp.sum(dconv * v, axis=0, keepdims=True))
        after = carry[...]
        up1 = jnp.where(rows < tm - 1, pltpu.roll(dconv, tm - 1, 0), after[0:1, :])
        up2 = jnp.where(rows < tm - 2, pltpu.roll(dconv, tm - 2, 0),
                        jnp.where(rows == tm - 2, after[0:1, :], after[1:2, :]))
        carry[...] = dconv[0:8, :]
        dv = cw_ref[2:3, :] * dconv + cw_ref[1:2, :] * up1 + cw_ref[0:1, :] * up2
        dproj_ref[:, D:2 * D] = (dv * u).astype(BF16)
        dproj_ref[:, 2 * D:3 * D] = (dv * cg).astype(BF16)

        @pl.when(i == nt - 1)
        def _():
            _write_gradient(dw_acc, dw_ref, dw16_ref, stage)
            for cp in _device_exchange(part_refs, recv_refs, send, recv):
                cp.wait()

    rev = lambda i: (nt - 1 - i, 0)
    fix = lambda i: (0, 0)
    halo = lambda i: (jnp.maximum((nt - 1 - i) * (tm // HALO) - 1, 0), 0)
    anyspace = pl.BlockSpec(memory_space=pl.ANY)
    dproj, dcw, dw, dw16, *got = pl.pallas_call(
        body, name="a_bwd", grid=(nt,),
        in_specs=[pl.BlockSpec((tm, D), rev), pl.BlockSpec((tm, D), rev), pl.BlockSpec((tm, 4 * D), rev),
                  pl.BlockSpec((HALO, 4 * D), halo), pl.BlockSpec((8, D), fix), pl.BlockSpec((D, D), fix)] + ex_in,
        out_specs=[pl.BlockSpec((tm, 4 * D), rev), pl.BlockSpec((8, D), fix), anyspace, anyspace] + ex_out,
        out_shape=[SDS((s, 4 * D), BF16), SDS((8, D), F32), SDS((D, D), F32), SDS((D, D), BF16)] + ex_shape,
        scratch_shapes=[pltpu.VMEM((8, D), F32), pltpu.VMEM((D, D), F32), pltpu.VMEM((D // 4, D), BF16)] + ex_sems,
        compiler_params=_params(("arbitrary",)),
    )(doa, ya, proj, proj, conv_w, w_out, *parts)
    return dproj, dcw, dw, dw16, got


def _dn1(dp_ref, w_ref):
    dn = _nt(dp_ref[:, 0:D], w_ref[0])
    for j in range(1, 4):
        dn = dn + _nt(dp_ref[:, D * j:D * (j + 1)], w_ref[j])
    return dn


def _a_in_bwd_matmul(dproj, win_g, tm, count, win_half, win_got):
    def body(dp_ref, w_ref, half_ref, got_in, dn_ref, got_ref, wcat, send, recv):
        del got_in

        @pl.when(pl.program_id(0) == 0)
        def _():
            _to_owner_core(half_ref, got_ref, send, recv, 1, "start")
            for j in range(N_CHIPS):
                pltpu.sync_copy(w_ref.at[j], wcat.at[:, pl.ds(D * j, D)])
        dn_ref[...] = _nt(dp_ref[...], wcat[...]).astype(BF16)

        @pl.when(pl.program_id(0) == count - 1)
        def _():
            _to_owner_core(half_ref, got_ref, send, recv, 1, "wait")

    row = lambda i: (i, 0)
    anyspace = pl.BlockSpec(memory_space=pl.ANY)
    return pl.pallas_call(
        body, name="a_in_bwd_matmul", grid=(count,),
        in_specs=[pl.BlockSpec((tm, 4 * D), row), anyspace, anyspace, anyspace],
        out_specs=[pl.BlockSpec((tm, D), row), anyspace],
        out_shape=[SDS((count * tm, D), BF16), SDS(win_got.shape, win_got.dtype)],
        scratch_shapes=[pltpu.VMEM((D, 4 * D), BF16)] + _owner_core_sems(),
        input_output_aliases={3: 1},
        compiler_params=_params(("arbitrary",)),
    )(dproj, win_g, win_half, win_got)


def _a_in_bwd(dn_first, dproj, x, dh1, win_g, g_pre, tm):
    s = x.shape[0]
    nt = s // tm
    count = dn_first.shape[0] // tm

    def body(dn_ref, dp_ref, x_ref, dh_ref, w_ref, g_ref, gx_ref, dg_ref, dn_s):
        i = pl.program_id(0)

        @pl.when(i == 0)
        def _():
            dg_ref[...] = jnp.zeros_like(dg_ref)

        @pl.when(i < count)
        def _():
            dn_s[...] = dn_ref[...].astype(F32)

        @pl.when(i >= count)
        def _():
            dn_s[...] = _dn1(dp_ref, w_ref)
        dn = dn_s[...]
        xv = x_ref[...]
        r = _rms_scale(xv)
        xh = xv * r
        _acc_row(dg_ref, 0, jnp.sum(dn * xh, axis=0, keepdims=True))
        dxh = dn * g_ref[...]
        gx_ref[...] = dh_ref[...] + r * (dxh - xh * jnp.mean(dxh * xh, axis=-1, keepdims=True))

    row = lambda i: (i, 0)
    fix = lambda i: (0, 0)
    return pl.pallas_call(
        body, name="a_in_bwd", grid=(nt,),
        in_specs=[pl.BlockSpec((tm, D), lambda i: (jnp.minimum(i, count - 1), 0)),
                  pl.BlockSpec((tm, 4 * D), lambda i: (jnp.maximum(i, count), 0)),
                  pl.BlockSpec((tm, D), row), pl.BlockSpec((tm, D), row),
                  pl.BlockSpec((4, D, D), lambda i: (0, 0, 0)), pl.BlockSpec((1, D), fix)],
        out_specs=[pl.BlockSpec((tm, D), row), pl.BlockSpec((8, D), fix)],
        out_shape=[SDS((s, D), F32), SDS((8, D), F32)],
        scratch_shapes=[pltpu.VMEM((tm, D), F32)],
        compiler_params=_params(("arbitrary",)),
    )(dn_first, dproj, x, dh1, win_g, g_pre)


def _dw_in_half(n1, dproj, core, tmw, name, to_owners=None, to_devices=None):
    s = n1.shape[0]
    h = D // 2
    nt = s // tmw
    sent_array = to_owners if to_owners is not None else to_devices
    rides = sent_array is not None
    if to_owners is not None:
        sems, got_shape = _owner_core_sems(), SDS((N_DEV - 1, h, D), BF16)
    elif to_devices is not None:
        _, _, (got_shape,), sems = _device_exchange_specs([to_devices])

    def body(*refs):
        a_ref, b_ref = refs[:2]
        o_ref, o16_ref = refs[2 + rides:4 + rides]
        j, t = pl.program_id(0), pl.program_id(1)

        def exchange(action):
            sent, got, send, recv = refs[2], refs[5], refs[6], refs[7]
            if to_owners is not None:
                _to_owner_core(sent, got, send, recv, 1 - core, action)
            else:
                for cp in _device_exchange([sent], [got], send, recv):
                    cp.start() if action == "start" else cp.wait()

        if rides:
            @pl.when((j == 0) & (t == 0))
            def _():
                exchange("start")

        @pl.when(t == 0)
        def _():
            o_ref[...] = jnp.zeros_like(o_ref)
        o_ref[0] += _tn(a_ref[...], b_ref[...])

        @pl.when(t == nt - 1)
        def _():
            o16_ref[...] = o_ref[...].astype(BF16)
        if rides:
            @pl.when((j == N_CHIPS - 1) & (t == nt - 1))
            def _():
                exchange("wait")

    anyspace = pl.BlockSpec(memory_space=pl.ANY)
    slot = pl.BlockSpec((1, h, D), lambda j, t: (j, 0, 0))
    return pl.pallas_call(
        body, name=name, grid=(N_CHIPS, nt),
        in_specs=[pl.BlockSpec((tmw, h), lambda j, t: (t, core)), pl.BlockSpec((tmw, D), lambda j, t: (t, j))]
        + [anyspace] * rides,
        out_specs=[slot, slot] + [anyspace] * rides,
        out_shape=[SDS((N_CHIPS, h, D), F32), SDS((N_CHIPS, h, D), BF16)] + ([got_shape] if rides else []),
        scratch_shapes=sems if rides else [],
        compiler_params=_params(("arbitrary", "arbitrary")),
    )(n1, dproj, *([sent_array] if rides else []))


def _sibling_exchange(name, to_sibling=(), shards=(), smalls=()):
    n_g, n_h, n_s = len(to_sibling), len(shards), len(smalls)

    def body(*refs):
        gs = refs[:n_g]
        pos = n_g + n_h
        small_ins = refs[pos:pos + n_s]
        pos += n_s
        rs, fs = refs[pos:pos + n_g], refs[pos + n_g:pos + n_g + n_h]
        pos += n_g + n_h
        small_alls = refs[pos:pos + n_s]
        pos += n_s
        dsend, drecv, ssend, srecv = refs[pos:]
        x, y, c = lax.axis_index("x"), lax.axis_index("y"), lax.axis_index("c")
        sibling = (x, y, 1 - c)
        sends, arrivals = [], []
        for a, (g, r) in enumerate(zip(gs, rs)):
            h = g.shape[1] // 2
            src = g.at[:, pl.ds(pl.multiple_of((1 - c) * h, 8), h), :]
            sends.append(pltpu.make_async_remote_copy(src_ref=src, dst_ref=r, send_sem=dsend.at[a], recv_sem=drecv.at[a],
                                                      device_id=sibling, device_id_type=MESH))
            arrivals.append(pltpu.make_async_remote_copy(src_ref=r, dst_ref=r, send_sem=dsend.at[a], recv_sem=drecv.at[a],
                                                         device_id=sibling, device_id_type=MESH))
        for b, full in enumerate(fs):
            h = full.shape[0] // 2
            mine = full.at[pl.ds(pl.multiple_of(c * h, 8), h)]
            theirs = full.at[pl.ds(pl.multiple_of((1 - c) * h, 8), h)]
            sends.append(pltpu.make_async_remote_copy(src_ref=mine, dst_ref=mine, send_sem=dsend.at[n_g + b],
                                                      recv_sem=drecv.at[n_g + b], device_id=sibling, device_id_type=MESH))
            arrivals.append(pltpu.make_async_remote_copy(src_ref=mine, dst_ref=theirs, send_sem=dsend.at[n_g + b],
                                                         recv_sem=drecv.at[n_g + b], device_id=sibling, device_id_type=MESH))
        me = 4 * x + 2 * y + c
        for k, (small_in, small_all) in enumerate(zip(small_ins, small_alls)):
            small_all[me] = small_in[...]
            for rel in range(1, N_DEV):
                fx, fy, fc = rel >> 2, (rel >> 1) & 1, rel & 1
                peer = (x + fx - 2 * x * fx, y + fy - 2 * y * fy, c + fc - 2 * c * fc)
                sender = 4 * peer[0] + 2 * peer[1] + peer[2]
                sem = (N_DEV - 1) * k + rel - 1
                sends.append(pltpu.make_async_remote_copy(
                    src_ref=small_in, dst_ref=small_all.at[me], send_sem=ssend.at[sem], recv_sem=srecv.at[sem],
                    device_id=peer, device_id_type=MESH))
                arrivals.append(pltpu.make_async_remote_copy(
                    src_ref=small_in, dst_ref=small_all.at[sender], send_sem=ssend.at[sem], recv_sem=srecv.at[sem],
                    device_id=peer, device_id_type=MESH))
        for cp in sends:
            cp.start()
        for cp in arrivals:
            cp.wait_recv()
        for cp in sends:
            cp.wait_send()

    anyspace = pl.BlockSpec(memory_space=pl.ANY)
    vm = pl.BlockSpec(memory_space=pltpu.VMEM)
    out_shape = [SDS((N_CHIPS, g.shape[1] // 2, g.shape[2]), F32) for g in to_sibling]
    out_shape += [SDS(full.shape, F32) for full in shards]
    out_shape += [SDS((N_DEV,) + sm.shape, F32) for sm in smalls]
    n_d2d = max(n_g + n_h, 1)
    n_all = (N_DEV - 1) * max(n_s, 1)
    outs = pl.pallas_call(
        body, name=name, out_shape=out_shape,
        in_specs=[anyspace] * (n_g + n_h) + [vm] * n_s, out_specs=[anyspace] * (n_g + n_h) + [vm] * n_s,
        scratch_shapes=[pltpu.SemaphoreType.DMA((n_d2d,)), pltpu.SemaphoreType.DMA((n_d2d,)),
                        pltpu.SemaphoreType.DMA((n_all,)), pltpu.SemaphoreType.DMA((n_all,))],
        input_output_aliases={n_g + b: n_g + b for b in range(n_h)},
    )(*to_sibling, *shards, *smalls)
    return outs[:n_g], outs[n_g:n_g + n_h], outs[n_g + n_h:]


def _add_win(where, lo, hi, r, name):
    _, h, cols = lo.shape
    tr = min(h, 256)
    nh = h // tr

    def body(where_ref, lo_ref, hi_ref, r_ref, o_ref):
        acc = jnp.where(where_ref[0] == 0, lo_ref[0], hi_ref[0])
        for k in range(N_DEV - 1):
            acc = acc + r_ref[k].astype(F32)
        o_ref[...] = acc

    own = pl.BlockSpec((1, tr, cols), lambda i, w: (w[1], i, 0))
    return pl.pallas_call(
        body, name=name,
        grid_spec=pltpu.PrefetchScalarGridSpec(
            num_scalar_prefetch=1, grid=(nh,),
            in_specs=[own, own, pl.BlockSpec((N_DEV - 1, tr, cols), lambda i, w: (0, i, 0))],
            out_specs=pl.BlockSpec((tr, cols), lambda i, w: (w[0] * nh + i, 0))),
        out_shape=SDS((2 * h, cols), F32),
        compiler_params=_params(("parallel",)),
    )(where, lo, hi, r)


def _add_devices(where, g, r, name):
    _, rows, cols = g.shape
    h = rows // 2
    tr = min(h, 256)
    nh = h // tr

    def body(where_ref, g_ref, r_ref, o_ref):
        del where_ref
        acc = g_ref[0]
        for k in range(N_DEV - 1):
            acc = acc + r_ref[k].astype(F32)
        o_ref[...] = acc

    return pl.pallas_call(
        body, name=name,
        grid_spec=pltpu.PrefetchScalarGridSpec(
            num_scalar_prefetch=1, grid=(nh,),
            in_specs=[pl.BlockSpec((1, tr, cols), lambda i, w: (w[1], w[0] * nh + i, 0)),
                      pl.BlockSpec((N_DEV - 1, tr, cols), lambda i, w: (0, i, 0))],
            out_specs=pl.BlockSpec((tr, cols), lambda i, w: (w[0] * nh + i, 0))),
        out_shape=SDS((rows, cols), F32),
        compiler_params=_params(("parallel",)),
    )(where, g, r)


def _sum_smalls(gathered):
    n = len(gathered)

    def body(*refs):
        for all_ref, o_ref in zip(refs[:n], refs[n:]):
            acc = all_ref[0]
            for dev in range(1, N_DEV):
                acc = acc + all_ref[dev]
            o_ref[...] = acc

    vm = pl.BlockSpec(memory_space=pltpu.VMEM)
    return pl.pallas_call(
        body, name="sum_smalls", out_shape=[SDS(a.shape[1:], F32) for a in gathered],
        in_specs=[vm] * n, out_specs=[vm] * n,
    )(*gathered)


def _adam_step(g, w, m, v):
    nm = ADAM_B1 * m + (1.0 - ADAM_B1) * g
    nv = ADAM_B2 * v + (1.0 - ADAM_B2) * (g * g)
    m_hat = nm / (1.0 - ADAM_B1 ** ADAM_STEP)
    v_hat = nv / (1.0 - ADAM_B2 ** ADAM_STEP)
    return -ADAM_LR * (m_hat / (jnp.sqrt(v_hat) + ADAM_EPS) + ADAM_WD * w), nm, nv


def _adamw(g, w, m, v, name):
    rows, cols = g.shape
    tr = min(rows, 256)

    def body(g_ref, w_ref, m_ref, v_ref, d_ref, nm_ref, nv_ref):
        d_ref[...], nm_ref[...], nv_ref[...] = _adam_step(g_ref[...], w_ref[...], m_ref[...], v_ref[...])

    spec = pl.BlockSpec((tr, cols), lambda i: (i, 0))
    return pl.pallas_call(
        body, name=name, grid=(rows // tr,), in_specs=[spec] * 4, out_specs=[spec] * 3,
        out_shape=[SDS(g.shape, F32)] * 3, compiler_params=_params(("parallel",)),
    )(g, w, m, v)


def _small_update(chip, tot, tot_rel, wmv):
    names = list(SMALL_PLACES)
    n = len(names)

    def body(chip_ref, tot_ref, quarter_ref, rel_ref, *refs):
        del chip_ref
        ins, outs = refs[:3 * n], refs[3 * n:]
        for i, nm in enumerate(names):
            source, row, (rows, cols) = SMALL_PLACES[nm]
            g = {"rows": tot_ref, "quarter": quarter_ref, "rel": rel_ref}[source][row:row + rows, 0:cols]
            outs[4 * i][...] = g
            outs[4 * i + 1][...], outs[4 * i + 2][...], outs[4 * i + 3][...] = _adam_step(
                g, ins[3 * i][...], ins[3 * i + 1][...], ins[3 * i + 2][...])

    whole = lambda shape: pl.BlockSpec(shape, lambda i, c: (0,) * len(shape))
    shapes = [SMALL_PLACES[nm][2] for nm in names]
    outs = pl.pallas_call(
        body, name="small_update",
        grid_spec=pltpu.PrefetchScalarGridSpec(
            num_scalar_prefetch=1, grid=(1,),
            in_specs=[whole(tot.shape), pl.BlockSpec((tot.shape[0], D // 4), lambda i, c: (0, c[0])),
                      whole(tot_rel.shape)] + [whole(shp) for shp in shapes for _ in range(3)],
            out_specs=[whole(shp) for shp in shapes for _ in range(4)]),
        out_shape=[SDS(shp, F32) for shp in shapes for _ in range(4)],
    )(chip, tot, tot, tot_rel, *[a for nm in names for a in wmv[nm]])
    return {nm: tuple(outs[4 * i:4 * i + 4]) for i, nm in enumerate(names)}


def _pad_rows(a, rows):
    return jnp.concatenate([a, jnp.zeros((rows - a.shape[0], a.shape[1]), a.dtype)], axis=0)


def _pad_cols(a, cols):
    return jnp.concatenate([a, jnp.zeros((a.shape[0], cols - a.shape[1]), a.dtype)], axis=1)


def kernel(x, a_pre_norm, a_w_in, a_conv_w, a_w_out, a_post_norm, kv_norm, w_kv, rel_bias, b_pre_norm, b_w_in, b_sinks, b_w_out, b_post_norm, loss_target, m_a_pre_norm, m_a_w_in, m_a_conv_w, m_a_w_out, m_a_post_norm, m_kv_norm, m_w_kv, m_rel_bias, m_b_pre_norm, m_b_w_in, m_b_sinks, m_b_w_out, m_b_post_norm, v_a_pre_norm, v_a_w_in, v_a_conv_w, v_a_w_out, v_a_post_norm, v_kv_norm, v_w_kv, v_rel_bias, v_b_pre_norm, v_b_w_in, v_b_sinks, v_b_w_out, v_b_post_norm):
    seq = x.shape[1]
    xs = x.reshape(seq, D)
    tgt = loss_target.reshape(seq, D)
    chip = 2 * lax.axis_index("x") + lax.axis_index("y")
    core = lax.axis_index("c")
    tm = _tile(seq, 512)
    tmw = _tile(seq, 1024)

    shards = [a_w_in[0], a_w_out[0], w_kv, b_w_in[0], b_w_out[0]]
    small_w = _pad_rows(jnp.concatenate([a_pre_norm, a_conv_w[0], a_post_norm], axis=0), 8)
    *own_only, small_g = _prepare_weights(shards, small_w)
    where = jnp.stack([core, chip]).astype(jnp.int32)
    small_full = small_g.transpose(1, 0, 2).reshape(8, D)
    g_apre, conv_w, g_apost = small_full[0:1], _pad_rows(small_full[1:4], 8), small_full[4:5]
    g_kv = kv_norm.reshape(1, D)

    proj, n1, (win_g, wouta_g, wkv_g, wbin_g, woutb_g) = _a_in(where[1:2], xs, g_apre, own_only, tmw)
    wouta = wouta_g.reshape(D, D)
    wkv = wkv_g.reshape(D, 2 * KV_W)
    woutb = woutb_g.reshape(D, D)
    ya, oa, h1 = _a_mix(proj, xs, conv_w, wouta, g_apost, tm)
    kv, q, zb = _b_in(h1, g_kv, b_pre_norm, wkv, wbin_g, tmw)
    tab = _bias_table(rel_bias, b_sinks.reshape(N_HEADS))
    att, stats = _attn_fwd(q, kv, tab)
    dh2, dqz, datt, loss_acc, dg_bpost, dw_outb, dw_outb16 = _mid(att, zb, h1, tgt, woutb, b_post_norm, tm)

    dqz, dkv, dtab = _attn_bwd(q, kv, datt, stats, tab, dqz)
    dh1, doa, dg_b, dw_bin, dw_kv, dw_bin16, dw_kv16 = _b_bwd(dqz, dkv, h1, dh2, oa, wbin_g, wkv, g_kv, b_pre_norm,
                                                              g_apost, tm)
    by_chip = lambda a, cols: a.reshape(N_CHIPS, D // 4, cols)
    grads1 = [by_chip(dw_kv, 2 * KV_W), dw_bin, by_chip(dw_outb, D)]
    sent1 = [by_chip(dw_kv16, 2 * KV_W), dw_bin16, by_chip(dw_outb16, D)]
    names1 = ["w_kv", "b_w_in", "b_w_out"]
    dproj, dconv_w, dw_outa, dw_outa16, from_devices1 = _a_bwd(doa, ya, proj, conv_w, wouta, tm, sent1)
    shards1 = [_add_devices(where, g, r, "add_devices_" + nm) for g, r, nm in zip(grads1, from_devices1, names1)]
    tmw2 = _tile(seq, 4096)
    win_lo, win_lo16, outa_got = _dw_in_half(n1, dproj, 0, tmw2, "dw_a_in_lo", to_devices=by_chip(dw_outa16, D))
    win_hi, win_hi16, win_got = _dw_in_half(n1, dproj, 1, tmw2, "dw_a_in_hi", to_owners=win_lo16)
    nt = seq // tmw
    dn_first, win_got = _a_in_bwd_matmul(dproj, win_g, tmw, max(nt - max(nt // 4, 1), 1), win_hi16, win_got)
    grad_x, dg_apre = _a_in_bwd(dn_first, dproj, xs, dh1, win_g, g_apre, tm)
    shards2 = [_add_win(where, win_lo, win_hi, win_got, "add_devices_a_w_in"),
               _add_devices(where, by_chip(dw_outa, D), outa_got, "add_devices_a_w_out")]
    drel, dsink = _bias_fold(dtab)

    smalls = jnp.concatenate([
        dg_apre[0:1], dg_b[2:3], dg_b[0:1], dg_b[1:2], dg_bpost[0:1], _pad_cols(dsink[0:1], D),
        _pad_cols(loss_acc[0:1], D), jnp.zeros((1, D), F32), dconv_w], axis=0)
    assert smalls.shape == (SMALL_ROWS, D)
    _, (g_wkv, g_wbin, g_woutb, g_win, g_wouta), gathered = _sibling_exchange(
        "share_last", shards=shards1 + shards2, smalls=(smalls, drel))
    tot, tot_rel = _sum_smalls(gathered)

    big = {}
    for nm, g, w, m, v in [("a_w_in", g_win, a_w_in, m_a_w_in, v_a_w_in), ("a_w_out", g_wouta, a_w_out, m_a_w_out, v_a_w_out),
                           ("w_kv", g_wkv, w_kv, m_w_kv, v_w_kv), ("b_w_in", g_wbin, b_w_in, m_b_w_in, v_b_w_in),
                           ("b_w_out", g_woutb, b_w_out, m_b_w_out, v_b_w_out)]:
        shp = w.shape
        two = (shp[-2], shp[-1])
        d, nm_, nv_ = _adamw(g, w.reshape(two), m.reshape(two), v.reshape(two), "adamw_" + nm)
        big[nm] = (g.reshape(shp), d.reshape(shp), nm_.reshape(shp), nv_.reshape(shp))

    given = {"a_pre_norm": (a_pre_norm, m_a_pre_norm, v_a_pre_norm), "a_conv_w": (a_conv_w, m_a_conv_w, v_a_conv_w),
             "a_post_norm": (a_post_norm, m_a_post_norm, v_a_post_norm), "kv_norm": (kv_norm, m_kv_norm, v_kv_norm),
             "rel_bias": (rel_bias, m_rel_bias, v_rel_bias), "b_pre_norm": (b_pre_norm, m_b_pre_norm, v_b_pre_norm),
             "b_sinks": (b_sinks, m_b_sinks, v_b_sinks), "b_post_norm": (b_post_norm, m_b_post_norm, v_b_post_norm)}
    small = _small_update(where[1:2], tot, tot_rel, {nm: tuple(a.reshape(SMALL_PLACES[nm][2]) for a in wmv)
                                            for nm, wmv in given.items()})
    order = ["a_pre_norm", "a_w_in", "a_conv_w", "a_w_out", "a_post_norm", "kv_norm", "w_kv", "rel_bias",
             "b_pre_norm", "b_w_in", "b_sinks", "b_w_out", "b_post_norm"]
    outs = []
    for which in range(4):
        for nm in order:
            outs.append(big[nm][which] if nm in big else small[nm][which].reshape(given[nm][0].shape))
    loss = 0.5 * tot[LOSS_ROW, 0]
    return (loss, grad_x.reshape(x.shape), *outs)
```

```python
import math

import jax
import jax.numpy as jnp
from jax import lax
from jax.experimental import pallas as pl
from jax.experimental.pallas import tpu as pltpu

F32 = jnp.float32
BF16 = jnp.bfloat16
MESH = pl.DeviceIdType.MESH
SDS = jax.ShapeDtypeStruct

D = 1024
HEAD_DIM = 64
N_HEADS = 16
N_KV = 2
GROUP = 8
KV_W = 128
BLK = 128
N_BUCKETS = 32
MAX_EXACT = 16
MAX_DISTANCE = 128
EPS = 1e-6
NEG_INF = -1e30
Q_SCALE = HEAD_DIM ** -0.5

ADAM_LR = 0.001
ADAM_B1 = 0.9
ADAM_B2 = 0.999
ADAM_EPS = 1e-08
ADAM_WD = 0.01
ADAM_STEP = 10

N_CHIPS = 4
N_DEV = 8
BIN_COLS = 2 * D // N_CHIPS
VMEM_LIMIT = 56 * 1024 * 1024
SMALL_ROWS = 16
LOSS_ROW = 6
SMALL_PLACES = {
    "a_pre_norm": ("quarter", 0, (1, D // 4)), "a_conv_w": ("quarter", 8, (3, D // 4)),
    "a_post_norm": ("quarter", 1, (1, D // 4)), "kv_norm": ("rows", 2, (1, D)),
    "rel_bias": ("rel", 0, (N_BUCKETS, N_HEADS)), "b_pre_norm": ("rows", 3, (1, D)),
    "b_sinks": ("rows", 5, (1, N_HEADS)), "b_post_norm": ("rows", 4, (1, D)),
}
HALO = 16


def _bucket_thresholds():
    def bucket(d):
        big = MAX_EXACT + int(math.log(d / MAX_EXACT) / math.log(MAX_DISTANCE / MAX_EXACT)
                              * (N_BUCKETS - MAX_EXACT))
        return d if d < MAX_EXACT else min(big, N_BUCKETS - 1)
    out = []
    for b in range(MAX_EXACT + 1, N_BUCKETS):
        out.append(min(d for d in range(MAX_EXACT, MAX_DISTANCE) if bucket(d) >= b))
    return tuple(out)


BUCKET_THRESHOLDS = _bucket_thresholds()


def _params(semantics=None, vmem=VMEM_LIMIT):
    return pltpu.CompilerParams(dimension_semantics=semantics, vmem_limit_bytes=vmem)


def _tile(n, pref):
    return pref if n >= 2 * pref else max(n // 2, 8)


def _rms_scale(v):
    return lax.rsqrt(jnp.mean(v * v, axis=-1, keepdims=True) + EPS)


def _nt(a, b):
    return lax.dot_general(a, b, (((1,), (1,)), ((), ())), preferred_element_type=F32)


def _tn(a, b):
    return lax.dot_general(a, b, (((0,), (0,)), ((), ())), preferred_element_type=F32)


def _nn(a, b):
    return jnp.dot(a, b, preferred_element_type=F32)


def _silu_parts(z):
    sg = jax.nn.sigmoid(z)
    return sg, z * sg


def _dsilu(z, sg):
    return sg * (1.0 + z * (1.0 - sg))


def _write_gradient(acc, out32, out16, stage):
    pltpu.sync_copy(acc, out32)
    rows = stage.shape[0]
    for k in range(acc.shape[0] // rows):
        stage[...] = acc[rows * k:rows * (k + 1), :].astype(BF16)
        pltpu.sync_copy(stage, out16.at[pl.ds(rows * k, rows)])


def _acc_row(ref, row, val):
    ref[row:row + 1, :] += val


def _gather_copies(outs, splits, ici_send, ici_recv, d2d_send, d2d_recv):
    x, y, c = lax.axis_index("x"), lax.axis_index("y"), lax.axis_index("c")
    k = 2 * x + y
    sibling = (x, y, 1 - c)

    def part(o_ref, chip, core, split):
        if not split:
            return o_ref.at[chip]
        h = o_ref.shape[1] // 2
        return o_ref.at[chip, pl.ds(pl.multiple_of(core * h, 16), h)]

    def remote(ref, a, j, sems, to):
        return pltpu.make_async_remote_copy(src_ref=ref, dst_ref=ref, send_sem=sems[0].at[3 * a + j],
                                            recv_sem=sems[1].at[3 * a + j], device_id=to, device_id_type=MESH)

    copies = []
    for a, (o_ref, split) in enumerate(zip(outs, splits)):
        for j, (px, py) in enumerate([(x, 1 - y), (1 - x, y), (1 - x, 1 - y)]):
            kj = 2 * px + py
            ici, d2d = (ici_send, ici_recv), (d2d_send, d2d_recv)
            copies.append((remote(part(o_ref, k, c, split), a, j, ici, (px, py, c)),
                           remote(part(o_ref, kj, c, split), a, j, ici, (px, py, c)),
                           remote(part(o_ref, kj, c, split), a, j, d2d, sibling) if split else None,
                           remote(part(o_ref, kj, 1 - c, split), a, j, d2d, sibling) if split else None))
    return copies


def _gather_sems(n):
    return [pltpu.SemaphoreType.DMA((3 * n,)) for _ in range(4)]


def _prepare_weights(shards, small):
    n = len(shards)

    def body(*refs):
        ins, small_in = refs[:n], refs[n]
        outs, small_out = refs[n + 1:2 * n + 1], refs[2 * n + 1]
        stages, put_sem = refs[2 * n + 2:3 * n + 2], refs[3 * n + 2]
        sems = refs[3 * n + 3:]
        k = 2 * lax.axis_index("x") + lax.axis_index("y")
        puts = []
        for a, (i_ref, stage, o_ref) in enumerate(zip(ins, stages, outs)):
            stage[...] = i_ref[...].astype(BF16)
            puts.append(pltpu.make_async_copy(stage, o_ref.at[k], put_sem.at[a]))
            puts[-1].start()
        small_out[k] = small_in[...]
        copies = _gather_copies([small_out], [False], *sems)
        for send, _, _, _ in copies:
            send.start()
        for _, arrival, _, _ in copies:
            arrival.wait_recv()
        for send, _, _, _ in copies:
            send.wait_send()
        for put in puts:
            put.wait()

    vm = pl.BlockSpec(memory_space=pltpu.VMEM)
    anyspace = pl.BlockSpec(memory_space=pl.ANY)
    out_shape = [SDS((N_CHIPS,) + s.shape, BF16) for s in shards] + [SDS((N_CHIPS,) + small.shape, F32)]
    return pl.pallas_call(
        body, name="prepare_weights", out_shape=out_shape,
        in_specs=[vm] * (n + 1), out_specs=[anyspace] * n + [vm],
        scratch_shapes=[pltpu.VMEM(s.shape, BF16) for s in shards] + [pltpu.SemaphoreType.DMA((n,))] + _gather_sems(1),
        compiler_params=pltpu.CompilerParams(vmem_limit_bytes=VMEM_LIMIT),
    )(*shards, small)


def _a_in(chip, x, g_pre, weights, tm):
    s = x.shape[0]
    nt = s // tm
    n = len(weights)

    def body(chip_ref, x_ref, g_ref, *refs):
        proj_ref, n1_ref = refs[n:n + 2]
        gathered = refs[n + 2:2 * n + 2]
        wbuf, n1_all, fetch_sem = refs[2 * n + 2:2 * n + 5]
        sems = refs[2 * n + 5:]
        jj, i = pl.program_id(0), pl.program_id(1)
        copies = _gather_copies(gathered, [True] * n, *sems)

        def fetch(rel):
            slot = jnp.bitwise_xor(chip_ref[0], rel)
            return pltpu.make_async_copy(gathered[0].at[slot], wbuf.at[rel % 2], fetch_sem.at[rel % 2])

        @pl.when((jj == 0) & (i == 0))
        def _():
            fetch(0).start()
            copies[0][0].start()
            copies[1][0].start()
            fetch(0).wait()

        for rel in (1, 2, 3):
            @pl.when((jj == rel) & (i == 0))
            def _():
                fetch(rel).wait()

        @pl.when(jj == 0)
        def _():
            xv = x_ref[...]
            n1 = (xv * _rms_scale(xv) * g_ref[...]).astype(BF16)
            n1_ref[...] = n1
            n1_all[i] = n1
        proj_ref[...] = _nn(n1_all[i], wbuf[jj % 2]).astype(BF16)

        for rel in (1, 2, 3):
            @pl.when((jj == rel - 1) & (i == max(nt - 2, nt // 2)))
            def _():
                _, arrival, forward, forwarded = copies[rel - 1]
                arrival.wait_recv()
                forward.start()
                forwarded.wait_recv()
                fetch(rel).start()
                if rel == 1:
                    for send, _, _, _ in copies[2:]:
                        send.start()

        @pl.when((jj == 3) & (i == max(nt - 2, 0)))
        def _():
            for _, arrival, forward, _ in copies[3:]:
                arrival.wait_recv()
                forward.start()

        @pl.when((jj == 3) & (i == nt - 1))
        def _():
            for _, _, _, forwarded in copies[3:]:
                forwarded.wait_recv()
            for send, _, forward, _ in copies:
                forward.wait_send()
                send.wait_send()

    anyspace = pl.BlockSpec(memory_space=pl.ANY)
    proj, n1, *gathered = pl.pallas_call(
        body, name="a_in",
        grid_spec=pltpu.PrefetchScalarGridSpec(
            num_scalar_prefetch=1, grid=(4, nt),
            in_specs=[pl.BlockSpec((tm, D), lambda jj, i, c: (jnp.where(jj == 0, i, nt - 1), 0)),
                      pl.BlockSpec((1, D), lambda jj, i, c: (0, 0))] + [anyspace] * n,
            out_specs=[pl.BlockSpec((tm, D), lambda jj, i, c: (i, jnp.bitwise_xor(c[0], jj))),
                       pl.BlockSpec((tm, D), lambda jj, i, c: (jnp.where(jj == 0, i, nt - 1), 0))] + [anyspace] * n,
            scratch_shapes=[pltpu.VMEM((2, D, D), BF16), pltpu.VMEM((nt, tm, D), BF16),
                            pltpu.SemaphoreType.DMA((2,))] + _gather_sems(n)),
        out_shape=[SDS((s, 4 * D), BF16), SDS((s, D), BF16)] + [SDS(w.shape, w.dtype) for w in weights],
        input_output_aliases={3 + a: 2 + a for a in range(n)},
        compiler_params=_params(("arbitrary", "arbitrary")),
    )(chip, x, g_pre, *weights)
    return proj, n1, gathered


def _shift_rows(v, last, second_last, rows):
    v1 = jnp.where(rows >= 1, pltpu.roll(v, 1, 0), last)
    v2 = jnp.where(rows >= 2, pltpu.roll(v, 2, 0), jnp.where(rows == 1, last, second_last))
    return v1, v2


def _a_mix(proj, x, conv_w, w_out, g_post, tm):
    s = x.shape[0]

    def body(proj_ref, x_ref, cw_ref, w_ref, g_ref, ya_ref, oa_ref, h1_ref, carry):
        @pl.when(pl.program_id(0) == 0)
        def _():
            carry[...] = jnp.zeros_like(carry)
        v = proj_ref[:, D:2 * D].astype(F32) * proj_ref[:, 2 * D:3 * D].astype(F32)
        rows = lax.broadcasted_iota(jnp.int32, (tm, D), 0)
        before = carry[...]
        v1, v2 = _shift_rows(v, before[7:8, :], before[6:7, :], rows)
        carry[...] = v[tm - 8:tm, :]
        conv = cw_ref[0:1, :] * v2 + cw_ref[1:2, :] * v1 + cw_ref[2:3, :] * v
        _, sz = _silu_parts(proj_ref[:, 3 * D:4 * D].astype(F32))
        ya = (proj_ref[:, 0:D].astype(F32) * conv * sz).astype(BF16)
        ya_ref[...] = ya
        oa = _nn(ya, w_ref[...])
        oa_ref[...] = oa.astype(BF16)
        h1_ref[...] = x_ref[...] + oa * _rms_scale(oa) * g_ref[...]

    row = lambda i: (i, 0)
    fix = lambda i: (0, 0)
    return pl.pallas_call(
        body, name="a_mix", grid=(s // tm,),
        in_specs=[pl.BlockSpec((tm, 4 * D), row), pl.BlockSpec((tm, D), row), pl.BlockSpec((8, D), fix),
                  pl.BlockSpec((D, D), fix), pl.BlockSpec((1, D), fix)],
        out_specs=[pl.BlockSpec((tm, D), row)] * 3,
        out_shape=[SDS((s, D), BF16), SDS((s, D), BF16), SDS((s, D), F32)],
        scratch_shapes=[pltpu.VMEM((8, D), F32)],
        compiler_params=_params(("arbitrary",)),
    )(proj, x, conv_w, w_out, g_post)


def _b_in(h1, g_kv, g_pre, w_kv, wbin_g, tm):
    s = h1.shape[0]

    def body(h_ref, gk_ref, gb_ref, wkv_ref, wb_ref, kv_ref, q_ref, z_ref):
        h = h_ref[...]
        hh = h * _rms_scale(h)
        nk = (hh * gk_ref[...]).astype(BF16)
        nb = (hh * gb_ref[...]).astype(BF16)
        kv_ref[...] = _nn(nk, wkv_ref[...]).astype(BF16)
        for j in range(2):
            q_ref[:, BIN_COLS * j:BIN_COLS * (j + 1)] = (_nn(nb, wb_ref[j]) * Q_SCALE).astype(BF16)
            z_ref[:, BIN_COLS * j:BIN_COLS * (j + 1)] = _nn(nb, wb_ref[2 + j]).astype(BF16)

    row = lambda i: (i, 0)
    fix = lambda i: (0, 0)
    return pl.pallas_call(
        body, name="b_in", grid=(s // tm,),
        in_specs=[pl.BlockSpec((tm, D), row), pl.BlockSpec((1, D), fix), pl.BlockSpec((1, D), fix),
                  pl.BlockSpec((D, 2 * KV_W), fix), pl.BlockSpec((N_CHIPS, D, BIN_COLS), lambda i: (0, 0, 0))],
        out_specs=[pl.BlockSpec((tm, 2 * KV_W), row), pl.BlockSpec((tm, D), row), pl.BlockSpec((tm, D), row)],
        out_shape=[SDS((s, 2 * KV_W), BF16), SDS((s, D), BF16), SDS((s, D), BF16)],
        compiler_params=_params(("parallel",)),
    )(h1, g_kv, g_pre, w_kv, wbin_g)


def _band_buckets():
    q = lax.broadcasted_iota(jnp.int32, (BLK, 2 * BLK), 0)
    k = lax.broadcasted_iota(jnp.int32, (BLK, 2 * BLK), 1)
    dist = q + BLK - k
    bucket = jnp.where(dist < MAX_EXACT, dist, MAX_EXACT)
    for t in BUCKET_THRESHOLDS:
        bucket = bucket + jnp.where(dist >= t, 1, 0)
    in_window = (dist >= 0) & (dist < BLK)
    return jnp.where(in_window, bucket, -1)


def _head_place(h):
    kh, j, e = h // GROUP, (h % GROUP) // 2, h % 2
    return kh, slice(BLK * j, BLK * (j + 1)), slice(2 * BLK * e, 2 * BLK * (e + 1))


def _bias_table(rel_bias, sinks):
    def body(rb_ref, sink_ref, tab_ref):
        bucket = _band_buckets()
        col = lax.broadcasted_iota(jnp.int32, (BLK, 2 * BLK), 1)
        for h in range(N_HEADS):
            acc = jnp.where(bucket < 0, NEG_INF, 0.0).astype(F32)
            for b in range(N_BUCKETS):
                acc = jnp.where(bucket == b, rb_ref[b, h], acc)
            acc = jnp.where(col == 0, sink_ref[h], acc)
            kh, rows, cols = _head_place(h)
            tab_ref[1, kh, rows, cols] = acc
            tab_ref[0, kh, rows, cols] = jnp.where((col > 0) & (col < BLK), NEG_INF, acc)

    return pl.pallas_call(
        body, name="bias_table", out_shape=SDS((2, N_KV, 4 * BLK, 4 * BLK), F32),
        in_specs=[pl.BlockSpec(memory_space=pltpu.SMEM), pl.BlockSpec(memory_space=pltpu.SMEM)],
        out_specs=pl.BlockSpec(memory_space=pltpu.VMEM),
    )(rel_bias, sinks)


def _bias_fold(dtab):
    def body(dtab_ref, out_ref, dsink_ref):
        bucket = _band_buckets()
        row = lax.broadcasted_iota(jnp.int32, (N_BUCKETS, 128), 0)
        lane = lax.broadcasted_iota(jnp.int32, (N_BUCKETS, 128), 1)
        row8 = lax.broadcasted_iota(jnp.int32, (8, 128), 0)
        lane8 = lax.broadcasted_iota(jnp.int32, (8, 128), 1)
        acc = jnp.zeros((N_BUCKETS, 128), F32)
        dsink = jnp.zeros((8, 128), F32)
        for h in range(N_HEADS):
            kh, rows, cols = _head_place(h)
            dt = dtab_ref[kh, rows, cols]
            for b in range(N_BUCKETS):
                val = jnp.sum(jnp.where(bucket == b, dt, 0.0))
                acc = acc + jnp.where((row == b) & (lane == h), val, 0.0)
            dsink = dsink + jnp.where((row8 == 0) & (lane8 == h), jnp.sum(dt[:, 0:1]), 0.0)
        out_ref[...] = acc
        dsink_ref[...] = dsink

    vm = pl.BlockSpec(memory_space=pltpu.VMEM)
    return pl.pallas_call(
        body, name="bias_fold", out_shape=[SDS((N_BUCKETS, 128), F32), SDS((8, 128), F32)],
        in_specs=[vm], out_specs=[vm, vm],
    )(dtab)


def _pair_operands(prev, cur):
    t = jnp.concatenate([prev, cur], axis=0).astype(F32)
    t = jnp.where(lax.broadcasted_iota(jnp.int32, t.shape, 0) == 0, 0.0, t)
    tr = pltpu.roll(t, HEAD_DIM, 1)
    lo = lax.broadcasted_iota(jnp.int32, t.shape, 1) < HEAD_DIM
    zero = jnp.zeros_like(t)
    head0 = jnp.concatenate([jnp.where(lo, t, zero), jnp.where(lo, zero, tr)], axis=0).astype(BF16)
    head1 = jnp.concatenate([jnp.where(lo, tr, zero), jnp.where(lo, zero, t)], axis=0).astype(BF16)
    return head0, head1


def _pair_fold(d0, d1):
    lo = lax.broadcasted_iota(jnp.int32, (2 * BLK, KV_W), 1) < HEAD_DIM
    zero = jnp.zeros((2 * BLK, KV_W), F32)
    g0 = jnp.where(lo, d0[0:256], zero) + pltpu.roll(jnp.where(lo, zero, d0[256:512]), HEAD_DIM, 1)
    g1 = pltpu.roll(jnp.where(lo, d1[0:256], zero), HEAD_DIM, 1) + jnp.where(lo, zero, d1[256:512])
    return jnp.where(lax.broadcasted_iota(jnp.int32, (2 * BLK, KV_W), 0) == 0, 0.0, g0 + g1)


def _stack_pairs(ref, kh, rows=slice(None)):
    return jnp.concatenate([ref[rows, 128 * (4 * kh + j):128 * (4 * kh + j + 1)] for j in range(4)], axis=0)


def _table_spec():
    return pl.BlockSpec((1, N_KV, 4 * BLK, 4 * BLK), lambda n: (jnp.minimum(n, 1), 0, 0, 0))


def _attn_fwd(q, kv, tab):
    s = q.shape[0]

    def body(q_ref, kp_ref, k0_ref, k1_ref, vp_ref, v0_ref, v1_ref, tab0_ref, tab1_ref, att_ref, stats_ref):
        lane = lax.broadcasted_iota(jnp.int32, (BLK, 128), 1)
        for sub, (kp, kc, vp, vc, tab_ref) in enumerate([(kp_ref, k0_ref, vp_ref, v0_ref, tab0_ref),
                                                         (k0_ref, k1_ref, v0_ref, v1_ref, tab1_ref)]):
            rows = slice(BLK * sub, BLK * (sub + 1))
            k2 = _pair_operands(kp[...], kc[...])
            v2 = _pair_operands(vp[...], vc[...])
            stats = jnp.zeros((BLK, 128), F32)
            for kh in range(N_KV):
                sc = _nt(_stack_pairs(q_ref, kh, rows), k2[kh])
                ps = []
                for e in range(2):
                    lg = sc[:, 256 * e:256 * (e + 1)] + tab_ref[0, kh, :, 256 * e:256 * (e + 1)]
                    m = jnp.max(lg, axis=-1, keepdims=True)
                    ex = jnp.exp(lg - m)
                    den = jnp.sum(ex, axis=-1, keepdims=True)
                    ps.append(ex * (1.0 / den))
                    lse = m + jnp.log(den)
                    for j in range(4):
                        stats = jnp.where(lane == GROUP * kh + 2 * j + e, lse[BLK * j:BLK * (j + 1)], stats)
                out = _nn(jnp.concatenate(ps, axis=1).astype(BF16), v2[kh])
                for j in range(4):
                    att_ref[rows, 128 * (4 * kh + j):128 * (4 * kh + j + 1)] = out[BLK * j:BLK * (j + 1)].astype(BF16)
            stats_ref[rows, :] = stats

    two = lambda m: (m, 0)
    table = lambda pick: pl.BlockSpec((1, N_KV, 4 * BLK, 4 * BLK), lambda m: (pick(m), 0, 0, 0))
    return pl.pallas_call(
        body, name="attn_fwd", grid=(s // (2 * BLK),),
        in_specs=[pl.BlockSpec((2 * BLK, D), two)]
        + [pl.BlockSpec((BLK, KV_W), lambda m, col=col, off=off: (jnp.maximum(2 * m + off, 0), col))
           for col in (0, 1) for off in (-1, 0, 1)]
        + [table(lambda m: jnp.minimum(m, 1)), table(lambda m: 1)],
        out_specs=[pl.BlockSpec((2 * BLK, D), two), pl.BlockSpec((2 * BLK, 128), two)],
        out_shape=[SDS((s, D), BF16), SDS((s, 128), F32)],
        compiler_params=_params(("parallel",)),
    )(q, kv, kv, kv, kv, kv, kv, tab, tab)


def _mid(att, zb, h1, tgt, w_out, g_post, tm):
    s = att.shape[0]
    nt = s // tm

    def body(att_ref, z_ref, h1_ref, t_ref, w_ref, g_ref,
             dh_ref, dqz_ref, datt_ref, loss_ref, dg_ref, dw_ref, dw16_ref, dw_acc, stage):
        @pl.when(pl.program_id(0) == 0)
        def _():
            loss_ref[...] = jnp.zeros_like(loss_ref)
            dg_ref[...] = jnp.zeros_like(dg_ref)
            dw_acc[...] = jnp.zeros_like(dw_acc)
        att = att_ref[...].astype(F32)
        z = z_ref[...].astype(F32)
        sg, sz = _silu_parts(z)
        ob = (att * sz).astype(BF16)
        y2 = _nn(ob, w_ref[...])
        r2 = _rms_scale(y2)
        yh = y2 * r2
        g = g_ref[...]
        err = (h1_ref[...] + yh * g) - t_ref[...]
        loss_ref[...] += jnp.sum(jnp.sum(err * err, axis=-1, keepdims=True) / D)
        dh = err / D
        dh_ref[...] = dh
        _acc_row(dg_ref, 0, jnp.sum(dh * yh, axis=0, keepdims=True))
        dyh = dh * g
        dy = (r2 * (dyh - yh * jnp.mean(dyh * yh, axis=-1, keepdims=True))).astype(BF16)
        dw_acc[...] += _tn(ob, dy)
        dob = _nt(dy, w_ref[...])
        datt_ref[...] = (dob * sz).astype(BF16)
        dqz_ref[...] = (dob * att * _dsilu(z, sg)).astype(BF16)

        @pl.when(pl.program_id(0) == nt - 1)
        def _():
            _write_gradient(dw_acc, dw_ref, dw16_ref, stage)

    row = lambda i: (i, 0)
    fix = lambda i: (0, 0)
    anyspace = pl.BlockSpec(memory_space=pl.ANY)
    return pl.pallas_call(
        body, name="mid", grid=(nt,),
        in_specs=[pl.BlockSpec((tm, D), row)] * 4 + [pl.BlockSpec((D, D), fix), pl.BlockSpec((1, D), fix)],
        out_specs=[pl.BlockSpec((tm, D), row), pl.BlockSpec((tm, D), lambda i: (i, 1)), pl.BlockSpec((tm, D), row),
                   pl.BlockSpec((8, 128), fix), pl.BlockSpec((8, D), fix), anyspace, anyspace],
        out_shape=[SDS((s, D), F32), SDS((s, 2 * D), BF16), SDS((s, D), BF16), SDS((8, 128), F32),
                   SDS((8, D), F32), SDS((D, D), F32), SDS((D, D), BF16)],
        scratch_shapes=[pltpu.VMEM((D, D), F32), pltpu.VMEM((D // 4, D), BF16)],
        compiler_params=_params(("arbitrary",)),
    )(att, zb, h1, tgt, w_out, g_post)


def _attn_bwd(q, kv, datt, stats, tab, dqz):
    s = q.shape[0]
    nb = s // BLK

    def body(q_ref, kp_ref, kc_ref, vp_ref, vc_ref, da_ref, st_ref, tab_ref, dqz_in,
             dq_ref, dkv_ref, dtab_ref, dk_carry, dv_carry):
        del dqz_in
        n = pl.program_id(0)

        @pl.when(n == 0)
        def _():
            dtab_ref[...] = jnp.zeros_like(dtab_ref)
            dk_carry[...] = jnp.zeros_like(dk_carry)
            dv_carry[...] = jnp.zeros_like(dv_carry)

        @pl.when(n < nb)
        def _():
            k2 = _pair_operands(kp_ref[...], kc_ref[...])
            v2 = _pair_operands(vp_ref[...], vc_ref[...])
            lane = lax.broadcasted_iota(jnp.int32, (BLK, 128), 1)
            stats = st_ref[...]
            dk2, dv2 = [], []
            for kh in range(N_KV):
                qs = _stack_pairs(q_ref, kh)
                das = _stack_pairs(da_ref, kh)
                sc = _nt(qs, k2[kh])
                dp = _nt(das, v2[kh])
                ps, dss = [], []
                for e in range(2):
                    heads = [GROUP * kh + 2 * j + e for j in range(4)]
                    lse = jnp.concatenate([jnp.sum(jnp.where(lane == h, stats, 0.0), axis=-1, keepdims=True)
                                           for h in heads], axis=0)
                    cols = slice(256 * e, 256 * (e + 1))
                    p = jnp.exp(sc[:, cols] + tab_ref[0, kh, :, cols] - lse)
                    delta = jnp.sum(p * dp[:, cols], axis=-1, keepdims=True)
                    ds = p * (dp[:, cols] - delta)
                    dtab_ref[kh, :, cols] += ds
                    ps.append(p)
                    dss.append(ds)
                p2 = jnp.concatenate(ps, axis=1).astype(BF16)
                ds2 = jnp.concatenate(dss, axis=1).astype(BF16)
                dq = _nn(ds2, k2[kh]) * Q_SCALE
                for j in range(4):
                    dq_ref[:, 128 * (4 * kh + j):128 * (4 * kh + j + 1)] = dq[BLK * j:BLK * (j + 1)].astype(BF16)
                dk2.append(_tn(ds2, qs))
                dv2.append(_tn(p2, das))
            dkk = _pair_fold(dk2[0], dk2[1])
            dvv = _pair_fold(dv2[0], dv2[1])
            dkv_ref[:, 0:KV_W] = (dk_carry[...] + dkk[0:BLK]).astype(BF16)
            dkv_ref[:, KV_W:2 * KV_W] = (dv_carry[...] + dvv[0:BLK]).astype(BF16)
            dk_carry[...] = dkk[BLK:2 * BLK]
            dv_carry[...] = dvv[BLK:2 * BLK]

        @pl.when(n == nb)
        def _():
            dkv_ref[:, 0:KV_W] = dk_carry[...].astype(BF16)
            dkv_ref[:, KV_W:2 * KV_W] = dv_carry[...].astype(BF16)

    cur = lambda n: (jnp.minimum(n, nb - 1), 0)
    prev = lambda n: (jnp.clip(n - 1, 0, nb - 1), 0)
    return pl.pallas_call(
        body, name="attn_bwd", grid=(nb + 1,),
        in_specs=[pl.BlockSpec((BLK, D), cur),
                  pl.BlockSpec((BLK, KV_W), prev), pl.BlockSpec((BLK, KV_W), cur),
                  pl.BlockSpec((BLK, KV_W), lambda n: (jnp.clip(n - 1, 0, nb - 1), 1)),
                  pl.BlockSpec((BLK, KV_W), lambda n: (jnp.minimum(n, nb - 1), 1)),
                  pl.BlockSpec((BLK, D), cur), pl.BlockSpec((BLK, 128), cur), _table_spec(),
                  pl.BlockSpec(memory_space=pl.ANY)],
        out_specs=[pl.BlockSpec((BLK, D), cur), pl.BlockSpec((BLK, 2 * KV_W), prev),
                   pl.BlockSpec((N_KV, 4 * BLK, 4 * BLK), lambda n: (0, 0, 0))],
        out_shape=[SDS((s, 2 * D), BF16), SDS((s, 2 * KV_W), BF16), SDS((N_KV, 4 * BLK, 4 * BLK), F32)],
        scratch_shapes=[pltpu.VMEM((BLK, KV_W), F32), pltpu.VMEM((BLK, KV_W), F32)],
        input_output_aliases={8: 0},
        compiler_params=_params(("arbitrary",)),
    )(q, kv, kv, kv, kv, datt, stats, tab, dqz)


def _b_bwd(dqz, dkv, h1, dh2, oa, wbin_g, w_kv, g_kv, g_pre, g_apost, tm):
    s = h1.shape[0]
    nt = s // tm

    def body(dqz_ref, dkv_ref, h_ref, dh2_ref, oa_ref, wb_ref, wkv_ref, gk_ref, gb_ref, ga_ref,
             dh1_ref, doa_ref, dg_ref, dwb_ref, dwkv_ref, dwb16_ref, dwkv16_ref, wcat, dwb_acc, dwkv_acc):
        @pl.when(pl.program_id(0) == 0)
        def _():
            dg_ref[...] = jnp.zeros_like(dg_ref)
            dwb_acc[...] = jnp.zeros_like(dwb_acc)
            dwkv_acc[...] = jnp.zeros_like(dwkv_acc)
            for j in range(N_CHIPS):
                pltpu.sync_copy(wb_ref.at[j], wcat.at[:, pl.ds(BIN_COLS * j, BIN_COLS)])
        dnb = _nt(dqz_ref[...], wcat[...])
        dnk = _nt(dkv_ref[...], wkv_ref[...])
        h = h_ref[...]
        r = _rms_scale(h)
        hh = h * r
        dwb_acc[...] += _tn((hh * gb_ref[...]).astype(BF16), dqz_ref[...])
        dwkv_acc[...] += _tn((hh * gk_ref[...]).astype(BF16), dkv_ref[...])
        _acc_row(dg_ref, 0, jnp.sum(dnk * hh, axis=0, keepdims=True))
        _acc_row(dg_ref, 1, jnp.sum(dnb * hh, axis=0, keepdims=True))
        dhh = dnb * gb_ref[...] + dnk * gk_ref[...]
        dh1 = dh2_ref[...] + r * (dhh - hh * jnp.mean(dhh * hh, axis=-1, keepdims=True))
        dh1_ref[...] = dh1
        oa = oa_ref[...].astype(F32)
        ra = _rms_scale(oa)
        oh = oa * ra
        _acc_row(dg_ref, 2, jnp.sum(dh1 * oh, axis=0, keepdims=True))
        doh = dh1 * ga_ref[...]
        doa_ref[...] = (ra * (doh - oh * jnp.mean(doh * oh, axis=-1, keepdims=True))).astype(BF16)

        @pl.when(pl.program_id(0) == nt - 1)
        def _():
            wcat[...] = dwb_acc[...].astype(BF16)
            for j in range(N_CHIPS):
                pltpu.sync_copy(dwb_acc.at[:, pl.ds(BIN_COLS * j, BIN_COLS)], dwb_ref.at[j])
                pltpu.sync_copy(wcat.at[:, pl.ds(BIN_COLS * j, BIN_COLS)], dwb16_ref.at[j])
            pltpu.sync_copy(dwkv_acc, dwkv_ref)
            wcat[:, 0:2 * KV_W] = dwkv_acc[...].astype(BF16)
            pltpu.sync_copy(wcat.at[:, pl.ds(0, 2 * KV_W)], dwkv16_ref)

    row = lambda i: (i, 0)
    fix = lambda i: (0, 0)
    anyspace = pl.BlockSpec(memory_space=pl.ANY)
    return pl.pallas_call(
        body, name="b_bwd", grid=(nt,),
        in_specs=[pl.BlockSpec((tm, 2 * D), row), pl.BlockSpec((tm, 2 * KV_W), row), pl.BlockSpec((tm, D), row),
                  pl.BlockSpec((tm, D), row), pl.BlockSpec((tm, D), row), anyspace, pl.BlockSpec((D, 2 * KV_W), fix),
                  pl.BlockSpec((1, D), fix), pl.BlockSpec((1, D), fix), pl.BlockSpec((1, D), fix)],
        out_specs=[pl.BlockSpec((tm, D), row), pl.BlockSpec((tm, D), row), pl.BlockSpec((8, D), fix)] + [anyspace] * 4,
        out_shape=[SDS((s, D), F32), SDS((s, D), BF16), SDS((8, D), F32), SDS((N_CHIPS, D, BIN_COLS), F32),
                   SDS((D, 2 * KV_W), F32), SDS((N_CHIPS, D, BIN_COLS), BF16), SDS((D, 2 * KV_W), BF16)],
        scratch_shapes=[pltpu.VMEM((D, 2 * D), BF16), pltpu.VMEM((D, 2 * D), F32), pltpu.VMEM((D, 2 * KV_W), F32)],
        compiler_params=_params(("arbitrary",)),
    )(dqz, dkv, h1, dh2, oa, wbin_g, w_kv, g_kv, g_pre, g_apost)


def _to_owner_core(pieces, r, send, recv, core, action):
    x, y, c = lax.axis_index("x"), lax.axis_index("y"), lax.axis_index("c")
    for kp in range(N_CHIPS):
        px, py = kp >> 1, kp & 1
        rel = 4 * (x + px - 2 * x * px) + 2 * (y + py - 2 * y * py) + (c + core - 2 * c * core)

        @pl.when(rel != 0)
        def _():
            cp = pltpu.make_async_remote_copy(src_ref=pieces.at[kp], dst_ref=r.at[rel - 1], send_sem=send.at[kp],
                                              recv_sem=recv.at[rel - 1], device_id=(px, py, core), device_id_type=MESH)
            if action == "start":
                cp.start()
            else:
                cp.wait_send()
    if action == "wait":
        @pl.when(c == core)
        def _():
            for rel in range(1, N_DEV):
                pltpu.make_async_remote_copy(src_ref=pieces.at[0], dst_ref=r.at[rel - 1], send_sem=send.at[0],
                                             recv_sem=recv.at[rel - 1], device_id=(x, y, c),
                                             device_id_type=MESH).wait_recv()


def _owner_core_sems():
    return [pltpu.SemaphoreType.DMA((N_CHIPS,)), pltpu.SemaphoreType.DMA((N_DEV - 1,))]


def _device_exchange(grads, recvs, send, recv):
    x, y, c = lax.axis_index("x"), lax.axis_index("y"), lax.axis_index("c")
    copies = []
    for a, (g, r) in enumerate(zip(grads, recvs)):
        h = g.shape[1] // 2
        for rel in range(1, N_DEV):
            fx, fy, fc = rel >> 2, (rel >> 1) & 1, rel & 1
            px, py, pc = x + fx - 2 * x * fx, y + fy - 2 * y * fy, c + fc - 2 * c * fc
            sem = (N_DEV - 1) * a + rel - 1
            copies.append(pltpu.make_async_remote_copy(
                src_ref=g.at[2 * px + py, pl.ds(pl.multiple_of(pc * h, 16), h)], dst_ref=r.at[rel - 1],
                send_sem=send.at[sem], recv_sem=recv.at[sem], device_id=(px, py, pc), device_id_type=MESH))
    return copies


def _device_exchange_specs(grads):
    anyspace = pl.BlockSpec(memory_space=pl.ANY)
    n = len(grads)
    count = (N_DEV - 1) * n
    return ([anyspace] * n, [anyspace] * n,
            [SDS((N_DEV - 1, g.shape[1] // 2, g.shape[2]), g.dtype) for g in grads],
            [pltpu.SemaphoreType.DMA((count,)), pltpu.SemaphoreType.DMA((count,))])


def _a_bwd(doa, ya, proj, conv_w, w_out, tm, parts):
    s = doa.shape[0]
    nt = s // tm
    n = len(parts)
    ex_in, ex_out, ex_shape, ex_sems = _device_exchange_specs(parts)

    def body(*refs):
        doa_ref, ya_ref, proj_ref, halo_ref, cw_ref, w_ref = refs[:6]
        part_refs = refs[6:6 + n]
        dproj_ref, dcw_ref, dw_ref, dw16_ref = refs[6 + n:10 + n]
        recv_refs = refs[10 + n:10 + 2 * n]
        carry, dw_acc, stage, send, recv = refs[10 + 2 * n:]
        i = pl.program_id(0)
        r = nt - 1 - i

        @pl.when(i == 0)
        def _():
            dcw_ref[...] = jnp.zeros_like(dcw_ref)
            carry[...] = jnp.zeros_like(carry)
            dw_acc[...] = jnp.zeros_like(dw_acc)
            for cp in _device_exchange(part_refs, recv_refs, send, recv):
                cp.start()
        dya = _nt(doa_ref[...], w_ref[...])
        dw_acc[...] += _tn(ya_ref[...], doa_ref[...])
        bg = proj_ref[:, 0:D].astype(F32)
        cg = proj_ref[:, D:2 * D].astype(F32)
        u = proj_ref[:, 2 * D:3 * D].astype(F32)
        z = proj_ref[:, 3 * D:4 * D].astype(F32)
        v = cg * u
        before = jnp.where(r > 0, halo_ref[:, D:2 * D].astype(F32) * halo_ref[:, 2 * D:3 * D].astype(F32), 0.0)
        rows = lax.broadcasted_iota(jnp.int32, (tm, D), 0)
        v1, v2 = _shift_rows(v, before[HALO - 1:HALO, :], before[HALO - 2:HALO - 1, :], rows)
        conv = cw_ref[0:1, :] * v2 + cw_ref[1:2, :] * v1 + cw_ref[2:3, :] * v
        sg, sz = _silu_parts(z)
        dproj_ref[:, 0:D] = (dya * conv * sz).astype(BF16)
        dproj_ref[:, 3 * D:4 * D] = (dya * bg * conv * _dsilu(z, sg)).astype(BF16)
        dconv = dya * bg * sz
        _acc_row(dcw_ref, 0, jnp.sum(dconv * v2, axis=0, keepdims=True))
        _acc_row(dcw_ref, 1, jnp.sum(dconv * v1, axis=0, keepdims=True))
        _acc_row(dcw_ref, 2, jnp.sum(dconv * v, axis=0, keepdims=True))
        after = carry[...]
        up1 = jnp.where(rows < tm - 1, pltpu.roll(dconv, tm - 1, 0), after[0:1, :])
        up2 = jnp.where(rows < tm - 2, pltpu.roll(dconv, tm - 2, 0),
                        jnp.where(rows == tm - 2, after[0:1, :], after[1:2, :]))
        carry[...] = dconv[0:8, :]
        dv = cw_ref[2:3, :] * dconv + cw_ref[1:2, :] * up1 + cw_ref[0:1, :] * up2
        dproj_ref[:, D:2 * D] = (dv * u).astype(BF16)
        dproj_ref[:, 2 * D:3 * D] = (dv * cg).astype(BF16)

        @pl.when(i == nt - 1)
        def _():
            _write_gradient(dw_acc, dw_ref, dw16_ref, stage)
            for cp in _device_exchange(part_refs, recv_refs, send, recv):
                cp.wait()

    rev = lambda i: (nt - 1 - i, 0)
    fix = lambda i: (0, 0)
    halo = lambda i: (jnp.maximum((nt - 1 - i) * (tm // HALO) - 1, 0), 0)
    anyspace = pl.BlockSpec(memory_space=pl.ANY)
    dproj, dcw, dw, dw16, *got = pl.pallas_call(
        body, name="a_bwd", grid=(nt,),
        in_specs=[pl.BlockSpec((tm, D), rev), pl.BlockSpec((tm, D), rev), pl.BlockSpec((tm, 4 * D), rev),
                  pl.BlockSpec((HALO, 4 * D), halo), pl.BlockSpec((8, D), fix), pl.BlockSpec((D, D), fix)] + ex_in,
        out_specs=[pl.BlockSpec((tm, 4 * D), rev), pl.BlockSpec((8, D), fix), anyspace, anyspace] + ex_out,
        out_shape=[SDS((s, 4 * D), BF16), SDS((8, D), F32), SDS((D, D), F32), SDS((D, D), BF16)] + ex_shape,
        scratch_shapes=[pltpu.VMEM((8, D), F32), pltpu.VMEM((D, D), F32), pltpu.VMEM((D // 4, D), BF16)] + ex_sems,
        compiler_params=_params(("arbitrary",)),
    )(doa, ya, proj, proj, conv_w, w_out, *parts)
    return dproj, dcw, dw, dw16, got


def _dn1(dp_ref, w_ref):
    dn = _nt(dp_ref[:, 0:D], w_ref[0])
    for j in range(1, 4):
        dn = dn + _nt(dp_ref[:, D * j:D * (j + 1)], w_ref[j])
    return dn


def _a_in_bwd_matmul(dproj, win_g, tm, count, win_half, win_got):
    def body(dp_ref, w_ref, half_ref, got_in, dn_ref, got_ref, wcat, send, recv):
        del got_in

        @pl.when(pl.program_id(0) == 0)
        def _():
            _to_owner_core(half_ref, got_ref, send, recv, 1, "start")
            for j in range(N_CHIPS):
                pltpu.sync_copy(w_ref.at[j], wcat.at[:, pl.ds(D * j, D)])
        dn_ref[...] = _nt(dp_ref[...], wcat[...]).astype(BF16)

        @pl.when(pl.program_id(0) == count - 1)
        def _():
            _to_owner_core(half_ref, got_ref, send, recv, 1, "wait")

    row = lambda i: (i, 0)
    anyspace = pl.BlockSpec(memory_space=pl.ANY)
    return pl.pallas_call(
        body, name="a_in_bwd_matmul", grid=(count,),
        in_specs=[pl.BlockSpec((tm, 4 * D), row), anyspace, anyspace, anyspace],
        out_specs=[pl.BlockSpec((tm, D), row), anyspace],
        out_shape=[SDS((count * tm, D), BF16), SDS(win_got.shape, win_got.dtype)],
        scratch_shapes=[pltpu.VMEM((D, 4 * D), BF16)] + _owner_core_sems(),
        input_output_aliases={3: 1},
        compiler_params=_params(("arbitrary",)),
    )(dproj, win_g, win_half, win_got)


def _a_in_bwd(dn_first, dproj, x, dh1, win_g, g_pre, tm):
    s = x.shape[0]
    nt = s // tm
    count = dn_first.shape[0] // tm

    def body(dn_ref, dp_ref, x_ref, dh_ref, w_ref, g_ref, gx_ref, dg_ref, dn_s):
        i = pl.program_id(0)

        @pl.when(i == 0)
        def _():
            dg_ref[...] = jnp.zeros_like(dg_ref)

        @pl.when(i < count)
        def _():
            dn_s[...] = dn_ref[...].astype(F32)

        @pl.when(i >= count)
        def _():
            dn_s[...] = _dn1(dp_ref, w_ref)
        dn = dn_s[...]
        xv = x_ref[...]
        r = _rms_scale(xv)
        xh = xv * r
        _acc_row(dg_ref, 0, jnp.sum(dn * xh, axis=0, keepdims=True))
        dxh = dn * g_ref[...]
        gx_ref[...] = dh_ref[...] + r * (dxh - xh * jnp.mean(dxh * xh, axis=-1, keepdims=True))

    row = lambda i: (i, 0)
    fix = lambda i: (0, 0)
    return pl.pallas_call(
        body, name="a_in_bwd", grid=(nt,),
        in_specs=[pl.BlockSpec((tm, D), lambda i: (jnp.minimum(i, count - 1), 0)),
                  pl.BlockSpec((tm, 4 * D), lambda i: (jnp.maximum(i, count), 0)),
                  pl.BlockSpec((tm, D), row), pl.BlockSpec((tm, D), row),
                  pl.BlockSpec((4, D, D), lambda i: (0, 0, 0)), pl.BlockSpec((1, D), fix)],
        out_specs=[pl.BlockSpec((tm, D), row), pl.BlockSpec((8, D), fix)],
        out_shape=[SDS((s, D), F32), SDS((8, D), F32)],
        scratch_shapes=[pltpu.VMEM((tm, D), F32)],
        compiler_params=_params(("arbitrary",)),
    )(dn_first, dproj, x, dh1, win_g, g_pre)


def _dw_in_half(n1, dproj, core, tmw, name, to_owners=None, to_devices=None):
    s = n1.shape[0]
    h = D // 2
    nt = s // tmw
    sent_array = to_owners if to_owners is not None else to_devices
    rides = sent_array is not None
    if to_owners is not None:
        sems, got_shape = _owner_core_sems(), SDS((N_DEV - 1, h, D), BF16)
    elif to_devices is not None:
        _, _, (got_shape,), sems = _device_exchange_specs([to_devices])

    def body(*refs):
        a_ref, b_ref = refs[:2]
        o_ref, o16_ref = refs[2 + rides:4 + rides]
        acc, stage = refs[4 + 2 * rides:6 + 2 * rides]
        t = pl.program_id(0)

        def exchange(action):
            sent, got, send, recv = refs[2], refs[5], refs[8], refs[9]
            if to_owners is not None:
                _to_owner_core(sent, got, send, recv, 1 - core, action)
            else:
                for cp in _device_exchange([sent], [got], send, recv):
                    cp.start() if action == "start" else cp.wait()

        @pl.when(t == 0)
        def _():
            acc[...] = jnp.zeros_like(acc)
            if rides:
                exchange("start")
        acc[...] += _tn(a_ref[...], b_ref[...])

        @pl.when(t == nt - 1)
        def _():
            for j in range(N_CHIPS):
                pltpu.sync_copy(acc.at[:, pl.ds(D * j, D)], o_ref.at[j])
                stage[...] = acc[:, D * j:D * (j + 1)].astype(BF16)
                pltpu.sync_copy(stage, o16_ref.at[j])
            if rides:
                exchange("wait")

    anyspace = pl.BlockSpec(memory_space=pl.ANY)
    return pl.pallas_call(
        body, name=name, grid=(nt,),
        in_specs=[pl.BlockSpec((tmw, h), lambda t: (t, core)), pl.BlockSpec((tmw, 4 * D), lambda t: (t, 0))]
        + [anyspace] * rides,
        out_specs=[anyspace, anyspace] + [anyspace] * rides,
        out_shape=[SDS((N_CHIPS, h, D), F32), SDS((N_CHIPS, h, D), BF16)] + ([got_shape] if rides else []),
        scratch_shapes=[pltpu.VMEM((h, 4 * D), F32), pltpu.VMEM((h, D), BF16)] + (sems if rides else []),
        compiler_params=_params(("arbitrary",)),
    )(n1, dproj, *([sent_array] if rides else []))


def _sibling_exchange(name, to_sibling=(), shards=(), smalls=()):
    n_g, n_h, n_s = len(to_sibling), len(shards), len(smalls)

    def body(*refs):
        gs = refs[:n_g]
        pos = n_g + n_h
        small_ins = refs[pos:pos + n_s]
        pos += n_s
        rs, fs = refs[pos:pos + n_g], refs[pos + n_g:pos + n_g + n_h]
        pos += n_g + n_h
        small_alls = refs[pos:pos + n_s]
        pos += n_s
        dsend, drecv, ssend, srecv = refs[pos:]
        x, y, c = lax.axis_index("x"), lax.axis_index("y"), lax.axis_index("c")
        sibling = (x, y, 1 - c)
        sends, arrivals = [], []
        for a, (g, r) in enumerate(zip(gs, rs)):
            h = g.shape[1] // 2
            src = g.at[:, pl.ds(pl.multiple_of((1 - c) * h, 8), h), :]
            sends.append(pltpu.make_async_remote_copy(src_ref=src, dst_ref=r, send_sem=dsend.at[a], recv_sem=drecv.at[a],
                                                      device_id=sibling, device_id_type=MESH))
            arrivals.append(pltpu.make_async_remote_copy(src_ref=r, dst_ref=r, send_sem=dsend.at[a], recv_sem=drecv.at[a],
                                                         device_id=sibling, device_id_type=MESH))
        for b, full in enumerate(fs):
            h = full.shape[0] // 2
            mine = full.at[pl.ds(pl.multiple_of(c * h, 8), h)]
            theirs = full.at[pl.ds(pl.multiple_of((1 - c) * h, 8), h)]
            sends.append(pltpu.make_async_remote_copy(src_ref=mine, dst_ref=mine, send_sem=dsend.at[n_g + b],
                                                      recv_sem=drecv.at[n_g + b], device_id=sibling, device_id_type=MESH))
            arrivals.append(pltpu.make_async_remote_copy(src_ref=mine, dst_ref=theirs, send_sem=dsend.at[n_g + b],
                                                         recv_sem=drecv.at[n_g + b], device_id=sibling, device_id_type=MESH))
        me = 4 * x + 2 * y + c
        for k, (small_in, small_all) in enumerate(zip(small_ins, small_alls)):
            small_all[me] = small_in[...]
            for rel in range(1, N_DEV):
                fx, fy, fc = rel >> 2, (rel >> 1) & 1, rel & 1
                peer = (x + fx - 2 * x * fx, y + fy - 2 * y * fy, c + fc - 2 * c * fc)
                sender = 4 * peer[0] + 2 * peer[1] + peer[2]
                sem = (N_DEV - 1) * k + rel - 1
                sends.append(pltpu.make_async_remote_copy(
                    src_ref=small_in, dst_ref=small_all.at[me], send_sem=ssend.at[sem], recv_sem=srecv.at[sem],
                    device_id=peer, device_id_type=MESH))
                arrivals.append(pltpu.make_async_remote_copy(
                    src_ref=small_in, dst_ref=small_all.at[sender], send_sem=ssend.at[sem], recv_sem=srecv.at[sem],
                    device_id=peer, device_id_type=MESH))
        for cp in sends:
            cp.start()
        for cp in arrivals:
            cp.wait_recv()
        for cp in sends:
            cp.wait_send()

    anyspace = pl.BlockSpec(memory_space=pl.ANY)
    vm = pl.BlockSpec(memory_space=pltpu.VMEM)
    out_shape = [SDS((N_CHIPS, g.shape[1] // 2, g.shape[2]), F32) for g in to_sibling]
    out_shape += [SDS(full.shape, F32) for full in shards]
    out_shape += [SDS((N_DEV,) + sm.shape, F32) for sm in smalls]
    n_d2d = max(n_g + n_h, 1)
    n_all = (N_DEV - 1) * max(n_s, 1)
    outs = pl.pallas_call(
        body, name=name, out_shape=out_shape,
        in_specs=[anyspace] * (n_g + n_h) + [vm] * n_s, out_specs=[anyspace] * (n_g + n_h) + [vm] * n_s,
        scratch_shapes=[pltpu.SemaphoreType.DMA((n_d2d,)), pltpu.SemaphoreType.DMA((n_d2d,)),
                        pltpu.SemaphoreType.DMA((n_all,)), pltpu.SemaphoreType.DMA((n_all,))],
        input_output_aliases={n_g + b: n_g + b for b in range(n_h)},
    )(*to_sibling, *shards, *smalls)
    return outs[:n_g], outs[n_g:n_g + n_h], outs[n_g + n_h:]


def _add_win(where, lo, hi, r, name):
    _, h, cols = lo.shape
    tr = min(h, 256)
    nh = h // tr

    def body(where_ref, lo_ref, hi_ref, r_ref, o_ref):
        acc = jnp.where(where_ref[0] == 0, lo_ref[0], hi_ref[0])
        for k in range(N_DEV - 1):
            acc = acc + r_ref[k].astype(F32)
        o_ref[...] = acc

    own = pl.BlockSpec((1, tr, cols), lambda i, w: (w[1], i, 0))
    return pl.pallas_call(
        body, name=name,
        grid_spec=pltpu.PrefetchScalarGridSpec(
            num_scalar_prefetch=1, grid=(nh,),
            in_specs=[own, own, pl.BlockSpec((N_DEV - 1, tr, cols), lambda i, w: (0, i, 0))],
            out_specs=pl.BlockSpec((tr, cols), lambda i, w: (w[0] * nh + i, 0))),
        out_shape=SDS((2 * h, cols), F32),
        compiler_params=_params(("parallel",)),
    )(where, lo, hi, r)


def _add_devices(where, g, r, name):
    _, rows, cols = g.shape
    h = rows // 2
    tr = min(h, 256)
    nh = h // tr

    def body(where_ref, g_ref, r_ref, o_ref):
        del where_ref
        acc = g_ref[0]
        for k in range(N_DEV - 1):
            acc = acc + r_ref[k].astype(F32)
        o_ref[...] = acc

    return pl.pallas_call(
        body, name=name,
        grid_spec=pltpu.PrefetchScalarGridSpec(
            num_scalar_prefetch=1, grid=(nh,),
            in_specs=[pl.BlockSpec((1, tr, cols), lambda i, w: (w[1], w[0] * nh + i, 0)),
                      pl.BlockSpec((N_DEV - 1, tr, cols), lambda i, w: (0, i, 0))],
            out_specs=pl.BlockSpec((tr, cols), lambda i, w: (w[0] * nh + i, 0))),
        out_shape=SDS((rows, cols), F32),
        compiler_params=_params(("parallel",)),
    )(where, g, r)


def _sum_smalls(gathered):
    n = len(gathered)

    def body(*refs):
        for all_ref, o_ref in zip(refs[:n], refs[n:]):
            acc = all_ref[0]
            for dev in range(1, N_DEV):
                acc = acc + all_ref[dev]
            o_ref[...] = acc

    vm = pl.BlockSpec(memory_space=pltpu.VMEM)
    return pl.pallas_call(
        body, name="sum_smalls", out_shape=[SDS(a.shape[1:], F32) for a in gathered],
        in_specs=[vm] * n, out_specs=[vm] * n,
    )(*gathered)


def _adam_step(g, w, m, v):
    nm = ADAM_B1 * m + (1.0 - ADAM_B1) * g
    nv = ADAM_B2 * v + (1.0 - ADAM_B2) * (g * g)
    m_hat = nm / (1.0 - ADAM_B1 ** ADAM_STEP)
    v_hat = nv / (1.0 - ADAM_B2 ** ADAM_STEP)
    return -ADAM_LR * (m_hat / (jnp.sqrt(v_hat) + ADAM_EPS) + ADAM_WD * w), nm, nv


def _adamw(g, w, m, v, name):
    rows, cols = g.shape
    tr = min(rows, 256)

    def body(g_ref, w_ref, m_ref, v_ref, d_ref, nm_ref, nv_ref):
        d_ref[...], nm_ref[...], nv_ref[...] = _adam_step(g_ref[...], w_ref[...], m_ref[...], v_ref[...])

    spec = pl.BlockSpec((tr, cols), lambda i: (i, 0))
    return pl.pallas_call(
        body, name=name, grid=(rows // tr,), in_specs=[spec] * 4, out_specs=[spec] * 3,
        out_shape=[SDS(g.shape, F32)] * 3, compiler_params=_params(("parallel",)),
    )(g, w, m, v)


def _small_update(chip, tot, tot_rel, wmv):
    names = list(SMALL_PLACES)
    n = len(names)

    def body(chip_ref, tot_ref, quarter_ref, rel_ref, *refs):
        del chip_ref
        ins, outs = refs[:3 * n], refs[3 * n:]
        for i, nm in enumerate(names):
            source, row, (rows, cols) = SMALL_PLACES[nm]
            g = {"rows": tot_ref, "quarter": quarter_ref, "rel": rel_ref}[source][row:row + rows, 0:cols]
            outs[4 * i][...] = g
            outs[4 * i + 1][...], outs[4 * i + 2][...], outs[4 * i + 3][...] = _adam_step(
                g, ins[3 * i][...], ins[3 * i + 1][...], ins[3 * i + 2][...])

    whole = lambda shape: pl.BlockSpec(shape, lambda i, c: (0,) * len(shape))
    shapes = [SMALL_PLACES[nm][2] for nm in names]
    outs = pl.pallas_call(
        body, name="small_update",
        grid_spec=pltpu.PrefetchScalarGridSpec(
            num_scalar_prefetch=1, grid=(1,),
            in_specs=[whole(tot.shape), pl.BlockSpec((tot.shape[0], D // 4), lambda i, c: (0, c[0])),
                      whole(tot_rel.shape)] + [whole(shp) for shp in shapes for _ in range(3)],
            out_specs=[whole(shp) for shp in shapes for _ in range(4)]),
        out_shape=[SDS(shp, F32) for shp in shapes for _ in range(4)],
    )(chip, tot, tot, tot_rel, *[a for nm in names for a in wmv[nm]])
    return {nm: tuple(outs[4 * i:4 * i + 4]) for i, nm in enumerate(names)}


def _pad_rows(a, rows):
    return jnp.concatenate([a, jnp.zeros((rows - a.shape[0], a.shape[1]), a.dtype)], axis=0)


def _pad_cols(a, cols):
    return jnp.concatenate([a, jnp.zeros((a.shape[0], cols - a.shape[1]), a.dtype)], axis=1)


def kernel(x, a_pre_norm, a_w_in, a_conv_w, a_w_out, a_post_norm, kv_norm, w_kv, rel_bias, b_pre_norm, b_w_in, b_sinks, b_w_out, b_post_norm, loss_target, m_a_pre_norm, m_a_w_in, m_a_conv_w, m_a_w_out, m_a_post_norm, m_kv_norm, m_w_kv, m_rel_bias, m_b_pre_norm, m_b_w_in, m_b_sinks, m_b_w_out, m_b_post_norm, v_a_pre_norm, v_a_w_in, v_a_conv_w, v_a_w_out, v_a_post_norm, v_kv_norm, v_w_kv, v_rel_bias, v_b_pre_norm, v_b_w_in, v_b_sinks, v_b_w_out, v_b_post_norm):
    seq = x.shape[1]
    xs = x.reshape(seq, D)
    tgt = loss_target.reshape(seq, D)
    chip = 2 * lax.axis_index("x") + lax.axis_index("y")
    core = lax.axis_index("c")
    tm = _tile(seq, 512)
    tmw = _tile(seq, 1024)

    shards = [a_w_in[0], a_w_out[0], w_kv, b_w_in[0], b_w_out[0]]
    small_w = _pad_rows(jnp.concatenate([a_pre_norm, a_conv_w[0], a_post_norm], axis=0), 8)
    *own_only, small_g = _prepare_weights(shards, small_w)
    where = jnp.stack([core, chip]).astype(jnp.int32)
    small_full = small_g.transpose(1, 0, 2).reshape(8, D)
    g_apre, conv_w, g_apost = small_full[0:1], _pad_rows(small_full[1:4], 8), small_full[4:5]
    g_kv = kv_norm.reshape(1, D)

    proj, n1, (win_g, wouta_g, wkv_g, wbin_g, woutb_g) = _a_in(where[1:2], xs, g_apre, own_only, tmw)
    wouta = wouta_g.reshape(D, D)
    wkv = wkv_g.reshape(D, 2 * KV_W)
    woutb = woutb_g.reshape(D, D)
    ya, oa, h1 = _a_mix(proj, xs, conv_w, wouta, g_apost, tm)
    kv, q, zb = _b_in(h1, g_kv, b_pre_norm, wkv, wbin_g, tmw)
    tab = _bias_table(rel_bias, b_sinks.reshape(N_HEADS))
    att, stats = _attn_fwd(q, kv, tab)
    dh2, dqz, datt, loss_acc, dg_bpost, dw_outb, dw_outb16 = _mid(att, zb, h1, tgt, woutb, b_post_norm, tm)

    dqz, dkv, dtab = _attn_bwd(q, kv, datt, stats, tab, dqz)
    dh1, doa, dg_b, dw_bin, dw_kv, dw_bin16, dw_kv16 = _b_bwd(dqz, dkv, h1, dh2, oa, wbin_g, wkv, g_kv, b_pre_norm,
                                                              g_apost, tm)
    by_chip = lambda a, cols: a.reshape(N_CHIPS, D // 4, cols)
    grads1 = [by_chip(dw_kv, 2 * KV_W), dw_bin, by_chip(dw_outb, D)]
    sent1 = [by_chip(dw_kv16, 2 * KV_W), dw_bin16, by_chip(dw_outb16, D)]
    names1 = ["w_kv", "b_w_in", "b_w_out"]
    dproj, dconv_w, dw_outa, dw_outa16, from_devices1 = _a_bwd(doa, ya, proj, conv_w, wouta, tm, sent1)
    shards1 = [_add_devices(where, g, r, "add_devices_" + nm) for g, r, nm in zip(grads1, from_devices1, names1)]
    win_lo, win_lo16, outa_got = _dw_in_half(n1, dproj, 0, tmw, "dw_a_in_lo", to_devices=by_chip(dw_outa16, D))
    win_hi, win_hi16, win_got = _dw_in_half(n1, dproj, 1, tmw, "dw_a_in_hi", to_owners=win_lo16)
    nt = seq // tmw
    dn_first, win_got = _a_in_bwd_matmul(dproj, win_g, tmw, max(nt - max(nt // 4, 1), 1), win_hi16, win_got)
    grad_x, dg_apre = _a_in_bwd(dn_first, dproj, xs, dh1, win_g, g_apre, tm)
    shards2 = [_add_win(where, win_lo, win_hi, win_got, "add_devices_a_w_in"),
               _add_devices(where, by_chip(dw_outa, D), outa_got, "add_devices_a_w_out")]
    drel, dsink = _bias_fold(dtab)

    smalls = jnp.concatenate([
        dg_apre[0:1], dg_b[2:3], dg_b[0:1], dg_b[1:2], dg_bpost[0:1], _pad_cols(dsink[0:1], D),
        _pad_cols(loss_acc[0:1], D), jnp.zeros((1, D), F32), dconv_w], axis=0)
    assert smalls.shape == (SMALL_ROWS, D)
    _, (g_wkv, g_wbin, g_woutb, g_win, g_wouta), gathered = _sibling_exchange(
        "share_last", shards=shards1 + shards2, smalls=(smalls, drel))
    tot, tot_rel = _sum_smalls(gathered)

    big = {}
    for nm, g, w, m, v in [("a_w_in", g_win, a_w_in, m_a_w_in, v_a_w_in), ("a_w_out", g_wouta, a_w_out, m_a_w_out, v_a_w_out),
                           ("w_kv", g_wkv, w_kv, m_w_kv, v_w_kv), ("b_w_in", g_wbin, b_w_in, m_b_w_in, v_b_w_in),
                           ("b_w_out", g_woutb, b_w_out, m_b_w_out, v_b_w_out)]:
        shp = w.shape
        two = (shp[-2], shp[-1])
        d, nm_, nv_ = _adamw(g, w.reshape(two), m.reshape(two), v.reshape(two), "adamw_" + nm)
        big[nm] = (g.reshape(shp), d.reshape(shp), nm_.reshape(shp), nv_.reshape(shp))

    given = {"a_pre_norm": (a_pre_norm, m_a_pre_norm, v_a_pre_norm), "a_conv_w": (a_conv_w, m_a_conv_w, v_a_conv_w),
             "a_post_norm": (a_post_norm, m_a_post_norm, v_a_post_norm), "kv_norm": (kv_norm, m_kv_norm, v_kv_norm),
             "rel_bias": (rel_bias, m_rel_bias, v_rel_bias), "b_pre_norm": (b_pre_norm, m_b_pre_norm, v_b_pre_norm),
             "b_sinks": (b_sinks, m_b_sinks, v_b_sinks), "b_post_norm": (b_post_norm, m_b_post_norm, v_b_post_norm)}
    small = _small_update(where[1:2], tot, tot_rel, {nm: tuple(a.reshape(SMALL_PLACES[nm][2]) for a in wmv)
                                            for nm, wmv in given.items()})
    order = ["a_pre_norm", "a_w_in", "a_conv_w", "a_w_out", "a_post_norm", "kv_norm", "w_kv", "rel_bias",
             "b_pre_norm", "b_w_in", "b_sinks", "b_w_out", "b_post_norm"]
    outs = []
    for which in range(4):
        for nm in order:
            outs.append(big[nm][which] if nm in big else small[nm][which].reshape(given[nm][0].shape))
    loss = 0.5 * tot[LOSS_ROW, 0]
    return (loss, grad_x.reshape(x.shape), *outs)
```

```python
import math

import jax
import jax.numpy as jnp
from jax import lax
from jax.experimental import pallas as pl
from jax.experimental.pallas import tpu as pltpu

F32 = jnp.float32
BF16 = jnp.bfloat16
MESH = pl.DeviceIdType.MESH
SDS = jax.ShapeDtypeStruct

D = 1024
HEAD_DIM = 64
N_HEADS = 16
N_KV = 2
GROUP = 8
KV_W = 128
BLK = 128
N_BUCKETS = 32
MAX_EXACT = 16
MAX_DISTANCE = 128
EPS = 1e-6
NEG_INF = -1e30
Q_SCALE = HEAD_DIM ** -0.5

ADAM_LR = 0.001
ADAM_B1 = 0.9
ADAM_B2 = 0.999
ADAM_EPS = 1e-08
ADAM_WD = 0.01
ADAM_STEP = 10

N_CHIPS = 4
N_DEV = 8
BIN_COLS = 2 * D // N_CHIPS
VMEM_LIMIT = 56 * 1024 * 1024
SMALL_ROWS = 16
LOSS_ROW = 6
SMALL_PLACES = {
    "a_pre_norm": ("quarter", 0, (1, D // 4)), "a_conv_w": ("quarter", 8, (3, D // 4)),
    "a_post_norm": ("quarter", 1, (1, D // 4)), "kv_norm": ("rows", 2, (1, D)),
    "rel_bias": ("rel", 0, (N_BUCKETS, N_HEADS)), "b_pre_norm": ("rows", 3, (1, D)),
    "b_sinks": ("rows", 5, (1, N_HEADS)), "b_post_norm": ("rows", 4, (1, D)),
}
HALO = 16


def _bucket_thresholds():
    def bucket(d):
        big = MAX_EXACT + int(math.log(d / MAX_EXACT) / math.log(MAX_DISTANCE / MAX_EXACT)
                              * (N_BUCKETS - MAX_EXACT))
        return d if d < MAX_EXACT else min(big, N_BUCKETS - 1)
    out = []
    for b in range(MAX_EXACT + 1, N_BUCKETS):
        out.append(min(d for d in range(MAX_EXACT, MAX_DISTANCE) if bucket(d) >= b))
    return tuple(out)


BUCKET_THRESHOLDS = _bucket_thresholds()


def _params(semantics=None, vmem=VMEM_LIMIT):
    return pltpu.CompilerParams(dimension_semantics=semantics, vmem_limit_bytes=vmem)


def _tile(n, pref):
    return pref if n >= 2 * pref else max(n // 2, 8)


def _rms_scale(v):
    return lax.rsqrt(jnp.mean(v * v, axis=-1, keepdims=True) + EPS)


def _nt(a, b):
    return lax.dot_general(a, b, (((1,), (1,)), ((), ())), preferred_element_type=F32)


def _tn(a, b):
    return lax.dot_general(a, b, (((0,), (0,)), ((), ())), preferred_element_type=F32)


def _nn(a, b):
    return jnp.dot(a, b, preferred_element_type=F32)


def _silu_parts(z):
    sg = jax.nn.sigmoid(z)
    return sg, z * sg


def _dsilu(z, sg):
    return sg * (1.0 + z * (1.0 - sg))


def _write_gradient(acc, out32, out16, stage):
    pltpu.sync_copy(acc, out32)
    rows = stage.shape[0]
    for k in range(acc.shape[0] // rows):
        stage[...] = acc[rows * k:rows * (k + 1), :].astype(BF16)
        pltpu.sync_copy(stage, out16.at[pl.ds(rows * k, rows)])


def _acc_row(ref, row, val):
    ref[row:row + 1, :] += val


def _gather_copies(outs, splits, ici_send, ici_recv, d2d_send, d2d_recv):
    x, y, c = lax.axis_index("x"), lax.axis_index("y"), lax.axis_index("c")
    k = 2 * x + y
    sibling = (x, y, 1 - c)

    def part(o_ref, chip, core, split):
        if not split:
            return o_ref.at[chip]
        h = o_ref.shape[1] // 2
        return o_ref.at[chip, pl.ds(pl.multiple_of(core * h, 16), h)]

    def remote(ref, a, j, sems, to):
        return pltpu.make_async_remote_copy(src_ref=ref, dst_ref=ref, send_sem=sems[0].at[3 * a + j],
                                            recv_sem=sems[1].at[3 * a + j], device_id=to, device_id_type=MESH)

    copies = []
    for a, (o_ref, split) in enumerate(zip(outs, splits)):
        for j, (px, py) in enumerate([(x, 1 - y), (1 - x, y), (1 - x, 1 - y)]):
            kj = 2 * px + py
            ici, d2d = (ici_send, ici_recv), (d2d_send, d2d_recv)
            copies.append((remote(part(o_ref, k, c, split), a, j, ici, (px, py, c)),
                           remote(part(o_ref, kj, c, split), a, j, ici, (px, py, c)),
                           remote(part(o_ref, kj, c, split), a, j, d2d, sibling) if split else None,
                           remote(part(o_ref, kj, 1 - c, split), a, j, d2d, sibling) if split else None))
    return copies


def _gather_sems(n):
    return [pltpu.SemaphoreType.DMA((3 * n,)) for _ in range(4)]


def _prepare_weights(shards, small):
    n = len(shards)

    def body(*refs):
        ins, small_in = refs[:n], refs[n]
        outs, small_out = refs[n + 1:2 * n + 1], refs[2 * n + 1]
        stages, put_sem = refs[2 * n + 2:3 * n + 2], refs[3 * n + 2]
        sems = refs[3 * n + 3:]
        k = 2 * lax.axis_index("x") + lax.axis_index("y")
        puts = []
        for a, (i_ref, stage, o_ref) in enumerate(zip(ins, stages, outs)):
            stage[...] = i_ref[...].astype(BF16)
            puts.append(pltpu.make_async_copy(stage, o_ref.at[k], put_sem.at[a]))
            puts[-1].start()
        small_out[k] = small_in[...]
        copies = _gather_copies([small_out], [False], *sems)
        for send, _, _, _ in copies:
            send.start()
        for _, arrival, _, _ in copies:
            arrival.wait_recv()
        for send, _, _, _ in copies:
            send.wait_send()
        for put in puts:
            put.wait()

    vm = pl.BlockSpec(memory_space=pltpu.VMEM)
    anyspace = pl.BlockSpec(memory_space=pl.ANY)
    out_shape = [SDS((N_CHIPS,) + s.shape, BF16) for s in shards] + [SDS((N_CHIPS,) + small.shape, F32)]
    return pl.pallas_call(
        body, name="prepare_weights", out_shape=out_shape,
        in_specs=[vm] * (n + 1), out_specs=[anyspace] * n + [vm],
        scratch_shapes=[pltpu.VMEM(s.shape, BF16) for s in shards] + [pltpu.SemaphoreType.DMA((n,))] + _gather_sems(1),
        compiler_params=pltpu.CompilerParams(vmem_limit_bytes=VMEM_LIMIT),
    )(*shards, small)


def _a_in(chip, x, g_pre, weights, tm):
    s = x.shape[0]
    nt = s // tm
    n = len(weights)

    def body(chip_ref, x_ref, g_ref, *refs):
        proj_ref, n1_ref = refs[n:n + 2]
        gathered = refs[n + 2:2 * n + 2]
        wbuf, n1_all, fetch_sem = refs[2 * n + 2:2 * n + 5]
        sems = refs[2 * n + 5:]
        jj, i = pl.program_id(0), pl.program_id(1)
        copies = _gather_copies(gathered, [True] * n, *sems)

        def fetch(rel):
            slot = jnp.bitwise_xor(chip_ref[0], rel)
            return pltpu.make_async_copy(gathered[0].at[slot], wbuf.at[rel % 2], fetch_sem.at[rel % 2])

        @pl.when((jj == 0) & (i == 0))
        def _():
            fetch(0).start()
            copies[0][0].start()
            copies[1][0].start()
            fetch(0).wait()

        for rel in (1, 2, 3):
            @pl.when((jj == rel) & (i == 0))
            def _():
                fetch(rel).wait()

        @pl.when(jj == 0)
        def _():
            xv = x_ref[...]
            n1 = (xv * _rms_scale(xv) * g_ref[...]).astype(BF16)
            n1_ref[...] = n1
            n1_all[i] = n1
        proj_ref[...] = _nn(n1_all[i], wbuf[jj % 2]).astype(BF16)

        for rel in (1, 2, 3):
            @pl.when((jj == rel - 1) & (i == max(nt - 2, nt // 2)))
            def _():
                _, arrival, forward, forwarded = copies[rel - 1]
                arrival.wait_recv()
                forward.start()
                forwarded.wait_recv()
                fetch(rel).start()
                if rel == 1:
                    for send, _, _, _ in copies[2:]:
                        send.start()

        @pl.when((jj == 3) & (i == max(nt - 2, 0)))
        def _():
            for _, arrival, forward, _ in copies[3:]:
                arrival.wait_recv()
                forward.start()

        @pl.when((jj == 3) & (i == nt - 1))
        def _():
            for _, _, _, forwarded in copies[3:]:
                forwarded.wait_recv()
            for send, _, forward, _ in copies:
                forward.wait_send()
                send.wait_send()

    anyspace = pl.BlockSpec(memory_space=pl.ANY)
    proj, n1, *gathered = pl.pallas_call(
        body, name="a_in",
        grid_spec=pltpu.PrefetchScalarGridSpec(
            num_scalar_prefetch=1, grid=(4, nt),
            in_specs=[pl.BlockSpec((tm, D), lambda jj, i, c: (jnp.where(jj == 0, i, nt - 1), 0)),
                      pl.BlockSpec((1, D), lambda jj, i, c: (0, 0))] + [anyspace] * n,
            out_specs=[pl.BlockSpec((tm, D), lambda jj, i, c: (i, jnp.bitwise_xor(c[0], jj))),
                       pl.BlockSpec((tm, D), lambda jj, i, c: (jnp.where(jj == 0, i, nt - 1), 0))] + [anyspace] * n,
            scratch_shapes=[pltpu.VMEM((2, D, D), BF16), pltpu.VMEM((nt, tm, D), BF16),
                            pltpu.SemaphoreType.DMA((2,))] + _gather_sems(n)),
        out_shape=[SDS((s, 4 * D), BF16), SDS((s, D), BF16)] + [SDS(w.shape, w.dtype) for w in weights],
        input_output_aliases={3 + a: 2 + a for a in range(n)},
        compiler_params=_params(("arbitrary", "arbitrary")),
    )(chip, x, g_pre, *weights)
    return proj, n1, gathered


def _shift_rows(v, last, second_last, rows):
    v1 = jnp.where(rows >= 1, pltpu.roll(v, 1, 0), last)
    v2 = jnp.where(rows >= 2, pltpu.roll(v, 2, 0), jnp.where(rows == 1, last, second_last))
    return v1, v2


def _a_mix(proj, x, conv_w, w_out, g_post, tm):
    s = x.shape[0]

    def body(proj_ref, x_ref, cw_ref, w_ref, g_ref, ya_ref, oa_ref, h1_ref, carry):
        @pl.when(pl.program_id(0) == 0)
        def _():
            carry[...] = jnp.zeros_like(carry)
        v = proj_ref[:, D:2 * D].astype(F32) * proj_ref[:, 2 * D:3 * D].astype(F32)
        rows = lax.broadcasted_iota(jnp.int32, (tm, D), 0)
        before = carry[...]
        v1, v2 = _shift_rows(v, before[7:8, :], before[6:7, :], rows)
        carry[...] = v[tm - 8:tm, :]
        conv = cw_ref[0:1, :] * v2 + cw_ref[1:2, :] * v1 + cw_ref[2:3, :] * v
        _, sz = _silu_parts(proj_ref[:, 3 * D:4 * D].astype(F32))
        ya = (proj_ref[:, 0:D].astype(F32) * conv * sz).astype(BF16)
        ya_ref[...] = ya
        oa = _nn(ya, w_ref[...])
        oa_ref[...] = oa.astype(BF16)
        h1_ref[...] = x_ref[...] + oa * _rms_scale(oa) * g_ref[...]

    row = lambda i: (i, 0)
    fix = lambda i: (0, 0)
    return pl.pallas_call(
        body, name="a_mix", grid=(s // tm,),
        in_specs=[pl.BlockSpec((tm, 4 * D), row), pl.BlockSpec((tm, D), row), pl.BlockSpec((8, D), fix),
                  pl.BlockSpec((D, D), fix), pl.BlockSpec((1, D), fix)],
        out_specs=[pl.BlockSpec((tm, D), row)] * 3,
        out_shape=[SDS((s, D), BF16), SDS((s, D), BF16), SDS((s, D), F32)],
        scratch_shapes=[pltpu.VMEM((8, D), F32)],
        compiler_params=_params(("arbitrary",)),
    )(proj, x, conv_w, w_out, g_post)


def _b_in(h1, g_kv, g_pre, w_kv, wbin_g, tm):
    s = h1.shape[0]

    def body(h_ref, gk_ref, gb_ref, wkv_ref, wb_ref, kv_ref, q_ref, z_ref):
        h = h_ref[...]
        hh = h * _rms_scale(h)
        nk = (hh * gk_ref[...]).astype(BF16)
        nb = (hh * gb_ref[...]).astype(BF16)
        kv_ref[...] = _nn(nk, wkv_ref[...]).astype(BF16)
        for j in range(2):
            q_ref[:, BIN_COLS * j:BIN_COLS * (j + 1)] = (_nn(nb, wb_ref[j]) * Q_SCALE).astype(BF16)
            z_ref[:, BIN_COLS * j:BIN_COLS * (j + 1)] = _nn(nb, wb_ref[2 + j]).astype(BF16)

    row = lambda i: (i, 0)
    fix = lambda i: (0, 0)
    return pl.pallas_call(
        body, name="b_in", grid=(s // tm,),
        in_specs=[pl.BlockSpec((tm, D), row), pl.BlockSpec((1, D), fix), pl.BlockSpec((1, D), fix),
                  pl.BlockSpec((D, 2 * KV_W), fix), pl.BlockSpec((N_CHIPS, D, BIN_COLS), lambda i: (0, 0, 0))],
        out_specs=[pl.BlockSpec((tm, 2 * KV_W), row), pl.BlockSpec((tm, D), row), pl.BlockSpec((tm, D), row)],
        out_shape=[SDS((s, 2 * KV_W), BF16), SDS((s, D), BF16), SDS((s, D), BF16)],
        compiler_params=_params(("parallel",)),
    )(h1, g_kv, g_pre, w_kv, wbin_g)


def _band_buckets():
    q = lax.broadcasted_iota(jnp.int32, (BLK, 2 * BLK), 0)
    k = lax.broadcasted_iota(jnp.int32, (BLK, 2 * BLK), 1)
    dist = q + BLK - k
    bucket = jnp.where(dist < MAX_EXACT, dist, MAX_EXACT)
    for t in BUCKET_THRESHOLDS:
        bucket = bucket + jnp.where(dist >= t, 1, 0)
    in_window = (dist >= 0) & (dist < BLK)
    return jnp.where(in_window, bucket, -1)


def _head_place(h):
    kh, j, e = h // GROUP, (h % GROUP) // 2, h % 2
    return kh, slice(BLK * j, BLK * (j + 1)), slice(2 * BLK * e, 2 * BLK * (e + 1))


def _bias_table(rel_bias, sinks):
    def body(rb_ref, sink_ref, tab_ref):
        bucket = _band_buckets()
        col = lax.broadcasted_iota(jnp.int32, (BLK, 2 * BLK), 1)
        for h in range(N_HEADS):
            acc = jnp.where(bucket < 0, NEG_INF, 0.0).astype(F32)
            for b in range(N_BUCKETS):
                acc = jnp.where(bucket == b, rb_ref[b, h], acc)
            acc = jnp.where(col == 0, sink_ref[h], acc)
            kh, rows, cols = _head_place(h)
            tab_ref[1, kh, rows, cols] = acc
            tab_ref[0, kh, rows, cols] = jnp.where((col > 0) & (col < BLK), NEG_INF, acc)

    return pl.pallas_call(
        body, name="bias_table", out_shape=SDS((2, N_KV, 4 * BLK, 4 * BLK), F32),
        in_specs=[pl.BlockSpec(memory_space=pltpu.SMEM), pl.BlockSpec(memory_space=pltpu.SMEM)],
        out_specs=pl.BlockSpec(memory_space=pltpu.VMEM),
    )(rel_bias, sinks)


def _bias_fold(dtab):
    def body(dtab_ref, out_ref, dsink_ref):
        bucket = _band_buckets()
        row = lax.broadcasted_iota(jnp.int32, (N_BUCKETS, 128), 0)
        lane = lax.broadcasted_iota(jnp.int32, (N_BUCKETS, 128), 1)
        row8 = lax.broadcasted_iota(jnp.int32, (8, 128), 0)
        lane8 = lax.broadcasted_iota(jnp.int32, (8, 128), 1)
        acc = jnp.zeros((N_BUCKETS, 128), F32)
        dsink = jnp.zeros((8, 128), F32)
        for h in range(N_HEADS):
            kh, rows, cols = _head_place(h)
            dt = dtab_ref[kh, rows, cols]
            for b in range(N_BUCKETS):
                val = jnp.sum(jnp.where(bucket == b, dt, 0.0))
                acc = acc + jnp.where((row == b) & (lane == h), val, 0.0)
            dsink = dsink + jnp.where((row8 == 0) & (lane8 == h), jnp.sum(dt[:, 0:1]), 0.0)
        out_ref[...] = acc
        dsink_ref[...] = dsink

    vm = pl.BlockSpec(memory_space=pltpu.VMEM)
    return pl.pallas_call(
        body, name="bias_fold", out_shape=[SDS((N_BUCKETS, 128), F32), SDS((8, 128), F32)],
        in_specs=[vm], out_specs=[vm, vm],
    )(dtab)


def _pair_operands(prev, cur):
    t = jnp.concatenate([prev, cur], axis=0).astype(F32)
    t = jnp.where(lax.broadcasted_iota(jnp.int32, t.shape, 0) == 0, 0.0, t)
    tr = pltpu.roll(t, HEAD_DIM, 1)
    lo = lax.broadcasted_iota(jnp.int32, t.shape, 1) < HEAD_DIM
    zero = jnp.zeros_like(t)
    head0 = jnp.concatenate([jnp.where(lo, t, zero), jnp.where(lo, zero, tr)], axis=0).astype(BF16)
    head1 = jnp.concatenate([jnp.where(lo, tr, zero), jnp.where(lo, zero, t)], axis=0).astype(BF16)
    return head0, head1


def _pair_fold(d0, d1):
    lo = lax.broadcasted_iota(jnp.int32, (2 * BLK, KV_W), 1) < HEAD_DIM
    zero = jnp.zeros((2 * BLK, KV_W), F32)
    g0 = jnp.where(lo, d0[0:256], zero) + pltpu.roll(jnp.where(lo, zero, d0[256:512]), HEAD_DIM, 1)
    g1 = pltpu.roll(jnp.where(lo, d1[0:256], zero), HEAD_DIM, 1) + jnp.where(lo, zero, d1[256:512])
    return jnp.where(lax.broadcasted_iota(jnp.int32, (2 * BLK, KV_W), 0) == 0, 0.0, g0 + g1)


def _stack_pairs(ref, kh, rows=slice(None)):
    return jnp.concatenate([ref[rows, 128 * (4 * kh + j):128 * (4 * kh + j + 1)] for j in range(4)], axis=0)


def _table_spec():
    return pl.BlockSpec((1, N_KV, 4 * BLK, 4 * BLK), lambda n: (jnp.minimum(n, 1), 0, 0, 0))


def _attn_fwd(q, kv, tab):
    s = q.shape[0]

    def body(q_ref, kp_ref, k0_ref, k1_ref, vp_ref, v0_ref, v1_ref, tab0_ref, tab1_ref, att_ref, stats_ref):
        lane = lax.broadcasted_iota(jnp.int32, (BLK, 128), 1)
        for sub, (kp, kc, vp, vc, tab_ref) in enumerate([(kp_ref, k0_ref, vp_ref, v0_ref, tab0_ref),
                                                         (k0_ref, k1_ref, v0_ref, v1_ref, tab1_ref)]):
            rows = slice(BLK * sub, BLK * (sub + 1))
            k2 = _pair_operands(kp[...], kc[...])
            v2 = _pair_operands(vp[...], vc[...])
            stats = jnp.zeros((BLK, 128), F32)
            for kh in range(N_KV):
                sc = _nt(_stack_pairs(q_ref, kh, rows), k2[kh])
                ps = []
                for e in range(2):
                    lg = sc[:, 256 * e:256 * (e + 1)] + tab_ref[0, kh, :, 256 * e:256 * (e + 1)]
                    m = jnp.max(lg, axis=-1, keepdims=True)
                    ex = jnp.exp(lg - m)
                    den = jnp.sum(ex, axis=-1, keepdims=True)
                    ps.append(ex * (1.0 / den))
                    lse = m + jnp.log(den)
                    for j in range(4):
                        stats = jnp.where(lane == GROUP * kh + 2 * j + e, lse[BLK * j:BLK * (j + 1)], stats)
                out = _nn(jnp.concatenate(ps, axis=1).astype(BF16), v2[kh])
                for j in range(4):
                    att_ref[rows, 128 * (4 * kh + j):128 * (4 * kh + j + 1)] = out[BLK * j:BLK * (j + 1)].astype(BF16)
            stats_ref[rows, :] = stats

    two = lambda m: (m, 0)
    table = lambda pick: pl.BlockSpec((1, N_KV, 4 * BLK, 4 * BLK), lambda m: (pick(m), 0, 0, 0))
    return pl.pallas_call(
        body, name="attn_fwd", grid=(s // (2 * BLK),),
        in_specs=[pl.BlockSpec((2 * BLK, D), two)]
        + [pl.BlockSpec((BLK, KV_W), lambda m, col=col, off=off: (jnp.maximum(2 * m + off, 0), col))
           for col in (0, 1) for off in (-1, 0, 1)]
        + [table(lambda m: jnp.minimum(m, 1)), table(lambda m: 1)],
        out_specs=[pl.BlockSpec((2 * BLK, D), two), pl.BlockSpec((2 * BLK, 128), two)],
        out_shape=[SDS((s, D), BF16), SDS((s, 128), F32)],
        compiler_params=_params(("parallel",)),
    )(q, kv, kv, kv, kv, kv, kv, tab, tab)


def _mid(att, zb, h1, tgt, w_out, g_post, tm):
    s = att.shape[0]
    nt = s // tm

    def body(att_ref, z_ref, h1_ref, t_ref, w_ref, g_ref,
             dh_ref, dqz_ref, datt_ref, loss_ref, dg_ref, dw_ref, dw16_ref, dw_acc, stage):
        @pl.when(pl.program_id(0) == 0)
        def _():
            loss_ref[...] = jnp.zeros_like(loss_ref)
            dg_ref[...] = jnp.zeros_like(dg_ref)
            dw_acc[...] = jnp.zeros_like(dw_acc)
        att = att_ref[...].astype(F32)
        z = z_ref[...].astype(F32)
        sg, sz = _silu_parts(z)
        ob = (att * sz).astype(BF16)
        y2 = _nn(ob, w_ref[...])
        r2 = _rms_scale(y2)
        yh = y2 * r2
        g = g_ref[...]
        err = (h1_ref[...] + yh * g) - t_ref[...]
        loss_ref[...] += jnp.sum(jnp.sum(err * err, axis=-1, keepdims=True) / D)
        dh = err / D
        dh_ref[...] = dh
        _acc_row(dg_ref, 0, jnp.sum(dh * yh, axis=0, keepdims=True))
        dyh = dh * g
        dy = (r2 * (dyh - yh * jnp.mean(dyh * yh, axis=-1, keepdims=True))).astype(BF16)
        dw_acc[...] += _tn(ob, dy)
        dob = _nt(dy, w_ref[...])
        datt_ref[...] = (dob * sz).astype(BF16)
        dqz_ref[...] = (dob * att * _dsilu(z, sg)).astype(BF16)

        @pl.when(pl.program_id(0) == nt - 1)
        def _():
            _write_gradient(dw_acc, dw_ref, dw16_ref, stage)

    row = lambda i: (i, 0)
    fix = lambda i: (0, 0)
    anyspace = pl.BlockSpec(memory_space=pl.ANY)
    return pl.pallas_call(
        body, name="mid", grid=(nt,),
        in_specs=[pl.BlockSpec((tm, D), row)] * 4 + [pl.BlockSpec((D, D), fix), pl.BlockSpec((1, D), fix)],
        out_specs=[pl.BlockSpec((tm, D), row), pl.BlockSpec((tm, D), lambda i: (i, 1)), pl.BlockSpec((tm, D), row),
                   pl.BlockSpec((8, 128), fix), pl.BlockSpec((8, D), fix), anyspace, anyspace],
        out_shape=[SDS((s, D), F32), SDS((s, 2 * D), BF16), SDS((s, D), BF16), SDS((8, 128), F32),
                   SDS((8, D), F32), SDS((D, D), F32), SDS((D, D), BF16)],
        scratch_shapes=[pltpu.VMEM((D, D), F32), pltpu.VMEM((D // 4, D), BF16)],
        compiler_params=_params(("arbitrary",)),
    )(att, zb, h1, tgt, w_out, g_post)


def _attn_bwd(q, kv, datt, stats, tab, dqz):
    s = q.shape[0]
    nb = s // BLK

    def body(q_ref, kp_ref, kc_ref, vp_ref, vc_ref, da_ref, st_ref, tab_ref, dqz_in,
             dq_ref, dkv_ref, dtab_ref, dk_carry, dv_carry):
        del dqz_in
        n = pl.program_id(0)

        @pl.when(n == 0)
        def _():
            dtab_ref[...] = jnp.zeros_like(dtab_ref)
            dk_carry[...] = jnp.zeros_like(dk_carry)
            dv_carry[...] = jnp.zeros_like(dv_carry)

        @pl.when(n < nb)
        def _():
            k2 = _pair_operands(kp_ref[...], kc_ref[...])
            v2 = _pair_operands(vp_ref[...], vc_ref[...])
            lane = lax.broadcasted_iota(jnp.int32, (BLK, 128), 1)
            stats = st_ref[...]
            dk2, dv2 = [], []
            for kh in range(N_KV):
                qs = _stack_pairs(q_ref, kh)
                das = _stack_pairs(da_ref, kh)
                sc = _nt(qs, k2[kh])
                dp = _nt(das, v2[kh])
                ps, dss = [], []
                for e in range(2):
                    heads = [GROUP * kh + 2 * j + e for j in range(4)]
                    lse = jnp.concatenate([jnp.sum(jnp.where(lane == h, stats, 0.0), axis=-1, keepdims=True)
                                           for h in heads], axis=0)
                    cols = slice(256 * e, 256 * (e + 1))
                    p = jnp.exp(sc[:, cols] + tab_ref[0, kh, :, cols] - lse)
                    delta = jnp.sum(p * dp[:, cols], axis=-1, keepdims=True)
                    ds = p * (dp[:, cols] - delta)
                    dtab_ref[kh, :, cols] += ds
                    ps.append(p)
                    dss.append(ds)
                p2 = jnp.concatenate(ps, axis=1).astype(BF16)
                ds2 = jnp.concatenate(dss, axis=1).astype(BF16)
                dq = _nn(ds2, k2[kh]) * Q_SCALE
                for j in range(4):
                    dq_ref[:, 128 * (4 * kh + j):128 * (4 * kh + j + 1)] = dq[BLK * j:BLK * (j + 1)].astype(BF16)
                dk2.append(_tn(ds2, qs))
                dv2.append(_tn(p2, das))
            dkk = _pair_fold(dk2[0], dk2[1])
            dvv = _pair_fold(dv2[0], dv2[1])
            dkv_ref[:, 0:KV_W] = (dk_carry[...] + dkk[0:BLK]).astype(BF16)
            dkv_ref[:, KV_W:2 * KV_W] = (dv_carry[...] + dvv[0:BLK]).astype(BF16)
            dk_carry[...] = dkk[BLK:2 * BLK]
            dv_carry[...] = dvv[BLK:2 * BLK]

        @pl.when(n == nb)
        def _():
            dkv_ref[:, 0:KV_W] = dk_carry[...].astype(BF16)
            dkv_ref[:, KV_W:2 * KV_W] = dv_carry[...].astype(BF16)

    cur = lambda n: (jnp.minimum(n, nb - 1), 0)
    prev = lambda n: (jnp.clip(n - 1, 0, nb - 1), 0)
    return pl.pallas_call(
        body, name="attn_bwd", grid=(nb + 1,),
        in_specs=[pl.BlockSpec((BLK, D), cur),
                  pl.BlockSpec((BLK, KV_W), prev), pl.BlockSpec((BLK, KV_W), cur),
                  pl.BlockSpec((BLK, KV_W), lambda n: (jnp.clip(n - 1, 0, nb - 1), 1)),
                  pl.BlockSpec((BLK, KV_W), lambda n: (jnp.minimum(n, nb - 1), 1)),
                  pl.BlockSpec((BLK, D), cur), pl.BlockSpec((BLK, 128), cur), _table_spec(),
                  pl.BlockSpec(memory_space=pl.ANY)],
        out_specs=[pl.BlockSpec((BLK, D), cur), pl.BlockSpec((BLK, 2 * KV_W), prev),
                   pl.BlockSpec((N_KV, 4 * BLK, 4 * BLK), lambda n: (0, 0, 0))],
        out_shape=[SDS((s, 2 * D), BF16), SDS((s, 2 * KV_W), BF16), SDS((N_KV, 4 * BLK, 4 * BLK), F32)],
        scratch_shapes=[pltpu.VMEM((BLK, KV_W), F32), pltpu.VMEM((BLK, KV_W), F32)],
        input_output_aliases={8: 0},
        compiler_params=_params(("arbitrary",)),
    )(q, kv, kv, kv, kv, datt, stats, tab, dqz)


def _b_bwd(dqz, dkv, h1, dh2, oa, wbin_g, w_kv, g_kv, g_pre, g_apost, tm):
    s = h1.shape[0]
    nt = s // tm

    def body(dqz_ref, dkv_ref, h_ref, dh2_ref, oa_ref, wb_ref, wkv_ref, gk_ref, gb_ref, ga_ref,
             dh1_ref, doa_ref, dg_ref, dwb_ref, dwkv_ref, dwb16_ref, dwkv16_ref, wcat, dwb_acc, dwkv_acc):
        @pl.when(pl.program_id(0) == 0)
        def _():
            dg_ref[...] = jnp.zeros_like(dg_ref)
            dwb_acc[...] = jnp.zeros_like(dwb_acc)
            dwkv_acc[...] = jnp.zeros_like(dwkv_acc)
            for j in range(N_CHIPS):
                pltpu.sync_copy(wb_ref.at[j], wcat.at[:, pl.ds(BIN_COLS * j, BIN_COLS)])
        dnb = _nt(dqz_ref[...], wcat[...])
        dnk = _nt(dkv_ref[...], wkv_ref[...])
        h = h_ref[...]
        r = _rms_scale(h)
        hh = h * r
        dwb_acc[...] += _tn((hh * gb_ref[...]).astype(BF16), dqz_ref[...])
        dwkv_acc[...] += _tn((hh * gk_ref[...]).astype(BF16), dkv_ref[...])
        _acc_row(dg_ref, 0, jnp.sum(dnk * hh, axis=0, keepdims=True))
        _acc_row(dg_ref, 1, jnp.sum(dnb * hh, axis=0, keepdims=True))
        dhh = dnb * gb_ref[...] + dnk * gk_ref[...]
        dh1 = dh2_ref[...] + r * (dhh - hh * jnp.mean(dhh * hh, axis=-1, keepdims=True))
        dh1_ref[...] = dh1
        oa = oa_ref[...].astype(F32)
        ra = _rms_scale(oa)
        oh = oa * ra
        _acc_row(dg_ref, 2, jnp.sum(dh1 * oh, axis=0, keepdims=True))
        doh = dh1 * ga_ref[...]
        doa_ref[...] = (ra * (doh - oh * jnp.mean(doh * oh, axis=-1, keepdims=True))).astype(BF16)

        @pl.when(pl.program_id(0) == nt - 1)
        def _():
            wcat[...] = dwb_acc[...].astype(BF16)
            for j in range(N_CHIPS):
                pltpu.sync_copy(dwb_acc.at[:, pl.ds(BIN_COLS * j, BIN_COLS)], dwb_ref.at[j])
                pltpu.sync_copy(wcat.at[:, pl.ds(BIN_COLS * j, BIN_COLS)], dwb16_ref.at[j])
            pltpu.sync_copy(dwkv_acc, dwkv_ref)
            wcat[:, 0:2 * KV_W] = dwkv_acc[...].astype(BF16)
            pltpu.sync_copy(wcat.at[:, pl.ds(0, 2 * KV_W)], dwkv16_ref)

    row = lambda i: (i, 0)
    fix = lambda i: (0, 0)
    anyspace = pl.BlockSpec(memory_space=pl.ANY)
    return pl.pallas_call(
        body, name="b_bwd", grid=(nt,),
        in_specs=[pl.BlockSpec((tm, 2 * D), row), pl.BlockSpec((tm, 2 * KV_W), row), pl.BlockSpec((tm, D), row),
                  pl.BlockSpec((tm, D), row), pl.BlockSpec((tm, D), row), anyspace, pl.BlockSpec((D, 2 * KV_W), fix),
                  pl.BlockSpec((1, D), fix), pl.BlockSpec((1, D), fix), pl.BlockSpec((1, D), fix)],
        out_specs=[pl.BlockSpec((tm, D), row), pl.BlockSpec((tm, D), row), pl.BlockSpec((8, D), fix)] + [anyspace] * 4,
        out_shape=[SDS((s, D), F32), SDS((s, D), BF16), SDS((8, D), F32), SDS((N_CHIPS, D, BIN_COLS), F32),
                   SDS((D, 2 * KV_W), F32), SDS((N_CHIPS, D, BIN_COLS), BF16), SDS((D, 2 * KV_W), BF16)],
        scratch_shapes=[pltpu.VMEM((D, 2 * D), BF16), pltpu.VMEM((D, 2 * D), F32), pltpu.VMEM((D, 2 * KV_W), F32)],
        compiler_params=_params(("arbitrary",)),
    )(dqz, dkv, h1, dh2, oa, wbin_g, w_kv, g_kv, g_pre, g_apost)


def _to_owner_core(pieces, r, send, recv, core, action):
    x, y, c = lax.axis_index("x"), lax.axis_index("y"), lax.axis_index("c")
    for kp in range(N_CHIPS):
        px, py = kp >> 1, kp & 1
        rel = 4 * (x + px - 2 * x * px) + 2 * (y + py - 2 * y * py) + (c + core - 2 * c * core)

        @pl.when(rel != 0)
        def _():
            cp = pltpu.make_async_remote_copy(src_ref=pieces.at[kp], dst_ref=r.at[rel - 1], send_sem=send.at[kp],
                                              recv_sem=recv.at[rel - 1], device_id=(px, py, core), device_id_type=MESH)
            if action == "start":
                cp.start()
            else:
                cp.wait_send()
    if action == "wait":
        @pl.when(c == core)
        def _():
            for rel in range(1, N_DEV):
                pltpu.make_async_remote_copy(src_ref=pieces.at[0], dst_ref=r.at[rel - 1], send_sem=send.at[0],
                                             recv_sem=recv.at[rel - 1], device_id=(x, y, c),
                                             device_id_type=MESH).wait_recv()


def _owner_core_sems():
    return [pltpu.SemaphoreType.DMA((N_CHIPS,)), pltpu.SemaphoreType.DMA((N_DEV - 1,))]


def _device_exchange(grads, recvs, send, recv):
    x, y, c = lax.axis_index("x"), lax.axis_index("y"), lax.axis_index("c")
    copies = []
    for a, (g, r) in enumerate(zip(grads, recvs)):
        h = g.shape[1] // 2
        for rel in range(1, N_DEV):
            fx, fy, fc = rel >> 2, (rel >> 1) & 1, rel & 1
            px, py, pc = x + fx - 2 * x * fx, y + fy - 2 * y * fy, c + fc - 2 * c * fc
            sem = (N_DEV - 1) * a + rel - 1
            copies.append(pltpu.make_async_remote_copy(
                src_ref=g.at[2 * px + py, pl.ds(pl.multiple_of(pc * h, 16), h)], dst_ref=r.at[rel - 1],
                send_sem=send.at[sem], recv_sem=recv.at[sem], device_id=(px, py, pc), device_id_type=MESH))
    return copies


def _device_exchange_specs(grads):
    anyspace = pl.BlockSpec(memory_space=pl.ANY)
    n = len(grads)
    count = (N_DEV - 1) * n
    return ([anyspace] * n, [anyspace] * n,
            [SDS((N_DEV - 1, g.shape[1] // 2, g.shape[2]), g.dtype) for g in grads],
            [pltpu.SemaphoreType.DMA((count,)), pltpu.SemaphoreType.DMA((count,))])


def _a_bwd(doa, ya, proj, conv_w, w_out, tm, parts):
    s = doa.shape[0]
    nt = s // tm
    n = len(parts)
    ex_in, ex_out, ex_shape, ex_sems = _device_exchange_specs(parts)

    def body(*refs):
        doa_ref, ya_ref, proj_ref, halo_ref, cw_ref, w_ref = refs[:6]
        part_refs = refs[6:6 + n]
        dproj_ref, dcw_ref, dw_ref, dw16_ref = refs[6 + n:10 + n]
        recv_refs = refs[10 + n:10 + 2 * n]
        carry, dw_acc, stage, send, recv = refs[10 + 2 * n:]
        i = pl.program_id(0)
        r = nt - 1 - i

        @pl.when(i == 0)
        def _():
            dcw_ref[...] = jnp.zeros_like(dcw_ref)
            carry[...] = jnp.zeros_like(carry)
            dw_acc[...] = jnp.zeros_like(dw_acc)
            for cp in _device_exchange(part_refs, recv_refs, send, recv):
                cp.start()
        dya = _nt(doa_ref[...], w_ref[...])
        dw_acc[...] += _tn(ya_ref[...], doa_ref[...])
        bg = proj_ref[:, 0:D].astype(F32)
        cg = proj_ref[:, D:2 * D].astype(F32)
        u = proj_ref[:, 2 * D:3 * D].astype(F32)
        z = proj_ref[:, 3 * D:4 * D].astype(F32)
        v = cg * u
        before = jnp.where(r > 0, halo_ref[:, D:2 * D].astype(F32) * halo_ref[:, 2 * D:3 * D].astype(F32), 0.0)
        rows = lax.broadcasted_iota(jnp.int32, (tm, D), 0)
        v1, v2 = _shift_rows(v, before[HALO - 1:HALO, :], before[HALO - 2:HALO - 1, :], rows)
        conv = cw_ref[0:1, :] * v2 + cw_ref[1:2, :] * v1 + cw_ref[2:3, :] * v
        sg, sz = _silu_parts(z)
        dproj_ref[:, 0:D] = (dya * conv * sz).astype(BF16)
        dproj_ref[:, 3 * D:4 * D] = (dya * bg * conv * _dsilu(z, sg)).astype(BF16)
        dconv = dya * bg * sz
        _acc_row(dcw_ref, 0, jnp.sum(dconv * v2, axis=0, keepdims=True))
        _acc_row(dcw_ref, 1, jnp.sum(dconv * v1, axis=0, keepdims=True))
        _acc_row(dcw_ref, 2, jnp.sum(dconv * v, axis=0, keepdims=True))
        after = carry[...]
        up1 = jnp.where(rows < tm - 1, pltpu.roll(dconv, tm - 1, 0), after[0:1, :])
        up2 = jnp.where(rows < tm - 2, pltpu.roll(dconv, tm - 2, 0),
                        jnp.where(rows == tm - 2, after[0:1, :], after[1:2, :]))
        carry[...] = dconv[0:8, :]
        dv = cw_ref[2:3, :] * dconv + cw_ref[1:2, :] * up1 + cw_ref[0:1, :] * up2
        dproj_ref[:, D:2 * D] = (dv * u).astype(BF16)
        dproj_ref[:, 2 * D:3 * D] = (dv * cg).astype(BF16)

        @pl.when(i == nt - 1)
        def _():
            _write_gradient(dw_acc, dw_ref, dw16_ref, stage)
            for cp in _device_exchange(part_refs, recv_refs, send, recv):
                cp.wait()

    rev = lambda i: (nt - 1 - i, 0)
    fix = lambda i: (0, 0)
    halo = lambda i: (jnp.maximum((nt - 1 - i) * (tm // HALO) - 1, 0), 0)
    anyspace = pl.BlockSpec(memory_space=pl.ANY)
    dproj, dcw, dw, dw16, *got = pl.pallas_call(
        body, name="a_bwd", grid=(nt,),
        in_specs=[pl.BlockSpec((tm, D), rev), pl.BlockSpec((tm, D), rev), pl.BlockSpec((tm, 4 * D), rev),
                  pl.BlockSpec((HALO, 4 * D), halo), pl.BlockSpec((8, D), fix), pl.BlockSpec((D, D), fix)] + ex_in,
        out_specs=[pl.BlockSpec((tm, 4 * D), rev), pl.BlockSpec((8, D), fix), anyspace, anyspace] + ex_out,
        out_shape=[SDS((s, 4 * D), BF16), SDS((8, D), F32), SDS((D, D), F32), SDS((D, D), BF16)] + ex_shape,
        scratch_shapes=[pltpu.VMEM((8, D), F32), pltpu.VMEM((D, D), F32), pltpu.VMEM((D // 4, D), BF16)] + ex_sems,
        compiler_params=_params(("arbitrary",)),
    )(doa, ya, proj, proj, conv_w, w_out, *parts)
    return dproj, dcw, dw, dw16, got


def _dn1(dp_ref, w_ref):
    dn = _nt(dp_ref[:, 0:D], w_ref[0])
    for j in range(1, 4):
        dn = dn + _nt(dp_ref[:, D * j:D * (j + 1)], w_ref[j])
    return dn


def _a_in_bwd_matmul(dproj, win_g, tm, count, win_half, win_got):
    def body(dp_ref, w_ref, half_ref, got_in, dn_ref, got_ref, wcat, send, recv):
        del got_in

        @pl.when(pl.program_id(0) == 0)
        def _():
            _to_owner_core(half_ref, got_ref, send, recv, 1, "start")
            for j in range(N_CHIPS):
                pltpu.sync_copy(w_ref.at[j], wcat.at[:, pl.ds(D * j, D)])
        dn_ref[...] = _nt(dp_ref[...], wcat[...]).astype(BF16)

        @pl.when(pl.program_id(0) == count - 1)
        def _():
            _to_owner_core(half_ref, got_ref, send, recv, 1, "wait")

    row = lambda i: (i, 0)
    anyspace = pl.BlockSpec(memory_space=pl.ANY)
    return pl.pallas_call(
        body, name="a_in_bwd_matmul", grid=(count,),
        in_specs=[pl.BlockSpec((tm, 4 * D), row), anyspace, anyspace, anyspace],
        out_specs=[pl.BlockSpec((tm, D), row), anyspace],
        out_shape=[SDS((count * tm, D), BF16), SDS(win_got.shape, win_got.dtype)],
        scratch_shapes=[pltpu.VMEM((D, 4 * D), BF16)] + _owner_core_sems(),
        input_output_aliases={3: 1},
        compiler_params=_params(("arbitrary",)),
    )(dproj, win_g, win_half, win_got)


def _a_in_bwd(dn_first, dproj, x, dh1, win_g, g_pre, tm):
    s = x.shape[0]
    nt = s // tm
    count = dn_first.shape[0] // tm

    def body(dn_ref, dp_ref, x_ref, dh_ref, w_ref, g_ref, gx_ref, dg_ref, dn_s):
        i = pl.program_id(0)

        @pl.when(i == 0)
        def _():
            dg_ref[...] = jnp.zeros_like(dg_ref)

        @pl.when(i < count)
        def _():
            dn_s[...] = dn_ref[...].astype(F32)

        @pl.when(i >= count)
        def _():
            dn_s[...] = _dn1(dp_ref, w_ref)
        dn = dn_s[...]
        xv = x_ref[...]
        r = _rms_scale(xv)
        xh = xv * r
        _acc_row(dg_ref, 0, jnp.sum(dn * xh, axis=0, keepdims=True))
        dxh = dn * g_ref[...]
        gx_ref[...] = dh_ref[...] + r * (dxh - xh * jnp.mean(dxh * xh, axis=-1, keepdims=True))

    row = lambda i: (i, 0)
    fix = lambda i: (0, 0)
    return pl.pallas_call(
        body, name="a_in_bwd", grid=(nt,),
        in_specs=[pl.BlockSpec((tm, D), lambda i: (jnp.minimum(i, count - 1), 0)),
                  pl.BlockSpec((tm, 4 * D), lambda i: (jnp.maximum(i, count), 0)),
                  pl.BlockSpec((tm, D), row), pl.BlockSpec((tm, D), row),
                  pl.BlockSpec((4, D, D), lambda i: (0, 0, 0)), pl.BlockSpec((1, D), fix)],
        out_specs=[pl.BlockSpec((tm, D), row), pl.BlockSpec((8, D), fix)],
        out_shape=[SDS((s, D), F32), SDS((8, D), F32)],
        scratch_shapes=[pltpu.VMEM((tm, D), F32)],
        compiler_params=_params(("arbitrary",)),
    )(dn_first, dproj, x, dh1, win_g, g_pre)


def _dw_in_half(n1, dproj, core, tmw, name, to_owners=None, to_devices=None):
    s = n1.shape[0]
    h = D // 2
    nt = s // tmw
    sent_array = to_owners if to_owners is not None else to_devices
    rides = sent_array is not None
    if to_owners is not None:
        sems, got_shape = _owner_core_sems(), SDS((N_DEV - 1, h, D), BF16)
    elif to_devices is not None:
        _, _, (got_shape,), sems = _device_exchange_specs([to_devices])

    def body(*refs):
        a_ref, b_ref = refs[:2]
        o_ref, o16_ref = refs[2 + rides:4 + rides]
        acc, stage = refs[4 + 2 * rides:6 + 2 * rides]
        t = pl.program_id(0)

        def exchange(action):
            sent, got, send, recv = refs[2], refs[5], refs[8], refs[9]
            if to_owners is not None:
                _to_owner_core(sent, got, send, recv, 1 - core, action)
            else:
                for cp in _device_exchange([sent], [got], send, recv):
                    cp.start() if action == "start" else cp.wait()

        @pl.when(t == 0)
        def _():
            acc[...] = jnp.zeros_like(acc)
            if rides:
                exchange("start")
        acc[...] += _tn(a_ref[...], b_ref[...])

        @pl.when(t == nt - 1)
        def _():
            for j in range(N_CHIPS):
                pltpu.sync_copy(acc.at[:, pl.ds(D * j, D)], o_ref.at[j])
                stage[...] = acc[:, D * j:D * (j + 1)].astype(BF16)
                pltpu.sync_copy(stage, o16_ref.at[j])
            if rides:
                exchange("wait")

    anyspace = pl.BlockSpec(memory_space=pl.ANY)
    return pl.pallas_call(
        body, name=name, grid=(nt,),
        in_specs=[pl.BlockSpec((tmw, h), lambda t: (t, core)), pl.BlockSpec((tmw, 4 * D), lambda t: (t, 0))]
        + [anyspace] * rides,
        out_specs=[anyspace, anyspace] + [anyspace] * rides,
        out_shape=[SDS((N_CHIPS, h, D), F32), SDS((N_CHIPS, h, D), BF16)] + ([got_shape] if rides else []),
        scratch_shapes=[pltpu.VMEM((h, 4 * D), F32), pltpu.VMEM((h, D), BF16)] + (sems if rides else []),
        compiler_params=_params(("arbitrary",)),
    )(n1, dproj, *([sent_array] if rides else []))


def _sibling_exchange(name, to_sibling=(), shards=(), smalls=()):
    n_g, n_h, n_s = len(to_sibling), len(shards), len(smalls)

    def body(*refs):
        gs = refs[:n_g]
        pos = n_g + n_h
        small_ins = refs[pos:pos + n_s]
        pos += n_s
        rs, fs = refs[pos:pos + n_g], refs[pos + n_g:pos + n_g + n_h]
        pos += n_g + n_h
        small_alls = refs[pos:pos + n_s]
        pos += n_s
        dsend, drecv, ssend, srecv = refs[pos:]
        x, y, c = lax.axis_index("x"), lax.axis_index("y"), lax.axis_index("c")
        sibling = (x, y, 1 - c)
        sends, arrivals = [], []
        for a, (g, r) in enumerate(zip(gs, rs)):
            h = g.shape[1] // 2
            src = g.at[:, pl.ds(pl.multiple_of((1 - c) * h, 8), h), :]
            sends.append(pltpu.make_async_remote_copy(src_ref=src, dst_ref=r, send_sem=dsend.at[a], recv_sem=drecv.at[a],
                                                      device_id=sibling, device_id_type=MESH))
            arrivals.append(pltpu.make_async_remote_copy(src_ref=r, dst_ref=r, send_sem=dsend.at[a], recv_sem=drecv.at[a],
                                                         device_id=sibling, device_id_type=MESH))
        for b, full in enumerate(fs):
            h = full.shape[0] // 2
            mine = full.at[pl.ds(pl.multiple_of(c * h, 8), h)]
            theirs = full.at[pl.ds(pl.multiple_of((1 - c) * h, 8), h)]
            sends.append(pltpu.make_async_remote_copy(src_ref=mine, dst_ref=mine, send_sem=dsend.at[n_g + b],
                                                      recv_sem=drecv.at[n_g + b], device_id=sibling, device_id_type=MESH))
            arrivals.append(pltpu.make_async_remote_copy(src_ref=mine, dst_ref=theirs, send_sem=dsend.at[n_g + b],
                                                         recv_sem=drecv.at[n_g + b], device_id=sibling, device_id_type=MESH))
        me = 4 * x + 2 * y + c
        for k, (small_in, small_all) in enumerate(zip(small_ins, small_alls)):
            small_all[me] = small_in[...]
            for rel in range(1, N_DEV):
                fx, fy, fc = rel >> 2, (rel >> 1) & 1, rel & 1
                peer = (x + fx - 2 * x * fx, y + fy - 2 * y * fy, c + fc - 2 * c * fc)
                sender = 4 * peer[0] + 2 * peer[1] + peer[2]
                sem = (N_DEV - 1) * k + rel - 1
                sends.append(pltpu.make_async_remote_copy(
                    src_ref=small_in, dst_ref=small_all.at[me], send_sem=ssend.at[sem], recv_sem=srecv.at[sem],
                    device_id=peer, device_id_type=MESH))
                arrivals.append(pltpu.make_async_remote_copy(
                    src_ref=small_in, dst_ref=small_all.at[sender], send_sem=ssend.at[sem], recv_sem=srecv.at[sem],
                    device_id=peer, device_id_type=MESH))
        for cp in sends:
            cp.start()
        for cp in arrivals:
            cp.wait_recv()
        for cp in sends:
            cp.wait_send()

    anyspace = pl.BlockSpec(memory_space=pl.ANY)
    vm = pl.BlockSpec(memory_space=pltpu.VMEM)
    out_shape = [SDS((N_CHIPS, g.shape[1] // 2, g.shape[2]), F32) for g in to_sibling]
    out_shape += [SDS(full.shape, F32) for full in shards]
    out_shape += [SDS((N_DEV,) + sm.shape, F32) for sm in smalls]
    n_d2d = max(n_g + n_h, 1)
    n_all = (N_DEV - 1) * max(n_s, 1)
    outs = pl.pallas_call(
        body, name=name, out_shape=out_shape,
        in_specs=[anyspace] * (n_g + n_h) + [vm] * n_s, out_specs=[anyspace] * (n_g + n_h) + [vm] * n_s,
        scratch_shapes=[pltpu.SemaphoreType.DMA((n_d2d,)), pltpu.SemaphoreType.DMA((n_d2d,)),
                        pltpu.SemaphoreType.DMA((n_all,)), pltpu.SemaphoreType.DMA((n_all,))],
        input_output_aliases={n_g + b: n_g + b for b in range(n_h)},
    )(*to_sibling, *shards, *smalls)
    return outs[:n_g], outs[n_g:n_g + n_h], outs[n_g + n_h:]


def _add_win(where, lo, hi, r, name):
    _, h, cols = lo.shape
    tr = min(h, 256)
    nh = h // tr

    def body(where_ref, lo_ref, hi_ref, r_ref, o_ref):
        acc = jnp.where(where_ref[0] == 0, lo_ref[0], hi_ref[0])
        for k in range(N_DEV - 1):
            acc = acc + r_ref[k].astype(F32)
        o_ref[...] = acc

    own = pl.BlockSpec((1, tr, cols), lambda i, w: (w[1], i, 0))
    return pl.pallas_call(
        body, name=name,
        grid_spec=pltpu.PrefetchScalarGridSpec(
            num_scalar_prefetch=1, grid=(nh,),
            in_specs=[own, own, pl.BlockSpec((N_DEV - 1, tr, cols), lambda i, w: (0, i, 0))],
            out_specs=pl.BlockSpec((tr, cols), lambda i, w: (w[0] * nh + i, 0))),
        out_shape=SDS((2 * h, cols), F32),
        compiler_params=_params(("parallel",)),
    )(where, lo, hi, r)


def _add_devices(where, g, r, name):
    _, rows, cols = g.shape
    h = rows // 2
    tr = min(h, 256)
    nh = h // tr

    def body(where_ref, g_ref, r_ref, o_ref):
        del where_ref
        acc = g_ref[0]
        for k in range(N_DEV - 1):
            acc = acc + r_ref[k].astype(F32)
        o_ref[...] = acc

    return pl.pallas_call(
        body, name=name,
        grid_spec=pltpu.PrefetchScalarGridSpec(
            num_scalar_prefetch=1, grid=(nh,),
            in_specs=[pl.BlockSpec((1, tr, cols), lambda i, w: (w[1], w[0] * nh + i, 0)),
                      pl.BlockSpec((N_DEV - 1, tr, cols), lambda i, w: (0, i, 0))],
            out_specs=pl.BlockSpec((tr, cols), lambda i, w: (w[0] * nh + i, 0))),
        out_shape=SDS((rows, cols), F32),
        compiler_params=_params(("parallel",)),
    )(where, g, r)


def _sum_smalls(gathered):
    n = len(gathered)

    def body(*refs):
        for all_ref, o_ref in zip(refs[:n], refs[n:]):
            acc = all_ref[0]
            for dev in range(1, N_DEV):
                acc = acc + all_ref[dev]
            o_ref[...] = acc

    vm = pl.BlockSpec(memory_space=pltpu.VMEM)
    return pl.pallas_call(
        body, name="sum_smalls", out_shape=[SDS(a.shape[1:], F32) for a in gathered],
        in_specs=[vm] * n, out_specs=[vm] * n,
    )(*gathered)


def _adam_step(g, w, m, v):
    nm = ADAM_B1 * m + (1.0 - ADAM_B1) * g
    nv = ADAM_B2 * v + (1.0 - ADAM_B2) * (g * g)
    m_hat = nm / (1.0 - ADAM_B1 ** ADAM_STEP)
    v_hat = nv / (1.0 - ADAM_B2 ** ADAM_STEP)
    return -ADAM_LR * (m_hat / (jnp.sqrt(v_hat) + ADAM_EPS) + ADAM_WD * w), nm, nv


def _adamw(g, w, m, v, name):
    rows, cols = g.shape
    tr = min(rows, 256)

    def body(g_ref, w_ref, m_ref, v_ref, d_ref, nm_ref, nv_ref):
        d_ref[...], nm_ref[...], nv_ref[...] = _adam_step(g_ref[...], w_ref[...], m_ref[...], v_ref[...])

    spec = pl.BlockSpec((tr, cols), lambda i: (i, 0))
    return pl.pallas_call(
        body, name=name, grid=(rows // tr,), in_specs=[spec] * 4, out_specs=[spec] * 3,
        out_shape=[SDS(g.shape, F32)] * 3, compiler_params=_params(("parallel",)),
    )(g, w, m, v)


def _small_update(chip, tot, tot_rel, wmv):
    names = list(SMALL_PLACES)
    n = len(names)

    def body(chip_ref, tot_ref, quarter_ref, rel_ref, *refs):
        del chip_ref
        ins, outs = refs[:3 * n], refs[3 * n:]
        for i, nm in enumerate(names):
            source, row, (rows, cols) = SMALL_PLACES[nm]
            g = {"rows": tot_ref, "quarter": quarter_ref, "rel": rel_ref}[source][row:row + rows, 0:cols]
            outs[4 * i][...] = g
            outs[4 * i + 1][...], outs[4 * i + 2][...], outs[4 * i + 3][...] = _adam_step(
                g, ins[3 * i][...], ins[3 * i + 1][...], ins[3 * i + 2][...])

    whole = lambda shape: pl.BlockSpec(shape, lambda i, c: (0,) * len(shape))
    shapes = [SMALL_PLACES[nm][2] for nm in names]
    outs = pl.pallas_call(
        body, name="small_update",
        grid_spec=pltpu.PrefetchScalarGridSpec(
            num_scalar_prefetch=1, grid=(1,),
            in_specs=[whole(tot.shape), pl.BlockSpec((tot.shape[0], D // 4), lambda i, c: (0, c[0])),
                      whole(tot_rel.shape)] + [whole(shp) for shp in shapes for _ in range(3)],
            out_specs=[whole(shp) for shp in shapes for _ in range(4)]),
        out_shape=[SDS(shp, F32) for shp in shapes for _ in range(4)],
    )(chip, tot, tot, tot_rel, *[a for nm in names for a in wmv[nm]])
    return {nm: tuple(outs[4 * i:4 * i + 4]) for i, nm in enumerate(names)}


def _pad_rows(a, rows):
    return jnp.concatenate([a, jnp.zeros((rows - a.shape[0], a.shape[1]), a.dtype)], axis=0)


def _pad_cols(a, cols):
    return jnp.concatenate([a, jnp.zeros((a.shape[0], cols - a.shape[1]), a.dtype)], axis=1)


def kernel(x, a_pre_norm, a_w_in, a_conv_w, a_w_out, a_post_norm, kv_norm, w_kv, rel_bias, b_pre_norm, b_w_in, b_sinks, b_w_out, b_post_norm, loss_target, m_a_pre_norm, m_a_w_in, m_a_conv_w, m_a_w_out, m_a_post_norm, m_kv_norm, m_w_kv, m_rel_bias, m_b_pre_norm, m_b_w_in, m_b_sinks, m_b_w_out, m_b_post_norm, v_a_pre_norm, v_a_w_in, v_a_conv_w, v_a_w_out, v_a_post_norm, v_kv_norm, v_w_kv, v_rel_bias, v_b_pre_norm, v_b_w_in, v_b_sinks, v_b_w_out, v_b_post_norm):
    seq = x.shape[1]
    xs = x.reshape(seq, D)
    tgt = loss_target.reshape(seq, D)
    chip = 2 * lax.axis_index("x") + lax.axis_index("y")
    core = lax.axis_index("c")
    tm = _tile(seq, 512)
    tmw = _tile(seq, 1024)

    shards = [a_w_in[0], a_w_out[0], w_kv, b_w_in[0], b_w_out[0]]
    small_w = _pad_rows(jnp.concatenate([a_pre_norm, a_conv_w[0], a_post_norm], axis=0), 8)
    *own_only, small_g = _prepare_weights(shards, small_w)
    where = jnp.stack([core, chip]).astype(jnp.int32)
    small_full = small_g.transpose(1, 0, 2).reshape(8, D)
    g_apre, conv_w, g_apost = small_full[0:1], _pad_rows(small_full[1:4], 8), small_full[4:5]
    g_kv = kv_norm.reshape(1, D)

    proj, n1, (win_g, wouta_g, wkv_g, wbin_g, woutb_g) = _a_in(where[1:2], xs, g_apre, own_only, tmw)
    wouta = wouta_g.reshape(D, D)
    wkv = wkv_g.reshape(D, 2 * KV_W)
    woutb = woutb_g.reshape(D, D)
    ya, oa, h1 = _a_mix(proj, xs, conv_w, wouta, g_apost, tm)
    kv, q, zb = _b_in(h1, g_kv, b_pre_norm, wkv, wbin_g, tmw)
    tab = _bias_table(rel_bias, b_sinks.reshape(N_HEADS))
    att, stats = _attn_fwd(q, kv, tab)
    dh2, dqz, datt, loss_acc, dg_bpost, dw_outb, dw_outb16 = _mid(att, zb, h1, tgt, woutb, b_post_norm, tm)

    dqz, dkv, dtab = _attn_bwd(q, kv, datt, stats, tab, dqz)
    dh1, doa, dg_b, dw_bin, dw_kv, dw_bin16, dw_kv16 = _b_bwd(dqz, dkv, h1, dh2, oa, wbin_g, wkv, g_kv, b_pre_norm,
                                                              g_apost, tm)
    by_chip = lambda a, cols: a.reshape(N_CHIPS, D // 4, cols)
    grads1 = [by_chip(dw_kv, 2 * KV_W), dw_bin, by_chip(dw_outb, D)]
    sent1 = [by_chip(dw_kv16, 2 * KV_W), dw_bin16, by_chip(dw_outb16, D)]
    names1 = ["w_kv", "b_w_in", "b_w_out"]
    dproj, dconv_w, dw_outa, dw_outa16, from_devices1 = _a_bwd(doa, ya, proj, conv_w, wouta, tm, sent1)
    shards1 = [_add_devices(where, g, r, "add_devices_" + nm) for g, r, nm in zip(grads1, from_devices1, names1)]
    tmw2 = _tile(seq, 2048)
    win_lo, win_lo16, outa_got = _dw_in_half(n1, dproj, 0, tmw2, "dw_a_in_lo", to_devices=by_chip(dw_outa16, D))
    win_hi, win_hi16, win_got = _dw_in_half(n1, dproj, 1, tmw2, "dw_a_in_hi", to_owners=win_lo16)
    nt = seq // tmw
    dn_first, win_got = _a_in_bwd_matmul(dproj, win_g, tmw, max(nt - max(nt // 4, 1), 1), win_hi16, win_got)
    grad_x, dg_apre = _a_in_bwd(dn_first, dproj, xs, dh1, win_g, g_apre, tm)
    shards2 = [_add_win(where, win_lo, win_hi, win_got, "add_devices_a_w_in"),
               _add_devices(where, by_chip(dw_outa, D), outa_got, "add_devices_a_w_out")]
    drel, dsink = _bias_fold(dtab)

    smalls = jnp.concatenate([
        dg_apre[0:1], dg_b[2:3], dg_b[0:1], dg_b[1:2], dg_bpost[0:1], _pad_cols(dsink[0:1], D),
        _pad_cols(loss_acc[0:1], D), jnp.zeros((1, D), F32), dconv_w], axis=0)
    assert smalls.shape == (SMALL_ROWS, D)
    _, (g_wkv, g_wbin, g_woutb, g_win, g_wouta), gathered = _sibling_exchange(
        "share_last", shards=shards1 + shards2, smalls=(smalls, drel))
    tot, tot_rel = _sum_smalls(gathered)

    big = {}
    for nm, g, w, m, v in [("a_w_in", g_win, a_w_in, m_a_w_in, v_a_w_in), ("a_w_out", g_wouta, a_w_out, m_a_w_out, v_a_w_out),
                           ("w_kv", g_wkv, w_kv, m_w_kv, v_w_kv), ("b_w_in", g_wbin, b_w_in, m_b_w_in, v_b_w_in),
                           ("b_w_out", g_woutb, b_w_out, m_b_w_out, v_b_w_out)]:
        shp = w.shape
        two = (shp[-2], shp[-1])
        d, nm_, nv_ = _adamw(g, w.reshape(two), m.reshape(two), v.reshape(two), "adamw_" + nm)
        big[nm] = (g.reshape(shp), d.reshape(shp), nm_.reshape(shp), nv_.reshape(shp))

    given = {"a_pre_norm": (a_pre_norm, m_a_pre_norm, v_a_pre_norm), "a_conv_w": (a_conv_w, m_a_conv_w, v_a_conv_w),
             "a_post_norm": (a_post_norm, m_a_post_norm, v_a_post_norm), "kv_norm": (kv_norm, m_kv_norm, v_kv_norm),
             "rel_bias": (rel_bias, m_rel_bias, v_rel_bias), "b_pre_norm": (b_pre_norm, m_b_pre_norm, v_b_pre_norm),
             "b_sinks": (b_sinks, m_b_sinks, v_b_sinks), "b_post_norm": (b_post_norm, m_b_post_norm, v_b_post_norm)}
    small = _small_update(where[1:2], tot, tot_rel, {nm: tuple(a.reshape(SMALL_PLACES[nm][2]) for a in wmv)
                                            for nm, wmv in given.items()})
    order = ["a_pre_norm", "a_w_in", "a_conv_w", "a_w_out", "a_post_norm", "kv_norm", "w_kv", "rel_bias",
             "b_pre_norm", "b_w_in", "b_sinks", "b_w_out", "b_post_norm"]
    outs = []
    for which in range(4):
        for nm in order:
            outs.append(big[nm][which] if nm in big else small[nm][which].reshape(given[nm][0].shape))
    loss = 0.5 * tot[LOSS_ROW, 0]
    return (loss, grad_x.reshape(x.shape), *outs)
```

```python
import math

import jax
import jax.numpy as jnp
from jax import lax
from jax.experimental import pallas as pl
from jax.experimental.pallas import tpu as pltpu

F32 = jnp.float32
BF16 = jnp.bfloat16
MESH = pl.DeviceIdType.MESH
SDS = jax.ShapeDtypeStruct

D = 1024
HEAD_DIM = 64
N_HEADS = 16
N_KV = 2
GROUP = 8
KV_W = 128
BLK = 128
N_BUCKETS = 32
MAX_EXACT = 16
MAX_DISTANCE = 128
EPS = 1e-6
NEG_INF = -1e30
Q_SCALE = HEAD_DIM ** -0.5

ADAM_LR = 0.001
ADAM_B1 = 0.9
ADAM_B2 = 0.999
ADAM_EPS = 1e-08
ADAM_WD = 0.01
ADAM_STEP = 10

N_CHIPS = 4
N_DEV = 8
BIN_COLS = 2 * D // N_CHIPS
VMEM_LIMIT = 56 * 1024 * 1024
SMALL_ROWS = 16
LOSS_ROW = 6
SMALL_PLACES = {
    "a_pre_norm": ("quarter", 0, (1, D // 4)), "a_conv_w": ("quarter", 8, (3, D // 4)),
    "a_post_norm": ("quarter", 1, (1, D // 4)), "kv_norm": ("rows", 2, (1, D)),
    "rel_bias": ("rel", 0, (N_BUCKETS, N_HEADS)), "b_pre_norm": ("rows", 3, (1, D)),
    "b_sinks": ("rows", 5, (1, N_HEADS)), "b_post_norm": ("rows", 4, (1, D)),
}


def _bucket_thresholds():
    def bucket(d):
        big = MAX_EXACT + int(math.log(d / MAX_EXACT) / math.log(MAX_DISTANCE / MAX_EXACT)
                              * (N_BUCKETS - MAX_EXACT))
        return d if d < MAX_EXACT else min(big, N_BUCKETS - 1)
    out = []
    for b in range(MAX_EXACT + 1, N_BUCKETS):
        out.append(min(d for d in range(MAX_EXACT, MAX_DISTANCE) if bucket(d) >= b))
    return tuple(out)


BUCKET_THRESHOLDS = _bucket_thresholds()


def _params(semantics=None, vmem=VMEM_LIMIT):
    return pltpu.CompilerParams(dimension_semantics=semantics, vmem_limit_bytes=vmem)


def _tile(n, pref):
    return pref if n >= 2 * pref else max(n // 2, 8)


def _rms_scale(v):
    return lax.rsqrt(jnp.mean(v * v, axis=-1, keepdims=True) + EPS)


def _nt(a, b):
    return lax.dot_general(a, b, (((1,), (1,)), ((), ())), preferred_element_type=F32)


def _tn(a, b):
    return lax.dot_general(a, b, (((0,), (0,)), ((), ())), preferred_element_type=F32)


def _nn(a, b):
    return jnp.dot(a, b, preferred_element_type=F32)


def _silu_parts(z):
    sg = jax.nn.sigmoid(z)
    return sg, z * sg


def _dsilu(z, sg):
    return sg * (1.0 + z * (1.0 - sg))


def _write_gradient(acc, out32, out16, stage):
    pltpu.sync_copy(acc, out32)
    rows = stage.shape[0]
    for k in range(acc.shape[0] // rows):
        stage[...] = acc[rows * k:rows * (k + 1), :].astype(BF16)
        pltpu.sync_copy(stage, out16.at[pl.ds(rows * k, rows)])


def _acc_row(ref, row, val):
    ref[row:row + 1, :] += val


def _gather_copies(outs, splits, ici_send, ici_recv, d2d_send, d2d_recv):
    x, y, c = lax.axis_index("x"), lax.axis_index("y"), lax.axis_index("c")
    k = 2 * x + y
    sibling = (x, y, 1 - c)

    def part(o_ref, chip, core, split):
        if not split:
            return o_ref.at[chip]
        h = o_ref.shape[1] // 2
        return o_ref.at[chip, pl.ds(pl.multiple_of(core * h, 16), h)]

    def remote(ref, a, j, sems, to):
        return pltpu.make_async_remote_copy(src_ref=ref, dst_ref=ref, send_sem=sems[0].at[3 * a + j],
                                            recv_sem=sems[1].at[3 * a + j], device_id=to, device_id_type=MESH)

    copies = []
    for a, (o_ref, split) in enumerate(zip(outs, splits)):
        for j, (px, py) in enumerate([(x, 1 - y), (1 - x, y), (1 - x, 1 - y)]):
            kj = 2 * px + py
            ici, d2d = (ici_send, ici_recv), (d2d_send, d2d_recv)
            copies.append((remote(part(o_ref, k, c, split), a, j, ici, (px, py, c)),
                           remote(part(o_ref, kj, c, split), a, j, ici, (px, py, c)),
                           remote(part(o_ref, kj, c, split), a, j, d2d, sibling) if split else None,
                           remote(part(o_ref, kj, 1 - c, split), a, j, d2d, sibling) if split else None))
    return copies


def _gather_sems(n):
    return [pltpu.SemaphoreType.DMA((3 * n,)) for _ in range(4)]


def _prepare_weights(shards, small):
    n = len(shards)

    def body(*refs):
        ins, small_in = refs[:n], refs[n]
        outs, small_out = refs[n + 1:2 * n + 1], refs[2 * n + 1]
        stages, put_sem = refs[2 * n + 2:3 * n + 2], refs[3 * n + 2]
        sems = refs[3 * n + 3:]
        k = 2 * lax.axis_index("x") + lax.axis_index("y")
        puts = []
        for a, (i_ref, stage, o_ref) in enumerate(zip(ins, stages, outs)):
            stage[...] = i_ref[...].astype(BF16)
            puts.append(pltpu.make_async_copy(stage, o_ref.at[k], put_sem.at[a]))
            puts[-1].start()
        small_out[k] = small_in[...]
        copies = _gather_copies([small_out], [False], *sems)
        for send, _, _, _ in copies:
            send.start()
        for _, arrival, _, _ in copies:
            arrival.wait_recv()
        for send, _, _, _ in copies:
            send.wait_send()
        for put in puts:
            put.wait()

    vm = pl.BlockSpec(memory_space=pltpu.VMEM)
    anyspace = pl.BlockSpec(memory_space=pl.ANY)
    out_shape = [SDS((N_CHIPS,) + s.shape, BF16) for s in shards] + [SDS((N_CHIPS,) + small.shape, F32)]
    return pl.pallas_call(
        body, name="prepare_weights", out_shape=out_shape,
        in_specs=[vm] * (n + 1), out_specs=[anyspace] * n + [vm],
        scratch_shapes=[pltpu.VMEM(s.shape, BF16) for s in shards] + [pltpu.SemaphoreType.DMA((n,))] + _gather_sems(1),
        compiler_params=pltpu.CompilerParams(vmem_limit_bytes=VMEM_LIMIT),
    )(*shards, small)


def _a_in(chip, x, g_pre, weights, tm):
    s = x.shape[0]
    nt = s // tm
    n = len(weights)

    def body(chip_ref, x_ref, g_ref, *refs):
        proj_ref, n1_ref = refs[n:n + 2]
        gathered = refs[n + 2:2 * n + 2]
        wbuf, n1_all, fetch_sem = refs[2 * n + 2:2 * n + 5]
        sems = refs[2 * n + 5:]
        jj, i = pl.program_id(0), pl.program_id(1)
        copies = _gather_copies(gathered, [True] * n, *sems)

        def fetch(rel):
            slot = jnp.bitwise_xor(chip_ref[0], rel)
            return pltpu.make_async_copy(gathered[0].at[slot], wbuf.at[rel % 2], fetch_sem.at[rel % 2])

        @pl.when((jj == 0) & (i == 0))
        def _():
            fetch(0).start()
            copies[0][0].start()
            copies[1][0].start()
            fetch(0).wait()

        for rel in (1, 2, 3):
            @pl.when((jj == rel) & (i == 0))
            def _():
                fetch(rel).wait()

        @pl.when(jj == 0)
        def _():
            xv = x_ref[...]
            n1 = (xv * _rms_scale(xv) * g_ref[...]).astype(BF16)
            n1_ref[...] = n1
            n1_all[i] = n1
        proj_ref[...] = _nn(n1_all[i], wbuf[jj % 2]).astype(BF16)

        for rel in (1, 2, 3):
            @pl.when((jj == rel - 1) & (i == max(nt - 2, nt // 2)))
            def _():
                _, arrival, forward, forwarded = copies[rel - 1]
                arrival.wait_recv()
                forward.start()
                forwarded.wait_recv()
                fetch(rel).start()
                if rel == 1:
                    for send, _, _, _ in copies[2:]:
                        send.start()

        @pl.when((jj == 3) & (i == max(nt - 2, 0)))
        def _():
            for _, arrival, forward, _ in copies[3:]:
                arrival.wait_recv()
                forward.start()

        @pl.when((jj == 3) & (i == nt - 1))
        def _():
            for _, _, _, forwarded in copies[3:]:
                forwarded.wait_recv()
            for send, _, forward, _ in copies:
                forward.wait_send()
                send.wait_send()

    anyspace = pl.BlockSpec(memory_space=pl.ANY)
    proj, n1, *gathered = pl.pallas_call(
        body, name="a_in",
        grid_spec=pltpu.PrefetchScalarGridSpec(
            num_scalar_prefetch=1, grid=(4, nt),
            in_specs=[pl.BlockSpec((tm, D), lambda jj, i, c: (jnp.where(jj == 0, i, nt - 1), 0)),
                      pl.BlockSpec((1, D), lambda jj, i, c: (0, 0))] + [anyspace] * n,
            out_specs=[pl.BlockSpec((tm, D), lambda jj, i, c: (i, jnp.bitwise_xor(c[0], jj))),
                       pl.BlockSpec((tm, D), lambda jj, i, c: (jnp.where(jj == 0, i, nt - 1), 0))] + [anyspace] * n,
            scratch_shapes=[pltpu.VMEM((2, D, D), BF16), pltpu.VMEM((nt, tm, D), BF16),
                            pltpu.SemaphoreType.DMA((2,))] + _gather_sems(n)),
        out_shape=[SDS((s, 4 * D), BF16), SDS((s, D), BF16)] + [SDS(w.shape, w.dtype) for w in weights],
        input_output_aliases={3 + a: 2 + a for a in range(n)},
        compiler_params=_params(("arbitrary", "arbitrary")),
    )(chip, x, g_pre, *weights)
    return proj, n1, gathered


def _shift_rows(v, last, second_last, rows):
    v1 = jnp.where(rows >= 1, pltpu.roll(v, 1, 0), last)
    v2 = jnp.where(rows >= 2, pltpu.roll(v, 2, 0), jnp.where(rows == 1, last, second_last))
    return v1, v2


def _a_mix(proj, x, conv_w, w_out, g_post, tm):
    s = x.shape[0]

    def body(proj_ref, x_ref, cw_ref, w_ref, g_ref, ya_ref, oa_ref, h1_ref, conv_ref, carry):
        @pl.when(pl.program_id(0) == 0)
        def _():
            carry[...] = jnp.zeros_like(carry)
        v = proj_ref[:, D:2 * D].astype(F32) * proj_ref[:, 2 * D:3 * D].astype(F32)
        rows = lax.broadcasted_iota(jnp.int32, (tm, D), 0)
        before = carry[...]
        v1, v2 = _shift_rows(v, before[7:8, :], before[6:7, :], rows)
        carry[...] = v[tm - 8:tm, :]
        conv = cw_ref[0:1, :] * v2 + cw_ref[1:2, :] * v1 + cw_ref[2:3, :] * v
        conv_ref[...] = conv.astype(BF16)
        _, sz = _silu_parts(proj_ref[:, 3 * D:4 * D].astype(F32))
        ya = (proj_ref[:, 0:D].astype(F32) * conv * sz).astype(BF16)
        ya_ref[...] = ya
        oa = _nn(ya, w_ref[...])
        oa_ref[...] = oa.astype(BF16)
        h1_ref[...] = x_ref[...] + oa * _rms_scale(oa) * g_ref[...]

    row = lambda i: (i, 0)
    fix = lambda i: (0, 0)
    return pl.pallas_call(
        body, name="a_mix", grid=(s // tm,),
        in_specs=[pl.BlockSpec((tm, 4 * D), row), pl.BlockSpec((tm, D), row), pl.BlockSpec((8, D), fix),
                  pl.BlockSpec((D, D), fix), pl.BlockSpec((1, D), fix)],
        out_specs=[pl.BlockSpec((tm, D), row)] * 4,
        out_shape=[SDS((s, D), BF16), SDS((s, D), BF16), SDS((s, D), F32), SDS((s, D), BF16)],
        scratch_shapes=[pltpu.VMEM((8, D), F32)],
        compiler_params=_params(("arbitrary",)),
    )(proj, x, conv_w, w_out, g_post)


def _b_in(h1, g_kv, g_pre, w_kv, wbin_g, tm):
    s = h1.shape[0]

    def body(h_ref, gk_ref, gb_ref, wkv_ref, wb_ref, kv_ref, q_ref, z_ref):
        h = h_ref[...]
        hh = h * _rms_scale(h)
        nk = (hh * gk_ref[...]).astype(BF16)
        nb = (hh * gb_ref[...]).astype(BF16)
        kv_ref[...] = _nn(nk, wkv_ref[...]).astype(BF16)
        for j in range(2):
            q_ref[:, BIN_COLS * j:BIN_COLS * (j + 1)] = (_nn(nb, wb_ref[j]) * Q_SCALE).astype(BF16)
            z_ref[:, BIN_COLS * j:BIN_COLS * (j + 1)] = _nn(nb, wb_ref[2 + j]).astype(BF16)

    row = lambda i: (i, 0)
    fix = lambda i: (0, 0)
    return pl.pallas_call(
        body, name="b_in", grid=(s // tm,),
        in_specs=[pl.BlockSpec((tm, D), row), pl.BlockSpec((1, D), fix), pl.BlockSpec((1, D), fix),
                  pl.BlockSpec((D, 2 * KV_W), fix), pl.BlockSpec((N_CHIPS, D, BIN_COLS), lambda i: (0, 0, 0))],
        out_specs=[pl.BlockSpec((tm, 2 * KV_W), row), pl.BlockSpec((tm, D), row), pl.BlockSpec((tm, D), row)],
        out_shape=[SDS((s, 2 * KV_W), BF16), SDS((s, D), BF16), SDS((s, D), BF16)],
        compiler_params=_params(("parallel",)),
    )(h1, g_kv, g_pre, w_kv, wbin_g)


def _band_buckets():
    q = lax.broadcasted_iota(jnp.int32, (BLK, 2 * BLK), 0)
    k = lax.broadcasted_iota(jnp.int32, (BLK, 2 * BLK), 1)
    dist = q + BLK - k
    bucket = jnp.where(dist < MAX_EXACT, dist, MAX_EXACT)
    for t in BUCKET_THRESHOLDS:
        bucket = bucket + jnp.where(dist >= t, 1, 0)
    in_window = (dist >= 0) & (dist < BLK)
    return jnp.where(in_window, bucket, -1)


def _head_place(h):
    kh, j, e = h // GROUP, (h % GROUP) // 2, h % 2
    return kh, slice(BLK * j, BLK * (j + 1)), slice(2 * BLK * e, 2 * BLK * (e + 1))


def _bias_table(rel_bias, sinks):
    def body(rb_ref, sink_ref, tab_ref):
        bucket = _band_buckets()
        col = lax.broadcasted_iota(jnp.int32, (BLK, 2 * BLK), 1)
        for h in range(N_HEADS):
            acc = jnp.where(bucket < 0, NEG_INF, 0.0).astype(F32)
            for b in range(N_BUCKETS):
                acc = jnp.where(bucket == b, rb_ref[b, h], acc)
            acc = jnp.where(col == 0, sink_ref[h], acc)
            kh, rows, cols = _head_place(h)
            tab_ref[1, kh, rows, cols] = acc
            tab_ref[0, kh, rows, cols] = jnp.where((col > 0) & (col < BLK), NEG_INF, acc)

    return pl.pallas_call(
        body, name="bias_table", out_shape=SDS((2, N_KV, 4 * BLK, 4 * BLK), F32),
        in_specs=[pl.BlockSpec(memory_space=pltpu.SMEM), pl.BlockSpec(memory_space=pltpu.SMEM)],
        out_specs=pl.BlockSpec(memory_space=pltpu.VMEM),
    )(rel_bias, sinks)


def _bias_fold(dtab):
    def body(dtab_ref, out_ref, dsink_ref):
        bucket = _band_buckets()
        row = lax.broadcasted_iota(jnp.int32, (N_BUCKETS, 128), 0)
        lane = lax.broadcasted_iota(jnp.int32, (N_BUCKETS, 128), 1)
        row8 = lax.broadcasted_iota(jnp.int32, (8, 128), 0)
        lane8 = lax.broadcasted_iota(jnp.int32, (8, 128), 1)
        acc = jnp.zeros((N_BUCKETS, 128), F32)
        dsink = jnp.zeros((8, 128), F32)
        for h in range(N_HEADS):
            kh, rows, cols = _head_place(h)
            dt = dtab_ref[kh, rows, cols]
            for b in range(N_BUCKETS):
                val = jnp.sum(jnp.where(bucket == b, dt, 0.0))
                acc = acc + jnp.where((row == b) & (lane == h), val, 0.0)
            dsink = dsink + jnp.where((row8 == 0) & (lane8 == h), jnp.sum(dt[:, 0:1]), 0.0)
        out_ref[...] = acc
        dsink_ref[...] = dsink

    vm = pl.BlockSpec(memory_space=pltpu.VMEM)
    return pl.pallas_call(
        body, name="bias_fold", out_shape=[SDS((N_BUCKETS, 128), F32), SDS((8, 128), F32)],
        in_specs=[vm], out_specs=[vm, vm],
    )(dtab)


def _pair_operands(prev, cur):
    t = jnp.concatenate([prev, cur], axis=0).astype(F32)
    t = jnp.where(lax.broadcasted_iota(jnp.int32, t.shape, 0) == 0, 0.0, t)
    tr = pltpu.roll(t, HEAD_DIM, 1)
    lo = lax.broadcasted_iota(jnp.int32, t.shape, 1) < HEAD_DIM
    zero = jnp.zeros_like(t)
    head0 = jnp.concatenate([jnp.where(lo, t, zero), jnp.where(lo, zero, tr)], axis=0).astype(BF16)
    head1 = jnp.concatenate([jnp.where(lo, tr, zero), jnp.where(lo, zero, t)], axis=0).astype(BF16)
    return head0, head1


def _pair_fold(d0, d1):
    lo = lax.broadcasted_iota(jnp.int32, (2 * BLK, KV_W), 1) < HEAD_DIM
    zero = jnp.zeros((2 * BLK, KV_W), F32)
    g0 = jnp.where(lo, d0[0:256], zero) + pltpu.roll(jnp.where(lo, zero, d0[256:512]), HEAD_DIM, 1)
    g1 = pltpu.roll(jnp.where(lo, d1[0:256], zero), HEAD_DIM, 1) + jnp.where(lo, zero, d1[256:512])
    return jnp.where(lax.broadcasted_iota(jnp.int32, (2 * BLK, KV_W), 0) == 0, 0.0, g0 + g1)


def _stack_pairs(ref, kh, rows=slice(None)):
    return jnp.concatenate([ref[rows, 128 * (4 * kh + j):128 * (4 * kh + j + 1)] for j in range(4)], axis=0)


def _table_spec():
    return pl.BlockSpec((1, N_KV, 4 * BLK, 4 * BLK), lambda n: (jnp.minimum(n, 1), 0, 0, 0))


def _attn_fwd(q, kv, tab):
    s = q.shape[0]

    def body(q_ref, kp_ref, k0_ref, k1_ref, vp_ref, v0_ref, v1_ref, tab0_ref, tab1_ref, att_ref, stats_ref):
        lane = lax.broadcasted_iota(jnp.int32, (BLK, 128), 1)
        for sub, (kp, kc, vp, vc, tab_ref) in enumerate([(kp_ref, k0_ref, vp_ref, v0_ref, tab0_ref),
                                                         (k0_ref, k1_ref, v0_ref, v1_ref, tab1_ref)]):
            rows = slice(BLK * sub, BLK * (sub + 1))
            k2 = _pair_operands(kp[...], kc[...])
            v2 = _pair_operands(vp[...], vc[...])
            stats = jnp.zeros((BLK, 128), F32)
            for kh in range(N_KV):
                sc = _nt(_stack_pairs(q_ref, kh, rows), k2[kh])
                ps = []
                for e in range(2):
                    lg = sc[:, 256 * e:256 * (e + 1)] + tab_ref[0, kh, :, 256 * e:256 * (e + 1)]
                    m = jnp.max(lg, axis=-1, keepdims=True)
                    ex = jnp.exp(lg - m)
                    den = jnp.sum(ex, axis=-1, keepdims=True)
                    ps.append(ex * (1.0 / den))
                    lse = m + jnp.log(den)
                    for j in range(4):
                        stats = jnp.where(lane == GROUP * kh + 2 * j + e, lse[BLK * j:BLK * (j + 1)], stats)
                out = _nn(jnp.concatenate(ps, axis=1).astype(BF16), v2[kh])
                for j in range(4):
                    att_ref[rows, 128 * (4 * kh + j):128 * (4 * kh + j + 1)] = out[BLK * j:BLK * (j + 1)].astype(BF16)
            stats_ref[rows, :] = stats

    two = lambda m: (m, 0)
    table = lambda pick: pl.BlockSpec((1, N_KV, 4 * BLK, 4 * BLK), lambda m: (pick(m), 0, 0, 0))
    return pl.pallas_call(
        body, name="attn_fwd", grid=(s // (2 * BLK),),
        in_specs=[pl.BlockSpec((2 * BLK, D), two)]
        + [pl.BlockSpec((BLK, KV_W), lambda m, col=col, off=off: (jnp.maximum(2 * m + off, 0), col))
           for col in (0, 1) for off in (-1, 0, 1)]
        + [table(lambda m: jnp.minimum(m, 1)), table(lambda m: 1)],
        out_specs=[pl.BlockSpec((2 * BLK, D), two), pl.BlockSpec((2 * BLK, 128), two)],
        out_shape=[SDS((s, D), BF16), SDS((s, 128), F32)],
        compiler_params=_params(("parallel",)),
    )(q, kv, kv, kv, kv, kv, kv, tab, tab)


def _mid(att, zb, h1, tgt, w_out, g_post, tm):
    s = att.shape[0]
    nt = s // tm

    def body(att_ref, z_ref, h1_ref, t_ref, w_ref, g_ref,
             dh_ref, dqz_ref, datt_ref, loss_ref, dg_ref, dw_ref, dw16_ref, dw_acc, stage):
        @pl.when(pl.program_id(0) == 0)
        def _():
            loss_ref[...] = jnp.zeros_like(loss_ref)
            dg_ref[...] = jnp.zeros_like(dg_ref)
            dw_acc[...] = jnp.zeros_like(dw_acc)
        att = att_ref[...].astype(F32)
        z = z_ref[...].astype(F32)
        sg, sz = _silu_parts(z)
        ob = (att * sz).astype(BF16)
        y2 = _nn(ob, w_ref[...])
        r2 = _rms_scale(y2)
        yh = y2 * r2
        g = g_ref[...]
        err = (h1_ref[...] + yh * g) - t_ref[...]
        loss_ref[...] += jnp.sum(jnp.sum(err * err, axis=-1, keepdims=True) / D)
        dh = err / D
        dh_ref[...] = dh
        _acc_row(dg_ref, 0, jnp.sum(dh * yh, axis=0, keepdims=True))
        dyh = dh * g
        dy = (r2 * (dyh - yh * jnp.mean(dyh * yh, axis=-1, keepdims=True))).astype(BF16)
        dw_acc[...] += _tn(ob, dy)
        dob = _nt(dy, w_ref[...])
        datt_ref[...] = (dob * sz).astype(BF16)
        dqz_ref[...] = (dob * att * _dsilu(z, sg)).astype(BF16)

        @pl.when(pl.program_id(0) == nt - 1)
        def _():
            _write_gradient(dw_acc, dw_ref, dw16_ref, stage)

    row = lambda i: (i, 0)
    fix = lambda i: (0, 0)
    anyspace = pl.BlockSpec(memory_space=pl.ANY)
    return pl.pallas_call(
        body, name="mid", grid=(nt,),
        in_specs=[pl.BlockSpec((tm, D), row)] * 4 + [pl.BlockSpec((D, D), fix), pl.BlockSpec((1, D), fix)],
        out_specs=[pl.BlockSpec((tm, D), row), pl.BlockSpec((tm, D), lambda i: (i, 1)), pl.BlockSpec((tm, D), row),
                   pl.BlockSpec((8, 128), fix), pl.BlockSpec((8, D), fix), anyspace, anyspace],
        out_shape=[SDS((s, D), F32), SDS((s, 2 * D), BF16), SDS((s, D), BF16), SDS((8, 128), F32),
                   SDS((8, D), F32), SDS((D, D), F32), SDS((D, D), BF16)],
        scratch_shapes=[pltpu.VMEM((D, D), F32), pltpu.VMEM((D // 4, D), BF16)],
        compiler_params=_params(("arbitrary",)),
    )(att, zb, h1, tgt, w_out, g_post)


def _attn_bwd(q, kv, datt, stats, tab, dqz):
    s = q.shape[0]
    nb = s // BLK

    def body(q_ref, kp_ref, kc_ref, vp_ref, vc_ref, da_ref, st_ref, tab_ref, dqz_in,
             dq_ref, dkv_ref, dtab_ref, dk_carry, dv_carry):
        del dqz_in
        n = pl.program_id(0)

        @pl.when(n == 0)
        def _():
            dtab_ref[...] = jnp.zeros_like(dtab_ref)
            dk_carry[...] = jnp.zeros_like(dk_carry)
            dv_carry[...] = jnp.zeros_like(dv_carry)

        @pl.when(n < nb)
        def _():
            k2 = _pair_operands(kp_ref[...], kc_ref[...])
            v2 = _pair_operands(vp_ref[...], vc_ref[...])
            lane = lax.broadcasted_iota(jnp.int32, (BLK, 128), 1)
            stats = st_ref[...]
            dk2, dv2 = [], []
            for kh in range(N_KV):
                qs = _stack_pairs(q_ref, kh)
                das = _stack_pairs(da_ref, kh)
                sc = _nt(qs, k2[kh])
                dp = _nt(das, v2[kh])
                ps, dss = [], []
                for e in range(2):
                    heads = [GROUP * kh + 2 * j + e for j in range(4)]
                    lse = jnp.concatenate([jnp.sum(jnp.where(lane == h, stats, 0.0), axis=-1, keepdims=True)
                                           for h in heads], axis=0)
                    cols = slice(256 * e, 256 * (e + 1))
                    p = jnp.exp(sc[:, cols] + tab_ref[0, kh, :, cols] - lse)
                    delta = jnp.sum(p * dp[:, cols], axis=-1, keepdims=True)
                    ds = p * (dp[:, cols] - delta)
                    dtab_ref[kh, :, cols] += ds
                    ps.append(p)
                    dss.append(ds)
                p2 = jnp.concatenate(ps, axis=1).astype(BF16)
                ds2 = jnp.concatenate(dss, axis=1).astype(BF16)
                dq = _nn(ds2, k2[kh]) * Q_SCALE
                for j in range(4):
                    dq_ref[:, 128 * (4 * kh + j):128 * (4 * kh + j + 1)] = dq[BLK * j:BLK * (j + 1)].astype(BF16)
                dk2.append(_tn(ds2, qs))
                dv2.append(_tn(p2, das))
            dkk = _pair_fold(dk2[0], dk2[1])
            dvv = _pair_fold(dv2[0], dv2[1])
            dkv_ref[:, 0:KV_W] = (dk_carry[...] + dkk[0:BLK]).astype(BF16)
            dkv_ref[:, KV_W:2 * KV_W] = (dv_carry[...] + dvv[0:BLK]).astype(BF16)
            dk_carry[...] = dkk[BLK:2 * BLK]
            dv_carry[...] = dvv[BLK:2 * BLK]

        @pl.when(n == nb)
        def _():
            dkv_ref[:, 0:KV_W] = dk_carry[...].astype(BF16)
            dkv_ref[:, KV_W:2 * KV_W] = dv_carry[...].astype(BF16)

    cur = lambda n: (jnp.minimum(n, nb - 1), 0)
    prev = lambda n: (jnp.clip(n - 1, 0, nb - 1), 0)
    return pl.pallas_call(
        body, name="attn_bwd", grid=(nb + 1,),
        in_specs=[pl.BlockSpec((BLK, D), cur),
                  pl.BlockSpec((BLK, KV_W), prev), pl.BlockSpec((BLK, KV_W), cur),
                  pl.BlockSpec((BLK, KV_W), lambda n: (jnp.clip(n - 1, 0, nb - 1), 1)),
                  pl.BlockSpec((BLK, KV_W), lambda n: (jnp.minimum(n, nb - 1), 1)),
                  pl.BlockSpec((BLK, D), cur), pl.BlockSpec((BLK, 128), cur), _table_spec(),
                  pl.BlockSpec(memory_space=pl.ANY)],
        out_specs=[pl.BlockSpec((BLK, D), cur), pl.BlockSpec((BLK, 2 * KV_W), prev),
                   pl.BlockSpec((N_KV, 4 * BLK, 4 * BLK), lambda n: (0, 0, 0))],
        out_shape=[SDS((s, 2 * D), BF16), SDS((s, 2 * KV_W), BF16), SDS((N_KV, 4 * BLK, 4 * BLK), F32)],
        scratch_shapes=[pltpu.VMEM((BLK, KV_W), F32), pltpu.VMEM((BLK, KV_W), F32)],
        input_output_aliases={8: 0},
        compiler_params=_params(("arbitrary",)),
    )(q, kv, kv, kv, kv, datt, stats, tab, dqz)


def _b_bwd(dqz, dkv, h1, dh2, oa, wbin_g, w_kv, g_kv, g_pre, g_apost, tm):
    s = h1.shape[0]
    nt = s // tm

    def body(dqz_ref, dkv_ref, h_ref, dh2_ref, oa_ref, wb_ref, wkv_ref, gk_ref, gb_ref, ga_ref,
             dh1_ref, doa_ref, dg_ref, dwb_ref, dwkv_ref, dwb16_ref, dwkv16_ref, wcat, dwb_acc, dwkv_acc):
        @pl.when(pl.program_id(0) == 0)
        def _():
            dg_ref[...] = jnp.zeros_like(dg_ref)
            dwb_acc[...] = jnp.zeros_like(dwb_acc)
            dwkv_acc[...] = jnp.zeros_like(dwkv_acc)
            for j in range(N_CHIPS):
                pltpu.sync_copy(wb_ref.at[j], wcat.at[:, pl.ds(BIN_COLS * j, BIN_COLS)])
        dnb = _nt(dqz_ref[...], wcat[...])
        dnk = _nt(dkv_ref[...], wkv_ref[...])
        h = h_ref[...]
        r = _rms_scale(h)
        hh = h * r
        dwb_acc[...] += _tn((hh * gb_ref[...]).astype(BF16), dqz_ref[...])
        dwkv_acc[...] += _tn((hh * gk_ref[...]).astype(BF16), dkv_ref[...])
        _acc_row(dg_ref, 0, jnp.sum(dnk * hh, axis=0, keepdims=True))
        _acc_row(dg_ref, 1, jnp.sum(dnb * hh, axis=0, keepdims=True))
        dhh = dnb * gb_ref[...] + dnk * gk_ref[...]
        dh1 = dh2_ref[...] + r * (dhh - hh * jnp.mean(dhh * hh, axis=-1, keepdims=True))
        dh1_ref[...] = dh1
        oa = oa_ref[...].astype(F32)
        ra = _rms_scale(oa)
        oh = oa * ra
        _acc_row(dg_ref, 2, jnp.sum(dh1 * oh, axis=0, keepdims=True))
        doh = dh1 * ga_ref[...]
        doa_ref[...] = (ra * (doh - oh * jnp.mean(doh * oh, axis=-1, keepdims=True))).astype(BF16)

        @pl.when(pl.program_id(0) == nt - 1)
        def _():
            wcat[...] = dwb_acc[...].astype(BF16)
            for j in range(N_CHIPS):
                pltpu.sync_copy(dwb_acc.at[:, pl.ds(BIN_COLS * j, BIN_COLS)], dwb_ref.at[j])
                pltpu.sync_copy(wcat.at[:, pl.ds(BIN_COLS * j, BIN_COLS)], dwb16_ref.at[j])
            pltpu.sync_copy(dwkv_acc, dwkv_ref)
            wcat[:, 0:2 * KV_W] = dwkv_acc[...].astype(BF16)
            pltpu.sync_copy(wcat.at[:, pl.ds(0, 2 * KV_W)], dwkv16_ref)

    row = lambda i: (i, 0)
    fix = lambda i: (0, 0)
    anyspace = pl.BlockSpec(memory_space=pl.ANY)
    return pl.pallas_call(
        body, name="b_bwd", grid=(nt,),
        in_specs=[pl.BlockSpec((tm, 2 * D), row), pl.BlockSpec((tm, 2 * KV_W), row), pl.BlockSpec((tm, D), row),
                  pl.BlockSpec((tm, D), row), pl.BlockSpec((tm, D), row), anyspace, pl.BlockSpec((D, 2 * KV_W), fix),
                  pl.BlockSpec((1, D), fix), pl.BlockSpec((1, D), fix), pl.BlockSpec((1, D), fix)],
        out_specs=[pl.BlockSpec((tm, D), row), pl.BlockSpec((tm, D), row), pl.BlockSpec((8, D), fix)] + [anyspace] * 4,
        out_shape=[SDS((s, D), F32), SDS((s, D), BF16), SDS((8, D), F32), SDS((N_CHIPS, D, BIN_COLS), F32),
                   SDS((D, 2 * KV_W), F32), SDS((N_CHIPS, D, BIN_COLS), BF16), SDS((D, 2 * KV_W), BF16)],
        scratch_shapes=[pltpu.VMEM((D, 2 * D), BF16), pltpu.VMEM((D, 2 * D), F32), pltpu.VMEM((D, 2 * KV_W), F32)],
        compiler_params=_params(("arbitrary",)),
    )(dqz, dkv, h1, dh2, oa, wbin_g, w_kv, g_kv, g_pre, g_apost)


def _to_owner_core(pieces, r, send, recv, core, action):
    x, y, c = lax.axis_index("x"), lax.axis_index("y"), lax.axis_index("c")
    for kp in range(N_CHIPS):
        px, py = kp >> 1, kp & 1
        rel = 4 * (x + px - 2 * x * px) + 2 * (y + py - 2 * y * py) + (c + core - 2 * c * core)

        @pl.when(rel != 0)
        def _():
            cp = pltpu.make_async_remote_copy(src_ref=pieces.at[kp], dst_ref=r.at[rel - 1], send_sem=send.at[kp],
                                              recv_sem=recv.at[rel - 1], device_id=(px, py, core), device_id_type=MESH)
            if action == "start":
                cp.start()
            else:
                cp.wait_send()
    if action == "wait":
        @pl.when(c == core)
        def _():
            for rel in range(1, N_DEV):
                pltpu.make_async_remote_copy(src_ref=pieces.at[0], dst_ref=r.at[rel - 1], send_sem=send.at[0],
                                             recv_sem=recv.at[rel - 1], device_id=(x, y, c),
                                             device_id_type=MESH).wait_recv()


def _owner_core_sems():
    return [pltpu.SemaphoreType.DMA((N_CHIPS,)), pltpu.SemaphoreType.DMA((N_DEV - 1,))]


def _device_exchange(grads, recvs, send, recv):
    x, y, c = lax.axis_index("x"), lax.axis_index("y"), lax.axis_index("c")
    copies = []
    for a, (g, r) in enumerate(zip(grads, recvs)):
        h = g.shape[1] // 2
        for rel in range(1, N_DEV):
            fx, fy, fc = rel >> 2, (rel >> 1) & 1, rel & 1
            px, py, pc = x + fx - 2 * x * fx, y + fy - 2 * y * fy, c + fc - 2 * c * fc
            sem = (N_DEV - 1) * a + rel - 1
            copies.append(pltpu.make_async_remote_copy(
                src_ref=g.at[2 * px + py, pl.ds(pl.multiple_of(pc * h, 16), h)], dst_ref=r.at[rel - 1],
                send_sem=send.at[sem], recv_sem=recv.at[sem], device_id=(px, py, pc), device_id_type=MESH))
    return copies


def _device_exchange_specs(grads):
    anyspace = pl.BlockSpec(memory_space=pl.ANY)
    n = len(grads)
    count = (N_DEV - 1) * n
    return ([anyspace] * n, [anyspace] * n,
            [SDS((N_DEV - 1, g.shape[1] // 2, g.shape[2]), g.dtype) for g in grads],
            [pltpu.SemaphoreType.DMA((count,)), pltpu.SemaphoreType.DMA((count,))])


def _a_bwd(doa, ya, conv, proj, conv_w, w_out, tm, parts):
    s = doa.shape[0]
    nt = s // tm
    n = len(parts)
    ex_in, ex_out, ex_shape, ex_sems = _device_exchange_specs(parts)

    def body(*refs):
        doa_ref, ya_ref, conv_ref, proj_ref, cw_ref, w_ref = refs[:6]
        part_refs = refs[6:6 + n]
        dproj_ref, dcw_ref, dw_ref, dw16_ref = refs[6 + n:10 + n]
        recv_refs = refs[10 + n:10 + 2 * n]
        carry, dw_acc, stage, send, recv = refs[10 + 2 * n:]
        i = pl.program_id(0)

        @pl.when(i == 0)
        def _():
            dcw_ref[...] = jnp.zeros_like(dcw_ref)
            carry[...] = jnp.zeros_like(carry)
            dw_acc[...] = jnp.zeros_like(dw_acc)
            for cp in _device_exchange(part_refs, recv_refs, send, recv):
                cp.start()
        dya = _nt(doa_ref[...], w_ref[...])
        dw_acc[...] += _tn(ya_ref[...], doa_ref[...])
        bg = proj_ref[:, 0:D].astype(F32)
        cg = proj_ref[:, D:2 * D].astype(F32)
        u = proj_ref[:, 2 * D:3 * D].astype(F32)
        z = proj_ref[:, 3 * D:4 * D].astype(F32)
        v = cg * u
        rows = lax.broadcasted_iota(jnp.int32, (tm, D), 0)
        conv = conv_ref[...].astype(F32)
        sg, sz = _silu_parts(z)
        dproj_ref[:, 0:D] = (dya * conv * sz).astype(BF16)
        dproj_ref[:, 3 * D:4 * D] = (dya * bg * conv * _dsilu(z, sg)).astype(BF16)
        dconv = dya * bg * sz
        after = carry[...]
        up1 = jnp.where(rows < tm - 1, pltpu.roll(dconv, tm - 1, 0), after[0:1, :])
        up2 = jnp.where(rows < tm - 2, pltpu.roll(dconv, tm - 2, 0),
                        jnp.where(rows == tm - 2, after[0:1, :], after[1:2, :]))
        carry[...] = dconv[0:8, :]
        _acc_row(dcw_ref, 0, jnp.sum(up2 * v, axis=0, keepdims=True))
        _acc_row(dcw_ref, 1, jnp.sum(up1 * v, axis=0, keepdims=True))
        _acc_row(dcw_ref, 2, jnp.sum(dconv * v, axis=0, keepdims=True))
        dv = cw_ref[2:3, :] * dconv + cw_ref[1:2, :] * up1 + cw_ref[0:1, :] * up2
        dproj_ref[:, D:2 * D] = (dv * u).astype(BF16)
        dproj_ref[:, 2 * D:3 * D] = (dv * cg).astype(BF16)

        @pl.when(i == nt - 1)
        def _():
            _write_gradient(dw_acc, dw_ref, dw16_ref, stage)
            for cp in _device_exchange(part_refs, recv_refs, send, recv):
                cp.wait()

    rev = lambda i: (nt - 1 - i, 0)
    fix = lambda i: (0, 0)
    anyspace = pl.BlockSpec(memory_space=pl.ANY)
    dproj, dcw, dw, dw16, *got = pl.pallas_call(
        body, name="a_bwd", grid=(nt,),
        in_specs=[pl.BlockSpec((tm, D), rev), pl.BlockSpec((tm, D), rev), pl.BlockSpec((tm, D), rev),
                  pl.BlockSpec((tm, 4 * D), rev), pl.BlockSpec((8, D), fix), pl.BlockSpec((D, D), fix)] + ex_in,
        out_specs=[pl.BlockSpec((tm, 4 * D), rev), pl.BlockSpec((8, D), fix), anyspace, anyspace] + ex_out,
        out_shape=[SDS((s, 4 * D), BF16), SDS((8, D), F32), SDS((D, D), F32), SDS((D, D), BF16)] + ex_shape,
        scratch_shapes=[pltpu.VMEM((8, D), F32), pltpu.VMEM((D, D), F32), pltpu.VMEM((D // 4, D), BF16)] + ex_sems,
        compiler_params=_params(("arbitrary",)),
    )(doa, ya, conv, proj, conv_w, w_out, *parts)
    return dproj, dcw, dw, dw16, got


def _dn1(dp_ref, w_ref):
    dn = _nt(dp_ref[:, 0:D], w_ref[0])
    for j in range(1, 4):
        dn = dn + _nt(dp_ref[:, D * j:D * (j + 1)], w_ref[j])
    return dn


def _a_in_bwd_matmul(dproj, win_g, tm, count, win_half, win_got):
    def body(dp_ref, w_ref, half_ref, got_in, dn_ref, got_ref, wcat, send, recv):
        del got_in

        @pl.when(pl.program_id(0) == 0)
        def _():
            _to_owner_core(half_ref, got_ref, send, recv, 1, "start")
            for j in range(N_CHIPS):
                pltpu.sync_copy(w_ref.at[j], wcat.at[:, pl.ds(D * j, D)])
        dn_ref[...] = _nt(dp_ref[...], wcat[...]).astype(BF16)

        @pl.when(pl.program_id(0) == count - 1)
        def _():
            _to_owner_core(half_ref, got_ref, send, recv, 1, "wait")

    row = lambda i: (i, 0)
    anyspace = pl.BlockSpec(memory_space=pl.ANY)
    return pl.pallas_call(
        body, name="a_in_bwd_matmul", grid=(count,),
        in_specs=[pl.BlockSpec((tm, 4 * D), row), anyspace, anyspace, anyspace],
        out_specs=[pl.BlockSpec((tm, D), row), anyspace],
        out_shape=[SDS((count * tm, D), BF16), SDS(win_got.shape, win_got.dtype)],
        scratch_shapes=[pltpu.VMEM((D, 4 * D), BF16)] + _owner_core_sems(),
        input_output_aliases={3: 1},
        compiler_params=_params(("arbitrary",)),
    )(dproj, win_g, win_half, win_got)


def _a_in_bwd(dn_first, dproj, x, dh1, win_g, g_pre, tm):
    s = x.shape[0]
    nt = s // tm
    count = dn_first.shape[0] // tm

    def body(dn_ref, dp_ref, x_ref, dh_ref, w_ref, g_ref, gx_ref, dg_ref, dn_s):
        i = pl.program_id(0)

        @pl.when(i == 0)
        def _():
            dg_ref[...] = jnp.zeros_like(dg_ref)

        @pl.when(i < count)
        def _():
            dn_s[...] = dn_ref[...].astype(F32)

        @pl.when(i >= count)
        def _():
            dn_s[...] = _dn1(dp_ref, w_ref)
        dn = dn_s[...]
        xv = x_ref[...]
        r = _rms_scale(xv)
        xh = xv * r
        _acc_row(dg_ref, 0, jnp.sum(dn * xh, axis=0, keepdims=True))
        dxh = dn * g_ref[...]
        gx_ref[...] = dh_ref[...] + r * (dxh - xh * jnp.mean(dxh * xh, axis=-1, keepdims=True))

    row = lambda i: (i, 0)
    fix = lambda i: (0, 0)
    return pl.pallas_call(
        body, name="a_in_bwd", grid=(nt,),
        in_specs=[pl.BlockSpec((tm, D), lambda i: (jnp.minimum(i, count - 1), 0)),
                  pl.BlockSpec((tm, 4 * D), lambda i: (jnp.maximum(i, count), 0)),
                  pl.BlockSpec((tm, D), row), pl.BlockSpec((tm, D), row),
                  pl.BlockSpec((4, D, D), lambda i: (0, 0, 0)), pl.BlockSpec((1, D), fix)],
        out_specs=[pl.BlockSpec((tm, D), row), pl.BlockSpec((8, D), fix)],
        out_shape=[SDS((s, D), F32), SDS((8, D), F32)],
        scratch_shapes=[pltpu.VMEM((tm, D), F32)],
        compiler_params=_params(("arbitrary",)),
    )(dn_first, dproj, x, dh1, win_g, g_pre)


def _dw_in_half(n1, dproj, core, tmw, name, to_owners=None, to_devices=None):
    s = n1.shape[0]
    h = D // 2
    nt = s // tmw
    sent_array = to_owners if to_owners is not None else to_devices
    rides = sent_array is not None
    if to_owners is not None:
        sems, got_shape = _owner_core_sems(), SDS((N_DEV - 1, h, D), BF16)
    elif to_devices is not None:
        _, _, (got_shape,), sems = _device_exchange_specs([to_devices])

    def body(*refs):
        a_ref, b_ref = refs[:2]
        o_ref, o16_ref = refs[2 + rides:4 + rides]
        j, t = pl.program_id(0), pl.program_id(1)

        def exchange(action):
            sent, got, send, recv = refs[2], refs[5], refs[6], refs[7]
            if to_owners is not None:
                _to_owner_core(sent, got, send, recv, 1 - core, action)
            else:
                for cp in _device_exchange([sent], [got], send, recv):
                    cp.start() if action == "start" else cp.wait()

        if rides:
            @pl.when((j == 0) & (t == 0))
            def _():
                exchange("start")

        @pl.when(t == 0)
        def _():
            o_ref[...] = jnp.zeros_like(o_ref)
        o_ref[0] += _tn(a_ref[...], b_ref[...])

        @pl.when(t == nt - 1)
        def _():
            o16_ref[...] = o_ref[...].astype(BF16)
        if rides:
            @pl.when((j == N_CHIPS - 1) & (t == nt - 1))
            def _():
                exchange("wait")

    anyspace = pl.BlockSpec(memory_space=pl.ANY)
    slot = pl.BlockSpec((1, h, D), lambda j, t: (j, 0, 0))
    return pl.pallas_call(
        body, name=name, grid=(N_CHIPS, nt),
        in_specs=[pl.BlockSpec((tmw, h), lambda j, t: (t, core)), pl.BlockSpec((tmw, D), lambda j, t: (t, j))]
        + [anyspace] * rides,
        out_specs=[slot, slot] + [anyspace] * rides,
        out_shape=[SDS((N_CHIPS, h, D), F32), SDS((N_CHIPS, h, D), BF16)] + ([got_shape] if rides else []),
        scratch_shapes=sems if rides else [],
        compiler_params=_params(("arbitrary", "arbitrary")),
    )(n1, dproj, *([sent_array] if rides else []))


def _sibling_exchange(name, to_sibling=(), shards=(), smalls=()):
    n_g, n_h, n_s = len(to_sibling), len(shards), len(smalls)

    def body(*refs):
        gs = refs[:n_g]
        pos = n_g + n_h
        small_ins = refs[pos:pos + n_s]
        pos += n_s
        rs, fs = refs[pos:pos + n_g], refs[pos + n_g:pos + n_g + n_h]
        pos += n_g + n_h
        small_alls = refs[pos:pos + n_s]
        pos += n_s
        dsend, drecv, ssend, srecv = refs[pos:]
        x, y, c = lax.axis_index("x"), lax.axis_index("y"), lax.axis_index("c")
        sibling = (x, y, 1 - c)
        sends, arrivals = [], []
        for a, (g, r) in enumerate(zip(gs, rs)):
            h = g.shape[1] // 2
            src = g.at[:, pl.ds(pl.multiple_of((1 - c) * h, 8), h), :]
            sends.append(pltpu.make_async_remote_copy(src_ref=src, dst_ref=r, send_sem=dsend.at[a], recv_sem=drecv.at[a],
                                                      device_id=sibling, device_id_type=MESH))
            arrivals.append(pltpu.make_async_remote_copy(src_ref=r, dst_ref=r, send_sem=dsend.at[a], recv_sem=drecv.at[a],
                                                         device_id=sibling, device_id_type=MESH))
        for b, full in enumerate(fs):
            h = full.shape[0] // 2
            mine = full.at[pl.ds(pl.multiple_of(c * h, 8), h)]
            theirs = full.at[pl.ds(pl.multiple_of((1 - c) * h, 8), h)]
            sends.append(pltpu.make_async_remote_copy(src_ref=mine, dst_ref=mine, send_sem=dsend.at[n_g + b],
                                                      recv_sem=drecv.at[n_g + b], device_id=sibling, device_id_type=MESH))
            arrivals.append(pltpu.make_async_remote_copy(src_ref=mine, dst_ref=theirs, send_sem=dsend.at[n_g + b],
                                                         recv_sem=drecv.at[n_g + b], device_id=sibling, device_id_type=MESH))
        me = 4 * x + 2 * y + c
        for k, (small_in, small_all) in enumerate(zip(small_ins, small_alls)):
            small_all[me] = small_in[...]
            for rel in range(1, N_DEV):
                fx, fy, fc = rel >> 2, (rel >> 1) & 1, rel & 1
                peer = (x + fx - 2 * x * fx, y + fy - 2 * y * fy, c + fc - 2 * c * fc)
                sender = 4 * peer[0] + 2 * peer[1] + peer[2]
                sem = (N_DEV - 1) * k + rel - 1
                sends.append(pltpu.make_async_remote_copy(
                    src_ref=small_in, dst_ref=small_all.at[me], send_sem=ssend.at[sem], recv_sem=srecv.at[sem],
                    device_id=peer, device_id_type=MESH))
                arrivals.append(pltpu.make_async_remote_copy(
                    src_ref=small_in, dst_ref=small_all.at[sender], send_sem=ssend.at[sem], recv_sem=srecv.at[sem],
                    device_id=peer, device_id_type=MESH))
        for cp in sends:
            cp.start()
        for cp in arrivals:
            cp.wait_recv()
        for cp in sends:
            cp.wait_send()

    anyspace = pl.BlockSpec(memory_space=pl.ANY)
    vm = pl.BlockSpec(memory_space=pltpu.VMEM)
    out_shape = [SDS((N_CHIPS, g.shape[1] // 2, g.shape[2]), F32) for g in to_sibling]
    out_shape += [SDS(full.shape, F32) for full in shards]
    out_shape += [SDS((N_DEV,) + sm.shape, F32) for sm in smalls]
    n_d2d = max(n_g + n_h, 1)
    n_all = (N_DEV - 1) * max(n_s, 1)
    outs = pl.pallas_call(
        body, name=name, out_shape=out_shape,
        in_specs=[anyspace] * (n_g + n_h) + [vm] * n_s, out_specs=[anyspace] * (n_g + n_h) + [vm] * n_s,
        scratch_shapes=[pltpu.SemaphoreType.DMA((n_d2d,)), pltpu.SemaphoreType.DMA((n_d2d,)),
                        pltpu.SemaphoreType.DMA((n_all,)), pltpu.SemaphoreType.DMA((n_all,))],
        input_output_aliases={n_g + b: n_g + b for b in range(n_h)},
    )(*to_sibling, *shards, *smalls)
    return outs[:n_g], outs[n_g:n_g + n_h], outs[n_g + n_h:]


def _add_win(where, lo, hi, r, name):
    _, h, cols = lo.shape
    tr = min(h, 256)
    nh = h // tr

    def body(where_ref, lo_ref, hi_ref, r_ref, o_ref):
        acc = jnp.where(where_ref[0] == 0, lo_ref[0], hi_ref[0])
        for k in range(N_DEV - 1):
            acc = acc + r_ref[k].astype(F32)
        o_ref[...] = acc

    own = pl.BlockSpec((1, tr, cols), lambda i, w: (w[1], i, 0))
    return pl.pallas_call(
        body, name=name,
        grid_spec=pltpu.PrefetchScalarGridSpec(
            num_scalar_prefetch=1, grid=(nh,),
            in_specs=[own, own, pl.BlockSpec((N_DEV - 1, tr, cols), lambda i, w: (0, i, 0))],
            out_specs=pl.BlockSpec((tr, cols), lambda i, w: (w[0] * nh + i, 0))),
        out_shape=SDS((2 * h, cols), F32),
        compiler_params=_params(("parallel",)),
    )(where, lo, hi, r)


def _add_devices(where, g, r, name):
    _, rows, cols = g.shape
    h = rows // 2
    tr = min(h, 256)
    nh = h // tr

    def body(where_ref, g_ref, r_ref, o_ref):
        del where_ref
        acc = g_ref[0]
        for k in range(N_DEV - 1):
            acc = acc + r_ref[k].astype(F32)
        o_ref[...] = acc

    return pl.pallas_call(
        body, name=name,
        grid_spec=pltpu.PrefetchScalarGridSpec(
            num_scalar_prefetch=1, grid=(nh,),
            in_specs=[pl.BlockSpec((1, tr, cols), lambda i, w: (w[1], w[0] * nh + i, 0)),
                      pl.BlockSpec((N_DEV - 1, tr, cols), lambda i, w: (0, i, 0))],
            out_specs=pl.BlockSpec((tr, cols), lambda i, w: (w[0] * nh + i, 0))),
        out_shape=SDS((rows, cols), F32),
        compiler_params=_params(("parallel",)),
    )(where, g, r)


def _sum_smalls(gathered):
    n = len(gathered)

    def body(*refs):
        for all_ref, o_ref in zip(refs[:n], refs[n:]):
            acc = all_ref[0]
            for dev in range(1, N_DEV):
                acc = acc + all_ref[dev]
            o_ref[...] = acc

    vm = pl.BlockSpec(memory_space=pltpu.VMEM)
    return pl.pallas_call(
        body, name="sum_smalls", out_shape=[SDS(a.shape[1:], F32) for a in gathered],
        in_specs=[vm] * n, out_specs=[vm] * n,
    )(*gathered)


def _adam_step(g, w, m, v):
    nm = ADAM_B1 * m + (1.0 - ADAM_B1) * g
    nv = ADAM_B2 * v + (1.0 - ADAM_B2) * (g * g)
    m_hat = nm / (1.0 - ADAM_B1 ** ADAM_STEP)
    v_hat = nv / (1.0 - ADAM_B2 ** ADAM_STEP)
    return -ADAM_LR * (m_hat / (jnp.sqrt(v_hat) + ADAM_EPS) + ADAM_WD * w), nm, nv


def _adamw(g, w, m, v, name):
    rows, cols = g.shape
    tr = min(rows, 256)

    def body(g_ref, w_ref, m_ref, v_ref, d_ref, nm_ref, nv_ref):
        d_ref[...], nm_ref[...], nv_ref[...] = _adam_step(g_ref[...], w_ref[...], m_ref[...], v_ref[...])

    spec = pl.BlockSpec((tr, cols), lambda i: (i, 0))
    return pl.pallas_call(
        body, name=name, grid=(rows // tr,), in_specs=[spec] * 4, out_specs=[spec] * 3,
        out_shape=[SDS(g.shape, F32)] * 3, compiler_params=_params(("parallel",)),
    )(g, w, m, v)


def _small_update(chip, tot, tot_rel, wmv):
    names = list(SMALL_PLACES)
    n = len(names)

    def body(chip_ref, tot_ref, quarter_ref, rel_ref, *refs):
        del chip_ref
        ins, outs = refs[:3 * n], refs[3 * n:]
        for i, nm in enumerate(names):
            source, row, (rows, cols) = SMALL_PLACES[nm]
            g = {"rows": tot_ref, "quarter": quarter_ref, "rel": rel_ref}[source][row:row + rows, 0:cols]
            outs[4 * i][...] = g
            outs[4 * i + 1][...], outs[4 * i + 2][...], outs[4 * i + 3][...] = _adam_step(
                g, ins[3 * i][...], ins[3 * i + 1][...], ins[3 * i + 2][...])

    whole = lambda shape: pl.BlockSpec(shape, lambda i, c: (0,) * len(shape))
    shapes = [SMALL_PLACES[nm][2] for nm in names]
    outs = pl.pallas_call(
        body, name="small_update",
        grid_spec=pltpu.PrefetchScalarGridSpec(
            num_scalar_prefetch=1, grid=(1,),
            in_specs=[whole(tot.shape), pl.BlockSpec((tot.shape[0], D // 4), lambda i, c: (0, c[0])),
                      whole(tot_rel.shape)] + [whole(shp) for shp in shapes for _ in range(3)],
            out_specs=[whole(shp) for shp in shapes for _ in range(4)]),
        out_shape=[SDS(shp, F32) for shp in shapes for _ in range(4)],
    )(chip, tot, tot, tot_rel, *[a for nm in names for a in wmv[nm]])
    return {nm: tuple(outs[4 * i:4 * i + 4]) for i, nm in enumerate(names)}


def _pad_rows(a, rows):
    return jnp.concatenate([a, jnp.zeros((rows - a.shape[0], a.shape[1]), a.dtype)], axis=0)


def _pad_cols(a, cols):
    return jnp.concatenate([a, jnp.zeros((a.shape[0], cols - a.shape[1]), a.dtype)], axis=1)


def kernel(x, a_pre_norm, a_w_in, a_conv_w, a_w_out, a_post_norm, kv_norm, w_kv, rel_bias, b_pre_norm, b_w_in, b_sinks, b_w_out, b_post_norm, loss_target, m_a_pre_norm, m_a_w_in, m_a_conv_w, m_a_w_out, m_a_post_norm, m_kv_norm, m_w_kv, m_rel_bias, m_b_pre_norm, m_b_w_in, m_b_sinks, m_b_w_out, m_b_post_norm, v_a_pre_norm, v_a_w_in, v_a_conv_w, v_a_w_out, v_a_post_norm, v_kv_norm, v_w_kv, v_rel_bias, v_b_pre_norm, v_b_w_in, v_b_sinks, v_b_w_out, v_b_post_norm):
    seq = x.shape[1]
    xs = x.reshape(seq, D)
    tgt = loss_target.reshape(seq, D)
    chip = 2 * lax.axis_index("x") + lax.axis_index("y")
    core = lax.axis_index("c")
    tm = _tile(seq, 512)
    tmw = _tile(seq, 1024)

    shards = [a_w_in[0], a_w_out[0], w_kv, b_w_in[0], b_w_out[0]]
    small_w = _pad_rows(jnp.concatenate([a_pre_norm, a_conv_w[0], a_post_norm], axis=0), 8)
    *own_only, small_g = _prepare_weights(shards, small_w)
    where = jnp.stack([core, chip]).astype(jnp.int32)
    small_full = small_g.transpose(1, 0, 2).reshape(8, D)
    g_apre, conv_w, g_apost = small_full[0:1], _pad_rows(small_full[1:4], 8), small_full[4:5]
    g_kv = kv_norm.reshape(1, D)

    proj, n1, (win_g, wouta_g, wkv_g, wbin_g, woutb_g) = _a_in(where[1:2], xs, g_apre, own_only, tmw)
    wouta = wouta_g.reshape(D, D)
    wkv = wkv_g.reshape(D, 2 * KV_W)
    woutb = woutb_g.reshape(D, D)
    ya, oa, h1, conv = _a_mix(proj, xs, conv_w, wouta, g_apost, tm)
    kv, q, zb = _b_in(h1, g_kv, b_pre_norm, wkv, wbin_g, tmw)
    tab = _bias_table(rel_bias, b_sinks.reshape(N_HEADS))
    att, stats = _attn_fwd(q, kv, tab)
    dh2, dqz, datt, loss_acc, dg_bpost, dw_outb, dw_outb16 = _mid(att, zb, h1, tgt, woutb, b_post_norm, tm)

    dqz, dkv, dtab = _attn_bwd(q, kv, datt, stats, tab, dqz)
    dh1, doa, dg_b, dw_bin, dw_kv, dw_bin16, dw_kv16 = _b_bwd(dqz, dkv, h1, dh2, oa, wbin_g, wkv, g_kv, b_pre_norm,
                                                              g_apost, tm)
    by_chip = lambda a, cols: a.reshape(N_CHIPS, D // 4, cols)
    grads1 = [by_chip(dw_kv, 2 * KV_W), dw_bin, by_chip(dw_outb, D)]
    sent1 = [by_chip(dw_kv16, 2 * KV_W), dw_bin16, by_chip(dw_outb16, D)]
    names1 = ["w_kv", "b_w_in", "b_w_out"]
    dproj, dconv_w, dw_outa, dw_outa16, from_devices1 = _a_bwd(doa, ya, conv, proj, conv_w, wouta, tm, sent1)
    shards1 = [_add_devices(where, g, r, "add_devices_" + nm) for g, r, nm in zip(grads1, from_devices1, names1)]
    tmw2 = _tile(seq, 4096)
    win_lo, win_lo16, outa_got = _dw_in_half(n1, dproj, 0, tmw2, "dw_a_in_lo", to_devices=by_chip(dw_outa16, D))
    win_hi, win_hi16, win_got = _dw_in_half(n1, dproj, 1, tmw2, "dw_a_in_hi", to_owners=win_lo16)
    nt = seq // tmw
    dn_first, win_got = _a_in_bwd_matmul(dproj, win_g, tmw, max(nt - max(nt // 4, 1), 1), win_hi16, win_got)
    grad_x, dg_apre = _a_in_bwd(dn_first, dproj, xs, dh1, win_g, g_apre, tm)
    shards2 = [_add_win(where, win_lo, win_hi, win_got, "add_devices_a_w_in"),
               _add_devices(where, by_chip(dw_outa, D), outa_got, "add_devices_a_w_out")]
    drel, dsink = _bias_fold(dtab)

    smalls = jnp.concatenate([
        dg_apre[0:1], dg_b[2:3], dg_b[0:1], dg_b[1:2], dg_bpost[0:1], _pad_cols(dsink[0:1], D),
        _pad_cols(loss_acc[0:1], D), jnp.zeros((1, D), F32), dconv_w], axis=0)
    assert smalls.shape == (SMALL_ROWS, D)
    _, (g_wkv, g_wbin, g_woutb, g_win, g_wouta), gathered = _sibling_exchange(
        "share_last", shards=shards1 + shards2, smalls=(smalls, drel))
    tot, tot_rel = _sum_smalls(gathered)

    big = {}
    for nm, g, w, m, v in [("a_w_in", g_win, a_w_in, m_a_w_in, v_a_w_in), ("a_w_out", g_wouta, a_w_out, m_a_w_out, v_a_w_out),
                           ("w_kv", g_wkv, w_kv, m_w_kv, v_w_kv), ("b_w_in", g_wbin, b_w_in, m_b_w_in, v_b_w_in),
                           ("b_w_out", g_woutb, b_w_out, m_b_w_out, v_b_w_out)]:
        shp = w.shape
        two = (shp[-2], shp[-1])
        d, nm_, nv_ = _adamw(g, w.reshape(two), m.reshape(two), v.reshape(two), "adamw_" + nm)
        big[nm] = (g.reshape(shp), d.reshape(shp), nm_.reshape(shp), nv_.reshape(shp))

    given = {"a_pre_norm": (a_pre_norm, m_a_pre_norm, v_a_pre_norm), "a_conv_w": (a_conv_w, m_a_conv_w, v_a_conv_w),
             "a_post_norm": (a_post_norm, m_a_post_norm, v_a_post_norm), "kv_norm": (kv_norm, m_kv_norm, v_kv_norm),
             "rel_bias": (rel_bias, m_rel_bias, v_rel_bias), "b_pre_norm": (b_pre_norm, m_b_pre_norm, v_b_pre_norm),
             "b_sinks": (b_sinks, m_b_sinks, v_b_sinks), "b_post_norm": (b_post_norm, m_b_post_norm, v_b_post_norm)}
    small = _small_update(where[1:2], tot, tot_rel, {nm: tuple(a.reshape(SMALL_PLACES[nm][2]) for a in wmv)
                                            for nm, wmv in given.items()})
    order = ["a_pre_norm", "a_w_in", "a_conv_w", "a_w_out", "a_post_norm", "kv_norm", "w_kv", "rel_bias",
             "b_pre_norm", "b_w_in", "b_sinks", "b_w_out", "b_post_norm"]
    outs = []
    for which in range(4):
        for nm in order:
            outs.append(big[nm][which] if nm in big else small[nm][which].reshape(given[nm][0].shape))
    loss = 0.5 * tot[LOSS_ROW, 0]
    return (loss, grad_x.reshape(x.shape), *outs)
```

```python
import math

import jax
import jax.numpy as jnp
from jax import lax
from jax.experimental import pallas as pl
from jax.experimental.pallas import tpu as pltpu

F32 = jnp.float32
BF16 = jnp.bfloat16
MESH = pl.DeviceIdType.MESH
SDS = jax.ShapeDtypeStruct

D = 1024
HEAD_DIM = 64
N_HEADS = 16
N_KV = 2
GROUP = 8
KV_W = 128
BLK = 128
N_BUCKETS = 32
MAX_EXACT = 16
MAX_DISTANCE = 128
EPS = 1e-6
NEG_INF = -1e30
Q_SCALE = HEAD_DIM ** -0.5

ADAM_LR = 0.001
ADAM_B1 = 0.9
ADAM_B2 = 0.999
ADAM_EPS = 1e-08
ADAM_WD = 0.01
ADAM_STEP = 10

N_CHIPS = 4
N_DEV = 8
BIN_COLS = 2 * D // N_CHIPS
VMEM_LIMIT = 56 * 1024 * 1024
SMALL_ROWS = 16
LOSS_ROW = 6
SMALL_PLACES = {
    "a_pre_norm": ("quarter", 0, (1, D // 4)), "a_conv_w": ("quarter", 8, (3, D // 4)),
    "a_post_norm": ("quarter", 1, (1, D // 4)), "kv_norm": ("rows", 2, (1, D)),
    "rel_bias": ("rel", 0, (N_BUCKETS, N_HEADS)), "b_pre_norm": ("rows", 3, (1, D)),
    "b_sinks": ("rows", 5, (1, N_HEADS)), "b_post_norm": ("rows", 4, (1, D)),
}


def _bucket_thresholds():
    def bucket(d):
        big = MAX_EXACT + int(math.log(d / MAX_EXACT) / math.log(MAX_DISTANCE / MAX_EXACT)
                              * (N_BUCKETS - MAX_EXACT))
        return d if d < MAX_EXACT else min(big, N_BUCKETS - 1)
    out = []
    for b in range(MAX_EXACT + 1, N_BUCKETS):
        out.append(min(d for d in range(MAX_EXACT, MAX_DISTANCE) if bucket(d) >= b))
    return tuple(out)


BUCKET_THRESHOLDS = _bucket_thresholds()


def _params(semantics=None, vmem=VMEM_LIMIT):
    return pltpu.CompilerParams(dimension_semantics=semantics, vmem_limit_bytes=vmem)


def _tile(n, pref):
    return pref if n >= 2 * pref else max(n // 2, 8)


def _rms_scale(v):
    return lax.rsqrt(jnp.mean(v * v, axis=-1, keepdims=True) + EPS)


def _nt(a, b):
    return lax.dot_general(a, b, (((1,), (1,)), ((), ())), preferred_element_type=F32)


def _tn(a, b):
    return lax.dot_general(a, b, (((0,), (0,)), ((), ())), preferred_element_type=F32)


def _nn(a, b):
    return jnp.dot(a, b, preferred_element_type=F32)


def _silu_parts(z):
    sg = jax.nn.sigmoid(z)
    return sg, z * sg


def _dsilu(z, sg):
    return sg * (1.0 + z * (1.0 - sg))


def _write_gradient(acc, out32, out16, stage):
    pltpu.sync_copy(acc, out32)
    rows = stage.shape[0]
    for k in range(acc.shape[0] // rows):
        stage[...] = acc[rows * k:rows * (k + 1), :].astype(BF16)
        pltpu.sync_copy(stage, out16.at[pl.ds(rows * k, rows)])


def _acc_row(ref, row, val):
    ref[row:row + 1, :] += val


def _gather_copies(outs, splits, ici_send, ici_recv, d2d_send, d2d_recv):
    x, y, c = lax.axis_index("x"), lax.axis_index("y"), lax.axis_index("c")
    k = 2 * x + y
    sibling = (x, y, 1 - c)

    def part(o_ref, chip, core, split):
        if not split:
            return o_ref.at[chip]
        h = o_ref.shape[1] // 2
        return o_ref.at[chip, pl.ds(pl.multiple_of(core * h, 16), h)]

    def remote(ref, a, j, sems, to):
        return pltpu.make_async_remote_copy(src_ref=ref, dst_ref=ref, send_sem=sems[0].at[3 * a + j],
                                            recv_sem=sems[1].at[3 * a + j], device_id=to, device_id_type=MESH)

    copies = []
    for a, (o_ref, split) in enumerate(zip(outs, splits)):
        for j, (px, py) in enumerate([(x, 1 - y), (1 - x, y), (1 - x, 1 - y)]):
            kj = 2 * px + py
            ici, d2d = (ici_send, ici_recv), (d2d_send, d2d_recv)
            copies.append((remote(part(o_ref, k, c, split), a, j, ici, (px, py, c)),
                           remote(part(o_ref, kj, c, split), a, j, ici, (px, py, c)),
                           remote(part(o_ref, kj, c, split), a, j, d2d, sibling) if split else None,
                           remote(part(o_ref, kj, 1 - c, split), a, j, d2d, sibling) if split else None))
    return copies


def _gather_sems(n):
    return [pltpu.SemaphoreType.DMA((3 * n,)) for _ in range(4)]


def _prepare_weights(shards, small):
    n = len(shards)

    def body(*refs):
        ins, small_in = refs[:n], refs[n]
        outs, small_out = refs[n + 1:2 * n + 1], refs[2 * n + 1]
        stages, put_sem = refs[2 * n + 2:3 * n + 2], refs[3 * n + 2]
        sems = refs[3 * n + 3:]
        k = 2 * lax.axis_index("x") + lax.axis_index("y")
        puts = []
        for a, (i_ref, stage, o_ref) in enumerate(zip(ins, stages, outs)):
            stage[...] = i_ref[...].astype(BF16)
            puts.append(pltpu.make_async_copy(stage, o_ref.at[k], put_sem.at[a]))
            puts[-1].start()
        small_out[k] = small_in[...]
        copies = _gather_copies([small_out], [False], *sems)
        for send, _, _, _ in copies:
            send.start()
        for _, arrival, _, _ in copies:
            arrival.wait_recv()
        for send, _, _, _ in copies:
            send.wait_send()
        for put in puts:
            put.wait()

    vm = pl.BlockSpec(memory_space=pltpu.VMEM)
    anyspace = pl.BlockSpec(memory_space=pl.ANY)
    out_shape = [SDS((N_CHIPS,) + s.shape, BF16) for s in shards] + [SDS((N_CHIPS,) + small.shape, F32)]
    return pl.pallas_call(
        body, name="prepare_weights", out_shape=out_shape,
        in_specs=[vm] * (n + 1), out_specs=[anyspace] * n + [vm],
        scratch_shapes=[pltpu.VMEM(s.shape, BF16) for s in shards] + [pltpu.SemaphoreType.DMA((n,))] + _gather_sems(1),
        compiler_params=pltpu.CompilerParams(vmem_limit_bytes=VMEM_LIMIT),
    )(*shards, small)


def _a_in(chip, x, g_pre, weights, tm):
    s = x.shape[0]
    nt = s // tm
    n = len(weights)

    def body(chip_ref, x_ref, g_ref, *refs):
        proj_ref, n1_ref = refs[n:n + 2]
        gathered = refs[n + 2:2 * n + 2]
        wbuf, n1_all, fetch_sem = refs[2 * n + 2:2 * n + 5]
        sems = refs[2 * n + 5:]
        jj, i = pl.program_id(0), pl.program_id(1)
        copies = _gather_copies(gathered, [True] * n, *sems)

        def fetch(rel):
            slot = jnp.bitwise_xor(chip_ref[0], rel)
            return pltpu.make_async_copy(gathered[0].at[slot], wbuf.at[rel % 2], fetch_sem.at[rel % 2])

        @pl.when((jj == 0) & (i == 0))
        def _():
            fetch(0).start()
            copies[0][0].start()
            copies[1][0].start()
            fetch(0).wait()

        for rel in (1, 2, 3):
            @pl.when((jj == rel) & (i == 0))
            def _():
                fetch(rel).wait()

        @pl.when(jj == 0)
        def _():
            xv = x_ref[...]
            n1 = (xv * _rms_scale(xv) * g_ref[...]).astype(BF16)
            n1_ref[...] = n1
            n1_all[i] = n1
        proj_ref[...] = _nn(n1_all[i], wbuf[jj % 2]).astype(BF16)

        for rel in (1, 2, 3):
            @pl.when((jj == rel - 1) & (i == max(nt - 2, nt // 2)))
            def _():
                _, arrival, forward, forwarded = copies[rel - 1]
                arrival.wait_recv()
                forward.start()
                forwarded.wait_recv()
                fetch(rel).start()
                if rel == 1:
                    for send, _, _, _ in copies[2:]:
                        send.start()

        @pl.when((jj == 3) & (i == max(nt - 2, 0)))
        def _():
            for _, arrival, forward, _ in copies[3:]:
                arrival.wait_recv()
                forward.start()

        @pl.when((jj == 3) & (i == nt - 1))
        def _():
            for _, _, _, forwarded in copies[3:]:
                forwarded.wait_recv()
            for send, _, forward, _ in copies:
                forward.wait_send()
                send.wait_send()

    anyspace = pl.BlockSpec(memory_space=pl.ANY)
    proj, n1, *gathered = pl.pallas_call(
        body, name="a_in",
        grid_spec=pltpu.PrefetchScalarGridSpec(
            num_scalar_prefetch=1, grid=(4, nt),
            in_specs=[pl.BlockSpec((tm, D), lambda jj, i, c: (jnp.where(jj == 0, i, nt - 1), 0)),
                      pl.BlockSpec((1, D), lambda jj, i, c: (0, 0))] + [anyspace] * n,
            out_specs=[pl.BlockSpec((tm, D), lambda jj, i, c: (i, jnp.bitwise_xor(c[0], jj))),
                       pl.BlockSpec((tm, D), lambda jj, i, c: (jnp.where(jj == 0, i, nt - 1), 0))] + [anyspace] * n,
            scratch_shapes=[pltpu.VMEM((2, D, D), BF16), pltpu.VMEM((nt, tm, D), BF16),
                            pltpu.SemaphoreType.DMA((2,))] + _gather_sems(n)),
        out_shape=[SDS((s, 4 * D), BF16), SDS((s, D), BF16)] + [SDS(w.shape, w.dtype) for w in weights],
        input_output_aliases={3 + a: 2 + a for a in range(n)},
        compiler_params=_params(("arbitrary", "arbitrary")),
    )(chip, x, g_pre, *weights)
    return proj, n1, gathered


def _shift_rows(v, last, second_last, rows):
    v1 = jnp.where(rows >= 1, pltpu.roll(v, 1, 0), last)
    v2 = jnp.where(rows >= 2, pltpu.roll(v, 2, 0), jnp.where(rows == 1, last, second_last))
    return v1, v2


def _a_mix(proj, x, conv_w, w_out, g_post, tm):
    s = x.shape[0]

    def body(proj_ref, x_ref, cw_ref, w_ref, g_ref, ya_ref, oa_ref, h1_ref, conv_ref, carry):
        @pl.when(pl.program_id(0) == 0)
        def _():
            carry[...] = jnp.zeros_like(carry)
        v = proj_ref[:, D:2 * D].astype(F32) * proj_ref[:, 2 * D:3 * D].astype(F32)
        rows = lax.broadcasted_iota(jnp.int32, (tm, D), 0)
        before = carry[...]
        v1, v2 = _shift_rows(v, before[7:8, :], before[6:7, :], rows)
        carry[...] = v[tm - 8:tm, :]
        conv = cw_ref[0:1, :] * v2 + cw_ref[1:2, :] * v1 + cw_ref[2:3, :] * v
        conv_ref[...] = conv.astype(BF16)
        _, sz = _silu_parts(proj_ref[:, 3 * D:4 * D].astype(F32))
        ya = (proj_ref[:, 0:D].astype(F32) * conv * sz).astype(BF16)
        ya_ref[...] = ya
        oa = _nn(ya, w_ref[...])
        oa_ref[...] = oa.astype(BF16)
        h1_ref[...] = x_ref[...] + oa * _rms_scale(oa) * g_ref[...]

    row = lambda i: (i, 0)
    fix = lambda i: (0, 0)
    return pl.pallas_call(
        body, name="a_mix", grid=(s // tm,),
        in_specs=[pl.BlockSpec((tm, 4 * D), row), pl.BlockSpec((tm, D), row), pl.BlockSpec((8, D), fix),
                  pl.BlockSpec((D, D), fix), pl.BlockSpec((1, D), fix)],
        out_specs=[pl.BlockSpec((tm, D), row)] * 4,
        out_shape=[SDS((s, D), BF16), SDS((s, D), BF16), SDS((s, D), F32), SDS((s, D), BF16)],
        scratch_shapes=[pltpu.VMEM((8, D), F32)],
        compiler_params=_params(("arbitrary",)),
    )(proj, x, conv_w, w_out, g_post)


def _b_in(h1, g_kv, g_pre, w_kv, wbin_g, tm):
    s = h1.shape[0]

    def body(h_ref, gk_ref, gb_ref, wkv_ref, wb_ref, kv_ref, q_ref, z_ref):
        h = h_ref[...]
        hh = h * _rms_scale(h)
        nk = (hh * gk_ref[...]).astype(BF16)
        nb = (hh * gb_ref[...]).astype(BF16)
        kv_ref[...] = _nn(nk, wkv_ref[...]).astype(BF16)
        for j in range(2):
            q_ref[:, BIN_COLS * j:BIN_COLS * (j + 1)] = (_nn(nb, wb_ref[j]) * Q_SCALE).astype(BF16)
            z_ref[:, BIN_COLS * j:BIN_COLS * (j + 1)] = _nn(nb, wb_ref[2 + j]).astype(BF16)

    row = lambda i: (i, 0)
    fix = lambda i: (0, 0)
    return pl.pallas_call(
        body, name="b_in", grid=(s // tm,),
        in_specs=[pl.BlockSpec((tm, D), row), pl.BlockSpec((1, D), fix), pl.BlockSpec((1, D), fix),
                  pl.BlockSpec((D, 2 * KV_W), fix), pl.BlockSpec((N_CHIPS, D, BIN_COLS), lambda i: (0, 0, 0))],
        out_specs=[pl.BlockSpec((tm, 2 * KV_W), row), pl.BlockSpec((tm, D), row), pl.BlockSpec((tm, D), row)],
        out_shape=[SDS((s, 2 * KV_W), BF16), SDS((s, D), BF16), SDS((s, D), BF16)],
        compiler_params=_params(("parallel",)),
    )(h1, g_kv, g_pre, w_kv, wbin_g)


def _band_buckets():
    q = lax.broadcasted_iota(jnp.int32, (BLK, 2 * BLK), 0)
    k = lax.broadcasted_iota(jnp.int32, (BLK, 2 * BLK), 1)
    dist = q + BLK - k
    bucket = jnp.where(dist < MAX_EXACT, dist, MAX_EXACT)
    for t in BUCKET_THRESHOLDS:
        bucket = bucket + jnp.where(dist >= t, 1, 0)
    in_window = (dist >= 0) & (dist < BLK)
    return jnp.where(in_window, bucket, -1)


def _head_place(h):
    kh, j, e = h // GROUP, (h % GROUP) // 2, h % 2
    return kh, slice(BLK * j, BLK * (j + 1)), slice(2 * BLK * e, 2 * BLK * (e + 1))


def _bias_table(rel_bias, sinks):
    def body(rb_ref, sink_ref, tab_ref):
        bucket = _band_buckets()
        col = lax.broadcasted_iota(jnp.int32, (BLK, 2 * BLK), 1)
        for h in range(N_HEADS):
            acc = jnp.where(bucket < 0, NEG_INF, 0.0).astype(F32)
            for b in range(N_BUCKETS):
                acc = jnp.where(bucket == b, rb_ref[b, h], acc)
            acc = jnp.where(col == 0, sink_ref[h], acc)
            kh, rows, cols = _head_place(h)
            tab_ref[1, kh, rows, cols] = acc
            tab_ref[0, kh, rows, cols] = jnp.where((col > 0) & (col < BLK), NEG_INF, acc)

    return pl.pallas_call(
        body, name="bias_table", out_shape=SDS((2, N_KV, 4 * BLK, 4 * BLK), F32),
        in_specs=[pl.BlockSpec(memory_space=pltpu.SMEM), pl.BlockSpec(memory_space=pltpu.SMEM)],
        out_specs=pl.BlockSpec(memory_space=pltpu.VMEM),
    )(rel_bias, sinks)


def _bias_fold(dtab):
    def body(dtab_ref, out_ref, dsink_ref):
        bucket = _band_buckets()
        row = lax.broadcasted_iota(jnp.int32, (N_BUCKETS, 128), 0)
        lane = lax.broadcasted_iota(jnp.int32, (N_BUCKETS, 128), 1)
        row8 = lax.broadcasted_iota(jnp.int32, (8, 128), 0)
        lane8 = lax.broadcasted_iota(jnp.int32, (8, 128), 1)
        acc = jnp.zeros((N_BUCKETS, 128), F32)
        dsink = jnp.zeros((8, 128), F32)
        for h in range(N_HEADS):
            kh, rows, cols = _head_place(h)
            dt = dtab_ref[kh, rows, cols]
            for b in range(N_BUCKETS):
                val = jnp.sum(jnp.where(bucket == b, dt, 0.0))
                acc = acc + jnp.where((row == b) & (lane == h), val, 0.0)
            dsink = dsink + jnp.where((row8 == 0) & (lane8 == h), jnp.sum(dt[:, 0:1]), 0.0)
        out_ref[...] = acc
        dsink_ref[...] = dsink

    vm = pl.BlockSpec(memory_space=pltpu.VMEM)
    return pl.pallas_call(
        body, name="bias_fold", out_shape=[SDS((N_BUCKETS, 128), F32), SDS((8, 128), F32)],
        in_specs=[vm], out_specs=[vm, vm],
    )(dtab)


def _pair_operands(prev, cur):
    t = jnp.concatenate([prev, cur], axis=0).astype(F32)
    t = jnp.where(lax.broadcasted_iota(jnp.int32, t.shape, 0) == 0, 0.0, t)
    tr = pltpu.roll(t, HEAD_DIM, 1)
    lo = lax.broadcasted_iota(jnp.int32, t.shape, 1) < HEAD_DIM
    zero = jnp.zeros_like(t)
    head0 = jnp.concatenate([jnp.where(lo, t, zero), jnp.where(lo, zero, tr)], axis=0).astype(BF16)
    head1 = jnp.concatenate([jnp.where(lo, tr, zero), jnp.where(lo, zero, t)], axis=0).astype(BF16)
    return head0, head1


def _pair_fold(d0, d1):
    lo = lax.broadcasted_iota(jnp.int32, (2 * BLK, KV_W), 1) < HEAD_DIM
    zero = jnp.zeros((2 * BLK, KV_W), F32)
    g0 = jnp.where(lo, d0[0:256], zero) + pltpu.roll(jnp.where(lo, zero, d0[256:512]), HEAD_DIM, 1)
    g1 = pltpu.roll(jnp.where(lo, d1[0:256], zero), HEAD_DIM, 1) + jnp.where(lo, zero, d1[256:512])
    return jnp.where(lax.broadcasted_iota(jnp.int32, (2 * BLK, KV_W), 0) == 0, 0.0, g0 + g1)


def _stack_pairs(ref, kh, rows=slice(None)):
    return jnp.concatenate([ref[rows, 128 * (4 * kh + j):128 * (4 * kh + j + 1)] for j in range(4)], axis=0)


def _table_spec():
    return pl.BlockSpec((1, N_KV, 4 * BLK, 4 * BLK), lambda n: (jnp.minimum(n, 1), 0, 0, 0))


def _attn_fwd(q, kv, tab):
    s = q.shape[0]

    def body(q_ref, kp_ref, k0_ref, k1_ref, vp_ref, v0_ref, v1_ref, tab0_ref, tab1_ref, att_ref, stats_ref):
        lane = lax.broadcasted_iota(jnp.int32, (BLK, 128), 1)
        for sub, (kp, kc, vp, vc, tab_ref) in enumerate([(kp_ref, k0_ref, vp_ref, v0_ref, tab0_ref),
                                                         (k0_ref, k1_ref, v0_ref, v1_ref, tab1_ref)]):
            rows = slice(BLK * sub, BLK * (sub + 1))
            k2 = _pair_operands(kp[...], kc[...])
            v2 = _pair_operands(vp[...], vc[...])
            stats = jnp.zeros((BLK, 128), F32)
            for kh in range(N_KV):
                sc = _nt(_stack_pairs(q_ref, kh, rows), k2[kh])
                ps = []
                for e in range(2):
                    lg = sc[:, 256 * e:256 * (e + 1)] + tab_ref[0, kh, :, 256 * e:256 * (e + 1)]
                    m = jnp.max(lg, axis=-1, keepdims=True)
                    ex = jnp.exp(lg - m)
                    den = jnp.sum(ex, axis=-1, keepdims=True)
                    ps.append(ex * (1.0 / den))
                    lse = m + jnp.log(den)
                    for j in range(4):
                        stats = jnp.where(lane == GROUP * kh + 2 * j + e, lse[BLK * j:BLK * (j + 1)], stats)
                out = _nn(jnp.concatenate(ps, axis=1).astype(BF16), v2[kh])
                for j in range(4):
                    att_ref[rows, 128 * (4 * kh + j):128 * (4 * kh + j + 1)] = out[BLK * j:BLK * (j + 1)].astype(BF16)
            stats_ref[rows, :] = stats

    two = lambda m: (m, 0)
    table = lambda pick: pl.BlockSpec((1, N_KV, 4 * BLK, 4 * BLK), lambda m: (pick(m), 0, 0, 0))
    return pl.pallas_call(
        body, name="attn_fwd", grid=(s // (2 * BLK),),
        in_specs=[pl.BlockSpec((2 * BLK, D), two)]
        + [pl.BlockSpec((BLK, KV_W), lambda m, col=col, off=off: (jnp.maximum(2 * m + off, 0), col))
           for col in (0, 1) for off in (-1, 0, 1)]
        + [table(lambda m: jnp.minimum(m, 1)), table(lambda m: 1)],
        out_specs=[pl.BlockSpec((2 * BLK, D), two), pl.BlockSpec((2 * BLK, 128), two)],
        out_shape=[SDS((s, D), BF16), SDS((s, 128), F32)],
        compiler_params=_params(("parallel",)),
    )(q, kv, kv, kv, kv, kv, kv, tab, tab)


def _mid(att, zb, h1, tgt, w_out, g_post, tm):
    s = att.shape[0]
    nt = s // tm

    def body(att_ref, z_ref, h1_ref, t_ref, w_ref, g_ref,
             dh_ref, dqz_ref, datt_ref, loss_ref, dg_ref, dw_ref, dw16_ref, dw_acc, stage):
        @pl.when(pl.program_id(0) == 0)
        def _():
            loss_ref[...] = jnp.zeros_like(loss_ref)
            dg_ref[...] = jnp.zeros_like(dg_ref)
            dw_acc[...] = jnp.zeros_like(dw_acc)
        att = att_ref[...].astype(F32)
        z = z_ref[...].astype(F32)
        sg, sz = _silu_parts(z)
        ob = (att * sz).astype(BF16)
        y2 = _nn(ob, w_ref[...])
        r2 = _rms_scale(y2)
        yh = y2 * r2
        g = g_ref[...]
        err = (h1_ref[...] + yh * g) - t_ref[...]
        loss_ref[...] += jnp.sum(jnp.sum(err * err, axis=-1, keepdims=True) / D)
        dh = err / D
        dh_ref[...] = dh
        _acc_row(dg_ref, 0, jnp.sum(dh * yh, axis=0, keepdims=True))
        dyh = dh * g
        dy = (r2 * (dyh - yh * jnp.mean(dyh * yh, axis=-1, keepdims=True))).astype(BF16)
        dw_acc[...] += _tn(ob, dy)
        dob = _nt(dy, w_ref[...])
        datt_ref[...] = (dob * sz).astype(BF16)
        dqz_ref[...] = (dob * att * _dsilu(z, sg)).astype(BF16)

        @pl.when(pl.program_id(0) == nt - 1)
        def _():
            _write_gradient(dw_acc, dw_ref, dw16_ref, stage)

    row = lambda i: (i, 0)
    fix = lambda i: (0, 0)
    anyspace = pl.BlockSpec(memory_space=pl.ANY)
    return pl.pallas_call(
        body, name="mid", grid=(nt,),
        in_specs=[pl.BlockSpec((tm, D), row)] * 4 + [pl.BlockSpec((D, D), fix), pl.BlockSpec((1, D), fix)],
        out_specs=[pl.BlockSpec((tm, D), row), pl.BlockSpec((tm, D), lambda i: (i, 1)), pl.BlockSpec((tm, D), row),
                   pl.BlockSpec((8, 128), fix), pl.BlockSpec((8, D), fix), anyspace, anyspace],
        out_shape=[SDS((s, D), F32), SDS((s, 2 * D), BF16), SDS((s, D), BF16), SDS((8, 128), F32),
                   SDS((8, D), F32), SDS((D, D), F32), SDS((D, D), BF16)],
        scratch_shapes=[pltpu.VMEM((D, D), F32), pltpu.VMEM((D // 4, D), BF16)],
        compiler_params=_params(("arbitrary",)),
    )(att, zb, h1, tgt, w_out, g_post)


def _attn_bwd(q, kv, datt, stats, tab, dqz):
    s = q.shape[0]
    nb = s // BLK

    def body(q_ref, kp_ref, kc_ref, vp_ref, vc_ref, da_ref, st_ref, tab_ref, dqz_in,
             dq_ref, dkv_ref, dtab_ref, dk_carry, dv_carry):
        del dqz_in
        n = pl.program_id(0)

        @pl.when(n == 0)
        def _():
            dtab_ref[...] = jnp.zeros_like(dtab_ref)
            dk_carry[...] = jnp.zeros_like(dk_carry)
            dv_carry[...] = jnp.zeros_like(dv_carry)

        @pl.when(n < nb)
        def _():
            k2 = _pair_operands(kp_ref[...], kc_ref[...])
            v2 = _pair_operands(vp_ref[...], vc_ref[...])
            lane = lax.broadcasted_iota(jnp.int32, (BLK, 128), 1)
            stats = st_ref[...]
            dk2, dv2 = [], []
            for kh in range(N_KV):
                qs = _stack_pairs(q_ref, kh)
                das = _stack_pairs(da_ref, kh)
                sc = _nt(qs, k2[kh])
                dp = _nt(das, v2[kh])
                ps, dss = [], []
                for e in range(2):
                    heads = [GROUP * kh + 2 * j + e for j in range(4)]
                    lse = jnp.concatenate([jnp.sum(jnp.where(lane == h, stats, 0.0), axis=-1, keepdims=True)
                                           for h in heads], axis=0)
                    cols = slice(256 * e, 256 * (e + 1))
                    p = jnp.exp(sc[:, cols] + tab_ref[0, kh, :, cols] - lse)
                    delta = jnp.sum(p * dp[:, cols], axis=-1, keepdims=True)
                    ds = p * (dp[:, cols] - delta)
                    dtab_ref[kh, :, cols] += ds
                    ps.append(p)
                    dss.append(ds)
                p2 = jnp.concatenate(ps, axis=1).astype(BF16)
                ds2 = jnp.concatenate(dss, axis=1).astype(BF16)
                dq = _nn(ds2, k2[kh]) * Q_SCALE
                for j in range(4):
                    dq_ref[:, 128 * (4 * kh + j):128 * (4 * kh + j + 1)] = dq[BLK * j:BLK * (j + 1)].astype(BF16)
                dk2.append(_tn(ds2, qs))
                dv2.append(_tn(p2, das))
            dkk = _pair_fold(dk2[0], dk2[1])
            dvv = _pair_fold(dv2[0], dv2[1])
            dkv_ref[:, 0:KV_W] = (dk_carry[...] + dkk[0:BLK]).astype(BF16)
            dkv_ref[:, KV_W:2 * KV_W] = (dv_carry[...] + dvv[0:BLK]).astype(BF16)
            dk_carry[...] = dkk[BLK:2 * BLK]
            dv_carry[...] = dvv[BLK:2 * BLK]

        @pl.when(n == nb)
        def _():
            dkv_ref[:, 0:KV_W] = dk_carry[...].astype(BF16)
            dkv_ref[:, KV_W:2 * KV_W] = dv_carry[...].astype(BF16)

    cur = lambda n: (jnp.minimum(n, nb - 1), 0)
    prev = lambda n: (jnp.clip(n - 1, 0, nb - 1), 0)
    return pl.pallas_call(
        body, name="attn_bwd", grid=(nb + 1,),
        in_specs=[pl.BlockSpec((BLK, D), cur),
                  pl.BlockSpec((BLK, KV_W), prev), pl.BlockSpec((BLK, KV_W), cur),
                  pl.BlockSpec((BLK, KV_W), lambda n: (jnp.clip(n - 1, 0, nb - 1), 1)),
                  pl.BlockSpec((BLK, KV_W), lambda n: (jnp.minimum(n, nb - 1), 1)),
                  pl.BlockSpec((BLK, D), cur), pl.BlockSpec((BLK, 128), cur), _table_spec(),
                  pl.BlockSpec(memory_space=pl.ANY)],
        out_specs=[pl.BlockSpec((BLK, D), cur), pl.BlockSpec((BLK, 2 * KV_W), prev),
                   pl.BlockSpec((N_KV, 4 * BLK, 4 * BLK), lambda n: (0, 0, 0))],
        out_shape=[SDS((s, 2 * D), BF16), SDS((s, 2 * KV_W), BF16), SDS((N_KV, 4 * BLK, 4 * BLK), F32)],
        scratch_shapes=[pltpu.VMEM((BLK, KV_W), F32), pltpu.VMEM((BLK, KV_W), F32)],
        input_output_aliases={8: 0},
        compiler_params=_params(("arbitrary",)),
    )(q, kv, kv, kv, kv, datt, stats, tab, dqz)


def _b_bwd(dqz, dkv, h1, dh2, oa, wbin_g, w_kv, g_kv, g_pre, g_apost, tm):
    s = h1.shape[0]
    nt = s // tm

    def body(dqz_ref, dkv_ref, h_ref, dh2_ref, oa_ref, wb_ref, wkv_ref, gk_ref, gb_ref, ga_ref,
             dh1_ref, doa_ref, dg_ref, dwb_ref, dwkv_ref, dwb16_ref, dwkv16_ref, wcat, dwb_acc, dwkv_acc):
        @pl.when(pl.program_id(0) == 0)
        def _():
            dg_ref[...] = jnp.zeros_like(dg_ref)
            dwb_acc[...] = jnp.zeros_like(dwb_acc)
            dwkv_acc[...] = jnp.zeros_like(dwkv_acc)
            for j in range(N_CHIPS):
                pltpu.sync_copy(wb_ref.at[j], wcat.at[:, pl.ds(BIN_COLS * j, BIN_COLS)])
        dnb = _nt(dqz_ref[...], wcat[...])
        dnk = _nt(dkv_ref[...], wkv_ref[...])
        h = h_ref[...]
        r = _rms_scale(h)
        hh = h * r
        dwb_acc[...] += _tn((hh * gb_ref[...]).astype(BF16), dqz_ref[...])
        dwkv_acc[...] += _tn((hh * gk_ref[...]).astype(BF16), dkv_ref[...])
        _acc_row(dg_ref, 0, jnp.sum(dnk * hh, axis=0, keepdims=True))
        _acc_row(dg_ref, 1, jnp.sum(dnb * hh, axis=0, keepdims=True))
        dhh = dnb * gb_ref[...] + dnk * gk_ref[...]
        dh1 = dh2_ref[...] + r * (dhh - hh * jnp.mean(dhh * hh, axis=-1, keepdims=True))
        dh1_ref[...] = dh1
        oa = oa_ref[...].astype(F32)
        ra = _rms_scale(oa)
        oh = oa * ra
        _acc_row(dg_ref, 2, jnp.sum(dh1 * oh, axis=0, keepdims=True))
        doh = dh1 * ga_ref[...]
        doa_ref[...] = (ra * (doh - oh * jnp.mean(doh * oh, axis=-1, keepdims=True))).astype(BF16)

        @pl.when(pl.program_id(0) == nt - 1)
        def _():
            wcat[...] = dwb_acc[...].astype(BF16)
            for j in range(N_CHIPS):
                pltpu.sync_copy(dwb_acc.at[:, pl.ds(BIN_COLS * j, BIN_COLS)], dwb_ref.at[j])
                pltpu.sync_copy(wcat.at[:, pl.ds(BIN_COLS * j, BIN_COLS)], dwb16_ref.at[j])
            pltpu.sync_copy(dwkv_acc, dwkv_ref)
            wcat[:, 0:2 * KV_W] = dwkv_acc[...].astype(BF16)
            pltpu.sync_copy(wcat.at[:, pl.ds(0, 2 * KV_W)], dwkv16_ref)

    row = lambda i: (i, 0)
    fix = lambda i: (0, 0)
    anyspace = pl.BlockSpec(memory_space=pl.ANY)
    return pl.pallas_call(
        body, name="b_bwd", grid=(nt,),
        in_specs=[pl.BlockSpec((tm, 2 * D), row), pl.BlockSpec((tm, 2 * KV_W), row), pl.BlockSpec((tm, D), row),
                  pl.BlockSpec((tm, D), row), pl.BlockSpec((tm, D), row), anyspace, pl.BlockSpec((D, 2 * KV_W), fix),
                  pl.BlockSpec((1, D), fix), pl.BlockSpec((1, D), fix), pl.BlockSpec((1, D), fix)],
        out_specs=[pl.BlockSpec((tm, D), row), pl.BlockSpec((tm, D), row), pl.BlockSpec((8, D), fix)] + [anyspace] * 4,
        out_shape=[SDS((s, D), F32), SDS((s, D), BF16), SDS((8, D), F32), SDS((N_CHIPS, D, BIN_COLS), F32),
                   SDS((D, 2 * KV_W), F32), SDS((N_CHIPS, D, BIN_COLS), BF16), SDS((D, 2 * KV_W), BF16)],
        scratch_shapes=[pltpu.VMEM((D, 2 * D), BF16), pltpu.VMEM((D, 2 * D), F32), pltpu.VMEM((D, 2 * KV_W), F32)],
        compiler_params=_params(("arbitrary",)),
    )(dqz, dkv, h1, dh2, oa, wbin_g, w_kv, g_kv, g_pre, g_apost)


def _to_owner_core(pieces, r, send, recv, core, action):
    x, y, c = lax.axis_index("x"), lax.axis_index("y"), lax.axis_index("c")
    for kp in range(N_CHIPS):
        px, py = kp >> 1, kp & 1
        rel = 4 * (x + px - 2 * x * px) + 2 * (y + py - 2 * y * py) + (c + core - 2 * c * core)

        @pl.when(rel != 0)
        def _():
            cp = pltpu.make_async_remote_copy(src_ref=pieces.at[kp], dst_ref=r.at[rel - 1], send_sem=send.at[kp],
                                              recv_sem=recv.at[rel - 1], device_id=(px, py, core), device_id_type=MESH)
            if action == "start":
                cp.start()
            else:
                cp.wait_send()
    if action == "wait":
        @pl.when(c == core)
        def _():
            for rel in range(1, N_DEV):
                pltpu.make_async_remote_copy(src_ref=pieces.at[0], dst_ref=r.at[rel - 1], send_sem=send.at[0],
                                             recv_sem=recv.at[rel - 1], device_id=(x, y, c),
                                             device_id_type=MESH).wait_recv()


def _owner_core_sems():
    return [pltpu.SemaphoreType.DMA((N_CHIPS,)), pltpu.SemaphoreType.DMA((N_DEV - 1,))]


def _device_exchange(grads, recvs, send, recv):
    x, y, c = lax.axis_index("x"), lax.axis_index("y"), lax.axis_index("c")
    copies = []
    for a, (g, r) in enumerate(zip(grads, recvs)):
        h = g.shape[1] // 2
        for rel in range(1, N_DEV):
            fx, fy, fc = rel >> 2, (rel >> 1) & 1, rel & 1
            px, py, pc = x + fx - 2 * x * fx, y + fy - 2 * y * fy, c + fc - 2 * c * fc
            sem = (N_DEV - 1) * a + rel - 1
            copies.append(pltpu.make_async_remote_copy(
                src_ref=g.at[2 * px + py, pl.ds(pl.multiple_of(pc * h, 16), h)], dst_ref=r.at[rel - 1],
                send_sem=send.at[sem], recv_sem=recv.at[sem], device_id=(px, py, pc), device_id_type=MESH))
    return copies


def _device_exchange_specs(grads):
    anyspace = pl.BlockSpec(memory_space=pl.ANY)
    n = len(grads)
    count = (N_DEV - 1) * n
    return ([anyspace] * n, [anyspace] * n,
            [SDS((N_DEV - 1, g.shape[1] // 2, g.shape[2]), g.dtype) for g in grads],
            [pltpu.SemaphoreType.DMA((count,)), pltpu.SemaphoreType.DMA((count,))])


def _a_bwd(doa, ya, conv, proj, conv_w, w_out, tm, parts):
    s = doa.shape[0]
    nt = s // tm
    n = len(parts)
    ex_in, ex_out, ex_shape, ex_sems = _device_exchange_specs(parts)

    def body(*refs):
        doa_ref, ya_ref, conv_ref, proj_ref, cw_ref, w_ref = refs[:6]
        part_refs = refs[6:6 + n]
        dproj_ref, dcw_ref, dw_ref, dw16_ref = refs[6 + n:10 + n]
        recv_refs = refs[10 + n:10 + 2 * n]
        carry, dw_acc, stage, send, recv = refs[10 + 2 * n:]
        i = pl.program_id(0)

        @pl.when(i == 0)
        def _():
            dcw_ref[...] = jnp.zeros_like(dcw_ref)
            carry[...] = jnp.zeros_like(carry)
            dw_acc[...] = jnp.zeros_like(dw_acc)
            for cp in _device_exchange(part_refs, recv_refs, send, recv):
                cp.start()
        dya = _nt(doa_ref[...], w_ref[...])
        dw_acc[...] += _tn(ya_ref[...], doa_ref[...])
        bg = proj_ref[:, 0:D].astype(F32)
        cg = proj_ref[:, D:2 * D].astype(F32)
        u = proj_ref[:, 2 * D:3 * D].astype(F32)
        z = proj_ref[:, 3 * D:4 * D].astype(F32)
        v = cg * u
        rows = lax.broadcasted_iota(jnp.int32, (tm, D), 0)
        conv = conv_ref[...].astype(F32)
        sg, sz = _silu_parts(z)
        dproj_ref[:, 0:D] = (dya * conv * sz).astype(BF16)
        dproj_ref[:, 3 * D:4 * D] = (dya * bg * conv * _dsilu(z, sg)).astype(BF16)
        dconv = dya * bg * sz
        after = carry[...]
        up1 = jnp.where(rows < tm - 1, pltpu.roll(dconv, tm - 1, 0), after[0:1, :])
        up2 = jnp.where(rows < tm - 2, pltpu.roll(dconv, tm - 2, 0),
                        jnp.where(rows == tm - 2, after[0:1, :], after[1:2, :]))
        carry[...] = dconv[0:8, :]
        _acc_row(dcw_ref, 0, jnp.sum(up2 * v, axis=0, keepdims=True))
        _acc_row(dcw_ref, 1, jnp.sum(up1 * v, axis=0, keepdims=True))
        _acc_row(dcw_ref, 2, jnp.sum(dconv * v, axis=0, keepdims=True))
        dv = cw_ref[2:3, :] * dconv + cw_ref[1:2, :] * up1 + cw_ref[0:1, :] * up2
        dproj_ref[:, D:2 * D] = (dv * u).astype(BF16)
        dproj_ref[:, 2 * D:3 * D] = (dv * cg).astype(BF16)

        @pl.when(i == nt - 1)
        def _():
            _write_gradient(dw_acc, dw_ref, dw16_ref, stage)
            for cp in _device_exchange(part_refs, recv_refs, send, recv):
                cp.wait()

    rev = lambda i: (nt - 1 - i, 0)
    fix = lambda i: (0, 0)
    anyspace = pl.BlockSpec(memory_space=pl.ANY)
    dproj, dcw, dw, dw16, *got = pl.pallas_call(
        body, name="a_bwd", grid=(nt,),
        in_specs=[pl.BlockSpec((tm, D), rev), pl.BlockSpec((tm, D), rev), pl.BlockSpec((tm, D), rev),
                  pl.BlockSpec((tm, 4 * D), rev), pl.BlockSpec((8, D), fix), pl.BlockSpec((D, D), fix)] + ex_in,
        out_specs=[pl.BlockSpec((tm, 4 * D), rev), pl.BlockSpec((8, D), fix), anyspace, anyspace] + ex_out,
        out_shape=[SDS((s, 4 * D), BF16), SDS((8, D), F32), SDS((D, D), F32), SDS((D, D), BF16)] + ex_shape,
        scratch_shapes=[pltpu.VMEM((8, D), F32), pltpu.VMEM((D, D), F32), pltpu.VMEM((D // 4, D), BF16)] + ex_sems,
        compiler_params=_params(("arbitrary",)),
    )(doa, ya, conv, proj, conv_w, w_out, *parts)
    return dproj, dcw, dw, dw16, got


def _dn1(dp_ref, w_ref):
    dn = _nt(dp_ref[:, 0:D], w_ref[0])
    for j in range(1, 4):
        dn = dn + _nt(dp_ref[:, D * j:D * (j + 1)], w_ref[j])
    return dn


def _a_in_bwd_matmul(dproj, win_g, tm, count, win_half, win_got):
    def body(dp_ref, w_ref, half_ref, got_in, dn_ref, got_ref, wcat, send, recv):
        del got_in

        @pl.when(pl.program_id(0) == 0)
        def _():
            _to_owner_core(half_ref, got_ref, send, recv, 1, "start")
            for j in range(N_CHIPS):
                pltpu.sync_copy(w_ref.at[j], wcat.at[:, pl.ds(D * j, D)])
        dn_ref[...] = _nt(dp_ref[...], wcat[...]).astype(BF16)

        @pl.when(pl.program_id(0) == count - 1)
        def _():
            _to_owner_core(half_ref, got_ref, send, recv, 1, "wait")

    row = lambda i: (i, 0)
    anyspace = pl.BlockSpec(memory_space=pl.ANY)
    return pl.pallas_call(
        body, name="a_in_bwd_matmul", grid=(count,),
        in_specs=[pl.BlockSpec((tm, 4 * D), row), anyspace, anyspace, anyspace],
        out_specs=[pl.BlockSpec((tm, D), row), anyspace],
        out_shape=[SDS((count * tm, D), BF16), SDS(win_got.shape, win_got.dtype)],
        scratch_shapes=[pltpu.VMEM((D, 4 * D), BF16)] + _owner_core_sems(),
        input_output_aliases={3: 1},
        compiler_params=_params(("arbitrary",)),
    )(dproj, win_g, win_half, win_got)


def _a_in_bwd(dn_first, dproj, x, dh1, win_g, g_pre, tm):
    s = x.shape[0]
    nt = s // tm
    count = dn_first.shape[0] // tm

    def body(dn_ref, dp_ref, x_ref, dh_ref, w_ref, g_ref, gx_ref, dg_ref, dn_s):
        i = pl.program_id(0)

        @pl.when(i == 0)
        def _():
            dg_ref[...] = jnp.zeros_like(dg_ref)

        @pl.when(i < count)
        def _():
            dn_s[...] = dn_ref[...].astype(F32)

        @pl.when(i >= count)
        def _():
            dn_s[...] = _dn1(dp_ref, w_ref)
        dn = dn_s[...]
        xv = x_ref[...]
        r = _rms_scale(xv)
        xh = xv * r
        _acc_row(dg_ref, 0, jnp.sum(dn * xh, axis=0, keepdims=True))
        dxh = dn * g_ref[...]
        gx_ref[...] = dh_ref[...] + r * (dxh - xh * jnp.mean(dxh * xh, axis=-1, keepdims=True))

    row = lambda i: (i, 0)
    fix = lambda i: (0, 0)
    return pl.pallas_call(
        body, name="a_in_bwd", grid=(nt,),
        in_specs=[pl.BlockSpec((tm, D), lambda i: (jnp.minimum(i, count - 1), 0)),
                  pl.BlockSpec((tm, 4 * D), lambda i: (jnp.maximum(i, count), 0)),
                  pl.BlockSpec((tm, D), row), pl.BlockSpec((tm, D), row),
                  pl.BlockSpec((4, D, D), lambda i: (0, 0, 0)), pl.BlockSpec((1, D), fix)],
        out_specs=[pl.BlockSpec((tm, D), row), pl.BlockSpec((8, D), fix)],
        out_shape=[SDS((s, D), F32), SDS((8, D), F32)],
        scratch_shapes=[pltpu.VMEM((tm, D), F32)],
        compiler_params=_params(("arbitrary",)),
    )(dn_first, dproj, x, dh1, win_g, g_pre)


def _dw_in_half(n1, dproj, core, tmw, name, to_owners=None, to_devices=None):
    s = n1.shape[0]
    h = D // 2
    nt = s // tmw
    sent_array = to_owners if to_owners is not None else to_devices
    rides = sent_array is not None
    if to_owners is not None:
        sems, got_shape = _owner_core_sems(), SDS((N_DEV - 1, h, D), BF16)
    elif to_devices is not None:
        _, _, (got_shape,), sems = _device_exchange_specs([to_devices])

    def body(*refs):
        a_ref, b_ref = refs[:2]
        o_ref, o16_ref = refs[2 + rides:4 + rides]
        j, t = pl.program_id(0), pl.program_id(1)

        def exchange(action):
            sent, got, send, recv = refs[2], refs[5], refs[6], refs[7]
            if to_owners is not None:
                _to_owner_core(sent, got, send, recv, 1 - core, action)
            else:
                for cp in _device_exchange([sent], [got], send, recv):
                    cp.start() if action == "start" else cp.wait()

        if rides:
            @pl.when((j == 0) & (t == 0))
            def _():
                exchange("start")

        @pl.when(t == 0)
        def _():
            o_ref[...] = jnp.zeros_like(o_ref)
        o_ref[0] += _tn(a_ref[...], b_ref[...])

        @pl.when(t == nt - 1)
        def _():
            o16_ref[...] = o_ref[...].astype(BF16)
        if rides:
            @pl.when((j == N_CHIPS - 1) & (t == nt - 1))
            def _():
                exchange("wait")

    anyspace = pl.BlockSpec(memory_space=pl.ANY)
    slot = pl.BlockSpec((1, h, D), lambda j, t: (j, 0, 0))
    return pl.pallas_call(
        body, name=name, grid=(N_CHIPS, nt),
        in_specs=[pl.BlockSpec((tmw, h), lambda j, t: (t, core)), pl.BlockSpec((tmw, D), lambda j, t: (t, j))]
        + [anyspace] * rides,
        out_specs=[slot, slot] + [anyspace] * rides,
        out_shape=[SDS((N_CHIPS, h, D), F32), SDS((N_CHIPS, h, D), BF16)] + ([got_shape] if rides else []),
        scratch_shapes=sems if rides else [],
        compiler_params=_params(("arbitrary", "arbitrary")),
    )(n1, dproj, *([sent_array] if rides else []))


def _share_and_gather(shards, smalls):
    n_h, n_s = len(shards), len(smalls)

    def body(*refs):
        small_ins = refs[n_h:n_h + n_s]
        fs = refs[n_h + n_s:2 * n_h + n_s]
        small_alls = refs[2 * n_h + n_s:2 * n_h + 2 * n_s]
        dsend, drecv, ssend, srecv = refs[2 * n_h + 2 * n_s:]
        x, y, c = lax.axis_index("x"), lax.axis_index("y"), lax.axis_index("c")
        sibling = (x, y, 1 - c)
        sends, arrivals = [], []
        for b, full in enumerate(fs):
            h = full.shape[0] // 2
            mine = full.at[pl.ds(pl.multiple_of(c * h, 8), h)]
            theirs = full.at[pl.ds(pl.multiple_of((1 - c) * h, 8), h)]
            sends.append(pltpu.make_async_remote_copy(src_ref=mine, dst_ref=mine, send_sem=dsend.at[b],
                                                      recv_sem=drecv.at[b], device_id=sibling, device_id_type=MESH))
            arrivals.append(pltpu.make_async_remote_copy(src_ref=mine, dst_ref=theirs, send_sem=dsend.at[b],
                                                         recv_sem=drecv.at[b], device_id=sibling, device_id_type=MESH))
        me = 4 * x + 2 * y + c
        for k, (small_in, small_all) in enumerate(zip(small_ins, small_alls)):
            small_all[me] = small_in[...]
            for rel in range(1, N_DEV):
                fx, fy, fc = rel >> 2, (rel >> 1) & 1, rel & 1
                peer = (x + fx - 2 * x * fx, y + fy - 2 * y * fy, c + fc - 2 * c * fc)
                sender = 4 * peer[0] + 2 * peer[1] + peer[2]
                sem = (N_DEV - 1) * k + rel - 1
                sends.append(pltpu.make_async_remote_copy(
                    src_ref=small_in, dst_ref=small_all.at[me], send_sem=ssend.at[sem], recv_sem=srecv.at[sem],
                    device_id=peer, device_id_type=MESH))
                arrivals.append(pltpu.make_async_remote_copy(
                    src_ref=small_in, dst_ref=small_all.at[sender], send_sem=ssend.at[sem], recv_sem=srecv.at[sem],
                    device_id=peer, device_id_type=MESH))
        for cp in sends:
            cp.start()
        for cp in arrivals:
            cp.wait_recv()
        for cp in sends:
            cp.wait_send()

    anyspace = pl.BlockSpec(memory_space=pl.ANY)
    vm = pl.BlockSpec(memory_space=pltpu.VMEM)
    out_shape = [SDS(full.shape, F32) for full in shards] + [SDS((N_DEV,) + sm.shape, F32) for sm in smalls]
    n_all = (N_DEV - 1) * n_s
    outs = pl.pallas_call(
        body, name="share_and_gather", out_shape=out_shape,
        in_specs=[anyspace] * n_h + [vm] * n_s, out_specs=[anyspace] * n_h + [vm] * n_s,
        scratch_shapes=[pltpu.SemaphoreType.DMA((n_h,)), pltpu.SemaphoreType.DMA((n_h,)),
                        pltpu.SemaphoreType.DMA((n_all,)), pltpu.SemaphoreType.DMA((n_all,))],
        input_output_aliases={b: b for b in range(n_h)},
    )(*shards, *smalls)
    return outs[:n_h], outs[n_h:]


def _add_win(where, lo, hi, r, name):
    _, h, cols = lo.shape
    tr = min(h, 256)
    nh = h // tr

    def body(where_ref, lo_ref, hi_ref, r_ref, o_ref):
        acc = jnp.where(where_ref[0] == 0, lo_ref[0], hi_ref[0])
        for k in range(N_DEV - 1):
            acc = acc + r_ref[k].astype(F32)
        o_ref[...] = acc

    own = pl.BlockSpec((1, tr, cols), lambda i, w: (w[1], i, 0))
    return pl.pallas_call(
        body, name=name,
        grid_spec=pltpu.PrefetchScalarGridSpec(
            num_scalar_prefetch=1, grid=(nh,),
            in_specs=[own, own, pl.BlockSpec((N_DEV - 1, tr, cols), lambda i, w: (0, i, 0))],
            out_specs=pl.BlockSpec((tr, cols), lambda i, w: (w[0] * nh + i, 0))),
        out_shape=SDS((2 * h, cols), F32),
        compiler_params=_params(("parallel",)),
    )(where, lo, hi, r)


def _add_devices(where, g, r, name):
    _, rows, cols = g.shape
    h = rows // 2
    tr = min(h, 256)
    nh = h // tr

    def body(where_ref, g_ref, r_ref, o_ref):
        del where_ref
        acc = g_ref[0]
        for k in range(N_DEV - 1):
            acc = acc + r_ref[k].astype(F32)
        o_ref[...] = acc

    return pl.pallas_call(
        body, name=name,
        grid_spec=pltpu.PrefetchScalarGridSpec(
            num_scalar_prefetch=1, grid=(nh,),
            in_specs=[pl.BlockSpec((1, tr, cols), lambda i, w: (w[1], w[0] * nh + i, 0)),
                      pl.BlockSpec((N_DEV - 1, tr, cols), lambda i, w: (0, i, 0))],
            out_specs=pl.BlockSpec((tr, cols), lambda i, w: (w[0] * nh + i, 0))),
        out_shape=SDS((rows, cols), F32),
        compiler_params=_params(("parallel",)),
    )(where, g, r)


def _sum_smalls(gathered):
    n = len(gathered)

    def body(*refs):
        for all_ref, o_ref in zip(refs[:n], refs[n:]):
            acc = all_ref[0]
            for dev in range(1, N_DEV):
                acc = acc + all_ref[dev]
            o_ref[...] = acc

    vm = pl.BlockSpec(memory_space=pltpu.VMEM)
    return pl.pallas_call(
        body, name="sum_smalls", out_shape=[SDS(a.shape[1:], F32) for a in gathered],
        in_specs=[vm] * n, out_specs=[vm] * n,
    )(*gathered)


def _adam_step(g, w, m, v):
    nm = ADAM_B1 * m + (1.0 - ADAM_B1) * g
    nv = ADAM_B2 * v + (1.0 - ADAM_B2) * (g * g)
    m_hat = nm / (1.0 - ADAM_B1 ** ADAM_STEP)
    v_hat = nv / (1.0 - ADAM_B2 ** ADAM_STEP)
    return -ADAM_LR * (m_hat / (jnp.sqrt(v_hat) + ADAM_EPS) + ADAM_WD * w), nm, nv


def _adamw(g, w, m, v, name):
    rows, cols = g.shape
    tr = min(rows, 256)

    def body(g_ref, w_ref, m_ref, v_ref, d_ref, nm_ref, nv_ref):
        d_ref[...], nm_ref[...], nv_ref[...] = _adam_step(g_ref[...], w_ref[...], m_ref[...], v_ref[...])

    spec = pl.BlockSpec((tr, cols), lambda i: (i, 0))
    return pl.pallas_call(
        body, name=name, grid=(rows // tr,), in_specs=[spec] * 4, out_specs=[spec] * 3,
        out_shape=[SDS(g.shape, F32)] * 3, compiler_params=_params(("parallel",)),
    )(g, w, m, v)


def _small_update(chip, tot, tot_rel, wmv):
    names = list(SMALL_PLACES)
    n = len(names)

    def body(chip_ref, tot_ref, quarter_ref, rel_ref, *refs):
        del chip_ref
        ins, outs = refs[:3 * n], refs[3 * n:]
        for i, nm in enumerate(names):
            source, row, (rows, cols) = SMALL_PLACES[nm]
            g = {"rows": tot_ref, "quarter": quarter_ref, "rel": rel_ref}[source][row:row + rows, 0:cols]
            outs[4 * i][...] = g
            outs[4 * i + 1][...], outs[4 * i + 2][...], outs[4 * i + 3][...] = _adam_step(
                g, ins[3 * i][...], ins[3 * i + 1][...], ins[3 * i + 2][...])

    whole = lambda shape: pl.BlockSpec(shape, lambda i, c: (0,) * len(shape))
    shapes = [SMALL_PLACES[nm][2] for nm in names]
    outs = pl.pallas_call(
        body, name="small_update",
        grid_spec=pltpu.PrefetchScalarGridSpec(
            num_scalar_prefetch=1, grid=(1,),
            in_specs=[whole(tot.shape), pl.BlockSpec((tot.shape[0], D // 4), lambda i, c: (0, c[0])),
                      whole(tot_rel.shape)] + [whole(shp) for shp in shapes for _ in range(3)],
            out_specs=[whole(shp) for shp in shapes for _ in range(4)]),
        out_shape=[SDS(shp, F32) for shp in shapes for _ in range(4)],
    )(chip, tot, tot, tot_rel, *[a for nm in names for a in wmv[nm]])
    return {nm: tuple(outs[4 * i:4 * i + 4]) for i, nm in enumerate(names)}


def _pad_rows(a, rows):
    return jnp.concatenate([a, jnp.zeros((rows - a.shape[0], a.shape[1]), a.dtype)], axis=0)


def _pad_cols(a, cols):
    return jnp.concatenate([a, jnp.zeros((a.shape[0], cols - a.shape[1]), a.dtype)], axis=1)


def kernel(x, a_pre_norm, a_w_in, a_conv_w, a_w_out, a_post_norm, kv_norm, w_kv, rel_bias, b_pre_norm, b_w_in, b_sinks, b_w_out, b_post_norm, loss_target, m_a_pre_norm, m_a_w_in, m_a_conv_w, m_a_w_out, m_a_post_norm, m_kv_norm, m_w_kv, m_rel_bias, m_b_pre_norm, m_b_w_in, m_b_sinks, m_b_w_out, m_b_post_norm, v_a_pre_norm, v_a_w_in, v_a_conv_w, v_a_w_out, v_a_post_norm, v_kv_norm, v_w_kv, v_rel_bias, v_b_pre_norm, v_b_w_in, v_b_sinks, v_b_w_out, v_b_post_norm):
    seq = x.shape[1]
    xs = x.reshape(seq, D)
    tgt = loss_target.reshape(seq, D)
    chip = 2 * lax.axis_index("x") + lax.axis_index("y")
    core = lax.axis_index("c")
    tm = _tile(seq, 512)
    tmw = _tile(seq, 1024)

    shards = [a_w_in[0], a_w_out[0], w_kv, b_w_in[0], b_w_out[0]]
    small_w = _pad_rows(jnp.concatenate([a_pre_norm, a_conv_w[0], a_post_norm], axis=0), 8)
    *own_only, small_g = _prepare_weights(shards, small_w)
    where = jnp.stack([core, chip]).astype(jnp.int32)
    small_full = small_g.transpose(1, 0, 2).reshape(8, D)
    g_apre, conv_w, g_apost = small_full[0:1], _pad_rows(small_full[1:4], 8), small_full[4:5]
    g_kv = kv_norm.reshape(1, D)

    proj, n1, (win_g, wouta_g, wkv_g, wbin_g, woutb_g) = _a_in(where[1:2], xs, g_apre, own_only, tmw)
    wouta = wouta_g.reshape(D, D)
    wkv = wkv_g.reshape(D, 2 * KV_W)
    woutb = woutb_g.reshape(D, D)
    ya, oa, h1, conv = _a_mix(proj, xs, conv_w, wouta, g_apost, tm)
    kv, q, zb = _b_in(h1, g_kv, b_pre_norm, wkv, wbin_g, tmw)
    tab = _bias_table(rel_bias, b_sinks.reshape(N_HEADS))
    att, stats = _attn_fwd(q, kv, tab)
    dh2, dqz, datt, loss_acc, dg_bpost, dw_outb, dw_outb16 = _mid(att, zb, h1, tgt, woutb, b_post_norm, tm)

    dqz, dkv, dtab = _attn_bwd(q, kv, datt, stats, tab, dqz)
    dh1, doa, dg_b, dw_bin, dw_kv, dw_bin16, dw_kv16 = _b_bwd(dqz, dkv, h1, dh2, oa, wbin_g, wkv, g_kv, b_pre_norm,
                                                              g_apost, tm)
    by_chip = lambda a, cols: a.reshape(N_CHIPS, D // 4, cols)
    grads1 = [by_chip(dw_kv, 2 * KV_W), dw_bin, by_chip(dw_outb, D)]
    sent1 = [by_chip(dw_kv16, 2 * KV_W), dw_bin16, by_chip(dw_outb16, D)]
    names1 = ["w_kv", "b_w_in", "b_w_out"]
    dproj, dconv_w, dw_outa, dw_outa16, from_devices1 = _a_bwd(doa, ya, conv, proj, conv_w, wouta, tm, sent1)
    shards1 = [_add_devices(where, g, r, "add_devices_" + nm) for g, r, nm in zip(grads1, from_devices1, names1)]
    tmw2 = _tile(seq, 4096)
    win_lo, win_lo16, outa_got = _dw_in_half(n1, dproj, 0, tmw2, "dw_a_in_lo", to_devices=by_chip(dw_outa16, D))
    win_hi, win_hi16, win_got = _dw_in_half(n1, dproj, 1, tmw2, "dw_a_in_hi", to_owners=win_lo16)
    nt = seq // tmw
    dn_first, win_got = _a_in_bwd_matmul(dproj, win_g, tmw, max(nt - max(nt // 4, 1), 1), win_hi16, win_got)
    grad_x, dg_apre = _a_in_bwd(dn_first, dproj, xs, dh1, win_g, g_apre, tm)
    shards2 = [_add_win(where, win_lo, win_hi, win_got, "add_devices_a_w_in"),
               _add_devices(where, by_chip(dw_outa, D), outa_got, "add_devices_a_w_out")]
    drel, dsink = _bias_fold(dtab)

    smalls = jnp.concatenate([
        dg_apre[0:1], dg_b[2:3], dg_b[0:1], dg_b[1:2], dg_bpost[0:1], _pad_cols(dsink[0:1], D),
        _pad_cols(loss_acc[0:1], D), jnp.zeros((1, D), F32), dconv_w], axis=0)
    assert smalls.shape == (SMALL_ROWS, D)
    (g_wkv, g_wbin, g_woutb, g_win, g_wouta), gathered = _share_and_gather(shards1 + shards2, (smalls, drel))
    tot, tot_rel = _sum_smalls(gathered)

    big = {}
    for nm, g, w, m, v in [("a_w_in", g_win, a_w_in, m_a_w_in, v_a_w_in), ("a_w_out", g_wouta, a_w_out, m_a_w_out, v_a_w_out),
                           ("w_kv", g_wkv, w_kv, m_w_kv, v_w_kv), ("b_w_in", g_wbin, b_w_in, m_b_w_in, v_b_w_in),
                           ("b_w_out", g_woutb, b_w_out, m_b_w_out, v_b_w_out)]:
        shp = w.shape
        two = (shp[-2], shp[-1])
        d, nm_, nv_ = _adamw(g, w.reshape(two), m.reshape(two), v.reshape(two), "adamw_" + nm)
        big[nm] = (g.reshape(shp), d.reshape(shp), nm_.reshape(shp), nv_.reshape(shp))

    given = {"a_pre_norm": (a_pre_norm, m_a_pre_norm, v_a_pre_norm), "a_conv_w": (a_conv_w, m_a_conv_w, v_a_conv_w),
             "a_post_norm": (a_post_norm, m_a_post_norm, v_a_post_norm), "kv_norm": (kv_norm, m_kv_norm, v_kv_norm),
             "rel_bias": (rel_bias, m_rel_bias, v_rel_bias), "b_pre_norm": (b_pre_norm, m_b_pre_norm, v_b_pre_norm),
             "b_sinks": (b_sinks, m_b_sinks, v_b_sinks), "b_post_norm": (b_post_norm, m_b_post_norm, v_b_post_norm)}
    small = _small_update(where[1:2], tot, tot_rel, {nm: tuple(a.reshape(SMALL_PLACES[nm][2]) for a in wmv)
                                            for nm, wmv in given.items()})
    order = ["a_pre_norm", "a_w_in", "a_conv_w", "a_w_out", "a_post_norm", "kv_norm", "w_kv", "rel_bias",
             "b_pre_norm", "b_w_in", "b_sinks", "b_w_out", "b_post_norm"]
    outs = []
    for which in range(4):
        for nm in order:
            outs.append(big[nm][which] if nm in big else small[nm][which].reshape(given[nm][0].shape))
    loss = 0.5 * tot[LOSS_ROW, 0]
    return (loss, grad_x.reshape(x.shape), *outs)
```

```python
import math

import jax
import jax.numpy as jnp
from jax import lax
from jax.experimental import pallas as pl
from jax.experimental.pallas import tpu as pltpu

F32 = jnp.float32
BF16 = jnp.bfloat16
MESH = pl.DeviceIdType.MESH
SDS = jax.ShapeDtypeStruct

D = 1024
HEAD_DIM = 64
N_HEADS = 16
N_KV = 2
GROUP = 8
KV_W = 128
BLK = 128
N_BUCKETS = 32
MAX_EXACT = 16
MAX_DISTANCE = 128
EPS = 1e-6
NEG_INF = -1e30
Q_SCALE = HEAD_DIM ** -0.5

ADAM_LR = 0.001
ADAM_B1 = 0.9
ADAM_B2 = 0.999
ADAM_EPS = 1e-08
ADAM_WD = 0.01
ADAM_STEP = 10

N_CHIPS = 4
N_DEV = 8
BIN_COLS = 2 * D // N_CHIPS
VMEM_LIMIT = 56 * 1024 * 1024
SMALL_ROWS = 16
LOSS_ROW = 6
SMALL_PLACES = {
    "a_pre_norm": ("quarter", 0, (1, D // 4)), "a_conv_w": ("quarter", 8, (3, D // 4)),
    "a_post_norm": ("quarter", 1, (1, D // 4)), "kv_norm": ("rows", 2, (1, D)),
    "rel_bias": ("rel", 0, (N_BUCKETS, N_HEADS)), "b_pre_norm": ("rows", 3, (1, D)),
    "b_sinks": ("rows", 5, (1, N_HEADS)), "b_post_norm": ("rows", 4, (1, D)),
}


def _bucket_thresholds():
    def bucket(d):
        big = MAX_EXACT + int(math.log(d / MAX_EXACT) / math.log(MAX_DISTANCE / MAX_EXACT)
                              * (N_BUCKETS - MAX_EXACT))
        return d if d < MAX_EXACT else min(big, N_BUCKETS - 1)
    out = []
    for b in range(MAX_EXACT + 1, N_BUCKETS):
        out.append(min(d for d in range(MAX_EXACT, MAX_DISTANCE) if bucket(d) >= b))
    return tuple(out)


BUCKET_THRESHOLDS = _bucket_thresholds()


def _params(semantics=None, vmem=VMEM_LIMIT):
    return pltpu.CompilerParams(dimension_semantics=semantics, vmem_limit_bytes=vmem)


def _tile(n, pref):
    return pref if n >= 2 * pref else max(n // 2, 8)


def _rms_scale(v):
    return lax.rsqrt(jnp.mean(v * v, axis=-1, keepdims=True) + EPS)


def _nt(a, b):
    return lax.dot_general(a, b, (((1,), (1,)), ((), ())), preferred_element_type=F32)


def _tn(a, b):
    return lax.dot_general(a, b, (((0,), (0,)), ((), ())), preferred_element_type=F32)


def _nn(a, b):
    return jnp.dot(a, b, preferred_element_type=F32)


def _silu_parts(z):
    sg = jax.nn.sigmoid(z)
    return sg, z * sg


def _dsilu(z, sg):
    return sg * (1.0 + z * (1.0 - sg))


def _write_gradient(acc, out32, out16, stage):
    pltpu.sync_copy(acc, out32)
    rows = stage.shape[0]
    for k in range(acc.shape[0] // rows):
        stage[...] = acc[rows * k:rows * (k + 1), :].astype(BF16)
        pltpu.sync_copy(stage, out16.at[pl.ds(rows * k, rows)])


def _acc_row(ref, row, val):
    ref[row:row + 1, :] += val


def _gather_copies(outs, splits, ici_send, ici_recv, d2d_send, d2d_recv):
    x, y, c = lax.axis_index("x"), lax.axis_index("y"), lax.axis_index("c")
    k = 2 * x + y
    sibling = (x, y, 1 - c)

    def part(o_ref, chip, core, split):
        if not split:
            return o_ref.at[chip]
        h = o_ref.shape[1] // 2
        return o_ref.at[chip, pl.ds(pl.multiple_of(core * h, 16), h)]

    def remote(ref, a, j, sems, to):
        return pltpu.make_async_remote_copy(src_ref=ref, dst_ref=ref, send_sem=sems[0].at[3 * a + j],
                                            recv_sem=sems[1].at[3 * a + j], device_id=to, device_id_type=MESH)

    copies = []
    for a, (o_ref, split) in enumerate(zip(outs, splits)):
        for j, (px, py) in enumerate([(x, 1 - y), (1 - x, y), (1 - x, 1 - y)]):
            kj = 2 * px + py
            ici, d2d = (ici_send, ici_recv), (d2d_send, d2d_recv)
            copies.append((remote(part(o_ref, k, c, split), a, j, ici, (px, py, c)),
                           remote(part(o_ref, kj, c, split), a, j, ici, (px, py, c)),
                           remote(part(o_ref, kj, c, split), a, j, d2d, sibling) if split else None,
                           remote(part(o_ref, kj, 1 - c, split), a, j, d2d, sibling) if split else None))
    return copies


def _gather_sems(n):
    return [pltpu.SemaphoreType.DMA((3 * n,)) for _ in range(4)]


def _prepare_weights(shards, small):
    n = len(shards)

    def body(*refs):
        ins, small_in = refs[:n], refs[n]
        outs, small_out = refs[n + 1:2 * n + 1], refs[2 * n + 1]
        stages, put_sem = refs[2 * n + 2:3 * n + 2], refs[3 * n + 2]
        sems = refs[3 * n + 3:]
        k = 2 * lax.axis_index("x") + lax.axis_index("y")
        puts = []
        for a, (i_ref, stage, o_ref) in enumerate(zip(ins, stages, outs)):
            stage[...] = i_ref[...].astype(BF16)
            puts.append(pltpu.make_async_copy(stage, o_ref.at[k], put_sem.at[a]))
            puts[-1].start()
        small_out[k] = small_in[...]
        copies = _gather_copies([small_out], [False], *sems)
        for send, _, _, _ in copies:
            send.start()
        for _, arrival, _, _ in copies:
            arrival.wait_recv()
        for send, _, _, _ in copies:
            send.wait_send()
        for put in puts:
            put.wait()

    vm = pl.BlockSpec(memory_space=pltpu.VMEM)
    anyspace = pl.BlockSpec(memory_space=pl.ANY)
    out_shape = [SDS((N_CHIPS,) + s.shape, BF16) for s in shards] + [SDS((N_CHIPS,) + small.shape, F32)]
    return pl.pallas_call(
        body, name="prepare_weights", out_shape=out_shape,
        in_specs=[vm] * (n + 1), out_specs=[anyspace] * n + [vm],
        scratch_shapes=[pltpu.VMEM(s.shape, BF16) for s in shards] + [pltpu.SemaphoreType.DMA((n,))] + _gather_sems(1),
        compiler_params=pltpu.CompilerParams(vmem_limit_bytes=VMEM_LIMIT),
    )(*shards, small)


def _a_in(chip, x, g_pre, weights, tm):
    s = x.shape[0]
    nt = s // tm
    n = len(weights)

    def body(chip_ref, x_ref, g_ref, *refs):
        proj_ref, n1_ref = refs[n:n + 2]
        gathered = refs[n + 2:2 * n + 2]
        wbuf, n1_all, fetch_sem = refs[2 * n + 2:2 * n + 5]
        sems = refs[2 * n + 5:]
        jj, i = pl.program_id(0), pl.program_id(1)
        copies = _gather_copies(gathered, [True] * n, *sems)

        def fetch(rel):
            slot = jnp.bitwise_xor(chip_ref[0], rel)
            return pltpu.make_async_copy(gathered[0].at[slot], wbuf.at[rel % 2], fetch_sem.at[rel % 2])

        @pl.when((jj == 0) & (i == 0))
        def _():
            fetch(0).start()
            copies[0][0].start()
            copies[1][0].start()
            fetch(0).wait()

        for rel in (1, 2, 3):
            @pl.when((jj == rel) & (i == 0))
            def _():
                fetch(rel).wait()

        @pl.when(jj == 0)
        def _():
            xv = x_ref[...]
            n1 = (xv * _rms_scale(xv) * g_ref[...]).astype(BF16)
            n1_ref[...] = n1
            n1_all[i] = n1
        proj_ref[...] = _nn(n1_all[i], wbuf[jj % 2]).astype(BF16)

        for rel in (1, 2, 3):
            @pl.when((jj == rel - 1) & (i == max(nt - 2, nt // 2)))
            def _():
                _, arrival, forward, forwarded = copies[rel - 1]
                arrival.wait_recv()
                forward.start()
                forwarded.wait_recv()
                fetch(rel).start()
                if rel == 1:
                    for send, _, _, _ in copies[2:]:
                        send.start()

        @pl.when((jj == 3) & (i == max(nt - 2, 0)))
        def _():
            for _, arrival, forward, _ in copies[3:]:
                arrival.wait_recv()
                forward.start()

        @pl.when((jj == 3) & (i == nt - 1))
        def _():
            for _, _, _, forwarded in copies[3:]:
                forwarded.wait_recv()
            for send, _, forward, _ in copies:
                forward.wait_send()
                send.wait_send()

    anyspace = pl.BlockSpec(memory_space=pl.ANY)
    proj, n1, *gathered = pl.pallas_call(
        body, name="a_in",
        grid_spec=pltpu.PrefetchScalarGridSpec(
            num_scalar_prefetch=1, grid=(4, nt),
            in_specs=[pl.BlockSpec((tm, D), lambda jj, i, c: (jnp.where(jj == 0, i, nt - 1), 0)),
                      pl.BlockSpec((1, D), lambda jj, i, c: (0, 0))] + [anyspace] * n,
            out_specs=[pl.BlockSpec((tm, D), lambda jj, i, c: (i, jnp.bitwise_xor(c[0], jj))),
                       pl.BlockSpec((tm, D), lambda jj, i, c: (jnp.where(jj == 0, i, nt - 1), 0))] + [anyspace] * n,
            scratch_shapes=[pltpu.VMEM((2, D, D), BF16), pltpu.VMEM((nt, tm, D), BF16),
                            pltpu.SemaphoreType.DMA((2,))] + _gather_sems(n)),
        out_shape=[SDS((s, 4 * D), BF16), SDS((s, D), BF16)] + [SDS(w.shape, w.dtype) for w in weights],
        input_output_aliases={3 + a: 2 + a for a in range(n)},
        compiler_params=_params(("arbitrary", "arbitrary")),
    )(chip, x, g_pre, *weights)
    return proj, n1, gathered


def _shift_rows(v, last, second_last, rows):
    v1 = jnp.where(rows >= 1, pltpu.roll(v, 1, 0), last)
    v2 = jnp.where(rows >= 2, pltpu.roll(v, 2, 0), jnp.where(rows == 1, last, second_last))
    return v1, v2


def _a_mix(proj, x, conv_w, w_out, g_post, tm):
    s = x.shape[0]

    def body(proj_ref, x_ref, cw_ref, w_ref, g_ref, ya_ref, oa_ref, h1_ref, conv_ref, carry):
        @pl.when(pl.program_id(0) == 0)
        def _():
            carry[...] = jnp.zeros_like(carry)
        v = proj_ref[:, D:2 * D].astype(F32) * proj_ref[:, 2 * D:3 * D].astype(F32)
        rows = lax.broadcasted_iota(jnp.int32, (tm, D), 0)
        before = carry[...]
        v1, v2 = _shift_rows(v, before[7:8, :], before[6:7, :], rows)
        carry[...] = v[tm - 8:tm, :]
        conv = cw_ref[0:1, :] * v2 + cw_ref[1:2, :] * v1 + cw_ref[2:3, :] * v
        conv_ref[...] = conv.astype(BF16)
        _, sz = _silu_parts(proj_ref[:, 3 * D:4 * D].astype(F32))
        ya = (proj_ref[:, 0:D].astype(F32) * conv * sz).astype(BF16)
        ya_ref[...] = ya
        oa = _nn(ya, w_ref[...])
        oa_ref[...] = oa.astype(BF16)
        h1_ref[...] = x_ref[...] + oa * _rms_scale(oa) * g_ref[...]

    row = lambda i: (i, 0)
    fix = lambda i: (0, 0)
    return pl.pallas_call(
        body, name="a_mix", grid=(s // tm,),
        in_specs=[pl.BlockSpec((tm, 4 * D), row), pl.BlockSpec((tm, D), row), pl.BlockSpec((8, D), fix),
                  pl.BlockSpec((D, D), fix), pl.BlockSpec((1, D), fix)],
        out_specs=[pl.BlockSpec((tm, D), row)] * 4,
        out_shape=[SDS((s, D), BF16), SDS((s, D), BF16), SDS((s, D), F32), SDS((s, D), BF16)],
        scratch_shapes=[pltpu.VMEM((8, D), F32)],
        compiler_params=_params(("arbitrary",)),
    )(proj, x, conv_w, w_out, g_post)


def _b_in(h1, g_kv, g_pre, w_kv, wbin_g, tm):
    s = h1.shape[0]

    def body(h_ref, gk_ref, gb_ref, wkv_ref, wb_ref, kv_ref, q_ref, z_ref):
        h = h_ref[...]
        hh = h * _rms_scale(h)
        nk = (hh * gk_ref[...]).astype(BF16)
        nb = (hh * gb_ref[...]).astype(BF16)
        kv_ref[...] = _nn(nk, wkv_ref[...]).astype(BF16)
        for j in range(2):
            q_ref[:, BIN_COLS * j:BIN_COLS * (j + 1)] = (_nn(nb, wb_ref[j]) * Q_SCALE).astype(BF16)
            z_ref[:, BIN_COLS * j:BIN_COLS * (j + 1)] = _nn(nb, wb_ref[2 + j]).astype(BF16)

    row = lambda i: (i, 0)
    fix = lambda i: (0, 0)
    return pl.pallas_call(
        body, name="b_in", grid=(s // tm,),
        in_specs=[pl.BlockSpec((tm, D), row), pl.BlockSpec((1, D), fix), pl.BlockSpec((1, D), fix),
                  pl.BlockSpec((D, 2 * KV_W), fix), pl.BlockSpec((N_CHIPS, D, BIN_COLS), lambda i: (0, 0, 0))],
        out_specs=[pl.BlockSpec((tm, 2 * KV_W), row), pl.BlockSpec((tm, D), row), pl.BlockSpec((tm, D), row)],
        out_shape=[SDS((s, 2 * KV_W), BF16), SDS((s, D), BF16), SDS((s, D), BF16)],
        compiler_params=_params(("parallel",)),
    )(h1, g_kv, g_pre, w_kv, wbin_g)


def _band_buckets():
    q = lax.broadcasted_iota(jnp.int32, (BLK, 2 * BLK), 0)
    k = lax.broadcasted_iota(jnp.int32, (BLK, 2 * BLK), 1)
    dist = q + BLK - k
    bucket = jnp.where(dist < MAX_EXACT, dist, MAX_EXACT)
    for t in BUCKET_THRESHOLDS:
        bucket = bucket + jnp.where(dist >= t, 1, 0)
    in_window = (dist >= 0) & (dist < BLK)
    return jnp.where(in_window, bucket, -1)


def _head_place(h):
    kh, j, e = h // GROUP, (h % GROUP) // 2, h % 2
    return kh, slice(BLK * j, BLK * (j + 1)), slice(2 * BLK * e, 2 * BLK * (e + 1))


def _bias_table(rel_bias, sinks):
    def body(rb_ref, sink_ref, tab_ref):
        bucket = _band_buckets()
        col = lax.broadcasted_iota(jnp.int32, (BLK, 2 * BLK), 1)
        for h in range(N_HEADS):
            acc = jnp.where(bucket < 0, NEG_INF, 0.0).astype(F32)
            for b in range(N_BUCKETS):
                acc = jnp.where(bucket == b, rb_ref[b, h], acc)
            acc = jnp.where(col == 0, sink_ref[h], acc)
            kh, rows, cols = _head_place(h)
            tab_ref[1, kh, rows, cols] = acc
            tab_ref[0, kh, rows, cols] = jnp.where((col > 0) & (col < BLK), NEG_INF, acc)

    return pl.pallas_call(
        body, name="bias_table", out_shape=SDS((2, N_KV, 4 * BLK, 4 * BLK), F32),
        in_specs=[pl.BlockSpec(memory_space=pltpu.SMEM), pl.BlockSpec(memory_space=pltpu.SMEM)],
        out_specs=pl.BlockSpec(memory_space=pltpu.VMEM),
    )(rel_bias, sinks)


def _bias_fold(dtab):
    def body(dtab_ref, out_ref, dsink_ref):
        bucket = _band_buckets()
        row = lax.broadcasted_iota(jnp.int32, (N_BUCKETS, 128), 0)
        lane = lax.broadcasted_iota(jnp.int32, (N_BUCKETS, 128), 1)
        row8 = lax.broadcasted_iota(jnp.int32, (8, 128), 0)
        lane8 = lax.broadcasted_iota(jnp.int32, (8, 128), 1)
        acc = jnp.zeros((N_BUCKETS, 128), F32)
        dsink = jnp.zeros((8, 128), F32)
        for h in range(N_HEADS):
            kh, rows, cols = _head_place(h)
            dt = dtab_ref[kh, rows, cols]
            for b in range(N_BUCKETS):
                val = jnp.sum(jnp.where(bucket == b, dt, 0.0))
                acc = acc + jnp.where((row == b) & (lane == h), val, 0.0)
            dsink = dsink + jnp.where((row8 == 0) & (lane8 == h), jnp.sum(dt[:, 0:1]), 0.0)
        out_ref[...] = acc
        dsink_ref[...] = dsink

    vm = pl.BlockSpec(memory_space=pltpu.VMEM)
    return pl.pallas_call(
        body, name="bias_fold", out_shape=[SDS((N_BUCKETS, 128), F32), SDS((8, 128), F32)],
        in_specs=[vm], out_specs=[vm, vm],
    )(dtab)


def _pair_operands(prev, cur):
    t = jnp.concatenate([prev, cur], axis=0).astype(F32)
    t = jnp.where(lax.broadcasted_iota(jnp.int32, t.shape, 0) == 0, 0.0, t)
    tr = pltpu.roll(t, HEAD_DIM, 1)
    lo = lax.broadcasted_iota(jnp.int32, t.shape, 1) < HEAD_DIM
    zero = jnp.zeros_like(t)
    head0 = jnp.concatenate([jnp.where(lo, t, zero), jnp.where(lo, zero, tr)], axis=0).astype(BF16)
    head1 = jnp.concatenate([jnp.where(lo, tr, zero), jnp.where(lo, zero, t)], axis=0).astype(BF16)
    return head0, head1


def _pair_fold(d0, d1):
    lo = lax.broadcasted_iota(jnp.int32, (2 * BLK, KV_W), 1) < HEAD_DIM
    zero = jnp.zeros((2 * BLK, KV_W), F32)
    g0 = jnp.where(lo, d0[0:256], zero) + pltpu.roll(jnp.where(lo, zero, d0[256:512]), HEAD_DIM, 1)
    g1 = pltpu.roll(jnp.where(lo, d1[0:256], zero), HEAD_DIM, 1) + jnp.where(lo, zero, d1[256:512])
    return jnp.where(lax.broadcasted_iota(jnp.int32, (2 * BLK, KV_W), 0) == 0, 0.0, g0 + g1)


def _stack_pairs(ref, kh):
    return jnp.concatenate([ref[:, 128 * (4 * kh + j):128 * (4 * kh + j + 1)] for j in range(4)], axis=0)


def _table_spec():
    return pl.BlockSpec((1, N_KV, 4 * BLK, 4 * BLK), lambda n: (jnp.minimum(n, 1), 0, 0, 0))


def _attn_fwd(q, kv, tab):
    s = q.shape[0]

    def body(q_ref, kp_ref, kc_ref, vp_ref, vc_ref, tab_ref, att_ref, stats_ref):
        k2 = _pair_operands(kp_ref[...], kc_ref[...])
        v2 = _pair_operands(vp_ref[...], vc_ref[...])
        lane = lax.broadcasted_iota(jnp.int32, (BLK, 128), 1)
        stats = jnp.zeros((BLK, 128), F32)
        for kh in range(N_KV):
            sc = _nt(_stack_pairs(q_ref, kh), k2[kh])
            ps = []
            for e in range(2):
                lg = sc[:, 256 * e:256 * (e + 1)] + tab_ref[0, kh, :, 256 * e:256 * (e + 1)]
                m = jnp.max(lg, axis=-1, keepdims=True)
                ex = jnp.exp(lg - m)
                den = jnp.sum(ex, axis=-1, keepdims=True)
                ps.append(ex * (1.0 / den))
                lse = m + jnp.log(den)
                for j in range(4):
                    stats = jnp.where(lane == GROUP * kh + 2 * j + e, lse[BLK * j:BLK * (j + 1)], stats)
            out = _nn(jnp.concatenate(ps, axis=1).astype(BF16), v2[kh])
            for j in range(4):
                att_ref[:, 128 * (4 * kh + j):128 * (4 * kh + j + 1)] = out[BLK * j:BLK * (j + 1)].astype(BF16)
        stats_ref[...] = stats

    cur = lambda n: (n, 0)
    prev = lambda n: (jnp.maximum(n - 1, 0), 0)
    return pl.pallas_call(
        body, name="attn_fwd", grid=(s // BLK,),
        in_specs=[pl.BlockSpec((BLK, D), cur),
                  pl.BlockSpec((BLK, KV_W), prev), pl.BlockSpec((BLK, KV_W), cur),
                  pl.BlockSpec((BLK, KV_W), lambda n: (jnp.maximum(n - 1, 0), 1)),
                  pl.BlockSpec((BLK, KV_W), lambda n: (n, 1)), _table_spec()],
        out_specs=[pl.BlockSpec((BLK, D), cur), pl.BlockSpec((BLK, 128), cur)],
        out_shape=[SDS((s, D), BF16), SDS((s, 128), F32)],
        compiler_params=_params(("parallel",)),
    )(q, kv, kv, kv, kv, tab)


def _mid(att, zb, h1, tgt, w_out, g_post, tm):
    s = att.shape[0]
    nt = s // tm

    def body(att_ref, z_ref, h1_ref, t_ref, w_ref, g_ref,
             dh_ref, dqz_ref, datt_ref, loss_ref, dg_ref, dw_ref, dw16_ref, dw_acc, stage):
        @pl.when(pl.program_id(0) == 0)
        def _():
            loss_ref[...] = jnp.zeros_like(loss_ref)
            dg_ref[...] = jnp.zeros_like(dg_ref)
            dw_acc[...] = jnp.zeros_like(dw_acc)
        att = att_ref[...].astype(F32)
        z = z_ref[...].astype(F32)
        sg, sz = _silu_parts(z)
        ob = (att * sz).astype(BF16)
        y2 = _nn(ob, w_ref[...])
        r2 = _rms_scale(y2)
        yh = y2 * r2
        g = g_ref[...]
        err = (h1_ref[...] + yh * g) - t_ref[...]
        loss_ref[...] += jnp.sum(jnp.sum(err * err, axis=-1, keepdims=True) / D)
        dh = err / D
        dh_ref[...] = dh
        _acc_row(dg_ref, 0, jnp.sum(dh * yh, axis=0, keepdims=True))
        dyh = dh * g
        dy = (r2 * (dyh - yh * jnp.mean(dyh * yh, axis=-1, keepdims=True))).astype(BF16)
        dw_acc[...] += _tn(ob, dy)
        dob = _nt(dy, w_ref[...])
        datt_ref[...] = (dob * sz).astype(BF16)
        dqz_ref[...] = (dob * att * _dsilu(z, sg)).astype(BF16)

        @pl.when(pl.program_id(0) == nt - 1)
        def _():
            _write_gradient(dw_acc, dw_ref, dw16_ref, stage)

    row = lambda i: (i, 0)
    fix = lambda i: (0, 0)
    anyspace = pl.BlockSpec(memory_space=pl.ANY)
    return pl.pallas_call(
        body, name="mid", grid=(nt,),
        in_specs=[pl.BlockSpec((tm, D), row)] * 4 + [pl.BlockSpec((D, D), fix), pl.BlockSpec((1, D), fix)],
        out_specs=[pl.BlockSpec((tm, D), row), pl.BlockSpec((tm, D), lambda i: (i, 1)), pl.BlockSpec((tm, D), row),
                   pl.BlockSpec((8, 128), fix), pl.BlockSpec((8, D), fix), anyspace, anyspace],
        out_shape=[SDS((s, D), F32), SDS((s, 2 * D), BF16), SDS((s, D), BF16), SDS((8, 128), F32),
                   SDS((8, D), F32), SDS((D, D), F32), SDS((D, D), BF16)],
        scratch_shapes=[pltpu.VMEM((D, D), F32), pltpu.VMEM((D // 4, D), BF16)],
        compiler_params=_params(("arbitrary",)),
    )(att, zb, h1, tgt, w_out, g_post)


def _attn_bwd(q, kv, datt, stats, tab, dqz):
    s = q.shape[0]
    nb = s // BLK

    def body(q_ref, kp_ref, kc_ref, vp_ref, vc_ref, da_ref, st_ref, tab_ref, dqz_in,
             dq_ref, dkv_ref, dtab_ref, dk_carry, dv_carry):
        del dqz_in
        n = pl.program_id(0)

        @pl.when(n == 0)
        def _():
            dtab_ref[...] = jnp.zeros_like(dtab_ref)
            dk_carry[...] = jnp.zeros_like(dk_carry)
            dv_carry[...] = jnp.zeros_like(dv_carry)

        @pl.when(n < nb)
        def _():
            k2 = _pair_operands(kp_ref[...], kc_ref[...])
            v2 = _pair_operands(vp_ref[...], vc_ref[...])
            lane = lax.broadcasted_iota(jnp.int32, (BLK, 128), 1)
            stats = st_ref[...]
            dk2, dv2 = [], []
            for kh in range(N_KV):
                qs = _stack_pairs(q_ref, kh)
                das = _stack_pairs(da_ref, kh)
                sc = _nt(qs, k2[kh])
                dp = _nt(das, v2[kh])
                ps, dss = [], []
                for e in range(2):
                    heads = [GROUP * kh + 2 * j + e for j in range(4)]
                    lse = jnp.concatenate([jnp.sum(jnp.where(lane == h, stats, 0.0), axis=-1, keepdims=True)
                                           for h in heads], axis=0)
                    cols = slice(256 * e, 256 * (e + 1))
                    p = jnp.exp(sc[:, cols] + tab_ref[0, kh, :, cols] - lse)
                    delta = jnp.sum(p * dp[:, cols], axis=-1, keepdims=True)
                    ds = p * (dp[:, cols] - delta)
                    dtab_ref[kh, :, cols] += ds
                    ps.append(p)
                    dss.append(ds)
                p2 = jnp.concatenate(ps, axis=1).astype(BF16)
                ds2 = jnp.concatenate(dss, axis=1).astype(BF16)
                dq = _nn(ds2, k2[kh]) * Q_SCALE
                for j in range(4):
                    dq_ref[:, 128 * (4 * kh + j):128 * (4 * kh + j + 1)] = dq[BLK * j:BLK * (j + 1)].astype(BF16)
                dk2.append(_tn(ds2, qs))
                dv2.append(_tn(p2, das))
            dkk = _pair_fold(dk2[0], dk2[1])
            dvv = _pair_fold(dv2[0], dv2[1])
            dkv_ref[:, 0:KV_W] = (dk_carry[...] + dkk[0:BLK]).astype(BF16)
            dkv_ref[:, KV_W:2 * KV_W] = (dv_carry[...] + dvv[0:BLK]).astype(BF16)
            dk_carry[...] = dkk[BLK:2 * BLK]
            dv_carry[...] = dvv[BLK:2 * BLK]

        @pl.when(n == nb)
        def _():
            dkv_ref[:, 0:KV_W] = dk_carry[...].astype(BF16)
            dkv_ref[:, KV_W:2 * KV_W] = dv_carry[...].astype(BF16)

    cur = lambda n: (jnp.minimum(n, nb - 1), 0)
    prev = lambda n: (jnp.clip(n - 1, 0, nb - 1), 0)
    return pl.pallas_call(
        body, name="attn_bwd", grid=(nb + 1,),
        in_specs=[pl.BlockSpec((BLK, D), cur),
                  pl.BlockSpec((BLK, KV_W), prev), pl.BlockSpec((BLK, KV_W), cur),
                  pl.BlockSpec((BLK, KV_W), lambda n: (jnp.clip(n - 1, 0, nb - 1), 1)),
                  pl.BlockSpec((BLK, KV_W), lambda n: (jnp.minimum(n, nb - 1), 1)),
                  pl.BlockSpec((BLK, D), cur), pl.BlockSpec((BLK, 128), cur), _table_spec(),
                  pl.BlockSpec(memory_space=pl.ANY)],
        out_specs=[pl.BlockSpec((BLK, D), cur), pl.BlockSpec((BLK, 2 * KV_W), prev),
                   pl.BlockSpec((N_KV, 4 * BLK, 4 * BLK), lambda n: (0, 0, 0))],
        out_shape=[SDS((s, 2 * D), BF16), SDS((s, 2 * KV_W), BF16), SDS((N_KV, 4 * BLK, 4 * BLK), F32)],
        scratch_shapes=[pltpu.VMEM((BLK, KV_W), F32), pltpu.VMEM((BLK, KV_W), F32)],
        input_output_aliases={8: 0},
        compiler_params=_params(("arbitrary",)),
    )(q, kv, kv, kv, kv, datt, stats, tab, dqz)


def _b_bwd(dqz, dkv, h1, dh2, oa, wbin_g, w_kv, g_kv, g_pre, g_apost, tm):
    s = h1.shape[0]
    nt = s // tm

    def body(dqz_ref, dkv_ref, h_ref, dh2_ref, oa_ref, wb_ref, wkv_ref, gk_ref, gb_ref, ga_ref,
             dh1_ref, doa_ref, dg_ref, dwb_ref, dwkv_ref, dwb16_ref, dwkv16_ref, wcat, dwb_acc, dwkv_acc, put_sem):
        @pl.when(pl.program_id(0) == 0)
        def _():
            dg_ref[...] = jnp.zeros_like(dg_ref)
            dwb_acc[...] = jnp.zeros_like(dwb_acc)
            dwkv_acc[...] = jnp.zeros_like(dwkv_acc)
            for j in range(N_CHIPS):
                pltpu.sync_copy(wb_ref.at[j], wcat.at[:, pl.ds(BIN_COLS * j, BIN_COLS)])
        dnb = _nt(dqz_ref[...], wcat[...])
        dnk = _nt(dkv_ref[...], wkv_ref[...])
        h = h_ref[...]
        r = _rms_scale(h)
        hh = h * r
        dwb_acc[...] += _tn((hh * gb_ref[...]).astype(BF16), dqz_ref[...])
        dwkv_acc[...] += _tn((hh * gk_ref[...]).astype(BF16), dkv_ref[...])
        _acc_row(dg_ref, 0, jnp.sum(dnk * hh, axis=0, keepdims=True))
        _acc_row(dg_ref, 1, jnp.sum(dnb * hh, axis=0, keepdims=True))
        dhh = dnb * gb_ref[...] + dnk * gk_ref[...]
        dh1 = dh2_ref[...] + r * (dhh - hh * jnp.mean(dhh * hh, axis=-1, keepdims=True))
        dh1_ref[...] = dh1
        oa = oa_ref[...].astype(F32)
        ra = _rms_scale(oa)
        oh = oa * ra
        _acc_row(dg_ref, 2, jnp.sum(dh1 * oh, axis=0, keepdims=True))
        doh = dh1 * ga_ref[...]
        doa_ref[...] = (ra * (doh - oh * jnp.mean(doh * oh, axis=-1, keepdims=True))).astype(BF16)

        @pl.when(pl.program_id(0) == nt - 1)
        def _():
            wcat[...] = dwb_acc[...].astype(BF16)
            puts = [pltpu.make_async_copy(dwkv_acc, dwkv_ref, put_sem.at[2 * N_CHIPS])]
            for j in range(N_CHIPS):
                cols = pl.ds(BIN_COLS * j, BIN_COLS)
                puts.append(pltpu.make_async_copy(dwb_acc.at[:, cols], dwb_ref.at[j], put_sem.at[2 * j]))
                puts.append(pltpu.make_async_copy(wcat.at[:, cols], dwb16_ref.at[j], put_sem.at[2 * j + 1]))
            for put in puts:
                put.start()
            for put in puts:
                put.wait()
            wcat[:, 0:2 * KV_W] = dwkv_acc[...].astype(BF16)
            pltpu.sync_copy(wcat.at[:, pl.ds(0, 2 * KV_W)], dwkv16_ref)

    row = lambda i: (i, 0)
    fix = lambda i: (0, 0)
    anyspace = pl.BlockSpec(memory_space=pl.ANY)
    return pl.pallas_call(
        body, name="b_bwd", grid=(nt,),
        in_specs=[pl.BlockSpec((tm, 2 * D), row), pl.BlockSpec((tm, 2 * KV_W), row), pl.BlockSpec((tm, D), row),
                  pl.BlockSpec((tm, D), row), pl.BlockSpec((tm, D), row), anyspace, pl.BlockSpec((D, 2 * KV_W), fix),
                  pl.BlockSpec((1, D), fix), pl.BlockSpec((1, D), fix), pl.BlockSpec((1, D), fix)],
        out_specs=[pl.BlockSpec((tm, D), row), pl.BlockSpec((tm, D), row), pl.BlockSpec((8, D), fix)] + [anyspace] * 4,
        out_shape=[SDS((s, D), F32), SDS((s, D), BF16), SDS((8, D), F32), SDS((N_CHIPS, D, BIN_COLS), F32),
                   SDS((D, 2 * KV_W), F32), SDS((N_CHIPS, D, BIN_COLS), BF16), SDS((D, 2 * KV_W), BF16)],
        scratch_shapes=[pltpu.VMEM((D, 2 * D), BF16), pltpu.VMEM((D, 2 * D), F32), pltpu.VMEM((D, 2 * KV_W), F32),
                        pltpu.SemaphoreType.DMA((2 * N_CHIPS + 1,))],
        compiler_params=_params(("arbitrary",)),
    )(dqz, dkv, h1, dh2, oa, wbin_g, w_kv, g_kv, g_pre, g_apost)


def _to_owner_core(pieces, r, send, recv, core, action):
    x, y, c = lax.axis_index("x"), lax.axis_index("y"), lax.axis_index("c")
    for kp in range(N_CHIPS):
        px, py = kp >> 1, kp & 1
        rel = 4 * (x + px - 2 * x * px) + 2 * (y + py - 2 * y * py) + (c + core - 2 * c * core)

        @pl.when(rel != 0)
        def _():
            cp = pltpu.make_async_remote_copy(src_ref=pieces.at[kp], dst_ref=r.at[rel - 1], send_sem=send.at[kp],
                                              recv_sem=recv.at[rel - 1], device_id=(px, py, core), device_id_type=MESH)
            if action == "start":
                cp.start()
            else:
                cp.wait_send()
    if action == "wait":
        @pl.when(c == core)
        def _():
            for rel in range(1, N_DEV):
                pltpu.make_async_remote_copy(src_ref=pieces.at[0], dst_ref=r.at[rel - 1], send_sem=send.at[0],
                                             recv_sem=recv.at[rel - 1], device_id=(x, y, c),
                                             device_id_type=MESH).wait_recv()


def _owner_core_sems():
    return [pltpu.SemaphoreType.DMA((N_CHIPS,)), pltpu.SemaphoreType.DMA((N_DEV - 1,))]


def _device_exchange(grads, recvs, send, recv):
    x, y, c = lax.axis_index("x"), lax.axis_index("y"), lax.axis_index("c")
    copies = []
    for a, (g, r) in enumerate(zip(grads, recvs)):
        h = g.shape[1] // 2
        for rel in range(1, N_DEV):
            fx, fy, fc = rel >> 2, (rel >> 1) & 1, rel & 1
            px, py, pc = x + fx - 2 * x * fx, y + fy - 2 * y * fy, c + fc - 2 * c * fc
            sem = (N_DEV - 1) * a + rel - 1
            copies.append(pltpu.make_async_remote_copy(
                src_ref=g.at[2 * px + py, pl.ds(pl.multiple_of(pc * h, 16), h)], dst_ref=r.at[rel - 1],
                send_sem=send.at[sem], recv_sem=recv.at[sem], device_id=(px, py, pc), device_id_type=MESH))
    return copies


def _device_exchange_specs(grads):
    anyspace = pl.BlockSpec(memory_space=pl.ANY)
    n = len(grads)
    count = (N_DEV - 1) * n
    return ([anyspace] * n, [anyspace] * n,
            [SDS((N_DEV - 1, g.shape[1] // 2, g.shape[2]), g.dtype) for g in grads],
            [pltpu.SemaphoreType.DMA((count,)), pltpu.SemaphoreType.DMA((count,))])


def _a_bwd(doa, ya, conv, proj, conv_w, w_out, tm, parts):
    s = doa.shape[0]
    nt = s // tm
    n = len(parts)
    ex_in, ex_out, ex_shape, ex_sems = _device_exchange_specs(parts)

    def body(*refs):
        doa_ref, ya_ref, conv_ref, proj_ref, cw_ref, w_ref = refs[:6]
        part_refs = refs[6:6 + n]
        dproj_ref, dcw_ref, dw_ref, dw16_ref = refs[6 + n:10 + n]
        recv_refs = refs[10 + n:10 + 2 * n]
        carry, dw_acc, stage, send, recv = refs[10 + 2 * n:]
        i = pl.program_id(0)

        @pl.when(i == 0)
        def _():
            dcw_ref[...] = jnp.zeros_like(dcw_ref)
            carry[...] = jnp.zeros_like(carry)
            dw_acc[...] = jnp.zeros_like(dw_acc)
            for cp in _device_exchange(part_refs, recv_refs, send, recv):
                cp.start()
        dya = _nt(doa_ref[...], w_ref[...])
        dw_acc[...] += _tn(ya_ref[...], doa_ref[...])
        bg = proj_ref[:, 0:D].astype(F32)
        cg = proj_ref[:, D:2 * D].astype(F32)
        u = proj_ref[:, 2 * D:3 * D].astype(F32)
        z = proj_ref[:, 3 * D:4 * D].astype(F32)
        v = cg * u
        rows = lax.broadcasted_iota(jnp.int32, (tm, D), 0)
        conv = conv_ref[...].astype(F32)
        sg, sz = _silu_parts(z)
        dproj_ref[:, 0:D] = (dya * conv * sz).astype(BF16)
        dproj_ref[:, 3 * D:4 * D] = (dya * bg * conv * _dsilu(z, sg)).astype(BF16)
        dconv = dya * bg * sz
        after = carry[...]
        up1 = jnp.where(rows < tm - 1, pltpu.roll(dconv, tm - 1, 0), after[0:1, :])
        up2 = jnp.where(rows < tm - 2, pltpu.roll(dconv, tm - 2, 0),
                        jnp.where(rows == tm - 2, after[0:1, :], after[1:2, :]))
        carry[...] = dconv[0:8, :]
        _acc_row(dcw_ref, 0, jnp.sum(up2 * v, axis=0, keepdims=True))
        _acc_row(dcw_ref, 1, jnp.sum(up1 * v, axis=0, keepdims=True))
        _acc_row(dcw_ref, 2, jnp.sum(dconv * v, axis=0, keepdims=True))
        dv = cw_ref[2:3, :] * dconv + cw_ref[1:2, :] * up1 + cw_ref[0:1, :] * up2
        dproj_ref[:, D:2 * D] = (dv * u).astype(BF16)
        dproj_ref[:, 2 * D:3 * D] = (dv * cg).astype(BF16)

        @pl.when(i == nt - 1)
        def _():
            _write_gradient(dw_acc, dw_ref, dw16_ref, stage)
            for cp in _device_exchange(part_refs, recv_refs, send, recv):
                cp.wait()

    rev = lambda i: (nt - 1 - i, 0)
    fix = lambda i: (0, 0)
    anyspace = pl.BlockSpec(memory_space=pl.ANY)
    dproj, dcw, dw, dw16, *got = pl.pallas_call(
        body, name="a_bwd", grid=(nt,),
        in_specs=[pl.BlockSpec((tm, D), rev), pl.BlockSpec((tm, D), rev), pl.BlockSpec((tm, D), rev),
                  pl.BlockSpec((tm, 4 * D), rev), pl.BlockSpec((8, D), fix), pl.BlockSpec((D, D), fix)] + ex_in,
        out_specs=[pl.BlockSpec((tm, 4 * D), rev), pl.BlockSpec((8, D), fix), anyspace, anyspace] + ex_out,
        out_shape=[SDS((s, 4 * D), BF16), SDS((8, D), F32), SDS((D, D), F32), SDS((D, D), BF16)] + ex_shape,
        scratch_shapes=[pltpu.VMEM((8, D), F32), pltpu.VMEM((D, D), F32), pltpu.VMEM((D // 4, D), BF16)] + ex_sems,
        compiler_params=_params(("arbitrary",)),
    )(doa, ya, conv, proj, conv_w, w_out, *parts)
    return dproj, dcw, dw, dw16, got


def _dn1(dp_ref, w_ref):
    dn = _nt(dp_ref[:, 0:D], w_ref[0])
    for j in range(1, 4):
        dn = dn + _nt(dp_ref[:, D * j:D * (j + 1)], w_ref[j])
    return dn


def _a_in_bwd_matmul(dproj, win_g, tm, count, win_half, win_got):
    def body(dp_ref, w_ref, half_ref, got_in, dn_ref, got_ref, wcat, send, recv):
        del got_in

        @pl.when(pl.program_id(0) == 0)
        def _():
            _to_owner_core(half_ref, got_ref, send, recv, 1, "start")
            for j in range(N_CHIPS):
                pltpu.sync_copy(w_ref.at[j], wcat.at[:, pl.ds(D * j, D)])
        dn_ref[...] = _nt(dp_ref[...], wcat[...]).astype(BF16)

        @pl.when(pl.program_id(0) == count - 1)
        def _():
            _to_owner_core(half_ref, got_ref, send, recv, 1, "wait")

    row = lambda i: (i, 0)
    anyspace = pl.BlockSpec(memory_space=pl.ANY)
    return pl.pallas_call(
        body, name="a_in_bwd_matmul", grid=(count,),
        in_specs=[pl.BlockSpec((tm, 4 * D), row), anyspace, anyspace, anyspace],
        out_specs=[pl.BlockSpec((tm, D), row), anyspace],
        out_shape=[SDS((count * tm, D), BF16), SDS(win_got.shape, win_got.dtype)],
        scratch_shapes=[pltpu.VMEM((D, 4 * D), BF16)] + _owner_core_sems(),
        input_output_aliases={3: 1},
        compiler_params=_params(("arbitrary",)),
    )(dproj, win_g, win_half, win_got)


def _a_in_bwd(dn_first, dproj, x, dh1, win_g, g_pre, tm):
    s = x.shape[0]
    nt = s // tm
    count = dn_first.shape[0] // tm

    def body(dn_ref, dp_ref, x_ref, dh_ref, w_ref, g_ref, gx_ref, dg_ref, dn_s):
        i = pl.program_id(0)

        @pl.when(i == 0)
        def _():
            dg_ref[...] = jnp.zeros_like(dg_ref)

        @pl.when(i < count)
        def _():
            dn_s[...] = dn_ref[...].astype(F32)

        @pl.when(i >= count)
        def _():
            dn_s[...] = _dn1(dp_ref, w_ref)
        dn = dn_s[...]
        xv = x_ref[...]
        r = _rms_scale(xv)
        xh = xv * r
        _acc_row(dg_ref, 0, jnp.sum(dn * xh, axis=0, keepdims=True))
        dxh = dn * g_ref[...]
        gx_ref[...] = dh_ref[...] + r * (dxh - xh * jnp.mean(dxh * xh, axis=-1, keepdims=True))

    row = lambda i: (i, 0)
    fix = lambda i: (0, 0)
    return pl.pallas_call(
        body, name="a_in_bwd", grid=(nt,),
        in_specs=[pl.BlockSpec((tm, D), lambda i: (jnp.minimum(i, count - 1), 0)),
                  pl.BlockSpec((tm, 4 * D), lambda i: (jnp.maximum(i, count), 0)),
                  pl.BlockSpec((tm, D), row), pl.BlockSpec((tm, D), row),
                  pl.BlockSpec((4, D, D), lambda i: (0, 0, 0)), pl.BlockSpec((1, D), fix)],
        out_specs=[pl.BlockSpec((tm, D), row), pl.BlockSpec((8, D), fix)],
        out_shape=[SDS((s, D), F32), SDS((8, D), F32)],
        scratch_shapes=[pltpu.VMEM((tm, D), F32)],
        compiler_params=_params(("arbitrary",)),
    )(dn_first, dproj, x, dh1, win_g, g_pre)


def _dw_in_half(n1, dproj, core, tmw, name, to_owners=None, to_devices=None):
    s = n1.shape[0]
    h = D // 2
    nt = s // tmw
    sent_array = to_owners if to_owners is not None else to_devices
    rides = sent_array is not None
    if to_owners is not None:
        sems, got_shape = _owner_core_sems(), SDS((N_DEV - 1, h, D), BF16)
    elif to_devices is not None:
        _, _, (got_shape,), sems = _device_exchange_specs([to_devices])

    def body(*refs):
        a_ref, b_ref = refs[:2]
        o_ref, o16_ref = refs[2 + rides:4 + rides]
        j, t = pl.program_id(0), pl.program_id(1)

        def exchange(action):
            sent, got, send, recv = refs[2], refs[5], refs[6], refs[7]
            if to_owners is not None:
                _to_owner_core(sent, got, send, recv, 1 - core, action)
            else:
                for cp in _device_exchange([sent], [got], send, recv):
                    cp.start() if action == "start" else cp.wait()

        if rides:
            @pl.when((j == 0) & (t == 0))
            def _():
                exchange("start")

        @pl.when(t == 0)
        def _():
            o_ref[...] = jnp.zeros_like(o_ref)
        o_ref[0] += _tn(a_ref[...], b_ref[...])

        @pl.when(t == nt - 1)
        def _():
            o16_ref[...] = o_ref[...].astype(BF16)
        if rides:
            @pl.when((j == N_CHIPS - 1) & (t == nt - 1))
            def _():
                exchange("wait")

    anyspace = pl.BlockSpec(memory_space=pl.ANY)
    slot = pl.BlockSpec((1, h, D), lambda j, t: (j, 0, 0))
    return pl.pallas_call(
        body, name=name, grid=(N_CHIPS, nt),
        in_specs=[pl.BlockSpec((tmw, h), lambda j, t: (t, core)), pl.BlockSpec((tmw, D), lambda j, t: (t, j))]
        + [anyspace] * rides,
        out_specs=[slot, slot] + [anyspace] * rides,
        out_shape=[SDS((N_CHIPS, h, D), F32), SDS((N_CHIPS, h, D), BF16)] + ([got_shape] if rides else []),
        scratch_shapes=sems if rides else [],
        compiler_params=_params(("arbitrary", "arbitrary")),
    )(n1, dproj, *([sent_array] if rides else []))


def _share_and_gather(shards, smalls):
    n_h, n_s = len(shards), len(smalls)

    def body(*refs):
        small_ins = refs[n_h:n_h + n_s]
        fs = refs[n_h + n_s:2 * n_h + n_s]
        small_alls = refs[2 * n_h + n_s:2 * n_h + 2 * n_s]
        dsend, drecv, ssend, srecv = refs[2 * n_h + 2 * n_s:]
        x, y, c = lax.axis_index("x"), lax.axis_index("y"), lax.axis_index("c")
        sibling = (x, y, 1 - c)
        sends, arrivals = [], []
        for b, full in enumerate(fs):
            h = full.shape[0] // 2
            mine = full.at[pl.ds(pl.multiple_of(c * h, 8), h)]
            theirs = full.at[pl.ds(pl.multiple_of((1 - c) * h, 8), h)]
            sends.append(pltpu.make_async_remote_copy(src_ref=mine, dst_ref=mine, send_sem=dsend.at[b],
                                                      recv_sem=drecv.at[b], device_id=sibling, device_id_type=MESH))
            arrivals.append(pltpu.make_async_remote_copy(src_ref=mine, dst_ref=theirs, send_sem=dsend.at[b],
                                                         recv_sem=drecv.at[b], device_id=sibling, device_id_type=MESH))
        me = 4 * x + 2 * y + c
        for k, (small_in, small_all) in enumerate(zip(small_ins, small_alls)):
            small_all[me] = small_in[...]
            for rel in range(1, N_DEV):
                fx, fy, fc = rel >> 2, (rel >> 1) & 1, rel & 1
                peer = (x + fx - 2 * x * fx, y + fy - 2 * y * fy, c + fc - 2 * c * fc)
                sender = 4 * peer[0] + 2 * peer[1] + peer[2]
                sem = (N_DEV - 1) * k + rel - 1
                sends.append(pltpu.make_async_remote_copy(
                    src_ref=small_in, dst_ref=small_all.at[me], send_sem=ssend.at[sem], recv_sem=srecv.at[sem],
                    device_id=peer, device_id_type=MESH))
                arrivals.append(pltpu.make_async_remote_copy(
                    src_ref=small_in, dst_ref=small_all.at[sender], send_sem=ssend.at[sem], recv_sem=srecv.at[sem],
                    device_id=peer, device_id_type=MESH))
        for cp in sends:
            cp.start()
        for cp in arrivals:
            cp.wait_recv()
        for cp in sends:
            cp.wait_send()

    anyspace = pl.BlockSpec(memory_space=pl.ANY)
    vm = pl.BlockSpec(memory_space=pltpu.VMEM)
    out_shape = [SDS(full.shape, F32) for full in shards] + [SDS((N_DEV,) + sm.shape, F32) for sm in smalls]
    n_all = (N_DEV - 1) * n_s
    outs = pl.pallas_call(
        body, name="share_and_gather", out_shape=out_shape,
        in_specs=[anyspace] * n_h + [vm] * n_s, out_specs=[anyspace] * n_h + [vm] * n_s,
        scratch_shapes=[pltpu.SemaphoreType.DMA((n_h,)), pltpu.SemaphoreType.DMA((n_h,)),
                        pltpu.SemaphoreType.DMA((n_all,)), pltpu.SemaphoreType.DMA((n_all,))],
        input_output_aliases={b: b for b in range(n_h)},
    )(*shards, *smalls)
    return outs[:n_h], outs[n_h:]


def _add_win(where, lo, hi, r, name):
    _, h, cols = lo.shape
    tr = min(h, 256)
    nh = h // tr

    def body(where_ref, lo_ref, hi_ref, r_ref, o_ref):
        acc = jnp.where(where_ref[0] == 0, lo_ref[0], hi_ref[0])
        for k in range(N_DEV - 1):
            acc = acc + r_ref[k].astype(F32)
        o_ref[...] = acc

    own = pl.BlockSpec((1, tr, cols), lambda i, w: (w[1], i, 0))
    return pl.pallas_call(
        body, name=name,
        grid_spec=pltpu.PrefetchScalarGridSpec(
            num_scalar_prefetch=1, grid=(nh,),
            in_specs=[own, own, pl.BlockSpec((N_DEV - 1, tr, cols), lambda i, w: (0, i, 0))],
            out_specs=pl.BlockSpec((tr, cols), lambda i, w: (w[0] * nh + i, 0))),
        out_shape=SDS((2 * h, cols), F32),
        compiler_params=_params(("parallel",)),
    )(where, lo, hi, r)


def _add_devices(where, g, r, name):
    _, rows, cols = g.shape
    h = rows // 2
    tr = min(h, 256)
    nh = h // tr

    def body(where_ref, g_ref, r_ref, o_ref):
        del where_ref
        acc = g_ref[0]
        for k in range(N_DEV - 1):
            acc = acc + r_ref[k].astype(F32)
        o_ref[...] = acc

    return pl.pallas_call(
        body, name=name,
        grid_spec=pltpu.PrefetchScalarGridSpec(
            num_scalar_prefetch=1, grid=(nh,),
            in_specs=[pl.BlockSpec((1, tr, cols), lambda i, w: (w[1], w[0] * nh + i, 0)),
                      pl.BlockSpec((N_DEV - 1, tr, cols), lambda i, w: (0, i, 0))],
            out_specs=pl.BlockSpec((tr, cols), lambda i, w: (w[0] * nh + i, 0))),
        out_shape=SDS((rows, cols), F32),
        compiler_params=_params(("parallel",)),
    )(where, g, r)


def _sum_smalls(gathered):
    n = len(gathered)

    def body(*refs):
        for all_ref, o_ref in zip(refs[:n], refs[n:]):
            acc = all_ref[0]
            for dev in range(1, N_DEV):
                acc = acc + all_ref[dev]
            o_ref[...] = acc

    vm = pl.BlockSpec(memory_space=pltpu.VMEM)
    return pl.pallas_call(
        body, name="sum_smalls", out_shape=[SDS(a.shape[1:], F32) for a in gathered],
        in_specs=[vm] * n, out_specs=[vm] * n,
    )(*gathered)


def _adam_step(g, w, m, v):
    nm = ADAM_B1 * m + (1.0 - ADAM_B1) * g
    nv = ADAM_B2 * v + (1.0 - ADAM_B2) * (g * g)
    m_hat = nm / (1.0 - ADAM_B1 ** ADAM_STEP)
    v_hat = nv / (1.0 - ADAM_B2 ** ADAM_STEP)
    return -ADAM_LR * (m_hat / (jnp.sqrt(v_hat) + ADAM_EPS) + ADAM_WD * w), nm, nv


def _adamw(g, w, m, v, name):
    rows, cols = g.shape
    tr = min(rows, 256)

    def body(g_ref, w_ref, m_ref, v_ref, d_ref, nm_ref, nv_ref):
        d_ref[...], nm_ref[...], nv_ref[...] = _adam_step(g_ref[...], w_ref[...], m_ref[...], v_ref[...])

    spec = pl.BlockSpec((tr, cols), lambda i: (i, 0))
    return pl.pallas_call(
        body, name=name, grid=(rows // tr,), in_specs=[spec] * 4, out_specs=[spec] * 3,
        out_shape=[SDS(g.shape, F32)] * 3, compiler_params=_params(("parallel",)),
    )(g, w, m, v)


def _small_update(chip, tot, tot_rel, wmv):
    names = list(SMALL_PLACES)
    n = len(names)

    def body(chip_ref, tot_ref, quarter_ref, rel_ref, *refs):
        del chip_ref
        ins, outs = refs[:3 * n], refs[3 * n:]
        for i, nm in enumerate(names):
            source, row, (rows, cols) = SMALL_PLACES[nm]
            g = {"rows": tot_ref, "quarter": quarter_ref, "rel": rel_ref}[source][row:row + rows, 0:cols]
            outs[4 * i][...] = g
            outs[4 * i + 1][...], outs[4 * i + 2][...], outs[4 * i + 3][...] = _adam_step(
                g, ins[3 * i][...], ins[3 * i + 1][...], ins[3 * i + 2][...])

    whole = lambda shape: pl.BlockSpec(shape, lambda i, c: (0,) * len(shape))
    shapes = [SMALL_PLACES[nm][2] for nm in names]
    outs = pl.pallas_call(
        body, name="small_update",
        grid_spec=pltpu.PrefetchScalarGridSpec(
            num_scalar_prefetch=1, grid=(1,),
            in_specs=[whole(tot.shape), pl.BlockSpec((tot.shape[0], D // 4), lambda i, c: (0, c[0])),
                      whole(tot_rel.shape)] + [whole(shp) for shp in shapes for _ in range(3)],
            out_specs=[whole(shp) for shp in shapes for _ in range(4)]),
        out_shape=[SDS(shp, F32) for shp in shapes for _ in range(4)],
    )(chip, tot, tot, tot_rel, *[a for nm in names for a in wmv[nm]])
    return {nm: tuple(outs[4 * i:4 * i + 4]) for i, nm in enumerate(names)}


def _pad_rows(a, rows):
    return jnp.concatenate([a, jnp.zeros((rows - a.shape[0], a.shape[1]), a.dtype)], axis=0)


def _pad_cols(a, cols):
    return jnp.concatenate([a, jnp.zeros((a.shape[0], cols - a.shape[1]), a.dtype)], axis=1)


def kernel(x, a_pre_norm, a_w_in, a_conv_w, a_w_out, a_post_norm, kv_norm, w_kv, rel_bias, b_pre_norm, b_w_in, b_sinks, b_w_out, b_post_norm, loss_target, m_a_pre_norm, m_a_w_in, m_a_conv_w, m_a_w_out, m_a_post_norm, m_kv_norm, m_w_kv, m_rel_bias, m_b_pre_norm, m_b_w_in, m_b_sinks, m_b_w_out, m_b_post_norm, v_a_pre_norm, v_a_w_in, v_a_conv_w, v_a_w_out, v_a_post_norm, v_kv_norm, v_w_kv, v_rel_bias, v_b_pre_norm, v_b_w_in, v_b_sinks, v_b_w_out, v_b_post_norm):
    seq = x.shape[1]
    xs = x.reshape(seq, D)
    tgt = loss_target.reshape(seq, D)
    chip = 2 * lax.axis_index("x") + lax.axis_index("y")
    core = lax.axis_index("c")
    tm = _tile(seq, 512)
    tmw = _tile(seq, 1024)

    shards = [a_w_in[0], a_w_out[0], w_kv, b_w_in[0], b_w_out[0]]
    small_w = _pad_rows(jnp.concatenate([a_pre_norm, a_conv_w[0], a_post_norm], axis=0), 8)
    *own_only, small_g = _prepare_weights(shards, small_w)
    where = jnp.stack([core, chip]).astype(jnp.int32)
    small_full = small_g.transpose(1, 0, 2).reshape(8, D)
    g_apre, conv_w, g_apost = small_full[0:1], _pad_rows(small_full[1:4], 8), small_full[4:5]
    g_kv = kv_norm.reshape(1, D)

    proj, n1, (win_g, wouta_g, wkv_g, wbin_g, woutb_g) = _a_in(where[1:2], xs, g_apre, own_only, tmw)
    wouta = wouta_g.reshape(D, D)
    wkv = wkv_g.reshape(D, 2 * KV_W)
    woutb = woutb_g.reshape(D, D)
    ya, oa, h1, conv = _a_mix(proj, xs, conv_w, wouta, g_apost, tm)
    kv, q, zb = _b_in(h1, g_kv, b_pre_norm, wkv, wbin_g, tmw)
    tab = _bias_table(rel_bias, b_sinks.reshape(N_HEADS))
    att, stats = _attn_fwd(q, kv, tab)
    dh2, dqz, datt, loss_acc, dg_bpost, dw_outb, dw_outb16 = _mid(att, zb, h1, tgt, woutb, b_post_norm, tm)

    dqz, dkv, dtab = _attn_bwd(q, kv, datt, stats, tab, dqz)
    dh1, doa, dg_b, dw_bin, dw_kv, dw_bin16, dw_kv16 = _b_bwd(dqz, dkv, h1, dh2, oa, wbin_g, wkv, g_kv, b_pre_norm,
                                                              g_apost, tm)
    by_chip = lambda a, cols: a.reshape(N_CHIPS, D // 4, cols)
    grads1 = [by_chip(dw_kv, 2 * KV_W), dw_bin, by_chip(dw_outb, D)]
    sent1 = [by_chip(dw_kv16, 2 * KV_W), dw_bin16, by_chip(dw_outb16, D)]
    names1 = ["w_kv", "b_w_in", "b_w_out"]
    dproj, dconv_w, dw_outa, dw_outa16, from_devices1 = _a_bwd(doa, ya, conv, proj, conv_w, wouta, tm, sent1)
    shards1 = [_add_devices(where, g, r, "add_devices_" + nm) for g, r, nm in zip(grads1, from_devices1, names1)]
    tmw2 = _tile(seq, 4096)
    win_lo, win_lo16, outa_got = _dw_in_half(n1, dproj, 0, tmw2, "dw_a_in_lo", to_devices=by_chip(dw_outa16, D))
    win_hi, win_hi16, win_got = _dw_in_half(n1, dproj, 1, tmw2, "dw_a_in_hi", to_owners=win_lo16)
    nt = seq // tmw
    dn_first, win_got = _a_in_bwd_matmul(dproj, win_g, tmw, max(nt - max(nt // 4, 1), 1), win_hi16, win_got)
    grad_x, dg_apre = _a_in_bwd(dn_first, dproj, xs, dh1, win_g, g_apre, tm)
    shards2 = [_add_win(where, win_lo, win_hi, win_got, "add_devices_a_w_in"),
               _add_devices(where, by_chip(dw_outa, D), outa_got, "add_devices_a_w_out")]
    drel, dsink = _bias_fold(dtab)

    smalls = jnp.concatenate([
        dg_apre[0:1], dg_b[2:3], dg_b[0:1], dg_b[1:2], dg_bpost[0:1], _pad_cols(dsink[0:1], D),
        _pad_cols(loss_acc[0:1], D), jnp.zeros((1, D), F32), dconv_w], axis=0)
    assert smalls.shape == (SMALL_ROWS, D)
    (g_wkv, g_wbin, g_woutb, g_win, g_wouta), gathered = _share_and_gather(shards1 + shards2, (smalls, drel))
    tot, tot_rel = _sum_smalls(gathered)

    big = {}
    for nm, g, w, m, v in [("a_w_in", g_win, a_w_in, m_a_w_in, v_a_w_in), ("a_w_out", g_wouta, a_w_out, m_a_w_out, v_a_w_out),
                           ("w_kv", g_wkv, w_kv, m_w_kv, v_w_kv), ("b_w_in", g_wbin, b_w_in, m_b_w_in, v_b_w_in),
                           ("b_w_out", g_woutb, b_w_out, m_b_w_out, v_b_w_out)]:
        shp = w.shape
        two = (shp[-2], shp[-1])
        d, nm_, nv_ = _adamw(g, w.reshape(two), m.reshape(two), v.reshape(two), "adamw_" + nm)
        big[nm] = (g.reshape(shp), d.reshape(shp), nm_.reshape(shp), nv_.reshape(shp))

    given = {"a_pre_norm": (a_pre_norm, m_a_pre_norm, v_a_pre_norm), "a_conv_w": (a_conv_w, m_a_conv_w, v_a_conv_w),
             "a_post_norm": (a_post_norm, m_a_post_norm, v_a_post_norm), "kv_norm": (kv_norm, m_kv_norm, v_kv_norm),
             "rel_bias": (rel_bias, m_rel_bias, v_rel_bias), "b_pre_norm": (b_pre_norm, m_b_pre_norm, v_b_pre_norm),
             "b_sinks": (b_sinks, m_b_sinks, v_b_sinks), "b_post_norm": (b_post_norm, m_b_post_norm, v_b_post_norm)}
    small = _small_update(where[1:2], tot, tot_rel, {nm: tuple(a.reshape(SMALL_PLACES[nm][2]) for a in wmv)
                                            for nm, wmv in given.items()})
    order = ["a_pre_norm", "a_w_in", "a_conv_w", "a_w_out", "a_post_norm", "kv_norm", "w_kv", "rel_bias",
             "b_pre_norm", "b_w_in", "b_sinks", "b_w_out", "b_post_norm"]
    outs = []
    for which in range(4):
        for nm in order:
            outs.append(big[nm][which] if nm in big else small[nm][which].reshape(given[nm][0].shape))
    loss = 0.5 * tot[LOSS_ROW, 0]
    return (loss, grad_x.reshape(x.shape), *outs)
```

```python
import math

import jax
import jax.numpy as jnp
from jax import lax
from jax.experimental import pallas as pl
from jax.experimental.pallas import tpu as pltpu

F32 = jnp.float32
BF16 = jnp.bfloat16
MESH = pl.DeviceIdType.MESH
SDS = jax.ShapeDtypeStruct

D = 1024
HEAD_DIM = 64
N_HEADS = 16
N_KV = 2
GROUP = 8
KV_W = 128
BLK = 128
N_BUCKETS = 32
MAX_EXACT = 16
MAX_DISTANCE = 128
EPS = 1e-6
NEG_INF = -1e30
Q_SCALE = HEAD_DIM ** -0.5

ADAM_LR = 0.001
ADAM_B1 = 0.9
ADAM_B2 = 0.999
ADAM_EPS = 1e-08
ADAM_WD = 0.01
ADAM_STEP = 10

N_CHIPS = 4
N_DEV = 8
BIN_COLS = 2 * D // N_CHIPS
VMEM_LIMIT = 56 * 1024 * 1024
SMALL_ROWS = 16
LOSS_ROW = 6
SMALL_PLACES = {
    "a_pre_norm": ("quarter", 0, (1, D // 4)), "a_conv_w": ("quarter", 8, (3, D // 4)),
    "a_post_norm": ("quarter", 1, (1, D // 4)), "kv_norm": ("rows", 2, (1, D)),
    "rel_bias": ("rel", 0, (N_BUCKETS, N_HEADS)), "b_pre_norm": ("rows", 3, (1, D)),
    "b_sinks": ("rows", 5, (1, N_HEADS)), "b_post_norm": ("rows", 4, (1, D)),
}


def _bucket_thresholds():
    def bucket(d):
        big = MAX_EXACT + int(math.log(d / MAX_EXACT) / math.log(MAX_DISTANCE / MAX_EXACT)
                              * (N_BUCKETS - MAX_EXACT))
        return d if d < MAX_EXACT else min(big, N_BUCKETS - 1)
    out = []
    for b in range(MAX_EXACT + 1, N_BUCKETS):
        out.append(min(d for d in range(MAX_EXACT, MAX_DISTANCE) if bucket(d) >= b))
    return tuple(out)


BUCKET_THRESHOLDS = _bucket_thresholds()


def _params(semantics=None, vmem=VMEM_LIMIT):
    return pltpu.CompilerParams(dimension_semantics=semantics, vmem_limit_bytes=vmem)


def _tile(n, pref):
    return pref if n >= 2 * pref else max(n // 2, 8)


def _rms_scale(v):
    return lax.rsqrt(jnp.mean(v * v, axis=-1, keepdims=True) + EPS)


def _nt(a, b):
    return lax.dot_general(a, b, (((1,), (1,)), ((), ())), preferred_element_type=F32)


def _tn(a, b):
    return lax.dot_general(a, b, (((0,), (0,)), ((), ())), preferred_element_type=F32)


def _nn(a, b):
    return jnp.dot(a, b, preferred_element_type=F32)


def _silu_parts(z):
    sg = jax.nn.sigmoid(z)
    return sg, z * sg


def _dsilu(z, sg):
    return sg * (1.0 + z * (1.0 - sg))


def _write_gradient(acc, out32, out16, stage, sem):
    whole = pltpu.make_async_copy(acc, out32, sem)
    whole.start()
    rows = stage.shape[0]
    for k in range(acc.shape[0] // rows):
        stage[...] = acc[rows * k:rows * (k + 1), :].astype(BF16)
        pltpu.sync_copy(stage, out16.at[pl.ds(rows * k, rows)])
    whole.wait()


def _acc_row(ref, row, val):
    ref[row:row + 1, :] += val


def _gather_copies(outs, splits, ici_send, ici_recv, d2d_send, d2d_recv):
    x, y, c = lax.axis_index("x"), lax.axis_index("y"), lax.axis_index("c")
    k = 2 * x + y
    sibling = (x, y, 1 - c)

    def part(o_ref, chip, core, split):
        if not split:
            return o_ref.at[chip]
        h = o_ref.shape[1] // 2
        return o_ref.at[chip, pl.ds(pl.multiple_of(core * h, 16), h)]

    def remote(ref, a, j, sems, to):
        return pltpu.make_async_remote_copy(src_ref=ref, dst_ref=ref, send_sem=sems[0].at[3 * a + j],
                                            recv_sem=sems[1].at[3 * a + j], device_id=to, device_id_type=MESH)

    copies = []
    for a, (o_ref, split) in enumerate(zip(outs, splits)):
        for j, (px, py) in enumerate([(x, 1 - y), (1 - x, y), (1 - x, 1 - y)]):
            kj = 2 * px + py
            ici, d2d = (ici_send, ici_recv), (d2d_send, d2d_recv)
            copies.append((remote(part(o_ref, k, c, split), a, j, ici, (px, py, c)),
                           remote(part(o_ref, kj, c, split), a, j, ici, (px, py, c)),
                           remote(part(o_ref, kj, c, split), a, j, d2d, sibling) if split else None,
                           remote(part(o_ref, kj, 1 - c, split), a, j, d2d, sibling) if split else None))
    return copies


def _gather_sems(n):
    return [pltpu.SemaphoreType.DMA((3 * n,)) for _ in range(4)]


def _prepare_weights(shards, small):
    n = len(shards)

    def body(*refs):
        ins, small_in = refs[:n], refs[n]
        outs, small_out = refs[n + 1:2 * n + 1], refs[2 * n + 1]
        stages, put_sem = refs[2 * n + 2:3 * n + 2], refs[3 * n + 2]
        sems = refs[3 * n + 3:]
        k = 2 * lax.axis_index("x") + lax.axis_index("y")
        puts = []
        for a, (i_ref, stage, o_ref) in enumerate(zip(ins, stages, outs)):
            stage[...] = i_ref[...].astype(BF16)
            puts.append(pltpu.make_async_copy(stage, o_ref.at[k], put_sem.at[a]))
            puts[-1].start()
        small_out[k] = small_in[...]
        copies = _gather_copies([small_out], [False], *sems)
        for send, _, _, _ in copies:
            send.start()
        for _, arrival, _, _ in copies:
            arrival.wait_recv()
        for send, _, _, _ in copies:
            send.wait_send()
        for put in puts:
            put.wait()

    vm = pl.BlockSpec(memory_space=pltpu.VMEM)
    anyspace = pl.BlockSpec(memory_space=pl.ANY)
    out_shape = [SDS((N_CHIPS,) + s.shape, BF16) for s in shards] + [SDS((N_CHIPS,) + small.shape, F32)]
    return pl.pallas_call(
        body, name="prepare_weights", out_shape=out_shape,
        in_specs=[vm] * (n + 1), out_specs=[anyspace] * n + [vm],
        scratch_shapes=[pltpu.VMEM(s.shape, BF16) for s in shards] + [pltpu.SemaphoreType.DMA((n,))] + _gather_sems(1),
        compiler_params=pltpu.CompilerParams(vmem_limit_bytes=VMEM_LIMIT),
    )(*shards, small)


def _a_in(chip, x, g_pre, weights, tm):
    s = x.shape[0]
    nt = s // tm
    n = len(weights)

    def body(chip_ref, x_ref, g_ref, *refs):
        proj_ref, n1_ref = refs[n:n + 2]
        gathered = refs[n + 2:2 * n + 2]
        wbuf, n1_all, fetch_sem = refs[2 * n + 2:2 * n + 5]
        sems = refs[2 * n + 5:]
        jj, i = pl.program_id(0), pl.program_id(1)
        copies = _gather_copies(gathered, [True] * n, *sems)

        def fetch(rel):
            slot = jnp.bitwise_xor(chip_ref[0], rel)
            return pltpu.make_async_copy(gathered[0].at[slot], wbuf.at[rel % 2], fetch_sem.at[rel % 2])

        @pl.when((jj == 0) & (i == 0))
        def _():
            fetch(0).start()
            copies[0][0].start()
            copies[1][0].start()
            fetch(0).wait()

        for rel in (1, 2, 3):
            @pl.when((jj == rel) & (i == 0))
            def _():
                fetch(rel).wait()

        @pl.when(jj == 0)
        def _():
            xv = x_ref[...]
            n1 = (xv * _rms_scale(xv) * g_ref[...]).astype(BF16)
            n1_ref[...] = n1
            n1_all[i] = n1
        proj_ref[...] = _nn(n1_all[i], wbuf[jj % 2]).astype(BF16)

        for rel in (1, 2, 3):
            @pl.when((jj == rel - 1) & (i == max(nt - 2, nt // 2)))
            def _():
                _, arrival, forward, forwarded = copies[rel - 1]
                arrival.wait_recv()
                forward.start()
                forwarded.wait_recv()
                fetch(rel).start()
                if rel == 1:
                    for send, _, _, _ in copies[2:]:
                        send.start()

        @pl.when((jj == 3) & (i == max(nt - 2, 0)))
        def _():
            for _, arrival, forward, _ in copies[3:]:
                arrival.wait_recv()
                forward.start()

        @pl.when((jj == 3) & (i == nt - 1))
        def _():
            for _, _, _, forwarded in copies[3:]:
                forwarded.wait_recv()
            for send, _, forward, _ in copies:
                forward.wait_send()
                send.wait_send()

    anyspace = pl.BlockSpec(memory_space=pl.ANY)
    proj, n1, *gathered = pl.pallas_call(
        body, name="a_in",
        grid_spec=pltpu.PrefetchScalarGridSpec(
            num_scalar_prefetch=1, grid=(4, nt),
            in_specs=[pl.BlockSpec((tm, D), lambda jj, i, c: (jnp.where(jj == 0, i, nt - 1), 0)),
                      pl.BlockSpec((1, D), lambda jj, i, c: (0, 0))] + [anyspace] * n,
            out_specs=[pl.BlockSpec((tm, D), lambda jj, i, c: (i, jnp.bitwise_xor(c[0], jj))),
                       pl.BlockSpec((tm, D), lambda jj, i, c: (jnp.where(jj == 0, i, nt - 1), 0))] + [anyspace] * n,
            scratch_shapes=[pltpu.VMEM((2, D, D), BF16), pltpu.VMEM((nt, tm, D), BF16),
                            pltpu.SemaphoreType.DMA((2,))] + _gather_sems(n)),
        out_shape=[SDS((s, 4 * D), BF16), SDS((s, D), BF16)] + [SDS(w.shape, w.dtype) for w in weights],
        input_output_aliases={3 + a: 2 + a for a in range(n)},
        compiler_params=_params(("arbitrary", "arbitrary")),
    )(chip, x, g_pre, *weights)
    return proj, n1, gathered


def _shift_rows(v, last, second_last, rows):
    v1 = jnp.where(rows >= 1, pltpu.roll(v, 1, 0), last)
    v2 = jnp.where(rows >= 2, pltpu.roll(v, 2, 0), jnp.where(rows == 1, last, second_last))
    return v1, v2


def _a_mix(proj, x, conv_w, w_out, g_post, tm):
    s = x.shape[0]

    def body(proj_ref, x_ref, cw_ref, w_ref, g_ref, ya_ref, oa_ref, h1_ref, conv_ref, carry):
        @pl.when(pl.program_id(0) == 0)
        def _():
            carry[...] = jnp.zeros_like(carry)
        v = proj_ref[:, D:2 * D].astype(F32) * proj_ref[:, 2 * D:3 * D].astype(F32)
        rows = lax.broadcasted_iota(jnp.int32, (tm, D), 0)
        before = carry[...]
        v1, v2 = _shift_rows(v, before[7:8, :], before[6:7, :], rows)
        carry[...] = v[tm - 8:tm, :]
        conv = cw_ref[0:1, :] * v2 + cw_ref[1:2, :] * v1 + cw_ref[2:3, :] * v
        conv_ref[...] = conv.astype(BF16)
        _, sz = _silu_parts(proj_ref[:, 3 * D:4 * D].astype(F32))
        ya = (proj_ref[:, 0:D].astype(F32) * conv * sz).astype(BF16)
        ya_ref[...] = ya
        oa = _nn(ya, w_ref[...])
        oa_ref[...] = oa.astype(BF16)
        h1_ref[...] = x_ref[...] + oa * _rms_scale(oa) * g_ref[...]

    row = lambda i: (i, 0)
    fix = lambda i: (0, 0)
    return pl.pallas_call(
        body, name="a_mix", grid=(s // tm,),
        in_specs=[pl.BlockSpec((tm, 4 * D), row), pl.BlockSpec((tm, D), row), pl.BlockSpec((8, D), fix),
                  pl.BlockSpec((D, D), fix), pl.BlockSpec((1, D), fix)],
        out_specs=[pl.BlockSpec((tm, D), row)] * 4,
        out_shape=[SDS((s, D), BF16), SDS((s, D), BF16), SDS((s, D), F32), SDS((s, D), BF16)],
        scratch_shapes=[pltpu.VMEM((8, D), F32)],
        compiler_params=_params(("arbitrary",)),
    )(proj, x, conv_w, w_out, g_post)


def _b_in(h1, g_kv, g_pre, w_kv, wbin_g, tm):
    s = h1.shape[0]

    def body(h_ref, gk_ref, gb_ref, wkv_ref, wb_ref, kv_ref, q_ref, z_ref):
        h = h_ref[...]
        hh = h * _rms_scale(h)
        nk = (hh * gk_ref[...]).astype(BF16)
        nb = (hh * gb_ref[...]).astype(BF16)
        kv_ref[...] = _nn(nk, wkv_ref[...]).astype(BF16)
        for j in range(2):
            q_ref[:, BIN_COLS * j:BIN_COLS * (j + 1)] = (_nn(nb, wb_ref[j]) * Q_SCALE).astype(BF16)
            z_ref[:, BIN_COLS * j:BIN_COLS * (j + 1)] = _nn(nb, wb_ref[2 + j]).astype(BF16)

    row = lambda i: (i, 0)
    fix = lambda i: (0, 0)
    return pl.pallas_call(
        body, name="b_in", grid=(s // tm,),
        in_specs=[pl.BlockSpec((tm, D), row), pl.BlockSpec((1, D), fix), pl.BlockSpec((1, D), fix),
                  pl.BlockSpec((D, 2 * KV_W), fix), pl.BlockSpec((N_CHIPS, D, BIN_COLS), lambda i: (0, 0, 0))],
        out_specs=[pl.BlockSpec((tm, 2 * KV_W), row), pl.BlockSpec((tm, D), row), pl.BlockSpec((tm, D), row)],
        out_shape=[SDS((s, 2 * KV_W), BF16), SDS((s, D), BF16), SDS((s, D), BF16)],
        compiler_params=_params(("parallel",)),
    )(h1, g_kv, g_pre, w_kv, wbin_g)


def _band_buckets():
    q = lax.broadcasted_iota(jnp.int32, (BLK, 2 * BLK), 0)
    k = lax.broadcasted_iota(jnp.int32, (BLK, 2 * BLK), 1)
    dist = q + BLK - k
    bucket = jnp.where(dist < MAX_EXACT, dist, MAX_EXACT)
    for t in BUCKET_THRESHOLDS:
        bucket = bucket + jnp.where(dist >= t, 1, 0)
    in_window = (dist >= 0) & (dist < BLK)
    return jnp.where(in_window, bucket, -1)


def _head_place(h):
    kh, j, e = h // GROUP, (h % GROUP) // 2, h % 2
    return kh, slice(BLK * j, BLK * (j + 1)), slice(2 * BLK * e, 2 * BLK * (e + 1))


def _bias_table(rel_bias, sinks):
    def body(rb_ref, sink_ref, tab_ref):
        bucket = _band_buckets()
        col = lax.broadcasted_iota(jnp.int32, (BLK, 2 * BLK), 1)
        for h in range(N_HEADS):
            acc = jnp.where(bucket < 0, NEG_INF, 0.0).astype(F32)
            for b in range(N_BUCKETS):
                acc = jnp.where(bucket == b, rb_ref[b, h], acc)
            acc = jnp.where(col == 0, sink_ref[h], acc)
            kh, rows, cols = _head_place(h)
            tab_ref[1, kh, rows, cols] = acc
            tab_ref[0, kh, rows, cols] = jnp.where((col > 0) & (col < BLK), NEG_INF, acc)

    return pl.pallas_call(
        body, name="bias_table", out_shape=SDS((2, N_KV, 4 * BLK, 4 * BLK), F32),
        in_specs=[pl.BlockSpec(memory_space=pltpu.SMEM), pl.BlockSpec(memory_space=pltpu.SMEM)],
        out_specs=pl.BlockSpec(memory_space=pltpu.VMEM),
    )(rel_bias, sinks)


def _bias_fold(dtab):
    def body(dtab_ref, out_ref, dsink_ref):
        bucket = _band_buckets()
        row = lax.broadcasted_iota(jnp.int32, (N_BUCKETS, 128), 0)
        lane = lax.broadcasted_iota(jnp.int32, (N_BUCKETS, 128), 1)
        row8 = lax.broadcasted_iota(jnp.int32, (8, 128), 0)
        lane8 = lax.broadcasted_iota(jnp.int32, (8, 128), 1)
        acc = jnp.zeros((N_BUCKETS, 128), F32)
        dsink = jnp.zeros((8, 128), F32)
        for h in range(N_HEADS):
            kh, rows, cols = _head_place(h)
            dt = dtab_ref[kh, rows, cols]
            for b in range(N_BUCKETS):
                val = jnp.sum(jnp.where(bucket == b, dt, 0.0))
                acc = acc + jnp.where((row == b) & (lane == h), val, 0.0)
            dsink = dsink + jnp.where((row8 == 0) & (lane8 == h), jnp.sum(dt[:, 0:1]), 0.0)
        out_ref[...] = acc
        dsink_ref[...] = dsink

    vm = pl.BlockSpec(memory_space=pltpu.VMEM)
    return pl.pallas_call(
        body, name="bias_fold", out_shape=[SDS((N_BUCKETS, 128), F32), SDS((8, 128), F32)],
        in_specs=[vm], out_specs=[vm, vm],
    )(dtab)


def _pair_operands(prev, cur):
    t = jnp.concatenate([prev, cur], axis=0).astype(F32)
    t = jnp.where(lax.broadcasted_iota(jnp.int32, t.shape, 0) == 0, 0.0, t)
    tr = pltpu.roll(t, HEAD_DIM, 1)
    lo = lax.broadcasted_iota(jnp.int32, t.shape, 1) < HEAD_DIM
    zero = jnp.zeros_like(t)
    head0 = jnp.concatenate([jnp.where(lo, t, zero), jnp.where(lo, zero, tr)], axis=0).astype(BF16)
    head1 = jnp.concatenate([jnp.where(lo, tr, zero), jnp.where(lo, zero, t)], axis=0).astype(BF16)
    return head0, head1


def _pair_fold(d0, d1):
    lo = lax.broadcasted_iota(jnp.int32, (2 * BLK, KV_W), 1) < HEAD_DIM
    zero = jnp.zeros((2 * BLK, KV_W), F32)
    g0 = jnp.where(lo, d0[0:256], zero) + pltpu.roll(jnp.where(lo, zero, d0[256:512]), HEAD_DIM, 1)
    g1 = pltpu.roll(jnp.where(lo, d1[0:256], zero), HEAD_DIM, 1) + jnp.where(lo, zero, d1[256:512])
    return jnp.where(lax.broadcasted_iota(jnp.int32, (2 * BLK, KV_W), 0) == 0, 0.0, g0 + g1)


def _stack_pairs(ref, kh):
    return jnp.concatenate([ref[:, 128 * (4 * kh + j):128 * (4 * kh + j + 1)] for j in range(4)], axis=0)


def _table_spec():
    return pl.BlockSpec((1, N_KV, 4 * BLK, 4 * BLK), lambda n: (jnp.minimum(n, 1), 0, 0, 0))


def _attn_fwd(q, kv, tab):
    s = q.shape[0]

    def body(q_ref, kp_ref, kc_ref, vp_ref, vc_ref, tab_ref, att_ref, stats_ref):
        k2 = _pair_operands(kp_ref[...], kc_ref[...])
        v2 = _pair_operands(vp_ref[...], vc_ref[...])
        lane = lax.broadcasted_iota(jnp.int32, (BLK, 128), 1)
        stats = jnp.zeros((BLK, 128), F32)
        for kh in range(N_KV):
            sc = _nt(_stack_pairs(q_ref, kh), k2[kh])
            ps = []
            for e in range(2):
                lg = sc[:, 256 * e:256 * (e + 1)] + tab_ref[0, kh, :, 256 * e:256 * (e + 1)]
                m = jnp.max(lg, axis=-1, keepdims=True)
                ex = jnp.exp(lg - m)
                den = jnp.sum(ex, axis=-1, keepdims=True)
                ps.append(ex * (1.0 / den))
                lse = m + jnp.log(den)
                for j in range(4):
                    stats = jnp.where(lane == GROUP * kh + 2 * j + e, lse[BLK * j:BLK * (j + 1)], stats)
            out = _nn(jnp.concatenate(ps, axis=1).astype(BF16), v2[kh])
            for j in range(4):
                att_ref[:, 128 * (4 * kh + j):128 * (4 * kh + j + 1)] = out[BLK * j:BLK * (j + 1)].astype(BF16)
        stats_ref[...] = stats

    cur = lambda n: (n, 0)
    prev = lambda n: (jnp.maximum(n - 1, 0), 0)
    return pl.pallas_call(
        body, name="attn_fwd", grid=(s // BLK,),
        in_specs=[pl.BlockSpec((BLK, D), cur),
                  pl.BlockSpec((BLK, KV_W), prev), pl.BlockSpec((BLK, KV_W), cur),
                  pl.BlockSpec((BLK, KV_W), lambda n: (jnp.maximum(n - 1, 0), 1)),
                  pl.BlockSpec((BLK, KV_W), lambda n: (n, 1)), _table_spec()],
        out_specs=[pl.BlockSpec((BLK, D), cur), pl.BlockSpec((BLK, 128), cur)],
        out_shape=[SDS((s, D), BF16), SDS((s, 128), F32)],
        compiler_params=_params(("parallel",)),
    )(q, kv, kv, kv, kv, tab)


def _mid(att, zb, h1, tgt, w_out, g_post, tm):
    s = att.shape[0]
    nt = s // tm

    def body(att_ref, z_ref, h1_ref, t_ref, w_ref, g_ref,
             dh_ref, dqz_ref, datt_ref, loss_ref, dg_ref, dw_ref, dw16_ref, dw_acc, stage, put_sem):
        @pl.when(pl.program_id(0) == 0)
        def _():
            loss_ref[...] = jnp.zeros_like(loss_ref)
            dg_ref[...] = jnp.zeros_like(dg_ref)
            dw_acc[...] = jnp.zeros_like(dw_acc)
        att = att_ref[...].astype(F32)
        z = z_ref[...].astype(F32)
        sg, sz = _silu_parts(z)
        ob = (att * sz).astype(BF16)
        y2 = _nn(ob, w_ref[...])
        r2 = _rms_scale(y2)
        yh = y2 * r2
        g = g_ref[...]
        err = (h1_ref[...] + yh * g) - t_ref[...]
        loss_ref[...] += jnp.sum(jnp.sum(err * err, axis=-1, keepdims=True) / D)
        dh = err / D
        dh_ref[...] = dh
        _acc_row(dg_ref, 0, jnp.sum(dh * yh, axis=0, keepdims=True))
        dyh = dh * g
        dy = (r2 * (dyh - yh * jnp.mean(dyh * yh, axis=-1, keepdims=True))).astype(BF16)
        dw_acc[...] += _tn(ob, dy)
        dob = _nt(dy, w_ref[...])
        datt_ref[...] = (dob * sz).astype(BF16)
        dqz_ref[...] = (dob * att * _dsilu(z, sg)).astype(BF16)

        @pl.when(pl.program_id(0) == nt - 1)
        def _():
            _write_gradient(dw_acc, dw_ref, dw16_ref, stage, put_sem)

    row = lambda i: (i, 0)
    fix = lambda i: (0, 0)
    anyspace = pl.BlockSpec(memory_space=pl.ANY)
    return pl.pallas_call(
        body, name="mid", grid=(nt,),
        in_specs=[pl.BlockSpec((tm, D), row)] * 4 + [pl.BlockSpec((D, D), fix), pl.BlockSpec((1, D), fix)],
        out_specs=[pl.BlockSpec((tm, D), row), pl.BlockSpec((tm, D), lambda i: (i, 1)), pl.BlockSpec((tm, D), row),
                   pl.BlockSpec((8, 128), fix), pl.BlockSpec((8, D), fix), anyspace, anyspace],
        out_shape=[SDS((s, D), F32), SDS((s, 2 * D), BF16), SDS((s, D), BF16), SDS((8, 128), F32),
                   SDS((8, D), F32), SDS((D, D), F32), SDS((D, D), BF16)],
        scratch_shapes=[pltpu.VMEM((D, D), F32), pltpu.VMEM((D // 4, D), BF16), pltpu.SemaphoreType.DMA],
        compiler_params=_params(("arbitrary",)),
    )(att, zb, h1, tgt, w_out, g_post)


def _attn_bwd(q, kv, datt, stats, tab, dqz):
    s = q.shape[0]
    nb = s // BLK

    def body(q_ref, kp_ref, kc_ref, vp_ref, vc_ref, da_ref, st_ref, tab_ref, dqz_in,
             dq_ref, dkv_ref, dtab_ref, dk_carry, dv_carry):
        del dqz_in
        n = pl.program_id(0)

        @pl.when(n == 0)
        def _():
            dtab_ref[...] = jnp.zeros_like(dtab_ref)
            dk_carry[...] = jnp.zeros_like(dk_carry)
            dv_carry[...] = jnp.zeros_like(dv_carry)

        @pl.when(n < nb)
        def _():
            k2 = _pair_operands(kp_ref[...], kc_ref[...])
            v2 = _pair_operands(vp_ref[...], vc_ref[...])
            lane = lax.broadcasted_iota(jnp.int32, (BLK, 128), 1)
            stats = st_ref[...]
            dk2, dv2 = [], []
            for kh in range(N_KV):
                qs = _stack_pairs(q_ref, kh)
                das = _stack_pairs(da_ref, kh)
                sc = _nt(qs, k2[kh])
                dp = _nt(das, v2[kh])
                ps, dss = [], []
                for e in range(2):
                    heads = [GROUP * kh + 2 * j + e for j in range(4)]
                    lse = jnp.concatenate([jnp.sum(jnp.where(lane == h, stats, 0.0), axis=-1, keepdims=True)
                                           for h in heads], axis=0)
                    cols = slice(256 * e, 256 * (e + 1))
                    p = jnp.exp(sc[:, cols] + tab_ref[0, kh, :, cols] - lse)
                    delta = jnp.sum(p * dp[:, cols], axis=-1, keepdims=True)
                    ds = p * (dp[:, cols] - delta)
                    dtab_ref[kh, :, cols] += ds
                    ps.append(p)
                    dss.append(ds)
                p2 = jnp.concatenate(ps, axis=1).astype(BF16)
                ds2 = jnp.concatenate(dss, axis=1).astype(BF16)
                dq = _nn(ds2, k2[kh]) * Q_SCALE
                for j in range(4):
                    dq_ref[:, 128 * (4 * kh + j):128 * (4 * kh + j + 1)] = dq[BLK * j:BLK * (j + 1)].astype(BF16)
                dk2.append(_tn(ds2, qs))
                dv2.append(_tn(p2, das))
            dkk = _pair_fold(dk2[0], dk2[1])
            dvv = _pair_fold(dv2[0], dv2[1])
            dkv_ref[:, 0:KV_W] = (dk_carry[...] + dkk[0:BLK]).astype(BF16)
            dkv_ref[:, KV_W:2 * KV_W] = (dv_carry[...] + dvv[0:BLK]).astype(BF16)
            dk_carry[...] = dkk[BLK:2 * BLK]
            dv_carry[...] = dvv[BLK:2 * BLK]

        @pl.when(n == nb)
        def _():
            dkv_ref[:, 0:KV_W] = dk_carry[...].astype(BF16)
            dkv_ref[:, KV_W:2 * KV_W] = dv_carry[...].astype(BF16)

    cur = lambda n: (jnp.minimum(n, nb - 1), 0)
    prev = lambda n: (jnp.clip(n - 1, 0, nb - 1), 0)
    return pl.pallas_call(
        body, name="attn_bwd", grid=(nb + 1,),
        in_specs=[pl.BlockSpec((BLK, D), cur),
                  pl.BlockSpec((BLK, KV_W), prev), pl.BlockSpec((BLK, KV_W), cur),
                  pl.BlockSpec((BLK, KV_W), lambda n: (jnp.clip(n - 1, 0, nb - 1), 1)),
                  pl.BlockSpec((BLK, KV_W), lambda n: (jnp.minimum(n, nb - 1), 1)),
                  pl.BlockSpec((BLK, D), cur), pl.BlockSpec((BLK, 128), cur), _table_spec(),
                  pl.BlockSpec(memory_space=pl.ANY)],
        out_specs=[pl.BlockSpec((BLK, D), cur), pl.BlockSpec((BLK, 2 * KV_W), prev),
                   pl.BlockSpec((N_KV, 4 * BLK, 4 * BLK), lambda n: (0, 0, 0))],
        out_shape=[SDS((s, 2 * D), BF16), SDS((s, 2 * KV_W), BF16), SDS((N_KV, 4 * BLK, 4 * BLK), F32)],
        scratch_shapes=[pltpu.VMEM((BLK, KV_W), F32), pltpu.VMEM((BLK, KV_W), F32)],
        input_output_aliases={8: 0},
        compiler_params=_params(("arbitrary",)),
    )(q, kv, kv, kv, kv, datt, stats, tab, dqz)


def _b_bwd(dqz, dkv, h1, dh2, oa, wbin_g, w_kv, g_kv, g_pre, g_apost, tm):
    s = h1.shape[0]
    nt = s // tm

    def body(dqz_ref, dkv_ref, h_ref, dh2_ref, oa_ref, wb_ref, wkv_ref, gk_ref, gb_ref, ga_ref,
             dh1_ref, doa_ref, dg_ref, dwb_ref, dwkv_ref, dwb16_ref, dwkv16_ref, wcat, dwb_acc, dwkv_acc, put_sem):
        @pl.when(pl.program_id(0) == 0)
        def _():
            dg_ref[...] = jnp.zeros_like(dg_ref)
            dwb_acc[...] = jnp.zeros_like(dwb_acc)
            dwkv_acc[...] = jnp.zeros_like(dwkv_acc)
            for j in range(N_CHIPS):
                pltpu.sync_copy(wb_ref.at[j], wcat.at[:, pl.ds(BIN_COLS * j, BIN_COLS)])
        dnb = _nt(dqz_ref[...], wcat[...])
        dnk = _nt(dkv_ref[...], wkv_ref[...])
        h = h_ref[...]
        r = _rms_scale(h)
        hh = h * r
        dwb_acc[...] += _tn((hh * gb_ref[...]).astype(BF16), dqz_ref[...])
        dwkv_acc[...] += _tn((hh * gk_ref[...]).astype(BF16), dkv_ref[...])
        _acc_row(dg_ref, 0, jnp.sum(dnk * hh, axis=0, keepdims=True))
        _acc_row(dg_ref, 1, jnp.sum(dnb * hh, axis=0, keepdims=True))
        dhh = dnb * gb_ref[...] + dnk * gk_ref[...]
        dh1 = dh2_ref[...] + r * (dhh - hh * jnp.mean(dhh * hh, axis=-1, keepdims=True))
        dh1_ref[...] = dh1
        oa = oa_ref[...].astype(F32)
        ra = _rms_scale(oa)
        oh = oa * ra
        _acc_row(dg_ref, 2, jnp.sum(dh1 * oh, axis=0, keepdims=True))
        doh = dh1 * ga_ref[...]
        doa_ref[...] = (ra * (doh - oh * jnp.mean(doh * oh, axis=-1, keepdims=True))).astype(BF16)

        @pl.when(pl.program_id(0) == nt - 1)
        def _():
            wcat[...] = dwb_acc[...].astype(BF16)
            puts = [pltpu.make_async_copy(dwkv_acc, dwkv_ref, put_sem.at[2 * N_CHIPS])]
            for j in range(N_CHIPS):
                cols = pl.ds(BIN_COLS * j, BIN_COLS)
                puts.append(pltpu.make_async_copy(dwb_acc.at[:, cols], dwb_ref.at[j], put_sem.at[2 * j]))
                puts.append(pltpu.make_async_copy(wcat.at[:, cols], dwb16_ref.at[j], put_sem.at[2 * j + 1]))
            for put in puts:
                put.start()
            for put in puts:
                put.wait()
            wcat[:, 0:2 * KV_W] = dwkv_acc[...].astype(BF16)
            pltpu.sync_copy(wcat.at[:, pl.ds(0, 2 * KV_W)], dwkv16_ref)

    row = lambda i: (i, 0)
    fix = lambda i: (0, 0)
    anyspace = pl.BlockSpec(memory_space=pl.ANY)
    return pl.pallas_call(
        body, name="b_bwd", grid=(nt,),
        in_specs=[pl.BlockSpec((tm, 2 * D), row), pl.BlockSpec((tm, 2 * KV_W), row), pl.BlockSpec((tm, D), row),
                  pl.BlockSpec((tm, D), row), pl.BlockSpec((tm, D), row), anyspace, pl.BlockSpec((D, 2 * KV_W), fix),
                  pl.BlockSpec((1, D), fix), pl.BlockSpec((1, D), fix), pl.BlockSpec((1, D), fix)],
        out_specs=[pl.BlockSpec((tm, D), row), pl.BlockSpec((tm, D), row), pl.BlockSpec((8, D), fix)] + [anyspace] * 4,
        out_shape=[SDS((s, D), F32), SDS((s, D), BF16), SDS((8, D), F32), SDS((N_CHIPS, D, BIN_COLS), F32),
                   SDS((D, 2 * KV_W), F32), SDS((N_CHIPS, D, BIN_COLS), BF16), SDS((D, 2 * KV_W), BF16)],
        scratch_shapes=[pltpu.VMEM((D, 2 * D), BF16), pltpu.VMEM((D, 2 * D), F32), pltpu.VMEM((D, 2 * KV_W), F32),
                        pltpu.SemaphoreType.DMA((2 * N_CHIPS + 1,))],
        compiler_params=_params(("arbitrary",)),
    )(dqz, dkv, h1, dh2, oa, wbin_g, w_kv, g_kv, g_pre, g_apost)


def _to_owner_core(pieces, r, send, recv, core, action):
    x, y, c = lax.axis_index("x"), lax.axis_index("y"), lax.axis_index("c")
    for kp in range(N_CHIPS):
        px, py = kp >> 1, kp & 1
        rel = 4 * (x + px - 2 * x * px) + 2 * (y + py - 2 * y * py) + (c + core - 2 * c * core)

        @pl.when(rel != 0)
        def _():
            cp = pltpu.make_async_remote_copy(src_ref=pieces.at[kp], dst_ref=r.at[rel - 1], send_sem=send.at[kp],
                                              recv_sem=recv.at[rel - 1], device_id=(px, py, core), device_id_type=MESH)
            if action == "start":
                cp.start()
            else:
                cp.wait_send()
    if action == "wait":
        @pl.when(c == core)
        def _():
            for rel in range(1, N_DEV):
                pltpu.make_async_remote_copy(src_ref=pieces.at[0], dst_ref=r.at[rel - 1], send_sem=send.at[0],
                                             recv_sem=recv.at[rel - 1], device_id=(x, y, c),
                                             device_id_type=MESH).wait_recv()


def _owner_core_sems():
    return [pltpu.SemaphoreType.DMA((N_CHIPS,)), pltpu.SemaphoreType.DMA((N_DEV - 1,))]


def _device_exchange(grads, recvs, send, recv):
    x, y, c = lax.axis_index("x"), lax.axis_index("y"), lax.axis_index("c")
    copies = []
    for a, (g, r) in enumerate(zip(grads, recvs)):
        h = g.shape[1] // 2
        for rel in range(1, N_DEV):
            fx, fy, fc = rel >> 2, (rel >> 1) & 1, rel & 1
            px, py, pc = x + fx - 2 * x * fx, y + fy - 2 * y * fy, c + fc - 2 * c * fc
            sem = (N_DEV - 1) * a + rel - 1
            copies.append(pltpu.make_async_remote_copy(
                src_ref=g.at[2 * px + py, pl.ds(pl.multiple_of(pc * h, 16), h)], dst_ref=r.at[rel - 1],
                send_sem=send.at[sem], recv_sem=recv.at[sem], device_id=(px, py, pc), device_id_type=MESH))
    return copies


def _device_exchange_specs(grads):
    anyspace = pl.BlockSpec(memory_space=pl.ANY)
    n = len(grads)
    count = (N_DEV - 1) * n
    return ([anyspace] * n, [anyspace] * n,
            [SDS((N_DEV - 1, g.shape[1] // 2, g.shape[2]), g.dtype) for g in grads],
            [pltpu.SemaphoreType.DMA((count,)), pltpu.SemaphoreType.DMA((count,))])


def _a_bwd(doa, ya, conv, proj, conv_w, w_out, tm, parts):
    s = doa.shape[0]
    nt = s // tm
    n = len(parts)
    ex_in, ex_out, ex_shape, ex_sems = _device_exchange_specs(parts)

    def body(*refs):
        doa_ref, ya_ref, conv_ref, proj_ref, cw_ref, w_ref = refs[:6]
        part_refs = refs[6:6 + n]
        dproj_ref, dcw_ref, dw_ref, dw16_ref = refs[6 + n:10 + n]
        recv_refs = refs[10 + n:10 + 2 * n]
        carry, dw_acc, stage, put_sem, send, recv = refs[10 + 2 * n:]
        i = pl.program_id(0)

        @pl.when(i == 0)
        def _():
            dcw_ref[...] = jnp.zeros_like(dcw_ref)
            carry[...] = jnp.zeros_like(carry)
            dw_acc[...] = jnp.zeros_like(dw_acc)
            for cp in _device_exchange(part_refs, recv_refs, send, recv):
                cp.start()
        dya = _nt(doa_ref[...], w_ref[...])
        dw_acc[...] += _tn(ya_ref[...], doa_ref[...])
        bg = proj_ref[:, 0:D].astype(F32)
        cg = proj_ref[:, D:2 * D].astype(F32)
        u = proj_ref[:, 2 * D:3 * D].astype(F32)
        z = proj_ref[:, 3 * D:4 * D].astype(F32)
        v = cg * u
        rows = lax.broadcasted_iota(jnp.int32, (tm, D), 0)
        conv = conv_ref[...].astype(F32)
        sg, sz = _silu_parts(z)
        dproj_ref[:, 0:D] = (dya * conv * sz).astype(BF16)
        dproj_ref[:, 3 * D:4 * D] = (dya * bg * conv * _dsilu(z, sg)).astype(BF16)
        dconv = dya * bg * sz
        after = carry[...]
        up1 = jnp.where(rows < tm - 1, pltpu.roll(dconv, tm - 1, 0), after[0:1, :])
        up2 = jnp.where(rows < tm - 2, pltpu.roll(dconv, tm - 2, 0),
                        jnp.where(rows == tm - 2, after[0:1, :], after[1:2, :]))
        carry[...] = dconv[0:8, :]
        _acc_row(dcw_ref, 0, jnp.sum(up2 * v, axis=0, keepdims=True))
        _acc_row(dcw_ref, 1, jnp.sum(up1 * v, axis=0, keepdims=True))
        _acc_row(dcw_ref, 2, jnp.sum(dconv * v, axis=0, keepdims=True))
        dv = cw_ref[2:3, :] * dconv + cw_ref[1:2, :] * up1 + cw_ref[0:1, :] * up2
        dproj_ref[:, D:2 * D] = (dv * u).astype(BF16)
        dproj_ref[:, 2 * D:3 * D] = (dv * cg).astype(BF16)

        @pl.when(i == nt - 1)
        def _():
            _write_gradient(dw_acc, dw_ref, dw16_ref, stage, put_sem)
            for cp in _device_exchange(part_refs, recv_refs, send, recv):
                cp.wait()

    rev = lambda i: (nt - 1 - i, 0)
    fix = lambda i: (0, 0)
    anyspace = pl.BlockSpec(memory_space=pl.ANY)
    dproj, dcw, dw, dw16, *got = pl.pallas_call(
        body, name="a_bwd", grid=(nt,),
        in_specs=[pl.BlockSpec((tm, D), rev), pl.BlockSpec((tm, D), rev), pl.BlockSpec((tm, D), rev),
                  pl.BlockSpec((tm, 4 * D), rev), pl.BlockSpec((8, D), fix), pl.BlockSpec((D, D), fix)] + ex_in,
        out_specs=[pl.BlockSpec((tm, 4 * D), rev), pl.BlockSpec((8, D), fix), anyspace, anyspace] + ex_out,
        out_shape=[SDS((s, 4 * D), BF16), SDS((8, D), F32), SDS((D, D), F32), SDS((D, D), BF16)] + ex_shape,
        scratch_shapes=[pltpu.VMEM((8, D), F32), pltpu.VMEM((D, D), F32), pltpu.VMEM((D // 4, D), BF16),
                        pltpu.SemaphoreType.DMA] + ex_sems,
        compiler_params=_params(("arbitrary",)),
    )(doa, ya, conv, proj, conv_w, w_out, *parts)
    return dproj, dcw, dw, dw16, got


def _dn1(dp_ref, w_ref):
    dn = _nt(dp_ref[:, 0:D], w_ref[0])
    for j in range(1, 4):
        dn = dn + _nt(dp_ref[:, D * j:D * (j + 1)], w_ref[j])
    return dn


def _a_in_bwd_matmul(dproj, win_g, tm, count, win_half, win_got):
    def body(dp_ref, w_ref, half_ref, got_in, dn_ref, got_ref, wcat, send, recv):
        del got_in

        @pl.when(pl.program_id(0) == 0)
        def _():
            _to_owner_core(half_ref, got_ref, send, recv, 1, "start")
            for j in range(N_CHIPS):
                pltpu.sync_copy(w_ref.at[j], wcat.at[:, pl.ds(D * j, D)])
        dn_ref[...] = _nt(dp_ref[...], wcat[...]).astype(BF16)

        @pl.when(pl.program_id(0) == count - 1)
        def _():
            _to_owner_core(half_ref, got_ref, send, recv, 1, "wait")

    row = lambda i: (i, 0)
    anyspace = pl.BlockSpec(memory_space=pl.ANY)
    return pl.pallas_call(
        body, name="a_in_bwd_matmul", grid=(count,),
        in_specs=[pl.BlockSpec((tm, 4 * D), row), anyspace, anyspace, anyspace],
        out_specs=[pl.BlockSpec((tm, D), row), anyspace],
        out_shape=[SDS((count * tm, D), BF16), SDS(win_got.shape, win_got.dtype)],
        scratch_shapes=[pltpu.VMEM((D, 4 * D), BF16)] + _owner_core_sems(),
        input_output_aliases={3: 1},
        compiler_params=_params(("arbitrary",)),
    )(dproj, win_g, win_half, win_got)


def _a_in_bwd(dn_first, dproj, x, dh1, win_g, g_pre, tm):
    s = x.shape[0]
    nt = s // tm
    count = dn_first.shape[0] // tm

    def body(dn_ref, dp_ref, x_ref, dh_ref, w_ref, g_ref, gx_ref, dg_ref, dn_s):
        i = pl.program_id(0)

        @pl.when(i == 0)
        def _():
            dg_ref[...] = jnp.zeros_like(dg_ref)

        @pl.when(i < count)
        def _():
            dn_s[...] = dn_ref[...].astype(F32)

        @pl.when(i >= count)
        def _():
            dn_s[...] = _dn1(dp_ref, w_ref)
        dn = dn_s[...]
        xv = x_ref[...]
        r = _rms_scale(xv)
        xh = xv * r
        _acc_row(dg_ref, 0, jnp.sum(dn * xh, axis=0, keepdims=True))
        dxh = dn * g_ref[...]
        gx_ref[...] = dh_ref[...] + r * (dxh - xh * jnp.mean(dxh * xh, axis=-1, keepdims=True))

    row = lambda i: (i, 0)
    fix = lambda i: (0, 0)
    return pl.pallas_call(
        body, name="a_in_bwd", grid=(nt,),
        in_specs=[pl.BlockSpec((tm, D), lambda i: (jnp.minimum(i, count - 1), 0)),
                  pl.BlockSpec((tm, 4 * D), lambda i: (jnp.maximum(i, count), 0)),
                  pl.BlockSpec((tm, D), row), pl.BlockSpec((tm, D), row),
                  pl.BlockSpec((4, D, D), lambda i: (0, 0, 0)), pl.BlockSpec((1, D), fix)],
        out_specs=[pl.BlockSpec((tm, D), row), pl.BlockSpec((8, D), fix)],
        out_shape=[SDS((s, D), F32), SDS((8, D), F32)],
        scratch_shapes=[pltpu.VMEM((tm, D), F32)],
        compiler_params=_params(("arbitrary",)),
    )(dn_first, dproj, x, dh1, win_g, g_pre)


def _dw_in_half(n1, dproj, core, tmw, name, to_owners=None, to_devices=None):
    s = n1.shape[0]
    h = D // 2
    nt = s // tmw
    sent_array = to_owners if to_owners is not None else to_devices
    rides = sent_array is not None
    if to_owners is not None:
        sems, got_shape = _owner_core_sems(), SDS((N_DEV - 1, h, D), BF16)
    elif to_devices is not None:
        _, _, (got_shape,), sems = _device_exchange_specs([to_devices])

    def body(*refs):
        a_ref, b_ref = refs[:2]
        o_ref, o16_ref = refs[2 + rides:4 + rides]
        j, t = pl.program_id(0), pl.program_id(1)

        def exchange(action):
            sent, got, send, recv = refs[2], refs[5], refs[6], refs[7]
            if to_owners is not None:
                _to_owner_core(sent, got, send, recv, 1 - core, action)
            else:
                for cp in _device_exchange([sent], [got], send, recv):
                    cp.start() if action == "start" else cp.wait()

        if rides:
            @pl.when((j == 0) & (t == 0))
            def _():
                exchange("start")

        @pl.when(t == 0)
        def _():
            o_ref[...] = jnp.zeros_like(o_ref)
        o_ref[0] += _tn(a_ref[...], b_ref[...])

        @pl.when(t == nt - 1)
        def _():
            o16_ref[...] = o_ref[...].astype(BF16)
        if rides:
            @pl.when((j == N_CHIPS - 1) & (t == nt - 1))
            def _():
                exchange("wait")

    anyspace = pl.BlockSpec(memory_space=pl.ANY)
    slot = pl.BlockSpec((1, h, D), lambda j, t: (j, 0, 0))
    return pl.pallas_call(
        body, name=name, grid=(N_CHIPS, nt),
        in_specs=[pl.BlockSpec((tmw, h), lambda j, t: (t, core)), pl.BlockSpec((tmw, D), lambda j, t: (t, j))]
        + [anyspace] * rides,
        out_specs=[slot, slot] + [anyspace] * rides,
        out_shape=[SDS((N_CHIPS, h, D), F32), SDS((N_CHIPS, h, D), BF16)] + ([got_shape] if rides else []),
        scratch_shapes=sems if rides else [],
        compiler_params=_params(("arbitrary", "arbitrary")),
    )(n1, dproj, *([sent_array] if rides else []))


def _share_and_gather(shards, smalls):
    n_h, n_s = len(shards), len(smalls)

    def body(*refs):
        small_ins = refs[n_h:n_h + n_s]
        fs = refs[n_h + n_s:2 * n_h + n_s]
        small_alls = refs[2 * n_h + n_s:2 * n_h + 2 * n_s]
        dsend, drecv, ssend, srecv = refs[2 * n_h + 2 * n_s:]
        x, y, c = lax.axis_index("x"), lax.axis_index("y"), lax.axis_index("c")
        sibling = (x, y, 1 - c)
        sends, arrivals = [], []
        for b, full in enumerate(fs):
            h = full.shape[0] // 2
            mine = full.at[pl.ds(pl.multiple_of(c * h, 8), h)]
            theirs = full.at[pl.ds(pl.multiple_of((1 - c) * h, 8), h)]
            sends.append(pltpu.make_async_remote_copy(src_ref=mine, dst_ref=mine, send_sem=dsend.at[b],
                                                      recv_sem=drecv.at[b], device_id=sibling, device_id_type=MESH))
            arrivals.append(pltpu.make_async_remote_copy(src_ref=mine, dst_ref=theirs, send_sem=dsend.at[b],
                                                         recv_sem=drecv.at[b], device_id=sibling, device_id_type=MESH))
        me = 4 * x + 2 * y + c
        for k, (small_in, small_all) in enumerate(zip(small_ins, small_alls)):
            small_all[me] = small_in[...]
            for rel in range(1, N_DEV):
                fx, fy, fc = rel >> 2, (rel >> 1) & 1, rel & 1
                peer = (x + fx - 2 * x * fx, y + fy - 2 * y * fy, c + fc - 2 * c * fc)
                sender = 4 * peer[0] + 2 * peer[1] + peer[2]
                sem = (N_DEV - 1) * k + rel - 1
                sends.append(pltpu.make_async_remote_copy(
                    src_ref=small_in, dst_ref=small_all.at[me], send_sem=ssend.at[sem], recv_sem=srecv.at[sem],
                    device_id=peer, device_id_type=MESH))
                arrivals.append(pltpu.make_async_remote_copy(
                    src_ref=small_in, dst_ref=small_all.at[sender], send_sem=ssend.at[sem], recv_sem=srecv.at[sem],
                    device_id=peer, device_id_type=MESH))
        for cp in sends:
            cp.start()
        for cp in arrivals:
            cp.wait_recv()
        for cp in sends:
            cp.wait_send()

    anyspace = pl.BlockSpec(memory_space=pl.ANY)
    vm = pl.BlockSpec(memory_space=pltpu.VMEM)
    out_shape = [SDS(full.shape, F32) for full in shards] + [SDS((N_DEV,) + sm.shape, F32) for sm in smalls]
    n_all = (N_DEV - 1) * n_s
    outs = pl.pallas_call(
        body, name="share_and_gather", out_shape=out_shape,
        in_specs=[anyspace] * n_h + [vm] * n_s, out_specs=[anyspace] * n_h + [vm] * n_s,
        scratch_shapes=[pltpu.SemaphoreType.DMA((n_h,)), pltpu.SemaphoreType.DMA((n_h,)),
                        pltpu.SemaphoreType.DMA((n_all,)), pltpu.SemaphoreType.DMA((n_all,))],
        input_output_aliases={b: b for b in range(n_h)},
    )(*shards, *smalls)
    return outs[:n_h], outs[n_h:]


def _add_win(where, lo, hi, r, name):
    _, h, cols = lo.shape
    tr = min(h, 256)
    nh = h // tr

    def body(where_ref, lo_ref, hi_ref, r_ref, o_ref):
        acc = jnp.where(where_ref[0] == 0, lo_ref[0], hi_ref[0])
        for k in range(N_DEV - 1):
            acc = acc + r_ref[k].astype(F32)
        o_ref[...] = acc

    own = pl.BlockSpec((1, tr, cols), lambda i, w: (w[1], i, 0))
    return pl.pallas_call(
        body, name=name,
        grid_spec=pltpu.PrefetchScalarGridSpec(
            num_scalar_prefetch=1, grid=(nh,),
            in_specs=[own, own, pl.BlockSpec((N_DEV - 1, tr, cols), lambda i, w: (0, i, 0))],
            out_specs=pl.BlockSpec((tr, cols), lambda i, w: (w[0] * nh + i, 0))),
        out_shape=SDS((2 * h, cols), F32),
        compiler_params=_params(("parallel",)),
    )(where, lo, hi, r)


def _add_devices(where, g, r, name):
    _, rows, cols = g.shape
    h = rows // 2
    tr = min(h, 256)
    nh = h // tr

    def body(where_ref, g_ref, r_ref, o_ref):
        del where_ref
        acc = g_ref[0]
        for k in range(N_DEV - 1):
            acc = acc + r_ref[k].astype(F32)
        o_ref[...] = acc

    return pl.pallas_call(
        body, name=name,
        grid_spec=pltpu.PrefetchScalarGridSpec(
            num_scalar_prefetch=1, grid=(nh,),
            in_specs=[pl.BlockSpec((1, tr, cols), lambda i, w: (w[1], w[0] * nh + i, 0)),
                      pl.BlockSpec((N_DEV - 1, tr, cols), lambda i, w: (0, i, 0))],
            out_specs=pl.BlockSpec((tr, cols), lambda i, w: (w[0] * nh + i, 0))),
        out_shape=SDS((rows, cols), F32),
        compiler_params=_params(("parallel",)),
    )(where, g, r)


def _sum_smalls(gathered):
    n = len(gathered)

    def body(*refs):
        for all_ref, o_ref in zip(refs[:n], refs[n:]):
            acc = all_ref[0]
            for dev in range(1, N_DEV):
                acc = acc + all_ref[dev]
            o_ref[...] = acc

    vm = pl.BlockSpec(memory_space=pltpu.VMEM)
    return pl.pallas_call(
        body, name="sum_smalls", out_shape=[SDS(a.shape[1:], F32) for a in gathered],
        in_specs=[vm] * n, out_specs=[vm] * n,
    )(*gathered)


def _adam_step(g, w, m, v):
    nm = ADAM_B1 * m + (1.0 - ADAM_B1) * g
    nv = ADAM_B2 * v + (1.0 - ADAM_B2) * (g * g)
    m_hat = nm / (1.0 - ADAM_B1 ** ADAM_STEP)
    v_hat = nv / (1.0 - ADAM_B2 ** ADAM_STEP)
    return -ADAM_LR * (m_hat / (jnp.sqrt(v_hat) + ADAM_EPS) + ADAM_WD * w), nm, nv


def _adamw(g, w, m, v, name):
    rows, cols = g.shape
    tr = min(rows, 256)

    def body(g_ref, w_ref, m_ref, v_ref, d_ref, nm_ref, nv_ref):
        d_ref[...], nm_ref[...], nv_ref[...] = _adam_step(g_ref[...], w_ref[...], m_ref[...], v_ref[...])

    spec = pl.BlockSpec((tr, cols), lambda i: (i, 0))
    return pl.pallas_call(
        body, name=name, grid=(rows // tr,), in_specs=[spec] * 4, out_specs=[spec] * 3,
        out_shape=[SDS(g.shape, F32)] * 3, compiler_params=_params(("parallel",)),
    )(g, w, m, v)


def _small_update(chip, tot, tot_rel, wmv):
    names = list(SMALL_PLACES)
    n = len(names)

    def body(chip_ref, tot_ref, quarter_ref, rel_ref, *refs):
        del chip_ref
        ins, outs = refs[:3 * n], refs[3 * n:]
        for i, nm in enumerate(names):
            source, row, (rows, cols) = SMALL_PLACES[nm]
            g = {"rows": tot_ref, "quarter": quarter_ref, "rel": rel_ref}[source][row:row + rows, 0:cols]
            outs[4 * i][...] = g
            outs[4 * i + 1][...], outs[4 * i + 2][...], outs[4 * i + 3][...] = _adam_step(
                g, ins[3 * i][...], ins[3 * i + 1][...], ins[3 * i + 2][...])

    whole = lambda shape: pl.BlockSpec(shape, lambda i, c: (0,) * len(shape))
    shapes = [SMALL_PLACES[nm][2] for nm in names]
    outs = pl.pallas_call(
        body, name="small_update",
        grid_spec=pltpu.PrefetchScalarGridSpec(
            num_scalar_prefetch=1, grid=(1,),
            in_specs=[whole(tot.shape), pl.BlockSpec((tot.shape[0], D // 4), lambda i, c: (0, c[0])),
                      whole(tot_rel.shape)] + [whole(shp) for shp in shapes for _ in range(3)],
            out_specs=[whole(shp) for shp in shapes for _ in range(4)]),
        out_shape=[SDS(shp, F32) for shp in shapes for _ in range(4)],
    )(chip, tot, tot, tot_rel, *[a for nm in names for a in wmv[nm]])
    return {nm: tuple(outs[4 * i:4 * i + 4]) for i, nm in enumerate(names)}


def _pad_rows(a, rows):
    return jnp.concatenate([a, jnp.zeros((rows - a.shape[0], a.shape[1]), a.dtype)], axis=0)


def _pad_cols(a, cols):
    return jnp.concatenate([a, jnp.zeros((a.shape[0], cols - a.shape[1]), a.dtype)], axis=1)


def kernel(x, a_pre_norm, a_w_in, a_conv_w, a_w_out, a_post_norm, kv_norm, w_kv, rel_bias, b_pre_norm, b_w_in, b_sinks, b_w_out, b_post_norm, loss_target, m_a_pre_norm, m_a_w_in, m_a_conv_w, m_a_w_out, m_a_post_norm, m_kv_norm, m_w_kv, m_rel_bias, m_b_pre_norm, m_b_w_in, m_b_sinks, m_b_w_out, m_b_post_norm, v_a_pre_norm, v_a_w_in, v_a_conv_w, v_a_w_out, v_a_post_norm, v_kv_norm, v_w_kv, v_rel_bias, v_b_pre_norm, v_b_w_in, v_b_sinks, v_b_w_out, v_b_post_norm):
    seq = x.shape[1]
    xs = x.reshape(seq, D)
    tgt = loss_target.reshape(seq, D)
    chip = 2 * lax.axis_index("x") + lax.axis_index("y")
    core = lax.axis_index("c")
    tm = _tile(seq, 512)
    tmw = _tile(seq, 1024)

    shards = [a_w_in[0], a_w_out[0], w_kv, b_w_in[0], b_w_out[0]]
    small_w = _pad_rows(jnp.concatenate([a_pre_norm, a_conv_w[0], a_post_norm], axis=0), 8)
    *own_only, small_g = _prepare_weights(shards, small_w)
    where = jnp.stack([core, chip]).astype(jnp.int32)
    small_full = small_g.transpose(1, 0, 2).reshape(8, D)
    g_apre, conv_w, g_apost = small_full[0:1], _pad_rows(small_full[1:4], 8), small_full[4:5]
    g_kv = kv_norm.reshape(1, D)

    proj, n1, (win_g, wouta_g, wkv_g, wbin_g, woutb_g) = _a_in(where[1:2], xs, g_apre, own_only, tmw)
    wouta = wouta_g.reshape(D, D)
    wkv = wkv_g.reshape(D, 2 * KV_W)
    woutb = woutb_g.reshape(D, D)
    ya, oa, h1, conv = _a_mix(proj, xs, conv_w, wouta, g_apost, tm)
    kv, q, zb = _b_in(h1, g_kv, b_pre_norm, wkv, wbin_g, tmw)
    tab = _bias_table(rel_bias, b_sinks.reshape(N_HEADS))
    att, stats = _attn_fwd(q, kv, tab)
    dh2, dqz, datt, loss_acc, dg_bpost, dw_outb, dw_outb16 = _mid(att, zb, h1, tgt, woutb, b_post_norm, tm)

    dqz, dkv, dtab = _attn_bwd(q, kv, datt, stats, tab, dqz)
    dh1, doa, dg_b, dw_bin, dw_kv, dw_bin16, dw_kv16 = _b_bwd(dqz, dkv, h1, dh2, oa, wbin_g, wkv, g_kv, b_pre_norm,
                                                              g_apost, tm)
    by_chip = lambda a, cols: a.reshape(N_CHIPS, D // 4, cols)
    grads1 = [by_chip(dw_kv, 2 * KV_W), dw_bin, by_chip(dw_outb, D)]
    sent1 = [by_chip(dw_kv16, 2 * KV_W), dw_bin16, by_chip(dw_outb16, D)]
    names1 = ["w_kv", "b_w_in", "b_w_out"]
    dproj, dconv_w, dw_outa, dw_outa16, from_devices1 = _a_bwd(doa, ya, conv, proj, conv_w, wouta, tm, sent1)
    shards1 = [_add_devices(where, g, r, "add_devices_" + nm) for g, r, nm in zip(grads1, from_devices1, names1)]
    tmw2 = _tile(seq, 4096)
    win_lo, win_lo16, outa_got = _dw_in_half(n1, dproj, 0, tmw2, "dw_a_in_lo", to_devices=by_chip(dw_outa16, D))
    win_hi, win_hi16, win_got = _dw_in_half(n1, dproj, 1, tmw2, "dw_a_in_hi", to_owners=win_lo16)
    nt = seq // tmw
    dn_first, win_got = _a_in_bwd_matmul(dproj, win_g, tmw, max(nt - max(nt // 4, 1), 1), win_hi16, win_got)
    grad_x, dg_apre = _a_in_bwd(dn_first, dproj, xs, dh1, win_g, g_apre, tm)
    shards2 = [_add_win(where, win_lo, win_hi, win_got, "add_devices_a_w_in"),
               _add_devices(where, by_chip(dw_outa, D), outa_got, "add_devices_a_w_out")]
    drel, dsink = _bias_fold(dtab)

    smalls = jnp.concatenate([
        dg_apre[0:1], dg_b[2:3], dg_b[0:1], dg_b[1:2], dg_bpost[0:1], _pad_cols(dsink[0:1], D),
        _pad_cols(loss_acc[0:1], D), jnp.zeros((1, D), F32), dconv_w], axis=0)
    assert smalls.shape == (SMALL_ROWS, D)
    (g_wkv, g_wbin, g_woutb, g_win, g_wouta), gathered = _share_and_gather(shards1 + shards2, (smalls, drel))
    tot, tot_rel = _sum_smalls(gathered)

    big = {}
    for nm, g, w, m, v in [("a_w_in", g_win, a_w_in, m_a_w_in, v_a_w_in), ("a_w_out", g_wouta, a_w_out, m_a_w_out, v_a_w_out),
                           ("w_kv", g_wkv, w_kv, m_w_kv, v_w_kv), ("b_w_in", g_wbin, b_w_in, m_b_w_in, v_b_w_in),
                           ("b_w_out", g_woutb, b_w_out, m_b_w_out, v_b_w_out)]:
        shp = w.shape
        two = (shp[-2], shp[-1])
        d, nm_, nv_ = _adamw(g, w.reshape(two), m.reshape(two), v.reshape(two), "adamw_" + nm)
        big[nm] = (g.reshape(shp), d.reshape(shp), nm_.reshape(shp), nv_.reshape(shp))

    given = {"a_pre_norm": (a_pre_norm, m_a_pre_norm, v_a_pre_norm), "a_conv_w": (a_conv_w, m_a_conv_w, v_a_conv_w),
             "a_post_norm": (a_post_norm, m_a_post_norm, v_a_post_norm), "kv_norm": (kv_norm, m_kv_norm, v_kv_norm),
             "rel_bias": (rel_bias, m_rel_bias, v_rel_bias), "b_pre_norm": (b_pre_norm, m_b_pre_norm, v_b_pre_norm),
             "b_sinks": (b_sinks, m_b_sinks, v_b_sinks), "b_post_norm": (b_post_norm, m_b_post_norm, v_b_post_norm)}
    small = _small_update(where[1:2], tot, tot_rel, {nm: tuple(a.reshape(SMALL_PLACES[nm][2]) for a in wmv)
                                            for nm, wmv in given.items()})
    order = ["a_pre_norm", "a_w_in", "a_conv_w", "a_w_out", "a_post_norm", "kv_norm", "w_kv", "rel_bias",
             "b_pre_norm", "b_w_in", "b_sinks", "b_w_out", "b_post_norm"]
    outs = []
    for which in range(4):
        for nm in order:
            outs.append(big[nm][which] if nm in big else small[nm][which].reshape(given[nm][0].shape))
    loss = 0.5 * tot[LOSS_ROW, 0]
    return (loss, grad_x.reshape(x.shape), *outs)
```

```python
import math

import jax
import jax.numpy as jnp
from jax import lax
from jax.experimental import pallas as pl
from jax.experimental.pallas import tpu as pltpu

F32 = jnp.float32
BF16 = jnp.bfloat16
MESH = pl.DeviceIdType.MESH
SDS = jax.ShapeDtypeStruct

D = 1024
HEAD_DIM = 64
N_HEADS = 16
N_KV = 2
GROUP = 8
KV_W = 128
BLK = 128
N_BUCKETS = 32
MAX_EXACT = 16
MAX_DISTANCE = 128
EPS = 1e-6
NEG_INF = -1e30
Q_SCALE = HEAD_DIM ** -0.5

ADAM_LR = 0.001
ADAM_B1 = 0.9
ADAM_B2 = 0.999
ADAM_EPS = 1e-08
ADAM_WD = 0.01
ADAM_STEP = 10

N_CHIPS = 4
N_DEV = 8
BIN_COLS = 2 * D // N_CHIPS
VMEM_LIMIT = 56 * 1024 * 1024
SMALL_ROWS = 16
LOSS_ROW = 6
SMALL_PLACES = {
    "a_pre_norm": ("quarter", 0, (1, D // 4)), "a_conv_w": ("quarter", 8, (3, D // 4)),
    "a_post_norm": ("quarter", 1, (1, D // 4)), "kv_norm": ("rows", 2, (1, D)),
    "rel_bias": ("rel", 0, (N_BUCKETS, N_HEADS)), "b_pre_norm": ("rows", 3, (1, D)),
    "b_sinks": ("rows", 5, (1, N_HEADS)), "b_post_norm": ("rows", 4, (1, D)),
}


def _bucket_thresholds():
    def bucket(d):
        big = MAX_EXACT + int(math.log(d / MAX_EXACT) / math.log(MAX_DISTANCE / MAX_EXACT)
                              * (N_BUCKETS - MAX_EXACT))
        return d if d < MAX_EXACT else min(big, N_BUCKETS - 1)
    out = []
    for b in range(MAX_EXACT + 1, N_BUCKETS):
        out.append(min(d for d in range(MAX_EXACT, MAX_DISTANCE) if bucket(d) >= b))
    return tuple(out)


BUCKET_THRESHOLDS = _bucket_thresholds()


def _params(semantics=None, vmem=VMEM_LIMIT):
    return pltpu.CompilerParams(dimension_semantics=semantics, vmem_limit_bytes=vmem)


def _tile(n, pref):
    return pref if n >= 2 * pref else max(n // 2, 8)


def _rms_scale(v):
    return lax.rsqrt(jnp.mean(v * v, axis=-1, keepdims=True) + EPS)


def _nt(a, b):
    return lax.dot_general(a, b, (((1,), (1,)), ((), ())), preferred_element_type=F32)


def _tn(a, b):
    return lax.dot_general(a, b, (((0,), (0,)), ((), ())), preferred_element_type=F32)


def _nn(a, b):
    return jnp.dot(a, b, preferred_element_type=F32)


def _silu_parts(z):
    sg = jax.nn.sigmoid(z)
    return sg, z * sg


def _dsilu(z, sg):
    return sg * (1.0 + z * (1.0 - sg))


def _write_gradient(acc, out32, out16, stage, sem):
    whole = pltpu.make_async_copy(acc, out32, sem)
    whole.start()
    rows = stage.shape[0]
    for k in range(acc.shape[0] // rows):
        stage[...] = acc[rows * k:rows * (k + 1), :].astype(BF16)
        pltpu.sync_copy(stage, out16.at[pl.ds(rows * k, rows)])
    whole.wait()


def _acc_row(ref, row, val):
    ref[row:row + 1, :] += val


def _gather_copies(outs, splits, ici_send, ici_recv, d2d_send, d2d_recv):
    x, y, c = lax.axis_index("x"), lax.axis_index("y"), lax.axis_index("c")
    k = 2 * x + y
    sibling = (x, y, 1 - c)

    def part(o_ref, chip, core, split):
        if not split:
            return o_ref.at[chip]
        h = o_ref.shape[1] // 2
        return o_ref.at[chip, pl.ds(pl.multiple_of(core * h, 16), h)]

    def remote(ref, a, j, sems, to):
        return pltpu.make_async_remote_copy(src_ref=ref, dst_ref=ref, send_sem=sems[0].at[3 * a + j],
                                            recv_sem=sems[1].at[3 * a + j], device_id=to, device_id_type=MESH)

    copies = []
    for a, (o_ref, split) in enumerate(zip(outs, splits)):
        for j, (px, py) in enumerate([(x, 1 - y), (1 - x, y), (1 - x, 1 - y)]):
            kj = 2 * px + py
            ici, d2d = (ici_send, ici_recv), (d2d_send, d2d_recv)
            copies.append((remote(part(o_ref, k, c, split), a, j, ici, (px, py, c)),
                           remote(part(o_ref, kj, c, split), a, j, ici, (px, py, c)),
                           remote(part(o_ref, kj, c, split), a, j, d2d, sibling) if split else None,
                           remote(part(o_ref, kj, 1 - c, split), a, j, d2d, sibling) if split else None))
    return copies


def _gather_sems(n):
    return [pltpu.SemaphoreType.DMA((3 * n,)) for _ in range(4)]


def _prepare_weights(shards, small):
    n = len(shards)

    def body(*refs):
        ins, small_in = refs[:n], refs[n]
        outs, small_out = refs[n + 1:2 * n + 1], refs[2 * n + 1]
        stages, put_sem = refs[2 * n + 2:3 * n + 2], refs[3 * n + 2]
        sems = refs[3 * n + 3:]
        k = 2 * lax.axis_index("x") + lax.axis_index("y")
        puts = []
        for a, (i_ref, stage, o_ref) in enumerate(zip(ins, stages, outs)):
            stage[...] = i_ref[...].astype(BF16)
            puts.append(pltpu.make_async_copy(stage, o_ref.at[k], put_sem.at[a]))
            puts[-1].start()
        small_out[k] = small_in[...]
        copies = _gather_copies([small_out], [False], *sems)
        for send, _, _, _ in copies:
            send.start()
        for _, arrival, _, _ in copies:
            arrival.wait_recv()
        for send, _, _, _ in copies:
            send.wait_send()
        for put in puts:
            put.wait()

    vm = pl.BlockSpec(memory_space=pltpu.VMEM)
    anyspace = pl.BlockSpec(memory_space=pl.ANY)
    out_shape = [SDS((N_CHIPS,) + s.shape, BF16) for s in shards] + [SDS((N_CHIPS,) + small.shape, F32)]
    return pl.pallas_call(
        body, name="prepare_weights", out_shape=out_shape,
        in_specs=[vm] * (n + 1), out_specs=[anyspace] * n + [vm],
        scratch_shapes=[pltpu.VMEM(s.shape, BF16) for s in shards] + [pltpu.SemaphoreType.DMA((n,))] + _gather_sems(1),
        compiler_params=pltpu.CompilerParams(vmem_limit_bytes=VMEM_LIMIT),
    )(*shards, small)


def _a_in(chip, x, g_pre, weights, tm):
    s = x.shape[0]
    nt = s // tm
    n = len(weights)

    def body(chip_ref, x_ref, g_ref, *refs):
        proj_ref, n1_ref = refs[n:n + 2]
        gathered = refs[n + 2:2 * n + 2]
        wbuf, n1_all, fetch_sem = refs[2 * n + 2:2 * n + 5]
        sems = refs[2 * n + 5:]
        jj, i = pl.program_id(0), pl.program_id(1)
        copies = _gather_copies(gathered, [True] * n, *sems)

        def fetch(rel):
            slot = jnp.bitwise_xor(chip_ref[0], rel)
            return pltpu.make_async_copy(gathered[0].at[slot], wbuf.at[rel % 2], fetch_sem.at[rel % 2])

        @pl.when((jj == 0) & (i == 0))
        def _():
            fetch(0).start()
            copies[0][0].start()
            copies[1][0].start()
            fetch(0).wait()

        for rel in (1, 2, 3):
            @pl.when((jj == rel) & (i == 0))
            def _():
                fetch(rel).wait()

        @pl.when(jj == 0)
        def _():
            xv = x_ref[...]
            n1 = (xv * _rms_scale(xv) * g_ref[...]).astype(BF16)
            n1_ref[...] = n1
            n1_all[i] = n1
        proj_ref[...] = _nn(n1_all[i], wbuf[jj % 2]).astype(BF16)

        for rel in (1, 2, 3):
            @pl.when((jj == rel - 1) & (i == max(nt - 2, nt // 2)))
            def _():
                _, arrival, forward, forwarded = copies[rel - 1]
                arrival.wait_recv()
                forward.start()
                forwarded.wait_recv()
                fetch(rel).start()
                if rel == 1:
                    for send, _, _, _ in copies[2:]:
                        send.start()

        @pl.when((jj == 3) & (i == max(nt - 2, 0)))
        def _():
            for _, arrival, forward, _ in copies[3:]:
                arrival.wait_recv()
                forward.start()

        @pl.when((jj == 3) & (i == nt - 1))
        def _():
            for _, _, _, forwarded in copies[3:]:
                forwarded.wait_recv()
            for send, _, forward, _ in copies:
                forward.wait_send()
                send.wait_send()

    anyspace = pl.BlockSpec(memory_space=pl.ANY)
    proj, n1, *gathered = pl.pallas_call(
        body, name="a_in",
        grid_spec=pltpu.PrefetchScalarGridSpec(
            num_scalar_prefetch=1, grid=(4, nt),
            in_specs=[pl.BlockSpec((tm, D), lambda jj, i, c: (jnp.where(jj == 0, i, nt - 1), 0)),
                      pl.BlockSpec((1, D), lambda jj, i, c: (0, 0))] + [anyspace] * n,
            out_specs=[pl.BlockSpec((tm, D), lambda jj, i, c: (i, jnp.bitwise_xor(c[0], jj))),
                       pl.BlockSpec((tm, D), lambda jj, i, c: (jnp.where(jj == 0, i, nt - 1), 0))] + [anyspace] * n,
            scratch_shapes=[pltpu.VMEM((2, D, D), BF16), pltpu.VMEM((nt, tm, D), BF16),
                            pltpu.SemaphoreType.DMA((2,))] + _gather_sems(n)),
        out_shape=[SDS((s, 4 * D), BF16), SDS((s, D), BF16)] + [SDS(w.shape, w.dtype) for w in weights],
        input_output_aliases={3 + a: 2 + a for a in range(n)},
        compiler_params=_params(("arbitrary", "arbitrary")),
    )(chip, x, g_pre, *weights)
    return proj, n1, gathered


def _shift_rows(v, last, second_last, rows):
    v1 = jnp.where(rows >= 1, pltpu.roll(v, 1, 0), last)
    v2 = jnp.where(rows >= 2, pltpu.roll(v, 2, 0), jnp.where(rows == 1, last, second_last))
    return v1, v2


def _a_mix(proj, x, conv_w, w_out, g_post, tm):
    s = x.shape[0]

    def body(proj_ref, x_ref, cw_ref, w_ref, g_ref, ya_ref, oa_ref, h1_ref, conv_ref, carry):
        @pl.when(pl.program_id(0) == 0)
        def _():
            carry[...] = jnp.zeros_like(carry)
        v = proj_ref[:, D:2 * D].astype(F32) * proj_ref[:, 2 * D:3 * D].astype(F32)
        rows = lax.broadcasted_iota(jnp.int32, (tm, D), 0)
        before = carry[...]
        v1, v2 = _shift_rows(v, before[7:8, :], before[6:7, :], rows)
        carry[...] = v[tm - 8:tm, :]
        conv = cw_ref[0:1, :] * v2 + cw_ref[1:2, :] * v1 + cw_ref[2:3, :] * v
        conv_ref[...] = conv.astype(BF16)
        _, sz = _silu_parts(proj_ref[:, 3 * D:4 * D].astype(F32))
        ya = (proj_ref[:, 0:D].astype(F32) * conv * sz).astype(BF16)
        ya_ref[...] = ya
        oa = _nn(ya, w_ref[...])
        oa_ref[...] = oa.astype(BF16)
        h1_ref[...] = x_ref[...] + oa * _rms_scale(oa) * g_ref[...]

    row = lambda i: (i, 0)
    fix = lambda i: (0, 0)
    return pl.pallas_call(
        body, name="a_mix", grid=(s // tm,),
        in_specs=[pl.BlockSpec((tm, 4 * D), row), pl.BlockSpec((tm, D), row), pl.BlockSpec((8, D), fix),
                  pl.BlockSpec((D, D), fix), pl.BlockSpec((1, D), fix)],
        out_specs=[pl.BlockSpec((tm, D), row)] * 4,
        out_shape=[SDS((s, D), BF16), SDS((s, D), BF16), SDS((s, D), F32), SDS((s, D), BF16)],
        scratch_shapes=[pltpu.VMEM((8, D), F32)],
        compiler_params=_params(("arbitrary",)),
    )(proj, x, conv_w, w_out, g_post)


def _b_in(h1, g_kv, g_pre, w_kv, wbin_g, tm):
    s = h1.shape[0]

    def body(h_ref, gk_ref, gb_ref, wkv_ref, wb_ref, kv_ref, q_ref, z_ref):
        h = h_ref[...]
        hh = h * _rms_scale(h)
        nk = (hh * gk_ref[...]).astype(BF16)
        nb = (hh * gb_ref[...]).astype(BF16)
        kv_ref[...] = _nn(nk, wkv_ref[...]).astype(BF16)
        for j in range(2):
            q_ref[:, BIN_COLS * j:BIN_COLS * (j + 1)] = (_nn(nb, wb_ref[j]) * Q_SCALE).astype(BF16)
            z_ref[:, BIN_COLS * j:BIN_COLS * (j + 1)] = _nn(nb, wb_ref[2 + j]).astype(BF16)

    row = lambda i: (i, 0)
    fix = lambda i: (0, 0)
    return pl.pallas_call(
        body, name="b_in", grid=(s // tm,),
        in_specs=[pl.BlockSpec((tm, D), row), pl.BlockSpec((1, D), fix), pl.BlockSpec((1, D), fix),
                  pl.BlockSpec((D, 2 * KV_W), fix), pl.BlockSpec((N_CHIPS, D, BIN_COLS), lambda i: (0, 0, 0))],
        out_specs=[pl.BlockSpec((tm, 2 * KV_W), row), pl.BlockSpec((tm, D), row), pl.BlockSpec((tm, D), row)],
        out_shape=[SDS((s, 2 * KV_W), BF16), SDS((s, D), BF16), SDS((s, D), BF16)],
        compiler_params=_params(("parallel",)),
    )(h1, g_kv, g_pre, w_kv, wbin_g)


def _band_buckets():
    q = lax.broadcasted_iota(jnp.int32, (BLK, 2 * BLK), 0)
    k = lax.broadcasted_iota(jnp.int32, (BLK, 2 * BLK), 1)
    dist = q + BLK - k
    bucket = jnp.where(dist < MAX_EXACT, dist, MAX_EXACT)
    for t in BUCKET_THRESHOLDS:
        bucket = bucket + jnp.where(dist >= t, 1, 0)
    in_window = (dist >= 0) & (dist < BLK)
    return jnp.where(in_window, bucket, -1)


def _head_place(h):
    kh, j, e = h // GROUP, (h % GROUP) // 2, h % 2
    return kh, slice(BLK * j, BLK * (j + 1)), slice(2 * BLK * e, 2 * BLK * (e + 1))


def _bias_table(rel_bias, sinks):
    def body(rb_ref, sink_ref, tab_ref):
        bucket = _band_buckets()
        col = lax.broadcasted_iota(jnp.int32, (BLK, 2 * BLK), 1)
        for h in range(N_HEADS):
            acc = jnp.where(bucket < 0, NEG_INF, 0.0).astype(F32)
            for b in range(N_BUCKETS):
                acc = jnp.where(bucket == b, rb_ref[b, h], acc)
            acc = jnp.where(col == 0, sink_ref[h], acc)
            kh, rows, cols = _head_place(h)
            tab_ref[1, kh, rows, cols] = acc
            tab_ref[0, kh, rows, cols] = jnp.where((col > 0) & (col < BLK), NEG_INF, acc)

    return pl.pallas_call(
        body, name="bias_table", out_shape=SDS((2, N_KV, 4 * BLK, 4 * BLK), F32),
        in_specs=[pl.BlockSpec(memory_space=pltpu.SMEM), pl.BlockSpec(memory_space=pltpu.SMEM)],
        out_specs=pl.BlockSpec(memory_space=pltpu.VMEM),
    )(rel_bias, sinks)


def _bias_fold(dtab):
    def body(dtab_ref, out_ref, dsink_ref):
        bucket = _band_buckets()
        row = lax.broadcasted_iota(jnp.int32, (N_BUCKETS, 128), 0)
        lane = lax.broadcasted_iota(jnp.int32, (N_BUCKETS, 128), 1)
        row8 = lax.broadcasted_iota(jnp.int32, (8, 128), 0)
        lane8 = lax.broadcasted_iota(jnp.int32, (8, 128), 1)
        acc = jnp.zeros((N_BUCKETS, 128), F32)
        dsink = jnp.zeros((8, 128), F32)
        for h in range(N_HEADS):
            kh, rows, cols = _head_place(h)
            dt = dtab_ref[kh, rows, cols]
            for b in range(N_BUCKETS):
                val = jnp.sum(jnp.where(bucket == b, dt, 0.0))
                acc = acc + jnp.where((row == b) & (lane == h), val, 0.0)
            dsink = dsink + jnp.where((row8 == 0) & (lane8 == h), jnp.sum(dt[:, 0:1]), 0.0)
        out_ref[...] = acc
        dsink_ref[...] = dsink

    vm = pl.BlockSpec(memory_space=pltpu.VMEM)
    return pl.pallas_call(
        body, name="bias_fold", out_shape=[SDS((N_BUCKETS, 128), F32), SDS((8, 128), F32)],
        in_specs=[vm], out_specs=[vm, vm],
    )(dtab)


def _pair_operands(prev, cur):
    t = jnp.concatenate([prev, cur], axis=0).astype(F32)
    t = jnp.where(lax.broadcasted_iota(jnp.int32, t.shape, 0) == 0, 0.0, t)
    tr = pltpu.roll(t, HEAD_DIM, 1)
    lo = lax.broadcasted_iota(jnp.int32, t.shape, 1) < HEAD_DIM
    zero = jnp.zeros_like(t)
    head0 = jnp.concatenate([jnp.where(lo, t, zero), jnp.where(lo, zero, tr)], axis=0).astype(BF16)
    head1 = jnp.concatenate([jnp.where(lo, tr, zero), jnp.where(lo, zero, t)], axis=0).astype(BF16)
    return head0, head1


def _pair_fold(d0, d1):
    lo = lax.broadcasted_iota(jnp.int32, (2 * BLK, KV_W), 1) < HEAD_DIM
    zero = jnp.zeros((2 * BLK, KV_W), F32)
    g0 = jnp.where(lo, d0[0:256], zero) + pltpu.roll(jnp.where(lo, zero, d0[256:512]), HEAD_DIM, 1)
    g1 = pltpu.roll(jnp.where(lo, d1[0:256], zero), HEAD_DIM, 1) + jnp.where(lo, zero, d1[256:512])
    return jnp.where(lax.broadcasted_iota(jnp.int32, (2 * BLK, KV_W), 0) == 0, 0.0, g0 + g1)


def _stack_pairs(ref, kh):
    return jnp.concatenate([ref[:, 128 * (4 * kh + j):128 * (4 * kh + j + 1)] for j in range(4)], axis=0)


def _table_spec():
    return pl.BlockSpec((1, N_KV, 4 * BLK, 4 * BLK), lambda n: (jnp.minimum(n, 1), 0, 0, 0))


def _attn_fwd(q, kv, tab):
    s = q.shape[0]

    def body(q_ref, kp_ref, kc_ref, vp_ref, vc_ref, tab_ref, att_ref, stats_ref):
        k2 = _pair_operands(kp_ref[...], kc_ref[...])
        v2 = _pair_operands(vp_ref[...], vc_ref[...])
        lane = lax.broadcasted_iota(jnp.int32, (BLK, 128), 1)
        stats = jnp.zeros((BLK, 128), F32)
        for kh in range(N_KV):
            sc = _nt(_stack_pairs(q_ref, kh), k2[kh])
            ps = []
            for e in range(2):
                lg = sc[:, 256 * e:256 * (e + 1)] + tab_ref[0, kh, :, 256 * e:256 * (e + 1)]
                m = jnp.max(lg, axis=-1, keepdims=True)
                ex = jnp.exp(lg - m)
                den = jnp.sum(ex, axis=-1, keepdims=True)
                ps.append(ex * (1.0 / den))
                lse = m + jnp.log(den)
                for j in range(4):
                    stats = jnp.where(lane == GROUP * kh + 2 * j + e, lse[BLK * j:BLK * (j + 1)], stats)
            out = _nn(jnp.concatenate(ps, axis=1).astype(BF16), v2[kh])
            for j in range(4):
                att_ref[:, 128 * (4 * kh + j):128 * (4 * kh + j + 1)] = out[BLK * j:BLK * (j + 1)].astype(BF16)
        stats_ref[...] = stats

    cur = lambda n: (n, 0)
    prev = lambda n: (jnp.maximum(n - 1, 0), 0)
    return pl.pallas_call(
        body, name="attn_fwd", grid=(s // BLK,),
        in_specs=[pl.BlockSpec((BLK, D), cur),
                  pl.BlockSpec((BLK, KV_W), prev), pl.BlockSpec((BLK, KV_W), cur),
                  pl.BlockSpec((BLK, KV_W), lambda n: (jnp.maximum(n - 1, 0), 1)),
                  pl.BlockSpec((BLK, KV_W), lambda n: (n, 1)), _table_spec()],
        out_specs=[pl.BlockSpec((BLK, D), cur), pl.BlockSpec((BLK, 128), cur)],
        out_shape=[SDS((s, D), BF16), SDS((s, 128), F32)],
        compiler_params=_params(("parallel",)),
    )(q, kv, kv, kv, kv, tab)


def _mid(att, zb, h1, tgt, w_out, g_post, tm):
    s = att.shape[0]
    nt = s // tm

    def body(att_ref, z_ref, h1_ref, t_ref, w_ref, g_ref,
             dh_ref, dqz_ref, datt_ref, loss_ref, dg_ref, dw_ref, dw16_ref, dw_acc, stage, put_sem):
        @pl.when(pl.program_id(0) == 0)
        def _():
            loss_ref[...] = jnp.zeros_like(loss_ref)
            dg_ref[...] = jnp.zeros_like(dg_ref)
            dw_acc[...] = jnp.zeros_like(dw_acc)
        att = att_ref[...].astype(F32)
        z = z_ref[...].astype(F32)
        sg, sz = _silu_parts(z)
        ob = (att * sz).astype(BF16)
        y2 = _nn(ob, w_ref[...])
        r2 = _rms_scale(y2)
        yh = y2 * r2
        g = g_ref[...]
        err = (h1_ref[...] + yh * g) - t_ref[...]
        loss_ref[...] += jnp.sum(jnp.sum(err * err, axis=-1, keepdims=True) / D)
        dh = err / D
        dh_ref[...] = dh
        _acc_row(dg_ref, 0, jnp.sum(dh * yh, axis=0, keepdims=True))
        dyh = dh * g
        dy = (r2 * (dyh - yh * jnp.mean(dyh * yh, axis=-1, keepdims=True))).astype(BF16)
        dw_acc[...] += _tn(ob, dy)
        dob = _nt(dy, w_ref[...])
        datt_ref[...] = (dob * sz).astype(BF16)
        dqz_ref[...] = (dob * att * _dsilu(z, sg)).astype(BF16)

        @pl.when(pl.program_id(0) == nt - 1)
        def _():
            _write_gradient(dw_acc, dw_ref, dw16_ref, stage, put_sem)

    row = lambda i: (i, 0)
    fix = lambda i: (0, 0)
    anyspace = pl.BlockSpec(memory_space=pl.ANY)
    return pl.pallas_call(
        body, name="mid", grid=(nt,),
        in_specs=[pl.BlockSpec((tm, D), row)] * 4 + [pl.BlockSpec((D, D), fix), pl.BlockSpec((1, D), fix)],
        out_specs=[pl.BlockSpec((tm, D), row), pl.BlockSpec((tm, D), lambda i: (i, 1)), pl.BlockSpec((tm, D), row),
                   pl.BlockSpec((8, 128), fix), pl.BlockSpec((8, D), fix), anyspace, anyspace],
        out_shape=[SDS((s, D), F32), SDS((s, 2 * D), BF16), SDS((s, D), BF16), SDS((8, 128), F32),
                   SDS((8, D), F32), SDS((D, D), F32), SDS((D, D), BF16)],
        scratch_shapes=[pltpu.VMEM((D, D), F32), pltpu.VMEM((D // 4, D), BF16), pltpu.SemaphoreType.DMA],
        compiler_params=_params(("arbitrary",)),
    )(att, zb, h1, tgt, w_out, g_post)


def _attn_bwd(q, kv, datt, stats, tab, dqz):
    s = q.shape[0]
    nb = s // BLK

    def body(q_ref, kp_ref, kc_ref, vp_ref, vc_ref, da_ref, st_ref, tab_ref, dqz_in,
             dq_ref, dkv_ref, dtab_ref, dk_carry, dv_carry):
        del dqz_in
        n = pl.program_id(0)

        @pl.when(n == 0)
        def _():
            dtab_ref[...] = jnp.zeros_like(dtab_ref)
            dk_carry[...] = jnp.zeros_like(dk_carry)
            dv_carry[...] = jnp.zeros_like(dv_carry)

        @pl.when(n < nb)
        def _():
            k2 = _pair_operands(kp_ref[...], kc_ref[...])
            v2 = _pair_operands(vp_ref[...], vc_ref[...])
            lane = lax.broadcasted_iota(jnp.int32, (BLK, 128), 1)
            stats = st_ref[...]
            dk2, dv2 = [], []
            for kh in range(N_KV):
                qs = _stack_pairs(q_ref, kh)
                das = _stack_pairs(da_ref, kh)
                sc = _nt(qs, k2[kh])
                dp = _nt(das, v2[kh])
                ps, dss = [], []
                for e in range(2):
                    heads = [GROUP * kh + 2 * j + e for j in range(4)]
                    lse = jnp.concatenate([jnp.sum(jnp.where(lane == h, stats, 0.0), axis=-1, keepdims=True)
                                           for h in heads], axis=0)
                    cols = slice(256 * e, 256 * (e + 1))
                    p = jnp.exp(sc[:, cols] + tab_ref[0, kh, :, cols] - lse)
                    delta = jnp.sum(p * dp[:, cols], axis=-1, keepdims=True)
                    ds = p * (dp[:, cols] - delta)
                    dtab_ref[kh, :, cols] += ds
                    ps.append(p)
                    dss.append(ds)
                p2 = jnp.concatenate(ps, axis=1).astype(BF16)
                ds2 = jnp.concatenate(dss, axis=1).astype(BF16)
                dq = _nn(ds2, k2[kh]) * Q_SCALE
                for j in range(4):
                    dq_ref[:, 128 * (4 * kh + j):128 * (4 * kh + j + 1)] = dq[BLK * j:BLK * (j + 1)].astype(BF16)
                dk2.append(_tn(ds2, qs))
                dv2.append(_tn(p2, das))
            dkk = _pair_fold(dk2[0], dk2[1])
            dvv = _pair_fold(dv2[0], dv2[1])
            dkv_ref[:, 0:KV_W] = (dk_carry[...] + dkk[0:BLK]).astype(BF16)
            dkv_ref[:, KV_W:2 * KV_W] = (dv_carry[...] + dvv[0:BLK]).astype(BF16)
            dk_carry[...] = dkk[BLK:2 * BLK]
            dv_carry[...] = dvv[BLK:2 * BLK]

        @pl.when(n == nb)
        def _():
            dkv_ref[:, 0:KV_W] = dk_carry[...].astype(BF16)
            dkv_ref[:, KV_W:2 * KV_W] = dv_carry[...].astype(BF16)

    cur = lambda n: (jnp.minimum(n, nb - 1), 0)
    prev = lambda n: (jnp.clip(n - 1, 0, nb - 1), 0)
    return pl.pallas_call(
        body, name="attn_bwd", grid=(nb + 1,),
        in_specs=[pl.BlockSpec((BLK, D), cur),
                  pl.BlockSpec((BLK, KV_W), prev), pl.BlockSpec((BLK, KV_W), cur),
                  pl.BlockSpec((BLK, KV_W), lambda n: (jnp.clip(n - 1, 0, nb - 1), 1)),
                  pl.BlockSpec((BLK, KV_W), lambda n: (jnp.minimum(n, nb - 1), 1)),
                  pl.BlockSpec((BLK, D), cur), pl.BlockSpec((BLK, 128), cur), _table_spec(),
                  pl.BlockSpec(memory_space=pl.ANY)],
        out_specs=[pl.BlockSpec((BLK, D), cur), pl.BlockSpec((BLK, 2 * KV_W), prev),
                   pl.BlockSpec((N_KV, 4 * BLK, 4 * BLK), lambda n: (0, 0, 0))],
        out_shape=[SDS((s, 2 * D), BF16), SDS((s, 2 * KV_W), BF16), SDS((N_KV, 4 * BLK, 4 * BLK), F32)],
        scratch_shapes=[pltpu.VMEM((BLK, KV_W), F32), pltpu.VMEM((BLK, KV_W), F32)],
        input_output_aliases={8: 0},
        compiler_params=_params(("arbitrary",)),
    )(q, kv, kv, kv, kv, datt, stats, tab, dqz)


def _b_bwd(dqz, dkv, h1, dh2, oa, wbin_g, w_kv, g_kv, g_pre, g_apost, tm):
    s = h1.shape[0]
    nt = s // tm

    def body(dqz_ref, dkv_ref, h_ref, dh2_ref, oa_ref, wb_ref, wkv_ref, gk_ref, gb_ref, ga_ref,
             dh1_ref, doa_ref, dg_ref, dwb_ref, dwkv_ref, dwb16_ref, dwkv16_ref, wcat, dwb_acc, dwkv_acc, put_sem):
        @pl.when(pl.program_id(0) == 0)
        def _():
            dg_ref[...] = jnp.zeros_like(dg_ref)
            dwb_acc[...] = jnp.zeros_like(dwb_acc)
            dwkv_acc[...] = jnp.zeros_like(dwkv_acc)
            for j in range(N_CHIPS):
                pltpu.sync_copy(wb_ref.at[j], wcat.at[:, pl.ds(BIN_COLS * j, BIN_COLS)])
        dnb = _nt(dqz_ref[...], wcat[...])
        dnk = _nt(dkv_ref[...], wkv_ref[...])
        h = h_ref[...]
        r = _rms_scale(h)
        hh = h * r
        dwb_acc[...] += _tn((hh * gb_ref[...]).astype(BF16), dqz_ref[...])
        dwkv_acc[...] += _tn((hh * gk_ref[...]).astype(BF16), dkv_ref[...])
        _acc_row(dg_ref, 0, jnp.sum(dnk * hh, axis=0, keepdims=True))
        _acc_row(dg_ref, 1, jnp.sum(dnb * hh, axis=0, keepdims=True))
        dhh = dnb * gb_ref[...] + dnk * gk_ref[...]
        dh1 = dh2_ref[...] + r * (dhh - hh * jnp.mean(dhh * hh, axis=-1, keepdims=True))
        dh1_ref[...] = dh1
        oa = oa_ref[...].astype(F32)
        ra = _rms_scale(oa)
        oh = oa * ra
        _acc_row(dg_ref, 2, jnp.sum(dh1 * oh, axis=0, keepdims=True))
        doh = dh1 * ga_ref[...]
        doa_ref[...] = (ra * (doh - oh * jnp.mean(doh * oh, axis=-1, keepdims=True))).astype(BF16)

        @pl.when(pl.program_id(0) == nt - 1)
        def _():
            wcat[...] = dwb_acc[...].astype(BF16)
            puts = [pltpu.make_async_copy(dwkv_acc, dwkv_ref, put_sem.at[2 * N_CHIPS])]
            for j in range(N_CHIPS):
                cols = pl.ds(BIN_COLS * j, BIN_COLS)
                puts.append(pltpu.make_async_copy(dwb_acc.at[:, cols], dwb_ref.at[j], put_sem.at[2 * j]))
                puts.append(pltpu.make_async_copy(wcat.at[:, cols], dwb16_ref.at[j], put_sem.at[2 * j + 1]))
            for put in puts:
                put.start()
            for put in puts:
                put.wait()
            wcat[:, 0:2 * KV_W] = dwkv_acc[...].astype(BF16)
            pltpu.sync_copy(wcat.at[:, pl.ds(0, 2 * KV_W)], dwkv16_ref)

    row = lambda i: (i, 0)
    fix = lambda i: (0, 0)
    anyspace = pl.BlockSpec(memory_space=pl.ANY)
    return pl.pallas_call(
        body, name="b_bwd", grid=(nt,),
        in_specs=[pl.BlockSpec((tm, 2 * D), row), pl.BlockSpec((tm, 2 * KV_W), row), pl.BlockSpec((tm, D), row),
                  pl.BlockSpec((tm, D), row), pl.BlockSpec((tm, D), row), anyspace, pl.BlockSpec((D, 2 * KV_W), fix),
                  pl.BlockSpec((1, D), fix), pl.BlockSpec((1, D), fix), pl.BlockSpec((1, D), fix)],
        out_specs=[pl.BlockSpec((tm, D), row), pl.BlockSpec((tm, D), row), pl.BlockSpec((8, D), fix)] + [anyspace] * 4,
        out_shape=[SDS((s, D), F32), SDS((s, D), BF16), SDS((8, D), F32), SDS((N_CHIPS, D, BIN_COLS), F32),
                   SDS((D, 2 * KV_W), F32), SDS((N_CHIPS, D, BIN_COLS), BF16), SDS((D, 2 * KV_W), BF16)],
        scratch_shapes=[pltpu.VMEM((D, 2 * D), BF16), pltpu.VMEM((D, 2 * D), F32), pltpu.VMEM((D, 2 * KV_W), F32),
                        pltpu.SemaphoreType.DMA((2 * N_CHIPS + 1,))],
        compiler_params=_params(("arbitrary",)),
    )(dqz, dkv, h1, dh2, oa, wbin_g, w_kv, g_kv, g_pre, g_apost)


def _to_owner_core(pieces, r, send, recv, core, action):
    x, y, c = lax.axis_index("x"), lax.axis_index("y"), lax.axis_index("c")
    for kp in range(N_CHIPS):
        px, py = kp >> 1, kp & 1
        rel = 4 * (x + px - 2 * x * px) + 2 * (y + py - 2 * y * py) + (c + core - 2 * c * core)

        @pl.when(rel != 0)
        def _():
            cp = pltpu.make_async_remote_copy(src_ref=pieces.at[kp], dst_ref=r.at[rel - 1], send_sem=send.at[kp],
                                              recv_sem=recv.at[rel - 1], device_id=(px, py, core), device_id_type=MESH)
            if action == "start":
                cp.start()
            else:
                cp.wait_send()
    if action == "wait":
        @pl.when(c == core)
        def _():
            for rel in range(1, N_DEV):
                pltpu.make_async_remote_copy(src_ref=pieces.at[0], dst_ref=r.at[rel - 1], send_sem=send.at[0],
                                             recv_sem=recv.at[rel - 1], device_id=(x, y, c),
                                             device_id_type=MESH).wait_recv()


def _owner_core_sems():
    return [pltpu.SemaphoreType.DMA((N_CHIPS,)), pltpu.SemaphoreType.DMA((N_DEV - 1,))]


def _device_exchange(grads, recvs, send, recv):
    x, y, c = lax.axis_index("x"), lax.axis_index("y"), lax.axis_index("c")
    copies = []
    for a, (g, r) in enumerate(zip(grads, recvs)):
        h = g.shape[1] // 2
        for rel in range(1, N_DEV):
            fx, fy, fc = rel >> 2, (rel >> 1) & 1, rel & 1
            px, py, pc = x + fx - 2 * x * fx, y + fy - 2 * y * fy, c + fc - 2 * c * fc
            sem = (N_DEV - 1) * a + rel - 1
            copies.append(pltpu.make_async_remote_copy(
                src_ref=g.at[2 * px + py, pl.ds(pl.multiple_of(pc * h, 16), h)], dst_ref=r.at[rel - 1],
                send_sem=send.at[sem], recv_sem=recv.at[sem], device_id=(px, py, pc), device_id_type=MESH))
    return copies


def _device_exchange_specs(grads):
    anyspace = pl.BlockSpec(memory_space=pl.ANY)
    n = len(grads)
    count = (N_DEV - 1) * n
    return ([anyspace] * n, [anyspace] * n,
            [SDS((N_DEV - 1, g.shape[1] // 2, g.shape[2]), g.dtype) for g in grads],
            [pltpu.SemaphoreType.DMA((count,)), pltpu.SemaphoreType.DMA((count,))])


def _a_bwd(doa, ya, conv, proj, conv_w, w_out, tm, parts):
    s = doa.shape[0]
    nt = s // tm
    n = len(parts)
    ex_in, ex_out, ex_shape, ex_sems = _device_exchange_specs(parts)

    def body(*refs):
        doa_ref, ya_ref, conv_ref, proj_ref, cw_ref, w_ref = refs[:6]
        part_refs = refs[6:6 + n]
        dproj_ref, dcw_ref, dw_ref, dw16_ref = refs[6 + n:10 + n]
        recv_refs = refs[10 + n:10 + 2 * n]
        carry, dw_acc, stage, put_sem, send, recv = refs[10 + 2 * n:]
        i = pl.program_id(0)

        @pl.when(i == 0)
        def _():
            dcw_ref[...] = jnp.zeros_like(dcw_ref)
            carry[...] = jnp.zeros_like(carry)
            dw_acc[...] = jnp.zeros_like(dw_acc)
            for cp in _device_exchange(part_refs, recv_refs, send, recv):
                cp.start()
        dya = _nt(doa_ref[...], w_ref[...])
        dw_acc[...] += _tn(ya_ref[...], doa_ref[...])
        bg = proj_ref[:, 0:D].astype(F32)
        cg = proj_ref[:, D:2 * D].astype(F32)
        u = proj_ref[:, 2 * D:3 * D].astype(F32)
        z = proj_ref[:, 3 * D:4 * D].astype(F32)
        v = cg * u
        rows = lax.broadcasted_iota(jnp.int32, (tm, D), 0)
        conv = conv_ref[...].astype(F32)
        sg, sz = _silu_parts(z)
        dproj_ref[:, 0:D] = (dya * conv * sz).astype(BF16)
        dproj_ref[:, 3 * D:4 * D] = (dya * bg * conv * _dsilu(z, sg)).astype(BF16)
        dconv = dya * bg * sz
        after = carry[...]
        up1 = jnp.where(rows < tm - 1, pltpu.roll(dconv, tm - 1, 0), after[0:1, :])
        up2 = jnp.where(rows < tm - 2, pltpu.roll(dconv, tm - 2, 0),
                        jnp.where(rows == tm - 2, after[0:1, :], after[1:2, :]))
        carry[...] = dconv[0:8, :]
        _acc_row(dcw_ref, 0, jnp.sum(up2 * v, axis=0, keepdims=True))
        _acc_row(dcw_ref, 1, jnp.sum(up1 * v, axis=0, keepdims=True))
        _acc_row(dcw_ref, 2, jnp.sum(dconv * v, axis=0, keepdims=True))
        dv = cw_ref[2:3, :] * dconv + cw_ref[1:2, :] * up1 + cw_ref[0:1, :] * up2
        dproj_ref[:, D:2 * D] = (dv * u).astype(BF16)
        dproj_ref[:, 2 * D:3 * D] = (dv * cg).astype(BF16)

        @pl.when(i == nt - 1)
        def _():
            _write_gradient(dw_acc, dw_ref, dw16_ref, stage, put_sem)
            for cp in _device_exchange(part_refs, recv_refs, send, recv):
                cp.wait()

    rev = lambda i: (nt - 1 - i, 0)
    fix = lambda i: (0, 0)
    anyspace = pl.BlockSpec(memory_space=pl.ANY)
    dproj, dcw, dw, dw16, *got = pl.pallas_call(
        body, name="a_bwd", grid=(nt,),
        in_specs=[pl.BlockSpec((tm, D), rev), pl.BlockSpec((tm, D), rev), pl.BlockSpec((tm, D), rev),
                  pl.BlockSpec((tm, 4 * D), rev), pl.BlockSpec((8, D), fix), pl.BlockSpec((D, D), fix)] + ex_in,
        out_specs=[pl.BlockSpec((tm, 4 * D), rev), pl.BlockSpec((8, D), fix), anyspace, anyspace] + ex_out,
        out_shape=[SDS((s, 4 * D), BF16), SDS((8, D), F32), SDS((D, D), F32), SDS((D, D), BF16)] + ex_shape,
        scratch_shapes=[pltpu.VMEM((8, D), F32), pltpu.VMEM((D, D), F32), pltpu.VMEM((D // 4, D), BF16),
                        pltpu.SemaphoreType.DMA] + ex_sems,
        compiler_params=_params(("arbitrary",)),
    )(doa, ya, conv, proj, conv_w, w_out, *parts)
    return dproj, dcw, dw, dw16, got


def _dn1(dp_ref, w_ref):
    dn = _nt(dp_ref[:, 0:D], w_ref[0])
    for j in range(1, 4):
        dn = dn + _nt(dp_ref[:, D * j:D * (j + 1)], w_ref[j])
    return dn


def _a_in_bwd_matmul(dproj, win_g, tm, count, win_half, win_got):
    def body(dp_ref, w_ref, half_ref, got_in, dn_ref, got_ref, wcat, send, recv):
        del got_in

        @pl.when(pl.program_id(0) == 0)
        def _():
            _to_owner_core(half_ref, got_ref, send, recv, 1, "start")
            for j in range(N_CHIPS):
                pltpu.sync_copy(w_ref.at[j], wcat.at[:, pl.ds(D * j, D)])
        dn_ref[...] = _nt(dp_ref[...], wcat[...]).astype(BF16)

        @pl.when(pl.program_id(0) == count - 1)
        def _():
            _to_owner_core(half_ref, got_ref, send, recv, 1, "wait")

    row = lambda i: (i, 0)
    anyspace = pl.BlockSpec(memory_space=pl.ANY)
    return pl.pallas_call(
        body, name="a_in_bwd_matmul", grid=(count,),
        in_specs=[pl.BlockSpec((tm, 4 * D), row), anyspace, anyspace, anyspace],
        out_specs=[pl.BlockSpec((tm, D), row), anyspace],
        out_shape=[SDS((count * tm, D), BF16), SDS(win_got.shape, win_got.dtype)],
        scratch_shapes=[pltpu.VMEM((D, 4 * D), BF16)] + _owner_core_sems(),
        input_output_aliases={3: 1},
        compiler_params=_params(("arbitrary",)),
    )(dproj, win_g, win_half, win_got)


def _a_in_bwd(dn_first, dproj, x, dh1, win_g, g_pre, tm):
    s = x.shape[0]
    nt = s // tm
    count = dn_first.shape[0] // tm

    def body(dn_ref, dp_ref, x_ref, dh_ref, w_ref, g_ref, gx_ref, dg_ref, dn_s):
        i = pl.program_id(0)

        @pl.when(i == 0)
        def _():
            dg_ref[...] = jnp.zeros_like(dg_ref)

        @pl.when(i < count)
        def _():
            dn_s[...] = dn_ref[...].astype(F32)

        @pl.when(i >= count)
        def _():
            dn_s[...] = _dn1(dp_ref, w_ref)
        dn = dn_s[...]
        xv = x_ref[...]
        r = _rms_scale(xv)
        xh = xv * r
        _acc_row(dg_ref, 0, jnp.sum(dn * xh, axis=0, keepdims=True))
        dxh = dn * g_ref[...]
        gx_ref[...] = dh_ref[...] + r * (dxh - xh * jnp.mean(dxh * xh, axis=-1, keepdims=True))

    row = lambda i: (i, 0)
    fix = lambda i: (0, 0)
    return pl.pallas_call(
        body, name="a_in_bwd", grid=(nt,),
        in_specs=[pl.BlockSpec((tm, D), lambda i: (jnp.minimum(i, count - 1), 0)),
                  pl.BlockSpec((tm, 4 * D), lambda i: (jnp.maximum(i, count), 0)),
                  pl.BlockSpec((tm, D), row), pl.BlockSpec((tm, D), row),
                  pl.BlockSpec((4, D, D), lambda i: (0, 0, 0)), pl.BlockSpec((1, D), fix)],
        out_specs=[pl.BlockSpec((tm, D), row), pl.BlockSpec((8, D), fix)],
        out_shape=[SDS((s, D), F32), SDS((8, D), F32)],
        scratch_shapes=[pltpu.VMEM((tm, D), F32)],
        compiler_params=_params(("arbitrary",)),
    )(dn_first, dproj, x, dh1, win_g, g_pre)


def _swap_halves(shards, send, recv):
    x, y, c = lax.axis_index("x"), lax.axis_index("y"), lax.axis_index("c")
    sibling = (x, y, 1 - c)
    copies = []
    for b, full in enumerate(shards):
        h = full.shape[0] // 2
        mine = full.at[pl.ds(pl.multiple_of(c * h, 8), h)]
        theirs = full.at[pl.ds(pl.multiple_of((1 - c) * h, 8), h)]
        copies.append((pltpu.make_async_remote_copy(src_ref=mine, dst_ref=mine, send_sem=send.at[b], recv_sem=recv.at[b],
                                                    device_id=sibling, device_id_type=MESH),
                       pltpu.make_async_remote_copy(src_ref=mine, dst_ref=theirs, send_sem=send.at[b], recv_sem=recv.at[b],
                                                    device_id=sibling, device_id_type=MESH)))
    return copies


def _dw_in_half(n1, dproj, core, tmw, name, to_owners=None, to_devices=None, shards=()):
    s = n1.shape[0]
    h = D // 2
    nt = s // tmw
    n_sh = len(shards)
    if to_owners is not None:
        sent_array, sems, got_shape = to_owners, _owner_core_sems(), SDS((N_DEV - 1, h, D), BF16)
    else:
        sent_array = to_devices
        _, _, (got_shape,), sems = _device_exchange_specs([to_devices])

    def body(*refs):
        a_ref, b_ref, sent = refs[:3]
        o_ref, o16_ref, got = refs[3 + n_sh:6 + n_sh]
        shard_refs = refs[6 + n_sh:6 + 2 * n_sh]
        send, recv = refs[6 + 2 * n_sh:8 + 2 * n_sh]
        swap_sems = refs[8 + 2 * n_sh:]
        j, t = pl.program_id(0), pl.program_id(1)

        def exchange(action):
            if to_owners is not None:
                _to_owner_core(sent, got, send, recv, 1 - core, action)
            else:
                for cp in _device_exchange([sent], [got], send, recv):
                    cp.start() if action == "start" else cp.wait()

        @pl.when((j == 0) & (t == 0))
        def _():
            exchange("start")
            if n_sh:
                for mine, _ in _swap_halves(shard_refs, *swap_sems):
                    mine.start()

        @pl.when(t == 0)
        def _():
            o_ref[...] = jnp.zeros_like(o_ref)
        o_ref[0] += _tn(a_ref[...], b_ref[...])

        @pl.when(t == nt - 1)
        def _():
            o16_ref[...] = o_ref[...].astype(BF16)

        @pl.when((j == N_CHIPS - 1) & (t == nt - 1))
        def _():
            exchange("wait")
            if n_sh:
                for mine, theirs in _swap_halves(shard_refs, *swap_sems):
                    theirs.wait_recv()
                    mine.wait_send()

    anyspace = pl.BlockSpec(memory_space=pl.ANY)
    slot = pl.BlockSpec((1, h, D), lambda j, t: (j, 0, 0))
    swap_scratch = [pltpu.SemaphoreType.DMA((n_sh,)), pltpu.SemaphoreType.DMA((n_sh,))] if n_sh else []
    return pl.pallas_call(
        body, name=name, grid=(N_CHIPS, nt),
        in_specs=[pl.BlockSpec((tmw, h), lambda j, t: (t, core)), pl.BlockSpec((tmw, D), lambda j, t: (t, j))]
        + [anyspace] * (1 + n_sh),
        out_specs=[slot, slot] + [anyspace] * (1 + n_sh),
        out_shape=[SDS((N_CHIPS, h, D), F32), SDS((N_CHIPS, h, D), BF16), got_shape]
        + [SDS(sh.shape, F32) for sh in shards],
        scratch_shapes=sems + swap_scratch,
        input_output_aliases={3 + b: 3 + b for b in range(n_sh)},
        compiler_params=_params(("arbitrary", "arbitrary")),
    )(n1, dproj, sent_array, *shards)


def _share_and_gather(shards, smalls):
    n_h, n_s = len(shards), len(smalls)

    def body(*refs):
        small_ins = refs[n_h:n_h + n_s]
        fs = refs[n_h + n_s:2 * n_h + n_s]
        small_alls = refs[2 * n_h + n_s:2 * n_h + 2 * n_s]
        dsend, drecv, ssend, srecv = refs[2 * n_h + 2 * n_s:]
        x, y, c = lax.axis_index("x"), lax.axis_index("y"), lax.axis_index("c")
        swaps = _swap_halves(fs, dsend, drecv)
        sends, arrivals = [mine for mine, _ in swaps], [theirs for _, theirs in swaps]
        me = 4 * x + 2 * y + c
        for k, (small_in, small_all) in enumerate(zip(small_ins, small_alls)):
            small_all[me] = small_in[...]
            for rel in range(1, N_DEV):
                fx, fy, fc = rel >> 2, (rel >> 1) & 1, rel & 1
                peer = (x + fx - 2 * x * fx, y + fy - 2 * y * fy, c + fc - 2 * c * fc)
                sender = 4 * peer[0] + 2 * peer[1] + peer[2]
                sem = (N_DEV - 1) * k + rel - 1
                sends.append(pltpu.make_async_remote_copy(
                    src_ref=small_in, dst_ref=small_all.at[me], send_sem=ssend.at[sem], recv_sem=srecv.at[sem],
                    device_id=peer, device_id_type=MESH))
                arrivals.append(pltpu.make_async_remote_copy(
                    src_ref=small_in, dst_ref=small_all.at[sender], send_sem=ssend.at[sem], recv_sem=srecv.at[sem],
                    device_id=peer, device_id_type=MESH))
        for cp in sends:
            cp.start()
        for cp in arrivals:
            cp.wait_recv()
        for cp in sends:
            cp.wait_send()

    anyspace = pl.BlockSpec(memory_space=pl.ANY)
    vm = pl.BlockSpec(memory_space=pltpu.VMEM)
    out_shape = [SDS(full.shape, F32) for full in shards] + [SDS((N_DEV,) + sm.shape, F32) for sm in smalls]
    n_all = (N_DEV - 1) * n_s
    outs = pl.pallas_call(
        body, name="share_and_gather", out_shape=out_shape,
        in_specs=[anyspace] * n_h + [vm] * n_s, out_specs=[anyspace] * n_h + [vm] * n_s,
        scratch_shapes=[pltpu.SemaphoreType.DMA((n_h,)), pltpu.SemaphoreType.DMA((n_h,)),
                        pltpu.SemaphoreType.DMA((n_all,)), pltpu.SemaphoreType.DMA((n_all,))],
        input_output_aliases={b: b for b in range(n_h)},
    )(*shards, *smalls)
    return outs[:n_h], outs[n_h:]


def _add_win(where, lo, hi, r, name):
    _, h, cols = lo.shape
    tr = min(h, 256)
    nh = h // tr

    def body(where_ref, lo_ref, hi_ref, r_ref, o_ref):
        acc = jnp.where(where_ref[0] == 0, lo_ref[0], hi_ref[0])
        for k in range(N_DEV - 1):
            acc = acc + r_ref[k].astype(F32)
        o_ref[...] = acc

    own = pl.BlockSpec((1, tr, cols), lambda i, w: (w[1], i, 0))
    return pl.pallas_call(
        body, name=name,
        grid_spec=pltpu.PrefetchScalarGridSpec(
            num_scalar_prefetch=1, grid=(nh,),
            in_specs=[own, own, pl.BlockSpec((N_DEV - 1, tr, cols), lambda i, w: (0, i, 0))],
            out_specs=pl.BlockSpec((tr, cols), lambda i, w: (w[0] * nh + i, 0))),
        out_shape=SDS((2 * h, cols), F32),
        compiler_params=_params(("parallel",)),
    )(where, lo, hi, r)


def _add_devices(where, g, r, name):
    _, rows, cols = g.shape
    h = rows // 2
    tr = min(h, 256)
    nh = h // tr

    def body(where_ref, g_ref, r_ref, o_ref):
        del where_ref
        acc = g_ref[0]
        for k in range(N_DEV - 1):
            acc = acc + r_ref[k].astype(F32)
        o_ref[...] = acc

    return pl.pallas_call(
        body, name=name,
        grid_spec=pltpu.PrefetchScalarGridSpec(
            num_scalar_prefetch=1, grid=(nh,),
            in_specs=[pl.BlockSpec((1, tr, cols), lambda i, w: (w[1], w[0] * nh + i, 0)),
                      pl.BlockSpec((N_DEV - 1, tr, cols), lambda i, w: (0, i, 0))],
            out_specs=pl.BlockSpec((tr, cols), lambda i, w: (w[0] * nh + i, 0))),
        out_shape=SDS((rows, cols), F32),
        compiler_params=_params(("parallel",)),
    )(where, g, r)


def _sum_smalls(gathered):
    n = len(gathered)

    def body(*refs):
        for all_ref, o_ref in zip(refs[:n], refs[n:]):
            acc = all_ref[0]
            for dev in range(1, N_DEV):
                acc = acc + all_ref[dev]
            o_ref[...] = acc

    vm = pl.BlockSpec(memory_space=pltpu.VMEM)
    return pl.pallas_call(
        body, name="sum_smalls", out_shape=[SDS(a.shape[1:], F32) for a in gathered],
        in_specs=[vm] * n, out_specs=[vm] * n,
    )(*gathered)


def _adam_step(g, w, m, v):
    nm = ADAM_B1 * m + (1.0 - ADAM_B1) * g
    nv = ADAM_B2 * v + (1.0 - ADAM_B2) * (g * g)
    m_hat = nm / (1.0 - ADAM_B1 ** ADAM_STEP)
    v_hat = nv / (1.0 - ADAM_B2 ** ADAM_STEP)
    return -ADAM_LR * (m_hat / (jnp.sqrt(v_hat) + ADAM_EPS) + ADAM_WD * w), nm, nv


def _adamw(g, w, m, v, name):
    rows, cols = g.shape
    tr = min(rows, 256)

    def body(g_ref, w_ref, m_ref, v_ref, d_ref, nm_ref, nv_ref):
        d_ref[...], nm_ref[...], nv_ref[...] = _adam_step(g_ref[...], w_ref[...], m_ref[...], v_ref[...])

    spec = pl.BlockSpec((tr, cols), lambda i: (i, 0))
    return pl.pallas_call(
        body, name=name, grid=(rows // tr,), in_specs=[spec] * 4, out_specs=[spec] * 3,
        out_shape=[SDS(g.shape, F32)] * 3, compiler_params=_params(("parallel",)),
    )(g, w, m, v)


def _small_update(chip, tot, tot_rel, wmv):
    names = list(SMALL_PLACES)
    n = len(names)

    def body(chip_ref, tot_ref, quarter_ref, rel_ref, *refs):
        del chip_ref
        ins, outs = refs[:3 * n], refs[3 * n:]
        for i, nm in enumerate(names):
            source, row, (rows, cols) = SMALL_PLACES[nm]
            g = {"rows": tot_ref, "quarter": quarter_ref, "rel": rel_ref}[source][row:row + rows, 0:cols]
            outs[4 * i][...] = g
            outs[4 * i + 1][...], outs[4 * i + 2][...], outs[4 * i + 3][...] = _adam_step(
                g, ins[3 * i][...], ins[3 * i + 1][...], ins[3 * i + 2][...])

    whole = lambda shape: pl.BlockSpec(shape, lambda i, c: (0,) * len(shape))
    shapes = [SMALL_PLACES[nm][2] for nm in names]
    outs = pl.pallas_call(
        body, name="small_update",
        grid_spec=pltpu.PrefetchScalarGridSpec(
            num_scalar_prefetch=1, grid=(1,),
            in_specs=[whole(tot.shape), pl.BlockSpec((tot.shape[0], D // 4), lambda i, c: (0, c[0])),
                      whole(tot_rel.shape)] + [whole(shp) for shp in shapes for _ in range(3)],
            out_specs=[whole(shp) for shp in shapes for _ in range(4)]),
        out_shape=[SDS(shp, F32) for shp in shapes for _ in range(4)],
    )(chip, tot, tot, tot_rel, *[a for nm in names for a in wmv[nm]])
    return {nm: tuple(outs[4 * i:4 * i + 4]) for i, nm in enumerate(names)}


def _pad_rows(a, rows):
    return jnp.concatenate([a, jnp.zeros((rows - a.shape[0], a.shape[1]), a.dtype)], axis=0)


def _pad_cols(a, cols):
    return jnp.concatenate([a, jnp.zeros((a.shape[0], cols - a.shape[1]), a.dtype)], axis=1)


def kernel(x, a_pre_norm, a_w_in, a_conv_w, a_w_out, a_post_norm, kv_norm, w_kv, rel_bias, b_pre_norm, b_w_in, b_sinks, b_w_out, b_post_norm, loss_target, m_a_pre_norm, m_a_w_in, m_a_conv_w, m_a_w_out, m_a_post_norm, m_kv_norm, m_w_kv, m_rel_bias, m_b_pre_norm, m_b_w_in, m_b_sinks, m_b_w_out, m_b_post_norm, v_a_pre_norm, v_a_w_in, v_a_conv_w, v_a_w_out, v_a_post_norm, v_kv_norm, v_w_kv, v_rel_bias, v_b_pre_norm, v_b_w_in, v_b_sinks, v_b_w_out, v_b_post_norm):
    seq = x.shape[1]
    xs = x.reshape(seq, D)
    tgt = loss_target.reshape(seq, D)
    chip = 2 * lax.axis_index("x") + lax.axis_index("y")
    core = lax.axis_index("c")
    tm = _tile(seq, 512)
    tmw = _tile(seq, 1024)

    shards = [a_w_in[0], a_w_out[0], w_kv, b_w_in[0], b_w_out[0]]
    small_w = _pad_rows(jnp.concatenate([a_pre_norm, a_conv_w[0], a_post_norm], axis=0), 8)
    *own_only, small_g = _prepare_weights(shards, small_w)
    where = jnp.stack([core, chip]).astype(jnp.int32)
    small_full = small_g.transpose(1, 0, 2).reshape(8, D)
    g_apre, conv_w, g_apost = small_full[0:1], _pad_rows(small_full[1:4], 8), small_full[4:5]
    g_kv = kv_norm.reshape(1, D)

    proj, n1, (win_g, wouta_g, wkv_g, wbin_g, woutb_g) = _a_in(where[1:2], xs, g_apre, own_only, tmw)
    wouta = wouta_g.reshape(D, D)
    wkv = wkv_g.reshape(D, 2 * KV_W)
    woutb = woutb_g.reshape(D, D)
    ya, oa, h1, conv = _a_mix(proj, xs, conv_w, wouta, g_apost, tm)
    kv, q, zb = _b_in(h1, g_kv, b_pre_norm, wkv, wbin_g, tmw)
    tab = _bias_table(rel_bias, b_sinks.reshape(N_HEADS))
    att, stats = _attn_fwd(q, kv, tab)
    dh2, dqz, datt, loss_acc, dg_bpost, dw_outb, dw_outb16 = _mid(att, zb, h1, tgt, woutb, b_post_norm, tm)

    dqz, dkv, dtab = _attn_bwd(q, kv, datt, stats, tab, dqz)
    dh1, doa, dg_b, dw_bin, dw_kv, dw_bin16, dw_kv16 = _b_bwd(dqz, dkv, h1, dh2, oa, wbin_g, wkv, g_kv, b_pre_norm,
                                                              g_apost, tm)
    by_chip = lambda a, cols: a.reshape(N_CHIPS, D // 4, cols)
    grads1 = [by_chip(dw_kv, 2 * KV_W), dw_bin, by_chip(dw_outb, D)]
    sent1 = [by_chip(dw_kv16, 2 * KV_W), dw_bin16, by_chip(dw_outb16, D)]
    names1 = ["w_kv", "b_w_in", "b_w_out"]
    dproj, dconv_w, dw_outa, dw_outa16, from_devices1 = _a_bwd(doa, ya, conv, proj, conv_w, wouta, tm, sent1)
    shards1 = [_add_devices(where, g, r, "add_devices_" + nm) for g, r, nm in zip(grads1, from_devices1, names1)]
    tmw2 = _tile(seq, 4096)
    win_lo, win_lo16, outa_got, g_wkv, g_wbin, g_woutb = _dw_in_half(
        n1, dproj, 0, tmw2, "dw_a_in_lo", to_devices=by_chip(dw_outa16, D), shards=shards1)
    win_hi, win_hi16, win_got = _dw_in_half(n1, dproj, 1, tmw2, "dw_a_in_hi", to_owners=win_lo16)
    nt = seq // tmw
    dn_first, win_got = _a_in_bwd_matmul(dproj, win_g, tmw, max(nt - max(nt // 4, 1), 1), win_hi16, win_got)
    grad_x, dg_apre = _a_in_bwd(dn_first, dproj, xs, dh1, win_g, g_apre, tm)
    shards2 = [_add_win(where, win_lo, win_hi, win_got, "add_devices_a_w_in"),
               _add_devices(where, by_chip(dw_outa, D), outa_got, "add_devices_a_w_out")]
    drel, dsink = _bias_fold(dtab)

    smalls = jnp.concatenate([
        dg_apre[0:1], dg_b[2:3], dg_b[0:1], dg_b[1:2], dg_bpost[0:1], _pad_cols(dsink[0:1], D),
        _pad_cols(loss_acc[0:1], D), jnp.zeros((1, D), F32), dconv_w], axis=0)
    assert smalls.shape == (SMALL_ROWS, D)
    (g_win, g_wouta), gathered = _share_and_gather(shards2, (smalls, drel))
    tot, tot_rel = _sum_smalls(gathered)

    big = {}
    for nm, g, w, m, v in [("a_w_in", g_win, a_w_in, m_a_w_in, v_a_w_in), ("a_w_out", g_wouta, a_w_out, m_a_w_out, v_a_w_out),
                           ("w_kv", g_wkv, w_kv, m_w_kv, v_w_kv), ("b_w_in", g_wbin, b_w_in, m_b_w_in, v_b_w_in),
                           ("b_w_out", g_woutb, b_w_out, m_b_w_out, v_b_w_out)]:
        shp = w.shape
        two = (shp[-2], shp[-1])
        d, nm_, nv_ = _adamw(g, w.reshape(two), m.reshape(two), v.reshape(two), "adamw_" + nm)
        big[nm] = (g.reshape(shp), d.reshape(shp), nm_.reshape(shp), nv_.reshape(shp))

    given = {"a_pre_norm": (a_pre_norm, m_a_pre_norm, v_a_pre_norm), "a_conv_w": (a_conv_w, m_a_conv_w, v_a_conv_w),
             "a_post_norm": (a_post_norm, m_a_post_norm, v_a_post_norm), "kv_norm": (kv_norm, m_kv_norm, v_kv_norm),
             "rel_bias": (rel_bias, m_rel_bias, v_rel_bias), "b_pre_norm": (b_pre_norm, m_b_pre_norm, v_b_pre_norm),
             "b_sinks": (b_sinks, m_b_sinks, v_b_sinks), "b_post_norm": (b_post_norm, m_b_post_norm, v_b_post_norm)}
    small = _small_update(where[1:2], tot, tot_rel, {nm: tuple(a.reshape(SMALL_PLACES[nm][2]) for a in wmv)
                                            for nm, wmv in given.items()})
    order = ["a_pre_norm", "a_w_in", "a_conv_w", "a_w_out", "a_post_norm", "kv_norm", "w_kv", "rel_bias",
             "b_pre_norm", "b_w_in", "b_sinks", "b_w_out", "b_post_norm"]
    outs = []
    for which in range(4):
        for nm in order:
            outs.append(big[nm][which] if nm in big else small[nm][which].reshape(given[nm][0].shape))
    loss = 0.5 * tot[LOSS_ROW, 0]
    return (loss, grad_x.reshape(x.shape), *outs)
```

```python
import math

import jax
import jax.numpy as jnp
from jax import lax
from jax.experimental import pallas as pl
from jax.experimental.pallas import tpu as pltpu

F32 = jnp.float32
BF16 = jnp.bfloat16
MESH = pl.DeviceIdType.MESH
SDS = jax.ShapeDtypeStruct

D = 1024
HEAD_DIM = 64
N_HEADS = 16
N_KV = 2
GROUP = 8
KV_W = 128
BLK = 128
N_BUCKETS = 32
MAX_EXACT = 16
MAX_DISTANCE = 128
EPS = 1e-6
NEG_INF = -1e30
Q_SCALE = HEAD_DIM ** -0.5

ADAM_LR = 0.001
ADAM_B1 = 0.9
ADAM_B2 = 0.999
ADAM_EPS = 1e-08
ADAM_WD = 0.01
ADAM_STEP = 10

N_CHIPS = 4
N_DEV = 8
BIN_COLS = 2 * D // N_CHIPS
VMEM_LIMIT = 56 * 1024 * 1024
SMALL_ROWS = 16
LOSS_ROW = 6
SMALL_PLACES = {
    "a_pre_norm": ("quarter", 0, (1, D // 4)), "a_conv_w": ("quarter", 8, (3, D // 4)),
    "a_post_norm": ("quarter", 1, (1, D // 4)), "kv_norm": ("rows", 2, (1, D)),
    "rel_bias": ("rel", 0, (N_BUCKETS, N_HEADS)), "b_pre_norm": ("rows", 3, (1, D)),
    "b_sinks": ("rows", 5, (1, N_HEADS)), "b_post_norm": ("rows", 4, (1, D)),
}


def _bucket_thresholds():
    def bucket(d):
        big = MAX_EXACT + int(math.log(d / MAX_EXACT) / math.log(MAX_DISTANCE / MAX_EXACT)
                              * (N_BUCKETS - MAX_EXACT))
        return d if d < MAX_EXACT else min(big, N_BUCKETS - 1)
    out = []
    for b in range(MAX_EXACT + 1, N_BUCKETS):
        out.append(min(d for d in range(MAX_EXACT, MAX_DISTANCE) if bucket(d) >= b))
    return tuple(out)


BUCKET_THRESHOLDS = _bucket_thresholds()


def _params(semantics=None, vmem=VMEM_LIMIT):
    return pltpu.CompilerParams(dimension_semantics=semantics, vmem_limit_bytes=vmem)


def _tile(n, pref):
    return pref if n >= 2 * pref else max(n // 2, 8)


def _rms_scale(v):
    return lax.rsqrt(jnp.mean(v * v, axis=-1, keepdims=True) + EPS)


def _nt(a, b):
    return lax.dot_general(a, b, (((1,), (1,)), ((), ())), preferred_element_type=F32)


def _tn(a, b):
    return lax.dot_general(a, b, (((0,), (0,)), ((), ())), preferred_element_type=F32)


def _nn(a, b):
    return jnp.dot(a, b, preferred_element_type=F32)


def _silu_parts(z):
    sg = jax.nn.sigmoid(z)
    return sg, z * sg


def _dsilu(z, sg):
    return sg * (1.0 + z * (1.0 - sg))


def _write_gradient(acc, out32, out16, stage, sem):
    whole = pltpu.make_async_copy(acc, out32, sem)
    whole.start()
    rows = stage.shape[0]
    for k in range(acc.shape[0] // rows):
        stage[...] = acc[rows * k:rows * (k + 1), :].astype(BF16)
        pltpu.sync_copy(stage, out16.at[pl.ds(rows * k, rows)])
    whole.wait()


def _acc_row(ref, row, val):
    ref[row:row + 1, :] += val


def _gather_copies(outs, splits, ici_send, ici_recv, d2d_send, d2d_recv):
    x, y, c = lax.axis_index("x"), lax.axis_index("y"), lax.axis_index("c")
    k = 2 * x + y
    sibling = (x, y, 1 - c)

    def part(o_ref, chip, core, split):
        if not split:
            return o_ref.at[chip]
        h = o_ref.shape[1] // 2
        return o_ref.at[chip, pl.ds(pl.multiple_of(core * h, 16), h)]

    def remote(ref, a, j, sems, to):
        return pltpu.make_async_remote_copy(src_ref=ref, dst_ref=ref, send_sem=sems[0].at[3 * a + j],
                                            recv_sem=sems[1].at[3 * a + j], device_id=to, device_id_type=MESH)

    copies = []
    for a, (o_ref, split) in enumerate(zip(outs, splits)):
        for j, (px, py) in enumerate([(x, 1 - y), (1 - x, y), (1 - x, 1 - y)]):
            kj = 2 * px + py
            ici, d2d = (ici_send, ici_recv), (d2d_send, d2d_recv)
            copies.append((remote(part(o_ref, k, c, split), a, j, ici, (px, py, c)),
                           remote(part(o_ref, kj, c, split), a, j, ici, (px, py, c)),
                           remote(part(o_ref, kj, c, split), a, j, d2d, sibling) if split else None,
                           remote(part(o_ref, kj, 1 - c, split), a, j, d2d, sibling) if split else None))
    return copies


def _gather_sems(n):
    return [pltpu.SemaphoreType.DMA((3 * n,)) for _ in range(4)]


def _prepare_weights(shards, small):
    n = len(shards)

    def body(*refs):
        ins, small_in = refs[:n], refs[n]
        outs, small_out = refs[n + 1:2 * n + 1], refs[2 * n + 1]
        stages, put_sem = refs[2 * n + 2:3 * n + 2], refs[3 * n + 2]
        sems = refs[3 * n + 3:]
        k = 2 * lax.axis_index("x") + lax.axis_index("y")
        puts = []
        for a, (i_ref, stage, o_ref) in enumerate(zip(ins, stages, outs)):
            stage[...] = i_ref[...].astype(BF16)
            puts.append(pltpu.make_async_copy(stage, o_ref.at[k], put_sem.at[a]))
            puts[-1].start()
        small_out[k] = small_in[...]
        copies = _gather_copies([small_out], [False], *sems)
        for send, _, _, _ in copies:
            send.start()
        for _, arrival, _, _ in copies:
            arrival.wait_recv()
        for send, _, _, _ in copies:
            send.wait_send()
        for put in puts:
            put.wait()

    vm = pl.BlockSpec(memory_space=pltpu.VMEM)
    anyspace = pl.BlockSpec(memory_space=pl.ANY)
    out_shape = [SDS((N_CHIPS,) + s.shape, BF16) for s in shards] + [SDS((N_CHIPS,) + small.shape, F32)]
    return pl.pallas_call(
        body, name="prepare_weights", out_shape=out_shape,
        in_specs=[vm] * (n + 1), out_specs=[anyspace] * n + [vm],
        scratch_shapes=[pltpu.VMEM(s.shape, BF16) for s in shards] + [pltpu.SemaphoreType.DMA((n,))] + _gather_sems(1),
        compiler_params=pltpu.CompilerParams(vmem_limit_bytes=VMEM_LIMIT),
    )(*shards, small)


def _a_in(chip, x, g_pre, weights, tm):
    s = x.shape[0]
    nt = s // tm
    n = len(weights)

    def body(chip_ref, x_ref, g_ref, *refs):
        proj_ref, n1_ref = refs[n:n + 2]
        gathered = refs[n + 2:2 * n + 2]
        wbuf, n1_all, fetch_sem = refs[2 * n + 2:2 * n + 5]
        sems = refs[2 * n + 5:]
        jj, i = pl.program_id(0), pl.program_id(1)
        copies = _gather_copies(gathered, [True] * n, *sems)

        def fetch(rel):
            slot = jnp.bitwise_xor(chip_ref[0], rel)
            return pltpu.make_async_copy(gathered[0].at[slot], wbuf.at[rel % 2], fetch_sem.at[rel % 2])

        @pl.when((jj == 0) & (i == 0))
        def _():
            fetch(0).start()
            copies[0][0].start()
            copies[1][0].start()
            fetch(0).wait()

        for rel in (1, 2, 3):
            @pl.when((jj == rel) & (i == 0))
            def _():
                fetch(rel).wait()

        @pl.when(jj == 0)
        def _():
            xv = x_ref[...]
            n1 = (xv * _rms_scale(xv) * g_ref[...]).astype(BF16)
            n1_ref[...] = n1
            n1_all[i] = n1
        proj_ref[...] = _nn(n1_all[i], wbuf[jj % 2]).astype(BF16)

        for rel in (1, 2, 3):
            @pl.when((jj == rel - 1) & (i == max(nt - 3, 0)))
            def _():
                _, arrival, forward, _ = copies[rel - 1]
                arrival.wait_recv()
                forward.start()
                if rel == 1:
                    for send, _, _, _ in copies[2:]:
                        send.start()

            @pl.when((jj == rel - 1) & (i == max(nt - 2, 0)))
            def _():
                copies[rel - 1][3].wait_recv()
                fetch(rel).start()

        @pl.when((jj == 3) & (i == max(nt - 2, 0)))
        def _():
            for _, arrival, forward, _ in copies[3:]:
                arrival.wait_recv()
                forward.start()

        @pl.when((jj == 3) & (i == nt - 1))
        def _():
            for _, _, _, forwarded in copies[3:]:
                forwarded.wait_recv()
            for send, _, forward, _ in copies:
                forward.wait_send()
                send.wait_send()

    anyspace = pl.BlockSpec(memory_space=pl.ANY)
    proj, n1, *gathered = pl.pallas_call(
        body, name="a_in",
        grid_spec=pltpu.PrefetchScalarGridSpec(
            num_scalar_prefetch=1, grid=(4, nt),
            in_specs=[pl.BlockSpec((tm, D), lambda jj, i, c: (jnp.where(jj == 0, i, nt - 1), 0)),
                      pl.BlockSpec((1, D), lambda jj, i, c: (0, 0))] + [anyspace] * n,
            out_specs=[pl.BlockSpec((tm, D), lambda jj, i, c: (i, jnp.bitwise_xor(c[0], jj))),
                       pl.BlockSpec((tm, D), lambda jj, i, c: (jnp.where(jj == 0, i, nt - 1), 0))] + [anyspace] * n,
            scratch_shapes=[pltpu.VMEM((2, D, D), BF16), pltpu.VMEM((nt, tm, D), BF16),
                            pltpu.SemaphoreType.DMA((2,))] + _gather_sems(n)),
        out_shape=[SDS((s, 4 * D), BF16), SDS((s, D), BF16)] + [SDS(w.shape, w.dtype) for w in weights],
        input_output_aliases={3 + a: 2 + a for a in range(n)},
        compiler_params=_params(("arbitrary", "arbitrary")),
    )(chip, x, g_pre, *weights)
    return proj, n1, gathered


def _shift_rows(v, last, second_last, rows):
    v1 = jnp.where(rows >= 1, pltpu.roll(v, 1, 0), last)
    v2 = jnp.where(rows >= 2, pltpu.roll(v, 2, 0), jnp.where(rows == 1, last, second_last))
    return v1, v2


def _a_mix(proj, x, conv_w, w_out, g_post, tm):
    s = x.shape[0]

    def body(proj_ref, x_ref, cw_ref, w_ref, g_ref, ya_ref, oa_ref, h1_ref, conv_ref, carry):
        @pl.when(pl.program_id(0) == 0)
        def _():
            carry[...] = jnp.zeros_like(carry)
        v = proj_ref[:, D:2 * D].astype(F32) * proj_ref[:, 2 * D:3 * D].astype(F32)
        rows = lax.broadcasted_iota(jnp.int32, (tm, D), 0)
        before = carry[...]
        v1, v2 = _shift_rows(v, before[7:8, :], before[6:7, :], rows)
        carry[...] = v[tm - 8:tm, :]
        conv = cw_ref[0:1, :] * v2 + cw_ref[1:2, :] * v1 + cw_ref[2:3, :] * v
        conv_ref[...] = conv.astype(BF16)
        _, sz = _silu_parts(proj_ref[:, 3 * D:4 * D].astype(F32))
        ya = (proj_ref[:, 0:D].astype(F32) * conv * sz).astype(BF16)
        ya_ref[...] = ya
        oa = _nn(ya, w_ref[...])
        oa_ref[...] = oa.astype(BF16)
        h1_ref[...] = x_ref[...] + oa * _rms_scale(oa) * g_ref[...]

    row = lambda i: (i, 0)
    fix = lambda i: (0, 0)
    return pl.pallas_call(
        body, name="a_mix", grid=(s // tm,),
        in_specs=[pl.BlockSpec((tm, 4 * D), row), pl.BlockSpec((tm, D), row), pl.BlockSpec((8, D), fix),
                  pl.BlockSpec((D, D), fix), pl.BlockSpec((1, D), fix)],
        out_specs=[pl.BlockSpec((tm, D), row)] * 4,
        out_shape=[SDS((s, D), BF16), SDS((s, D), BF16), SDS((s, D), F32), SDS((s, D), BF16)],
        scratch_shapes=[pltpu.VMEM((8, D), F32)],
        compiler_params=_params(("arbitrary",)),
    )(proj, x, conv_w, w_out, g_post)


def _b_in(h1, g_kv, g_pre, w_kv, wbin_g, tm):
    s = h1.shape[0]

    def body(h_ref, gk_ref, gb_ref, wkv_ref, wb_ref, kv_ref, q_ref, z_ref):
        h = h_ref[...]
        hh = h * _rms_scale(h)
        nk = (hh * gk_ref[...]).astype(BF16)
        nb = (hh * gb_ref[...]).astype(BF16)
        kv_ref[...] = _nn(nk, wkv_ref[...]).astype(BF16)
        for j in range(2):
            q_ref[:, BIN_COLS * j:BIN_COLS * (j + 1)] = (_nn(nb, wb_ref[j]) * Q_SCALE).astype(BF16)
            z_ref[:, BIN_COLS * j:BIN_COLS * (j + 1)] = _nn(nb, wb_ref[2 + j]).astype(BF16)

    row = lambda i: (i, 0)
    fix = lambda i: (0, 0)
    return pl.pallas_call(
        body, name="b_in", grid=(s // tm,),
        in_specs=[pl.BlockSpec((tm, D), row), pl.BlockSpec((1, D), fix), pl.BlockSpec((1, D), fix),
                  pl.BlockSpec((D, 2 * KV_W), fix), pl.BlockSpec((N_CHIPS, D, BIN_COLS), lambda i: (0, 0, 0))],
        out_specs=[pl.BlockSpec((tm, 2 * KV_W), row), pl.BlockSpec((tm, D), row), pl.BlockSpec((tm, D), row)],
        out_shape=[SDS((s, 2 * KV_W), BF16), SDS((s, D), BF16), SDS((s, D), BF16)],
        compiler_params=_params(("parallel",)),
    )(h1, g_kv, g_pre, w_kv, wbin_g)


def _band_buckets():
    q = lax.broadcasted_iota(jnp.int32, (BLK, 2 * BLK), 0)
    k = lax.broadcasted_iota(jnp.int32, (BLK, 2 * BLK), 1)
    dist = q + BLK - k
    bucket = jnp.where(dist < MAX_EXACT, dist, MAX_EXACT)
    for t in BUCKET_THRESHOLDS:
        bucket = bucket + jnp.where(dist >= t, 1, 0)
    in_window = (dist >= 0) & (dist < BLK)
    return jnp.where(in_window, bucket, -1)


def _head_place(h):
    kh, j, e = h // GROUP, (h % GROUP) // 2, h % 2
    return kh, slice(BLK * j, BLK * (j + 1)), slice(2 * BLK * e, 2 * BLK * (e + 1))


def _bias_table(rel_bias, sinks):
    def body(rb_ref, sink_ref, tab_ref):
        bucket = _band_buckets()
        col = lax.broadcasted_iota(jnp.int32, (BLK, 2 * BLK), 1)
        for h in range(N_HEADS):
            acc = jnp.where(bucket < 0, NEG_INF, 0.0).astype(F32)
            for b in range(N_BUCKETS):
                acc = jnp.where(bucket == b, rb_ref[b, h], acc)
            acc = jnp.where(col == 0, sink_ref[h], acc)
            kh, rows, cols = _head_place(h)
            tab_ref[1, kh, rows, cols] = acc
            tab_ref[0, kh, rows, cols] = jnp.where((col > 0) & (col < BLK), NEG_INF, acc)

    return pl.pallas_call(
        body, name="bias_table", out_shape=SDS((2, N_KV, 4 * BLK, 4 * BLK), F32),
        in_specs=[pl.BlockSpec(memory_space=pltpu.SMEM), pl.BlockSpec(memory_space=pltpu.SMEM)],
        out_specs=pl.BlockSpec(memory_space=pltpu.VMEM),
    )(rel_bias, sinks)


def _bias_fold(dtab):
    def body(dtab_ref, out_ref, dsink_ref):
        bucket = _band_buckets()
        row = lax.broadcasted_iota(jnp.int32, (N_BUCKETS, 128), 0)
        lane = lax.broadcasted_iota(jnp.int32, (N_BUCKETS, 128), 1)
        row8 = lax.broadcasted_iota(jnp.int32, (8, 128), 0)
        lane8 = lax.broadcasted_iota(jnp.int32, (8, 128), 1)
        acc = jnp.zeros((N_BUCKETS, 128), F32)
        dsink = jnp.zeros((8, 128), F32)
        for h in range(N_HEADS):
            kh, rows, cols = _head_place(h)
            dt = dtab_ref[kh, rows, cols]
            for b in range(N_BUCKETS):
                val = jnp.sum(jnp.where(bucket == b, dt, 0.0))
                acc = acc + jnp.where((row == b) & (lane == h), val, 0.0)
            dsink = dsink + jnp.where((row8 == 0) & (lane8 == h), jnp.sum(dt[:, 0:1]), 0.0)
        out_ref[...] = acc
        dsink_ref[...] = dsink

    vm = pl.BlockSpec(memory_space=pltpu.VMEM)
    return pl.pallas_call(
        body, name="bias_fold", out_shape=[SDS((N_BUCKETS, 128), F32), SDS((8, 128), F32)],
        in_specs=[vm], out_specs=[vm, vm],
    )(dtab)


def _pair_operands(prev, cur):
    t = jnp.concatenate([prev, cur], axis=0).astype(F32)
    t = jnp.where(lax.broadcasted_iota(jnp.int32, t.shape, 0) == 0, 0.0, t)
    tr = pltpu.roll(t, HEAD_DIM, 1)
    lo = lax.broadcasted_iota(jnp.int32, t.shape, 1) < HEAD_DIM
    zero = jnp.zeros_like(t)
    head0 = jnp.concatenate([jnp.where(lo, t, zero), jnp.where(lo, zero, tr)], axis=0).astype(BF16)
    head1 = jnp.concatenate([jnp.where(lo, tr, zero), jnp.where(lo, zero, t)], axis=0).astype(BF16)
    return head0, head1


def _pair_fold(d0, d1):
    lo = lax.broadcasted_iota(jnp.int32, (2 * BLK, KV_W), 1) < HEAD_DIM
    zero = jnp.zeros((2 * BLK, KV_W), F32)
    g0 = jnp.where(lo, d0[0:256], zero) + pltpu.roll(jnp.where(lo, zero, d0[256:512]), HEAD_DIM, 1)
    g1 = pltpu.roll(jnp.where(lo, d1[0:256], zero), HEAD_DIM, 1) + jnp.where(lo, zero, d1[256:512])
    return jnp.where(lax.broadcasted_iota(jnp.int32, (2 * BLK, KV_W), 0) == 0, 0.0, g0 + g1)


def _stack_pairs(ref, kh):
    return jnp.concatenate([ref[:, 128 * (4 * kh + j):128 * (4 * kh + j + 1)] for j in range(4)], axis=0)


def _table_spec():
    return pl.BlockSpec((1, N_KV, 4 * BLK, 4 * BLK), lambda n: (jnp.minimum(n, 1), 0, 0, 0))


def _attn_fwd(q, kv, tab):
    s = q.shape[0]

    def body(q_ref, kp_ref, kc_ref, vp_ref, vc_ref, tab_ref, att_ref, stats_ref):
        k2 = _pair_operands(kp_ref[...], kc_ref[...])
        v2 = _pair_operands(vp_ref[...], vc_ref[...])
        lane = lax.broadcasted_iota(jnp.int32, (BLK, 128), 1)
        stats = jnp.zeros((BLK, 128), F32)
        for kh in range(N_KV):
            sc = _nt(_stack_pairs(q_ref, kh), k2[kh])
            ps = []
            for e in range(2):
                lg = sc[:, 256 * e:256 * (e + 1)] + tab_ref[0, kh, :, 256 * e:256 * (e + 1)]
                m = jnp.max(lg, axis=-1, keepdims=True)
                ex = jnp.exp(lg - m)
                den = jnp.sum(ex, axis=-1, keepdims=True)
                ps.append(ex * (1.0 / den))
                lse = m + jnp.log(den)
                for j in range(4):
                    stats = jnp.where(lane == GROUP * kh + 2 * j + e, lse[BLK * j:BLK * (j + 1)], stats)
            out = _nn(jnp.concatenate(ps, axis=1).astype(BF16), v2[kh])
            for j in range(4):
                att_ref[:, 128 * (4 * kh + j):128 * (4 * kh + j + 1)] = out[BLK * j:BLK * (j + 1)].astype(BF16)
        stats_ref[...] = stats

    cur = lambda n: (n, 0)
    prev = lambda n: (jnp.maximum(n - 1, 0), 0)
    return pl.pallas_call(
        body, name="attn_fwd", grid=(s // BLK,),
        in_specs=[pl.BlockSpec((BLK, D), cur),
                  pl.BlockSpec((BLK, KV_W), prev), pl.BlockSpec((BLK, KV_W), cur),
                  pl.BlockSpec((BLK, KV_W), lambda n: (jnp.maximum(n - 1, 0), 1)),
                  pl.BlockSpec((BLK, KV_W), lambda n: (n, 1)), _table_spec()],
        out_specs=[pl.BlockSpec((BLK, D), cur), pl.BlockSpec((BLK, 128), cur)],
        out_shape=[SDS((s, D), BF16), SDS((s, 128), F32)],
        compiler_params=_params(("parallel",)),
    )(q, kv, kv, kv, kv, tab)


def _mid(att, zb, h1, tgt, w_out, g_post, tm):
    s = att.shape[0]
    nt = s // tm

    def body(att_ref, z_ref, h1_ref, t_ref, w_ref, g_ref,
             dh_ref, dqz_ref, datt_ref, loss_ref, dg_ref, dw_ref, dw16_ref, dw_acc, stage, put_sem):
        @pl.when(pl.program_id(0) == 0)
        def _():
            loss_ref[...] = jnp.zeros_like(loss_ref)
            dg_ref[...] = jnp.zeros_like(dg_ref)
            dw_acc[...] = jnp.zeros_like(dw_acc)
        att = att_ref[...].astype(F32)
        z = z_ref[...].astype(F32)
        sg, sz = _silu_parts(z)
        ob = (att * sz).astype(BF16)
        y2 = _nn(ob, w_ref[...])
        r2 = _rms_scale(y2)
        yh = y2 * r2
        g = g_ref[...]
        err = (h1_ref[...] + yh * g) - t_ref[...]
        loss_ref[...] += jnp.sum(jnp.sum(err * err, axis=-1, keepdims=True) / D)
        dh = err / D
        dh_ref[...] = dh
        _acc_row(dg_ref, 0, jnp.sum(dh * yh, axis=0, keepdims=True))
        dyh = dh * g
        dy = (r2 * (dyh - yh * jnp.mean(dyh * yh, axis=-1, keepdims=True))).astype(BF16)
        dw_acc[...] += _tn(ob, dy)
        dob = _nt(dy, w_ref[...])
        datt_ref[...] = (dob * sz).astype(BF16)
        dqz_ref[...] = (dob * att * _dsilu(z, sg)).astype(BF16)

        @pl.when(pl.program_id(0) == nt - 1)
        def _():
            _write_gradient(dw_acc, dw_ref, dw16_ref, stage, put_sem)

    row = lambda i: (i, 0)
    fix = lambda i: (0, 0)
    anyspace = pl.BlockSpec(memory_space=pl.ANY)
    return pl.pallas_call(
        body, name="mid", grid=(nt,),
        in_specs=[pl.BlockSpec((tm, D), row)] * 4 + [pl.BlockSpec((D, D), fix), pl.BlockSpec((1, D), fix)],
        out_specs=[pl.BlockSpec((tm, D), row), pl.BlockSpec((tm, D), lambda i: (i, 1)), pl.BlockSpec((tm, D), row),
                   pl.BlockSpec((8, 128), fix), pl.BlockSpec((8, D), fix), anyspace, anyspace],
        out_shape=[SDS((s, D), F32), SDS((s, 2 * D), BF16), SDS((s, D), BF16), SDS((8, 128), F32),
                   SDS((8, D), F32), SDS((D, D), F32), SDS((D, D), BF16)],
        scratch_shapes=[pltpu.VMEM((D, D), F32), pltpu.VMEM((D // 4, D), BF16), pltpu.SemaphoreType.DMA],
        compiler_params=_params(("arbitrary",)),
    )(att, zb, h1, tgt, w_out, g_post)


def _attn_bwd(q, kv, datt, stats, tab, dqz):
    s = q.shape[0]
    nb = s // BLK

    def body(q_ref, kp_ref, kc_ref, vp_ref, vc_ref, da_ref, st_ref, tab_ref, dqz_in,
             dq_ref, dkv_ref, dtab_ref, dk_carry, dv_carry):
        del dqz_in
        n = pl.program_id(0)

        @pl.when(n == 0)
        def _():
            dtab_ref[...] = jnp.zeros_like(dtab_ref)
            dk_carry[...] = jnp.zeros_like(dk_carry)
            dv_carry[...] = jnp.zeros_like(dv_carry)

        @pl.when(n < nb)
        def _():
            k2 = _pair_operands(kp_ref[...], kc_ref[...])
            v2 = _pair_operands(vp_ref[...], vc_ref[...])
            lane = lax.broadcasted_iota(jnp.int32, (BLK, 128), 1)
            stats = st_ref[...]
            dk2, dv2 = [], []
            for kh in range(N_KV):
                qs = _stack_pairs(q_ref, kh)
                das = _stack_pairs(da_ref, kh)
                sc = _nt(qs, k2[kh])
                dp = _nt(das, v2[kh])
                ps, dss = [], []
                for e in range(2):
                    heads = [GROUP * kh + 2 * j + e for j in range(4)]
                    lse = jnp.concatenate([jnp.sum(jnp.where(lane == h, stats, 0.0), axis=-1, keepdims=True)
                                           for h in heads], axis=0)
                    cols = slice(256 * e, 256 * (e + 1))
                    p = jnp.exp(sc[:, cols] + tab_ref[0, kh, :, cols] - lse)
                    delta = jnp.sum(p * dp[:, cols], axis=-1, keepdims=True)
                    ds = p * (dp[:, cols] - delta)
                    dtab_ref[kh, :, cols] += ds
                    ps.append(p)
                    dss.append(ds)
                p2 = jnp.concatenate(ps, axis=1).astype(BF16)
                ds2 = jnp.concatenate(dss, axis=1).astype(BF16)
                dq = _nn(ds2, k2[kh]) * Q_SCALE
                for j in range(4):
                    dq_ref[:, 128 * (4 * kh + j):128 * (4 * kh + j + 1)] = dq[BLK * j:BLK * (j + 1)].astype(BF16)
                dk2.append(_tn(ds2, qs))
                dv2.append(_tn(p2, das))
            dkk = _pair_fold(dk2[0], dk2[1])
            dvv = _pair_fold(dv2[0], dv2[1])
            dkv_ref[:, 0:KV_W] = (dk_carry[...] + dkk[0:BLK]).astype(BF16)
            dkv_ref[:, KV_W:2 * KV_W] = (dv_carry[...] + dvv[0:BLK]).astype(BF16)
            dk_carry[...] = dkk[BLK:2 * BLK]
            dv_carry[...] = dvv[BLK:2 * BLK]

        @pl.when(n == nb)
        def _():
            dkv_ref[:, 0:KV_W] = dk_carry[...].astype(BF16)
            dkv_ref[:, KV_W:2 * KV_W] = dv_carry[...].astype(BF16)

    cur = lambda n: (jnp.minimum(n, nb - 1), 0)
    prev = lambda n: (jnp.clip(n - 1, 0, nb - 1), 0)
    return pl.pallas_call(
        body, name="attn_bwd", grid=(nb + 1,),
        in_specs=[pl.BlockSpec((BLK, D), cur),
                  pl.BlockSpec((BLK, KV_W), prev), pl.BlockSpec((BLK, KV_W), cur),
                  pl.BlockSpec((BLK, KV_W), lambda n: (jnp.clip(n - 1, 0, nb - 1), 1)),
                  pl.BlockSpec((BLK, KV_W), lambda n: (jnp.minimum(n, nb - 1), 1)),
                  pl.BlockSpec((BLK, D), cur), pl.BlockSpec((BLK, 128), cur), _table_spec(),
                  pl.BlockSpec(memory_space=pl.ANY)],
        out_specs=[pl.BlockSpec((BLK, D), cur), pl.BlockSpec((BLK, 2 * KV_W), prev),
                   pl.BlockSpec((N_KV, 4 * BLK, 4 * BLK), lambda n: (0, 0, 0))],
        out_shape=[SDS((s, 2 * D), BF16), SDS((s, 2 * KV_W), BF16), SDS((N_KV, 4 * BLK, 4 * BLK), F32)],
        scratch_shapes=[pltpu.VMEM((BLK, KV_W), F32), pltpu.VMEM((BLK, KV_W), F32)],
        input_output_aliases={8: 0},
        compiler_params=_params(("arbitrary",)),
    )(q, kv, kv, kv, kv, datt, stats, tab, dqz)


def _b_bwd(dqz, dkv, h1, dh2, oa, wbin_g, w_kv, g_kv, g_pre, g_apost, tm):
    s = h1.shape[0]
    nt = s // tm

    def body(dqz_ref, dkv_ref, h_ref, dh2_ref, oa_ref, wb_ref, wkv_ref, gk_ref, gb_ref, ga_ref,
             dh1_ref, doa_ref, dg_ref, dwb_ref, dwkv_ref, dwb16_ref, dwkv16_ref, wcat, dwb_acc, dwkv_acc, put_sem):
        @pl.when(pl.program_id(0) == 0)
        def _():
            dg_ref[...] = jnp.zeros_like(dg_ref)
            dwb_acc[...] = jnp.zeros_like(dwb_acc)
            dwkv_acc[...] = jnp.zeros_like(dwkv_acc)
            for j in range(N_CHIPS):
                pltpu.sync_copy(wb_ref.at[j], wcat.at[:, pl.ds(BIN_COLS * j, BIN_COLS)])
        dnb = _nt(dqz_ref[...], wcat[...])
        dnk = _nt(dkv_ref[...], wkv_ref[...])
        h = h_ref[...]
        r = _rms_scale(h)
        hh = h * r
        dwb_acc[...] += _tn((hh * gb_ref[...]).astype(BF16), dqz_ref[...])
        dwkv_acc[...] += _tn((hh * gk_ref[...]).astype(BF16), dkv_ref[...])
        _acc_row(dg_ref, 0, jnp.sum(dnk * hh, axis=0, keepdims=True))
        _acc_row(dg_ref, 1, jnp.sum(dnb * hh, axis=0, keepdims=True))
        dhh = dnb * gb_ref[...] + dnk * gk_ref[...]
        dh1 = dh2_ref[...] + r * (dhh - hh * jnp.mean(dhh * hh, axis=-1, keepdims=True))
        dh1_ref[...] = dh1
        oa = oa_ref[...].astype(F32)
        ra = _rms_scale(oa)
        oh = oa * ra
        _acc_row(dg_ref, 2, jnp.sum(dh1 * oh, axis=0, keepdims=True))
        doh = dh1 * ga_ref[...]
        doa_ref[...] = (ra * (doh - oh * jnp.mean(doh * oh, axis=-1, keepdims=True))).astype(BF16)

        @pl.when(pl.program_id(0) == nt - 1)
        def _():
            wcat[...] = dwb_acc[...].astype(BF16)
            puts = [pltpu.make_async_copy(dwkv_acc, dwkv_ref, put_sem.at[2 * N_CHIPS])]
            for j in range(N_CHIPS):
                cols = pl.ds(BIN_COLS * j, BIN_COLS)
                puts.append(pltpu.make_async_copy(dwb_acc.at[:, cols], dwb_ref.at[j], put_sem.at[2 * j]))
                puts.append(pltpu.make_async_copy(wcat.at[:, cols], dwb16_ref.at[j], put_sem.at[2 * j + 1]))
            for put in puts:
                put.start()
            for put in puts:
                put.wait()
            wcat[:, 0:2 * KV_W] = dwkv_acc[...].astype(BF16)
            pltpu.sync_copy(wcat.at[:, pl.ds(0, 2 * KV_W)], dwkv16_ref)

    row = lambda i: (i, 0)
    fix = lambda i: (0, 0)
    anyspace = pl.BlockSpec(memory_space=pl.ANY)
    return pl.pallas_call(
        body, name="b_bwd", grid=(nt,),
        in_specs=[pl.BlockSpec((tm, 2 * D), row), pl.BlockSpec((tm, 2 * KV_W), row), pl.BlockSpec((tm, D), row),
                  pl.BlockSpec((tm, D), row), pl.BlockSpec((tm, D), row), anyspace, pl.BlockSpec((D, 2 * KV_W), fix),
                  pl.BlockSpec((1, D), fix), pl.BlockSpec((1, D), fix), pl.BlockSpec((1, D), fix)],
        out_specs=[pl.BlockSpec((tm, D), row), pl.BlockSpec((tm, D), row), pl.BlockSpec((8, D), fix)] + [anyspace] * 4,
        out_shape=[SDS((s, D), F32), SDS((s, D), BF16), SDS((8, D), F32), SDS((N_CHIPS, D, BIN_COLS), F32),
                   SDS((D, 2 * KV_W), F32), SDS((N_CHIPS, D, BIN_COLS), BF16), SDS((D, 2 * KV_W), BF16)],
        scratch_shapes=[pltpu.VMEM((D, 2 * D), BF16), pltpu.VMEM((D, 2 * D), F32), pltpu.VMEM((D, 2 * KV_W), F32),
                        pltpu.SemaphoreType.DMA((2 * N_CHIPS + 1,))],
        compiler_params=_params(("arbitrary",)),
    )(dqz, dkv, h1, dh2, oa, wbin_g, w_kv, g_kv, g_pre, g_apost)


def _to_owner_core(pieces, r, send, recv, core, action):
    x, y, c = lax.axis_index("x"), lax.axis_index("y"), lax.axis_index("c")
    for kp in range(N_CHIPS):
        px, py = kp >> 1, kp & 1
        rel = 4 * (x + px - 2 * x * px) + 2 * (y + py - 2 * y * py) + (c + core - 2 * c * core)

        @pl.when(rel != 0)
        def _():
            cp = pltpu.make_async_remote_copy(src_ref=pieces.at[kp], dst_ref=r.at[rel - 1], send_sem=send.at[kp],
                                              recv_sem=recv.at[rel - 1], device_id=(px, py, core), device_id_type=MESH)
            if action == "start":
                cp.start()
            else:
                cp.wait_send()
    if action == "wait":
        @pl.when(c == core)
        def _():
            for rel in range(1, N_DEV):
                pltpu.make_async_remote_copy(src_ref=pieces.at[0], dst_ref=r.at[rel - 1], send_sem=send.at[0],
                                             recv_sem=recv.at[rel - 1], device_id=(x, y, c),
                                             device_id_type=MESH).wait_recv()


def _owner_core_sems():
    return [pltpu.SemaphoreType.DMA((N_CHIPS,)), pltpu.SemaphoreType.DMA((N_DEV - 1,))]


def _device_exchange(grads, recvs, send, recv):
    x, y, c = lax.axis_index("x"), lax.axis_index("y"), lax.axis_index("c")
    copies = []
    for a, (g, r) in enumerate(zip(grads, recvs)):
        h = g.shape[1] // 2
        for rel in range(1, N_DEV):
            fx, fy, fc = rel >> 2, (rel >> 1) & 1, rel & 1
            px, py, pc = x + fx - 2 * x * fx, y + fy - 2 * y * fy, c + fc - 2 * c * fc
            sem = (N_DEV - 1) * a + rel - 1
            copies.append(pltpu.make_async_remote_copy(
                src_ref=g.at[2 * px + py, pl.ds(pl.multiple_of(pc * h, 16), h)], dst_ref=r.at[rel - 1],
                send_sem=send.at[sem], recv_sem=recv.at[sem], device_id=(px, py, pc), device_id_type=MESH))
    return copies


def _device_exchange_specs(grads):
    anyspace = pl.BlockSpec(memory_space=pl.ANY)
    n = len(grads)
    count = (N_DEV - 1) * n
    return ([anyspace] * n, [anyspace] * n,
            [SDS((N_DEV - 1, g.shape[1] // 2, g.shape[2]), g.dtype) for g in grads],
            [pltpu.SemaphoreType.DMA((count,)), pltpu.SemaphoreType.DMA((count,))])


def _a_bwd(doa, ya, conv, proj, conv_w, w_out, tm, parts):
    s = doa.shape[0]
    nt = s // tm
    n = len(parts)
    ex_in, ex_out, ex_shape, ex_sems = _device_exchange_specs(parts)

    def body(*refs):
        doa_ref, ya_ref, conv_ref, proj_ref, cw_ref, w_ref = refs[:6]
        part_refs = refs[6:6 + n]
        dproj_ref, dcw_ref, dw_ref, dw16_ref = refs[6 + n:10 + n]
        recv_refs = refs[10 + n:10 + 2 * n]
        carry, dw_acc, stage, put_sem, send, recv = refs[10 + 2 * n:]
        i = pl.program_id(0)

        @pl.when(i == 0)
        def _():
            dcw_ref[...] = jnp.zeros_like(dcw_ref)
            carry[...] = jnp.zeros_like(carry)
            dw_acc[...] = jnp.zeros_like(dw_acc)
            for cp in _device_exchange(part_refs, recv_refs, send, recv):
                cp.start()
        dya = _nt(doa_ref[...], w_ref[...])
        dw_acc[...] += _tn(ya_ref[...], doa_ref[...])
        bg = proj_ref[:, 0:D].astype(F32)
        cg = proj_ref[:, D:2 * D].astype(F32)
        u = proj_ref[:, 2 * D:3 * D].astype(F32)
        z = proj_ref[:, 3 * D:4 * D].astype(F32)
        v = cg * u
        rows = lax.broadcasted_iota(jnp.int32, (tm, D), 0)
        conv = conv_ref[...].astype(F32)
        sg, sz = _silu_parts(z)
        dproj_ref[:, 0:D] = (dya * conv * sz).astype(BF16)
        dproj_ref[:, 3 * D:4 * D] = (dya * bg * conv * _dsilu(z, sg)).astype(BF16)
        dconv = dya * bg * sz
        after = carry[...]
        up1 = jnp.where(rows < tm - 1, pltpu.roll(dconv, tm - 1, 0), after[0:1, :])
        up2 = jnp.where(rows < tm - 2, pltpu.roll(dconv, tm - 2, 0),
                        jnp.where(rows == tm - 2, after[0:1, :], after[1:2, :]))
        carry[...] = dconv[0:8, :]
        _acc_row(dcw_ref, 0, jnp.sum(up2 * v, axis=0, keepdims=True))
        _acc_row(dcw_ref, 1, jnp.sum(up1 * v, axis=0, keepdims=True))
        _acc_row(dcw_ref, 2, jnp.sum(dconv * v, axis=0, keepdims=True))
        dv = cw_ref[2:3, :] * dconv + cw_ref[1:2, :] * up1 + cw_ref[0:1, :] * up2
        dproj_ref[:, D:2 * D] = (dv * u).astype(BF16)
        dproj_ref[:, 2 * D:3 * D] = (dv * cg).astype(BF16)

        @pl.when(i == nt - 1)
        def _():
            _write_gradient(dw_acc, dw_ref, dw16_ref, stage, put_sem)
            for cp in _device_exchange(part_refs, recv_refs, send, recv):
                cp.wait()

    rev = lambda i: (nt - 1 - i, 0)
    fix = lambda i: (0, 0)
    anyspace = pl.BlockSpec(memory_space=pl.ANY)
    dproj, dcw, dw, dw16, *got = pl.pallas_call(
        body, name="a_bwd", grid=(nt,),
        in_specs=[pl.BlockSpec((tm, D), rev), pl.BlockSpec((tm, D), rev), pl.BlockSpec((tm, D), rev),
                  pl.BlockSpec((tm, 4 * D), rev), pl.BlockSpec((8, D), fix), pl.BlockSpec((D, D), fix)] + ex_in,
        out_specs=[pl.BlockSpec((tm, 4 * D), rev), pl.BlockSpec((8, D), fix), anyspace, anyspace] + ex_out,
        out_shape=[SDS((s, 4 * D), BF16), SDS((8, D), F32), SDS((D, D), F32), SDS((D, D), BF16)] + ex_shape,
        scratch_shapes=[pltpu.VMEM((8, D), F32), pltpu.VMEM((D, D), F32), pltpu.VMEM((D // 4, D), BF16),
                        pltpu.SemaphoreType.DMA] + ex_sems,
        compiler_params=_params(("arbitrary",)),
    )(doa, ya, conv, proj, conv_w, w_out, *parts)
    return dproj, dcw, dw, dw16, got


def _dn1(dp_ref, w_ref):
    dn = _nt(dp_ref[:, 0:D], w_ref[0])
    for j in range(1, 4):
        dn = dn + _nt(dp_ref[:, D * j:D * (j + 1)], w_ref[j])
    return dn


def _a_in_bwd_matmul(dproj, win_g, tm, count, win_half, win_got):
    def body(dp_ref, w_ref, half_ref, got_in, dn_ref, got_ref, wcat, send, recv):
        del got_in

        @pl.when(pl.program_id(0) == 0)
        def _():
            _to_owner_core(half_ref, got_ref, send, recv, 1, "start")
            for j in range(N_CHIPS):
                pltpu.sync_copy(w_ref.at[j], wcat.at[:, pl.ds(D * j, D)])
        dn_ref[...] = _nt(dp_ref[...], wcat[...]).astype(BF16)

        @pl.when(pl.program_id(0) == count - 1)
        def _():
            _to_owner_core(half_ref, got_ref, send, recv, 1, "wait")

    row = lambda i: (i, 0)
    anyspace = pl.BlockSpec(memory_space=pl.ANY)
    return pl.pallas_call(
        body, name="a_in_bwd_matmul", grid=(count,),
        in_specs=[pl.BlockSpec((tm, 4 * D), row), anyspace, anyspace, anyspace],
        out_specs=[pl.BlockSpec((tm, D), row), anyspace],
        out_shape=[SDS((count * tm, D), BF16), SDS(win_got.shape, win_got.dtype)],
        scratch_shapes=[pltpu.VMEM((D, 4 * D), BF16)] + _owner_core_sems(),
        input_output_aliases={3: 1},
        compiler_params=_params(("arbitrary",)),
    )(dproj, win_g, win_half, win_got)


def _a_in_bwd(dn_first, dproj, x, dh1, win_g, g_pre, tm):
    s = x.shape[0]
    nt = s // tm
    count = dn_first.shape[0] // tm

    def body(dn_ref, dp_ref, x_ref, dh_ref, w_ref, g_ref, gx_ref, dg_ref, dn_s):
        i = pl.program_id(0)

        @pl.when(i == 0)
        def _():
            dg_ref[...] = jnp.zeros_like(dg_ref)

        @pl.when(i < count)
        def _():
            dn_s[...] = dn_ref[...].astype(F32)

        @pl.when(i >= count)
        def _():
            dn_s[...] = _dn1(dp_ref, w_ref)
        dn = dn_s[...]
        xv = x_ref[...]
        r = _rms_scale(xv)
        xh = xv * r
        _acc_row(dg_ref, 0, jnp.sum(dn * xh, axis=0, keepdims=True))
        dxh = dn * g_ref[...]
        gx_ref[...] = dh_ref[...] + r * (dxh - xh * jnp.mean(dxh * xh, axis=-1, keepdims=True))

    row = lambda i: (i, 0)
    fix = lambda i: (0, 0)
    return pl.pallas_call(
        body, name="a_in_bwd", grid=(nt,),
        in_specs=[pl.BlockSpec((tm, D), lambda i: (jnp.minimum(i, count - 1), 0)),
                  pl.BlockSpec((tm, 4 * D), lambda i: (jnp.maximum(i, count), 0)),
                  pl.BlockSpec((tm, D), row), pl.BlockSpec((tm, D), row),
                  pl.BlockSpec((4, D, D), lambda i: (0, 0, 0)), pl.BlockSpec((1, D), fix)],
        out_specs=[pl.BlockSpec((tm, D), row), pl.BlockSpec((8, D), fix)],
        out_shape=[SDS((s, D), F32), SDS((8, D), F32)],
        scratch_shapes=[pltpu.VMEM((tm, D), F32)],
        compiler_params=_params(("arbitrary",)),
    )(dn_first, dproj, x, dh1, win_g, g_pre)


def _swap_halves(shards, send, recv):
    x, y, c = lax.axis_index("x"), lax.axis_index("y"), lax.axis_index("c")
    sibling = (x, y, 1 - c)
    copies = []
    for b, full in enumerate(shards):
        h = full.shape[0] // 2
        mine = full.at[pl.ds(pl.multiple_of(c * h, 8), h)]
        theirs = full.at[pl.ds(pl.multiple_of((1 - c) * h, 8), h)]
        copies.append((pltpu.make_async_remote_copy(src_ref=mine, dst_ref=mine, send_sem=send.at[b], recv_sem=recv.at[b],
                                                    device_id=sibling, device_id_type=MESH),
                       pltpu.make_async_remote_copy(src_ref=mine, dst_ref=theirs, send_sem=send.at[b], recv_sem=recv.at[b],
                                                    device_id=sibling, device_id_type=MESH)))
    return copies


def _dw_in_half(n1, dproj, core, tmw, name, to_owners=None, to_devices=None, shards=()):
    s = n1.shape[0]
    h = D // 2
    nt = s // tmw
    n_sh = len(shards)
    if to_owners is not None:
        sent_array, sems, got_shape = to_owners, _owner_core_sems(), SDS((N_DEV - 1, h, D), BF16)
    else:
        sent_array = to_devices
        _, _, (got_shape,), sems = _device_exchange_specs([to_devices])

    def body(*refs):
        a_ref, b_ref, sent = refs[:3]
        o_ref, o16_ref, got = refs[3 + n_sh:6 + n_sh]
        shard_refs = refs[6 + n_sh:6 + 2 * n_sh]
        send, recv = refs[6 + 2 * n_sh:8 + 2 * n_sh]
        swap_sems = refs[8 + 2 * n_sh:]
        j, t = pl.program_id(0), pl.program_id(1)

        def exchange(action):
            if to_owners is not None:
                _to_owner_core(sent, got, send, recv, 1 - core, action)
            else:
                for cp in _device_exchange([sent], [got], send, recv):
                    cp.start() if action == "start" else cp.wait()

        @pl.when((j == 0) & (t == 0))
        def _():
            exchange("start")
            if n_sh:
                for mine, _ in _swap_halves(shard_refs, *swap_sems):
                    mine.start()

        @pl.when(t == 0)
        def _():
            o_ref[...] = jnp.zeros_like(o_ref)
        o_ref[0] += _tn(a_ref[...], b_ref[...])

        @pl.when(t == nt - 1)
        def _():
            o16_ref[...] = o_ref[...].astype(BF16)

        @pl.when((j == N_CHIPS - 1) & (t == nt - 1))
        def _():
            exchange("wait")
            if n_sh:
                for mine, theirs in _swap_halves(shard_refs, *swap_sems):
                    theirs.wait_recv()
                    mine.wait_send()

    anyspace = pl.BlockSpec(memory_space=pl.ANY)
    slot = pl.BlockSpec((1, h, D), lambda j, t: (j, 0, 0))
    swap_scratch = [pltpu.SemaphoreType.DMA((n_sh,)), pltpu.SemaphoreType.DMA((n_sh,))] if n_sh else []
    return pl.pallas_call(
        body, name=name, grid=(N_CHIPS, nt),
        in_specs=[pl.BlockSpec((tmw, h), lambda j, t: (t, core)), pl.BlockSpec((tmw, D), lambda j, t: (t, j))]
        + [anyspace] * (1 + n_sh),
        out_specs=[slot, slot] + [anyspace] * (1 + n_sh),
        out_shape=[SDS((N_CHIPS, h, D), F32), SDS((N_CHIPS, h, D), BF16), got_shape]
        + [SDS(sh.shape, F32) for sh in shards],
        scratch_shapes=sems + swap_scratch,
        input_output_aliases={3 + b: 3 + b for b in range(n_sh)},
        compiler_params=_params(("arbitrary", "arbitrary")),
    )(n1, dproj, sent_array, *shards)


def _share_and_gather(shards, smalls):
    n_h, n_s = len(shards), len(smalls)

    def body(*refs):
        small_ins = refs[n_h:n_h + n_s]
        fs = refs[n_h + n_s:2 * n_h + n_s]
        small_alls = refs[2 * n_h + n_s:2 * n_h + 2 * n_s]
        dsend, drecv, ssend, srecv = refs[2 * n_h + 2 * n_s:]
        x, y, c = lax.axis_index("x"), lax.axis_index("y"), lax.axis_index("c")
        swaps = _swap_halves(fs, dsend, drecv)
        sends, arrivals = [mine for mine, _ in swaps], [theirs for _, theirs in swaps]
        me = 4 * x + 2 * y + c
        for k, (small_in, small_all) in enumerate(zip(small_ins, small_alls)):
            small_all[me] = small_in[...]
            for rel in range(1, N_DEV):
                fx, fy, fc = rel >> 2, (rel >> 1) & 1, rel & 1
                peer = (x + fx - 2 * x * fx, y + fy - 2 * y * fy, c + fc - 2 * c * fc)
                sender = 4 * peer[0] + 2 * peer[1] + peer[2]
                sem = (N_DEV - 1) * k + rel - 1
                sends.append(pltpu.make_async_remote_copy(
                    src_ref=small_in, dst_ref=small_all.at[me], send_sem=ssend.at[sem], recv_sem=srecv.at[sem],
                    device_id=peer, device_id_type=MESH))
                arrivals.append(pltpu.make_async_remote_copy(
                    src_ref=small_in, dst_ref=small_all.at[sender], send_sem=ssend.at[sem], recv_sem=srecv.at[sem],
                    device_id=peer, device_id_type=MESH))
        for cp in sends:
            cp.start()
        for cp in arrivals:
            cp.wait_recv()
        for cp in sends:
            cp.wait_send()

    anyspace = pl.BlockSpec(memory_space=pl.ANY)
    vm = pl.BlockSpec(memory_space=pltpu.VMEM)
    out_shape = [SDS(full.shape, F32) for full in shards] + [SDS((N_DEV,) + sm.shape, F32) for sm in smalls]
    n_all = (N_DEV - 1) * n_s
    outs = pl.pallas_call(
        body, name="share_and_gather", out_shape=out_shape,
        in_specs=[anyspace] * n_h + [vm] * n_s, out_specs=[anyspace] * n_h + [vm] * n_s,
        scratch_shapes=[pltpu.SemaphoreType.DMA((n_h,)), pltpu.SemaphoreType.DMA((n_h,)),
                        pltpu.SemaphoreType.DMA((n_all,)), pltpu.SemaphoreType.DMA((n_all,))],
        input_output_aliases={b: b for b in range(n_h)},
    )(*shards, *smalls)
    return outs[:n_h], outs[n_h:]


def _add_win(where, lo, hi, r, name):
    _, h, cols = lo.shape
    tr = min(h, 256)
    nh = h // tr

    def body(where_ref, lo_ref, hi_ref, r_ref, o_ref):
        acc = jnp.where(where_ref[0] == 0, lo_ref[0], hi_ref[0])
        for k in range(N_DEV - 1):
            acc = acc + r_ref[k].astype(F32)
        o_ref[...] = acc

    own = pl.BlockSpec((1, tr, cols), lambda i, w: (w[1], i, 0))
    return pl.pallas_call(
        body, name=name,
        grid_spec=pltpu.PrefetchScalarGridSpec(
            num_scalar_prefetch=1, grid=(nh,),
            in_specs=[own, own, pl.BlockSpec((N_DEV - 1, tr, cols), lambda i, w: (0, i, 0))],
            out_specs=pl.BlockSpec((tr, cols), lambda i, w: (w[0] * nh + i, 0))),
        out_shape=SDS((2 * h, cols), F32),
        compiler_params=_params(("parallel",)),
    )(where, lo, hi, r)


def _add_devices(where, g, r, name):
    _, rows, cols = g.shape
    h = rows // 2
    tr = min(h, 256)
    nh = h // tr

    def body(where_ref, g_ref, r_ref, o_ref):
        del where_ref
        acc = g_ref[0]
        for k in range(N_DEV - 1):
            acc = acc + r_ref[k].astype(F32)
        o_ref[...] = acc

    return pl.pallas_call(
        body, name=name,
        grid_spec=pltpu.PrefetchScalarGridSpec(
            num_scalar_prefetch=1, grid=(nh,),
            in_specs=[pl.BlockSpec((1, tr, cols), lambda i, w: (w[1], w[0] * nh + i, 0)),
                      pl.BlockSpec((N_DEV - 1, tr, cols), lambda i, w: (0, i, 0))],
            out_specs=pl.BlockSpec((tr, cols), lambda i, w: (w[0] * nh + i, 0))),
        out_shape=SDS((rows, cols), F32),
        compiler_params=_params(("parallel",)),
    )(where, g, r)


def _sum_smalls(gathered):
    n = len(gathered)

    def body(*refs):
        for all_ref, o_ref in zip(refs[:n], refs[n:]):
            acc = all_ref[0]
            for dev in range(1, N_DEV):
                acc = acc + all_ref[dev]
            o_ref[...] = acc

    vm = pl.BlockSpec(memory_space=pltpu.VMEM)
    return pl.pallas_call(
        body, name="sum_smalls", out_shape=[SDS(a.shape[1:], F32) for a in gathered],
        in_specs=[vm] * n, out_specs=[vm] * n,
    )(*gathered)


def _adam_step(g, w, m, v):
    nm = ADAM_B1 * m + (1.0 - ADAM_B1) * g
    nv = ADAM_B2 * v + (1.0 - ADAM_B2) * (g * g)
    m_hat = nm / (1.0 - ADAM_B1 ** ADAM_STEP)
    v_hat = nv / (1.0 - ADAM_B2 ** ADAM_STEP)
    return -ADAM_LR * (m_hat / (jnp.sqrt(v_hat) + ADAM_EPS) + ADAM_WD * w), nm, nv


def _adamw(g, w, m, v, name):
    rows, cols = g.shape
    tr = min(rows, 256)

    def body(g_ref, w_ref, m_ref, v_ref, d_ref, nm_ref, nv_ref):
        d_ref[...], nm_ref[...], nv_ref[...] = _adam_step(g_ref[...], w_ref[...], m_ref[...], v_ref[...])

    spec = pl.BlockSpec((tr, cols), lambda i: (i, 0))
    return pl.pallas_call(
        body, name=name, grid=(rows // tr,), in_specs=[spec] * 4, out_specs=[spec] * 3,
        out_shape=[SDS(g.shape, F32)] * 3, compiler_params=_params(("parallel",)),
    )(g, w, m, v)


def _small_update(chip, tot, tot_rel, wmv):
    names = list(SMALL_PLACES)
    n = len(names)

    def body(chip_ref, tot_ref, quarter_ref, rel_ref, *refs):
        del chip_ref
        ins, outs = refs[:3 * n], refs[3 * n:]
        for i, nm in enumerate(names):
            source, row, (rows, cols) = SMALL_PLACES[nm]
            g = {"rows": tot_ref, "quarter": quarter_ref, "rel": rel_ref}[source][row:row + rows, 0:cols]
            outs[4 * i][...] = g
            outs[4 * i + 1][...], outs[4 * i + 2][...], outs[4 * i + 3][...] = _adam_step(
                g, ins[3 * i][...], ins[3 * i + 1][...], ins[3 * i + 2][...])

    whole = lambda shape: pl.BlockSpec(shape, lambda i, c: (0,) * len(shape))
    shapes = [SMALL_PLACES[nm][2] for nm in names]
    outs = pl.pallas_call(
        body, name="small_update",
        grid_spec=pltpu.PrefetchScalarGridSpec(
            num_scalar_prefetch=1, grid=(1,),
            in_specs=[whole(tot.shape), pl.BlockSpec((tot.shape[0], D // 4), lambda i, c: (0, c[0])),
                      whole(tot_rel.shape)] + [whole(shp) for shp in shapes for _ in range(3)],
            out_specs=[whole(shp) for shp in shapes for _ in range(4)]),
        out_shape=[SDS(shp, F32) for shp in shapes for _ in range(4)],
    )(chip, tot, tot, tot_rel, *[a for nm in names for a in wmv[nm]])
    return {nm: tuple(outs[4 * i:4 * i + 4]) for i, nm in enumerate(names)}


def _pad_rows(a, rows):
    return jnp.concatenate([a, jnp.zeros((rows - a.shape[0], a.shape[1]), a.dtype)], axis=0)


def _pad_cols(a, cols):
    return jnp.concatenate([a, jnp.zeros((a.shape[0], cols - a.shape[1]), a.dtype)], axis=1)


def kernel(x, a_pre_norm, a_w_in, a_conv_w, a_w_out, a_post_norm, kv_norm, w_kv, rel_bias, b_pre_norm, b_w_in, b_sinks, b_w_out, b_post_norm, loss_target, m_a_pre_norm, m_a_w_in, m_a_conv_w, m_a_w_out, m_a_post_norm, m_kv_norm, m_w_kv, m_rel_bias, m_b_pre_norm, m_b_w_in, m_b_sinks, m_b_w_out, m_b_post_norm, v_a_pre_norm, v_a_w_in, v_a_conv_w, v_a_w_out, v_a_post_norm, v_kv_norm, v_w_kv, v_rel_bias, v_b_pre_norm, v_b_w_in, v_b_sinks, v_b_w_out, v_b_post_norm):
    seq = x.shape[1]
    xs = x.reshape(seq, D)
    tgt = loss_target.reshape(seq, D)
    chip = 2 * lax.axis_index("x") + lax.axis_index("y")
    core = lax.axis_index("c")
    tm = _tile(seq, 512)
    tmw = _tile(seq, 1024)

    shards = [a_w_in[0], a_w_out[0], w_kv, b_w_in[0], b_w_out[0]]
    small_w = _pad_rows(jnp.concatenate([a_pre_norm, a_conv_w[0], a_post_norm], axis=0), 8)
    *own_only, small_g = _prepare_weights(shards, small_w)
    where = jnp.stack([core, chip]).astype(jnp.int32)
    small_full = small_g.transpose(1, 0, 2).reshape(8, D)
    g_apre, conv_w, g_apost = small_full[0:1], _pad_rows(small_full[1:4], 8), small_full[4:5]
    g_kv = kv_norm.reshape(1, D)

    proj, n1, (win_g, wouta_g, wkv_g, wbin_g, woutb_g) = _a_in(where[1:2], xs, g_apre, own_only, tmw)
    wouta = wouta_g.reshape(D, D)
    wkv = wkv_g.reshape(D, 2 * KV_W)
    woutb = woutb_g.reshape(D, D)
    ya, oa, h1, conv = _a_mix(proj, xs, conv_w, wouta, g_apost, tm)
    kv, q, zb = _b_in(h1, g_kv, b_pre_norm, wkv, wbin_g, tmw)
    tab = _bias_table(rel_bias, b_sinks.reshape(N_HEADS))
    att, stats = _attn_fwd(q, kv, tab)
    dh2, dqz, datt, loss_acc, dg_bpost, dw_outb, dw_outb16 = _mid(att, zb, h1, tgt, woutb, b_post_norm, tm)

    dqz, dkv, dtab = _attn_bwd(q, kv, datt, stats, tab, dqz)
    dh1, doa, dg_b, dw_bin, dw_kv, dw_bin16, dw_kv16 = _b_bwd(dqz, dkv, h1, dh2, oa, wbin_g, wkv, g_kv, b_pre_norm,
                                                              g_apost, tm)
    by_chip = lambda a, cols: a.reshape(N_CHIPS, D // 4, cols)
    grads1 = [by_chip(dw_kv, 2 * KV_W), dw_bin, by_chip(dw_outb, D)]
    sent1 = [by_chip(dw_kv16, 2 * KV_W), dw_bin16, by_chip(dw_outb16, D)]
    names1 = ["w_kv", "b_w_in", "b_w_out"]
    dproj, dconv_w, dw_outa, dw_outa16, from_devices1 = _a_bwd(doa, ya, conv, proj, conv_w, wouta, tm, sent1)
    shards1 = [_add_devices(where, g, r, "add_devices_" + nm) for g, r, nm in zip(grads1, from_devices1, names1)]
    tmw2 = _tile(seq, 4096)
    win_lo, win_lo16, outa_got, g_wkv, g_wbin, g_woutb = _dw_in_half(
        n1, dproj, 0, tmw2, "dw_a_in_lo", to_devices=by_chip(dw_outa16, D), shards=shards1)
    win_hi, win_hi16, win_got = _dw_in_half(n1, dproj, 1, tmw2, "dw_a_in_hi", to_owners=win_lo16)
    nt = seq // tmw
    dn_first, win_got = _a_in_bwd_matmul(dproj, win_g, tmw, max(nt - max(nt // 4, 1), 1), win_hi16, win_got)
    grad_x, dg_apre = _a_in_bwd(dn_first, dproj, xs, dh1, win_g, g_apre, tm)
    shards2 = [_add_win(where, win_lo, win_hi, win_got, "add_devices_a_w_in"),
               _add_devices(where, by_chip(dw_outa, D), outa_got, "add_devices_a_w_out")]
    drel, dsink = _bias_fold(dtab)

    smalls = jnp.concatenate([
        dg_apre[0:1], dg_b[2:3], dg_b[0:1], dg_b[1:2], dg_bpost[0:1], _pad_cols(dsink[0:1], D),
        _pad_cols(loss_acc[0:1], D), jnp.zeros((1, D), F32), dconv_w], axis=0)
    assert smalls.shape == (SMALL_ROWS, D)
    (g_win, g_wouta), gathered = _share_and_gather(shards2, (smalls, drel))
    tot, tot_rel = _sum_smalls(gathered)

    big = {}
    for nm, g, w, m, v in [("a_w_in", g_win, a_w_in, m_a_w_in, v_a_w_in), ("a_w_out", g_wouta, a_w_out, m_a_w_out, v_a_w_out),
                           ("w_kv", g_wkv, w_kv, m_w_kv, v_w_kv), ("b_w_in", g_wbin, b_w_in, m_b_w_in, v_b_w_in),
                           ("b_w_out", g_woutb, b_w_out, m_b_w_out, v_b_w_out)]:
        shp = w.shape
        two = (shp[-2], shp[-1])
        d, nm_, nv_ = _adamw(g, w.reshape(two), m.reshape(two), v.reshape(two), "adamw_" + nm)
        big[nm] = (g.reshape(shp), d.reshape(shp), nm_.reshape(shp), nv_.reshape(shp))

    given = {"a_pre_norm": (a_pre_norm, m_a_pre_norm, v_a_pre_norm), "a_conv_w": (a_conv_w, m_a_conv_w, v_a_conv_w),
             "a_post_norm": (a_post_norm, m_a_post_norm, v_a_post_norm), "kv_norm": (kv_norm, m_kv_norm, v_kv_norm),
             "rel_bias": (rel_bias, m_rel_bias, v_rel_bias), "b_pre_norm": (b_pre_norm, m_b_pre_norm, v_b_pre_norm),
             "b_sinks": (b_sinks, m_b_sinks, v_b_sinks), "b_post_norm": (b_post_norm, m_b_post_norm, v_b_post_norm)}
    small = _small_update(where[1:2], tot, tot_rel, {nm: tuple(a.reshape(SMALL_PLACES[nm][2]) for a in wmv)
                                            for nm, wmv in given.items()})
    order = ["a_pre_norm", "a_w_in", "a_conv_w", "a_w_out", "a_post_norm", "kv_norm", "w_kv", "rel_bias",
             "b_pre_norm", "b_w_in", "b_sinks", "b_w_out", "b_post_norm"]
    outs = []
    for which in range(4):
        for nm in order:
            outs.append(big[nm][which] if nm in big else small[nm][which].reshape(given[nm][0].shape))
    loss = 0.5 * tot[LOSS_ROW, 0]
    return (loss, grad_x.reshape(x.shape), *outs)
```

```python
import math

import jax
import jax.numpy as jnp
from jax import lax
from jax.experimental import pallas as pl
from jax.experimental.pallas import tpu as pltpu

F32 = jnp.float32
BF16 = jnp.bfloat16
MESH = pl.DeviceIdType.MESH
SDS = jax.ShapeDtypeStruct

D = 1024
HEAD_DIM = 64
N_HEADS = 16
N_KV = 2
GROUP = 8
KV_W = 128
BLK = 128
N_BUCKETS = 32
MAX_EXACT = 16
MAX_DISTANCE = 128
EPS = 1e-6
NEG_INF = -1e30
Q_SCALE = HEAD_DIM ** -0.5

ADAM_LR = 0.001
ADAM_B1 = 0.9
ADAM_B2 = 0.999
ADAM_EPS = 1e-08
ADAM_WD = 0.01
ADAM_STEP = 10

N_CHIPS = 4
N_DEV = 8
BIN_COLS = 2 * D // N_CHIPS
VMEM_LIMIT = 56 * 1024 * 1024
SMALL_ROWS = 16
LOSS_ROW = 6
SMALL_PLACES = {
    "a_pre_norm": ("quarter", 0, (1, D // 4)), "a_conv_w": ("quarter", 8, (3, D // 4)),
    "a_post_norm": ("quarter", 1, (1, D // 4)), "kv_norm": ("rows", 2, (1, D)),
    "rel_bias": ("rel", 0, (N_BUCKETS, N_HEADS)), "b_pre_norm": ("rows", 3, (1, D)),
    "b_sinks": ("rows", 5, (1, N_HEADS)), "b_post_norm": ("rows", 4, (1, D)),
}


def _bucket_thresholds():
    def bucket(d):
        big = MAX_EXACT + int(math.log(d / MAX_EXACT) / math.log(MAX_DISTANCE / MAX_EXACT)
                              * (N_BUCKETS - MAX_EXACT))
        return d if d < MAX_EXACT else min(big, N_BUCKETS - 1)
    out = []
    for b in range(MAX_EXACT + 1, N_BUCKETS):
        out.append(min(d for d in range(MAX_EXACT, MAX_DISTANCE) if bucket(d) >= b))
    return tuple(out)


BUCKET_THRESHOLDS = _bucket_thresholds()


def _params(semantics=None, vmem=VMEM_LIMIT):
    return pltpu.CompilerParams(dimension_semantics=semantics, vmem_limit_bytes=vmem)


def _tile(n, pref):
    return pref if n >= 2 * pref else max(n // 2, 8)


def _rms_scale(v):
    return lax.rsqrt(jnp.mean(v * v, axis=-1, keepdims=True) + EPS)


def _nt(a, b):
    return lax.dot_general(a, b, (((1,), (1,)), ((), ())), preferred_element_type=F32)


def _tn(a, b):
    return lax.dot_general(a, b, (((0,), (0,)), ((), ())), preferred_element_type=F32)


def _nn(a, b):
    return jnp.dot(a, b, preferred_element_type=F32)


def _silu_parts(z):
    sg = jax.nn.sigmoid(z)
    return sg, z * sg


def _dsilu(z, sg):
    return sg * (1.0 + z * (1.0 - sg))


def _write_gradient(acc, out32, out16, stage, sem):
    whole = pltpu.make_async_copy(acc, out32, sem)
    whole.start()
    rows = stage.shape[0]
    for k in range(acc.shape[0] // rows):
        stage[...] = acc[rows * k:rows * (k + 1), :].astype(BF16)
        pltpu.sync_copy(stage, out16.at[pl.ds(rows * k, rows)])
    whole.wait()


def _acc_row(ref, row, val):
    ref[row:row + 1, :] += val


def _gather_copies(outs, splits, ici_send, ici_recv, d2d_send, d2d_recv):
    x, y, c = lax.axis_index("x"), lax.axis_index("y"), lax.axis_index("c")
    k = 2 * x + y
    sibling = (x, y, 1 - c)

    def part(o_ref, chip, core, split):
        if not split:
            return o_ref.at[chip]
        h = o_ref.shape[1] // 2
        return o_ref.at[chip, pl.ds(pl.multiple_of(core * h, 16), h)]

    def remote(ref, a, j, sems, to):
        return pltpu.make_async_remote_copy(src_ref=ref, dst_ref=ref, send_sem=sems[0].at[3 * a + j],
                                            recv_sem=sems[1].at[3 * a + j], device_id=to, device_id_type=MESH)

    copies = []
    for a, (o_ref, split) in enumerate(zip(outs, splits)):
        for j, (px, py) in enumerate([(x, 1 - y), (1 - x, y), (1 - x, 1 - y)]):
            kj = 2 * px + py
            ici, d2d = (ici_send, ici_recv), (d2d_send, d2d_recv)
            copies.append((remote(part(o_ref, k, c, split), a, j, ici, (px, py, c)),
                           remote(part(o_ref, kj, c, split), a, j, ici, (px, py, c)),
                           remote(part(o_ref, kj, c, split), a, j, d2d, sibling) if split else None,
                           remote(part(o_ref, kj, 1 - c, split), a, j, d2d, sibling) if split else None))
    return copies


def _gather_sems(n):
    return [pltpu.SemaphoreType.DMA((3 * n,)) for _ in range(4)]


def _prepare_weights(shards, small):
    n = len(shards)

    def body(*refs):
        ins, small_in = refs[:n], refs[n]
        outs, small_out = refs[n + 1:2 * n + 1], refs[2 * n + 1]
        stages, put_sem = refs[2 * n + 2:3 * n + 2], refs[3 * n + 2]
        sems = refs[3 * n + 3:]
        k = 2 * lax.axis_index("x") + lax.axis_index("y")
        puts = []
        for a, (i_ref, stage, o_ref) in enumerate(zip(ins, stages, outs)):
            stage[...] = i_ref[...].astype(BF16)
            puts.append(pltpu.make_async_copy(stage, o_ref.at[k], put_sem.at[a]))
            puts[-1].start()
        small_out[k] = small_in[...]
        copies = _gather_copies([small_out], [False], *sems)
        for send, _, _, _ in copies:
            send.start()
        for _, arrival, _, _ in copies:
            arrival.wait_recv()
        for send, _, _, _ in copies:
            send.wait_send()
        for put in puts:
            put.wait()

    vm = pl.BlockSpec(memory_space=pltpu.VMEM)
    anyspace = pl.BlockSpec(memory_space=pl.ANY)
    out_shape = [SDS((N_CHIPS,) + s.shape, BF16) for s in shards] + [SDS((N_CHIPS,) + small.shape, F32)]
    return pl.pallas_call(
        body, name="prepare_weights", out_shape=out_shape,
        in_specs=[vm] * (n + 1), out_specs=[anyspace] * n + [vm],
        scratch_shapes=[pltpu.VMEM(s.shape, BF16) for s in shards] + [pltpu.SemaphoreType.DMA((n,))] + _gather_sems(1),
        compiler_params=pltpu.CompilerParams(vmem_limit_bytes=VMEM_LIMIT),
    )(*shards, small)


def _a_in(chip, x, g_pre, weights, tm):
    s = x.shape[0]
    nt = s // tm
    n = len(weights)

    def norm(x_tile, g_ref):
        xv = x_tile[...]
        return (xv * _rms_scale(xv) * g_ref[...]).astype(BF16)

    def body(chip_ref, x0_ref, xn_ref, g_ref, *refs):
        proj_ref, n1_ref = refs[n:n + 2]
        gathered = refs[n + 2:2 * n + 2]
        wbuf, n1_all, fetch_sem = refs[2 * n + 2:2 * n + 5]
        sems = refs[2 * n + 5:]
        jj, i = pl.program_id(0), pl.program_id(1)
        copies = _gather_copies(gathered, [True] * n, *sems)

        def fetch(rel):
            slot = jnp.bitwise_xor(chip_ref[0], rel)
            return pltpu.make_async_copy(gathered[0].at[slot], wbuf.at[rel % 2], fetch_sem.at[rel % 2])

        @pl.when((jj == 0) & (i == 0))
        def _():
            fetch(0).start()
            copies[0][0].start()
            copies[1][0].start()
            fetch(0).wait()

        for rel in (1, 2, 3):
            @pl.when((jj == rel) & (i == 0))
            def _():
                fetch(rel).wait()

        @pl.when(jj == 0)
        def _():
            @pl.when(i == 0)
            def _():
                n1_all[0] = norm(x0_ref, g_ref)
            n1_ref[...] = n1_all[i]
            ahead = norm(xn_ref, g_ref)
            proj_ref[...] = _nn(n1_all[i], wbuf[0]).astype(BF16)
            n1_all[jnp.minimum(i + 1, nt - 1)] = ahead

        @pl.when(jj > 0)
        def _():
            proj_ref[...] = _nn(n1_all[i], wbuf[jj % 2]).astype(BF16)

        for rel in (1, 2, 3):
            @pl.when((jj == rel - 1) & (i == max(nt - 3, 0)))
            def _():
                _, arrival, forward, _ = copies[rel - 1]
                arrival.wait_recv()
                forward.start()
                if rel == 1:
                    for send, _, _, _ in copies[2:]:
                        send.start()

            @pl.when((jj == rel - 1) & (i == max(nt - 2, 0)))
            def _():
                copies[rel - 1][3].wait_recv()
                fetch(rel).start()

        @pl.when((jj == 3) & (i == max(nt - 2, 0)))
        def _():
            for _, arrival, forward, _ in copies[3:]:
                arrival.wait_recv()
                forward.start()

        @pl.when((jj == 3) & (i == nt - 1))
        def _():
            for _, _, _, forwarded in copies[3:]:
                forwarded.wait_recv()
            for send, _, forward, _ in copies:
                forward.wait_send()
                send.wait_send()

    anyspace = pl.BlockSpec(memory_space=pl.ANY)
    proj, n1, *gathered = pl.pallas_call(
        body, name="a_in",
        grid_spec=pltpu.PrefetchScalarGridSpec(
            num_scalar_prefetch=1, grid=(4, nt),
            in_specs=[pl.BlockSpec((tm, D), lambda jj, i, c: (0, 0)),
                      pl.BlockSpec((tm, D), lambda jj, i, c: (jnp.where(jj == 0, jnp.minimum(i + 1, nt - 1), nt - 1), 0)),
                      pl.BlockSpec((1, D), lambda jj, i, c: (0, 0))] + [anyspace] * n,
            out_specs=[pl.BlockSpec((tm, D), lambda jj, i, c: (i, jnp.bitwise_xor(c[0], jj))),
                       pl.BlockSpec((tm, D), lambda jj, i, c: (jnp.where(jj == 0, i, nt - 1), 0))] + [anyspace] * n,
            scratch_shapes=[pltpu.VMEM((2, D, D), BF16), pltpu.VMEM((nt, tm, D), BF16),
                            pltpu.SemaphoreType.DMA((2,))] + _gather_sems(n)),
        out_shape=[SDS((s, 4 * D), BF16), SDS((s, D), BF16)] + [SDS(w.shape, w.dtype) for w in weights],
        input_output_aliases={4 + a: 2 + a for a in range(n)},
        compiler_params=_params(("arbitrary", "arbitrary")),
    )(chip, x, x, g_pre, *weights)
    return proj, n1, gathered


def _shift_rows(v, last, second_last, rows):
    v1 = jnp.where(rows >= 1, pltpu.roll(v, 1, 0), last)
    v2 = jnp.where(rows >= 2, pltpu.roll(v, 2, 0), jnp.where(rows == 1, last, second_last))
    return v1, v2


def _a_mix(proj, x, conv_w, w_out, g_post, tm):
    s = x.shape[0]

    def body(proj_ref, x_ref, cw_ref, w_ref, g_ref, ya_ref, oa_ref, h1_ref, conv_ref, carry):
        @pl.when(pl.program_id(0) == 0)
        def _():
            carry[...] = jnp.zeros_like(carry)
        v = proj_ref[:, D:2 * D].astype(F32) * proj_ref[:, 2 * D:3 * D].astype(F32)
        rows = lax.broadcasted_iota(jnp.int32, (tm, D), 0)
        before = carry[...]
        v1, v2 = _shift_rows(v, before[7:8, :], before[6:7, :], rows)
        carry[...] = v[tm - 8:tm, :]
        conv = cw_ref[0:1, :] * v2 + cw_ref[1:2, :] * v1 + cw_ref[2:3, :] * v
        conv_ref[...] = conv.astype(BF16)
        _, sz = _silu_parts(proj_ref[:, 3 * D:4 * D].astype(F32))
        ya = (proj_ref[:, 0:D].astype(F32) * conv * sz).astype(BF16)
        ya_ref[...] = ya
        oa = _nn(ya, w_ref[...])
        oa_ref[...] = oa.astype(BF16)
        h1_ref[...] = x_ref[...] + oa * _rms_scale(oa) * g_ref[...]

    row = lambda i: (i, 0)
    fix = lambda i: (0, 0)
    return pl.pallas_call(
        body, name="a_mix", grid=(s // tm,),
        in_specs=[pl.BlockSpec((tm, 4 * D), row), pl.BlockSpec((tm, D), row), pl.BlockSpec((8, D), fix),
                  pl.BlockSpec((D, D), fix), pl.BlockSpec((1, D), fix)],
        out_specs=[pl.BlockSpec((tm, D), row)] * 4,
        out_shape=[SDS((s, D), BF16), SDS((s, D), BF16), SDS((s, D), F32), SDS((s, D), BF16)],
        scratch_shapes=[pltpu.VMEM((8, D), F32)],
        compiler_params=_params(("arbitrary",)),
    )(proj, x, conv_w, w_out, g_post)


def _b_in(h1, g_kv, g_pre, w_kv, wbin_g, tm):
    s = h1.shape[0]

    def body(h_ref, gk_ref, gb_ref, wkv_ref, wb_ref, kv_ref, q_ref, z_ref):
        h = h_ref[...]
        hh = h * _rms_scale(h)
        nk = (hh * gk_ref[...]).astype(BF16)
        nb = (hh * gb_ref[...]).astype(BF16)
        kv_ref[...] = _nn(nk, wkv_ref[...]).astype(BF16)
        for j in range(2):
            q_ref[:, BIN_COLS * j:BIN_COLS * (j + 1)] = (_nn(nb, wb_ref[j]) * Q_SCALE).astype(BF16)
            z_ref[:, BIN_COLS * j:BIN_COLS * (j + 1)] = _nn(nb, wb_ref[2 + j]).astype(BF16)

    row = lambda i: (i, 0)
    fix = lambda i: (0, 0)
    return pl.pallas_call(
        body, name="b_in", grid=(s // tm,),
        in_specs=[pl.BlockSpec((tm, D), row), pl.BlockSpec((1, D), fix), pl.BlockSpec((1, D), fix),
                  pl.BlockSpec((D, 2 * KV_W), fix), pl.BlockSpec((N_CHIPS, D, BIN_COLS), lambda i: (0, 0, 0))],
        out_specs=[pl.BlockSpec((tm, 2 * KV_W), row), pl.BlockSpec((tm, D), row), pl.BlockSpec((tm, D), row)],
        out_shape=[SDS((s, 2 * KV_W), BF16), SDS((s, D), BF16), SDS((s, D), BF16)],
        compiler_params=_params(("parallel",)),
    )(h1, g_kv, g_pre, w_kv, wbin_g)


def _band_buckets():
    q = lax.broadcasted_iota(jnp.int32, (BLK, 2 * BLK), 0)
    k = lax.broadcasted_iota(jnp.int32, (BLK, 2 * BLK), 1)
    dist = q + BLK - k
    bucket = jnp.where(dist < MAX_EXACT, dist, MAX_EXACT)
    for t in BUCKET_THRESHOLDS:
        bucket = bucket + jnp.where(dist >= t, 1, 0)
    in_window = (dist >= 0) & (dist < BLK)
    return jnp.where(in_window, bucket, -1)


def _head_place(h):
    kh, j, e = h // GROUP, (h % GROUP) // 2, h % 2
    return kh, slice(BLK * j, BLK * (j + 1)), slice(2 * BLK * e, 2 * BLK * (e + 1))


def _bias_table(rel_bias, sinks):
    def body(rb_ref, sink_ref, tab_ref):
        bucket = _band_buckets()
        col = lax.broadcasted_iota(jnp.int32, (BLK, 2 * BLK), 1)
        for h in range(N_HEADS):
            acc = jnp.where(bucket < 0, NEG_INF, 0.0).astype(F32)
            for b in range(N_BUCKETS):
                acc = jnp.where(bucket == b, rb_ref[b, h], acc)
            acc = jnp.where(col == 0, sink_ref[h], acc)
            kh, rows, cols = _head_place(h)
            tab_ref[1, kh, rows, cols] = acc
            tab_ref[0, kh, rows, cols] = jnp.where((col > 0) & (col < BLK), NEG_INF, acc)

    return pl.pallas_call(
        body, name="bias_table", out_shape=SDS((2, N_KV, 4 * BLK, 4 * BLK), F32),
        in_specs=[pl.BlockSpec(memory_space=pltpu.SMEM), pl.BlockSpec(memory_space=pltpu.SMEM)],
        out_specs=pl.BlockSpec(memory_space=pltpu.VMEM),
    )(rel_bias, sinks)


def _bias_fold(dtab):
    def body(dtab_ref, out_ref, dsink_ref):
        bucket = _band_buckets()
        row = lax.broadcasted_iota(jnp.int32, (N_BUCKETS, 128), 0)
        lane = lax.broadcasted_iota(jnp.int32, (N_BUCKETS, 128), 1)
        row8 = lax.broadcasted_iota(jnp.int32, (8, 128), 0)
        lane8 = lax.broadcasted_iota(jnp.int32, (8, 128), 1)
        acc = jnp.zeros((N_BUCKETS, 128), F32)
        dsink = jnp.zeros((8, 128), F32)
        for h in range(N_HEADS):
            kh, rows, cols = _head_place(h)
            dt = dtab_ref[kh, rows, cols]
            for b in range(N_BUCKETS):
                val = jnp.sum(jnp.where(bucket == b, dt, 0.0))
                acc = acc + jnp.where((row == b) & (lane == h), val, 0.0)
            dsink = dsink + jnp.where((row8 == 0) & (lane8 == h), jnp.sum(dt[:, 0:1]), 0.0)
        out_ref[...] = acc
        dsink_ref[...] = dsink

    vm = pl.BlockSpec(memory_space=pltpu.VMEM)
    return pl.pallas_call(
        body, name="bias_fold", out_shape=[SDS((N_BUCKETS, 128), F32), SDS((8, 128), F32)],
        in_specs=[vm], out_specs=[vm, vm],
    )(dtab)


def _pair_operands(prev, cur):
    t = jnp.concatenate([prev, cur], axis=0).astype(F32)
    t = jnp.where(lax.broadcasted_iota(jnp.int32, t.shape, 0) == 0, 0.0, t)
    tr = pltpu.roll(t, HEAD_DIM, 1)
    lo = lax.broadcasted_iota(jnp.int32, t.shape, 1) < HEAD_DIM
    zero = jnp.zeros_like(t)
    head0 = jnp.concatenate([jnp.where(lo, t, zero), jnp.where(lo, zero, tr)], axis=0).astype(BF16)
    head1 = jnp.concatenate([jnp.where(lo, tr, zero), jnp.where(lo, zero, t)], axis=0).astype(BF16)
    return head0, head1


def _pair_fold(d0, d1):
    lo = lax.broadcasted_iota(jnp.int32, (2 * BLK, KV_W), 1) < HEAD_DIM
    zero = jnp.zeros((2 * BLK, KV_W), F32)
    g0 = jnp.where(lo, d0[0:256], zero) + pltpu.roll(jnp.where(lo, zero, d0[256:512]), HEAD_DIM, 1)
    g1 = pltpu.roll(jnp.where(lo, d1[0:256], zero), HEAD_DIM, 1) + jnp.where(lo, zero, d1[256:512])
    return jnp.where(lax.broadcasted_iota(jnp.int32, (2 * BLK, KV_W), 0) == 0, 0.0, g0 + g1)


def _stack_pairs(ref, kh):
    return jnp.concatenate([ref[:, 128 * (4 * kh + j):128 * (4 * kh + j + 1)] for j in range(4)], axis=0)


def _table_spec():
    return pl.BlockSpec((1, N_KV, 4 * BLK, 4 * BLK), lambda n: (jnp.minimum(n, 1), 0, 0, 0))


def _attn_fwd(q, kv, tab):
    s = q.shape[0]

    def body(q_ref, kp_ref, kc_ref, vp_ref, vc_ref, tab_ref, att_ref, stats_ref):
        k2 = _pair_operands(kp_ref[...], kc_ref[...])
        v2 = _pair_operands(vp_ref[...], vc_ref[...])
        lane = lax.broadcasted_iota(jnp.int32, (BLK, 128), 1)
        stats = jnp.zeros((BLK, 128), F32)
        for kh in range(N_KV):
            sc = _nt(_stack_pairs(q_ref, kh), k2[kh])
            ps = []
            for e in range(2):
                lg = sc[:, 256 * e:256 * (e + 1)] + tab_ref[0, kh, :, 256 * e:256 * (e + 1)]
                m = jnp.max(lg, axis=-1, keepdims=True)
                ex = jnp.exp(lg - m)
                den = jnp.sum(ex, axis=-1, keepdims=True)
                ps.append(ex * (1.0 / den))
                lse = m + jnp.log(den)
                for j in range(4):
                    stats = jnp.where(lane == GROUP * kh + 2 * j + e, lse[BLK * j:BLK * (j + 1)], stats)
            out = _nn(jnp.concatenate(ps, axis=1).astype(BF16), v2[kh])
            for j in range(4):
                att_ref[:, 128 * (4 * kh + j):128 * (4 * kh + j + 1)] = out[BLK * j:BLK * (j + 1)].astype(BF16)
        stats_ref[...] = stats

    cur = lambda n: (n, 0)
    prev = lambda n: (jnp.maximum(n - 1, 0), 0)
    return pl.pallas_call(
        body, name="attn_fwd", grid=(s // BLK,),
        in_specs=[pl.BlockSpec((BLK, D), cur),
                  pl.BlockSpec((BLK, KV_W), prev), pl.BlockSpec((BLK, KV_W), cur),
                  pl.BlockSpec((BLK, KV_W), lambda n: (jnp.maximum(n - 1, 0), 1)),
                  pl.BlockSpec((BLK, KV_W), lambda n: (n, 1)), _table_spec()],
        out_specs=[pl.BlockSpec((BLK, D), cur), pl.BlockSpec((BLK, 128), cur)],
        out_shape=[SDS((s, D), BF16), SDS((s, 128), F32)],
        compiler_params=_params(("parallel",)),
    )(q, kv, kv, kv, kv, tab)


def _mid(att, zb, h1, tgt, w_out, g_post, tm):
    s = att.shape[0]
    nt = s // tm

    def body(att_ref, z_ref, h1_ref, t_ref, w_ref, g_ref,
             dh_ref, dqz_ref, datt_ref, loss_ref, dg_ref, dw_ref, dw16_ref, dw_acc, stage, put_sem):
        @pl.when(pl.program_id(0) == 0)
        def _():
            loss_ref[...] = jnp.zeros_like(loss_ref)
            dg_ref[...] = jnp.zeros_like(dg_ref)
            dw_acc[...] = jnp.zeros_like(dw_acc)
        att = att_ref[...].astype(F32)
        z = z_ref[...].astype(F32)
        sg, sz = _silu_parts(z)
        ob = (att * sz).astype(BF16)
        y2 = _nn(ob, w_ref[...])
        r2 = _rms_scale(y2)
        yh = y2 * r2
        g = g_ref[...]
        err = (h1_ref[...] + yh * g) - t_ref[...]
        loss_ref[...] += jnp.sum(jnp.sum(err * err, axis=-1, keepdims=True) / D)
        dh = err / D
        dh_ref[...] = dh
        _acc_row(dg_ref, 0, jnp.sum(dh * yh, axis=0, keepdims=True))
        dyh = dh * g
        dy = (r2 * (dyh - yh * jnp.mean(dyh * yh, axis=-1, keepdims=True))).astype(BF16)
        dw_acc[...] += _tn(ob, dy)
        dob = _nt(dy, w_ref[...])
        datt_ref[...] = (dob * sz).astype(BF16)
        dqz_ref[...] = (dob * att * _dsilu(z, sg)).astype(BF16)

        @pl.when(pl.program_id(0) == nt - 1)
        def _():
            _write_gradient(dw_acc, dw_ref, dw16_ref, stage, put_sem)

    row = lambda i: (i, 0)
    fix = lambda i: (0, 0)
    anyspace = pl.BlockSpec(memory_space=pl.ANY)
    return pl.pallas_call(
        body, name="mid", grid=(nt,),
        in_specs=[pl.BlockSpec((tm, D), row)] * 4 + [pl.BlockSpec((D, D), fix), pl.BlockSpec((1, D), fix)],
        out_specs=[pl.BlockSpec((tm, D), row), pl.BlockSpec((tm, D), lambda i: (i, 1)), pl.BlockSpec((tm, D), row),
                   pl.BlockSpec((8, 128), fix), pl.BlockSpec((8, D), fix), anyspace, anyspace],
        out_shape=[SDS((s, D), F32), SDS((s, 2 * D), BF16), SDS((s, D), BF16), SDS((8, 128), F32),
                   SDS((8, D), F32), SDS((D, D), F32), SDS((D, D), BF16)],
        scratch_shapes=[pltpu.VMEM((D, D), F32), pltpu.VMEM((D // 4, D), BF16), pltpu.SemaphoreType.DMA],
        compiler_params=_params(("arbitrary",)),
    )(att, zb, h1, tgt, w_out, g_post)


def _attn_bwd(q, kv, datt, stats, tab, dqz):
    s = q.shape[0]
    nb = s // BLK

    def body(q_ref, kp_ref, kc_ref, vp_ref, vc_ref, da_ref, st_ref, tab_ref, dqz_in,
             dq_ref, dkv_ref, dtab_ref, dk_carry, dv_carry):
        del dqz_in
        n = pl.program_id(0)

        @pl.when(n == 0)
        def _():
            dtab_ref[...] = jnp.zeros_like(dtab_ref)
            dk_carry[...] = jnp.zeros_like(dk_carry)
            dv_carry[...] = jnp.zeros_like(dv_carry)

        @pl.when(n < nb)
        def _():
            k2 = _pair_operands(kp_ref[...], kc_ref[...])
            v2 = _pair_operands(vp_ref[...], vc_ref[...])
            lane = lax.broadcasted_iota(jnp.int32, (BLK, 128), 1)
            stats = st_ref[...]
            dk2, dv2 = [], []
            for kh in range(N_KV):
                qs = _stack_pairs(q_ref, kh)
                das = _stack_pairs(da_ref, kh)
                sc = _nt(qs, k2[kh])
                dp = _nt(das, v2[kh])
                ps, dss = [], []
                for e in range(2):
                    heads = [GROUP * kh + 2 * j + e for j in range(4)]
                    lse = jnp.concatenate([jnp.sum(jnp.where(lane == h, stats, 0.0), axis=-1, keepdims=True)
                                           for h in heads], axis=0)
                    cols = slice(256 * e, 256 * (e + 1))
                    p = jnp.exp(sc[:, cols] + tab_ref[0, kh, :, cols] - lse)
                    delta = jnp.sum(p * dp[:, cols], axis=-1, keepdims=True)
                    ds = p * (dp[:, cols] - delta)
                    dtab_ref[kh, :, cols] += ds
                    ps.append(p)
                    dss.append(ds)
                p2 = jnp.concatenate(ps, axis=1).astype(BF16)
                ds2 = jnp.concatenate(dss, axis=1).astype(BF16)
                dq = _nn(ds2, k2[kh]) * Q_SCALE
                for j in range(4):
                    dq_ref[:, 128 * (4 * kh + j):128 * (4 * kh + j + 1)] = dq[BLK * j:BLK * (j + 1)].astype(BF16)
                dk2.append(_tn(ds2, qs))
                dv2.append(_tn(p2, das))
            dkk = _pair_fold(dk2[0], dk2[1])
            dvv = _pair_fold(dv2[0], dv2[1])
            dkv_ref[:, 0:KV_W] = (dk_carry[...] + dkk[0:BLK]).astype(BF16)
            dkv_ref[:, KV_W:2 * KV_W] = (dv_carry[...] + dvv[0:BLK]).astype(BF16)
            dk_carry[...] = dkk[BLK:2 * BLK]
            dv_carry[...] = dvv[BLK:2 * BLK]

        @pl.when(n == nb)
        def _():
            dkv_ref[:, 0:KV_W] = dk_carry[...].astype(BF16)
            dkv_ref[:, KV_W:2 * KV_W] = dv_carry[...].astype(BF16)

    cur = lambda n: (jnp.minimum(n, nb - 1), 0)
    prev = lambda n: (jnp.clip(n - 1, 0, nb - 1), 0)
    return pl.pallas_call(
        body, name="attn_bwd", grid=(nb + 1,),
        in_specs=[pl.BlockSpec((BLK, D), cur),
                  pl.BlockSpec((BLK, KV_W), prev), pl.BlockSpec((BLK, KV_W), cur),
                  pl.BlockSpec((BLK, KV_W), lambda n: (jnp.clip(n - 1, 0, nb - 1), 1)),
                  pl.BlockSpec((BLK, KV_W), lambda n: (jnp.minimum(n, nb - 1), 1)),
                  pl.BlockSpec((BLK, D), cur), pl.BlockSpec((BLK, 128), cur), _table_spec(),
                  pl.BlockSpec(memory_space=pl.ANY)],
        out_specs=[pl.BlockSpec((BLK, D), cur), pl.BlockSpec((BLK, 2 * KV_W), prev),
                   pl.BlockSpec((N_KV, 4 * BLK, 4 * BLK), lambda n: (0, 0, 0))],
        out_shape=[SDS((s, 2 * D), BF16), SDS((s, 2 * KV_W), BF16), SDS((N_KV, 4 * BLK, 4 * BLK), F32)],
        scratch_shapes=[pltpu.VMEM((BLK, KV_W), F32), pltpu.VMEM((BLK, KV_W), F32)],
        input_output_aliases={8: 0},
        compiler_params=_params(("arbitrary",)),
    )(q, kv, kv, kv, kv, datt, stats, tab, dqz)


def _b_bwd(dqz, dkv, h1, dh2, oa, wbin_g, w_kv, g_kv, g_pre, g_apost, tm):
    s = h1.shape[0]
    nt = s // tm

    def body(dqz_ref, dkv_ref, h_ref, dh2_ref, oa_ref, wb_ref, wkv_ref, gk_ref, gb_ref, ga_ref,
             dh1_ref, doa_ref, dg_ref, dwb_ref, dwkv_ref, dwb16_ref, dwkv16_ref, wcat, dwb_acc, dwkv_acc, put_sem):
        @pl.when(pl.program_id(0) == 0)
        def _():
            dg_ref[...] = jnp.zeros_like(dg_ref)
            dwb_acc[...] = jnp.zeros_like(dwb_acc)
            dwkv_acc[...] = jnp.zeros_like(dwkv_acc)
            for j in range(N_CHIPS):
                pltpu.sync_copy(wb_ref.at[j], wcat.at[:, pl.ds(BIN_COLS * j, BIN_COLS)])
        dnb = _nt(dqz_ref[...], wcat[...])
        dnk = _nt(dkv_ref[...], wkv_ref[...])
        h = h_ref[...]
        r = _rms_scale(h)
        hh = h * r
        dwb_acc[...] += _tn((hh * gb_ref[...]).astype(BF16), dqz_ref[...])
        dwkv_acc[...] += _tn((hh * gk_ref[...]).astype(BF16), dkv_ref[...])
        _acc_row(dg_ref, 0, jnp.sum(dnk * hh, axis=0, keepdims=True))
        _acc_row(dg_ref, 1, jnp.sum(dnb * hh, axis=0, keepdims=True))
        dhh = dnb * gb_ref[...] + dnk * gk_ref[...]
        dh1 = dh2_ref[...] + r * (dhh - hh * jnp.mean(dhh * hh, axis=-1, keepdims=True))
        dh1_ref[...] = dh1
        oa = oa_ref[...].astype(F32)
        ra = _rms_scale(oa)
        oh = oa * ra
        _acc_row(dg_ref, 2, jnp.sum(dh1 * oh, axis=0, keepdims=True))
        doh = dh1 * ga_ref[...]
        doa_ref[...] = (ra * (doh - oh * jnp.mean(doh * oh, axis=-1, keepdims=True))).astype(BF16)

        @pl.when(pl.program_id(0) == nt - 1)
        def _():
            wcat[...] = dwb_acc[...].astype(BF16)
            puts = [pltpu.make_async_copy(dwkv_acc, dwkv_ref, put_sem.at[2 * N_CHIPS])]
            for j in range(N_CHIPS):
                cols = pl.ds(BIN_COLS * j, BIN_COLS)
                puts.append(pltpu.make_async_copy(dwb_acc.at[:, cols], dwb_ref.at[j], put_sem.at[2 * j]))
                puts.append(pltpu.make_async_copy(wcat.at[:, cols], dwb16_ref.at[j], put_sem.at[2 * j + 1]))
            for put in puts:
                put.start()
            for put in puts:
                put.wait()
            wcat[:, 0:2 * KV_W] = dwkv_acc[...].astype(BF16)
            pltpu.sync_copy(wcat.at[:, pl.ds(0, 2 * KV_W)], dwkv16_ref)

    row = lambda i: (i, 0)
    fix = lambda i: (0, 0)
    anyspace = pl.BlockSpec(memory_space=pl.ANY)
    return pl.pallas_call(
        body, name="b_bwd", grid=(nt,),
        in_specs=[pl.BlockSpec((tm, 2 * D), row), pl.BlockSpec((tm, 2 * KV_W), row), pl.BlockSpec((tm, D), row),
                  pl.BlockSpec((tm, D), row), pl.BlockSpec((tm, D), row), anyspace, pl.BlockSpec((D, 2 * KV_W), fix),
                  pl.BlockSpec((1, D), fix), pl.BlockSpec((1, D), fix), pl.BlockSpec((1, D), fix)],
        out_specs=[pl.BlockSpec((tm, D), row), pl.BlockSpec((tm, D), row), pl.BlockSpec((8, D), fix)] + [anyspace] * 4,
        out_shape=[SDS((s, D), F32), SDS((s, D), BF16), SDS((8, D), F32), SDS((N_CHIPS, D, BIN_COLS), F32),
                   SDS((D, 2 * KV_W), F32), SDS((N_CHIPS, D, BIN_COLS), BF16), SDS((D, 2 * KV_W), BF16)],
        scratch_shapes=[pltpu.VMEM((D, 2 * D), BF16), pltpu.VMEM((D, 2 * D), F32), pltpu.VMEM((D, 2 * KV_W), F32),
                        pltpu.SemaphoreType.DMA((2 * N_CHIPS + 1,))],
        compiler_params=_params(("arbitrary",)),
    )(dqz, dkv, h1, dh2, oa, wbin_g, w_kv, g_kv, g_pre, g_apost)


def _to_owner_core(pieces, r, send, recv, core, action):
    x, y, c = lax.axis_index("x"), lax.axis_index("y"), lax.axis_index("c")
    for kp in range(N_CHIPS):
        px, py = kp >> 1, kp & 1
        rel = 4 * (x + px - 2 * x * px) + 2 * (y + py - 2 * y * py) + (c + core - 2 * c * core)

        @pl.when(rel != 0)
        def _():
            cp = pltpu.make_async_remote_copy(src_ref=pieces.at[kp], dst_ref=r.at[rel - 1], send_sem=send.at[kp],
                                              recv_sem=recv.at[rel - 1], device_id=(px, py, core), device_id_type=MESH)
            if action == "start":
                cp.start()
            else:
                cp.wait_send()
    if action == "wait":
        @pl.when(c == core)
        def _():
            for rel in range(1, N_DEV):
                pltpu.make_async_remote_copy(src_ref=pieces.at[0], dst_ref=r.at[rel - 1], send_sem=send.at[0],
                                             recv_sem=recv.at[rel - 1], device_id=(x, y, c),
                                             device_id_type=MESH).wait_recv()


def _owner_core_sems():
    return [pltpu.SemaphoreType.DMA((N_CHIPS,)), pltpu.SemaphoreType.DMA((N_DEV - 1,))]


def _device_exchange(grads, recvs, send, recv):
    x, y, c = lax.axis_index("x"), lax.axis_index("y"), lax.axis_index("c")
    copies = []
    for a, (g, r) in enumerate(zip(grads, recvs)):
        h = g.shape[1] // 2
        for rel in range(1, N_DEV):
            fx, fy, fc = rel >> 2, (rel >> 1) & 1, rel & 1
            px, py, pc = x + fx - 2 * x * fx, y + fy - 2 * y * fy, c + fc - 2 * c * fc
            sem = (N_DEV - 1) * a + rel - 1
            copies.append(pltpu.make_async_remote_copy(
                src_ref=g.at[2 * px + py, pl.ds(pl.multiple_of(pc * h, 16), h)], dst_ref=r.at[rel - 1],
                send_sem=send.at[sem], recv_sem=recv.at[sem], device_id=(px, py, pc), device_id_type=MESH))
    return copies


def _device_exchange_specs(grads):
    anyspace = pl.BlockSpec(memory_space=pl.ANY)
    n = len(grads)
    count = (N_DEV - 1) * n
    return ([anyspace] * n, [anyspace] * n,
            [SDS((N_DEV - 1, g.shape[1] // 2, g.shape[2]), g.dtype) for g in grads],
            [pltpu.SemaphoreType.DMA((count,)), pltpu.SemaphoreType.DMA((count,))])


def _a_bwd(doa, ya, conv, proj, conv_w, w_out, tm, parts):
    s = doa.shape[0]
    nt = s // tm
    n = len(parts)
    ex_in, ex_out, ex_shape, ex_sems = _device_exchange_specs(parts)

    def body(*refs):
        doa_ref, ya_ref, conv_ref, proj_ref, cw_ref, w_ref = refs[:6]
        part_refs = refs[6:6 + n]
        dproj_ref, dcw_ref, dw_ref, dw16_ref = refs[6 + n:10 + n]
        recv_refs = refs[10 + n:10 + 2 * n]
        carry, dw_acc, stage, put_sem, send, recv = refs[10 + 2 * n:]
        i = pl.program_id(0)

        @pl.when(i == 0)
        def _():
            dcw_ref[...] = jnp.zeros_like(dcw_ref)
            carry[...] = jnp.zeros_like(carry)
            dw_acc[...] = jnp.zeros_like(dw_acc)
            for cp in _device_exchange(part_refs, recv_refs, send, recv):
                cp.start()
        dya = _nt(doa_ref[...], w_ref[...])
        dw_acc[...] += _tn(ya_ref[...], doa_ref[...])
        bg = proj_ref[:, 0:D].astype(F32)
        cg = proj_ref[:, D:2 * D].astype(F32)
        u = proj_ref[:, 2 * D:3 * D].astype(F32)
        z = proj_ref[:, 3 * D:4 * D].astype(F32)
        v = cg * u
        rows = lax.broadcasted_iota(jnp.int32, (tm, D), 0)
        conv = conv_ref[...].astype(F32)
        sg, sz = _silu_parts(z)
        dproj_ref[:, 0:D] = (dya * conv * sz).astype(BF16)
        dproj_ref[:, 3 * D:4 * D] = (dya * bg * conv * _dsilu(z, sg)).astype(BF16)
        dconv = dya * bg * sz
        after = carry[...]
        up1 = jnp.where(rows < tm - 1, pltpu.roll(dconv, tm - 1, 0), after[0:1, :])
        up2 = jnp.where(rows < tm - 2, pltpu.roll(dconv, tm - 2, 0),
                        jnp.where(rows == tm - 2, after[0:1, :], after[1:2, :]))
        carry[...] = dconv[0:8, :]
        _acc_row(dcw_ref, 0, jnp.sum(up2 * v, axis=0, keepdims=True))
        _acc_row(dcw_ref, 1, jnp.sum(up1 * v, axis=0, keepdims=True))
        _acc_row(dcw_ref, 2, jnp.sum(dconv * v, axis=0, keepdims=True))
        dv = cw_ref[2:3, :] * dconv + cw_ref[1:2, :] * up1 + cw_ref[0:1, :] * up2
        dproj_ref[:, D:2 * D] = (dv * u).astype(BF16)
        dproj_ref[:, 2 * D:3 * D] = (dv * cg).astype(BF16)

        @pl.when(i == nt - 1)
        def _():
            _write_gradient(dw_acc, dw_ref, dw16_ref, stage, put_sem)
            for cp in _device_exchange(part_refs, recv_refs, send, recv):
                cp.wait()

    rev = lambda i: (nt - 1 - i, 0)
    fix = lambda i: (0, 0)
    anyspace = pl.BlockSpec(memory_space=pl.ANY)
    dproj, dcw, dw, dw16, *got = pl.pallas_call(
        body, name="a_bwd", grid=(nt,),
        in_specs=[pl.BlockSpec((tm, D), rev), pl.BlockSpec((tm, D), rev), pl.BlockSpec((tm, D), rev),
                  pl.BlockSpec((tm, 4 * D), rev), pl.BlockSpec((8, D), fix), pl.BlockSpec((D, D), fix)] + ex_in,
        out_specs=[pl.BlockSpec((tm, 4 * D), rev), pl.BlockSpec((8, D), fix), anyspace, anyspace] + ex_out,
        out_shape=[SDS((s, 4 * D), BF16), SDS((8, D), F32), SDS((D, D), F32), SDS((D, D), BF16)] + ex_shape,
        scratch_shapes=[pltpu.VMEM((8, D), F32), pltpu.VMEM((D, D), F32), pltpu.VMEM((D // 4, D), BF16),
                        pltpu.SemaphoreType.DMA] + ex_sems,
        compiler_params=_params(("arbitrary",)),
    )(doa, ya, conv, proj, conv_w, w_out, *parts)
    return dproj, dcw, dw, dw16, got


def _dn1(dp_ref, w_ref):
    dn = _nt(dp_ref[:, 0:D], w_ref[0])
    for j in range(1, 4):
        dn = dn + _nt(dp_ref[:, D * j:D * (j + 1)], w_ref[j])
    return dn


def _a_in_bwd_matmul(dproj, win_g, tm, count, win_half, win_got):
    def body(dp_ref, w_ref, half_ref, got_in, dn_ref, got_ref, wcat, send, recv):
        del got_in

        @pl.when(pl.program_id(0) == 0)
        def _():
            _to_owner_core(half_ref, got_ref, send, recv, 1, "start")
            for j in range(N_CHIPS):
                pltpu.sync_copy(w_ref.at[j], wcat.at[:, pl.ds(D * j, D)])
        dn_ref[...] = _nt(dp_ref[...], wcat[...]).astype(BF16)

        @pl.when(pl.program_id(0) == count - 1)
        def _():
            _to_owner_core(half_ref, got_ref, send, recv, 1, "wait")

    row = lambda i: (i, 0)
    anyspace = pl.BlockSpec(memory_space=pl.ANY)
    return pl.pallas_call(
        body, name="a_in_bwd_matmul", grid=(count,),
        in_specs=[pl.BlockSpec((tm, 4 * D), row), anyspace, anyspace, anyspace],
        out_specs=[pl.BlockSpec((tm, D), row), anyspace],
        out_shape=[SDS((count * tm, D), BF16), SDS(win_got.shape, win_got.dtype)],
        scratch_shapes=[pltpu.VMEM((D, 4 * D), BF16)] + _owner_core_sems(),
        input_output_aliases={3: 1},
        compiler_params=_params(("arbitrary",)),
    )(dproj, win_g, win_half, win_got)


def _a_in_bwd(dn_first, dproj, x, dh1, win_g, g_pre, tm):
    s = x.shape[0]
    nt = s // tm
    count = dn_first.shape[0] // tm

    def body(dn_ref, dp_ref, x_ref, dh_ref, w_ref, g_ref, gx_ref, dg_ref, dn_s):
        i = pl.program_id(0)

        @pl.when(i == 0)
        def _():
            dg_ref[...] = jnp.zeros_like(dg_ref)

        @pl.when(i < count)
        def _():
            dn_s[...] = dn_ref[...].astype(F32)

        @pl.when(i >= count)
        def _():
            dn_s[...] = _dn1(dp_ref, w_ref)
        dn = dn_s[...]
        xv = x_ref[...]
        r = _rms_scale(xv)
        xh = xv * r
        _acc_row(dg_ref, 0, jnp.sum(dn * xh, axis=0, keepdims=True))
        dxh = dn * g_ref[...]
        gx_ref[...] = dh_ref[...] + r * (dxh - xh * jnp.mean(dxh * xh, axis=-1, keepdims=True))

    row = lambda i: (i, 0)
    fix = lambda i: (0, 0)
    return pl.pallas_call(
        body, name="a_in_bwd", grid=(nt,),
        in_specs=[pl.BlockSpec((tm, D), lambda i: (jnp.minimum(i, count - 1), 0)),
                  pl.BlockSpec((tm, 4 * D), lambda i: (jnp.maximum(i, count), 0)),
                  pl.BlockSpec((tm, D), row), pl.BlockSpec((tm, D), row),
                  pl.BlockSpec((4, D, D), lambda i: (0, 0, 0)), pl.BlockSpec((1, D), fix)],
        out_specs=[pl.BlockSpec((tm, D), row), pl.BlockSpec((8, D), fix)],
        out_shape=[SDS((s, D), F32), SDS((8, D), F32)],
        scratch_shapes=[pltpu.VMEM((tm, D), F32)],
        compiler_params=_params(("arbitrary",)),
    )(dn_first, dproj, x, dh1, win_g, g_pre)


def _swap_halves(shards, send, recv):
    x, y, c = lax.axis_index("x"), lax.axis_index("y"), lax.axis_index("c")
    sibling = (x, y, 1 - c)
    copies = []
    for b, full in enumerate(shards):
        h = full.shape[0] // 2
        mine = full.at[pl.ds(pl.multiple_of(c * h, 8), h)]
        theirs = full.at[pl.ds(pl.multiple_of((1 - c) * h, 8), h)]
        copies.append((pltpu.make_async_remote_copy(src_ref=mine, dst_ref=mine, send_sem=send.at[b], recv_sem=recv.at[b],
                                                    device_id=sibling, device_id_type=MESH),
                       pltpu.make_async_remote_copy(src_ref=mine, dst_ref=theirs, send_sem=send.at[b], recv_sem=recv.at[b],
                                                    device_id=sibling, device_id_type=MESH)))
    return copies


def _dw_in_half(n1, dproj, core, tmw, name, to_owners=None, to_devices=None, shards=()):
    s = n1.shape[0]
    h = D // 2
    nt = s // tmw
    n_sh = len(shards)
    if to_owners is not None:
        sent_array, sems, got_shape = to_owners, _owner_core_sems(), SDS((N_DEV - 1, h, D), BF16)
    else:
        sent_array = to_devices
        _, _, (got_shape,), sems = _device_exchange_specs([to_devices])

    def body(*refs):
        a_ref, b_ref, sent = refs[:3]
        o_ref, o16_ref, got = refs[3 + n_sh:6 + n_sh]
        shard_refs = refs[6 + n_sh:6 + 2 * n_sh]
        send, recv = refs[6 + 2 * n_sh:8 + 2 * n_sh]
        swap_sems = refs[8 + 2 * n_sh:]
        j, t = pl.program_id(0), pl.program_id(1)

        def exchange(action):
            if to_owners is not None:
                _to_owner_core(sent, got, send, recv, 1 - core, action)
            else:
                for cp in _device_exchange([sent], [got], send, recv):
                    cp.start() if action == "start" else cp.wait()

        @pl.when((j == 0) & (t == 0))
        def _():
            exchange("start")
            if n_sh:
                for mine, _ in _swap_halves(shard_refs, *swap_sems):
                    mine.start()

        @pl.when(t == 0)
        def _():
            o_ref[...] = jnp.zeros_like(o_ref)
        o_ref[0] += _tn(a_ref[...], b_ref[...])

        @pl.when(t == nt - 1)
        def _():
            o16_ref[...] = o_ref[...].astype(BF16)

        @pl.when((j == N_CHIPS - 1) & (t == nt - 1))
        def _():
            exchange("wait")
            if n_sh:
                for mine, theirs in _swap_halves(shard_refs, *swap_sems):
                    theirs.wait_recv()
                    mine.wait_send()

    anyspace = pl.BlockSpec(memory_space=pl.ANY)
    slot = pl.BlockSpec((1, h, D), lambda j, t: (j, 0, 0))
    swap_scratch = [pltpu.SemaphoreType.DMA((n_sh,)), pltpu.SemaphoreType.DMA((n_sh,))] if n_sh else []
    return pl.pallas_call(
        body, name=name, grid=(N_CHIPS, nt),
        in_specs=[pl.BlockSpec((tmw, h), lambda j, t: (t, core)), pl.BlockSpec((tmw, D), lambda j, t: (t, j))]
        + [anyspace] * (1 + n_sh),
        out_specs=[slot, slot] + [anyspace] * (1 + n_sh),
        out_shape=[SDS((N_CHIPS, h, D), F32), SDS((N_CHIPS, h, D), BF16), got_shape]
        + [SDS(sh.shape, F32) for sh in shards],
        scratch_shapes=sems + swap_scratch,
        input_output_aliases={3 + b: 3 + b for b in range(n_sh)},
        compiler_params=_params(("arbitrary", "arbitrary")),
    )(n1, dproj, sent_array, *shards)


def _share_and_gather(shards, smalls):
    n_h, n_s = len(shards), len(smalls)

    def body(*refs):
        small_ins = refs[n_h:n_h + n_s]
        fs = refs[n_h + n_s:2 * n_h + n_s]
        small_alls = refs[2 * n_h + n_s:2 * n_h + 2 * n_s]
        dsend, drecv, ssend, srecv = refs[2 * n_h + 2 * n_s:]
        x, y, c = lax.axis_index("x"), lax.axis_index("y"), lax.axis_index("c")
        swaps = _swap_halves(fs, dsend, drecv)
        sends, arrivals = [mine for mine, _ in swaps], [theirs for _, theirs in swaps]
        me = 4 * x + 2 * y + c
        for k, (small_in, small_all) in enumerate(zip(small_ins, small_alls)):
            small_all[me] = small_in[...]
            for rel in range(1, N_DEV):
                fx, fy, fc = rel >> 2, (rel >> 1) & 1, rel & 1
                peer = (x + fx - 2 * x * fx, y + fy - 2 * y * fy, c + fc - 2 * c * fc)
                sender = 4 * peer[0] + 2 * peer[1] + peer[2]
                sem = (N_DEV - 1) * k + rel - 1
                sends.append(pltpu.make_async_remote_copy(
                    src_ref=small_in, dst_ref=small_all.at[me], send_sem=ssend.at[sem], recv_sem=srecv.at[sem],
                    device_id=peer, device_id_type=MESH))
                arrivals.append(pltpu.make_async_remote_copy(
                    src_ref=small_in, dst_ref=small_all.at[sender], send_sem=ssend.at[sem], recv_sem=srecv.at[sem],
                    device_id=peer, device_id_type=MESH))
        for cp in sends:
            cp.start()
        for cp in arrivals:
            cp.wait_recv()
        for cp in sends:
            cp.wait_send()

    anyspace = pl.BlockSpec(memory_space=pl.ANY)
    vm = pl.BlockSpec(memory_space=pltpu.VMEM)
    out_shape = [SDS(full.shape, F32) for full in shards] + [SDS((N_DEV,) + sm.shape, F32) for sm in smalls]
    n_all = (N_DEV - 1) * n_s
    outs = pl.pallas_call(
        body, name="share_and_gather", out_shape=out_shape,
        in_specs=[anyspace] * n_h + [vm] * n_s, out_specs=[anyspace] * n_h + [vm] * n_s,
        scratch_shapes=[pltpu.SemaphoreType.DMA((n_h,)), pltpu.SemaphoreType.DMA((n_h,)),
                        pltpu.SemaphoreType.DMA((n_all,)), pltpu.SemaphoreType.DMA((n_all,))],
        input_output_aliases={b: b for b in range(n_h)},
    )(*shards, *smalls)
    return outs[:n_h], outs[n_h:]


def _add_win(where, lo, hi, r, name):
    _, h, cols = lo.shape
    tr = min(h, 256)
    nh = h // tr

    def body(where_ref, lo_ref, hi_ref, r_ref, o_ref):
        acc = jnp.where(where_ref[0] == 0, lo_ref[0], hi_ref[0])
        for k in range(N_DEV - 1):
            acc = acc + r_ref[k].astype(F32)
        o_ref[...] = acc

    own = pl.BlockSpec((1, tr, cols), lambda i, w: (w[1], i, 0))
    return pl.pallas_call(
        body, name=name,
        grid_spec=pltpu.PrefetchScalarGridSpec(
            num_scalar_prefetch=1, grid=(nh,),
            in_specs=[own, own, pl.BlockSpec((N_DEV - 1, tr, cols), lambda i, w: (0, i, 0))],
            out_specs=pl.BlockSpec((tr, cols), lambda i, w: (w[0] * nh + i, 0))),
        out_shape=SDS((2 * h, cols), F32),
        compiler_params=_params(("parallel",)),
    )(where, lo, hi, r)


def _add_devices(where, g, r, name):
    _, rows, cols = g.shape
    h = rows // 2
    tr = min(h, 256)
    nh = h // tr

    def body(where_ref, g_ref, r_ref, o_ref):
        del where_ref
        acc = g_ref[0]
        for k in range(N_DEV - 1):
            acc = acc + r_ref[k].astype(F32)
        o_ref[...] = acc

    return pl.pallas_call(
        body, name=name,
        grid_spec=pltpu.PrefetchScalarGridSpec(
            num_scalar_prefetch=1, grid=(nh,),
            in_specs=[pl.BlockSpec((1, tr, cols), lambda i, w: (w[1], w[0] * nh + i, 0)),
                      pl.BlockSpec((N_DEV - 1, tr, cols), lambda i, w: (0, i, 0))],
            out_specs=pl.BlockSpec((tr, cols), lambda i, w: (w[0] * nh + i, 0))),
        out_shape=SDS((rows, cols), F32),
        compiler_params=_params(("parallel",)),
    )(where, g, r)


def _sum_smalls(gathered):
    n = len(gathered)

    def body(*refs):
        for all_ref, o_ref in zip(refs[:n], refs[n:]):
            acc = all_ref[0]
            for dev in range(1, N_DEV):
                acc = acc + all_ref[dev]
            o_ref[...] = acc

    vm = pl.BlockSpec(memory_space=pltpu.VMEM)
    return pl.pallas_call(
        body, name="sum_smalls", out_shape=[SDS(a.shape[1:], F32) for a in gathered],
        in_specs=[vm] * n, out_specs=[vm] * n,
    )(*gathered)


def _adam_step(g, w, m, v):
    nm = ADAM_B1 * m + (1.0 - ADAM_B1) * g
    nv = ADAM_B2 * v + (1.0 - ADAM_B2) * (g * g)
    m_hat = nm / (1.0 - ADAM_B1 ** ADAM_STEP)
    v_hat = nv / (1.0 - ADAM_B2 ** ADAM_STEP)
    return -ADAM_LR * (m_hat / (jnp.sqrt(v_hat) + ADAM_EPS) + ADAM_WD * w), nm, nv


def _adamw(g, w, m, v, name):
    rows, cols = g.shape
    tr = min(rows, 256)

    def body(g_ref, w_ref, m_ref, v_ref, d_ref, nm_ref, nv_ref):
        d_ref[...], nm_ref[...], nv_ref[...] = _adam_step(g_ref[...], w_ref[...], m_ref[...], v_ref[...])

    spec = pl.BlockSpec((tr, cols), lambda i: (i, 0))
    return pl.pallas_call(
        body, name=name, grid=(rows // tr,), in_specs=[spec] * 4, out_specs=[spec] * 3,
        out_shape=[SDS(g.shape, F32)] * 3, compiler_params=_params(("parallel",)),
    )(g, w, m, v)


def _small_update(chip, tot, tot_rel, wmv):
    names = list(SMALL_PLACES)
    n = len(names)

    def body(chip_ref, tot_ref, quarter_ref, rel_ref, *refs):
        del chip_ref
        ins, outs = refs[:3 * n], refs[3 * n:]
        for i, nm in enumerate(names):
            source, row, (rows, cols) = SMALL_PLACES[nm]
            g = {"rows": tot_ref, "quarter": quarter_ref, "rel": rel_ref}[source][row:row + rows, 0:cols]
            outs[4 * i][...] = g
            outs[4 * i + 1][...], outs[4 * i + 2][...], outs[4 * i + 3][...] = _adam_step(
                g, ins[3 * i][...], ins[3 * i + 1][...], ins[3 * i + 2][...])

    whole = lambda shape: pl.BlockSpec(shape, lambda i, c: (0,) * len(shape))
    shapes = [SMALL_PLACES[nm][2] for nm in names]
    outs = pl.pallas_call(
        body, name="small_update",
        grid_spec=pltpu.PrefetchScalarGridSpec(
            num_scalar_prefetch=1, grid=(1,),
            in_specs=[whole(tot.shape), pl.BlockSpec((tot.shape[0], D // 4), lambda i, c: (0, c[0])),
                      whole(tot_rel.shape)] + [whole(shp) for shp in shapes for _ in range(3)],
            out_specs=[whole(shp) for shp in shapes for _ in range(4)]),
        out_shape=[SDS(shp, F32) for shp in shapes for _ in range(4)],
    )(chip, tot, tot, tot_rel, *[a for nm in names for a in wmv[nm]])
    return {nm: tuple(outs[4 * i:4 * i + 4]) for i, nm in enumerate(names)}


def _pad_rows(a, rows):
    return jnp.concatenate([a, jnp.zeros((rows - a.shape[0], a.shape[1]), a.dtype)], axis=0)


def _pad_cols(a, cols):
    return jnp.concatenate([a, jnp.zeros((a.shape[0], cols - a.shape[1]), a.dtype)], axis=1)


def kernel(x, a_pre_norm, a_w_in, a_conv_w, a_w_out, a_post_norm, kv_norm, w_kv, rel_bias, b_pre_norm, b_w_in, b_sinks, b_w_out, b_post_norm, loss_target, m_a_pre_norm, m_a_w_in, m_a_conv_w, m_a_w_out, m_a_post_norm, m_kv_norm, m_w_kv, m_rel_bias, m_b_pre_norm, m_b_w_in, m_b_sinks, m_b_w_out, m_b_post_norm, v_a_pre_norm, v_a_w_in, v_a_conv_w, v_a_w_out, v_a_post_norm, v_kv_norm, v_w_kv, v_rel_bias, v_b_pre_norm, v_b_w_in, v_b_sinks, v_b_w_out, v_b_post_norm):
    seq = x.shape[1]
    xs = x.reshape(seq, D)
    tgt = loss_target.reshape(seq, D)
    chip = 2 * lax.axis_index("x") + lax.axis_index("y")
    core = lax.axis_index("c")
    tm = _tile(seq, 512)
    tmw = _tile(seq, 1024)

    shards = [a_w_in[0], a_w_out[0], w_kv, b_w_in[0], b_w_out[0]]
    small_w = _pad_rows(jnp.concatenate([a_pre_norm, a_conv_w[0], a_post_norm], axis=0), 8)
    *own_only, small_g = _prepare_weights(shards, small_w)
    where = jnp.stack([core, chip]).astype(jnp.int32)
    small_full = small_g.transpose(1, 0, 2).reshape(8, D)
    g_apre, conv_w, g_apost = small_full[0:1], _pad_rows(small_full[1:4], 8), small_full[4:5]
    g_kv = kv_norm.reshape(1, D)

    proj, n1, (win_g, wouta_g, wkv_g, wbin_g, woutb_g) = _a_in(where[1:2], xs, g_apre, own_only, tmw)
    wouta = wouta_g.reshape(D, D)
    wkv = wkv_g.reshape(D, 2 * KV_W)
    woutb = woutb_g.reshape(D, D)
    ya, oa, h1, conv = _a_mix(proj, xs, conv_w, wouta, g_apost, tm)
    kv, q, zb = _b_in(h1, g_kv, b_pre_norm, wkv, wbin_g, tmw)
    tab = _bias_table(rel_bias, b_sinks.reshape(N_HEADS))
    att, stats = _attn_fwd(q, kv, tab)
    dh2, dqz, datt, loss_acc, dg_bpost, dw_outb, dw_outb16 = _mid(att, zb, h1, tgt, woutb, b_post_norm, tm)

    dqz, dkv, dtab = _attn_bwd(q, kv, datt, stats, tab, dqz)
    dh1, doa, dg_b, dw_bin, dw_kv, dw_bin16, dw_kv16 = _b_bwd(dqz, dkv, h1, dh2, oa, wbin_g, wkv, g_kv, b_pre_norm,
                                                              g_apost, tm)
    by_chip = lambda a, cols: a.reshape(N_CHIPS, D // 4, cols)
    grads1 = [by_chip(dw_kv, 2 * KV_W), dw_bin, by_chip(dw_outb, D)]
    sent1 = [by_chip(dw_kv16, 2 * KV_W), dw_bin16, by_chip(dw_outb16, D)]
    names1 = ["w_kv", "b_w_in", "b_w_out"]
    dproj, dconv_w, dw_outa, dw_outa16, from_devices1 = _a_bwd(doa, ya, conv, proj, conv_w, wouta, tm, sent1)
    shards1 = [_add_devices(where, g, r, "add_devices_" + nm) for g, r, nm in zip(grads1, from_devices1, names1)]
    tmw2 = _tile(seq, 4096)
    win_lo, win_lo16, outa_got, g_wkv, g_wbin, g_woutb = _dw_in_half(
        n1, dproj, 0, tmw2, "dw_a_in_lo", to_devices=by_chip(dw_outa16, D), shards=shards1)
    win_hi, win_hi16, win_got = _dw_in_half(n1, dproj, 1, tmw2, "dw_a_in_hi", to_owners=win_lo16)
    nt = seq // tmw
    dn_first, win_got = _a_in_bwd_matmul(dproj, win_g, tmw, max(nt - max(nt // 4, 1), 1), win_hi16, win_got)
    grad_x, dg_apre = _a_in_bwd(dn_first, dproj, xs, dh1, win_g, g_apre, tm)
    shards2 = [_add_win(where, win_lo, win_hi, win_got, "add_devices_a_w_in"),
               _add_devices(where, by_chip(dw_outa, D), outa_got, "add_devices_a_w_out")]
    drel, dsink = _bias_fold(dtab)

    smalls = jnp.concatenate([
        dg_apre[0:1], dg_b[2:3], dg_b[0:1], dg_b[1:2], dg_bpost[0:1], _pad_cols(dsink[0:1], D),
        _pad_cols(loss_acc[0:1], D), jnp.zeros((1, D), F32), dconv_w], axis=0)
    assert smalls.shape == (SMALL_ROWS, D)
    (g_win, g_wouta), gathered = _share_and_gather(shards2, (smalls, drel))
    tot, tot_rel = _sum_smalls(gathered)

    big = {}
    for nm, g, w, m, v in [("a_w_in", g_win, a_w_in, m_a_w_in, v_a_w_in), ("a_w_out", g_wouta, a_w_out, m_a_w_out, v_a_w_out),
                           ("w_kv", g_wkv, w_kv, m_w_kv, v_w_kv), ("b_w_in", g_wbin, b_w_in, m_b_w_in, v_b_w_in),
                           ("b_w_out", g_woutb, b_w_out, m_b_w_out, v_b_w_out)]:
        shp = w.shape
        two = (shp[-2], shp[-1])
        d, nm_, nv_ = _adamw(g, w.reshape(two), m.reshape(two), v.reshape(two), "adamw_" + nm)
        big[nm] = (g.reshape(shp), d.reshape(shp), nm_.reshape(shp), nv_.reshape(shp))

    given = {"a_pre_norm": (a_pre_norm, m_a_pre_norm, v_a_pre_norm), "a_conv_w": (a_conv_w, m_a_conv_w, v_a_conv_w),
             "a_post_norm": (a_post_norm, m_a_post_norm, v_a_post_norm), "kv_norm": (kv_norm, m_kv_norm, v_kv_norm),
             "rel_bias": (rel_bias, m_rel_bias, v_rel_bias), "b_pre_norm": (b_pre_norm, m_b_pre_norm, v_b_pre_norm),
             "b_sinks": (b_sinks, m_b_sinks, v_b_sinks), "b_post_norm": (b_post_norm, m_b_post_norm, v_b_post_norm)}
    small = _small_update(where[1:2], tot, tot_rel, {nm: tuple(a.reshape(SMALL_PLACES[nm][2]) for a in wmv)
                                            for nm, wmv in given.items()})
    order = ["a_pre_norm", "a_w_in", "a_conv_w", "a_w_out", "a_post_norm", "kv_norm", "w_kv", "rel_bias",
             "b_pre_norm", "b_w_in", "b_sinks", "b_w_out", "b_post_norm"]
    outs = []
    for which in range(4):
        for nm in order:
            outs.append(big[nm][which] if nm in big else small[nm][which].reshape(given[nm][0].shape))
    loss = 0.5 * tot[LOSS_ROW, 0]
    return (loss, grad_x.reshape(x.shape), *outs)
```

```python
import math

import jax
import jax.numpy as jnp
from jax import lax
from jax.experimental import pallas as pl
from jax.experimental.pallas import tpu as pltpu

F32 = jnp.float32
BF16 = jnp.bfloat16
MESH = pl.DeviceIdType.MESH
SDS = jax.ShapeDtypeStruct

D = 1024
HEAD_DIM = 64
N_HEADS = 16
N_KV = 2
GROUP = 8
KV_W = 128
BLK = 128
N_BUCKETS = 32
MAX_EXACT = 16
MAX_DISTANCE = 128
EPS = 1e-6
NEG_INF = -1e30
Q_SCALE = HEAD_DIM ** -0.5

ADAM_LR = 0.001
ADAM_B1 = 0.9
ADAM_B2 = 0.999
ADAM_EPS = 1e-08
ADAM_WD = 0.01
ADAM_STEP = 10

N_CHIPS = 4
N_DEV = 8
BIN_COLS = 2 * D // N_CHIPS
VMEM_LIMIT = 56 * 1024 * 1024
SMALL_ROWS = 16
LOSS_ROW = 6
SMALL_PLACES = {
    "a_pre_norm": ("quarter", 0, (1, D // 4)), "a_conv_w": ("quarter", 8, (3, D // 4)),
    "a_post_norm": ("quarter", 1, (1, D // 4)), "kv_norm": ("rows", 2, (1, D)),
    "rel_bias": ("rel", 0, (N_BUCKETS, N_HEADS)), "b_pre_norm": ("rows", 3, (1, D)),
    "b_sinks": ("rows", 5, (1, N_HEADS)), "b_post_norm": ("rows", 4, (1, D)),
}


def _bucket_thresholds():
    def bucket(d):
        big = MAX_EXACT + int(math.log(d / MAX_EXACT) / math.log(MAX_DISTANCE / MAX_EXACT)
                              * (N_BUCKETS - MAX_EXACT))
        return d if d < MAX_EXACT else min(big, N_BUCKETS - 1)
    out = []
    for b in range(MAX_EXACT + 1, N_BUCKETS):
        out.append(min(d for d in range(MAX_EXACT, MAX_DISTANCE) if bucket(d) >= b))
    return tuple(out)


BUCKET_THRESHOLDS = _bucket_thresholds()


def _params(semantics=None, vmem=VMEM_LIMIT):
    return pltpu.CompilerParams(dimension_semantics=semantics, vmem_limit_bytes=vmem)


def _tile(n, pref):
    return pref if n >= 2 * pref else max(n // 2, 8)


def _rms_scale(v):
    return lax.rsqrt(jnp.mean(v * v, axis=-1, keepdims=True) + EPS)


def _nt(a, b):
    return lax.dot_general(a, b, (((1,), (1,)), ((), ())), preferred_element_type=F32)


def _tn(a, b):
    return lax.dot_general(a, b, (((0,), (0,)), ((), ())), preferred_element_type=F32)


def _nn(a, b):
    return jnp.dot(a, b, preferred_element_type=F32)


def _silu_parts(z):
    sg = jax.nn.sigmoid(z)
    return sg, z * sg


def _dsilu(z, sg):
    return sg * (1.0 + z * (1.0 - sg))


def _write_gradient(acc, out32, out16, stage, sem):
    whole = pltpu.make_async_copy(acc, out32, sem)
    whole.start()
    rows = stage.shape[0]
    for k in range(acc.shape[0] // rows):
        stage[...] = acc[rows * k:rows * (k + 1), :].astype(BF16)
        pltpu.sync_copy(stage, out16.at[pl.ds(rows * k, rows)])
    whole.wait()


def _acc_row(ref, row, val):
    ref[row:row + 1, :] += val


def _gather_copies(outs, splits, ici_send, ici_recv, d2d_send, d2d_recv):
    x, y, c = lax.axis_index("x"), lax.axis_index("y"), lax.axis_index("c")
    k = 2 * x + y
    sibling = (x, y, 1 - c)

    def part(o_ref, chip, core, split):
        if not split:
            return o_ref.at[chip]
        h = o_ref.shape[1] // 2
        return o_ref.at[chip, pl.ds(pl.multiple_of(core * h, 16), h)]

    def remote(ref, a, j, sems, to):
        return pltpu.make_async_remote_copy(src_ref=ref, dst_ref=ref, send_sem=sems[0].at[3 * a + j],
                                            recv_sem=sems[1].at[3 * a + j], device_id=to, device_id_type=MESH)

    copies = []
    for a, (o_ref, split) in enumerate(zip(outs, splits)):
        for j, (px, py) in enumerate([(x, 1 - y), (1 - x, y), (1 - x, 1 - y)]):
            kj = 2 * px + py
            ici, d2d = (ici_send, ici_recv), (d2d_send, d2d_recv)
            copies.append((remote(part(o_ref, k, c, split), a, j, ici, (px, py, c)),
                           remote(part(o_ref, kj, c, split), a, j, ici, (px, py, c)),
                           remote(part(o_ref, kj, c, split), a, j, d2d, sibling) if split else None,
                           remote(part(o_ref, kj, 1 - c, split), a, j, d2d, sibling) if split else None))
    return copies


def _gather_sems(n):
    return [pltpu.SemaphoreType.DMA((3 * n,)) for _ in range(4)]


def _prepare_weights(shards, small):
    n = len(shards)

    def body(*refs):
        ins, small_in = refs[:n], refs[n]
        outs, small_out = refs[n + 1:2 * n + 1], refs[2 * n + 1]
        stages, put_sem = refs[2 * n + 2:3 * n + 2], refs[3 * n + 2]
        sems = refs[3 * n + 3:]
        k = 2 * lax.axis_index("x") + lax.axis_index("y")
        puts = []
        for a, (i_ref, stage, o_ref) in enumerate(zip(ins, stages, outs)):
            stage[...] = i_ref[...].astype(BF16)
            puts.append(pltpu.make_async_copy(stage, o_ref.at[k], put_sem.at[a]))
            puts[-1].start()
        small_out[k] = small_in[...]
        copies = _gather_copies([small_out], [False], *sems)
        for send, _, _, _ in copies:
            send.start()
        for _, arrival, _, _ in copies:
            arrival.wait_recv()
        for send, _, _, _ in copies:
            send.wait_send()
        for put in puts:
            put.wait()

    vm = pl.BlockSpec(memory_space=pltpu.VMEM)
    anyspace = pl.BlockSpec(memory_space=pl.ANY)
    out_shape = [SDS((N_CHIPS,) + s.shape, BF16) for s in shards] + [SDS((N_CHIPS,) + small.shape, F32)]
    return pl.pallas_call(
        body, name="prepare_weights", out_shape=out_shape,
        in_specs=[vm] * (n + 1), out_specs=[anyspace] * n + [vm],
        scratch_shapes=[pltpu.VMEM(s.shape, BF16) for s in shards] + [pltpu.SemaphoreType.DMA((n,))] + _gather_sems(1),
        compiler_params=pltpu.CompilerParams(vmem_limit_bytes=VMEM_LIMIT),
    )(*shards, small)


def _a_in(chip, x, g_pre, weights, tm):
    s = x.shape[0]
    nt = s // tm
    n = len(weights)

    def body(chip_ref, x_ref, g_ref, *refs):
        proj_ref, n1_ref = refs[n:n + 2]
        gathered = refs[n + 2:2 * n + 2]
        wbuf, n1_all, fetch_sem = refs[2 * n + 2:2 * n + 5]
        sems = refs[2 * n + 5:]
        jj, i = pl.program_id(0), pl.program_id(1)
        copies = _gather_copies(gathered, [True] * n, *sems)

        def fetch(rel):
            slot = jnp.bitwise_xor(chip_ref[0], rel)
            return pltpu.make_async_copy(gathered[0].at[slot], wbuf.at[rel % 2], fetch_sem.at[rel % 2])

        @pl.when((jj == 0) & (i == 0))
        def _():
            fetch(0).start()
            copies[0][0].start()
            copies[1][0].start()
            fetch(0).wait()

        for rel in (1, 2, 3):
            @pl.when((jj == rel) & (i == 0))
            def _():
                fetch(rel).wait()

        @pl.when(jj == 0)
        def _():
            xv = x_ref[...]
            n1 = (xv * _rms_scale(xv) * g_ref[...]).astype(BF16)
            n1_ref[...] = n1
            n1_all[i] = n1
        proj_ref[...] = _nn(n1_all[i], wbuf[jj % 2]).astype(BF16)

        for rel in (1, 2, 3):
            @pl.when((jj == rel - 1) & (i == max(nt - 3, 0)))
            def _():
                _, arrival, forward, _ = copies[rel - 1]
                arrival.wait_recv()
                forward.start()
                if rel == 1:
                    for send, _, _, _ in copies[2:]:
                        send.start()

            @pl.when((jj == rel - 1) & (i == max(nt - 2, 0)))
            def _():
                copies[rel - 1][3].wait_recv()
                fetch(rel).start()

        @pl.when((jj == 3) & (i == max(nt - 2, 0)))
        def _():
            for _, arrival, forward, _ in copies[3:]:
                arrival.wait_recv()
                forward.start()

        @pl.when((jj == 3) & (i == nt - 1))
        def _():
            for _, _, _, forwarded in copies[3:]:
                forwarded.wait_recv()
            for send, _, forward, _ in copies:
                forward.wait_send()
                send.wait_send()

    anyspace = pl.BlockSpec(memory_space=pl.ANY)
    proj, n1, *gathered = pl.pallas_call(
        body, name="a_in",
        grid_spec=pltpu.PrefetchScalarGridSpec(
            num_scalar_prefetch=1, grid=(4, nt),
            in_specs=[pl.BlockSpec((tm, D), lambda jj, i, c: (jnp.where(jj == 0, i, nt - 1), 0)),
                      pl.BlockSpec((1, D), lambda jj, i, c: (0, 0))] + [anyspace] * n,
            out_specs=[pl.BlockSpec((tm, D), lambda jj, i, c: (i, jnp.bitwise_xor(c[0], jj))),
                       pl.BlockSpec((tm, D), lambda jj, i, c: (jnp.where(jj == 0, i, nt - 1), 0))] + [anyspace] * n,
            scratch_shapes=[pltpu.VMEM((2, D, D), BF16), pltpu.VMEM((nt, tm, D), BF16),
                            pltpu.SemaphoreType.DMA((2,))] + _gather_sems(n)),
        out_shape=[SDS((s, 4 * D), BF16), SDS((s, D), BF16)] + [SDS(w.shape, w.dtype) for w in weights],
        input_output_aliases={3 + a: 2 + a for a in range(n)},
        compiler_params=_params(("arbitrary", "arbitrary")),
    )(chip, x, g_pre, *weights)
    return proj, n1, gathered


def _shift_rows(v, last, second_last, rows):
    v1 = jnp.where(rows >= 1, pltpu.roll(v, 1, 0), last)
    v2 = jnp.where(rows >= 2, pltpu.roll(v, 2, 0), jnp.where(rows == 1, last, second_last))
    return v1, v2


def _a_mix(proj, x, conv_w, w_out, g_post, tm):
    s = x.shape[0]

    def body(proj_ref, x_ref, cw_ref, w_ref, g_ref, ya_ref, oa_ref, h1_ref, conv_ref, carry):
        @pl.when(pl.program_id(0) == 0)
        def _():
            carry[...] = jnp.zeros_like(carry)
        v = proj_ref[:, D:2 * D].astype(F32) * proj_ref[:, 2 * D:3 * D].astype(F32)
        rows = lax.broadcasted_iota(jnp.int32, (tm, D), 0)
        before = carry[...]
        v1, v2 = _shift_rows(v, before[7:8, :], before[6:7, :], rows)
        carry[...] = v[tm - 8:tm, :]
        conv = cw_ref[0:1, :] * v2 + cw_ref[1:2, :] * v1 + cw_ref[2:3, :] * v
        conv_ref[...] = conv.astype(BF16)
        _, sz = _silu_parts(proj_ref[:, 3 * D:4 * D].astype(F32))
        ya = (proj_ref[:, 0:D].astype(F32) * conv * sz).astype(BF16)
        ya_ref[...] = ya
        oa = _nn(ya, w_ref[...])
        oa_ref[...] = oa.astype(BF16)
        h1_ref[...] = x_ref[...] + oa * _rms_scale(oa) * g_ref[...]

    row = lambda i: (i, 0)
    fix = lambda i: (0, 0)
    return pl.pallas_call(
        body, name="a_mix", grid=(s // tm,),
        in_specs=[pl.BlockSpec((tm, 4 * D), row), pl.BlockSpec((tm, D), row), pl.BlockSpec((8, D), fix),
                  pl.BlockSpec((D, D), fix), pl.BlockSpec((1, D), fix)],
        out_specs=[pl.BlockSpec((tm, D), row)] * 4,
        out_shape=[SDS((s, D), BF16), SDS((s, D), BF16), SDS((s, D), F32), SDS((s, D), BF16)],
        scratch_shapes=[pltpu.VMEM((8, D), F32)],
        compiler_params=_params(("arbitrary",)),
    )(proj, x, conv_w, w_out, g_post)


def _b_in(h1, g_kv, g_pre, w_kv, wbin_g, tm):
    s = h1.shape[0]

    def body(h_ref, gk_ref, gb_ref, wkv_ref, wb_ref, kv_ref, q_ref, z_ref):
        h = h_ref[...]
        hh = h * _rms_scale(h)
        nk = (hh * gk_ref[...]).astype(BF16)
        nb = (hh * gb_ref[...]).astype(BF16)
        kv_ref[...] = _nn(nk, wkv_ref[...]).astype(BF16)
        for j in range(2):
            q_ref[:, BIN_COLS * j:BIN_COLS * (j + 1)] = (_nn(nb, wb_ref[j]) * Q_SCALE).astype(BF16)
            z_ref[:, BIN_COLS * j:BIN_COLS * (j + 1)] = _nn(nb, wb_ref[2 + j]).astype(BF16)

    row = lambda i: (i, 0)
    fix = lambda i: (0, 0)
    return pl.pallas_call(
        body, name="b_in", grid=(s // tm,),
        in_specs=[pl.BlockSpec((tm, D), row), pl.BlockSpec((1, D), fix), pl.BlockSpec((1, D), fix),
                  pl.BlockSpec((D, 2 * KV_W), fix), pl.BlockSpec((N_CHIPS, D, BIN_COLS), lambda i: (0, 0, 0))],
        out_specs=[pl.BlockSpec((tm, 2 * KV_W), row), pl.BlockSpec((tm, D), row), pl.BlockSpec((tm, D), row)],
        out_shape=[SDS((s, 2 * KV_W), BF16), SDS((s, D), BF16), SDS((s, D), BF16)],
        compiler_params=_params(("parallel",)),
    )(h1, g_kv, g_pre, w_kv, wbin_g)


def _band_buckets():
    q = lax.broadcasted_iota(jnp.int32, (BLK, 2 * BLK), 0)
    k = lax.broadcasted_iota(jnp.int32, (BLK, 2 * BLK), 1)
    dist = q + BLK - k
    bucket = jnp.where(dist < MAX_EXACT, dist, MAX_EXACT)
    for t in BUCKET_THRESHOLDS:
        bucket = bucket + jnp.where(dist >= t, 1, 0)
    in_window = (dist >= 0) & (dist < BLK)
    return jnp.where(in_window, bucket, -1)


def _head_place(h):
    kh, j, e = h // GROUP, (h % GROUP) // 2, h % 2
    return kh, slice(BLK * j, BLK * (j + 1)), slice(2 * BLK * e, 2 * BLK * (e + 1))


def _bias_table(rel_bias, sinks):
    def body(rb_ref, sink_ref, tab_ref):
        bucket = _band_buckets()
        col = lax.broadcasted_iota(jnp.int32, (BLK, 2 * BLK), 1)
        for h in range(N_HEADS):
            acc = jnp.where(bucket < 0, NEG_INF, 0.0).astype(F32)
            for b in range(N_BUCKETS):
                acc = jnp.where(bucket == b, rb_ref[b, h], acc)
            acc = jnp.where(col == 0, sink_ref[h], acc)
            kh, rows, cols = _head_place(h)
            tab_ref[1, kh, rows, cols] = acc
            tab_ref[0, kh, rows, cols] = jnp.where((col > 0) & (col < BLK), NEG_INF, acc)

    return pl.pallas_call(
        body, name="bias_table", out_shape=SDS((2, N_KV, 4 * BLK, 4 * BLK), F32),
        in_specs=[pl.BlockSpec(memory_space=pltpu.SMEM), pl.BlockSpec(memory_space=pltpu.SMEM)],
        out_specs=pl.BlockSpec(memory_space=pltpu.VMEM),
    )(rel_bias, sinks)


def _bias_fold(dtab):
    def body(dtab_ref, out_ref, dsink_ref):
        along = lax.broadcasted_iota(jnp.int32, (1, BLK), 1)
        dist = jnp.where(along == 0, 0, BLK - along)
        bucket = jnp.where(dist < MAX_EXACT, dist, MAX_EXACT)
        for t in BUCKET_THRESHOLDS:
            bucket = bucket + jnp.where(dist >= t, 1, 0)
        query = lax.broadcasted_iota(jnp.int32, (BLK, BLK), 0)
        col = lax.broadcasted_iota(jnp.int32, (BLK, BLK), 1)
        row = lax.broadcasted_iota(jnp.int32, (N_BUCKETS, 128), 0)
        lane = lax.broadcasted_iota(jnp.int32, (N_BUCKETS, 128), 1)
        row8 = lax.broadcasted_iota(jnp.int32, (8, 128), 0)
        lane8 = lax.broadcasted_iota(jnp.int32, (8, 128), 1)
        acc = jnp.zeros((N_BUCKETS, 128), F32)
        dsink = jnp.zeros((8, 128), F32)
        for h in range(N_HEADS):
            kh, rows, cols = _head_place(h)
            dt = dtab_ref[kh, rows, cols]
            band = jnp.where(col == 0, 0.0, dt[:, 0:BLK]) + dt[:, BLK:2 * BLK]
            for digit in range(BLK.bit_length() - 1):
                band = jnp.where((query >> digit) & 1 == 1, pltpu.roll(band, BLK - (1 << digit), 1), band)
            by_dist = jnp.sum(band, axis=0, keepdims=True)
            for b in range(N_BUCKETS):
                val = jnp.sum(jnp.where(bucket == b, by_dist, 0.0))
                acc = acc + jnp.where((row == b) & (lane == h), val, 0.0)
            dsink = dsink + jnp.where((row8 == 0) & (lane8 == h), jnp.sum(dt[:, 0:1]), 0.0)
        out_ref[...] = acc
        dsink_ref[...] = dsink

    vm = pl.BlockSpec(memory_space=pltpu.VMEM)
    return pl.pallas_call(
        body, name="bias_fold", out_shape=[SDS((N_BUCKETS, 128), F32), SDS((8, 128), F32)],
        in_specs=[vm], out_specs=[vm, vm],
    )(dtab)


def _pair_operands(prev, cur):
    t = jnp.concatenate([prev, cur], axis=0).astype(F32)
    t = jnp.where(lax.broadcasted_iota(jnp.int32, t.shape, 0) == 0, 0.0, t)
    tr = pltpu.roll(t, HEAD_DIM, 1)
    lo = lax.broadcasted_iota(jnp.int32, t.shape, 1) < HEAD_DIM
    zero = jnp.zeros_like(t)
    head0 = jnp.concatenate([jnp.where(lo, t, zero), jnp.where(lo, zero, tr)], axis=0).astype(BF16)
    head1 = jnp.concatenate([jnp.where(lo, tr, zero), jnp.where(lo, zero, t)], axis=0).astype(BF16)
    return head0, head1


def _pair_fold(d0, d1):
    lo = lax.broadcasted_iota(jnp.int32, (2 * BLK, KV_W), 1) < HEAD_DIM
    zero = jnp.zeros((2 * BLK, KV_W), F32)
    g0 = jnp.where(lo, d0[0:256], zero) + pltpu.roll(jnp.where(lo, zero, d0[256:512]), HEAD_DIM, 1)
    g1 = pltpu.roll(jnp.where(lo, d1[0:256], zero), HEAD_DIM, 1) + jnp.where(lo, zero, d1[256:512])
    return jnp.where(lax.broadcasted_iota(jnp.int32, (2 * BLK, KV_W), 0) == 0, 0.0, g0 + g1)


def _stack_pairs(ref, kh):
    return jnp.concatenate([ref[:, 128 * (4 * kh + j):128 * (4 * kh + j + 1)] for j in range(4)], axis=0)


def _table_spec():
    return pl.BlockSpec((1, N_KV, 4 * BLK, 4 * BLK), lambda n: (jnp.minimum(n, 1), 0, 0, 0))


def _attn_fwd(q, kv, tab):
    s = q.shape[0]

    def body(q_ref, kp_ref, kc_ref, vp_ref, vc_ref, tab_ref, att_ref, stats_ref):
        k2 = _pair_operands(kp_ref[...], kc_ref[...])
        v2 = _pair_operands(vp_ref[...], vc_ref[...])
        lane = lax.broadcasted_iota(jnp.int32, (BLK, 128), 1)
        stats = jnp.zeros((BLK, 128), F32)
        for kh in range(N_KV):
            sc = _nt(_stack_pairs(q_ref, kh), k2[kh])
            ps = []
            for e in range(2):
                lg = sc[:, 256 * e:256 * (e + 1)] + tab_ref[0, kh, :, 256 * e:256 * (e + 1)]
                m = jnp.max(lg, axis=-1, keepdims=True)
                ex = jnp.exp(lg - m)
                den = jnp.sum(ex, axis=-1, keepdims=True)
                ps.append(ex * (1.0 / den))
                lse = m + jnp.log(den)
                for j in range(4):
                    stats = jnp.where(lane == GROUP * kh + 2 * j + e, lse[BLK * j:BLK * (j + 1)], stats)
            out = _nn(jnp.concatenate(ps, axis=1).astype(BF16), v2[kh])
            for j in range(4):
                att_ref[:, 128 * (4 * kh + j):128 * (4 * kh + j + 1)] = out[BLK * j:BLK * (j + 1)].astype(BF16)
        stats_ref[...] = stats

    cur = lambda n: (n, 0)
    prev = lambda n: (jnp.maximum(n - 1, 0), 0)
    return pl.pallas_call(
        body, name="attn_fwd", grid=(s // BLK,),
        in_specs=[pl.BlockSpec((BLK, D), cur),
                  pl.BlockSpec((BLK, KV_W), prev), pl.BlockSpec((BLK, KV_W), cur),
                  pl.BlockSpec((BLK, KV_W), lambda n: (jnp.maximum(n - 1, 0), 1)),
                  pl.BlockSpec((BLK, KV_W), lambda n: (n, 1)), _table_spec()],
        out_specs=[pl.BlockSpec((BLK, D), cur), pl.BlockSpec((BLK, 128), cur)],
        out_shape=[SDS((s, D), BF16), SDS((s, 128), F32)],
        compiler_params=_params(("parallel",)),
    )(q, kv, kv, kv, kv, tab)


def _mid(att, zb, h1, tgt, w_out, g_post, tm):
    s = att.shape[0]
    nt = s // tm

    def body(att_ref, z_ref, h1_ref, t_ref, w_ref, g_ref,
             dh_ref, dqz_ref, datt_ref, loss_ref, dg_ref, dw_ref, dw16_ref, dw_acc, stage, put_sem):
        @pl.when(pl.program_id(0) == 0)
        def _():
            loss_ref[...] = jnp.zeros_like(loss_ref)
            dg_ref[...] = jnp.zeros_like(dg_ref)
            dw_acc[...] = jnp.zeros_like(dw_acc)
        att = att_ref[...].astype(F32)
        z = z_ref[...].astype(F32)
        sg, sz = _silu_parts(z)
        ob = (att * sz).astype(BF16)
        y2 = _nn(ob, w_ref[...])
        r2 = _rms_scale(y2)
        yh = y2 * r2
        g = g_ref[...]
        err = (h1_ref[...] + yh * g) - t_ref[...]
        loss_ref[...] += jnp.sum(jnp.sum(err * err, axis=-1, keepdims=True) / D)
        dh = err / D
        dh_ref[...] = dh
        _acc_row(dg_ref, 0, jnp.sum(dh * yh, axis=0, keepdims=True))
        dyh = dh * g
        dy = (r2 * (dyh - yh * jnp.mean(dyh * yh, axis=-1, keepdims=True))).astype(BF16)
        dw_acc[...] += _tn(ob, dy)
        dob = _nt(dy, w_ref[...])
        datt_ref[...] = (dob * sz).astype(BF16)
        dqz_ref[...] = (dob * att * _dsilu(z, sg)).astype(BF16)

        @pl.when(pl.program_id(0) == nt - 1)
        def _():
            _write_gradient(dw_acc, dw_ref, dw16_ref, stage, put_sem)

    row = lambda i: (i, 0)
    fix = lambda i: (0, 0)
    anyspace = pl.BlockSpec(memory_space=pl.ANY)
    return pl.pallas_call(
        body, name="mid", grid=(nt,),
        in_specs=[pl.BlockSpec((tm, D), row)] * 4 + [pl.BlockSpec((D, D), fix), pl.BlockSpec((1, D), fix)],
        out_specs=[pl.BlockSpec((tm, D), row), pl.BlockSpec((tm, D), lambda i: (i, 1)), pl.BlockSpec((tm, D), row),
                   pl.BlockSpec((8, 128), fix), pl.BlockSpec((8, D), fix), anyspace, anyspace],
        out_shape=[SDS((s, D), F32), SDS((s, 2 * D), BF16), SDS((s, D), BF16), SDS((8, 128), F32),
                   SDS((8, D), F32), SDS((D, D), F32), SDS((D, D), BF16)],
        scratch_shapes=[pltpu.VMEM((D, D), F32), pltpu.VMEM((D // 4, D), BF16), pltpu.SemaphoreType.DMA],
        compiler_params=_params(("arbitrary",)),
    )(att, zb, h1, tgt, w_out, g_post)


def _attn_bwd(q, kv, datt, stats, tab, dqz):
    s = q.shape[0]
    nb = s // BLK

    def body(q_ref, kp_ref, kc_ref, vp_ref, vc_ref, da_ref, st_ref, tab_ref, dqz_in,
             dq_ref, dkv_ref, dtab_ref, dk_carry, dv_carry):
        del dqz_in
        n = pl.program_id(0)

        @pl.when(n == 0)
        def _():
            dtab_ref[...] = jnp.zeros_like(dtab_ref)
            dk_carry[...] = jnp.zeros_like(dk_carry)
            dv_carry[...] = jnp.zeros_like(dv_carry)

        @pl.when(n < nb)
        def _():
            k2 = _pair_operands(kp_ref[...], kc_ref[...])
            v2 = _pair_operands(vp_ref[...], vc_ref[...])
            lane = lax.broadcasted_iota(jnp.int32, (BLK, 128), 1)
            stats = st_ref[...]
            dk2, dv2 = [], []
            for kh in range(N_KV):
                qs = _stack_pairs(q_ref, kh)
                das = _stack_pairs(da_ref, kh)
                sc = _nt(qs, k2[kh])
                dp = _nt(das, v2[kh])
                ps, dss = [], []
                for e in range(2):
                    heads = [GROUP * kh + 2 * j + e for j in range(4)]
                    lse = jnp.concatenate([jnp.sum(jnp.where(lane == h, stats, 0.0), axis=-1, keepdims=True)
                                           for h in heads], axis=0)
                    cols = slice(256 * e, 256 * (e + 1))
                    p = jnp.exp(sc[:, cols] + tab_ref[0, kh, :, cols] - lse)
                    delta = jnp.sum(p * dp[:, cols], axis=-1, keepdims=True)
                    ds = p * (dp[:, cols] - delta)
                    dtab_ref[kh, :, cols] += ds
                    ps.append(p)
                    dss.append(ds)
                p2 = jnp.concatenate(ps, axis=1).astype(BF16)
                ds2 = jnp.concatenate(dss, axis=1).astype(BF16)
                dq = _nn(ds2, k2[kh]) * Q_SCALE
                for j in range(4):
                    dq_ref[:, 128 * (4 * kh + j):128 * (4 * kh + j + 1)] = dq[BLK * j:BLK * (j + 1)].astype(BF16)
                dk2.append(_tn(ds2, qs))
                dv2.append(_tn(p2, das))
            dkk = _pair_fold(dk2[0], dk2[1])
            dvv = _pair_fold(dv2[0], dv2[1])
            dkv_ref[:, 0:KV_W] = (dk_carry[...] + dkk[0:BLK]).astype(BF16)
            dkv_ref[:, KV_W:2 * KV_W] = (dv_carry[...] + dvv[0:BLK]).astype(BF16)
            dk_carry[...] = dkk[BLK:2 * BLK]
            dv_carry[...] = dvv[BLK:2 * BLK]

        @pl.when(n == nb)
        def _():
            dkv_ref[:, 0:KV_W] = dk_carry[...].astype(BF16)
            dkv_ref[:, KV_W:2 * KV_W] = dv_carry[...].astype(BF16)

    cur = lambda n: (jnp.minimum(n, nb - 1), 0)
    prev = lambda n: (jnp.clip(n - 1, 0, nb - 1), 0)
    return pl.pallas_call(
        body, name="attn_bwd", grid=(nb + 1,),
        in_specs=[pl.BlockSpec((BLK, D), cur),
                  pl.BlockSpec((BLK, KV_W), prev), pl.BlockSpec((BLK, KV_W), cur),
                  pl.BlockSpec((BLK, KV_W), lambda n: (jnp.clip(n - 1, 0, nb - 1), 1)),
                  pl.BlockSpec((BLK, KV_W), lambda n: (jnp.minimum(n, nb - 1), 1)),
                  pl.BlockSpec((BLK, D), cur), pl.BlockSpec((BLK, 128), cur), _table_spec(),
                  pl.BlockSpec(memory_space=pl.ANY)],
        out_specs=[pl.BlockSpec((BLK, D), cur), pl.BlockSpec((BLK, 2 * KV_W), prev),
                   pl.BlockSpec((N_KV, 4 * BLK, 4 * BLK), lambda n: (0, 0, 0))],
        out_shape=[SDS((s, 2 * D), BF16), SDS((s, 2 * KV_W), BF16), SDS((N_KV, 4 * BLK, 4 * BLK), F32)],
        scratch_shapes=[pltpu.VMEM((BLK, KV_W), F32), pltpu.VMEM((BLK, KV_W), F32)],
        input_output_aliases={8: 0},
        compiler_params=_params(("arbitrary",)),
    )(q, kv, kv, kv, kv, datt, stats, tab, dqz)


def _b_bwd(dqz, dkv, h1, dh2, oa, wbin_g, w_kv, g_kv, g_pre, g_apost, tm):
    s = h1.shape[0]
    nt = s // tm

    def body(dqz_ref, dkv_ref, h_ref, dh2_ref, oa_ref, wb_ref, wkv_ref, gk_ref, gb_ref, ga_ref,
             dh1_ref, doa_ref, dg_ref, dwb_ref, dwkv_ref, dwb16_ref, dwkv16_ref, wcat, dwb_acc, dwkv_acc, put_sem):
        @pl.when(pl.program_id(0) == 0)
        def _():
            dg_ref[...] = jnp.zeros_like(dg_ref)
            dwb_acc[...] = jnp.zeros_like(dwb_acc)
            dwkv_acc[...] = jnp.zeros_like(dwkv_acc)
            for j in range(N_CHIPS):
                pltpu.sync_copy(wb_ref.at[j], wcat.at[:, pl.ds(BIN_COLS * j, BIN_COLS)])
        dnb = _nt(dqz_ref[...], wcat[...])
        dnk = _nt(dkv_ref[...], wkv_ref[...])
        h = h_ref[...]
        r = _rms_scale(h)
        hh = h * r
        dwb_acc[...] += _tn((hh * gb_ref[...]).astype(BF16), dqz_ref[...])
        dwkv_acc[...] += _tn((hh * gk_ref[...]).astype(BF16), dkv_ref[...])
        _acc_row(dg_ref, 0, jnp.sum(dnk * hh, axis=0, keepdims=True))
        _acc_row(dg_ref, 1, jnp.sum(dnb * hh, axis=0, keepdims=True))
        dhh = dnb * gb_ref[...] + dnk * gk_ref[...]
        dh1 = dh2_ref[...] + r * (dhh - hh * jnp.mean(dhh * hh, axis=-1, keepdims=True))
        dh1_ref[...] = dh1
        oa = oa_ref[...].astype(F32)
        ra = _rms_scale(oa)
        oh = oa * ra
        _acc_row(dg_ref, 2, jnp.sum(dh1 * oh, axis=0, keepdims=True))
        doh = dh1 * ga_ref[...]
        doa_ref[...] = (ra * (doh - oh * jnp.mean(doh * oh, axis=-1, keepdims=True))).astype(BF16)

        @pl.when(pl.program_id(0) == nt - 1)
        def _():
            wcat[...] = dwb_acc[...].astype(BF16)
            puts = [pltpu.make_async_copy(dwkv_acc, dwkv_ref, put_sem.at[2 * N_CHIPS])]
            for j in range(N_CHIPS):
                cols = pl.ds(BIN_COLS * j, BIN_COLS)
                puts.append(pltpu.make_async_copy(dwb_acc.at[:, cols], dwb_ref.at[j], put_sem.at[2 * j]))
                puts.append(pltpu.make_async_copy(wcat.at[:, cols], dwb16_ref.at[j], put_sem.at[2 * j + 1]))
            for put in puts:
                put.start()
            for put in puts:
                put.wait()
            wcat[:, 0:2 * KV_W] = dwkv_acc[...].astype(BF16)
            pltpu.sync_copy(wcat.at[:, pl.ds(0, 2 * KV_W)], dwkv16_ref)

    row = lambda i: (i, 0)
    fix = lambda i: (0, 0)
    anyspace = pl.BlockSpec(memory_space=pl.ANY)
    return pl.pallas_call(
        body, name="b_bwd", grid=(nt,),
        in_specs=[pl.BlockSpec((tm, 2 * D), row), pl.BlockSpec((tm, 2 * KV_W), row), pl.BlockSpec((tm, D), row),
                  pl.BlockSpec((tm, D), row), pl.BlockSpec((tm, D), row), anyspace, pl.BlockSpec((D, 2 * KV_W), fix),
                  pl.BlockSpec((1, D), fix), pl.BlockSpec((1, D), fix), pl.BlockSpec((1, D), fix)],
        out_specs=[pl.BlockSpec((tm, D), row), pl.BlockSpec((tm, D), row), pl.BlockSpec((8, D), fix)] + [anyspace] * 4,
        out_shape=[SDS((s, D), F32), SDS((s, D), BF16), SDS((8, D), F32), SDS((N_CHIPS, D, BIN_COLS), F32),
                   SDS((D, 2 * KV_W), F32), SDS((N_CHIPS, D, BIN_COLS), BF16), SDS((D, 2 * KV_W), BF16)],
        scratch_shapes=[pltpu.VMEM((D, 2 * D), BF16), pltpu.VMEM((D, 2 * D), F32), pltpu.VMEM((D, 2 * KV_W), F32),
                        pltpu.SemaphoreType.DMA((2 * N_CHIPS + 1,))],
        compiler_params=_params(("arbitrary",)),
    )(dqz, dkv, h1, dh2, oa, wbin_g, w_kv, g_kv, g_pre, g_apost)


def _to_owner_core(pieces, r, send, recv, core, action):
    x, y, c = lax.axis_index("x"), lax.axis_index("y"), lax.axis_index("c")
    for kp in range(N_CHIPS):
        px, py = kp >> 1, kp & 1
        rel = 4 * (x + px - 2 * x * px) + 2 * (y + py - 2 * y * py) + (c + core - 2 * c * core)

        @pl.when(rel != 0)
        def _():
            cp = pltpu.make_async_remote_copy(src_ref=pieces.at[kp], dst_ref=r.at[rel - 1], send_sem=send.at[kp],
                                              recv_sem=recv.at[rel - 1], device_id=(px, py, core), device_id_type=MESH)
            if action == "start":
                cp.start()
            else:
                cp.wait_send()
    if action == "wait":
        @pl.when(c == core)
        def _():
            for rel in range(1, N_DEV):
                pltpu.make_async_remote_copy(src_ref=pieces.at[0], dst_ref=r.at[rel - 1], send_sem=send.at[0],
                                             recv_sem=recv.at[rel - 1], device_id=(x, y, c),
                                             device_id_type=MESH).wait_recv()


def _owner_core_sems():
    return [pltpu.SemaphoreType.DMA((N_CHIPS,)), pltpu.SemaphoreType.DMA((N_DEV - 1,))]


def _device_exchange(grads, recvs, send, recv):
    x, y, c = lax.axis_index("x"), lax.axis_index("y"), lax.axis_index("c")
    copies = []
    for a, (g, r) in enumerate(zip(grads, recvs)):
        h = g.shape[1] // 2
        for rel in range(1, N_DEV):
            fx, fy, fc = rel >> 2, (rel >> 1) & 1, rel & 1
            px, py, pc = x + fx - 2 * x * fx, y + fy - 2 * y * fy, c + fc - 2 * c * fc
            sem = (N_DEV - 1) * a + rel - 1
            copies.append(pltpu.make_async_remote_copy(
                src_ref=g.at[2 * px + py, pl.ds(pl.multiple_of(pc * h, 16), h)], dst_ref=r.at[rel - 1],
                send_sem=send.at[sem], recv_sem=recv.at[sem], device_id=(px, py, pc), device_id_type=MESH))
    return copies


def _device_exchange_specs(grads):
    anyspace = pl.BlockSpec(memory_space=pl.ANY)
    n = len(grads)
    count = (N_DEV - 1) * n
    return ([anyspace] * n, [anyspace] * n,
            [SDS((N_DEV - 1, g.shape[1] // 2, g.shape[2]), g.dtype) for g in grads],
            [pltpu.SemaphoreType.DMA((count,)), pltpu.SemaphoreType.DMA((count,))])


def _a_bwd(doa, ya, conv, proj, conv_w, w_out, tm, parts):
    s = doa.shape[0]
    nt = s // tm
    n = len(parts)
    ex_in, ex_out, ex_shape, ex_sems = _device_exchange_specs(parts)

    def body(*refs):
        doa_ref, ya_ref, conv_ref, proj_ref, cw_ref, w_ref = refs[:6]
        part_refs = refs[6:6 + n]
        dproj_ref, dcw_ref, dw_ref, dw16_ref = refs[6 + n:10 + n]
        recv_refs = refs[10 + n:10 + 2 * n]
        carry, dw_acc, stage, put_sem, send, recv = refs[10 + 2 * n:]
        i = pl.program_id(0)

        @pl.when(i == 0)
        def _():
            dcw_ref[...] = jnp.zeros_like(dcw_ref)
            carry[...] = jnp.zeros_like(carry)
            dw_acc[...] = jnp.zeros_like(dw_acc)
            for cp in _device_exchange(part_refs, recv_refs, send, recv):
                cp.start()
        dya = _nt(doa_ref[...], w_ref[...])
        dw_acc[...] += _tn(ya_ref[...], doa_ref[...])
        bg = proj_ref[:, 0:D].astype(F32)
        cg = proj_ref[:, D:2 * D].astype(F32)
        u = proj_ref[:, 2 * D:3 * D].astype(F32)
        z = proj_ref[:, 3 * D:4 * D].astype(F32)
        v = cg * u
        rows = lax.broadcasted_iota(jnp.int32, (tm, D), 0)
        conv = conv_ref[...].astype(F32)
        sg, sz = _silu_parts(z)
        dproj_ref[:, 0:D] = (dya * conv * sz).astype(BF16)
        dproj_ref[:, 3 * D:4 * D] = (dya * bg * conv * _dsilu(z, sg)).astype(BF16)
        dconv = dya * bg * sz
        after = carry[...]
        up1 = jnp.where(rows < tm - 1, pltpu.roll(dconv, tm - 1, 0), after[0:1, :])
        up2 = jnp.where(rows < tm - 2, pltpu.roll(dconv, tm - 2, 0),
                        jnp.where(rows == tm - 2, after[0:1, :], after[1:2, :]))
        carry[...] = dconv[0:8, :]
        _acc_row(dcw_ref, 0, jnp.sum(up2 * v, axis=0, keepdims=True))
        _acc_row(dcw_ref, 1, jnp.sum(up1 * v, axis=0, keepdims=True))
        _acc_row(dcw_ref, 2, jnp.sum(dconv * v, axis=0, keepdims=True))
        dv = cw_ref[2:3, :] * dconv + cw_ref[1:2, :] * up1 + cw_ref[0:1, :] * up2
        dproj_ref[:, D:2 * D] = (dv * u).astype(BF16)
        dproj_ref[:, 2 * D:3 * D] = (dv * cg).astype(BF16)

        @pl.when(i == nt - 1)
        def _():
            _write_gradient(dw_acc, dw_ref, dw16_ref, stage, put_sem)
            for cp in _device_exchange(part_refs, recv_refs, send, recv):
                cp.wait()

    rev = lambda i: (nt - 1 - i, 0)
    fix = lambda i: (0, 0)
    anyspace = pl.BlockSpec(memory_space=pl.ANY)
    dproj, dcw, dw, dw16, *got = pl.pallas_call(
        body, name="a_bwd", grid=(nt,),
        in_specs=[pl.BlockSpec((tm, D), rev), pl.BlockSpec((tm, D), rev), pl.BlockSpec((tm, D), rev),
                  pl.BlockSpec((tm, 4 * D), rev), pl.BlockSpec((8, D), fix), pl.BlockSpec((D, D), fix)] + ex_in,
        out_specs=[pl.BlockSpec((tm, 4 * D), rev), pl.BlockSpec((8, D), fix), anyspace, anyspace] + ex_out,
        out_shape=[SDS((s, 4 * D), BF16), SDS((8, D), F32), SDS((D, D), F32), SDS((D, D), BF16)] + ex_shape,
        scratch_shapes=[pltpu.VMEM((8, D), F32), pltpu.VMEM((D, D), F32), pltpu.VMEM((D // 4, D), BF16),
                        pltpu.SemaphoreType.DMA] + ex_sems,
        compiler_params=_params(("arbitrary",)),
    )(doa, ya, conv, proj, conv_w, w_out, *parts)
    return dproj, dcw, dw, dw16, got


def _dn1(dp_ref, w_ref):
    dn = _nt(dp_ref[:, 0:D], w_ref[0])
    for j in range(1, 4):
        dn = dn + _nt(dp_ref[:, D * j:D * (j + 1)], w_ref[j])
    return dn


def _a_in_bwd_matmul(dproj, win_g, tm, count, win_half, win_got):
    def body(dp_ref, w_ref, half_ref, got_in, dn_ref, got_ref, wcat, send, recv):
        del got_in

        @pl.when(pl.program_id(0) == 0)
        def _():
            _to_owner_core(half_ref, got_ref, send, recv, 1, "start")
            for j in range(N_CHIPS):
                pltpu.sync_copy(w_ref.at[j], wcat.at[:, pl.ds(D * j, D)])
        dn_ref[...] = _nt(dp_ref[...], wcat[...]).astype(BF16)

        @pl.when(pl.program_id(0) == count - 1)
        def _():
            _to_owner_core(half_ref, got_ref, send, recv, 1, "wait")

    row = lambda i: (i, 0)
    anyspace = pl.BlockSpec(memory_space=pl.ANY)
    return pl.pallas_call(
        body, name="a_in_bwd_matmul", grid=(count,),
        in_specs=[pl.BlockSpec((tm, 4 * D), row), anyspace, anyspace, anyspace],
        out_specs=[pl.BlockSpec((tm, D), row), anyspace],
        out_shape=[SDS((count * tm, D), BF16), SDS(win_got.shape, win_got.dtype)],
        scratch_shapes=[pltpu.VMEM((D, 4 * D), BF16)] + _owner_core_sems(),
        input_output_aliases={3: 1},
        compiler_params=_params(("arbitrary",)),
    )(dproj, win_g, win_half, win_got)


def _a_in_bwd(dn_first, dproj, x, dh1, win_g, g_pre, tm):
    s = x.shape[0]
    nt = s // tm
    count = dn_first.shape[0] // tm

    def body(dn_ref, dp_ref, x_ref, dh_ref, w_ref, g_ref, gx_ref, dg_ref, dn_s):
        i = pl.program_id(0)

        @pl.when(i == 0)
        def _():
            dg_ref[...] = jnp.zeros_like(dg_ref)

        @pl.when(i < count)
        def _():
            dn_s[...] = dn_ref[...].astype(F32)

        @pl.when(i >= count)
        def _():
            dn_s[...] = _dn1(dp_ref, w_ref)
        dn = dn_s[...]
        xv = x_ref[...]
        r = _rms_scale(xv)
        xh = xv * r
        _acc_row(dg_ref, 0, jnp.sum(dn * xh, axis=0, keepdims=True))
        dxh = dn * g_ref[...]
        gx_ref[...] = dh_ref[...] + r * (dxh - xh * jnp.mean(dxh * xh, axis=-1, keepdims=True))

    row = lambda i: (i, 0)
    fix = lambda i: (0, 0)
    return pl.pallas_call(
        body, name="a_in_bwd", grid=(nt,),
        in_specs=[pl.BlockSpec((tm, D), lambda i: (jnp.minimum(i, count - 1), 0)),
                  pl.BlockSpec((tm, 4 * D), lambda i: (jnp.maximum(i, count), 0)),
                  pl.BlockSpec((tm, D), row), pl.BlockSpec((tm, D), row),
                  pl.BlockSpec((4, D, D), lambda i: (0, 0, 0)), pl.BlockSpec((1, D), fix)],
        out_specs=[pl.BlockSpec((tm, D), row), pl.BlockSpec((8, D), fix)],
        out_shape=[SDS((s, D), F32), SDS((8, D), F32)],
        scratch_shapes=[pltpu.VMEM((tm, D), F32)],
        compiler_params=_params(("arbitrary",)),
    )(dn_first, dproj, x, dh1, win_g, g_pre)


def _swap_halves(shards, send, recv):
    x, y, c = lax.axis_index("x"), lax.axis_index("y"), lax.axis_index("c")
    sibling = (x, y, 1 - c)
    copies = []
    for b, full in enumerate(shards):
        h = full.shape[0] // 2
        mine = full.at[pl.ds(pl.multiple_of(c * h, 8), h)]
        theirs = full.at[pl.ds(pl.multiple_of((1 - c) * h, 8), h)]
        copies.append((pltpu.make_async_remote_copy(src_ref=mine, dst_ref=mine, send_sem=send.at[b], recv_sem=recv.at[b],
                                                    device_id=sibling, device_id_type=MESH),
                       pltpu.make_async_remote_copy(src_ref=mine, dst_ref=theirs, send_sem=send.at[b], recv_sem=recv.at[b],
                                                    device_id=sibling, device_id_type=MESH)))
    return copies


def _dw_in_half(n1, dproj, core, tmw, name, to_owners=None, to_devices=None, shards=()):
    s = n1.shape[0]
    h = D // 2
    nt = s // tmw
    n_sh = len(shards)
    if to_owners is not None:
        sent_array, sems, got_shape = to_owners, _owner_core_sems(), SDS((N_DEV - 1, h, D), BF16)
    else:
        sent_array = to_devices
        _, _, (got_shape,), sems = _device_exchange_specs([to_devices])

    def body(*refs):
        a_ref, b_ref, sent = refs[:3]
        o_ref, o16_ref, got = refs[3 + n_sh:6 + n_sh]
        shard_refs = refs[6 + n_sh:6 + 2 * n_sh]
        send, recv = refs[6 + 2 * n_sh:8 + 2 * n_sh]
        swap_sems = refs[8 + 2 * n_sh:]
        j, t = pl.program_id(0), pl.program_id(1)

        def exchange(action):
            if to_owners is not None:
                _to_owner_core(sent, got, send, recv, 1 - core, action)
            else:
                for cp in _device_exchange([sent], [got], send, recv):
                    cp.start() if action == "start" else cp.wait()

        @pl.when((j == 0) & (t == 0))
        def _():
            exchange("start")
            if n_sh:
                for mine, _ in _swap_halves(shard_refs, *swap_sems):
                    mine.start()

        @pl.when(t == 0)
        def _():
            o_ref[...] = jnp.zeros_like(o_ref)
        o_ref[0] += _tn(a_ref[...], b_ref[...])

        @pl.when(t == nt - 1)
        def _():
            o16_ref[...] = o_ref[...].astype(BF16)

        @pl.when((j == N_CHIPS - 1) & (t == nt - 1))
        def _():
            exchange("wait")
            if n_sh:
                for mine, theirs in _swap_halves(shard_refs, *swap_sems):
                    theirs.wait_recv()
                    mine.wait_send()

    anyspace = pl.BlockSpec(memory_space=pl.ANY)
    slot = pl.BlockSpec((1, h, D), lambda j, t: (j, 0, 0))
    swap_scratch = [pltpu.SemaphoreType.DMA((n_sh,)), pltpu.SemaphoreType.DMA((n_sh,))] if n_sh else []
    return pl.pallas_call(
        body, name=name, grid=(N_CHIPS, nt),
        in_specs=[pl.BlockSpec((tmw, h), lambda j, t: (t, core)), pl.BlockSpec((tmw, D), lambda j, t: (t, j))]
        + [anyspace] * (1 + n_sh),
        out_specs=[slot, slot] + [anyspace] * (1 + n_sh),
        out_shape=[SDS((N_CHIPS, h, D), F32), SDS((N_CHIPS, h, D), BF16), got_shape]
        + [SDS(sh.shape, F32) for sh in shards],
        scratch_shapes=sems + swap_scratch,
        input_output_aliases={3 + b: 3 + b for b in range(n_sh)},
        compiler_params=_params(("arbitrary", "arbitrary")),
    )(n1, dproj, sent_array, *shards)


def _share_and_gather(shards, smalls):
    n_h, n_s = len(shards), len(smalls)

    def body(*refs):
        small_ins = refs[n_h:n_h + n_s]
        fs = refs[n_h + n_s:2 * n_h + n_s]
        small_alls = refs[2 * n_h + n_s:2 * n_h + 2 * n_s]
        dsend, drecv, ssend, srecv = refs[2 * n_h + 2 * n_s:]
        x, y, c = lax.axis_index("x"), lax.axis_index("y"), lax.axis_index("c")
        swaps = _swap_halves(fs, dsend, drecv)
        sends, arrivals = [mine for mine, _ in swaps], [theirs for _, theirs in swaps]
        me = 4 * x + 2 * y + c
        for k, (small_in, small_all) in enumerate(zip(small_ins, small_alls)):
            small_all[me] = small_in[...]
            for rel in range(1, N_DEV):
                fx, fy, fc = rel >> 2, (rel >> 1) & 1, rel & 1
                peer = (x + fx - 2 * x * fx, y + fy - 2 * y * fy, c + fc - 2 * c * fc)
                sender = 4 * peer[0] + 2 * peer[1] + peer[2]
                sem = (N_DEV - 1) * k + rel - 1
                sends.append(pltpu.make_async_remote_copy(
                    src_ref=small_in, dst_ref=small_all.at[me], send_sem=ssend.at[sem], recv_sem=srecv.at[sem],
                    device_id=peer, device_id_type=MESH))
                arrivals.append(pltpu.make_async_remote_copy(
                    src_ref=small_in, dst_ref=small_all.at[sender], send_sem=ssend.at[sem], recv_sem=srecv.at[sem],
                    device_id=peer, device_id_type=MESH))
        for cp in sends:
            cp.start()
        for cp in arrivals:
            cp.wait_recv()
        for cp in sends:
            cp.wait_send()

    anyspace = pl.BlockSpec(memory_space=pl.ANY)
    vm = pl.BlockSpec(memory_space=pltpu.VMEM)
    out_shape = [SDS(full.shape, F32) for full in shards] + [SDS((N_DEV,) + sm.shape, F32) for sm in smalls]
    n_all = (N_DEV - 1) * n_s
    outs = pl.pallas_call(
        body, name="share_and_gather", out_shape=out_shape,
        in_specs=[anyspace] * n_h + [vm] * n_s, out_specs=[anyspace] * n_h + [vm] * n_s,
        scratch_shapes=[pltpu.SemaphoreType.DMA((n_h,)), pltpu.SemaphoreType.DMA((n_h,)),
                        pltpu.SemaphoreType.DMA((n_all,)), pltpu.SemaphoreType.DMA((n_all,))],
        input_output_aliases={b: b for b in range(n_h)},
    )(*shards, *smalls)
    return outs[:n_h], outs[n_h:]


def _add_win(where, lo, hi, r, name):
    _, h, cols = lo.shape
    tr = min(h, 256)
    nh = h // tr

    def body(where_ref, lo_ref, hi_ref, r_ref, o_ref):
        acc = jnp.where(where_ref[0] == 0, lo_ref[0], hi_ref[0])
        for k in range(N_DEV - 1):
            acc = acc + r_ref[k].astype(F32)
        o_ref[...] = acc

    own = pl.BlockSpec((1, tr, cols), lambda i, w: (w[1], i, 0))
    return pl.pallas_call(
        body, name=name,
        grid_spec=pltpu.PrefetchScalarGridSpec(
            num_scalar_prefetch=1, grid=(nh,),
            in_specs=[own, own, pl.BlockSpec((N_DEV - 1, tr, cols), lambda i, w: (0, i, 0))],
            out_specs=pl.BlockSpec((tr, cols), lambda i, w: (w[0] * nh + i, 0))),
        out_shape=SDS((2 * h, cols), F32),
        compiler_params=_params(("parallel",)),
    )(where, lo, hi, r)


def _add_devices(where, g, r, name):
    _, rows, cols = g.shape
    h = rows // 2
    tr = min(h, 256)
    nh = h // tr

    def body(where_ref, g_ref, r_ref, o_ref):
        del where_ref
        acc = g_ref[0]
        for k in range(N_DEV - 1):
            acc = acc + r_ref[k].astype(F32)
        o_ref[...] = acc

    return pl.pallas_call(
        body, name=name,
        grid_spec=pltpu.PrefetchScalarGridSpec(
            num_scalar_prefetch=1, grid=(nh,),
            in_specs=[pl.BlockSpec((1, tr, cols), lambda i, w: (w[1], w[0] * nh + i, 0)),
                      pl.BlockSpec((N_DEV - 1, tr, cols), lambda i, w: (0, i, 0))],
            out_specs=pl.BlockSpec((tr, cols), lambda i, w: (w[0] * nh + i, 0))),
        out_shape=SDS((rows, cols), F32),
        compiler_params=_params(("parallel",)),
    )(where, g, r)


def _sum_smalls(gathered):
    n = len(gathered)

    def body(*refs):
        for all_ref, o_ref in zip(refs[:n], refs[n:]):
            acc = all_ref[0]
            for dev in range(1, N_DEV):
                acc = acc + all_ref[dev]
            o_ref[...] = acc

    vm = pl.BlockSpec(memory_space=pltpu.VMEM)
    return pl.pallas_call(
        body, name="sum_smalls", out_shape=[SDS(a.shape[1:], F32) for a in gathered],
        in_specs=[vm] * n, out_specs=[vm] * n,
    )(*gathered)


def _adam_step(g, w, m, v):
    nm = ADAM_B1 * m + (1.0 - ADAM_B1) * g
    nv = ADAM_B2 * v + (1.0 - ADAM_B2) * (g * g)
    m_hat = nm / (1.0 - ADAM_B1 ** ADAM_STEP)
    v_hat = nv / (1.0 - ADAM_B2 ** ADAM_STEP)
    return -ADAM_LR * (m_hat / (jnp.sqrt(v_hat) + ADAM_EPS) + ADAM_WD * w), nm, nv


def _adamw(g, w, m, v, name):
    rows, cols = g.shape
    tr = min(rows, 256)

    def body(g_ref, w_ref, m_ref, v_ref, d_ref, nm_ref, nv_ref):
        d_ref[...], nm_ref[...], nv_ref[...] = _adam_step(g_ref[...], w_ref[...], m_ref[...], v_ref[...])

    spec = pl.BlockSpec((tr, cols), lambda i: (i, 0))
    return pl.pallas_call(
        body, name=name, grid=(rows // tr,), in_specs=[spec] * 4, out_specs=[spec] * 3,
        out_shape=[SDS(g.shape, F32)] * 3, compiler_params=_params(("parallel",)),
    )(g, w, m, v)


def _small_update(chip, tot, tot_rel, wmv):
    names = list(SMALL_PLACES)
    n = len(names)

    def body(chip_ref, tot_ref, quarter_ref, rel_ref, *refs):
        del chip_ref
        ins, outs = refs[:3 * n], refs[3 * n:]
        for i, nm in enumerate(names):
            source, row, (rows, cols) = SMALL_PLACES[nm]
            g = {"rows": tot_ref, "quarter": quarter_ref, "rel": rel_ref}[source][row:row + rows, 0:cols]
            outs[4 * i][...] = g
            outs[4 * i + 1][...], outs[4 * i + 2][...], outs[4 * i + 3][...] = _adam_step(
                g, ins[3 * i][...], ins[3 * i + 1][...], ins[3 * i + 2][...])

    whole = lambda shape: pl.BlockSpec(shape, lambda i, c: (0,) * len(shape))
    shapes = [SMALL_PLACES[nm][2] for nm in names]
    outs = pl.pallas_call(
        body, name="small_update",
        grid_spec=pltpu.PrefetchScalarGridSpec(
            num_scalar_prefetch=1, grid=(1,),
            in_specs=[whole(tot.shape), pl.BlockSpec((tot.shape[0], D // 4), lambda i, c: (0, c[0])),
                      whole(tot_rel.shape)] + [whole(shp) for shp in shapes for _ in range(3)],
            out_specs=[whole(shp) for shp in shapes for _ in range(4)]),
        out_shape=[SDS(shp, F32) for shp in shapes for _ in range(4)],
    )(chip, tot, tot, tot_rel, *[a for nm in names for a in wmv[nm]])
    return {nm: tuple(outs[4 * i:4 * i + 4]) for i, nm in enumerate(names)}


def _pad_rows(a, rows):
    return jnp.concatenate([a, jnp.zeros((rows - a.shape[0], a.shape[1]), a.dtype)], axis=0)


def _pad_cols(a, cols):
    return jnp.concatenate([a, jnp.zeros((a.shape[0], cols - a.shape[1]), a.dtype)], axis=1)


def kernel(x, a_pre_norm, a_w_in, a_conv_w, a_w_out, a_post_norm, kv_norm, w_kv, rel_bias, b_pre_norm, b_w_in, b_sinks, b_w_out, b_post_norm, loss_target, m_a_pre_norm, m_a_w_in, m_a_conv_w, m_a_w_out, m_a_post_norm, m_kv_norm, m_w_kv, m_rel_bias, m_b_pre_norm, m_b_w_in, m_b_sinks, m_b_w_out, m_b_post_norm, v_a_pre_norm, v_a_w_in, v_a_conv_w, v_a_w_out, v_a_post_norm, v_kv_norm, v_w_kv, v_rel_bias, v_b_pre_norm, v_b_w_in, v_b_sinks, v_b_w_out, v_b_post_norm):
    seq = x.shape[1]
    xs = x.reshape(seq, D)
    tgt = loss_target.reshape(seq, D)
    chip = 2 * lax.axis_index("x") + lax.axis_index("y")
    core = lax.axis_index("c")
    tm = _tile(seq, 512)
    tmw = _tile(seq, 1024)

    shards = [a_w_in[0], a_w_out[0], w_kv, b_w_in[0], b_w_out[0]]
    small_w = _pad_rows(jnp.concatenate([a_pre_norm, a_conv_w[0], a_post_norm], axis=0), 8)
    *own_only, small_g = _prepare_weights(shards, small_w)
    where = jnp.stack([core, chip]).astype(jnp.int32)
    small_full = small_g.transpose(1, 0, 2).reshape(8, D)
    g_apre, conv_w, g_apost = small_full[0:1], _pad_rows(small_full[1:4], 8), small_full[4:5]
    g_kv = kv_norm.reshape(1, D)

    proj, n1, (win_g, wouta_g, wkv_g, wbin_g, woutb_g) = _a_in(where[1:2], xs, g_apre, own_only, tmw)
    wouta = wouta_g.reshape(D, D)
    wkv = wkv_g.reshape(D, 2 * KV_W)
    woutb = woutb_g.reshape(D, D)
    ya, oa, h1, conv = _a_mix(proj, xs, conv_w, wouta, g_apost, tm)
    kv, q, zb = _b_in(h1, g_kv, b_pre_norm, wkv, wbin_g, tmw)
    tab = _bias_table(rel_bias, b_sinks.reshape(N_HEADS))
    att, stats = _attn_fwd(q, kv, tab)
    dh2, dqz, datt, loss_acc, dg_bpost, dw_outb, dw_outb16 = _mid(att, zb, h1, tgt, woutb, b_post_norm, tm)

    dqz, dkv, dtab = _attn_bwd(q, kv, datt, stats, tab, dqz)
    dh1, doa, dg_b, dw_bin, dw_kv, dw_bin16, dw_kv16 = _b_bwd(dqz, dkv, h1, dh2, oa, wbin_g, wkv, g_kv, b_pre_norm,
                                                              g_apost, tm)
    by_chip = lambda a, cols: a.reshape(N_CHIPS, D // 4, cols)
    grads1 = [by_chip(dw_kv, 2 * KV_W), dw_bin, by_chip(dw_outb, D)]
    sent1 = [by_chip(dw_kv16, 2 * KV_W), dw_bin16, by_chip(dw_outb16, D)]
    names1 = ["w_kv", "b_w_in", "b_w_out"]
    dproj, dconv_w, dw_outa, dw_outa16, from_devices1 = _a_bwd(doa, ya, conv, proj, conv_w, wouta, tm, sent1)
    shards1 = [_add_devices(where, g, r, "add_devices_" + nm) for g, r, nm in zip(grads1, from_devices1, names1)]
    tmw2 = _tile(seq, 4096)
    win_lo, win_lo16, outa_got, g_wkv, g_wbin, g_woutb = _dw_in_half(
        n1, dproj, 0, tmw2, "dw_a_in_lo", to_devices=by_chip(dw_outa16, D), shards=shards1)
    win_hi, win_hi16, win_got = _dw_in_half(n1, dproj, 1, tmw2, "dw_a_in_hi", to_owners=win_lo16)
    nt = seq // tmw
    dn_first, win_got = _a_in_bwd_matmul(dproj, win_g, tmw, max(nt - max(nt // 4, 1), 1), win_hi16, win_got)
    grad_x, dg_apre = _a_in_bwd(dn_first, dproj, xs, dh1, win_g, g_apre, tm)
    shards2 = [_add_win(where, win_lo, win_hi, win_got, "add_devices_a_w_in"),
               _add_devices(where, by_chip(dw_outa, D), outa_got, "add_devices_a_w_out")]
    drel, dsink = _bias_fold(dtab)

    smalls = jnp.concatenate([
        dg_apre[0:1], dg_b[2:3], dg_b[0:1], dg_b[1:2], dg_bpost[0:1], _pad_cols(dsink[0:1], D),
        _pad_cols(loss_acc[0:1], D), jnp.zeros((1, D), F32), dconv_w], axis=0)
    assert smalls.shape == (SMALL_ROWS, D)
    (g_win, g_wouta), gathered = _share_and_gather(shards2, (smalls, drel))
    tot, tot_rel = _sum_smalls(gathered)

    big = {}
    for nm, g, w, m, v in [("a_w_in", g_win, a_w_in, m_a_w_in, v_a_w_in), ("a_w_out", g_wouta, a_w_out, m_a_w_out, v_a_w_out),
                           ("w_kv", g_wkv, w_kv, m_w_kv, v_w_kv), ("b_w_in", g_wbin, b_w_in, m_b_w_in, v_b_w_in),
                           ("b_w_out", g_woutb, b_w_out, m_b_w_out, v_b_w_out)]:
        shp = w.shape
        two = (shp[-2], shp[-1])
        d, nm_, nv_ = _adamw(g, w.reshape(two), m.reshape(two), v.reshape(two), "adamw_" + nm)
        big[nm] = (g.reshape(shp), d.reshape(shp), nm_.reshape(shp), nv_.reshape(shp))

    given = {"a_pre_norm": (a_pre_norm, m_a_pre_norm, v_a_pre_norm), "a_conv_w": (a_conv_w, m_a_conv_w, v_a_conv_w),
             "a_post_norm": (a_post_norm, m_a_post_norm, v_a_post_norm), "kv_norm": (kv_norm, m_kv_norm, v_kv_norm),
             "rel_bias": (rel_bias, m_rel_bias, v_rel_bias), "b_pre_norm": (b_pre_norm, m_b_pre_norm, v_b_pre_norm),
             "b_sinks": (b_sinks, m_b_sinks, v_b_sinks), "b_post_norm": (b_post_norm, m_b_post_norm, v_b_post_norm)}
    small = _small_update(where[1:2], tot, tot_rel, {nm: tuple(a.reshape(SMALL_PLACES[nm][2]) for a in wmv)
                                            for nm, wmv in given.items()})
    order = ["a_pre_norm", "a_w_in", "a_conv_w", "a_w_out", "a_post_norm", "kv_norm", "w_kv", "rel_bias",
             "b_pre_norm", "b_w_in", "b_sinks", "b_w_out", "b_post_norm"]
    outs = []
    for which in range(4):
        for nm in order:
            outs.append(big[nm][which] if nm in big else small[nm][which].reshape(given[nm][0].shape))
    loss = 0.5 * tot[LOSS_ROW, 0]
    return (loss, grad_x.reshape(x.shape), *outs)
```

```python
import math

import jax
import jax.numpy as jnp
from jax import lax
from jax.experimental import pallas as pl
from jax.experimental.pallas import tpu as pltpu

F32 = jnp.float32
BF16 = jnp.bfloat16
MESH = pl.DeviceIdType.MESH
SDS = jax.ShapeDtypeStruct

D = 1024
HEAD_DIM = 64
N_HEADS = 16
N_KV = 2
GROUP = 8
KV_W = 128
BLK = 128
N_BUCKETS = 32
MAX_EXACT = 16
MAX_DISTANCE = 128
EPS = 1e-6
NEG_INF = -1e30
Q_SCALE = HEAD_DIM ** -0.5

ADAM_LR = 0.001
ADAM_B1 = 0.9
ADAM_B2 = 0.999
ADAM_EPS = 1e-08
ADAM_WD = 0.01
ADAM_STEP = 10

N_CHIPS = 4
N_DEV = 8
BIN_COLS = 2 * D // N_CHIPS
VMEM_LIMIT = 56 * 1024 * 1024
SMALL_ROWS = 16
LOSS_ROW = 6
SMALL_PLACES = {
    "a_pre_norm": ("quarter", 0, (1, D // 4)), "a_conv_w": ("quarter", 8, (3, D // 4)),
    "a_post_norm": ("quarter", 1, (1, D // 4)), "kv_norm": ("rows", 2, (1, D)),
    "rel_bias": ("rel", 0, (N_BUCKETS, N_HEADS)), "b_pre_norm": ("rows", 3, (1, D)),
    "b_sinks": ("rows", 5, (1, N_HEADS)), "b_post_norm": ("rows", 4, (1, D)),
}


def _bucket_thresholds():
    def bucket(d):
        big = MAX_EXACT + int(math.log(d / MAX_EXACT) / math.log(MAX_DISTANCE / MAX_EXACT)
                              * (N_BUCKETS - MAX_EXACT))
        return d if d < MAX_EXACT else min(big, N_BUCKETS - 1)
    out = []
    for b in range(MAX_EXACT + 1, N_BUCKETS):
        out.append(min(d for d in range(MAX_EXACT, MAX_DISTANCE) if bucket(d) >= b))
    return tuple(out)


BUCKET_THRESHOLDS = _bucket_thresholds()


def _params(semantics=None, vmem=VMEM_LIMIT):
    return pltpu.CompilerParams(dimension_semantics=semantics, vmem_limit_bytes=vmem)


def _tile(n, pref):
    return pref if n >= 2 * pref else max(n // 2, 8)


def _rms_scale(v):
    return lax.rsqrt(jnp.mean(v * v, axis=-1, keepdims=True) + EPS)


def _nt(a, b):
    return lax.dot_general(a, b, (((1,), (1,)), ((), ())), preferred_element_type=F32)


def _tn(a, b):
    return lax.dot_general(a, b, (((0,), (0,)), ((), ())), preferred_element_type=F32)


def _nn(a, b):
    return jnp.dot(a, b, preferred_element_type=F32)


def _silu_parts(z):
    sg = jax.nn.sigmoid(z)
    return sg, z * sg


def _dsilu(z, sg):
    return sg * (1.0 + z * (1.0 - sg))


def _write_gradient(acc, out32, out16, stage, sem):
    whole = pltpu.make_async_copy(acc, out32, sem)
    whole.start()
    rows = stage.shape[0]
    for k in range(acc.shape[0] // rows):
        stage[...] = acc[rows * k:rows * (k + 1), :].astype(BF16)
        pltpu.sync_copy(stage, out16.at[pl.ds(rows * k, rows)])
    whole.wait()


def _acc_row(ref, row, val):
    ref[row:row + 1, :] += val


def _gather_copies(outs, splits, ici_send, ici_recv, d2d_send, d2d_recv):
    x, y, c = lax.axis_index("x"), lax.axis_index("y"), lax.axis_index("c")
    k = 2 * x + y
    sibling = (x, y, 1 - c)

    def part(o_ref, chip, core, split):
        if not split:
            return o_ref.at[chip]
        h = o_ref.shape[1] // 2
        return o_ref.at[chip, pl.ds(pl.multiple_of(core * h, 16), h)]

    def remote(ref, a, j, sems, to):
        return pltpu.make_async_remote_copy(src_ref=ref, dst_ref=ref, send_sem=sems[0].at[3 * a + j],
                                            recv_sem=sems[1].at[3 * a + j], device_id=to, device_id_type=MESH)

    copies = []
    for a, (o_ref, split) in enumerate(zip(outs, splits)):
        for j, (px, py) in enumerate([(x, 1 - y), (1 - x, y), (1 - x, 1 - y)]):
            kj = 2 * px + py
            ici, d2d = (ici_send, ici_recv), (d2d_send, d2d_recv)
            copies.append((remote(part(o_ref, k, c, split), a, j, ici, (px, py, c)),
                           remote(part(o_ref, kj, c, split), a, j, ici, (px, py, c)),
                           remote(part(o_ref, kj, c, split), a, j, d2d, sibling) if split else None,
                           remote(part(o_ref, kj, 1 - c, split), a, j, d2d, sibling) if split else None))
    return copies


def _gather_sems(n):
    return [pltpu.SemaphoreType.DMA((3 * n,)) for _ in range(4)]


def _prepare_weights(shards, small):
    n = len(shards)

    def body(*refs):
        ins, small_in = refs[:n], refs[n]
        outs, small_out = refs[n + 1:2 * n + 1], refs[2 * n + 1]
        stages, put_sem = refs[2 * n + 2:3 * n + 2], refs[3 * n + 2]
        sems = refs[3 * n + 3:]
        k = 2 * lax.axis_index("x") + lax.axis_index("y")
        puts = []
        for a, (i_ref, stage, o_ref) in enumerate(zip(ins, stages, outs)):
            stage[...] = i_ref[...].astype(BF16)
            puts.append(pltpu.make_async_copy(stage, o_ref.at[k], put_sem.at[a]))
            puts[-1].start()
        small_out[k] = small_in[...]
        copies = _gather_copies([small_out], [False], *sems)
        for send, _, _, _ in copies:
            send.start()
        for _, arrival, _, _ in copies:
            arrival.wait_recv()
        for send, _, _, _ in copies:
            send.wait_send()
        for put in puts:
            put.wait()

    vm = pl.BlockSpec(memory_space=pltpu.VMEM)
    anyspace = pl.BlockSpec(memory_space=pl.ANY)
    out_shape = [SDS((N_CHIPS,) + s.shape, BF16) for s in shards] + [SDS((N_CHIPS,) + small.shape, F32)]
    return pl.pallas_call(
        body, name="prepare_weights", out_shape=out_shape,
        in_specs=[vm] * (n + 1), out_specs=[anyspace] * n + [vm],
        scratch_shapes=[pltpu.VMEM(s.shape, BF16) for s in shards] + [pltpu.SemaphoreType.DMA((n,))] + _gather_sems(1),
        compiler_params=pltpu.CompilerParams(vmem_limit_bytes=VMEM_LIMIT),
    )(*shards, small)


def _a_in(chip, x, g_pre, weights, tm):
    s = x.shape[0]
    nt = s // tm
    n = len(weights)

    def body(chip_ref, x_ref, g_ref, *refs):
        proj_ref, n1_ref = refs[n:n + 2]
        gathered = refs[n + 2:2 * n + 2]
        wbuf, n1_all, fetch_sem = refs[2 * n + 2:2 * n + 5]
        sems = refs[2 * n + 5:]
        jj, i = pl.program_id(0), pl.program_id(1)
        copies = _gather_copies(gathered, [True] * n, *sems)

        def fetch(rel):
            slot = jnp.bitwise_xor(chip_ref[0], rel)
            return pltpu.make_async_copy(gathered[0].at[slot], wbuf.at[rel % 2], fetch_sem.at[rel % 2])

        @pl.when((jj == 0) & (i == 0))
        def _():
            fetch(0).start()
            copies[0][0].start()
            copies[1][0].start()
            fetch(0).wait()

        for rel in (1, 2, 3):
            @pl.when((jj == rel) & (i == 0))
            def _():
                fetch(rel).wait()

        @pl.when(jj == 0)
        def _():
            xv = x_ref[...]
            n1 = (xv * _rms_scale(xv) * g_ref[...]).astype(BF16)
            n1_ref[...] = n1
            n1_all[i] = n1
        proj_ref[...] = _nn(n1_all[i], wbuf[jj % 2]).astype(BF16)

        for rel in (1, 2, 3):
            @pl.when((jj == rel - 1) & (i == max(nt - 3, 0)))
            def _():
                _, arrival, forward, _ = copies[rel - 1]
                arrival.wait_recv()
                forward.start()
                if rel == 1:
                    for send, _, _, _ in copies[2:]:
                        send.start()

            @pl.when((jj == rel - 1) & (i == max(nt - 2, 0)))
            def _():
                copies[rel - 1][3].wait_recv()
                fetch(rel).start()

        @pl.when((jj == 3) & (i == max(nt - 2, 0)))
        def _():
            for _, arrival, forward, _ in copies[3:]:
                arrival.wait_recv()
                forward.start()

        @pl.when((jj == 3) & (i == nt - 1))
        def _():
            for _, _, _, forwarded in copies[3:]:
                forwarded.wait_recv()
            for send, _, forward, _ in copies:
                forward.wait_send()
                send.wait_send()

    anyspace = pl.BlockSpec(memory_space=pl.ANY)
    proj, n1, *gathered = pl.pallas_call(
        body, name="a_in",
        grid_spec=pltpu.PrefetchScalarGridSpec(
            num_scalar_prefetch=1, grid=(4, nt),
            in_specs=[pl.BlockSpec((tm, D), lambda jj, i, c: (jnp.where(jj == 0, i, nt - 1), 0)),
                      pl.BlockSpec((1, D), lambda jj, i, c: (0, 0))] + [anyspace] * n,
            out_specs=[pl.BlockSpec((tm, D), lambda jj, i, c: (i, jnp.bitwise_xor(c[0], jj))),
                       pl.BlockSpec((tm, D), lambda jj, i, c: (jnp.where(jj == 0, i, nt - 1), 0))] + [anyspace] * n,
            scratch_shapes=[pltpu.VMEM((2, D, D), BF16), pltpu.VMEM((nt, tm, D), BF16),
                            pltpu.SemaphoreType.DMA((2,))] + _gather_sems(n)),
        out_shape=[SDS((s, 4 * D), BF16), SDS((s, D), BF16)] + [SDS(w.shape, w.dtype) for w in weights],
        input_output_aliases={3 + a: 2 + a for a in range(n)},
        compiler_params=_params(("arbitrary", "arbitrary")),
    )(chip, x, g_pre, *weights)
    return proj, n1, gathered


def _shift_rows(v, last, second_last, rows):
    v1 = jnp.where(rows >= 1, pltpu.roll(v, 1, 0), last)
    v2 = jnp.where(rows >= 2, pltpu.roll(v, 2, 0), jnp.where(rows == 1, last, second_last))
    return v1, v2


def _a_mix(proj, x, conv_w, w_out, g_post, tm):
    s = x.shape[0]

    def body(proj_ref, x_ref, cw_ref, w_ref, g_ref, ya_ref, oa_ref, h1_ref, conv_ref, carry):
        @pl.when(pl.program_id(0) == 0)
        def _():
            carry[...] = jnp.zeros_like(carry)
        v = proj_ref[:, D:2 * D].astype(F32) * proj_ref[:, 2 * D:3 * D].astype(F32)
        rows = lax.broadcasted_iota(jnp.int32, (tm, D), 0)
        before = carry[...]
        v1, v2 = _shift_rows(v, before[7:8, :], before[6:7, :], rows)
        carry[...] = v[tm - 8:tm, :]
        conv = cw_ref[0:1, :] * v2 + cw_ref[1:2, :] * v1 + cw_ref[2:3, :] * v
        conv_ref[...] = conv.astype(BF16)
        _, sz = _silu_parts(proj_ref[:, 3 * D:4 * D].astype(F32))
        ya = (proj_ref[:, 0:D].astype(F32) * conv * sz).astype(BF16)
        ya_ref[...] = ya
        oa = _nn(ya, w_ref[...])
        oa_ref[...] = oa.astype(BF16)
        h1_ref[...] = x_ref[...] + oa * _rms_scale(oa) * g_ref[...]

    row = lambda i: (i, 0)
    fix = lambda i: (0, 0)
    return pl.pallas_call(
        body, name="a_mix", grid=(s // tm,),
        in_specs=[pl.BlockSpec((tm, 4 * D), row), pl.BlockSpec((tm, D), row), pl.BlockSpec((8, D), fix),
                  pl.BlockSpec((D, D), fix), pl.BlockSpec((1, D), fix)],
        out_specs=[pl.BlockSpec((tm, D), row)] * 4,
        out_shape=[SDS((s, D), BF16), SDS((s, D), BF16), SDS((s, D), F32), SDS((s, D), BF16)],
        scratch_shapes=[pltpu.VMEM((8, D), F32)],
        compiler_params=_params(("arbitrary",)),
    )(proj, x, conv_w, w_out, g_post)


def _b_in(h1, g_kv, g_pre, w_kv, wbin_g, tm):
    s = h1.shape[0]

    def body(h_ref, gk_ref, gb_ref, wkv_ref, wb_ref, kv_ref, q_ref, z_ref):
        h = h_ref[...]
        hh = h * _rms_scale(h)
        nk = (hh * gk_ref[...]).astype(BF16)
        nb = (hh * gb_ref[...]).astype(BF16)
        kv_ref[...] = _nn(nk, wkv_ref[...]).astype(BF16)
        for j in range(2):
            q_ref[:, BIN_COLS * j:BIN_COLS * (j + 1)] = (_nn(nb, wb_ref[j]) * Q_SCALE).astype(BF16)
            z_ref[:, BIN_COLS * j:BIN_COLS * (j + 1)] = _nn(nb, wb_ref[2 + j]).astype(BF16)

    row = lambda i: (i, 0)
    fix = lambda i: (0, 0)
    return pl.pallas_call(
        body, name="b_in", grid=(s // tm,),
        in_specs=[pl.BlockSpec((tm, D), row), pl.BlockSpec((1, D), fix), pl.BlockSpec((1, D), fix),
                  pl.BlockSpec((D, 2 * KV_W), fix), pl.BlockSpec((N_CHIPS, D, BIN_COLS), lambda i: (0, 0, 0))],
        out_specs=[pl.BlockSpec((tm, 2 * KV_W), row), pl.BlockSpec((tm, D), row), pl.BlockSpec((tm, D), row)],
        out_shape=[SDS((s, 2 * KV_W), BF16), SDS((s, D), BF16), SDS((s, D), BF16)],
        compiler_params=_params(("parallel",)),
    )(h1, g_kv, g_pre, w_kv, wbin_g)


def _buckets(dist):
    bucket = jnp.where(dist < MAX_EXACT, dist, MAX_EXACT)
    for t in BUCKET_THRESHOLDS:
        bucket = bucket + jnp.where(dist >= t, 1, 0)
    return bucket


def _head_place(h):
    kh, j, e = h // GROUP, (h % GROUP) // 2, h % 2
    return kh, slice(BLK * j, BLK * (j + 1)), slice(2 * BLK * e, 2 * BLK * (e + 1))


def _bias_table(rel_bias, sinks):
    def body(rb_ref, sink_ref, tab_ref):
        along = lax.broadcasted_iota(jnp.int32, (8, BLK), 1)
        bucket = _buckets(jnp.where(along == 0, 0, BLK - along))
        query = lax.broadcasted_iota(jnp.int32, (BLK, BLK), 0)
        col = lax.broadcasted_iota(jnp.int32, (BLK, BLK), 1)
        for h in range(N_HEADS):
            by_dist = jnp.zeros((8, BLK), F32)
            for b in range(N_BUCKETS):
                by_dist = jnp.where(bucket == b, rb_ref[b, h], by_dist)
            band = jnp.concatenate([by_dist] * (BLK // 8), axis=0)
            for digit in range(BLK.bit_length() - 1):
                band = jnp.where((query >> digit) & 1 == 1, pltpu.roll(band, 1 << digit, 1), band)
            cur = jnp.where(col <= query, band, NEG_INF)
            kh, rows, cols = _head_place(h)
            prev_cols, cur_cols = slice(cols.start, cols.start + BLK), slice(cols.start + BLK, cols.stop)
            tab_ref[1, kh, rows, prev_cols] = jnp.where(col == 0, sink_ref[h], jnp.where(col > query, band, NEG_INF))
            tab_ref[1, kh, rows, cur_cols] = cur
            tab_ref[0, kh, rows, prev_cols] = jnp.where(col == 0, sink_ref[h], NEG_INF)
            tab_ref[0, kh, rows, cur_cols] = cur

    return pl.pallas_call(
        body, name="bias_table", out_shape=SDS((2, N_KV, 4 * BLK, 4 * BLK), F32),
        in_specs=[pl.BlockSpec(memory_space=pltpu.SMEM), pl.BlockSpec(memory_space=pltpu.SMEM)],
        out_specs=pl.BlockSpec(memory_space=pltpu.VMEM),
    )(rel_bias, sinks)


def _bias_fold(dtab):
    def body(dtab_ref, out_ref, dsink_ref):
        bucket = _buckets(lax.broadcasted_iota(jnp.int32, (BLK, 128), 0))
        col = lax.broadcasted_iota(jnp.int32, (BLK, BLK), 1)
        row8 = lax.broadcasted_iota(jnp.int32, (8, 128), 0)
        lane8 = lax.broadcasted_iota(jnp.int32, (8, 128), 1)
        by_dist = jnp.zeros((BLK, 128), F32)
        dsink = jnp.zeros((8, 128), F32)
        for h in range(N_HEADS):
            kh, rows, cols = _head_place(h)
            dt = dtab_ref[kh, rows, cols]
            band = jnp.where(col == 0, 0.0, dt[:, 0:BLK]) + dt[:, BLK:2 * BLK]
            for digit in range(BLK.bit_length() - 1):
                band = jnp.where((col >> digit) & 1 == 1, pltpu.roll(band, BLK - (1 << digit), 0), band)
            by_dist = jnp.where(col == h, jnp.sum(band, axis=1, keepdims=True), by_dist)
            dsink = dsink + jnp.where((row8 == 0) & (lane8 == h), jnp.sum(dt[:, 0:1]), 0.0)
        for b in range(N_BUCKETS):
            out_ref[b:b + 1, :] = jnp.sum(jnp.where(bucket == b, by_dist, 0.0), axis=0, keepdims=True)
        dsink_ref[...] = dsink

    vm = pl.BlockSpec(memory_space=pltpu.VMEM)
    return pl.pallas_call(
        body, name="bias_fold", out_shape=[SDS((N_BUCKETS, 128), F32), SDS((8, 128), F32)],
        in_specs=[vm], out_specs=[vm, vm],
    )(dtab)


def _pair_operands(prev, cur):
    t = jnp.concatenate([prev, cur], axis=0).astype(F32)
    t = jnp.where(lax.broadcasted_iota(jnp.int32, t.shape, 0) == 0, 0.0, t)
    tr = pltpu.roll(t, HEAD_DIM, 1)
    lo = lax.broadcasted_iota(jnp.int32, t.shape, 1) < HEAD_DIM
    zero = jnp.zeros_like(t)
    head0 = jnp.concatenate([jnp.where(lo, t, zero), jnp.where(lo, zero, tr)], axis=0).astype(BF16)
    head1 = jnp.concatenate([jnp.where(lo, tr, zero), jnp.where(lo, zero, t)], axis=0).astype(BF16)
    return head0, head1


def _pair_fold(d0, d1):
    lo = lax.broadcasted_iota(jnp.int32, (2 * BLK, KV_W), 1) < HEAD_DIM
    zero = jnp.zeros((2 * BLK, KV_W), F32)
    g0 = jnp.where(lo, d0[0:256], zero) + pltpu.roll(jnp.where(lo, zero, d0[256:512]), HEAD_DIM, 1)
    g1 = pltpu.roll(jnp.where(lo, d1[0:256], zero), HEAD_DIM, 1) + jnp.where(lo, zero, d1[256:512])
    return jnp.where(lax.broadcasted_iota(jnp.int32, (2 * BLK, KV_W), 0) == 0, 0.0, g0 + g1)


def _stack_pairs(ref, kh):
    return jnp.concatenate([ref[:, 128 * (4 * kh + j):128 * (4 * kh + j + 1)] for j in range(4)], axis=0)


def _table_spec():
    return pl.BlockSpec((1, N_KV, 4 * BLK, 4 * BLK), lambda n: (jnp.minimum(n, 1), 0, 0, 0))


def _attn_fwd(q, kv, tab):
    s = q.shape[0]

    def body(q_ref, kp_ref, kc_ref, vp_ref, vc_ref, tab_ref, att_ref, stats_ref):
        k2 = _pair_operands(kp_ref[...], kc_ref[...])
        v2 = _pair_operands(vp_ref[...], vc_ref[...])
        lane = lax.broadcasted_iota(jnp.int32, (BLK, 128), 1)
        stats = jnp.zeros((BLK, 128), F32)
        for kh in range(N_KV):
            sc = _nt(_stack_pairs(q_ref, kh), k2[kh])
            ps = []
            for e in range(2):
                lg = sc[:, 256 * e:256 * (e + 1)] + tab_ref[0, kh, :, 256 * e:256 * (e + 1)]
                m = jnp.max(lg, axis=-1, keepdims=True)
                ex = jnp.exp(lg - m)
                den = jnp.sum(ex, axis=-1, keepdims=True)
                ps.append(ex * (1.0 / den))
                lse = m + jnp.log(den)
                for j in range(4):
                    stats = jnp.where(lane == GROUP * kh + 2 * j + e, lse[BLK * j:BLK * (j + 1)], stats)
            out = _nn(jnp.concatenate(ps, axis=1).astype(BF16), v2[kh])
            for j in range(4):
                att_ref[:, 128 * (4 * kh + j):128 * (4 * kh + j + 1)] = out[BLK * j:BLK * (j + 1)].astype(BF16)
        stats_ref[...] = stats

    cur = lambda n: (n, 0)
    prev = lambda n: (jnp.maximum(n - 1, 0), 0)
    return pl.pallas_call(
        body, name="attn_fwd", grid=(s // BLK,),
        in_specs=[pl.BlockSpec((BLK, D), cur),
                  pl.BlockSpec((BLK, KV_W), prev), pl.BlockSpec((BLK, KV_W), cur),
                  pl.BlockSpec((BLK, KV_W), lambda n: (jnp.maximum(n - 1, 0), 1)),
                  pl.BlockSpec((BLK, KV_W), lambda n: (n, 1)), _table_spec()],
        out_specs=[pl.BlockSpec((BLK, D), cur), pl.BlockSpec((BLK, 128), cur)],
        out_shape=[SDS((s, D), BF16), SDS((s, 128), F32)],
        compiler_params=_params(("parallel",)),
    )(q, kv, kv, kv, kv, tab)


def _mid(att, zb, h1, tgt, w_out, g_post, tm):
    s = att.shape[0]
    nt = s // tm

    def body(att_ref, z_ref, h1_ref, t_ref, w_ref, g_ref,
             dh_ref, dqz_ref, datt_ref, loss_ref, dg_ref, dw_ref, dw16_ref, dw_acc, stage, put_sem):
        @pl.when(pl.program_id(0) == 0)
        def _():
            loss_ref[...] = jnp.zeros_like(loss_ref)
            dg_ref[...] = jnp.zeros_like(dg_ref)
            dw_acc[...] = jnp.zeros_like(dw_acc)
        att = att_ref[...].astype(F32)
        z = z_ref[...].astype(F32)
        sg, sz = _silu_parts(z)
        ob = (att * sz).astype(BF16)
        y2 = _nn(ob, w_ref[...])
        r2 = _rms_scale(y2)
        yh = y2 * r2
        g = g_ref[...]
        err = (h1_ref[...] + yh * g) - t_ref[...]
        loss_ref[...] += jnp.sum(jnp.sum(err * err, axis=-1, keepdims=True) / D)
        dh = err / D
        dh_ref[...] = dh
        _acc_row(dg_ref, 0, jnp.sum(dh * yh, axis=0, keepdims=True))
        dyh = dh * g
        dy = (r2 * (dyh - yh * jnp.mean(dyh * yh, axis=-1, keepdims=True))).astype(BF16)
        dw_acc[...] += _tn(ob, dy)
        dob = _nt(dy, w_ref[...])
        datt_ref[...] = (dob * sz).astype(BF16)
        dqz_ref[...] = (dob * att * _dsilu(z, sg)).astype(BF16)

        @pl.when(pl.program_id(0) == nt - 1)
        def _():
            _write_gradient(dw_acc, dw_ref, dw16_ref, stage, put_sem)

    row = lambda i: (i, 0)
    fix = lambda i: (0, 0)
    anyspace = pl.BlockSpec(memory_space=pl.ANY)
    return pl.pallas_call(
        body, name="mid", grid=(nt,),
        in_specs=[pl.BlockSpec((tm, D), row)] * 4 + [pl.BlockSpec((D, D), fix), pl.BlockSpec((1, D), fix)],
        out_specs=[pl.BlockSpec((tm, D), row), pl.BlockSpec((tm, D), lambda i: (i, 1)), pl.BlockSpec((tm, D), row),
                   pl.BlockSpec((8, 128), fix), pl.BlockSpec((8, D), fix), anyspace, anyspace],
        out_shape=[SDS((s, D), F32), SDS((s, 2 * D), BF16), SDS((s, D), BF16), SDS((8, 128), F32),
                   SDS((8, D), F32), SDS((D, D), F32), SDS((D, D), BF16)],
        scratch_shapes=[pltpu.VMEM((D, D), F32), pltpu.VMEM((D // 4, D), BF16), pltpu.SemaphoreType.DMA],
        compiler_params=_params(("arbitrary",)),
    )(att, zb, h1, tgt, w_out, g_post)


def _attn_bwd(q, kv, datt, stats, tab, dqz):
    s = q.shape[0]
    nb = s // BLK

    def body(q_ref, kp_ref, kc_ref, vp_ref, vc_ref, da_ref, st_ref, tab_ref, dqz_in,
             dq_ref, dkv_ref, dtab_ref, dk_carry, dv_carry):
        del dqz_in
        n = pl.program_id(0)

        @pl.when(n == 0)
        def _():
            dtab_ref[...] = jnp.zeros_like(dtab_ref)
            dk_carry[...] = jnp.zeros_like(dk_carry)
            dv_carry[...] = jnp.zeros_like(dv_carry)

        @pl.when(n < nb)
        def _():
            k2 = _pair_operands(kp_ref[...], kc_ref[...])
            v2 = _pair_operands(vp_ref[...], vc_ref[...])
            lane = lax.broadcasted_iota(jnp.int32, (BLK, 128), 1)
            stats = st_ref[...]
            dk2, dv2 = [], []
            for kh in range(N_KV):
                qs = _stack_pairs(q_ref, kh)
                das = _stack_pairs(da_ref, kh)
                sc = _nt(qs, k2[kh])
                dp = _nt(das, v2[kh])
                ps, dss = [], []
                for e in range(2):
                    heads = [GROUP * kh + 2 * j + e for j in range(4)]
                    lse = jnp.concatenate([jnp.sum(jnp.where(lane == h, stats, 0.0), axis=-1, keepdims=True)
                                           for h in heads], axis=0)
                    cols = slice(256 * e, 256 * (e + 1))
                    p = jnp.exp(sc[:, cols] + tab_ref[0, kh, :, cols] - lse)
                    delta = jnp.sum(p * dp[:, cols], axis=-1, keepdims=True)
                    ds = p * (dp[:, cols] - delta)
                    dtab_ref[kh, :, cols] += ds
                    ps.append(p)
                    dss.append(ds)
                p2 = jnp.concatenate(ps, axis=1).astype(BF16)
                ds2 = jnp.concatenate(dss, axis=1).astype(BF16)
                dq = _nn(ds2, k2[kh]) * Q_SCALE
                for j in range(4):
                    dq_ref[:, 128 * (4 * kh + j):128 * (4 * kh + j + 1)] = dq[BLK * j:BLK * (j + 1)].astype(BF16)
                dk2.append(_tn(ds2, qs))
                dv2.append(_tn(p2, das))
            dkk = _pair_fold(dk2[0], dk2[1])
            dvv = _pair_fold(dv2[0], dv2[1])
            dkv_ref[:, 0:KV_W] = (dk_carry[...] + dkk[0:BLK]).astype(BF16)
            dkv_ref[:, KV_W:2 * KV_W] = (dv_carry[...] + dvv[0:BLK]).astype(BF16)
            dk_carry[...] = dkk[BLK:2 * BLK]
            dv_carry[...] = dvv[BLK:2 * BLK]

        @pl.when(n == nb)
        def _():
            dkv_ref[:, 0:KV_W] = dk_carry[...].astype(BF16)
            dkv_ref[:, KV_W:2 * KV_W] = dv_carry[...].astype(BF16)

    cur = lambda n: (jnp.minimum(n, nb - 1), 0)
    prev = lambda n: (jnp.clip(n - 1, 0, nb - 1), 0)
    return pl.pallas_call(
        body, name="attn_bwd", grid=(nb + 1,),
        in_specs=[pl.BlockSpec((BLK, D), cur),
                  pl.BlockSpec((BLK, KV_W), prev), pl.BlockSpec((BLK, KV_W), cur),
                  pl.BlockSpec((BLK, KV_W), lambda n: (jnp.clip(n - 1, 0, nb - 1), 1)),
                  pl.BlockSpec((BLK, KV_W), lambda n: (jnp.minimum(n, nb - 1), 1)),
                  pl.BlockSpec((BLK, D), cur), pl.BlockSpec((BLK, 128), cur), _table_spec(),
                  pl.BlockSpec(memory_space=pl.ANY)],
        out_specs=[pl.BlockSpec((BLK, D), cur), pl.BlockSpec((BLK, 2 * KV_W), prev),
                   pl.BlockSpec((N_KV, 4 * BLK, 4 * BLK), lambda n: (0, 0, 0))],
        out_shape=[SDS((s, 2 * D), BF16), SDS((s, 2 * KV_W), BF16), SDS((N_KV, 4 * BLK, 4 * BLK), F32)],
        scratch_shapes=[pltpu.VMEM((BLK, KV_W), F32), pltpu.VMEM((BLK, KV_W), F32)],
        input_output_aliases={8: 0},
        compiler_params=_params(("arbitrary",)),
    )(q, kv, kv, kv, kv, datt, stats, tab, dqz)


def _b_bwd(dqz, dkv, h1, dh2, oa, wbin_g, w_kv, g_kv, g_pre, g_apost, tm):
    s = h1.shape[0]
    nt = s // tm

    def body(dqz_ref, dkv_ref, h_ref, dh2_ref, oa_ref, wb_ref, wkv_ref, gk_ref, gb_ref, ga_ref,
             dh1_ref, doa_ref, dg_ref, dwb_ref, dwkv_ref, dwb16_ref, dwkv16_ref, wcat, dwb_acc, dwkv_acc, put_sem):
        @pl.when(pl.program_id(0) == 0)
        def _():
            dg_ref[...] = jnp.zeros_like(dg_ref)
            dwb_acc[...] = jnp.zeros_like(dwb_acc)
            dwkv_acc[...] = jnp.zeros_like(dwkv_acc)
            for j in range(N_CHIPS):
                pltpu.sync_copy(wb_ref.at[j], wcat.at[:, pl.ds(BIN_COLS * j, BIN_COLS)])
        dnb = _nt(dqz_ref[...], wcat[...])
        dnk = _nt(dkv_ref[...], wkv_ref[...])
        h = h_ref[...]
        r = _rms_scale(h)
        hh = h * r
        dwb_acc[...] += _tn((hh * gb_ref[...]).astype(BF16), dqz_ref[...])
        dwkv_acc[...] += _tn((hh * gk_ref[...]).astype(BF16), dkv_ref[...])
        _acc_row(dg_ref, 0, jnp.sum(dnk * hh, axis=0, keepdims=True))
        _acc_row(dg_ref, 1, jnp.sum(dnb * hh, axis=0, keepdims=True))
        dhh = dnb * gb_ref[...] + dnk * gk_ref[...]
        dh1 = dh2_ref[...] + r * (dhh - hh * jnp.mean(dhh * hh, axis=-1, keepdims=True))
        dh1_ref[...] = dh1
        oa = oa_ref[...].astype(F32)
        ra = _rms_scale(oa)
        oh = oa * ra
        _acc_row(dg_ref, 2, jnp.sum(dh1 * oh, axis=0, keepdims=True))
        doh = dh1 * ga_ref[...]
        doa_ref[...] = (ra * (doh - oh * jnp.mean(doh * oh, axis=-1, keepdims=True))).astype(BF16)

        @pl.when(pl.program_id(0) == nt - 1)
        def _():
            wcat[...] = dwb_acc[...].astype(BF16)
            puts = [pltpu.make_async_copy(dwkv_acc, dwkv_ref, put_sem.at[2 * N_CHIPS])]
            for j in range(N_CHIPS):
                cols = pl.ds(BIN_COLS * j, BIN_COLS)
                puts.append(pltpu.make_async_copy(dwb_acc.at[:, cols], dwb_ref.at[j], put_sem.at[2 * j]))
                puts.append(pltpu.make_async_copy(wcat.at[:, cols], dwb16_ref.at[j], put_sem.at[2 * j + 1]))
            for put in puts:
                put.start()
            for put in puts:
                put.wait()
            wcat[:, 0:2 * KV_W] = dwkv_acc[...].astype(BF16)
            pltpu.sync_copy(wcat.at[:, pl.ds(0, 2 * KV_W)], dwkv16_ref)

    row = lambda i: (i, 0)
    fix = lambda i: (0, 0)
    anyspace = pl.BlockSpec(memory_space=pl.ANY)
    return pl.pallas_call(
        body, name="b_bwd", grid=(nt,),
        in_specs=[pl.BlockSpec((tm, 2 * D), row), pl.BlockSpec((tm, 2 * KV_W), row), pl.BlockSpec((tm, D), row),
                  pl.BlockSpec((tm, D), row), pl.BlockSpec((tm, D), row), anyspace, pl.BlockSpec((D, 2 * KV_W), fix),
                  pl.BlockSpec((1, D), fix), pl.BlockSpec((1, D), fix), pl.BlockSpec((1, D), fix)],
        out_specs=[pl.BlockSpec((tm, D), row), pl.BlockSpec((tm, D), row), pl.BlockSpec((8, D), fix)] + [anyspace] * 4,
        out_shape=[SDS((s, D), F32), SDS((s, D), BF16), SDS((8, D), F32), SDS((N_CHIPS, D, BIN_COLS), F32),
                   SDS((D, 2 * KV_W), F32), SDS((N_CHIPS, D, BIN_COLS), BF16), SDS((D, 2 * KV_W), BF16)],
        scratch_shapes=[pltpu.VMEM((D, 2 * D), BF16), pltpu.VMEM((D, 2 * D), F32), pltpu.VMEM((D, 2 * KV_W), F32),
                        pltpu.SemaphoreType.DMA((2 * N_CHIPS + 1,))],
        compiler_params=_params(("arbitrary",)),
    )(dqz, dkv, h1, dh2, oa, wbin_g, w_kv, g_kv, g_pre, g_apost)


def _to_owner_core(pieces, r, send, recv, core, action):
    x, y, c = lax.axis_index("x"), lax.axis_index("y"), lax.axis_index("c")
    for kp in range(N_CHIPS):
        px, py = kp >> 1, kp & 1
        rel = 4 * (x + px - 2 * x * px) + 2 * (y + py - 2 * y * py) + (c + core - 2 * c * core)

        @pl.when(rel != 0)
        def _():
            cp = pltpu.make_async_remote_copy(src_ref=pieces.at[kp], dst_ref=r.at[rel - 1], send_sem=send.at[kp],
                                              recv_sem=recv.at[rel - 1], device_id=(px, py, core), device_id_type=MESH)
            if action == "start":
                cp.start()
            else:
                cp.wait_send()
    if action == "wait":
        @pl.when(c == core)
        def _():
            for rel in range(1, N_DEV):
                pltpu.make_async_remote_copy(src_ref=pieces.at[0], dst_ref=r.at[rel - 1], send_sem=send.at[0],
                                             recv_sem=recv.at[rel - 1], device_id=(x, y, c),
                                             device_id_type=MESH).wait_recv()


def _owner_core_sems():
    return [pltpu.SemaphoreType.DMA((N_CHIPS,)), pltpu.SemaphoreType.DMA((N_DEV - 1,))]


def _device_exchange(grads, recvs, send, recv):
    x, y, c = lax.axis_index("x"), lax.axis_index("y"), lax.axis_index("c")
    copies = []
    for a, (g, r) in enumerate(zip(grads, recvs)):
        h = g.shape[1] // 2
        for rel in range(1, N_DEV):
            fx, fy, fc = rel >> 2, (rel >> 1) & 1, rel & 1
            px, py, pc = x + fx - 2 * x * fx, y + fy - 2 * y * fy, c + fc - 2 * c * fc
            sem = (N_DEV - 1) * a + rel - 1
            copies.append(pltpu.make_async_remote_copy(
                src_ref=g.at[2 * px + py, pl.ds(pl.multiple_of(pc * h, 16), h)], dst_ref=r.at[rel - 1],
                send_sem=send.at[sem], recv_sem=recv.at[sem], device_id=(px, py, pc), device_id_type=MESH))
    return copies


def _device_exchange_specs(grads):
    anyspace = pl.BlockSpec(memory_space=pl.ANY)
    n = len(grads)
    count = (N_DEV - 1) * n
    return ([anyspace] * n, [anyspace] * n,
            [SDS((N_DEV - 1, g.shape[1] // 2, g.shape[2]), g.dtype) for g in grads],
            [pltpu.SemaphoreType.DMA((count,)), pltpu.SemaphoreType.DMA((count,))])


def _a_bwd(doa, ya, conv, proj, conv_w, w_out, tm, parts):
    s = doa.shape[0]
    nt = s // tm
    n = len(parts)
    ex_in, ex_out, ex_shape, ex_sems = _device_exchange_specs(parts)

    def body(*refs):
        doa_ref, ya_ref, conv_ref, proj_ref, cw_ref, w_ref = refs[:6]
        part_refs = refs[6:6 + n]
        dproj_ref, dcw_ref, dw_ref, dw16_ref = refs[6 + n:10 + n]
        recv_refs = refs[10 + n:10 + 2 * n]
        carry, dw_acc, stage, put_sem, send, recv = refs[10 + 2 * n:]
        i = pl.program_id(0)

        @pl.when(i == 0)
        def _():
            dcw_ref[...] = jnp.zeros_like(dcw_ref)
            carry[...] = jnp.zeros_like(carry)
            dw_acc[...] = jnp.zeros_like(dw_acc)
            for cp in _device_exchange(part_refs, recv_refs, send, recv):
                cp.start()
        dya = _nt(doa_ref[...], w_ref[...])
        dw_acc[...] += _tn(ya_ref[...], doa_ref[...])
        bg = proj_ref[:, 0:D].astype(F32)
        cg = proj_ref[:, D:2 * D].astype(F32)
        u = proj_ref[:, 2 * D:3 * D].astype(F32)
        z = proj_ref[:, 3 * D:4 * D].astype(F32)
        v = cg * u
        rows = lax.broadcasted_iota(jnp.int32, (tm, D), 0)
        conv = conv_ref[...].astype(F32)
        sg, sz = _silu_parts(z)
        dproj_ref[:, 0:D] = (dya * conv * sz).astype(BF16)
        dproj_ref[:, 3 * D:4 * D] = (dya * bg * conv * _dsilu(z, sg)).astype(BF16)
        dconv = dya * bg * sz
        after = carry[...]
        up1 = jnp.where(rows < tm - 1, pltpu.roll(dconv, tm - 1, 0), after[0:1, :])
        up2 = jnp.where(rows < tm - 2, pltpu.roll(dconv, tm - 2, 0),
                        jnp.where(rows == tm - 2, after[0:1, :], after[1:2, :]))
        carry[...] = dconv[0:8, :]
        _acc_row(dcw_ref, 0, jnp.sum(up2 * v, axis=0, keepdims=True))
        _acc_row(dcw_ref, 1, jnp.sum(up1 * v, axis=0, keepdims=True))
        _acc_row(dcw_ref, 2, jnp.sum(dconv * v, axis=0, keepdims=True))
        dv = cw_ref[2:3, :] * dconv + cw_ref[1:2, :] * up1 + cw_ref[0:1, :] * up2
        dproj_ref[:, D:2 * D] = (dv * u).astype(BF16)
        dproj_ref[:, 2 * D:3 * D] = (dv * cg).astype(BF16)

        @pl.when(i == nt - 1)
        def _():
            _write_gradient(dw_acc, dw_ref, dw16_ref, stage, put_sem)
            for cp in _device_exchange(part_refs, recv_refs, send, recv):
                cp.wait()

    rev = lambda i: (nt - 1 - i, 0)
    fix = lambda i: (0, 0)
    anyspace = pl.BlockSpec(memory_space=pl.ANY)
    dproj, dcw, dw, dw16, *got = pl.pallas_call(
        body, name="a_bwd", grid=(nt,),
        in_specs=[pl.BlockSpec((tm, D), rev), pl.BlockSpec((tm, D), rev), pl.BlockSpec((tm, D), rev),
                  pl.BlockSpec((tm, 4 * D), rev), pl.BlockSpec((8, D), fix), pl.BlockSpec((D, D), fix)] + ex_in,
        out_specs=[pl.BlockSpec((tm, 4 * D), rev), pl.BlockSpec((8, D), fix), anyspace, anyspace] + ex_out,
        out_shape=[SDS((s, 4 * D), BF16), SDS((8, D), F32), SDS((D, D), F32), SDS((D, D), BF16)] + ex_shape,
        scratch_shapes=[pltpu.VMEM((8, D), F32), pltpu.VMEM((D, D), F32), pltpu.VMEM((D // 4, D), BF16),
                        pltpu.SemaphoreType.DMA] + ex_sems,
        compiler_params=_params(("arbitrary",)),
    )(doa, ya, conv, proj, conv_w, w_out, *parts)
    return dproj, dcw, dw, dw16, got


def _dn1(dp_ref, w_ref):
    dn = _nt(dp_ref[:, 0:D], w_ref[0])
    for j in range(1, 4):
        dn = dn + _nt(dp_ref[:, D * j:D * (j + 1)], w_ref[j])
    return dn


def _a_in_bwd_matmul(dproj, win_g, tm, count, win_half, win_got):
    def body(dp_ref, w_ref, half_ref, got_in, dn_ref, got_ref, wcat, send, recv):
        del got_in

        @pl.when(pl.program_id(0) == 0)
        def _():
            _to_owner_core(half_ref, got_ref, send, recv, 1, "start")
            for j in range(N_CHIPS):
                pltpu.sync_copy(w_ref.at[j], wcat.at[:, pl.ds(D * j, D)])
        dn_ref[...] = _nt(dp_ref[...], wcat[...]).astype(BF16)

        @pl.when(pl.program_id(0) == count - 1)
        def _():
            _to_owner_core(half_ref, got_ref, send, recv, 1, "wait")

    row = lambda i: (i, 0)
    anyspace = pl.BlockSpec(memory_space=pl.ANY)
    return pl.pallas_call(
        body, name="a_in_bwd_matmul", grid=(count,),
        in_specs=[pl.BlockSpec((tm, 4 * D), row), anyspace, anyspace, anyspace],
        out_specs=[pl.BlockSpec((tm, D), row), anyspace],
        out_shape=[SDS((count * tm, D), BF16), SDS(win_got.shape, win_got.dtype)],
        scratch_shapes=[pltpu.VMEM((D, 4 * D), BF16)] + _owner_core_sems(),
        input_output_aliases={3: 1},
        compiler_params=_params(("arbitrary",)),
    )(dproj, win_g, win_half, win_got)


def _a_in_bwd(dn_first, dproj, x, dh1, win_g, g_pre, tm):
    s = x.shape[0]
    nt = s // tm
    count = dn_first.shape[0] // tm

    def body(dn_ref, dp_ref, x_ref, dh_ref, w_ref, g_ref, gx_ref, dg_ref, dn_s):
        i = pl.program_id(0)

        @pl.when(i == 0)
        def _():
            dg_ref[...] = jnp.zeros_like(dg_ref)

        @pl.when(i < count)
        def _():
            dn_s[...] = dn_ref[...].astype(F32)

        @pl.when(i >= count)
        def _():
            dn_s[...] = _dn1(dp_ref, w_ref)
        dn = dn_s[...]
        xv = x_ref[...]
        r = _rms_scale(xv)
        xh = xv * r
        _acc_row(dg_ref, 0, jnp.sum(dn * xh, axis=0, keepdims=True))
        dxh = dn * g_ref[...]
        gx_ref[...] = dh_ref[...] + r * (dxh - xh * jnp.mean(dxh * xh, axis=-1, keepdims=True))

    row = lambda i: (i, 0)
    fix = lambda i: (0, 0)
    return pl.pallas_call(
        body, name="a_in_bwd", grid=(nt,),
        in_specs=[pl.BlockSpec((tm, D), lambda i: (jnp.minimum(i, count - 1), 0)),
                  pl.BlockSpec((tm, 4 * D), lambda i: (jnp.maximum(i, count), 0)),
                  pl.BlockSpec((tm, D), row), pl.BlockSpec((tm, D), row),
                  pl.BlockSpec((4, D, D), lambda i: (0, 0, 0)), pl.BlockSpec((1, D), fix)],
        out_specs=[pl.BlockSpec((tm, D), row), pl.BlockSpec((8, D), fix)],
        out_shape=[SDS((s, D), F32), SDS((8, D), F32)],
        scratch_shapes=[pltpu.VMEM((tm, D), F32)],
        compiler_params=_params(("arbitrary",)),
    )(dn_first, dproj, x, dh1, win_g, g_pre)


def _swap_halves(shards, send, recv):
    x, y, c = lax.axis_index("x"), lax.axis_index("y"), lax.axis_index("c")
    sibling = (x, y, 1 - c)
    copies = []
    for b, full in enumerate(shards):
        h = full.shape[0] // 2
        mine = full.at[pl.ds(pl.multiple_of(c * h, 8), h)]
        theirs = full.at[pl.ds(pl.multiple_of((1 - c) * h, 8), h)]
        copies.append((pltpu.make_async_remote_copy(src_ref=mine, dst_ref=mine, send_sem=send.at[b], recv_sem=recv.at[b],
                                                    device_id=sibling, device_id_type=MESH),
                       pltpu.make_async_remote_copy(src_ref=mine, dst_ref=theirs, send_sem=send.at[b], recv_sem=recv.at[b],
                                                    device_id=sibling, device_id_type=MESH)))
    return copies


def _dw_in_half(n1, dproj, core, tmw, name, to_owners=None, to_devices=None, shards=()):
    s = n1.shape[0]
    h = D // 2
    nt = s // tmw
    n_sh = len(shards)
    if to_owners is not None:
        sent_array, sems, got_shape = to_owners, _owner_core_sems(), SDS((N_DEV - 1, h, D), BF16)
    else:
        sent_array = to_devices
        _, _, (got_shape,), sems = _device_exchange_specs([to_devices])

    def body(*refs):
        a_ref, b_ref, sent = refs[:3]
        o_ref, o16_ref, got = refs[3 + n_sh:6 + n_sh]
        shard_refs = refs[6 + n_sh:6 + 2 * n_sh]
        send, recv = refs[6 + 2 * n_sh:8 + 2 * n_sh]
        swap_sems = refs[8 + 2 * n_sh:]
        j, t = pl.program_id(0), pl.program_id(1)

        def exchange(action):
            if to_owners is not None:
                _to_owner_core(sent, got, send, recv, 1 - core, action)
            else:
                for cp in _device_exchange([sent], [got], send, recv):
                    cp.start() if action == "start" else cp.wait()

        @pl.when((j == 0) & (t == 0))
        def _():
            exchange("start")
            if n_sh:
                for mine, _ in _swap_halves(shard_refs, *swap_sems):
                    mine.start()

        @pl.when(t == 0)
        def _():
            o_ref[...] = jnp.zeros_like(o_ref)
        o_ref[0] += _tn(a_ref[...], b_ref[...])

        @pl.when(t == nt - 1)
        def _():
            o16_ref[...] = o_ref[...].astype(BF16)

        @pl.when((j == N_CHIPS - 1) & (t == nt - 1))
        def _():
            exchange("wait")
            if n_sh:
                for mine, theirs in _swap_halves(shard_refs, *swap_sems):
                    theirs.wait_recv()
                    mine.wait_send()

    anyspace = pl.BlockSpec(memory_space=pl.ANY)
    slot = pl.BlockSpec((1, h, D), lambda j, t: (j, 0, 0))
    swap_scratch = [pltpu.SemaphoreType.DMA((n_sh,)), pltpu.SemaphoreType.DMA((n_sh,))] if n_sh else []
    return pl.pallas_call(
        body, name=name, grid=(N_CHIPS, nt),
        in_specs=[pl.BlockSpec((tmw, h), lambda j, t: (t, core)), pl.BlockSpec((tmw, D), lambda j, t: (t, j))]
        + [anyspace] * (1 + n_sh),
        out_specs=[slot, slot] + [anyspace] * (1 + n_sh),
        out_shape=[SDS((N_CHIPS, h, D), F32), SDS((N_CHIPS, h, D), BF16), got_shape]
        + [SDS(sh.shape, F32) for sh in shards],
        scratch_shapes=sems + swap_scratch,
        input_output_aliases={3 + b: 3 + b for b in range(n_sh)},
        compiler_params=_params(("arbitrary", "arbitrary")),
    )(n1, dproj, sent_array, *shards)


def _share_and_gather(shards, smalls):
    n_h, n_s = len(shards), len(smalls)

    def body(*refs):
        small_ins = refs[n_h:n_h + n_s]
        fs = refs[n_h + n_s:2 * n_h + n_s]
        small_alls = refs[2 * n_h + n_s:2 * n_h + 2 * n_s]
        dsend, drecv, ssend, srecv = refs[2 * n_h + 2 * n_s:]
        x, y, c = lax.axis_index("x"), lax.axis_index("y"), lax.axis_index("c")
        swaps = _swap_halves(fs, dsend, drecv)
        sends, arrivals = [mine for mine, _ in swaps], [theirs for _, theirs in swaps]
        me = 4 * x + 2 * y + c
        for k, (small_in, small_all) in enumerate(zip(small_ins, small_alls)):
            small_all[me] = small_in[...]
            for rel in range(1, N_DEV):
                fx, fy, fc = rel >> 2, (rel >> 1) & 1, rel & 1
                peer = (x + fx - 2 * x * fx, y + fy - 2 * y * fy, c + fc - 2 * c * fc)
                sender = 4 * peer[0] + 2 * peer[1] + peer[2]
                sem = (N_DEV - 1) * k + rel - 1
                sends.append(pltpu.make_async_remote_copy(
                    src_ref=small_in, dst_ref=small_all.at[me], send_sem=ssend.at[sem], recv_sem=srecv.at[sem],
                    device_id=peer, device_id_type=MESH))
                arrivals.append(pltpu.make_async_remote_copy(
                    src_ref=small_in, dst_ref=small_all.at[sender], send_sem=ssend.at[sem], recv_sem=srecv.at[sem],
                    device_id=peer, device_id_type=MESH))
        for cp in sends:
            cp.start()
        for cp in arrivals:
            cp.wait_recv()
        for cp in sends:
            cp.wait_send()

    anyspace = pl.BlockSpec(memory_space=pl.ANY)
    vm = pl.BlockSpec(memory_space=pltpu.VMEM)
    out_shape = [SDS(full.shape, F32) for full in shards] + [SDS((N_DEV,) + sm.shape, F32) for sm in smalls]
    n_all = (N_DEV - 1) * n_s
    outs = pl.pallas_call(
        body, name="share_and_gather", out_shape=out_shape,
        in_specs=[anyspace] * n_h + [vm] * n_s, out_specs=[anyspace] * n_h + [vm] * n_s,
        scratch_shapes=[pltpu.SemaphoreType.DMA((n_h,)), pltpu.SemaphoreType.DMA((n_h,)),
                        pltpu.SemaphoreType.DMA((n_all,)), pltpu.SemaphoreType.DMA((n_all,))],
        input_output_aliases={b: b for b in range(n_h)},
    )(*shards, *smalls)
    return outs[:n_h], outs[n_h:]


def _add_win(where, lo, hi, r, name):
    _, h, cols = lo.shape
    tr = min(h, 256)
    nh = h // tr

    def body(where_ref, lo_ref, hi_ref, r_ref, o_ref):
        acc = jnp.where(where_ref[0] == 0, lo_ref[0], hi_ref[0])
        for k in range(N_DEV - 1):
            acc = acc + r_ref[k].astype(F32)
        o_ref[...] = acc

    own = pl.BlockSpec((1, tr, cols), lambda i, w: (w[1], i, 0))
    return pl.pallas_call(
        body, name=name,
        grid_spec=pltpu.PrefetchScalarGridSpec(
            num_scalar_prefetch=1, grid=(nh,),
            in_specs=[own, own, pl.BlockSpec((N_DEV - 1, tr, cols), lambda i, w: (0, i, 0))],
            out_specs=pl.BlockSpec((tr, cols), lambda i, w: (w[0] * nh + i, 0))),
        out_shape=SDS((2 * h, cols), F32),
        compiler_params=_params(("parallel",)),
    )(where, lo, hi, r)


def _add_devices(where, g, r, name):
    _, rows, cols = g.shape
    h = rows // 2
    tr = min(h, 256)
    nh = h // tr

    def body(where_ref, g_ref, r_ref, o_ref):
        del where_ref
        acc = g_ref[0]
        for k in range(N_DEV - 1):
            acc = acc + r_ref[k].astype(F32)
        o_ref[...] = acc

    return pl.pallas_call(
        body, name=name,
        grid_spec=pltpu.PrefetchScalarGridSpec(
            num_scalar_prefetch=1, grid=(nh,),
            in_specs=[pl.BlockSpec((1, tr, cols), lambda i, w: (w[1], w[0] * nh + i, 0)),
                      pl.BlockSpec((N_DEV - 1, tr, cols), lambda i, w: (0, i, 0))],
            out_specs=pl.BlockSpec((tr, cols), lambda i, w: (w[0] * nh + i, 0))),
        out_shape=SDS((rows, cols), F32),
        compiler_params=_params(("parallel",)),
    )(where, g, r)


def _sum_smalls(gathered):
    n = len(gathered)

    def body(*refs):
        for all_ref, o_ref in zip(refs[:n], refs[n:]):
            acc = all_ref[0]
            for dev in range(1, N_DEV):
                acc = acc + all_ref[dev]
            o_ref[...] = acc

    vm = pl.BlockSpec(memory_space=pltpu.VMEM)
    return pl.pallas_call(
        body, name="sum_smalls", out_shape=[SDS(a.shape[1:], F32) for a in gathered],
        in_specs=[vm] * n, out_specs=[vm] * n,
    )(*gathered)


def _adam_step(g, w, m, v):
    nm = ADAM_B1 * m + (1.0 - ADAM_B1) * g
    nv = ADAM_B2 * v + (1.0 - ADAM_B2) * (g * g)
    m_hat = nm / (1.0 - ADAM_B1 ** ADAM_STEP)
    v_hat = nv / (1.0 - ADAM_B2 ** ADAM_STEP)
    return -ADAM_LR * (m_hat / (jnp.sqrt(v_hat) + ADAM_EPS) + ADAM_WD * w), nm, nv


def _adamw(g, w, m, v, name):
    rows, cols = g.shape
    tr = min(rows, 256)

    def body(g_ref, w_ref, m_ref, v_ref, d_ref, nm_ref, nv_ref):
        d_ref[...], nm_ref[...], nv_ref[...] = _adam_step(g_ref[...], w_ref[...], m_ref[...], v_ref[...])

    spec = pl.BlockSpec((tr, cols), lambda i: (i, 0))
    return pl.pallas_call(
        body, name=name, grid=(rows // tr,), in_specs=[spec] * 4, out_specs=[spec] * 3,
        out_shape=[SDS(g.shape, F32)] * 3, compiler_params=_params(("parallel",)),
    )(g, w, m, v)


def _small_update(chip, tot, tot_rel, wmv):
    names = list(SMALL_PLACES)
    n = len(names)

    def body(chip_ref, tot_ref, quarter_ref, rel_ref, *refs):
        del chip_ref
        ins, outs = refs[:3 * n], refs[3 * n:]
        for i, nm in enumerate(names):
            source, row, (rows, cols) = SMALL_PLACES[nm]
            g = {"rows": tot_ref, "quarter": quarter_ref, "rel": rel_ref}[source][row:row + rows, 0:cols]
            outs[4 * i][...] = g
            outs[4 * i + 1][...], outs[4 * i + 2][...], outs[4 * i + 3][...] = _adam_step(
                g, ins[3 * i][...], ins[3 * i + 1][...], ins[3 * i + 2][...])

    whole = lambda shape: pl.BlockSpec(shape, lambda i, c: (0,) * len(shape))
    shapes = [SMALL_PLACES[nm][2] for nm in names]
    outs = pl.pallas_call(
        body, name="small_update",
        grid_spec=pltpu.PrefetchScalarGridSpec(
            num_scalar_prefetch=1, grid=(1,),
            in_specs=[whole(tot.shape), pl.BlockSpec((tot.shape[0], D // 4), lambda i, c: (0, c[0])),
                      whole(tot_rel.shape)] + [whole(shp) for shp in shapes for _ in range(3)],
            out_specs=[whole(shp) for shp in shapes for _ in range(4)]),
        out_shape=[SDS(shp, F32) for shp in shapes for _ in range(4)],
    )(chip, tot, tot, tot_rel, *[a for nm in names for a in wmv[nm]])
    return {nm: tuple(outs[4 * i:4 * i + 4]) for i, nm in enumerate(names)}


def _pad_rows(a, rows):
    return jnp.concatenate([a, jnp.zeros((rows - a.shape[0], a.shape[1]), a.dtype)], axis=0)


def _pad_cols(a, cols):
    return jnp.concatenate([a, jnp.zeros((a.shape[0], cols - a.shape[1]), a.dtype)], axis=1)


def kernel(x, a_pre_norm, a_w_in, a_conv_w, a_w_out, a_post_norm, kv_norm, w_kv, rel_bias, b_pre_norm, b_w_in, b_sinks, b_w_out, b_post_norm, loss_target, m_a_pre_norm, m_a_w_in, m_a_conv_w, m_a_w_out, m_a_post_norm, m_kv_norm, m_w_kv, m_rel_bias, m_b_pre_norm, m_b_w_in, m_b_sinks, m_b_w_out, m_b_post_norm, v_a_pre_norm, v_a_w_in, v_a_conv_w, v_a_w_out, v_a_post_norm, v_kv_norm, v_w_kv, v_rel_bias, v_b_pre_norm, v_b_w_in, v_b_sinks, v_b_w_out, v_b_post_norm):
    seq = x.shape[1]
    xs = x.reshape(seq, D)
    tgt = loss_target.reshape(seq, D)
    chip = 2 * lax.axis_index("x") + lax.axis_index("y")
    core = lax.axis_index("c")
    tm = _tile(seq, 512)
    tmw = _tile(seq, 1024)

    shards = [a_w_in[0], a_w_out[0], w_kv, b_w_in[0], b_w_out[0]]
    small_w = _pad_rows(jnp.concatenate([a_pre_norm, a_conv_w[0], a_post_norm], axis=0), 8)
    *own_only, small_g = _prepare_weights(shards, small_w)
    where = jnp.stack([core, chip]).astype(jnp.int32)
    small_full = small_g.transpose(1, 0, 2).reshape(8, D)
    g_apre, conv_w, g_apost = small_full[0:1], _pad_rows(small_full[1:4], 8), small_full[4:5]
    g_kv = kv_norm.reshape(1, D)

    proj, n1, (win_g, wouta_g, wkv_g, wbin_g, woutb_g) = _a_in(where[1:2], xs, g_apre, own_only, tmw)
    wouta = wouta_g.reshape(D, D)
    wkv = wkv_g.reshape(D, 2 * KV_W)
    woutb = woutb_g.reshape(D, D)
    ya, oa, h1, conv = _a_mix(proj, xs, conv_w, wouta, g_apost, tm)
    kv, q, zb = _b_in(h1, g_kv, b_pre_norm, wkv, wbin_g, tmw)
    tab = _bias_table(rel_bias, b_sinks.reshape(N_HEADS))
    att, stats = _attn_fwd(q, kv, tab)
    dh2, dqz, datt, loss_acc, dg_bpost, dw_outb, dw_outb16 = _mid(att, zb, h1, tgt, woutb, b_post_norm, tm)

    dqz, dkv, dtab = _attn_bwd(q, kv, datt, stats, tab, dqz)
    dh1, doa, dg_b, dw_bin, dw_kv, dw_bin16, dw_kv16 = _b_bwd(dqz, dkv, h1, dh2, oa, wbin_g, wkv, g_kv, b_pre_norm,
                                                              g_apost, tm)
    by_chip = lambda a, cols: a.reshape(N_CHIPS, D // 4, cols)
    grads1 = [by_chip(dw_kv, 2 * KV_W), dw_bin, by_chip(dw_outb, D)]
    sent1 = [by_chip(dw_kv16, 2 * KV_W), dw_bin16, by_chip(dw_outb16, D)]
    names1 = ["w_kv", "b_w_in", "b_w_out"]
    dproj, dconv_w, dw_outa, dw_outa16, from_devices1 = _a_bwd(doa, ya, conv, proj, conv_w, wouta, tm, sent1)
    shards1 = [_add_devices(where, g, r, "add_devices_" + nm) for g, r, nm in zip(grads1, from_devices1, names1)]
    tmw2 = _tile(seq, 4096)
    win_lo, win_lo16, outa_got, g_wkv, g_wbin, g_woutb = _dw_in_half(
        n1, dproj, 0, tmw2, "dw_a_in_lo", to_devices=by_chip(dw_outa16, D), shards=shards1)
    win_hi, win_hi16, win_got = _dw_in_half(n1, dproj, 1, tmw2, "dw_a_in_hi", to_owners=win_lo16)
    nt = seq // tmw
    dn_first, win_got = _a_in_bwd_matmul(dproj, win_g, tmw, max(nt - max(nt // 4, 1), 1), win_hi16, win_got)
    grad_x, dg_apre = _a_in_bwd(dn_first, dproj, xs, dh1, win_g, g_apre, tm)
    shards2 = [_add_win(where, win_lo, win_hi, win_got, "add_devices_a_w_in"),
               _add_devices(where, by_chip(dw_outa, D), outa_got, "add_devices_a_w_out")]
    drel, dsink = _bias_fold(dtab)

    smalls = jnp.concatenate([
        dg_apre[0:1], dg_b[2:3], dg_b[0:1], dg_b[1:2], dg_bpost[0:1], _pad_cols(dsink[0:1], D),
        _pad_cols(loss_acc[0:1], D), jnp.zeros((1, D), F32), dconv_w], axis=0)
    assert smalls.shape == (SMALL_ROWS, D)
    (g_win, g_wouta), gathered = _share_and_gather(shards2, (smalls, drel))
    tot, tot_rel = _sum_smalls(gathered)

    big = {}
    for nm, g, w, m, v in [("a_w_in", g_win, a_w_in, m_a_w_in, v_a_w_in), ("a_w_out", g_wouta, a_w_out, m_a_w_out, v_a_w_out),
                           ("w_kv", g_wkv, w_kv, m_w_kv, v_w_kv), ("b_w_in", g_wbin, b_w_in, m_b_w_in, v_b_w_in),
                           ("b_w_out", g_woutb, b_w_out, m_b_w_out, v_b_w_out)]:
        shp = w.shape
        two = (shp[-2], shp[-1])
        d, nm_, nv_ = _adamw(g, w.reshape(two), m.reshape(two), v.reshape(two), "adamw_" + nm)
        big[nm] = (g.reshape(shp), d.reshape(shp), nm_.reshape(shp), nv_.reshape(shp))

    given = {"a_pre_norm": (a_pre_norm, m_a_pre_norm, v_a_pre_norm), "a_conv_w": (a_conv_w, m_a_conv_w, v_a_conv_w),
             "a_post_norm": (a_post_norm, m_a_post_norm, v_a_post_norm), "kv_norm": (kv_norm, m_kv_norm, v_kv_norm),
             "rel_bias": (rel_bias, m_rel_bias, v_rel_bias), "b_pre_norm": (b_pre_norm, m_b_pre_norm, v_b_pre_norm),
             "b_sinks": (b_sinks, m_b_sinks, v_b_sinks), "b_post_norm": (b_post_norm, m_b_post_norm, v_b_post_norm)}
    small = _small_update(where[1:2], tot, tot_rel, {nm: tuple(a.reshape(SMALL_PLACES[nm][2]) for a in wmv)
                                            for nm, wmv in given.items()})
    order = ["a_pre_norm", "a_w_in", "a_conv_w", "a_w_out", "a_post_norm", "kv_norm", "w_kv", "rel_bias",
             "b_pre_norm", "b_w_in", "b_sinks", "b_w_out", "b_post_norm"]
    outs = []
    for which in range(4):
        for nm in order:
            outs.append(big[nm][which] if nm in big else small[nm][which].reshape(given[nm][0].shape))
    loss = 0.5 * tot[LOSS_ROW, 0]
    return (loss, grad_x.reshape(x.shape), *outs)
```

```python
import math

import jax
import jax.numpy as jnp
from jax import lax
from jax.experimental import pallas as pl
from jax.experimental.pallas import tpu as pltpu

F32 = jnp.float32
BF16 = jnp.bfloat16
MESH = pl.DeviceIdType.MESH
SDS = jax.ShapeDtypeStruct

D = 1024
HEAD_DIM = 64
N_HEADS = 16
N_KV = 2
GROUP = 8
KV_W = 128
BLK = 128
N_BUCKETS = 32
MAX_EXACT = 16
MAX_DISTANCE = 128
EPS = 1e-6
NEG_INF = -1e30
Q_SCALE = HEAD_DIM ** -0.5

ADAM_LR = 0.001
ADAM_B1 = 0.9
ADAM_B2 = 0.999
ADAM_EPS = 1e-08
ADAM_WD = 0.01
ADAM_STEP = 10

N_CHIPS = 4
N_DEV = 8
BIN_COLS = 2 * D // N_CHIPS
VMEM_LIMIT = 56 * 1024 * 1024
SMALL_ROWS = 16
LOSS_ROW = 6
SMALL_PLACES = {
    "a_pre_norm": ("quarter", 0, (1, D // 4)), "a_conv_w": ("quarter", 8, (3, D // 4)),
    "a_post_norm": ("quarter", 1, (1, D // 4)), "kv_norm": ("rows", 2, (D,)),
    "rel_bias": ("rel", 0, (N_BUCKETS, N_HEADS)), "b_pre_norm": ("rows", 3, (1, D)),
    "b_sinks": ("rows", 5, (1, N_HEADS)), "b_post_norm": ("rows", 4, (1, D)),
}


def _bucket_thresholds():
    def bucket(d):
        big = MAX_EXACT + int(math.log(d / MAX_EXACT) / math.log(MAX_DISTANCE / MAX_EXACT)
                              * (N_BUCKETS - MAX_EXACT))
        return d if d < MAX_EXACT else min(big, N_BUCKETS - 1)
    out = []
    for b in range(MAX_EXACT + 1, N_BUCKETS):
        out.append(min(d for d in range(MAX_EXACT, MAX_DISTANCE) if bucket(d) >= b))
    return tuple(out)


BUCKET_THRESHOLDS = _bucket_thresholds()


def _params(semantics=None, vmem=VMEM_LIMIT):
    return pltpu.CompilerParams(dimension_semantics=semantics, vmem_limit_bytes=vmem)


def _tile(n, pref):
    return pref if n >= 2 * pref else max(n // 2, 8)


def _rms_scale(v):
    return lax.rsqrt(jnp.mean(v * v, axis=-1, keepdims=True) + EPS)


def _nt(a, b):
    return lax.dot_general(a, b, (((1,), (1,)), ((), ())), preferred_element_type=F32)


def _tn(a, b):
    return lax.dot_general(a, b, (((0,), (0,)), ((), ())), preferred_element_type=F32)


def _nn(a, b):
    return jnp.dot(a, b, preferred_element_type=F32)


def _silu_parts(z):
    sg = jax.nn.sigmoid(z)
    return sg, z * sg


def _dsilu(z, sg):
    return sg * (1.0 + z * (1.0 - sg))


def _write_gradient(acc, out32, out16, stage, sem):
    whole = pltpu.make_async_copy(acc, out32, sem)
    whole.start()
    rows = stage.shape[0]
    for k in range(acc.shape[0] // rows):
        stage[...] = acc[rows * k:rows * (k + 1), :].astype(BF16)
        pltpu.sync_copy(stage, out16.at[pl.ds(rows * k, rows)])
    whole.wait()


def _acc_row(ref, row, val):
    ref[row:row + 1, :] += val


def _gather_copies(outs, splits, ici_send, ici_recv, d2d_send, d2d_recv):
    x, y, c = lax.axis_index("x"), lax.axis_index("y"), lax.axis_index("c")
    k = 2 * x + y
    sibling = (x, y, 1 - c)

    def part(o_ref, chip, core, split):
        if not split:
            return o_ref.at[chip]
        h = o_ref.shape[1] // 2
        return o_ref.at[chip, pl.ds(pl.multiple_of(core * h, 16), h)]

    def remote(ref, a, j, sems, to):
        return pltpu.make_async_remote_copy(src_ref=ref, dst_ref=ref, send_sem=sems[0].at[3 * a + j],
                                            recv_sem=sems[1].at[3 * a + j], device_id=to, device_id_type=MESH)

    copies = []
    for a, (o_ref, split) in enumerate(zip(outs, splits)):
        for j, (px, py) in enumerate([(x, 1 - y), (1 - x, y), (1 - x, 1 - y)]):
            kj = 2 * px + py
            ici, d2d = (ici_send, ici_recv), (d2d_send, d2d_recv)
            copies.append((remote(part(o_ref, k, c, split), a, j, ici, (px, py, c)),
                           remote(part(o_ref, kj, c, split), a, j, ici, (px, py, c)),
                           remote(part(o_ref, kj, c, split), a, j, d2d, sibling) if split else None,
                           remote(part(o_ref, kj, 1 - c, split), a, j, d2d, sibling) if split else None))
    return copies


def _gather_sems(n):
    return [pltpu.SemaphoreType.DMA((3 * n,)) for _ in range(4)]


def _prepare_weights(shards, small):
    n = len(shards)

    def body(*refs):
        ins, small_in = refs[:n], refs[n]
        outs, small_out = refs[n + 1:2 * n + 1], refs[2 * n + 1]
        stages, put_sem = refs[2 * n + 2:3 * n + 2], refs[3 * n + 2]
        sems = refs[3 * n + 3:]
        k = 2 * lax.axis_index("x") + lax.axis_index("y")
        puts = []
        for a, (i_ref, stage, o_ref) in enumerate(zip(ins, stages, outs)):
            stage[...] = i_ref[...].astype(BF16)
            puts.append(pltpu.make_async_copy(stage, o_ref.at[k], put_sem.at[a]))
            puts[-1].start()
        small_out[k] = small_in[...]
        copies = _gather_copies([small_out], [False], *sems)
        for send, _, _, _ in copies:
            send.start()
        for _, arrival, _, _ in copies:
            arrival.wait_recv()
        for send, _, _, _ in copies:
            send.wait_send()
        for put in puts:
            put.wait()

    vm = pl.BlockSpec(memory_space=pltpu.VMEM)
    anyspace = pl.BlockSpec(memory_space=pl.ANY)
    out_shape = [SDS((N_CHIPS,) + s.shape, BF16) for s in shards] + [SDS((N_CHIPS,) + small.shape, F32)]
    return pl.pallas_call(
        body, name="prepare_weights", out_shape=out_shape,
        in_specs=[vm] * (n + 1), out_specs=[anyspace] * n + [vm],
        scratch_shapes=[pltpu.VMEM(s.shape, BF16) for s in shards] + [pltpu.SemaphoreType.DMA((n,))] + _gather_sems(1),
        compiler_params=pltpu.CompilerParams(vmem_limit_bytes=VMEM_LIMIT),
    )(*shards, small)


def _a_in(chip, x, g_pre, weights, tm):
    s = x.shape[0]
    nt = s // tm
    n = len(weights)

    def body(chip_ref, x_ref, g_ref, *refs):
        proj_ref, n1_ref = refs[n:n + 2]
        gathered = refs[n + 2:2 * n + 2]
        wbuf, n1_all, fetch_sem = refs[2 * n + 2:2 * n + 5]
        sems = refs[2 * n + 5:]
        jj, i = pl.program_id(0), pl.program_id(1)
        copies = _gather_copies(gathered, [True] * n, *sems)

        def fetch(rel):
            slot = jnp.bitwise_xor(chip_ref[0], rel)
            return pltpu.make_async_copy(gathered[0].at[slot], wbuf.at[rel % 2], fetch_sem.at[rel % 2])

        @pl.when((jj == 0) & (i == 0))
        def _():
            fetch(0).start()
            copies[0][0].start()
            copies[1][0].start()
            fetch(0).wait()

        for rel in (1, 2, 3):
            @pl.when((jj == rel) & (i == 0))
            def _():
                fetch(rel).wait()

        @pl.when(jj == 0)
        def _():
            xv = x_ref[...]
            n1 = (xv * _rms_scale(xv) * g_ref[...]).astype(BF16)
            n1_ref[...] = n1
            n1_all[i] = n1
        proj_ref[...] = _nn(n1_all[i], wbuf[jj % 2]).astype(BF16)

        for rel in (1, 2, 3):
            @pl.when((jj == rel - 1) & (i == max(nt - 3, 0)))
            def _():
                _, arrival, forward, _ = copies[rel - 1]
                arrival.wait_recv()
                forward.start()
                if rel == 1:
                    for send, _, _, _ in copies[2:]:
                        send.start()

            @pl.when((jj == rel - 1) & (i == max(nt - 2, 0)))
            def _():
                copies[rel - 1][3].wait_recv()
                fetch(rel).start()

        @pl.when((jj == 3) & (i == max(nt - 2, 0)))
        def _():
            for _, arrival, forward, _ in copies[3:]:
                arrival.wait_recv()
                forward.start()

        @pl.when((jj == 3) & (i == nt - 1))
        def _():
            for _, _, _, forwarded in copies[3:]:
                forwarded.wait_recv()
            for send, _, forward, _ in copies:
                forward.wait_send()
                send.wait_send()

    anyspace = pl.BlockSpec(memory_space=pl.ANY)
    proj, n1, *gathered = pl.pallas_call(
        body, name="a_in",
        grid_spec=pltpu.PrefetchScalarGridSpec(
            num_scalar_prefetch=1, grid=(4, nt),
            in_specs=[pl.BlockSpec((tm, D), lambda jj, i, c: (jnp.where(jj == 0, i, nt - 1), 0)),
                      pl.BlockSpec((1, D), lambda jj, i, c: (0, 0))] + [anyspace] * n,
            out_specs=[pl.BlockSpec((tm, D), lambda jj, i, c: (i, jnp.bitwise_xor(c[0], jj))),
                       pl.BlockSpec((tm, D), lambda jj, i, c: (jnp.where(jj == 0, i, nt - 1), 0))] + [anyspace] * n,
            scratch_shapes=[pltpu.VMEM((2, D, D), BF16), pltpu.VMEM((nt, tm, D), BF16),
                            pltpu.SemaphoreType.DMA((2,))] + _gather_sems(n)),
        out_shape=[SDS((s, 4 * D), BF16), SDS((s, D), BF16)] + [SDS(w.shape, w.dtype) for w in weights],
        input_output_aliases={3 + a: 2 + a for a in range(n)},
        compiler_params=_params(("arbitrary", "arbitrary")),
    )(chip, x, g_pre, *weights)
    return proj, n1, gathered


def _shift_rows(v, last, second_last, rows):
    v1 = jnp.where(rows >= 1, pltpu.roll(v, 1, 0), last)
    v2 = jnp.where(rows >= 2, pltpu.roll(v, 2, 0), jnp.where(rows == 1, last, second_last))
    return v1, v2


def _a_mix(proj, x, conv_w, w_out, g_post, tm):
    s = x.shape[0]

    def body(proj_ref, x_ref, cw_ref, w_ref, g_ref, ya_ref, oa_ref, h1_ref, conv_ref, carry):
        @pl.when(pl.program_id(0) == 0)
        def _():
            carry[...] = jnp.zeros_like(carry)
        v = proj_ref[:, D:2 * D].astype(F32) * proj_ref[:, 2 * D:3 * D].astype(F32)
        rows = lax.broadcasted_iota(jnp.int32, (tm, D), 0)
        before = carry[...]
        v1, v2 = _shift_rows(v, before[7:8, :], before[6:7, :], rows)
        carry[...] = v[tm - 8:tm, :]
        conv = cw_ref[0:1, :] * v2 + cw_ref[1:2, :] * v1 + cw_ref[2:3, :] * v
        conv_ref[...] = conv.astype(BF16)
        _, sz = _silu_parts(proj_ref[:, 3 * D:4 * D].astype(F32))
        ya = (proj_ref[:, 0:D].astype(F32) * conv * sz).astype(BF16)
        ya_ref[...] = ya
        oa = _nn(ya, w_ref[...])
        oa_ref[...] = oa.astype(BF16)
        h1_ref[...] = x_ref[...] + oa * _rms_scale(oa) * g_ref[...]

    row = lambda i: (i, 0)
    fix = lambda i: (0, 0)
    return pl.pallas_call(
        body, name="a_mix", grid=(s // tm,),
        in_specs=[pl.BlockSpec((tm, 4 * D), row), pl.BlockSpec((tm, D), row), pl.BlockSpec((8, D), fix),
                  pl.BlockSpec((D, D), fix), pl.BlockSpec((1, D), fix)],
        out_specs=[pl.BlockSpec((tm, D), row)] * 4,
        out_shape=[SDS((s, D), BF16), SDS((s, D), BF16), SDS((s, D), F32), SDS((s, D), BF16)],
        scratch_shapes=[pltpu.VMEM((8, D), F32)],
        compiler_params=_params(("arbitrary",)),
    )(proj, x, conv_w, w_out, g_post)


def _b_in(h1, g_kv, g_pre, w_kv, wbin_g, tm):
    s = h1.shape[0]

    def body(h_ref, gk_ref, gb_ref, wkv_ref, wb_ref, kv_ref, q_ref, z_ref):
        h = h_ref[...]
        hh = h * _rms_scale(h)
        nk = (hh * gk_ref[...]).astype(BF16)
        nb = (hh * gb_ref[...]).astype(BF16)
        kv_ref[...] = _nn(nk, wkv_ref[...]).astype(BF16)
        for j in range(2):
            q_ref[:, BIN_COLS * j:BIN_COLS * (j + 1)] = (_nn(nb, wb_ref[j]) * Q_SCALE).astype(BF16)
            z_ref[:, BIN_COLS * j:BIN_COLS * (j + 1)] = _nn(nb, wb_ref[2 + j]).astype(BF16)

    row = lambda i: (i, 0)
    fix = lambda i: (0, 0)
    return pl.pallas_call(
        body, name="b_in", grid=(s // tm,),
        in_specs=[pl.BlockSpec((tm, D), row), pl.BlockSpec((1, D), fix), pl.BlockSpec((1, D), fix),
                  pl.BlockSpec((D, 2 * KV_W), fix), pl.BlockSpec((N_CHIPS, D, BIN_COLS), lambda i: (0, 0, 0))],
        out_specs=[pl.BlockSpec((tm, 2 * KV_W), row), pl.BlockSpec((tm, D), row), pl.BlockSpec((tm, D), row)],
        out_shape=[SDS((s, 2 * KV_W), BF16), SDS((s, D), BF16), SDS((s, D), BF16)],
        compiler_params=_params(("parallel",)),
    )(h1, g_kv, g_pre, w_kv, wbin_g)


def _buckets(dist):
    bucket = jnp.where(dist < MAX_EXACT, dist, MAX_EXACT)
    for t in BUCKET_THRESHOLDS:
        bucket = bucket + jnp.where(dist >= t, 1, 0)
    return bucket


def _head_place(h):
    kh, j, e = h // GROUP, (h % GROUP) // 2, h % 2
    return kh, slice(BLK * j, BLK * (j + 1)), slice(2 * BLK * e, 2 * BLK * (e + 1))


def _bias_table(rel_bias, sinks):
    def body(rb_ref, sink_ref, tab_ref):
        along = lax.broadcasted_iota(jnp.int32, (8, BLK), 1)
        row8 = lax.broadcasted_iota(jnp.int32, (8, BLK), 0)
        bucket = _buckets(jnp.where(along == 0, 0, BLK - along))
        query = lax.broadcasted_iota(jnp.int32, (BLK, BLK), 0)
        col = lax.broadcasted_iota(jnp.int32, (BLK, BLK), 1)
        for h in range(N_HEADS):
            by_dist = jnp.zeros((8, BLK), F32)
            for b in range(N_BUCKETS):
                by_dist = jnp.where(bucket == b, rb_ref[b, h], by_dist)
            for digit in range(3):
                by_dist = jnp.where((row8 >> digit) & 1 == 1, pltpu.roll(by_dist, 1 << digit, 1), by_dist)
            band = jnp.concatenate([by_dist] + [pltpu.roll(by_dist, 8 * g, 1) for g in range(1, BLK // 8)], axis=0)
            cur = jnp.where(col <= query, band, NEG_INF)
            kh, rows, cols = _head_place(h)
            prev_cols, cur_cols = slice(cols.start, cols.start + BLK), slice(cols.start + BLK, cols.stop)
            tab_ref[1, kh, rows, prev_cols] = jnp.where(col == 0, sink_ref[h], jnp.where(col > query, band, NEG_INF))
            tab_ref[1, kh, rows, cur_cols] = cur
            tab_ref[0, kh, rows, prev_cols] = jnp.where(col == 0, sink_ref[h], NEG_INF)
            tab_ref[0, kh, rows, cur_cols] = cur

    return pl.pallas_call(
        body, name="bias_table", out_shape=SDS((2, N_KV, 4 * BLK, 4 * BLK), F32),
        in_specs=[pl.BlockSpec(memory_space=pltpu.SMEM), pl.BlockSpec(memory_space=pltpu.SMEM)],
        out_specs=pl.BlockSpec(memory_space=pltpu.VMEM),
    )(rel_bias, sinks)


def _bias_fold(dtab):
    def body(dtab_ref, out_ref, dsink_ref):
        bucket = _buckets(lax.broadcasted_iota(jnp.int32, (BLK, 128), 0))
        col = lax.broadcasted_iota(jnp.int32, (BLK, BLK), 1)
        row8 = lax.broadcasted_iota(jnp.int32, (8, 128), 0)
        lane8 = lax.broadcasted_iota(jnp.int32, (8, 128), 1)
        by_dist = jnp.zeros((BLK, 128), F32)
        dsink = jnp.zeros((8, 128), F32)
        for h in range(N_HEADS):
            kh, rows, cols = _head_place(h)
            dt = dtab_ref[kh, rows, cols]
            band = jnp.where(col == 0, 0.0, dt[:, 0:BLK]) + dt[:, BLK:2 * BLK]
            for digit in range(BLK.bit_length() - 1):
                band = jnp.where((col >> digit) & 1 == 1, pltpu.roll(band, BLK - (1 << digit), 0), band)
            by_dist = jnp.where(col == h, jnp.sum(band, axis=1, keepdims=True), by_dist)
            dsink = dsink + jnp.where((row8 == 0) & (lane8 == h), jnp.sum(dt[:, 0:1]), 0.0)
        for b in range(N_BUCKETS):
            out_ref[b:b + 1, :] = jnp.sum(jnp.where(bucket == b, by_dist, 0.0), axis=0, keepdims=True)
        dsink_ref[...] = dsink

    vm = pl.BlockSpec(memory_space=pltpu.VMEM)
    return pl.pallas_call(
        body, name="bias_fold", out_shape=[SDS((N_BUCKETS, 128), F32), SDS((8, 128), F32)],
        in_specs=[vm], out_specs=[vm, vm],
    )(dtab)


def _pair_operands(prev, cur):
    t = jnp.concatenate([prev, cur], axis=0).astype(F32)
    t = jnp.where(lax.broadcasted_iota(jnp.int32, t.shape, 0) == 0, 0.0, t)
    tr = pltpu.roll(t, HEAD_DIM, 1)
    lo = lax.broadcasted_iota(jnp.int32, t.shape, 1) < HEAD_DIM
    zero = jnp.zeros_like(t)
    head0 = jnp.concatenate([jnp.where(lo, t, zero), jnp.where(lo, zero, tr)], axis=0).astype(BF16)
    head1 = jnp.concatenate([jnp.where(lo, tr, zero), jnp.where(lo, zero, t)], axis=0).astype(BF16)
    return head0, head1


def _pair_fold(d0, d1):
    lo = lax.broadcasted_iota(jnp.int32, (2 * BLK, KV_W), 1) < HEAD_DIM
    zero = jnp.zeros((2 * BLK, KV_W), F32)
    g0 = jnp.where(lo, d0[0:256], zero) + pltpu.roll(jnp.where(lo, zero, d0[256:512]), HEAD_DIM, 1)
    g1 = pltpu.roll(jnp.where(lo, d1[0:256], zero), HEAD_DIM, 1) + jnp.where(lo, zero, d1[256:512])
    return jnp.where(lax.broadcasted_iota(jnp.int32, (2 * BLK, KV_W), 0) == 0, 0.0, g0 + g1)


def _stack_pairs(ref, kh):
    return jnp.concatenate([ref[:, 128 * (4 * kh + j):128 * (4 * kh + j + 1)] for j in range(4)], axis=0)


def _table_spec():
    return pl.BlockSpec((1, N_KV, 4 * BLK, 4 * BLK), lambda n: (jnp.minimum(n, 1), 0, 0, 0))


def _attn_fwd(q, kv, tab):
    s = q.shape[0]

    def body(q_ref, kp_ref, kc_ref, vp_ref, vc_ref, tab_ref, att_ref, stats_ref):
        k2 = _pair_operands(kp_ref[...], kc_ref[...])
        v2 = _pair_operands(vp_ref[...], vc_ref[...])
        lane = lax.broadcasted_iota(jnp.int32, (BLK, 128), 1)
        stats = jnp.zeros((BLK, 128), F32)
        for kh in range(N_KV):
            sc = _nt(_stack_pairs(q_ref, kh), k2[kh])
            ps = []
            for e in range(2):
                lg = sc[:, 256 * e:256 * (e + 1)] + tab_ref[0, kh, :, 256 * e:256 * (e + 1)]
                m = jnp.max(lg, axis=-1, keepdims=True)
                ex = jnp.exp(lg - m)
                den = jnp.sum(ex, axis=-1, keepdims=True)
                ps.append(ex * (1.0 / den))
                lse = m + jnp.log(den)
                for j in range(4):
                    stats = jnp.where(lane == GROUP * kh + 2 * j + e, lse[BLK * j:BLK * (j + 1)], stats)
            out = _nn(jnp.concatenate(ps, axis=1).astype(BF16), v2[kh])
            for j in range(4):
                att_ref[:, 128 * (4 * kh + j):128 * (4 * kh + j + 1)] = out[BLK * j:BLK * (j + 1)].astype(BF16)
        stats_ref[...] = stats

    cur = lambda n: (n, 0)
    prev = lambda n: (jnp.maximum(n - 1, 0), 0)
    return pl.pallas_call(
        body, name="attn_fwd", grid=(s // BLK,),
        in_specs=[pl.BlockSpec((BLK, D), cur),
                  pl.BlockSpec((BLK, KV_W), prev), pl.BlockSpec((BLK, KV_W), cur),
                  pl.BlockSpec((BLK, KV_W), lambda n: (jnp.maximum(n - 1, 0), 1)),
                  pl.BlockSpec((BLK, KV_W), lambda n: (n, 1)), _table_spec()],
        out_specs=[pl.BlockSpec((BLK, D), cur), pl.BlockSpec((BLK, 128), cur)],
        out_shape=[SDS((s, D), BF16), SDS((s, 128), F32)],
        compiler_params=_params(("parallel",)),
    )(q, kv, kv, kv, kv, tab)


def _mid(att, zb, h1, tgt, w_out, g_post, tm):
    s = att.shape[0]
    nt = s // tm

    def body(att_ref, z_ref, h1_ref, t_ref, w_ref, g_ref,
             dh_ref, dqz_ref, datt_ref, loss_ref, dg_ref, dw_ref, dw16_ref, dw_acc, stage, put_sem):
        @pl.when(pl.program_id(0) == 0)
        def _():
            loss_ref[...] = jnp.zeros_like(loss_ref)
            dg_ref[...] = jnp.zeros_like(dg_ref)
            dw_acc[...] = jnp.zeros_like(dw_acc)
        att = att_ref[...].astype(F32)
        z = z_ref[...].astype(F32)
        sg, sz = _silu_parts(z)
        ob = (att * sz).astype(BF16)
        y2 = _nn(ob, w_ref[...])
        r2 = _rms_scale(y2)
        yh = y2 * r2
        g = g_ref[...]
        err = (h1_ref[...] + yh * g) - t_ref[...]
        loss_ref[...] += jnp.sum(jnp.sum(err * err, axis=-1, keepdims=True) / D)
        dh = err / D
        dh_ref[...] = dh
        _acc_row(dg_ref, 0, jnp.sum(dh * yh, axis=0, keepdims=True))
        dyh = dh * g
        dy = (r2 * (dyh - yh * jnp.mean(dyh * yh, axis=-1, keepdims=True))).astype(BF16)
        dw_acc[...] += _tn(ob, dy)
        dob = _nt(dy, w_ref[...])
        datt_ref[...] = (dob * sz).astype(BF16)
        dqz_ref[...] = (dob * att * _dsilu(z, sg)).astype(BF16)

        @pl.when(pl.program_id(0) == nt - 1)
        def _():
            _write_gradient(dw_acc, dw_ref, dw16_ref, stage, put_sem)

    row = lambda i: (i, 0)
    fix = lambda i: (0, 0)
    anyspace = pl.BlockSpec(memory_space=pl.ANY)
    return pl.pallas_call(
        body, name="mid", grid=(nt,),
        in_specs=[pl.BlockSpec((tm, D), row)] * 4 + [pl.BlockSpec((D, D), fix), pl.BlockSpec((1, D), fix)],
        out_specs=[pl.BlockSpec((tm, D), row), pl.BlockSpec((tm, D), lambda i: (i, 1)), pl.BlockSpec((tm, D), row),
                   pl.BlockSpec((8, 128), fix), pl.BlockSpec((8, D), fix), anyspace, anyspace],
        out_shape=[SDS((s, D), F32), SDS((s, 2 * D), BF16), SDS((s, D), BF16), SDS((8, 128), F32),
                   SDS((8, D), F32), SDS((D, D), F32), SDS((D, D), BF16)],
        scratch_shapes=[pltpu.VMEM((D, D), F32), pltpu.VMEM((D // 4, D), BF16), pltpu.SemaphoreType.DMA],
        compiler_params=_params(("arbitrary",)),
    )(att, zb, h1, tgt, w_out, g_post)


def _attn_bwd(q, kv, datt, stats, tab, dqz):
    s = q.shape[0]
    nb = s // BLK

    def body(q_ref, kp_ref, kc_ref, vp_ref, vc_ref, da_ref, st_ref, tab_ref, dqz_in,
             dq_ref, dkv_ref, dtab_ref, dk_carry, dv_carry):
        del dqz_in
        n = pl.program_id(0)

        @pl.when(n == 0)
        def _():
            dtab_ref[...] = jnp.zeros_like(dtab_ref)
            dk_carry[...] = jnp.zeros_like(dk_carry)
            dv_carry[...] = jnp.zeros_like(dv_carry)

        @pl.when(n < nb)
        def _():
            k2 = _pair_operands(kp_ref[...], kc_ref[...])
            v2 = _pair_operands(vp_ref[...], vc_ref[...])
            lane = lax.broadcasted_iota(jnp.int32, (BLK, 128), 1)
            stats = st_ref[...]
            dk2, dv2 = [], []
            for kh in range(N_KV):
                qs = _stack_pairs(q_ref, kh)
                das = _stack_pairs(da_ref, kh)
                sc = _nt(qs, k2[kh])
                dp = _nt(das, v2[kh])
                ps, dss = [], []
                for e in range(2):
                    heads = [GROUP * kh + 2 * j + e for j in range(4)]
                    lse = jnp.concatenate([jnp.sum(jnp.where(lane == h, stats, 0.0), axis=-1, keepdims=True)
                                           for h in heads], axis=0)
                    cols = slice(256 * e, 256 * (e + 1))
                    p = jnp.exp(sc[:, cols] + tab_ref[0, kh, :, cols] - lse)
                    delta = jnp.sum(p * dp[:, cols], axis=-1, keepdims=True)
                    ds = p * (dp[:, cols] - delta)
                    dtab_ref[kh, :, cols] += ds
                    ps.append(p)
                    dss.append(ds)
                p2 = jnp.concatenate(ps, axis=1).astype(BF16)
                ds2 = jnp.concatenate(dss, axis=1).astype(BF16)
                dq = _nn(ds2, k2[kh]) * Q_SCALE
                for j in range(4):
                    dq_ref[:, 128 * (4 * kh + j):128 * (4 * kh + j + 1)] = dq[BLK * j:BLK * (j + 1)].astype(BF16)
                dk2.append(_tn(ds2, qs))
                dv2.append(_tn(p2, das))
            dkk = _pair_fold(dk2[0], dk2[1])
            dvv = _pair_fold(dv2[0], dv2[1])
            dkv_ref[:, 0:KV_W] = (dk_carry[...] + dkk[0:BLK]).astype(BF16)
            dkv_ref[:, KV_W:2 * KV_W] = (dv_carry[...] + dvv[0:BLK]).astype(BF16)
            dk_carry[...] = dkk[BLK:2 * BLK]
            dv_carry[...] = dvv[BLK:2 * BLK]

        @pl.when(n == nb)
        def _():
            dkv_ref[:, 0:KV_W] = dk_carry[...].astype(BF16)
            dkv_ref[:, KV_W:2 * KV_W] = dv_carry[...].astype(BF16)

    cur = lambda n: (jnp.minimum(n, nb - 1), 0)
    prev = lambda n: (jnp.clip(n - 1, 0, nb - 1), 0)
    return pl.pallas_call(
        body, name="attn_bwd", grid=(nb + 1,),
        in_specs=[pl.BlockSpec((BLK, D), cur),
                  pl.BlockSpec((BLK, KV_W), prev), pl.BlockSpec((BLK, KV_W), cur),
                  pl.BlockSpec((BLK, KV_W), lambda n: (jnp.clip(n - 1, 0, nb - 1), 1)),
                  pl.BlockSpec((BLK, KV_W), lambda n: (jnp.minimum(n, nb - 1), 1)),
                  pl.BlockSpec((BLK, D), cur), pl.BlockSpec((BLK, 128), cur), _table_spec(),
                  pl.BlockSpec(memory_space=pl.ANY)],
        out_specs=[pl.BlockSpec((BLK, D), cur), pl.BlockSpec((BLK, 2 * KV_W), prev),
                   pl.BlockSpec((N_KV, 4 * BLK, 4 * BLK), lambda n: (0, 0, 0))],
        out_shape=[SDS((s, 2 * D), BF16), SDS((s, 2 * KV_W), BF16), SDS((N_KV, 4 * BLK, 4 * BLK), F32)],
        scratch_shapes=[pltpu.VMEM((BLK, KV_W), F32), pltpu.VMEM((BLK, KV_W), F32)],
        input_output_aliases={8: 0},
        compiler_params=_params(("arbitrary",)),
    )(q, kv, kv, kv, kv, datt, stats, tab, dqz)


def _b_bwd(dqz, dkv, h1, dh2, oa, wbin_g, w_kv, g_kv, g_pre, g_apost, tm):
    s = h1.shape[0]
    nt = s // tm

    def body(dqz_ref, dkv_ref, h_ref, dh2_ref, oa_ref, wb_ref, wkv_ref, gk_ref, gb_ref, ga_ref,
             dh1_ref, doa_ref, dg_ref, dwb_ref, dwkv_ref, dwb16_ref, dwkv16_ref, wcat, dwb_acc, dwkv_acc, put_sem):
        @pl.when(pl.program_id(0) == 0)
        def _():
            dg_ref[...] = jnp.zeros_like(dg_ref)
            dwb_acc[...] = jnp.zeros_like(dwb_acc)
            dwkv_acc[...] = jnp.zeros_like(dwkv_acc)
            for j in range(N_CHIPS):
                pltpu.sync_copy(wb_ref.at[j], wcat.at[:, pl.ds(BIN_COLS * j, BIN_COLS)])
        dnb = _nt(dqz_ref[...], wcat[...])
        dnk = _nt(dkv_ref[...], wkv_ref[...])
        h = h_ref[...]
        r = _rms_scale(h)
        hh = h * r
        dwb_acc[...] += _tn((hh * gb_ref[...]).astype(BF16), dqz_ref[...])
        dwkv_acc[...] += _tn((hh * gk_ref[...]).astype(BF16), dkv_ref[...])
        _acc_row(dg_ref, 0, jnp.sum(dnk * hh, axis=0, keepdims=True))
        _acc_row(dg_ref, 1, jnp.sum(dnb * hh, axis=0, keepdims=True))
        dhh = dnb * gb_ref[...] + dnk * gk_ref[...]
        dh1 = dh2_ref[...] + r * (dhh - hh * jnp.mean(dhh * hh, axis=-1, keepdims=True))
        dh1_ref[...] = dh1
        oa = oa_ref[...].astype(F32)
        ra = _rms_scale(oa)
        oh = oa * ra
        _acc_row(dg_ref, 2, jnp.sum(dh1 * oh, axis=0, keepdims=True))
        doh = dh1 * ga_ref[...]
        doa_ref[...] = (ra * (doh - oh * jnp.mean(doh * oh, axis=-1, keepdims=True))).astype(BF16)

        @pl.when(pl.program_id(0) == nt - 1)
        def _():
            wcat[...] = dwb_acc[...].astype(BF16)
            puts = [pltpu.make_async_copy(dwkv_acc, dwkv_ref, put_sem.at[2 * N_CHIPS])]
            for j in range(N_CHIPS):
                cols = pl.ds(BIN_COLS * j, BIN_COLS)
                puts.append(pltpu.make_async_copy(dwb_acc.at[:, cols], dwb_ref.at[j], put_sem.at[2 * j]))
                puts.append(pltpu.make_async_copy(wcat.at[:, cols], dwb16_ref.at[j], put_sem.at[2 * j + 1]))
            for put in puts:
                put.start()
            for put in puts:
                put.wait()
            wcat[:, 0:2 * KV_W] = dwkv_acc[...].astype(BF16)
            pltpu.sync_copy(wcat.at[:, pl.ds(0, 2 * KV_W)], dwkv16_ref)

    row = lambda i: (i, 0)
    fix = lambda i: (0, 0)
    anyspace = pl.BlockSpec(memory_space=pl.ANY)
    return pl.pallas_call(
        body, name="b_bwd", grid=(nt,),
        in_specs=[pl.BlockSpec((tm, 2 * D), row), pl.BlockSpec((tm, 2 * KV_W), row), pl.BlockSpec((tm, D), row),
                  pl.BlockSpec((tm, D), row), pl.BlockSpec((tm, D), row), anyspace, pl.BlockSpec((D, 2 * KV_W), fix),
                  pl.BlockSpec((1, D), fix), pl.BlockSpec((1, D), fix), pl.BlockSpec((1, D), fix)],
        out_specs=[pl.BlockSpec((tm, D), row), pl.BlockSpec((tm, D), row), pl.BlockSpec((8, D), fix)] + [anyspace] * 4,
        out_shape=[SDS((s, D), F32), SDS((s, D), BF16), SDS((8, D), F32), SDS((N_CHIPS, D, BIN_COLS), F32),
                   SDS((D, 2 * KV_W), F32), SDS((N_CHIPS, D, BIN_COLS), BF16), SDS((D, 2 * KV_W), BF16)],
        scratch_shapes=[pltpu.VMEM((D, 2 * D), BF16), pltpu.VMEM((D, 2 * D), F32), pltpu.VMEM((D, 2 * KV_W), F32),
                        pltpu.SemaphoreType.DMA((2 * N_CHIPS + 1,))],
        compiler_params=_params(("arbitrary",)),
    )(dqz, dkv, h1, dh2, oa, wbin_g, w_kv, g_kv, g_pre, g_apost)


def _to_owner_core(pieces, r, send, recv, core, action):
    x, y, c = lax.axis_index("x"), lax.axis_index("y"), lax.axis_index("c")
    for kp in range(N_CHIPS):
        px, py = kp >> 1, kp & 1
        rel = 4 * (x + px - 2 * x * px) + 2 * (y + py - 2 * y * py) + (c + core - 2 * c * core)

        @pl.when(rel != 0)
        def _():
            cp = pltpu.make_async_remote_copy(src_ref=pieces.at[kp], dst_ref=r.at[rel - 1], send_sem=send.at[kp],
                                              recv_sem=recv.at[rel - 1], device_id=(px, py, core), device_id_type=MESH)
            if action == "start":
                cp.start()
            else:
                cp.wait_send()
    if action == "wait":
        @pl.when(c == core)
        def _():
            for rel in range(1, N_DEV):
                pltpu.make_async_remote_copy(src_ref=pieces.at[0], dst_ref=r.at[rel - 1], send_sem=send.at[0],
                                             recv_sem=recv.at[rel - 1], device_id=(x, y, c),
                                             device_id_type=MESH).wait_recv()


def _owner_core_sems():
    return [pltpu.SemaphoreType.DMA((N_CHIPS,)), pltpu.SemaphoreType.DMA((N_DEV - 1,))]


def _device_exchange(grads, recvs, send, recv):
    x, y, c = lax.axis_index("x"), lax.axis_index("y"), lax.axis_index("c")
    copies = []
    for a, (g, r) in enumerate(zip(grads, recvs)):
        h = g.shape[1] // 2
        for rel in range(1, N_DEV):
            fx, fy, fc = rel >> 2, (rel >> 1) & 1, rel & 1
            px, py, pc = x + fx - 2 * x * fx, y + fy - 2 * y * fy, c + fc - 2 * c * fc
            sem = (N_DEV - 1) * a + rel - 1
            copies.append(pltpu.make_async_remote_copy(
                src_ref=g.at[2 * px + py, pl.ds(pl.multiple_of(pc * h, 16), h)], dst_ref=r.at[rel - 1],
                send_sem=send.at[sem], recv_sem=recv.at[sem], device_id=(px, py, pc), device_id_type=MESH))
    return copies


def _device_exchange_specs(grads):
    anyspace = pl.BlockSpec(memory_space=pl.ANY)
    n = len(grads)
    count = (N_DEV - 1) * n
    return ([anyspace] * n, [anyspace] * n,
            [SDS((N_DEV - 1, g.shape[1] // 2, g.shape[2]), g.dtype) for g in grads],
            [pltpu.SemaphoreType.DMA((count,)), pltpu.SemaphoreType.DMA((count,))])


def _a_bwd(doa, ya, conv, proj, conv_w, w_out, tm, parts):
    s = doa.shape[0]
    nt = s // tm
    n = len(parts)
    ex_in, ex_out, ex_shape, ex_sems = _device_exchange_specs(parts)

    def body(*refs):
        doa_ref, ya_ref, conv_ref, proj_ref, cw_ref, w_ref = refs[:6]
        part_refs = refs[6:6 + n]
        dproj_ref, dcw_ref, dw_ref, dw16_ref = refs[6 + n:10 + n]
        recv_refs = refs[10 + n:10 + 2 * n]
        carry, dw_acc, stage, put_sem, send, recv = refs[10 + 2 * n:]
        i = pl.program_id(0)

        @pl.when(i == 0)
        def _():
            dcw_ref[...] = jnp.zeros_like(dcw_ref)
            carry[...] = jnp.zeros_like(carry)
            dw_acc[...] = jnp.zeros_like(dw_acc)
            for cp in _device_exchange(part_refs, recv_refs, send, recv):
                cp.start()
        dya = _nt(doa_ref[...], w_ref[...])
        dw_acc[...] += _tn(ya_ref[...], doa_ref[...])
        bg = proj_ref[:, 0:D].astype(F32)
        cg = proj_ref[:, D:2 * D].astype(F32)
        u = proj_ref[:, 2 * D:3 * D].astype(F32)
        z = proj_ref[:, 3 * D:4 * D].astype(F32)
        v = cg * u
        rows = lax.broadcasted_iota(jnp.int32, (tm, D), 0)
        conv = conv_ref[...].astype(F32)
        sg, sz = _silu_parts(z)
        dproj_ref[:, 0:D] = (dya * conv * sz).astype(BF16)
        dproj_ref[:, 3 * D:4 * D] = (dya * bg * conv * _dsilu(z, sg)).astype(BF16)
        dconv = dya * bg * sz
        after = carry[...]
        up1 = jnp.where(rows < tm - 1, pltpu.roll(dconv, tm - 1, 0), after[0:1, :])
        up2 = jnp.where(rows < tm - 2, pltpu.roll(dconv, tm - 2, 0),
                        jnp.where(rows == tm - 2, after[0:1, :], after[1:2, :]))
        carry[...] = dconv[0:8, :]
        _acc_row(dcw_ref, 0, jnp.sum(up2 * v, axis=0, keepdims=True))
        _acc_row(dcw_ref, 1, jnp.sum(up1 * v, axis=0, keepdims=True))
        _acc_row(dcw_ref, 2, jnp.sum(dconv * v, axis=0, keepdims=True))
        dv = cw_ref[2:3, :] * dconv + cw_ref[1:2, :] * up1 + cw_ref[0:1, :] * up2
        dproj_ref[:, D:2 * D] = (dv * u).astype(BF16)
        dproj_ref[:, 2 * D:3 * D] = (dv * cg).astype(BF16)

        @pl.when(i == nt - 1)
        def _():
            _write_gradient(dw_acc, dw_ref, dw16_ref, stage, put_sem)
            for cp in _device_exchange(part_refs, recv_refs, send, recv):
                cp.wait()

    rev = lambda i: (nt - 1 - i, 0)
    fix = lambda i: (0, 0)
    anyspace = pl.BlockSpec(memory_space=pl.ANY)
    dproj, dcw, dw, dw16, *got = pl.pallas_call(
        body, name="a_bwd", grid=(nt,),
        in_specs=[pl.BlockSpec((tm, D), rev), pl.BlockSpec((tm, D), rev), pl.BlockSpec((tm, D), rev),
                  pl.BlockSpec((tm, 4 * D), rev), pl.BlockSpec((8, D), fix), pl.BlockSpec((D, D), fix)] + ex_in,
        out_specs=[pl.BlockSpec((tm, 4 * D), rev), pl.BlockSpec((8, D), fix), anyspace, anyspace] + ex_out,
        out_shape=[SDS((s, 4 * D), BF16), SDS((8, D), F32), SDS((D, D), F32), SDS((D, D), BF16)] + ex_shape,
        scratch_shapes=[pltpu.VMEM((8, D), F32), pltpu.VMEM((D, D), F32), pltpu.VMEM((D // 4, D), BF16),
                        pltpu.SemaphoreType.DMA] + ex_sems,
        compiler_params=_params(("arbitrary",)),
    )(doa, ya, conv, proj, conv_w, w_out, *parts)
    return dproj, dcw, dw, dw16, got


def _dn1(dp_ref, w_ref):
    dn = _nt(dp_ref[:, 0:D], w_ref[0])
    for j in range(1, 4):
        dn = dn + _nt(dp_ref[:, D * j:D * (j + 1)], w_ref[j])
    return dn


def _a_in_bwd_matmul(dproj, win_g, tm, count, win_half, win_got):
    def body(dp_ref, w_ref, half_ref, got_in, dn_ref, got_ref, wcat, send, recv):
        del got_in

        @pl.when(pl.program_id(0) == 0)
        def _():
            _to_owner_core(half_ref, got_ref, send, recv, 1, "start")
            for j in range(N_CHIPS):
                pltpu.sync_copy(w_ref.at[j], wcat.at[:, pl.ds(D * j, D)])
        dn_ref[...] = _nt(dp_ref[...], wcat[...]).astype(BF16)

        @pl.when(pl.program_id(0) == count - 1)
        def _():
            _to_owner_core(half_ref, got_ref, send, recv, 1, "wait")

    row = lambda i: (i, 0)
    anyspace = pl.BlockSpec(memory_space=pl.ANY)
    return pl.pallas_call(
        body, name="a_in_bwd_matmul", grid=(count,),
        in_specs=[pl.BlockSpec((tm, 4 * D), row), anyspace, anyspace, anyspace],
        out_specs=[pl.BlockSpec((tm, D), row), anyspace],
        out_shape=[SDS((count * tm, D), BF16), SDS(win_got.shape, win_got.dtype)],
        scratch_shapes=[pltpu.VMEM((D, 4 * D), BF16)] + _owner_core_sems(),
        input_output_aliases={3: 1},
        compiler_params=_params(("arbitrary",)),
    )(dproj, win_g, win_half, win_got)


def _a_in_bwd(dn_first, dproj, x, dh1, win_g, g_pre, tm):
    s = x.shape[0]
    nt = s // tm
    count = dn_first.shape[0] // tm

    def body(dn_ref, dp_ref, x_ref, dh_ref, w_ref, g_ref, gx_ref, dg_ref, dn_s):
        i = pl.program_id(0)

        @pl.when(i == 0)
        def _():
            dg_ref[...] = jnp.zeros_like(dg_ref)

        @pl.when(i < count)
        def _():
            dn_s[...] = dn_ref[...].astype(F32)

        @pl.when(i >= count)
        def _():
            dn_s[...] = _dn1(dp_ref, w_ref)
        dn = dn_s[...]
        xv = x_ref[...]
        r = _rms_scale(xv)
        xh = xv * r
        _acc_row(dg_ref, 0, jnp.sum(dn * xh, axis=0, keepdims=True))
        dxh = dn * g_ref[...]
        gx_ref[...] = dh_ref[...] + r * (dxh - xh * jnp.mean(dxh * xh, axis=-1, keepdims=True))

    row = lambda i: (i, 0)
    fix = lambda i: (0, 0)
    return pl.pallas_call(
        body, name="a_in_bwd", grid=(nt,),
        in_specs=[pl.BlockSpec((tm, D), lambda i: (jnp.minimum(i, count - 1), 0)),
                  pl.BlockSpec((tm, 4 * D), lambda i: (jnp.maximum(i, count), 0)),
                  pl.BlockSpec((tm, D), row), pl.BlockSpec((tm, D), row),
                  pl.BlockSpec((4, D, D), lambda i: (0, 0, 0)), pl.BlockSpec((1, D), fix)],
        out_specs=[pl.BlockSpec((tm, D), row), pl.BlockSpec((8, D), fix)],
        out_shape=[SDS((s, D), F32), SDS((8, D), F32)],
        scratch_shapes=[pltpu.VMEM((tm, D), F32)],
        compiler_params=_params(("arbitrary",)),
    )(dn_first, dproj, x, dh1, win_g, g_pre)


def _swap_halves(shards, send, recv):
    x, y, c = lax.axis_index("x"), lax.axis_index("y"), lax.axis_index("c")
    sibling = (x, y, 1 - c)
    copies = []
    for b, full in enumerate(shards):
        h = full.shape[0] // 2
        mine = full.at[pl.ds(pl.multiple_of(c * h, 8), h)]
        theirs = full.at[pl.ds(pl.multiple_of((1 - c) * h, 8), h)]
        copies.append((pltpu.make_async_remote_copy(src_ref=mine, dst_ref=mine, send_sem=send.at[b], recv_sem=recv.at[b],
                                                    device_id=sibling, device_id_type=MESH),
                       pltpu.make_async_remote_copy(src_ref=mine, dst_ref=theirs, send_sem=send.at[b], recv_sem=recv.at[b],
                                                    device_id=sibling, device_id_type=MESH)))
    return copies


def _dw_in_half(n1, dproj, core, tmw, name, to_owners=None, to_devices=None, shards=()):
    s = n1.shape[0]
    h = D // 2
    nt = s // tmw
    n_sh = len(shards)
    if to_owners is not None:
        sent_array, sems, got_shape = to_owners, _owner_core_sems(), SDS((N_DEV - 1, h, D), BF16)
    else:
        sent_array = to_devices
        _, _, (got_shape,), sems = _device_exchange_specs([to_devices])

    def body(*refs):
        a_ref, b_ref, sent = refs[:3]
        o_ref, o16_ref, got = refs[3 + n_sh:6 + n_sh]
        shard_refs = refs[6 + n_sh:6 + 2 * n_sh]
        send, recv = refs[6 + 2 * n_sh:8 + 2 * n_sh]
        swap_sems = refs[8 + 2 * n_sh:]
        j, t = pl.program_id(0), pl.program_id(1)

        def exchange(action):
            if to_owners is not None:
                _to_owner_core(sent, got, send, recv, 1 - core, action)
            else:
                for cp in _device_exchange([sent], [got], send, recv):
                    cp.start() if action == "start" else cp.wait()

        @pl.when((j == 0) & (t == 0))
        def _():
            exchange("start")
            if n_sh:
                for mine, _ in _swap_halves(shard_refs, *swap_sems):
                    mine.start()

        @pl.when(t == 0)
        def _():
            o_ref[...] = jnp.zeros_like(o_ref)
        o_ref[0] += _tn(a_ref[...], b_ref[...])

        @pl.when(t == nt - 1)
        def _():
            o16_ref[...] = o_ref[...].astype(BF16)

        @pl.when((j == N_CHIPS - 1) & (t == nt - 1))
        def _():
            exchange("wait")
            if n_sh:
                for mine, theirs in _swap_halves(shard_refs, *swap_sems):
                    theirs.wait_recv()
                    mine.wait_send()

    anyspace = pl.BlockSpec(memory_space=pl.ANY)
    slot = pl.BlockSpec((1, h, D), lambda j, t: (j, 0, 0))
    swap_scratch = [pltpu.SemaphoreType.DMA((n_sh,)), pltpu.SemaphoreType.DMA((n_sh,))] if n_sh else []
    return pl.pallas_call(
        body, name=name, grid=(N_CHIPS, nt),
        in_specs=[pl.BlockSpec((tmw, h), lambda j, t: (t, core)), pl.BlockSpec((tmw, D), lambda j, t: (t, j))]
        + [anyspace] * (1 + n_sh),
        out_specs=[slot, slot] + [anyspace] * (1 + n_sh),
        out_shape=[SDS((N_CHIPS, h, D), F32), SDS((N_CHIPS, h, D), BF16), got_shape]
        + [SDS(sh.shape, F32) for sh in shards],
        scratch_shapes=sems + swap_scratch,
        input_output_aliases={3 + b: 3 + b for b in range(n_sh)},
        compiler_params=_params(("arbitrary", "arbitrary")),
    )(n1, dproj, sent_array, *shards)


def _share_and_gather(shards, smalls):
    n_h, n_s = len(shards), len(smalls)

    def body(*refs):
        small_ins = refs[n_h:n_h + n_s]
        fs = refs[n_h + n_s:2 * n_h + n_s]
        small_alls = refs[2 * n_h + n_s:2 * n_h + 2 * n_s]
        dsend, drecv, ssend, srecv = refs[2 * n_h + 2 * n_s:]
        x, y, c = lax.axis_index("x"), lax.axis_index("y"), lax.axis_index("c")
        swaps = _swap_halves(fs, dsend, drecv)
        sends, arrivals = [mine for mine, _ in swaps], [theirs for _, theirs in swaps]
        me = 4 * x + 2 * y + c
        for k, (small_in, small_all) in enumerate(zip(small_ins, small_alls)):
            small_all[me] = small_in[...]
            for rel in range(1, N_DEV):
                fx, fy, fc = rel >> 2, (rel >> 1) & 1, rel & 1
                peer = (x + fx - 2 * x * fx, y + fy - 2 * y * fy, c + fc - 2 * c * fc)
                sender = 4 * peer[0] + 2 * peer[1] + peer[2]
                sem = (N_DEV - 1) * k + rel - 1
                sends.append(pltpu.make_async_remote_copy(
                    src_ref=small_in, dst_ref=small_all.at[me], send_sem=ssend.at[sem], recv_sem=srecv.at[sem],
                    device_id=peer, device_id_type=MESH))
                arrivals.append(pltpu.make_async_remote_copy(
                    src_ref=small_in, dst_ref=small_all.at[sender], send_sem=ssend.at[sem], recv_sem=srecv.at[sem],
                    device_id=peer, device_id_type=MESH))
        for cp in sends:
            cp.start()
        for cp in arrivals:
            cp.wait_recv()
        for cp in sends:
            cp.wait_send()

    anyspace = pl.BlockSpec(memory_space=pl.ANY)
    vm = pl.BlockSpec(memory_space=pltpu.VMEM)
    out_shape = [SDS(full.shape, F32) for full in shards] + [SDS((N_DEV,) + sm.shape, F32) for sm in smalls]
    n_all = (N_DEV - 1) * n_s
    outs = pl.pallas_call(
        body, name="share_and_gather", out_shape=out_shape,
        in_specs=[anyspace] * n_h + [vm] * n_s, out_specs=[anyspace] * n_h + [vm] * n_s,
        scratch_shapes=[pltpu.SemaphoreType.DMA((n_h,)), pltpu.SemaphoreType.DMA((n_h,)),
                        pltpu.SemaphoreType.DMA((n_all,)), pltpu.SemaphoreType.DMA((n_all,))],
        input_output_aliases={b: b for b in range(n_h)},
    )(*shards, *smalls)
    return outs[:n_h], outs[n_h:]


def _add_win(where, lo, hi, r, name):
    _, h, cols = lo.shape
    tr = min(h, 256)
    nh = h // tr

    def body(where_ref, lo_ref, hi_ref, r_ref, o_ref):
        acc = jnp.where(where_ref[0] == 0, lo_ref[0], hi_ref[0])
        for k in range(N_DEV - 1):
            acc = acc + r_ref[k].astype(F32)
        o_ref[...] = acc

    own = pl.BlockSpec((1, tr, cols), lambda i, w: (w[1], i, 0))
    return pl.pallas_call(
        body, name=name,
        grid_spec=pltpu.PrefetchScalarGridSpec(
            num_scalar_prefetch=1, grid=(nh,),
            in_specs=[own, own, pl.BlockSpec((N_DEV - 1, tr, cols), lambda i, w: (0, i, 0))],
            out_specs=pl.BlockSpec((tr, cols), lambda i, w: (w[0] * nh + i, 0))),
        out_shape=SDS((2 * h, cols), F32),
        compiler_params=_params(("parallel",)),
    )(where, lo, hi, r)


def _add_devices(where, g, r, name):
    _, rows, cols = g.shape
    h = rows // 2
    tr = min(h, 256)
    nh = h // tr

    def body(where_ref, g_ref, r_ref, o_ref):
        del where_ref
        acc = g_ref[0]
        for k in range(N_DEV - 1):
            acc = acc + r_ref[k].astype(F32)
        o_ref[...] = acc

    return pl.pallas_call(
        body, name=name,
        grid_spec=pltpu.PrefetchScalarGridSpec(
            num_scalar_prefetch=1, grid=(nh,),
            in_specs=[pl.BlockSpec((1, tr, cols), lambda i, w: (w[1], w[0] * nh + i, 0)),
                      pl.BlockSpec((N_DEV - 1, tr, cols), lambda i, w: (0, i, 0))],
            out_specs=pl.BlockSpec((tr, cols), lambda i, w: (w[0] * nh + i, 0))),
        out_shape=SDS((rows, cols), F32),
        compiler_params=_params(("parallel",)),
    )(where, g, r)


def _sum_smalls(gathered):
    n = len(gathered)

    def body(*refs):
        for all_ref, o_ref in zip(refs[:n], refs[n:]):
            acc = all_ref[0]
            for dev in range(1, N_DEV):
                acc = acc + all_ref[dev]
            o_ref[...] = acc

    vm = pl.BlockSpec(memory_space=pltpu.VMEM)
    return pl.pallas_call(
        body, name="sum_smalls", out_shape=[SDS(a.shape[1:], F32) for a in gathered],
        in_specs=[vm] * n, out_specs=[vm] * n,
    )(*gathered)


def _adam_step(g, w, m, v):
    nm = ADAM_B1 * m + (1.0 - ADAM_B1) * g
    nv = ADAM_B2 * v + (1.0 - ADAM_B2) * (g * g)
    m_hat = nm / (1.0 - ADAM_B1 ** ADAM_STEP)
    v_hat = nv / (1.0 - ADAM_B2 ** ADAM_STEP)
    return -ADAM_LR * (m_hat / (jnp.sqrt(v_hat) + ADAM_EPS) + ADAM_WD * w), nm, nv


def _adamw(g, w, m, v, name):
    rows, cols = g.shape
    tr = min(rows, 256)

    def body(g_ref, w_ref, m_ref, v_ref, d_ref, nm_ref, nv_ref):
        d_ref[...], nm_ref[...], nv_ref[...] = _adam_step(g_ref[...], w_ref[...], m_ref[...], v_ref[...])

    spec = pl.BlockSpec((tr, cols), lambda i: (i, 0))
    return pl.pallas_call(
        body, name=name, grid=(rows // tr,), in_specs=[spec] * 4, out_specs=[spec] * 3,
        out_shape=[SDS(g.shape, F32)] * 3, compiler_params=_params(("parallel",)),
    )(g, w, m, v)


def _small_update(chip, tot, tot_rel, wmv):
    names = list(SMALL_PLACES)
    n = len(names)

    def body(chip_ref, tot_ref, quarter_ref, rel_ref, *refs):
        del chip_ref
        ins, outs = refs[:3 * n], refs[3 * n:]
        for i, nm in enumerate(names):
            source, row, shape = SMALL_PLACES[nm]
            from_ref = {"rows": tot_ref, "quarter": quarter_ref, "rel": rel_ref}[source]
            g = from_ref[row, 0:shape[0]] if len(shape) == 1 else from_ref[row:row + shape[0], 0:shape[1]]
            outs[4 * i][...] = g
            outs[4 * i + 1][...], outs[4 * i + 2][...], outs[4 * i + 3][...] = _adam_step(
                g, ins[3 * i][...], ins[3 * i + 1][...], ins[3 * i + 2][...])

    whole = lambda shape: pl.BlockSpec(shape, lambda i, c: (0,) * len(shape))
    shapes = [SMALL_PLACES[nm][2] for nm in names]
    outs = pl.pallas_call(
        body, name="small_update",
        grid_spec=pltpu.PrefetchScalarGridSpec(
            num_scalar_prefetch=1, grid=(1,),
            in_specs=[whole(tot.shape), pl.BlockSpec((tot.shape[0], D // 4), lambda i, c: (0, c[0])),
                      whole(tot_rel.shape)] + [whole(shp) for shp in shapes for _ in range(3)],
            out_specs=[whole(shp) for shp in shapes for _ in range(4)]),
        out_shape=[SDS(shp, F32) for shp in shapes for _ in range(4)],
    )(chip, tot, tot, tot_rel, *[a for nm in names for a in wmv[nm]])
    return {nm: tuple(outs[4 * i:4 * i + 4]) for i, nm in enumerate(names)}


def _pad_rows(a, rows):
    return jnp.concatenate([a, jnp.zeros((rows - a.shape[0], a.shape[1]), a.dtype)], axis=0)


def _pad_cols(a, cols):
    return jnp.concatenate([a, jnp.zeros((a.shape[0], cols - a.shape[1]), a.dtype)], axis=1)


def kernel(x, a_pre_norm, a_w_in, a_conv_w, a_w_out, a_post_norm, kv_norm, w_kv, rel_bias, b_pre_norm, b_w_in, b_sinks, b_w_out, b_post_norm, loss_target, m_a_pre_norm, m_a_w_in, m_a_conv_w, m_a_w_out, m_a_post_norm, m_kv_norm, m_w_kv, m_rel_bias, m_b_pre_norm, m_b_w_in, m_b_sinks, m_b_w_out, m_b_post_norm, v_a_pre_norm, v_a_w_in, v_a_conv_w, v_a_w_out, v_a_post_norm, v_kv_norm, v_w_kv, v_rel_bias, v_b_pre_norm, v_b_w_in, v_b_sinks, v_b_w_out, v_b_post_norm):
    seq = x.shape[1]
    xs = x.reshape(seq, D)
    tgt = loss_target.reshape(seq, D)
    chip = 2 * lax.axis_index("x") + lax.axis_index("y")
    core = lax.axis_index("c")
    tm = _tile(seq, 512)
    tmw = _tile(seq, 1024)

    shards = [a_w_in[0], a_w_out[0], w_kv, b_w_in[0], b_w_out[0]]
    small_w = _pad_rows(jnp.concatenate([a_pre_norm, a_conv_w[0], a_post_norm], axis=0), 8)
    *own_only, small_g = _prepare_weights(shards, small_w)
    where = jnp.stack([core, chip]).astype(jnp.int32)
    small_full = small_g.transpose(1, 0, 2).reshape(8, D)
    g_apre, conv_w, g_apost = small_full[0:1], _pad_rows(small_full[1:4], 8), small_full[4:5]
    g_kv = kv_norm.reshape(1, D)

    proj, n1, (win_g, wouta_g, wkv_g, wbin_g, woutb_g) = _a_in(where[1:2], xs, g_apre, own_only, tmw)
    wouta = wouta_g.reshape(D, D)
    wkv = wkv_g.reshape(D, 2 * KV_W)
    woutb = woutb_g.reshape(D, D)
    ya, oa, h1, conv = _a_mix(proj, xs, conv_w, wouta, g_apost, tm)
    kv, q, zb = _b_in(h1, g_kv, b_pre_norm, wkv, wbin_g, tmw)
    tab = _bias_table(rel_bias, b_sinks.reshape(N_HEADS))
    att, stats = _attn_fwd(q, kv, tab)
    dh2, dqz, datt, loss_acc, dg_bpost, dw_outb, dw_outb16 = _mid(att, zb, h1, tgt, woutb, b_post_norm, tm)

    dqz, dkv, dtab = _attn_bwd(q, kv, datt, stats, tab, dqz)
    dh1, doa, dg_b, dw_bin, dw_kv, dw_bin16, dw_kv16 = _b_bwd(dqz, dkv, h1, dh2, oa, wbin_g, wkv, g_kv, b_pre_norm,
                                                              g_apost, tm)
    by_chip = lambda a, cols: a.reshape(N_CHIPS, D // 4, cols)
    grads1 = [by_chip(dw_kv, 2 * KV_W), dw_bin, by_chip(dw_outb, D)]
    sent1 = [by_chip(dw_kv16, 2 * KV_W), dw_bin16, by_chip(dw_outb16, D)]
    names1 = ["w_kv", "b_w_in", "b_w_out"]
    dproj, dconv_w, dw_outa, dw_outa16, from_devices1 = _a_bwd(doa, ya, conv, proj, conv_w, wouta, tm, sent1)
    shards1 = [_add_devices(where, g, r, "add_devices_" + nm) for g, r, nm in zip(grads1, from_devices1, names1)]
    tmw2 = _tile(seq, 4096)
    win_lo, win_lo16, outa_got, g_wkv, g_wbin, g_woutb = _dw_in_half(
        n1, dproj, 0, tmw2, "dw_a_in_lo", to_devices=by_chip(dw_outa16, D), shards=shards1)
    win_hi, win_hi16, win_got = _dw_in_half(n1, dproj, 1, tmw2, "dw_a_in_hi", to_owners=win_lo16)
    nt = seq // tmw
    dn_first, win_got = _a_in_bwd_matmul(dproj, win_g, tmw, max(nt - max(nt // 4, 1), 1), win_hi16, win_got)
    grad_x, dg_apre = _a_in_bwd(dn_first, dproj, xs, dh1, win_g, g_apre, tm)
    shards2 = [_add_win(where, win_lo, win_hi, win_got, "add_devices_a_w_in"),
               _add_devices(where, by_chip(dw_outa, D), outa_got, "add_devices_a_w_out")]
    drel, dsink = _bias_fold(dtab)

    smalls = jnp.concatenate([
        dg_apre[0:1], dg_b[2:3], dg_b[0:1], dg_b[1:2], dg_bpost[0:1], _pad_cols(dsink[0:1], D),
        _pad_cols(loss_acc[0:1], D), jnp.zeros((1, D), F32), dconv_w], axis=0)
    assert smalls.shape == (SMALL_ROWS, D)
    (g_win, g_wouta), gathered = _share_and_gather(shards2, (smalls, drel))
    tot, tot_rel = _sum_smalls(gathered)

    big = {}
    for nm, g, w, m, v in [("a_w_in", g_win, a_w_in, m_a_w_in, v_a_w_in), ("a_w_out", g_wouta, a_w_out, m_a_w_out, v_a_w_out),
                           ("w_kv", g_wkv, w_kv, m_w_kv, v_w_kv), ("b_w_in", g_wbin, b_w_in, m_b_w_in, v_b_w_in),
                           ("b_w_out", g_woutb, b_w_out, m_b_w_out, v_b_w_out)]:
        shp = w.shape
        two = (shp[-2], shp[-1])
        d, nm_, nv_ = _adamw(g, w.reshape(two), m.reshape(two), v.reshape(two), "adamw_" + nm)
        big[nm] = (g.reshape(shp), d.reshape(shp), nm_.reshape(shp), nv_.reshape(shp))

    given = {"a_pre_norm": (a_pre_norm, m_a_pre_norm, v_a_pre_norm), "a_conv_w": (a_conv_w, m_a_conv_w, v_a_conv_w),
             "a_post_norm": (a_post_norm, m_a_post_norm, v_a_post_norm), "kv_norm": (kv_norm, m_kv_norm, v_kv_norm),
             "rel_bias": (rel_bias, m_rel_bias, v_rel_bias), "b_pre_norm": (b_pre_norm, m_b_pre_norm, v_b_pre_norm),
             "b_sinks": (b_sinks, m_b_sinks, v_b_sinks), "b_post_norm": (b_post_norm, m_b_post_norm, v_b_post_norm)}
    small = _small_update(where[1:2], tot, tot_rel, {nm: tuple(a.reshape(SMALL_PLACES[nm][2]) for a in wmv)
                                            for nm, wmv in given.items()})
    order = ["a_pre_norm", "a_w_in", "a_conv_w", "a_w_out", "a_post_norm", "kv_norm", "w_kv", "rel_bias",
             "b_pre_norm", "b_w_in", "b_sinks", "b_w_out", "b_post_norm"]
    outs = []
    for which in range(4):
        for nm in order:
            outs.append(big[nm][which] if nm in big else small[nm][which].reshape(given[nm][0].shape))
    loss = 0.5 * tot[LOSS_ROW, 0]
    return (loss, grad_x.reshape(x.shape), *outs)
```

```python
import math

import jax
import jax.numpy as jnp
from jax import lax
from jax.experimental import pallas as pl
from jax.experimental.pallas import tpu as pltpu

F32 = jnp.float32
BF16 = jnp.bfloat16
MESH = pl.DeviceIdType.MESH
SDS = jax.ShapeDtypeStruct

D = 1024
HEAD_DIM = 64
N_HEADS = 16
N_KV = 2
GROUP = 8
KV_W = 128
BLK = 128
N_BUCKETS = 32
MAX_EXACT = 16
MAX_DISTANCE = 128
EPS = 1e-6
NEG_INF = -1e30
Q_SCALE = HEAD_DIM ** -0.5

ADAM_LR = 0.001
ADAM_B1 = 0.9
ADAM_B2 = 0.999
ADAM_EPS = 1e-08
ADAM_WD = 0.01
ADAM_STEP = 10

N_CHIPS = 4
N_DEV = 8
BIN_COLS = 2 * D // N_CHIPS
VMEM_LIMIT = 56 * 1024 * 1024
SMALL_ROWS = 16
LOSS_ROW = 6
SMALL_PLACES = {
    "a_pre_norm": ("quarter", 0, (1, D // 4)), "a_conv_w": ("quarter", 8, (3, 1, D // 4)),
    "a_post_norm": ("quarter", 1, (1, D // 4)), "kv_norm": ("rows", 2, (1, D)),
    "rel_bias": ("rel", 0, (N_HEADS, N_BUCKETS)), "b_pre_norm": ("rows", 3, (1, D)),
    "b_sinks": ("rows", 5, (1, N_HEADS)), "b_post_norm": ("rows", 4, (1, D)),
}


def _bucket_thresholds():
    def bucket(d):
        big = MAX_EXACT + int(math.log(d / MAX_EXACT) / math.log(MAX_DISTANCE / MAX_EXACT)
                              * (N_BUCKETS - MAX_EXACT))
        return d if d < MAX_EXACT else min(big, N_BUCKETS - 1)
    out = []
    for b in range(MAX_EXACT + 1, N_BUCKETS):
        out.append(min(d for d in range(MAX_EXACT, MAX_DISTANCE) if bucket(d) >= b))
    return tuple(out)


BUCKET_THRESHOLDS = _bucket_thresholds()


def _params(semantics=None, vmem=VMEM_LIMIT):
    return pltpu.CompilerParams(dimension_semantics=semantics, vmem_limit_bytes=vmem)


def _tile(n, pref):
    return pref if n >= 2 * pref else max(n // 2, 8)


def _rms_scale(v):
    return lax.rsqrt(jnp.mean(v * v, axis=-1, keepdims=True) + EPS)


def _nt(a, b):
    return lax.dot_general(a, b, (((1,), (1,)), ((), ())), preferred_element_type=F32)


def _tn(a, b):
    return lax.dot_general(a, b, (((0,), (0,)), ((), ())), preferred_element_type=F32)


def _nn(a, b):
    return jnp.dot(a, b, preferred_element_type=F32)


def _silu_parts(z):
    sg = jax.nn.sigmoid(z)
    return sg, z * sg


def _dsilu(z, sg):
    return sg * (1.0 + z * (1.0 - sg))


def _write_gradient(acc, out32, out16, stage, sem):
    whole = pltpu.make_async_copy(acc, out32, sem)
    whole.start()
    rows = stage.shape[0]
    for k in range(acc.shape[0] // rows):
        stage[...] = acc[rows * k:rows * (k + 1), :].astype(BF16)
        pltpu.sync_copy(stage, out16.at[pl.ds(rows * k, rows)])
    whole.wait()


def _acc_row(ref, row, val):
    ref[row:row + 1, :] += val


def _gather_copies(outs, splits, ici_send, ici_recv, d2d_send, d2d_recv):
    x, y, c = lax.axis_index("x"), lax.axis_index("y"), lax.axis_index("c")
    k = 2 * x + y
    sibling = (x, y, 1 - c)

    def part(o_ref, chip, core, split):
        if not split:
            return o_ref.at[chip]
        h = o_ref.shape[1] // 2
        return o_ref.at[chip, pl.ds(pl.multiple_of(core * h, 16), h)]

    def remote(ref, a, j, sems, to):
        return pltpu.make_async_remote_copy(src_ref=ref, dst_ref=ref, send_sem=sems[0].at[3 * a + j],
                                            recv_sem=sems[1].at[3 * a + j], device_id=to, device_id_type=MESH)

    copies = []
    for a, (o_ref, split) in enumerate(zip(outs, splits)):
        for j, (px, py) in enumerate([(x, 1 - y), (1 - x, y), (1 - x, 1 - y)]):
            kj = 2 * px + py
            ici, d2d = (ici_send, ici_recv), (d2d_send, d2d_recv)
            copies.append((remote(part(o_ref, k, c, split), a, j, ici, (px, py, c)),
                           remote(part(o_ref, kj, c, split), a, j, ici, (px, py, c)),
                           remote(part(o_ref, kj, c, split), a, j, d2d, sibling) if split else None,
                           remote(part(o_ref, kj, 1 - c, split), a, j, d2d, sibling) if split else None))
    return copies


def _gather_sems(n):
    return [pltpu.SemaphoreType.DMA((3 * n,)) for _ in range(4)]


def _prepare_weights(shards, small):
    n = len(shards)

    def body(*refs):
        ins, small_in = refs[:n], refs[n]
        outs, small_out = refs[n + 1:2 * n + 1], refs[2 * n + 1]
        stages, put_sem = refs[2 * n + 2:3 * n + 2], refs[3 * n + 2]
        sems = refs[3 * n + 3:]
        k = 2 * lax.axis_index("x") + lax.axis_index("y")
        puts = []
        for a, (i_ref, stage, o_ref) in enumerate(zip(ins, stages, outs)):
            stage[...] = i_ref[...].astype(BF16)
            puts.append(pltpu.make_async_copy(stage, o_ref.at[k], put_sem.at[a]))
            puts[-1].start()
        small_out[k] = small_in[...]
        copies = _gather_copies([small_out], [False], *sems)
        for send, _, _, _ in copies:
            send.start()
        for _, arrival, _, _ in copies:
            arrival.wait_recv()
        for send, _, _, _ in copies:
            send.wait_send()
        for put in puts:
            put.wait()

    vm = pl.BlockSpec(memory_space=pltpu.VMEM)
    anyspace = pl.BlockSpec(memory_space=pl.ANY)
    out_shape = [SDS((N_CHIPS,) + s.shape, BF16) for s in shards] + [SDS((N_CHIPS,) + small.shape, F32)]
    return pl.pallas_call(
        body, name="prepare_weights", out_shape=out_shape,
        in_specs=[vm] * (n + 1), out_specs=[anyspace] * n + [vm],
        scratch_shapes=[pltpu.VMEM(s.shape, BF16) for s in shards] + [pltpu.SemaphoreType.DMA((n,))] + _gather_sems(1),
        compiler_params=pltpu.CompilerParams(vmem_limit_bytes=VMEM_LIMIT),
    )(*shards, small)


def _a_in(chip, x, g_pre, weights, tm):
    s = x.shape[0]
    nt = s // tm
    n = len(weights)

    def body(chip_ref, x_ref, g_ref, *refs):
        proj_ref, n1_ref = refs[n:n + 2]
        gathered = refs[n + 2:2 * n + 2]
        wbuf, n1_all, fetch_sem = refs[2 * n + 2:2 * n + 5]
        sems = refs[2 * n + 5:]
        jj, i = pl.program_id(0), pl.program_id(1)
        copies = _gather_copies(gathered, [True] * n, *sems)

        def fetch(rel):
            slot = jnp.bitwise_xor(chip_ref[0], rel)
            return pltpu.make_async_copy(gathered[0].at[slot], wbuf.at[rel % 2], fetch_sem.at[rel % 2])

        @pl.when((jj == 0) & (i == 0))
        def _():
            fetch(0).start()
            copies[0][0].start()
            copies[1][0].start()
            fetch(0).wait()

        for rel in (1, 2, 3):
            @pl.when((jj == rel) & (i == 0))
            def _():
                fetch(rel).wait()

        @pl.when(jj == 0)
        def _():
            xv = x_ref[...]
            n1 = (xv * _rms_scale(xv) * g_ref[...]).astype(BF16)
            n1_ref[...] = n1
            n1_all[i] = n1
        proj_ref[...] = _nn(n1_all[i], wbuf[jj % 2]).astype(BF16)

        for rel in (1, 2, 3):
            @pl.when((jj == rel - 1) & (i == max(nt - 3, 0)))
            def _():
                _, arrival, forward, _ = copies[rel - 1]
                arrival.wait_recv()
                forward.start()
                if rel == 1:
                    for send, _, _, _ in copies[2:]:
                        send.start()

            @pl.when((jj == rel - 1) & (i == max(nt - 2, 0)))
            def _():
                copies[rel - 1][3].wait_recv()
                fetch(rel).start()

        @pl.when((jj == 3) & (i == max(nt - 2, 0)))
        def _():
            for _, arrival, forward, _ in copies[3:]:
                arrival.wait_recv()
                forward.start()

        @pl.when((jj == 3) & (i == nt - 1))
        def _():
            for _, _, _, forwarded in copies[3:]:
                forwarded.wait_recv()
            for send, _, forward, _ in copies:
                forward.wait_send()
                send.wait_send()

    anyspace = pl.BlockSpec(memory_space=pl.ANY)
    proj, n1, *gathered = pl.pallas_call(
        body, name="a_in",
        grid_spec=pltpu.PrefetchScalarGridSpec(
            num_scalar_prefetch=1, grid=(4, nt),
            in_specs=[pl.BlockSpec((tm, D), lambda jj, i, c: (jnp.where(jj == 0, i, nt - 1), 0)),
                      pl.BlockSpec((1, D), lambda jj, i, c: (0, 0))] + [anyspace] * n,
            out_specs=[pl.BlockSpec((tm, D), lambda jj, i, c: (i, jnp.bitwise_xor(c[0], jj))),
                       pl.BlockSpec((tm, D), lambda jj, i, c: (jnp.where(jj == 0, i, nt - 1), 0))] + [anyspace] * n,
            scratch_shapes=[pltpu.VMEM((2, D, D), BF16), pltpu.VMEM((nt, tm, D), BF16),
                            pltpu.SemaphoreType.DMA((2,))] + _gather_sems(n)),
        out_shape=[SDS((s, 4 * D), BF16), SDS((s, D), BF16)] + [SDS(w.shape, w.dtype) for w in weights],
        input_output_aliases={3 + a: 2 + a for a in range(n)},
        compiler_params=_params(("arbitrary", "arbitrary")),
    )(chip, x, g_pre, *weights)
    return proj, n1, gathered


def _shift_rows(v, last, second_last, rows):
    v1 = jnp.where(rows >= 1, pltpu.roll(v, 1, 0), last)
    v2 = jnp.where(rows >= 2, pltpu.roll(v, 2, 0), jnp.where(rows == 1, last, second_last))
    return v1, v2


def _a_mix(proj, x, conv_w, w_out, g_post, tm):
    s = x.shape[0]

    def body(proj_ref, x_ref, cw_ref, w_ref, g_ref, ya_ref, oa_ref, h1_ref, conv_ref, carry):
        @pl.when(pl.program_id(0) == 0)
        def _():
            carry[...] = jnp.zeros_like(carry)
        v = proj_ref[:, D:2 * D].astype(F32) * proj_ref[:, 2 * D:3 * D].astype(F32)
        rows = lax.broadcasted_iota(jnp.int32, (tm, D), 0)
        before = carry[...]
        v1, v2 = _shift_rows(v, before[7:8, :], before[6:7, :], rows)
        carry[...] = v[tm - 8:tm, :]
        conv = cw_ref[0:1, :] * v2 + cw_ref[1:2, :] * v1 + cw_ref[2:3, :] * v
        conv_ref[...] = conv.astype(BF16)
        _, sz = _silu_parts(proj_ref[:, 3 * D:4 * D].astype(F32))
        ya = (proj_ref[:, 0:D].astype(F32) * conv * sz).astype(BF16)
        ya_ref[...] = ya
        oa = _nn(ya, w_ref[...])
        oa_ref[...] = oa.astype(BF16)
        h1_ref[...] = x_ref[...] + oa * _rms_scale(oa) * g_ref[...]

    row = lambda i: (i, 0)
    fix = lambda i: (0, 0)
    return pl.pallas_call(
        body, name="a_mix", grid=(s // tm,),
        in_specs=[pl.BlockSpec((tm, 4 * D), row), pl.BlockSpec((tm, D), row), pl.BlockSpec((8, D), fix),
                  pl.BlockSpec((D, D), fix), pl.BlockSpec((1, D), fix)],
        out_specs=[pl.BlockSpec((tm, D), row)] * 4,
        out_shape=[SDS((s, D), BF16), SDS((s, D), BF16), SDS((s, D), F32), SDS((s, D), BF16)],
        scratch_shapes=[pltpu.VMEM((8, D), F32)],
        compiler_params=_params(("arbitrary",)),
    )(proj, x, conv_w, w_out, g_post)


def _b_in(h1, g_kv, g_pre, w_kv, wbin_g, tm):
    s = h1.shape[0]

    def body(h_ref, gk_ref, gb_ref, wkv_ref, wb_ref, kv_ref, q_ref, z_ref):
        h = h_ref[...]
        hh = h * _rms_scale(h)
        nk = (hh * gk_ref[...]).astype(BF16)
        nb = (hh * gb_ref[...]).astype(BF16)
        kv_ref[...] = _nn(nk, wkv_ref[...]).astype(BF16)
        for j in range(2):
            q_ref[:, BIN_COLS * j:BIN_COLS * (j + 1)] = (_nn(nb, wb_ref[j]) * Q_SCALE).astype(BF16)
            z_ref[:, BIN_COLS * j:BIN_COLS * (j + 1)] = _nn(nb, wb_ref[2 + j]).astype(BF16)

    row = lambda i: (i, 0)
    fix = lambda i: (0, 0)
    return pl.pallas_call(
        body, name="b_in", grid=(s // tm,),
        in_specs=[pl.BlockSpec((tm, D), row), pl.BlockSpec((1, D), fix), pl.BlockSpec((1, D), fix),
                  pl.BlockSpec((D, 2 * KV_W), fix), pl.BlockSpec((N_CHIPS, D, BIN_COLS), lambda i: (0, 0, 0))],
        out_specs=[pl.BlockSpec((tm, 2 * KV_W), row), pl.BlockSpec((tm, D), row), pl.BlockSpec((tm, D), row)],
        out_shape=[SDS((s, 2 * KV_W), BF16), SDS((s, D), BF16), SDS((s, D), BF16)],
        compiler_params=_params(("parallel",)),
    )(h1, g_kv, g_pre, w_kv, wbin_g)


def _buckets(dist):
    bucket = jnp.where(dist < MAX_EXACT, dist, MAX_EXACT)
    for t in BUCKET_THRESHOLDS:
        bucket = bucket + jnp.where(dist >= t, 1, 0)
    return bucket


def _head_place(h):
    kh, j, e = h // GROUP, (h % GROUP) // 2, h % 2
    return kh, slice(BLK * j, BLK * (j + 1)), slice(2 * BLK * e, 2 * BLK * (e + 1))


def _bias_table(rel_bias, sinks):
    def body(rb_ref, sink_ref, tab_ref):
        along = lax.broadcasted_iota(jnp.int32, (8, BLK), 1)
        row8 = lax.broadcasted_iota(jnp.int32, (8, BLK), 0)
        bucket = _buckets(jnp.where(along == 0, 0, BLK - along))
        query = lax.broadcasted_iota(jnp.int32, (BLK, BLK), 0)
        col = lax.broadcasted_iota(jnp.int32, (BLK, BLK), 1)
        for h in range(N_HEADS):
            by_dist = jnp.zeros((8, BLK), F32)
            for b in range(N_BUCKETS):
                by_dist = jnp.where(bucket == b, rb_ref[h, b], by_dist)
            for digit in range(3):
                by_dist = jnp.where((row8 >> digit) & 1 == 1, pltpu.roll(by_dist, 1 << digit, 1), by_dist)
            band = jnp.concatenate([by_dist] + [pltpu.roll(by_dist, 8 * g, 1) for g in range(1, BLK // 8)], axis=0)
            cur = jnp.where(col <= query, band, NEG_INF)
            kh, rows, cols = _head_place(h)
            prev_cols, cur_cols = slice(cols.start, cols.start + BLK), slice(cols.start + BLK, cols.stop)
            tab_ref[1, kh, rows, prev_cols] = jnp.where(col == 0, sink_ref[h], jnp.where(col > query, band, NEG_INF))
            tab_ref[1, kh, rows, cur_cols] = cur
            tab_ref[0, kh, rows, prev_cols] = jnp.where(col == 0, sink_ref[h], NEG_INF)
            tab_ref[0, kh, rows, cur_cols] = cur

    return pl.pallas_call(
        body, name="bias_table", out_shape=SDS((2, N_KV, 4 * BLK, 4 * BLK), F32),
        in_specs=[pl.BlockSpec(memory_space=pltpu.SMEM), pl.BlockSpec(memory_space=pltpu.SMEM)],
        out_specs=pl.BlockSpec(memory_space=pltpu.VMEM),
    )(rel_bias, sinks)


def _bias_fold(dtab):
    def body(dtab_ref, out_ref, dsink_ref):
        lane = lax.broadcasted_iota(jnp.int32, (N_HEADS, BLK), 1)
        bucket = _buckets(lane)
        col = lax.broadcasted_iota(jnp.int32, (BLK, BLK), 1)
        row8 = lax.broadcasted_iota(jnp.int32, (8, 128), 0)
        lane8 = lax.broadcasted_iota(jnp.int32, (8, 128), 1)
        by_dist = jnp.zeros((BLK, 128), F32)
        dsink = jnp.zeros((8, 128), F32)
        for h in range(N_HEADS):
            kh, rows, cols = _head_place(h)
            dt = dtab_ref[kh, rows, cols]
            band = jnp.where(col == 0, 0.0, dt[:, 0:BLK]) + dt[:, BLK:2 * BLK]
            for digit in range(BLK.bit_length() - 1):
                band = jnp.where((col >> digit) & 1 == 1, pltpu.roll(band, BLK - (1 << digit), 0), band)
            by_dist = jnp.where(col == h, jnp.sum(band, axis=1, keepdims=True), by_dist)
            dsink = dsink + jnp.where((row8 == 0) & (lane8 == h), jnp.sum(dt[:, 0:1]), 0.0)
        by_head = by_dist.T[0:N_HEADS]
        folded = jnp.zeros((N_HEADS, 128), F32)
        for b in range(N_BUCKETS):
            folded = jnp.where(lane == b, jnp.sum(jnp.where(bucket == b, by_head, 0.0), axis=1, keepdims=True), folded)
        out_ref[...] = folded
        dsink_ref[...] = dsink

    vm = pl.BlockSpec(memory_space=pltpu.VMEM)
    return pl.pallas_call(
        body, name="bias_fold", out_shape=[SDS((N_HEADS, 128), F32), SDS((8, 128), F32)],
        in_specs=[vm], out_specs=[vm, vm],
    )(dtab)


def _pair_operands(prev, cur):
    t = jnp.concatenate([prev, cur], axis=0).astype(F32)
    t = jnp.where(lax.broadcasted_iota(jnp.int32, t.shape, 0) == 0, 0.0, t)
    tr = pltpu.roll(t, HEAD_DIM, 1)
    lo = lax.broadcasted_iota(jnp.int32, t.shape, 1) < HEAD_DIM
    zero = jnp.zeros_like(t)
    head0 = jnp.concatenate([jnp.where(lo, t, zero), jnp.where(lo, zero, tr)], axis=0).astype(BF16)
    head1 = jnp.concatenate([jnp.where(lo, tr, zero), jnp.where(lo, zero, t)], axis=0).astype(BF16)
    return head0, head1


def _pair_fold(d0, d1):
    lo = lax.broadcasted_iota(jnp.int32, (2 * BLK, KV_W), 1) < HEAD_DIM
    zero = jnp.zeros((2 * BLK, KV_W), F32)
    g0 = jnp.where(lo, d0[0:256], zero) + pltpu.roll(jnp.where(lo, zero, d0[256:512]), HEAD_DIM, 1)
    g1 = pltpu.roll(jnp.where(lo, d1[0:256], zero), HEAD_DIM, 1) + jnp.where(lo, zero, d1[256:512])
    return jnp.where(lax.broadcasted_iota(jnp.int32, (2 * BLK, KV_W), 0) == 0, 0.0, g0 + g1)


def _stack_pairs(ref, kh):
    return jnp.concatenate([ref[:, 128 * (4 * kh + j):128 * (4 * kh + j + 1)] for j in range(4)], axis=0)


def _table_spec():
    return pl.BlockSpec((1, N_KV, 4 * BLK, 4 * BLK), lambda n: (jnp.minimum(n, 1), 0, 0, 0))


def _attn_fwd(q, kv, tab):
    s = q.shape[0]

    def body(q_ref, kp_ref, kc_ref, vp_ref, vc_ref, tab_ref, att_ref, stats_ref):
        k2 = _pair_operands(kp_ref[...], kc_ref[...])
        v2 = _pair_operands(vp_ref[...], vc_ref[...])
        lane = lax.broadcasted_iota(jnp.int32, (BLK, 128), 1)
        stats = jnp.zeros((BLK, 128), F32)
        for kh in range(N_KV):
            sc = _nt(_stack_pairs(q_ref, kh), k2[kh])
            ps = []
            for e in range(2):
                lg = sc[:, 256 * e:256 * (e + 1)] + tab_ref[0, kh, :, 256 * e:256 * (e + 1)]
                m = jnp.max(lg, axis=-1, keepdims=True)
                ex = jnp.exp(lg - m)
                den = jnp.sum(ex, axis=-1, keepdims=True)
                ps.append(ex * (1.0 / den))
                lse = m + jnp.log(den)
                for j in range(4):
                    stats = jnp.where(lane == GROUP * kh + 2 * j + e, lse[BLK * j:BLK * (j + 1)], stats)
            out = _nn(jnp.concatenate(ps, axis=1).astype(BF16), v2[kh])
            for j in range(4):
                att_ref[:, 128 * (4 * kh + j):128 * (4 * kh + j + 1)] = out[BLK * j:BLK * (j + 1)].astype(BF16)
        stats_ref[...] = stats

    cur = lambda n: (n, 0)
    prev = lambda n: (jnp.maximum(n - 1, 0), 0)
    return pl.pallas_call(
        body, name="attn_fwd", grid=(s // BLK,),
        in_specs=[pl.BlockSpec((BLK, D), cur),
                  pl.BlockSpec((BLK, KV_W), prev), pl.BlockSpec((BLK, KV_W), cur),
                  pl.BlockSpec((BLK, KV_W), lambda n: (jnp.maximum(n - 1, 0), 1)),
                  pl.BlockSpec((BLK, KV_W), lambda n: (n, 1)), _table_spec()],
        out_specs=[pl.BlockSpec((BLK, D), cur), pl.BlockSpec((BLK, 128), cur)],
        out_shape=[SDS((s, D), BF16), SDS((s, 128), F32)],
        compiler_params=_params(("parallel",)),
    )(q, kv, kv, kv, kv, tab)


def _mid(att, zb, h1, tgt, w_out, g_post, tm):
    s = att.shape[0]
    nt = s // tm

    def body(att_ref, z_ref, h1_ref, t_ref, w_ref, g_ref,
             dh_ref, dqz_ref, datt_ref, loss_ref, dg_ref, dw_ref, dw16_ref, dw_acc, stage, put_sem):
        @pl.when(pl.program_id(0) == 0)
        def _():
            loss_ref[...] = jnp.zeros_like(loss_ref)
            dg_ref[...] = jnp.zeros_like(dg_ref)
            dw_acc[...] = jnp.zeros_like(dw_acc)
        att = att_ref[...].astype(F32)
        z = z_ref[...].astype(F32)
        sg, sz = _silu_parts(z)
        ob = (att * sz).astype(BF16)
        y2 = _nn(ob, w_ref[...])
        r2 = _rms_scale(y2)
        yh = y2 * r2
        g = g_ref[...]
        err = (h1_ref[...] + yh * g) - t_ref[...]
        loss_ref[...] += jnp.sum(jnp.sum(err * err, axis=-1, keepdims=True) / D)
        dh = err / D
        dh_ref[...] = dh
        _acc_row(dg_ref, 0, jnp.sum(dh * yh, axis=0, keepdims=True))
        dyh = dh * g
        dy = (r2 * (dyh - yh * jnp.mean(dyh * yh, axis=-1, keepdims=True))).astype(BF16)
        dw_acc[...] += _tn(ob, dy)
        dob = _nt(dy, w_ref[...])
        datt_ref[...] = (dob * sz).astype(BF16)
        dqz_ref[...] = (dob * att * _dsilu(z, sg)).astype(BF16)

        @pl.when(pl.program_id(0) == nt - 1)
        def _():
            _write_gradient(dw_acc, dw_ref, dw16_ref, stage, put_sem)

    row = lambda i: (i, 0)
    fix = lambda i: (0, 0)
    anyspace = pl.BlockSpec(memory_space=pl.ANY)
    return pl.pallas_call(
        body, name="mid", grid=(nt,),
        in_specs=[pl.BlockSpec((tm, D), row)] * 4 + [pl.BlockSpec((D, D), fix), pl.BlockSpec((1, D), fix)],
        out_specs=[pl.BlockSpec((tm, D), row), pl.BlockSpec((tm, D), lambda i: (i, 1)), pl.BlockSpec((tm, D), row),
                   pl.BlockSpec((8, 128), fix), pl.BlockSpec((8, D), fix), anyspace, anyspace],
        out_shape=[SDS((s, D), F32), SDS((s, 2 * D), BF16), SDS((s, D), BF16), SDS((8, 128), F32),
                   SDS((8, D), F32), SDS((D, D), F32), SDS((D, D), BF16)],
        scratch_shapes=[pltpu.VMEM((D, D), F32), pltpu.VMEM((D // 4, D), BF16), pltpu.SemaphoreType.DMA],
        compiler_params=_params(("arbitrary",)),
    )(att, zb, h1, tgt, w_out, g_post)


def _attn_bwd(q, kv, datt, stats, tab, dqz):
    s = q.shape[0]
    nb = s // BLK

    def body(q_ref, kp_ref, kc_ref, vp_ref, vc_ref, da_ref, st_ref, tab_ref, dqz_in,
             dq_ref, dkv_ref, dtab_ref, dk_carry, dv_carry):
        del dqz_in
        n = pl.program_id(0)

        @pl.when(n == 0)
        def _():
            dtab_ref[...] = jnp.zeros_like(dtab_ref)
            dk_carry[...] = jnp.zeros_like(dk_carry)
            dv_carry[...] = jnp.zeros_like(dv_carry)

        @pl.when(n < nb)
        def _():
            k2 = _pair_operands(kp_ref[...], kc_ref[...])
            v2 = _pair_operands(vp_ref[...], vc_ref[...])
            lane = lax.broadcasted_iota(jnp.int32, (BLK, 128), 1)
            stats = st_ref[...]
            dk2, dv2 = [], []
            for kh in range(N_KV):
                qs = _stack_pairs(q_ref, kh)
                das = _stack_pairs(da_ref, kh)
                sc = _nt(qs, k2[kh])
                dp = _nt(das, v2[kh])
                ps, dss = [], []
                for e in range(2):
                    heads = [GROUP * kh + 2 * j + e for j in range(4)]
                    lse = jnp.concatenate([jnp.sum(jnp.where(lane == h, stats, 0.0), axis=-1, keepdims=True)
                                           for h in heads], axis=0)
                    cols = slice(256 * e, 256 * (e + 1))
                    p = jnp.exp(sc[:, cols] + tab_ref[0, kh, :, cols] - lse)
                    delta = jnp.sum(p * dp[:, cols], axis=-1, keepdims=True)
                    ds = p * (dp[:, cols] - delta)
                    dtab_ref[kh, :, cols] += ds
                    ps.append(p)
                    dss.append(ds)
                p2 = jnp.concatenate(ps, axis=1).astype(BF16)
                ds2 = jnp.concatenate(dss, axis=1).astype(BF16)
                dq = _nn(ds2, k2[kh]) * Q_SCALE
                for j in range(4):
                    dq_ref[:, 128 * (4 * kh + j):128 * (4 * kh + j + 1)] = dq[BLK * j:BLK * (j + 1)].astype(BF16)
                dk2.append(_tn(ds2, qs))
                dv2.append(_tn(p2, das))
            dkk = _pair_fold(dk2[0], dk2[1])
            dvv = _pair_fold(dv2[0], dv2[1])
            dkv_ref[:, 0:KV_W] = (dk_carry[...] + dkk[0:BLK]).astype(BF16)
            dkv_ref[:, KV_W:2 * KV_W] = (dv_carry[...] + dvv[0:BLK]).astype(BF16)
            dk_carry[...] = dkk[BLK:2 * BLK]
            dv_carry[...] = dvv[BLK:2 * BLK]

        @pl.when(n == nb)
        def _():
            dkv_ref[:, 0:KV_W] = dk_carry[...].astype(BF16)
            dkv_ref[:, KV_W:2 * KV_W] = dv_carry[...].astype(BF16)

    cur = lambda n: (jnp.minimum(n, nb - 1), 0)
    prev = lambda n: (jnp.clip(n - 1, 0, nb - 1), 0)
    return pl.pallas_call(
        body, name="attn_bwd", grid=(nb + 1,),
        in_specs=[pl.BlockSpec((BLK, D), cur),
                  pl.BlockSpec((BLK, KV_W), prev), pl.BlockSpec((BLK, KV_W), cur),
                  pl.BlockSpec((BLK, KV_W), lambda n: (jnp.clip(n - 1, 0, nb - 1), 1)),
                  pl.BlockSpec((BLK, KV_W), lambda n: (jnp.minimum(n, nb - 1), 1)),
                  pl.BlockSpec((BLK, D), cur), pl.BlockSpec((BLK, 128), cur), _table_spec(),
                  pl.BlockSpec(memory_space=pl.ANY)],
        out_specs=[pl.BlockSpec((BLK, D), cur), pl.BlockSpec((BLK, 2 * KV_W), prev),
                   pl.BlockSpec((N_KV, 4 * BLK, 4 * BLK), lambda n: (0, 0, 0))],
        out_shape=[SDS((s, 2 * D), BF16), SDS((s, 2 * KV_W), BF16), SDS((N_KV, 4 * BLK, 4 * BLK), F32)],
        scratch_shapes=[pltpu.VMEM((BLK, KV_W), F32), pltpu.VMEM((BLK, KV_W), F32)],
        input_output_aliases={8: 0},
        compiler_params=_params(("arbitrary",)),
    )(q, kv, kv, kv, kv, datt, stats, tab, dqz)


def _b_bwd(dqz, dkv, h1, dh2, oa, wbin_g, w_kv, g_kv, g_pre, g_apost, tm):
    s = h1.shape[0]
    nt = s // tm

    def body(dqz_ref, dkv_ref, h_ref, dh2_ref, oa_ref, wb_ref, wkv_ref, gk_ref, gb_ref, ga_ref,
             dh1_ref, doa_ref, dg_ref, dwb_ref, dwkv_ref, dwb16_ref, dwkv16_ref, wcat, dwb_acc, dwkv_acc, put_sem):
        @pl.when(pl.program_id(0) == 0)
        def _():
            dg_ref[...] = jnp.zeros_like(dg_ref)
            dwb_acc[...] = jnp.zeros_like(dwb_acc)
            dwkv_acc[...] = jnp.zeros_like(dwkv_acc)
            for j in range(N_CHIPS):
                pltpu.sync_copy(wb_ref.at[j], wcat.at[:, pl.ds(BIN_COLS * j, BIN_COLS)])
        dnb = _nt(dqz_ref[...], wcat[...])
        dnk = _nt(dkv_ref[...], wkv_ref[...])
        h = h_ref[...]
        r = _rms_scale(h)
        hh = h * r
        dwb_acc[...] += _tn((hh * gb_ref[...]).astype(BF16), dqz_ref[...])
        dwkv_acc[...] += _tn((hh * gk_ref[...]).astype(BF16), dkv_ref[...])
        _acc_row(dg_ref, 0, jnp.sum(dnk * hh, axis=0, keepdims=True))
        _acc_row(dg_ref, 1, jnp.sum(dnb * hh, axis=0, keepdims=True))
        dhh = dnb * gb_ref[...] + dnk * gk_ref[...]
        dh1 = dh2_ref[...] + r * (dhh - hh * jnp.mean(dhh * hh, axis=-1, keepdims=True))
        dh1_ref[...] = dh1
        oa = oa_ref[...].astype(F32)
        ra = _rms_scale(oa)
        oh = oa * ra
        _acc_row(dg_ref, 2, jnp.sum(dh1 * oh, axis=0, keepdims=True))
        doh = dh1 * ga_ref[...]
        doa_ref[...] = (ra * (doh - oh * jnp.mean(doh * oh, axis=-1, keepdims=True))).astype(BF16)

        @pl.when(pl.program_id(0) == nt - 1)
        def _():
            wcat[...] = dwb_acc[...].astype(BF16)
            puts = [pltpu.make_async_copy(dwkv_acc, dwkv_ref, put_sem.at[2 * N_CHIPS])]
            for j in range(N_CHIPS):
                cols = pl.ds(BIN_COLS * j, BIN_COLS)
                puts.append(pltpu.make_async_copy(dwb_acc.at[:, cols], dwb_ref.at[j], put_sem.at[2 * j]))
                puts.append(pltpu.make_async_copy(wcat.at[:, cols], dwb16_ref.at[j], put_sem.at[2 * j + 1]))
            for put in puts:
                put.start()
            for put in puts:
                put.wait()
            wcat[:, 0:2 * KV_W] = dwkv_acc[...].astype(BF16)
            pltpu.sync_copy(wcat.at[:, pl.ds(0, 2 * KV_W)], dwkv16_ref)

    row = lambda i: (i, 0)
    fix = lambda i: (0, 0)
    anyspace = pl.BlockSpec(memory_space=pl.ANY)
    return pl.pallas_call(
        body, name="b_bwd", grid=(nt,),
        in_specs=[pl.BlockSpec((tm, 2 * D), row), pl.BlockSpec((tm, 2 * KV_W), row), pl.BlockSpec((tm, D), row),
                  pl.BlockSpec((tm, D), row), pl.BlockSpec((tm, D), row), anyspace, pl.BlockSpec((D, 2 * KV_W), fix),
                  pl.BlockSpec((1, D), fix), pl.BlockSpec((1, D), fix), pl.BlockSpec((1, D), fix)],
        out_specs=[pl.BlockSpec((tm, D), row), pl.BlockSpec((tm, D), row), pl.BlockSpec((8, D), fix)] + [anyspace] * 4,
        out_shape=[SDS((s, D), F32), SDS((s, D), BF16), SDS((8, D), F32), SDS((N_CHIPS, D, BIN_COLS), F32),
                   SDS((D, 2 * KV_W), F32), SDS((N_CHIPS, D, BIN_COLS), BF16), SDS((D, 2 * KV_W), BF16)],
        scratch_shapes=[pltpu.VMEM((D, 2 * D), BF16), pltpu.VMEM((D, 2 * D), F32), pltpu.VMEM((D, 2 * KV_W), F32),
                        pltpu.SemaphoreType.DMA((2 * N_CHIPS + 1,))],
        compiler_params=_params(("arbitrary",)),
    )(dqz, dkv, h1, dh2, oa, wbin_g, w_kv, g_kv, g_pre, g_apost)


def _to_owner_core(pieces, r, send, recv, core, action):
    x, y, c = lax.axis_index("x"), lax.axis_index("y"), lax.axis_index("c")
    for kp in range(N_CHIPS):
        px, py = kp >> 1, kp & 1
        rel = 4 * (x + px - 2 * x * px) + 2 * (y + py - 2 * y * py) + (c + core - 2 * c * core)

        @pl.when(rel != 0)
        def _():
            cp = pltpu.make_async_remote_copy(src_ref=pieces.at[kp], dst_ref=r.at[rel - 1], send_sem=send.at[kp],
                                              recv_sem=recv.at[rel - 1], device_id=(px, py, core), device_id_type=MESH)
            if action == "start":
                cp.start()
            else:
                cp.wait_send()
    if action == "wait":
        @pl.when(c == core)
        def _():
            for rel in range(1, N_DEV):
                pltpu.make_async_remote_copy(src_ref=pieces.at[0], dst_ref=r.at[rel - 1], send_sem=send.at[0],
                                             recv_sem=recv.at[rel - 1], device_id=(x, y, c),
                                             device_id_type=MESH).wait_recv()


def _owner_core_sems():
    return [pltpu.SemaphoreType.DMA((N_CHIPS,)), pltpu.SemaphoreType.DMA((N_DEV - 1,))]


def _device_exchange(grads, recvs, send, recv):
    x, y, c = lax.axis_index("x"), lax.axis_index("y"), lax.axis_index("c")
    copies = []
    for a, (g, r) in enumerate(zip(grads, recvs)):
        h = g.shape[1] // 2
        for rel in range(1, N_DEV):
            fx, fy, fc = rel >> 2, (rel >> 1) & 1, rel & 1
            px, py, pc = x + fx - 2 * x * fx, y + fy - 2 * y * fy, c + fc - 2 * c * fc
            sem = (N_DEV - 1) * a + rel - 1
            copies.append(pltpu.make_async_remote_copy(
                src_ref=g.at[2 * px + py, pl.ds(pl.multiple_of(pc * h, 16), h)], dst_ref=r.at[rel - 1],
                send_sem=send.at[sem], recv_sem=recv.at[sem], device_id=(px, py, pc), device_id_type=MESH))
    return copies


def _device_exchange_specs(grads):
    anyspace = pl.BlockSpec(memory_space=pl.ANY)
    n = len(grads)
    count = (N_DEV - 1) * n
    return ([anyspace] * n, [anyspace] * n,
            [SDS((N_DEV - 1, g.shape[1] // 2, g.shape[2]), g.dtype) for g in grads],
            [pltpu.SemaphoreType.DMA((count,)), pltpu.SemaphoreType.DMA((count,))])


def _a_bwd(doa, ya, conv, proj, conv_w, w_out, tm, parts):
    s = doa.shape[0]
    nt = s // tm
    n = len(parts)
    ex_in, ex_out, ex_shape, ex_sems = _device_exchange_specs(parts)

    def body(*refs):
        doa_ref, ya_ref, conv_ref, proj_ref, cw_ref, w_ref = refs[:6]
        part_refs = refs[6:6 + n]
        dproj_ref, dcw_ref, dw_ref, dw16_ref = refs[6 + n:10 + n]
        recv_refs = refs[10 + n:10 + 2 * n]
        carry, dw_acc, stage, put_sem, send, recv = refs[10 + 2 * n:]
        i = pl.program_id(0)

        @pl.when(i == 0)
        def _():
            dcw_ref[...] = jnp.zeros_like(dcw_ref)
            carry[...] = jnp.zeros_like(carry)
            dw_acc[...] = jnp.zeros_like(dw_acc)
            for cp in _device_exchange(part_refs, recv_refs, send, recv):
                cp.start()
        dya = _nt(doa_ref[...], w_ref[...])
        dw_acc[...] += _tn(ya_ref[...], doa_ref[...])
        bg = proj_ref[:, 0:D].astype(F32)
        cg = proj_ref[:, D:2 * D].astype(F32)
        u = proj_ref[:, 2 * D:3 * D].astype(F32)
        z = proj_ref[:, 3 * D:4 * D].astype(F32)
        v = cg * u
        rows = lax.broadcasted_iota(jnp.int32, (tm, D), 0)
        conv = conv_ref[...].astype(F32)
        sg, sz = _silu_parts(z)
        dproj_ref[:, 0:D] = (dya * conv * sz).astype(BF16)
        dproj_ref[:, 3 * D:4 * D] = (dya * bg * conv * _dsilu(z, sg)).astype(BF16)
        dconv = dya * bg * sz
        after = carry[...]
        up1 = jnp.where(rows < tm - 1, pltpu.roll(dconv, tm - 1, 0), after[0:1, :])
        up2 = jnp.where(rows < tm - 2, pltpu.roll(dconv, tm - 2, 0),
                        jnp.where(rows == tm - 2, after[0:1, :], after[1:2, :]))
        carry[...] = dconv[0:8, :]
        _acc_row(dcw_ref, 0, jnp.sum(up2 * v, axis=0, keepdims=True))
        _acc_row(dcw_ref, 1, jnp.sum(up1 * v, axis=0, keepdims=True))
        _acc_row(dcw_ref, 2, jnp.sum(dconv * v, axis=0, keepdims=True))
        dv = cw_ref[2:3, :] * dconv + cw_ref[1:2, :] * up1 + cw_ref[0:1, :] * up2
        dproj_ref[:, D:2 * D] = (dv * u).astype(BF16)
        dproj_ref[:, 2 * D:3 * D] = (dv * cg).astype(BF16)

        @pl.when(i == nt - 1)
        def _():
            _write_gradient(dw_acc, dw_ref, dw16_ref, stage, put_sem)
            for cp in _device_exchange(part_refs, recv_refs, send, recv):
                cp.wait()

    rev = lambda i: (nt - 1 - i, 0)
    fix = lambda i: (0, 0)
    anyspace = pl.BlockSpec(memory_space=pl.ANY)
    dproj, dcw, dw, dw16, *got = pl.pallas_call(
        body, name="a_bwd", grid=(nt,),
        in_specs=[pl.BlockSpec((tm, D), rev), pl.BlockSpec((tm, D), rev), pl.BlockSpec((tm, D), rev),
                  pl.BlockSpec((tm, 4 * D), rev), pl.BlockSpec((8, D), fix), pl.BlockSpec((D, D), fix)] + ex_in,
        out_specs=[pl.BlockSpec((tm, 4 * D), rev), pl.BlockSpec((8, D), fix), anyspace, anyspace] + ex_out,
        out_shape=[SDS((s, 4 * D), BF16), SDS((8, D), F32), SDS((D, D), F32), SDS((D, D), BF16)] + ex_shape,
        scratch_shapes=[pltpu.VMEM((8, D), F32), pltpu.VMEM((D, D), F32), pltpu.VMEM((D // 4, D), BF16),
                        pltpu.SemaphoreType.DMA] + ex_sems,
        compiler_params=_params(("arbitrary",)),
    )(doa, ya, conv, proj, conv_w, w_out, *parts)
    return dproj, dcw, dw, dw16, got


def _dn1(dp_ref, w_ref):
    dn = _nt(dp_ref[:, 0:D], w_ref[0])
    for j in range(1, 4):
        dn = dn + _nt(dp_ref[:, D * j:D * (j + 1)], w_ref[j])
    return dn


def _a_in_bwd_matmul(dproj, win_g, tm, count, win_half, win_got):
    def body(dp_ref, w_ref, half_ref, got_in, dn_ref, got_ref, wcat, send, recv):
        del got_in

        @pl.when(pl.program_id(0) == 0)
        def _():
            _to_owner_core(half_ref, got_ref, send, recv, 1, "start")
            for j in range(N_CHIPS):
                pltpu.sync_copy(w_ref.at[j], wcat.at[:, pl.ds(D * j, D)])
        dn_ref[...] = _nt(dp_ref[...], wcat[...]).astype(BF16)

        @pl.when(pl.program_id(0) == count - 1)
        def _():
            _to_owner_core(half_ref, got_ref, send, recv, 1, "wait")

    row = lambda i: (i, 0)
    anyspace = pl.BlockSpec(memory_space=pl.ANY)
    return pl.pallas_call(
        body, name="a_in_bwd_matmul", grid=(count,),
        in_specs=[pl.BlockSpec((tm, 4 * D), row), anyspace, anyspace, anyspace],
        out_specs=[pl.BlockSpec((tm, D), row), anyspace],
        out_shape=[SDS((count * tm, D), BF16), SDS(win_got.shape, win_got.dtype)],
        scratch_shapes=[pltpu.VMEM((D, 4 * D), BF16)] + _owner_core_sems(),
        input_output_aliases={3: 1},
        compiler_params=_params(("arbitrary",)),
    )(dproj, win_g, win_half, win_got)


def _a_in_bwd(dn_first, dproj, x, dh1, win_g, g_pre, tm):
    s = x.shape[0]
    nt = s // tm
    count = dn_first.shape[0] // tm

    def body(dn_ref, dp_ref, x_ref, dh_ref, w_ref, g_ref, gx_ref, dg_ref, dn_s):
        i = pl.program_id(0)

        @pl.when(i == 0)
        def _():
            dg_ref[...] = jnp.zeros_like(dg_ref)

        @pl.when(i < count)
        def _():
            dn_s[...] = dn_ref[...].astype(F32)

        @pl.when(i >= count)
        def _():
            dn_s[...] = _dn1(dp_ref, w_ref)
        dn = dn_s[...]
        xv = x_ref[...]
        r = _rms_scale(xv)
        xh = xv * r
        _acc_row(dg_ref, 0, jnp.sum(dn * xh, axis=0, keepdims=True))
        dxh = dn * g_ref[...]
        gx_ref[...] = dh_ref[...] + r * (dxh - xh * jnp.mean(dxh * xh, axis=-1, keepdims=True))

    row = lambda i: (i, 0)
    fix = lambda i: (0, 0)
    return pl.pallas_call(
        body, name="a_in_bwd", grid=(nt,),
        in_specs=[pl.BlockSpec((tm, D), lambda i: (jnp.minimum(i, count - 1), 0)),
                  pl.BlockSpec((tm, 4 * D), lambda i: (jnp.maximum(i, count), 0)),
                  pl.BlockSpec((tm, D), row), pl.BlockSpec((tm, D), row),
                  pl.BlockSpec((4, D, D), lambda i: (0, 0, 0)), pl.BlockSpec((1, D), fix)],
        out_specs=[pl.BlockSpec((tm, D), row), pl.BlockSpec((8, D), fix)],
        out_shape=[SDS((s, D), F32), SDS((8, D), F32)],
        scratch_shapes=[pltpu.VMEM((tm, D), F32)],
        compiler_params=_params(("arbitrary",)),
    )(dn_first, dproj, x, dh1, win_g, g_pre)


def _swap_halves(shards, send, recv):
    x, y, c = lax.axis_index("x"), lax.axis_index("y"), lax.axis_index("c")
    sibling = (x, y, 1 - c)
    copies = []
    for b, full in enumerate(shards):
        h = full.shape[0] // 2
        mine = full.at[pl.ds(pl.multiple_of(c * h, 8), h)]
        theirs = full.at[pl.ds(pl.multiple_of((1 - c) * h, 8), h)]
        copies.append((pltpu.make_async_remote_copy(src_ref=mine, dst_ref=mine, send_sem=send.at[b], recv_sem=recv.at[b],
                                                    device_id=sibling, device_id_type=MESH),
                       pltpu.make_async_remote_copy(src_ref=mine, dst_ref=theirs, send_sem=send.at[b], recv_sem=recv.at[b],
                                                    device_id=sibling, device_id_type=MESH)))
    return copies


def _dw_in_half(n1, dproj, core, tmw, name, to_owners=None, to_devices=None, shards=()):
    s = n1.shape[0]
    h = D // 2
    nt = s // tmw
    n_sh = len(shards)
    if to_owners is not None:
        sent_array, sems, got_shape = to_owners, _owner_core_sems(), SDS((N_DEV - 1, h, D), BF16)
    else:
        sent_array = to_devices
        _, _, (got_shape,), sems = _device_exchange_specs([to_devices])

    def body(*refs):
        a_ref, b_ref, sent = refs[:3]
        o_ref, o16_ref, got = refs[3 + n_sh:6 + n_sh]
        shard_refs = refs[6 + n_sh:6 + 2 * n_sh]
        send, recv = refs[6 + 2 * n_sh:8 + 2 * n_sh]
        swap_sems = refs[8 + 2 * n_sh:]
        j, t = pl.program_id(0), pl.program_id(1)

        def exchange(action):
            if to_owners is not None:
                _to_owner_core(sent, got, send, recv, 1 - core, action)
            else:
                for cp in _device_exchange([sent], [got], send, recv):
                    cp.start() if action == "start" else cp.wait()

        @pl.when((j == 0) & (t == 0))
        def _():
            exchange("start")
            if n_sh:
                for mine, _ in _swap_halves(shard_refs, *swap_sems):
                    mine.start()

        @pl.when(t == 0)
        def _():
            o_ref[...] = jnp.zeros_like(o_ref)
        o_ref[0] += _tn(a_ref[...], b_ref[...])

        @pl.when(t == nt - 1)
        def _():
            o16_ref[...] = o_ref[...].astype(BF16)

        @pl.when((j == N_CHIPS - 1) & (t == nt - 1))
        def _():
            exchange("wait")
            if n_sh:
                for mine, theirs in _swap_halves(shard_refs, *swap_sems):
                    theirs.wait_recv()
                    mine.wait_send()

    anyspace = pl.BlockSpec(memory_space=pl.ANY)
    slot = pl.BlockSpec((1, h, D), lambda j, t: (j, 0, 0))
    swap_scratch = [pltpu.SemaphoreType.DMA((n_sh,)), pltpu.SemaphoreType.DMA((n_sh,))] if n_sh else []
    return pl.pallas_call(
        body, name=name, grid=(N_CHIPS, nt),
        in_specs=[pl.BlockSpec((tmw, h), lambda j, t: (t, core)), pl.BlockSpec((tmw, D), lambda j, t: (t, j))]
        + [anyspace] * (1 + n_sh),
        out_specs=[slot, slot] + [anyspace] * (1 + n_sh),
        out_shape=[SDS((N_CHIPS, h, D), F32), SDS((N_CHIPS, h, D), BF16), got_shape]
        + [SDS(sh.shape, F32) for sh in shards],
        scratch_shapes=sems + swap_scratch,
        input_output_aliases={3 + b: 3 + b for b in range(n_sh)},
        compiler_params=_params(("arbitrary", "arbitrary")),
    )(n1, dproj, sent_array, *shards)


def _share_and_gather(shards, smalls):
    n_h, n_s = len(shards), len(smalls)

    def body(*refs):
        small_ins = refs[n_h:n_h + n_s]
        fs = refs[n_h + n_s:2 * n_h + n_s]
        small_alls = refs[2 * n_h + n_s:2 * n_h + 2 * n_s]
        dsend, drecv, ssend, srecv = refs[2 * n_h + 2 * n_s:]
        x, y, c = lax.axis_index("x"), lax.axis_index("y"), lax.axis_index("c")
        swaps = _swap_halves(fs, dsend, drecv)
        sends, arrivals = [mine for mine, _ in swaps], [theirs for _, theirs in swaps]
        me = 4 * x + 2 * y + c
        for k, (small_in, small_all) in enumerate(zip(small_ins, small_alls)):
            small_all[me] = small_in[...]
            for rel in range(1, N_DEV):
                fx, fy, fc = rel >> 2, (rel >> 1) & 1, rel & 1
                peer = (x + fx - 2 * x * fx, y + fy - 2 * y * fy, c + fc - 2 * c * fc)
                sender = 4 * peer[0] + 2 * peer[1] + peer[2]
                sem = (N_DEV - 1) * k + rel - 1
                sends.append(pltpu.make_async_remote_copy(
                    src_ref=small_in, dst_ref=small_all.at[me], send_sem=ssend.at[sem], recv_sem=srecv.at[sem],
                    device_id=peer, device_id_type=MESH))
                arrivals.append(pltpu.make_async_remote_copy(
                    src_ref=small_in, dst_ref=small_all.at[sender], send_sem=ssend.at[sem], recv_sem=srecv.at[sem],
                    device_id=peer, device_id_type=MESH))
        for cp in sends:
            cp.start()
        for cp in arrivals:
            cp.wait_recv()
        for cp in sends:
            cp.wait_send()

    anyspace = pl.BlockSpec(memory_space=pl.ANY)
    vm = pl.BlockSpec(memory_space=pltpu.VMEM)
    out_shape = [SDS(full.shape, F32) for full in shards] + [SDS((N_DEV,) + sm.shape, F32) for sm in smalls]
    n_all = (N_DEV - 1) * n_s
    outs = pl.pallas_call(
        body, name="share_and_gather", out_shape=out_shape,
        in_specs=[anyspace] * n_h + [vm] * n_s, out_specs=[anyspace] * n_h + [vm] * n_s,
        scratch_shapes=[pltpu.SemaphoreType.DMA((n_h,)), pltpu.SemaphoreType.DMA((n_h,)),
                        pltpu.SemaphoreType.DMA((n_all,)), pltpu.SemaphoreType.DMA((n_all,))],
        input_output_aliases={b: b for b in range(n_h)},
    )(*shards, *smalls)
    return outs[:n_h], outs[n_h:]


def _add_win(where, lo, hi, r, name):
    _, h, cols = lo.shape
    tr = min(h, 256)
    nh = h // tr

    def body(where_ref, lo_ref, hi_ref, r_ref, o_ref):
        acc = jnp.where(where_ref[0] == 0, lo_ref[0], hi_ref[0])
        for k in range(N_DEV - 1):
            acc = acc + r_ref[k].astype(F32)
        o_ref[...] = acc

    own = pl.BlockSpec((1, tr, cols), lambda i, w: (w[1], i, 0))
    return pl.pallas_call(
        body, name=name,
        grid_spec=pltpu.PrefetchScalarGridSpec(
            num_scalar_prefetch=1, grid=(nh,),
            in_specs=[own, own, pl.BlockSpec((N_DEV - 1, tr, cols), lambda i, w: (0, i, 0))],
            out_specs=pl.BlockSpec((tr, cols), lambda i, w: (w[0] * nh + i, 0))),
        out_shape=SDS((2 * h, cols), F32),
        compiler_params=_params(("parallel",)),
    )(where, lo, hi, r)


def _add_devices(where, g, r, name):
    _, rows, cols = g.shape
    h = rows // 2
    tr = min(h, 256)
    nh = h // tr

    def body(where_ref, g_ref, r_ref, o_ref):
        del where_ref
        acc = g_ref[0]
        for k in range(N_DEV - 1):
            acc = acc + r_ref[k].astype(F32)
        o_ref[...] = acc

    return pl.pallas_call(
        body, name=name,
        grid_spec=pltpu.PrefetchScalarGridSpec(
            num_scalar_prefetch=1, grid=(nh,),
            in_specs=[pl.BlockSpec((1, tr, cols), lambda i, w: (w[1], w[0] * nh + i, 0)),
                      pl.BlockSpec((N_DEV - 1, tr, cols), lambda i, w: (0, i, 0))],
            out_specs=pl.BlockSpec((tr, cols), lambda i, w: (w[0] * nh + i, 0))),
        out_shape=SDS((rows, cols), F32),
        compiler_params=_params(("parallel",)),
    )(where, g, r)


def _sum_smalls(gathered):
    n = len(gathered)

    def body(*refs):
        for all_ref, o_ref in zip(refs[:n], refs[n:]):
            acc = all_ref[0]
            for dev in range(1, N_DEV):
                acc = acc + all_ref[dev]
            o_ref[...] = acc

    vm = pl.BlockSpec(memory_space=pltpu.VMEM)
    return pl.pallas_call(
        body, name="sum_smalls", out_shape=[SDS(a.shape[1:], F32) for a in gathered],
        in_specs=[vm] * n, out_specs=[vm] * n,
    )(*gathered)


def _adam_step(g, w, m, v):
    nm = ADAM_B1 * m + (1.0 - ADAM_B1) * g
    nv = ADAM_B2 * v + (1.0 - ADAM_B2) * (g * g)
    m_hat = nm / (1.0 - ADAM_B1 ** ADAM_STEP)
    v_hat = nv / (1.0 - ADAM_B2 ** ADAM_STEP)
    return -ADAM_LR * (m_hat / (jnp.sqrt(v_hat) + ADAM_EPS) + ADAM_WD * w), nm, nv


def _adamw(g, w, m, v, name):
    rows, cols = g.shape
    tr = min(rows, 256)

    def body(g_ref, w_ref, m_ref, v_ref, d_ref, nm_ref, nv_ref):
        d_ref[...], nm_ref[...], nv_ref[...] = _adam_step(g_ref[...], w_ref[...], m_ref[...], v_ref[...])

    spec = pl.BlockSpec((tr, cols), lambda i: (i, 0))
    return pl.pallas_call(
        body, name=name, grid=(rows // tr,), in_specs=[spec] * 4, out_specs=[spec] * 3,
        out_shape=[SDS(g.shape, F32)] * 3, compiler_params=_params(("parallel",)),
    )(g, w, m, v)


def _small_update(chip, tot, tot_rel, wmv):
    names = list(SMALL_PLACES)
    n = len(names)

    def body(chip_ref, tot_ref, quarter_ref, rel_ref, *refs):
        del chip_ref
        ins, outs = refs[:3 * n], refs[3 * n:]
        for i, nm in enumerate(names):
            source, row, shape = SMALL_PLACES[nm]
            from_ref = {"rows": tot_ref, "quarter": quarter_ref, "rel": rel_ref}[source]
            for at in ([Ellipsis] if len(shape) == 2 else range(shape[0])):
                g = from_ref[row:row + shape[0], 0:shape[-1]] if at is Ellipsis else from_ref[row + at:row + at + 1, 0:shape[-1]]
                outs[4 * i][at] = g
                outs[4 * i + 1][at], outs[4 * i + 2][at], outs[4 * i + 3][at] = _adam_step(
                    g, ins[3 * i][at], ins[3 * i + 1][at], ins[3 * i + 2][at])

    whole = lambda shape: pl.BlockSpec(shape, lambda i, c: (0,) * len(shape))
    shapes = [SMALL_PLACES[nm][2] for nm in names]
    outs = pl.pallas_call(
        body, name="small_update",
        grid_spec=pltpu.PrefetchScalarGridSpec(
            num_scalar_prefetch=1, grid=(1,),
            in_specs=[whole(tot.shape), pl.BlockSpec((tot.shape[0], D // 4), lambda i, c: (0, c[0])),
                      whole(tot_rel.shape)] + [whole(shp) for shp in shapes for _ in range(3)],
            out_specs=[whole(shp) for shp in shapes for _ in range(4)]),
        out_shape=[SDS(shp, F32) for shp in shapes for _ in range(4)],
    )(chip, tot, tot, tot_rel, *[a for nm in names for a in wmv[nm]])
    return {nm: tuple(outs[4 * i:4 * i + 4]) for i, nm in enumerate(names)}


def _pad_rows(a, rows):
    return jnp.concatenate([a, jnp.zeros((rows - a.shape[0], a.shape[1]), a.dtype)], axis=0)


def _pad_cols(a, cols):
    return jnp.concatenate([a, jnp.zeros((a.shape[0], cols - a.shape[1]), a.dtype)], axis=1)


def kernel(x, a_pre_norm, a_w_in, a_conv_w, a_w_out, a_post_norm, kv_norm, w_kv, rel_bias, b_pre_norm, b_w_in, b_sinks, b_w_out, b_post_norm, loss_target, m_a_pre_norm, m_a_w_in, m_a_conv_w, m_a_w_out, m_a_post_norm, m_kv_norm, m_w_kv, m_rel_bias, m_b_pre_norm, m_b_w_in, m_b_sinks, m_b_w_out, m_b_post_norm, v_a_pre_norm, v_a_w_in, v_a_conv_w, v_a_w_out, v_a_post_norm, v_kv_norm, v_w_kv, v_rel_bias, v_b_pre_norm, v_b_w_in, v_b_sinks, v_b_w_out, v_b_post_norm):
    seq = x.shape[1]
    xs = x.reshape(seq, D)
    tgt = loss_target.reshape(seq, D)
    chip = 2 * lax.axis_index("x") + lax.axis_index("y")
    core = lax.axis_index("c")
    tm = _tile(seq, 512)
    tmw = _tile(seq, 1024)

    shards = [a_w_in[0], a_w_out[0], w_kv, b_w_in[0], b_w_out[0]]
    small_w = _pad_rows(jnp.concatenate([a_pre_norm, a_conv_w[0], a_post_norm], axis=0), 8)
    *own_only, small_g = _prepare_weights(shards, small_w)
    where = jnp.stack([core, chip]).astype(jnp.int32)
    small_full = small_g.transpose(1, 0, 2).reshape(8, D)
    g_apre, conv_w, g_apost = small_full[0:1], _pad_rows(small_full[1:4], 8), small_full[4:5]
    g_kv = kv_norm.reshape(1, D)

    proj, n1, (win_g, wouta_g, wkv_g, wbin_g, woutb_g) = _a_in(where[1:2], xs, g_apre, own_only, tmw)
    wouta = wouta_g.reshape(D, D)
    wkv = wkv_g.reshape(D, 2 * KV_W)
    woutb = woutb_g.reshape(D, D)
    ya, oa, h1, conv = _a_mix(proj, xs, conv_w, wouta, g_apost, tm)
    kv, q, zb = _b_in(h1, g_kv, b_pre_norm, wkv, wbin_g, tmw)
    tab = _bias_table(rel_bias.T, b_sinks.reshape(N_HEADS))
    att, stats = _attn_fwd(q, kv, tab)
    dh2, dqz, datt, loss_acc, dg_bpost, dw_outb, dw_outb16 = _mid(att, zb, h1, tgt, woutb, b_post_norm, tm)

    dqz, dkv, dtab = _attn_bwd(q, kv, datt, stats, tab, dqz)
    dh1, doa, dg_b, dw_bin, dw_kv, dw_bin16, dw_kv16 = _b_bwd(dqz, dkv, h1, dh2, oa, wbin_g, wkv, g_kv, b_pre_norm,
                                                              g_apost, tm)
    by_chip = lambda a, cols: a.reshape(N_CHIPS, D // 4, cols)
    grads1 = [by_chip(dw_kv, 2 * KV_W), dw_bin, by_chip(dw_outb, D)]
    sent1 = [by_chip(dw_kv16, 2 * KV_W), dw_bin16, by_chip(dw_outb16, D)]
    names1 = ["w_kv", "b_w_in", "b_w_out"]
    dproj, dconv_w, dw_outa, dw_outa16, from_devices1 = _a_bwd(doa, ya, conv, proj, conv_w, wouta, tm, sent1)
    shards1 = [_add_devices(where, g, r, "add_devices_" + nm) for g, r, nm in zip(grads1, from_devices1, names1)]
    tmw2 = _tile(seq, 4096)
    win_lo, win_lo16, outa_got, g_wkv, g_wbin, g_woutb = _dw_in_half(
        n1, dproj, 0, tmw2, "dw_a_in_lo", to_devices=by_chip(dw_outa16, D), shards=shards1)
    win_hi, win_hi16, win_got = _dw_in_half(n1, dproj, 1, tmw2, "dw_a_in_hi", to_owners=win_lo16)
    nt = seq // tmw
    dn_first, win_got = _a_in_bwd_matmul(dproj, win_g, tmw, max(nt - max(nt // 4, 1), 1), win_hi16, win_got)
    grad_x, dg_apre = _a_in_bwd(dn_first, dproj, xs, dh1, win_g, g_apre, tm)
    shards2 = [_add_win(where, win_lo, win_hi, win_got, "add_devices_a_w_in"),
               _add_devices(where, by_chip(dw_outa, D), outa_got, "add_devices_a_w_out")]
    drel, dsink = _bias_fold(dtab)

    smalls = jnp.concatenate([
        dg_apre[0:1], dg_b[2:3], dg_b[0:1], dg_b[1:2], dg_bpost[0:1], _pad_cols(dsink[0:1], D),
        _pad_cols(loss_acc[0:1], D), jnp.zeros((1, D), F32), dconv_w], axis=0)
    assert smalls.shape == (SMALL_ROWS, D)
    (g_win, g_wouta), gathered = _share_and_gather(shards2, (smalls, drel))
    tot, tot_rel = _sum_smalls(gathered)

    big = {}
    for nm, g, w, m, v in [("a_w_in", g_win, a_w_in, m_a_w_in, v_a_w_in), ("a_w_out", g_wouta, a_w_out, m_a_w_out, v_a_w_out),
                           ("w_kv", g_wkv, w_kv, m_w_kv, v_w_kv), ("b_w_in", g_wbin, b_w_in, m_b_w_in, v_b_w_in),
                           ("b_w_out", g_woutb, b_w_out, m_b_w_out, v_b_w_out)]:
        shp = w.shape
        two = (shp[-2], shp[-1])
        d, nm_, nv_ = _adamw(g, w.reshape(two), m.reshape(two), v.reshape(two), "adamw_" + nm)
        big[nm] = (g.reshape(shp), d.reshape(shp), nm_.reshape(shp), nv_.reshape(shp))

    given = {"a_pre_norm": (a_pre_norm, m_a_pre_norm, v_a_pre_norm), "a_conv_w": (a_conv_w, m_a_conv_w, v_a_conv_w),
             "a_post_norm": (a_post_norm, m_a_post_norm, v_a_post_norm), "kv_norm": (kv_norm, m_kv_norm, v_kv_norm),
             "rel_bias": (rel_bias, m_rel_bias, v_rel_bias), "b_pre_norm": (b_pre_norm, m_b_pre_norm, v_b_pre_norm),
             "b_sinks": (b_sinks, m_b_sinks, v_b_sinks), "b_post_norm": (b_post_norm, m_b_post_norm, v_b_post_norm)}
    to_kernel = lambda nm, a: a.T if nm == "rel_bias" else a.reshape(SMALL_PLACES[nm][2])
    from_kernel = lambda nm, a: a.T if nm == "rel_bias" else a.reshape(given[nm][0].shape)
    small = _small_update(where[1:2], tot, tot_rel, {nm: tuple(to_kernel(nm, a) for a in wmv)
                                            for nm, wmv in given.items()})
    order = ["a_pre_norm", "a_w_in", "a_conv_w", "a_w_out", "a_post_norm", "kv_norm", "w_kv", "rel_bias",
             "b_pre_norm", "b_w_in", "b_sinks", "b_w_out", "b_post_norm"]
    outs = []
    for which in range(4):
        for nm in order:
            outs.append(big[nm][which] if nm in big else from_kernel(nm, small[nm][which]))
    loss = 0.5 * tot[LOSS_ROW, 0]
    return (loss, grad_x.reshape(x.shape), *outs)
```

```python
import math

import jax
import jax.numpy as jnp
from jax import lax
from jax.experimental import pallas as pl
from jax.experimental.pallas import tpu as pltpu

F32 = jnp.float32
BF16 = jnp.bfloat16
MESH = pl.DeviceIdType.MESH
SDS = jax.ShapeDtypeStruct

D = 1024
HEAD_DIM = 64
N_HEADS = 16
N_KV = 2
GROUP = 8
KV_W = 128
BLK = 128
N_BUCKETS = 32
MAX_EXACT = 16
MAX_DISTANCE = 128
EPS = 1e-6
NEG_INF = -1e30
Q_SCALE = HEAD_DIM ** -0.5

ADAM_LR = 0.001
ADAM_B1 = 0.9
ADAM_B2 = 0.999
ADAM_EPS = 1e-08
ADAM_WD = 0.01
ADAM_STEP = 10

N_CHIPS = 4
N_DEV = 8
BIN_COLS = 2 * D // N_CHIPS
VMEM_LIMIT = 56 * 1024 * 1024
SMALL_ROWS = 16
LOSS_ROW = 6
SMALL_PLACES = {
    "a_pre_norm": ("quarter", 0, (1, D // 4)), "a_conv_w": ("quarter", 8, (3, 1, D // 4)),
    "a_post_norm": ("quarter", 1, (1, D // 4)), "kv_norm": ("rows", 2, (1, D)),
    "rel_bias": ("rel", 0, (N_HEADS, N_BUCKETS)), "b_pre_norm": ("rows", 3, (1, D)),
    "b_sinks": ("rows", 5, (1, N_HEADS)), "b_post_norm": ("rows", 4, (1, D)),
}


def _bucket_thresholds():
    def bucket(d):
        big = MAX_EXACT + int(math.log(d / MAX_EXACT) / math.log(MAX_DISTANCE / MAX_EXACT)
                              * (N_BUCKETS - MAX_EXACT))
        return d if d < MAX_EXACT else min(big, N_BUCKETS - 1)
    out = []
    for b in range(MAX_EXACT + 1, N_BUCKETS):
        out.append(min(d for d in range(MAX_EXACT, MAX_DISTANCE) if bucket(d) >= b))
    return tuple(out)


BUCKET_THRESHOLDS = _bucket_thresholds()


def _params(semantics=None, vmem=VMEM_LIMIT):
    return pltpu.CompilerParams(dimension_semantics=semantics, vmem_limit_bytes=vmem)


def _tile(n, pref):
    return pref if n >= 2 * pref else max(n // 2, 8)


def _rms_scale(v):
    return lax.rsqrt(jnp.mean(v * v, axis=-1, keepdims=True) + EPS)


def _nt(a, b):
    return lax.dot_general(a, b, (((1,), (1,)), ((), ())), preferred_element_type=F32)


def _tn(a, b):
    return lax.dot_general(a, b, (((0,), (0,)), ((), ())), preferred_element_type=F32)


def _nn(a, b):
    return jnp.dot(a, b, preferred_element_type=F32)


def _silu_parts(z):
    sg = jax.nn.sigmoid(z)
    return sg, z * sg


def _dsilu(z, sg):
    return sg * (1.0 + z * (1.0 - sg))


def _write_gradient(acc, out32, out16, stage, sem):
    whole = pltpu.make_async_copy(acc, out32, sem)
    whole.start()
    rows = stage.shape[0]
    for k in range(acc.shape[0] // rows):
        stage[...] = acc[rows * k:rows * (k + 1), :].astype(BF16)
        pltpu.sync_copy(stage, out16.at[pl.ds(rows * k, rows)])
    whole.wait()


def _acc_row(ref, row, val):
    ref[row:row + 1, :] += val


def _gather_copies(outs, splits, ici_send, ici_recv, d2d_send, d2d_recv):
    x, y, c = lax.axis_index("x"), lax.axis_index("y"), lax.axis_index("c")
    k = 2 * x + y
    sibling = (x, y, 1 - c)

    def part(o_ref, chip, core, split):
        if not split:
            return o_ref.at[chip]
        h = o_ref.shape[1] // 2
        return o_ref.at[chip, pl.ds(pl.multiple_of(core * h, 16), h)]

    def remote(ref, a, j, sems, to):
        return pltpu.make_async_remote_copy(src_ref=ref, dst_ref=ref, send_sem=sems[0].at[3 * a + j],
                                            recv_sem=sems[1].at[3 * a + j], device_id=to, device_id_type=MESH)

    copies = []
    for a, (o_ref, split) in enumerate(zip(outs, splits)):
        for j, (px, py) in enumerate([(x, 1 - y), (1 - x, y), (1 - x, 1 - y)]):
            kj = 2 * px + py
            ici, d2d = (ici_send, ici_recv), (d2d_send, d2d_recv)
            copies.append((remote(part(o_ref, k, c, split), a, j, ici, (px, py, c)),
                           remote(part(o_ref, kj, c, split), a, j, ici, (px, py, c)),
                           remote(part(o_ref, kj, c, split), a, j, d2d, sibling) if split else None,
                           remote(part(o_ref, kj, 1 - c, split), a, j, d2d, sibling) if split else None))
    return copies


def _gather_sems(n):
    return [pltpu.SemaphoreType.DMA((3 * n,)) for _ in range(4)]


def _prepare_weights(shards, small):
    n = len(shards)

    def body(*refs):
        ins, small_in = refs[:n], refs[n]
        outs, small_out = refs[n + 1:2 * n + 1], refs[2 * n + 1]
        stages, put_sem = refs[2 * n + 2:3 * n + 2], refs[3 * n + 2]
        sems = refs[3 * n + 3:]
        k = 2 * lax.axis_index("x") + lax.axis_index("y")
        puts = []
        for a, (i_ref, stage, o_ref) in enumerate(zip(ins, stages, outs)):
            stage[...] = i_ref[...].astype(BF16)
            puts.append(pltpu.make_async_copy(stage, o_ref.at[k], put_sem.at[a]))
            puts[-1].start()
        small_out[k] = small_in[...]
        copies = _gather_copies([small_out], [False], *sems)
        for send, _, _, _ in copies:
            send.start()
        for _, arrival, _, _ in copies:
            arrival.wait_recv()
        for send, _, _, _ in copies:
            send.wait_send()
        for put in puts:
            put.wait()

    vm = pl.BlockSpec(memory_space=pltpu.VMEM)
    anyspace = pl.BlockSpec(memory_space=pl.ANY)
    out_shape = [SDS((N_CHIPS,) + s.shape, BF16) for s in shards] + [SDS((N_CHIPS,) + small.shape, F32)]
    return pl.pallas_call(
        body, name="prepare_weights", out_shape=out_shape,
        in_specs=[vm] * (n + 1), out_specs=[anyspace] * n + [vm],
        scratch_shapes=[pltpu.VMEM(s.shape, BF16) for s in shards] + [pltpu.SemaphoreType.DMA((n,))] + _gather_sems(1),
        compiler_params=pltpu.CompilerParams(vmem_limit_bytes=VMEM_LIMIT),
    )(*shards, small)


def _a_in(chip, x, g_pre, weights, tm):
    s = x.shape[0]
    nt = s // tm
    n = len(weights)

    def body(chip_ref, x_ref, g_ref, *refs):
        proj_ref, n1_ref = refs[n:n + 2]
        gathered = refs[n + 2:2 * n + 2]
        wbuf, n1_all, fetch_sem = refs[2 * n + 2:2 * n + 5]
        sems = refs[2 * n + 5:]
        jj, i = pl.program_id(0), pl.program_id(1)
        copies = _gather_copies(gathered, [True] * n, *sems)

        def fetch(rel):
            slot = jnp.bitwise_xor(chip_ref[0], rel)
            return pltpu.make_async_copy(gathered[0].at[slot], wbuf.at[rel % 2], fetch_sem.at[rel % 2])

        @pl.when((jj == 0) & (i == 0))
        def _():
            fetch(0).start()
            copies[0][0].start()
            copies[1][0].start()
            fetch(0).wait()

        for rel in (1, 2, 3):
            @pl.when((jj == rel) & (i == 0))
            def _():
                fetch(rel).wait()

        @pl.when(jj == 0)
        def _():
            xv = x_ref[...]
            n1 = (xv * _rms_scale(xv) * g_ref[...]).astype(BF16)
            n1_ref[...] = n1
            n1_all[i] = n1
        proj_ref[...] = _nn(n1_all[i], wbuf[jj % 2]).astype(BF16)

        for rel in (1, 2, 3):
            @pl.when((jj == rel - 1) & (i == max(nt - 3, 0)))
            def _():
                _, arrival, forward, _ = copies[rel - 1]
                arrival.wait_recv()
                forward.start()
                if rel == 1:
                    for send, _, _, _ in copies[2:]:
                        send.start()

            @pl.when((jj == rel - 1) & (i == max(nt - 2, 0)))
            def _():
                copies[rel - 1][3].wait_recv()
                fetch(rel).start()

        @pl.when((jj == 3) & (i == max(nt - 2, 0)))
        def _():
            for _, arrival, forward, _ in copies[3:]:
                arrival.wait_recv()
                forward.start()

        @pl.when((jj == 3) & (i == nt - 1))
        def _():
            for _, _, _, forwarded in copies[3:]:
                forwarded.wait_recv()
            for send, _, forward, _ in copies:
                forward.wait_send()
                send.wait_send()

    anyspace = pl.BlockSpec(memory_space=pl.ANY)
    proj, n1, *gathered = pl.pallas_call(
        body, name="a_in",
        grid_spec=pltpu.PrefetchScalarGridSpec(
            num_scalar_prefetch=1, grid=(4, nt),
            in_specs=[pl.BlockSpec((tm, D), lambda jj, i, c: (jnp.where(jj == 0, i, nt - 1), 0)),
                      pl.BlockSpec((1, D), lambda jj, i, c: (0, 0))] + [anyspace] * n,
            out_specs=[pl.BlockSpec((tm, D), lambda jj, i, c: (i, jnp.bitwise_xor(c[0], jj))),
                       pl.BlockSpec((tm, D), lambda jj, i, c: (jnp.where(jj == 0, i, nt - 1), 0))] + [anyspace] * n,
            scratch_shapes=[pltpu.VMEM((2, D, D), BF16), pltpu.VMEM((nt, tm, D), BF16),
                            pltpu.SemaphoreType.DMA((2,))] + _gather_sems(n)),
        out_shape=[SDS((s, 4 * D), BF16), SDS((s, D), BF16)] + [SDS(w.shape, w.dtype) for w in weights],
        input_output_aliases={3 + a: 2 + a for a in range(n)},
        compiler_params=_params(("arbitrary", "arbitrary")),
    )(chip, x, g_pre, *weights)
    return proj, n1, gathered


def _shift_rows(v, last, second_last, rows):
    v1 = jnp.where(rows >= 1, pltpu.roll(v, 1, 0), last)
    v2 = jnp.where(rows >= 2, pltpu.roll(v, 2, 0), jnp.where(rows == 1, last, second_last))
    return v1, v2


def _a_mix(proj, x, conv_w, w_out, g_post, tm):
    s = x.shape[0]

    def body(proj_ref, x_ref, cw_ref, w_ref, g_ref, ya_ref, oa_ref, h1_ref, conv_ref, carry):
        @pl.when(pl.program_id(0) == 0)
        def _():
            carry[...] = jnp.zeros_like(carry)
        v = proj_ref[:, D:2 * D].astype(F32) * proj_ref[:, 2 * D:3 * D].astype(F32)
        rows = lax.broadcasted_iota(jnp.int32, (tm, D), 0)
        before = carry[...]
        v1, v2 = _shift_rows(v, before[7:8, :], before[6:7, :], rows)
        carry[...] = v[tm - 8:tm, :]
        conv = cw_ref[0:1, :] * v2 + cw_ref[1:2, :] * v1 + cw_ref[2:3, :] * v
        conv_ref[...] = conv.astype(BF16)
        _, sz = _silu_parts(proj_ref[:, 3 * D:4 * D].astype(F32))
        ya = (proj_ref[:, 0:D].astype(F32) * conv * sz).astype(BF16)
        ya_ref[...] = ya
        oa = _nn(ya, w_ref[...])
        oa_ref[...] = oa.astype(BF16)
        h1_ref[...] = x_ref[...] + oa * _rms_scale(oa) * g_ref[...]

    row = lambda i: (i, 0)
    fix = lambda i: (0, 0)
    return pl.pallas_call(
        body, name="a_mix", grid=(s // tm,),
        in_specs=[pl.BlockSpec((tm, 4 * D), row), pl.BlockSpec((tm, D), row), pl.BlockSpec((8, D), fix),
                  pl.BlockSpec((D, D), fix), pl.BlockSpec((1, D), fix)],
        out_specs=[pl.BlockSpec((tm, D), row)] * 4,
        out_shape=[SDS((s, D), BF16), SDS((s, D), BF16), SDS((s, D), F32), SDS((s, D), BF16)],
        scratch_shapes=[pltpu.VMEM((8, D), F32)],
        compiler_params=_params(("arbitrary",)),
    )(proj, x, conv_w, w_out, g_post)


def _b_in(h1, g_kv, g_pre, w_kv, wbin_g, tm):
    s = h1.shape[0]

    def body(h_ref, gk_ref, gb_ref, wkv_ref, wb_ref, kv_ref, q_ref, z_ref):
        h = h_ref[...]
        hh = h * _rms_scale(h)
        nk = (hh * gk_ref[...]).astype(BF16)
        nb = (hh * gb_ref[...]).astype(BF16)
        kv_ref[...] = _nn(nk, wkv_ref[...]).astype(BF16)
        for j in range(2):
            q_ref[:, BIN_COLS * j:BIN_COLS * (j + 1)] = (_nn(nb, wb_ref[j]) * Q_SCALE).astype(BF16)
            z_ref[:, BIN_COLS * j:BIN_COLS * (j + 1)] = _nn(nb, wb_ref[2 + j]).astype(BF16)

    row = lambda i: (i, 0)
    fix = lambda i: (0, 0)
    return pl.pallas_call(
        body, name="b_in", grid=(s // tm,),
        in_specs=[pl.BlockSpec((tm, D), row), pl.BlockSpec((1, D), fix), pl.BlockSpec((1, D), fix),
                  pl.BlockSpec((D, 2 * KV_W), fix), pl.BlockSpec((N_CHIPS, D, BIN_COLS), lambda i: (0, 0, 0))],
        out_specs=[pl.BlockSpec((tm, 2 * KV_W), row), pl.BlockSpec((tm, D), row), pl.BlockSpec((tm, D), row)],
        out_shape=[SDS((s, 2 * KV_W), BF16), SDS((s, D), BF16), SDS((s, D), BF16)],
        compiler_params=_params(("parallel",)),
    )(h1, g_kv, g_pre, w_kv, wbin_g)


def _buckets(dist):
    bucket = jnp.where(dist < MAX_EXACT, dist, MAX_EXACT)
    for t in BUCKET_THRESHOLDS:
        bucket = bucket + jnp.where(dist >= t, 1, 0)
    return bucket


def _head_place(h):
    kh, j, e = h // GROUP, (h % GROUP) // 2, h % 2
    return kh, slice(BLK * j, BLK * (j + 1)), slice(2 * BLK * e, 2 * BLK * (e + 1))


def _bias_table(rel_bias, sinks):
    def body(rb_ref, sink_ref, tab_ref):
        along = lax.broadcasted_iota(jnp.int32, (8, BLK), 1)
        row8 = lax.broadcasted_iota(jnp.int32, (8, BLK), 0)
        bucket = _buckets(jnp.where(along == 0, 0, BLK - along))
        query = lax.broadcasted_iota(jnp.int32, (BLK, BLK), 0)
        col = lax.broadcasted_iota(jnp.int32, (BLK, BLK), 1)
        for h in range(N_HEADS):
            by_dist = jnp.zeros((8, BLK), F32)
            for b in range(N_BUCKETS):
                by_dist = jnp.where(bucket == b, rb_ref[h, b], by_dist)
            for digit in range(3):
                by_dist = jnp.where((row8 >> digit) & 1 == 1, pltpu.roll(by_dist, 1 << digit, 1), by_dist)
            band = jnp.concatenate([by_dist] + [pltpu.roll(by_dist, 8 * g, 1) for g in range(1, BLK // 8)], axis=0)
            cur = jnp.where(col <= query, band, NEG_INF)
            kh, rows, cols = _head_place(h)
            prev_cols, cur_cols = slice(cols.start, cols.start + BLK), slice(cols.start + BLK, cols.stop)
            tab_ref[1, kh, rows, prev_cols] = jnp.where(col == 0, sink_ref[h], jnp.where(col > query, band, NEG_INF))
            tab_ref[1, kh, rows, cur_cols] = cur
            tab_ref[0, kh, rows, prev_cols] = jnp.where(col == 0, sink_ref[h], NEG_INF)
            tab_ref[0, kh, rows, cur_cols] = cur

    return pl.pallas_call(
        body, name="bias_table", out_shape=SDS((2, N_KV, 4 * BLK, 4 * BLK), F32),
        in_specs=[pl.BlockSpec(memory_space=pltpu.SMEM), pl.BlockSpec(memory_space=pltpu.SMEM)],
        out_specs=pl.BlockSpec(memory_space=pltpu.VMEM),
    )(rel_bias, sinks)


def _bias_fold(dtab, sink_row, pieces):
    def body(dtab_ref, *refs):
        piece_refs, (out_ref, smalls_ref) = refs[:len(pieces)], refs[len(pieces):]
        lane = lax.broadcasted_iota(jnp.int32, (N_HEADS, BLK), 1)
        bucket = _buckets(lane)
        col = lax.broadcasted_iota(jnp.int32, (BLK, BLK), 1)
        row8 = lax.broadcasted_iota(jnp.int32, (8, 128), 0)
        lane8 = lax.broadcasted_iota(jnp.int32, (8, 128), 1)
        by_dist = jnp.zeros((BLK, 128), F32)
        dsink = jnp.zeros((8, 128), F32)
        for h in range(N_HEADS):
            kh, rows, cols = _head_place(h)
            dt = dtab_ref[kh, rows, cols]
            band = jnp.where(col == 0, 0.0, dt[:, 0:BLK]) + dt[:, BLK:2 * BLK]
            for digit in range(BLK.bit_length() - 1):
                band = jnp.where((col >> digit) & 1 == 1, pltpu.roll(band, BLK - (1 << digit), 0), band)
            by_dist = jnp.where(col == h, jnp.sum(band, axis=1, keepdims=True), by_dist)
            dsink = dsink + jnp.where((row8 == 0) & (lane8 == h), jnp.sum(dt[:, 0:1]), 0.0)
        by_head = by_dist.T[0:N_HEADS]
        folded = jnp.zeros((N_HEADS, 128), F32)
        for b in range(N_BUCKETS):
            folded = jnp.where(lane == b, jnp.sum(jnp.where(bucket == b, by_head, 0.0), axis=1, keepdims=True), folded)
        out_ref[...] = folded
        smalls_ref[...] = jnp.zeros((SMALL_ROWS, D), F32)
        smalls_ref[sink_row:sink_row + 1, 0:128] = dsink[0:1]
        for (to_row, a, first, rows), ref in zip(pieces, piece_refs):
            smalls_ref[to_row:to_row + rows, 0:a.shape[1]] = ref[first:first + rows, :]

    vm = pl.BlockSpec(memory_space=pltpu.VMEM)
    return pl.pallas_call(
        body, name="bias_fold", out_shape=[SDS((N_HEADS, 128), F32), SDS((SMALL_ROWS, D), F32)],
        in_specs=[vm] * (1 + len(pieces)), out_specs=[vm, vm],
    )(dtab, *[a for _, a, _, _ in pieces])


def _pair_operands(prev, cur):
    t = jnp.concatenate([prev, cur], axis=0).astype(F32)
    t = jnp.where(lax.broadcasted_iota(jnp.int32, t.shape, 0) == 0, 0.0, t)
    tr = pltpu.roll(t, HEAD_DIM, 1)
    lo = lax.broadcasted_iota(jnp.int32, t.shape, 1) < HEAD_DIM
    zero = jnp.zeros_like(t)
    head0 = jnp.concatenate([jnp.where(lo, t, zero), jnp.where(lo, zero, tr)], axis=0).astype(BF16)
    head1 = jnp.concatenate([jnp.where(lo, tr, zero), jnp.where(lo, zero, t)], axis=0).astype(BF16)
    return head0, head1


def _pair_fold(d0, d1):
    lo = lax.broadcasted_iota(jnp.int32, (2 * BLK, KV_W), 1) < HEAD_DIM
    zero = jnp.zeros((2 * BLK, KV_W), F32)
    g0 = jnp.where(lo, d0[0:256], zero) + pltpu.roll(jnp.where(lo, zero, d0[256:512]), HEAD_DIM, 1)
    g1 = pltpu.roll(jnp.where(lo, d1[0:256], zero), HEAD_DIM, 1) + jnp.where(lo, zero, d1[256:512])
    return jnp.where(lax.broadcasted_iota(jnp.int32, (2 * BLK, KV_W), 0) == 0, 0.0, g0 + g1)


def _stack_pairs(ref, kh):
    return jnp.concatenate([ref[:, 128 * (4 * kh + j):128 * (4 * kh + j + 1)] for j in range(4)], axis=0)


def _table_spec():
    return pl.BlockSpec((1, N_KV, 4 * BLK, 4 * BLK), lambda n: (jnp.minimum(n, 1), 0, 0, 0))


def _attn_fwd(q, kv, tab):
    s = q.shape[0]

    def body(q_ref, kp_ref, kc_ref, vp_ref, vc_ref, tab_ref, att_ref, stats_ref):
        k2 = _pair_operands(kp_ref[...], kc_ref[...])
        v2 = _pair_operands(vp_ref[...], vc_ref[...])
        lane = lax.broadcasted_iota(jnp.int32, (BLK, 128), 1)
        stats = jnp.zeros((BLK, 128), F32)
        for kh in range(N_KV):
            sc = _nt(_stack_pairs(q_ref, kh), k2[kh])
            ps = []
            for e in range(2):
                lg = sc[:, 256 * e:256 * (e + 1)] + tab_ref[0, kh, :, 256 * e:256 * (e + 1)]
                m = jnp.max(lg, axis=-1, keepdims=True)
                ex = jnp.exp(lg - m)
                den = jnp.sum(ex, axis=-1, keepdims=True)
                ps.append(ex * (1.0 / den))
                lse = m + jnp.log(den)
                for j in range(4):
                    stats = jnp.where(lane == GROUP * kh + 2 * j + e, lse[BLK * j:BLK * (j + 1)], stats)
            out = _nn(jnp.concatenate(ps, axis=1).astype(BF16), v2[kh])
            for j in range(4):
                att_ref[:, 128 * (4 * kh + j):128 * (4 * kh + j + 1)] = out[BLK * j:BLK * (j + 1)].astype(BF16)
        stats_ref[...] = stats

    cur = lambda n: (n, 0)
    prev = lambda n: (jnp.maximum(n - 1, 0), 0)
    return pl.pallas_call(
        body, name="attn_fwd", grid=(s // BLK,),
        in_specs=[pl.BlockSpec((BLK, D), cur),
                  pl.BlockSpec((BLK, KV_W), prev), pl.BlockSpec((BLK, KV_W), cur),
                  pl.BlockSpec((BLK, KV_W), lambda n: (jnp.maximum(n - 1, 0), 1)),
                  pl.BlockSpec((BLK, KV_W), lambda n: (n, 1)), _table_spec()],
        out_specs=[pl.BlockSpec((BLK, D), cur), pl.BlockSpec((BLK, 128), cur)],
        out_shape=[SDS((s, D), BF16), SDS((s, 128), F32)],
        compiler_params=_params(("parallel",)),
    )(q, kv, kv, kv, kv, tab)


def _mid(att, zb, h1, tgt, w_out, g_post, tm):
    s = att.shape[0]
    nt = s // tm

    def body(att_ref, z_ref, h1_ref, t_ref, w_ref, g_ref,
             dh_ref, dqz_ref, datt_ref, loss_ref, dg_ref, dw_ref, dw16_ref, dw_acc, stage, put_sem):
        @pl.when(pl.program_id(0) == 0)
        def _():
            loss_ref[...] = jnp.zeros_like(loss_ref)
            dg_ref[...] = jnp.zeros_like(dg_ref)
            dw_acc[...] = jnp.zeros_like(dw_acc)
        att = att_ref[...].astype(F32)
        z = z_ref[...].astype(F32)
        sg, sz = _silu_parts(z)
        ob = (att * sz).astype(BF16)
        y2 = _nn(ob, w_ref[...])
        r2 = _rms_scale(y2)
        yh = y2 * r2
        g = g_ref[...]
        err = (h1_ref[...] + yh * g) - t_ref[...]
        loss_ref[...] += jnp.sum(jnp.sum(err * err, axis=-1, keepdims=True) / D)
        dh = err / D
        dh_ref[...] = dh
        _acc_row(dg_ref, 0, jnp.sum(dh * yh, axis=0, keepdims=True))
        dyh = dh * g
        dy = (r2 * (dyh - yh * jnp.mean(dyh * yh, axis=-1, keepdims=True))).astype(BF16)
        dw_acc[...] += _tn(ob, dy)
        dob = _nt(dy, w_ref[...])
        datt_ref[...] = (dob * sz).astype(BF16)
        dqz_ref[...] = (dob * att * _dsilu(z, sg)).astype(BF16)

        @pl.when(pl.program_id(0) == nt - 1)
        def _():
            _write_gradient(dw_acc, dw_ref, dw16_ref, stage, put_sem)

    row = lambda i: (i, 0)
    fix = lambda i: (0, 0)
    anyspace = pl.BlockSpec(memory_space=pl.ANY)
    return pl.pallas_call(
        body, name="mid", grid=(nt,),
        in_specs=[pl.BlockSpec((tm, D), row)] * 4 + [pl.BlockSpec((D, D), fix), pl.BlockSpec((1, D), fix)],
        out_specs=[pl.BlockSpec((tm, D), row), pl.BlockSpec((tm, D), lambda i: (i, 1)), pl.BlockSpec((tm, D), row),
                   pl.BlockSpec((8, 128), fix), pl.BlockSpec((8, D), fix), anyspace, anyspace],
        out_shape=[SDS((s, D), F32), SDS((s, 2 * D), BF16), SDS((s, D), BF16), SDS((8, 128), F32),
                   SDS((8, D), F32), SDS((D, D), F32), SDS((D, D), BF16)],
        scratch_shapes=[pltpu.VMEM((D, D), F32), pltpu.VMEM((D // 4, D), BF16), pltpu.SemaphoreType.DMA],
        compiler_params=_params(("arbitrary",)),
    )(att, zb, h1, tgt, w_out, g_post)


def _attn_bwd(q, kv, datt, stats, tab, dqz):
    s = q.shape[0]
    nb = s // BLK

    def body(q_ref, kp_ref, kc_ref, vp_ref, vc_ref, da_ref, st_ref, tab_ref, dqz_in,
             dq_ref, dkv_ref, dtab_ref, dk_carry, dv_carry):
        del dqz_in
        n = pl.program_id(0)

        @pl.when(n == 0)
        def _():
            dtab_ref[...] = jnp.zeros_like(dtab_ref)
            dk_carry[...] = jnp.zeros_like(dk_carry)
            dv_carry[...] = jnp.zeros_like(dv_carry)

        @pl.when(n < nb)
        def _():
            k2 = _pair_operands(kp_ref[...], kc_ref[...])
            v2 = _pair_operands(vp_ref[...], vc_ref[...])
            lane = lax.broadcasted_iota(jnp.int32, (BLK, 128), 1)
            stats = st_ref[...]
            dk2, dv2 = [], []
            for kh in range(N_KV):
                qs = _stack_pairs(q_ref, kh)
                das = _stack_pairs(da_ref, kh)
                sc = _nt(qs, k2[kh])
                dp = _nt(das, v2[kh])
                ps, dss = [], []
                for e in range(2):
                    heads = [GROUP * kh + 2 * j + e for j in range(4)]
                    lse = jnp.concatenate([jnp.sum(jnp.where(lane == h, stats, 0.0), axis=-1, keepdims=True)
                                           for h in heads], axis=0)
                    cols = slice(256 * e, 256 * (e + 1))
                    p = jnp.exp(sc[:, cols] + tab_ref[0, kh, :, cols] - lse)
                    delta = jnp.sum(p * dp[:, cols], axis=-1, keepdims=True)
                    ds = p * (dp[:, cols] - delta)
                    dtab_ref[kh, :, cols] += ds
                    ps.append(p)
                    dss.append(ds)
                p2 = jnp.concatenate(ps, axis=1).astype(BF16)
                ds2 = jnp.concatenate(dss, axis=1).astype(BF16)
                dq = _nn(ds2, k2[kh]) * Q_SCALE
                for j in range(4):
                    dq_ref[:, 128 * (4 * kh + j):128 * (4 * kh + j + 1)] = dq[BLK * j:BLK * (j + 1)].astype(BF16)
                dk2.append(_tn(ds2, qs))
                dv2.append(_tn(p2, das))
            dkk = _pair_fold(dk2[0], dk2[1])
            dvv = _pair_fold(dv2[0], dv2[1])
            dkv_ref[:, 0:KV_W] = (dk_carry[...] + dkk[0:BLK]).astype(BF16)
            dkv_ref[:, KV_W:2 * KV_W] = (dv_carry[...] + dvv[0:BLK]).astype(BF16)
            dk_carry[...] = dkk[BLK:2 * BLK]
            dv_carry[...] = dvv[BLK:2 * BLK]

        @pl.when(n == nb)
        def _():
            dkv_ref[:, 0:KV_W] = dk_carry[...].astype(BF16)
            dkv_ref[:, KV_W:2 * KV_W] = dv_carry[...].astype(BF16)

    cur = lambda n: (jnp.minimum(n, nb - 1), 0)
    prev = lambda n: (jnp.clip(n - 1, 0, nb - 1), 0)
    return pl.pallas_call(
        body, name="attn_bwd", grid=(nb + 1,),
        in_specs=[pl.BlockSpec((BLK, D), cur),
                  pl.BlockSpec((BLK, KV_W), prev), pl.BlockSpec((BLK, KV_W), cur),
                  pl.BlockSpec((BLK, KV_W), lambda n: (jnp.clip(n - 1, 0, nb - 1), 1)),
                  pl.BlockSpec((BLK, KV_W), lambda n: (jnp.minimum(n, nb - 1), 1)),
                  pl.BlockSpec((BLK, D), cur), pl.BlockSpec((BLK, 128), cur), _table_spec(),
                  pl.BlockSpec(memory_space=pl.ANY)],
        out_specs=[pl.BlockSpec((BLK, D), cur), pl.BlockSpec((BLK, 2 * KV_W), prev),
                   pl.BlockSpec((N_KV, 4 * BLK, 4 * BLK), lambda n: (0, 0, 0))],
        out_shape=[SDS((s, 2 * D), BF16), SDS((s, 2 * KV_W), BF16), SDS((N_KV, 4 * BLK, 4 * BLK), F32)],
        scratch_shapes=[pltpu.VMEM((BLK, KV_W), F32), pltpu.VMEM((BLK, KV_W), F32)],
        input_output_aliases={8: 0},
        compiler_params=_params(("arbitrary",)),
    )(q, kv, kv, kv, kv, datt, stats, tab, dqz)


def _b_bwd(dqz, dkv, h1, dh2, oa, wbin_g, w_kv, g_kv, g_pre, g_apost, tm):
    s = h1.shape[0]
    nt = s // tm

    def body(dqz_ref, dkv_ref, h_ref, dh2_ref, oa_ref, wb_ref, wkv_ref, gk_ref, gb_ref, ga_ref,
             dh1_ref, doa_ref, dg_ref, dwb_ref, dwkv_ref, dwb16_ref, dwkv16_ref, wcat, dwb_acc, dwkv_acc, put_sem):
        @pl.when(pl.program_id(0) == 0)
        def _():
            dg_ref[...] = jnp.zeros_like(dg_ref)
            dwb_acc[...] = jnp.zeros_like(dwb_acc)
            dwkv_acc[...] = jnp.zeros_like(dwkv_acc)
            for j in range(N_CHIPS):
                pltpu.sync_copy(wb_ref.at[j], wcat.at[:, pl.ds(BIN_COLS * j, BIN_COLS)])
        dnb = _nt(dqz_ref[...], wcat[...])
        dnk = _nt(dkv_ref[...], wkv_ref[...])
        h = h_ref[...]
        r = _rms_scale(h)
        hh = h * r
        dwb_acc[...] += _tn((hh * gb_ref[...]).astype(BF16), dqz_ref[...])
        dwkv_acc[...] += _tn((hh * gk_ref[...]).astype(BF16), dkv_ref[...])
        _acc_row(dg_ref, 0, jnp.sum(dnk * hh, axis=0, keepdims=True))
        _acc_row(dg_ref, 1, jnp.sum(dnb * hh, axis=0, keepdims=True))
        dhh = dnb * gb_ref[...] + dnk * gk_ref[...]
        dh1 = dh2_ref[...] + r * (dhh - hh * jnp.mean(dhh * hh, axis=-1, keepdims=True))
        dh1_ref[...] = dh1
        oa = oa_ref[...].astype(F32)
        ra = _rms_scale(oa)
        oh = oa * ra
        _acc_row(dg_ref, 2, jnp.sum(dh1 * oh, axis=0, keepdims=True))
        doh = dh1 * ga_ref[...]
        doa_ref[...] = (ra * (doh - oh * jnp.mean(doh * oh, axis=-1, keepdims=True))).astype(BF16)

        @pl.when(pl.program_id(0) == nt - 1)
        def _():
            wcat[...] = dwb_acc[...].astype(BF16)
            puts = [pltpu.make_async_copy(dwkv_acc, dwkv_ref, put_sem.at[2 * N_CHIPS])]
            for j in range(N_CHIPS):
                cols = pl.ds(BIN_COLS * j, BIN_COLS)
                puts.append(pltpu.make_async_copy(dwb_acc.at[:, cols], dwb_ref.at[j], put_sem.at[2 * j]))
                puts.append(pltpu.make_async_copy(wcat.at[:, cols], dwb16_ref.at[j], put_sem.at[2 * j + 1]))
            for put in puts:
                put.start()
            for put in puts:
                put.wait()
            wcat[:, 0:2 * KV_W] = dwkv_acc[...].astype(BF16)
            pltpu.sync_copy(wcat.at[:, pl.ds(0, 2 * KV_W)], dwkv16_ref)

    row = lambda i: (i, 0)
    fix = lambda i: (0, 0)
    anyspace = pl.BlockSpec(memory_space=pl.ANY)
    return pl.pallas_call(
        body, name="b_bwd", grid=(nt,),
        in_specs=[pl.BlockSpec((tm, 2 * D), row), pl.BlockSpec((tm, 2 * KV_W), row), pl.BlockSpec((tm, D), row),
                  pl.BlockSpec((tm, D), row), pl.BlockSpec((tm, D), row), anyspace, pl.BlockSpec((D, 2 * KV_W), fix),
                  pl.BlockSpec((1, D), fix), pl.BlockSpec((1, D), fix), pl.BlockSpec((1, D), fix)],
        out_specs=[pl.BlockSpec((tm, D), row), pl.BlockSpec((tm, D), row), pl.BlockSpec((8, D), fix)] + [anyspace] * 4,
        out_shape=[SDS((s, D), F32), SDS((s, D), BF16), SDS((8, D), F32), SDS((N_CHIPS, D, BIN_COLS), F32),
                   SDS((D, 2 * KV_W), F32), SDS((N_CHIPS, D, BIN_COLS), BF16), SDS((D, 2 * KV_W), BF16)],
        scratch_shapes=[pltpu.VMEM((D, 2 * D), BF16), pltpu.VMEM((D, 2 * D), F32), pltpu.VMEM((D, 2 * KV_W), F32),
                        pltpu.SemaphoreType.DMA((2 * N_CHIPS + 1,))],
        compiler_params=_params(("arbitrary",)),
    )(dqz, dkv, h1, dh2, oa, wbin_g, w_kv, g_kv, g_pre, g_apost)


def _to_owner_core(pieces, r, send, recv, core, action):
    x, y, c = lax.axis_index("x"), lax.axis_index("y"), lax.axis_index("c")
    for kp in range(N_CHIPS):
        px, py = kp >> 1, kp & 1
        rel = 4 * (x + px - 2 * x * px) + 2 * (y + py - 2 * y * py) + (c + core - 2 * c * core)

        @pl.when(rel != 0)
        def _():
            cp = pltpu.make_async_remote_copy(src_ref=pieces.at[kp], dst_ref=r.at[rel - 1], send_sem=send.at[kp],
                                              recv_sem=recv.at[rel - 1], device_id=(px, py, core), device_id_type=MESH)
            if action == "start":
                cp.start()
            else:
                cp.wait_send()
    if action == "wait":
        @pl.when(c == core)
        def _():
            for rel in range(1, N_DEV):
                pltpu.make_async_remote_copy(src_ref=pieces.at[0], dst_ref=r.at[rel - 1], send_sem=send.at[0],
                                             recv_sem=recv.at[rel - 1], device_id=(x, y, c),
                                             device_id_type=MESH).wait_recv()


def _owner_core_sems():
    return [pltpu.SemaphoreType.DMA((N_CHIPS,)), pltpu.SemaphoreType.DMA((N_DEV - 1,))]


def _device_exchange(grads, recvs, send, recv):
    x, y, c = lax.axis_index("x"), lax.axis_index("y"), lax.axis_index("c")
    copies = []
    for a, (g, r) in enumerate(zip(grads, recvs)):
        h = g.shape[1] // 2
        for rel in range(1, N_DEV):
            fx, fy, fc = rel >> 2, (rel >> 1) & 1, rel & 1
            px, py, pc = x + fx - 2 * x * fx, y + fy - 2 * y * fy, c + fc - 2 * c * fc
            sem = (N_DEV - 1) * a + rel - 1
            copies.append(pltpu.make_async_remote_copy(
                src_ref=g.at[2 * px + py, pl.ds(pl.multiple_of(pc * h, 16), h)], dst_ref=r.at[rel - 1],
                send_sem=send.at[sem], recv_sem=recv.at[sem], device_id=(px, py, pc), device_id_type=MESH))
    return copies


def _device_exchange_specs(grads):
    anyspace = pl.BlockSpec(memory_space=pl.ANY)
    n = len(grads)
    count = (N_DEV - 1) * n
    return ([anyspace] * n, [anyspace] * n,
            [SDS((N_DEV - 1, g.shape[1] // 2, g.shape[2]), g.dtype) for g in grads],
            [pltpu.SemaphoreType.DMA((count,)), pltpu.SemaphoreType.DMA((count,))])


def _a_bwd(doa, ya, conv, proj, conv_w, w_out, tm, parts):
    s = doa.shape[0]
    nt = s // tm
    n = len(parts)
    ex_in, ex_out, ex_shape, ex_sems = _device_exchange_specs(parts)

    def body(*refs):
        doa_ref, ya_ref, conv_ref, proj_ref, cw_ref, w_ref = refs[:6]
        part_refs = refs[6:6 + n]
        dproj_ref, dcw_ref, dw_ref, dw16_ref = refs[6 + n:10 + n]
        recv_refs = refs[10 + n:10 + 2 * n]
        carry, dw_acc, stage, put_sem, send, recv = refs[10 + 2 * n:]
        i = pl.program_id(0)

        @pl.when(i == 0)
        def _():
            dcw_ref[...] = jnp.zeros_like(dcw_ref)
            carry[...] = jnp.zeros_like(carry)
            dw_acc[...] = jnp.zeros_like(dw_acc)
            for cp in _device_exchange(part_refs, recv_refs, send, recv):
                cp.start()
        dya = _nt(doa_ref[...], w_ref[...])
        dw_acc[...] += _tn(ya_ref[...], doa_ref[...])
        bg = proj_ref[:, 0:D].astype(F32)
        cg = proj_ref[:, D:2 * D].astype(F32)
        u = proj_ref[:, 2 * D:3 * D].astype(F32)
        z = proj_ref[:, 3 * D:4 * D].astype(F32)
        v = cg * u
        rows = lax.broadcasted_iota(jnp.int32, (tm, D), 0)
        conv = conv_ref[...].astype(F32)
        sg, sz = _silu_parts(z)
        dproj_ref[:, 0:D] = (dya * conv * sz).astype(BF16)
        dproj_ref[:, 3 * D:4 * D] = (dya * bg * conv * _dsilu(z, sg)).astype(BF16)
        dconv = dya * bg * sz
        after = carry[...]
        up1 = jnp.where(rows < tm - 1, pltpu.roll(dconv, tm - 1, 0), after[0:1, :])
        up2 = jnp.where(rows < tm - 2, pltpu.roll(dconv, tm - 2, 0),
                        jnp.where(rows == tm - 2, after[0:1, :], after[1:2, :]))
        carry[...] = dconv[0:8, :]
        _acc_row(dcw_ref, 0, jnp.sum(up2 * v, axis=0, keepdims=True))
        _acc_row(dcw_ref, 1, jnp.sum(up1 * v, axis=0, keepdims=True))
        _acc_row(dcw_ref, 2, jnp.sum(dconv * v, axis=0, keepdims=True))
        dv = cw_ref[2:3, :] * dconv + cw_ref[1:2, :] * up1 + cw_ref[0:1, :] * up2
        dproj_ref[:, D:2 * D] = (dv * u).astype(BF16)
        dproj_ref[:, 2 * D:3 * D] = (dv * cg).astype(BF16)

        @pl.when(i == nt - 1)
        def _():
            _write_gradient(dw_acc, dw_ref, dw16_ref, stage, put_sem)
            for cp in _device_exchange(part_refs, recv_refs, send, recv):
                cp.wait()

    rev = lambda i: (nt - 1 - i, 0)
    fix = lambda i: (0, 0)
    anyspace = pl.BlockSpec(memory_space=pl.ANY)
    dproj, dcw, dw, dw16, *got = pl.pallas_call(
        body, name="a_bwd", grid=(nt,),
        in_specs=[pl.BlockSpec((tm, D), rev), pl.BlockSpec((tm, D), rev), pl.BlockSpec((tm, D), rev),
                  pl.BlockSpec((tm, 4 * D), rev), pl.BlockSpec((8, D), fix), pl.BlockSpec((D, D), fix)] + ex_in,
        out_specs=[pl.BlockSpec((tm, 4 * D), rev), pl.BlockSpec((8, D), fix), anyspace, anyspace] + ex_out,
        out_shape=[SDS((s, 4 * D), BF16), SDS((8, D), F32), SDS((D, D), F32), SDS((D, D), BF16)] + ex_shape,
        scratch_shapes=[pltpu.VMEM((8, D), F32), pltpu.VMEM((D, D), F32), pltpu.VMEM((D // 4, D), BF16),
                        pltpu.SemaphoreType.DMA] + ex_sems,
        compiler_params=_params(("arbitrary",)),
    )(doa, ya, conv, proj, conv_w, w_out, *parts)
    return dproj, dcw, dw, dw16, got


def _dn1(dp_ref, w_ref):
    dn = _nt(dp_ref[:, 0:D], w_ref[0])
    for j in range(1, 4):
        dn = dn + _nt(dp_ref[:, D * j:D * (j + 1)], w_ref[j])
    return dn


def _a_in_bwd_matmul(dproj, win_g, tm, count, win_half, win_got):
    def body(dp_ref, w_ref, half_ref, got_in, dn_ref, got_ref, wcat, send, recv):
        del got_in

        @pl.when(pl.program_id(0) == 0)
        def _():
            _to_owner_core(half_ref, got_ref, send, recv, 1, "start")
            for j in range(N_CHIPS):
                pltpu.sync_copy(w_ref.at[j], wcat.at[:, pl.ds(D * j, D)])
        dn_ref[...] = _nt(dp_ref[...], wcat[...]).astype(BF16)

        @pl.when(pl.program_id(0) == count - 1)
        def _():
            _to_owner_core(half_ref, got_ref, send, recv, 1, "wait")

    row = lambda i: (i, 0)
    anyspace = pl.BlockSpec(memory_space=pl.ANY)
    return pl.pallas_call(
        body, name="a_in_bwd_matmul", grid=(count,),
        in_specs=[pl.BlockSpec((tm, 4 * D), row), anyspace, anyspace, anyspace],
        out_specs=[pl.BlockSpec((tm, D), row), anyspace],
        out_shape=[SDS((count * tm, D), BF16), SDS(win_got.shape, win_got.dtype)],
        scratch_shapes=[pltpu.VMEM((D, 4 * D), BF16)] + _owner_core_sems(),
        input_output_aliases={3: 1},
        compiler_params=_params(("arbitrary",)),
    )(dproj, win_g, win_half, win_got)


def _a_in_bwd(dn_first, dproj, x, dh1, win_g, g_pre, tm):
    s = x.shape[0]
    nt = s // tm
    count = dn_first.shape[0] // tm

    def body(dn_ref, dp_ref, x_ref, dh_ref, w_ref, g_ref, gx_ref, dg_ref, dn_s):
        i = pl.program_id(0)

        @pl.when(i == 0)
        def _():
            dg_ref[...] = jnp.zeros_like(dg_ref)

        @pl.when(i < count)
        def _():
            dn_s[...] = dn_ref[...].astype(F32)

        @pl.when(i >= count)
        def _():
            dn_s[...] = _dn1(dp_ref, w_ref)
        dn = dn_s[...]
        xv = x_ref[...]
        r = _rms_scale(xv)
        xh = xv * r
        _acc_row(dg_ref, 0, jnp.sum(dn * xh, axis=0, keepdims=True))
        dxh = dn * g_ref[...]
        gx_ref[...] = dh_ref[...] + r * (dxh - xh * jnp.mean(dxh * xh, axis=-1, keepdims=True))

    row = lambda i: (i, 0)
    fix = lambda i: (0, 0)
    return pl.pallas_call(
        body, name="a_in_bwd", grid=(nt,),
        in_specs=[pl.BlockSpec((tm, D), lambda i: (jnp.minimum(i, count - 1), 0)),
                  pl.BlockSpec((tm, 4 * D), lambda i: (jnp.maximum(i, count), 0)),
                  pl.BlockSpec((tm, D), row), pl.BlockSpec((tm, D), row),
                  pl.BlockSpec((4, D, D), lambda i: (0, 0, 0)), pl.BlockSpec((1, D), fix)],
        out_specs=[pl.BlockSpec((tm, D), row), pl.BlockSpec((8, D), fix)],
        out_shape=[SDS((s, D), F32), SDS((8, D), F32)],
        scratch_shapes=[pltpu.VMEM((tm, D), F32)],
        compiler_params=_params(("arbitrary",)),
    )(dn_first, dproj, x, dh1, win_g, g_pre)


def _swap_halves(shards, send, recv):
    x, y, c = lax.axis_index("x"), lax.axis_index("y"), lax.axis_index("c")
    sibling = (x, y, 1 - c)
    copies = []
    for b, full in enumerate(shards):
        h = full.shape[0] // 2
        mine = full.at[pl.ds(pl.multiple_of(c * h, 8), h)]
        theirs = full.at[pl.ds(pl.multiple_of((1 - c) * h, 8), h)]
        copies.append((pltpu.make_async_remote_copy(src_ref=mine, dst_ref=mine, send_sem=send.at[b], recv_sem=recv.at[b],
                                                    device_id=sibling, device_id_type=MESH),
                       pltpu.make_async_remote_copy(src_ref=mine, dst_ref=theirs, send_sem=send.at[b], recv_sem=recv.at[b],
                                                    device_id=sibling, device_id_type=MESH)))
    return copies


def _dw_in_half(n1, dproj, core, tmw, name, to_owners=None, to_devices=None, shards=()):
    s = n1.shape[0]
    h = D // 2
    nt = s // tmw
    n_sh = len(shards)
    if to_owners is not None:
        sent_array, sems, got_shape = to_owners, _owner_core_sems(), SDS((N_DEV - 1, h, D), BF16)
    else:
        sent_array = to_devices
        _, _, (got_shape,), sems = _device_exchange_specs([to_devices])

    def body(*refs):
        a_ref, b_ref, sent = refs[:3]
        o_ref, o16_ref, got = refs[3 + n_sh:6 + n_sh]
        shard_refs = refs[6 + n_sh:6 + 2 * n_sh]
        send, recv = refs[6 + 2 * n_sh:8 + 2 * n_sh]
        swap_sems = refs[8 + 2 * n_sh:]
        j, t = pl.program_id(0), pl.program_id(1)

        def exchange(action):
            if to_owners is not None:
                _to_owner_core(sent, got, send, recv, 1 - core, action)
            else:
                for cp in _device_exchange([sent], [got], send, recv):
                    cp.start() if action == "start" else cp.wait()

        @pl.when((j == 0) & (t == 0))
        def _():
            exchange("start")
            if n_sh:
                for mine, _ in _swap_halves(shard_refs, *swap_sems):
                    mine.start()

        @pl.when(t == 0)
        def _():
            o_ref[...] = jnp.zeros_like(o_ref)
        o_ref[0] += _tn(a_ref[...], b_ref[...])

        @pl.when(t == nt - 1)
        def _():
            o16_ref[...] = o_ref[...].astype(BF16)

        @pl.when((j == N_CHIPS - 1) & (t == nt - 1))
        def _():
            exchange("wait")
            if n_sh:
                for mine, theirs in _swap_halves(shard_refs, *swap_sems):
                    theirs.wait_recv()
                    mine.wait_send()

    anyspace = pl.BlockSpec(memory_space=pl.ANY)
    slot = pl.BlockSpec((1, h, D), lambda j, t: (j, 0, 0))
    swap_scratch = [pltpu.SemaphoreType.DMA((n_sh,)), pltpu.SemaphoreType.DMA((n_sh,))] if n_sh else []
    return pl.pallas_call(
        body, name=name, grid=(N_CHIPS, nt),
        in_specs=[pl.BlockSpec((tmw, h), lambda j, t: (t, core)), pl.BlockSpec((tmw, D), lambda j, t: (t, j))]
        + [anyspace] * (1 + n_sh),
        out_specs=[slot, slot] + [anyspace] * (1 + n_sh),
        out_shape=[SDS((N_CHIPS, h, D), F32), SDS((N_CHIPS, h, D), BF16), got_shape]
        + [SDS(sh.shape, F32) for sh in shards],
        scratch_shapes=sems + swap_scratch,
        input_output_aliases={3 + b: 3 + b for b in range(n_sh)},
        compiler_params=_params(("arbitrary", "arbitrary")),
    )(n1, dproj, sent_array, *shards)


def _share_and_gather(shards, smalls):
    n_h, n_s = len(shards), len(smalls)

    def body(*refs):
        small_ins = refs[n_h:n_h + n_s]
        fs = refs[n_h + n_s:2 * n_h + n_s]
        small_alls = refs[2 * n_h + n_s:2 * n_h + 2 * n_s]
        dsend, drecv, ssend, srecv = refs[2 * n_h + 2 * n_s:]
        x, y, c = lax.axis_index("x"), lax.axis_index("y"), lax.axis_index("c")
        swaps = _swap_halves(fs, dsend, drecv)
        sends, arrivals = [mine for mine, _ in swaps], [theirs for _, theirs in swaps]
        me = 4 * x + 2 * y + c
        for k, (small_in, small_all) in enumerate(zip(small_ins, small_alls)):
            small_all[me] = small_in[...]
            for rel in range(1, N_DEV):
                fx, fy, fc = rel >> 2, (rel >> 1) & 1, rel & 1
                peer = (x + fx - 2 * x * fx, y + fy - 2 * y * fy, c + fc - 2 * c * fc)
                sender = 4 * peer[0] + 2 * peer[1] + peer[2]
                sem = (N_DEV - 1) * k + rel - 1
                sends.append(pltpu.make_async_remote_copy(
                    src_ref=small_in, dst_ref=small_all.at[me], send_sem=ssend.at[sem], recv_sem=srecv.at[sem],
                    device_id=peer, device_id_type=MESH))
                arrivals.append(pltpu.make_async_remote_copy(
                    src_ref=small_in, dst_ref=small_all.at[sender], send_sem=ssend.at[sem], recv_sem=srecv.at[sem],
                    device_id=peer, device_id_type=MESH))
        for cp in sends:
            cp.start()
        for cp in arrivals:
            cp.wait_recv()
        for cp in sends:
            cp.wait_send()

    anyspace = pl.BlockSpec(memory_space=pl.ANY)
    vm = pl.BlockSpec(memory_space=pltpu.VMEM)
    out_shape = [SDS(full.shape, F32) for full in shards] + [SDS((N_DEV,) + sm.shape, F32) for sm in smalls]
    n_all = (N_DEV - 1) * n_s
    outs = pl.pallas_call(
        body, name="share_and_gather", out_shape=out_shape,
        in_specs=[anyspace] * n_h + [vm] * n_s, out_specs=[anyspace] * n_h + [vm] * n_s,
        scratch_shapes=[pltpu.SemaphoreType.DMA((n_h,)), pltpu.SemaphoreType.DMA((n_h,)),
                        pltpu.SemaphoreType.DMA((n_all,)), pltpu.SemaphoreType.DMA((n_all,))],
        input_output_aliases={b: b for b in range(n_h)},
    )(*shards, *smalls)
    return outs[:n_h], outs[n_h:]


def _add_win(where, lo, hi, r, name):
    _, h, cols = lo.shape
    tr = min(h, 256)
    nh = h // tr

    def body(where_ref, lo_ref, hi_ref, r_ref, o_ref):
        acc = jnp.where(where_ref[0] == 0, lo_ref[0], hi_ref[0])
        for k in range(N_DEV - 1):
            acc = acc + r_ref[k].astype(F32)
        o_ref[...] = acc

    own = pl.BlockSpec((1, tr, cols), lambda i, w: (w[1], i, 0))
    return pl.pallas_call(
        body, name=name,
        grid_spec=pltpu.PrefetchScalarGridSpec(
            num_scalar_prefetch=1, grid=(nh,),
            in_specs=[own, own, pl.BlockSpec((N_DEV - 1, tr, cols), lambda i, w: (0, i, 0))],
            out_specs=pl.BlockSpec((tr, cols), lambda i, w: (w[0] * nh + i, 0))),
        out_shape=SDS((2 * h, cols), F32),
        compiler_params=_params(("parallel",)),
    )(where, lo, hi, r)


def _add_devices(where, g, r, name):
    _, rows, cols = g.shape
    h = rows // 2
    tr = min(h, 256)
    nh = h // tr

    def body(where_ref, g_ref, r_ref, o_ref):
        del where_ref
        acc = g_ref[0]
        for k in range(N_DEV - 1):
            acc = acc + r_ref[k].astype(F32)
        o_ref[...] = acc

    return pl.pallas_call(
        body, name=name,
        grid_spec=pltpu.PrefetchScalarGridSpec(
            num_scalar_prefetch=1, grid=(nh,),
            in_specs=[pl.BlockSpec((1, tr, cols), lambda i, w: (w[1], w[0] * nh + i, 0)),
                      pl.BlockSpec((N_DEV - 1, tr, cols), lambda i, w: (0, i, 0))],
            out_specs=pl.BlockSpec((tr, cols), lambda i, w: (w[0] * nh + i, 0))),
        out_shape=SDS((rows, cols), F32),
        compiler_params=_params(("parallel",)),
    )(where, g, r)


def _sum_smalls(gathered):
    n = len(gathered)

    def body(*refs):
        for all_ref, o_ref in zip(refs[:n], refs[n:]):
            acc = all_ref[0]
            for dev in range(1, N_DEV):
                acc = acc + all_ref[dev]
            o_ref[...] = acc

    vm = pl.BlockSpec(memory_space=pltpu.VMEM)
    return pl.pallas_call(
        body, name="sum_smalls", out_shape=[SDS(a.shape[1:], F32) for a in gathered],
        in_specs=[vm] * n, out_specs=[vm] * n,
    )(*gathered)


def _adam_step(g, w, m, v):
    nm = ADAM_B1 * m + (1.0 - ADAM_B1) * g
    nv = ADAM_B2 * v + (1.0 - ADAM_B2) * (g * g)
    m_hat = nm / (1.0 - ADAM_B1 ** ADAM_STEP)
    v_hat = nv / (1.0 - ADAM_B2 ** ADAM_STEP)
    return -ADAM_LR * (m_hat / (jnp.sqrt(v_hat) + ADAM_EPS) + ADAM_WD * w), nm, nv


def _adamw(g, w, m, v, name):
    rows, cols = g.shape
    tr = min(rows, 256)

    def body(g_ref, w_ref, m_ref, v_ref, d_ref, nm_ref, nv_ref):
        d_ref[...], nm_ref[...], nv_ref[...] = _adam_step(g_ref[...], w_ref[...], m_ref[...], v_ref[...])

    spec = pl.BlockSpec((tr, cols), lambda i: (i, 0))
    return pl.pallas_call(
        body, name=name, grid=(rows // tr,), in_specs=[spec] * 4, out_specs=[spec] * 3,
        out_shape=[SDS(g.shape, F32)] * 3, compiler_params=_params(("parallel",)),
    )(g, w, m, v)


def _small_update(chip, tot, tot_rel, wmv):
    names = list(SMALL_PLACES)
    n = len(names)

    def body(chip_ref, tot_ref, quarter_ref, rel_ref, *refs):
        del chip_ref
        ins, outs = refs[:3 * n], refs[3 * n:]
        for i, nm in enumerate(names):
            source, row, shape = SMALL_PLACES[nm]
            from_ref = {"rows": tot_ref, "quarter": quarter_ref, "rel": rel_ref}[source]
            for at in ([Ellipsis] if len(shape) == 2 else range(shape[0])):
                g = from_ref[row:row + shape[0], 0:shape[-1]] if at is Ellipsis else from_ref[row + at:row + at + 1, 0:shape[-1]]
                outs[4 * i][at] = g
                outs[4 * i + 1][at], outs[4 * i + 2][at], outs[4 * i + 3][at] = _adam_step(
                    g, ins[3 * i][at], ins[3 * i + 1][at], ins[3 * i + 2][at])

    whole = lambda shape: pl.BlockSpec(shape, lambda i, c: (0,) * len(shape))
    shapes = [SMALL_PLACES[nm][2] for nm in names]
    outs = pl.pallas_call(
        body, name="small_update",
        grid_spec=pltpu.PrefetchScalarGridSpec(
            num_scalar_prefetch=1, grid=(1,),
            in_specs=[whole(tot.shape), pl.BlockSpec((tot.shape[0], D // 4), lambda i, c: (0, c[0])),
                      whole(tot_rel.shape)] + [whole(shp) for shp in shapes for _ in range(3)],
            out_specs=[whole(shp) for shp in shapes for _ in range(4)]),
        out_shape=[SDS(shp, F32) for shp in shapes for _ in range(4)],
    )(chip, tot, tot, tot_rel, *[a for nm in names for a in wmv[nm]])
    return {nm: tuple(outs[4 * i:4 * i + 4]) for i, nm in enumerate(names)}


def _pad_rows(a, rows):
    return jnp.concatenate([a, jnp.zeros((rows - a.shape[0], a.shape[1]), a.dtype)], axis=0)


def kernel(x, a_pre_norm, a_w_in, a_conv_w, a_w_out, a_post_norm, kv_norm, w_kv, rel_bias, b_pre_norm, b_w_in, b_sinks, b_w_out, b_post_norm, loss_target, m_a_pre_norm, m_a_w_in, m_a_conv_w, m_a_w_out, m_a_post_norm, m_kv_norm, m_w_kv, m_rel_bias, m_b_pre_norm, m_b_w_in, m_b_sinks, m_b_w_out, m_b_post_norm, v_a_pre_norm, v_a_w_in, v_a_conv_w, v_a_w_out, v_a_post_norm, v_kv_norm, v_w_kv, v_rel_bias, v_b_pre_norm, v_b_w_in, v_b_sinks, v_b_w_out, v_b_post_norm):
    seq = x.shape[1]
    xs = x.reshape(seq, D)
    tgt = loss_target.reshape(seq, D)
    chip = 2 * lax.axis_index("x") + lax.axis_index("y")
    core = lax.axis_index("c")
    tm = _tile(seq, 512)
    tmw = _tile(seq, 1024)

    shards = [a_w_in[0], a_w_out[0], w_kv, b_w_in[0], b_w_out[0]]
    small_w = _pad_rows(jnp.concatenate([a_pre_norm, a_conv_w[0], a_post_norm], axis=0), 8)
    *own_only, small_g = _prepare_weights(shards, small_w)
    where = jnp.stack([core, chip]).astype(jnp.int32)
    small_full = small_g.transpose(1, 0, 2).reshape(8, D)
    g_apre, conv_w, g_apost = small_full[0:1], _pad_rows(small_full[1:4], 8), small_full[4:5]
    g_kv = kv_norm.reshape(1, D)

    proj, n1, (win_g, wouta_g, wkv_g, wbin_g, woutb_g) = _a_in(where[1:2], xs, g_apre, own_only, tmw)
    wouta = wouta_g.reshape(D, D)
    wkv = wkv_g.reshape(D, 2 * KV_W)
    woutb = woutb_g.reshape(D, D)
    ya, oa, h1, conv = _a_mix(proj, xs, conv_w, wouta, g_apost, tm)
    kv, q, zb = _b_in(h1, g_kv, b_pre_norm, wkv, wbin_g, tmw)
    tab = _bias_table(rel_bias.T, b_sinks.reshape(N_HEADS))
    att, stats = _attn_fwd(q, kv, tab)
    dh2, dqz, datt, loss_acc, dg_bpost, dw_outb, dw_outb16 = _mid(att, zb, h1, tgt, woutb, b_post_norm, tm)

    dqz, dkv, dtab = _attn_bwd(q, kv, datt, stats, tab, dqz)
    dh1, doa, dg_b, dw_bin, dw_kv, dw_bin16, dw_kv16 = _b_bwd(dqz, dkv, h1, dh2, oa, wbin_g, wkv, g_kv, b_pre_norm,
                                                              g_apost, tm)
    by_chip = lambda a, cols: a.reshape(N_CHIPS, D // 4, cols)
    grads1 = [by_chip(dw_kv, 2 * KV_W), dw_bin, by_chip(dw_outb, D)]
    sent1 = [by_chip(dw_kv16, 2 * KV_W), dw_bin16, by_chip(dw_outb16, D)]
    names1 = ["w_kv", "b_w_in", "b_w_out"]
    dproj, dconv_w, dw_outa, dw_outa16, from_devices1 = _a_bwd(doa, ya, conv, proj, conv_w, wouta, tm, sent1)
    shards1 = [_add_devices(where, g, r, "add_devices_" + nm) for g, r, nm in zip(grads1, from_devices1, names1)]
    tmw2 = _tile(seq, 4096)
    win_lo, win_lo16, outa_got, g_wkv, g_wbin, g_woutb = _dw_in_half(
        n1, dproj, 0, tmw2, "dw_a_in_lo", to_devices=by_chip(dw_outa16, D), shards=shards1)
    win_hi, win_hi16, win_got = _dw_in_half(n1, dproj, 1, tmw2, "dw_a_in_hi", to_owners=win_lo16)
    nt = seq // tmw
    dn_first, win_got = _a_in_bwd_matmul(dproj, win_g, tmw, max(nt - max(nt // 4, 1), 1), win_hi16, win_got)
    grad_x, dg_apre = _a_in_bwd(dn_first, dproj, xs, dh1, win_g, g_apre, tm)
    shards2 = [_add_win(where, win_lo, win_hi, win_got, "add_devices_a_w_in"),
               _add_devices(where, by_chip(dw_outa, D), outa_got, "add_devices_a_w_out")]

    assert dconv_w.shape == (8, D)
    drel, smalls = _bias_fold(dtab, SMALL_PLACES["b_sinks"][1], [
        (0, dg_apre, 0, 1), (1, dg_b, 2, 1), (2, dg_b, 0, 1), (3, dg_b, 1, 1), (4, dg_bpost, 0, 1),
        (LOSS_ROW, loss_acc, 0, 1), (8, dconv_w, 0, 8)])
    (g_win, g_wouta), gathered = _share_and_gather(shards2, (smalls, drel))
    tot, tot_rel = _sum_smalls(gathered)

    big = {}
    for nm, g, w, m, v in [("a_w_in", g_win, a_w_in, m_a_w_in, v_a_w_in), ("a_w_out", g_wouta, a_w_out, m_a_w_out, v_a_w_out),
                           ("w_kv", g_wkv, w_kv, m_w_kv, v_w_kv), ("b_w_in", g_wbin, b_w_in, m_b_w_in, v_b_w_in),
                           ("b_w_out", g_woutb, b_w_out, m_b_w_out, v_b_w_out)]:
        shp = w.shape
        two = (shp[-2], shp[-1])
        d, nm_, nv_ = _adamw(g, w.reshape(two), m.reshape(two), v.reshape(two), "adamw_" + nm)
        big[nm] = (g.reshape(shp), d.reshape(shp), nm_.reshape(shp), nv_.reshape(shp))

    given = {"a_pre_norm": (a_pre_norm, m_a_pre_norm, v_a_pre_norm), "a_conv_w": (a_conv_w, m_a_conv_w, v_a_conv_w),
             "a_post_norm": (a_post_norm, m_a_post_norm, v_a_post_norm), "kv_norm": (kv_norm, m_kv_norm, v_kv_norm),
             "rel_bias": (rel_bias, m_rel_bias, v_rel_bias), "b_pre_norm": (b_pre_norm, m_b_pre_norm, v_b_pre_norm),
             "b_sinks": (b_sinks, m_b_sinks, v_b_sinks), "b_post_norm": (b_post_norm, m_b_post_norm, v_b_post_norm)}
    to_kernel = lambda nm, a: a.T if nm == "rel_bias" else a.reshape(SMALL_PLACES[nm][2])
    from_kernel = lambda nm, a: a.T if nm == "rel_bias" else a.reshape(given[nm][0].shape)
    small = _small_update(where[1:2], tot, tot_rel, {nm: tuple(to_kernel(nm, a) for a in wmv)
                                            for nm, wmv in given.items()})
    order = ["a_pre_norm", "a_w_in", "a_conv_w", "a_w_out", "a_post_norm", "kv_norm", "w_kv", "rel_bias",
             "b_pre_norm", "b_w_in", "b_sinks", "b_w_out", "b_post_norm"]
    outs = []
    for which in range(4):
        for nm in order:
            outs.append(big[nm][which] if nm in big else from_kernel(nm, small[nm][which]))
    loss = 0.5 * tot[LOSS_ROW, 0]
    return (loss, grad_x.reshape(x.shape), *outs)
```

```python
import math

import jax
import jax.numpy as jnp
from jax import lax
from jax.experimental import pallas as pl
from jax.experimental.pallas import tpu as pltpu

F32 = jnp.float32
BF16 = jnp.bfloat16
MESH = pl.DeviceIdType.MESH
SDS = jax.ShapeDtypeStruct

D = 1024
HEAD_DIM = 64
N_HEADS = 16
N_KV = 2
GROUP = 8
KV_W = 128
BLK = 128
N_BUCKETS = 32
MAX_EXACT = 16
MAX_DISTANCE = 128
EPS = 1e-6
NEG_INF = -1e30
Q_SCALE = HEAD_DIM ** -0.5

ADAM_LR = 0.001
ADAM_B1 = 0.9
ADAM_B2 = 0.999
ADAM_EPS = 1e-08
ADAM_WD = 0.01
ADAM_STEP = 10

N_CHIPS = 4
N_DEV = 8
BIN_COLS = 2 * D // N_CHIPS
VMEM_LIMIT = 56 * 1024 * 1024
SMALL_ROWS = 16
LOSS_ROW = 6
SMALL_PLACES = {
    "a_pre_norm": ("quarter", 0, (1, D // 4)), "a_conv_w": ("quarter", 8, (3, 1, D // 4)),
    "a_post_norm": ("quarter", 1, (1, D // 4)), "kv_norm": ("rows", 2, (1, D)),
    "rel_bias": ("rel", 0, (N_HEADS, N_BUCKETS)), "b_pre_norm": ("rows", 3, (1, D)),
    "b_sinks": ("rows", 5, (1, N_HEADS)), "b_post_norm": ("rows", 4, (1, D)),
}


def _bucket_thresholds():
    def bucket(d):
        big = MAX_EXACT + int(math.log(d / MAX_EXACT) / math.log(MAX_DISTANCE / MAX_EXACT)
                              * (N_BUCKETS - MAX_EXACT))
        return d if d < MAX_EXACT else min(big, N_BUCKETS - 1)
    out = []
    for b in range(MAX_EXACT + 1, N_BUCKETS):
        out.append(min(d for d in range(MAX_EXACT, MAX_DISTANCE) if bucket(d) >= b))
    return tuple(out)


BUCKET_THRESHOLDS = _bucket_thresholds()


def _params(semantics=None, vmem=VMEM_LIMIT):
    return pltpu.CompilerParams(dimension_semantics=semantics, vmem_limit_bytes=vmem)


def _tile(n, pref):
    return pref if n >= 2 * pref else max(n // 2, 8)


def _rms_scale(v):
    return lax.rsqrt(jnp.mean(v * v, axis=-1, keepdims=True) + EPS)


def _nt(a, b):
    return lax.dot_general(a, b, (((1,), (1,)), ((), ())), preferred_element_type=F32)


def _tn(a, b):
    return lax.dot_general(a, b, (((0,), (0,)), ((), ())), preferred_element_type=F32)


def _nn(a, b):
    return jnp.dot(a, b, preferred_element_type=F32)


def _silu_parts(z):
    sg = jax.nn.sigmoid(z)
    return sg, z * sg


def _dsilu(z, sg):
    return sg * (1.0 + z * (1.0 - sg))


def _write_gradient(acc, out32, out16, stage, sem):
    whole = pltpu.make_async_copy(acc, out32, sem)
    whole.start()
    rows = stage.shape[0]
    for k in range(acc.shape[0] // rows):
        stage[...] = acc[rows * k:rows * (k + 1), :].astype(BF16)
        pltpu.sync_copy(stage, out16.at[pl.ds(rows * k, rows)])
    whole.wait()


def _acc_row(ref, row, val):
    ref[row:row + 1, :] += val


def _gather_copies(outs, splits, ici_send, ici_recv, d2d_send, d2d_recv):
    x, y, c = lax.axis_index("x"), lax.axis_index("y"), lax.axis_index("c")
    k = 2 * x + y
    sibling = (x, y, 1 - c)

    def part(o_ref, chip, core, split):
        if not split:
            return o_ref.at[chip]
        h = o_ref.shape[1] // 2
        return o_ref.at[chip, pl.ds(pl.multiple_of(core * h, 16), h)]

    def remote(ref, a, j, sems, to):
        return pltpu.make_async_remote_copy(src_ref=ref, dst_ref=ref, send_sem=sems[0].at[3 * a + j],
                                            recv_sem=sems[1].at[3 * a + j], device_id=to, device_id_type=MESH)

    copies = []
    for a, (o_ref, split) in enumerate(zip(outs, splits)):
        for j, (px, py) in enumerate([(x, 1 - y), (1 - x, y), (1 - x, 1 - y)]):
            kj = 2 * px + py
            ici, d2d = (ici_send, ici_recv), (d2d_send, d2d_recv)
            copies.append((remote(part(o_ref, k, c, split), a, j, ici, (px, py, c)),
                           remote(part(o_ref, kj, c, split), a, j, ici, (px, py, c)),
                           remote(part(o_ref, kj, c, split), a, j, d2d, sibling) if split else None,
                           remote(part(o_ref, kj, 1 - c, split), a, j, d2d, sibling) if split else None))
    return copies


def _gather_sems(n):
    return [pltpu.SemaphoreType.DMA((3 * n,)) for _ in range(4)]


def _prepare_weights(shards, small):
    n = len(shards)

    def body(*refs):
        ins, small_in = refs[:n], refs[n]
        outs, small_out = refs[n + 1:2 * n + 1], refs[2 * n + 1]
        stages, put_sem = refs[2 * n + 2:3 * n + 2], refs[3 * n + 2]
        sems = refs[3 * n + 3:]
        k = 2 * lax.axis_index("x") + lax.axis_index("y")
        puts = []
        for a, (i_ref, stage, o_ref) in enumerate(zip(ins, stages, outs)):
            stage[...] = i_ref[...].astype(BF16)
            puts.append(pltpu.make_async_copy(stage, o_ref.at[k], put_sem.at[a]))
            puts[-1].start()
        small_out[k] = small_in[...]
        copies = _gather_copies([small_out], [False], *sems)
        for send, _, _, _ in copies:
            send.start()
        for _, arrival, _, _ in copies:
            arrival.wait_recv()
        for send, _, _, _ in copies:
            send.wait_send()
        for put in puts:
            put.wait()

    vm = pl.BlockSpec(memory_space=pltpu.VMEM)
    anyspace = pl.BlockSpec(memory_space=pl.ANY)
    out_shape = [SDS((N_CHIPS,) + s.shape, BF16) for s in shards] + [SDS((N_CHIPS,) + small.shape, F32)]
    return pl.pallas_call(
        body, name="prepare_weights", out_shape=out_shape,
        in_specs=[vm] * (n + 1), out_specs=[anyspace] * n + [vm],
        scratch_shapes=[pltpu.VMEM(s.shape, BF16) for s in shards] + [pltpu.SemaphoreType.DMA((n,))] + _gather_sems(1),
        compiler_params=pltpu.CompilerParams(vmem_limit_bytes=VMEM_LIMIT),
    )(*shards, small)


def _a_in(chip, x, g_pre, weights, tm):
    s = x.shape[0]
    nt = s // tm
    n = len(weights)

    def body(chip_ref, x_ref, g_ref, *refs):
        proj_ref, n1_ref = refs[n:n + 2]
        gathered = refs[n + 2:2 * n + 2]
        wbuf, n1_all, fetch_sem = refs[2 * n + 2:2 * n + 5]
        sems = refs[2 * n + 5:]
        jj, i = pl.program_id(0), pl.program_id(1)
        copies = _gather_copies(gathered, [True] * n, *sems)

        def fetch(rel):
            slot = jnp.bitwise_xor(chip_ref[0], rel)
            return pltpu.make_async_copy(gathered[0].at[slot], wbuf.at[rel % 2], fetch_sem.at[rel % 2])

        @pl.when((jj == 0) & (i == 0))
        def _():
            fetch(0).start()
            copies[0][0].start()
            copies[1][0].start()
            fetch(0).wait()

        for rel in (1, 2, 3):
            @pl.when((jj == rel) & (i == 0))
            def _():
                fetch(rel).wait()

        @pl.when(jj == 0)
        def _():
            xv = x_ref[...]
            n1 = (xv * _rms_scale(xv) * g_ref[...]).astype(BF16)
            n1_ref[...] = n1
            n1_all[i] = n1
        proj_ref[...] = _nn(n1_all[i], wbuf[jj % 2]).astype(BF16)

        for rel in (1, 2, 3):
            @pl.when((jj == rel - 1) & (i == max(nt - 3, 0)))
            def _():
                _, arrival, forward, _ = copies[rel - 1]
                arrival.wait_recv()
                forward.start()
                if rel == 1:
                    for send, _, _, _ in copies[2:]:
                        send.start()

            @pl.when((jj == rel - 1) & (i == max(nt - 2, 0)))
            def _():
                copies[rel - 1][3].wait_recv()
                fetch(rel).start()

        @pl.when((jj == 3) & (i == max(nt - 2, 0)))
        def _():
            for _, arrival, forward, _ in copies[3:]:
                arrival.wait_recv()
                forward.start()

        @pl.when((jj == 3) & (i == nt - 1))
        def _():
            for _, _, _, forwarded in copies[3:]:
                forwarded.wait_recv()
            for send, _, forward, _ in copies:
                forward.wait_send()
                send.wait_send()

    anyspace = pl.BlockSpec(memory_space=pl.ANY)
    proj, n1, *gathered = pl.pallas_call(
        body, name="a_in",
        grid_spec=pltpu.PrefetchScalarGridSpec(
            num_scalar_prefetch=1, grid=(4, nt),
            in_specs=[pl.BlockSpec((tm, D), lambda jj, i, c: (jnp.where(jj == 0, i, nt - 1), 0)),
                      pl.BlockSpec((1, D), lambda jj, i, c: (0, 0))] + [anyspace] * n,
            out_specs=[pl.BlockSpec((tm, D), lambda jj, i, c: (i, jnp.bitwise_xor(c[0], jj))),
                       pl.BlockSpec((tm, D), lambda jj, i, c: (jnp.where(jj == 0, i, nt - 1), 0))] + [anyspace] * n,
            scratch_shapes=[pltpu.VMEM((2, D, D), BF16), pltpu.VMEM((nt, tm, D), BF16),
                            pltpu.SemaphoreType.DMA((2,))] + _gather_sems(n)),
        out_shape=[SDS((s, 4 * D), BF16), SDS((s, D), BF16)] + [SDS(w.shape, w.dtype) for w in weights],
        input_output_aliases={3 + a: 2 + a for a in range(n)},
        compiler_params=_params(("arbitrary", "arbitrary")),
    )(chip, x, g_pre, *weights)
    return proj, n1, gathered


def _shift_rows(v, last, second_last, rows):
    v1 = jnp.where(rows >= 1, pltpu.roll(v, 1, 0), last)
    v2 = jnp.where(rows >= 2, pltpu.roll(v, 2, 0), jnp.where(rows == 1, last, second_last))
    return v1, v2


def _a_mix(proj, x, conv_w, w_out, g_post, tm):
    s = x.shape[0]

    def body(proj_ref, x_ref, cw_ref, w_ref, g_ref, ya_ref, oa_ref, h1_ref, conv_ref, carry):
        @pl.when(pl.program_id(0) == 0)
        def _():
            carry[...] = jnp.zeros_like(carry)
        v = proj_ref[:, D:2 * D].astype(F32) * proj_ref[:, 2 * D:3 * D].astype(F32)
        rows = lax.broadcasted_iota(jnp.int32, (tm, D), 0)
        before = carry[...]
        v1, v2 = _shift_rows(v, before[7:8, :], before[6:7, :], rows)
        carry[...] = v[tm - 8:tm, :]
        conv = cw_ref[0:1, :] * v2 + cw_ref[1:2, :] * v1 + cw_ref[2:3, :] * v
        conv_ref[...] = conv.astype(BF16)
        _, sz = _silu_parts(proj_ref[:, 3 * D:4 * D].astype(F32))
        ya = (proj_ref[:, 0:D].astype(F32) * conv * sz).astype(BF16)
        ya_ref[...] = ya
        oa = _nn(ya, w_ref[...])
        oa_ref[...] = oa.astype(BF16)
        h1_ref[...] = x_ref[...] + oa * _rms_scale(oa) * g_ref[...]

    row = lambda i: (i, 0)
    fix = lambda i: (0, 0)
    return pl.pallas_call(
        body, name="a_mix", grid=(s // tm,),
        in_specs=[pl.BlockSpec((tm, 4 * D), row), pl.BlockSpec((tm, D), row), pl.BlockSpec((8, D), fix),
                  pl.BlockSpec((D, D), fix), pl.BlockSpec((1, D), fix)],
        out_specs=[pl.BlockSpec((tm, D), row)] * 4,
        out_shape=[SDS((s, D), BF16), SDS((s, D), BF16), SDS((s, D), F32), SDS((s, D), BF16)],
        scratch_shapes=[pltpu.VMEM((8, D), F32)],
        compiler_params=_params(("arbitrary",)),
    )(proj, x, conv_w, w_out, g_post)


def _b_in(h1, g_kv, g_pre, w_kv, wbin_g, tm):
    s = h1.shape[0]

    def body(h_ref, gk_ref, gb_ref, wkv_ref, wb_ref, kv_ref, q_ref, z_ref):
        h = h_ref[...]
        hh = h * _rms_scale(h)
        nk = (hh * gk_ref[...]).astype(BF16)
        nb = (hh * gb_ref[...]).astype(BF16)
        kv_ref[...] = _nn(nk, wkv_ref[...]).astype(BF16)
        for j in range(2):
            q_ref[:, BIN_COLS * j:BIN_COLS * (j + 1)] = (_nn(nb, wb_ref[j]) * Q_SCALE).astype(BF16)
            z_ref[:, BIN_COLS * j:BIN_COLS * (j + 1)] = _nn(nb, wb_ref[2 + j]).astype(BF16)

    row = lambda i: (i, 0)
    fix = lambda i: (0, 0)
    return pl.pallas_call(
        body, name="b_in", grid=(s // tm,),
        in_specs=[pl.BlockSpec((tm, D), row), pl.BlockSpec((1, D), fix), pl.BlockSpec((1, D), fix),
                  pl.BlockSpec((D, 2 * KV_W), fix), pl.BlockSpec((N_CHIPS, D, BIN_COLS), lambda i: (0, 0, 0))],
        out_specs=[pl.BlockSpec((tm, 2 * KV_W), row), pl.BlockSpec((tm, D), row), pl.BlockSpec((tm, D), row)],
        out_shape=[SDS((s, 2 * KV_W), BF16), SDS((s, D), BF16), SDS((s, D), BF16)],
        compiler_params=_params(("parallel",)),
    )(h1, g_kv, g_pre, w_kv, wbin_g)


def _buckets(dist):
    bucket = jnp.where(dist < MAX_EXACT, dist, MAX_EXACT)
    for t in BUCKET_THRESHOLDS:
        bucket = bucket + jnp.where(dist >= t, 1, 0)
    return bucket


def _head_place(h):
    kh, j, e = h // GROUP, (h % GROUP) // 2, h % 2
    return kh, slice(BLK * j, BLK * (j + 1)), slice(2 * BLK * e, 2 * BLK * (e + 1))


def _bias_table(rel_bias, sinks):
    def body(rb_ref, sink_ref, tab_ref):
        along = lax.broadcasted_iota(jnp.int32, (8, BLK), 1)
        row8 = lax.broadcasted_iota(jnp.int32, (8, BLK), 0)
        bucket = _buckets(jnp.where(along == 0, 0, BLK - along))
        query = lax.broadcasted_iota(jnp.int32, (BLK, BLK), 0)
        col = lax.broadcasted_iota(jnp.int32, (BLK, BLK), 1)
        for h in range(N_HEADS):
            by_dist = jnp.zeros((8, BLK), F32)
            for b in range(N_BUCKETS):
                by_dist = jnp.where(bucket == b, rb_ref[h, b], by_dist)
            for digit in range(3):
                by_dist = jnp.where((row8 >> digit) & 1 == 1, pltpu.roll(by_dist, 1 << digit, 1), by_dist)
            band = jnp.concatenate([by_dist] + [pltpu.roll(by_dist, 8 * g, 1) for g in range(1, BLK // 8)], axis=0)
            cur = jnp.where(col <= query, band, NEG_INF)
            kh, rows, cols = _head_place(h)
            prev_cols, cur_cols = slice(cols.start, cols.start + BLK), slice(cols.start + BLK, cols.stop)
            tab_ref[1, kh, rows, prev_cols] = jnp.where(col == 0, sink_ref[h], jnp.where(col > query, band, NEG_INF))
            tab_ref[1, kh, rows, cur_cols] = cur
            tab_ref[0, kh, rows, prev_cols] = jnp.where(col == 0, sink_ref[h], NEG_INF)
            tab_ref[0, kh, rows, cur_cols] = cur

    return pl.pallas_call(
        body, name="bias_table", out_shape=SDS((2, N_KV, 4 * BLK, 4 * BLK), F32),
        in_specs=[pl.BlockSpec(memory_space=pltpu.SMEM), pl.BlockSpec(memory_space=pltpu.SMEM)],
        out_specs=pl.BlockSpec(memory_space=pltpu.VMEM),
    )(rel_bias, sinks)


def _bias_fold(dtab, sink_row, pieces):
    def body(dtab_ref, *refs):
        piece_refs, (out_ref, smalls_ref) = refs[:len(pieces)], refs[len(pieces):]
        lane = lax.broadcasted_iota(jnp.int32, (N_HEADS, BLK), 1)
        bucket = _buckets(lane)
        col = lax.broadcasted_iota(jnp.int32, (BLK, BLK), 1)
        row8 = lax.broadcasted_iota(jnp.int32, (8, 128), 0)
        lane8 = lax.broadcasted_iota(jnp.int32, (8, 128), 1)
        by_dist = jnp.zeros((BLK, 128), F32)
        dsink = jnp.zeros((8, 128), F32)
        for h in range(N_HEADS):
            kh, rows, cols = _head_place(h)
            dt = dtab_ref[kh, rows, cols]
            band = jnp.where(col == 0, 0.0, dt[:, 0:BLK]) + dt[:, BLK:2 * BLK]
            for digit in range(BLK.bit_length() - 1):
                band = jnp.where((col >> digit) & 1 == 1, pltpu.roll(band, BLK - (1 << digit), 0), band)
            by_dist = jnp.where(col == h, jnp.sum(band, axis=1, keepdims=True), by_dist)
            dsink = dsink + jnp.where((row8 == 0) & (lane8 == h), jnp.sum(dt[:, 0:1]), 0.0)
        by_head = by_dist.T[0:N_HEADS]
        folded = jnp.zeros((N_HEADS, 128), F32)
        for b in range(N_BUCKETS):
            folded = jnp.where(lane == b, jnp.sum(jnp.where(bucket == b, by_head, 0.0), axis=1, keepdims=True), folded)
        out_ref[...] = folded
        smalls_ref[...] = jnp.zeros((SMALL_ROWS, D), F32)
        smalls_ref[sink_row:sink_row + 1, 0:128] = dsink[0:1]
        for (to_row, a, first, rows), ref in zip(pieces, piece_refs):
            smalls_ref[to_row:to_row + rows, 0:a.shape[1]] = ref[first:first + rows, :]

    vm = pl.BlockSpec(memory_space=pltpu.VMEM)
    return pl.pallas_call(
        body, name="bias_fold", out_shape=[SDS((N_HEADS, 128), F32), SDS((SMALL_ROWS, D), F32)],
        in_specs=[vm] * (1 + len(pieces)), out_specs=[vm, vm],
    )(dtab, *[a for _, a, _, _ in pieces])


def _pair_operands(prev, cur):
    t = jnp.concatenate([prev, cur], axis=0).astype(F32)
    t = jnp.where(lax.broadcasted_iota(jnp.int32, t.shape, 0) == 0, 0.0, t)
    tr = pltpu.roll(t, HEAD_DIM, 1)
    lo = lax.broadcasted_iota(jnp.int32, t.shape, 1) < HEAD_DIM
    zero = jnp.zeros_like(t)
    head0 = jnp.concatenate([jnp.where(lo, t, zero), jnp.where(lo, zero, tr)], axis=0).astype(BF16)
    head1 = jnp.concatenate([jnp.where(lo, tr, zero), jnp.where(lo, zero, t)], axis=0).astype(BF16)
    return head0, head1


def _pair_fold(d0, d1):
    lo = lax.broadcasted_iota(jnp.int32, (2 * BLK, KV_W), 1) < HEAD_DIM
    zero = jnp.zeros((2 * BLK, KV_W), F32)
    g0 = jnp.where(lo, d0[0:256], zero) + pltpu.roll(jnp.where(lo, zero, d0[256:512]), HEAD_DIM, 1)
    g1 = pltpu.roll(jnp.where(lo, d1[0:256], zero), HEAD_DIM, 1) + jnp.where(lo, zero, d1[256:512])
    return jnp.where(lax.broadcasted_iota(jnp.int32, (2 * BLK, KV_W), 0) == 0, 0.0, g0 + g1)


def _stack_pairs(ref, kh):
    return jnp.concatenate([ref[:, 128 * (4 * kh + j):128 * (4 * kh + j + 1)] for j in range(4)], axis=0)


def _table_spec():
    return pl.BlockSpec((1, N_KV, 4 * BLK, 4 * BLK), lambda n: (jnp.minimum(n, 1), 0, 0, 0))


def _attn_fwd(q, kv, tab):
    s = q.shape[0]

    def body(q_ref, kp_ref, kc_ref, vp_ref, vc_ref, tab_ref, att_ref, stats_ref):
        k2 = _pair_operands(kp_ref[...], kc_ref[...])
        v2 = _pair_operands(vp_ref[...], vc_ref[...])
        lane = lax.broadcasted_iota(jnp.int32, (BLK, 128), 1)
        stats = jnp.zeros((BLK, 128), F32)
        for kh in range(N_KV):
            sc = _nt(_stack_pairs(q_ref, kh), k2[kh])
            ps = []
            for e in range(2):
                lg = sc[:, 256 * e:256 * (e + 1)] + tab_ref[0, kh, :, 256 * e:256 * (e + 1)]
                m = jnp.max(lg, axis=-1, keepdims=True)
                ex = jnp.exp(lg - m)
                den = jnp.sum(ex, axis=-1, keepdims=True)
                ps.append(ex * (1.0 / den))
                lse = m + jnp.log(den)
                for j in range(4):
                    stats = jnp.where(lane == GROUP * kh + 2 * j + e, lse[BLK * j:BLK * (j + 1)], stats)
            out = _nn(jnp.concatenate(ps, axis=1).astype(BF16), v2[kh])
            for j in range(4):
                att_ref[:, 128 * (4 * kh + j):128 * (4 * kh + j + 1)] = out[BLK * j:BLK * (j + 1)].astype(BF16)
        stats_ref[...] = stats

    cur = lambda n: (n, 0)
    prev = lambda n: (jnp.maximum(n - 1, 0), 0)
    return pl.pallas_call(
        body, name="attn_fwd", grid=(s // BLK,),
        in_specs=[pl.BlockSpec((BLK, D), cur),
                  pl.BlockSpec((BLK, KV_W), prev), pl.BlockSpec((BLK, KV_W), cur),
                  pl.BlockSpec((BLK, KV_W), lambda n: (jnp.maximum(n - 1, 0), 1)),
                  pl.BlockSpec((BLK, KV_W), lambda n: (n, 1)), _table_spec()],
        out_specs=[pl.BlockSpec((BLK, D), cur), pl.BlockSpec((BLK, 128), cur)],
        out_shape=[SDS((s, D), BF16), SDS((s, 128), F32)],
        compiler_params=_params(("parallel",)),
    )(q, kv, kv, kv, kv, tab)


def _mid(att, zb, h1, tgt, w_out, g_post, tm):
    s = att.shape[0]
    nt = s // tm

    def body(att_ref, z_ref, h1_ref, t_ref, w_ref, g_ref,
             dh_ref, dqz_ref, datt_ref, loss_ref, dg_ref, dw_ref, dw16_ref, dw_acc, stage, put_sem):
        @pl.when(pl.program_id(0) == 0)
        def _():
            loss_ref[...] = jnp.zeros_like(loss_ref)
            dg_ref[...] = jnp.zeros_like(dg_ref)
            dw_acc[...] = jnp.zeros_like(dw_acc)
        att = att_ref[...].astype(F32)
        z = z_ref[...].astype(F32)
        sg, sz = _silu_parts(z)
        ob = (att * sz).astype(BF16)
        y2 = _nn(ob, w_ref[...])
        r2 = _rms_scale(y2)
        yh = y2 * r2
        g = g_ref[...]
        err = (h1_ref[...] + yh * g) - t_ref[...]
        loss_ref[...] += jnp.sum(jnp.sum(err * err, axis=-1, keepdims=True) / D)
        dh = err / D
        dh_ref[...] = dh
        _acc_row(dg_ref, 0, jnp.sum(dh * yh, axis=0, keepdims=True))
        dyh = dh * g
        dy = (r2 * (dyh - yh * jnp.mean(dyh * yh, axis=-1, keepdims=True))).astype(BF16)
        dw_acc[...] += _tn(ob, dy)
        dob = _nt(dy, w_ref[...])
        datt_ref[...] = (dob * sz).astype(BF16)
        dqz_ref[...] = (dob * att * _dsilu(z, sg)).astype(BF16)

        @pl.when(pl.program_id(0) == nt - 1)
        def _():
            _write_gradient(dw_acc, dw_ref, dw16_ref, stage, put_sem)

    row = lambda i: (i, 0)
    fix = lambda i: (0, 0)
    anyspace = pl.BlockSpec(memory_space=pl.ANY)
    return pl.pallas_call(
        body, name="mid", grid=(nt,),
        in_specs=[pl.BlockSpec((tm, D), row)] * 4 + [pl.BlockSpec((D, D), fix), pl.BlockSpec((1, D), fix)],
        out_specs=[pl.BlockSpec((tm, D), row), pl.BlockSpec((tm, D), lambda i: (i, 1)), pl.BlockSpec((tm, D), row),
                   pl.BlockSpec((8, 128), fix), pl.BlockSpec((8, D), fix), anyspace, anyspace],
        out_shape=[SDS((s, D), F32), SDS((s, 2 * D), BF16), SDS((s, D), BF16), SDS((8, 128), F32),
                   SDS((8, D), F32), SDS((D, D), F32), SDS((D, D), BF16)],
        scratch_shapes=[pltpu.VMEM((D, D), F32), pltpu.VMEM((D // 4, D), BF16), pltpu.SemaphoreType.DMA],
        compiler_params=_params(("arbitrary",)),
    )(att, zb, h1, tgt, w_out, g_post)


def _attn_bwd(q, kv, datt, stats, tab, dqz):
    s = q.shape[0]
    nb = s // BLK

    def body(q_ref, kp_ref, kc_ref, vp_ref, vc_ref, da_ref, st_ref, tab_ref, dqz_in,
             dq_ref, dkv_ref, dtab_ref, dk_carry, dv_carry):
        del dqz_in
        n = pl.program_id(0)

        @pl.when(n == 0)
        def _():
            dtab_ref[...] = jnp.zeros_like(dtab_ref)
            dk_carry[...] = jnp.zeros_like(dk_carry)
            dv_carry[...] = jnp.zeros_like(dv_carry)

        @pl.when(n < nb)
        def _():
            k2 = _pair_operands(kp_ref[...], kc_ref[...])
            v2 = _pair_operands(vp_ref[...], vc_ref[...])
            lane = lax.broadcasted_iota(jnp.int32, (BLK, 128), 1)
            stats = st_ref[...]
            dk2, dv2 = [], []
            for kh in range(N_KV):
                qs = _stack_pairs(q_ref, kh)
                das = _stack_pairs(da_ref, kh)
                sc = _nt(qs, k2[kh])
                dp = _nt(das, v2[kh])
                ps, dss = [], []
                for e in range(2):
                    heads = [GROUP * kh + 2 * j + e for j in range(4)]
                    lse = jnp.concatenate([jnp.sum(jnp.where(lane == h, stats, 0.0), axis=-1, keepdims=True)
                                           for h in heads], axis=0)
                    cols = slice(256 * e, 256 * (e + 1))
                    p = jnp.exp(sc[:, cols] + tab_ref[0, kh, :, cols] - lse)
                    delta = jnp.sum(p * dp[:, cols], axis=-1, keepdims=True)
                    ds = p * (dp[:, cols] - delta)
                    dtab_ref[kh, :, cols] += ds
                    ps.append(p)
                    dss.append(ds)
                p2 = jnp.concatenate(ps, axis=1).astype(BF16)
                ds2 = jnp.concatenate(dss, axis=1).astype(BF16)
                dq = _nn(ds2, k2[kh]) * Q_SCALE
                for j in range(4):
                    dq_ref[:, 128 * (4 * kh + j):128 * (4 * kh + j + 1)] = dq[BLK * j:BLK * (j + 1)].astype(BF16)
                dk2.append(_tn(ds2, qs))
                dv2.append(_tn(p2, das))
            dkk = _pair_fold(dk2[0], dk2[1])
            dvv = _pair_fold(dv2[0], dv2[1])
            dkv_ref[:, 0:KV_W] = (dk_carry[...] + dkk[0:BLK]).astype(BF16)
            dkv_ref[:, KV_W:2 * KV_W] = (dv_carry[...] + dvv[0:BLK]).astype(BF16)
            dk_carry[...] = dkk[BLK:2 * BLK]
            dv_carry[...] = dvv[BLK:2 * BLK]

        @pl.when(n == nb)
        def _():
            dkv_ref[:, 0:KV_W] = dk_carry[...].astype(BF16)
            dkv_ref[:, KV_W:2 * KV_W] = dv_carry[...].astype(BF16)

    cur = lambda n: (jnp.minimum(n, nb - 1), 0)
    prev = lambda n: (jnp.clip(n - 1, 0, nb - 1), 0)
    return pl.pallas_call(
        body, name="attn_bwd", grid=(nb + 1,),
        in_specs=[pl.BlockSpec((BLK, D), cur),
                  pl.BlockSpec((BLK, KV_W), prev), pl.BlockSpec((BLK, KV_W), cur),
                  pl.BlockSpec((BLK, KV_W), lambda n: (jnp.clip(n - 1, 0, nb - 1), 1)),
                  pl.BlockSpec((BLK, KV_W), lambda n: (jnp.minimum(n, nb - 1), 1)),
                  pl.BlockSpec((BLK, D), cur), pl.BlockSpec((BLK, 128), cur), _table_spec(),
                  pl.BlockSpec(memory_space=pl.ANY)],
        out_specs=[pl.BlockSpec((BLK, D), cur), pl.BlockSpec((BLK, 2 * KV_W), prev),
                   pl.BlockSpec((N_KV, 4 * BLK, 4 * BLK), lambda n: (0, 0, 0))],
        out_shape=[SDS((s, 2 * D), BF16), SDS((s, 2 * KV_W), BF16), SDS((N_KV, 4 * BLK, 4 * BLK), F32)],
        scratch_shapes=[pltpu.VMEM((BLK, KV_W), F32), pltpu.VMEM((BLK, KV_W), F32)],
        input_output_aliases={8: 0},
        compiler_params=_params(("arbitrary",)),
    )(q, kv, kv, kv, kv, datt, stats, tab, dqz)


def _b_bwd(dqz, dkv, h1, dh2, oa, wbin_g, w_kv, g_kv, g_pre, g_apost, tm):
    s = h1.shape[0]
    nt = s // tm

    def body(dqz_ref, dkv_ref, h_ref, dh2_ref, oa_ref, wb_ref, wkv_ref, gk_ref, gb_ref, ga_ref,
             dh1_ref, doa_ref, dg_ref, dwb_ref, dwkv_ref, dwb16_ref, dwkv16_ref, wcat, dwb_acc, dwkv_acc, put_sem):
        @pl.when(pl.program_id(0) == 0)
        def _():
            dg_ref[...] = jnp.zeros_like(dg_ref)
            dwb_acc[...] = jnp.zeros_like(dwb_acc)
            dwkv_acc[...] = jnp.zeros_like(dwkv_acc)
            for j in range(N_CHIPS):
                pltpu.sync_copy(wb_ref.at[j], wcat.at[:, pl.ds(BIN_COLS * j, BIN_COLS)])
        dnb = _nt(dqz_ref[...], wcat[...])
        dnk = _nt(dkv_ref[...], wkv_ref[...])
        h = h_ref[...]
        r = _rms_scale(h)
        hh = h * r
        dwb_acc[...] += _tn((hh * gb_ref[...]).astype(BF16), dqz_ref[...])
        dwkv_acc[...] += _tn((hh * gk_ref[...]).astype(BF16), dkv_ref[...])
        _acc_row(dg_ref, 0, jnp.sum(dnk * hh, axis=0, keepdims=True))
        _acc_row(dg_ref, 1, jnp.sum(dnb * hh, axis=0, keepdims=True))
        dhh = dnb * gb_ref[...] + dnk * gk_ref[...]
        dh1 = dh2_ref[...] + r * (dhh - hh * jnp.mean(dhh * hh, axis=-1, keepdims=True))
        dh1_ref[...] = dh1
        oa = oa_ref[...].astype(F32)
        ra = _rms_scale(oa)
        oh = oa * ra
        _acc_row(dg_ref, 2, jnp.sum(dh1 * oh, axis=0, keepdims=True))
        doh = dh1 * ga_ref[...]
        doa_ref[...] = (ra * (doh - oh * jnp.mean(doh * oh, axis=-1, keepdims=True))).astype(BF16)

        @pl.when(pl.program_id(0) == nt - 1)
        def _():
            wcat[...] = dwb_acc[...].astype(BF16)
            puts = [pltpu.make_async_copy(dwkv_acc, dwkv_ref, put_sem.at[2 * N_CHIPS])]
            for j in range(N_CHIPS):
                cols = pl.ds(BIN_COLS * j, BIN_COLS)
                puts.append(pltpu.make_async_copy(dwb_acc.at[:, cols], dwb_ref.at[j], put_sem.at[2 * j]))
                puts.append(pltpu.make_async_copy(wcat.at[:, cols], dwb16_ref.at[j], put_sem.at[2 * j + 1]))
            for put in puts:
                put.start()
            for put in puts:
                put.wait()
            wcat[:, 0:2 * KV_W] = dwkv_acc[...].astype(BF16)
            pltpu.sync_copy(wcat.at[:, pl.ds(0, 2 * KV_W)], dwkv16_ref)

    row = lambda i: (i, 0)
    fix = lambda i: (0, 0)
    anyspace = pl.BlockSpec(memory_space=pl.ANY)
    return pl.pallas_call(
        body, name="b_bwd", grid=(nt,),
        in_specs=[pl.BlockSpec((tm, 2 * D), row), pl.BlockSpec((tm, 2 * KV_W), row), pl.BlockSpec((tm, D), row),
                  pl.BlockSpec((tm, D), row), pl.BlockSpec((tm, D), row), anyspace, pl.BlockSpec((D, 2 * KV_W), fix),
                  pl.BlockSpec((1, D), fix), pl.BlockSpec((1, D), fix), pl.BlockSpec((1, D), fix)],
        out_specs=[pl.BlockSpec((tm, D), row), pl.BlockSpec((tm, D), row), pl.BlockSpec((8, D), fix)] + [anyspace] * 4,
        out_shape=[SDS((s, D), F32), SDS((s, D), BF16), SDS((8, D), F32), SDS((N_CHIPS, D, BIN_COLS), F32),
                   SDS((D, 2 * KV_W), F32), SDS((N_CHIPS, D, BIN_COLS), BF16), SDS((D, 2 * KV_W), BF16)],
        scratch_shapes=[pltpu.VMEM((D, 2 * D), BF16), pltpu.VMEM((D, 2 * D), F32), pltpu.VMEM((D, 2 * KV_W), F32),
                        pltpu.SemaphoreType.DMA((2 * N_CHIPS + 1,))],
        compiler_params=_params(("arbitrary",)),
    )(dqz, dkv, h1, dh2, oa, wbin_g, w_kv, g_kv, g_pre, g_apost)


def _to_owner_core(pieces, r, send, recv, core, action):
    x, y, c = lax.axis_index("x"), lax.axis_index("y"), lax.axis_index("c")
    for kp in range(N_CHIPS):
        px, py = kp >> 1, kp & 1
        rel = 4 * (x + px - 2 * x * px) + 2 * (y + py - 2 * y * py) + (c + core - 2 * c * core)

        @pl.when(rel != 0)
        def _():
            cp = pltpu.make_async_remote_copy(src_ref=pieces.at[kp], dst_ref=r.at[rel - 1], send_sem=send.at[kp],
                                              recv_sem=recv.at[rel - 1], device_id=(px, py, core), device_id_type=MESH)
            if action == "start":
                cp.start()
            else:
                cp.wait_send()
    if action == "wait":
        @pl.when(c == core)
        def _():
            for rel in range(1, N_DEV):
                pltpu.make_async_remote_copy(src_ref=pieces.at[0], dst_ref=r.at[rel - 1], send_sem=send.at[0],
                                             recv_sem=recv.at[rel - 1], device_id=(x, y, c),
                                             device_id_type=MESH).wait_recv()


def _owner_core_sems():
    return [pltpu.SemaphoreType.DMA((N_CHIPS,)), pltpu.SemaphoreType.DMA((N_DEV - 1,))]


def _device_exchange(grads, recvs, send, recv):
    x, y, c = lax.axis_index("x"), lax.axis_index("y"), lax.axis_index("c")
    copies = []
    for a, (g, r) in enumerate(zip(grads, recvs)):
        h = g.shape[1] // 2
        for rel in range(1, N_DEV):
            fx, fy, fc = rel >> 2, (rel >> 1) & 1, rel & 1
            px, py, pc = x + fx - 2 * x * fx, y + fy - 2 * y * fy, c + fc - 2 * c * fc
            sem = (N_DEV - 1) * a + rel - 1
            copies.append(pltpu.make_async_remote_copy(
                src_ref=g.at[2 * px + py, pl.ds(pl.multiple_of(pc * h, 16), h)], dst_ref=r.at[rel - 1],
                send_sem=send.at[sem], recv_sem=recv.at[sem], device_id=(px, py, pc), device_id_type=MESH))
    return copies


def _device_exchange_specs(grads):
    anyspace = pl.BlockSpec(memory_space=pl.ANY)
    n = len(grads)
    count = (N_DEV - 1) * n
    return ([anyspace] * n, [anyspace] * n,
            [SDS((N_DEV - 1, g.shape[1] // 2, g.shape[2]), g.dtype) for g in grads],
            [pltpu.SemaphoreType.DMA((count,)), pltpu.SemaphoreType.DMA((count,))])


def _a_bwd(doa, ya, conv, proj, conv_w, w_out, tm, parts):
    s = doa.shape[0]
    nt = s // tm
    n = len(parts)
    ex_in, ex_out, ex_shape, ex_sems = _device_exchange_specs(parts)

    def body(*refs):
        doa_ref, ya_ref, conv_ref, proj_ref, cw_ref, w_ref = refs[:6]
        part_refs = refs[6:6 + n]
        dproj_ref, dcw_ref, dw_ref, dw16_ref = refs[6 + n:10 + n]
        recv_refs = refs[10 + n:10 + 2 * n]
        carry, dw_acc, stage, put_sem, send, recv = refs[10 + 2 * n:]
        i = pl.program_id(0)

        @pl.when(i == 0)
        def _():
            dcw_ref[...] = jnp.zeros_like(dcw_ref)
            carry[...] = jnp.zeros_like(carry)
            dw_acc[...] = jnp.zeros_like(dw_acc)
            for cp in _device_exchange(part_refs, recv_refs, send, recv):
                cp.start()
        dya = _nt(doa_ref[...], w_ref[...])
        dw_acc[...] += _tn(ya_ref[...], doa_ref[...])
        bg = proj_ref[:, 0:D].astype(F32)
        cg = proj_ref[:, D:2 * D].astype(F32)
        u = proj_ref[:, 2 * D:3 * D].astype(F32)
        z = proj_ref[:, 3 * D:4 * D].astype(F32)
        v = cg * u
        rows = lax.broadcasted_iota(jnp.int32, (tm, D), 0)
        conv = conv_ref[...].astype(F32)
        sg, sz = _silu_parts(z)
        dproj_ref[:, 0:D] = (dya * conv * sz).astype(BF16)
        dproj_ref[:, 3 * D:4 * D] = (dya * bg * conv * _dsilu(z, sg)).astype(BF16)
        dconv = dya * bg * sz
        after = carry[...]
        up1 = jnp.where(rows < tm - 1, pltpu.roll(dconv, tm - 1, 0), after[0:1, :])
        up2 = jnp.where(rows < tm - 2, pltpu.roll(dconv, tm - 2, 0),
                        jnp.where(rows == tm - 2, after[0:1, :], after[1:2, :]))
        carry[...] = dconv[0:8, :]
        _acc_row(dcw_ref, 0, jnp.sum(up2 * v, axis=0, keepdims=True))
        _acc_row(dcw_ref, 1, jnp.sum(up1 * v, axis=0, keepdims=True))
        _acc_row(dcw_ref, 2, jnp.sum(dconv * v, axis=0, keepdims=True))
        dv = cw_ref[2:3, :] * dconv + cw_ref[1:2, :] * up1 + cw_ref[0:1, :] * up2
        dproj_ref[:, D:2 * D] = (dv * u).astype(BF16)
        dproj_ref[:, 2 * D:3 * D] = (dv * cg).astype(BF16)

        @pl.when(i == nt - 1)
        def _():
            _write_gradient(dw_acc, dw_ref, dw16_ref, stage, put_sem)
            for cp in _device_exchange(part_refs, recv_refs, send, recv):
                cp.wait()

    rev = lambda i: (nt - 1 - i, 0)
    fix = lambda i: (0, 0)
    anyspace = pl.BlockSpec(memory_space=pl.ANY)
    dproj, dcw, dw, dw16, *got = pl.pallas_call(
        body, name="a_bwd", grid=(nt,),
        in_specs=[pl.BlockSpec((tm, D), rev), pl.BlockSpec((tm, D), rev), pl.BlockSpec((tm, D), rev),
                  pl.BlockSpec((tm, 4 * D), rev), pl.BlockSpec((8, D), fix), pl.BlockSpec((D, D), fix)] + ex_in,
        out_specs=[pl.BlockSpec((tm, 4 * D), rev), pl.BlockSpec((8, D), fix), anyspace, anyspace] + ex_out,
        out_shape=[SDS((s, 4 * D), BF16), SDS((8, D), F32), SDS((D, D), F32), SDS((D, D), BF16)] + ex_shape,
        scratch_shapes=[pltpu.VMEM((8, D), F32), pltpu.VMEM((D, D), F32), pltpu.VMEM((D // 4, D), BF16),
                        pltpu.SemaphoreType.DMA] + ex_sems,
        compiler_params=_params(("arbitrary",)),
    )(doa, ya, conv, proj, conv_w, w_out, *parts)
    return dproj, dcw, dw, dw16, got


def _dn1(dp_ref, w_ref):
    dn = _nt(dp_ref[:, 0:D], w_ref[0])
    for j in range(1, 4):
        dn = dn + _nt(dp_ref[:, D * j:D * (j + 1)], w_ref[j])
    return dn


def _a_in_bwd_matmul(dproj, win_g, tm, count, win_half, win_got):
    def body(dp_ref, w_ref, half_ref, got_in, dn_ref, got_ref, wcat, send, recv):
        del got_in

        @pl.when(pl.program_id(0) == 0)
        def _():
            _to_owner_core(half_ref, got_ref, send, recv, 1, "start")
            for j in range(N_CHIPS):
                pltpu.sync_copy(w_ref.at[j], wcat.at[:, pl.ds(D * j, D)])
        dn_ref[...] = _nt(dp_ref[...], wcat[...]).astype(BF16)

        @pl.when(pl.program_id(0) == count - 1)
        def _():
            _to_owner_core(half_ref, got_ref, send, recv, 1, "wait")

    row = lambda i: (i, 0)
    anyspace = pl.BlockSpec(memory_space=pl.ANY)
    return pl.pallas_call(
        body, name="a_in_bwd_matmul", grid=(count,),
        in_specs=[pl.BlockSpec((tm, 4 * D), row), anyspace, anyspace, anyspace],
        out_specs=[pl.BlockSpec((tm, D), row), anyspace],
        out_shape=[SDS((count * tm, D), BF16), SDS(win_got.shape, win_got.dtype)],
        scratch_shapes=[pltpu.VMEM((D, 4 * D), BF16)] + _owner_core_sems(),
        input_output_aliases={3: 1},
        compiler_params=_params(("arbitrary",)),
    )(dproj, win_g, win_half, win_got)


def _a_in_bwd(dn_first, dproj, x, dh1, win_g, g_pre, tm):
    s = x.shape[0]
    nt = s // tm
    count = dn_first.shape[0] // tm

    def body(dn_ref, dp_ref, x_ref, dh_ref, w_ref, g_ref, gx_ref, dg_ref, dn_s):
        i = pl.program_id(0)

        @pl.when(i == 0)
        def _():
            dg_ref[...] = jnp.zeros_like(dg_ref)

        @pl.when(i < count)
        def _():
            dn_s[...] = dn_ref[...].astype(F32)

        @pl.when(i >= count)
        def _():
            dn_s[...] = _dn1(dp_ref, w_ref)
        dn = dn_s[...]
        xv = x_ref[...]
        r = _rms_scale(xv)
        xh = xv * r
        _acc_row(dg_ref, 0, jnp.sum(dn * xh, axis=0, keepdims=True))
        dxh = dn * g_ref[...]
        gx_ref[...] = dh_ref[...] + r * (dxh - xh * jnp.mean(dxh * xh, axis=-1, keepdims=True))

    row = lambda i: (i, 0)
    fix = lambda i: (0, 0)
    return pl.pallas_call(
        body, name="a_in_bwd", grid=(nt,),
        in_specs=[pl.BlockSpec((tm, D), lambda i: (jnp.minimum(i, count - 1), 0)),
                  pl.BlockSpec((tm, 4 * D), lambda i: (jnp.maximum(i, count), 0)),
                  pl.BlockSpec((tm, D), row), pl.BlockSpec((tm, D), row),
                  pl.BlockSpec((4, D, D), lambda i: (0, 0, 0)), pl.BlockSpec((1, D), fix)],
        out_specs=[pl.BlockSpec((tm, D), row), pl.BlockSpec((8, D), fix)],
        out_shape=[SDS((s, D), F32), SDS((8, D), F32)],
        scratch_shapes=[pltpu.VMEM((tm, D), F32)],
        compiler_params=_params(("arbitrary",)),
    )(dn_first, dproj, x, dh1, win_g, g_pre)


def _swap_halves(shards, send, recv):
    x, y, c = lax.axis_index("x"), lax.axis_index("y"), lax.axis_index("c")
    sibling = (x, y, 1 - c)
    copies = []
    for b, full in enumerate(shards):
        h = full.shape[0] // 2
        mine = full.at[pl.ds(pl.multiple_of(c * h, 8), h)]
        theirs = full.at[pl.ds(pl.multiple_of((1 - c) * h, 8), h)]
        copies.append((pltpu.make_async_remote_copy(src_ref=mine, dst_ref=mine, send_sem=send.at[b], recv_sem=recv.at[b],
                                                    device_id=sibling, device_id_type=MESH),
                       pltpu.make_async_remote_copy(src_ref=mine, dst_ref=theirs, send_sem=send.at[b], recv_sem=recv.at[b],
                                                    device_id=sibling, device_id_type=MESH)))
    return copies


def _dw_in_half(n1, dproj, core, tmw, name, to_owners=None, to_devices=None, shards=()):
    s = n1.shape[0]
    h = D // 2
    nt = s // tmw
    n_sh = len(shards)
    if to_owners is not None:
        sent_array, sems, got_shape = to_owners, _owner_core_sems(), SDS((N_DEV - 1, h, D), BF16)
    else:
        sent_array = to_devices
        _, _, (got_shape,), sems = _device_exchange_specs([to_devices])

    def body(*refs):
        a_ref, b_ref, sent = refs[:3]
        o_ref, o16_ref, got = refs[3 + n_sh:6 + n_sh]
        shard_refs = refs[6 + n_sh:6 + 2 * n_sh]
        send, recv = refs[6 + 2 * n_sh:8 + 2 * n_sh]
        swap_sems = refs[8 + 2 * n_sh:]
        j, t = pl.program_id(0), pl.program_id(1)

        def exchange(action):
            if to_owners is not None:
                _to_owner_core(sent, got, send, recv, 1 - core, action)
            else:
                for cp in _device_exchange([sent], [got], send, recv):
                    cp.start() if action == "start" else cp.wait()

        @pl.when((j == 0) & (t == 0))
        def _():
            exchange("start")
            if n_sh:
                for mine, _ in _swap_halves(shard_refs, *swap_sems):
                    mine.start()

        @pl.when(t == 0)
        def _():
            o_ref[...] = jnp.zeros_like(o_ref)
        o_ref[0] += _tn(a_ref[...], b_ref[...])

        @pl.when(t == nt - 1)
        def _():
            o16_ref[...] = o_ref[...].astype(BF16)

        @pl.when((j == N_CHIPS - 1) & (t == nt - 1))
        def _():
            exchange("wait")
            if n_sh:
                for mine, theirs in _swap_halves(shard_refs, *swap_sems):
                    theirs.wait_recv()
                    mine.wait_send()

    anyspace = pl.BlockSpec(memory_space=pl.ANY)
    slot = pl.BlockSpec((1, h, D), lambda j, t: (j, 0, 0))
    swap_scratch = [pltpu.SemaphoreType.DMA((n_sh,)), pltpu.SemaphoreType.DMA((n_sh,))] if n_sh else []
    return pl.pallas_call(
        body, name=name, grid=(N_CHIPS, nt),
        in_specs=[pl.BlockSpec((tmw, h), lambda j, t: (t, core)), pl.BlockSpec((tmw, D), lambda j, t: (t, j))]
        + [anyspace] * (1 + n_sh),
        out_specs=[slot, slot] + [anyspace] * (1 + n_sh),
        out_shape=[SDS((N_CHIPS, h, D), F32), SDS((N_CHIPS, h, D), BF16), got_shape]
        + [SDS(sh.shape, F32) for sh in shards],
        scratch_shapes=sems + swap_scratch,
        input_output_aliases={3 + b: 3 + b for b in range(n_sh)},
        compiler_params=_params(("arbitrary", "arbitrary")),
    )(n1, dproj, sent_array, *shards)


def _share_and_gather(shards, smalls):
    n_h, n_s = len(shards), len(smalls)

    def body(*refs):
        small_ins = refs[n_h:n_h + n_s]
        fs = refs[n_h + n_s:2 * n_h + n_s]
        small_alls = refs[2 * n_h + n_s:2 * n_h + 2 * n_s]
        dsend, drecv, ssend, srecv = refs[2 * n_h + 2 * n_s:]
        x, y, c = lax.axis_index("x"), lax.axis_index("y"), lax.axis_index("c")
        swaps = _swap_halves(fs, dsend, drecv)
        sends, arrivals = [mine for mine, _ in swaps], [theirs for _, theirs in swaps]
        me = 4 * x + 2 * y + c
        for k, (small_in, small_all) in enumerate(zip(small_ins, small_alls)):
            small_all[me] = small_in[...]
            for rel in range(1, N_DEV):
                fx, fy, fc = rel >> 2, (rel >> 1) & 1, rel & 1
                peer = (x + fx - 2 * x * fx, y + fy - 2 * y * fy, c + fc - 2 * c * fc)
                sender = 4 * peer[0] + 2 * peer[1] + peer[2]
                sem = (N_DEV - 1) * k + rel - 1
                sends.append(pltpu.make_async_remote_copy(
                    src_ref=small_in, dst_ref=small_all.at[me], send_sem=ssend.at[sem], recv_sem=srecv.at[sem],
                    device_id=peer, device_id_type=MESH))
                arrivals.append(pltpu.make_async_remote_copy(
                    src_ref=small_in, dst_ref=small_all.at[sender], send_sem=ssend.at[sem], recv_sem=srecv.at[sem],
                    device_id=peer, device_id_type=MESH))
        for cp in sends:
            cp.start()
        for cp in arrivals:
            cp.wait_recv()
        for cp in sends:
            cp.wait_send()

    anyspace = pl.BlockSpec(memory_space=pl.ANY)
    vm = pl.BlockSpec(memory_space=pltpu.VMEM)
    out_shape = [SDS(full.shape, F32) for full in shards] + [SDS((N_DEV,) + sm.shape, F32) for sm in smalls]
    n_all = (N_DEV - 1) * n_s
    outs = pl.pallas_call(
        body, name="share_and_gather", out_shape=out_shape,
        in_specs=[anyspace] * n_h + [vm] * n_s, out_specs=[anyspace] * n_h + [vm] * n_s,
        scratch_shapes=[pltpu.SemaphoreType.DMA((n_h,)), pltpu.SemaphoreType.DMA((n_h,)),
                        pltpu.SemaphoreType.DMA((n_all,)), pltpu.SemaphoreType.DMA((n_all,))],
        input_output_aliases={b: b for b in range(n_h)},
    )(*shards, *smalls)
    return outs[:n_h], outs[n_h:]


def _add_win(where, lo, hi, r, name):
    _, h, cols = lo.shape
    tr = min(h, 256)
    nh = h // tr

    def body(where_ref, lo_ref, hi_ref, r_ref, o_ref):
        acc = jnp.where(where_ref[0] == 0, lo_ref[0], hi_ref[0])
        for k in range(N_DEV - 1):
            acc = acc + r_ref[k].astype(F32)
        o_ref[...] = acc

    own = pl.BlockSpec((1, tr, cols), lambda i, w: (w[1], i, 0))
    return pl.pallas_call(
        body, name=name,
        grid_spec=pltpu.PrefetchScalarGridSpec(
            num_scalar_prefetch=1, grid=(nh,),
            in_specs=[own, own, pl.BlockSpec((N_DEV - 1, tr, cols), lambda i, w: (0, i, 0))],
            out_specs=pl.BlockSpec((tr, cols), lambda i, w: (w[0] * nh + i, 0))),
        out_shape=SDS((2 * h, cols), F32),
        compiler_params=_params(("parallel",)),
    )(where, lo, hi, r)


def _add_devices(where, g, r, name):
    _, rows, cols = g.shape
    h = rows // 2
    tr = min(h, 256)
    nh = h // tr

    def body(where_ref, g_ref, r_ref, o_ref):
        del where_ref
        acc = g_ref[0]
        for k in range(N_DEV - 1):
            acc = acc + r_ref[k].astype(F32)
        o_ref[...] = acc

    return pl.pallas_call(
        body, name=name,
        grid_spec=pltpu.PrefetchScalarGridSpec(
            num_scalar_prefetch=1, grid=(nh,),
            in_specs=[pl.BlockSpec((1, tr, cols), lambda i, w: (w[1], w[0] * nh + i, 0)),
                      pl.BlockSpec((N_DEV - 1, tr, cols), lambda i, w: (0, i, 0))],
            out_specs=pl.BlockSpec((tr, cols), lambda i, w: (w[0] * nh + i, 0))),
        out_shape=SDS((rows, cols), F32),
        compiler_params=_params(("parallel",)),
    )(where, g, r)


def _sum_smalls(gathered):
    n = len(gathered)

    def body(*refs):
        for all_ref, o_ref in zip(refs[:n], refs[n:]):
            acc = all_ref[0]
            for dev in range(1, N_DEV):
                acc = acc + all_ref[dev]
            o_ref[...] = acc

    vm = pl.BlockSpec(memory_space=pltpu.VMEM)
    return pl.pallas_call(
        body, name="sum_smalls", out_shape=[SDS(a.shape[1:], F32) for a in gathered],
        in_specs=[vm] * n, out_specs=[vm] * n,
    )(*gathered)


def _adam_step(g, w, m, v):
    nm = ADAM_B1 * m + (1.0 - ADAM_B1) * g
    nv = ADAM_B2 * v + (1.0 - ADAM_B2) * (g * g)
    m_hat = nm / (1.0 - ADAM_B1 ** ADAM_STEP)
    v_hat = nv / (1.0 - ADAM_B2 ** ADAM_STEP)
    return -ADAM_LR * (m_hat / (jnp.sqrt(v_hat) + ADAM_EPS) + ADAM_WD * w), nm, nv


def _adamw(g, w, m, v, name):
    rows, cols = g.shape
    tr = min(rows, 256)

    def body(g_ref, w_ref, m_ref, v_ref, g_out_ref, d_ref, nm_ref, nv_ref):
        g_out_ref[...] = g_ref[...]
        d_ref[...], nm_ref[...], nv_ref[...] = _adam_step(g_ref[...], w_ref[...], m_ref[...], v_ref[...])

    spec = pl.BlockSpec((tr, cols), lambda i: (i, 0))
    return pl.pallas_call(
        body, name=name, grid=(rows // tr,), in_specs=[spec] * 4, out_specs=[spec] * 4,
        out_shape=[SDS(g.shape, F32)] * 4, compiler_params=_params(("parallel",)),
    )(g, w, m, v)


def _small_update(chip, tot, tot_rel, wmv):
    names = list(SMALL_PLACES)
    n = len(names)

    def body(chip_ref, tot_ref, quarter_ref, rel_ref, *refs):
        del chip_ref
        ins, outs = refs[:3 * n], refs[3 * n:]
        outs[4 * n][...] = 0.5 * tot_ref[LOSS_ROW:LOSS_ROW + 1, 0:1]
        for i, nm in enumerate(names):
            source, row, shape = SMALL_PLACES[nm]
            from_ref = {"rows": tot_ref, "quarter": quarter_ref, "rel": rel_ref}[source]
            for at in ([Ellipsis] if len(shape) == 2 else range(shape[0])):
                g = from_ref[row:row + shape[0], 0:shape[-1]] if at is Ellipsis else from_ref[row + at:row + at + 1, 0:shape[-1]]
                outs[4 * i][at] = g
                outs[4 * i + 1][at], outs[4 * i + 2][at], outs[4 * i + 3][at] = _adam_step(
                    g, ins[3 * i][at], ins[3 * i + 1][at], ins[3 * i + 2][at])

    whole = lambda shape: pl.BlockSpec(shape, lambda i, c: (0,) * len(shape))
    shapes = [SMALL_PLACES[nm][2] for nm in names]
    outs = pl.pallas_call(
        body, name="small_update",
        grid_spec=pltpu.PrefetchScalarGridSpec(
            num_scalar_prefetch=1, grid=(1,),
            in_specs=[whole(tot.shape), pl.BlockSpec((tot.shape[0], D // 4), lambda i, c: (0, c[0])),
                      whole(tot_rel.shape)] + [whole(shp) for shp in shapes for _ in range(3)],
            out_specs=[whole(shp) for shp in shapes for _ in range(4)] + [whole((1, 1))]),
        out_shape=[SDS(shp, F32) for shp in shapes for _ in range(4)] + [SDS((1, 1), F32)],
    )(chip, tot, tot, tot_rel, *[a for nm in names for a in wmv[nm]])
    return {nm: tuple(outs[4 * i:4 * i + 4]) for i, nm in enumerate(names)}, outs[4 * n].reshape(())


def _pad_rows(a, rows):
    return jnp.concatenate([a, jnp.zeros((rows - a.shape[0], a.shape[1]), a.dtype)], axis=0)


def kernel(x, a_pre_norm, a_w_in, a_conv_w, a_w_out, a_post_norm, kv_norm, w_kv, rel_bias, b_pre_norm, b_w_in, b_sinks, b_w_out, b_post_norm, loss_target, m_a_pre_norm, m_a_w_in, m_a_conv_w, m_a_w_out, m_a_post_norm, m_kv_norm, m_w_kv, m_rel_bias, m_b_pre_norm, m_b_w_in, m_b_sinks, m_b_w_out, m_b_post_norm, v_a_pre_norm, v_a_w_in, v_a_conv_w, v_a_w_out, v_a_post_norm, v_kv_norm, v_w_kv, v_rel_bias, v_b_pre_norm, v_b_w_in, v_b_sinks, v_b_w_out, v_b_post_norm):
    seq = x.shape[1]
    xs = x.reshape(seq, D)
    tgt = loss_target.reshape(seq, D)
    chip = 2 * lax.axis_index("x") + lax.axis_index("y")
    core = lax.axis_index("c")
    tm = _tile(seq, 512)
    tmw = _tile(seq, 1024)

    shards = [a_w_in[0], a_w_out[0], w_kv, b_w_in[0], b_w_out[0]]
    small_w = _pad_rows(jnp.concatenate([a_pre_norm, a_conv_w[0], a_post_norm], axis=0), 8)
    *own_only, small_g = _prepare_weights(shards, small_w)
    where = jnp.stack([core, chip]).astype(jnp.int32)
    small_full = small_g.transpose(1, 0, 2).reshape(8, D)
    g_apre, conv_w, g_apost = small_full[0:1], _pad_rows(small_full[1:4], 8), small_full[4:5]
    g_kv = kv_norm.reshape(1, D)

    proj, n1, (win_g, wouta_g, wkv_g, wbin_g, woutb_g) = _a_in(where[1:2], xs, g_apre, own_only, tmw)
    wouta = wouta_g.reshape(D, D)
    wkv = wkv_g.reshape(D, 2 * KV_W)
    woutb = woutb_g.reshape(D, D)
    ya, oa, h1, conv = _a_mix(proj, xs, conv_w, wouta, g_apost, tm)
    kv, q, zb = _b_in(h1, g_kv, b_pre_norm, wkv, wbin_g, tmw)
    tab = _bias_table(rel_bias.T, b_sinks.reshape(N_HEADS))
    att, stats = _attn_fwd(q, kv, tab)
    dh2, dqz, datt, loss_acc, dg_bpost, dw_outb, dw_outb16 = _mid(att, zb, h1, tgt, woutb, b_post_norm, tm)

    dqz, dkv, dtab = _attn_bwd(q, kv, datt, stats, tab, dqz)
    dh1, doa, dg_b, dw_bin, dw_kv, dw_bin16, dw_kv16 = _b_bwd(dqz, dkv, h1, dh2, oa, wbin_g, wkv, g_kv, b_pre_norm,
                                                              g_apost, tm)
    by_chip = lambda a, cols: a.reshape(N_CHIPS, D // 4, cols)
    grads1 = [by_chip(dw_kv, 2 * KV_W), dw_bin, by_chip(dw_outb, D)]
    sent1 = [by_chip(dw_kv16, 2 * KV_W), dw_bin16, by_chip(dw_outb16, D)]
    names1 = ["w_kv", "b_w_in", "b_w_out"]
    dproj, dconv_w, dw_outa, dw_outa16, from_devices1 = _a_bwd(doa, ya, conv, proj, conv_w, wouta, tm, sent1)
    shards1 = [_add_devices(where, g, r, "add_devices_" + nm) for g, r, nm in zip(grads1, from_devices1, names1)]
    tmw2 = _tile(seq, 4096)
    win_lo, win_lo16, outa_got, g_wkv, g_wbin, g_woutb = _dw_in_half(
        n1, dproj, 0, tmw2, "dw_a_in_lo", to_devices=by_chip(dw_outa16, D), shards=shards1)
    win_hi, win_hi16, win_got = _dw_in_half(n1, dproj, 1, tmw2, "dw_a_in_hi", to_owners=win_lo16)
    nt = seq // tmw
    dn_first, win_got = _a_in_bwd_matmul(dproj, win_g, tmw, max(nt - max(nt // 4, 1), 1), win_hi16, win_got)
    grad_x, dg_apre = _a_in_bwd(dn_first, dproj, xs, dh1, win_g, g_apre, tm)
    shards2 = [_add_win(where, win_lo, win_hi, win_got, "add_devices_a_w_in"),
               _add_devices(where, by_chip(dw_outa, D), outa_got, "add_devices_a_w_out")]

    assert dconv_w.shape == (8, D)
    drel, smalls = _bias_fold(dtab, SMALL_PLACES["b_sinks"][1], [
        (0, dg_apre, 0, 1), (1, dg_b, 2, 1), (2, dg_b, 0, 1), (3, dg_b, 1, 1), (4, dg_bpost, 0, 1),
        (LOSS_ROW, loss_acc, 0, 1), (8, dconv_w, 0, 8)])
    (g_win, g_wouta), gathered = _share_and_gather(shards2, (smalls, drel))
    tot, tot_rel = _sum_smalls(gathered)

    big = {}
    for nm, g, w, m, v in [("a_w_in", g_win, a_w_in, m_a_w_in, v_a_w_in), ("a_w_out", g_wouta, a_w_out, m_a_w_out, v_a_w_out),
                           ("w_kv", g_wkv, w_kv, m_w_kv, v_w_kv), ("b_w_in", g_wbin, b_w_in, m_b_w_in, v_b_w_in),
                           ("b_w_out", g_woutb, b_w_out, m_b_w_out, v_b_w_out)]:
        shp = w.shape
        two = (shp[-2], shp[-1])
        big[nm] = tuple(a.reshape(shp) for a in _adamw(g, w.reshape(two), m.reshape(two), v.reshape(two), "adamw_" + nm))

    given = {"a_pre_norm": (a_pre_norm, m_a_pre_norm, v_a_pre_norm), "a_conv_w": (a_conv_w, m_a_conv_w, v_a_conv_w),
             "a_post_norm": (a_post_norm, m_a_post_norm, v_a_post_norm), "kv_norm": (kv_norm, m_kv_norm, v_kv_norm),
             "rel_bias": (rel_bias, m_rel_bias, v_rel_bias), "b_pre_norm": (b_pre_norm, m_b_pre_norm, v_b_pre_norm),
             "b_sinks": (b_sinks, m_b_sinks, v_b_sinks), "b_post_norm": (b_post_norm, m_b_post_norm, v_b_post_norm)}
    to_kernel = lambda nm, a: a.T if nm == "rel_bias" else a.reshape(SMALL_PLACES[nm][2])
    from_kernel = lambda nm, a: a.T if nm == "rel_bias" else a.reshape(given[nm][0].shape)
    small, loss = _small_update(where[1:2], tot, tot_rel, {nm: tuple(to_kernel(nm, a) for a in wmv)
                                            for nm, wmv in given.items()})
    order = ["a_pre_norm", "a_w_in", "a_conv_w", "a_w_out", "a_post_norm", "kv_norm", "w_kv", "rel_bias",
             "b_pre_norm", "b_w_in", "b_sinks", "b_w_out", "b_post_norm"]
    outs = []
    for which in range(4):
        for nm in order:
            outs.append(big[nm][which] if nm in big else from_kernel(nm, small[nm][which]))
    return (loss, grad_x.reshape(x.shape), *outs)
```

```python
import math

import jax
import jax.numpy as jnp
from jax import lax
from jax.experimental import pallas as pl
from jax.experimental.pallas import tpu as pltpu

F32 = jnp.float32
BF16 = jnp.bfloat16
MESH = pl.DeviceIdType.MESH
SDS = jax.ShapeDtypeStruct

D = 1024
HEAD_DIM = 64
N_HEADS = 16
N_KV = 2
GROUP = 8
KV_W = 128
BLK = 128
N_BUCKETS = 32
MAX_EXACT = 16
MAX_DISTANCE = 128
EPS = 1e-6
NEG_INF = -1e30
Q_SCALE = HEAD_DIM ** -0.5

ADAM_LR = 0.001
ADAM_B1 = 0.9
ADAM_B2 = 0.999
ADAM_EPS = 1e-08
ADAM_WD = 0.01
ADAM_STEP = 10

N_CHIPS = 4
N_DEV = 8
BIN_COLS = 2 * D // N_CHIPS
VMEM_LIMIT = 56 * 1024 * 1024
SMALL_ROWS = 16
LOSS_ROW = 6
SMALL_PLACES = {
    "a_pre_norm": ("quarter", 0, (1, D // 4)), "a_conv_w": ("quarter", 8, (3, 1, D // 4)),
    "a_post_norm": ("quarter", 1, (1, D // 4)), "kv_norm": ("rows", 2, (1, D)),
    "rel_bias": ("rel", 0, (N_HEADS, N_BUCKETS)), "b_pre_norm": ("rows", 3, (1, D)),
    "b_sinks": ("rows", 5, (1, N_HEADS)), "b_post_norm": ("rows", 4, (1, D)),
}


def _bucket_thresholds():
    def bucket(d):
        big = MAX_EXACT + int(math.log(d / MAX_EXACT) / math.log(MAX_DISTANCE / MAX_EXACT)
                              * (N_BUCKETS - MAX_EXACT))
        return d if d < MAX_EXACT else min(big, N_BUCKETS - 1)
    out = []
    for b in range(MAX_EXACT + 1, N_BUCKETS):
        out.append(min(d for d in range(MAX_EXACT, MAX_DISTANCE) if bucket(d) >= b))
    return tuple(out)


BUCKET_THRESHOLDS = _bucket_thresholds()


def _params(semantics=None, vmem=VMEM_LIMIT):
    return pltpu.CompilerParams(dimension_semantics=semantics, vmem_limit_bytes=vmem)


def _tile(n, pref):
    return pref if n >= 2 * pref else max(n // 2, 8)


def _rms_scale(v):
    return lax.rsqrt(jnp.mean(v * v, axis=-1, keepdims=True) + EPS)


def _nt(a, b):
    return lax.dot_general(a, b, (((1,), (1,)), ((), ())), preferred_element_type=F32)


def _tn(a, b):
    return lax.dot_general(a, b, (((0,), (0,)), ((), ())), preferred_element_type=F32)


def _nn(a, b):
    return jnp.dot(a, b, preferred_element_type=F32)


def _silu_parts(z):
    sg = jax.nn.sigmoid(z)
    return sg, z * sg


def _dsilu(z, sg):
    return sg * (1.0 + z * (1.0 - sg))


def _write_gradient(acc, out32, out16, stage, sem):
    whole = pltpu.make_async_copy(acc, out32, sem)
    whole.start()
    rows = stage.shape[0]
    for k in range(acc.shape[0] // rows):
        stage[...] = acc[rows * k:rows * (k + 1), :].astype(BF16)
        pltpu.sync_copy(stage, out16.at[pl.ds(rows * k, rows)])
    whole.wait()


def _acc_row(ref, row, val):
    ref[row:row + 1, :] += val


def _gather_copies(outs, splits, ici_send, ici_recv, d2d_send, d2d_recv):
    x, y, c = lax.axis_index("x"), lax.axis_index("y"), lax.axis_index("c")
    k = 2 * x + y
    sibling = (x, y, 1 - c)

    def part(o_ref, chip, core, split):
        if not split:
            return o_ref.at[chip]
        h = o_ref.shape[1] // 2
        return o_ref.at[chip, pl.ds(pl.multiple_of(core * h, 16), h)]

    def remote(ref, a, j, sems, to):
        return pltpu.make_async_remote_copy(src_ref=ref, dst_ref=ref, send_sem=sems[0].at[3 * a + j],
                                            recv_sem=sems[1].at[3 * a + j], device_id=to, device_id_type=MESH)

    copies = []
    for a, (o_ref, split) in enumerate(zip(outs, splits)):
        for j, (px, py) in enumerate([(x, 1 - y), (1 - x, y), (1 - x, 1 - y)]):
            kj = 2 * px + py
            ici, d2d = (ici_send, ici_recv), (d2d_send, d2d_recv)
            copies.append((remote(part(o_ref, k, c, split), a, j, ici, (px, py, c)),
                           remote(part(o_ref, kj, c, split), a, j, ici, (px, py, c)),
                           remote(part(o_ref, kj, c, split), a, j, d2d, sibling) if split else None,
                           remote(part(o_ref, kj, 1 - c, split), a, j, d2d, sibling) if split else None))
    return copies


def _gather_sems(n):
    return [pltpu.SemaphoreType.DMA((3 * n,)) for _ in range(4)]


def _prepare_weights(shards, small):
    n = len(shards)

    def body(*refs):
        ins, small_in = refs[:n], refs[n]
        outs, small_out = refs[n + 1:2 * n + 1], refs[2 * n + 1]
        stages, put_sem = refs[2 * n + 2:3 * n + 2], refs[3 * n + 2]
        sems = refs[3 * n + 3:]
        k = 2 * lax.axis_index("x") + lax.axis_index("y")
        puts = []
        for a, (i_ref, stage, o_ref) in enumerate(zip(ins, stages, outs)):
            stage[...] = i_ref[...].astype(BF16)
            puts.append(pltpu.make_async_copy(stage, o_ref.at[k], put_sem.at[a]))
            puts[-1].start()
        small_out[k] = small_in[...]
        copies = _gather_copies([small_out], [False], *sems)
        for send, _, _, _ in copies:
            send.start()
        for _, arrival, _, _ in copies:
            arrival.wait_recv()
        for send, _, _, _ in copies:
            send.wait_send()
        for put in puts:
            put.wait()

    vm = pl.BlockSpec(memory_space=pltpu.VMEM)
    anyspace = pl.BlockSpec(memory_space=pl.ANY)
    out_shape = [SDS((N_CHIPS,) + s.shape, BF16) for s in shards] + [SDS((N_CHIPS,) + small.shape, F32)]
    return pl.pallas_call(
        body, name="prepare_weights", out_shape=out_shape,
        in_specs=[vm] * (n + 1), out_specs=[anyspace] * n + [vm],
        scratch_shapes=[pltpu.VMEM(s.shape, BF16) for s in shards] + [pltpu.SemaphoreType.DMA((n,))] + _gather_sems(1),
        compiler_params=pltpu.CompilerParams(vmem_limit_bytes=VMEM_LIMIT),
    )(*shards, small)


def _a_in(chip, x, g_pre, weights, tm):
    s = x.shape[0]
    nt = s // tm
    n = len(weights)

    def body(chip_ref, x_ref, g_ref, *refs):
        proj_ref, n1_ref = refs[n:n + 2]
        gathered = refs[n + 2:2 * n + 2]
        wbuf, n1_all, fetch_sem = refs[2 * n + 2:2 * n + 5]
        sems = refs[2 * n + 5:]
        jj, i = pl.program_id(0), pl.program_id(1)
        copies = _gather_copies(gathered, [True] * n, *sems)

        def fetch(rel):
            slot = jnp.bitwise_xor(chip_ref[0], rel)
            return pltpu.make_async_copy(gathered[0].at[slot], wbuf.at[rel % 2], fetch_sem.at[rel % 2])

        @pl.when((jj == 0) & (i == 0))
        def _():
            fetch(0).start()
            copies[0][0].start()
            copies[1][0].start()
            fetch(0).wait()

        for rel in (1, 2, 3):
            @pl.when((jj == rel) & (i == 0))
            def _():
                fetch(rel).wait()

        @pl.when(jj == 0)
        def _():
            xv = x_ref[...]
            n1 = (xv * _rms_scale(xv) * g_ref[...]).astype(BF16)
            n1_ref[...] = n1
            n1_all[i] = n1
        proj_ref[...] = _nn(n1_all[i], wbuf[jj % 2]).astype(BF16)

        for rel in (1, 2, 3):
            @pl.when((jj == rel - 1) & (i == max(nt - 3, 0)))
            def _():
                _, arrival, forward, _ = copies[rel - 1]
                arrival.wait_recv()
                forward.start()
                if rel == 1:
                    for send, _, _, _ in copies[2:]:
                        send.start()

            @pl.when((jj == rel - 1) & (i == max(nt - 2, 0)))
            def _():
                copies[rel - 1][3].wait_recv()
                fetch(rel).start()

        @pl.when((jj == 3) & (i == max(nt - 2, 0)))
        def _():
            for _, arrival, forward, _ in copies[3:]:
                arrival.wait_recv()
                forward.start()

        @pl.when((jj == 3) & (i == nt - 1))
        def _():
            for _, _, _, forwarded in copies[3:]:
                forwarded.wait_recv()
            for send, _, forward, _ in copies:
                forward.wait_send()
                send.wait_send()

    anyspace = pl.BlockSpec(memory_space=pl.ANY)
    proj, n1, *gathered = pl.pallas_call(
        body, name="a_in",
        grid_spec=pltpu.PrefetchScalarGridSpec(
            num_scalar_prefetch=1, grid=(4, nt),
            in_specs=[pl.BlockSpec((tm, D), lambda jj, i, c: (jnp.where(jj == 0, i, nt - 1), 0)),
                      pl.BlockSpec((1, D), lambda jj, i, c: (0, 0))] + [anyspace] * n,
            out_specs=[pl.BlockSpec((tm, D), lambda jj, i, c: (i, jnp.bitwise_xor(c[0], jj))),
                       pl.BlockSpec((tm, D), lambda jj, i, c: (jnp.where(jj == 0, i, nt - 1), 0))] + [anyspace] * n,
            scratch_shapes=[pltpu.VMEM((2, D, D), BF16), pltpu.VMEM((nt, tm, D), BF16),
                            pltpu.SemaphoreType.DMA((2,))] + _gather_sems(n)),
        out_shape=[SDS((s, 4 * D), BF16), SDS((s, D), BF16)] + [SDS(w.shape, w.dtype) for w in weights],
        input_output_aliases={3 + a: 2 + a for a in range(n)},
        compiler_params=_params(("arbitrary", "arbitrary")),
    )(chip, x, g_pre, *weights)
    return proj, n1, gathered


def _shift_rows(v, last, second_last, rows):
    v1 = jnp.where(rows >= 1, pltpu.roll(v, 1, 0), last)
    v2 = jnp.where(rows >= 2, pltpu.roll(v, 2, 0), jnp.where(rows == 1, last, second_last))
    return v1, v2


def _a_mix(proj, x, conv_w, w_out, g_post, tm):
    s = x.shape[0]

    def body(proj_ref, x_ref, cw_ref, w_ref, g_ref, ya_ref, oa_ref, h1_ref, conv_ref, carry):
        @pl.when(pl.program_id(0) == 0)
        def _():
            carry[...] = jnp.zeros_like(carry)
        v = proj_ref[:, D:2 * D].astype(F32) * proj_ref[:, 2 * D:3 * D].astype(F32)
        rows = lax.broadcasted_iota(jnp.int32, (tm, D), 0)
        before = carry[...]
        v1, v2 = _shift_rows(v, before[7:8, :], before[6:7, :], rows)
        carry[...] = v[tm - 8:tm, :]
        conv = cw_ref[0:1, :] * v2 + cw_ref[1:2, :] * v1 + cw_ref[2:3, :] * v
        conv_ref[...] = conv.astype(BF16)
        _, sz = _silu_parts(proj_ref[:, 3 * D:4 * D].astype(F32))
        ya = (proj_ref[:, 0:D].astype(F32) * conv * sz).astype(BF16)
        ya_ref[...] = ya
        oa = _nn(ya, w_ref[...])
        oa_ref[...] = oa.astype(BF16)
        h1_ref[...] = x_ref[...] + oa * _rms_scale(oa) * g_ref[...]

    row = lambda i: (i, 0)
    fix = lambda i: (0, 0)
    return pl.pallas_call(
        body, name="a_mix", grid=(s // tm,),
        in_specs=[pl.BlockSpec((tm, 4 * D), row), pl.BlockSpec((tm, D), row), pl.BlockSpec((8, D), fix),
                  pl.BlockSpec((D, D), fix), pl.BlockSpec((1, D), fix)],
        out_specs=[pl.BlockSpec((tm, D), row)] * 4,
        out_shape=[SDS((s, D), BF16), SDS((s, D), BF16), SDS((s, D), F32), SDS((s, D), BF16)],
        scratch_shapes=[pltpu.VMEM((8, D), F32)],
        compiler_params=_params(("arbitrary",)),
    )(proj, x, conv_w, w_out, g_post)


def _b_in(h1, g_kv, g_pre, w_kv, wbin_g, tm):
    s = h1.shape[0]

    def body(h_ref, gk_ref, gb_ref, wkv_ref, wb_ref, kv_ref, q_ref, z_ref):
        h = h_ref[...]
        hh = h * _rms_scale(h)
        nk = (hh * gk_ref[...]).astype(BF16)
        nb = (hh * gb_ref[...]).astype(BF16)
        kv_ref[...] = _nn(nk, wkv_ref[...]).astype(BF16)
        for j in range(2):
            q_ref[:, BIN_COLS * j:BIN_COLS * (j + 1)] = (_nn(nb, wb_ref[j]) * Q_SCALE).astype(BF16)
            z_ref[:, BIN_COLS * j:BIN_COLS * (j + 1)] = _nn(nb, wb_ref[2 + j]).astype(BF16)

    row = lambda i: (i, 0)
    fix = lambda i: (0, 0)
    return pl.pallas_call(
        body, name="b_in", grid=(s // tm,),
        in_specs=[pl.BlockSpec((tm, D), row), pl.BlockSpec((1, D), fix), pl.BlockSpec((1, D), fix),
                  pl.BlockSpec((D, 2 * KV_W), fix), pl.BlockSpec((N_CHIPS, D, BIN_COLS), lambda i: (0, 0, 0))],
        out_specs=[pl.BlockSpec((tm, 2 * KV_W), row), pl.BlockSpec((tm, D), row), pl.BlockSpec((tm, D), row)],
        out_shape=[SDS((s, 2 * KV_W), BF16), SDS((s, D), BF16), SDS((s, D), BF16)],
        compiler_params=_params(("parallel",)),
    )(h1, g_kv, g_pre, w_kv, wbin_g)


def _buckets(dist):
    bucket = jnp.where(dist < MAX_EXACT, dist, MAX_EXACT)
    for t in BUCKET_THRESHOLDS:
        bucket = bucket + jnp.where(dist >= t, 1, 0)
    return bucket


def _head_place(h):
    kh, j, e = h // GROUP, (h % GROUP) // 2, h % 2
    return kh, slice(BLK * j, BLK * (j + 1)), slice(2 * BLK * e, 2 * BLK * (e + 1))


def _bias_table(rel_bias, sinks):
    def body(rb_ref, sink_ref, tab_ref):
        along = lax.broadcasted_iota(jnp.int32, (8, BLK), 1)
        row8 = lax.broadcasted_iota(jnp.int32, (8, BLK), 0)
        bucket = _buckets(jnp.where(along == 0, 0, BLK - along))
        query = lax.broadcasted_iota(jnp.int32, (BLK, BLK), 0)
        col = lax.broadcasted_iota(jnp.int32, (BLK, BLK), 1)
        for h in range(N_HEADS):
            by_dist = jnp.zeros((8, BLK), F32)
            for b in range(N_BUCKETS):
                by_dist = jnp.where(bucket == b, rb_ref[h, b], by_dist)
            for digit in range(3):
                by_dist = jnp.where((row8 >> digit) & 1 == 1, pltpu.roll(by_dist, 1 << digit, 1), by_dist)
            band = jnp.concatenate([by_dist] + [pltpu.roll(by_dist, 8 * g, 1) for g in range(1, BLK // 8)], axis=0)
            cur = jnp.where(col <= query, band, NEG_INF)
            kh, rows, cols = _head_place(h)
            prev_cols, cur_cols = slice(cols.start, cols.start + BLK), slice(cols.start + BLK, cols.stop)
            tab_ref[1, kh, rows, prev_cols] = jnp.where(col == 0, sink_ref[h], jnp.where(col > query, band, NEG_INF))
            tab_ref[1, kh, rows, cur_cols] = cur
            tab_ref[0, kh, rows, prev_cols] = jnp.where(col == 0, sink_ref[h], NEG_INF)
            tab_ref[0, kh, rows, cur_cols] = cur

    return pl.pallas_call(
        body, name="bias_table", out_shape=SDS((2, N_KV, 4 * BLK, 4 * BLK), F32),
        in_specs=[pl.BlockSpec(memory_space=pltpu.SMEM), pl.BlockSpec(memory_space=pltpu.SMEM)],
        out_specs=pl.BlockSpec(memory_space=pltpu.VMEM),
    )(rel_bias, sinks)


def _bias_fold(dtab, sink_row, pieces):
    def body(dtab_ref, *refs):
        piece_refs, (out_ref, smalls_ref) = refs[:len(pieces)], refs[len(pieces):]
        lane = lax.broadcasted_iota(jnp.int32, (N_HEADS, BLK), 1)
        bucket = _buckets(lane)
        col = lax.broadcasted_iota(jnp.int32, (BLK, BLK), 1)
        row8 = lax.broadcasted_iota(jnp.int32, (8, 128), 0)
        lane8 = lax.broadcasted_iota(jnp.int32, (8, 128), 1)
        by_dist = jnp.zeros((BLK, 128), F32)
        dsink = jnp.zeros((8, 128), F32)
        for h in range(N_HEADS):
            kh, rows, cols = _head_place(h)
            dt = dtab_ref[kh, rows, cols]
            band = jnp.where(col == 0, 0.0, dt[:, 0:BLK]) + dt[:, BLK:2 * BLK]
            for digit in range(BLK.bit_length() - 1):
                band = jnp.where((col >> digit) & 1 == 1, pltpu.roll(band, BLK - (1 << digit), 0), band)
            by_dist = jnp.where(col == h, jnp.sum(band, axis=1, keepdims=True), by_dist)
            dsink = dsink + jnp.where((row8 == 0) & (lane8 == h), jnp.sum(dt[:, 0:1]), 0.0)
        by_head = by_dist.T[0:N_HEADS]
        folded = jnp.zeros((N_HEADS, 128), F32)
        for b in range(N_BUCKETS):
            folded = jnp.where(lane == b, jnp.sum(jnp.where(bucket == b, by_head, 0.0), axis=1, keepdims=True), folded)
        out_ref[...] = folded
        smalls_ref[...] = jnp.zeros((SMALL_ROWS, D), F32)
        smalls_ref[sink_row:sink_row + 1, 0:128] = dsink[0:1]
        for (to_row, a, first, rows), ref in zip(pieces, piece_refs):
            smalls_ref[to_row:to_row + rows, 0:a.shape[1]] = ref[first:first + rows, :]

    vm = pl.BlockSpec(memory_space=pltpu.VMEM)
    return pl.pallas_call(
        body, name="bias_fold", out_shape=[SDS((N_HEADS, 128), F32), SDS((SMALL_ROWS, D), F32)],
        in_specs=[vm] * (1 + len(pieces)), out_specs=[vm, vm],
    )(dtab, *[a for _, a, _, _ in pieces])


def _pair_operands(prev, cur):
    t = jnp.concatenate([prev, cur], axis=0).astype(F32)
    t = jnp.where(lax.broadcasted_iota(jnp.int32, t.shape, 0) == 0, 0.0, t)
    tr = pltpu.roll(t, HEAD_DIM, 1)
    lo = lax.broadcasted_iota(jnp.int32, t.shape, 1) < HEAD_DIM
    zero = jnp.zeros_like(t)
    head0 = jnp.concatenate([jnp.where(lo, t, zero), jnp.where(lo, zero, tr)], axis=0).astype(BF16)
    head1 = jnp.concatenate([jnp.where(lo, tr, zero), jnp.where(lo, zero, t)], axis=0).astype(BF16)
    return head0, head1


def _pair_fold(d0, d1):
    lo = lax.broadcasted_iota(jnp.int32, (2 * BLK, KV_W), 1) < HEAD_DIM
    zero = jnp.zeros((2 * BLK, KV_W), F32)
    g0 = jnp.where(lo, d0[0:256], zero) + pltpu.roll(jnp.where(lo, zero, d0[256:512]), HEAD_DIM, 1)
    g1 = pltpu.roll(jnp.where(lo, d1[0:256], zero), HEAD_DIM, 1) + jnp.where(lo, zero, d1[256:512])
    return jnp.where(lax.broadcasted_iota(jnp.int32, (2 * BLK, KV_W), 0) == 0, 0.0, g0 + g1)


def _stack_pairs(ref, kh):
    return jnp.concatenate([ref[:, 128 * (4 * kh + j):128 * (4 * kh + j + 1)] for j in range(4)], axis=0)


def _table_spec():
    return pl.BlockSpec((1, N_KV, 4 * BLK, 4 * BLK), lambda n: (jnp.minimum(n, 1), 0, 0, 0))


def _attn_fwd(q, kv, tab):
    s = q.shape[0]

    def body(q_ref, kp_ref, kc_ref, vp_ref, vc_ref, tab_ref, att_ref, stats_ref):
        k2 = _pair_operands(kp_ref[...], kc_ref[...])
        v2 = _pair_operands(vp_ref[...], vc_ref[...])
        lane = lax.broadcasted_iota(jnp.int32, (BLK, 128), 1)
        stats = jnp.zeros((BLK, 128), F32)
        for kh in range(N_KV):
            sc = _nt(_stack_pairs(q_ref, kh), k2[kh])
            ps = []
            for e in range(2):
                lg = sc[:, 256 * e:256 * (e + 1)] + tab_ref[0, kh, :, 256 * e:256 * (e + 1)]
                m = jnp.max(lg, axis=-1, keepdims=True)
                ex = jnp.exp(lg - m)
                den = jnp.sum(ex, axis=-1, keepdims=True)
                ps.append(ex * (1.0 / den))
                lse = m + jnp.log(den)
                for j in range(4):
                    stats = jnp.where(lane == GROUP * kh + 2 * j + e, lse[BLK * j:BLK * (j + 1)], stats)
            out = _nn(jnp.concatenate(ps, axis=1).astype(BF16), v2[kh])
            for j in range(4):
                att_ref[:, 128 * (4 * kh + j):128 * (4 * kh + j + 1)] = out[BLK * j:BLK * (j + 1)].astype(BF16)
        stats_ref[...] = stats

    cur = lambda n: (n, 0)
    prev = lambda n: (jnp.maximum(n - 1, 0), 0)
    return pl.pallas_call(
        body, name="attn_fwd", grid=(s // BLK,),
        in_specs=[pl.BlockSpec((BLK, D), cur),
                  pl.BlockSpec((BLK, KV_W), prev), pl.BlockSpec((BLK, KV_W), cur),
                  pl.BlockSpec((BLK, KV_W), lambda n: (jnp.maximum(n - 1, 0), 1)),
                  pl.BlockSpec((BLK, KV_W), lambda n: (n, 1)), _table_spec()],
        out_specs=[pl.BlockSpec((BLK, D), cur), pl.BlockSpec((BLK, 128), cur)],
        out_shape=[SDS((s, D), BF16), SDS((s, 128), F32)],
        compiler_params=_params(("parallel",)),
    )(q, kv, kv, kv, kv, tab)


def _mid(att, zb, h1, tgt, w_out, g_post, tm):
    s = att.shape[0]
    nt = s // tm

    def body(att_ref, z_ref, h1_ref, t_ref, w_ref, g_ref,
             dh_ref, dqz_ref, datt_ref, loss_ref, dg_ref, dw_ref, dw16_ref, dw_acc, stage, put_sem):
        @pl.when(pl.program_id(0) == 0)
        def _():
            loss_ref[...] = jnp.zeros_like(loss_ref)
            dg_ref[...] = jnp.zeros_like(dg_ref)
            dw_acc[...] = jnp.zeros_like(dw_acc)
        att = att_ref[...].astype(F32)
        z = z_ref[...].astype(F32)
        sg, sz = _silu_parts(z)
        ob = (att * sz).astype(BF16)
        y2 = _nn(ob, w_ref[...])
        r2 = _rms_scale(y2)
        yh = y2 * r2
        g = g_ref[...]
        err = (h1_ref[...] + yh * g) - t_ref[...]
        loss_ref[...] += jnp.sum(jnp.sum(err * err, axis=-1, keepdims=True) / D)
        dh = err / D
        dh_ref[...] = dh
        _acc_row(dg_ref, 0, jnp.sum(dh * yh, axis=0, keepdims=True))
        dyh = dh * g
        dy = (r2 * (dyh - yh * jnp.mean(dyh * yh, axis=-1, keepdims=True))).astype(BF16)
        dw_acc[...] += _tn(ob, dy)
        dob = _nt(dy, w_ref[...])
        datt_ref[...] = (dob * sz).astype(BF16)
        dqz_ref[...] = (dob * att * _dsilu(z, sg)).astype(BF16)

        @pl.when(pl.program_id(0) == nt - 1)
        def _():
            _write_gradient(dw_acc, dw_ref, dw16_ref, stage, put_sem)

    row = lambda i: (i, 0)
    fix = lambda i: (0, 0)
    anyspace = pl.BlockSpec(memory_space=pl.ANY)
    return pl.pallas_call(
        body, name="mid", grid=(nt,),
        in_specs=[pl.BlockSpec((tm, D), row)] * 4 + [pl.BlockSpec((D, D), fix), pl.BlockSpec((1, D), fix)],
        out_specs=[pl.BlockSpec((tm, D), row), pl.BlockSpec((tm, D), lambda i: (i, 1)), pl.BlockSpec((tm, D), row),
                   pl.BlockSpec((8, 128), fix), pl.BlockSpec((8, D), fix), anyspace, anyspace],
        out_shape=[SDS((s, D), F32), SDS((s, 2 * D), BF16), SDS((s, D), BF16), SDS((8, 128), F32),
                   SDS((8, D), F32), SDS((D, D), F32), SDS((D, D), BF16)],
        scratch_shapes=[pltpu.VMEM((D, D), F32), pltpu.VMEM((D // 4, D), BF16), pltpu.SemaphoreType.DMA],
        compiler_params=_params(("arbitrary",)),
    )(att, zb, h1, tgt, w_out, g_post)


def _attn_bwd(q, kv, datt, stats, tab, dqz):
    s = q.shape[0]
    nb = s // BLK

    def body(q_ref, kp_ref, kc_ref, vp_ref, vc_ref, da_ref, st_ref, tab_ref, dqz_in,
             dq_ref, dkv_ref, dtab_ref, dk_carry, dv_carry):
        del dqz_in
        n = pl.program_id(0)

        @pl.when(n == 0)
        def _():
            dtab_ref[...] = jnp.zeros_like(dtab_ref)
            dk_carry[...] = jnp.zeros_like(dk_carry)
            dv_carry[...] = jnp.zeros_like(dv_carry)

        @pl.when(n < nb)
        def _():
            k2 = _pair_operands(kp_ref[...], kc_ref[...])
            v2 = _pair_operands(vp_ref[...], vc_ref[...])
            lane = lax.broadcasted_iota(jnp.int32, (BLK, 128), 1)
            stats = st_ref[...]
            dk2, dv2 = [], []
            for kh in range(N_KV):
                qs = _stack_pairs(q_ref, kh)
                das = _stack_pairs(da_ref, kh)
                sc = _nt(qs, k2[kh])
                dp = _nt(das, v2[kh])
                ps, dss = [], []
                for e in range(2):
                    heads = [GROUP * kh + 2 * j + e for j in range(4)]
                    lse = jnp.concatenate([jnp.sum(jnp.where(lane == h, stats, 0.0), axis=-1, keepdims=True)
                                           for h in heads], axis=0)
                    cols = slice(256 * e, 256 * (e + 1))
                    p = jnp.exp(sc[:, cols] + tab_ref[0, kh, :, cols] - lse)
                    delta = jnp.sum(p * dp[:, cols], axis=-1, keepdims=True)
                    ds = p * (dp[:, cols] - delta)
                    dtab_ref[kh, :, cols] += ds
                    ps.append(p)
                    dss.append(ds)
                p2 = jnp.concatenate(ps, axis=1).astype(BF16)
                ds2 = jnp.concatenate(dss, axis=1).astype(BF16)
                dq = _nn(ds2, k2[kh]) * Q_SCALE
                for j in range(4):
                    dq_ref[:, 128 * (4 * kh + j):128 * (4 * kh + j + 1)] = dq[BLK * j:BLK * (j + 1)].astype(BF16)
                dk2.append(_tn(ds2, qs))
                dv2.append(_tn(p2, das))
            dkk = _pair_fold(dk2[0], dk2[1])
            dvv = _pair_fold(dv2[0], dv2[1])
            dkv_ref[:, 0:KV_W] = (dk_carry[...] + dkk[0:BLK]).astype(BF16)
            dkv_ref[:, KV_W:2 * KV_W] = (dv_carry[...] + dvv[0:BLK]).astype(BF16)
            dk_carry[...] = dkk[BLK:2 * BLK]
            dv_carry[...] = dvv[BLK:2 * BLK]

        @pl.when(n == nb)
        def _():
            dkv_ref[:, 0:KV_W] = dk_carry[...].astype(BF16)
            dkv_ref[:, KV_W:2 * KV_W] = dv_carry[...].astype(BF16)

    cur = lambda n: (jnp.minimum(n, nb - 1), 0)
    prev = lambda n: (jnp.clip(n - 1, 0, nb - 1), 0)
    return pl.pallas_call(
        body, name="attn_bwd", grid=(nb + 1,),
        in_specs=[pl.BlockSpec((BLK, D), cur),
                  pl.BlockSpec((BLK, KV_W), prev), pl.BlockSpec((BLK, KV_W), cur),
                  pl.BlockSpec((BLK, KV_W), lambda n: (jnp.clip(n - 1, 0, nb - 1), 1)),
                  pl.BlockSpec((BLK, KV_W), lambda n: (jnp.minimum(n, nb - 1), 1)),
                  pl.BlockSpec((BLK, D), cur), pl.BlockSpec((BLK, 128), cur), _table_spec(),
                  pl.BlockSpec(memory_space=pl.ANY)],
        out_specs=[pl.BlockSpec((BLK, D), cur), pl.BlockSpec((BLK, 2 * KV_W), prev),
                   pl.BlockSpec((N_KV, 4 * BLK, 4 * BLK), lambda n: (0, 0, 0))],
        out_shape=[SDS((s, 2 * D), BF16), SDS((s, 2 * KV_W), BF16), SDS((N_KV, 4 * BLK, 4 * BLK), F32)],
        scratch_shapes=[pltpu.VMEM((BLK, KV_W), F32), pltpu.VMEM((BLK, KV_W), F32)],
        input_output_aliases={8: 0},
        compiler_params=_params(("arbitrary",)),
    )(q, kv, kv, kv, kv, datt, stats, tab, dqz)


def _b_bwd(dqz, dkv, h1, dh2, oa, wbin_g, w_kv, g_kv, g_pre, g_apost, tm):
    s = h1.shape[0]
    nt = s // tm

    def body(dqz_ref, dkv_ref, h_ref, dh2_ref, oa_ref, wb_ref, wkv_ref, gk_ref, gb_ref, ga_ref,
             dh1_ref, doa_ref, dg_ref, dwb_ref, dwkv_ref, dwb16_ref, dwkv16_ref, wcat, dwb_acc, dwkv_acc, put_sem):
        @pl.when(pl.program_id(0) == 0)
        def _():
            dg_ref[...] = jnp.zeros_like(dg_ref)
            dwb_acc[...] = jnp.zeros_like(dwb_acc)
            dwkv_acc[...] = jnp.zeros_like(dwkv_acc)
            for j in range(N_CHIPS):
                pltpu.sync_copy(wb_ref.at[j], wcat.at[:, pl.ds(BIN_COLS * j, BIN_COLS)])
        dnb = _nt(dqz_ref[...], wcat[...])
        dnk = _nt(dkv_ref[...], wkv_ref[...])
        h = h_ref[...]
        r = _rms_scale(h)
        hh = h * r
        dwb_acc[...] += _tn((hh * gb_ref[...]).astype(BF16), dqz_ref[...])
        dwkv_acc[...] += _tn((hh * gk_ref[...]).astype(BF16), dkv_ref[...])
        _acc_row(dg_ref, 0, jnp.sum(dnk * hh, axis=0, keepdims=True))
        _acc_row(dg_ref, 1, jnp.sum(dnb * hh, axis=0, keepdims=True))
        dhh = dnb * gb_ref[...] + dnk * gk_ref[...]
        dh1 = dh2_ref[...] + r * (dhh - hh * jnp.mean(dhh * hh, axis=-1, keepdims=True))
        dh1_ref[...] = dh1
        oa = oa_ref[...].astype(F32)
        ra = _rms_scale(oa)
        oh = oa * ra
        _acc_row(dg_ref, 2, jnp.sum(dh1 * oh, axis=0, keepdims=True))
        doh = dh1 * ga_ref[...]
        doa_ref[...] = (ra * (doh - oh * jnp.mean(doh * oh, axis=-1, keepdims=True))).astype(BF16)

        @pl.when(pl.program_id(0) == nt - 1)
        def _():
            wcat[...] = dwb_acc[...].astype(BF16)
            puts = [pltpu.make_async_copy(dwkv_acc, dwkv_ref, put_sem.at[2 * N_CHIPS])]
            for j in range(N_CHIPS):
                cols = pl.ds(BIN_COLS * j, BIN_COLS)
                puts.append(pltpu.make_async_copy(dwb_acc.at[:, cols], dwb_ref.at[j], put_sem.at[2 * j]))
                puts.append(pltpu.make_async_copy(wcat.at[:, cols], dwb16_ref.at[j], put_sem.at[2 * j + 1]))
            for put in puts:
                put.start()
            for put in puts:
                put.wait()
            wcat[:, 0:2 * KV_W] = dwkv_acc[...].astype(BF16)
            pltpu.sync_copy(wcat.at[:, pl.ds(0, 2 * KV_W)], dwkv16_ref)

    row = lambda i: (i, 0)
    fix = lambda i: (0, 0)
    anyspace = pl.BlockSpec(memory_space=pl.ANY)
    return pl.pallas_call(
        body, name="b_bwd", grid=(nt,),
        in_specs=[pl.BlockSpec((tm, 2 * D), row), pl.BlockSpec((tm, 2 * KV_W), row), pl.BlockSpec((tm, D), row),
                  pl.BlockSpec((tm, D), row), pl.BlockSpec((tm, D), row), anyspace, pl.BlockSpec((D, 2 * KV_W), fix),
                  pl.BlockSpec((1, D), fix), pl.BlockSpec((1, D), fix), pl.BlockSpec((1, D), fix)],
        out_specs=[pl.BlockSpec((tm, D), row), pl.BlockSpec((tm, D), row), pl.BlockSpec((8, D), fix)] + [anyspace] * 4,
        out_shape=[SDS((s, D), F32), SDS((s, D), BF16), SDS((8, D), F32), SDS((N_CHIPS, D, BIN_COLS), F32),
                   SDS((D, 2 * KV_W), F32), SDS((N_CHIPS, D, BIN_COLS), BF16), SDS((D, 2 * KV_W), BF16)],
        scratch_shapes=[pltpu.VMEM((D, 2 * D), BF16), pltpu.VMEM((D, 2 * D), F32), pltpu.VMEM((D, 2 * KV_W), F32),
                        pltpu.SemaphoreType.DMA((2 * N_CHIPS + 1,))],
        compiler_params=_params(("arbitrary",)),
    )(dqz, dkv, h1, dh2, oa, wbin_g, w_kv, g_kv, g_pre, g_apost)


def _to_owner_core(pieces, r, send, recv, core, action):
    x, y, c = lax.axis_index("x"), lax.axis_index("y"), lax.axis_index("c")
    for kp in range(N_CHIPS):
        px, py = kp >> 1, kp & 1
        rel = 4 * (x + px - 2 * x * px) + 2 * (y + py - 2 * y * py) + (c + core - 2 * c * core)

        @pl.when(rel != 0)
        def _():
            cp = pltpu.make_async_remote_copy(src_ref=pieces.at[kp], dst_ref=r.at[rel - 1], send_sem=send.at[kp],
                                              recv_sem=recv.at[rel - 1], device_id=(px, py, core), device_id_type=MESH)
            if action == "start":
                cp.start()
            else:
                cp.wait_send()
    if action == "wait":
        @pl.when(c == core)
        def _():
            for rel in range(1, N_DEV):
                pltpu.make_async_remote_copy(src_ref=pieces.at[0], dst_ref=r.at[rel - 1], send_sem=send.at[0],
                                             recv_sem=recv.at[rel - 1], device_id=(x, y, c),
                                             device_id_type=MESH).wait_recv()


def _owner_core_sems():
    return [pltpu.SemaphoreType.DMA((N_CHIPS,)), pltpu.SemaphoreType.DMA((N_DEV - 1,))]


def _device_exchange(grads, recvs, send, recv):
    x, y, c = lax.axis_index("x"), lax.axis_index("y"), lax.axis_index("c")
    copies = []
    for a, (g, r) in enumerate(zip(grads, recvs)):
        h = g.shape[1] // 2
        for rel in range(1, N_DEV):
            fx, fy, fc = rel >> 2, (rel >> 1) & 1, rel & 1
            px, py, pc = x + fx - 2 * x * fx, y + fy - 2 * y * fy, c + fc - 2 * c * fc
            sem = (N_DEV - 1) * a + rel - 1
            copies.append(pltpu.make_async_remote_copy(
                src_ref=g.at[2 * px + py, pl.ds(pl.multiple_of(pc * h, 16), h)], dst_ref=r.at[rel - 1],
                send_sem=send.at[sem], recv_sem=recv.at[sem], device_id=(px, py, pc), device_id_type=MESH))
    return copies


def _device_exchange_specs(grads):
    anyspace = pl.BlockSpec(memory_space=pl.ANY)
    n = len(grads)
    count = (N_DEV - 1) * n
    return ([anyspace] * n, [anyspace] * n,
            [SDS((N_DEV - 1, g.shape[1] // 2, g.shape[2]), g.dtype) for g in grads],
            [pltpu.SemaphoreType.DMA((count,)), pltpu.SemaphoreType.DMA((count,))])


def _a_bwd(doa, ya, conv, proj, conv_w, w_out, tm, parts):
    s = doa.shape[0]
    nt = s // tm
    n = len(parts)
    ex_in, ex_out, ex_shape, ex_sems = _device_exchange_specs(parts)

    def body(*refs):
        doa_ref, ya_ref, conv_ref, proj_ref, cw_ref, w_ref = refs[:6]
        part_refs = refs[6:6 + n]
        dproj_ref, dcw_ref, dw_ref, dw16_ref = refs[6 + n:10 + n]
        recv_refs = refs[10 + n:10 + 2 * n]
        carry, dw_acc, stage, put_sem, send, recv = refs[10 + 2 * n:]
        i = pl.program_id(0)

        @pl.when(i == 0)
        def _():
            dcw_ref[...] = jnp.zeros_like(dcw_ref)
            carry[...] = jnp.zeros_like(carry)
            dw_acc[...] = jnp.zeros_like(dw_acc)
            for cp in _device_exchange(part_refs, recv_refs, send, recv):
                cp.start()
        dya = _nt(doa_ref[...], w_ref[...])
        dw_acc[...] += _tn(ya_ref[...], doa_ref[...])
        bg = proj_ref[:, 0:D].astype(F32)
        cg = proj_ref[:, D:2 * D].astype(F32)
        u = proj_ref[:, 2 * D:3 * D].astype(F32)
        z = proj_ref[:, 3 * D:4 * D].astype(F32)
        v = cg * u
        rows = lax.broadcasted_iota(jnp.int32, (tm, D), 0)
        conv = conv_ref[...].astype(F32)
        sg, sz = _silu_parts(z)
        dproj_ref[:, 0:D] = (dya * conv * sz).astype(BF16)
        dproj_ref[:, 3 * D:4 * D] = (dya * bg * conv * _dsilu(z, sg)).astype(BF16)
        dconv = dya * bg * sz
        after = carry[...]
        up1 = jnp.where(rows < tm - 1, pltpu.roll(dconv, tm - 1, 0), after[0:1, :])
        up2 = jnp.where(rows < tm - 2, pltpu.roll(dconv, tm - 2, 0),
                        jnp.where(rows == tm - 2, after[0:1, :], after[1:2, :]))
        carry[...] = dconv[0:8, :]
        _acc_row(dcw_ref, 0, jnp.sum(up2 * v, axis=0, keepdims=True))
        _acc_row(dcw_ref, 1, jnp.sum(up1 * v, axis=0, keepdims=True))
        _acc_row(dcw_ref, 2, jnp.sum(dconv * v, axis=0, keepdims=True))
        dv = cw_ref[2:3, :] * dconv + cw_ref[1:2, :] * up1 + cw_ref[0:1, :] * up2
        dproj_ref[:, D:2 * D] = (dv * u).astype(BF16)
        dproj_ref[:, 2 * D:3 * D] = (dv * cg).astype(BF16)

        @pl.when(i == nt - 1)
        def _():
            _write_gradient(dw_acc, dw_ref, dw16_ref, stage, put_sem)
            for cp in _device_exchange(part_refs, recv_refs, send, recv):
                cp.wait()

    rev = lambda i: (nt - 1 - i, 0)
    fix = lambda i: (0, 0)
    anyspace = pl.BlockSpec(memory_space=pl.ANY)
    dproj, dcw, dw, dw16, *got = pl.pallas_call(
        body, name="a_bwd", grid=(nt,),
        in_specs=[pl.BlockSpec((tm, D), rev), pl.BlockSpec((tm, D), rev), pl.BlockSpec((tm, D), rev),
                  pl.BlockSpec((tm, 4 * D), rev), pl.BlockSpec((8, D), fix), pl.BlockSpec((D, D), fix)] + ex_in,
        out_specs=[pl.BlockSpec((tm, 4 * D), rev), pl.BlockSpec((8, D), fix), anyspace, anyspace] + ex_out,
        out_shape=[SDS((s, 4 * D), BF16), SDS((8, D), F32), SDS((D, D), F32), SDS((D, D), BF16)] + ex_shape,
        scratch_shapes=[pltpu.VMEM((8, D), F32), pltpu.VMEM((D, D), F32), pltpu.VMEM((D // 4, D), BF16),
                        pltpu.SemaphoreType.DMA] + ex_sems,
        compiler_params=_params(("arbitrary",)),
    )(doa, ya, conv, proj, conv_w, w_out, *parts)
    return dproj, dcw, dw, dw16, got


def _dn1(dp_ref, w_ref):
    dn = _nt(dp_ref[:, 0:D], w_ref[0])
    for j in range(1, 4):
        dn = dn + _nt(dp_ref[:, D * j:D * (j + 1)], w_ref[j])
    return dn


def _a_in_bwd_matmul(dproj, win_g, tm, count, win_half, win_got):
    def body(dp_ref, w_ref, half_ref, got_in, dn_ref, got_ref, wcat, send, recv):
        del got_in

        @pl.when(pl.program_id(0) == 0)
        def _():
            _to_owner_core(half_ref, got_ref, send, recv, 1, "start")
            for j in range(N_CHIPS):
                pltpu.sync_copy(w_ref.at[j], wcat.at[:, pl.ds(D * j, D)])
        dn_ref[...] = _nt(dp_ref[...], wcat[...]).astype(BF16)

        @pl.when(pl.program_id(0) == count - 1)
        def _():
            _to_owner_core(half_ref, got_ref, send, recv, 1, "wait")

    row = lambda i: (i, 0)
    anyspace = pl.BlockSpec(memory_space=pl.ANY)
    return pl.pallas_call(
        body, name="a_in_bwd_matmul", grid=(count,),
        in_specs=[pl.BlockSpec((tm, 4 * D), row), anyspace, anyspace, anyspace],
        out_specs=[pl.BlockSpec((tm, D), row), anyspace],
        out_shape=[SDS((count * tm, D), BF16), SDS(win_got.shape, win_got.dtype)],
        scratch_shapes=[pltpu.VMEM((D, 4 * D), BF16)] + _owner_core_sems(),
        input_output_aliases={3: 1},
        compiler_params=_params(("arbitrary",)),
    )(dproj, win_g, win_half, win_got)


def _a_in_bwd(dn_first, dproj, x, dh1, win_g, g_pre, tm):
    s = x.shape[0]
    nt = s // tm
    count = dn_first.shape[0] // tm

    def body(dn_ref, dp_ref, x_ref, dh_ref, w_ref, g_ref, gx_ref, dg_ref, dn_s):
        i = pl.program_id(0)

        @pl.when(i == 0)
        def _():
            dg_ref[...] = jnp.zeros_like(dg_ref)

        @pl.when(i < count)
        def _():
            dn_s[...] = dn_ref[...].astype(F32)

        @pl.when(i >= count)
        def _():
            dn_s[...] = _dn1(dp_ref, w_ref)
        dn = dn_s[...]
        xv = x_ref[...]
        r = _rms_scale(xv)
        xh = xv * r
        _acc_row(dg_ref, 0, jnp.sum(dn * xh, axis=0, keepdims=True))
        dxh = dn * g_ref[...]
        gx_ref[...] = dh_ref[...] + r * (dxh - xh * jnp.mean(dxh * xh, axis=-1, keepdims=True))

    row = lambda i: (i, 0)
    fix = lambda i: (0, 0)
    return pl.pallas_call(
        body, name="a_in_bwd", grid=(nt,),
        in_specs=[pl.BlockSpec((tm, D), lambda i: (jnp.minimum(i, count - 1), 0)),
                  pl.BlockSpec((tm, 4 * D), lambda i: (jnp.maximum(i, count), 0)),
                  pl.BlockSpec((tm, D), row), pl.BlockSpec((tm, D), row),
                  pl.BlockSpec((4, D, D), lambda i: (0, 0, 0)), pl.BlockSpec((1, D), fix)],
        out_specs=[pl.BlockSpec((tm, D), row), pl.BlockSpec((8, D), fix)],
        out_shape=[SDS((s, D), F32), SDS((8, D), F32)],
        scratch_shapes=[pltpu.VMEM((tm, D), F32)],
        compiler_params=_params(("arbitrary",)),
    )(dn_first, dproj, x, dh1, win_g, g_pre)


def _swap_halves(shards, send, recv):
    x, y, c = lax.axis_index("x"), lax.axis_index("y"), lax.axis_index("c")
    sibling = (x, y, 1 - c)
    copies = []
    for b, full in enumerate(shards):
        h = full.shape[0] // 2
        mine = full.at[pl.ds(pl.multiple_of(c * h, 8), h)]
        theirs = full.at[pl.ds(pl.multiple_of((1 - c) * h, 8), h)]
        copies.append((pltpu.make_async_remote_copy(src_ref=mine, dst_ref=mine, send_sem=send.at[b], recv_sem=recv.at[b],
                                                    device_id=sibling, device_id_type=MESH),
                       pltpu.make_async_remote_copy(src_ref=mine, dst_ref=theirs, send_sem=send.at[b], recv_sem=recv.at[b],
                                                    device_id=sibling, device_id_type=MESH)))
    return copies


def _dw_in_half(n1, dproj, core, tmw, name, to_owners=None, to_devices=None, shards=()):
    s = n1.shape[0]
    h = D // 2
    nt = s // tmw
    n_sh = len(shards)
    if to_owners is not None:
        sent_array, sems, got_shape = to_owners, _owner_core_sems(), SDS((N_DEV - 1, h, D), BF16)
    else:
        sent_array = to_devices
        _, _, (got_shape,), sems = _device_exchange_specs([to_devices])

    def body(*refs):
        a_ref, b_ref, sent = refs[:3]
        o_ref, o16_ref, got = refs[3 + n_sh:6 + n_sh]
        shard_refs = refs[6 + n_sh:6 + 2 * n_sh]
        send, recv = refs[6 + 2 * n_sh:8 + 2 * n_sh]
        swap_sems = refs[8 + 2 * n_sh:]
        j, t = pl.program_id(0), pl.program_id(1)

        def exchange(action):
            if to_owners is not None:
                _to_owner_core(sent, got, send, recv, 1 - core, action)
            else:
                for cp in _device_exchange([sent], [got], send, recv):
                    cp.start() if action == "start" else cp.wait()

        @pl.when((j == 0) & (t == 0))
        def _():
            exchange("start")
            if n_sh:
                for mine, _ in _swap_halves(shard_refs, *swap_sems):
                    mine.start()

        @pl.when(t == 0)
        def _():
            o_ref[...] = jnp.zeros_like(o_ref)
        o_ref[0] += _tn(a_ref[...], b_ref[...])

        @pl.when(t == nt - 1)
        def _():
            o16_ref[...] = o_ref[...].astype(BF16)

        @pl.when((j == N_CHIPS - 1) & (t == nt - 1))
        def _():
            exchange("wait")
            if n_sh:
                for mine, theirs in _swap_halves(shard_refs, *swap_sems):
                    theirs.wait_recv()
                    mine.wait_send()

    anyspace = pl.BlockSpec(memory_space=pl.ANY)
    slot = pl.BlockSpec((1, h, D), lambda j, t: (j, 0, 0))
    swap_scratch = [pltpu.SemaphoreType.DMA((n_sh,)), pltpu.SemaphoreType.DMA((n_sh,))] if n_sh else []
    return pl.pallas_call(
        body, name=name, grid=(N_CHIPS, nt),
        in_specs=[pl.BlockSpec((tmw, h), lambda j, t: (t, core)), pl.BlockSpec((tmw, D), lambda j, t: (t, j))]
        + [anyspace] * (1 + n_sh),
        out_specs=[slot, slot] + [anyspace] * (1 + n_sh),
        out_shape=[SDS((N_CHIPS, h, D), F32), SDS((N_CHIPS, h, D), BF16), got_shape]
        + [SDS(sh.shape, F32) for sh in shards],
        scratch_shapes=sems + swap_scratch,
        input_output_aliases={3 + b: 3 + b for b in range(n_sh)},
        compiler_params=_params(("arbitrary", "arbitrary")),
    )(n1, dproj, sent_array, *shards)


def _share_and_gather(shards, smalls):
    n_h, n_s = len(shards), len(smalls)

    def body(*refs):
        small_ins = refs[n_h:n_h + n_s]
        fs = refs[n_h + n_s:2 * n_h + n_s]
        small_alls = refs[2 * n_h + n_s:2 * n_h + 2 * n_s]
        dsend, drecv, ssend, srecv = refs[2 * n_h + 2 * n_s:]
        x, y, c = lax.axis_index("x"), lax.axis_index("y"), lax.axis_index("c")
        swaps = _swap_halves(fs, dsend, drecv)
        sends, arrivals = [mine for mine, _ in swaps], [theirs for _, theirs in swaps]
        me = 4 * x + 2 * y + c
        for k, (small_in, small_all) in enumerate(zip(small_ins, small_alls)):
            small_all[me] = small_in[...]
            for rel in range(1, N_DEV):
                fx, fy, fc = rel >> 2, (rel >> 1) & 1, rel & 1
                peer = (x + fx - 2 * x * fx, y + fy - 2 * y * fy, c + fc - 2 * c * fc)
                sender = 4 * peer[0] + 2 * peer[1] + peer[2]
                sem = (N_DEV - 1) * k + rel - 1
                sends.append(pltpu.make_async_remote_copy(
                    src_ref=small_in, dst_ref=small_all.at[me], send_sem=ssend.at[sem], recv_sem=srecv.at[sem],
                    device_id=peer, device_id_type=MESH))
                arrivals.append(pltpu.make_async_remote_copy(
                    src_ref=small_in, dst_ref=small_all.at[sender], send_sem=ssend.at[sem], recv_sem=srecv.at[sem],
                    device_id=peer, device_id_type=MESH))
        for cp in sends:
            cp.start()
        for cp in arrivals:
            cp.wait_recv()
        for cp in sends:
            cp.wait_send()

    anyspace = pl.BlockSpec(memory_space=pl.ANY)
    vm = pl.BlockSpec(memory_space=pltpu.VMEM)
    out_shape = [SDS(full.shape, F32) for full in shards] + [SDS((N_DEV,) + sm.shape, F32) for sm in smalls]
    n_all = (N_DEV - 1) * n_s
    outs = pl.pallas_call(
        body, name="share_and_gather", out_shape=out_shape,
        in_specs=[anyspace] * n_h + [vm] * n_s, out_specs=[anyspace] * n_h + [vm] * n_s,
        scratch_shapes=[pltpu.SemaphoreType.DMA((n_h,)), pltpu.SemaphoreType.DMA((n_h,)),
                        pltpu.SemaphoreType.DMA((n_all,)), pltpu.SemaphoreType.DMA((n_all,))],
        input_output_aliases={b: b for b in range(n_h)},
    )(*shards, *smalls)
    return outs[:n_h], outs[n_h:]


def _add_win(where, lo, hi, r, name):
    _, h, cols = lo.shape
    tr = min(h, 256)
    nh = h // tr

    def body(where_ref, lo_ref, hi_ref, r_ref, o_ref):
        acc = jnp.where(where_ref[0] == 0, lo_ref[0], hi_ref[0])
        for k in range(N_DEV - 1):
            acc = acc + r_ref[k].astype(F32)
        o_ref[...] = acc

    own = pl.BlockSpec((1, tr, cols), lambda i, w: (w[1], i, 0))
    return pl.pallas_call(
        body, name=name,
        grid_spec=pltpu.PrefetchScalarGridSpec(
            num_scalar_prefetch=1, grid=(nh,),
            in_specs=[own, own, pl.BlockSpec((N_DEV - 1, tr, cols), lambda i, w: (0, i, 0))],
            out_specs=pl.BlockSpec((tr, cols), lambda i, w: (w[0] * nh + i, 0))),
        out_shape=SDS((2 * h, cols), F32),
        compiler_params=_params(("parallel",)),
    )(where, lo, hi, r)


def _add_devices(where, g, r, name):
    _, rows, cols = g.shape
    h = rows // 2
    tr = min(h, 256)
    nh = h // tr

    def body(where_ref, g_ref, r_ref, o_ref):
        del where_ref
        acc = g_ref[0]
        for k in range(N_DEV - 1):
            acc = acc + r_ref[k].astype(F32)
        o_ref[...] = acc

    return pl.pallas_call(
        body, name=name,
        grid_spec=pltpu.PrefetchScalarGridSpec(
            num_scalar_prefetch=1, grid=(nh,),
            in_specs=[pl.BlockSpec((1, tr, cols), lambda i, w: (w[1], w[0] * nh + i, 0)),
                      pl.BlockSpec((N_DEV - 1, tr, cols), lambda i, w: (0, i, 0))],
            out_specs=pl.BlockSpec((tr, cols), lambda i, w: (w[0] * nh + i, 0))),
        out_shape=SDS((rows, cols), F32),
        compiler_params=_params(("parallel",)),
    )(where, g, r)


def _sum_smalls(gathered):
    n = len(gathered)

    def body(*refs):
        for all_ref, o_ref in zip(refs[:n], refs[n:]):
            acc = all_ref[0]
            for dev in range(1, N_DEV):
                acc = acc + all_ref[dev]
            o_ref[...] = acc

    vm = pl.BlockSpec(memory_space=pltpu.VMEM)
    return pl.pallas_call(
        body, name="sum_smalls", out_shape=[SDS(a.shape[1:], F32) for a in gathered],
        in_specs=[vm] * n, out_specs=[vm] * n,
    )(*gathered)


def _adam_step(g, w, m, v):
    nm = ADAM_B1 * m + (1.0 - ADAM_B1) * g
    nv = ADAM_B2 * v + (1.0 - ADAM_B2) * (g * g)
    m_hat = nm / (1.0 - ADAM_B1 ** ADAM_STEP)
    v_hat = nv / (1.0 - ADAM_B2 ** ADAM_STEP)
    return -ADAM_LR * (m_hat / (jnp.sqrt(v_hat) + ADAM_EPS) + ADAM_WD * w), nm, nv


def _adamw(g, w, m, v, name):
    rows, cols = g.shape
    tr = min(rows, 256)

    def body(g_ref, w_ref, m_ref, v_ref, d_ref, nm_ref, nv_ref):
        d_ref[...], nm_ref[...], nv_ref[...] = _adam_step(g_ref[...], w_ref[...], m_ref[...], v_ref[...])

    spec = pl.BlockSpec((tr, cols), lambda i: (i, 0))
    return pl.pallas_call(
        body, name=name, grid=(rows // tr,), in_specs=[spec] * 4, out_specs=[spec] * 3,
        out_shape=[SDS(g.shape, F32)] * 3, compiler_params=_params(("parallel",)),
    )(g, w, m, v)


def _small_update(chip, tot, tot_rel, wmv):
    names = list(SMALL_PLACES)
    n = len(names)

    def body(chip_ref, tot_ref, quarter_ref, rel_ref, *refs):
        del chip_ref
        ins, outs = refs[:3 * n], refs[3 * n:]
        outs[4 * n][...] = 0.5 * tot_ref[LOSS_ROW:LOSS_ROW + 1, 0:1]
        for i, nm in enumerate(names):
            source, row, shape = SMALL_PLACES[nm]
            from_ref = {"rows": tot_ref, "quarter": quarter_ref, "rel": rel_ref}[source]
            for at in ([Ellipsis] if len(shape) == 2 else range(shape[0])):
                g = from_ref[row:row + shape[0], 0:shape[-1]] if at is Ellipsis else from_ref[row + at:row + at + 1, 0:shape[-1]]
                outs[4 * i][at] = g
                outs[4 * i + 1][at], outs[4 * i + 2][at], outs[4 * i + 3][at] = _adam_step(
                    g, ins[3 * i][at], ins[3 * i + 1][at], ins[3 * i + 2][at])

    whole = lambda shape: pl.BlockSpec(shape, lambda i, c: (0,) * len(shape))
    shapes = [SMALL_PLACES[nm][2] for nm in names]
    outs = pl.pallas_call(
        body, name="small_update",
        grid_spec=pltpu.PrefetchScalarGridSpec(
            num_scalar_prefetch=1, grid=(1,),
            in_specs=[whole(tot.shape), pl.BlockSpec((tot.shape[0], D // 4), lambda i, c: (0, c[0])),
                      whole(tot_rel.shape)] + [whole(shp) for shp in shapes for _ in range(3)],
            out_specs=[whole(shp) for shp in shapes for _ in range(4)] + [whole((1, 1))]),
        out_shape=[SDS(shp, F32) for shp in shapes for _ in range(4)] + [SDS((1, 1), F32)],
    )(chip, tot, tot, tot_rel, *[a for nm in names for a in wmv[nm]])
    return {nm: tuple(outs[4 * i:4 * i + 4]) for i, nm in enumerate(names)}, outs[4 * n].reshape(())


def _pad_rows(a, rows):
    return jnp.concatenate([a, jnp.zeros((rows - a.shape[0], a.shape[1]), a.dtype)], axis=0)


def kernel(x, a_pre_norm, a_w_in, a_conv_w, a_w_out, a_post_norm, kv_norm, w_kv, rel_bias, b_pre_norm, b_w_in, b_sinks, b_w_out, b_post_norm, loss_target, m_a_pre_norm, m_a_w_in, m_a_conv_w, m_a_w_out, m_a_post_norm, m_kv_norm, m_w_kv, m_rel_bias, m_b_pre_norm, m_b_w_in, m_b_sinks, m_b_w_out, m_b_post_norm, v_a_pre_norm, v_a_w_in, v_a_conv_w, v_a_w_out, v_a_post_norm, v_kv_norm, v_w_kv, v_rel_bias, v_b_pre_norm, v_b_w_in, v_b_sinks, v_b_w_out, v_b_post_norm):
    seq = x.shape[1]
    xs = x.reshape(seq, D)
    tgt = loss_target.reshape(seq, D)
    chip = 2 * lax.axis_index("x") + lax.axis_index("y")
    core = lax.axis_index("c")
    tm = _tile(seq, 512)
    tmw = _tile(seq, 1024)

    shards = [a_w_in[0], a_w_out[0], w_kv, b_w_in[0], b_w_out[0]]
    small_w = _pad_rows(jnp.concatenate([a_pre_norm, a_conv_w[0], a_post_norm], axis=0), 8)
    *own_only, small_g = _prepare_weights(shards, small_w)
    where = jnp.stack([core, chip]).astype(jnp.int32)
    small_full = small_g.transpose(1, 0, 2).reshape(8, D)
    g_apre, conv_w, g_apost = small_full[0:1], _pad_rows(small_full[1:4], 8), small_full[4:5]
    g_kv = kv_norm.reshape(1, D)

    proj, n1, (win_g, wouta_g, wkv_g, wbin_g, woutb_g) = _a_in(where[1:2], xs, g_apre, own_only, tmw)
    wouta = wouta_g.reshape(D, D)
    wkv = wkv_g.reshape(D, 2 * KV_W)
    woutb = woutb_g.reshape(D, D)
    ya, oa, h1, conv = _a_mix(proj, xs, conv_w, wouta, g_apost, tm)
    kv, q, zb = _b_in(h1, g_kv, b_pre_norm, wkv, wbin_g, tmw)
    tab = _bias_table(rel_bias.T, b_sinks.reshape(N_HEADS))
    att, stats = _attn_fwd(q, kv, tab)
    dh2, dqz, datt, loss_acc, dg_bpost, dw_outb, dw_outb16 = _mid(att, zb, h1, tgt, woutb, b_post_norm, tm)

    dqz, dkv, dtab = _attn_bwd(q, kv, datt, stats, tab, dqz)
    dh1, doa, dg_b, dw_bin, dw_kv, dw_bin16, dw_kv16 = _b_bwd(dqz, dkv, h1, dh2, oa, wbin_g, wkv, g_kv, b_pre_norm,
                                                              g_apost, tm)
    by_chip = lambda a, cols: a.reshape(N_CHIPS, D // 4, cols)
    grads1 = [by_chip(dw_kv, 2 * KV_W), dw_bin, by_chip(dw_outb, D)]
    sent1 = [by_chip(dw_kv16, 2 * KV_W), dw_bin16, by_chip(dw_outb16, D)]
    names1 = ["w_kv", "b_w_in", "b_w_out"]
    dproj, dconv_w, dw_outa, dw_outa16, from_devices1 = _a_bwd(doa, ya, conv, proj, conv_w, wouta, tm, sent1)
    shards1 = [_add_devices(where, g, r, "add_devices_" + nm) for g, r, nm in zip(grads1, from_devices1, names1)]
    tmw2 = _tile(seq, 4096)
    win_lo, win_lo16, outa_got, g_wkv, g_wbin, g_woutb = _dw_in_half(
        n1, dproj, 0, tmw2, "dw_a_in_lo", to_devices=by_chip(dw_outa16, D), shards=shards1)
    win_hi, win_hi16, win_got = _dw_in_half(n1, dproj, 1, tmw2, "dw_a_in_hi", to_owners=win_lo16)
    nt = seq // tmw
    dn_first, win_got = _a_in_bwd_matmul(dproj, win_g, tmw, max(nt - max(nt // 4, 1), 1), win_hi16, win_got)
    grad_x, dg_apre = _a_in_bwd(dn_first, dproj, xs, dh1, win_g, g_apre, tm)
    shards2 = [_add_win(where, win_lo, win_hi, win_got, "add_devices_a_w_in"),
               _add_devices(where, by_chip(dw_outa, D), outa_got, "add_devices_a_w_out")]

    assert dconv_w.shape == (8, D)
    drel, smalls = _bias_fold(dtab, SMALL_PLACES["b_sinks"][1], [
        (0, dg_apre, 0, 1), (1, dg_b, 2, 1), (2, dg_b, 0, 1), (3, dg_b, 1, 1), (4, dg_bpost, 0, 1),
        (LOSS_ROW, loss_acc, 0, 1), (8, dconv_w, 0, 8)])
    (g_win, g_wouta), gathered = _share_and_gather(shards2, (smalls, drel))
    tot, tot_rel = _sum_smalls(gathered)

    big = {}
    for nm, g, w, m, v in [("a_w_in", g_win, a_w_in, m_a_w_in, v_a_w_in), ("a_w_out", g_wouta, a_w_out, m_a_w_out, v_a_w_out),
                           ("w_kv", g_wkv, w_kv, m_w_kv, v_w_kv), ("b_w_in", g_wbin, b_w_in, m_b_w_in, v_b_w_in),
                           ("b_w_out", g_woutb, b_w_out, m_b_w_out, v_b_w_out)]:
        shp = w.shape
        two = (shp[-2], shp[-1])
        d, nm_, nv_ = _adamw(g, w.reshape(two), m.reshape(two), v.reshape(two), "adamw_" + nm)
        big[nm] = (g.reshape(shp), d.reshape(shp), nm_.reshape(shp), nv_.reshape(shp))

    given = {"a_pre_norm": (a_pre_norm, m_a_pre_norm, v_a_pre_norm), "a_conv_w": (a_conv_w, m_a_conv_w, v_a_conv_w),
             "a_post_norm": (a_post_norm, m_a_post_norm, v_a_post_norm), "kv_norm": (kv_norm, m_kv_norm, v_kv_norm),
             "rel_bias": (rel_bias, m_rel_bias, v_rel_bias), "b_pre_norm": (b_pre_norm, m_b_pre_norm, v_b_pre_norm),
             "b_sinks": (b_sinks, m_b_sinks, v_b_sinks), "b_post_norm": (b_post_norm, m_b_post_norm, v_b_post_norm)}
    to_kernel = lambda nm, a: a.T if nm == "rel_bias" else a.reshape(SMALL_PLACES[nm][2])
    from_kernel = lambda nm, a: a.T if nm == "rel_bias" else a.reshape(given[nm][0].shape)
    small, loss = _small_update(where[1:2], tot, tot_rel, {nm: tuple(to_kernel(nm, a) for a in wmv)
                                            for nm, wmv in given.items()})
    order = ["a_pre_norm", "a_w_in", "a_conv_w", "a_w_out", "a_post_norm", "kv_norm", "w_kv", "rel_bias",
             "b_pre_norm", "b_w_in", "b_sinks", "b_w_out", "b_post_norm"]
    outs = []
    for which in range(4):
        for nm in order:
            outs.append(big[nm][which] if nm in big else from_kernel(nm, small[nm][which]))
    return (loss, grad_x.reshape(x.shape), *outs)
```
